```python
import math
import jax, jax.numpy as jnp
from jax import lax
import numpy as np

D_MODEL = 1024
BATCH = 8
SEQ = 4096
DEPTH = 2

CHUNK = 64
D_FF = 2816
MACARON_WEIGHT = 0.5
RMS_EPS = 1e-6
A_WIDTH = D_MODEL // 2
A_GROUPS = 8
CONV_WIDTH = 3
SB_HEADS = 8
SB_HEAD_DIM = (D_MODEL // 2) // SB_HEADS
B_WIDTH = SB_HEADS * SB_HEAD_DIM
SB_BLOCK = 128
AB_IN_COLS = 3 * A_WIDTH + 3 * B_WIDTH
C_WIDTH = D_MODEL
C_HEAD_DIM = 128
C_HEADS = C_WIDTH // C_HEAD_DIM
C_IN_COLS = 4 * C_WIDTH
N_EVEN = (DEPTH + 1) // 2
N_ODD = DEPTH // 2

kernel_name = "hybrid_shortconv_stickbreak_hgrn2_macaron"


def rms_norm(x, gain):
    x32 = x.astype(jnp.float32)
    y = x32 * lax.rsqrt(jnp.mean(x32 * x32, axis=-1, keepdims=True) + RMS_EPS)
    return (y * gain.astype(jnp.float32)).astype(x.dtype)


def swiglu(h, w_gate, w_up, w_down):
    return (jax.nn.silu(h @ w_gate) * (h @ w_up)) @ w_down


def stick_breaking_attention(q, k, v):
    bsz, nh, s_len, dh = q.shape
    nb = s_len // SB_BLOCK
    scale = 1.0 / math.sqrt(dh)
    qb = jnp.moveaxis(q.reshape(bsz, nh, nb, SB_BLOCK, dh), 2, 0)
    starts = jnp.arange(nb, dtype=jnp.int32) * SB_BLOCK
    kpos = jnp.arange(s_len, dtype=jnp.int32)

    def one_block(args):
        qi, start = args
        qpos = start + jnp.arange(SB_BLOCK, dtype=jnp.int32)
        mask = kpos[None, :] < qpos[:, None]
        z = jnp.einsum('bhqd,bhkd->bhqk', qi, k) * scale
        log_beta = jax.nn.log_sigmoid(z)
        log_keep = jnp.where(mask, jax.nn.log_sigmoid(-z), 0.0)
        later = lax.cumsum(log_keep, axis=3, reverse=True) - log_keep
        w = jnp.where(mask, jnp.exp(log_beta + later), 0.0)
        return jnp.einsum('bhqk,bhkd->bhqd', w, v)

    out = lax.map(one_block, (qb, starts))
    return jnp.moveaxis(out, 0, 2).reshape(bsz, nh, s_len, dh)


def shortconv_stickbreak_mixer(h, w_in, conv_w, w_out):
    bsz, s_len, _ = h.shape
    proj = h @ w_in
    a_b, a_c, a_x, q, k, v = jnp.split(proj, 6, axis=-1)
    u = a_c * a_x
    conv = lax.conv_general_dilated(
        u, conv_w[:, None, :].astype(u.dtype), window_strides=(1,),
        padding=[(CONV_WIDTH - 1, 0)], dimension_numbers=('NWC', 'WIO', 'NWC'),
        feature_group_count=A_WIDTH)
    y_a = a_b * conv
    def heads(t):
        return t.reshape(bsz, s_len, SB_HEADS, SB_HEAD_DIM).transpose(0, 2, 1, 3).astype(jnp.float32)
    y_b = stick_breaking_attention(heads(q), heads(k), heads(v))
    y_b = y_b.transpose(0, 2, 1, 3).reshape(bsz, s_len, B_WIDTH).astype(h.dtype)
    return jnp.concatenate([y_a, y_b], axis=-1) @ w_out


def chunkwise_gated_recurrence(q, log_f, k, v):
    bsz, nh, s_len, dk = q.shape
    dv = v.shape[-1]
    n_chunks = s_len // CHUNK

    def to_chunks(t):
        return jnp.moveaxis(t.reshape(bsz, nh, n_chunks, CHUNK, t.shape[-1]), 2, 0)

    tri = jnp.tril(jnp.ones((CHUNK, CHUNK), dtype=bool))

    def step(state, inp):
        qc, gc, kc, vc = inp
        b = jnp.cumsum(gc, axis=2)
        o_inter = jnp.einsum('bhtk,bhkv->bhtv', qc * jnp.exp(b), state)
        diff = b[:, :, :, None, :] - b[:, :, None, :, :]
        decay = jnp.exp(jnp.where(tri[None, None, :, :, None], diff, -jnp.inf))
        scores = jnp.einsum('bhtk,bhsk,bhtsk->bhts', qc, kc, decay)
        o_intra = jnp.einsum('bhts,bhsv->bhtv', scores, vc)
        b_last = b[:, :, -1:, :]
        new_state = (jnp.exp(b_last[:, :, 0, :])[..., None] * state
                     + jnp.einsum('bhsk,bhsv->bhkv', kc * jnp.exp(b_last - b), vc))
        return new_state, o_inter + o_intra

    state0 = jnp.zeros((bsz, nh, dk, dv), jnp.float32)
    _, out = lax.scan(step, state0, (to_chunks(q), to_chunks(log_f), to_chunks(k), to_chunks(v)))
    return jnp.moveaxis(out, 0, 2).reshape(bsz, nh, s_len, dv)


def hgrn2_mixer(h, w_in, lower_bound, out_norm, w_out):
    bsz, s_len, _ = h.shape
    proj = h @ w_in
    q, f, i, g = jnp.split(proj, 4, axis=-1)
    lb = lower_bound.astype(jnp.float32)
    log_f = jnp.logaddexp(jnp.log(lb), jnp.log1p(-lb) + jax.nn.log_sigmoid(f.astype(jnp.float32)))
    k = -jnp.expm1(log_f)
    q = jax.nn.silu(q.astype(jnp.float32))

    def heads(t):
        return t.reshape(bsz, s_len, C_HEADS, C_HEAD_DIM).transpose(0, 2, 1, 3).astype(jnp.float32)

    o = chunkwise_gated_recurrence(heads(q), heads(log_f), heads(k), heads(i))
    o = o * lax.rsqrt(jnp.mean(o * o, axis=-1, keepdims=True) + RMS_EPS) * out_norm.astype(jnp.float32)
    o = o.transpose(0, 2, 1, 3).reshape(bsz, s_len, C_WIDTH)
    o = (o * jax.nn.silu(g.astype(jnp.float32))).astype(h.dtype)
    return o @ w_out


def _fwd_setup_inputs(seed: int = 0) -> dict:
    key = jax.random.key(seed)
    ks = jax.random.split(key, 20)
    f32 = jnp.float32

    def w(k, shape, fan_in):
        return jax.random.normal(k, shape, f32) * (fan_in ** -0.5)

    def gain(k, shape):
        return 1.0 + 0.02 * jax.random.normal(k, shape, f32)

    return {
        "x": jax.random.normal(ks[0], (BATCH, SEQ, D_MODEL), f32),
        "ffn_pre_norm": gain(ks[1], (DEPTH, D_MODEL)),
        "ffn_pre_w_gate": w(ks[2], (DEPTH, D_MODEL, D_FF), D_MODEL),
        "ffn_pre_w_up": w(ks[3], (DEPTH, D_MODEL, D_FF), D_MODEL),
        "ffn_pre_w_down": w(ks[4], (DEPTH, D_FF, D_MODEL), D_FF),
        "mix_norm": gain(ks[5], (DEPTH, D_MODEL)),
        "ffn_post_norm": gain(ks[6], (DEPTH, D_MODEL)),
        "ffn_post_w_gate": w(ks[7], (DEPTH, D_MODEL, D_FF), D_MODEL),
        "ffn_post_w_up": w(ks[8], (DEPTH, D_MODEL, D_FF), D_MODEL),
        "ffn_post_w_down": w(ks[9], (DEPTH, D_FF, D_MODEL), D_FF),
        "ab_w_in": w(ks[10], (N_EVEN, D_MODEL, AB_IN_COLS), D_MODEL),
        "ab_conv_w": w(ks[11], (N_EVEN, CONV_WIDTH, A_WIDTH), CONV_WIDTH),
        "ab_w_out": w(ks[12], (N_EVEN, A_WIDTH + B_WIDTH, D_MODEL), A_WIDTH + B_WIDTH),
        "c_w_in": w(ks[13], (N_ODD, D_MODEL, C_IN_COLS), D_MODEL),
        "c_lower_bounds": 0.1 * jax.random.normal(ks[14], (DEPTH, C_WIDTH), f32),
        "c_out_norm": gain(ks[15], (N_ODD, C_HEAD_DIM)),
        "c_w_out": w(ks[16], (N_ODD, C_WIDTH, D_MODEL), C_WIDTH),
        "final_norm": gain(ks[17], (D_MODEL,)),
    }


def _fwd_reference(x, ffn_pre_norm, ffn_pre_w_gate, ffn_pre_w_up, ffn_pre_w_down, mix_norm,
              ffn_post_norm, ffn_post_w_gate, ffn_post_w_up, ffn_post_w_down,
              ab_w_in, ab_conv_w, ab_w_out, c_w_in, c_lower_bounds, c_out_norm, c_w_out,
              final_norm):
    lb_soft = jax.nn.softmax(c_lower_bounds.astype(jnp.float32), axis=0)
    lb_cum = jnp.cumsum(lb_soft, axis=0)
    lower_bounds = lb_cum - lb_cum[0:1]

    h = x
    for layer in range(DEPTH):
        h = h + MACARON_WEIGHT * swiglu(rms_norm(h, ffn_pre_norm[layer]), ffn_pre_w_gate[layer],
                                        ffn_pre_w_up[layer], ffn_pre_w_down[layer])
        hn = rms_norm(h, mix_norm[layer])
        if layer % 2 == 0:
            e = layer // 2
            h = h + shortconv_stickbreak_mixer(hn, ab_w_in[e], ab_conv_w[e], ab_w_out[e])
        else:
            o = layer // 2
            h = h + hgrn2_mixer(hn, c_w_in[o], lower_bounds[layer], c_out_norm[o], c_w_out[o])
        h = h + MACARON_WEIGHT * swiglu(rms_norm(h, ffn_post_norm[layer]), ffn_post_w_gate[layer],
                                        ffn_post_w_up[layer], ffn_post_w_down[layer])
    return rms_norm(h, final_norm)


import jax as _jax
import jax.numpy as _jnp

TWIN_FORMAT = 'train_step'
FWD_PARAMS = ['x', 'ffn_pre_norm', 'ffn_pre_w_gate', 'ffn_pre_w_up', 'ffn_pre_w_down', 'mix_norm', 'ffn_post_norm', 'ffn_post_w_gate', 'ffn_post_w_up', 'ffn_post_w_down', 'ab_w_in', 'ab_conv_w', 'ab_w_out', 'c_w_in', 'c_lower_bounds', 'c_out_norm', 'c_w_out', 'final_norm']
TWIN_WEIGHTS = ['ffn_pre_norm', 'ffn_pre_w_gate', 'ffn_pre_w_up', 'ffn_pre_w_down', 'mix_norm', 'ffn_post_norm', 'ffn_post_w_gate', 'ffn_post_w_up', 'ffn_post_w_down', 'ab_w_in', 'ab_conv_w', 'ab_w_out', 'c_w_in', 'c_lower_bounds', 'c_out_norm', 'c_w_out', 'final_norm']
TWIN_DIFF_INPUT = 'x'
TWIN_INPUTS = ['x', 'ffn_pre_norm', 'ffn_pre_w_gate', 'ffn_pre_w_up', 'ffn_pre_w_down', 'mix_norm', 'ffn_post_norm', 'ffn_post_w_gate', 'ffn_post_w_up', 'ffn_post_w_down', 'ab_w_in', 'ab_conv_w', 'ab_w_out', 'c_w_in', 'c_lower_bounds', 'c_out_norm', 'c_w_out', 'final_norm', 'loss_target', 'm_ffn_pre_norm', 'm_ffn_pre_w_gate', 'm_ffn_pre_w_up', 'm_ffn_pre_w_down', 'm_mix_norm', 'm_ffn_post_norm', 'm_ffn_post_w_gate', 'm_ffn_post_w_up', 'm_ffn_post_w_down', 'm_ab_w_in', 'm_ab_conv_w', 'm_ab_w_out', 'm_c_w_in', 'm_c_lower_bounds', 'm_c_out_norm', 'm_c_w_out', 'm_final_norm', 'v_ffn_pre_norm', 'v_ffn_pre_w_gate', 'v_ffn_pre_w_up', 'v_ffn_pre_w_down', 'v_mix_norm', 'v_ffn_post_norm', 'v_ffn_post_w_gate', 'v_ffn_post_w_up', 'v_ffn_post_w_down', 'v_ab_w_in', 'v_ab_conv_w', 'v_ab_w_out', 'v_c_w_in', 'v_c_lower_bounds', 'v_c_out_norm', 'v_c_w_out', 'v_final_norm']
TWIN_OUTPUTS = ['loss', 'grad_x', 'grad_ffn_pre_norm', 'grad_ffn_pre_w_gate', 'grad_ffn_pre_w_up', 'grad_ffn_pre_w_down', 'grad_mix_norm', 'grad_ffn_post_norm', 'grad_ffn_post_w_gate', 'grad_ffn_post_w_up', 'grad_ffn_post_w_down', 'grad_ab_w_in', 'grad_ab_conv_w', 'grad_ab_w_out', 'grad_c_w_in', 'grad_c_lower_bounds', 'grad_c_out_norm', 'grad_c_w_out', 'grad_final_norm', 'delta_ffn_pre_norm', 'delta_ffn_pre_w_gate', 'delta_ffn_pre_w_up', 'delta_ffn_pre_w_down', 'delta_mix_norm', 'delta_ffn_post_norm', 'delta_ffn_post_w_gate', 'delta_ffn_post_w_up', 'delta_ffn_post_w_down', 'delta_ab_w_in', 'delta_ab_conv_w', 'delta_ab_w_out', 'delta_c_w_in', 'delta_c_lower_bounds', 'delta_c_out_norm', 'delta_c_w_out', 'delta_final_norm', 'new_m_ffn_pre_norm', 'new_m_ffn_pre_w_gate', 'new_m_ffn_pre_w_up', 'new_m_ffn_pre_w_down', 'new_m_mix_norm', 'new_m_ffn_post_norm', 'new_m_ffn_post_w_gate', 'new_m_ffn_post_w_up', 'new_m_ffn_post_w_down', 'new_m_ab_w_in', 'new_m_ab_conv_w', 'new_m_ab_w_out', 'new_m_c_w_in', 'new_m_c_lower_bounds', 'new_m_c_out_norm', 'new_m_c_w_out', 'new_m_final_norm', 'new_v_ffn_pre_norm', 'new_v_ffn_pre_w_gate', 'new_v_ffn_pre_w_up', 'new_v_ffn_pre_w_down', 'new_v_mix_norm', 'new_v_ffn_post_norm', 'new_v_ffn_post_w_gate', 'new_v_ffn_post_w_up', 'new_v_ffn_post_w_down', 'new_v_ab_w_in', 'new_v_ab_conv_w', 'new_v_ab_w_out', 'new_v_c_w_in', 'new_v_c_lower_bounds', 'new_v_c_out_norm', 'new_v_c_w_out', 'new_v_final_norm']
TWIN_LEAF_KINDS = {'loss': 'loss', 'grad_x': 'grad_x', 'grad_ffn_pre_norm': 'grad_w', 'grad_ffn_pre_w_gate': 'grad_w', 'grad_ffn_pre_w_up': 'grad_w', 'grad_ffn_pre_w_down': 'grad_w', 'grad_mix_norm': 'grad_w', 'grad_ffn_post_norm': 'grad_w', 'grad_ffn_post_w_gate': 'grad_w', 'grad_ffn_post_w_up': 'grad_w', 'grad_ffn_post_w_down': 'grad_w', 'grad_ab_w_in': 'grad_w', 'grad_ab_conv_w': 'grad_w', 'grad_ab_w_out': 'grad_w', 'grad_c_w_in': 'grad_w', 'grad_c_lower_bounds': 'grad_w', 'grad_c_out_norm': 'grad_w', 'grad_c_w_out': 'grad_w', 'grad_final_norm': 'grad_w', 'delta_ffn_pre_norm': 'delta_w', 'delta_ffn_pre_w_gate': 'delta_w', 'delta_ffn_pre_w_up': 'delta_w', 'delta_ffn_pre_w_down': 'delta_w', 'delta_mix_norm': 'delta_w', 'delta_ffn_post_norm': 'delta_w', 'delta_ffn_post_w_gate': 'delta_w', 'delta_ffn_post_w_up': 'delta_w', 'delta_ffn_post_w_down': 'delta_w', 'delta_ab_w_in': 'delta_w', 'delta_ab_conv_w': 'delta_w', 'delta_ab_w_out': 'delta_w', 'delta_c_w_in': 'delta_w', 'delta_c_lower_bounds': 'delta_w', 'delta_c_out_norm': 'delta_w', 'delta_c_w_out': 'delta_w', 'delta_final_norm': 'delta_w', 'new_m_ffn_pre_norm': 'new_m', 'new_m_ffn_pre_w_gate': 'new_m', 'new_m_ffn_pre_w_up': 'new_m', 'new_m_ffn_pre_w_down': 'new_m', 'new_m_mix_norm': 'new_m', 'new_m_ffn_post_norm': 'new_m', 'new_m_ffn_post_w_gate': 'new_m', 'new_m_ffn_post_w_up': 'new_m', 'new_m_ffn_post_w_down': 'new_m', 'new_m_ab_w_in': 'new_m', 'new_m_ab_conv_w': 'new_m', 'new_m_ab_w_out': 'new_m', 'new_m_c_w_in': 'new_m', 'new_m_c_lower_bounds': 'new_m', 'new_m_c_out_norm': 'new_m', 'new_m_c_w_out': 'new_m', 'new_m_final_norm': 'new_m', 'new_v_ffn_pre_norm': 'new_v', 'new_v_ffn_pre_w_gate': 'new_v', 'new_v_ffn_pre_w_up': 'new_v', 'new_v_ffn_pre_w_down': 'new_v', 'new_v_mix_norm': 'new_v', 'new_v_ffn_post_norm': 'new_v', 'new_v_ffn_post_w_gate': 'new_v', 'new_v_ffn_post_w_up': 'new_v', 'new_v_ffn_post_w_down': 'new_v', 'new_v_ab_w_in': 'new_v', 'new_v_ab_conv_w': 'new_v', 'new_v_ab_w_out': 'new_v', 'new_v_c_w_in': 'new_v', 'new_v_c_lower_bounds': 'new_v', 'new_v_c_out_norm': 'new_v', 'new_v_c_w_out': 'new_v', 'new_v_final_norm': 'new_v'}


def _forward(args):
    return _fwd_reference(*[args[k] for k in FWD_PARAMS])


def _output_shape():
    out = _jax.eval_shape(lambda: _forward(_fwd_setup_inputs(0)))
    return out.shape, out.dtype

N_MICROBATCH = 1
ADAM_LR = 0.001
ADAM_B1 = 0.9
ADAM_B2 = 0.999
ADAM_EPS = 1e-08
ADAM_WD = 0.01
ADAM_STEP = 10
PER_EXAMPLE_BATCH_AXIS = {'x': 0, 'loss_target': 0}
SHARED_INPUTS = []
_WEIGHT_DTYPES = {'ffn_pre_norm': _jnp.float32, 'ffn_pre_w_gate': _jnp.float32, 'ffn_pre_w_up': _jnp.float32, 'ffn_pre_w_down': _jnp.float32, 'mix_norm': _jnp.float32, 'ffn_post_norm': _jnp.float32, 'ffn_post_w_gate': _jnp.float32, 'ffn_post_w_up': _jnp.float32, 'ffn_post_w_down': _jnp.float32, 'ab_w_in': _jnp.float32, 'ab_conv_w': _jnp.float32, 'ab_w_out': _jnp.float32, 'c_w_in': _jnp.float32, 'c_lower_bounds': _jnp.float32, 'c_out_norm': _jnp.float32, 'c_w_out': _jnp.float32, 'final_norm': _jnp.float32}
MOMENT_SCALE = {'ffn_pre_norm': 8.631289e-02, 'ffn_pre_w_gate': 3.733472e-02, 'ffn_pre_w_up': 3.609568e-02, 'ffn_pre_w_down': 5.991300e-02, 'mix_norm': 1.662956e-01, 'ffn_post_norm': 5.592490e-02, 'ffn_post_w_gate': 2.439046e-02, 'ffn_post_w_up': 2.367259e-02, 'ffn_post_w_down': 3.915344e-02, 'ab_w_in': 1.174203e-01, 'ab_conv_w': 1.492441e-01, 'ab_w_out': 1.286597e-01, 'c_w_in': 5.143731e-02, 'c_lower_bounds': 6.915951e-03, 'c_out_norm': 2.112406e-01, 'c_w_out': 7.099115e-02, 'final_norm': 3.197935e+01}


def _to_microbatches(a, axis):
    t = _jnp.moveaxis(a, axis, 0)
    t = t.reshape((N_MICROBATCH, t.shape[0] // N_MICROBATCH) + t.shape[1:])
    return _jnp.moveaxis(t, 1, axis + 1)


def setup_inputs(seed: int = 0) -> dict:
    inp = _fwd_setup_inputs(seed)
    key = _jax.random.fold_in(_jax.random.key(seed), 7919)
    shape, _ = _output_shape()
    out = dict(inp)
    out["loss_target"] = _jax.random.normal(_jax.random.fold_in(key, 0), shape, _jnp.float32)
    for i, name in enumerate(TWIN_WEIGHTS):
        w = inp[name].astype(_jnp.float32)
        if MOMENT_SCALE is None:
            s = _jnp.sqrt(_jnp.mean(_jnp.square(w)) + 1e-30)
        else:
            s = MOMENT_SCALE[name]
        km, kv = _jax.random.split(_jax.random.fold_in(key, i + 1))
        out[name] = w
        out["m_" + name] = s * _jax.random.normal(km, w.shape, _jnp.float32)
        out["v_" + name] = (s * s) * _jax.random.uniform(kv, w.shape, _jnp.float32, 0.5, 1.5)
    if N_MICROBATCH > 1:
        for name, axis in PER_EXAMPLE_BATCH_AXIS.items():
            out[name] = _to_microbatches(out[name], axis)
    return {'x': out['x'], 'ffn_pre_norm': out['ffn_pre_norm'], 'ffn_pre_w_gate': out['ffn_pre_w_gate'], 'ffn_pre_w_up': out['ffn_pre_w_up'], 'ffn_pre_w_down': out['ffn_pre_w_down'], 'mix_norm': out['mix_norm'], 'ffn_post_norm': out['ffn_post_norm'], 'ffn_post_w_gate': out['ffn_post_w_gate'], 'ffn_post_w_up': out['ffn_post_w_up'], 'ffn_post_w_down': out['ffn_post_w_down'], 'ab_w_in': out['ab_w_in'], 'ab_conv_w': out['ab_conv_w'], 'ab_w_out': out['ab_w_out'], 'c_w_in': out['c_w_in'], 'c_lower_bounds': out['c_lower_bounds'], 'c_out_norm': out['c_out_norm'], 'c_w_out': out['c_w_out'], 'final_norm': out['final_norm'], 'loss_target': out['loss_target'], 'm_ffn_pre_norm': out['m_ffn_pre_norm'], 'm_ffn_pre_w_gate': out['m_ffn_pre_w_gate'], 'm_ffn_pre_w_up': out['m_ffn_pre_w_up'], 'm_ffn_pre_w_down': out['m_ffn_pre_w_down'], 'm_mix_norm': out['m_mix_norm'], 'm_ffn_post_norm': out['m_ffn_post_norm'], 'm_ffn_post_w_gate': out['m_ffn_post_w_gate'], 'm_ffn_post_w_up': out['m_ffn_post_w_up'], 'm_ffn_post_w_down': out['m_ffn_post_w_down'], 'm_ab_w_in': out['m_ab_w_in'], 'm_ab_conv_w': out['m_ab_conv_w'], 'm_ab_w_out': out['m_ab_w_out'], 'm_c_w_in': out['m_c_w_in'], 'm_c_lower_bounds': out['m_c_lower_bounds'], 'm_c_out_norm': out['m_c_out_norm'], 'm_c_w_out': out['m_c_w_out'], 'm_final_norm': out['m_final_norm'], 'v_ffn_pre_norm': out['v_ffn_pre_norm'], 'v_ffn_pre_w_gate': out['v_ffn_pre_w_gate'], 'v_ffn_pre_w_up': out['v_ffn_pre_w_up'], 'v_ffn_pre_w_down': out['v_ffn_pre_w_down'], 'v_mix_norm': out['v_mix_norm'], 'v_ffn_post_norm': out['v_ffn_post_norm'], 'v_ffn_post_w_gate': out['v_ffn_post_w_gate'], 'v_ffn_post_w_up': out['v_ffn_post_w_up'], 'v_ffn_post_w_down': out['v_ffn_post_w_down'], 'v_ab_w_in': out['v_ab_w_in'], 'v_ab_conv_w': out['v_ab_conv_w'], 'v_ab_w_out': out['v_ab_w_out'], 'v_c_w_in': out['v_c_w_in'], 'v_c_lower_bounds': out['v_c_lower_bounds'], 'v_c_out_norm': out['v_c_out_norm'], 'v_c_w_out': out['v_c_w_out'], 'v_final_norm': out['v_final_norm']}


def _loss(weights, diff, rest, loss_target):
    with _jax.named_scope("forward"):
        args = {**rest, TWIN_DIFF_INPUT: diff, **{k: w.astype(_WEIGHT_DTYPES[k]) for k, w in weights.items()}}
        y = _forward(args)
    with _jax.named_scope("loss_head"):
        err = _jnp.square(y.astype(_jnp.float32) - loss_target)
        return 0.5 * _jnp.sum(_jnp.mean(err, axis=-1)) if err.ndim else 0.5 * err


def _adamw(w, g, m, v):
    m = ADAM_B1 * m + (1.0 - ADAM_B1) * g
    v = ADAM_B2 * v + (1.0 - ADAM_B2) * _jnp.square(g)
    m_hat = m / (1.0 - ADAM_B1 ** ADAM_STEP)
    v_hat = v / (1.0 - ADAM_B2 ** ADAM_STEP)
    delta = -ADAM_LR * (m_hat / (_jnp.sqrt(v_hat) + ADAM_EPS) + ADAM_WD * w)
    return delta, m, v


def reference(x, ffn_pre_norm, ffn_pre_w_gate, ffn_pre_w_up, ffn_pre_w_down, mix_norm, ffn_post_norm, ffn_post_w_gate, ffn_post_w_up, ffn_post_w_down, ab_w_in, ab_conv_w, ab_w_out, c_w_in, c_lower_bounds, c_out_norm, c_w_out, final_norm, loss_target, m_ffn_pre_norm, m_ffn_pre_w_gate, m_ffn_pre_w_up, m_ffn_pre_w_down, m_mix_norm, m_ffn_post_norm, m_ffn_post_w_gate, m_ffn_post_w_up, m_ffn_post_w_down, m_ab_w_in, m_ab_conv_w, m_ab_w_out, m_c_w_in, m_c_lower_bounds, m_c_out_norm, m_c_w_out, m_final_norm, v_ffn_pre_norm, v_ffn_pre_w_gate, v_ffn_pre_w_up, v_ffn_pre_w_down, v_mix_norm, v_ffn_post_norm, v_ffn_post_w_gate, v_ffn_post_w_up, v_ffn_post_w_down, v_ab_w_in, v_ab_conv_w, v_ab_w_out, v_c_w_in, v_c_lower_bounds, v_c_out_norm, v_c_w_out, v_final_norm):
    given = dict(x=x, ffn_pre_norm=ffn_pre_norm, ffn_pre_w_gate=ffn_pre_w_gate, ffn_pre_w_up=ffn_pre_w_up, ffn_pre_w_down=ffn_pre_w_down, mix_norm=mix_norm, ffn_post_norm=ffn_post_norm, ffn_post_w_gate=ffn_post_w_gate, ffn_post_w_up=ffn_post_w_up, ffn_post_w_down=ffn_post_w_down, ab_w_in=ab_w_in, ab_conv_w=ab_conv_w, ab_w_out=ab_w_out, c_w_in=c_w_in, c_lower_bounds=c_lower_bounds, c_out_norm=c_out_norm, c_w_out=c_w_out, final_norm=final_norm, loss_target=loss_target, m_ffn_pre_norm=m_ffn_pre_norm, m_ffn_pre_w_gate=m_ffn_pre_w_gate, m_ffn_pre_w_up=m_ffn_pre_w_up, m_ffn_pre_w_down=m_ffn_pre_w_down, m_mix_norm=m_mix_norm, m_ffn_post_norm=m_ffn_post_norm, m_ffn_post_w_gate=m_ffn_post_w_gate, m_ffn_post_w_up=m_ffn_post_w_up, m_ffn_post_w_down=m_ffn_post_w_down, m_ab_w_in=m_ab_w_in, m_ab_conv_w=m_ab_conv_w, m_ab_w_out=m_ab_w_out, m_c_w_in=m_c_w_in, m_c_lower_bounds=m_c_lower_bounds, m_c_out_norm=m_c_out_norm, m_c_w_out=m_c_w_out, m_final_norm=m_final_norm, v_ffn_pre_norm=v_ffn_pre_norm, v_ffn_pre_w_gate=v_ffn_pre_w_gate, v_ffn_pre_w_up=v_ffn_pre_w_up, v_ffn_pre_w_down=v_ffn_pre_w_down, v_mix_norm=v_mix_norm, v_ffn_post_norm=v_ffn_post_norm, v_ffn_post_w_gate=v_ffn_post_w_gate, v_ffn_post_w_up=v_ffn_post_w_up, v_ffn_post_w_down=v_ffn_post_w_down, v_ab_w_in=v_ab_w_in, v_ab_conv_w=v_ab_conv_w, v_ab_w_out=v_ab_w_out, v_c_w_in=v_c_w_in, v_c_lower_bounds=v_c_lower_bounds, v_c_out_norm=v_c_out_norm, v_c_w_out=v_c_w_out, v_final_norm=v_final_norm)
    weights = {n: given[n] for n in TWIN_WEIGHTS}
    shared = {n: given[n] for n in SHARED_INPUTS}
    per_example = {n: given[n] for n in ['x']}
    grad_fn = _jax.value_and_grad(_loss, argnums=(0, 1))

    def one_microbatch(ex, loss_target):
        ex = dict(ex)
        diff = ex.pop(TWIN_DIFF_INPUT)
        return grad_fn(weights, diff, {**shared, **ex}, loss_target)

    if N_MICROBATCH == 1:
        loss, (grad_w, grad_x) = one_microbatch(per_example, given["loss_target"])
    else:
        def body(carry, xs):
            loss_sum, grad_sum = carry
            l_k, (gw_k, gx_k) = one_microbatch(xs[0], xs[1])
            with _jax.named_scope("update"):
                return (loss_sum + l_k, _jax.tree.map(_jnp.add, grad_sum, gw_k)), gx_k

        init = (_jnp.zeros((), _jnp.float32), _jax.tree.map(_jnp.zeros_like, weights))
        (loss, grad_w), grad_x = _jax.lax.scan(body, init, (per_example, given["loss_target"]))
    with _jax.named_scope("update"):
        delta_w, new_m, new_v = {}, {}, {}
        for n in TWIN_WEIGHTS:
            delta_w[n], new_m[n], new_v[n] = _adamw(weights[n], grad_w[n], given["m_" + n], given["v_" + n])
    return (loss, grad_x, *[grad_w[n] for n in TWIN_WEIGHTS], *[delta_w[n] for n in TWIN_WEIGHTS],
            *[new_m[n] for n in TWIN_WEIGHTS], *[new_v[n] for n in TWIN_WEIGHTS])
```

```python
import functools
import math

import jax
import jax.numpy as jnp
from jax import lax
from jax.experimental import pallas as pl
from jax.experimental.pallas import tpu as pltpu

F32 = jnp.float32
BF16 = jnp.bfloat16
MESH = pl.DeviceIdType.MESH

RMS_EPS = 1e-6
MACARON = 0.5
LANES = 128
CHUNK = 64
N_LEVELS = 6
SB_KEYS = 128
ADAM_LR, ADAM_B1, ADAM_B2, ADAM_EPS, ADAM_WD, ADAM_STEP = 0.001, 0.9, 0.999, 1e-08, 0.01, 10
VMEM_LIMIT = 48 * 1024 * 1024


def _cp(**kw):
    return pltpu.CompilerParams(vmem_limit_bytes=VMEM_LIMIT, **kw)


def _sigmoid(x):
    return 1.0 / (1.0 + jnp.exp(-x))


def _bf(x):
    return x if x.dtype == BF16 else x.astype(BF16)


def _split3(x):
    hi = x.astype(BF16)
    r1 = x - hi.astype(F32)
    mid = r1.astype(BF16)
    lo = (r1 - mid.astype(F32)).astype(BF16)
    return hi, mid, lo


def _dot(a, b, ca=1, cb=0):
    return lax.dot_general(a, b, (((ca,), (cb,)), ((), ())), preferred_element_type=F32)


def _dot_exact_lhs(m, x):
    hi, mid, lo = _split3(x)
    return _dot(m, hi) + _dot(m, mid) + _dot(m, lo)


def _dot_exact_rhs(x, m):
    hi, mid, lo = _split3(x)
    return _dot(hi, m) + _dot(mid, m) + _dot(lo, m)


def _mm(terms, *, name, ta=False, tb=False, out_dtype=F32, residual=None, alpha=1.0, tm=512, tn=512):
    nt = len(terms)
    a0, b0 = terms[0]
    m = a0.shape[1] if ta else a0.shape[0]
    n = b0.shape[0] if tb else b0.shape[1]
    tm, tn = min(tm, m), min(tn, n)
    assert m % tm == 0 and n % tn == 0, (name, m, n, tm, tn)
    has_res = residual is not None

    def body(*refs):
        o_ref = refs[-1]
        acc = None
        for i in range(nt):
            a = _bf(refs[2 * i][...])
            b = _bf(refs[2 * i + 1][...])
            p = _dot(a, b, 0 if ta else 1, 1 if tb else 0)
            acc = p if acc is None else acc + p
        if alpha != 1.0:
            acc = acc * alpha
        if has_res:
            acc = acc + refs[2 * nt][...]
        o_ref[...] = acc.astype(out_dtype)

    in_specs, args = [], []
    for a, b in terms:
        k = a.shape[0] if ta else a.shape[1]
        assert (b.shape[1] if tb else b.shape[0]) == k, (name, a.shape, b.shape)
        in_specs.append(pl.BlockSpec((k, tm), lambda i, j: (0, i)) if ta else pl.BlockSpec((tm, k), lambda i, j: (i, 0)))
        in_specs.append(pl.BlockSpec((tn, k), lambda i, j: (j, 0)) if tb else pl.BlockSpec((k, tn), lambda i, j: (0, j)))
        args += [a, b]
    if has_res:
        in_specs.append(pl.BlockSpec((tm, tn), lambda i, j: (i, j)))
        args.append(residual)
    return pl.pallas_call(
        body, name=name, grid=(m // tm, n // tn), in_specs=in_specs,
        out_specs=pl.BlockSpec((tm, tn), lambda i, j: (i, j)),
        out_shape=jax.ShapeDtypeStruct((m, n), out_dtype), compiler_params=_cp())(*args)


def _rmsnorm_fwd(x, gain, *, name, tm=512):
    t, d = x.shape
    tm = min(tm, t)

    def body(x_ref, g_ref, o_ref):
        xv = x_ref[...]
        rstd = lax.rsqrt(jnp.mean(xv * xv, axis=-1, keepdims=True) + RMS_EPS)
        o_ref[...] = (xv * rstd * g_ref[...]).astype(BF16)

    return pl.pallas_call(
        body, name=name, grid=(t // tm,),
        in_specs=[pl.BlockSpec((tm, d), lambda i: (i, 0)), pl.BlockSpec((1, d), lambda i: (0, 0))],
        out_specs=pl.BlockSpec((tm, d), lambda i: (i, 0)),
        out_shape=jax.ShapeDtypeStruct((t, d), BF16), compiler_params=_cp())(x, gain)


def _rmsnorm_bwd(x, gain, dxn, dres, *, name, tm=512):
    t, d = x.shape
    tm = min(tm, t)

    def body(x_ref, g_ref, dxn_ref, dres_ref, dx_ref, dg_ref):
        xv = x_ref[...]
        rstd = lax.rsqrt(jnp.mean(xv * xv, axis=-1, keepdims=True) + RMS_EPS)
        xhat = xv * rstd
        dxn_v = dxn_ref[...]
        dxhat = dxn_v * g_ref[...]
        dx = rstd * (dxhat - xhat * jnp.mean(dxhat * xhat, axis=-1, keepdims=True))
        dx_ref[...] = dres_ref[...] + dx

        @pl.when(pl.program_id(0) == 0)
        def _():
            dg_ref[...] = jnp.zeros_like(dg_ref)

        dg_ref[...] += jnp.sum(dxn_v * xhat, axis=0, keepdims=True)

    row = pl.BlockSpec((tm, d), lambda i: (i, 0))
    vec = pl.BlockSpec((1, d), lambda i: (0, 0))
    return pl.pallas_call(
        body, name=name, grid=(t // tm,), in_specs=[row, vec, row, row], out_specs=[row, vec],
        out_shape=[jax.ShapeDtypeStruct((t, d), F32), jax.ShapeDtypeStruct((1, d), F32)],
        compiler_params=_cp())(x, gain, dxn, dres)


def _loss_head(h, gain, target, *, name, tm=512):
    t, d = h.shape
    tm = min(tm, t)

    def body(h_ref, g_ref, t_ref, dh_ref, dg_ref, loss_ref):
        hv = h_ref[...]
        rstd = lax.rsqrt(jnp.mean(hv * hv, axis=-1, keepdims=True) + RMS_EPS)
        xhat = hv * rstd
        err = xhat * g_ref[...] - t_ref[...]
        dy = err * (1.0 / d)
        dxhat = dy * g_ref[...]
        dh_ref[...] = rstd * (dxhat - xhat * jnp.mean(dxhat * xhat, axis=-1, keepdims=True))

        @pl.when(pl.program_id(0) == 0)
        def _():
            dg_ref[...] = jnp.zeros_like(dg_ref)
            loss_ref[...] = jnp.zeros_like(loss_ref)

        dg_ref[...] += jnp.sum(dy * xhat, axis=0, keepdims=True)
        part = jnp.sum(jnp.sum(err * err, axis=-1, keepdims=True), axis=0, keepdims=True) * (0.5 / d)
        loss_ref[...] += jnp.broadcast_to(part, loss_ref.shape)

    row = pl.BlockSpec((tm, d), lambda i: (i, 0))
    vec = pl.BlockSpec((1, d), lambda i: (0, 0))
    return pl.pallas_call(
        body, name=name, grid=(t // tm,), in_specs=[row, vec, row],
        out_specs=[row, vec, pl.BlockSpec((1, LANES), lambda i: (0, 0))],
        out_shape=[jax.ShapeDtypeStruct((t, d), F32), jax.ShapeDtypeStruct((1, d), F32),
                   jax.ShapeDtypeStruct((1, LANES), F32)],
        compiler_params=_cp())(h, gain, target)


def _norm_gate_up(x, gain, wg, wu, *, name, tm=512, tf=1408):
    t, d = x.shape
    f = wg.shape[1]
    tm, tf = min(tm, t), min(tf, f)
    assert f % tf == 0

    def body(x_ref, g_ref, wg_ref, wu_ref, xn_ref, gg_ref, uu_ref, act_ref):
        @pl.when(pl.program_id(1) == 0)
        def _():
            xv = x_ref[...]
            rstd = lax.rsqrt(jnp.mean(xv * xv, axis=-1, keepdims=True) + RMS_EPS)
            xn_ref[...] = (xv * rstd * g_ref[...]).astype(BF16)

        xn = xn_ref[...]
        gv = _dot(xn, wg_ref[...])
        uv = _dot(xn, wu_ref[...])
        gg_ref[...] = gv.astype(BF16)
        uu_ref[...] = uv.astype(BF16)
        act_ref[...] = (gv * _sigmoid(gv) * uv).astype(BF16)

    row = pl.BlockSpec((tm, d), lambda i, j: (i, 0))
    wsp = pl.BlockSpec((d, tf), lambda i, j: (0, j))
    osp = pl.BlockSpec((tm, tf), lambda i, j: (i, j))
    return pl.pallas_call(
        body, name=name, grid=(t // tm, f // tf),
        in_specs=[row, pl.BlockSpec((1, d), lambda i, j: (0, 0)), wsp, wsp],
        out_specs=[row, osp, osp, osp],
        out_shape=[jax.ShapeDtypeStruct((t, d), BF16)] + [jax.ShapeDtypeStruct((t, f), BF16)] * 3,
        compiler_params=_cp())(x, gain, wg, wu)


def _swiglu_bwd(dout, wd, gg, uu, *, name, tm=512, tf=1408):
    t, d = dout.shape
    f = wd.shape[0]
    tm, tf = min(tm, t), min(tf, f)

    def body(do_ref, wd_ref, g_ref, u_ref, dg_ref, du_ref):
        dact = _dot((do_ref[...] * MACARON).astype(BF16), wd_ref[...], 1, 1)
        gv = g_ref[...].astype(F32)
        uv = u_ref[...].astype(F32)
        sg = _sigmoid(gv)
        dg_ref[...] = (dact * uv * (sg * (1.0 + gv * (1.0 - sg)))).astype(BF16)
        du_ref[...] = (dact * (gv * sg)).astype(BF16)

    osp = pl.BlockSpec((tm, tf), lambda i, j: (i, j))
    return pl.pallas_call(
        body, name=name, grid=(t // tm, f // tf),
        in_specs=[pl.BlockSpec((tm, d), lambda i, j: (i, 0)), pl.BlockSpec((tf, d), lambda i, j: (j, 0)), osp, osp],
        out_specs=[osp, osp], out_shape=[jax.ShapeDtypeStruct((t, f), BF16)] * 2,
        compiler_params=_cp())(dout, wd, gg, uu)


def _shift_down(x, n):
    rows = lax.broadcasted_iota(jnp.int32, x.shape, 0)
    return jnp.where(rows >= n, pltpu.roll(x, n, 0), 0.0)


def _shift_up(x, n):
    t = x.shape[0]
    rows = lax.broadcasted_iota(jnp.int32, x.shape, 0)
    return jnp.where(rows < t - n, pltpu.roll(x, t - n, 0), 0.0)


def _conv_fwd(pa, conv_w, *, name):
    t = pa.shape[0]
    nb = pa.shape[1] // 3 // LANES

    def body(b_ref, c_ref, x_ref, w_ref, y_ref):
        u = c_ref[...] * x_ref[...]
        w = w_ref[...]
        conv = w[2:3, :] * u + w[1:2, :] * _shift_down(u, 1) + w[0:1, :] * _shift_down(u, 2)
        y_ref[...] = (b_ref[...] * conv).astype(BF16)

    def col(off):
        return pl.BlockSpec((t, LANES), lambda j: (0, off + j))

    return pl.pallas_call(
        body, name=name, grid=(nb,),
        in_specs=[col(0), col(nb), col(2 * nb), pl.BlockSpec((3, LANES), lambda j: (0, j))],
        out_specs=pl.BlockSpec((t, LANES), lambda j: (0, j)),
        out_shape=jax.ShapeDtypeStruct((t, nb * LANES), BF16), compiler_params=_cp())(pa, pa, pa, conv_w)


def _conv_bwd(pa, dy, conv_w, *, name):
    t = pa.shape[0]
    nb = pa.shape[1] // 3 // LANES

    def body(b_ref, c_ref, x_ref, dy_ref, w_ref, db_ref, dc_ref, dx_ref, dw_ref):
        cv, xv = c_ref[...], x_ref[...]
        u = cv * xv
        u1, u2 = _shift_down(u, 1), _shift_down(u, 2)
        w = w_ref[...]
        conv = w[2:3, :] * u + w[1:2, :] * u1 + w[0:1, :] * u2
        dyv = dy_ref[...]
        db_ref[...] = (dyv * conv).astype(BF16)
        dconv = dyv * b_ref[...]
        du = w[2:3, :] * dconv + w[1:2, :] * _shift_up(dconv, 1) + w[0:1, :] * _shift_up(dconv, 2)
        dc_ref[...] = (du * xv).astype(BF16)
        dx_ref[...] = (du * cv).astype(BF16)
        dw_ref[0:1, :] = jnp.sum(dconv * u2, axis=0, keepdims=True)
        dw_ref[1:2, :] = jnp.sum(dconv * u1, axis=0, keepdims=True)
        dw_ref[2:3, :] = jnp.sum(dconv * u, axis=0, keepdims=True)

    def col(off):
        return pl.BlockSpec((t, LANES), lambda j: (0, off + j))

    osp = pl.BlockSpec((t, LANES), lambda j: (0, j))
    wsp = pl.BlockSpec((3, LANES), lambda j: (0, j))
    return pl.pallas_call(
        body, name=name, grid=(nb,), in_specs=[col(0), col(nb), col(2 * nb), col(0), wsp],
        out_specs=[osp, osp, osp, wsp],
        out_shape=[jax.ShapeDtypeStruct((t, nb * LANES), BF16)] * 3 + [jax.ShapeDtypeStruct((3, nb * LANES), F32)],
        compiler_params=_cp())(pa, pa, pa, dy, conv_w)


def _sb_consts():
    j = lax.broadcasted_iota(jnp.int32, (SB_KEYS, SB_KEYS), 0)
    s = lax.broadcasted_iota(jnp.int32, (SB_KEYS, SB_KEYS), 1)
    ones = jnp.ones((SB_KEYS, SB_KEYS), BF16)
    after = (j > s).astype(BF16)
    upto = (j <= s).astype(BF16)
    before = (j < s).astype(BF16)
    return jnp.concatenate([after, ones], axis=1), jnp.concatenate([upto, before, ones], axis=0).reshape(3, SB_KEYS, SB_KEYS)


def _log_sigmoid(z):
    return jnp.minimum(z, 0.0) - jnp.log(1.0 + jnp.exp(-jnp.abs(z)))


def _attn_fwd(pb, *, name, tq=256):
    t = pb.shape[0]
    npair = pb.shape[1] // 3 // LANES
    tq = min(tq, t)
    cmat, _ = _sb_consts()
    scale = 1.0 / math.sqrt(LANES // 2)

    def body(q_ref, k_ref, v_ref, c_ref, y_ref, lt_ref):
        i = pl.program_id(1)
        lane = lax.broadcasted_iota(jnp.int32, (tq, LANES), 1)
        rowpos = i * tq + lax.broadcasted_iota(jnp.int32, (tq, SB_KEYS), 0)
        colid = lax.broadcasted_iota(jnp.int32, (tq, SB_KEYS), 1)
        q2 = q_ref[...] * jnp.asarray(scale, BF16)
        cm = c_ref[...]
        nkb = (i + 1) * (tq // SB_KEYS)
        outs, tots = [], []
        for hh in range(2):
            qh = jnp.where((lane >= LANES // 2) == (hh == 1), q2, jnp.zeros_like(q2))

            def step(n, carry):
                run, acc = carry
                jb = nkb - 1 - n
                kb = k_ref[pl.ds(pl.multiple_of(jb * SB_KEYS, SB_KEYS), SB_KEYS), :]
                vb = v_ref[pl.ds(pl.multiple_of(jb * SB_KEYS, SB_KEYS), SB_KEYS), :]
                z = _dot(qh, kb, 1, 1)
                mask = (jb * SB_KEYS + colid) < rowpos
                lb = _log_sigmoid(z)
                lk = jnp.where(mask, lb - z, 0.0)
                cs = _dot_exact_rhs(lk, cm)
                w = jnp.where(mask, jnp.exp(lb + run + cs[:, :SB_KEYS]), 0.0)
                acc = acc + _dot(w.astype(BF16), vb)
                return run + cs[:, SB_KEYS:], acc

            run, acc = lax.fori_loop(0, nkb, step, (jnp.zeros((tq, LANES), F32), jnp.zeros((tq, LANES), F32)))
            outs.append(acc)
            tots.append(run)
        hi = lane >= LANES // 2
        y_ref[...] = jnp.where(hi, outs[1], outs[0]).astype(BF16)
        lt_ref[...] = jnp.where(hi, tots[1], tots[0])

    return pl.pallas_call(
        body, name=name, grid=(npair, t // tq),
        in_specs=[pl.BlockSpec((tq, LANES), lambda p, i: (i, p)),
                  pl.BlockSpec((t, LANES), lambda p, i: (0, npair + p)),
                  pl.BlockSpec((t, LANES), lambda p, i: (0, 2 * npair + p)),
                  pl.BlockSpec((SB_KEYS, 2 * SB_KEYS), lambda p, i: (0, 0))],
        out_specs=[pl.BlockSpec((tq, LANES), lambda p, i: (i, p))] * 2,
        out_shape=[jax.ShapeDtypeStruct((t, npair * LANES), BF16), jax.ShapeDtypeStruct((t, npair * LANES), F32)],
        compiler_params=_cp())(pb, pb, pb, cmat)


def _attn_bwd(pb, dy, ltot, *, name, tq=256):
    t = pb.shape[0]
    npair = pb.shape[1] // 3 // LANES
    tq = min(tq, t)
    nq = t // tq
    _, cmats = _sb_consts()
    scale = 1.0 / math.sqrt(LANES // 2)

    def body(q_ref, k_ref, v_ref, dy_ref, lt_ref, c_ref, dq_ref, dk_ref, dv_ref, dk_acc, dv_acc):
        i = pl.program_id(1)

        @pl.when(i == 0)
        def _():
            dk_acc[...] = jnp.zeros_like(dk_acc)
            dv_acc[...] = jnp.zeros_like(dv_acc)

        lane = lax.broadcasted_iota(jnp.int32, (tq, LANES), 1)
        rowpos = i * tq + lax.broadcasted_iota(jnp.int32, (tq, SB_KEYS), 0)
        colid = lax.broadcasted_iota(jnp.int32, (tq, SB_KEYS), 1)
        q2 = q_ref[...] * jnp.asarray(scale, BF16)
        do2 = dy_ref[...].astype(BF16)
        ltv = lt_ref[...]
        upto, before, ones = c_ref[0], c_ref[1], c_ref[2]
        nkb = (i + 1) * (tq // SB_KEYS)
        dq_tot = jnp.zeros((tq, LANES), F32)
        for hh in range(2):
            sel = (lane >= LANES // 2) == (hh == 1)
            qh = jnp.where(sel, q2, jnp.zeros_like(q2))
            doh = jnp.where(sel, do2, jnp.zeros_like(do2))
            lt_h = _dot_exact_rhs(jnp.where(sel, ltv, 0.0), ones) * (2.0 / LANES)

            def step(jb, carry):
                csum, prun, dq = carry
                rows = pl.ds(pl.multiple_of(jb * SB_KEYS, SB_KEYS), SB_KEYS)
                kb = k_ref[rows, :]
                vb = v_ref[rows, :]
                z = _dot(qh, kb, 1, 1)
                mask = (jb * SB_KEYS + colid) < rowpos
                lb = _log_sigmoid(z)
                lk = jnp.where(mask, lb - z, 0.0)
                hi, mid, lo = _split3(lk)
                incl = _dot(hi, upto) + _dot(mid, upto) + _dot(lo, upto)
                ltot_blk = _dot(hi, ones) + _dot(mid, ones) + _dot(lo, ones)
                a = jnp.where(mask, jnp.exp(lb + (lt_h - csum - incl)), 0.0)
                e = a * _dot(doh, vb, 1, 1)
                ehi, emid, elo = _split3(e)
                pex = prun + _dot(ehi, before) + _dot(emid, before) + _dot(elo, before)
                etot = _dot(ehi, ones) + _dot(emid, ones) + _dot(elo, ones)
                beta = jnp.exp(lb)
                dz = jnp.where(mask, e * (1.0 - beta) - pex * beta, 0.0).astype(BF16)
                dq = dq + _dot(dz, kb)
                dk_acc[rows, :] += _dot(dz, qh, 0, 0)
                dv_acc[rows, :] += _dot(a.astype(BF16), doh, 0, 0)
                return csum + ltot_blk, prun + etot, dq

            zero = jnp.zeros((tq, LANES), F32)
            _, _, dqh = lax.fori_loop(0, nkb, step, (zero, zero, zero))
            dq_tot = dq_tot + jnp.where(sel, dqh, 0.0)
        dq_ref[...] = (dq_tot * scale).astype(BF16)

        @pl.when(i == nq - 1)
        def _():
            dk_ref[...] = dk_acc[...].astype(BF16)
            dv_ref[...] = dv_acc[...].astype(BF16)

    blk = pl.BlockSpec((tq, LANES), lambda p, i: (i, p))
    full = pl.BlockSpec((t, LANES), lambda p, i: (0, p))
    return pl.pallas_call(
        body, name=name, grid=(npair, nq),
        in_specs=[blk,
                  pl.BlockSpec((t, LANES), lambda p, i: (0, npair + p)),
                  pl.BlockSpec((t, LANES), lambda p, i: (0, 2 * npair + p)),
                  pl.BlockSpec((tq, LANES), lambda p, i: (i, npair + p)),
                  blk,
                  pl.BlockSpec((3, SB_KEYS, SB_KEYS), lambda p, i: (0, 0, 0))],
        out_specs=[blk, full, full],
        out_shape=[jax.ShapeDtypeStruct((t, npair * LANES), BF16)] * 3,
        scratch_shapes=[pltpu.VMEM((t, LANES), F32), pltpu.VMEM((t, LANES), F32)],
        compiler_params=_cp())(pb, pb, pb, dy, ltot, cmats)


def _hgrn_consts():
    t = lax.broadcasted_iota(jnp.int32, (CHUNK, CHUNK), 0)
    s = lax.broadcasted_iota(jnp.int32, (CHUNK, CHUNK), 1)
    tri = (s <= t)
    cum = [tri.astype(F32)]
    masks = []
    for lvl in range(N_LEVELS):
        half = CHUNK >> (lvl + 1)
        ref_row = (t // (2 * half)) * (2 * half) + half - 1
        cum.append((s <= ref_row).astype(F32))
        same = (t // (2 * half)) == (s // (2 * half))
        masks.append((same & (t % (2 * half) >= half) & (s % (2 * half) < half)).astype(F32))
    masks.append((t == s).astype(F32))
    cum_all = jnp.concatenate(cum, axis=0).astype(BF16)
    suffix = (s >= t).astype(BF16)
    return cum_all, jnp.stack(masks), suffix


def _hgrn_gates(qr, fr, lbv):
    sg = _sigmoid(fr)
    fval = lbv + (1.0 - lbv) * sg
    kk = (1.0 - lbv) * _sigmoid(-fr)
    sq = _sigmoid(qr)
    return sg, fval, jnp.log(fval), kk, sq, qr * sq


def _lower_bound(c_ref):
    c = c_ref[...]
    mx = jnp.max(c, axis=0, keepdims=True)
    ex = jnp.exp(c - mx)
    return ex[1:2, :] / jnp.sum(ex, axis=0, keepdims=True)


def _hgrn_levels(ball, qs, kk):
    b = ball[:CHUNK]
    out = []
    for lvl in range(N_LEVELS):
        bref = ball[(lvl + 1) * CHUNK:(lvl + 2) * CHUNK]
        eq = jnp.exp(jnp.minimum(b - bref, 0.0))
        ek = jnp.exp(jnp.minimum(bref - b, 0.0))
        out.append((qs * eq, kk * ek, eq, ek))
    out.append((qs, kk, None, None))
    return out


def _split2(x):
    hi = x.astype(BF16)
    return hi, (x - hi.astype(F32)).astype(BF16)


def _hgrn_fwd(pc, c_lb, out_norm, *, name, tc=512):
    t = pc.shape[0]
    nh = pc.shape[1] // 4 // LANES
    tc = min(tc, t)
    nch = tc // CHUNK
    cum_all, masks, _ = _hgrn_consts()

    def body(q_ref, f_ref, i_ref, g_ref, lb_ref, on_ref, cum_ref, m_ref, y_ref, o_ref, st_ref, state):
        @pl.when(pl.program_id(1) == 0)
        def _():
            state[...] = jnp.zeros_like(state)

        lbv = _lower_bound(lb_ref)
        onv = on_ref[...]

        def chunk(c, carry):
            rows = pl.ds(pl.multiple_of(c * CHUNK, CHUNK), CHUNK)
            _, _, g, kk, _, qs = _hgrn_gates(q_ref[rows, :], f_ref[rows, :], lbv)
            vb = i_ref[rows, :].astype(BF16)
            ball = _dot_exact_lhs(cum_ref[...], g)
            b = ball[:CHUNK]
            scores = jnp.zeros((CHUNK, CHUNK), F32)
            for lvl, (ql, kl, _, _) in enumerate(_hgrn_levels(ball, qs, kk)):
                scores = scores + _dot(ql.astype(BF16), kl.astype(BF16), 1, 1) * m_ref[lvl]
            st = state[...]
            st_ref[c] = st
            o = _dot(scores.astype(BF16), vb) + _dot((qs * jnp.exp(b)).astype(BF16), st.astype(BF16), 1, 1)
            blast = b[CHUNK - 1:CHUNK, :]
            kdec = (kk * jnp.exp(blast - b)).astype(BF16)
            state[...] = st * jnp.exp(blast) + _dot(vb, kdec, 0, 0)
            o_ref[rows, :] = o
            rstd = lax.rsqrt(jnp.mean(o * o, axis=-1, keepdims=True) + RMS_EPS)
            gate = g_ref[rows, :]
            y_ref[rows, :] = (o * rstd * onv * (gate * _sigmoid(gate))).astype(BF16)
            return carry

        lax.fori_loop(0, nch, chunk, 0)

    def col(off):
        return pl.BlockSpec((tc, LANES), lambda h, i: (i, off + h))

    osp = pl.BlockSpec((tc, LANES), lambda h, i: (i, h))
    return pl.pallas_call(
        body, name=name, grid=(nh, t // tc),
        in_specs=[col(0), col(nh), col(2 * nh), col(3 * nh),
                  pl.BlockSpec((2, LANES), lambda h, i: (0, h)),
                  pl.BlockSpec((1, LANES), lambda h, i: (0, 0)),
                  pl.BlockSpec(cum_all.shape, lambda h, i: (0, 0)),
                  pl.BlockSpec(masks.shape, lambda h, i: (0, 0, 0))],
        out_specs=[osp, osp, pl.BlockSpec((None, nch, LANES, LANES), lambda h, i: (h, i, 0, 0))],
        out_shape=[jax.ShapeDtypeStruct((t, nh * LANES), BF16), jax.ShapeDtypeStruct((t, nh * LANES), F32),
                   jax.ShapeDtypeStruct((nh, t // CHUNK, LANES, LANES), F32)],
        scratch_shapes=[pltpu.VMEM((LANES, LANES), F32)],
        compiler_params=_cp())(pc, pc, pc, pc, c_lb, out_norm, cum_all, masks)


def _hgrn_bwd(pc, o_saved, states, dy, c_lb, out_norm, *, name, tc=512):
    t = pc.shape[0]
    nh = pc.shape[1] // 4 // LANES
    tc = min(tc, t)
    nch = tc // CHUNK
    nt = t // tc
    cum_all, masks, suffix = _hgrn_consts()

    def body(q_ref, f_ref, i_ref, g_ref, o_ref, st_ref, dy_ref, lb_ref, on_ref, cum_ref, m_ref, suf_ref,
             dq_ref, df_ref, di_ref, dg_ref, dlb_ref, don_ref, dstate):
        @pl.when(pl.program_id(1) == 0)
        def _():
            dstate[...] = jnp.zeros_like(dstate)
            dlb_ref[...] = jnp.zeros_like(dlb_ref)
            don_ref[...] = jnp.zeros_like(don_ref)

        lbv = _lower_bound(lb_ref)
        onv = on_ref[...]

        def chunk(n, carry):
            c = nch - 1 - n
            rows = pl.ds(pl.multiple_of(c * CHUNK, CHUNK), CHUNK)
            qr = q_ref[rows, :]
            sg, fval, g, kk, sq, qs = _hgrn_gates(qr, f_ref[rows, :], lbv)
            vb = i_ref[rows, :].astype(BF16)
            o = o_ref[rows, :]
            gate = g_ref[rows, :]
            sgt = _sigmoid(gate)
            rstd = lax.rsqrt(jnp.mean(o * o, axis=-1, keepdims=True) + RMS_EPS)
            ohat = o * rstd
            dyv = dy_ref[rows, :]
            don = dyv * (gate * sgt)
            dg_ref[rows, :] = (dyv * ohat * onv * (sgt * (1.0 + gate * (1.0 - sgt)))).astype(BF16)
            don_ref[...] += jnp.sum(don * ohat, axis=0, keepdims=True)
            dxhat = don * onv
            dob = (rstd * (dxhat - ohat * jnp.mean(dxhat * ohat, axis=-1, keepdims=True))).astype(BF16)
            ball = _dot_exact_lhs(cum_ref[...], g)
            b = ball[:CHUNK]
            blast = b[CHUNK - 1:CHUNK, :]
            eb = jnp.exp(b)
            edec = jnp.exp(blast - b)
            st32 = st_ref[c]
            st = st32.astype(BF16)
            dst = dstate[...]
            dstb = dst.astype(BF16)
            da = _dot(dob, vb, 1, 1)
            levels = _hgrn_levels(ball, qs, kk)
            scores = jnp.zeros((CHUNK, CHUNK), F32)
            dq = eb * _dot(dob, st)
            dk_inter = edec * _dot(vb, dstb)
            dk = dk_inter
            for lvl, (ql, kl, eq, ek) in enumerate(levels):
                mk = m_ref[lvl]
                (qh, qlo), (kh, klo) = _split2(ql), _split2(kl)
                scores = scores + _dot(qh, kh, 1, 1) * mk
                dal = (da * mk).astype(BF16)
                dql = _dot(dal, kh) + _dot(dal, klo)
                dkl = _dot(dal, qh, 0, 0) + _dot(dal, qlo, 0, 0)
                dq = dq + (dql if eq is None else dql * eq)
                dk = dk + (dkl if ek is None else dkl * ek)
            kdec = (kk * edec).astype(BF16)
            dv = _dot(scores.astype(BF16), dob, 0, 0) + _dot(kdec, dstb, 1, 1)
            dstate[...] = dst * jnp.exp(blast) + _dot(dob, (qs * eb).astype(BF16), 0, 0)
            db = qs * dq - kk * dk
            last = jnp.sum(kk * dk_inter, axis=0, keepdims=True) + jnp.exp(blast) * jnp.sum(dst * st32, axis=0, keepdims=True)
            dgl = _dot_exact_lhs(suf_ref[...], db) + last
            dfv = dgl / fval - dk
            df_ref[rows, :] = (dfv * (1.0 - lbv) * sg * (1.0 - sg)).astype(BF16)
            dlb_ref[...] += jnp.sum(dfv * (1.0 - sg), axis=0, keepdims=True)
            dq_ref[rows, :] = (dq * (sq * (1.0 + qr * (1.0 - sq)))).astype(BF16)
            di_ref[rows, :] = dv.astype(BF16)
            return carry

        lax.fori_loop(0, nch, chunk, 0)

    def col(off):
        return pl.BlockSpec((tc, LANES), lambda h, i: (nt - 1 - i, off + h))

    osp = pl.BlockSpec((tc, LANES), lambda h, i: (nt - 1 - i, h))
    vec = pl.BlockSpec((1, LANES), lambda h, i: (0, h))
    return pl.pallas_call(
        body, name=name, grid=(nh, nt),
        in_specs=[col(0), col(nh), col(2 * nh), col(3 * nh), osp,
                  pl.BlockSpec((None, nch, LANES, LANES), lambda h, i: (h, nt - 1 - i, 0, 0)),
                  osp,
                  pl.BlockSpec((2, LANES), lambda h, i: (0, h)),
                  pl.BlockSpec((1, LANES), lambda h, i: (0, 0)),
                  pl.BlockSpec(cum_all.shape, lambda h, i: (0, 0)),
                  pl.BlockSpec(masks.shape, lambda h, i: (0, 0, 0)),
                  pl.BlockSpec(suffix.shape, lambda h, i: (0, 0))],
        out_specs=[osp, osp, osp, osp, vec, vec],
        out_shape=[jax.ShapeDtypeStruct((t, nh * LANES), BF16)] * 4 + [jax.ShapeDtypeStruct((1, nh * LANES), F32)] * 2,
        scratch_shapes=[pltpu.VMEM((LANES, LANES), F32)],
        compiler_params=_cp())(pc, pc, pc, pc, o_saved, states, dy, c_lb, out_norm, cum_all, masks, suffix)


HBM_SPEC = pl.BlockSpec(memory_space=pltpu.HBM)


def _all_gather(xs, *, name):
    def body(x_ref, out_ref, send_sems, recv_sems, local_sem):
        x, y, c = lax.axis_index("x"), lax.axis_index("y"), lax.axis_index("c")
        me, sibling = (x, y, c), (x, y, 1 - c)
        chips = [(1 - x, y), (x, 1 - y), (1 - x, 1 - y)]

        def rows(px, py, pc):
            return out_ref.at[4 * px + 2 * py + pc]

        def copy(k, block, to, src=None):
            return pltpu.make_async_remote_copy(
                src_ref=rows(*block) if src is None else src, dst_ref=rows(*block),
                send_sem=send_sems.at[k], recv_sem=recv_sems.at[k], device_id=to, device_id_type=MESH)

        mine = pltpu.make_async_copy(x_ref, rows(*me), local_sem)
        mine.start()
        first = [copy(0, me, sibling, src=x_ref)]
        first += [copy(1 + j, me, (*chip, c), src=x_ref) for j, chip in enumerate(chips)]
        for cp in first:
            cp.start()
        passed = [copy(4 + j, (*chip, c), sibling) for j, chip in enumerate(chips)]
        for j, chip in enumerate(chips):
            copy(1 + j, (*chip, c), me).wait_recv()
            passed[j].start()
        copy(0, sibling, me).wait_recv()
        for j, chip in enumerate(chips):
            copy(4 + j, (*chip, 1 - c), me).wait_recv()
        for cp in first + passed:
            cp.wait_send()
        mine.wait()

    return pl.pallas_call(
        body, name=name, in_specs=[HBM_SPEC], out_specs=HBM_SPEC,
        out_shape=jax.ShapeDtypeStruct((8,) + xs.shape, xs.dtype),
        scratch_shapes=[pltpu.SemaphoreType.DMA((7,)), pltpu.SemaphoreType.DMA((7,)), pltpu.SemaphoreType.DMA])(xs)


def _sibling_exchange(s, *, name):
    def body(s_ref, rb_ref, send_sem, recv_sem):
        x, y, c = lax.axis_index("x"), lax.axis_index("y"), lax.axis_index("c")
        cp = pltpu.make_async_remote_copy(
            src_ref=s_ref.at[1 - c], dst_ref=rb_ref, send_sem=send_sem, recv_sem=recv_sem,
            device_id=(x, y, 1 - c), device_id_type=MESH)
        cp.start()
        cp.wait()

    return pl.pallas_call(
        body, name=name, in_specs=[HBM_SPEC], out_specs=HBM_SPEC,
        out_shape=jax.ShapeDtypeStruct(s.shape[1:], s.dtype),
        scratch_shapes=[pltpu.SemaphoreType.DMA, pltpu.SemaphoreType.DMA])(s)


def _pair_add(s, rb, core, *, name, tb=512):
    _, n, c = s.shape
    assert n % tb == 0

    def body(core_ref, a_ref, b_ref, o_ref):
        o_ref[...] = (a_ref[...].astype(F32) + b_ref[...].astype(F32)).astype(BF16)

    return pl.pallas_call(
        body, name=name,
        grid_spec=pltpu.PrefetchScalarGridSpec(
            num_scalar_prefetch=1, grid=(n // tb,),
            in_specs=[pl.BlockSpec((None, tb, c), lambda i, cr: (cr[0], i, 0)),
                      pl.BlockSpec((tb, c), lambda i, cr: (i, 0))],
            out_specs=pl.BlockSpec((tb, c), lambda i, cr: (i, 0))),
        out_shape=jax.ShapeDtypeStruct((n, c), BF16), compiler_params=_cp())(core, s, rb)


def _chip_exchange(p, *, name):
    def body(p_ref, out_ref, send_sems, recv_sems, local_sem):
        x, y, c = lax.axis_index("x"), lax.axis_index("y"), lax.axis_index("c")
        mine = 2 * x + y
        own = pltpu.make_async_copy(p_ref.at[mine], out_ref.at[mine], local_sem)
        own.start()
        copies = []
        for k, (tx, ty) in enumerate([(1 - x, y), (x, 1 - y), (1 - x, 1 - y)]):
            copies.append(pltpu.make_async_remote_copy(
                src_ref=p_ref.at[2 * tx + ty], dst_ref=out_ref.at[mine],
                send_sem=send_sems.at[k], recv_sem=recv_sems.at[k], device_id=(tx, ty, c), device_id_type=MESH))
        for cp in copies:
            cp.start()
        for cp in copies:
            cp.wait()
        own.wait()

    return pl.pallas_call(
        body, name=name, in_specs=[HBM_SPEC], out_specs=HBM_SPEC,
        out_shape=jax.ShapeDtypeStruct(p.shape, p.dtype),
        scratch_shapes=[pltpu.SemaphoreType.DMA((3,)), pltpu.SemaphoreType.DMA((3,)), pltpu.SemaphoreType.DMA])(p)


def _adamw_math(w, g, m, v):
    m2 = ADAM_B1 * m + (1.0 - ADAM_B1) * g
    v2 = ADAM_B2 * v + (1.0 - ADAM_B2) * (g * g)
    m_hat = m2 / (1.0 - ADAM_B1 ** ADAM_STEP)
    v_hat = v2 / (1.0 - ADAM_B2 ** ADAM_STEP)
    return -ADAM_LR * (m_hat / (jnp.sqrt(v_hat) + ADAM_EPS) + ADAM_WD * w), m2, v2


def _adamw_shard(parts, off, w, m, v, *, name):
    n, c = w.shape
    tb = next(b for b in (512, 384, 352, 256, 128, 64, 16) if n % b == 0 and off % b == 0)

    def body(p0, p1, p2, p3, w_ref, m_ref, v_ref, g_out, d_out, m_out, v_out):
        g = ((p0[...].astype(F32) + p1[...].astype(F32)) + p2[...].astype(F32)) + p3[...].astype(F32)
        d, m2, v2 = _adamw_math(w_ref[...], g, m_ref[...], v_ref[...])
        g_out[...] = g
        d_out[...] = d
        m_out[...] = m2
        v_out[...] = v2

    def part(ch):
        return pl.BlockSpec((None, tb, c), lambda i: (ch, off // tb + i, 0))

    row = pl.BlockSpec((tb, c), lambda i: (i, 0))
    return pl.pallas_call(
        body, name=name, grid=(n // tb,), in_specs=[part(0), part(1), part(2), part(3), row, row, row],
        out_specs=[row] * 4, out_shape=[jax.ShapeDtypeStruct((n, c), F32)] * 4,
        compiler_params=_cp())(parts, parts, parts, parts, w, m, v)


SMALL_ROWS = 16
ROW_LB = 7


def _small_update(gath, w, m, v, *, name):
    def body(g_ref, w_ref, m_ref, v_ref, g_out, d_out, m_out, v_out):
        tot = g_ref[0]
        for k in range(1, 8):
            tot = tot + g_ref[k]
        wv = w_ref[...]
        c0, c1 = wv[ROW_LB:ROW_LB + 1, :], wv[ROW_LB + 1:ROW_LB + 2, :]
        mx = jnp.maximum(c0, c1)
        e0, e1 = jnp.exp(c0 - mx), jnp.exp(c1 - mx)
        lb = e1 / (e0 + e1)
        gl = tot[ROW_LB:ROW_LB + 1, :] * lb * (1.0 - lb)
        row = lax.broadcasted_iota(jnp.int32, tot.shape, 0)
        g = jnp.where(row == ROW_LB, -gl, jnp.where(row == ROW_LB + 1, gl, tot))
        d, m2, v2 = _adamw_math(wv, g, m_ref[...], v_ref[...])
        g_out[...] = g
        d_out[...] = d
        m_out[...] = m2
        v_out[...] = v2

    return pl.pallas_call(
        body, name=name, out_shape=[jax.ShapeDtypeStruct(w.shape, F32)] * 4, compiler_params=_cp())(gath, w, m, v)


D_MODEL = 1024
PACK_ROWS = 5632
CONV_ROW = 5376


def _ffn_fwd(h, gain, wg, wu, wd, tag):
    xn, gg, uu, act = _norm_gate_up(h, gain, wg, wu, name=f"{tag}_gate_up")
    out = _mm([(act, wd)], residual=h, alpha=MACARON, tn=1024, name=f"{tag}_down")
    return out, (h, xn, gg, uu, act)


def _ffn_bwd(dout, saved, gain, wg, wu, wd, tag):
    h, xn, gg, uu, act = saved
    dg, du = _swiglu_bwd(dout, wd, gg, uu, name=f"{tag}_dact")
    dwd = _mm([(act, dout)], ta=True, alpha=MACARON, tm=256, tn=512, name=f"{tag}_dwd")
    dwg = _mm([(xn, dg)], ta=True, tm=512, tn=256, name=f"{tag}_dwg")
    dwu = _mm([(xn, du)], ta=True, tm=512, tn=256, name=f"{tag}_dwu")
    dxn = _mm([(dg, wg), (du, wu)], tb=True, tm=256, tn=512, name=f"{tag}_dxn")
    dh, dgain = _rmsnorm_bwd(h, gain, dxn, dout, name=f"{tag}_norm_bwd")
    return dh, dwg, dwu, dwd, dgain


def kernel(x, ffn_pre_norm, ffn_pre_w_gate, ffn_pre_w_up, ffn_pre_w_down, mix_norm, ffn_post_norm, ffn_post_w_gate, ffn_post_w_up, ffn_post_w_down, ab_w_in, ab_conv_w, ab_w_out, c_w_in, c_lower_bounds, c_out_norm, c_w_out, final_norm, loss_target, m_ffn_pre_norm, m_ffn_pre_w_gate, m_ffn_pre_w_up, m_ffn_pre_w_down, m_mix_norm, m_ffn_post_norm, m_ffn_post_w_gate, m_ffn_post_w_up, m_ffn_post_w_down, m_ab_w_in, m_ab_conv_w, m_ab_w_out, m_c_w_in, m_c_lower_bounds, m_c_out_norm, m_c_w_out, m_final_norm, v_ffn_pre_norm, v_ffn_pre_w_gate, v_ffn_pre_w_up, v_ffn_pre_w_down, v_mix_norm, v_ffn_post_norm, v_ffn_post_w_gate, v_ffn_post_w_up, v_ffn_post_w_down, v_ab_w_in, v_ab_conv_w, v_ab_w_out, v_c_w_in, v_c_lower_bounds, v_c_out_norm, v_c_w_out, v_final_norm):
    d = D_MODEL
    h0 = x[0]
    target = loss_target[0]
    core = lax.axis_index("c").astype(jnp.int32).reshape(1)

    big = [("pre_g", ffn_pre_w_gate, m_ffn_pre_w_gate, v_ffn_pre_w_gate),
           ("pre_u", ffn_pre_w_up, m_ffn_pre_w_up, v_ffn_pre_w_up),
           ("pre_d", ffn_pre_w_down, m_ffn_pre_w_down, v_ffn_pre_w_down),
           ("post_g", ffn_post_w_gate, m_ffn_post_w_gate, v_ffn_post_w_gate),
           ("post_u", ffn_post_w_up, m_ffn_post_w_up, v_ffn_post_w_up),
           ("post_d", ffn_post_w_down, m_ffn_post_w_down, v_ffn_post_w_down),
           ("ab_in", ab_w_in, m_ab_w_in, v_ab_w_in),
           ("ab_out", ab_w_out, m_ab_w_out, v_ab_w_out),
           ("c_in", c_w_in, m_c_w_in, v_c_w_in),
           ("c_out", c_w_out, m_c_w_out, v_c_w_out)]
    offs, off = {}, 0
    for tag, w, _, _ in big:
        offs[tag] = off
        off += w.size // d
    assert off == CONV_ROW

    def conv_rows(a, split):
        flat = a.reshape(-1)
        if split:
            hi = flat.astype(BF16)
            flat = jnp.concatenate([hi, (flat - hi.astype(F32)).astype(BF16)])
        return jnp.zeros((16, d), flat.dtype).at[0, :flat.shape[0]].set(flat)

    nconv = ab_conv_w.size
    wpack = jnp.concatenate(
        [w.reshape(-1, d).astype(BF16) for _, w, _, _ in big]
        + [conv_rows(ab_conv_w, True), jnp.zeros((PACK_ROWS - CONV_ROW - 16, d), BF16)], axis=0)
    gath = _all_gather(wpack, name="gather_weights")

    def col_w(tag, layer, ncol):
        o = offs[tag] + layer * ncol
        return gath[:, o:o + ncol, :].reshape(8, d, ncol).transpose(1, 0, 2).reshape(d, 8 * ncol)

    def row_w(tag, layer, nrow):
        o = offs[tag] + layer * nrow
        return gath[:, o:o + nrow, :].reshape(8 * nrow, d)

    f_loc = ffn_pre_w_gate.shape[2]
    ffn_w = {}
    for pos in ("pre", "post"):
        for layer in range(2):
            ffn_w[pos, layer] = (col_w(f"{pos}_g", layer, f_loc), col_w(f"{pos}_u", layer, f_loc),
                                 row_w(f"{pos}_d", layer, f_loc))
    w_ab_in = col_w("ab_in", 0, ab_w_in.shape[2])
    w_ab_out = row_w("ab_out", 0, ab_w_out.shape[1])
    w_c_in = col_w("c_in", 0, c_w_in.shape[2])
    w_c_out = row_w("c_out", 0, c_w_out.shape[1])
    cg = gath[:, CONV_ROW, :2 * nconv].astype(F32)
    conv_w = (cg[:, :nconv] + cg[:, nconv:]).reshape(8, 3, -1).transpose(1, 0, 2).reshape(3, -1)
    half = w_ab_in.shape[1] // 2
    w_a_in, w_b_in = w_ab_in[:, :half], w_ab_in[:, half:]
    aw = half // 3

    h1, s_pre0 = _ffn_fwd(h0, ffn_pre_norm[0:1], *ffn_w["pre", 0], "l0pre")
    hn0 = _rmsnorm_fwd(h1, mix_norm[0:1], name="l0_mix_norm")
    pa = _mm([(hn0, w_a_in)], tn=512, name="ab_proj_a")
    pb = _mm([(hn0, w_b_in)], tn=512, out_dtype=BF16, name="ab_proj_b")
    ya = _conv_fwd(pa, conv_w, name="conv_fwd")
    yb, ltot = _attn_fwd(pb, name="attn_fwd")
    h2 = _mm([(ya, w_ab_out[:aw]), (yb, w_ab_out[aw:])], residual=h1, tn=1024, name="ab_out")
    h3, s_post0 = _ffn_fwd(h2, ffn_post_norm[0:1], *ffn_w["post", 0], "l0post")
    h4, s_pre1 = _ffn_fwd(h3, ffn_pre_norm[1:2], *ffn_w["pre", 1], "l1pre")
    hn1 = _rmsnorm_fwd(h4, mix_norm[1:2], name="l1_mix_norm")
    pc = _mm([(hn1, w_c_in)], tn=512, name="c_proj")
    yc, o_saved, states = _hgrn_fwd(pc, c_lower_bounds, c_out_norm, name="hgrn_fwd")
    h5 = _mm([(yc, w_c_out)], residual=h4, tn=1024, name="c_out")
    h6, s_post1 = _ffn_fwd(h5, ffn_post_norm[1:2], *ffn_w["post", 1], "l1post")
    dh6, d_final, loss_vec = _loss_head(h6, final_norm.reshape(1, d), target, name="loss_head")

    gw = {}
    dh5, gw["post_g", 1], gw["post_u", 1], gw["post_d", 1], d_post1 = _ffn_bwd(
        dh6, s_post1, ffn_post_norm[1:2], *ffn_w["post", 1], "l1post")
    dyc = _mm([(dh5, w_c_out)], tb=True, tn=512, name="c_out_dy")
    g_c_out = _mm([(yc, dh5)], ta=True, name="c_out_dw")
    dcq, dcf, dci, dcg, dlb, d_onorm = _hgrn_bwd(pc, o_saved, states, dyc, c_lower_bounds, c_out_norm, name="hgrn_bwd")
    dparts = [dcq, dcf, dci, dcg]
    g_c_in = jnp.concatenate([_mm([(hn1, dp)], ta=True, name=f"c_in_dw{i}") for i, dp in enumerate(dparts)], axis=1)
    cw = w_c_in.shape[1] // 4
    dhn1 = _mm([(dp, w_c_in[:, i * cw:(i + 1) * cw]) for i, dp in enumerate(dparts)], tb=True, tm=256, name="c_in_dx")
    dh4, d_mix1 = _rmsnorm_bwd(h4, mix_norm[1:2], dhn1, dh5, name="l1_mix_norm_bwd")
    dh3, gw["pre_g", 1], gw["pre_u", 1], gw["pre_d", 1], d_pre1 = _ffn_bwd(
        dh4, s_pre1, ffn_pre_norm[1:2], *ffn_w["pre", 1], "l1pre")
    dh2, gw["post_g", 0], gw["post_u", 0], gw["post_d", 0], d_post0 = _ffn_bwd(
        dh3, s_post0, ffn_post_norm[0:1], *ffn_w["post", 0], "l0post")
    dyab = _mm([(dh2, w_ab_out)], tb=True, tn=512, name="ab_out_dy")
    g_ab_out = jnp.concatenate([_mm([(ya, dh2)], ta=True, name="ab_out_dw_a"),
                                _mm([(yb, dh2)], ta=True, name="ab_out_dw_b")], axis=0)
    dab, dac, dax, g_conv = _conv_bwd(pa, dyab, conv_w, name="conv_bwd")
    dq, dk, dv = _attn_bwd(pb, dyab, ltot, name="attn_bwd")
    dparts = [dab, dac, dax, dq, dk, dv]
    g_ab_in = jnp.concatenate([_mm([(hn0, dp)], ta=True, name=f"ab_in_dw{i}") for i, dp in enumerate(dparts)], axis=1)
    dhn0 = _mm([(dp, w_ab_in[:, i * aw:(i + 1) * aw]) for i, dp in enumerate(dparts)], tb=True, tm=256, name="ab_in_dx")
    dh1, d_mix0 = _rmsnorm_bwd(h1, mix_norm[0:1], dhn0, dh2, name="l0_mix_norm_bwd")
    dh0, gw["pre_g", 0], gw["pre_u", 0], gw["pre_d", 0], d_pre0 = _ffn_bwd(
        dh1, s_pre0, ffn_pre_norm[0:1], *ffn_w["pre", 0], "l0pre")

    def col_g(g):
        ncol = g.shape[1] // 8
        return g.reshape(d, 8, ncol).transpose(1, 0, 2).reshape(8, ncol, d)

    def row_g(g):
        return g.reshape(8, g.shape[0] // 8, d)

    pieces = []
    for pos in ("pre", "post"):
        for kind, fn in (("g", col_g), ("u", col_g), ("d", row_g)):
            pieces += [fn(gw[f"{pos}_{kind}", 0]), fn(gw[f"{pos}_{kind}", 1])]
    pieces += [col_g(g_ab_in), row_g(g_ab_out), col_g(g_c_in), row_g(g_c_out)]
    gconv_own = g_conv.reshape(3, 8, -1).transpose(1, 0, 2).reshape(8, -1)
    pieces.append(jnp.zeros((8, 16, d), F32).at[:, 0, :nconv].set(gconv_own))
    pieces.append(jnp.zeros((8, PACK_ROWS - CONV_ROW - 16, d), F32))
    gpack = jnp.concatenate([p.astype(BF16) for p in pieces], axis=1)
    send = gpack.reshape(4, 2, PACK_ROWS, d).transpose(1, 0, 2, 3)
    from_sibling = _sibling_exchange(send, name="grad_sibling_exchange")
    chip_part = _pair_add(send.reshape(2, 4 * PACK_ROWS, d), from_sibling.reshape(4 * PACK_ROWS, d), core,
                          name="grad_pair_add").reshape(4, PACK_ROWS, d)
    parts = _chip_exchange(chip_part, name="grad_chip_exchange")

    upd = {}
    for tag, w, m, v in big:
        res = _adamw_shard(parts, offs[tag], w.reshape(-1, d), m.reshape(-1, d), v.reshape(-1, d), name=f"adamw_{tag}")
        upd[tag] = [r.reshape(w.shape) for r in res]
    res = _adamw_shard(parts, CONV_ROW, conv_rows(ab_conv_w, False), conv_rows(m_ab_conv_w, False),
                       conv_rows(v_ab_conv_w, False), name="adamw_conv")
    upd["conv"] = [r[0, :nconv].reshape(ab_conv_w.shape) for r in res]

    def small_pack(pre, mix, post, final, lbs, onorm):
        on = jnp.zeros((1, d), F32).at[:, :onorm.shape[1]].set(onorm)
        rows = jnp.concatenate([pre, mix, post, final.reshape(1, d), lbs, on], axis=0)
        return jnp.concatenate([rows, jnp.zeros((SMALL_ROWS - rows.shape[0], d), F32)], axis=0)

    dlb2 = jnp.concatenate([dlb, jnp.zeros_like(dlb)], axis=0)
    d_on = d_onorm.reshape(-1, c_out_norm.shape[1]).sum(axis=0, keepdims=True)
    gsmall = small_pack(jnp.concatenate([d_pre0, d_pre1], 0), jnp.concatenate([d_mix0, d_mix1], 0),
                        jnp.concatenate([d_post0, d_post1], 0), d_final, dlb2, d_on)
    gsmall_all = _all_gather(gsmall, name="gather_small_grads")
    sres = _small_update(
        gsmall_all,
        small_pack(ffn_pre_norm, mix_norm, ffn_post_norm, final_norm, c_lower_bounds, c_out_norm),
        small_pack(m_ffn_pre_norm, m_mix_norm, m_ffn_post_norm, m_final_norm, m_c_lower_bounds, m_c_out_norm),
        small_pack(v_ffn_pre_norm, v_mix_norm, v_ffn_post_norm, v_final_norm, v_c_lower_bounds, v_c_out_norm),
        name="small_update")

    def small_out(r):
        return {"pre_norm": r[0:2], "mix_norm": r[2:4], "post_norm": r[4:6], "final": r[6],
                "lb": r[ROW_LB:ROW_LB + 2], "onorm": r[9:10, :c_out_norm.shape[1]]}

    small = [small_out(r) for r in sres]
    outs = []
    for k in range(4):
        s = small[k]
        outs += [s["pre_norm"], upd["pre_g"][k], upd["pre_u"][k], upd["pre_d"][k], s["mix_norm"], s["post_norm"],
                 upd["post_g"][k], upd["post_u"][k], upd["post_d"][k], upd["ab_in"][k], upd["conv"][k],
                 upd["ab_out"][k], upd["c_in"][k], s["lb"], s["onorm"], upd["c_out"][k], s["final"]]
    loss = lax.psum(loss_vec[0, 0], ("x", "y", "c"))
    return (loss, dh0[None], *outs)
```

```python
import functools
import math

import jax
import jax.numpy as jnp
from jax import lax
from jax.experimental import pallas as pl
from jax.experimental.pallas import tpu as pltpu

F32 = jnp.float32
BF16 = jnp.bfloat16
MESH = pl.DeviceIdType.MESH

RMS_EPS = 1e-6
MACARON = 0.5
LANES = 128
CHUNK = 64
N_LEVELS = 6
SB_KEYS = 256
ADAM_LR, ADAM_B1, ADAM_B2, ADAM_EPS, ADAM_WD, ADAM_STEP = 0.001, 0.9, 0.999, 1e-08, 0.01, 10
VMEM_LIMIT = 48 * 1024 * 1024


def _cp(**kw):
    return pltpu.CompilerParams(vmem_limit_bytes=VMEM_LIMIT, **kw)


def _sigmoid(x):
    return 1.0 / (1.0 + jnp.exp(-x))


def _bf(x):
    return x if x.dtype == BF16 else x.astype(BF16)


def _split3(x):
    hi = x.astype(BF16)
    r1 = x - hi.astype(F32)
    mid = r1.astype(BF16)
    lo = (r1 - mid.astype(F32)).astype(BF16)
    return hi, mid, lo


def _dot(a, b, ca=1, cb=0):
    return lax.dot_general(a, b, (((ca,), (cb,)), ((), ())), preferred_element_type=F32)


def _dot_exact_lhs(m, x):
    hi, mid, lo = _split3(x)
    return _dot(m, hi) + _dot(m, mid) + _dot(m, lo)


def _dot_exact_rhs(x, m):
    hi, mid, lo = _split3(x)
    return _dot(hi, m) + _dot(mid, m) + _dot(lo, m)


def _mm(terms, *, name, ta=False, tb=False, out_dtype=F32, residual=None, alpha=1.0, tm=512, tn=512):
    nt = len(terms)
    a0, b0 = terms[0]
    m = a0.shape[1] if ta else a0.shape[0]
    n = b0.shape[0] if tb else b0.shape[1]
    tm, tn = min(tm, m), min(tn, n)
    assert m % tm == 0 and n % tn == 0, (name, m, n, tm, tn)
    has_res = residual is not None

    def body(*refs):
        o_ref = refs[-1]
        acc = None
        for i in range(nt):
            a = _bf(refs[2 * i][...])
            b = _bf(refs[2 * i + 1][...])
            p = _dot(a, b, 0 if ta else 1, 1 if tb else 0)
            acc = p if acc is None else acc + p
        if alpha != 1.0:
            acc = acc * alpha
        if has_res:
            acc = acc + refs[2 * nt][...]
        o_ref[...] = acc.astype(out_dtype)

    in_specs, args = [], []
    for a, b in terms:
        k = a.shape[0] if ta else a.shape[1]
        assert (b.shape[1] if tb else b.shape[0]) == k, (name, a.shape, b.shape)
        in_specs.append(pl.BlockSpec((k, tm), lambda i, j: (0, i)) if ta else pl.BlockSpec((tm, k), lambda i, j: (i, 0)))
        in_specs.append(pl.BlockSpec((tn, k), lambda i, j: (j, 0)) if tb else pl.BlockSpec((k, tn), lambda i, j: (0, j)))
        args += [a, b]
    if has_res:
        in_specs.append(pl.BlockSpec((tm, tn), lambda i, j: (i, j)))
        args.append(residual)
    return pl.pallas_call(
        body, name=name, grid=(m // tm, n // tn), in_specs=in_specs,
        out_specs=pl.BlockSpec((tm, tn), lambda i, j: (i, j)),
        out_shape=jax.ShapeDtypeStruct((m, n), out_dtype), compiler_params=_cp())(*args)


def _rmsnorm_fwd(x, gain, *, name, tm=512):
    t, d = x.shape
    tm = min(tm, t)

    def body(x_ref, g_ref, o_ref):
        xv = x_ref[...]
        rstd = lax.rsqrt(jnp.mean(xv * xv, axis=-1, keepdims=True) + RMS_EPS)
        o_ref[...] = (xv * rstd * g_ref[...]).astype(BF16)

    return pl.pallas_call(
        body, name=name, grid=(t // tm,),
        in_specs=[pl.BlockSpec((tm, d), lambda i: (i, 0)), pl.BlockSpec((1, d), lambda i: (0, 0))],
        out_specs=pl.BlockSpec((tm, d), lambda i: (i, 0)),
        out_shape=jax.ShapeDtypeStruct((t, d), BF16), compiler_params=_cp())(x, gain)


def _rmsnorm_bwd(x, gain, dxn, dres, *, name, tm=512):
    t, d = x.shape
    tm = min(tm, t)

    def body(x_ref, g_ref, dxn_ref, dres_ref, dx_ref, dg_ref):
        xv = x_ref[...]
        rstd = lax.rsqrt(jnp.mean(xv * xv, axis=-1, keepdims=True) + RMS_EPS)
        xhat = xv * rstd
        dxn_v = dxn_ref[...]
        dxhat = dxn_v * g_ref[...]
        dx = rstd * (dxhat - xhat * jnp.mean(dxhat * xhat, axis=-1, keepdims=True))
        dx_ref[...] = dres_ref[...] + dx

        @pl.when(pl.program_id(0) == 0)
        def _():
            dg_ref[...] = jnp.zeros_like(dg_ref)

        dg_ref[...] += jnp.sum(dxn_v * xhat, axis=0, keepdims=True)

    row = pl.BlockSpec((tm, d), lambda i: (i, 0))
    vec = pl.BlockSpec((1, d), lambda i: (0, 0))
    return pl.pallas_call(
        body, name=name, grid=(t // tm,), in_specs=[row, vec, row, row], out_specs=[row, vec],
        out_shape=[jax.ShapeDtypeStruct((t, d), F32), jax.ShapeDtypeStruct((1, d), F32)],
        compiler_params=_cp())(x, gain, dxn, dres)


def _loss_head(h, gain, target, *, name, tm=512):
    t, d = h.shape
    tm = min(tm, t)

    def body(h_ref, g_ref, t_ref, dh_ref, dg_ref, loss_ref):
        hv = h_ref[...]
        rstd = lax.rsqrt(jnp.mean(hv * hv, axis=-1, keepdims=True) + RMS_EPS)
        xhat = hv * rstd
        err = xhat * g_ref[...] - t_ref[...]
        dy = err * (1.0 / d)
        dxhat = dy * g_ref[...]
        dh_ref[...] = rstd * (dxhat - xhat * jnp.mean(dxhat * xhat, axis=-1, keepdims=True))

        @pl.when(pl.program_id(0) == 0)
        def _():
            dg_ref[...] = jnp.zeros_like(dg_ref)
            loss_ref[...] = jnp.zeros_like(loss_ref)

        dg_ref[...] += jnp.sum(dy * xhat, axis=0, keepdims=True)
        part = jnp.sum(jnp.sum(err * err, axis=-1, keepdims=True), axis=0, keepdims=True) * (0.5 / d)
        loss_ref[...] += jnp.broadcast_to(part, loss_ref.shape)

    row = pl.BlockSpec((tm, d), lambda i: (i, 0))
    vec = pl.BlockSpec((1, d), lambda i: (0, 0))
    return pl.pallas_call(
        body, name=name, grid=(t // tm,), in_specs=[row, vec, row],
        out_specs=[row, vec, pl.BlockSpec((1, LANES), lambda i: (0, 0))],
        out_shape=[jax.ShapeDtypeStruct((t, d), F32), jax.ShapeDtypeStruct((1, d), F32),
                   jax.ShapeDtypeStruct((1, LANES), F32)],
        compiler_params=_cp())(h, gain, target)


def _norm_gate_up(x, gain, wg, wu, *, name, tm=512, tf=1408):
    t, d = x.shape
    f = wg.shape[1]
    tm, tf = min(tm, t), min(tf, f)
    assert f % tf == 0

    def body(x_ref, g_ref, wg_ref, wu_ref, xn_ref, gg_ref, uu_ref, act_ref):
        @pl.when(pl.program_id(1) == 0)
        def _():
            xv = x_ref[...]
            rstd = lax.rsqrt(jnp.mean(xv * xv, axis=-1, keepdims=True) + RMS_EPS)
            xn_ref[...] = (xv * rstd * g_ref[...]).astype(BF16)

        xn = xn_ref[...]
        gv = _dot(xn, wg_ref[...])
        uv = _dot(xn, wu_ref[...])
        gg_ref[...] = gv.astype(BF16)
        uu_ref[...] = uv.astype(BF16)
        act_ref[...] = (gv * _sigmoid(gv) * uv).astype(BF16)

    row = pl.BlockSpec((tm, d), lambda i, j: (i, 0))
    wsp = pl.BlockSpec((d, tf), lambda i, j: (0, j))
    osp = pl.BlockSpec((tm, tf), lambda i, j: (i, j))
    return pl.pallas_call(
        body, name=name, grid=(t // tm, f // tf),
        in_specs=[row, pl.BlockSpec((1, d), lambda i, j: (0, 0)), wsp, wsp],
        out_specs=[row, osp, osp, osp],
        out_shape=[jax.ShapeDtypeStruct((t, d), BF16)] + [jax.ShapeDtypeStruct((t, f), BF16)] * 3,
        compiler_params=_cp())(x, gain, wg, wu)


def _swiglu_bwd(dout, wd, gg, uu, *, name, tm=512, tf=1408):
    t, d = dout.shape
    f = wd.shape[0]
    tm, tf = min(tm, t), min(tf, f)

    def body(do_ref, wd_ref, g_ref, u_ref, dg_ref, du_ref):
        dact = _dot((do_ref[...] * MACARON).astype(BF16), wd_ref[...], 1, 1)
        gv = g_ref[...].astype(F32)
        uv = u_ref[...].astype(F32)
        sg = _sigmoid(gv)
        dg_ref[...] = (dact * uv * (sg * (1.0 + gv * (1.0 - sg)))).astype(BF16)
        du_ref[...] = (dact * (gv * sg)).astype(BF16)

    osp = pl.BlockSpec((tm, tf), lambda i, j: (i, j))
    return pl.pallas_call(
        body, name=name, grid=(t // tm, f // tf),
        in_specs=[pl.BlockSpec((tm, d), lambda i, j: (i, 0)), pl.BlockSpec((tf, d), lambda i, j: (j, 0)), osp, osp],
        out_specs=[osp, osp], out_shape=[jax.ShapeDtypeStruct((t, f), BF16)] * 2,
        compiler_params=_cp())(dout, wd, gg, uu)


def _shift_down(x, n):
    rows = lax.broadcasted_iota(jnp.int32, x.shape, 0)
    return jnp.where(rows >= n, pltpu.roll(x, n, 0), 0.0)


def _shift_up(x, n):
    t = x.shape[0]
    rows = lax.broadcasted_iota(jnp.int32, x.shape, 0)
    return jnp.where(rows < t - n, pltpu.roll(x, t - n, 0), 0.0)


def _conv_fwd(pa, conv_w, *, name):
    t = pa.shape[0]
    nb = pa.shape[1] // 3 // LANES

    def body(b_ref, c_ref, x_ref, w_ref, y_ref):
        u = c_ref[...] * x_ref[...]
        w = w_ref[...]
        conv = w[2:3, :] * u + w[1:2, :] * _shift_down(u, 1) + w[0:1, :] * _shift_down(u, 2)
        y_ref[...] = (b_ref[...] * conv).astype(BF16)

    def col(off):
        return pl.BlockSpec((t, LANES), lambda j: (0, off + j))

    return pl.pallas_call(
        body, name=name, grid=(nb,),
        in_specs=[col(0), col(nb), col(2 * nb), pl.BlockSpec((3, LANES), lambda j: (0, j))],
        out_specs=pl.BlockSpec((t, LANES), lambda j: (0, j)),
        out_shape=jax.ShapeDtypeStruct((t, nb * LANES), BF16), compiler_params=_cp())(pa, pa, pa, conv_w)


def _conv_bwd(pa, dy, conv_w, *, name):
    t = pa.shape[0]
    nb = pa.shape[1] // 3 // LANES

    def body(b_ref, c_ref, x_ref, dy_ref, w_ref, db_ref, dc_ref, dx_ref, dw_ref):
        cv, xv = c_ref[...], x_ref[...]
        u = cv * xv
        u1, u2 = _shift_down(u, 1), _shift_down(u, 2)
        w = w_ref[...]
        conv = w[2:3, :] * u + w[1:2, :] * u1 + w[0:1, :] * u2
        dyv = dy_ref[...]
        db_ref[...] = (dyv * conv).astype(BF16)
        dconv = dyv * b_ref[...]
        du = w[2:3, :] * dconv + w[1:2, :] * _shift_up(dconv, 1) + w[0:1, :] * _shift_up(dconv, 2)
        dc_ref[...] = (du * xv).astype(BF16)
        dx_ref[...] = (du * cv).astype(BF16)
        dw_ref[0:1, :] = jnp.sum(dconv * u2, axis=0, keepdims=True)
        dw_ref[1:2, :] = jnp.sum(dconv * u1, axis=0, keepdims=True)
        dw_ref[2:3, :] = jnp.sum(dconv * u, axis=0, keepdims=True)

    def col(off):
        return pl.BlockSpec((t, LANES), lambda j: (0, off + j))

    osp = pl.BlockSpec((t, LANES), lambda j: (0, j))
    wsp = pl.BlockSpec((3, LANES), lambda j: (0, j))
    return pl.pallas_call(
        body, name=name, grid=(nb,), in_specs=[col(0), col(nb), col(2 * nb), col(0), wsp],
        out_specs=[osp, osp, osp, wsp],
        out_shape=[jax.ShapeDtypeStruct((t, nb * LANES), BF16)] * 3 + [jax.ShapeDtypeStruct((3, nb * LANES), F32)],
        compiler_params=_cp())(pa, pa, pa, dy, conv_w)


def _sb_consts():
    j = lax.broadcasted_iota(jnp.int32, (SB_KEYS, SB_KEYS), 0)
    s = lax.broadcasted_iota(jnp.int32, (SB_KEYS, SB_KEYS), 1)
    after = (j > s).astype(BF16)
    upto = (j <= s).astype(BF16)
    before = (j < s).astype(BF16)
    return after, jnp.stack([upto, before])


def _log_sigmoid(z):
    return jnp.minimum(z, 0.0) - jnp.log(1.0 + jnp.exp(-jnp.abs(z)))


def _attn_fwd(pb, *, name, tq=256):
    t = pb.shape[0]
    npair = pb.shape[1] // 3 // LANES
    tq = min(tq, t)
    cmat, _ = _sb_consts()
    scale = 1.0 / math.sqrt(LANES // 2)

    def body(q_ref, k_ref, v_ref, c_ref, y_ref, lt_ref):
        i = pl.program_id(1)
        lane = lax.broadcasted_iota(jnp.int32, (tq, LANES), 1)
        rowpos = i * tq + lax.broadcasted_iota(jnp.int32, (tq, SB_KEYS), 0)
        colid = lax.broadcasted_iota(jnp.int32, (tq, SB_KEYS), 1)
        q2 = q_ref[...] * jnp.asarray(scale, BF16)
        cm = c_ref[...]
        hi_lanes = lane >= LANES // 2
        qhs = [jnp.where(hi_lanes == (hh == 1), q2, jnp.zeros_like(q2)) for hh in range(2)]
        per_q = tq // SB_KEYS

        def blk(jb):
            return pl.ds(pl.multiple_of(jb * SB_KEYS, SB_KEYS), SB_KEYS)

        def scores(jb):
            kb = k_ref[blk(jb), :]
            return tuple(_dot(qhs[hh], kb, 1, 1) for hh in range(2))

        def weights(jb, zs, runs, masked):
            mask = (jb * SB_KEYS + colid) < rowpos if masked else None
            ws, new_runs = [], []
            for hh in range(2):
                lb = _log_sigmoid(zs[hh])
                lk = lb - zs[hh]
                if masked:
                    lk = jnp.where(mask, lk, 0.0)
                lk_hi, lk_lo = _split2(lk)
                cs = _dot(lk_hi, cm) + _dot(lk_lo, cm)
                w = jnp.exp(lb + runs[hh] + cs)
                if masked:
                    w = jnp.where(mask, w, 0.0)
                ws.append(w.astype(BF16))
                new_runs.append(runs[hh] + jnp.sum(lk, axis=1, keepdims=True))
            return tuple(ws), tuple(new_runs)

        def values(jb, accs, ws):
            vb = v_ref[blk(jb), :]
            return tuple(accs[hh] + _dot(ws[hh], vb) for hh in range(2))

        zero = jnp.zeros((tq, LANES), F32)
        zero_col = jnp.zeros((tq, 1), F32)
        nfull = i * per_q
        runs, accs, ws = (zero_col, zero_col), (zero, zero), None
        zs = scores(nfull + per_q - 1)
        for dblk in reversed(range(per_q)):
            jb = nfull + dblk
            zs_next = scores(jnp.maximum(jb - 1, 0))
            if ws is not None:
                accs = values(jb + 1, accs, ws)
            ws, runs = weights(jb, zs, runs, True)
            zs = zs_next

        def full_block(n, carry):
            zs, ws, runs, accs = carry
            jb = nfull - 1 - n
            zs_next = scores(jnp.maximum(jb - 1, 0))
            accs = values(jb + 1, accs, ws)
            ws, runs = weights(jb, zs, runs, False)
            return zs_next, ws, runs, accs

        _, ws, runs, accs = lax.fori_loop(0, nfull, full_block, (zs, ws, runs, accs))
        accs = values(0, accs, ws)
        y_ref[...] = jnp.where(hi_lanes, accs[1], accs[0]).astype(BF16)
        lt_ref[...] = jnp.where(hi_lanes, runs[1], runs[0])

    return pl.pallas_call(
        body, name=name, grid=(npair, t // tq),
        in_specs=[pl.BlockSpec((tq, LANES), lambda p, i: (i, p)),
                  pl.BlockSpec((t, LANES), lambda p, i: (0, npair + p)),
                  pl.BlockSpec((t, LANES), lambda p, i: (0, 2 * npair + p)),
                  pl.BlockSpec((SB_KEYS, SB_KEYS), lambda p, i: (0, 0))],
        out_specs=[pl.BlockSpec((tq, LANES), lambda p, i: (i, p))] * 2,
        out_shape=[jax.ShapeDtypeStruct((t, npair * LANES), BF16), jax.ShapeDtypeStruct((t, npair * LANES), F32)],
        compiler_params=_cp())(pb, pb, pb, cmat)


def _attn_bwd(pb, dy, ltot, *, name, tq=256):
    t = pb.shape[0]
    npair = pb.shape[1] // 3 // LANES
    tq = min(tq, t)
    nq = t // tq
    _, cmats = _sb_consts()
    scale = 1.0 / math.sqrt(LANES // 2)

    def body(q_ref, k_ref, v_ref, dy_ref, lt_ref, c_ref, dq_ref, dk_ref, dv_ref, dk_acc, dv_acc):
        i = pl.program_id(1)

        @pl.when(i == 0)
        def _():
            dk_acc[...] = jnp.zeros_like(dk_acc)
            dv_acc[...] = jnp.zeros_like(dv_acc)

        lane = lax.broadcasted_iota(jnp.int32, (tq, LANES), 1)
        rowpos = i * tq + lax.broadcasted_iota(jnp.int32, (tq, SB_KEYS), 0)
        colid = lax.broadcasted_iota(jnp.int32, (tq, SB_KEYS), 1)
        q2 = q_ref[...] * jnp.asarray(scale, BF16)
        do2 = dy_ref[...].astype(BF16)
        ltv = lt_ref[...]
        c_upto, c_before = c_ref[0], c_ref[1]
        hi_lanes = lane >= LANES // 2
        sels = [hi_lanes == (hh == 1) for hh in range(2)]
        qhs = [jnp.where(s, q2, jnp.zeros_like(q2)) for s in sels]
        dohs = [jnp.where(s, do2, jnp.zeros_like(do2)) for s in sels]
        lts = [ltv[:, 0:1], ltv[:, LANES // 2:LANES // 2 + 1]]
        per_q = tq // SB_KEYS

        def blk(jb):
            return pl.ds(pl.multiple_of(jb * SB_KEYS, SB_KEYS), SB_KEYS)

        def scores(jb):
            kb, vb = k_ref[blk(jb), :], v_ref[blk(jb), :]
            return tuple((_dot(qhs[hh], kb, 1, 1), _dot(dohs[hh], vb, 1, 1)) for hh in range(2))

        def products(jb, dqs, pend):
            kb = k_ref[blk(jb), :]
            dk = _dot(pend[0][0], qhs[0], 0, 0) + _dot(pend[1][0], qhs[1], 0, 0)
            dv = _dot(pend[0][1], dohs[0], 0, 0) + _dot(pend[1][1], dohs[1], 0, 0)
            dk_acc[blk(jb), :] += dk
            dv_acc[blk(jb), :] += dv
            return tuple(dqs[hh] + _dot(pend[hh][0], kb) for hh in range(2))

        def chain(jb, zs, sums, masked):
            mask = (jb * SB_KEYS + colid) < rowpos if masked else None
            pend, new_sums = [], []
            for hh in range(2):
                z, da = zs[hh]
                csum, prun = sums[hh]
                lb = _log_sigmoid(z)
                lk = lb - z
                if masked:
                    lk = jnp.where(mask, lk, 0.0)
                lk_hi, lk_lo = _split2(lk)
                cs = _dot(lk_hi, c_upto) + _dot(lk_lo, c_upto)
                a = jnp.exp(lb + ((lts[hh] - csum) - cs))
                if masked:
                    a = jnp.where(mask, a, 0.0)
                e = a * da
                e_hi, e_lo = _split2(e)
                ce = _dot(e_hi, c_before) + _dot(e_lo, c_before)
                beta = jnp.exp(lb)
                dz = e * (1.0 - beta) - (prun + ce) * beta
                if masked:
                    dz = jnp.where(mask, dz, 0.0)
                pend.append((dz.astype(BF16), a.astype(BF16)))
                new_sums.append((csum + jnp.sum(lk, axis=1, keepdims=True), prun + jnp.sum(e, axis=1, keepdims=True)))
            return tuple(pend), tuple(new_sums)

        zero = jnp.zeros((tq, LANES), F32)
        zero_b = jnp.zeros((tq, SB_KEYS), BF16)
        nfull = i * per_q
        last = nfull + per_q - 1

        def full_block(jb, carry):
            zs, pend, sums, dqs = carry
            zs_next = scores(jb + 1)
            dqs = products(jnp.maximum(jb - 1, 0), dqs, pend)
            pend, sums = chain(jb, zs, sums, False)
            return zs_next, pend, sums, dqs

        zero_col = jnp.zeros((tq, 1), F32)
        carry = (scores(0), ((zero_b, zero_b),) * 2, ((zero_col, zero_col),) * 2, (zero, zero))
        zs, pend, sums, dqs = lax.fori_loop(0, nfull, full_block, carry)
        for dblk in range(per_q):
            jb = nfull + dblk
            zs_next = scores(jnp.minimum(jb + 1, last))
            dqs = products(jnp.maximum(jb - 1, 0), dqs, pend)
            pend, sums = chain(jb, zs, sums, True)
            zs = zs_next
        dqs = products(last, dqs, pend)
        dq_ref[...] = (jnp.where(hi_lanes, dqs[1], dqs[0]) * scale).astype(BF16)

        @pl.when(i == nq - 1)
        def _():
            dk_ref[...] = dk_acc[...].astype(BF16)
            dv_ref[...] = dv_acc[...].astype(BF16)

    blk = pl.BlockSpec((tq, LANES), lambda p, i: (i, p))
    full = pl.BlockSpec((t, LANES), lambda p, i: (0, p))
    return pl.pallas_call(
        body, name=name, grid=(npair, nq),
        in_specs=[blk,
                  pl.BlockSpec((t, LANES), lambda p, i: (0, npair + p)),
                  pl.BlockSpec((t, LANES), lambda p, i: (0, 2 * npair + p)),
                  pl.BlockSpec((tq, LANES), lambda p, i: (i, npair + p)),
                  blk,
                  pl.BlockSpec((2, SB_KEYS, SB_KEYS), lambda p, i: (0, 0, 0))],
        out_specs=[blk, full, full],
        out_shape=[jax.ShapeDtypeStruct((t, npair * LANES), BF16)] * 3,
        scratch_shapes=[pltpu.VMEM((t, LANES), F32), pltpu.VMEM((t, LANES), F32)],
        compiler_params=_cp())(pb, pb, pb, dy, ltot, cmats)


def _hgrn_consts():
    t = lax.broadcasted_iota(jnp.int32, (CHUNK, CHUNK), 0)
    s = lax.broadcasted_iota(jnp.int32, (CHUNK, CHUNK), 1)
    tri = (s <= t)
    cum = [tri.astype(F32)]
    masks = []
    for lvl in range(N_LEVELS):
        half = CHUNK >> (lvl + 1)
        ref_row = (t // (2 * half)) * (2 * half) + half - 1
        cum.append((s <= ref_row).astype(F32))
        same = (t // (2 * half)) == (s // (2 * half))
        masks.append((same & (t % (2 * half) >= half) & (s % (2 * half) < half)).astype(F32))
    masks.append((t == s).astype(F32))
    cum_all = jnp.concatenate(cum, axis=0).astype(BF16)
    suffix = (s >= t).astype(BF16)
    return cum_all, jnp.stack(masks), suffix


def _hgrn_gates(qr, fr, lbv):
    sg = _sigmoid(fr)
    fval = lbv + (1.0 - lbv) * sg
    kk = (1.0 - lbv) * _sigmoid(-fr)
    sq = _sigmoid(qr)
    return sg, fval, jnp.log(fval), kk, sq, qr * sq


def _lower_bound(c_ref):
    c = c_ref[...]
    mx = jnp.max(c, axis=0, keepdims=True)
    ex = jnp.exp(c - mx)
    return ex[1:2, :] / jnp.sum(ex, axis=0, keepdims=True)


def _hgrn_levels(ball, qs, kk):
    b = ball[:CHUNK]
    out = []
    for lvl in range(N_LEVELS):
        bref = ball[(lvl + 1) * CHUNK:(lvl + 2) * CHUNK]
        eq = jnp.exp(jnp.minimum(b - bref, 0.0))
        ek = jnp.exp(jnp.minimum(bref - b, 0.0))
        out.append((qs * eq, kk * ek, eq, ek))
    out.append((qs, kk, None, None))
    return out


def _split2(x):
    hi = x.astype(BF16)
    return hi, (x - hi.astype(F32)).astype(BF16)


def _hgrn_fwd(pc, c_lb, out_norm, *, name, tc=512):
    t = pc.shape[0]
    nh = pc.shape[1] // 4 // LANES
    tc = min(tc, t)
    nch = tc // CHUNK
    cum_all, masks, _ = _hgrn_consts()

    def body(q_ref, f_ref, i_ref, g_ref, lb_ref, on_ref, cum_ref, m_ref, y_ref, o_ref, st_ref, state):
        @pl.when(pl.program_id(1) == 0)
        def _():
            state[...] = jnp.zeros_like(state)

        lbv = _lower_bound(lb_ref)
        onv = on_ref[...]

        def chunk(c, carry):
            rows = pl.ds(pl.multiple_of(c * CHUNK, CHUNK), CHUNK)
            _, _, g, kk, _, qs = _hgrn_gates(q_ref[rows, :], f_ref[rows, :], lbv)
            vb = i_ref[rows, :].astype(BF16)
            ball = _dot_exact_lhs(cum_ref[...], g)
            b = ball[:CHUNK]
            scores = jnp.zeros((CHUNK, CHUNK), F32)
            for lvl, (ql, kl, _, _) in enumerate(_hgrn_levels(ball, qs, kk)):
                scores = scores + _dot(ql.astype(BF16), kl.astype(BF16), 1, 1) * m_ref[lvl]
            st = state[...]
            st_ref[c] = st
            o = _dot(scores.astype(BF16), vb) + _dot((qs * jnp.exp(b)).astype(BF16), st.astype(BF16), 1, 1)
            blast = b[CHUNK - 1:CHUNK, :]
            kdec = (kk * jnp.exp(blast - b)).astype(BF16)
            state[...] = st * jnp.exp(blast) + _dot(vb, kdec, 0, 0)
            o_ref[rows, :] = o
            rstd = lax.rsqrt(jnp.mean(o * o, axis=-1, keepdims=True) + RMS_EPS)
            gate = g_ref[rows, :]
            y_ref[rows, :] = (o * rstd * onv * (gate * _sigmoid(gate))).astype(BF16)
            return carry

        lax.fori_loop(0, nch, chunk, 0)

    def col(off):
        return pl.BlockSpec((tc, LANES), lambda h, i: (i, off + h))

    osp = pl.BlockSpec((tc, LANES), lambda h, i: (i, h))
    return pl.pallas_call(
        body, name=name, grid=(nh, t // tc),
        in_specs=[col(0), col(nh), col(2 * nh), col(3 * nh),
                  pl.BlockSpec((2, LANES), lambda h, i: (0, h)),
                  pl.BlockSpec((1, LANES), lambda h, i: (0, 0)),
                  pl.BlockSpec(cum_all.shape, lambda h, i: (0, 0)),
                  pl.BlockSpec(masks.shape, lambda h, i: (0, 0, 0))],
        out_specs=[osp, osp, pl.BlockSpec((None, nch, LANES, LANES), lambda h, i: (h, i, 0, 0))],
        out_shape=[jax.ShapeDtypeStruct((t, nh * LANES), BF16), jax.ShapeDtypeStruct((t, nh * LANES), F32),
                   jax.ShapeDtypeStruct((nh, t // CHUNK, LANES, LANES), F32)],
        scratch_shapes=[pltpu.VMEM((LANES, LANES), F32)],
        compiler_params=_cp())(pc, pc, pc, pc, c_lb, out_norm, cum_all, masks)


def _hgrn_bwd(pc, o_saved, states, dy, c_lb, out_norm, *, name, tc=512):
    t = pc.shape[0]
    nh = pc.shape[1] // 4 // LANES
    tc = min(tc, t)
    nch = tc // CHUNK
    nt = t // tc
    cum_all, masks, suffix = _hgrn_consts()

    def body(q_ref, f_ref, i_ref, g_ref, o_ref, st_ref, dy_ref, lb_ref, on_ref, cum_ref, m_ref, suf_ref,
             dq_ref, df_ref, di_ref, dg_ref, dlb_ref, don_ref, dstate):
        @pl.when(pl.program_id(1) == 0)
        def _():
            dstate[...] = jnp.zeros_like(dstate)
            dlb_ref[...] = jnp.zeros_like(dlb_ref)
            don_ref[...] = jnp.zeros_like(don_ref)

        lbv = _lower_bound(lb_ref)
        onv = on_ref[...]

        def chunk(n, carry):
            c = nch - 1 - n
            rows = pl.ds(pl.multiple_of(c * CHUNK, CHUNK), CHUNK)
            qr = q_ref[rows, :]
            sg, fval, g, kk, sq, qs = _hgrn_gates(qr, f_ref[rows, :], lbv)
            vb = i_ref[rows, :].astype(BF16)
            o = o_ref[rows, :]
            gate = g_ref[rows, :]
            sgt = _sigmoid(gate)
            rstd = lax.rsqrt(jnp.mean(o * o, axis=-1, keepdims=True) + RMS_EPS)
            ohat = o * rstd
            dyv = dy_ref[rows, :]
            don = dyv * (gate * sgt)
            dg_ref[rows, :] = (dyv * ohat * onv * (sgt * (1.0 + gate * (1.0 - sgt)))).astype(BF16)
            don_ref[...] += jnp.sum(don * ohat, axis=0, keepdims=True)
            dxhat = don * onv
            dob = (rstd * (dxhat - ohat * jnp.mean(dxhat * ohat, axis=-1, keepdims=True))).astype(BF16)
            ball = _dot_exact_lhs(cum_ref[...], g)
            b = ball[:CHUNK]
            blast = b[CHUNK - 1:CHUNK, :]
            eb = jnp.exp(b)
            edec = jnp.exp(blast - b)
            st32 = st_ref[c]
            st = st32.astype(BF16)
            dst = dstate[...]
            dstb = dst.astype(BF16)
            da = _dot(dob, vb, 1, 1)
            levels = _hgrn_levels(ball, qs, kk)
            scores = jnp.zeros((CHUNK, CHUNK), F32)
            dq = eb * _dot(dob, st)
            dk_inter = edec * _dot(vb, dstb)
            dk = dk_inter
            for lvl, (ql, kl, eq, ek) in enumerate(levels):
                mk = m_ref[lvl]
                (qh, qlo), (kh, klo) = _split2(ql), _split2(kl)
                scores = scores + _dot(qh, kh, 1, 1) * mk
                dal = (da * mk).astype(BF16)
                dql = _dot(dal, kh) + _dot(dal, klo)
                dkl = _dot(dal, qh, 0, 0) + _dot(dal, qlo, 0, 0)
                dq = dq + (dql if eq is None else dql * eq)
                dk = dk + (dkl if ek is None else dkl * ek)
            kdec = (kk * edec).astype(BF16)
            dv = _dot(scores.astype(BF16), dob, 0, 0) + _dot(kdec, dstb, 1, 1)
            dstate[...] = dst * jnp.exp(blast) + _dot(dob, (qs * eb).astype(BF16), 0, 0)
            db = qs * dq - kk * dk
            last = jnp.sum(kk * dk_inter, axis=0, keepdims=True) + jnp.exp(blast) * jnp.sum(dst * st32, axis=0, keepdims=True)
            dgl = _dot_exact_lhs(suf_ref[...], db) + last
            dfv = dgl / fval - dk
            df_ref[rows, :] = (dfv * (1.0 - lbv) * sg * (1.0 - sg)).astype(BF16)
            dlb_ref[...] += jnp.sum(dfv * (1.0 - sg), axis=0, keepdims=True)
            dq_ref[rows, :] = (dq * (sq * (1.0 + qr * (1.0 - sq)))).astype(BF16)
            di_ref[rows, :] = dv.astype(BF16)
            return carry

        lax.fori_loop(0, nch, chunk, 0)

    def col(off):
        return pl.BlockSpec((tc, LANES), lambda h, i: (nt - 1 - i, off + h))

    osp = pl.BlockSpec((tc, LANES), lambda h, i: (nt - 1 - i, h))
    vec = pl.BlockSpec((1, LANES), lambda h, i: (0, h))
    return pl.pallas_call(
        body, name=name, grid=(nh, nt),
        in_specs=[col(0), col(nh), col(2 * nh), col(3 * nh), osp,
                  pl.BlockSpec((None, nch, LANES, LANES), lambda h, i: (h, nt - 1 - i, 0, 0)),
                  osp,
                  pl.BlockSpec((2, LANES), lambda h, i: (0, h)),
                  pl.BlockSpec((1, LANES), lambda h, i: (0, 0)),
                  pl.BlockSpec(cum_all.shape, lambda h, i: (0, 0)),
                  pl.BlockSpec(masks.shape, lambda h, i: (0, 0, 0)),
                  pl.BlockSpec(suffix.shape, lambda h, i: (0, 0))],
        out_specs=[osp, osp, osp, osp, vec, vec],
        out_shape=[jax.ShapeDtypeStruct((t, nh * LANES), BF16)] * 4 + [jax.ShapeDtypeStruct((1, nh * LANES), F32)] * 2,
        scratch_shapes=[pltpu.VMEM((LANES, LANES), F32)],
        compiler_params=_cp())(pc, pc, pc, pc, o_saved, states, dy, c_lb, out_norm, cum_all, masks, suffix)


HBM_SPEC = pl.BlockSpec(memory_space=pltpu.HBM)


def _all_gather(xs, *, name):
    def body(x_ref, out_ref, send_sems, recv_sems, local_sem):
        x, y, c = lax.axis_index("x"), lax.axis_index("y"), lax.axis_index("c")
        me, sibling = (x, y, c), (x, y, 1 - c)
        chips = [(1 - x, y), (x, 1 - y), (1 - x, 1 - y)]

        def rows(px, py, pc):
            return out_ref.at[4 * px + 2 * py + pc]

        def copy(k, block, to, src=None):
            return pltpu.make_async_remote_copy(
                src_ref=rows(*block) if src is None else src, dst_ref=rows(*block),
                send_sem=send_sems.at[k], recv_sem=recv_sems.at[k], device_id=to, device_id_type=MESH)

        mine = pltpu.make_async_copy(x_ref, rows(*me), local_sem)
        mine.start()
        first = [copy(0, me, sibling, src=x_ref)]
        first += [copy(1 + j, me, (*chip, c), src=x_ref) for j, chip in enumerate(chips)]
        for cp in first:
            cp.start()
        passed = [copy(4 + j, (*chip, c), sibling) for j, chip in enumerate(chips)]
        for j, chip in enumerate(chips):
            copy(1 + j, (*chip, c), me).wait_recv()
            passed[j].start()
        copy(0, sibling, me).wait_recv()
        for j, chip in enumerate(chips):
            copy(4 + j, (*chip, 1 - c), me).wait_recv()
        for cp in first + passed:
            cp.wait_send()
        mine.wait()

    return pl.pallas_call(
        body, name=name, in_specs=[HBM_SPEC], out_specs=HBM_SPEC,
        out_shape=jax.ShapeDtypeStruct((8,) + xs.shape, xs.dtype),
        scratch_shapes=[pltpu.SemaphoreType.DMA((7,)), pltpu.SemaphoreType.DMA((7,)), pltpu.SemaphoreType.DMA])(xs)


def _sibling_exchange(s, *, name):
    def body(s_ref, rb_ref, send_sem, recv_sem):
        x, y, c = lax.axis_index("x"), lax.axis_index("y"), lax.axis_index("c")
        cp = pltpu.make_async_remote_copy(
            src_ref=s_ref.at[1 - c], dst_ref=rb_ref, send_sem=send_sem, recv_sem=recv_sem,
            device_id=(x, y, 1 - c), device_id_type=MESH)
        cp.start()
        cp.wait()

    return pl.pallas_call(
        body, name=name, in_specs=[HBM_SPEC], out_specs=HBM_SPEC,
        out_shape=jax.ShapeDtypeStruct(s.shape[1:], s.dtype),
        scratch_shapes=[pltpu.SemaphoreType.DMA, pltpu.SemaphoreType.DMA])(s)


def _pair_add(s, rb, core, *, name, tb=512):
    _, n, c = s.shape
    assert n % tb == 0

    def body(core_ref, a_ref, b_ref, o_ref):
        o_ref[...] = (a_ref[...].astype(F32) + b_ref[...].astype(F32)).astype(BF16)

    return pl.pallas_call(
        body, name=name,
        grid_spec=pltpu.PrefetchScalarGridSpec(
            num_scalar_prefetch=1, grid=(n // tb,),
            in_specs=[pl.BlockSpec((None, tb, c), lambda i, cr: (cr[0], i, 0)),
                      pl.BlockSpec((tb, c), lambda i, cr: (i, 0))],
            out_specs=pl.BlockSpec((tb, c), lambda i, cr: (i, 0))),
        out_shape=jax.ShapeDtypeStruct((n, c), BF16), compiler_params=_cp())(core, s, rb)


def _chip_exchange(p, *, name):
    def body(p_ref, out_ref, send_sems, recv_sems, local_sem):
        x, y, c = lax.axis_index("x"), lax.axis_index("y"), lax.axis_index("c")
        mine = 2 * x + y
        own = pltpu.make_async_copy(p_ref.at[mine], out_ref.at[mine], local_sem)
        own.start()
        copies = []
        for k, (tx, ty) in enumerate([(1 - x, y), (x, 1 - y), (1 - x, 1 - y)]):
            copies.append(pltpu.make_async_remote_copy(
                src_ref=p_ref.at[2 * tx + ty], dst_ref=out_ref.at[mine],
                send_sem=send_sems.at[k], recv_sem=recv_sems.at[k], device_id=(tx, ty, c), device_id_type=MESH))
        for cp in copies:
            cp.start()
        for cp in copies:
            cp.wait()
        own.wait()

    return pl.pallas_call(
        body, name=name, in_specs=[HBM_SPEC], out_specs=HBM_SPEC,
        out_shape=jax.ShapeDtypeStruct(p.shape, p.dtype),
        scratch_shapes=[pltpu.SemaphoreType.DMA((3,)), pltpu.SemaphoreType.DMA((3,)), pltpu.SemaphoreType.DMA])(p)


def _adamw_math(w, g, m, v):
    m2 = ADAM_B1 * m + (1.0 - ADAM_B1) * g
    v2 = ADAM_B2 * v + (1.0 - ADAM_B2) * (g * g)
    m_hat = m2 / (1.0 - ADAM_B1 ** ADAM_STEP)
    v_hat = v2 / (1.0 - ADAM_B2 ** ADAM_STEP)
    return -ADAM_LR * (m_hat / (jnp.sqrt(v_hat) + ADAM_EPS) + ADAM_WD * w), m2, v2


def _adamw_shard(parts, off, w, m, v, *, name):
    n, c = w.shape
    tb = next(b for b in (512, 384, 352, 256, 128, 64, 16) if n % b == 0 and off % b == 0)

    def body(p0, p1, p2, p3, w_ref, m_ref, v_ref, g_out, d_out, m_out, v_out):
        g = ((p0[...].astype(F32) + p1[...].astype(F32)) + p2[...].astype(F32)) + p3[...].astype(F32)
        d, m2, v2 = _adamw_math(w_ref[...], g, m_ref[...], v_ref[...])
        g_out[...] = g
        d_out[...] = d
        m_out[...] = m2
        v_out[...] = v2

    def part(ch):
        return pl.BlockSpec((None, tb, c), lambda i: (ch, off // tb + i, 0))

    row = pl.BlockSpec((tb, c), lambda i: (i, 0))
    return pl.pallas_call(
        body, name=name, grid=(n // tb,), in_specs=[part(0), part(1), part(2), part(3), row, row, row],
        out_specs=[row] * 4, out_shape=[jax.ShapeDtypeStruct((n, c), F32)] * 4,
        compiler_params=_cp())(parts, parts, parts, parts, w, m, v)


SMALL_ROWS = 16
ROW_LB = 7


def _small_update(gath, w, m, v, *, name):
    def body(g_ref, w_ref, m_ref, v_ref, g_out, d_out, m_out, v_out):
        tot = g_ref[0]
        for k in range(1, 8):
            tot = tot + g_ref[k]
        wv = w_ref[...]
        c0, c1 = wv[ROW_LB:ROW_LB + 1, :], wv[ROW_LB + 1:ROW_LB + 2, :]
        mx = jnp.maximum(c0, c1)
        e0, e1 = jnp.exp(c0 - mx), jnp.exp(c1 - mx)
        lb = e1 / (e0 + e1)
        gl = tot[ROW_LB:ROW_LB + 1, :] * lb * (1.0 - lb)
        row = lax.broadcasted_iota(jnp.int32, tot.shape, 0)
        g = jnp.where(row == ROW_LB, -gl, jnp.where(row == ROW_LB + 1, gl, tot))
        d, m2, v2 = _adamw_math(wv, g, m_ref[...], v_ref[...])
        g_out[...] = g
        d_out[...] = d
        m_out[...] = m2
        v_out[...] = v2

    return pl.pallas_call(
        body, name=name, out_shape=[jax.ShapeDtypeStruct(w.shape, F32)] * 4, compiler_params=_cp())(gath, w, m, v)


D_MODEL = 1024
PACK_ROWS = 5632
CONV_ROW = 5376


def _ffn_fwd(h, gain, wg, wu, wd, tag):
    xn, gg, uu, act = _norm_gate_up(h, gain, wg, wu, name=f"{tag}_gate_up")
    out = _mm([(act, wd)], residual=h, alpha=MACARON, tn=1024, name=f"{tag}_down")
    return out, (h, xn, gg, uu, act)


def _ffn_bwd(dout, saved, gain, wg, wu, wd, tag):
    h, xn, gg, uu, act = saved
    dg, du = _swiglu_bwd(dout, wd, gg, uu, name=f"{tag}_dact")
    dwd = _mm([(act, dout)], ta=True, alpha=MACARON, tm=256, tn=512, name=f"{tag}_dwd")
    dwg = _mm([(xn, dg)], ta=True, tm=512, tn=256, name=f"{tag}_dwg")
    dwu = _mm([(xn, du)], ta=True, tm=512, tn=256, name=f"{tag}_dwu")
    dxn = _mm([(dg, wg), (du, wu)], tb=True, tm=256, tn=512, name=f"{tag}_dxn")
    dh, dgain = _rmsnorm_bwd(h, gain, dxn, dout, name=f"{tag}_norm_bwd")
    return dh, dwg, dwu, dwd, dgain


def kernel(x, ffn_pre_norm, ffn_pre_w_gate, ffn_pre_w_up, ffn_pre_w_down, mix_norm, ffn_post_norm, ffn_post_w_gate, ffn_post_w_up, ffn_post_w_down, ab_w_in, ab_conv_w, ab_w_out, c_w_in, c_lower_bounds, c_out_norm, c_w_out, final_norm, loss_target, m_ffn_pre_norm, m_ffn_pre_w_gate, m_ffn_pre_w_up, m_ffn_pre_w_down, m_mix_norm, m_ffn_post_norm, m_ffn_post_w_gate, m_ffn_post_w_up, m_ffn_post_w_down, m_ab_w_in, m_ab_conv_w, m_ab_w_out, m_c_w_in, m_c_lower_bounds, m_c_out_norm, m_c_w_out, m_final_norm, v_ffn_pre_norm, v_ffn_pre_w_gate, v_ffn_pre_w_up, v_ffn_pre_w_down, v_mix_norm, v_ffn_post_norm, v_ffn_post_w_gate, v_ffn_post_w_up, v_ffn_post_w_down, v_ab_w_in, v_ab_conv_w, v_ab_w_out, v_c_w_in, v_c_lower_bounds, v_c_out_norm, v_c_w_out, v_final_norm):
    d = D_MODEL
    h0 = x[0]
    target = loss_target[0]
    core = lax.axis_index("c").astype(jnp.int32).reshape(1)

    big = [("pre_g", ffn_pre_w_gate, m_ffn_pre_w_gate, v_ffn_pre_w_gate),
           ("pre_u", ffn_pre_w_up, m_ffn_pre_w_up, v_ffn_pre_w_up),
           ("pre_d", ffn_pre_w_down, m_ffn_pre_w_down, v_ffn_pre_w_down),
           ("post_g", ffn_post_w_gate, m_ffn_post_w_gate, v_ffn_post_w_gate),
           ("post_u", ffn_post_w_up, m_ffn_post_w_up, v_ffn_post_w_up),
           ("post_d", ffn_post_w_down, m_ffn_post_w_down, v_ffn_post_w_down),
           ("ab_in", ab_w_in, m_ab_w_in, v_ab_w_in),
           ("ab_out", ab_w_out, m_ab_w_out, v_ab_w_out),
           ("c_in", c_w_in, m_c_w_in, v_c_w_in),
           ("c_out", c_w_out, m_c_w_out, v_c_w_out)]
    offs, off = {}, 0
    for tag, w, _, _ in big:
        offs[tag] = off
        off += w.size // d
    assert off == CONV_ROW

    def conv_rows(a, split):
        flat = a.reshape(-1)
        if split:
            hi = flat.astype(BF16)
            flat = jnp.concatenate([hi, (flat - hi.astype(F32)).astype(BF16)])
        return jnp.zeros((16, d), flat.dtype).at[0, :flat.shape[0]].set(flat)

    nconv = ab_conv_w.size
    wpack = jnp.concatenate(
        [w.reshape(-1, d).astype(BF16) for _, w, _, _ in big]
        + [conv_rows(ab_conv_w, True), jnp.zeros((PACK_ROWS - CONV_ROW - 16, d), BF16)], axis=0)
    gath = _all_gather(wpack, name="gather_weights")

    def col_w(tag, layer, ncol):
        o = offs[tag] + layer * ncol
        return gath[:, o:o + ncol, :].reshape(8, d, ncol).transpose(1, 0, 2).reshape(d, 8 * ncol)

    def row_w(tag, layer, nrow):
        o = offs[tag] + layer * nrow
        return gath[:, o:o + nrow, :].reshape(8 * nrow, d)

    f_loc = ffn_pre_w_gate.shape[2]
    ffn_w = {}
    for pos in ("pre", "post"):
        for layer in range(2):
            ffn_w[pos, layer] = (col_w(f"{pos}_g", layer, f_loc), col_w(f"{pos}_u", layer, f_loc),
                                 row_w(f"{pos}_d", layer, f_loc))
    w_ab_in = col_w("ab_in", 0, ab_w_in.shape[2])
    w_ab_out = row_w("ab_out", 0, ab_w_out.shape[1])
    w_c_in = col_w("c_in", 0, c_w_in.shape[2])
    w_c_out = row_w("c_out", 0, c_w_out.shape[1])
    cg = gath[:, CONV_ROW, :2 * nconv].astype(F32)
    conv_w = (cg[:, :nconv] + cg[:, nconv:]).reshape(8, 3, -1).transpose(1, 0, 2).reshape(3, -1)
    half = w_ab_in.shape[1] // 2
    w_a_in, w_b_in = w_ab_in[:, :half], w_ab_in[:, half:]
    aw = half // 3

    h1, s_pre0 = _ffn_fwd(h0, ffn_pre_norm[0:1], *ffn_w["pre", 0], "l0pre")
    hn0 = _rmsnorm_fwd(h1, mix_norm[0:1], name="l0_mix_norm")
    pa = _mm([(hn0, w_a_in)], tn=512, name="ab_proj_a")
    pb = _mm([(hn0, w_b_in)], tn=512, out_dtype=BF16, name="ab_proj_b")
    ya = _conv_fwd(pa, conv_w, name="conv_fwd")
    yb, ltot = _attn_fwd(pb, name="attn_fwd")
    h2 = _mm([(ya, w_ab_out[:aw]), (yb, w_ab_out[aw:])], residual=h1, tn=1024, name="ab_out")
    h3, s_post0 = _ffn_fwd(h2, ffn_post_norm[0:1], *ffn_w["post", 0], "l0post")
    h4, s_pre1 = _ffn_fwd(h3, ffn_pre_norm[1:2], *ffn_w["pre", 1], "l1pre")
    hn1 = _rmsnorm_fwd(h4, mix_norm[1:2], name="l1_mix_norm")
    pc = _mm([(hn1, w_c_in)], tn=512, name="c_proj")
    yc, o_saved, states = _hgrn_fwd(pc, c_lower_bounds, c_out_norm, name="hgrn_fwd")
    h5 = _mm([(yc, w_c_out)], residual=h4, tn=1024, name="c_out")
    h6, s_post1 = _ffn_fwd(h5, ffn_post_norm[1:2], *ffn_w["post", 1], "l1post")
    dh6, d_final, loss_vec = _loss_head(h6, final_norm.reshape(1, d), target, name="loss_head")

    gw = {}
    dh5, gw["post_g", 1], gw["post_u", 1], gw["post_d", 1], d_post1 = _ffn_bwd(
        dh6, s_post1, ffn_post_norm[1:2], *ffn_w["post", 1], "l1post")
    dyc = _mm([(dh5, w_c_out)], tb=True, tn=512, name="c_out_dy")
    g_c_out = _mm([(yc, dh5)], ta=True, name="c_out_dw")
    dcq, dcf, dci, dcg, dlb, d_onorm = _hgrn_bwd(pc, o_saved, states, dyc, c_lower_bounds, c_out_norm, name="hgrn_bwd")
    dparts = [dcq, dcf, dci, dcg]
    g_c_in = jnp.concatenate([_mm([(hn1, dp)], ta=True, name=f"c_in_dw{i}") for i, dp in enumerate(dparts)], axis=1)
    cw = w_c_in.shape[1] // 4
    dhn1 = _mm([(dp, w_c_in[:, i * cw:(i + 1) * cw]) for i, dp in enumerate(dparts)], tb=True, tm=256, name="c_in_dx")
    dh4, d_mix1 = _rmsnorm_bwd(h4, mix_norm[1:2], dhn1, dh5, name="l1_mix_norm_bwd")
    dh3, gw["pre_g", 1], gw["pre_u", 1], gw["pre_d", 1], d_pre1 = _ffn_bwd(
        dh4, s_pre1, ffn_pre_norm[1:2], *ffn_w["pre", 1], "l1pre")
    dh2, gw["post_g", 0], gw["post_u", 0], gw["post_d", 0], d_post0 = _ffn_bwd(
        dh3, s_post0, ffn_post_norm[0:1], *ffn_w["post", 0], "l0post")
    dyab = _mm([(dh2, w_ab_out)], tb=True, tn=512, name="ab_out_dy")
    g_ab_out = jnp.concatenate([_mm([(ya, dh2)], ta=True, name="ab_out_dw_a"),
                                _mm([(yb, dh2)], ta=True, name="ab_out_dw_b")], axis=0)
    dab, dac, dax, g_conv = _conv_bwd(pa, dyab, conv_w, name="conv_bwd")
    dq, dk, dv = _attn_bwd(pb, dyab, ltot, name="attn_bwd")
    dparts = [dab, dac, dax, dq, dk, dv]
    g_ab_in = jnp.concatenate([_mm([(hn0, dp)], ta=True, name=f"ab_in_dw{i}") for i, dp in enumerate(dparts)], axis=1)
    dhn0 = _mm([(dp, w_ab_in[:, i * aw:(i + 1) * aw]) for i, dp in enumerate(dparts)], tb=True, tm=256, name="ab_in_dx")
    dh1, d_mix0 = _rmsnorm_bwd(h1, mix_norm[0:1], dhn0, dh2, name="l0_mix_norm_bwd")
    dh0, gw["pre_g", 0], gw["pre_u", 0], gw["pre_d", 0], d_pre0 = _ffn_bwd(
        dh1, s_pre0, ffn_pre_norm[0:1], *ffn_w["pre", 0], "l0pre")

    def col_g(g):
        ncol = g.shape[1] // 8
        return g.reshape(d, 8, ncol).transpose(1, 0, 2).reshape(8, ncol, d)

    def row_g(g):
        return g.reshape(8, g.shape[0] // 8, d)

    pieces = []
    for pos in ("pre", "post"):
        for kind, fn in (("g", col_g), ("u", col_g), ("d", row_g)):
            pieces += [fn(gw[f"{pos}_{kind}", 0]), fn(gw[f"{pos}_{kind}", 1])]
    pieces += [col_g(g_ab_in), row_g(g_ab_out), col_g(g_c_in), row_g(g_c_out)]
    gconv_own = g_conv.reshape(3, 8, -1).transpose(1, 0, 2).reshape(8, -1)
    pieces.append(jnp.zeros((8, 16, d), F32).at[:, 0, :nconv].set(gconv_own))
    pieces.append(jnp.zeros((8, PACK_ROWS - CONV_ROW - 16, d), F32))
    gpack = jnp.concatenate([p.astype(BF16) for p in pieces], axis=1)
    send = gpack.reshape(4, 2, PACK_ROWS, d).transpose(1, 0, 2, 3)
    from_sibling = _sibling_exchange(send, name="grad_sibling_exchange")
    chip_part = _pair_add(send.reshape(2, 4 * PACK_ROWS, d), from_sibling.reshape(4 * PACK_ROWS, d), core,
                          name="grad_pair_add").reshape(4, PACK_ROWS, d)
    parts = _chip_exchange(chip_part, name="grad_chip_exchange")

    upd = {}
    for tag, w, m, v in big:
        res = _adamw_shard(parts, offs[tag], w.reshape(-1, d), m.reshape(-1, d), v.reshape(-1, d), name=f"adamw_{tag}")
        upd[tag] = [r.reshape(w.shape) for r in res]
    res = _adamw_shard(parts, CONV_ROW, conv_rows(ab_conv_w, False), conv_rows(m_ab_conv_w, False),
                       conv_rows(v_ab_conv_w, False), name="adamw_conv")
    upd["conv"] = [r[0, :nconv].reshape(ab_conv_w.shape) for r in res]

    def small_pack(pre, mix, post, final, lbs, onorm):
        on = jnp.zeros((1, d), F32).at[:, :onorm.shape[1]].set(onorm)
        rows = jnp.concatenate([pre, mix, post, final.reshape(1, d), lbs, on], axis=0)
        return jnp.concatenate([rows, jnp.zeros((SMALL_ROWS - rows.shape[0], d), F32)], axis=0)

    dlb2 = jnp.concatenate([dlb, jnp.zeros_like(dlb)], axis=0)
    d_on = d_onorm.reshape(-1, c_out_norm.shape[1]).sum(axis=0, keepdims=True)
    gsmall = small_pack(jnp.concatenate([d_pre0, d_pre1], 0), jnp.concatenate([d_mix0, d_mix1], 0),
                        jnp.concatenate([d_post0, d_post1], 0), d_final, dlb2, d_on)
    gsmall_all = _all_gather(gsmall, name="gather_small_grads")
    sres = _small_update(
        gsmall_all,
        small_pack(ffn_pre_norm, mix_norm, ffn_post_norm, final_norm, c_lower_bounds, c_out_norm),
        small_pack(m_ffn_pre_norm, m_mix_norm, m_ffn_post_norm, m_final_norm, m_c_lower_bounds, m_c_out_norm),
        small_pack(v_ffn_pre_norm, v_mix_norm, v_ffn_post_norm, v_final_norm, v_c_lower_bounds, v_c_out_norm),
        name="small_update")

    def small_out(r):
        return {"pre_norm": r[0:2], "mix_norm": r[2:4], "post_norm": r[4:6], "final": r[6],
                "lb": r[ROW_LB:ROW_LB + 2], "onorm": r[9:10, :c_out_norm.shape[1]]}

    small = [small_out(r) for r in sres]
    outs = []
    for k in range(4):
        s = small[k]
        outs += [s["pre_norm"], upd["pre_g"][k], upd["pre_u"][k], upd["pre_d"][k], s["mix_norm"], s["post_norm"],
                 upd["post_g"][k], upd["post_u"][k], upd["post_d"][k], upd["ab_in"][k], upd["conv"][k],
                 upd["ab_out"][k], upd["c_in"][k], s["lb"], s["onorm"], upd["c_out"][k], s["final"]]
    loss = lax.psum(loss_vec[0, 0], ("x", "y", "c"))
    return (loss, dh0[None], *outs)
```

```python
import functools
import math

import jax
import jax.numpy as jnp
from jax import lax
from jax.experimental import pallas as pl
from jax.experimental.pallas import tpu as pltpu

F32 = jnp.float32
BF16 = jnp.bfloat16
MESH = pl.DeviceIdType.MESH

RMS_EPS = 1e-6
MACARON = 0.5
LANES = 128
CHUNK = 64
N_LEVELS = 6
SB_KEYS = 256
ADAM_LR, ADAM_B1, ADAM_B2, ADAM_EPS, ADAM_WD, ADAM_STEP = 0.001, 0.9, 0.999, 1e-08, 0.01, 10
VMEM_LIMIT = 48 * 1024 * 1024


def _cp(**kw):
    return pltpu.CompilerParams(vmem_limit_bytes=VMEM_LIMIT, **kw)


def _sigmoid(x):
    return 1.0 / (1.0 + jnp.exp(-x))


def _bf(x):
    return x if x.dtype == BF16 else x.astype(BF16)


def _split3(x):
    hi = x.astype(BF16)
    r1 = x - hi.astype(F32)
    mid = r1.astype(BF16)
    lo = (r1 - mid.astype(F32)).astype(BF16)
    return hi, mid, lo


def _dot(a, b, ca=1, cb=0):
    return lax.dot_general(a, b, (((ca,), (cb,)), ((), ())), preferred_element_type=F32)


def _dot_exact_lhs(m, x):
    hi, mid, lo = _split3(x)
    return _dot(m, hi) + _dot(m, mid) + _dot(m, lo)


def _dot_exact_rhs(x, m):
    hi, mid, lo = _split3(x)
    return _dot(hi, m) + _dot(mid, m) + _dot(lo, m)


def _mm(terms, *, name, ta=False, tb=False, out_dtype=F32, residual=None, alpha=1.0, tm=512, tn=512):
    nt = len(terms)
    a0, b0 = terms[0]
    m = a0.shape[1] if ta else a0.shape[0]
    n = b0.shape[0] if tb else b0.shape[1]
    tm, tn = min(tm, m), min(tn, n)
    assert m % tm == 0 and n % tn == 0, (name, m, n, tm, tn)
    has_res = residual is not None

    def body(*refs):
        o_ref = refs[-1]
        acc = None
        for i in range(nt):
            a = _bf(refs[2 * i][...])
            b = _bf(refs[2 * i + 1][...])
            p = _dot(a, b, 0 if ta else 1, 1 if tb else 0)
            acc = p if acc is None else acc + p
        if alpha != 1.0:
            acc = acc * alpha
        if has_res:
            acc = acc + refs[2 * nt][...]
        o_ref[...] = acc.astype(out_dtype)

    in_specs, args = [], []
    for a, b in terms:
        k = a.shape[0] if ta else a.shape[1]
        assert (b.shape[1] if tb else b.shape[0]) == k, (name, a.shape, b.shape)
        in_specs.append(pl.BlockSpec((k, tm), lambda i, j: (0, i)) if ta else pl.BlockSpec((tm, k), lambda i, j: (i, 0)))
        in_specs.append(pl.BlockSpec((tn, k), lambda i, j: (j, 0)) if tb else pl.BlockSpec((k, tn), lambda i, j: (0, j)))
        args += [a, b]
    if has_res:
        in_specs.append(pl.BlockSpec((tm, tn), lambda i, j: (i, j)))
        args.append(residual)
    return pl.pallas_call(
        body, name=name, grid=(m // tm, n // tn), in_specs=in_specs,
        out_specs=pl.BlockSpec((tm, tn), lambda i, j: (i, j)),
        out_shape=jax.ShapeDtypeStruct((m, n), out_dtype), compiler_params=_cp())(*args)


def _rmsnorm_fwd(x, gain, *, name, tm=512):
    t, d = x.shape
    tm = min(tm, t)

    def body(x_ref, g_ref, o_ref):
        xv = x_ref[...]
        rstd = lax.rsqrt(jnp.mean(xv * xv, axis=-1, keepdims=True) + RMS_EPS)
        o_ref[...] = (xv * rstd * g_ref[...]).astype(BF16)

    return pl.pallas_call(
        body, name=name, grid=(t // tm,),
        in_specs=[pl.BlockSpec((tm, d), lambda i: (i, 0)), pl.BlockSpec((1, d), lambda i: (0, 0))],
        out_specs=pl.BlockSpec((tm, d), lambda i: (i, 0)),
        out_shape=jax.ShapeDtypeStruct((t, d), BF16), compiler_params=_cp())(x, gain)


def _rmsnorm_bwd(x, gain, dxn, dres, *, name, tm=512):
    t, d = x.shape
    tm = min(tm, t)

    def body(x_ref, g_ref, dxn_ref, dres_ref, dx_ref, dg_ref):
        xv = x_ref[...]
        rstd = lax.rsqrt(jnp.mean(xv * xv, axis=-1, keepdims=True) + RMS_EPS)
        xhat = xv * rstd
        dxn_v = dxn_ref[...]
        dxhat = dxn_v * g_ref[...]
        dx = rstd * (dxhat - xhat * jnp.mean(dxhat * xhat, axis=-1, keepdims=True))
        dx_ref[...] = dres_ref[...] + dx

        @pl.when(pl.program_id(0) == 0)
        def _():
            dg_ref[...] = jnp.zeros_like(dg_ref)

        dg_ref[...] += jnp.sum(dxn_v * xhat, axis=0, keepdims=True)

    row = pl.BlockSpec((tm, d), lambda i: (i, 0))
    vec = pl.BlockSpec((1, d), lambda i: (0, 0))
    return pl.pallas_call(
        body, name=name, grid=(t // tm,), in_specs=[row, vec, row, row], out_specs=[row, vec],
        out_shape=[jax.ShapeDtypeStruct((t, d), F32), jax.ShapeDtypeStruct((1, d), F32)],
        compiler_params=_cp())(x, gain, dxn, dres)


def _loss_head(h, gain, target, *, name, tm=512):
    t, d = h.shape
    tm = min(tm, t)

    def body(h_ref, g_ref, t_ref, dh_ref, dg_ref, loss_ref):
        hv = h_ref[...]
        rstd = lax.rsqrt(jnp.mean(hv * hv, axis=-1, keepdims=True) + RMS_EPS)
        xhat = hv * rstd
        err = xhat * g_ref[...] - t_ref[...]
        dy = err * (1.0 / d)
        dxhat = dy * g_ref[...]
        dh_ref[...] = rstd * (dxhat - xhat * jnp.mean(dxhat * xhat, axis=-1, keepdims=True))

        @pl.when(pl.program_id(0) == 0)
        def _():
            dg_ref[...] = jnp.zeros_like(dg_ref)
            loss_ref[...] = jnp.zeros_like(loss_ref)

        dg_ref[...] += jnp.sum(dy * xhat, axis=0, keepdims=True)
        part = jnp.sum(jnp.sum(err * err, axis=-1, keepdims=True), axis=0, keepdims=True) * (0.5 / d)
        loss_ref[...] += jnp.broadcast_to(part, loss_ref.shape)

    row = pl.BlockSpec((tm, d), lambda i: (i, 0))
    vec = pl.BlockSpec((1, d), lambda i: (0, 0))
    return pl.pallas_call(
        body, name=name, grid=(t // tm,), in_specs=[row, vec, row],
        out_specs=[row, vec, pl.BlockSpec((1, LANES), lambda i: (0, 0))],
        out_shape=[jax.ShapeDtypeStruct((t, d), F32), jax.ShapeDtypeStruct((1, d), F32),
                   jax.ShapeDtypeStruct((1, LANES), F32)],
        compiler_params=_cp())(h, gain, target)


def _norm_gate_up(x, gain, wg, wu, *, name, tm=512, tf=1408):
    t, d = x.shape
    f = wg.shape[0]
    tm, tf = min(tm, t), min(tf, f)
    assert f % tf == 0

    def body(x_ref, g_ref, wg_ref, wu_ref, xn_ref, gg_ref, uu_ref, act_ref):
        @pl.when(pl.program_id(1) == 0)
        def _():
            xv = x_ref[...]
            rstd = lax.rsqrt(jnp.mean(xv * xv, axis=-1, keepdims=True) + RMS_EPS)
            xn_ref[...] = (xv * rstd * g_ref[...]).astype(BF16)

        xn = xn_ref[...]
        gv = _dot(xn, wg_ref[...], 1, 1)
        uv = _dot(xn, wu_ref[...], 1, 1)
        gg_ref[...] = gv.astype(BF16)
        uu_ref[...] = uv.astype(BF16)
        act_ref[...] = (gv * _sigmoid(gv) * uv).astype(BF16)

    row = pl.BlockSpec((tm, d), lambda i, j: (i, 0))
    wsp = pl.BlockSpec((tf, d), lambda i, j: (j, 0))
    osp = pl.BlockSpec((tm, tf), lambda i, j: (i, j))
    return pl.pallas_call(
        body, name=name, grid=(t // tm, f // tf),
        in_specs=[row, pl.BlockSpec((1, d), lambda i, j: (0, 0)), wsp, wsp],
        out_specs=[row, osp, osp, osp],
        out_shape=[jax.ShapeDtypeStruct((t, d), BF16)] + [jax.ShapeDtypeStruct((t, f), BF16)] * 3,
        compiler_params=_cp())(x, gain, wg, wu)


def _swiglu_bwd(dout, wd, gg, uu, *, name, tm=512, tf=1408):
    t, d = dout.shape
    f = wd.shape[0]
    tm, tf = min(tm, t), min(tf, f)

    def body(do_ref, wd_ref, g_ref, u_ref, dg_ref, du_ref):
        dact = _dot((do_ref[...] * MACARON).astype(BF16), wd_ref[...], 1, 1)
        gv = g_ref[...].astype(F32)
        uv = u_ref[...].astype(F32)
        sg = _sigmoid(gv)
        dg_ref[...] = (dact * uv * (sg * (1.0 + gv * (1.0 - sg)))).astype(BF16)
        du_ref[...] = (dact * (gv * sg)).astype(BF16)

    osp = pl.BlockSpec((tm, tf), lambda i, j: (i, j))
    return pl.pallas_call(
        body, name=name, grid=(t // tm, f // tf),
        in_specs=[pl.BlockSpec((tm, d), lambda i, j: (i, 0)), pl.BlockSpec((tf, d), lambda i, j: (j, 0)), osp, osp],
        out_specs=[osp, osp], out_shape=[jax.ShapeDtypeStruct((t, f), BF16)] * 2,
        compiler_params=_cp())(dout, wd, gg, uu)


def _shift_down(x, n):
    rows = lax.broadcasted_iota(jnp.int32, x.shape, 0)
    return jnp.where(rows >= n, pltpu.roll(x, n, 0), 0.0)


def _shift_up(x, n):
    t = x.shape[0]
    rows = lax.broadcasted_iota(jnp.int32, x.shape, 0)
    return jnp.where(rows < t - n, pltpu.roll(x, t - n, 0), 0.0)


def _conv_fwd(pa, conv_w, *, name):
    t = pa.shape[0]
    nb = pa.shape[1] // 3 // LANES

    def body(b_ref, c_ref, x_ref, w_ref, y_ref):
        u = c_ref[...] * x_ref[...]
        w = w_ref[...]
        conv = w[2:3, :] * u + w[1:2, :] * _shift_down(u, 1) + w[0:1, :] * _shift_down(u, 2)
        y_ref[...] = (b_ref[...] * conv).astype(BF16)

    def col(off):
        return pl.BlockSpec((t, LANES), lambda j: (0, off + j))

    return pl.pallas_call(
        body, name=name, grid=(nb,),
        in_specs=[col(0), col(nb), col(2 * nb), pl.BlockSpec((3, LANES), lambda j: (0, j))],
        out_specs=pl.BlockSpec((t, LANES), lambda j: (0, j)),
        out_shape=jax.ShapeDtypeStruct((t, nb * LANES), BF16), compiler_params=_cp())(pa, pa, pa, conv_w)


def _conv_bwd(pa, dy, conv_w, *, name):
    t = pa.shape[0]
    nb = pa.shape[1] // 3 // LANES

    def body(b_ref, c_ref, x_ref, dy_ref, w_ref, db_ref, dc_ref, dx_ref, dw_ref):
        cv, xv = c_ref[...], x_ref[...]
        u = cv * xv
        u1, u2 = _shift_down(u, 1), _shift_down(u, 2)
        w = w_ref[...]
        conv = w[2:3, :] * u + w[1:2, :] * u1 + w[0:1, :] * u2
        dyv = dy_ref[...]
        db_ref[...] = (dyv * conv).astype(BF16)
        dconv = dyv * b_ref[...]
        du = w[2:3, :] * dconv + w[1:2, :] * _shift_up(dconv, 1) + w[0:1, :] * _shift_up(dconv, 2)
        dc_ref[...] = (du * xv).astype(BF16)
        dx_ref[...] = (du * cv).astype(BF16)
        dw_ref[0:1, :] = jnp.sum(dconv * u2, axis=0, keepdims=True)
        dw_ref[1:2, :] = jnp.sum(dconv * u1, axis=0, keepdims=True)
        dw_ref[2:3, :] = jnp.sum(dconv * u, axis=0, keepdims=True)

    def col(off):
        return pl.BlockSpec((t, LANES), lambda j: (0, off + j))

    osp = pl.BlockSpec((t, LANES), lambda j: (0, j))
    wsp = pl.BlockSpec((3, LANES), lambda j: (0, j))
    return pl.pallas_call(
        body, name=name, grid=(nb,), in_specs=[col(0), col(nb), col(2 * nb), col(0), wsp],
        out_specs=[osp, osp, osp, wsp],
        out_shape=[jax.ShapeDtypeStruct((t, nb * LANES), BF16)] * 3 + [jax.ShapeDtypeStruct((3, nb * LANES), F32)],
        compiler_params=_cp())(pa, pa, pa, dy, conv_w)


def _sb_consts():
    j = lax.broadcasted_iota(jnp.int32, (SB_KEYS, SB_KEYS), 0)
    s = lax.broadcasted_iota(jnp.int32, (SB_KEYS, SB_KEYS), 1)
    after = (j > s).astype(BF16)
    upto = (j <= s).astype(BF16)
    before = (j < s).astype(BF16)
    return after, jnp.stack([upto, before])


def _log_sigmoid(z):
    return jnp.minimum(z, 0.0) - jnp.log(1.0 + jnp.exp(-jnp.abs(z)))


def _attn_fwd(pb, *, name, tq=256):
    t = pb.shape[0]
    npair = pb.shape[1] // 3 // LANES
    tq = min(tq, t)
    cmat, _ = _sb_consts()
    scale = 1.0 / math.sqrt(LANES // 2)

    def body(q_ref, k_ref, v_ref, c_ref, y_ref, lt_ref):
        i = pl.program_id(1)
        lane = lax.broadcasted_iota(jnp.int32, (tq, LANES), 1)
        rowpos = i * tq + lax.broadcasted_iota(jnp.int32, (tq, SB_KEYS), 0)
        colid = lax.broadcasted_iota(jnp.int32, (tq, SB_KEYS), 1)
        q2 = q_ref[...] * jnp.asarray(scale, BF16)
        cm = c_ref[...]
        hi_lanes = lane >= LANES // 2
        qhs = [jnp.where(hi_lanes == (hh == 1), q2, jnp.zeros_like(q2)) for hh in range(2)]
        per_q = tq // SB_KEYS

        def blk(jb):
            return pl.ds(pl.multiple_of(jb * SB_KEYS, SB_KEYS), SB_KEYS)

        def scores(jb):
            kb = k_ref[blk(jb), :]
            return tuple(_dot(qhs[hh], kb, 1, 1) for hh in range(2))

        def weights(jb, zs, runs, masked):
            mask = (jb * SB_KEYS + colid) < rowpos if masked else None
            ws, new_runs = [], []
            for hh in range(2):
                lb = _log_sigmoid(zs[hh])
                lk = lb - zs[hh]
                if masked:
                    lk = jnp.where(mask, lk, 0.0)
                lk_hi, lk_lo = _split2(lk)
                cs = _dot(lk_hi, cm) + _dot(lk_lo, cm)
                w = jnp.exp(lb + runs[hh] + cs)
                if masked:
                    w = jnp.where(mask, w, 0.0)
                ws.append(w.astype(BF16))
                new_runs.append(runs[hh] + jnp.sum(lk, axis=1, keepdims=True))
            return tuple(ws), tuple(new_runs)

        def values(jb, accs, ws):
            vb = v_ref[blk(jb), :]
            return tuple(accs[hh] + _dot(ws[hh], vb) for hh in range(2))

        zero = jnp.zeros((tq, LANES), F32)
        zero_col = jnp.zeros((tq, 1), F32)
        nfull = i * per_q
        runs, accs, ws = (zero_col, zero_col), (zero, zero), None
        zs = scores(nfull + per_q - 1)
        for dblk in reversed(range(per_q)):
            jb = nfull + dblk
            zs_next = scores(jnp.maximum(jb - 1, 0))
            if ws is not None:
                accs = values(jb + 1, accs, ws)
            ws, runs = weights(jb, zs, runs, True)
            zs = zs_next

        def full_block(n, carry):
            zs, ws, runs, accs = carry
            jb = nfull - 1 - n
            zs_next = scores(jnp.maximum(jb - 1, 0))
            accs = values(jb + 1, accs, ws)
            ws, runs = weights(jb, zs, runs, False)
            return zs_next, ws, runs, accs

        _, ws, runs, accs = lax.fori_loop(0, nfull, full_block, (zs, ws, runs, accs))
        accs = values(0, accs, ws)
        y_ref[...] = jnp.where(hi_lanes, accs[1], accs[0]).astype(BF16)
        lt_ref[...] = jnp.where(hi_lanes, runs[1], runs[0])

    return pl.pallas_call(
        body, name=name, grid=(npair, t // tq),
        in_specs=[pl.BlockSpec((tq, LANES), lambda p, i: (i, p)),
                  pl.BlockSpec((t, LANES), lambda p, i: (0, npair + p)),
                  pl.BlockSpec((t, LANES), lambda p, i: (0, 2 * npair + p)),
                  pl.BlockSpec((SB_KEYS, SB_KEYS), lambda p, i: (0, 0))],
        out_specs=[pl.BlockSpec((tq, LANES), lambda p, i: (i, p))] * 2,
        out_shape=[jax.ShapeDtypeStruct((t, npair * LANES), BF16), jax.ShapeDtypeStruct((t, npair * LANES), F32)],
        compiler_params=_cp())(pb, pb, pb, cmat)


def _attn_bwd(pb, dy, ltot, *, name, tq=256):
    t = pb.shape[0]
    npair = pb.shape[1] // 3 // LANES
    tq = min(tq, t)
    nq = t // tq
    _, cmats = _sb_consts()
    scale = 1.0 / math.sqrt(LANES // 2)

    def body(q_ref, k_ref, v_ref, dy_ref, lt_ref, c_ref, dq_ref, dk_ref, dv_ref, dk_acc, dv_acc):
        i = pl.program_id(1)

        @pl.when(i == 0)
        def _():
            dk_acc[...] = jnp.zeros_like(dk_acc)
            dv_acc[...] = jnp.zeros_like(dv_acc)

        lane = lax.broadcasted_iota(jnp.int32, (tq, LANES), 1)
        rowpos = i * tq + lax.broadcasted_iota(jnp.int32, (tq, SB_KEYS), 0)
        colid = lax.broadcasted_iota(jnp.int32, (tq, SB_KEYS), 1)
        q2 = q_ref[...] * jnp.asarray(scale, BF16)
        do2 = dy_ref[...].astype(BF16)
        ltv = lt_ref[...]
        c_upto, c_before = c_ref[0], c_ref[1]
        hi_lanes = lane >= LANES // 2
        sels = [hi_lanes == (hh == 1) for hh in range(2)]
        qhs = [jnp.where(s, q2, jnp.zeros_like(q2)) for s in sels]
        dohs = [jnp.where(s, do2, jnp.zeros_like(do2)) for s in sels]
        lts = [ltv[:, 0:1], ltv[:, LANES // 2:LANES // 2 + 1]]
        per_q = tq // SB_KEYS

        def blk(jb):
            return pl.ds(pl.multiple_of(jb * SB_KEYS, SB_KEYS), SB_KEYS)

        def scores(jb):
            kb, vb = k_ref[blk(jb), :], v_ref[blk(jb), :]
            return tuple((_dot(qhs[hh], kb, 1, 1), _dot(dohs[hh], vb, 1, 1)) for hh in range(2))

        def products(jb, dqs, pend):
            kb = k_ref[blk(jb), :]
            dk = _dot(pend[0][0], qhs[0], 0, 0) + _dot(pend[1][0], qhs[1], 0, 0)
            dv = _dot(pend[0][1], dohs[0], 0, 0) + _dot(pend[1][1], dohs[1], 0, 0)
            dk_acc[blk(jb), :] += dk
            dv_acc[blk(jb), :] += dv
            return tuple(dqs[hh] + _dot(pend[hh][0], kb) for hh in range(2))

        def chain(jb, zs, sums, masked):
            mask = (jb * SB_KEYS + colid) < rowpos if masked else None
            pend, new_sums = [], []
            for hh in range(2):
                z, da = zs[hh]
                csum, prun = sums[hh]
                lb = _log_sigmoid(z)
                lk = lb - z
                if masked:
                    lk = jnp.where(mask, lk, 0.0)
                lk_hi, lk_lo = _split2(lk)
                cs = _dot(lk_hi, c_upto) + _dot(lk_lo, c_upto)
                a = jnp.exp(lb + ((lts[hh] - csum) - cs))
                if masked:
                    a = jnp.where(mask, a, 0.0)
                e = a * da
                e_hi, e_lo = _split2(e)
                ce = _dot(e_hi, c_before) + _dot(e_lo, c_before)
                beta = jnp.exp(lb)
                dz = e * (1.0 - beta) - (prun + ce) * beta
                if masked:
                    dz = jnp.where(mask, dz, 0.0)
                pend.append((dz.astype(BF16), a.astype(BF16)))
                new_sums.append((csum + jnp.sum(lk, axis=1, keepdims=True), prun + jnp.sum(e, axis=1, keepdims=True)))
            return tuple(pend), tuple(new_sums)

        zero = jnp.zeros((tq, LANES), F32)
        zero_b = jnp.zeros((tq, SB_KEYS), BF16)
        nfull = i * per_q
        last = nfull + per_q - 1

        def full_block(jb, carry):
            zs, pend, sums, dqs = carry
            zs_next = scores(jb + 1)
            dqs = products(jnp.maximum(jb - 1, 0), dqs, pend)
            pend, sums = chain(jb, zs, sums, False)
            return zs_next, pend, sums, dqs

        zero_col = jnp.zeros((tq, 1), F32)
        carry = (scores(0), ((zero_b, zero_b),) * 2, ((zero_col, zero_col),) * 2, (zero, zero))
        zs, pend, sums, dqs = lax.fori_loop(0, nfull, full_block, carry)
        for dblk in range(per_q):
            jb = nfull + dblk
            zs_next = scores(jnp.minimum(jb + 1, last))
            dqs = products(jnp.maximum(jb - 1, 0), dqs, pend)
            pend, sums = chain(jb, zs, sums, True)
            zs = zs_next
        dqs = products(last, dqs, pend)
        dq_ref[...] = (jnp.where(hi_lanes, dqs[1], dqs[0]) * scale).astype(BF16)

        @pl.when(i == nq - 1)
        def _():
            dk_ref[...] = dk_acc[...].astype(BF16)
            dv_ref[...] = dv_acc[...].astype(BF16)

    blk = pl.BlockSpec((tq, LANES), lambda p, i: (i, p))
    full = pl.BlockSpec((t, LANES), lambda p, i: (0, p))
    return pl.pallas_call(
        body, name=name, grid=(npair, nq),
        in_specs=[blk,
                  pl.BlockSpec((t, LANES), lambda p, i: (0, npair + p)),
                  pl.BlockSpec((t, LANES), lambda p, i: (0, 2 * npair + p)),
                  pl.BlockSpec((tq, LANES), lambda p, i: (i, npair + p)),
                  blk,
                  pl.BlockSpec((2, SB_KEYS, SB_KEYS), lambda p, i: (0, 0, 0))],
        out_specs=[blk, full, full],
        out_shape=[jax.ShapeDtypeStruct((t, npair * LANES), BF16)] * 3,
        scratch_shapes=[pltpu.VMEM((t, LANES), F32), pltpu.VMEM((t, LANES), F32)],
        compiler_params=_cp())(pb, pb, pb, dy, ltot, cmats)


def _hgrn_consts():
    t = lax.broadcasted_iota(jnp.int32, (CHUNK, CHUNK), 0)
    s = lax.broadcasted_iota(jnp.int32, (CHUNK, CHUNK), 1)
    tri = (s <= t)
    cum = [tri.astype(F32)]
    masks = []
    for lvl in range(N_LEVELS):
        half = CHUNK >> (lvl + 1)
        ref_row = (t // (2 * half)) * (2 * half) + half - 1
        cum.append((s <= ref_row).astype(F32))
        same = (t // (2 * half)) == (s // (2 * half))
        masks.append((same & (t % (2 * half) >= half) & (s % (2 * half) < half)).astype(F32))
    masks.append((t == s).astype(F32))
    cum_all = jnp.concatenate(cum, axis=0).astype(BF16)
    suffix = (s >= t).astype(BF16)
    return cum_all, jnp.stack(masks), suffix


def _hgrn_gates(qr, fr, lbv):
    sg = _sigmoid(fr)
    fval = lbv + (1.0 - lbv) * sg
    kk = (1.0 - lbv) * _sigmoid(-fr)
    sq = _sigmoid(qr)
    return sg, fval, jnp.log(fval), kk, sq, qr * sq


def _lower_bound(c_ref):
    c = c_ref[...]
    mx = jnp.max(c, axis=0, keepdims=True)
    ex = jnp.exp(c - mx)
    return ex[1:2, :] / jnp.sum(ex, axis=0, keepdims=True)


def _hgrn_levels(ball, qs, kk):
    b = ball[:CHUNK]
    out = []
    for lvl in range(N_LEVELS):
        bref = ball[(lvl + 1) * CHUNK:(lvl + 2) * CHUNK]
        eq = jnp.exp(jnp.minimum(b - bref, 0.0))
        ek = jnp.exp(jnp.minimum(bref - b, 0.0))
        out.append((qs * eq, kk * ek, eq, ek))
    out.append((qs, kk, None, None))
    return out


def _split2(x):
    hi = x.astype(BF16)
    return hi, (x - hi.astype(F32)).astype(BF16)


def _hgrn_fwd(pc, c_lb, out_norm, *, name, tc=512):
    t = pc.shape[0]
    nh = pc.shape[1] // 4 // LANES
    tc = min(tc, t)
    nch = tc // CHUNK
    cum_all, masks, _ = _hgrn_consts()

    def body(q_ref, f_ref, i_ref, g_ref, lb_ref, on_ref, cum_ref, m_ref, y_ref, o_ref, st_ref, state):
        @pl.when(pl.program_id(1) == 0)
        def _():
            state[...] = jnp.zeros_like(state)

        lbv = _lower_bound(lb_ref)
        onv = on_ref[...]

        def chunk(c, carry):
            rows = pl.ds(pl.multiple_of(c * CHUNK, CHUNK), CHUNK)
            _, _, g, kk, _, qs = _hgrn_gates(q_ref[rows, :], f_ref[rows, :], lbv)
            vb = i_ref[rows, :].astype(BF16)
            ball = _dot_exact_lhs(cum_ref[...], g)
            b = ball[:CHUNK]
            scores = jnp.zeros((CHUNK, CHUNK), F32)
            for lvl, (ql, kl, _, _) in enumerate(_hgrn_levels(ball, qs, kk)):
                scores = scores + _dot(ql.astype(BF16), kl.astype(BF16), 1, 1) * m_ref[lvl]
            st = state[...]
            st_ref[c] = st
            o = _dot(scores.astype(BF16), vb) + _dot((qs * jnp.exp(b)).astype(BF16), st.astype(BF16), 1, 1)
            blast = b[CHUNK - 1:CHUNK, :]
            kdec = (kk * jnp.exp(blast - b)).astype(BF16)
            state[...] = st * jnp.exp(blast) + _dot(vb, kdec, 0, 0)
            o_ref[rows, :] = o
            rstd = lax.rsqrt(jnp.mean(o * o, axis=-1, keepdims=True) + RMS_EPS)
            gate = g_ref[rows, :]
            y_ref[rows, :] = (o * rstd * onv * (gate * _sigmoid(gate))).astype(BF16)
            return carry

        lax.fori_loop(0, nch, chunk, 0)

    def col(off):
        return pl.BlockSpec((tc, LANES), lambda h, i: (i, off + h))

    osp = pl.BlockSpec((tc, LANES), lambda h, i: (i, h))
    return pl.pallas_call(
        body, name=name, grid=(nh, t // tc),
        in_specs=[col(0), col(nh), col(2 * nh), col(3 * nh),
                  pl.BlockSpec((2, LANES), lambda h, i: (0, h)),
                  pl.BlockSpec((1, LANES), lambda h, i: (0, 0)),
                  pl.BlockSpec(cum_all.shape, lambda h, i: (0, 0)),
                  pl.BlockSpec(masks.shape, lambda h, i: (0, 0, 0))],
        out_specs=[osp, osp, pl.BlockSpec((None, nch, LANES, LANES), lambda h, i: (h, i, 0, 0))],
        out_shape=[jax.ShapeDtypeStruct((t, nh * LANES), BF16), jax.ShapeDtypeStruct((t, nh * LANES), F32),
                   jax.ShapeDtypeStruct((nh, t // CHUNK, LANES, LANES), F32)],
        scratch_shapes=[pltpu.VMEM((LANES, LANES), F32)],
        compiler_params=_cp())(pc, pc, pc, pc, c_lb, out_norm, cum_all, masks)


def _hgrn_bwd(pc, o_saved, states, dy, c_lb, out_norm, *, name, tc=512):
    t = pc.shape[0]
    nh = pc.shape[1] // 4 // LANES
    tc = min(tc, t)
    nch = tc // CHUNK
    nt = t // tc
    cum_all, masks, suffix = _hgrn_consts()

    def body(q_ref, f_ref, i_ref, g_ref, o_ref, st_ref, dy_ref, lb_ref, on_ref, cum_ref, m_ref, suf_ref,
             dq_ref, df_ref, di_ref, dg_ref, dlb_ref, don_ref, dstate):
        @pl.when(pl.program_id(1) == 0)
        def _():
            dstate[...] = jnp.zeros_like(dstate)
            dlb_ref[...] = jnp.zeros_like(dlb_ref)
            don_ref[...] = jnp.zeros_like(don_ref)

        lbv = _lower_bound(lb_ref)
        onv = on_ref[...]

        def chunk(n, carry):
            c = nch - 1 - n
            rows = pl.ds(pl.multiple_of(c * CHUNK, CHUNK), CHUNK)
            qr = q_ref[rows, :]
            sg, fval, g, kk, sq, qs = _hgrn_gates(qr, f_ref[rows, :], lbv)
            vb = i_ref[rows, :].astype(BF16)
            o = o_ref[rows, :]
            gate = g_ref[rows, :]
            sgt = _sigmoid(gate)
            rstd = lax.rsqrt(jnp.mean(o * o, axis=-1, keepdims=True) + RMS_EPS)
            ohat = o * rstd
            dyv = dy_ref[rows, :]
            don = dyv * (gate * sgt)
            dg_ref[rows, :] = (dyv * ohat * onv * (sgt * (1.0 + gate * (1.0 - sgt)))).astype(BF16)
            don_ref[...] += jnp.sum(don * ohat, axis=0, keepdims=True)
            dxhat = don * onv
            dob = (rstd * (dxhat - ohat * jnp.mean(dxhat * ohat, axis=-1, keepdims=True))).astype(BF16)
            ball = _dot_exact_lhs(cum_ref[...], g)
            b = ball[:CHUNK]
            blast = b[CHUNK - 1:CHUNK, :]
            eb = jnp.exp(b)
            edec = jnp.exp(blast - b)
            st32 = st_ref[c]
            st = st32.astype(BF16)
            dst = dstate[...]
            dstb = dst.astype(BF16)
            da = _dot(dob, vb, 1, 1)
            levels = _hgrn_levels(ball, qs, kk)
            scores = jnp.zeros((CHUNK, CHUNK), F32)
            dq = eb * _dot(dob, st)
            dk_inter = edec * _dot(vb, dstb)
            dk = dk_inter
            for lvl, (ql, kl, eq, ek) in enumerate(levels):
                mk = m_ref[lvl]
                (qh, qlo), (kh, klo) = _split2(ql), _split2(kl)
                scores = scores + _dot(qh, kh, 1, 1) * mk
                dal = (da * mk).astype(BF16)
                dql = _dot(dal, kh) + _dot(dal, klo)
                dkl = _dot(dal, qh, 0, 0) + _dot(dal, qlo, 0, 0)
                dq = dq + (dql if eq is None else dql * eq)
                dk = dk + (dkl if ek is None else dkl * ek)
            kdec = (kk * edec).astype(BF16)
            dv = _dot(scores.astype(BF16), dob, 0, 0) + _dot(kdec, dstb, 1, 1)
            dstate[...] = dst * jnp.exp(blast) + _dot(dob, (qs * eb).astype(BF16), 0, 0)
            db = qs * dq - kk * dk
            last = jnp.sum(kk * dk_inter, axis=0, keepdims=True) + jnp.exp(blast) * jnp.sum(dst * st32, axis=0, keepdims=True)
            dgl = _dot_exact_lhs(suf_ref[...], db) + last
            dfv = dgl / fval - dk
            df_ref[rows, :] = (dfv * (1.0 - lbv) * sg * (1.0 - sg)).astype(BF16)
            dlb_ref[...] += jnp.sum(dfv * (1.0 - sg), axis=0, keepdims=True)
            dq_ref[rows, :] = (dq * (sq * (1.0 + qr * (1.0 - sq)))).astype(BF16)
            di_ref[rows, :] = dv.astype(BF16)
            return carry

        lax.fori_loop(0, nch, chunk, 0)

    def col(off):
        return pl.BlockSpec((tc, LANES), lambda h, i: (nt - 1 - i, off + h))

    osp = pl.BlockSpec((tc, LANES), lambda h, i: (nt - 1 - i, h))
    vec = pl.BlockSpec((1, LANES), lambda h, i: (0, h))
    return pl.pallas_call(
        body, name=name, grid=(nh, nt),
        in_specs=[col(0), col(nh), col(2 * nh), col(3 * nh), osp,
                  pl.BlockSpec((None, nch, LANES, LANES), lambda h, i: (h, nt - 1 - i, 0, 0)),
                  osp,
                  pl.BlockSpec((2, LANES), lambda h, i: (0, h)),
                  pl.BlockSpec((1, LANES), lambda h, i: (0, 0)),
                  pl.BlockSpec(cum_all.shape, lambda h, i: (0, 0)),
                  pl.BlockSpec(masks.shape, lambda h, i: (0, 0, 0)),
                  pl.BlockSpec(suffix.shape, lambda h, i: (0, 0))],
        out_specs=[osp, osp, osp, osp, vec, vec],
        out_shape=[jax.ShapeDtypeStruct((t, nh * LANES), BF16)] * 4 + [jax.ShapeDtypeStruct((1, nh * LANES), F32)] * 2,
        scratch_shapes=[pltpu.VMEM((LANES, LANES), F32)],
        compiler_params=_cp())(pc, pc, pc, pc, o_saved, states, dy, c_lb, out_norm, cum_all, masks, suffix)


HBM_SPEC = pl.BlockSpec(memory_space=pltpu.HBM)


def _all_gather(xs, *, name):
    def body(x_ref, out_ref, send_sems, recv_sems, local_sem):
        x, y, c = lax.axis_index("x"), lax.axis_index("y"), lax.axis_index("c")
        me, sibling = (x, y, c), (x, y, 1 - c)
        chips = [(1 - x, y), (x, 1 - y), (1 - x, 1 - y)]

        def rows(px, py, pc):
            return out_ref.at[4 * px + 2 * py + pc]

        def copy(k, block, to, src=None):
            return pltpu.make_async_remote_copy(
                src_ref=rows(*block) if src is None else src, dst_ref=rows(*block),
                send_sem=send_sems.at[k], recv_sem=recv_sems.at[k], device_id=to, device_id_type=MESH)

        mine = pltpu.make_async_copy(x_ref, rows(*me), local_sem)
        mine.start()
        first = [copy(0, me, sibling, src=x_ref)]
        first += [copy(1 + j, me, (*chip, c), src=x_ref) for j, chip in enumerate(chips)]
        for cp in first:
            cp.start()
        passed = [copy(4 + j, (*chip, c), sibling) for j, chip in enumerate(chips)]
        for j, chip in enumerate(chips):
            copy(1 + j, (*chip, c), me).wait_recv()
            passed[j].start()
        copy(0, sibling, me).wait_recv()
        for j, chip in enumerate(chips):
            copy(4 + j, (*chip, 1 - c), me).wait_recv()
        for cp in first + passed:
            cp.wait_send()
        mine.wait()

    return pl.pallas_call(
        body, name=name, in_specs=[HBM_SPEC], out_specs=HBM_SPEC,
        out_shape=jax.ShapeDtypeStruct((8,) + xs.shape, xs.dtype),
        scratch_shapes=[pltpu.SemaphoreType.DMA((7,)), pltpu.SemaphoreType.DMA((7,)), pltpu.SemaphoreType.DMA])(xs)


def _sibling_exchange(s, *, name):
    def body(s_ref, rb_ref, send_sem, recv_sem):
        x, y, c = lax.axis_index("x"), lax.axis_index("y"), lax.axis_index("c")
        cp = pltpu.make_async_remote_copy(
            src_ref=s_ref.at[1 - c], dst_ref=rb_ref, send_sem=send_sem, recv_sem=recv_sem,
            device_id=(x, y, 1 - c), device_id_type=MESH)
        cp.start()
        cp.wait()

    return pl.pallas_call(
        body, name=name, in_specs=[HBM_SPEC], out_specs=HBM_SPEC,
        out_shape=jax.ShapeDtypeStruct(s.shape[1:], s.dtype),
        scratch_shapes=[pltpu.SemaphoreType.DMA, pltpu.SemaphoreType.DMA])(s)


def _pair_add(s, rb, core, *, name, tb=512):
    _, n, c = s.shape
    assert n % tb == 0

    def body(core_ref, a_ref, b_ref, o_ref):
        o_ref[...] = (a_ref[...].astype(F32) + b_ref[...].astype(F32)).astype(BF16)

    return pl.pallas_call(
        body, name=name,
        grid_spec=pltpu.PrefetchScalarGridSpec(
            num_scalar_prefetch=1, grid=(n // tb,),
            in_specs=[pl.BlockSpec((None, tb, c), lambda i, cr: (cr[0], i, 0)),
                      pl.BlockSpec((tb, c), lambda i, cr: (i, 0))],
            out_specs=pl.BlockSpec((tb, c), lambda i, cr: (i, 0))),
        out_shape=jax.ShapeDtypeStruct((n, c), BF16), compiler_params=_cp())(core, s, rb)


def _chip_exchange(p, *, name):
    def body(p_ref, out_ref, send_sems, recv_sems, local_sem):
        x, y, c = lax.axis_index("x"), lax.axis_index("y"), lax.axis_index("c")
        mine = 2 * x + y
        own = pltpu.make_async_copy(p_ref.at[mine], out_ref.at[mine], local_sem)
        own.start()
        copies = []
        for k, (tx, ty) in enumerate([(1 - x, y), (x, 1 - y), (1 - x, 1 - y)]):
            copies.append(pltpu.make_async_remote_copy(
                src_ref=p_ref.at[2 * tx + ty], dst_ref=out_ref.at[mine],
                send_sem=send_sems.at[k], recv_sem=recv_sems.at[k], device_id=(tx, ty, c), device_id_type=MESH))
        for cp in copies:
            cp.start()
        for cp in copies:
            cp.wait()
        own.wait()

    return pl.pallas_call(
        body, name=name, in_specs=[HBM_SPEC], out_specs=HBM_SPEC,
        out_shape=jax.ShapeDtypeStruct(p.shape, p.dtype),
        scratch_shapes=[pltpu.SemaphoreType.DMA((3,)), pltpu.SemaphoreType.DMA((3,)), pltpu.SemaphoreType.DMA])(p)


def _adamw_math(w, g, m, v):
    m2 = ADAM_B1 * m + (1.0 - ADAM_B1) * g
    v2 = ADAM_B2 * v + (1.0 - ADAM_B2) * (g * g)
    m_hat = m2 / (1.0 - ADAM_B1 ** ADAM_STEP)
    v_hat = v2 / (1.0 - ADAM_B2 ** ADAM_STEP)
    return -ADAM_LR * (m_hat / (jnp.sqrt(v_hat) + ADAM_EPS) + ADAM_WD * w), m2, v2


def _grad_sum(parts, *, name, tb=512):
    _, r, c = parts.shape

    def body(p0, p1, p2, p3, g_out):
        g_out[...] = ((p0[...].astype(F32) + p1[...].astype(F32)) + p2[...].astype(F32)) + p3[...].astype(F32)

    def part(ch):
        return pl.BlockSpec((None, tb, c), lambda i: (ch, i, 0))

    return pl.pallas_call(
        body, name=name, grid=(r // tb,), in_specs=[part(0), part(1), part(2), part(3)],
        out_specs=pl.BlockSpec((tb, c), lambda i: (i, 0)), out_shape=jax.ShapeDtypeStruct((r, c), F32),
        compiler_params=_cp())(parts, parts, parts, parts)


def _adamw_shard(g, off, w, m, v, *, name):
    n, c = w.shape
    tb = next(b for b in (512, 384, 352, 256, 128, 64, 16) if n % b == 0 and off % b == 0)

    def body(g_ref, w_ref, m_ref, v_ref, d_out, m_out, v_out):
        d, m2, v2 = _adamw_math(w_ref[...], g_ref[...], m_ref[...], v_ref[...])
        d_out[...] = d
        m_out[...] = m2
        v_out[...] = v2

    row = pl.BlockSpec((tb, c), lambda i: (i, 0))
    return pl.pallas_call(
        body, name=name, grid=(n // tb,),
        in_specs=[pl.BlockSpec((tb, c), lambda i: (off // tb + i, 0)), row, row, row],
        out_specs=[row] * 3, out_shape=[jax.ShapeDtypeStruct((n, c), F32)] * 3,
        compiler_params=_cp())(g, w, m, v)


SLOT = 8
SMALL_ROWS = 6 * SLOT
ROW_LB = 4 * SLOT


def _small_update(gath, w, m, v, *, name):
    def body(g_ref, w_ref, m_ref, v_ref, g_out, d_out, m_out, v_out):
        tot = g_ref[0]
        for k in range(1, 8):
            tot = tot + g_ref[k]
        wv = w_ref[...]
        c0, c1 = wv[ROW_LB:ROW_LB + 1, :], wv[ROW_LB + 1:ROW_LB + 2, :]
        mx = jnp.maximum(c0, c1)
        e0, e1 = jnp.exp(c0 - mx), jnp.exp(c1 - mx)
        lb = e1 / (e0 + e1)
        gl = tot[ROW_LB:ROW_LB + 1, :] * lb * (1.0 - lb)
        row = lax.broadcasted_iota(jnp.int32, tot.shape, 0)
        g = jnp.where(row == ROW_LB, -gl, jnp.where(row == ROW_LB + 1, gl, tot))
        d, m2, v2 = _adamw_math(wv, g, m_ref[...], v_ref[...])
        g_out[...] = g
        d_out[...] = d
        m_out[...] = m2
        v_out[...] = v2

    return pl.pallas_call(
        body, name=name, out_shape=[jax.ShapeDtypeStruct(w.shape, F32)] * 4, compiler_params=_cp())(gath, w, m, v)


D_MODEL = 1024
PACK_ROWS = 5632
CONV_ROW = 5376


def _ffn_fwd(h, gain, wg, wu, wd, tag):
    xn, gg, uu, act = _norm_gate_up(h, gain, wg, wu, name=f"{tag}_gate_up")
    out = _mm([(act, wd)], residual=h, alpha=MACARON, tn=1024, name=f"{tag}_down")
    return out, (h, xn, gg, uu, act)


def _ffn_bwd(dout, saved, gain, wg, wu, wd, tag):
    h, xn, gg, uu, act = saved
    dg, du = _swiglu_bwd(dout, wd, gg, uu, name=f"{tag}_dact")
    dwd = _mm([(act, dout)], ta=True, alpha=MACARON, tm=256, tn=512, out_dtype=BF16, name=f"{tag}_dwd")
    dwg = _mm([(dg, xn)], ta=True, tm=256, tn=512, out_dtype=BF16, name=f"{tag}_dwg")
    dwu = _mm([(du, xn)], ta=True, tm=256, tn=512, out_dtype=BF16, name=f"{tag}_dwu")
    dxn = _mm([(dg, wg), (du, wu)], tm=256, tn=512, name=f"{tag}_dxn")
    dh, dgain = _rmsnorm_bwd(h, gain, dxn, dout, name=f"{tag}_norm_bwd")
    return dh, dwg, dwu, dwd, dgain


def kernel(x, ffn_pre_norm, ffn_pre_w_gate, ffn_pre_w_up, ffn_pre_w_down, mix_norm, ffn_post_norm, ffn_post_w_gate, ffn_post_w_up, ffn_post_w_down, ab_w_in, ab_conv_w, ab_w_out, c_w_in, c_lower_bounds, c_out_norm, c_w_out, final_norm, loss_target, m_ffn_pre_norm, m_ffn_pre_w_gate, m_ffn_pre_w_up, m_ffn_pre_w_down, m_mix_norm, m_ffn_post_norm, m_ffn_post_w_gate, m_ffn_post_w_up, m_ffn_post_w_down, m_ab_w_in, m_ab_conv_w, m_ab_w_out, m_c_w_in, m_c_lower_bounds, m_c_out_norm, m_c_w_out, m_final_norm, v_ffn_pre_norm, v_ffn_pre_w_gate, v_ffn_pre_w_up, v_ffn_pre_w_down, v_mix_norm, v_ffn_post_norm, v_ffn_post_w_gate, v_ffn_post_w_up, v_ffn_post_w_down, v_ab_w_in, v_ab_conv_w, v_ab_w_out, v_c_w_in, v_c_lower_bounds, v_c_out_norm, v_c_w_out, v_final_norm):
    d = D_MODEL
    h0 = x[0]
    target = loss_target[0]
    core = lax.axis_index("c").astype(jnp.int32).reshape(1)

    big = [("pre_g", ffn_pre_w_gate, m_ffn_pre_w_gate, v_ffn_pre_w_gate),
           ("pre_u", ffn_pre_w_up, m_ffn_pre_w_up, v_ffn_pre_w_up),
           ("pre_d", ffn_pre_w_down, m_ffn_pre_w_down, v_ffn_pre_w_down),
           ("post_g", ffn_post_w_gate, m_ffn_post_w_gate, v_ffn_post_w_gate),
           ("post_u", ffn_post_w_up, m_ffn_post_w_up, v_ffn_post_w_up),
           ("post_d", ffn_post_w_down, m_ffn_post_w_down, v_ffn_post_w_down),
           ("ab_in", ab_w_in, m_ab_w_in, v_ab_w_in),
           ("ab_out", ab_w_out, m_ab_w_out, v_ab_w_out),
           ("c_in", c_w_in, m_c_w_in, v_c_w_in),
           ("c_out", c_w_out, m_c_w_out, v_c_w_out)]
    offs, off = {}, 0
    for tag, w, _, _ in big:
        offs[tag] = off
        off += w.size // d
    assert off == CONV_ROW

    def conv_rows(a, split):
        flat = a.reshape(-1)
        if split:
            hi = flat.astype(BF16)
            flat = jnp.concatenate([hi, (flat - hi.astype(F32)).astype(BF16)])
        return jnp.zeros((16, d), flat.dtype).at[0, :flat.shape[0]].set(flat)

    nconv = ab_conv_w.size
    col_sharded = {"pre_g", "pre_u", "post_g", "post_u", "ab_in", "c_in"}

    def pack_rows(tag, a):
        return (jnp.swapaxes(a, 1, 2) if tag in col_sharded else a).reshape(-1, d)

    wpack = jnp.concatenate(
        [pack_rows(tag, w).astype(BF16) for tag, w, _, _ in big]
        + [conv_rows(ab_conv_w, True), jnp.zeros((PACK_ROWS - CONV_ROW - 16, d), BF16)], axis=0)
    gath = _all_gather(wpack, name="gather_weights")

    def full_w(tag, layer, nrow):
        o = offs[tag] + layer * nrow
        return gath[:, o:o + nrow, :].reshape(8 * nrow, d)

    f_loc = ffn_pre_w_gate.shape[2]
    ffn_w = {}
    for pos in ("pre", "post"):
        for layer in range(2):
            ffn_w[pos, layer] = tuple(full_w(f"{pos}_{kind}", layer, f_loc) for kind in "gud")
    w_ab_in = full_w("ab_in", 0, ab_w_in.shape[2])
    w_ab_out = full_w("ab_out", 0, ab_w_out.shape[1])
    w_c_in = full_w("c_in", 0, c_w_in.shape[2])
    w_c_out = full_w("c_out", 0, c_w_out.shape[1])
    cg = gath[:, CONV_ROW, :2 * nconv].astype(F32)
    conv_w = (cg[:, :nconv] + cg[:, nconv:]).reshape(8, 3, -1).transpose(1, 0, 2).reshape(3, -1)
    half = w_ab_in.shape[0] // 2
    w_a_in, w_b_in = w_ab_in[:half], w_ab_in[half:]
    aw = half // 3

    h1, s_pre0 = _ffn_fwd(h0, ffn_pre_norm[0:1], *ffn_w["pre", 0], "l0pre")
    hn0 = _rmsnorm_fwd(h1, mix_norm[0:1], name="l0_mix_norm")
    pa = _mm([(hn0, w_a_in)], tb=True, tn=512, name="ab_proj_a")
    pb = _mm([(hn0, w_b_in)], tb=True, tn=512, out_dtype=BF16, name="ab_proj_b")
    ya = _conv_fwd(pa, conv_w, name="conv_fwd")
    yb, ltot = _attn_fwd(pb, name="attn_fwd")
    h2 = _mm([(ya, w_ab_out[:aw]), (yb, w_ab_out[aw:])], residual=h1, tn=1024, name="ab_out")
    h3, s_post0 = _ffn_fwd(h2, ffn_post_norm[0:1], *ffn_w["post", 0], "l0post")
    h4, s_pre1 = _ffn_fwd(h3, ffn_pre_norm[1:2], *ffn_w["pre", 1], "l1pre")
    hn1 = _rmsnorm_fwd(h4, mix_norm[1:2], name="l1_mix_norm")
    pc = _mm([(hn1, w_c_in)], tb=True, tn=512, name="c_proj")
    yc, o_saved, states = _hgrn_fwd(pc, c_lower_bounds, c_out_norm, name="hgrn_fwd")
    h5 = _mm([(yc, w_c_out)], residual=h4, tn=1024, name="c_out")
    h6, s_post1 = _ffn_fwd(h5, ffn_post_norm[1:2], *ffn_w["post", 1], "l1post")
    dh6, d_final, loss_vec = _loss_head(h6, final_norm.reshape(1, d), target, name="loss_head")

    gw = {}
    dh5, gw["post_g", 1], gw["post_u", 1], gw["post_d", 1], d_post1 = _ffn_bwd(
        dh6, s_post1, ffn_post_norm[1:2], *ffn_w["post", 1], "l1post")
    dyc = _mm([(dh5, w_c_out)], tb=True, tn=512, name="c_out_dy")
    g_c_out = _mm([(yc, dh5)], ta=True, out_dtype=BF16, name="c_out_dw")
    dcq, dcf, dci, dcg, dlb, d_onorm = _hgrn_bwd(pc, o_saved, states, dyc, c_lower_bounds, c_out_norm, name="hgrn_bwd")
    dparts = [dcq, dcf, dci, dcg]
    g_c_in = jnp.concatenate(
        [_mm([(dp, hn1)], ta=True, out_dtype=BF16, name=f"c_in_dw{i}") for i, dp in enumerate(dparts)], axis=0)
    cw = w_c_in.shape[0] // 4
    dhn1 = _mm([(dp, w_c_in[i * cw:(i + 1) * cw]) for i, dp in enumerate(dparts)], tm=256, name="c_in_dx")
    dh4, d_mix1 = _rmsnorm_bwd(h4, mix_norm[1:2], dhn1, dh5, name="l1_mix_norm_bwd")
    dh3, gw["pre_g", 1], gw["pre_u", 1], gw["pre_d", 1], d_pre1 = _ffn_bwd(
        dh4, s_pre1, ffn_pre_norm[1:2], *ffn_w["pre", 1], "l1pre")
    dh2, gw["post_g", 0], gw["post_u", 0], gw["post_d", 0], d_post0 = _ffn_bwd(
        dh3, s_post0, ffn_post_norm[0:1], *ffn_w["post", 0], "l0post")
    dyab = _mm([(dh2, w_ab_out)], tb=True, tn=512, name="ab_out_dy")
    g_ab_out = jnp.concatenate([_mm([(ya, dh2)], ta=True, out_dtype=BF16, name="ab_out_dw_a"),
                                _mm([(yb, dh2)], ta=True, out_dtype=BF16, name="ab_out_dw_b")], axis=0)
    dab, dac, dax, g_conv = _conv_bwd(pa, dyab, conv_w, name="conv_bwd")
    dq, dk, dv = _attn_bwd(pb, dyab, ltot, name="attn_bwd")
    dparts = [dab, dac, dax, dq, dk, dv]
    g_ab_in = jnp.concatenate(
        [_mm([(dp, hn0)], ta=True, out_dtype=BF16, name=f"ab_in_dw{i}") for i, dp in enumerate(dparts)], axis=0)
    dhn0 = _mm([(dp, w_ab_in[i * aw:(i + 1) * aw]) for i, dp in enumerate(dparts)], tm=256, name="ab_in_dx")
    dh1, d_mix0 = _rmsnorm_bwd(h1, mix_norm[0:1], dhn0, dh2, name="l0_mix_norm_bwd")
    dh0, gw["pre_g", 0], gw["pre_u", 0], gw["pre_d", 0], d_pre0 = _ffn_bwd(
        dh1, s_pre0, ffn_pre_norm[0:1], *ffn_w["pre", 0], "l0pre")

    def row_g(g):
        return g.reshape(8, g.shape[0] // 8, d)

    pieces = []
    for pos in ("pre", "post"):
        for kind in "gud":
            pieces += [row_g(gw[f"{pos}_{kind}", 0]), row_g(gw[f"{pos}_{kind}", 1])]
    pieces += [row_g(g_ab_in), row_g(g_ab_out), row_g(g_c_in), row_g(g_c_out)]
    gconv_own = g_conv.reshape(3, 8, -1).transpose(1, 0, 2).reshape(8, -1)
    pieces.append(jnp.zeros((8, 16, d), F32).at[:, 0, :nconv].set(gconv_own))
    pieces.append(jnp.zeros((8, PACK_ROWS - CONV_ROW - 16, d), F32))
    gpack = jnp.concatenate([p.astype(BF16) for p in pieces], axis=1)
    send = gpack.reshape(4, 2, PACK_ROWS, d).transpose(1, 0, 2, 3)
    from_sibling = _sibling_exchange(send, name="grad_sibling_exchange")
    chip_part = _pair_add(send.reshape(2, 4 * PACK_ROWS, d), from_sibling.reshape(4 * PACK_ROWS, d), core,
                          name="grad_pair_add").reshape(4, PACK_ROWS, d)
    parts = _chip_exchange(chip_part, name="grad_chip_exchange")

    g_all = _grad_sum(parts, name="grad_sum")
    upd = {}
    for tag, w, m, v in big:
        n = w.size // d
        g_rows = g_all[offs[tag]:offs[tag] + n]
        if tag in col_sharded:
            g_nat = jnp.swapaxes(g_rows.reshape(w.shape[0], w.shape[2], d), 1, 2)
            res = _adamw_shard(g_nat.reshape(n, d), 0, w.reshape(n, d), m.reshape(n, d), v.reshape(n, d),
                               name=f"adamw_{tag}")
        else:
            g_nat = g_rows.reshape(w.shape)
            res = _adamw_shard(g_all, offs[tag], w.reshape(n, d), m.reshape(n, d), v.reshape(n, d), name=f"adamw_{tag}")
        upd[tag] = [g_nat] + [r.reshape(w.shape) for r in res]
    res = _adamw_shard(g_all, CONV_ROW, conv_rows(ab_conv_w, False), conv_rows(m_ab_conv_w, False),
                       conv_rows(v_ab_conv_w, False), name="adamw_conv")
    upd["conv"] = [r[0, :nconv].reshape(ab_conv_w.shape) for r in [g_all[CONV_ROW:CONV_ROW + 16]] + list(res)]

    def small_pack(pre, mix, post, final, lbs, onorm):
        def slot(parts):
            out, r = jnp.zeros((SLOT, d), F32), 0
            for a in (parts if isinstance(parts, tuple) else (parts,)):
                out = out.at[r:r + a.shape[0], :a.shape[1]].set(a)
                r += a.shape[0]
            return out

        return jnp.concatenate([slot(pre), slot(mix), slot(post), slot(final.reshape(1, d)), slot(lbs), slot(onorm)], axis=0)

    d_on = d_onorm.reshape(-1, c_out_norm.shape[1]).sum(axis=0, keepdims=True)
    gsmall = small_pack((d_pre0, d_pre1), (d_mix0, d_mix1), (d_post0, d_post1), d_final, dlb, d_on)
    gsmall_all = _all_gather(gsmall, name="gather_small_grads")
    sres = _small_update(
        gsmall_all,
        small_pack(ffn_pre_norm, mix_norm, ffn_post_norm, final_norm, c_lower_bounds, c_out_norm),
        small_pack(m_ffn_pre_norm, m_mix_norm, m_ffn_post_norm, m_final_norm, m_c_lower_bounds, m_c_out_norm),
        small_pack(v_ffn_pre_norm, v_mix_norm, v_ffn_post_norm, v_final_norm, v_c_lower_bounds, v_c_out_norm),
        name="small_update")

    def small_out(r):
        return {"pre_norm": r[0:2], "mix_norm": r[SLOT:SLOT + 2], "post_norm": r[2 * SLOT:2 * SLOT + 2],
                "final": r[3 * SLOT], "lb": r[ROW_LB:ROW_LB + 2], "onorm": r[5 * SLOT:5 * SLOT + 1, :c_out_norm.shape[1]]}

    small = [small_out(r) for r in sres]
    outs = []
    for k in range(4):
        s = small[k]
        outs += [s["pre_norm"], upd["pre_g"][k], upd["pre_u"][k], upd["pre_d"][k], s["mix_norm"], s["post_norm"],
                 upd["post_g"][k], upd["post_u"][k], upd["post_d"][k], upd["ab_in"][k], upd["conv"][k],
                 upd["ab_out"][k], upd["c_in"][k], s["lb"], s["onorm"], upd["c_out"][k], s["final"]]
    loss = lax.psum(loss_vec[0, 0], ("x", "y", "c"))
    return (loss, dh0[None], *outs)
```

```python
import functools
import math

import jax
import jax.numpy as jnp
from jax import lax
from jax.experimental import pallas as pl
from jax.experimental.pallas import tpu as pltpu

F32 = jnp.float32
BF16 = jnp.bfloat16
MESH = pl.DeviceIdType.MESH

RMS_EPS = 1e-6
MACARON = 0.5
LANES = 128
CHUNK = 64
N_LEVELS = 6
SB_KEYS = 256
ADAM_LR, ADAM_B1, ADAM_B2, ADAM_EPS, ADAM_WD, ADAM_STEP = 0.001, 0.9, 0.999, 1e-08, 0.01, 10
VMEM_LIMIT = 48 * 1024 * 1024


def _cp(**kw):
    return pltpu.CompilerParams(vmem_limit_bytes=VMEM_LIMIT, **kw)


def _sigmoid(x):
    return 1.0 / (1.0 + jnp.exp(-x))


def _bf(x):
    return x if x.dtype == BF16 else x.astype(BF16)


def _split3(x):
    hi = x.astype(BF16)
    r1 = x - hi.astype(F32)
    mid = r1.astype(BF16)
    lo = (r1 - mid.astype(F32)).astype(BF16)
    return hi, mid, lo


def _dot(a, b, ca=1, cb=0):
    return lax.dot_general(a, b, (((ca,), (cb,)), ((), ())), preferred_element_type=F32)


def _dot_exact_lhs(m, x):
    hi, mid, lo = _split3(x)
    return _dot(m, hi) + _dot(m, mid) + _dot(m, lo)


def _dot_exact_rhs(x, m):
    hi, mid, lo = _split3(x)
    return _dot(hi, m) + _dot(mid, m) + _dot(lo, m)


def _mm(terms, *, name, ta=False, tb=False, out_dtype=F32, residual=None, alpha=1.0, tm=512, tn=512):
    nt = len(terms)
    a0, b0 = terms[0]
    m = a0.shape[1] if ta else a0.shape[0]
    n = b0.shape[0] if tb else b0.shape[1]
    tm, tn = min(tm, m), min(tn, n)
    assert m % tm == 0 and n % tn == 0, (name, m, n, tm, tn)
    has_res = residual is not None

    def body(*refs):
        o_ref = refs[-1]
        acc = None
        for i in range(nt):
            a = _bf(refs[2 * i][...])
            b = _bf(refs[2 * i + 1][...])
            p = _dot(a, b, 0 if ta else 1, 1 if tb else 0)
            acc = p if acc is None else acc + p
        if alpha != 1.0:
            acc = acc * alpha
        if has_res:
            acc = acc + refs[2 * nt][...]
        o_ref[...] = acc.astype(out_dtype)

    in_specs, args = [], []
    for a, b in terms:
        k = a.shape[0] if ta else a.shape[1]
        assert (b.shape[1] if tb else b.shape[0]) == k, (name, a.shape, b.shape)
        in_specs.append(pl.BlockSpec((k, tm), lambda i, j: (0, i)) if ta else pl.BlockSpec((tm, k), lambda i, j: (i, 0)))
        in_specs.append(pl.BlockSpec((tn, k), lambda i, j: (j, 0)) if tb else pl.BlockSpec((k, tn), lambda i, j: (0, j)))
        args += [a, b]
    if has_res:
        in_specs.append(pl.BlockSpec((tm, tn), lambda i, j: (i, j)))
        args.append(residual)
    return pl.pallas_call(
        body, name=name, grid=(m // tm, n // tn), in_specs=in_specs,
        out_specs=pl.BlockSpec((tm, tn), lambda i, j: (i, j)),
        out_shape=jax.ShapeDtypeStruct((m, n), out_dtype), compiler_params=_cp())(*args)


def _rmsnorm_fwd(x, gain, *, name, tm=512):
    t, d = x.shape
    tm = min(tm, t)

    def body(x_ref, g_ref, o_ref):
        xv = x_ref[...]
        rstd = lax.rsqrt(jnp.mean(xv * xv, axis=-1, keepdims=True) + RMS_EPS)
        o_ref[...] = (xv * rstd * g_ref[...]).astype(BF16)

    return pl.pallas_call(
        body, name=name, grid=(t // tm,),
        in_specs=[pl.BlockSpec((tm, d), lambda i: (i, 0)), pl.BlockSpec((1, d), lambda i: (0, 0))],
        out_specs=pl.BlockSpec((tm, d), lambda i: (i, 0)),
        out_shape=jax.ShapeDtypeStruct((t, d), BF16), compiler_params=_cp())(x, gain)


def _rmsnorm_bwd(x, gain, dxn, dres, *, name, tm=512):
    t, d = x.shape
    tm = min(tm, t)

    def body(x_ref, g_ref, dxn_ref, dres_ref, dx_ref, dg_ref):
        xv = x_ref[...]
        rstd = lax.rsqrt(jnp.mean(xv * xv, axis=-1, keepdims=True) + RMS_EPS)
        xhat = xv * rstd
        dxn_v = dxn_ref[...]
        dxhat = dxn_v * g_ref[...]
        dx = rstd * (dxhat - xhat * jnp.mean(dxhat * xhat, axis=-1, keepdims=True))
        dx_ref[...] = dres_ref[...] + dx

        @pl.when(pl.program_id(0) == 0)
        def _():
            dg_ref[...] = jnp.zeros_like(dg_ref)

        dg_ref[...] += jnp.sum(dxn_v * xhat, axis=0, keepdims=True)

    row = pl.BlockSpec((tm, d), lambda i: (i, 0))
    vec = pl.BlockSpec((1, d), lambda i: (0, 0))
    return pl.pallas_call(
        body, name=name, grid=(t // tm,), in_specs=[row, vec, row, row], out_specs=[row, vec],
        out_shape=[jax.ShapeDtypeStruct((t, d), F32), jax.ShapeDtypeStruct((1, d), F32)],
        compiler_params=_cp())(x, gain, dxn, dres)


def _loss_head(h, gain, target, *, name, tm=512):
    t, d = h.shape
    tm = min(tm, t)

    def body(h_ref, g_ref, t_ref, dh_ref, dg_ref, loss_ref):
        hv = h_ref[...]
        rstd = lax.rsqrt(jnp.mean(hv * hv, axis=-1, keepdims=True) + RMS_EPS)
        xhat = hv * rstd
        err = xhat * g_ref[...] - t_ref[...]
        dy = err * (1.0 / d)
        dxhat = dy * g_ref[...]
        dh_ref[...] = rstd * (dxhat - xhat * jnp.mean(dxhat * xhat, axis=-1, keepdims=True))

        @pl.when(pl.program_id(0) == 0)
        def _():
            dg_ref[...] = jnp.zeros_like(dg_ref)
            loss_ref[...] = jnp.zeros_like(loss_ref)

        dg_ref[...] += jnp.sum(dy * xhat, axis=0, keepdims=True)
        part = jnp.sum(jnp.sum(err * err, axis=-1, keepdims=True), axis=0, keepdims=True) * (0.5 / d)
        loss_ref[...] += jnp.broadcast_to(part, loss_ref.shape)

    row = pl.BlockSpec((tm, d), lambda i: (i, 0))
    vec = pl.BlockSpec((1, d), lambda i: (0, 0))
    return pl.pallas_call(
        body, name=name, grid=(t // tm,), in_specs=[row, vec, row],
        out_specs=[row, vec, pl.BlockSpec((1, LANES), lambda i: (0, 0))],
        out_shape=[jax.ShapeDtypeStruct((t, d), F32), jax.ShapeDtypeStruct((1, d), F32),
                   jax.ShapeDtypeStruct((1, LANES), F32)],
        compiler_params=_cp())(h, gain, target)


def _norm_gate_up(x, gain, wg, wu, *, name, tm=512, tf=1408):
    t, d = x.shape
    f = wg.shape[0]
    tm, tf = min(tm, t), min(tf, f)
    assert f % tf == 0

    def body(x_ref, g_ref, wg_ref, wu_ref, xn_ref, gg_ref, uu_ref, act_ref):
        @pl.when(pl.program_id(1) == 0)
        def _():
            xv = x_ref[...]
            rstd = lax.rsqrt(jnp.mean(xv * xv, axis=-1, keepdims=True) + RMS_EPS)
            xn_ref[...] = (xv * rstd * g_ref[...]).astype(BF16)

        xn = xn_ref[...]
        gv = _dot(xn, wg_ref[...], 1, 1)
        uv = _dot(xn, wu_ref[...], 1, 1)
        gg_ref[...] = gv.astype(BF16)
        uu_ref[...] = uv.astype(BF16)
        act_ref[...] = (gv * _sigmoid(gv) * uv).astype(BF16)

    row = pl.BlockSpec((tm, d), lambda i, j: (i, 0))
    wsp = pl.BlockSpec((tf, d), lambda i, j: (j, 0))
    osp = pl.BlockSpec((tm, tf), lambda i, j: (i, j))
    return pl.pallas_call(
        body, name=name, grid=(t // tm, f // tf),
        in_specs=[row, pl.BlockSpec((1, d), lambda i, j: (0, 0)), wsp, wsp],
        out_specs=[row, osp, osp, osp],
        out_shape=[jax.ShapeDtypeStruct((t, d), BF16)] + [jax.ShapeDtypeStruct((t, f), BF16)] * 3,
        compiler_params=_cp())(x, gain, wg, wu)


def _swiglu_bwd(dout, wd, gg, uu, *, name, tm=512, tf=1408):
    t, d = dout.shape
    f = wd.shape[0]
    tm, tf = min(tm, t), min(tf, f)

    def body(do_ref, wd_ref, g_ref, u_ref, dg_ref, du_ref):
        dact = _dot((do_ref[...] * MACARON).astype(BF16), wd_ref[...], 1, 1)
        gv = g_ref[...].astype(F32)
        uv = u_ref[...].astype(F32)
        sg = _sigmoid(gv)
        dg_ref[...] = (dact * uv * (sg * (1.0 + gv * (1.0 - sg)))).astype(BF16)
        du_ref[...] = (dact * (gv * sg)).astype(BF16)

    osp = pl.BlockSpec((tm, tf), lambda i, j: (i, j))
    return pl.pallas_call(
        body, name=name, grid=(t // tm, f // tf),
        in_specs=[pl.BlockSpec((tm, d), lambda i, j: (i, 0)), pl.BlockSpec((tf, d), lambda i, j: (j, 0)), osp, osp],
        out_specs=[osp, osp], out_shape=[jax.ShapeDtypeStruct((t, f), BF16)] * 2,
        compiler_params=_cp())(dout, wd, gg, uu)


def _shift_down(x, n):
    rows = lax.broadcasted_iota(jnp.int32, x.shape, 0)
    return jnp.where(rows >= n, pltpu.roll(x, n, 0), 0.0)


def _shift_up(x, n):
    t = x.shape[0]
    rows = lax.broadcasted_iota(jnp.int32, x.shape, 0)
    return jnp.where(rows < t - n, pltpu.roll(x, t - n, 0), 0.0)


def _conv_fwd(pa, conv_w, *, name):
    t = pa.shape[0]
    nb = pa.shape[1] // 3 // LANES

    def body(b_ref, c_ref, x_ref, w_ref, y_ref):
        u = c_ref[...] * x_ref[...]
        w = w_ref[...]
        conv = w[2:3, :] * u + w[1:2, :] * _shift_down(u, 1) + w[0:1, :] * _shift_down(u, 2)
        y_ref[...] = (b_ref[...] * conv).astype(BF16)

    def col(off):
        return pl.BlockSpec((t, LANES), lambda j: (0, off + j))

    return pl.pallas_call(
        body, name=name, grid=(nb,),
        in_specs=[col(0), col(nb), col(2 * nb), pl.BlockSpec((3, LANES), lambda j: (0, j))],
        out_specs=pl.BlockSpec((t, LANES), lambda j: (0, j)),
        out_shape=jax.ShapeDtypeStruct((t, nb * LANES), BF16), compiler_params=_cp())(pa, pa, pa, conv_w)


def _conv_bwd(pa, dy, conv_w, *, name):
    t = pa.shape[0]
    nb = pa.shape[1] // 3 // LANES

    def body(b_ref, c_ref, x_ref, dy_ref, w_ref, db_ref, dc_ref, dx_ref, dw_ref):
        cv, xv = c_ref[...], x_ref[...]
        u = cv * xv
        u1, u2 = _shift_down(u, 1), _shift_down(u, 2)
        w = w_ref[...]
        conv = w[2:3, :] * u + w[1:2, :] * u1 + w[0:1, :] * u2
        dyv = dy_ref[...]
        db_ref[...] = (dyv * conv).astype(BF16)
        dconv = dyv * b_ref[...]
        du = w[2:3, :] * dconv + w[1:2, :] * _shift_up(dconv, 1) + w[0:1, :] * _shift_up(dconv, 2)
        dc_ref[...] = (du * xv).astype(BF16)
        dx_ref[...] = (du * cv).astype(BF16)
        dw_ref[0:1, :] = jnp.sum(dconv * u2, axis=0, keepdims=True)
        dw_ref[1:2, :] = jnp.sum(dconv * u1, axis=0, keepdims=True)
        dw_ref[2:3, :] = jnp.sum(dconv * u, axis=0, keepdims=True)

    def col(off):
        return pl.BlockSpec((t, LANES), lambda j: (0, off + j))

    osp = pl.BlockSpec((t, LANES), lambda j: (0, j))
    wsp = pl.BlockSpec((3, LANES), lambda j: (0, j))
    return pl.pallas_call(
        body, name=name, grid=(nb,), in_specs=[col(0), col(nb), col(2 * nb), col(0), wsp],
        out_specs=[osp, osp, osp, wsp],
        out_shape=[jax.ShapeDtypeStruct((t, nb * LANES), BF16)] * 3 + [jax.ShapeDtypeStruct((3, nb * LANES), F32)],
        compiler_params=_cp())(pa, pa, pa, dy, conv_w)


def _sb_consts():
    j = lax.broadcasted_iota(jnp.int32, (SB_KEYS, SB_KEYS), 0)
    s = lax.broadcasted_iota(jnp.int32, (SB_KEYS, SB_KEYS), 1)
    after = (j > s).astype(BF16)
    upto = (j <= s).astype(BF16)
    before = (j < s).astype(BF16)
    return after, jnp.stack([upto, before])


def _log_sigmoid(z):
    return jnp.minimum(z, 0.0) - jnp.log(1.0 + jnp.exp(-jnp.abs(z)))


def _attn_fwd(pb, late_pack, *, name, tq=256):
    t = pb.shape[0]
    npair = pb.shape[1] // 3 // LANES
    tq = min(tq, t)
    nq = t // tq
    cmat, _ = _sb_consts()
    scale = 1.0 / math.sqrt(LANES // 2)

    def body(q_ref, k_ref, v_ref, c_ref, late_ref, y_ref, lt_ref, gath_ref, *sems):
        i = pl.program_id(1)
        pair = pl.program_id(0)
        start, forward, finish = _gather_phases(late_ref, gath_ref, *sems)
        pl.when((pair == 0) & (i == 0))(start)
        pl.when((pair == npair - 1) & (i == nq // 2))(forward)
        lane = lax.broadcasted_iota(jnp.int32, (tq, LANES), 1)
        rowpos = i * tq + lax.broadcasted_iota(jnp.int32, (tq, SB_KEYS), 0)
        colid = lax.broadcasted_iota(jnp.int32, (tq, SB_KEYS), 1)
        q2 = q_ref[...] * jnp.asarray(scale, BF16)
        cm = c_ref[...]
        hi_lanes = lane >= LANES // 2
        qhs = [jnp.where(hi_lanes == (hh == 1), q2, jnp.zeros_like(q2)) for hh in range(2)]
        per_q = tq // SB_KEYS

        def blk(jb):
            return pl.ds(pl.multiple_of(jb * SB_KEYS, SB_KEYS), SB_KEYS)

        def scores(jb):
            kb = k_ref[blk(jb), :]
            return tuple(_dot(qhs[hh], kb, 1, 1) for hh in range(2))

        def weights(jb, zs, runs, masked):
            mask = (jb * SB_KEYS + colid) < rowpos if masked else None
            ws, new_runs = [], []
            for hh in range(2):
                lb = _log_sigmoid(zs[hh])
                lk = lb - zs[hh]
                if masked:
                    lk = jnp.where(mask, lk, 0.0)
                lk_hi, lk_lo = _split2(lk)
                cs = _dot(lk_hi, cm) + _dot(lk_lo, cm)
                w = jnp.exp(lb + runs[hh] + cs)
                if masked:
                    w = jnp.where(mask, w, 0.0)
                ws.append(w.astype(BF16))
                new_runs.append(runs[hh] + jnp.sum(lk, axis=1, keepdims=True))
            return tuple(ws), tuple(new_runs)

        def values(jb, accs, ws):
            vb = v_ref[blk(jb), :]
            return tuple(accs[hh] + _dot(ws[hh], vb) for hh in range(2))

        zero = jnp.zeros((tq, LANES), F32)
        zero_col = jnp.zeros((tq, 1), F32)
        nfull = i * per_q
        runs, accs, ws = (zero_col, zero_col), (zero, zero), None
        zs = scores(nfull + per_q - 1)
        for dblk in reversed(range(per_q)):
            jb = nfull + dblk
            zs_next = scores(jnp.maximum(jb - 1, 0))
            if ws is not None:
                accs = values(jb + 1, accs, ws)
            ws, runs = weights(jb, zs, runs, True)
            zs = zs_next

        def full_block(n, carry):
            zs, ws, runs, accs = carry
            jb = nfull - 1 - n
            zs_next = scores(jnp.maximum(jb - 1, 0))
            accs = values(jb + 1, accs, ws)
            ws, runs = weights(jb, zs, runs, False)
            return zs_next, ws, runs, accs

        _, ws, runs, accs = lax.fori_loop(0, nfull, full_block, (zs, ws, runs, accs))
        accs = values(0, accs, ws)
        y_ref[...] = jnp.where(hi_lanes, accs[1], accs[0]).astype(BF16)
        lt_ref[...] = jnp.where(hi_lanes, runs[1], runs[0])
        pl.when((pair == npair - 1) & (i == nq - 1))(finish)

    return pl.pallas_call(
        body, name=name, grid=(npair, nq),
        in_specs=[pl.BlockSpec((tq, LANES), lambda p, i: (i, p)),
                  pl.BlockSpec((t, LANES), lambda p, i: (0, npair + p)),
                  pl.BlockSpec((t, LANES), lambda p, i: (0, 2 * npair + p)),
                  pl.BlockSpec((SB_KEYS, SB_KEYS), lambda p, i: (0, 0)),
                  HBM_SPEC],
        out_specs=[pl.BlockSpec((tq, LANES), lambda p, i: (i, p))] * 2 + [HBM_SPEC],
        out_shape=[jax.ShapeDtypeStruct((t, npair * LANES), BF16), jax.ShapeDtypeStruct((t, npair * LANES), F32),
                   jax.ShapeDtypeStruct((8,) + late_pack.shape, late_pack.dtype)],
        scratch_shapes=_gather_scratch(),
        compiler_params=_cp(dimension_semantics=("arbitrary", "arbitrary")))(pb, pb, pb, cmat, late_pack)


def _attn_bwd(pb, dy, ltot, chip_part, *, name, tq=256):
    t = pb.shape[0]
    npair = pb.shape[1] // 3 // LANES
    tq = min(tq, t)
    nq = t // tq
    _, cmats = _sb_consts()
    scale = 1.0 / math.sqrt(LANES // 2)

    def body(q_ref, k_ref, v_ref, dy_ref, lt_ref, c_ref, part_ref, dq_ref, dk_ref, dv_ref, parts_ref, dk_acc, dv_acc, *sems):
        i = pl.program_id(1)
        pair = pl.program_id(0)
        start, finish = _chip_exchange_phases(part_ref, parts_ref, *sems)
        pl.when((pair == 0) & (i == 0))(start)

        @pl.when(i == 0)
        def _():
            dk_acc[...] = jnp.zeros_like(dk_acc)
            dv_acc[...] = jnp.zeros_like(dv_acc)

        lane = lax.broadcasted_iota(jnp.int32, (tq, LANES), 1)
        rowpos = i * tq + lax.broadcasted_iota(jnp.int32, (tq, SB_KEYS), 0)
        colid = lax.broadcasted_iota(jnp.int32, (tq, SB_KEYS), 1)
        q2 = q_ref[...] * jnp.asarray(scale, BF16)
        do2 = dy_ref[...].astype(BF16)
        ltv = lt_ref[...]
        c_upto, c_before = c_ref[0], c_ref[1]
        hi_lanes = lane >= LANES // 2
        sels = [hi_lanes == (hh == 1) for hh in range(2)]
        qhs = [jnp.where(s, q2, jnp.zeros_like(q2)) for s in sels]
        dohs = [jnp.where(s, do2, jnp.zeros_like(do2)) for s in sels]
        lts = [ltv[:, 0:1], ltv[:, LANES // 2:LANES // 2 + 1]]
        per_q = tq // SB_KEYS

        def blk(jb):
            return pl.ds(pl.multiple_of(jb * SB_KEYS, SB_KEYS), SB_KEYS)

        def scores(jb):
            kb, vb = k_ref[blk(jb), :], v_ref[blk(jb), :]
            return tuple((_dot(qhs[hh], kb, 1, 1), _dot(dohs[hh], vb, 1, 1)) for hh in range(2))

        def products(jb, dqs, pend):
            kb = k_ref[blk(jb), :]
            dk = _dot(pend[0][0], qhs[0], 0, 0) + _dot(pend[1][0], qhs[1], 0, 0)
            dv = _dot(pend[0][1], dohs[0], 0, 0) + _dot(pend[1][1], dohs[1], 0, 0)
            dk_acc[blk(jb), :] += dk
            dv_acc[blk(jb), :] += dv
            return tuple(dqs[hh] + _dot(pend[hh][0], kb) for hh in range(2))

        def chain(jb, zs, sums, masked):
            mask = (jb * SB_KEYS + colid) < rowpos if masked else None
            pend, new_sums = [], []
            for hh in range(2):
                z, da = zs[hh]
                csum, prun = sums[hh]
                lb = _log_sigmoid(z)
                lk = lb - z
                if masked:
                    lk = jnp.where(mask, lk, 0.0)
                lk_hi, lk_lo = _split2(lk)
                cs = _dot(lk_hi, c_upto) + _dot(lk_lo, c_upto)
                a = jnp.exp(lb + ((lts[hh] - csum) - cs))
                if masked:
                    a = jnp.where(mask, a, 0.0)
                e = a * da
                e_hi, e_lo = _split2(e)
                ce = _dot(e_hi, c_before) + _dot(e_lo, c_before)
                beta = jnp.exp(lb)
                dz = e * (1.0 - beta) - (prun + ce) * beta
                if masked:
                    dz = jnp.where(mask, dz, 0.0)
                pend.append((dz.astype(BF16), a.astype(BF16)))
                new_sums.append((csum + jnp.sum(lk, axis=1, keepdims=True), prun + jnp.sum(e, axis=1, keepdims=True)))
            return tuple(pend), tuple(new_sums)

        zero = jnp.zeros((tq, LANES), F32)
        zero_b = jnp.zeros((tq, SB_KEYS), BF16)
        nfull = i * per_q
        last = nfull + per_q - 1

        def full_block(jb, carry):
            zs, pend, sums, dqs = carry
            zs_next = scores(jb + 1)
            dqs = products(jnp.maximum(jb - 1, 0), dqs, pend)
            pend, sums = chain(jb, zs, sums, False)
            return zs_next, pend, sums, dqs

        zero_col = jnp.zeros((tq, 1), F32)
        carry = (scores(0), ((zero_b, zero_b),) * 2, ((zero_col, zero_col),) * 2, (zero, zero))
        zs, pend, sums, dqs = lax.fori_loop(0, nfull, full_block, carry)
        for dblk in range(per_q):
            jb = nfull + dblk
            zs_next = scores(jnp.minimum(jb + 1, last))
            dqs = products(jnp.maximum(jb - 1, 0), dqs, pend)
            pend, sums = chain(jb, zs, sums, True)
            zs = zs_next
        dqs = products(last, dqs, pend)
        dq_ref[...] = (jnp.where(hi_lanes, dqs[1], dqs[0]) * scale).astype(BF16)

        @pl.when(i == nq - 1)
        def _():
            dk_ref[...] = dk_acc[...].astype(BF16)
            dv_ref[...] = dv_acc[...].astype(BF16)

        pl.when((pair == npair - 1) & (i == nq - 1))(finish)

    blk = pl.BlockSpec((tq, LANES), lambda p, i: (i, p))
    full = pl.BlockSpec((t, LANES), lambda p, i: (0, p))
    return pl.pallas_call(
        body, name=name, grid=(npair, nq),
        in_specs=[blk,
                  pl.BlockSpec((t, LANES), lambda p, i: (0, npair + p)),
                  pl.BlockSpec((t, LANES), lambda p, i: (0, 2 * npair + p)),
                  pl.BlockSpec((tq, LANES), lambda p, i: (i, npair + p)),
                  blk,
                  pl.BlockSpec((2, SB_KEYS, SB_KEYS), lambda p, i: (0, 0, 0)),
                  HBM_SPEC],
        out_specs=[blk, full, full, HBM_SPEC],
        out_shape=[jax.ShapeDtypeStruct((t, npair * LANES), BF16)] * 3 + [jax.ShapeDtypeStruct(chip_part.shape, chip_part.dtype)],
        scratch_shapes=[pltpu.VMEM((t, LANES), F32), pltpu.VMEM((t, LANES), F32)] + _chip_exchange_scratch(),
        compiler_params=_cp(dimension_semantics=("arbitrary", "arbitrary")))(pb, pb, pb, dy, ltot, cmats, chip_part)


def _hgrn_consts():
    t = lax.broadcasted_iota(jnp.int32, (CHUNK, CHUNK), 0)
    s = lax.broadcasted_iota(jnp.int32, (CHUNK, CHUNK), 1)
    tri = (s <= t)
    cum = [tri.astype(F32)]
    masks = []
    for lvl in range(N_LEVELS):
        half = CHUNK >> (lvl + 1)
        ref_row = (t // (2 * half)) * (2 * half) + half - 1
        cum.append((s <= ref_row).astype(F32))
        same = (t // (2 * half)) == (s // (2 * half))
        masks.append((same & (t % (2 * half) >= half) & (s % (2 * half) < half)).astype(F32))
    masks.append((t == s).astype(F32))
    cum_all = jnp.concatenate(cum, axis=0).astype(BF16)
    suffix = (s >= t).astype(BF16)
    return cum_all, jnp.stack(masks), suffix


def _hgrn_gates(qr, fr, lbv):
    sg = _sigmoid(fr)
    fval = lbv + (1.0 - lbv) * sg
    kk = (1.0 - lbv) * _sigmoid(-fr)
    sq = _sigmoid(qr)
    return sg, fval, jnp.log(fval), kk, sq, qr * sq


def _lower_bound(c_ref):
    c = c_ref[...]
    mx = jnp.max(c, axis=0, keepdims=True)
    ex = jnp.exp(c - mx)
    return ex[1:2, :] / jnp.sum(ex, axis=0, keepdims=True)


def _hgrn_levels(ball, qs, kk):
    b = ball[:CHUNK]
    out = []
    for lvl in range(N_LEVELS):
        bref = ball[(lvl + 1) * CHUNK:(lvl + 2) * CHUNK]
        eq = jnp.exp(jnp.minimum(b - bref, 0.0))
        ek = jnp.exp(jnp.minimum(bref - b, 0.0))
        out.append((qs * eq, kk * ek, eq, ek))
    out.append((qs, kk, None, None))
    return out


def _split2(x):
    hi = x.astype(BF16)
    return hi, (x - hi.astype(F32)).astype(BF16)


def _hgrn_fwd(pc, c_lb, out_norm, *, name, tc=512):
    t = pc.shape[0]
    nh = pc.shape[1] // 4 // LANES
    tc = min(tc, t)
    nch = tc // CHUNK
    cum_all, masks, _ = _hgrn_consts()

    def body(q_ref, f_ref, i_ref, g_ref, lb_ref, on_ref, cum_ref, m_ref, y_ref, o_ref, st_ref, state):
        @pl.when(pl.program_id(1) == 0)
        def _():
            state[...] = jnp.zeros_like(state)

        lbv = _lower_bound(lb_ref)
        onv = on_ref[...]

        def chunk(c, carry):
            rows = pl.ds(pl.multiple_of(c * CHUNK, CHUNK), CHUNK)
            _, _, g, kk, _, qs = _hgrn_gates(q_ref[rows, :], f_ref[rows, :], lbv)
            vb = i_ref[rows, :].astype(BF16)
            ball = _dot_exact_lhs(cum_ref[...], g)
            b = ball[:CHUNK]
            scores = jnp.zeros((CHUNK, CHUNK), F32)
            for lvl, (ql, kl, _, _) in enumerate(_hgrn_levels(ball, qs, kk)):
                scores = scores + _dot(ql.astype(BF16), kl.astype(BF16), 1, 1) * m_ref[lvl]
            st = state[...]
            st_ref[c] = st
            o = _dot(scores.astype(BF16), vb) + _dot((qs * jnp.exp(b)).astype(BF16), st.astype(BF16), 1, 1)
            blast = b[CHUNK - 1:CHUNK, :]
            kdec = (kk * jnp.exp(blast - b)).astype(BF16)
            state[...] = st * jnp.exp(blast) + _dot(vb, kdec, 0, 0)
            o_ref[rows, :] = o
            rstd = lax.rsqrt(jnp.mean(o * o, axis=-1, keepdims=True) + RMS_EPS)
            gate = g_ref[rows, :]
            y_ref[rows, :] = (o * rstd * onv * (gate * _sigmoid(gate))).astype(BF16)
            return carry

        lax.fori_loop(0, nch, chunk, 0)

    def col(off):
        return pl.BlockSpec((tc, LANES), lambda h, i: (i, off + h))

    osp = pl.BlockSpec((tc, LANES), lambda h, i: (i, h))
    return pl.pallas_call(
        body, name=name, grid=(nh, t // tc),
        in_specs=[col(0), col(nh), col(2 * nh), col(3 * nh),
                  pl.BlockSpec((2, LANES), lambda h, i: (0, h)),
                  pl.BlockSpec((1, LANES), lambda h, i: (0, 0)),
                  pl.BlockSpec(cum_all.shape, lambda h, i: (0, 0)),
                  pl.BlockSpec(masks.shape, lambda h, i: (0, 0, 0))],
        out_specs=[osp, osp, pl.BlockSpec((None, nch, LANES, LANES), lambda h, i: (h, i, 0, 0))],
        out_shape=[jax.ShapeDtypeStruct((t, nh * LANES), BF16), jax.ShapeDtypeStruct((t, nh * LANES), F32),
                   jax.ShapeDtypeStruct((nh, t // CHUNK, LANES, LANES), F32)],
        scratch_shapes=[pltpu.VMEM((LANES, LANES), F32)],
        compiler_params=_cp())(pc, pc, pc, pc, c_lb, out_norm, cum_all, masks)


def _hgrn_bwd(pc, o_saved, states, dy, c_lb, out_norm, *, name, tc=512):
    t = pc.shape[0]
    nh = pc.shape[1] // 4 // LANES
    tc = min(tc, t)
    nch = tc // CHUNK
    nt = t // tc
    cum_all, masks, suffix = _hgrn_consts()

    def body(q_ref, f_ref, i_ref, g_ref, o_ref, st_ref, dy_ref, lb_ref, on_ref, cum_ref, m_ref, suf_ref,
             dq_ref, df_ref, di_ref, dg_ref, dlb_ref, don_ref, dstate):
        @pl.when(pl.program_id(1) == 0)
        def _():
            dstate[...] = jnp.zeros_like(dstate)
            dlb_ref[...] = jnp.zeros_like(dlb_ref)
            don_ref[...] = jnp.zeros_like(don_ref)

        lbv = _lower_bound(lb_ref)
        onv = on_ref[...]

        def chunk(n, carry):
            c = nch - 1 - n
            rows = pl.ds(pl.multiple_of(c * CHUNK, CHUNK), CHUNK)
            qr = q_ref[rows, :]
            sg, fval, g, kk, sq, qs = _hgrn_gates(qr, f_ref[rows, :], lbv)
            vb = i_ref[rows, :].astype(BF16)
            o = o_ref[rows, :]
            gate = g_ref[rows, :]
            sgt = _sigmoid(gate)
            rstd = lax.rsqrt(jnp.mean(o * o, axis=-1, keepdims=True) + RMS_EPS)
            ohat = o * rstd
            dyv = dy_ref[rows, :]
            don = dyv * (gate * sgt)
            dg_ref[rows, :] = (dyv * ohat * onv * (sgt * (1.0 + gate * (1.0 - sgt)))).astype(BF16)
            don_ref[...] += jnp.sum(don * ohat, axis=0, keepdims=True)
            dxhat = don * onv
            dob = (rstd * (dxhat - ohat * jnp.mean(dxhat * ohat, axis=-1, keepdims=True))).astype(BF16)
            ball = _dot_exact_lhs(cum_ref[...], g)
            b = ball[:CHUNK]
            blast = b[CHUNK - 1:CHUNK, :]
            eb = jnp.exp(b)
            edec = jnp.exp(blast - b)
            st32 = st_ref[c]
            st = st32.astype(BF16)
            dst = dstate[...]
            dstb = dst.astype(BF16)
            da = _dot(dob, vb, 1, 1)
            levels = _hgrn_levels(ball, qs, kk)
            scores = jnp.zeros((CHUNK, CHUNK), F32)
            dq = eb * _dot(dob, st)
            dk_inter = edec * _dot(vb, dstb)
            dk = dk_inter
            for lvl, (ql, kl, eq, ek) in enumerate(levels):
                mk = m_ref[lvl]
                (qh, qlo), (kh, klo) = _split2(ql), _split2(kl)
                scores = scores + _dot(qh, kh, 1, 1) * mk
                dal = (da * mk).astype(BF16)
                dql = _dot(dal, kh) + _dot(dal, klo)
                dkl = _dot(dal, qh, 0, 0) + _dot(dal, qlo, 0, 0)
                dq = dq + (dql if eq is None else dql * eq)
                dk = dk + (dkl if ek is None else dkl * ek)
            kdec = (kk * edec).astype(BF16)
            dv = _dot(scores.astype(BF16), dob, 0, 0) + _dot(kdec, dstb, 1, 1)
            dstate[...] = dst * jnp.exp(blast) + _dot(dob, (qs * eb).astype(BF16), 0, 0)
            db = qs * dq - kk * dk
            last = jnp.sum(kk * dk_inter, axis=0, keepdims=True) + jnp.exp(blast) * jnp.sum(dst * st32, axis=0, keepdims=True)
            dgl = _dot_exact_lhs(suf_ref[...], db) + last
            dfv = dgl / fval - dk
            df_ref[rows, :] = (dfv * (1.0 - lbv) * sg * (1.0 - sg)).astype(BF16)
            dlb_ref[...] += jnp.sum(dfv * (1.0 - sg), axis=0, keepdims=True)
            dq_ref[rows, :] = (dq * (sq * (1.0 + qr * (1.0 - sq)))).astype(BF16)
            di_ref[rows, :] = dv.astype(BF16)
            return carry

        lax.fori_loop(0, nch, chunk, 0)

    def col(off):
        return pl.BlockSpec((tc, LANES), lambda h, i: (nt - 1 - i, off + h))

    osp = pl.BlockSpec((tc, LANES), lambda h, i: (nt - 1 - i, h))
    vec = pl.BlockSpec((1, LANES), lambda h, i: (0, h))
    return pl.pallas_call(
        body, name=name, grid=(nh, nt),
        in_specs=[col(0), col(nh), col(2 * nh), col(3 * nh), osp,
                  pl.BlockSpec((None, nch, LANES, LANES), lambda h, i: (h, nt - 1 - i, 0, 0)),
                  osp,
                  pl.BlockSpec((2, LANES), lambda h, i: (0, h)),
                  pl.BlockSpec((1, LANES), lambda h, i: (0, 0)),
                  pl.BlockSpec(cum_all.shape, lambda h, i: (0, 0)),
                  pl.BlockSpec(masks.shape, lambda h, i: (0, 0, 0)),
                  pl.BlockSpec(suffix.shape, lambda h, i: (0, 0))],
        out_specs=[osp, osp, osp, osp, vec, vec],
        out_shape=[jax.ShapeDtypeStruct((t, nh * LANES), BF16)] * 4 + [jax.ShapeDtypeStruct((1, nh * LANES), F32)] * 2,
        scratch_shapes=[pltpu.VMEM((LANES, LANES), F32)],
        compiler_params=_cp())(pc, pc, pc, pc, o_saved, states, dy, c_lb, out_norm, cum_all, masks, suffix)


HBM_SPEC = pl.BlockSpec(memory_space=pltpu.HBM)


def _gather_scratch():
    return [pltpu.SemaphoreType.DMA((7,)), pltpu.SemaphoreType.DMA((7,)), pltpu.SemaphoreType.DMA]


def _gather_phases(x_ref, out_ref, send_sems, recv_sems, local_sem):
    x, y, c = lax.axis_index("x"), lax.axis_index("y"), lax.axis_index("c")
    me, sibling = (x, y, c), (x, y, 1 - c)
    chips = [(1 - x, y), (x, 1 - y), (1 - x, 1 - y)]

    def rows(px, py, pc):
        return out_ref.at[4 * px + 2 * py + pc]

    def copy(k, block, to, src=None):
        return pltpu.make_async_remote_copy(
            src_ref=rows(*block) if src is None else src, dst_ref=rows(*block),
            send_sem=send_sems.at[k], recv_sem=recv_sems.at[k], device_id=to, device_id_type=MESH)

    mine = pltpu.make_async_copy(x_ref, rows(*me), local_sem)
    first = [copy(0, me, sibling, src=x_ref)]
    first += [copy(1 + j, me, (*chip, c), src=x_ref) for j, chip in enumerate(chips)]
    passed = [copy(4 + j, (*chip, c), sibling) for j, chip in enumerate(chips)]

    def start():
        mine.start()
        for cp in first:
            cp.start()

    def forward():
        for j, chip in enumerate(chips):
            copy(1 + j, (*chip, c), me).wait_recv()
            passed[j].start()

    def finish():
        copy(0, sibling, me).wait_recv()
        for j, chip in enumerate(chips):
            copy(4 + j, (*chip, 1 - c), me).wait_recv()
        for cp in first + passed:
            cp.wait_send()
        mine.wait()

    return start, forward, finish


def _all_gather(xs, *, name):
    def body(x_ref, out_ref, *sems):
        start, forward, finish = _gather_phases(x_ref, out_ref, *sems)
        start()
        forward()
        finish()

    return pl.pallas_call(
        body, name=name, in_specs=[HBM_SPEC], out_specs=HBM_SPEC,
        out_shape=jax.ShapeDtypeStruct((8,) + xs.shape, xs.dtype), scratch_shapes=_gather_scratch())(xs)


def _sibling_exchange(s, *, name):
    def body(s_ref, rb_ref, send_sem, recv_sem):
        x, y, c = lax.axis_index("x"), lax.axis_index("y"), lax.axis_index("c")
        cp = pltpu.make_async_remote_copy(
            src_ref=s_ref.at[1 - c], dst_ref=rb_ref, send_sem=send_sem, recv_sem=recv_sem,
            device_id=(x, y, 1 - c), device_id_type=MESH)
        cp.start()
        cp.wait()

    return pl.pallas_call(
        body, name=name, in_specs=[HBM_SPEC], out_specs=HBM_SPEC,
        out_shape=jax.ShapeDtypeStruct(s.shape[1:], s.dtype),
        scratch_shapes=[pltpu.SemaphoreType.DMA, pltpu.SemaphoreType.DMA])(s)


def _row_tile(n, cap=1024):
    return max(b for b in range(16, cap + 1, 16) if n % b == 0)


def _pair_add(s, rb, core, *, name):
    _, n, c = s.shape
    tb = _row_tile(n)

    def body(core_ref, a_ref, b_ref, o_ref):
        o_ref[...] = (a_ref[...].astype(F32) + b_ref[...].astype(F32)).astype(BF16)

    return pl.pallas_call(
        body, name=name,
        grid_spec=pltpu.PrefetchScalarGridSpec(
            num_scalar_prefetch=1, grid=(n // tb,),
            in_specs=[pl.BlockSpec((None, tb, c), lambda i, cr: (cr[0], i, 0)),
                      pl.BlockSpec((tb, c), lambda i, cr: (i, 0))],
            out_specs=pl.BlockSpec((tb, c), lambda i, cr: (i, 0))),
        out_shape=jax.ShapeDtypeStruct((n, c), BF16), compiler_params=_cp())(core, s, rb)


def _chip_exchange(p, *, name):
    def body(p_ref, out_ref, *sems):
        start, finish = _chip_exchange_phases(p_ref, out_ref, *sems)
        start()
        finish()

    return pl.pallas_call(
        body, name=name, in_specs=[HBM_SPEC], out_specs=HBM_SPEC,
        out_shape=jax.ShapeDtypeStruct(p.shape, p.dtype), scratch_shapes=_chip_exchange_scratch())(p)


def _chip_exchange_scratch():
    return [pltpu.SemaphoreType.DMA((3,)), pltpu.SemaphoreType.DMA((3,)), pltpu.SemaphoreType.DMA]


def _chip_exchange_phases(p_ref, out_ref, send_sems, recv_sems, local_sem):
    x, y, c = lax.axis_index("x"), lax.axis_index("y"), lax.axis_index("c")
    mine = 2 * x + y
    own = pltpu.make_async_copy(p_ref.at[mine], out_ref.at[mine], local_sem)
    copies = [pltpu.make_async_remote_copy(
        src_ref=p_ref.at[2 * tx + ty], dst_ref=out_ref.at[mine],
        send_sem=send_sems.at[k], recv_sem=recv_sems.at[k], device_id=(tx, ty, c), device_id_type=MESH)
        for k, (tx, ty) in enumerate([(1 - x, y), (x, 1 - y), (1 - x, 1 - y)])]

    def start():
        own.start()
        for cp in copies:
            cp.start()

    def finish():
        for cp in copies:
            cp.wait()
        own.wait()

    return start, finish


def _adamw_math(w, g, m, v):
    m2 = ADAM_B1 * m + (1.0 - ADAM_B1) * g
    v2 = ADAM_B2 * v + (1.0 - ADAM_B2) * (g * g)
    m_hat = m2 / (1.0 - ADAM_B1 ** ADAM_STEP)
    v_hat = v2 / (1.0 - ADAM_B2 ** ADAM_STEP)
    return -ADAM_LR * (m_hat / (jnp.sqrt(v_hat) + ADAM_EPS) + ADAM_WD * w), m2, v2


def _grad_sum(parts, *, name):
    _, r, c = parts.shape
    tb = _row_tile(r)

    def body(p0, p1, p2, p3, g_out):
        g_out[...] = ((p0[...].astype(F32) + p1[...].astype(F32)) + p2[...].astype(F32)) + p3[...].astype(F32)

    def part(ch):
        return pl.BlockSpec((None, tb, c), lambda i: (ch, i, 0))

    return pl.pallas_call(
        body, name=name, grid=(r // tb,), in_specs=[part(0), part(1), part(2), part(3)],
        out_specs=pl.BlockSpec((tb, c), lambda i: (i, 0)), out_shape=jax.ShapeDtypeStruct((r, c), F32),
        compiler_params=_cp())(parts, parts, parts, parts)


def _adamw_shard(g, g_off, w, m, v, w_off, n, prev, *, name):
    c = w.shape[1]
    tb = next(b for b in range(min(n, 512), 0, -16) if n % b == 0 and g_off % b == 0 and w_off % b == 0)

    def body(g_ref, w_ref, m_ref, v_ref, *rest):
        d_out, m_out, v_out = rest[-3:]
        d, m2, v2 = _adamw_math(w_ref[...], g_ref[...], m_ref[...], v_ref[...])
        d_out[...] = d
        m_out[...] = m2
        v_out[...] = v2

    row = pl.BlockSpec((tb, c), lambda i: (w_off // tb + i, 0))
    prev = list(prev) if prev is not None else []
    return pl.pallas_call(
        body, name=name, grid=(n // tb,),
        in_specs=[pl.BlockSpec((tb, c), lambda i: (g_off // tb + i, 0)), row, row, row] + [pl.BlockSpec(memory_space=pl.ANY)] * len(prev),
        out_specs=[row] * 3, out_shape=[jax.ShapeDtypeStruct(w.shape, F32)] * 3,
        input_output_aliases={4 + k: k for k in range(len(prev))},
        compiler_params=_cp())(g, w, m, v, *prev)


SLOT = 8
SMALL_ROWS = 6 * SLOT
ROW_LB = 4 * SLOT


def _small_update(gath, w, m, v, *, name):
    def body(g_ref, w_ref, m_ref, v_ref, g_out, d_out, m_out, v_out):
        tot = g_ref[0]
        for k in range(1, 8):
            tot = tot + g_ref[k]
        wv = w_ref[...]
        c0, c1 = wv[ROW_LB:ROW_LB + 1, :], wv[ROW_LB + 1:ROW_LB + 2, :]
        mx = jnp.maximum(c0, c1)
        e0, e1 = jnp.exp(c0 - mx), jnp.exp(c1 - mx)
        lb = e1 / (e0 + e1)
        gl = tot[ROW_LB:ROW_LB + 1, :] * lb * (1.0 - lb)
        row = lax.broadcasted_iota(jnp.int32, tot.shape, 0)
        g = jnp.where(row == ROW_LB, -gl, jnp.where(row == ROW_LB + 1, gl, tot))
        d, m2, v2 = _adamw_math(wv, g, m_ref[...], v_ref[...])
        g_out[...] = g
        d_out[...] = d
        m_out[...] = m2
        v_out[...] = v2

    return pl.pallas_call(
        body, name=name, out_shape=[jax.ShapeDtypeStruct(w.shape, F32)] * 4, compiler_params=_cp())(gath, w, m, v)


D_MODEL = 1024


def _ffn_fwd(h, gain, wg, wu, wd, tag):
    xn, gg, uu, act = _norm_gate_up(h, gain, wg, wu, name=f"{tag}_gate_up")
    out = _mm([(act, wd)], residual=h, alpha=MACARON, tn=1024, name=f"{tag}_down")
    return out, (h, xn, gg, uu, act)


def _ffn_bwd(dout, saved, gain, wg, wu, wd, tag):
    h, xn, gg, uu, act = saved
    dg, du = _swiglu_bwd(dout, wd, gg, uu, name=f"{tag}_dact")
    dwd = _mm([(act, dout)], ta=True, alpha=MACARON, tm=256, tn=512, out_dtype=BF16, name=f"{tag}_dwd")
    dwg = _mm([(dg, xn)], ta=True, tm=256, tn=512, out_dtype=BF16, name=f"{tag}_dwg")
    dwu = _mm([(du, xn)], ta=True, tm=256, tn=512, out_dtype=BF16, name=f"{tag}_dwu")
    dxn = _mm([(dg, wg), (du, wu)], tm=256, tn=512, name=f"{tag}_dxn")
    dh, dgain = _rmsnorm_bwd(h, gain, dxn, dout, name=f"{tag}_norm_bwd")
    return dh, dwg, dwu, dwd, dgain


def kernel(x, ffn_pre_norm, ffn_pre_w_gate, ffn_pre_w_up, ffn_pre_w_down, mix_norm, ffn_post_norm, ffn_post_w_gate, ffn_post_w_up, ffn_post_w_down, ab_w_in, ab_conv_w, ab_w_out, c_w_in, c_lower_bounds, c_out_norm, c_w_out, final_norm, loss_target, m_ffn_pre_norm, m_ffn_pre_w_gate, m_ffn_pre_w_up, m_ffn_pre_w_down, m_mix_norm, m_ffn_post_norm, m_ffn_post_w_gate, m_ffn_post_w_up, m_ffn_post_w_down, m_ab_w_in, m_ab_conv_w, m_ab_w_out, m_c_w_in, m_c_lower_bounds, m_c_out_norm, m_c_w_out, m_final_norm, v_ffn_pre_norm, v_ffn_pre_w_gate, v_ffn_pre_w_up, v_ffn_pre_w_down, v_mix_norm, v_ffn_post_norm, v_ffn_post_w_gate, v_ffn_post_w_up, v_ffn_post_w_down, v_ab_w_in, v_ab_conv_w, v_ab_w_out, v_c_w_in, v_c_lower_bounds, v_c_out_norm, v_c_w_out, v_final_norm):
    d = D_MODEL
    h0 = x[0]
    target = loss_target[0]
    core = lax.axis_index("c").astype(jnp.int32).reshape(1)

    big = [("pre_g", ffn_pre_w_gate, m_ffn_pre_w_gate, v_ffn_pre_w_gate),
           ("pre_u", ffn_pre_w_up, m_ffn_pre_w_up, v_ffn_pre_w_up),
           ("pre_d", ffn_pre_w_down, m_ffn_pre_w_down, v_ffn_pre_w_down),
           ("post_g", ffn_post_w_gate, m_ffn_post_w_gate, v_ffn_post_w_gate),
           ("post_u", ffn_post_w_up, m_ffn_post_w_up, v_ffn_post_w_up),
           ("post_d", ffn_post_w_down, m_ffn_post_w_down, v_ffn_post_w_down),
           ("ab_in", ab_w_in, m_ab_w_in, v_ab_w_in),
           ("ab_out", ab_w_out, m_ab_w_out, v_ab_w_out),
           ("c_in", c_w_in, m_c_w_in, v_c_w_in),
           ("c_out", c_w_out, m_c_w_out, v_c_w_out)]
    by_tag = {tag: (w, m, v) for tag, w, m, v in big}

    def layer_rows(tag):
        w = by_tag[tag][0]
        return w.size // d // w.shape[0]

    def layout(items):
        offs, off = {}, 0
        for item in items:
            offs[item] = off
            off += layer_rows(item[0])
        return offs, off

    ffn = [f"{pos}_{kind}" for pos in ("pre", "post") for kind in "gud"]
    early_items = [("pre_g", 0), ("pre_u", 0), ("pre_d", 0), ("ab_in", 0)]
    late_items = ([("pre_g", 1), ("pre_u", 1), ("pre_d", 1)] + [(f"post_{kind}", l) for l in (0, 1) for kind in "gud"]
                  + [("ab_out", 0), ("c_in", 0), ("c_out", 0)])
    early_offs, early_conv_row = layout(early_items)
    late_offs, _ = layout(late_items)
    grad_items = {"A": [(tag, 1) for tag in ffn] + [("c_in", 0), ("c_out", 0)],
                  "B": [(tag, 0) for tag in ffn] + [("ab_in", 0), ("ab_out", 0)]}
    grad_offs = {k: layout(items)[0] for k, items in grad_items.items()}
    grad_conv_row = layout(grad_items["B"])[1]

    def conv_rows(a, split):
        flat = a.reshape(-1)
        if split:
            hi = flat.astype(BF16)
            flat = jnp.concatenate([hi, (flat - hi.astype(F32)).astype(BF16)])
        return jnp.zeros((16, d), flat.dtype).at[0, :flat.shape[0]].set(flat)

    nconv = ab_conv_w.size
    col_sharded = {"pre_g", "pre_u", "post_g", "post_u", "ab_in", "c_in"}

    def pack_rows(item):
        tag, layer = item
        a = by_tag[tag][0][layer]
        return (a.T if tag in col_sharded else a).reshape(-1, d).astype(BF16)

    early_pack = jnp.concatenate([pack_rows(item) for item in early_items] + [conv_rows(ab_conv_w, True)], axis=0)
    late_pack = jnp.concatenate([pack_rows(item) for item in late_items], axis=0)
    gath_early = _all_gather(early_pack, name="gather_early_weights")

    def full_w(gath, offs, tag, layer, nrow):
        o = offs[tag, layer]
        return gath[:, o:o + nrow, :].reshape(8 * nrow, d)

    f_loc = ffn_pre_w_gate.shape[2]
    ffn_w = {("pre", 0): tuple(full_w(gath_early, early_offs, f"pre_{kind}", 0, f_loc) for kind in "gud")}
    w_ab_in = full_w(gath_early, early_offs, "ab_in", 0, ab_w_in.shape[2])
    cg = gath_early[:, early_conv_row, :2 * nconv].astype(F32)
    conv_w = (cg[:, :nconv] + cg[:, nconv:]).reshape(8, 3, -1).transpose(1, 0, 2).reshape(3, -1)
    half = w_ab_in.shape[0] // 2
    w_a_in, w_b_in = w_ab_in[:half], w_ab_in[half:]
    aw = half // 3

    h1, s_pre0 = _ffn_fwd(h0, ffn_pre_norm[0:1], *ffn_w["pre", 0], "l0pre")
    hn0 = _rmsnorm_fwd(h1, mix_norm[0:1], name="l0_mix_norm")
    pa = _mm([(hn0, w_a_in)], tb=True, tn=512, name="ab_proj_a")
    pb = _mm([(hn0, w_b_in)], tb=True, tn=512, out_dtype=BF16, name="ab_proj_b")
    ya = _conv_fwd(pa, conv_w, name="conv_fwd")
    yb, ltot, gath_late = _attn_fwd(pb, late_pack, name="attn_fwd_gather_late_weights")
    for pos, layer in (("post", 0), ("pre", 1), ("post", 1)):
        ffn_w[pos, layer] = tuple(full_w(gath_late, late_offs, f"{pos}_{kind}", layer, f_loc) for kind in "gud")
    w_ab_out = full_w(gath_late, late_offs, "ab_out", 0, ab_w_out.shape[1])
    w_c_in = full_w(gath_late, late_offs, "c_in", 0, c_w_in.shape[2])
    w_c_out = full_w(gath_late, late_offs, "c_out", 0, c_w_out.shape[1])
    h2 = _mm([(ya, w_ab_out[:aw]), (yb, w_ab_out[aw:])], residual=h1, tn=1024, name="ab_out")
    h3, s_post0 = _ffn_fwd(h2, ffn_post_norm[0:1], *ffn_w["post", 0], "l0post")
    h4, s_pre1 = _ffn_fwd(h3, ffn_pre_norm[1:2], *ffn_w["pre", 1], "l1pre")
    hn1 = _rmsnorm_fwd(h4, mix_norm[1:2], name="l1_mix_norm")
    pc = _mm([(hn1, w_c_in)], tb=True, tn=512, name="c_proj")
    yc, o_saved, states = _hgrn_fwd(pc, c_lower_bounds, c_out_norm, name="hgrn_fwd")
    h5 = _mm([(yc, w_c_out)], residual=h4, tn=1024, name="c_out")
    h6, s_post1 = _ffn_fwd(h5, ffn_post_norm[1:2], *ffn_w["post", 1], "l1post")
    dh6, d_final, loss_vec = _loss_head(h6, final_norm.reshape(1, d), target, name="loss_head")

    gw = {}
    dh5, gw["post_g", 1], gw["post_u", 1], gw["post_d", 1], d_post1 = _ffn_bwd(
        dh6, s_post1, ffn_post_norm[1:2], *ffn_w["post", 1], "l1post")
    dyc = _mm([(dh5, w_c_out)], tb=True, tn=512, name="c_out_dy")
    g_c_out = _mm([(yc, dh5)], ta=True, out_dtype=BF16, name="c_out_dw")
    dcq, dcf, dci, dcg, dlb, d_onorm = _hgrn_bwd(pc, o_saved, states, dyc, c_lower_bounds, c_out_norm, name="hgrn_bwd")
    dparts = [dcq, dcf, dci, dcg]
    g_c_in = jnp.concatenate(
        [_mm([(dp, hn1)], ta=True, out_dtype=BF16, name=f"c_in_dw{i}") for i, dp in enumerate(dparts)], axis=0)
    cw = w_c_in.shape[0] // 4
    dhn1 = _mm([(dp, w_c_in[i * cw:(i + 1) * cw]) for i, dp in enumerate(dparts)], tm=256, name="c_in_dx")
    dh4, d_mix1 = _rmsnorm_bwd(h4, mix_norm[1:2], dhn1, dh5, name="l1_mix_norm_bwd")
    dh3, gw["pre_g", 1], gw["pre_u", 1], gw["pre_d", 1], d_pre1 = _ffn_bwd(
        dh4, s_pre1, ffn_pre_norm[1:2], *ffn_w["pre", 1], "l1pre")
    dh2, gw["post_g", 0], gw["post_u", 0], gw["post_d", 0], d_post0 = _ffn_bwd(
        dh3, s_post0, ffn_post_norm[0:1], *ffn_w["post", 0], "l0post")
    dyab = _mm([(dh2, w_ab_out)], tb=True, tn=512, name="ab_out_dy")
    g_ab_out = jnp.concatenate([_mm([(ya, dh2)], ta=True, out_dtype=BF16, name="ab_out_dw_a"),
                                _mm([(yb, dh2)], ta=True, out_dtype=BF16, name="ab_out_dw_b")], axis=0)
    dab, dac, dax, g_conv = _conv_bwd(pa, dyab, conv_w, name="conv_bwd")

    def chip_partials(key, grads, extra=()):
        gpack = jnp.concatenate([grads[item].reshape(8, -1, d) for item in grad_items[key]] + list(extra), axis=1)
        rows = gpack.shape[1]
        send = gpack.reshape(4, 2, rows, d).transpose(1, 0, 2, 3)
        from_sibling = _sibling_exchange(send, name=f"grad{key}_sibling_exchange")
        return _pair_add(send.reshape(2, 4 * rows, d), from_sibling.reshape(4 * rows, d), core,
                         name=f"grad{key}_pair_add").reshape(4, rows, d)

    gw["c_in", 0], gw["c_out", 0] = g_c_in, g_c_out
    chip_part_a = chip_partials("A", gw)
    dq, dk, dv, parts_a = _attn_bwd(pb, dyab, ltot, chip_part_a, name="attn_bwd_exchange_layer1_grads")
    dparts = [dab, dac, dax, dq, dk, dv]
    g_ab_in = jnp.concatenate(
        [_mm([(dp, hn0)], ta=True, out_dtype=BF16, name=f"ab_in_dw{i}") for i, dp in enumerate(dparts)], axis=0)
    dhn0 = _mm([(dp, w_ab_in[i * aw:(i + 1) * aw]) for i, dp in enumerate(dparts)], tm=256, name="ab_in_dx")
    dh1, d_mix0 = _rmsnorm_bwd(h1, mix_norm[0:1], dhn0, dh2, name="l0_mix_norm_bwd")
    dh0, gw["pre_g", 0], gw["pre_u", 0], gw["pre_d", 0], d_pre0 = _ffn_bwd(
        dh1, s_pre0, ffn_pre_norm[0:1], *ffn_w["pre", 0], "l0pre")

    gw["ab_in", 0], gw["ab_out", 0] = g_ab_in, g_ab_out
    gconv_own = g_conv.reshape(3, 8, -1).transpose(1, 0, 2).reshape(8, -1)
    conv_piece = jnp.zeros((8, 16, d), F32).at[:, 0, :nconv].set(gconv_own).astype(BF16)
    parts_b = _chip_exchange(chip_partials("B", gw, [conv_piece]), name="gradB_chip_exchange")
    g_sum = {"A": _grad_sum(parts_a, name="gradA_sum"), "B": _grad_sum(parts_b, name="gradB_sum")}

    upd = {}
    for tag, w, m, v in big:
        nl = layer_rows(tag)
        w2, m2, v2 = (a.reshape(-1, d) for a in (w, m, v))
        g_layers, res = [], None
        for key in ("B", "A"):
            for t2, layer in grad_items[key]:
                if t2 != tag:
                    continue
                off = grad_offs[key][tag, layer]
                if tag in col_sharded:
                    g_nat = g_sum[key][off:off + nl].T.reshape(nl, d)
                    res = _adamw_shard(g_nat, 0, w2, m2, v2, layer * nl, nl, res, name=f"adamw_{tag}{layer}")
                else:
                    g_nat = g_sum[key][off:off + nl]
                    res = _adamw_shard(g_sum[key], off, w2, m2, v2, layer * nl, nl, res, name=f"adamw_{tag}{layer}")
                g_layers.append(g_nat)
        upd[tag] = [jnp.stack(g_layers).reshape(w.shape)] + [r.reshape(w.shape) for r in res]
    res = _adamw_shard(g_sum["B"], grad_conv_row, conv_rows(ab_conv_w, False), conv_rows(m_ab_conv_w, False),
                       conv_rows(v_ab_conv_w, False), 0, 16, None, name="adamw_conv")
    g_conv_rows = g_sum["B"][grad_conv_row:grad_conv_row + 16]
    upd["conv"] = [r[0, :nconv].reshape(ab_conv_w.shape) for r in [g_conv_rows] + list(res)]

    def small_pack(pre, mix, post, final, lbs, onorm):
        def slot(parts):
            out, r = jnp.zeros((SLOT, d), F32), 0
            for a in (parts if isinstance(parts, tuple) else (parts,)):
                out = out.at[r:r + a.shape[0], :a.shape[1]].set(a)
                r += a.shape[0]
            return out

        return jnp.concatenate([slot(pre), slot(mix), slot(post), slot(final.reshape(1, d)), slot(lbs), slot(onorm)], axis=0)

    d_on = d_onorm.reshape(-1, c_out_norm.shape[1]).sum(axis=0, keepdims=True)
    gsmall = small_pack((d_pre0, d_pre1), (d_mix0, d_mix1), (d_post0, d_post1), d_final, dlb, d_on)
    gsmall_all = _all_gather(gsmall, name="gather_small_grads")
    sres = _small_update(
        gsmall_all,
        small_pack(ffn_pre_norm, mix_norm, ffn_post_norm, final_norm, c_lower_bounds, c_out_norm),
        small_pack(m_ffn_pre_norm, m_mix_norm, m_ffn_post_norm, m_final_norm, m_c_lower_bounds, m_c_out_norm),
        small_pack(v_ffn_pre_norm, v_mix_norm, v_ffn_post_norm, v_final_norm, v_c_lower_bounds, v_c_out_norm),
        name="small_update")

    def small_out(r):
        return {"pre_norm": r[0:2], "mix_norm": r[SLOT:SLOT + 2], "post_norm": r[2 * SLOT:2 * SLOT + 2],
                "final": r[3 * SLOT], "lb": r[ROW_LB:ROW_LB + 2], "onorm": r[5 * SLOT:5 * SLOT + 1, :c_out_norm.shape[1]]}

    small = [small_out(r) for r in sres]
    outs = []
    for k in range(4):
        s = small[k]
        outs += [s["pre_norm"], upd["pre_g"][k], upd["pre_u"][k], upd["pre_d"][k], s["mix_norm"], s["post_norm"],
                 upd["post_g"][k], upd["post_u"][k], upd["post_d"][k], upd["ab_in"][k], upd["conv"][k],
                 upd["ab_out"][k], upd["c_in"][k], s["lb"], s["onorm"], upd["c_out"][k], s["final"]]
    loss = lax.psum(loss_vec[0, 0], ("x", "y", "c"))
    return (loss, dh0[None], *outs)
```

```python
import functools
import math

import jax
import jax.numpy as jnp
from jax import lax
from jax.experimental import pallas as pl
from jax.experimental.pallas import tpu as pltpu

F32 = jnp.float32
BF16 = jnp.bfloat16
MESH = pl.DeviceIdType.MESH

RMS_EPS = 1e-6
MACARON = 0.5
LANES = 128
CHUNK = 64
N_LEVELS = 6
SB_KEYS = 256
ADAM_LR, ADAM_B1, ADAM_B2, ADAM_EPS, ADAM_WD, ADAM_STEP = 0.001, 0.9, 0.999, 1e-08, 0.01, 10
VMEM_LIMIT = 48 * 1024 * 1024


def _cp(**kw):
    return pltpu.CompilerParams(vmem_limit_bytes=VMEM_LIMIT, **kw)


def _sigmoid(x):
    return 1.0 / (1.0 + jnp.exp(-x))


def _bf(x):
    return x if x.dtype == BF16 else x.astype(BF16)


def _split3(x):
    hi = x.astype(BF16)
    r1 = x - hi.astype(F32)
    mid = r1.astype(BF16)
    lo = (r1 - mid.astype(F32)).astype(BF16)
    return hi, mid, lo


def _dot(a, b, ca=1, cb=0):
    return lax.dot_general(a, b, (((ca,), (cb,)), ((), ())), preferred_element_type=F32)


def _dot_exact_lhs(m, x):
    hi, mid, lo = _split3(x)
    return _dot(m, hi) + _dot(m, mid) + _dot(m, lo)


def _dot_exact_rhs(x, m):
    hi, mid, lo = _split3(x)
    return _dot(hi, m) + _dot(mid, m) + _dot(lo, m)


def _mm(terms, *, name, ta=False, tb=False, out_dtype=F32, residual=None, alpha=1.0, tm=512, tn=512):
    nt = len(terms)
    a0, b0 = terms[0]
    m = a0.shape[1] if ta else a0.shape[0]
    n = b0.shape[0] if tb else b0.shape[1]
    tm, tn = min(tm, m), min(tn, n)
    assert m % tm == 0 and n % tn == 0, (name, m, n, tm, tn)
    has_res = residual is not None

    def body(*refs):
        o_ref = refs[-1]
        acc = None
        for i in range(nt):
            a = _bf(refs[2 * i][...])
            b = _bf(refs[2 * i + 1][...])
            p = _dot(a, b, 0 if ta else 1, 1 if tb else 0)
            acc = p if acc is None else acc + p
        if alpha != 1.0:
            acc = acc * alpha
        if has_res:
            acc = acc + refs[2 * nt][...]
        o_ref[...] = acc.astype(out_dtype)

    in_specs, args = [], []
    for a, b in terms:
        k = a.shape[0] if ta else a.shape[1]
        assert (b.shape[1] if tb else b.shape[0]) == k, (name, a.shape, b.shape)
        in_specs.append(pl.BlockSpec((k, tm), lambda i, j: (0, i)) if ta else pl.BlockSpec((tm, k), lambda i, j: (i, 0)))
        in_specs.append(pl.BlockSpec((tn, k), lambda i, j: (j, 0)) if tb else pl.BlockSpec((k, tn), lambda i, j: (0, j)))
        args += [a, b]
    if has_res:
        in_specs.append(pl.BlockSpec((tm, tn), lambda i, j: (i, j)))
        args.append(residual)
    return pl.pallas_call(
        body, name=name, grid=(m // tm, n // tn), in_specs=in_specs,
        out_specs=pl.BlockSpec((tm, tn), lambda i, j: (i, j)),
        out_shape=jax.ShapeDtypeStruct((m, n), out_dtype), compiler_params=_cp())(*args)


def _rmsnorm_fwd(x, gain, *, name, tm=512):
    t, d = x.shape
    tm = min(tm, t)

    def body(x_ref, g_ref, o_ref):
        xv = x_ref[...]
        rstd = lax.rsqrt(jnp.mean(xv * xv, axis=-1, keepdims=True) + RMS_EPS)
        o_ref[...] = (xv * rstd * g_ref[...]).astype(BF16)

    return pl.pallas_call(
        body, name=name, grid=(t // tm,),
        in_specs=[pl.BlockSpec((tm, d), lambda i: (i, 0)), pl.BlockSpec((1, d), lambda i: (0, 0))],
        out_specs=pl.BlockSpec((tm, d), lambda i: (i, 0)),
        out_shape=jax.ShapeDtypeStruct((t, d), BF16), compiler_params=_cp())(x, gain)


def _rmsnorm_bwd(x, gain, dxn, dres, *, name, tm=512):
    t, d = x.shape
    tm = min(tm, t)

    def body(x_ref, g_ref, dxn_ref, dres_ref, dx_ref, dg_ref):
        xv = x_ref[...]
        rstd = lax.rsqrt(jnp.mean(xv * xv, axis=-1, keepdims=True) + RMS_EPS)
        xhat = xv * rstd
        dxn_v = dxn_ref[...]
        dxhat = dxn_v * g_ref[...]
        dx = rstd * (dxhat - xhat * jnp.mean(dxhat * xhat, axis=-1, keepdims=True))
        dx_ref[...] = dres_ref[...] + dx

        @pl.when(pl.program_id(0) == 0)
        def _():
            dg_ref[...] = jnp.zeros_like(dg_ref)

        dg_ref[...] += jnp.sum(dxn_v * xhat, axis=0, keepdims=True)

    row = pl.BlockSpec((tm, d), lambda i: (i, 0))
    vec = pl.BlockSpec((1, d), lambda i: (0, 0))
    return pl.pallas_call(
        body, name=name, grid=(t // tm,), in_specs=[row, vec, row, row], out_specs=[row, vec],
        out_shape=[jax.ShapeDtypeStruct((t, d), F32), jax.ShapeDtypeStruct((1, d), F32)],
        compiler_params=_cp())(x, gain, dxn, dres)


def _loss_head(h, gain, target, *, name, tm=512):
    t, d = h.shape
    tm = min(tm, t)

    def body(h_ref, g_ref, t_ref, dh_ref, dg_ref, loss_ref):
        hv = h_ref[...]
        rstd = lax.rsqrt(jnp.mean(hv * hv, axis=-1, keepdims=True) + RMS_EPS)
        xhat = hv * rstd
        err = xhat * g_ref[...] - t_ref[...]
        dy = err * (1.0 / d)
        dxhat = dy * g_ref[...]
        dh_ref[...] = rstd * (dxhat - xhat * jnp.mean(dxhat * xhat, axis=-1, keepdims=True))

        @pl.when(pl.program_id(0) == 0)
        def _():
            dg_ref[...] = jnp.zeros_like(dg_ref)
            loss_ref[...] = jnp.zeros_like(loss_ref)

        dg_ref[...] += jnp.sum(dy * xhat, axis=0, keepdims=True)
        part = jnp.sum(jnp.sum(err * err, axis=-1, keepdims=True), axis=0, keepdims=True) * (0.5 / d)
        loss_ref[...] += jnp.broadcast_to(part, loss_ref.shape)

    row = pl.BlockSpec((tm, d), lambda i: (i, 0))
    vec = pl.BlockSpec((1, d), lambda i: (0, 0))
    return pl.pallas_call(
        body, name=name, grid=(t // tm,), in_specs=[row, vec, row],
        out_specs=[row, vec, pl.BlockSpec((1, LANES), lambda i: (0, 0))],
        out_shape=[jax.ShapeDtypeStruct((t, d), F32), jax.ShapeDtypeStruct((1, d), F32),
                   jax.ShapeDtypeStruct((1, LANES), F32)],
        compiler_params=_cp())(h, gain, target)


def _norm_gate_up(x, gain, wg, wu, *, name, tm=512, tf=1408):
    t, d = x.shape
    f = wg.shape[0]
    tm, tf = min(tm, t), min(tf, f)
    assert f % tf == 0

    def body(x_ref, g_ref, wg_ref, wu_ref, xn_ref, gg_ref, uu_ref, act_ref):
        @pl.when(pl.program_id(1) == 0)
        def _():
            xv = x_ref[...]
            rstd = lax.rsqrt(jnp.mean(xv * xv, axis=-1, keepdims=True) + RMS_EPS)
            xn_ref[...] = (xv * rstd * g_ref[...]).astype(BF16)

        xn = xn_ref[...]
        gv = _dot(xn, wg_ref[...], 1, 1)
        uv = _dot(xn, wu_ref[...], 1, 1)
        gg_ref[...] = gv.astype(BF16)
        uu_ref[...] = uv.astype(BF16)
        act_ref[...] = (gv * _sigmoid(gv) * uv).astype(BF16)

    row = pl.BlockSpec((tm, d), lambda i, j: (i, 0))
    wsp = pl.BlockSpec((tf, d), lambda i, j: (j, 0))
    osp = pl.BlockSpec((tm, tf), lambda i, j: (i, j))
    return pl.pallas_call(
        body, name=name, grid=(t // tm, f // tf),
        in_specs=[row, pl.BlockSpec((1, d), lambda i, j: (0, 0)), wsp, wsp],
        out_specs=[row, osp, osp, osp],
        out_shape=[jax.ShapeDtypeStruct((t, d), BF16)] + [jax.ShapeDtypeStruct((t, f), BF16)] * 3,
        compiler_params=_cp())(x, gain, wg, wu)


def _swiglu_bwd(dout, wd, gg, uu, *, name, tm=512, tf=1408):
    t, d = dout.shape
    f = wd.shape[0]
    tm, tf = min(tm, t), min(tf, f)

    def body(do_ref, wd_ref, g_ref, u_ref, dg_ref, du_ref):
        dact = _dot((do_ref[...] * MACARON).astype(BF16), wd_ref[...], 1, 1)
        gv = g_ref[...].astype(F32)
        uv = u_ref[...].astype(F32)
        sg = _sigmoid(gv)
        dg_ref[...] = (dact * uv * (sg * (1.0 + gv * (1.0 - sg)))).astype(BF16)
        du_ref[...] = (dact * (gv * sg)).astype(BF16)

    osp = pl.BlockSpec((tm, tf), lambda i, j: (i, j))
    return pl.pallas_call(
        body, name=name, grid=(t // tm, f // tf),
        in_specs=[pl.BlockSpec((tm, d), lambda i, j: (i, 0)), pl.BlockSpec((tf, d), lambda i, j: (j, 0)), osp, osp],
        out_specs=[osp, osp], out_shape=[jax.ShapeDtypeStruct((t, f), BF16)] * 2,
        compiler_params=_cp())(dout, wd, gg, uu)


def _shift_down(x, n):
    rows = lax.broadcasted_iota(jnp.int32, x.shape, 0)
    return jnp.where(rows >= n, pltpu.roll(x, n, 0), 0.0)


def _shift_up(x, n):
    t = x.shape[0]
    rows = lax.broadcasted_iota(jnp.int32, x.shape, 0)
    return jnp.where(rows < t - n, pltpu.roll(x, t - n, 0), 0.0)


def _conv_fwd(pa, conv_w, *, name):
    t = pa.shape[0]
    nb = pa.shape[1] // 3 // LANES

    def body(b_ref, c_ref, x_ref, w_ref, y_ref):
        u = c_ref[...] * x_ref[...]
        w = w_ref[...]
        conv = w[2:3, :] * u + w[1:2, :] * _shift_down(u, 1) + w[0:1, :] * _shift_down(u, 2)
        y_ref[...] = (b_ref[...] * conv).astype(BF16)

    def col(off):
        return pl.BlockSpec((t, LANES), lambda j: (0, off + j))

    return pl.pallas_call(
        body, name=name, grid=(nb,),
        in_specs=[col(0), col(nb), col(2 * nb), pl.BlockSpec((3, LANES), lambda j: (0, j))],
        out_specs=pl.BlockSpec((t, LANES), lambda j: (0, j)),
        out_shape=jax.ShapeDtypeStruct((t, nb * LANES), BF16), compiler_params=_cp())(pa, pa, pa, conv_w)


def _conv_bwd(pa, dy, conv_w, *, name):
    t = pa.shape[0]
    nb = pa.shape[1] // 3 // LANES

    def body(b_ref, c_ref, x_ref, dy_ref, w_ref, db_ref, dc_ref, dx_ref, dw_ref):
        cv, xv = c_ref[...], x_ref[...]
        u = cv * xv
        u1, u2 = _shift_down(u, 1), _shift_down(u, 2)
        w = w_ref[...]
        conv = w[2:3, :] * u + w[1:2, :] * u1 + w[0:1, :] * u2
        dyv = dy_ref[...]
        db_ref[...] = (dyv * conv).astype(BF16)
        dconv = dyv * b_ref[...]
        du = w[2:3, :] * dconv + w[1:2, :] * _shift_up(dconv, 1) + w[0:1, :] * _shift_up(dconv, 2)
        dc_ref[...] = (du * xv).astype(BF16)
        dx_ref[...] = (du * cv).astype(BF16)
        dw_ref[0:1, :] = jnp.sum(dconv * u2, axis=0, keepdims=True)
        dw_ref[1:2, :] = jnp.sum(dconv * u1, axis=0, keepdims=True)
        dw_ref[2:3, :] = jnp.sum(dconv * u, axis=0, keepdims=True)

    def col(off):
        return pl.BlockSpec((t, LANES), lambda j: (0, off + j))

    osp = pl.BlockSpec((t, LANES), lambda j: (0, j))
    wsp = pl.BlockSpec((3, LANES), lambda j: (0, j))
    return pl.pallas_call(
        body, name=name, grid=(nb,), in_specs=[col(0), col(nb), col(2 * nb), col(0), wsp],
        out_specs=[osp, osp, osp, wsp],
        out_shape=[jax.ShapeDtypeStruct((t, nb * LANES), BF16)] * 3 + [jax.ShapeDtypeStruct((3, nb * LANES), F32)],
        compiler_params=_cp())(pa, pa, pa, dy, conv_w)


def _sb_consts():
    j = lax.broadcasted_iota(jnp.int32, (SB_KEYS, SB_KEYS), 0)
    s = lax.broadcasted_iota(jnp.int32, (SB_KEYS, SB_KEYS), 1)
    after = (j > s).astype(BF16)
    upto = (j <= s).astype(BF16)
    before = (j < s).astype(BF16)
    return after, jnp.stack([upto, before])


def _log_sigmoid(z):
    return jnp.minimum(z, 0.0) - jnp.log(1.0 + jnp.exp(-jnp.abs(z)))


def _attn_fwd(pb, late_pack, *, name, tq=256):
    t = pb.shape[0]
    npair = pb.shape[1] // 3 // LANES
    tq = min(tq, t)
    nq = t // tq
    cmat, _ = _sb_consts()
    scale = 1.0 / math.sqrt(LANES // 2)

    def body(q_ref, k_ref, v_ref, c_ref, late_ref, y_ref, lt_ref, gath_ref, *sems):
        i = pl.program_id(1)
        pair = pl.program_id(0)
        start, forward, finish = _gather_phases(late_ref, gath_ref, *sems)
        pl.when((pair == 0) & (i == 0))(start)
        pl.when((pair == npair - 1) & (i == nq // 2))(forward)
        lane = lax.broadcasted_iota(jnp.int32, (tq, LANES), 1)
        rowpos = i * tq + lax.broadcasted_iota(jnp.int32, (tq, SB_KEYS), 0)
        colid = lax.broadcasted_iota(jnp.int32, (tq, SB_KEYS), 1)
        q2 = q_ref[...] * jnp.asarray(scale, BF16)
        cm = c_ref[...]
        hi_lanes = lane >= LANES // 2
        qhs = [jnp.where(hi_lanes == (hh == 1), q2, jnp.zeros_like(q2)) for hh in range(2)]
        per_q = tq // SB_KEYS

        def blk(jb):
            return pl.ds(pl.multiple_of(jb * SB_KEYS, SB_KEYS), SB_KEYS)

        def scores(jb):
            kb = k_ref[blk(jb), :]
            return tuple(_dot(qhs[hh], kb, 1, 1) for hh in range(2))

        def weights(jb, zs, runs, masked):
            mask = (jb * SB_KEYS + colid) < rowpos if masked else None
            ws, new_runs = [], []
            for hh in range(2):
                lb = _log_sigmoid(zs[hh])
                lk = lb - zs[hh]
                if masked:
                    lk = jnp.where(mask, lk, 0.0)
                lk_hi, lk_lo = _split2(lk)
                cs = _dot(lk_hi, cm) + _dot(lk_lo, cm)
                w = jnp.exp(lb + runs[hh] + cs)
                if masked:
                    w = jnp.where(mask, w, 0.0)
                ws.append(w.astype(BF16))
                new_runs.append(runs[hh] + jnp.sum(lk, axis=1, keepdims=True))
            return tuple(ws), tuple(new_runs)

        def values(jb, accs, ws):
            vb = v_ref[blk(jb), :]
            return tuple(accs[hh] + _dot(ws[hh], vb) for hh in range(2))

        zero = jnp.zeros((tq, LANES), F32)
        zero_col = jnp.zeros((tq, 1), F32)
        nfull = i * per_q
        runs, accs, ws = (zero_col, zero_col), (zero, zero), None
        zs = scores(nfull + per_q - 1)
        for dblk in reversed(range(per_q)):
            jb = nfull + dblk
            zs_next = scores(jnp.maximum(jb - 1, 0))
            if ws is not None:
                accs = values(jb + 1, accs, ws)
            ws, runs = weights(jb, zs, runs, True)
            zs = zs_next

        def full_block(n, carry):
            zs, ws, runs, accs = carry
            jb = nfull - 1 - n
            zs_next = scores(jnp.maximum(jb - 1, 0))
            accs = values(jb + 1, accs, ws)
            ws, runs = weights(jb, zs, runs, False)
            return zs_next, ws, runs, accs

        _, ws, runs, accs = lax.fori_loop(0, nfull, full_block, (zs, ws, runs, accs))
        accs = values(0, accs, ws)
        y_ref[...] = jnp.where(hi_lanes, accs[1], accs[0]).astype(BF16)
        lt_ref[...] = jnp.where(hi_lanes, runs[1], runs[0])
        pl.when((pair == npair - 1) & (i == nq - 1))(finish)

    return pl.pallas_call(
        body, name=name, grid=(npair, nq),
        in_specs=[pl.BlockSpec((tq, LANES), lambda p, i: (i, p)),
                  pl.BlockSpec((t, LANES), lambda p, i: (0, npair + p)),
                  pl.BlockSpec((t, LANES), lambda p, i: (0, 2 * npair + p)),
                  pl.BlockSpec((SB_KEYS, SB_KEYS), lambda p, i: (0, 0)),
                  HBM_SPEC],
        out_specs=[pl.BlockSpec((tq, LANES), lambda p, i: (i, p))] * 2 + [HBM_SPEC],
        out_shape=[jax.ShapeDtypeStruct((t, npair * LANES), BF16), jax.ShapeDtypeStruct((t, npair * LANES), F32),
                   jax.ShapeDtypeStruct((8,) + late_pack.shape, late_pack.dtype)],
        scratch_shapes=_gather_scratch(),
        compiler_params=_cp(dimension_semantics=("arbitrary", "arbitrary")))(pb, pb, pb, cmat, late_pack)


def _attn_bwd(pb, dy, ltot, chip_part, *, name, tq=256):
    t = pb.shape[0]
    npair = pb.shape[1] // 3 // LANES
    tq = min(tq, t)
    nq = t // tq
    _, cmats = _sb_consts()
    scale = 1.0 / math.sqrt(LANES // 2)

    def body(q_ref, k_ref, v_ref, dy_ref, lt_ref, c_ref, part_ref, dq_ref, dk_ref, dv_ref, parts_ref, dk_acc, dv_acc, *sems):
        i = pl.program_id(1)
        pair = pl.program_id(0)
        start, finish = _chip_exchange_phases(part_ref, parts_ref, *sems)
        pl.when((pair == 0) & (i == 0))(start)

        @pl.when(i == 0)
        def _():
            dk_acc[...] = jnp.zeros_like(dk_acc)
            dv_acc[...] = jnp.zeros_like(dv_acc)

        lane = lax.broadcasted_iota(jnp.int32, (tq, LANES), 1)
        rowpos = i * tq + lax.broadcasted_iota(jnp.int32, (tq, SB_KEYS), 0)
        colid = lax.broadcasted_iota(jnp.int32, (tq, SB_KEYS), 1)
        q2 = q_ref[...] * jnp.asarray(scale, BF16)
        do2 = dy_ref[...].astype(BF16)
        ltv = lt_ref[...]
        c_upto, c_before = c_ref[0], c_ref[1]
        hi_lanes = lane >= LANES // 2
        sels = [hi_lanes == (hh == 1) for hh in range(2)]
        qhs = [jnp.where(s, q2, jnp.zeros_like(q2)) for s in sels]
        dohs = [jnp.where(s, do2, jnp.zeros_like(do2)) for s in sels]
        lts = [ltv[:, 0:1], ltv[:, LANES // 2:LANES // 2 + 1]]
        per_q = tq // SB_KEYS

        def blk(jb):
            return pl.ds(pl.multiple_of(jb * SB_KEYS, SB_KEYS), SB_KEYS)

        def scores(jb):
            kb, vb = k_ref[blk(jb), :], v_ref[blk(jb), :]
            return tuple((_dot(qhs[hh], kb, 1, 1), _dot(dohs[hh], vb, 1, 1)) for hh in range(2))

        def products(jb, dqs, pend):
            kb = k_ref[blk(jb), :]
            dk = _dot(pend[0][0], qhs[0], 0, 0) + _dot(pend[1][0], qhs[1], 0, 0)
            dv = _dot(pend[0][1], dohs[0], 0, 0) + _dot(pend[1][1], dohs[1], 0, 0)
            dk_acc[blk(jb), :] += dk
            dv_acc[blk(jb), :] += dv
            return tuple(dqs[hh] + _dot(pend[hh][0], kb) for hh in range(2))

        def chain(jb, zs, sums, masked):
            mask = (jb * SB_KEYS + colid) < rowpos if masked else None
            pend, new_sums = [], []
            for hh in range(2):
                z, da = zs[hh]
                csum, prun = sums[hh]
                lb = _log_sigmoid(z)
                lk = lb - z
                if masked:
                    lk = jnp.where(mask, lk, 0.0)
                lk_hi, lk_lo = _split2(lk)
                cs = _dot(lk_hi, c_upto) + _dot(lk_lo, c_upto)
                a = jnp.exp(lb + ((lts[hh] - csum) - cs))
                if masked:
                    a = jnp.where(mask, a, 0.0)
                e = a * da
                e_hi, e_lo = _split2(e)
                ce = _dot(e_hi, c_before) + _dot(e_lo, c_before)
                beta = jnp.exp(lb)
                dz = e * (1.0 - beta) - (prun + ce) * beta
                if masked:
                    dz = jnp.where(mask, dz, 0.0)
                pend.append((dz.astype(BF16), a.astype(BF16)))
                new_sums.append((csum + jnp.sum(lk, axis=1, keepdims=True), prun + jnp.sum(e, axis=1, keepdims=True)))
            return tuple(pend), tuple(new_sums)

        zero = jnp.zeros((tq, LANES), F32)
        zero_b = jnp.zeros((tq, SB_KEYS), BF16)
        nfull = i * per_q
        last = nfull + per_q - 1

        def full_block(jb, carry):
            zs, pend, sums, dqs = carry
            zs_next = scores(jb + 1)
            dqs = products(jnp.maximum(jb - 1, 0), dqs, pend)
            pend, sums = chain(jb, zs, sums, False)
            return zs_next, pend, sums, dqs

        zero_col = jnp.zeros((tq, 1), F32)
        carry = (scores(0), ((zero_b, zero_b),) * 2, ((zero_col, zero_col),) * 2, (zero, zero))
        zs, pend, sums, dqs = lax.fori_loop(0, nfull, full_block, carry)
        for dblk in range(per_q):
            jb = nfull + dblk
            zs_next = scores(jnp.minimum(jb + 1, last))
            dqs = products(jnp.maximum(jb - 1, 0), dqs, pend)
            pend, sums = chain(jb, zs, sums, True)
            zs = zs_next
        dqs = products(last, dqs, pend)
        dq_ref[...] = (jnp.where(hi_lanes, dqs[1], dqs[0]) * scale).astype(BF16)

        @pl.when(i == nq - 1)
        def _():
            dk_ref[...] = dk_acc[...].astype(BF16)
            dv_ref[...] = dv_acc[...].astype(BF16)

        pl.when((pair == npair - 1) & (i == nq - 1))(finish)

    blk = pl.BlockSpec((tq, LANES), lambda p, i: (i, p))
    full = pl.BlockSpec((t, LANES), lambda p, i: (0, p))
    return pl.pallas_call(
        body, name=name, grid=(npair, nq),
        in_specs=[blk,
                  pl.BlockSpec((t, LANES), lambda p, i: (0, npair + p)),
                  pl.BlockSpec((t, LANES), lambda p, i: (0, 2 * npair + p)),
                  pl.BlockSpec((tq, LANES), lambda p, i: (i, npair + p)),
                  blk,
                  pl.BlockSpec((2, SB_KEYS, SB_KEYS), lambda p, i: (0, 0, 0)),
                  HBM_SPEC],
        out_specs=[blk, full, full, HBM_SPEC],
        out_shape=[jax.ShapeDtypeStruct((t, npair * LANES), BF16)] * 3 + [jax.ShapeDtypeStruct(chip_part.shape, chip_part.dtype)],
        scratch_shapes=[pltpu.VMEM((t, LANES), F32), pltpu.VMEM((t, LANES), F32)] + _chip_exchange_scratch(),
        compiler_params=_cp(dimension_semantics=("arbitrary", "arbitrary")))(pb, pb, pb, dy, ltot, cmats, chip_part)


def _hgrn_consts():
    t = lax.broadcasted_iota(jnp.int32, (CHUNK, CHUNK), 0)
    s = lax.broadcasted_iota(jnp.int32, (CHUNK, CHUNK), 1)
    tri = (s <= t)
    cum = [tri.astype(F32)]
    masks = []
    for lvl in range(N_LEVELS):
        half = CHUNK >> (lvl + 1)
        ref_row = (t // (2 * half)) * (2 * half) + half - 1
        cum.append((s <= ref_row).astype(F32))
        same = (t // (2 * half)) == (s // (2 * half))
        masks.append((same & (t % (2 * half) >= half) & (s % (2 * half) < half)).astype(F32))
    masks.append((t == s).astype(F32))
    cum_all = jnp.concatenate(cum, axis=0).astype(BF16)
    suffix = (s >= t).astype(BF16)
    return cum_all, jnp.stack(masks), suffix


def _hgrn_gates(qr, fr, lbv):
    sg = _sigmoid(fr)
    fval = lbv + (1.0 - lbv) * sg
    kk = (1.0 - lbv) * _sigmoid(-fr)
    sq = _sigmoid(qr)
    return sg, fval, jnp.log(fval), kk, sq, qr * sq


def _lower_bound(c_ref):
    c = c_ref[...]
    mx = jnp.max(c, axis=0, keepdims=True)
    ex = jnp.exp(c - mx)
    return ex[1:2, :] / jnp.sum(ex, axis=0, keepdims=True)


def _hgrn_levels(ball, qs, kk):
    b = ball[:CHUNK]
    out = []
    for lvl in range(N_LEVELS):
        bref = ball[(lvl + 1) * CHUNK:(lvl + 2) * CHUNK]
        eq = jnp.exp(jnp.minimum(b - bref, 0.0))
        ek = jnp.exp(jnp.minimum(bref - b, 0.0))
        out.append((qs * eq, kk * ek, eq, ek))
    out.append((qs, kk, None, None))
    return out


def _split2(x):
    hi = x.astype(BF16)
    return hi, (x - hi.astype(F32)).astype(BF16)


def _hgrn_fwd(pc, c_lb, out_norm, *, name, tc=512):
    t = pc.shape[0]
    nh = pc.shape[1] // 4 // LANES
    tc = min(tc, t)
    nch = tc // CHUNK
    cum_all, masks, _ = _hgrn_consts()

    def body(q_ref, f_ref, i_ref, g_ref, lb_ref, on_ref, cum_ref, m_ref, y_ref, o_ref, st_ref, state):
        @pl.when(pl.program_id(1) == 0)
        def _():
            state[...] = jnp.zeros_like(state)

        lbv = _lower_bound(lb_ref)
        onv = on_ref[...]

        def chunk(c, carry):
            rows = pl.ds(pl.multiple_of(c * CHUNK, CHUNK), CHUNK)
            _, _, g, kk, _, qs = _hgrn_gates(q_ref[rows, :], f_ref[rows, :], lbv)
            vb = i_ref[rows, :].astype(BF16)
            ball = _dot_exact_lhs(cum_ref[...], g)
            b = ball[:CHUNK]
            scores = jnp.zeros((CHUNK, CHUNK), F32)
            for lvl, (ql, kl, _, _) in enumerate(_hgrn_levels(ball, qs, kk)):
                scores = scores + _dot(ql.astype(BF16), kl.astype(BF16), 1, 1) * m_ref[lvl]
            st = state[...]
            st_ref[c] = st
            o = _dot(scores.astype(BF16), vb) + _dot((qs * jnp.exp(b)).astype(BF16), st.astype(BF16), 1, 1)
            blast = b[CHUNK - 1:CHUNK, :]
            kdec = (kk * jnp.exp(blast - b)).astype(BF16)
            state[...] = st * jnp.exp(blast) + _dot(vb, kdec, 0, 0)
            o_ref[rows, :] = o
            rstd = lax.rsqrt(jnp.mean(o * o, axis=-1, keepdims=True) + RMS_EPS)
            gate = g_ref[rows, :]
            y_ref[rows, :] = (o * rstd * onv * (gate * _sigmoid(gate))).astype(BF16)
            return carry

        lax.fori_loop(0, nch, chunk, 0, unroll=2)

    def col(off):
        return pl.BlockSpec((tc, LANES), lambda h, i: (i, off + h))

    osp = pl.BlockSpec((tc, LANES), lambda h, i: (i, h))
    return pl.pallas_call(
        body, name=name, grid=(nh, t // tc),
        in_specs=[col(0), col(nh), col(2 * nh), col(3 * nh),
                  pl.BlockSpec((2, LANES), lambda h, i: (0, h)),
                  pl.BlockSpec((1, LANES), lambda h, i: (0, 0)),
                  pl.BlockSpec(cum_all.shape, lambda h, i: (0, 0)),
                  pl.BlockSpec(masks.shape, lambda h, i: (0, 0, 0))],
        out_specs=[osp, osp, pl.BlockSpec((None, nch, LANES, LANES), lambda h, i: (h, i, 0, 0))],
        out_shape=[jax.ShapeDtypeStruct((t, nh * LANES), BF16), jax.ShapeDtypeStruct((t, nh * LANES), F32),
                   jax.ShapeDtypeStruct((nh, t // CHUNK, LANES, LANES), F32)],
        scratch_shapes=[pltpu.VMEM((LANES, LANES), F32)],
        compiler_params=_cp())(pc, pc, pc, pc, c_lb, out_norm, cum_all, masks)


def _hgrn_bwd(pc, o_saved, states, dy, c_lb, out_norm, *, name, tc=512):
    t = pc.shape[0]
    nh = pc.shape[1] // 4 // LANES
    tc = min(tc, t)
    nch = tc // CHUNK
    nt = t // tc
    cum_all, masks, suffix = _hgrn_consts()

    def body(q_ref, f_ref, i_ref, g_ref, o_ref, st_ref, dy_ref, lb_ref, on_ref, cum_ref, m_ref, suf_ref,
             dq_ref, df_ref, di_ref, dg_ref, dlb_ref, don_ref, dstate):
        @pl.when(pl.program_id(1) == 0)
        def _():
            dstate[...] = jnp.zeros_like(dstate)
            dlb_ref[...] = jnp.zeros_like(dlb_ref)
            don_ref[...] = jnp.zeros_like(don_ref)

        lbv = _lower_bound(lb_ref)
        onv = on_ref[...]

        def chunk(n, carry):
            c = nch - 1 - n
            rows = pl.ds(pl.multiple_of(c * CHUNK, CHUNK), CHUNK)
            qr = q_ref[rows, :]
            sg, fval, g, kk, sq, qs = _hgrn_gates(qr, f_ref[rows, :], lbv)
            vb = i_ref[rows, :].astype(BF16)
            o = o_ref[rows, :]
            gate = g_ref[rows, :]
            sgt = _sigmoid(gate)
            rstd = lax.rsqrt(jnp.mean(o * o, axis=-1, keepdims=True) + RMS_EPS)
            ohat = o * rstd
            dyv = dy_ref[rows, :]
            don = dyv * (gate * sgt)
            dg_ref[rows, :] = (dyv * ohat * onv * (sgt * (1.0 + gate * (1.0 - sgt)))).astype(BF16)
            don_ref[...] += jnp.sum(don * ohat, axis=0, keepdims=True)
            dxhat = don * onv
            dob = (rstd * (dxhat - ohat * jnp.mean(dxhat * ohat, axis=-1, keepdims=True))).astype(BF16)
            ball = _dot_exact_lhs(cum_ref[...], g)
            b = ball[:CHUNK]
            blast = b[CHUNK - 1:CHUNK, :]
            eb = jnp.exp(b)
            edec = jnp.exp(blast - b)
            st32 = st_ref[c]
            st = st32.astype(BF16)
            dst = dstate[...]
            dstb = dst.astype(BF16)
            da = _dot(dob, vb, 1, 1)
            levels = _hgrn_levels(ball, qs, kk)
            scores = jnp.zeros((CHUNK, CHUNK), F32)
            dq = eb * _dot(dob, st)
            dk_inter = edec * _dot(vb, dstb)
            dk = dk_inter
            for lvl, (ql, kl, eq, ek) in enumerate(levels):
                mk = m_ref[lvl]
                (qh, qlo), (kh, klo) = _split2(ql), _split2(kl)
                scores = scores + _dot(qh, kh, 1, 1) * mk
                dal = (da * mk).astype(BF16)
                dql = _dot(dal, kh) + _dot(dal, klo)
                dkl = _dot(dal, qh, 0, 0) + _dot(dal, qlo, 0, 0)
                dq = dq + (dql if eq is None else dql * eq)
                dk = dk + (dkl if ek is None else dkl * ek)
            kdec = (kk * edec).astype(BF16)
            dv = _dot(scores.astype(BF16), dob, 0, 0) + _dot(kdec, dstb, 1, 1)
            dstate[...] = dst * jnp.exp(blast) + _dot(dob, (qs * eb).astype(BF16), 0, 0)
            db = qs * dq - kk * dk
            last = jnp.sum(kk * dk_inter, axis=0, keepdims=True) + jnp.exp(blast) * jnp.sum(dst * st32, axis=0, keepdims=True)
            dgl = _dot_exact_lhs(suf_ref[...], db) + last
            dfv = dgl / fval - dk
            df_ref[rows, :] = (dfv * (1.0 - lbv) * sg * (1.0 - sg)).astype(BF16)
            dlb_ref[...] += jnp.sum(dfv * (1.0 - sg), axis=0, keepdims=True)
            dq_ref[rows, :] = (dq * (sq * (1.0 + qr * (1.0 - sq)))).astype(BF16)
            di_ref[rows, :] = dv.astype(BF16)
            return carry

        lax.fori_loop(0, nch, chunk, 0, unroll=2)

    def col(off):
        return pl.BlockSpec((tc, LANES), lambda h, i: (nt - 1 - i, off + h))

    osp = pl.BlockSpec((tc, LANES), lambda h, i: (nt - 1 - i, h))
    vec = pl.BlockSpec((1, LANES), lambda h, i: (0, h))
    return pl.pallas_call(
        body, name=name, grid=(nh, nt),
        in_specs=[col(0), col(nh), col(2 * nh), col(3 * nh), osp,
                  pl.BlockSpec((None, nch, LANES, LANES), lambda h, i: (h, nt - 1 - i, 0, 0)),
                  osp,
                  pl.BlockSpec((2, LANES), lambda h, i: (0, h)),
                  pl.BlockSpec((1, LANES), lambda h, i: (0, 0)),
                  pl.BlockSpec(cum_all.shape, lambda h, i: (0, 0)),
                  pl.BlockSpec(masks.shape, lambda h, i: (0, 0, 0)),
                  pl.BlockSpec(suffix.shape, lambda h, i: (0, 0))],
        out_specs=[osp, osp, osp, osp, vec, vec],
        out_shape=[jax.ShapeDtypeStruct((t, nh * LANES), BF16)] * 4 + [jax.ShapeDtypeStruct((1, nh * LANES), F32)] * 2,
        scratch_shapes=[pltpu.VMEM((LANES, LANES), F32)],
        compiler_params=_cp())(pc, pc, pc, pc, o_saved, states, dy, c_lb, out_norm, cum_all, masks, suffix)


HBM_SPEC = pl.BlockSpec(memory_space=pltpu.HBM)


def _gather_scratch():
    return [pltpu.SemaphoreType.DMA((7,)), pltpu.SemaphoreType.DMA((7,)), pltpu.SemaphoreType.DMA]


def _gather_phases(x_ref, out_ref, send_sems, recv_sems, local_sem):
    x, y, c = lax.axis_index("x"), lax.axis_index("y"), lax.axis_index("c")
    me, sibling = (x, y, c), (x, y, 1 - c)
    chips = [(1 - x, y), (x, 1 - y), (1 - x, 1 - y)]

    def rows(px, py, pc):
        return out_ref.at[4 * px + 2 * py + pc]

    def copy(k, block, to, src=None):
        return pltpu.make_async_remote_copy(
            src_ref=rows(*block) if src is None else src, dst_ref=rows(*block),
            send_sem=send_sems.at[k], recv_sem=recv_sems.at[k], device_id=to, device_id_type=MESH)

    mine = pltpu.make_async_copy(x_ref, rows(*me), local_sem)
    first = [copy(0, me, sibling, src=x_ref)]
    first += [copy(1 + j, me, (*chip, c), src=x_ref) for j, chip in enumerate(chips)]
    passed = [copy(4 + j, (*chip, c), sibling) for j, chip in enumerate(chips)]

    def start():
        mine.start()
        for cp in first:
            cp.start()

    def forward():
        for j, chip in enumerate(chips):
            copy(1 + j, (*chip, c), me).wait_recv()
            passed[j].start()

    def finish():
        copy(0, sibling, me).wait_recv()
        for j, chip in enumerate(chips):
            copy(4 + j, (*chip, 1 - c), me).wait_recv()
        for cp in first + passed:
            cp.wait_send()
        mine.wait()

    return start, forward, finish


def _all_gather(xs, *, name):
    def body(x_ref, out_ref, *sems):
        start, forward, finish = _gather_phases(x_ref, out_ref, *sems)
        start()
        forward()
        finish()

    return pl.pallas_call(
        body, name=name, in_specs=[HBM_SPEC], out_specs=HBM_SPEC,
        out_shape=jax.ShapeDtypeStruct((8,) + xs.shape, xs.dtype), scratch_shapes=_gather_scratch())(xs)


def _sibling_exchange(s, *, name):
    def body(s_ref, rb_ref, send_sem, recv_sem):
        x, y, c = lax.axis_index("x"), lax.axis_index("y"), lax.axis_index("c")
        cp = pltpu.make_async_remote_copy(
            src_ref=s_ref.at[1 - c], dst_ref=rb_ref, send_sem=send_sem, recv_sem=recv_sem,
            device_id=(x, y, 1 - c), device_id_type=MESH)
        cp.start()
        cp.wait()

    return pl.pallas_call(
        body, name=name, in_specs=[HBM_SPEC], out_specs=HBM_SPEC,
        out_shape=jax.ShapeDtypeStruct(s.shape[1:], s.dtype),
        scratch_shapes=[pltpu.SemaphoreType.DMA, pltpu.SemaphoreType.DMA])(s)


def _row_tile(n, cap=1024):
    return max(b for b in range(16, cap + 1, 16) if n % b == 0)


def _pair_add(s, rb, core, *, name):
    _, n, c = s.shape
    tb = _row_tile(n)

    def body(core_ref, a_ref, b_ref, o_ref):
        o_ref[...] = (a_ref[...].astype(F32) + b_ref[...].astype(F32)).astype(BF16)

    return pl.pallas_call(
        body, name=name,
        grid_spec=pltpu.PrefetchScalarGridSpec(
            num_scalar_prefetch=1, grid=(n // tb,),
            in_specs=[pl.BlockSpec((None, tb, c), lambda i, cr: (cr[0], i, 0)),
                      pl.BlockSpec((tb, c), lambda i, cr: (i, 0))],
            out_specs=pl.BlockSpec((tb, c), lambda i, cr: (i, 0))),
        out_shape=jax.ShapeDtypeStruct((n, c), BF16), compiler_params=_cp())(core, s, rb)


def _chip_exchange(p, *, name):
    def body(p_ref, out_ref, *sems):
        start, finish = _chip_exchange_phases(p_ref, out_ref, *sems)
        start()
        finish()

    return pl.pallas_call(
        body, name=name, in_specs=[HBM_SPEC], out_specs=HBM_SPEC,
        out_shape=jax.ShapeDtypeStruct(p.shape, p.dtype), scratch_shapes=_chip_exchange_scratch())(p)


def _chip_exchange_scratch():
    return [pltpu.SemaphoreType.DMA((3,)), pltpu.SemaphoreType.DMA((3,)), pltpu.SemaphoreType.DMA]


def _chip_exchange_phases(p_ref, out_ref, send_sems, recv_sems, local_sem):
    x, y, c = lax.axis_index("x"), lax.axis_index("y"), lax.axis_index("c")
    mine = 2 * x + y
    own = pltpu.make_async_copy(p_ref.at[mine], out_ref.at[mine], local_sem)
    copies = [pltpu.make_async_remote_copy(
        src_ref=p_ref.at[2 * tx + ty], dst_ref=out_ref.at[mine],
        send_sem=send_sems.at[k], recv_sem=recv_sems.at[k], device_id=(tx, ty, c), device_id_type=MESH)
        for k, (tx, ty) in enumerate([(1 - x, y), (x, 1 - y), (1 - x, 1 - y)])]

    def start():
        own.start()
        for cp in copies:
            cp.start()

    def finish():
        for cp in copies:
            cp.wait()
        own.wait()

    return start, finish


def _adamw_math(w, g, m, v):
    m2 = ADAM_B1 * m + (1.0 - ADAM_B1) * g
    v2 = ADAM_B2 * v + (1.0 - ADAM_B2) * (g * g)
    m_hat = m2 / (1.0 - ADAM_B1 ** ADAM_STEP)
    v_hat = v2 / (1.0 - ADAM_B2 ** ADAM_STEP)
    return -ADAM_LR * (m_hat / (jnp.sqrt(v_hat) + ADAM_EPS) + ADAM_WD * w), m2, v2


def _grad_sum(parts, *, name):
    _, r, c = parts.shape
    tb = _row_tile(r)

    def body(p0, p1, p2, p3, g_out):
        g_out[...] = ((p0[...].astype(F32) + p1[...].astype(F32)) + p2[...].astype(F32)) + p3[...].astype(F32)

    def part(ch):
        return pl.BlockSpec((None, tb, c), lambda i: (ch, i, 0))

    return pl.pallas_call(
        body, name=name, grid=(r // tb,), in_specs=[part(0), part(1), part(2), part(3)],
        out_specs=pl.BlockSpec((tb, c), lambda i: (i, 0)), out_shape=jax.ShapeDtypeStruct((r, c), F32),
        compiler_params=_cp())(parts, parts, parts, parts)


def _adamw_shard(g, g_off, w, m, v, layer, prev, *, name):
    _, r, c = w.shape
    tb = next(b for b in range(min(r, 512), 0, -8) if r % b == 0 and g_off % b == 0)

    def body(g_ref, w_ref, m_ref, v_ref, *rest):
        d_out, m_out, v_out = rest[-3:]
        d, m2, v2 = _adamw_math(w_ref[...], g_ref[...], m_ref[...], v_ref[...])
        d_out[...] = d
        m_out[...] = m2
        v_out[...] = v2

    blk = pl.BlockSpec((None, tb, c), lambda i: (layer, i, 0))
    prev = list(prev) if prev is not None else []
    return pl.pallas_call(
        body, name=name, grid=(r // tb,),
        in_specs=[pl.BlockSpec((tb, c), lambda i: (g_off // tb + i, 0)), blk, blk, blk] + [pl.BlockSpec(memory_space=pl.ANY)] * len(prev),
        out_specs=[blk] * 3, out_shape=[jax.ShapeDtypeStruct(w.shape, F32)] * 3,
        input_output_aliases={4 + k: k for k in range(len(prev))},
        compiler_params=_cp())(g, w, m, v, *prev)


SLOT = 8
SMALL_ROWS = 6 * SLOT
ROW_LB = 4 * SLOT


def _small_update(gath, w, m, v, *, name):
    def body(g_ref, w_ref, m_ref, v_ref, g_out, d_out, m_out, v_out):
        tot = g_ref[0]
        for k in range(1, 8):
            tot = tot + g_ref[k]
        wv = w_ref[...]
        c0, c1 = wv[ROW_LB:ROW_LB + 1, :], wv[ROW_LB + 1:ROW_LB + 2, :]
        mx = jnp.maximum(c0, c1)
        e0, e1 = jnp.exp(c0 - mx), jnp.exp(c1 - mx)
        lb = e1 / (e0 + e1)
        gl = tot[ROW_LB:ROW_LB + 1, :] * lb * (1.0 - lb)
        row = lax.broadcasted_iota(jnp.int32, tot.shape, 0)
        g = jnp.where(row == ROW_LB, -gl, jnp.where(row == ROW_LB + 1, gl, tot))
        d, m2, v2 = _adamw_math(wv, g, m_ref[...], v_ref[...])
        g_out[...] = g
        d_out[...] = d
        m_out[...] = m2
        v_out[...] = v2

    return pl.pallas_call(
        body, name=name, out_shape=[jax.ShapeDtypeStruct(w.shape, F32)] * 4, compiler_params=_cp())(gath, w, m, v)


D_MODEL = 1024


def _ffn_fwd(h, gain, wg, wu, wd, tag):
    xn, gg, uu, act = _norm_gate_up(h, gain, wg, wu, name=f"{tag}_gate_up")
    out = _mm([(act, wd)], residual=h, alpha=MACARON, tn=1024, name=f"{tag}_down")
    return out, (h, xn, gg, uu, act)


def _ffn_bwd(dout, saved, gain, wg, wu, wd, tag):
    h, xn, gg, uu, act = saved
    dg, du = _swiglu_bwd(dout, wd, gg, uu, name=f"{tag}_dact")
    dwd = _mm([(act, dout)], ta=True, alpha=MACARON, tm=256, tn=512, out_dtype=BF16, name=f"{tag}_dwd")
    dwg = _mm([(dg, xn)], ta=True, tm=256, tn=512, out_dtype=BF16, name=f"{tag}_dwg")
    dwu = _mm([(du, xn)], ta=True, tm=256, tn=512, out_dtype=BF16, name=f"{tag}_dwu")
    dxn = _mm([(dg, wg), (du, wu)], tm=256, tn=512, name=f"{tag}_dxn")
    dh, dgain = _rmsnorm_bwd(h, gain, dxn, dout, name=f"{tag}_norm_bwd")
    return dh, dwg, dwu, dwd, dgain


def kernel(x, ffn_pre_norm, ffn_pre_w_gate, ffn_pre_w_up, ffn_pre_w_down, mix_norm, ffn_post_norm, ffn_post_w_gate, ffn_post_w_up, ffn_post_w_down, ab_w_in, ab_conv_w, ab_w_out, c_w_in, c_lower_bounds, c_out_norm, c_w_out, final_norm, loss_target, m_ffn_pre_norm, m_ffn_pre_w_gate, m_ffn_pre_w_up, m_ffn_pre_w_down, m_mix_norm, m_ffn_post_norm, m_ffn_post_w_gate, m_ffn_post_w_up, m_ffn_post_w_down, m_ab_w_in, m_ab_conv_w, m_ab_w_out, m_c_w_in, m_c_lower_bounds, m_c_out_norm, m_c_w_out, m_final_norm, v_ffn_pre_norm, v_ffn_pre_w_gate, v_ffn_pre_w_up, v_ffn_pre_w_down, v_mix_norm, v_ffn_post_norm, v_ffn_post_w_gate, v_ffn_post_w_up, v_ffn_post_w_down, v_ab_w_in, v_ab_conv_w, v_ab_w_out, v_c_w_in, v_c_lower_bounds, v_c_out_norm, v_c_w_out, v_final_norm):
    d = D_MODEL
    h0 = x[0]
    target = loss_target[0]
    core = lax.axis_index("c").astype(jnp.int32).reshape(1)

    big = [("pre_g", ffn_pre_w_gate, m_ffn_pre_w_gate, v_ffn_pre_w_gate),
           ("pre_u", ffn_pre_w_up, m_ffn_pre_w_up, v_ffn_pre_w_up),
           ("pre_d", ffn_pre_w_down, m_ffn_pre_w_down, v_ffn_pre_w_down),
           ("post_g", ffn_post_w_gate, m_ffn_post_w_gate, v_ffn_post_w_gate),
           ("post_u", ffn_post_w_up, m_ffn_post_w_up, v_ffn_post_w_up),
           ("post_d", ffn_post_w_down, m_ffn_post_w_down, v_ffn_post_w_down),
           ("ab_in", ab_w_in, m_ab_w_in, v_ab_w_in),
           ("ab_out", ab_w_out, m_ab_w_out, v_ab_w_out),
           ("c_in", c_w_in, m_c_w_in, v_c_w_in),
           ("c_out", c_w_out, m_c_w_out, v_c_w_out)]
    by_tag = {tag: (w, m, v) for tag, w, m, v in big}

    def layer_rows(tag):
        w = by_tag[tag][0]
        return w.size // d // w.shape[0]

    def layout(items):
        offs, off = {}, 0
        for item in items:
            offs[item] = off
            off += layer_rows(item[0])
        return offs, off

    ffn = [f"{pos}_{kind}" for pos in ("pre", "post") for kind in "gud"]
    early_items = [("pre_g", 0), ("pre_u", 0), ("pre_d", 0), ("ab_in", 0)]
    late_items = ([("pre_g", 1), ("pre_u", 1), ("pre_d", 1)] + [(f"post_{kind}", l) for l in (0, 1) for kind in "gud"]
                  + [("ab_out", 0), ("c_in", 0), ("c_out", 0)])
    early_offs, early_conv_row = layout(early_items)
    late_offs, _ = layout(late_items)
    grad_items = {"A": ([(tag, 1) for tag in ffn] + [(f"post_{kind}", 0) for kind in "gud"]
                        + [("c_in", 0), ("c_out", 0), ("ab_out", 0)]),
                  "B": [(f"pre_{kind}", 0) for kind in "gud"] + [("ab_in", 0)]}
    grad_offs = {k: layout(items)[0] for k, items in grad_items.items()}
    grad_conv_row = layout(grad_items["B"])[1]

    def conv_rows(a, split):
        flat = a.reshape(-1)
        if split:
            hi = flat.astype(BF16)
            flat = jnp.concatenate([hi, (flat - hi.astype(F32)).astype(BF16)])
        return jnp.zeros((16, d), flat.dtype).at[0, :flat.shape[0]].set(flat)

    nconv = ab_conv_w.size
    col_sharded = {"pre_g", "pre_u", "post_g", "post_u", "ab_in", "c_in"}

    def pack_rows(item):
        tag, layer = item
        a = by_tag[tag][0][layer]
        return (a.T if tag in col_sharded else a).reshape(-1, d).astype(BF16)

    early_pack = jnp.concatenate([pack_rows(item) for item in early_items] + [conv_rows(ab_conv_w, True)], axis=0)
    late_pack = jnp.concatenate([pack_rows(item) for item in late_items], axis=0)
    gath_early = _all_gather(early_pack, name="gather_early_weights")

    def full_w(gath, offs, tag, layer, nrow):
        o = offs[tag, layer]
        return gath[:, o:o + nrow, :].reshape(8 * nrow, d)

    f_loc = ffn_pre_w_gate.shape[2]
    ffn_w = {("pre", 0): tuple(full_w(gath_early, early_offs, f"pre_{kind}", 0, f_loc) for kind in "gud")}
    w_ab_in = full_w(gath_early, early_offs, "ab_in", 0, ab_w_in.shape[2])
    cg = gath_early[:, early_conv_row, :2 * nconv].astype(F32)
    conv_w = (cg[:, :nconv] + cg[:, nconv:]).reshape(8, 3, -1).transpose(1, 0, 2).reshape(3, -1)
    half = w_ab_in.shape[0] // 2
    w_a_in, w_b_in = w_ab_in[:half], w_ab_in[half:]
    aw = half // 3

    h1, s_pre0 = _ffn_fwd(h0, ffn_pre_norm[0:1], *ffn_w["pre", 0], "l0pre")
    hn0 = _rmsnorm_fwd(h1, mix_norm[0:1], name="l0_mix_norm")
    pa = _mm([(hn0, w_a_in)], tb=True, tn=512, name="ab_proj_a")
    pb = _mm([(hn0, w_b_in)], tb=True, tn=512, out_dtype=BF16, name="ab_proj_b")
    ya = _conv_fwd(pa, conv_w, name="conv_fwd")
    yb, ltot, gath_late = _attn_fwd(pb, late_pack, name="attn_fwd_gather_late_weights")
    for pos, layer in (("post", 0), ("pre", 1), ("post", 1)):
        ffn_w[pos, layer] = tuple(full_w(gath_late, late_offs, f"{pos}_{kind}", layer, f_loc) for kind in "gud")
    w_ab_out = full_w(gath_late, late_offs, "ab_out", 0, ab_w_out.shape[1])
    w_c_in = full_w(gath_late, late_offs, "c_in", 0, c_w_in.shape[2])
    w_c_out = full_w(gath_late, late_offs, "c_out", 0, c_w_out.shape[1])
    h2 = _mm([(ya, w_ab_out[:aw]), (yb, w_ab_out[aw:])], residual=h1, tn=1024, name="ab_out")
    h3, s_post0 = _ffn_fwd(h2, ffn_post_norm[0:1], *ffn_w["post", 0], "l0post")
    h4, s_pre1 = _ffn_fwd(h3, ffn_pre_norm[1:2], *ffn_w["pre", 1], "l1pre")
    hn1 = _rmsnorm_fwd(h4, mix_norm[1:2], name="l1_mix_norm")
    pc = _mm([(hn1, w_c_in)], tb=True, tn=512, name="c_proj")
    yc, o_saved, states = _hgrn_fwd(pc, c_lower_bounds, c_out_norm, name="hgrn_fwd")
    h5 = _mm([(yc, w_c_out)], residual=h4, tn=1024, name="c_out")
    h6, s_post1 = _ffn_fwd(h5, ffn_post_norm[1:2], *ffn_w["post", 1], "l1post")
    dh6, d_final, loss_vec = _loss_head(h6, final_norm.reshape(1, d), target, name="loss_head")

    gw = {}
    dh5, gw["post_g", 1], gw["post_u", 1], gw["post_d", 1], d_post1 = _ffn_bwd(
        dh6, s_post1, ffn_post_norm[1:2], *ffn_w["post", 1], "l1post")
    dyc = _mm([(dh5, w_c_out)], tb=True, tn=512, name="c_out_dy")
    g_c_out = _mm([(yc, dh5)], ta=True, out_dtype=BF16, name="c_out_dw")
    dcq, dcf, dci, dcg, dlb, d_onorm = _hgrn_bwd(pc, o_saved, states, dyc, c_lower_bounds, c_out_norm, name="hgrn_bwd")
    dparts = [dcq, dcf, dci, dcg]
    g_c_in = jnp.concatenate(
        [_mm([(dp, hn1)], ta=True, out_dtype=BF16, name=f"c_in_dw{i}") for i, dp in enumerate(dparts)], axis=0)
    cw = w_c_in.shape[0] // 4
    dhn1 = _mm([(dp, w_c_in[i * cw:(i + 1) * cw]) for i, dp in enumerate(dparts)], tm=256, name="c_in_dx")
    dh4, d_mix1 = _rmsnorm_bwd(h4, mix_norm[1:2], dhn1, dh5, name="l1_mix_norm_bwd")
    dh3, gw["pre_g", 1], gw["pre_u", 1], gw["pre_d", 1], d_pre1 = _ffn_bwd(
        dh4, s_pre1, ffn_pre_norm[1:2], *ffn_w["pre", 1], "l1pre")
    dh2, gw["post_g", 0], gw["post_u", 0], gw["post_d", 0], d_post0 = _ffn_bwd(
        dh3, s_post0, ffn_post_norm[0:1], *ffn_w["post", 0], "l0post")
    dyab = _mm([(dh2, w_ab_out)], tb=True, tn=512, name="ab_out_dy")
    g_ab_out = jnp.concatenate([_mm([(ya, dh2)], ta=True, out_dtype=BF16, name="ab_out_dw_a"),
                                _mm([(yb, dh2)], ta=True, out_dtype=BF16, name="ab_out_dw_b")], axis=0)
    dab, dac, dax, g_conv = _conv_bwd(pa, dyab, conv_w, name="conv_bwd")

    def chip_partials(key, grads, extra=()):
        gpack = jnp.concatenate([grads[item].reshape(8, -1, d) for item in grad_items[key]] + list(extra), axis=1)
        rows = gpack.shape[1]
        send = gpack.reshape(4, 2, rows, d).transpose(1, 0, 2, 3)
        from_sibling = _sibling_exchange(send, name=f"grad{key}_sibling_exchange")
        return _pair_add(send.reshape(2, 4 * rows, d), from_sibling.reshape(4 * rows, d), core,
                         name=f"grad{key}_pair_add").reshape(4, rows, d)

    gw["c_in", 0], gw["c_out", 0], gw["ab_out", 0] = g_c_in, g_c_out, g_ab_out
    chip_part_a = chip_partials("A", gw)
    dq, dk, dv, parts_a = _attn_bwd(pb, dyab, ltot, chip_part_a, name="attn_bwd_exchange_grads_a")
    dparts = [dab, dac, dax, dq, dk, dv]
    g_ab_in = jnp.concatenate(
        [_mm([(dp, hn0)], ta=True, out_dtype=BF16, name=f"ab_in_dw{i}") for i, dp in enumerate(dparts)], axis=0)
    dhn0 = _mm([(dp, w_ab_in[i * aw:(i + 1) * aw]) for i, dp in enumerate(dparts)], tm=256, name="ab_in_dx")
    dh1, d_mix0 = _rmsnorm_bwd(h1, mix_norm[0:1], dhn0, dh2, name="l0_mix_norm_bwd")
    dh0, gw["pre_g", 0], gw["pre_u", 0], gw["pre_d", 0], d_pre0 = _ffn_bwd(
        dh1, s_pre0, ffn_pre_norm[0:1], *ffn_w["pre", 0], "l0pre")

    gw["ab_in", 0] = g_ab_in
    gconv_own = g_conv.reshape(3, 8, -1).transpose(1, 0, 2).reshape(8, -1)
    conv_piece = jnp.zeros((8, 16, d), F32).at[:, 0, :nconv].set(gconv_own).astype(BF16)
    parts_b = _chip_exchange(chip_partials("B", gw, [conv_piece]), name="gradB_chip_exchange")
    g_sum = {"A": _grad_sum(parts_a, name="gradA_sum"), "B": _grad_sum(parts_b, name="gradB_sum")}

    upd = {}
    for tag, w, m, v in big:
        nl = layer_rows(tag)
        g_layers = {}
        for key in ("A", "B"):
            for t2, layer in grad_items[key]:
                if t2 == tag:
                    off = grad_offs[key][tag, layer]
                    g_rows = g_sum[key][off:off + nl]
                    g_layers[layer] = (g_rows.T, 0) if tag in col_sharded else (g_sum[key], off)
                    g_layers[layer] += (g_rows.T if tag in col_sharded else g_rows,)
        res = None
        for layer in sorted(g_layers):
            g_arr, off, _ = g_layers[layer]
            res = _adamw_shard(g_arr, off, w, m, v, layer, res, name=f"adamw_{tag}{layer}")
        upd[tag] = [jnp.stack([g_layers[layer][2] for layer in sorted(g_layers)])] + list(res)
    res = _adamw_shard(g_sum["B"], grad_conv_row, *(conv_rows(a, False)[None] for a in (ab_conv_w, m_ab_conv_w, v_ab_conv_w)),
                       0, None, name="adamw_conv")
    g_conv_rows = g_sum["B"][grad_conv_row:grad_conv_row + 16]
    upd["conv"] = [r[0, :nconv].reshape(ab_conv_w.shape) for r in [g_conv_rows] + [r[0] for r in res]]

    def small_pack(pre, mix, post, final, lbs, onorm):
        def slot(parts):
            out, r = jnp.zeros((SLOT, d), F32), 0
            for a in (parts if isinstance(parts, tuple) else (parts,)):
                out = out.at[r:r + a.shape[0], :a.shape[1]].set(a)
                r += a.shape[0]
            return out

        return jnp.concatenate([slot(pre), slot(mix), slot(post), slot(final.reshape(1, d)), slot(lbs), slot(onorm)], axis=0)

    d_on = d_onorm.reshape(-1, c_out_norm.shape[1]).sum(axis=0, keepdims=True)
    gsmall = small_pack((d_pre0, d_pre1), (d_mix0, d_mix1), (d_post0, d_post1), d_final, dlb, d_on)
    gsmall_all = _all_gather(gsmall, name="gather_small_grads")
    sres = _small_update(
        gsmall_all,
        small_pack(ffn_pre_norm, mix_norm, ffn_post_norm, final_norm, c_lower_bounds, c_out_norm),
        small_pack(m_ffn_pre_norm, m_mix_norm, m_ffn_post_norm, m_final_norm, m_c_lower_bounds, m_c_out_norm),
        small_pack(v_ffn_pre_norm, v_mix_norm, v_ffn_post_norm, v_final_norm, v_c_lower_bounds, v_c_out_norm),
        name="small_update")

    def small_out(r):
        return {"pre_norm": r[0:2], "mix_norm": r[SLOT:SLOT + 2], "post_norm": r[2 * SLOT:2 * SLOT + 2],
                "final": r[3 * SLOT], "lb": r[ROW_LB:ROW_LB + 2], "onorm": r[5 * SLOT:5 * SLOT + 1, :c_out_norm.shape[1]]}

    small = [small_out(r) for r in sres]
    outs = []
    for k in range(4):
        s = small[k]
        outs += [s["pre_norm"], upd["pre_g"][k], upd["pre_u"][k], upd["pre_d"][k], s["mix_norm"], s["post_norm"],
                 upd["post_g"][k], upd["post_u"][k], upd["post_d"][k], upd["ab_in"][k], upd["conv"][k],
                 upd["ab_out"][k], upd["c_in"][k], s["lb"], s["onorm"], upd["c_out"][k], s["final"]]
    loss = lax.psum(loss_vec[0, 0], ("x", "y", "c"))
    return (loss, dh0[None], *outs)
```

```python
import functools
import math

import jax
import jax.numpy as jnp
from jax import lax
from jax.experimental import pallas as pl
from jax.experimental.pallas import tpu as pltpu

F32 = jnp.float32
BF16 = jnp.bfloat16
MESH = pl.DeviceIdType.MESH

RMS_EPS = 1e-6
MACARON = 0.5
LANES = 128
CHUNK = 64
N_LEVELS = 6
SB_KEYS = 256
ADAM_LR, ADAM_B1, ADAM_B2, ADAM_EPS, ADAM_WD, ADAM_STEP = 0.001, 0.9, 0.999, 1e-08, 0.01, 10
VMEM_LIMIT = 48 * 1024 * 1024


def _cp(**kw):
    return pltpu.CompilerParams(vmem_limit_bytes=VMEM_LIMIT, **kw)


def _sigmoid(x):
    return 1.0 / (1.0 + jnp.exp(-x))


def _bf(x):
    return x if x.dtype == BF16 else x.astype(BF16)


def _split3(x):
    hi = x.astype(BF16)
    r1 = x - hi.astype(F32)
    mid = r1.astype(BF16)
    lo = (r1 - mid.astype(F32)).astype(BF16)
    return hi, mid, lo


def _dot(a, b, ca=1, cb=0):
    return lax.dot_general(a, b, (((ca,), (cb,)), ((), ())), preferred_element_type=F32)


def _dot_exact_lhs(m, x):
    hi, mid, lo = _split3(x)
    return _dot(m, hi) + _dot(m, mid) + _dot(m, lo)


def _dot_exact_rhs(x, m):
    hi, mid, lo = _split3(x)
    return _dot(hi, m) + _dot(mid, m) + _dot(lo, m)


def _mm(terms, *, name, ta=False, tb=False, out_dtype=F32, residual=None, alpha=1.0, tm=512, tn=512):
    nt = len(terms)
    a0, b0 = terms[0]
    m = a0.shape[1] if ta else a0.shape[0]
    n = b0.shape[0] if tb else b0.shape[1]
    tm, tn = min(tm, m), min(tn, n)
    assert m % tm == 0 and n % tn == 0, (name, m, n, tm, tn)
    has_res = residual is not None

    def body(*refs):
        o_ref = refs[-1]
        acc = None
        for i in range(nt):
            a = _bf(refs[2 * i][...])
            b = _bf(refs[2 * i + 1][...])
            p = _dot(a, b, 0 if ta else 1, 1 if tb else 0)
            acc = p if acc is None else acc + p
        if alpha != 1.0:
            acc = acc * alpha
        if has_res:
            acc = acc + refs[2 * nt][...]
        o_ref[...] = acc.astype(out_dtype)

    in_specs, args = [], []
    for a, b in terms:
        k = a.shape[0] if ta else a.shape[1]
        assert (b.shape[1] if tb else b.shape[0]) == k, (name, a.shape, b.shape)
        in_specs.append(pl.BlockSpec((k, tm), lambda i, j: (0, i)) if ta else pl.BlockSpec((tm, k), lambda i, j: (i, 0)))
        in_specs.append(pl.BlockSpec((tn, k), lambda i, j: (j, 0)) if tb else pl.BlockSpec((k, tn), lambda i, j: (0, j)))
        args += [a, b]
    if has_res:
        in_specs.append(pl.BlockSpec((tm, tn), lambda i, j: (i, j)))
        args.append(residual)
    return pl.pallas_call(
        body, name=name, grid=(m // tm, n // tn), in_specs=in_specs,
        out_specs=pl.BlockSpec((tm, tn), lambda i, j: (i, j)),
        out_shape=jax.ShapeDtypeStruct((m, n), out_dtype), compiler_params=_cp())(*args)


def _rmsnorm_fwd(x, gain, *, name, tm=512):
    t, d = x.shape
    tm = min(tm, t)

    def body(x_ref, g_ref, o_ref):
        xv = x_ref[...]
        rstd = lax.rsqrt(jnp.mean(xv * xv, axis=-1, keepdims=True) + RMS_EPS)
        o_ref[...] = (xv * rstd * g_ref[...]).astype(BF16)

    return pl.pallas_call(
        body, name=name, grid=(t // tm,),
        in_specs=[pl.BlockSpec((tm, d), lambda i: (i, 0)), pl.BlockSpec((1, d), lambda i: (0, 0))],
        out_specs=pl.BlockSpec((tm, d), lambda i: (i, 0)),
        out_shape=jax.ShapeDtypeStruct((t, d), BF16), compiler_params=_cp())(x, gain)


def _rmsnorm_bwd(x, gain, dxn, dres, *, name, scale, tm=512):
    t, d = x.shape
    tm = min(tm, t)

    def body(x_ref, g_ref, dxn_ref, dres_ref, dx_ref, dxb_ref, dg_ref):
        xv = x_ref[...]
        rstd = lax.rsqrt(jnp.mean(xv * xv, axis=-1, keepdims=True) + RMS_EPS)
        xhat = xv * rstd
        dxn_v = dxn_ref[...]
        dxhat = dxn_v * g_ref[...]
        dx = dres_ref[...] + rstd * (dxhat - xhat * jnp.mean(dxhat * xhat, axis=-1, keepdims=True))
        dx_ref[...] = dx
        dxb_ref[...] = (dx * scale).astype(BF16)

        @pl.when(pl.program_id(0) == 0)
        def _():
            dg_ref[...] = jnp.zeros_like(dg_ref)

        dg_ref[...] += jnp.sum(dxn_v * xhat, axis=0, keepdims=True)

    row = pl.BlockSpec((tm, d), lambda i: (i, 0))
    vec = pl.BlockSpec((1, d), lambda i: (0, 0))
    return pl.pallas_call(
        body, name=name, grid=(t // tm,), in_specs=[row, vec, row, row], out_specs=[row, row, vec],
        out_shape=[jax.ShapeDtypeStruct((t, d), F32), jax.ShapeDtypeStruct((t, d), BF16), jax.ShapeDtypeStruct((1, d), F32)],
        compiler_params=_cp())(x, gain, dxn, dres)


def _loss_head(h, gain, target, *, name, tm=512):
    t, d = h.shape
    tm = min(tm, t)

    def body(h_ref, g_ref, t_ref, dh_ref, dhb_ref, dg_ref, loss_ref):
        hv = h_ref[...]
        rstd = lax.rsqrt(jnp.mean(hv * hv, axis=-1, keepdims=True) + RMS_EPS)
        xhat = hv * rstd
        err = xhat * g_ref[...] - t_ref[...]
        dy = err * (1.0 / d)
        dxhat = dy * g_ref[...]
        dh = rstd * (dxhat - xhat * jnp.mean(dxhat * xhat, axis=-1, keepdims=True))
        dh_ref[...] = dh
        dhb_ref[...] = (dh * MACARON).astype(BF16)

        @pl.when(pl.program_id(0) == 0)
        def _():
            dg_ref[...] = jnp.zeros_like(dg_ref)
            loss_ref[...] = jnp.zeros_like(loss_ref)

        dg_ref[...] += jnp.sum(dy * xhat, axis=0, keepdims=True)
        part = jnp.sum(jnp.sum(err * err, axis=-1, keepdims=True), axis=0, keepdims=True) * (0.5 / d)
        loss_ref[...] += jnp.broadcast_to(part, loss_ref.shape)

    row = pl.BlockSpec((tm, d), lambda i: (i, 0))
    vec = pl.BlockSpec((1, d), lambda i: (0, 0))
    return pl.pallas_call(
        body, name=name, grid=(t // tm,), in_specs=[row, vec, row],
        out_specs=[row, row, vec, pl.BlockSpec((1, LANES), lambda i: (0, 0))],
        out_shape=[jax.ShapeDtypeStruct((t, d), F32), jax.ShapeDtypeStruct((t, d), BF16), jax.ShapeDtypeStruct((1, d), F32),
                   jax.ShapeDtypeStruct((1, LANES), F32)],
        compiler_params=_cp())(h, gain, target)


def _norm_gate_up(x, gain, wg, wu, *, name, tm=512, tf=1408):
    t, d = x.shape
    f = wg.shape[0]
    tm, tf = min(tm, t), min(tf, f)
    assert f % tf == 0

    def body(x_ref, g_ref, wg_ref, wu_ref, xn_ref, gg_ref, uu_ref, act_ref):
        @pl.when(pl.program_id(1) == 0)
        def _():
            xv = x_ref[...]
            rstd = lax.rsqrt(jnp.mean(xv * xv, axis=-1, keepdims=True) + RMS_EPS)
            xn_ref[...] = (xv * rstd * g_ref[...]).astype(BF16)

        xn = xn_ref[...]
        gv = _dot(xn, wg_ref[...], 1, 1)
        uv = _dot(xn, wu_ref[...], 1, 1)
        gg_ref[...] = gv.astype(BF16)
        uu_ref[...] = uv.astype(BF16)
        act_ref[...] = (gv * _sigmoid(gv) * uv).astype(BF16)

    row = pl.BlockSpec((tm, d), lambda i, j: (i, 0))
    wsp = pl.BlockSpec((tf, d), lambda i, j: (j, 0))
    osp = pl.BlockSpec((tm, tf), lambda i, j: (i, j))
    return pl.pallas_call(
        body, name=name, grid=(t // tm, f // tf),
        in_specs=[row, pl.BlockSpec((1, d), lambda i, j: (0, 0)), wsp, wsp],
        out_specs=[row, osp, osp, osp],
        out_shape=[jax.ShapeDtypeStruct((t, d), BF16)] + [jax.ShapeDtypeStruct((t, f), BF16)] * 3,
        compiler_params=_cp())(x, gain, wg, wu)


def _swiglu_bwd(dout, wd, gg, uu, *, name, tm=512, tf=1408):
    t, d = dout.shape
    f = wd.shape[0]
    tm, tf = min(tm, t), min(tf, f)

    def body(do_ref, wd_ref, g_ref, u_ref, dg_ref, du_ref):
        dact = _dot(do_ref[...], wd_ref[...], 1, 1)
        gv = g_ref[...].astype(F32)
        uv = u_ref[...].astype(F32)
        sg = _sigmoid(gv)
        dg_ref[...] = (dact * uv * (sg * (1.0 + gv * (1.0 - sg)))).astype(BF16)
        du_ref[...] = (dact * (gv * sg)).astype(BF16)

    osp = pl.BlockSpec((tm, tf), lambda i, j: (i, j))
    return pl.pallas_call(
        body, name=name, grid=(t // tm, f // tf),
        in_specs=[pl.BlockSpec((tm, d), lambda i, j: (i, 0)), pl.BlockSpec((tf, d), lambda i, j: (j, 0)), osp, osp],
        out_specs=[osp, osp], out_shape=[jax.ShapeDtypeStruct((t, f), BF16)] * 2,
        compiler_params=_cp())(dout, wd, gg, uu)


def _shift_down(x, n):
    rows = lax.broadcasted_iota(jnp.int32, x.shape, 0)
    return jnp.where(rows >= n, pltpu.roll(x, n, 0), 0.0)


def _shift_up(x, n):
    t = x.shape[0]
    rows = lax.broadcasted_iota(jnp.int32, x.shape, 0)
    return jnp.where(rows < t - n, pltpu.roll(x, t - n, 0), 0.0)


def _conv_fwd(pa, conv_w, *, name):
    t = pa.shape[0]
    nb = pa.shape[1] // 3 // LANES

    def body(b_ref, c_ref, x_ref, w_ref, y_ref):
        u = c_ref[...] * x_ref[...]
        w = w_ref[...]
        conv = w[2:3, :] * u + w[1:2, :] * _shift_down(u, 1) + w[0:1, :] * _shift_down(u, 2)
        y_ref[...] = (b_ref[...] * conv).astype(BF16)

    def col(off):
        return pl.BlockSpec((t, LANES), lambda j: (0, off + j))

    return pl.pallas_call(
        body, name=name, grid=(nb,),
        in_specs=[col(0), col(nb), col(2 * nb), pl.BlockSpec((3, LANES), lambda j: (0, j))],
        out_specs=pl.BlockSpec((t, LANES), lambda j: (0, j)),
        out_shape=jax.ShapeDtypeStruct((t, nb * LANES), BF16), compiler_params=_cp())(pa, pa, pa, conv_w)


def _conv_bwd(pa, dy, conv_w, *, name):
    t = pa.shape[0]
    nb = pa.shape[1] // 3 // LANES

    def body(b_ref, c_ref, x_ref, dy_ref, w_ref, db_ref, dc_ref, dx_ref, dw_ref):
        cv, xv = c_ref[...], x_ref[...]
        u = cv * xv
        u1, u2 = _shift_down(u, 1), _shift_down(u, 2)
        w = w_ref[...]
        conv = w[2:3, :] * u + w[1:2, :] * u1 + w[0:1, :] * u2
        dyv = dy_ref[...]
        db_ref[...] = (dyv * conv).astype(BF16)
        dconv = dyv * b_ref[...]
        du = w[2:3, :] * dconv + w[1:2, :] * _shift_up(dconv, 1) + w[0:1, :] * _shift_up(dconv, 2)
        dc_ref[...] = (du * xv).astype(BF16)
        dx_ref[...] = (du * cv).astype(BF16)
        dw_ref[0:1, :] = jnp.sum(dconv * u2, axis=0, keepdims=True)
        dw_ref[1:2, :] = jnp.sum(dconv * u1, axis=0, keepdims=True)
        dw_ref[2:3, :] = jnp.sum(dconv * u, axis=0, keepdims=True)

    def col(off):
        return pl.BlockSpec((t, LANES), lambda j: (0, off + j))

    osp = pl.BlockSpec((t, LANES), lambda j: (0, j))
    wsp = pl.BlockSpec((3, LANES), lambda j: (0, j))
    return pl.pallas_call(
        body, name=name, grid=(nb,), in_specs=[col(0), col(nb), col(2 * nb), col(0), wsp],
        out_specs=[osp, osp, osp, wsp],
        out_shape=[jax.ShapeDtypeStruct((t, nb * LANES), BF16)] * 3 + [jax.ShapeDtypeStruct((3, nb * LANES), F32)],
        compiler_params=_cp())(pa, pa, pa, dy, conv_w)


def _sb_consts():
    j = lax.broadcasted_iota(jnp.int32, (SB_KEYS, SB_KEYS), 0)
    s = lax.broadcasted_iota(jnp.int32, (SB_KEYS, SB_KEYS), 1)
    after = (j > s).astype(BF16)
    upto = (j <= s).astype(BF16)
    before = (j < s).astype(BF16)
    return after, jnp.stack([upto, before])


def _log_sigmoid(z):
    return jnp.minimum(z, 0.0) - jnp.log(1.0 + jnp.exp(-jnp.abs(z)))


def _attn_fwd(pb, late_pack, seg_rows, *, name, tq=256):
    t = pb.shape[0]
    npair = pb.shape[1] // 3 // LANES
    tq = min(tq, t)
    nq = t // tq
    cmat, _ = _sb_consts()
    scale = 1.0 / math.sqrt(LANES // 2)

    nseg = len(seg_rows)

    def body(q_ref, k_ref, v_ref, c_ref, late_ref, y_ref, lt_ref, *rest):
        i = pl.program_id(1)
        pair = pl.program_id(0)
        start, forward, finish = _gather_phases(late_ref, rest[:nseg], seg_rows, *rest[nseg:])
        pl.when((pair == 0) & (i == 0))(start)
        pl.when((pair == npair - 1) & (i == nq // 2))(forward)
        lane = lax.broadcasted_iota(jnp.int32, (tq, LANES), 1)
        rowpos = i * tq + lax.broadcasted_iota(jnp.int32, (tq, SB_KEYS), 0)
        colid = lax.broadcasted_iota(jnp.int32, (tq, SB_KEYS), 1)
        q2 = q_ref[...] * jnp.asarray(scale, BF16)
        cm = c_ref[...]
        hi_lanes = lane >= LANES // 2
        qhs = [jnp.where(hi_lanes == (hh == 1), q2, jnp.zeros_like(q2)) for hh in range(2)]
        per_q = tq // SB_KEYS

        def blk(jb):
            return pl.ds(pl.multiple_of(jb * SB_KEYS, SB_KEYS), SB_KEYS)

        def scores(jb):
            kb = k_ref[blk(jb), :]
            return tuple(_dot(qhs[hh], kb, 1, 1) for hh in range(2))

        def weights(jb, zs, runs, masked):
            mask = (jb * SB_KEYS + colid) < rowpos if masked else None
            ws, new_runs = [], []
            for hh in range(2):
                lb = _log_sigmoid(zs[hh])
                lk = lb - zs[hh]
                if masked:
                    lk = jnp.where(mask, lk, 0.0)
                lk_hi, lk_lo = _split2(lk)
                cs = _dot(lk_hi, cm) + _dot(lk_lo, cm)
                w = jnp.exp(lb + runs[hh] + cs)
                if masked:
                    w = jnp.where(mask, w, 0.0)
                ws.append(w.astype(BF16))
                new_runs.append(runs[hh] + jnp.sum(lk, axis=1, keepdims=True))
            return tuple(ws), tuple(new_runs)

        def values(jb, accs, ws):
            vb = v_ref[blk(jb), :]
            return tuple(accs[hh] + _dot(ws[hh], vb) for hh in range(2))

        zero = jnp.zeros((tq, LANES), F32)
        zero_col = jnp.zeros((tq, 1), F32)
        nfull = i * per_q
        runs, accs, ws = (zero_col, zero_col), (zero, zero), None
        zs = scores(nfull + per_q - 1)
        for dblk in reversed(range(per_q)):
            jb = nfull + dblk
            zs_next = scores(jnp.maximum(jb - 1, 0))
            if ws is not None:
                accs = values(jb + 1, accs, ws)
            ws, runs = weights(jb, zs, runs, True)
            zs = zs_next

        def full_block(n, carry):
            zs, ws, runs, accs = carry
            jb = nfull - 1 - n
            zs_next = scores(jnp.maximum(jb - 1, 0))
            accs = values(jb + 1, accs, ws)
            ws, runs = weights(jb, zs, runs, False)
            return zs_next, ws, runs, accs

        _, ws, runs, accs = lax.fori_loop(0, nfull, full_block, (zs, ws, runs, accs))
        accs = values(0, accs, ws)
        y_ref[...] = jnp.where(hi_lanes, accs[1], accs[0]).astype(BF16)
        lt_ref[...] = jnp.where(hi_lanes, runs[1], runs[0])
        pl.when((pair == npair - 1) & (i == nq - 1))(finish)

    return pl.pallas_call(
        body, name=name, grid=(npair, nq),
        in_specs=[pl.BlockSpec((tq, LANES), lambda p, i: (i, p)),
                  pl.BlockSpec((t, LANES), lambda p, i: (0, npair + p)),
                  pl.BlockSpec((t, LANES), lambda p, i: (0, 2 * npair + p)),
                  pl.BlockSpec((SB_KEYS, SB_KEYS), lambda p, i: (0, 0)),
                  HBM_SPEC],
        out_specs=[pl.BlockSpec((tq, LANES), lambda p, i: (i, p))] * 2 + [HBM_SPEC] * nseg,
        out_shape=[jax.ShapeDtypeStruct((t, npair * LANES), BF16), jax.ShapeDtypeStruct((t, npair * LANES), F32),
                   ] + [jax.ShapeDtypeStruct((8, n, late_pack.shape[1]), late_pack.dtype) for n in seg_rows],
        scratch_shapes=_gather_scratch(),
        compiler_params=_cp(dimension_semantics=("arbitrary", "arbitrary")))(pb, pb, pb, cmat, late_pack)


def _attn_bwd(pb, dy, ltot, chip_part, *, name, tq=256):
    t = pb.shape[0]
    npair = pb.shape[1] // 3 // LANES
    tq = min(tq, t)
    nq = t // tq
    _, cmats = _sb_consts()
    scale = 1.0 / math.sqrt(LANES // 2)

    def body(q_ref, k_ref, v_ref, dy_ref, lt_ref, c_ref, part_ref, dq_ref, dk_ref, dv_ref, parts_ref, dk_acc, dv_acc, *sems):
        i = pl.program_id(1)
        pair = pl.program_id(0)
        start, finish = _chip_exchange_phases(part_ref, parts_ref, *sems)
        pl.when((pair == 0) & (i == 0))(start)

        @pl.when(i == 0)
        def _():
            dk_acc[...] = jnp.zeros_like(dk_acc)
            dv_acc[...] = jnp.zeros_like(dv_acc)

        lane = lax.broadcasted_iota(jnp.int32, (tq, LANES), 1)
        rowpos = i * tq + lax.broadcasted_iota(jnp.int32, (tq, SB_KEYS), 0)
        colid = lax.broadcasted_iota(jnp.int32, (tq, SB_KEYS), 1)
        q2 = q_ref[...] * jnp.asarray(scale, BF16)
        do2 = dy_ref[...].astype(BF16)
        ltv = lt_ref[...]
        c_upto, c_before = c_ref[0], c_ref[1]
        hi_lanes = lane >= LANES // 2
        sels = [hi_lanes == (hh == 1) for hh in range(2)]
        qhs = [jnp.where(s, q2, jnp.zeros_like(q2)) for s in sels]
        dohs = [jnp.where(s, do2, jnp.zeros_like(do2)) for s in sels]
        lts = [ltv[:, 0:1], ltv[:, LANES // 2:LANES // 2 + 1]]
        per_q = tq // SB_KEYS

        def blk(jb):
            return pl.ds(pl.multiple_of(jb * SB_KEYS, SB_KEYS), SB_KEYS)

        def scores(jb):
            kb, vb = k_ref[blk(jb), :], v_ref[blk(jb), :]
            return tuple((_dot(qhs[hh], kb, 1, 1), _dot(dohs[hh], vb, 1, 1)) for hh in range(2))

        def products(jb, dqs, pend):
            kb = k_ref[blk(jb), :]
            dk = _dot(pend[0][0], qhs[0], 0, 0) + _dot(pend[1][0], qhs[1], 0, 0)
            dv = _dot(pend[0][1], dohs[0], 0, 0) + _dot(pend[1][1], dohs[1], 0, 0)
            dk_acc[blk(jb), :] += dk
            dv_acc[blk(jb), :] += dv
            return tuple(dqs[hh] + _dot(pend[hh][0], kb) for hh in range(2))

        def chain(jb, zs, sums, masked):
            mask = (jb * SB_KEYS + colid) < rowpos if masked else None
            pend, new_sums = [], []
            for hh in range(2):
                z, da = zs[hh]
                csum, prun = sums[hh]
                lb = _log_sigmoid(z)
                lk = lb - z
                if masked:
                    lk = jnp.where(mask, lk, 0.0)
                lk_hi, lk_lo = _split2(lk)
                cs = _dot(lk_hi, c_upto) + _dot(lk_lo, c_upto)
                a = jnp.exp(lb + ((lts[hh] - csum) - cs))
                if masked:
                    a = jnp.where(mask, a, 0.0)
                e = a * da
                e_hi, e_lo = _split2(e)
                ce = _dot(e_hi, c_before) + _dot(e_lo, c_before)
                beta = jnp.exp(lb)
                dz = e * (1.0 - beta) - (prun + ce) * beta
                if masked:
                    dz = jnp.where(mask, dz, 0.0)
                pend.append((dz.astype(BF16), a.astype(BF16)))
                new_sums.append((csum + jnp.sum(lk, axis=1, keepdims=True), prun + jnp.sum(e, axis=1, keepdims=True)))
            return tuple(pend), tuple(new_sums)

        zero = jnp.zeros((tq, LANES), F32)
        zero_b = jnp.zeros((tq, SB_KEYS), BF16)
        nfull = i * per_q
        last = nfull + per_q - 1

        def full_block(jb, carry):
            zs, pend, sums, dqs = carry
            zs_next = scores(jb + 1)
            dqs = products(jnp.maximum(jb - 1, 0), dqs, pend)
            pend, sums = chain(jb, zs, sums, False)
            return zs_next, pend, sums, dqs

        zero_col = jnp.zeros((tq, 1), F32)
        carry = (scores(0), ((zero_b, zero_b),) * 2, ((zero_col, zero_col),) * 2, (zero, zero))
        zs, pend, sums, dqs = lax.fori_loop(0, nfull, full_block, carry)
        for dblk in range(per_q):
            jb = nfull + dblk
            zs_next = scores(jnp.minimum(jb + 1, last))
            dqs = products(jnp.maximum(jb - 1, 0), dqs, pend)
            pend, sums = chain(jb, zs, sums, True)
            zs = zs_next
        dqs = products(last, dqs, pend)
        dq_ref[...] = (jnp.where(hi_lanes, dqs[1], dqs[0]) * scale).astype(BF16)

        @pl.when(i == nq - 1)
        def _():
            dk_ref[...] = dk_acc[...].astype(BF16)
            dv_ref[...] = dv_acc[...].astype(BF16)

        pl.when((pair == npair - 1) & (i == nq - 1))(finish)

    blk = pl.BlockSpec((tq, LANES), lambda p, i: (i, p))
    full = pl.BlockSpec((t, LANES), lambda p, i: (0, p))
    return pl.pallas_call(
        body, name=name, grid=(npair, nq),
        in_specs=[blk,
                  pl.BlockSpec((t, LANES), lambda p, i: (0, npair + p)),
                  pl.BlockSpec((t, LANES), lambda p, i: (0, 2 * npair + p)),
                  pl.BlockSpec((tq, LANES), lambda p, i: (i, npair + p)),
                  blk,
                  pl.BlockSpec((2, SB_KEYS, SB_KEYS), lambda p, i: (0, 0, 0)),
                  HBM_SPEC],
        out_specs=[blk, full, full, HBM_SPEC],
        out_shape=[jax.ShapeDtypeStruct((t, npair * LANES), BF16)] * 3 + [jax.ShapeDtypeStruct(chip_part.shape, chip_part.dtype)],
        scratch_shapes=[pltpu.VMEM((t, LANES), F32), pltpu.VMEM((t, LANES), F32)] + _chip_exchange_scratch(),
        compiler_params=_cp(dimension_semantics=("arbitrary", "arbitrary")))(pb, pb, pb, dy, ltot, cmats, chip_part)


def _hgrn_consts():
    t = lax.broadcasted_iota(jnp.int32, (CHUNK, CHUNK), 0)
    s = lax.broadcasted_iota(jnp.int32, (CHUNK, CHUNK), 1)
    tri = (s <= t)
    cum = [tri.astype(F32)]
    masks = []
    for lvl in range(N_LEVELS):
        half = CHUNK >> (lvl + 1)
        ref_row = (t // (2 * half)) * (2 * half) + half - 1
        cum.append((s <= ref_row).astype(F32))
        same = (t // (2 * half)) == (s // (2 * half))
        masks.append((same & (t % (2 * half) >= half) & (s % (2 * half) < half)).astype(F32))
    masks.append((t == s).astype(F32))
    cum_all = jnp.concatenate(cum, axis=0).astype(BF16)
    suffix = (s >= t).astype(BF16)
    return cum_all, jnp.stack(masks), suffix


def _hgrn_gates(qr, fr, lbv):
    sg = _sigmoid(fr)
    fval = lbv + (1.0 - lbv) * sg
    kk = (1.0 - lbv) * _sigmoid(-fr)
    sq = _sigmoid(qr)
    return sg, fval, jnp.log(fval), kk, sq, qr * sq


def _lower_bound(c_ref):
    c = c_ref[...]
    mx = jnp.max(c, axis=0, keepdims=True)
    ex = jnp.exp(c - mx)
    return ex[1:2, :] / jnp.sum(ex, axis=0, keepdims=True)


def _hgrn_levels(ball, qs, kk):
    b = ball[:CHUNK]
    out = []
    for lvl in range(N_LEVELS):
        bref = ball[(lvl + 1) * CHUNK:(lvl + 2) * CHUNK]
        eq = jnp.exp(jnp.minimum(b - bref, 0.0))
        ek = jnp.exp(jnp.minimum(bref - b, 0.0))
        out.append((qs * eq, kk * ek, eq, ek))
    out.append((qs, kk, None, None))
    return out


def _split2(x):
    hi = x.astype(BF16)
    return hi, (x - hi.astype(F32)).astype(BF16)


def _hgrn_fwd(pc, c_lb, out_norm, *, name, tc=512):
    t = pc.shape[0]
    nh = pc.shape[1] // 4 // LANES
    tc = min(tc, t)
    nch = tc // CHUNK
    cum_all, masks, _ = _hgrn_consts()

    def body(q_ref, f_ref, i_ref, g_ref, lb_ref, on_ref, cum_ref, m_ref, y_ref, o_ref, st_ref, state):
        @pl.when(pl.program_id(1) == 0)
        def _():
            state[...] = jnp.zeros_like(state)

        lbv = _lower_bound(lb_ref)
        onv = on_ref[...]

        def chunk(c, carry):
            rows = pl.ds(pl.multiple_of(c * CHUNK, CHUNK), CHUNK)
            _, _, g, kk, _, qs = _hgrn_gates(q_ref[rows, :], f_ref[rows, :], lbv)
            vb = i_ref[rows, :].astype(BF16)
            ball = _dot_exact_lhs(cum_ref[...], g)
            b = ball[:CHUNK]
            scores = jnp.zeros((CHUNK, CHUNK), F32)
            for lvl, (ql, kl, _, _) in enumerate(_hgrn_levels(ball, qs, kk)):
                scores = scores + _dot(ql.astype(BF16), kl.astype(BF16), 1, 1) * m_ref[lvl]
            st = state[...]
            st_ref[c] = st
            o = _dot(scores.astype(BF16), vb) + _dot((qs * jnp.exp(b)).astype(BF16), st.astype(BF16), 1, 1)
            blast = b[CHUNK - 1:CHUNK, :]
            kdec = (kk * jnp.exp(blast - b)).astype(BF16)
            state[...] = st * jnp.exp(blast) + _dot(vb, kdec, 0, 0)
            o_ref[rows, :] = o
            rstd = lax.rsqrt(jnp.mean(o * o, axis=-1, keepdims=True) + RMS_EPS)
            gate = g_ref[rows, :]
            y_ref[rows, :] = (o * rstd * onv * (gate * _sigmoid(gate))).astype(BF16)
            return carry

        lax.fori_loop(0, nch, chunk, 0, unroll=2)

    def col(off):
        return pl.BlockSpec((tc, LANES), lambda h, i: (i, off + h))

    osp = pl.BlockSpec((tc, LANES), lambda h, i: (i, h))
    return pl.pallas_call(
        body, name=name, grid=(nh, t // tc),
        in_specs=[col(0), col(nh), col(2 * nh), col(3 * nh),
                  pl.BlockSpec((2, LANES), lambda h, i: (0, h)),
                  pl.BlockSpec((1, LANES), lambda h, i: (0, 0)),
                  pl.BlockSpec(cum_all.shape, lambda h, i: (0, 0)),
                  pl.BlockSpec(masks.shape, lambda h, i: (0, 0, 0))],
        out_specs=[osp, osp, pl.BlockSpec((None, nch, LANES, LANES), lambda h, i: (h, i, 0, 0))],
        out_shape=[jax.ShapeDtypeStruct((t, nh * LANES), BF16), jax.ShapeDtypeStruct((t, nh * LANES), F32),
                   jax.ShapeDtypeStruct((nh, t // CHUNK, LANES, LANES), F32)],
        scratch_shapes=[pltpu.VMEM((LANES, LANES), F32)],
        compiler_params=_cp())(pc, pc, pc, pc, c_lb, out_norm, cum_all, masks)


def _hgrn_bwd(pc, o_saved, states, dy, c_lb, out_norm, *, name, tc=512):
    t = pc.shape[0]
    nh = pc.shape[1] // 4 // LANES
    tc = min(tc, t)
    nch = tc // CHUNK
    nt = t // tc
    cum_all, masks, suffix = _hgrn_consts()

    def body(q_ref, f_ref, i_ref, g_ref, o_ref, st_ref, dy_ref, lb_ref, on_ref, cum_ref, m_ref, suf_ref,
             dq_ref, df_ref, di_ref, dg_ref, dlb_ref, don_ref, dstate):
        @pl.when(pl.program_id(1) == 0)
        def _():
            dstate[...] = jnp.zeros_like(dstate)
            dlb_ref[...] = jnp.zeros_like(dlb_ref)
            don_ref[...] = jnp.zeros_like(don_ref)

        lbv = _lower_bound(lb_ref)
        onv = on_ref[...]

        def chunk(n, carry):
            c = nch - 1 - n
            rows = pl.ds(pl.multiple_of(c * CHUNK, CHUNK), CHUNK)
            qr = q_ref[rows, :]
            sg, fval, g, kk, sq, qs = _hgrn_gates(qr, f_ref[rows, :], lbv)
            vb = i_ref[rows, :].astype(BF16)
            o = o_ref[rows, :]
            gate = g_ref[rows, :]
            sgt = _sigmoid(gate)
            rstd = lax.rsqrt(jnp.mean(o * o, axis=-1, keepdims=True) + RMS_EPS)
            ohat = o * rstd
            dyv = dy_ref[rows, :]
            don = dyv * (gate * sgt)
            dg_ref[rows, :] = (dyv * ohat * onv * (sgt * (1.0 + gate * (1.0 - sgt)))).astype(BF16)
            don_ref[...] += jnp.sum(don * ohat, axis=0, keepdims=True)
            dxhat = don * onv
            dob = (rstd * (dxhat - ohat * jnp.mean(dxhat * ohat, axis=-1, keepdims=True))).astype(BF16)
            ball = _dot_exact_lhs(cum_ref[...], g)
            b = ball[:CHUNK]
            blast = b[CHUNK - 1:CHUNK, :]
            eb = jnp.exp(b)
            edec = jnp.exp(blast - b)
            st32 = st_ref[c]
            st = st32.astype(BF16)
            dst = dstate[...]
            dstb = dst.astype(BF16)
            da = _dot(dob, vb, 1, 1)
            levels = _hgrn_levels(ball, qs, kk)
            scores = jnp.zeros((CHUNK, CHUNK), F32)
            dq = eb * _dot(dob, st)
            dk_inter = edec * _dot(vb, dstb)
            dk = dk_inter
            for lvl, (ql, kl, eq, ek) in enumerate(levels):
                mk = m_ref[lvl]
                (qh, qlo), (kh, klo) = _split2(ql), _split2(kl)
                scores = scores + _dot(qh, kh, 1, 1) * mk
                dal = (da * mk).astype(BF16)
                dql = _dot(dal, kh) + _dot(dal, klo)
                dkl = _dot(dal, qh, 0, 0) + _dot(dal, qlo, 0, 0)
                dq = dq + (dql if eq is None else dql * eq)
                dk = dk + (dkl if ek is None else dkl * ek)
            kdec = (kk * edec).astype(BF16)
            dv = _dot(scores.astype(BF16), dob, 0, 0) + _dot(kdec, dstb, 1, 1)
            dstate[...] = dst * jnp.exp(blast) + _dot(dob, (qs * eb).astype(BF16), 0, 0)
            db = qs * dq - kk * dk
            last = jnp.sum(kk * dk_inter, axis=0, keepdims=True) + jnp.exp(blast) * jnp.sum(dst * st32, axis=0, keepdims=True)
            dgl = _dot_exact_lhs(suf_ref[...], db) + last
            dfv = dgl / fval - dk
            df_ref[rows, :] = (dfv * (1.0 - lbv) * sg * (1.0 - sg)).astype(BF16)
            dlb_ref[...] += jnp.sum(dfv * (1.0 - sg), axis=0, keepdims=True)
            dq_ref[rows, :] = (dq * (sq * (1.0 + qr * (1.0 - sq)))).astype(BF16)
            di_ref[rows, :] = dv.astype(BF16)
            return carry

        lax.fori_loop(0, nch, chunk, 0, unroll=2)

    def col(off):
        return pl.BlockSpec((tc, LANES), lambda h, i: (nt - 1 - i, off + h))

    osp = pl.BlockSpec((tc, LANES), lambda h, i: (nt - 1 - i, h))
    vec = pl.BlockSpec((1, LANES), lambda h, i: (0, h))
    return pl.pallas_call(
        body, name=name, grid=(nh, nt),
        in_specs=[col(0), col(nh), col(2 * nh), col(3 * nh), osp,
                  pl.BlockSpec((None, nch, LANES, LANES), lambda h, i: (h, nt - 1 - i, 0, 0)),
                  osp,
                  pl.BlockSpec((2, LANES), lambda h, i: (0, h)),
                  pl.BlockSpec((1, LANES), lambda h, i: (0, 0)),
                  pl.BlockSpec(cum_all.shape, lambda h, i: (0, 0)),
                  pl.BlockSpec(masks.shape, lambda h, i: (0, 0, 0)),
                  pl.BlockSpec(suffix.shape, lambda h, i: (0, 0))],
        out_specs=[osp, osp, osp, osp, vec, vec],
        out_shape=[jax.ShapeDtypeStruct((t, nh * LANES), BF16)] * 4 + [jax.ShapeDtypeStruct((1, nh * LANES), F32)] * 2,
        scratch_shapes=[pltpu.VMEM((LANES, LANES), F32)],
        compiler_params=_cp())(pc, pc, pc, pc, o_saved, states, dy, c_lb, out_norm, cum_all, masks, suffix)


HBM_SPEC = pl.BlockSpec(memory_space=pltpu.HBM)


def _gather_scratch():
    return [pltpu.SemaphoreType.DMA((7,)), pltpu.SemaphoreType.DMA((7,)), pltpu.SemaphoreType.DMA]


def _gather_phases(x_ref, out_refs, seg_rows, send_sems, recv_sems, local_sem):
    x, y, c = lax.axis_index("x"), lax.axis_index("y"), lax.axis_index("c")
    me, sibling = (x, y, c), (x, y, 1 - c)
    chips = [(1 - x, y), (x, 1 - y), (1 - x, 1 - y)]
    offs = [sum(seg_rows[:s]) for s in range(len(seg_rows))]
    assert sum(seg_rows) == x_ref.shape[0]

    def index(px, py, pc):
        return 4 * px + 2 * py + pc

    def copies(k, block, to, own):
        return [pltpu.make_async_remote_copy(
            src_ref=x_ref.at[pl.ds(offs[s], n)] if own else out_refs[s].at[index(*block)],
            dst_ref=out_refs[s].at[index(*block)],
            send_sem=send_sems.at[k], recv_sem=recv_sems.at[k], device_id=to, device_id_type=MESH)
            for s, n in enumerate(seg_rows)]

    def all_bytes(k):
        return pltpu.make_async_remote_copy(src_ref=x_ref, dst_ref=x_ref, send_sem=send_sems.at[k],
                                            recv_sem=recv_sems.at[k], device_id=me, device_id_type=MESH)

    mine = [pltpu.make_async_copy(x_ref.at[pl.ds(offs[s], n)], out_refs[s].at[index(*me)], local_sem)
            for s, n in enumerate(seg_rows)]
    first = copies(0, me, sibling, True)
    for j, chip in enumerate(chips):
        first += copies(1 + j, me, (*chip, c), True)

    def start():
        for cp in mine + first:
            cp.start()

    def forward():
        for j, chip in enumerate(chips):
            all_bytes(1 + j).wait_recv()
            for cp in copies(4 + j, (*chip, c), sibling, False):
                cp.start()

    def finish():
        all_bytes(0).wait_recv()
        for j in range(3):
            all_bytes(4 + j).wait_recv()
        for k in range(7):
            all_bytes(k).wait_send()
        pltpu.make_async_copy(x_ref, x_ref, local_sem).wait()

    return start, forward, finish


def _all_gather(xs, seg_rows=None, *, name):
    segs = [xs.shape[0]] if seg_rows is None else list(seg_rows)

    def body(x_ref, *rest):
        start, forward, finish = _gather_phases(x_ref, rest[:len(segs)], segs, *rest[len(segs):])
        start()
        forward()
        finish()

    outs = pl.pallas_call(
        body, name=name, in_specs=[HBM_SPEC], out_specs=[HBM_SPEC] * len(segs),
        out_shape=[jax.ShapeDtypeStruct((8, n, xs.shape[1]), xs.dtype) for n in segs],
        scratch_shapes=_gather_scratch())(xs)
    return outs[0] if seg_rows is None else outs


def _sibling_exchange(s, *, name):
    def body(s_ref, rb_ref, send_sem, recv_sem):
        x, y, c = lax.axis_index("x"), lax.axis_index("y"), lax.axis_index("c")
        cp = pltpu.make_async_remote_copy(
            src_ref=s_ref.at[1 - c], dst_ref=rb_ref, send_sem=send_sem, recv_sem=recv_sem,
            device_id=(x, y, 1 - c), device_id_type=MESH)
        cp.start()
        cp.wait()

    return pl.pallas_call(
        body, name=name, in_specs=[HBM_SPEC], out_specs=HBM_SPEC,
        out_shape=jax.ShapeDtypeStruct(s.shape[1:], s.dtype),
        scratch_shapes=[pltpu.SemaphoreType.DMA, pltpu.SemaphoreType.DMA])(s)


def _row_tile(n, cap=1024):
    return max(b for b in range(16, cap + 1, 16) if n % b == 0)


def _pair_add(s, rb, core, *, name):
    _, n, c = s.shape
    tb = _row_tile(n)

    def body(core_ref, a_ref, b_ref, o_ref):
        o_ref[...] = (a_ref[...].astype(F32) + b_ref[...].astype(F32)).astype(BF16)

    return pl.pallas_call(
        body, name=name,
        grid_spec=pltpu.PrefetchScalarGridSpec(
            num_scalar_prefetch=1, grid=(n // tb,),
            in_specs=[pl.BlockSpec((None, tb, c), lambda i, cr: (cr[0], i, 0)),
                      pl.BlockSpec((tb, c), lambda i, cr: (i, 0))],
            out_specs=pl.BlockSpec((tb, c), lambda i, cr: (i, 0))),
        out_shape=jax.ShapeDtypeStruct((n, c), BF16), compiler_params=_cp())(core, s, rb)


def _chip_exchange(p, *, name):
    def body(p_ref, out_ref, *sems):
        start, finish = _chip_exchange_phases(p_ref, out_ref, *sems)
        start()
        finish()

    return pl.pallas_call(
        body, name=name, in_specs=[HBM_SPEC], out_specs=HBM_SPEC,
        out_shape=jax.ShapeDtypeStruct(p.shape, p.dtype), scratch_shapes=_chip_exchange_scratch())(p)


def _chip_exchange_scratch():
    return [pltpu.SemaphoreType.DMA((3,)), pltpu.SemaphoreType.DMA((3,)), pltpu.SemaphoreType.DMA]


def _chip_exchange_phases(p_ref, out_ref, send_sems, recv_sems, local_sem):
    x, y, c = lax.axis_index("x"), lax.axis_index("y"), lax.axis_index("c")
    mine = 2 * x + y
    own = pltpu.make_async_copy(p_ref.at[mine], out_ref.at[mine], local_sem)
    copies = [pltpu.make_async_remote_copy(
        src_ref=p_ref.at[2 * tx + ty], dst_ref=out_ref.at[mine],
        send_sem=send_sems.at[k], recv_sem=recv_sems.at[k], device_id=(tx, ty, c), device_id_type=MESH)
        for k, (tx, ty) in enumerate([(1 - x, y), (x, 1 - y), (1 - x, 1 - y)])]

    def start():
        own.start()
        for cp in copies:
            cp.start()

    def finish():
        for cp in copies:
            cp.wait()
        own.wait()

    return start, finish


def _adamw_math(w, g, m, v):
    m2 = ADAM_B1 * m + (1.0 - ADAM_B1) * g
    v2 = ADAM_B2 * v + (1.0 - ADAM_B2) * (g * g)
    m_hat = m2 / (1.0 - ADAM_B1 ** ADAM_STEP)
    v_hat = v2 / (1.0 - ADAM_B2 ** ADAM_STEP)
    return -ADAM_LR * (m_hat / (jnp.sqrt(v_hat) + ADAM_EPS) + ADAM_WD * w), m2, v2


def _grad_sum(parts, *, name):
    _, r, c = parts.shape
    tb = _row_tile(r)

    def body(p0, p1, p2, p3, g_out):
        g_out[...] = ((p0[...].astype(F32) + p1[...].astype(F32)) + p2[...].astype(F32)) + p3[...].astype(F32)

    def part(ch):
        return pl.BlockSpec((None, tb, c), lambda i: (ch, i, 0))

    return pl.pallas_call(
        body, name=name, grid=(r // tb,), in_specs=[part(0), part(1), part(2), part(3)],
        out_specs=pl.BlockSpec((tb, c), lambda i: (i, 0)), out_shape=jax.ShapeDtypeStruct((r, c), F32),
        compiler_params=_cp())(parts, parts, parts, parts)


def _adamw_shard(g, g_off, w, m, v, layer, prev, *, name):
    _, r, c = w.shape
    tb = next(b for b in range(min(r, 512), 0, -8) if r % b == 0 and g_off % b == 0)

    def body(g_ref, w_ref, m_ref, v_ref, *rest):
        d_out, m_out, v_out = rest[-3:]
        d, m2, v2 = _adamw_math(w_ref[...], g_ref[...], m_ref[...], v_ref[...])
        d_out[...] = d
        m_out[...] = m2
        v_out[...] = v2

    blk = pl.BlockSpec((None, tb, c), lambda i: (layer, i, 0))
    prev = list(prev) if prev is not None else []
    return pl.pallas_call(
        body, name=name, grid=(r // tb,),
        in_specs=[pl.BlockSpec((tb, c), lambda i: (g_off // tb + i, 0)), blk, blk, blk] + [pl.BlockSpec(memory_space=pl.ANY)] * len(prev),
        out_specs=[blk] * 3, out_shape=[jax.ShapeDtypeStruct(w.shape, F32)] * 3,
        input_output_aliases={4 + k: k for k in range(len(prev))},
        compiler_params=_cp())(g, w, m, v, *prev)


SLOT = 8
SMALL_ROWS = 6 * SLOT
ROW_LB = 4 * SLOT


def _small_update(gath, w, m, v, *, name):
    def body(g_ref, w_ref, m_ref, v_ref, g_out, d_out, m_out, v_out):
        tot = g_ref[0]
        for k in range(1, 8):
            tot = tot + g_ref[k]
        wv = w_ref[...]
        c0, c1 = wv[ROW_LB:ROW_LB + 1, :], wv[ROW_LB + 1:ROW_LB + 2, :]
        mx = jnp.maximum(c0, c1)
        e0, e1 = jnp.exp(c0 - mx), jnp.exp(c1 - mx)
        lb = e1 / (e0 + e1)
        gl = tot[ROW_LB:ROW_LB + 1, :] * lb * (1.0 - lb)
        row = lax.broadcasted_iota(jnp.int32, tot.shape, 0)
        g = jnp.where(row == ROW_LB, -gl, jnp.where(row == ROW_LB + 1, gl, tot))
        d, m2, v2 = _adamw_math(wv, g, m_ref[...], v_ref[...])
        g_out[...] = g
        d_out[...] = d
        m_out[...] = m2
        v_out[...] = v2

    return pl.pallas_call(
        body, name=name, out_shape=[jax.ShapeDtypeStruct(w.shape, F32)] * 4, compiler_params=_cp())(gath, w, m, v)


D_MODEL = 1024


def _ffn_fwd(h, gain, wg, wu, wd, tag):
    xn, gg, uu, act = _norm_gate_up(h, gain, wg, wu, name=f"{tag}_gate_up")
    out = _mm([(act, wd)], residual=h, alpha=MACARON, tn=1024, name=f"{tag}_down")
    return out, (h, xn, gg, uu, act)


def _ffn_bwd(dout, dout_half, saved, gain, wg, wu, wd, tag, next_scale):
    h, xn, gg, uu, act = saved
    dg, du = _swiglu_bwd(dout_half, wd, gg, uu, name=f"{tag}_dact")
    dwd = _mm([(act, dout_half)], ta=True, tm=256, tn=1024, out_dtype=BF16, name=f"{tag}_dwd")
    dwg = _mm([(dg, xn)], ta=True, tm=256, tn=1024, out_dtype=BF16, name=f"{tag}_dwg")
    dwu = _mm([(du, xn)], ta=True, tm=256, tn=1024, out_dtype=BF16, name=f"{tag}_dwu")
    dxn = _mm([(dg, wg), (du, wu)], tm=512, tn=1024, name=f"{tag}_dxn")
    dh, dh_b, dgain = _rmsnorm_bwd(h, gain, dxn, dout, scale=next_scale, name=f"{tag}_norm_bwd")
    return dh, dh_b, dwg, dwu, dwd, dgain


def kernel(x, ffn_pre_norm, ffn_pre_w_gate, ffn_pre_w_up, ffn_pre_w_down, mix_norm, ffn_post_norm, ffn_post_w_gate, ffn_post_w_up, ffn_post_w_down, ab_w_in, ab_conv_w, ab_w_out, c_w_in, c_lower_bounds, c_out_norm, c_w_out, final_norm, loss_target, m_ffn_pre_norm, m_ffn_pre_w_gate, m_ffn_pre_w_up, m_ffn_pre_w_down, m_mix_norm, m_ffn_post_norm, m_ffn_post_w_gate, m_ffn_post_w_up, m_ffn_post_w_down, m_ab_w_in, m_ab_conv_w, m_ab_w_out, m_c_w_in, m_c_lower_bounds, m_c_out_norm, m_c_w_out, m_final_norm, v_ffn_pre_norm, v_ffn_pre_w_gate, v_ffn_pre_w_up, v_ffn_pre_w_down, v_mix_norm, v_ffn_post_norm, v_ffn_post_w_gate, v_ffn_post_w_up, v_ffn_post_w_down, v_ab_w_in, v_ab_conv_w, v_ab_w_out, v_c_w_in, v_c_lower_bounds, v_c_out_norm, v_c_w_out, v_final_norm):
    d = D_MODEL
    h0 = x[0]
    target = loss_target[0]
    core = lax.axis_index("c").astype(jnp.int32).reshape(1)

    big = [("pre_g", ffn_pre_w_gate, m_ffn_pre_w_gate, v_ffn_pre_w_gate),
           ("pre_u", ffn_pre_w_up, m_ffn_pre_w_up, v_ffn_pre_w_up),
           ("pre_d", ffn_pre_w_down, m_ffn_pre_w_down, v_ffn_pre_w_down),
           ("post_g", ffn_post_w_gate, m_ffn_post_w_gate, v_ffn_post_w_gate),
           ("post_u", ffn_post_w_up, m_ffn_post_w_up, v_ffn_post_w_up),
           ("post_d", ffn_post_w_down, m_ffn_post_w_down, v_ffn_post_w_down),
           ("ab_in", ab_w_in, m_ab_w_in, v_ab_w_in),
           ("ab_out", ab_w_out, m_ab_w_out, v_ab_w_out),
           ("c_in", c_w_in, m_c_w_in, v_c_w_in),
           ("c_out", c_w_out, m_c_w_out, v_c_w_out)]
    by_tag = {tag: (w, m, v) for tag, w, m, v in big}

    def layer_rows(tag):
        w = by_tag[tag][0]
        return w.size // d // w.shape[0]

    def layout(items):
        offs, off = {}, 0
        for item in items:
            offs[item] = off
            off += layer_rows(item[0])
        return offs, off

    ffn = [f"{pos}_{kind}" for pos in ("pre", "post") for kind in "gud"]
    early_items = [("pre_g", 0), ("pre_u", 0), ("pre_d", 0), ("ab_in", 0)]
    late_items = ([("pre_g", 1), ("pre_u", 1), ("pre_d", 1)] + [(f"post_{kind}", l) for l in (0, 1) for kind in "gud"]
                  + [("ab_out", 0), ("c_in", 0), ("c_out", 0)])
    grad_items = {"A": ([(tag, 1) for tag in ffn] + [(f"post_{kind}", 0) for kind in "gud"]
                        + [("c_in", 0), ("c_out", 0), ("ab_out", 0)]),
                  "B": [(f"pre_{kind}", 0) for kind in "gud"] + [("ab_in", 0)]}
    grad_offs = {k: layout(items)[0] for k, items in grad_items.items()}
    grad_conv_row = layout(grad_items["B"])[1]

    def conv_rows(a, split):
        flat = a.reshape(-1)
        if split:
            hi = flat.astype(BF16)
            flat = jnp.concatenate([hi, (flat - hi.astype(F32)).astype(BF16)])
        return jnp.zeros((16, d), flat.dtype).at[0, :flat.shape[0]].set(flat)

    nconv = ab_conv_w.size
    col_sharded = {"pre_g", "pre_u", "post_g", "post_u", "ab_in", "c_in"}

    def pack_rows(item):
        tag, layer = item
        a = by_tag[tag][0][layer]
        return (a.T if tag in col_sharded else a).reshape(-1, d).astype(BF16)

    early_pack = jnp.concatenate([pack_rows(item) for item in early_items] + [conv_rows(ab_conv_w, True)], axis=0)
    late_pack = jnp.concatenate([pack_rows(item) for item in late_items], axis=0)
    early_w = _all_gather(early_pack, [layer_rows(tag) for tag, _ in early_items] + [16], name="gather_early_weights")
    full = {item: g.reshape(-1, d) for item, g in zip(early_items, early_w)}
    ffn_w = {("pre", 0): tuple(full[f"pre_{kind}", 0] for kind in "gud")}
    w_ab_in = full["ab_in", 0]
    cg = early_w[-1][:, 0, :2 * nconv].astype(F32)
    conv_w = (cg[:, :nconv] + cg[:, nconv:]).reshape(8, 3, -1).transpose(1, 0, 2).reshape(3, -1)
    half = w_ab_in.shape[0] // 2
    w_a_in, w_b_in = w_ab_in[:half], w_ab_in[half:]
    aw = half // 3

    h1, s_pre0 = _ffn_fwd(h0, ffn_pre_norm[0:1], *ffn_w["pre", 0], "l0pre")
    hn0 = _rmsnorm_fwd(h1, mix_norm[0:1], name="l0_mix_norm")
    pa = _mm([(hn0, w_a_in)], tb=True, tn=1536, name="ab_proj_a")
    pb = _mm([(hn0, w_b_in)], tb=True, tn=1536, out_dtype=BF16, name="ab_proj_b")
    ya = _conv_fwd(pa, conv_w, name="conv_fwd")
    yb, ltot, *late_w = _attn_fwd(pb, late_pack, [layer_rows(tag) for tag, _ in late_items],
                                  name="attn_fwd_gather_late_weights")
    full.update({item: g.reshape(-1, d) for item, g in zip(late_items, late_w)})
    for pos, layer in (("post", 0), ("pre", 1), ("post", 1)):
        ffn_w[pos, layer] = tuple(full[f"{pos}_{kind}", layer] for kind in "gud")
    w_ab_out, w_c_in, w_c_out = full["ab_out", 0], full["c_in", 0], full["c_out", 0]
    h2 = _mm([(ya, w_ab_out[:aw]), (yb, w_ab_out[aw:])], residual=h1, tn=1024, name="ab_out")
    h3, s_post0 = _ffn_fwd(h2, ffn_post_norm[0:1], *ffn_w["post", 0], "l0post")
    h4, s_pre1 = _ffn_fwd(h3, ffn_pre_norm[1:2], *ffn_w["pre", 1], "l1pre")
    hn1 = _rmsnorm_fwd(h4, mix_norm[1:2], name="l1_mix_norm")
    pc = _mm([(hn1, w_c_in)], tb=True, tm=256, tn=4096, name="c_proj")
    yc, o_saved, states = _hgrn_fwd(pc, c_lower_bounds, c_out_norm, name="hgrn_fwd")
    h5 = _mm([(yc, w_c_out)], residual=h4, tn=1024, name="c_out")
    h6, s_post1 = _ffn_fwd(h5, ffn_post_norm[1:2], *ffn_w["post", 1], "l1post")
    dh6, dh6_b, d_final, loss_vec = _loss_head(h6, final_norm.reshape(1, d), target, name="loss_head")

    gw = {}
    dh5, dh5_b, gw["post_g", 1], gw["post_u", 1], gw["post_d", 1], d_post1 = _ffn_bwd(
        dh6, dh6_b, s_post1, ffn_post_norm[1:2], *ffn_w["post", 1], "l1post", 1.0)
    dyc = _mm([(dh5_b, w_c_out)], tb=True, tn=1024, name="c_out_dy")
    g_c_out = _mm([(yc, dh5_b)], ta=True, tm=256, tn=1024, out_dtype=BF16, name="c_out_dw")
    dcq, dcf, dci, dcg, dlb, d_onorm = _hgrn_bwd(pc, o_saved, states, dyc, c_lower_bounds, c_out_norm, name="hgrn_bwd")
    dparts = [dcq, dcf, dci, dcg]
    g_c_in = jnp.concatenate(
        [_mm([(dp, hn1)], ta=True, tm=256, tn=1024, out_dtype=BF16, name=f"c_in_dw{i}") for i, dp in enumerate(dparts)],
        axis=0)
    cw = w_c_in.shape[0] // 4
    dhn1 = _mm([(dp, w_c_in[i * cw:(i + 1) * cw]) for i, dp in enumerate(dparts)], tm=512, tn=1024, name="c_in_dx")
    dh4, dh4_b, d_mix1 = _rmsnorm_bwd(h4, mix_norm[1:2], dhn1, dh5, scale=MACARON, name="l1_mix_norm_bwd")
    dh3, dh3_b, gw["pre_g", 1], gw["pre_u", 1], gw["pre_d", 1], d_pre1 = _ffn_bwd(
        dh4, dh4_b, s_pre1, ffn_pre_norm[1:2], *ffn_w["pre", 1], "l1pre", MACARON)
    dh2, dh2_b, gw["post_g", 0], gw["post_u", 0], gw["post_d", 0], d_post0 = _ffn_bwd(
        dh3, dh3_b, s_post0, ffn_post_norm[0:1], *ffn_w["post", 0], "l0post", 1.0)
    dyab = _mm([(dh2_b, w_ab_out)], tb=True, tn=1024, name="ab_out_dy")
    g_ab_out = jnp.concatenate([_mm([(ya, dh2_b)], ta=True, tm=256, tn=1024, out_dtype=BF16, name="ab_out_dw_a"),
                                _mm([(yb, dh2_b)], ta=True, tm=256, tn=1024, out_dtype=BF16, name="ab_out_dw_b")], axis=0)
    dab, dac, dax, g_conv = _conv_bwd(pa, dyab, conv_w, name="conv_bwd")

    def chip_partials(key, grads, extra=()):
        gpack = jnp.concatenate([grads[item].reshape(8, -1, d) for item in grad_items[key]] + list(extra), axis=1)
        rows = gpack.shape[1]
        send = gpack.reshape(4, 2, rows, d).transpose(1, 0, 2, 3)
        from_sibling = _sibling_exchange(send, name=f"grad{key}_sibling_exchange")
        return _pair_add(send.reshape(2, 4 * rows, d), from_sibling.reshape(4 * rows, d), core,
                         name=f"grad{key}_pair_add").reshape(4, rows, d)

    gw["c_in", 0], gw["c_out", 0], gw["ab_out", 0] = g_c_in, g_c_out, g_ab_out
    chip_part_a = chip_partials("A", gw)
    dq, dk, dv, parts_a = _attn_bwd(pb, dyab, ltot, chip_part_a, name="attn_bwd_exchange_grads_a")
    dparts = [dab, dac, dax, dq, dk, dv]
    g_ab_in = jnp.concatenate(
        [_mm([(dp, hn0)], ta=True, tm=256, tn=1024, out_dtype=BF16, name=f"ab_in_dw{i}") for i, dp in enumerate(dparts)],
        axis=0)
    dhn0 = _mm([(dp, w_ab_in[i * aw:(i + 1) * aw]) for i, dp in enumerate(dparts)], tm=512, tn=1024, name="ab_in_dx")
    dh1, dh1_b, d_mix0 = _rmsnorm_bwd(h1, mix_norm[0:1], dhn0, dh2, scale=MACARON, name="l0_mix_norm_bwd")
    dh0, _, gw["pre_g", 0], gw["pre_u", 0], gw["pre_d", 0], d_pre0 = _ffn_bwd(
        dh1, dh1_b, s_pre0, ffn_pre_norm[0:1], *ffn_w["pre", 0], "l0pre", 1.0)

    gw["ab_in", 0] = g_ab_in
    gconv_own = g_conv.reshape(3, 8, -1).transpose(1, 0, 2).reshape(8, -1)
    conv_piece = jnp.zeros((8, 16, d), F32).at[:, 0, :nconv].set(gconv_own).astype(BF16)
    parts_b = _chip_exchange(chip_partials("B", gw, [conv_piece]), name="gradB_chip_exchange")
    g_sum = {"A": _grad_sum(parts_a, name="gradA_sum"), "B": _grad_sum(parts_b, name="gradB_sum")}

    upd = {}
    for tag, w, m, v in big:
        nl = layer_rows(tag)
        g_layers = {}
        for key in ("A", "B"):
            for t2, layer in grad_items[key]:
                if t2 == tag:
                    off = grad_offs[key][tag, layer]
                    g_rows = g_sum[key][off:off + nl]
                    g_layers[layer] = (g_rows.T, 0) if tag in col_sharded else (g_sum[key], off)
                    g_layers[layer] += (g_rows.T if tag in col_sharded else g_rows,)
        res = None
        for layer in sorted(g_layers):
            g_arr, off, _ = g_layers[layer]
            res = _adamw_shard(g_arr, off, w, m, v, layer, res, name=f"adamw_{tag}{layer}")
        upd[tag] = [jnp.stack([g_layers[layer][2] for layer in sorted(g_layers)])] + list(res)
    res = _adamw_shard(g_sum["B"], grad_conv_row, *(conv_rows(a, False)[None] for a in (ab_conv_w, m_ab_conv_w, v_ab_conv_w)),
                       0, None, name="adamw_conv")
    g_conv_rows = g_sum["B"][grad_conv_row:grad_conv_row + 16]
    upd["conv"] = [r[0, :nconv].reshape(ab_conv_w.shape) for r in [g_conv_rows] + [r[0] for r in res]]

    def small_pack(pre, mix, post, final, lbs, onorm):
        def slot(parts):
            out, r = jnp.zeros((SLOT, d), F32), 0
            for a in (parts if isinstance(parts, tuple) else (parts,)):
                out = out.at[r:r + a.shape[0], :a.shape[1]].set(a)
                r += a.shape[0]
            return out

        return jnp.concatenate([slot(pre), slot(mix), slot(post), slot(final.reshape(1, d)), slot(lbs), slot(onorm)], axis=0)

    d_on = d_onorm.reshape(-1, c_out_norm.shape[1]).sum(axis=0, keepdims=True)
    gsmall = small_pack((d_pre0, d_pre1), (d_mix0, d_mix1), (d_post0, d_post1), d_final, dlb, d_on)
    gsmall_all = _all_gather(gsmall, name="gather_small_grads")
    sres = _small_update(
        gsmall_all,
        small_pack(ffn_pre_norm, mix_norm, ffn_post_norm, final_norm, c_lower_bounds, c_out_norm),
        small_pack(m_ffn_pre_norm, m_mix_norm, m_ffn_post_norm, m_final_norm, m_c_lower_bounds, m_c_out_norm),
        small_pack(v_ffn_pre_norm, v_mix_norm, v_ffn_post_norm, v_final_norm, v_c_lower_bounds, v_c_out_norm),
        name="small_update")

    def small_out(r):
        return {"pre_norm": r[0:2], "mix_norm": r[SLOT:SLOT + 2], "post_norm": r[2 * SLOT:2 * SLOT + 2],
                "final": r[3 * SLOT], "lb": r[ROW_LB:ROW_LB + 2], "onorm": r[5 * SLOT:5 * SLOT + 1, :c_out_norm.shape[1]]}

    small = [small_out(r) for r in sres]
    outs = []
    for k in range(4):
        s = small[k]
        outs += [s["pre_norm"], upd["pre_g"][k], upd["pre_u"][k], upd["pre_d"][k], s["mix_norm"], s["post_norm"],
                 upd["post_g"][k], upd["post_u"][k], upd["post_d"][k], upd["ab_in"][k], upd["conv"][k],
                 upd["ab_out"][k], upd["c_in"][k], s["lb"], s["onorm"], upd["c_out"][k], s["final"]]
    loss = lax.psum(loss_vec[0, 0], ("x", "y", "c"))
    return (loss, dh0[None], *outs)
```

```python
import functools
import math

import jax
import jax.numpy as jnp
from jax import lax
from jax.experimental import pallas as pl
from jax.experimental.pallas import tpu as pltpu

F32 = jnp.float32
BF16 = jnp.bfloat16
MESH = pl.DeviceIdType.MESH

RMS_EPS = 1e-6
MACARON = 0.5
LANES = 128
CHUNK = 64
N_LEVELS = 6
SB_KEYS = 256
ADAM_LR, ADAM_B1, ADAM_B2, ADAM_EPS, ADAM_WD, ADAM_STEP = 0.001, 0.9, 0.999, 1e-08, 0.01, 10
VMEM_LIMIT = 48 * 1024 * 1024


def _cp(**kw):
    return pltpu.CompilerParams(vmem_limit_bytes=VMEM_LIMIT, **kw)


def _sigmoid(x):
    return 1.0 / (1.0 + jnp.exp(-x))


def _bf(x):
    return x if x.dtype == BF16 else x.astype(BF16)


def _split3(x):
    hi = x.astype(BF16)
    r1 = x - hi.astype(F32)
    mid = r1.astype(BF16)
    lo = (r1 - mid.astype(F32)).astype(BF16)
    return hi, mid, lo


def _dot(a, b, ca=1, cb=0):
    return lax.dot_general(a, b, (((ca,), (cb,)), ((), ())), preferred_element_type=F32)


def _dot_exact_lhs(m, x):
    hi, mid, lo = _split3(x)
    return _dot(m, hi) + _dot(m, mid) + _dot(m, lo)


def _dot_exact_rhs(x, m):
    hi, mid, lo = _split3(x)
    return _dot(hi, m) + _dot(mid, m) + _dot(lo, m)


def _mm(terms, *, name, ta=False, tb=False, out_dtype=F32, residual=None, alpha=1.0, tm=512, tn=512):
    nt = len(terms)
    a0, b0 = terms[0]
    m = a0.shape[1] if ta else a0.shape[0]
    n = b0.shape[0] if tb else b0.shape[1]
    tm, tn = min(tm, m), min(tn, n)
    assert m % tm == 0 and n % tn == 0, (name, m, n, tm, tn)
    has_res = residual is not None

    def body(*refs):
        o_ref = refs[-1]
        acc = None
        for i in range(nt):
            a = _bf(refs[2 * i][...])
            b = _bf(refs[2 * i + 1][...])
            p = _dot(a, b, 0 if ta else 1, 1 if tb else 0)
            acc = p if acc is None else acc + p
        if alpha != 1.0:
            acc = acc * alpha
        if has_res:
            acc = acc + refs[2 * nt][...]
        o_ref[...] = acc.astype(out_dtype)

    in_specs, args = [], []
    for a, b in terms:
        k = a.shape[0] if ta else a.shape[1]
        assert (b.shape[1] if tb else b.shape[0]) == k, (name, a.shape, b.shape)
        in_specs.append(pl.BlockSpec((k, tm), lambda i, j: (0, i)) if ta else pl.BlockSpec((tm, k), lambda i, j: (i, 0)))
        in_specs.append(pl.BlockSpec((tn, k), lambda i, j: (j, 0)) if tb else pl.BlockSpec((k, tn), lambda i, j: (0, j)))
        args += [a, b]
    if has_res:
        in_specs.append(pl.BlockSpec((tm, tn), lambda i, j: (i, j)))
        args.append(residual)
    return pl.pallas_call(
        body, name=name, grid=(m // tm, n // tn), in_specs=in_specs,
        out_specs=pl.BlockSpec((tm, tn), lambda i, j: (i, j)),
        out_shape=jax.ShapeDtypeStruct((m, n), out_dtype), compiler_params=_cp())(*args)


def _rmsnorm_fwd(x, gain, *, name, tm=512):
    t, d = x.shape
    tm = min(tm, t)

    def body(x_ref, g_ref, o_ref):
        xv = x_ref[...]
        rstd = lax.rsqrt(jnp.mean(xv * xv, axis=-1, keepdims=True) + RMS_EPS)
        o_ref[...] = (xv * rstd * g_ref[...]).astype(BF16)

    return pl.pallas_call(
        body, name=name, grid=(t // tm,),
        in_specs=[pl.BlockSpec((tm, d), lambda i: (i, 0)), pl.BlockSpec((1, d), lambda i: (0, 0))],
        out_specs=pl.BlockSpec((tm, d), lambda i: (i, 0)),
        out_shape=jax.ShapeDtypeStruct((t, d), BF16), compiler_params=_cp())(x, gain)


def _rmsnorm_bwd(x, gain, dxn, dres, *, name, scale, tm=512):
    t, d = x.shape
    tm = min(tm, t)

    def body(x_ref, g_ref, dxn_ref, dres_ref, dx_ref, dxb_ref, dg_ref):
        xv = x_ref[...]
        rstd = lax.rsqrt(jnp.mean(xv * xv, axis=-1, keepdims=True) + RMS_EPS)
        xhat = xv * rstd
        dxn_v = dxn_ref[...]
        dxhat = dxn_v * g_ref[...]
        dx = dres_ref[...] + rstd * (dxhat - xhat * jnp.mean(dxhat * xhat, axis=-1, keepdims=True))
        dx_ref[...] = dx
        dxb_ref[...] = (dx * scale).astype(BF16)

        @pl.when(pl.program_id(0) == 0)
        def _():
            dg_ref[...] = jnp.zeros_like(dg_ref)

        dg_ref[...] += jnp.sum(dxn_v * xhat, axis=0, keepdims=True)

    row = pl.BlockSpec((tm, d), lambda i: (i, 0))
    vec = pl.BlockSpec((1, d), lambda i: (0, 0))
    return pl.pallas_call(
        body, name=name, grid=(t // tm,), in_specs=[row, vec, row, row], out_specs=[row, row, vec],
        out_shape=[jax.ShapeDtypeStruct((t, d), F32), jax.ShapeDtypeStruct((t, d), BF16), jax.ShapeDtypeStruct((1, d), F32)],
        compiler_params=_cp())(x, gain, dxn, dres)


def _loss_head(h, gain, target, *, name, tm=512):
    t, d = h.shape
    tm = min(tm, t)

    def body(h_ref, g_ref, t_ref, dh_ref, dhb_ref, dg_ref, loss_ref):
        hv = h_ref[...]
        rstd = lax.rsqrt(jnp.mean(hv * hv, axis=-1, keepdims=True) + RMS_EPS)
        xhat = hv * rstd
        err = xhat * g_ref[...] - t_ref[...]
        dy = err * (1.0 / d)
        dxhat = dy * g_ref[...]
        dh = rstd * (dxhat - xhat * jnp.mean(dxhat * xhat, axis=-1, keepdims=True))
        dh_ref[...] = dh
        dhb_ref[...] = (dh * MACARON).astype(BF16)

        @pl.when(pl.program_id(0) == 0)
        def _():
            dg_ref[...] = jnp.zeros_like(dg_ref)
            loss_ref[...] = jnp.zeros_like(loss_ref)

        dg_ref[...] += jnp.sum(dy * xhat, axis=0, keepdims=True)
        part = jnp.sum(jnp.sum(err * err, axis=-1, keepdims=True), axis=0, keepdims=True) * (0.5 / d)
        loss_ref[...] += jnp.broadcast_to(part, loss_ref.shape)

    row = pl.BlockSpec((tm, d), lambda i: (i, 0))
    vec = pl.BlockSpec((1, d), lambda i: (0, 0))
    return pl.pallas_call(
        body, name=name, grid=(t // tm,), in_specs=[row, vec, row],
        out_specs=[row, row, vec, pl.BlockSpec((1, LANES), lambda i: (0, 0))],
        out_shape=[jax.ShapeDtypeStruct((t, d), F32), jax.ShapeDtypeStruct((t, d), BF16), jax.ShapeDtypeStruct((1, d), F32),
                   jax.ShapeDtypeStruct((1, LANES), F32)],
        compiler_params=_cp())(h, gain, target)


def _norm_gate_up(x, gain, wg, wu, *, name, tm=512, tf=1408):
    t, d = x.shape
    f = wg.shape[0]
    tm, tf = min(tm, t), min(tf, f)
    assert f % tf == 0

    def body(x_ref, g_ref, wg_ref, wu_ref, xn_ref, gg_ref, uu_ref, act_ref):
        @pl.when(pl.program_id(1) == 0)
        def _():
            xv = x_ref[...]
            rstd = lax.rsqrt(jnp.mean(xv * xv, axis=-1, keepdims=True) + RMS_EPS)
            xn_ref[...] = (xv * rstd * g_ref[...]).astype(BF16)

        xn = xn_ref[...]
        gv = _dot(xn, wg_ref[...], 1, 1)
        uv = _dot(xn, wu_ref[...], 1, 1)
        gg_ref[...] = gv.astype(BF16)
        uu_ref[...] = uv.astype(BF16)
        act_ref[...] = (gv * _sigmoid(gv) * uv).astype(BF16)

    row = pl.BlockSpec((tm, d), lambda i, j: (i, 0))
    wsp = pl.BlockSpec((tf, d), lambda i, j: (j, 0))
    osp = pl.BlockSpec((tm, tf), lambda i, j: (i, j))
    return pl.pallas_call(
        body, name=name, grid=(t // tm, f // tf),
        in_specs=[row, pl.BlockSpec((1, d), lambda i, j: (0, 0)), wsp, wsp],
        out_specs=[row, osp, osp, osp],
        out_shape=[jax.ShapeDtypeStruct((t, d), BF16)] + [jax.ShapeDtypeStruct((t, f), BF16)] * 3,
        compiler_params=_cp())(x, gain, wg, wu)


def _swiglu_bwd(dout, wd, gg, uu, *, name, tm=512, tf=1408):
    t, d = dout.shape
    f = wd.shape[0]
    tm, tf = min(tm, t), min(tf, f)

    def body(do_ref, wd_ref, g_ref, u_ref, dg_ref, du_ref):
        dact = _dot(do_ref[...], wd_ref[...], 1, 1)
        gv = g_ref[...].astype(F32)
        uv = u_ref[...].astype(F32)
        sg = _sigmoid(gv)
        dg_ref[...] = (dact * uv * (sg * (1.0 + gv * (1.0 - sg)))).astype(BF16)
        du_ref[...] = (dact * (gv * sg)).astype(BF16)

    osp = pl.BlockSpec((tm, tf), lambda i, j: (i, j))
    return pl.pallas_call(
        body, name=name, grid=(t // tm, f // tf),
        in_specs=[pl.BlockSpec((tm, d), lambda i, j: (i, 0)), pl.BlockSpec((tf, d), lambda i, j: (j, 0)), osp, osp],
        out_specs=[osp, osp], out_shape=[jax.ShapeDtypeStruct((t, f), BF16)] * 2,
        compiler_params=_cp())(dout, wd, gg, uu)


def _shift_down(x, n):
    rows = lax.broadcasted_iota(jnp.int32, x.shape, 0)
    return jnp.where(rows >= n, pltpu.roll(x, n, 0), 0.0)


def _shift_up(x, n):
    t = x.shape[0]
    rows = lax.broadcasted_iota(jnp.int32, x.shape, 0)
    return jnp.where(rows < t - n, pltpu.roll(x, t - n, 0), 0.0)


def _conv_fwd(pa, conv_w, *, name):
    t = pa.shape[0]
    nb = pa.shape[1] // 3 // LANES

    def body(b_ref, c_ref, x_ref, w_ref, y_ref):
        u = c_ref[...] * x_ref[...]
        w = w_ref[...]
        conv = w[2:3, :] * u + w[1:2, :] * _shift_down(u, 1) + w[0:1, :] * _shift_down(u, 2)
        y_ref[...] = (b_ref[...] * conv).astype(BF16)

    def col(off):
        return pl.BlockSpec((t, LANES), lambda j: (0, off + j))

    return pl.pallas_call(
        body, name=name, grid=(nb,),
        in_specs=[col(0), col(nb), col(2 * nb), pl.BlockSpec((3, LANES), lambda j: (0, j))],
        out_specs=pl.BlockSpec((t, LANES), lambda j: (0, j)),
        out_shape=jax.ShapeDtypeStruct((t, nb * LANES), BF16), compiler_params=_cp())(pa, pa, pa, conv_w)


def _conv_bwd(pa, dy, conv_w, *, name):
    t = pa.shape[0]
    nb = pa.shape[1] // 3 // LANES

    def body(b_ref, c_ref, x_ref, dy_ref, w_ref, db_ref, dc_ref, dx_ref, dw_ref):
        cv, xv = c_ref[...], x_ref[...]
        u = cv * xv
        u1, u2 = _shift_down(u, 1), _shift_down(u, 2)
        w = w_ref[...]
        conv = w[2:3, :] * u + w[1:2, :] * u1 + w[0:1, :] * u2
        dyv = dy_ref[...]
        db_ref[...] = (dyv * conv).astype(BF16)
        dconv = dyv * b_ref[...]
        du = w[2:3, :] * dconv + w[1:2, :] * _shift_up(dconv, 1) + w[0:1, :] * _shift_up(dconv, 2)
        dc_ref[...] = (du * xv).astype(BF16)
        dx_ref[...] = (du * cv).astype(BF16)
        dw_ref[0:1, :] = jnp.sum(dconv * u2, axis=0, keepdims=True)
        dw_ref[1:2, :] = jnp.sum(dconv * u1, axis=0, keepdims=True)
        dw_ref[2:3, :] = jnp.sum(dconv * u, axis=0, keepdims=True)

    def col(off):
        return pl.BlockSpec((t, LANES), lambda j: (0, off + j))

    osp = pl.BlockSpec((t, LANES), lambda j: (0, j))
    wsp = pl.BlockSpec((3, LANES), lambda j: (0, j))
    return pl.pallas_call(
        body, name=name, grid=(nb,), in_specs=[col(0), col(nb), col(2 * nb), col(0), wsp],
        out_specs=[osp, osp, osp, wsp],
        out_shape=[jax.ShapeDtypeStruct((t, nb * LANES), BF16)] * 3 + [jax.ShapeDtypeStruct((3, nb * LANES), F32)],
        compiler_params=_cp())(pa, pa, pa, dy, conv_w)


def _sb_consts():
    j = lax.broadcasted_iota(jnp.int32, (SB_KEYS, SB_KEYS), 0)
    s = lax.broadcasted_iota(jnp.int32, (SB_KEYS, SB_KEYS), 1)
    after = (j > s).astype(BF16)
    upto = (j <= s).astype(BF16)
    before = (j < s).astype(BF16)
    return after, jnp.stack([upto, before])


def _log_sigmoid(z):
    return jnp.minimum(z, 0.0) - jnp.log(1.0 + jnp.exp(-jnp.abs(z)))


def _attn_fwd(pb, late_pack, seg_rows, *, name, tq=256):
    t = pb.shape[0]
    npair = pb.shape[1] // 3 // LANES
    tq = min(tq, t)
    nq = t // tq
    cmat, _ = _sb_consts()
    scale = 1.0 / math.sqrt(LANES // 2)

    nseg = len(seg_rows)

    def body(q_ref, k_ref, v_ref, c_ref, late_ref, y_ref, lt_ref, *rest):
        i = pl.program_id(1)
        pair = pl.program_id(0)
        start, forward, finish = _gather_phases(late_ref, rest[:nseg], seg_rows, *rest[nseg:])
        pl.when((pair == 0) & (i == 0))(start)
        pl.when((pair == npair - 1) & (i == nq // 2))(forward)
        lane = lax.broadcasted_iota(jnp.int32, (tq, LANES), 1)
        rowpos = i * tq + lax.broadcasted_iota(jnp.int32, (tq, SB_KEYS), 0)
        colid = lax.broadcasted_iota(jnp.int32, (tq, SB_KEYS), 1)
        q2 = q_ref[...] * jnp.asarray(scale, BF16)
        cm = c_ref[...]
        hi_lanes = lane >= LANES // 2
        qhs = [jnp.where(hi_lanes == (hh == 1), q2, jnp.zeros_like(q2)) for hh in range(2)]
        per_q = tq // SB_KEYS

        def blk(jb):
            return pl.ds(pl.multiple_of(jb * SB_KEYS, SB_KEYS), SB_KEYS)

        def scores(jb):
            kb = k_ref[blk(jb), :]
            return tuple(_dot(qhs[hh], kb, 1, 1) for hh in range(2))

        def weights(jb, zs, runs, masked):
            mask = (jb * SB_KEYS + colid) < rowpos if masked else None
            ws, new_runs = [], []
            for hh in range(2):
                lb = _log_sigmoid(zs[hh])
                lk = lb - zs[hh]
                if masked:
                    lk = jnp.where(mask, lk, 0.0)
                lk_hi, lk_lo = _split2(lk)
                cs = _dot(lk_hi, cm) + _dot(lk_lo, cm)
                w = jnp.exp(lb + runs[hh] + cs)
                if masked:
                    w = jnp.where(mask, w, 0.0)
                ws.append(w.astype(BF16))
                new_runs.append(runs[hh] + jnp.sum(lk, axis=1, keepdims=True))
            return tuple(ws), tuple(new_runs)

        def values(jb, accs, ws):
            vb = v_ref[blk(jb), :]
            return tuple(accs[hh] + _dot(ws[hh], vb) for hh in range(2))

        zero = jnp.zeros((tq, LANES), F32)
        zero_col = jnp.zeros((tq, 1), F32)
        nfull = i * per_q
        runs, accs, ws = (zero_col, zero_col), (zero, zero), None
        zs = scores(nfull + per_q - 1)
        for dblk in reversed(range(per_q)):
            jb = nfull + dblk
            zs_next = scores(jnp.maximum(jb - 1, 0))
            if ws is not None:
                accs = values(jb + 1, accs, ws)
            ws, runs = weights(jb, zs, runs, True)
            zs = zs_next

        def full_block(n, carry):
            zs, ws, runs, accs = carry
            jb = nfull - 1 - n
            zs_next = scores(jnp.maximum(jb - 1, 0))
            accs = values(jb + 1, accs, ws)
            ws, runs = weights(jb, zs, runs, False)
            return zs_next, ws, runs, accs

        _, ws, runs, accs = lax.fori_loop(0, nfull, full_block, (zs, ws, runs, accs))
        accs = values(0, accs, ws)
        y_ref[...] = jnp.where(hi_lanes, accs[1], accs[0]).astype(BF16)
        lt_ref[...] = jnp.where(hi_lanes, runs[1], runs[0])
        pl.when((pair == npair - 1) & (i == nq - 1))(finish)

    return pl.pallas_call(
        body, name=name, grid=(npair, nq),
        in_specs=[pl.BlockSpec((tq, LANES), lambda p, i: (i, p)),
                  pl.BlockSpec((t, LANES), lambda p, i: (0, npair + p)),
                  pl.BlockSpec((t, LANES), lambda p, i: (0, 2 * npair + p)),
                  pl.BlockSpec((SB_KEYS, SB_KEYS), lambda p, i: (0, 0)),
                  HBM_SPEC],
        out_specs=[pl.BlockSpec((tq, LANES), lambda p, i: (i, p))] * 2 + [HBM_SPEC] * nseg,
        out_shape=[jax.ShapeDtypeStruct((t, npair * LANES), BF16), jax.ShapeDtypeStruct((t, npair * LANES), F32),
                   ] + [jax.ShapeDtypeStruct((8, n, late_pack.shape[1]), late_pack.dtype) for n in seg_rows],
        scratch_shapes=_gather_scratch(),
        compiler_params=_cp(dimension_semantics=("arbitrary", "arbitrary")))(pb, pb, pb, cmat, late_pack)


def _attn_bwd(pb, dy, ltot, chip_part, *, name, tq=256):
    t = pb.shape[0]
    npair = pb.shape[1] // 3 // LANES
    tq = min(tq, t)
    nq = t // tq
    _, cmats = _sb_consts()
    scale = 1.0 / math.sqrt(LANES // 2)

    def body(q_ref, k_ref, v_ref, dy_ref, lt_ref, c_ref, part_ref, dq_ref, dk_ref, dv_ref, parts_ref, dk_acc, dv_acc, *sems):
        i = pl.program_id(1)
        pair = pl.program_id(0)
        start, finish = _chip_exchange_phases(part_ref, parts_ref, *sems)
        pl.when((pair == 0) & (i == 0))(start)

        @pl.when(i == 0)
        def _():
            dk_acc[...] = jnp.zeros_like(dk_acc)
            dv_acc[...] = jnp.zeros_like(dv_acc)

        lane = lax.broadcasted_iota(jnp.int32, (tq, LANES), 1)
        rowpos = i * tq + lax.broadcasted_iota(jnp.int32, (tq, SB_KEYS), 0)
        colid = lax.broadcasted_iota(jnp.int32, (tq, SB_KEYS), 1)
        q2 = q_ref[...] * jnp.asarray(scale, BF16)
        do2 = dy_ref[...].astype(BF16)
        ltv = lt_ref[...]
        c_upto, c_before = c_ref[0], c_ref[1]
        hi_lanes = lane >= LANES // 2
        sels = [hi_lanes == (hh == 1) for hh in range(2)]
        qhs = [jnp.where(s, q2, jnp.zeros_like(q2)) for s in sels]
        dohs = [jnp.where(s, do2, jnp.zeros_like(do2)) for s in sels]
        lts = [ltv[:, 0:1], ltv[:, LANES // 2:LANES // 2 + 1]]
        per_q = tq // SB_KEYS

        def blk(jb):
            return pl.ds(pl.multiple_of(jb * SB_KEYS, SB_KEYS), SB_KEYS)

        def scores(jb):
            kb, vb = k_ref[blk(jb), :], v_ref[blk(jb), :]
            return tuple((_dot(qhs[hh], kb, 1, 1), _dot(dohs[hh], vb, 1, 1)) for hh in range(2))

        def products(jb, dqs, pend):
            kb = k_ref[blk(jb), :]
            dk = _dot(pend[0][0], qhs[0], 0, 0) + _dot(pend[1][0], qhs[1], 0, 0)
            dv = _dot(pend[0][1], dohs[0], 0, 0) + _dot(pend[1][1], dohs[1], 0, 0)
            dk_acc[blk(jb), :] += dk
            dv_acc[blk(jb), :] += dv
            return tuple(dqs[hh] + _dot(pend[hh][0], kb) for hh in range(2))

        def chain(jb, zs, sums, masked):
            mask = (jb * SB_KEYS + colid) < rowpos if masked else None
            pend, new_sums = [], []
            for hh in range(2):
                z, da = zs[hh]
                csum, prun = sums[hh]
                lb = _log_sigmoid(z)
                lk = lb - z
                if masked:
                    lk = jnp.where(mask, lk, 0.0)
                lk_hi, lk_lo = _split2(lk)
                cs = _dot(lk_hi, c_upto) + _dot(lk_lo, c_upto)
                a = jnp.exp(lb + ((lts[hh] - csum) - cs))
                if masked:
                    a = jnp.where(mask, a, 0.0)
                e = a * da
                e_hi, e_lo = _split2(e)
                ce = _dot(e_hi, c_before) + _dot(e_lo, c_before)
                beta = jnp.exp(lb)
                dz = e * (1.0 - beta) - (prun + ce) * beta
                if masked:
                    dz = jnp.where(mask, dz, 0.0)
                pend.append((dz.astype(BF16), a.astype(BF16)))
                new_sums.append((csum + jnp.sum(lk, axis=1, keepdims=True), prun + jnp.sum(e, axis=1, keepdims=True)))
            return tuple(pend), tuple(new_sums)

        zero = jnp.zeros((tq, LANES), F32)
        zero_b = jnp.zeros((tq, SB_KEYS), BF16)
        nfull = i * per_q
        last = nfull + per_q - 1

        def full_block(jb, carry):
            zs, pend, sums, dqs = carry
            zs_next = scores(jb + 1)
            dqs = products(jnp.maximum(jb - 1, 0), dqs, pend)
            pend, sums = chain(jb, zs, sums, False)
            return zs_next, pend, sums, dqs

        zero_col = jnp.zeros((tq, 1), F32)
        carry = (scores(0), ((zero_b, zero_b),) * 2, ((zero_col, zero_col),) * 2, (zero, zero))
        zs, pend, sums, dqs = lax.fori_loop(0, nfull, full_block, carry)
        for dblk in range(per_q):
            jb = nfull + dblk
            zs_next = scores(jnp.minimum(jb + 1, last))
            dqs = products(jnp.maximum(jb - 1, 0), dqs, pend)
            pend, sums = chain(jb, zs, sums, True)
            zs = zs_next
        dqs = products(last, dqs, pend)
        dq_ref[...] = (jnp.where(hi_lanes, dqs[1], dqs[0]) * scale).astype(BF16)

        @pl.when(i == nq - 1)
        def _():
            dk_ref[...] = dk_acc[...].astype(BF16)
            dv_ref[...] = dv_acc[...].astype(BF16)

        pl.when((pair == npair - 1) & (i == nq - 1))(finish)

    blk = pl.BlockSpec((tq, LANES), lambda p, i: (i, p))
    full = pl.BlockSpec((t, LANES), lambda p, i: (0, p))
    return pl.pallas_call(
        body, name=name, grid=(npair, nq),
        in_specs=[blk,
                  pl.BlockSpec((t, LANES), lambda p, i: (0, npair + p)),
                  pl.BlockSpec((t, LANES), lambda p, i: (0, 2 * npair + p)),
                  pl.BlockSpec((tq, LANES), lambda p, i: (i, npair + p)),
                  blk,
                  pl.BlockSpec((2, SB_KEYS, SB_KEYS), lambda p, i: (0, 0, 0)),
                  HBM_SPEC],
        out_specs=[blk, full, full, HBM_SPEC],
        out_shape=[jax.ShapeDtypeStruct((t, npair * LANES), BF16)] * 3 + [jax.ShapeDtypeStruct(chip_part.shape, chip_part.dtype)],
        scratch_shapes=[pltpu.VMEM((t, LANES), F32), pltpu.VMEM((t, LANES), F32)] + _chip_exchange_scratch(),
        compiler_params=_cp(dimension_semantics=("arbitrary", "arbitrary")))(pb, pb, pb, dy, ltot, cmats, chip_part)


def _hgrn_consts():
    t = lax.broadcasted_iota(jnp.int32, (CHUNK, CHUNK), 0)
    s = lax.broadcasted_iota(jnp.int32, (CHUNK, CHUNK), 1)
    tri = (s <= t)
    cum = [tri.astype(F32)]
    masks = []
    for lvl in range(N_LEVELS):
        half = CHUNK >> (lvl + 1)
        ref_row = (t // (2 * half)) * (2 * half) + half - 1
        cum.append((s <= ref_row).astype(F32))
        same = (t // (2 * half)) == (s // (2 * half))
        masks.append((same & (t % (2 * half) >= half) & (s % (2 * half) < half)).astype(F32))
    masks.append((t == s).astype(F32))
    cum_all = jnp.concatenate(cum, axis=0).astype(BF16)
    suffix = (s >= t).astype(BF16)
    return cum_all, jnp.stack(masks), suffix


def _hgrn_gates(qr, fr, lbv):
    sg = _sigmoid(fr)
    fval = lbv + (1.0 - lbv) * sg
    kk = (1.0 - lbv) * _sigmoid(-fr)
    sq = _sigmoid(qr)
    return sg, fval, jnp.log(fval), kk, sq, qr * sq


def _lower_bound(c_ref):
    c = c_ref[...]
    mx = jnp.max(c, axis=0, keepdims=True)
    ex = jnp.exp(c - mx)
    return ex[1:2, :] / jnp.sum(ex, axis=0, keepdims=True)


def _hgrn_levels(ball, qs, kk):
    b = ball[:CHUNK]
    out = []
    for lvl in range(N_LEVELS):
        bref = ball[(lvl + 1) * CHUNK:(lvl + 2) * CHUNK]
        eq = jnp.exp(jnp.minimum(b - bref, 0.0))
        ek = jnp.exp(jnp.minimum(bref - b, 0.0))
        out.append((qs * eq, kk * ek, eq, ek))
    out.append((qs, kk, None, None))
    return out


def _split2(x):
    hi = x.astype(BF16)
    return hi, (x - hi.astype(F32)).astype(BF16)


def _hgrn_fwd(pc, c_lb, out_norm, *, name, tc=512):
    t = pc.shape[0]
    nh = pc.shape[1] // 4 // LANES
    tc = min(tc, t)
    nch = tc // CHUNK
    cum_all, masks, _ = _hgrn_consts()

    def body(q_ref, f_ref, i_ref, g_ref, lb_ref, on_ref, cum_ref, m_ref, y_ref, o_ref, st_ref, state):
        @pl.when(pl.program_id(1) == 0)
        def _():
            state[...] = jnp.zeros_like(state)

        lbv = _lower_bound(lb_ref)
        onv = on_ref[...]

        def chunk(c, carry):
            rows = pl.ds(pl.multiple_of(c * CHUNK, CHUNK), CHUNK)
            _, _, g, kk, _, qs = _hgrn_gates(q_ref[rows, :], f_ref[rows, :], lbv)
            vb = i_ref[rows, :].astype(BF16)
            ball = _dot_exact_lhs(cum_ref[...], g)
            b = ball[:CHUNK]
            scores = jnp.zeros((CHUNK, CHUNK), F32)
            for lvl, (ql, kl, _, _) in enumerate(_hgrn_levels(ball, qs, kk)):
                scores = scores + _dot(ql.astype(BF16), kl.astype(BF16), 1, 1) * m_ref[lvl]
            st = state[...]
            st_ref[c] = st
            o = _dot(scores.astype(BF16), vb) + _dot((qs * jnp.exp(b)).astype(BF16), st.astype(BF16), 1, 1)
            blast = b[CHUNK - 1:CHUNK, :]
            kdec = (kk * jnp.exp(blast - b)).astype(BF16)
            state[...] = st * jnp.exp(blast) + _dot(vb, kdec, 0, 0)
            o_ref[rows, :] = o
            rstd = lax.rsqrt(jnp.mean(o * o, axis=-1, keepdims=True) + RMS_EPS)
            gate = g_ref[rows, :]
            y_ref[rows, :] = (o * rstd * onv * (gate * _sigmoid(gate))).astype(BF16)
            return carry

        lax.fori_loop(0, nch, chunk, 0, unroll=2)

    def col(off):
        return pl.BlockSpec((tc, LANES), lambda h, i: (i, off + h))

    osp = pl.BlockSpec((tc, LANES), lambda h, i: (i, h))
    return pl.pallas_call(
        body, name=name, grid=(nh, t // tc),
        in_specs=[col(0), col(nh), col(2 * nh), col(3 * nh),
                  pl.BlockSpec((2, LANES), lambda h, i: (0, h)),
                  pl.BlockSpec((1, LANES), lambda h, i: (0, 0)),
                  pl.BlockSpec(cum_all.shape, lambda h, i: (0, 0)),
                  pl.BlockSpec(masks.shape, lambda h, i: (0, 0, 0))],
        out_specs=[osp, osp, pl.BlockSpec((None, nch, LANES, LANES), lambda h, i: (h, i, 0, 0))],
        out_shape=[jax.ShapeDtypeStruct((t, nh * LANES), BF16), jax.ShapeDtypeStruct((t, nh * LANES), F32),
                   jax.ShapeDtypeStruct((nh, t // CHUNK, LANES, LANES), F32)],
        scratch_shapes=[pltpu.VMEM((LANES, LANES), F32)],
        compiler_params=_cp())(pc, pc, pc, pc, c_lb, out_norm, cum_all, masks)


def _hgrn_bwd(pc, o_saved, states, dy, c_lb, out_norm, *, name, tc=512):
    t = pc.shape[0]
    nh = pc.shape[1] // 4 // LANES
    tc = min(tc, t)
    nch = tc // CHUNK
    nt = t // tc
    cum_all, masks, suffix = _hgrn_consts()

    def body(q_ref, f_ref, i_ref, g_ref, o_ref, st_ref, dy_ref, lb_ref, on_ref, cum_ref, m_ref, suf_ref,
             dq_ref, df_ref, di_ref, dg_ref, dlb_ref, don_ref, dstate):
        @pl.when(pl.program_id(1) == 0)
        def _():
            dstate[...] = jnp.zeros_like(dstate)
            dlb_ref[...] = jnp.zeros_like(dlb_ref)
            don_ref[...] = jnp.zeros_like(don_ref)

        lbv = _lower_bound(lb_ref)
        onv = on_ref[...]

        def chunk(n, carry):
            c = nch - 1 - n
            rows = pl.ds(pl.multiple_of(c * CHUNK, CHUNK), CHUNK)
            qr = q_ref[rows, :]
            sg, fval, g, kk, sq, qs = _hgrn_gates(qr, f_ref[rows, :], lbv)
            vb = i_ref[rows, :].astype(BF16)
            o = o_ref[rows, :]
            gate = g_ref[rows, :]
            sgt = _sigmoid(gate)
            rstd = lax.rsqrt(jnp.mean(o * o, axis=-1, keepdims=True) + RMS_EPS)
            ohat = o * rstd
            dyv = dy_ref[rows, :]
            don = dyv * (gate * sgt)
            dg_ref[rows, :] = (dyv * ohat * onv * (sgt * (1.0 + gate * (1.0 - sgt)))).astype(BF16)
            don_ref[...] += jnp.sum(don * ohat, axis=0, keepdims=True)
            dxhat = don * onv
            dob = (rstd * (dxhat - ohat * jnp.mean(dxhat * ohat, axis=-1, keepdims=True))).astype(BF16)
            ball = _dot_exact_lhs(cum_ref[...], g)
            b = ball[:CHUNK]
            blast = b[CHUNK - 1:CHUNK, :]
            eb = jnp.exp(b)
            edec = jnp.exp(blast - b)
            st32 = st_ref[c]
            st = st32.astype(BF16)
            dst = dstate[...]
            dstb = dst.astype(BF16)
            da = _dot(dob, vb, 1, 1)
            levels = _hgrn_levels(ball, qs, kk)
            scores = jnp.zeros((CHUNK, CHUNK), F32)
            dq = eb * _dot(dob, st)
            dk_inter = edec * _dot(vb, dstb)
            dk = dk_inter
            for lvl, (ql, kl, eq, ek) in enumerate(levels):
                mk = m_ref[lvl]
                (qh, qlo), (kh, klo) = _split2(ql), _split2(kl)
                scores = scores + _dot(qh, kh, 1, 1) * mk
                dal = (da * mk).astype(BF16)
                dql = _dot(dal, kh) + _dot(dal, klo)
                dkl = _dot(dal, qh, 0, 0) + _dot(dal, qlo, 0, 0)
                dq = dq + (dql if eq is None else dql * eq)
                dk = dk + (dkl if ek is None else dkl * ek)
            kdec = (kk * edec).astype(BF16)
            dv = _dot(scores.astype(BF16), dob, 0, 0) + _dot(kdec, dstb, 1, 1)
            dstate[...] = dst * jnp.exp(blast) + _dot(dob, (qs * eb).astype(BF16), 0, 0)
            db = qs * dq - kk * dk
            last = jnp.sum(kk * dk_inter, axis=0, keepdims=True) + jnp.exp(blast) * jnp.sum(dst * st32, axis=0, keepdims=True)
            dgl = _dot_exact_lhs(suf_ref[...], db) + last
            dfv = dgl / fval - dk
            df_ref[rows, :] = (dfv * (1.0 - lbv) * sg * (1.0 - sg)).astype(BF16)
            dlb_ref[...] += jnp.sum(dfv * (1.0 - sg), axis=0, keepdims=True)
            dq_ref[rows, :] = (dq * (sq * (1.0 + qr * (1.0 - sq)))).astype(BF16)
            di_ref[rows, :] = dv.astype(BF16)
            return carry

        lax.fori_loop(0, nch, chunk, 0, unroll=2)

    def col(off):
        return pl.BlockSpec((tc, LANES), lambda h, i: (nt - 1 - i, off + h))

    osp = pl.BlockSpec((tc, LANES), lambda h, i: (nt - 1 - i, h))
    vec = pl.BlockSpec((1, LANES), lambda h, i: (0, h))
    return pl.pallas_call(
        body, name=name, grid=(nh, nt),
        in_specs=[col(0), col(nh), col(2 * nh), col(3 * nh), osp,
                  pl.BlockSpec((None, nch, LANES, LANES), lambda h, i: (h, nt - 1 - i, 0, 0)),
                  osp,
                  pl.BlockSpec((2, LANES), lambda h, i: (0, h)),
                  pl.BlockSpec((1, LANES), lambda h, i: (0, 0)),
                  pl.BlockSpec(cum_all.shape, lambda h, i: (0, 0)),
                  pl.BlockSpec(masks.shape, lambda h, i: (0, 0, 0)),
                  pl.BlockSpec(suffix.shape, lambda h, i: (0, 0))],
        out_specs=[osp, osp, osp, osp, vec, vec],
        out_shape=[jax.ShapeDtypeStruct((t, nh * LANES), BF16)] * 4 + [jax.ShapeDtypeStruct((1, nh * LANES), F32)] * 2,
        scratch_shapes=[pltpu.VMEM((LANES, LANES), F32)],
        compiler_params=_cp())(pc, pc, pc, pc, o_saved, states, dy, c_lb, out_norm, cum_all, masks, suffix)


HBM_SPEC = pl.BlockSpec(memory_space=pltpu.HBM)


def _gather_scratch():
    return [pltpu.SemaphoreType.DMA((7,)), pltpu.SemaphoreType.DMA((7,)), pltpu.SemaphoreType.DMA]


def _gather_phases(x_ref, out_refs, seg_rows, send_sems, recv_sems, local_sem):
    x, y, c = lax.axis_index("x"), lax.axis_index("y"), lax.axis_index("c")
    me, sibling = (x, y, c), (x, y, 1 - c)
    chips = [(1 - x, y), (x, 1 - y), (1 - x, 1 - y)]
    offs = [sum(seg_rows[:s]) for s in range(len(seg_rows))]
    assert sum(seg_rows) == x_ref.shape[0]

    def index(px, py, pc):
        return 4 * px + 2 * py + pc

    def copies(k, block, to, own):
        return [pltpu.make_async_remote_copy(
            src_ref=x_ref.at[pl.ds(offs[s], n)] if own else out_refs[s].at[index(*block)],
            dst_ref=out_refs[s].at[index(*block)],
            send_sem=send_sems.at[k], recv_sem=recv_sems.at[k], device_id=to, device_id_type=MESH)
            for s, n in enumerate(seg_rows)]

    def all_bytes(k):
        return pltpu.make_async_remote_copy(src_ref=x_ref, dst_ref=x_ref, send_sem=send_sems.at[k],
                                            recv_sem=recv_sems.at[k], device_id=me, device_id_type=MESH)

    mine = [pltpu.make_async_copy(x_ref.at[pl.ds(offs[s], n)], out_refs[s].at[index(*me)], local_sem)
            for s, n in enumerate(seg_rows)]
    first = copies(0, me, sibling, True)
    for j, chip in enumerate(chips):
        first += copies(1 + j, me, (*chip, c), True)

    def start():
        for cp in mine + first:
            cp.start()

    def forward():
        for j, chip in enumerate(chips):
            all_bytes(1 + j).wait_recv()
            for cp in copies(4 + j, (*chip, c), sibling, False):
                cp.start()

    def finish():
        all_bytes(0).wait_recv()
        for j in range(3):
            all_bytes(4 + j).wait_recv()
        for k in range(7):
            all_bytes(k).wait_send()
        pltpu.make_async_copy(x_ref, x_ref, local_sem).wait()

    return start, forward, finish


def _all_gather(xs, seg_rows=None, *, name):
    segs = [xs.shape[0]] if seg_rows is None else list(seg_rows)

    def body(x_ref, *rest):
        start, forward, finish = _gather_phases(x_ref, rest[:len(segs)], segs, *rest[len(segs):])
        start()
        forward()
        finish()

    outs = pl.pallas_call(
        body, name=name, in_specs=[HBM_SPEC], out_specs=[HBM_SPEC] * len(segs),
        out_shape=[jax.ShapeDtypeStruct((8, n, xs.shape[1]), xs.dtype) for n in segs],
        scratch_shapes=_gather_scratch())(xs)
    return outs[0] if seg_rows is None else outs


def _sibling_exchange(s, *, name):
    def body(s_ref, rb_ref, send_sem, recv_sem):
        x, y, c = lax.axis_index("x"), lax.axis_index("y"), lax.axis_index("c")
        cp = pltpu.make_async_remote_copy(
            src_ref=s_ref.at[:, 1 - c], dst_ref=rb_ref, send_sem=send_sem, recv_sem=recv_sem,
            device_id=(x, y, 1 - c), device_id_type=MESH)
        cp.start()
        cp.wait()

    return pl.pallas_call(
        body, name=name, in_specs=[HBM_SPEC], out_specs=HBM_SPEC,
        out_shape=jax.ShapeDtypeStruct(s.shape[:1] + s.shape[2:], s.dtype),
        scratch_shapes=[pltpu.SemaphoreType.DMA, pltpu.SemaphoreType.DMA])(s)


def _row_tile(n, cap=1024):
    return max(b for b in range(16, cap + 1, 16) if n % b == 0)


def _pair_add(s, rb, core, *, name):
    nchip, _, r, c = s.shape
    tb = _row_tile(r)

    def body(core_ref, a_ref, b_ref, o_ref):
        o_ref[...] = (a_ref[...].astype(F32) + b_ref[...].astype(F32)).astype(BF16)

    blk = pl.BlockSpec((None, tb, c), lambda ch, i, cr: (ch, i, 0))
    return pl.pallas_call(
        body, name=name,
        grid_spec=pltpu.PrefetchScalarGridSpec(
            num_scalar_prefetch=1, grid=(nchip, r // tb),
            in_specs=[pl.BlockSpec((None, None, tb, c), lambda ch, i, cr: (ch, cr[0], i, 0)), blk],
            out_specs=blk),
        out_shape=jax.ShapeDtypeStruct((nchip, r, c), BF16), compiler_params=_cp())(core, s, rb)


def _chip_exchange_scratch():
    return [pltpu.SemaphoreType.DMA((3,)), pltpu.SemaphoreType.DMA((3,)), pltpu.SemaphoreType.DMA]


def _chip_exchange_phases(p_ref, out_ref, send_sems, recv_sems, local_sem):
    x, y, c = lax.axis_index("x"), lax.axis_index("y"), lax.axis_index("c")
    mine = 2 * x + y
    own = pltpu.make_async_copy(p_ref.at[mine], out_ref.at[mine], local_sem)
    copies = [pltpu.make_async_remote_copy(
        src_ref=p_ref.at[2 * tx + ty], dst_ref=out_ref.at[mine],
        send_sem=send_sems.at[k], recv_sem=recv_sems.at[k], device_id=(tx, ty, c), device_id_type=MESH)
        for k, (tx, ty) in enumerate([(1 - x, y), (x, 1 - y), (1 - x, 1 - y)])]

    def start():
        own.start()
        for cp in copies:
            cp.start()

    def finish():
        for cp in copies:
            cp.wait()
        own.wait()

    return start, finish


def _adamw_math(w, g, m, v):
    m2 = ADAM_B1 * m + (1.0 - ADAM_B1) * g
    v2 = ADAM_B2 * v + (1.0 - ADAM_B2) * (g * g)
    m_hat = m2 / (1.0 - ADAM_B1 ** ADAM_STEP)
    v_hat = v2 / (1.0 - ADAM_B2 ** ADAM_STEP)
    return -ADAM_LR * (m_hat / (jnp.sqrt(v_hat) + ADAM_EPS) + ADAM_WD * w), m2, v2


def _grad_sum(parts, *, name):
    _, r, c = parts.shape
    tb = _row_tile(r)

    def body(p0, p1, p2, p3, g_out):
        g_out[...] = ((p0[...].astype(F32) + p1[...].astype(F32)) + p2[...].astype(F32)) + p3[...].astype(F32)

    def part(ch):
        return pl.BlockSpec((None, tb, c), lambda i: (ch, i, 0))

    return pl.pallas_call(
        body, name=name, grid=(r // tb,), in_specs=[part(0), part(1), part(2), part(3)],
        out_specs=pl.BlockSpec((tb, c), lambda i: (i, 0)), out_shape=jax.ShapeDtypeStruct((r, c), F32),
        compiler_params=_cp())(parts, parts, parts, parts)


def _adamw_shard(g, g_off, w, m, v, layer, prev, *, name):
    _, r, c = w.shape
    tb = next(b for b in range(min(r, 512), 0, -8) if r % b == 0 and g_off % b == 0)

    def body(g_ref, w_ref, m_ref, v_ref, *rest):
        d_out, m_out, v_out = rest[-3:]
        d, m2, v2 = _adamw_math(w_ref[...], g_ref[...], m_ref[...], v_ref[...])
        d_out[...] = d
        m_out[...] = m2
        v_out[...] = v2

    blk = pl.BlockSpec((None, tb, c), lambda i: (layer, i, 0))
    prev = list(prev) if prev is not None else []
    return pl.pallas_call(
        body, name=name, grid=(r // tb,),
        in_specs=[pl.BlockSpec((tb, c), lambda i: (g_off // tb + i, 0)), blk, blk, blk] + [pl.BlockSpec(memory_space=pl.ANY)] * len(prev),
        out_specs=[blk] * 3, out_shape=[jax.ShapeDtypeStruct(w.shape, F32)] * 3,
        input_output_aliases={4 + k: k for k in range(len(prev))},
        compiler_params=_cp())(g, w, m, v, *prev)


SLOT = 8
SMALL_ROWS = 6 * SLOT
ROW_LB = 4 * SLOT


def _small_update(gath, w, m, v, *, name):
    def body(g_ref, w_ref, m_ref, v_ref, g_out, d_out, m_out, v_out):
        tot = g_ref[0]
        for k in range(1, 8):
            tot = tot + g_ref[k]
        wv = w_ref[...]
        c0, c1 = wv[ROW_LB:ROW_LB + 1, :], wv[ROW_LB + 1:ROW_LB + 2, :]
        mx = jnp.maximum(c0, c1)
        e0, e1 = jnp.exp(c0 - mx), jnp.exp(c1 - mx)
        lb = e1 / (e0 + e1)
        gl = tot[ROW_LB:ROW_LB + 1, :] * lb * (1.0 - lb)
        row = lax.broadcasted_iota(jnp.int32, tot.shape, 0)
        g = jnp.where(row == ROW_LB, -gl, jnp.where(row == ROW_LB + 1, gl, tot))
        d, m2, v2 = _adamw_math(wv, g, m_ref[...], v_ref[...])
        g_out[...] = g
        d_out[...] = d
        m_out[...] = m2
        v_out[...] = v2

    return pl.pallas_call(
        body, name=name, out_shape=[jax.ShapeDtypeStruct(w.shape, F32)] * 4, compiler_params=_cp())(gath, w, m, v)


D_MODEL = 1024


def _ffn_fwd(h, gain, wg, wu, wd, tag):
    xn, gg, uu, act = _norm_gate_up(h, gain, wg, wu, name=f"{tag}_gate_up")
    out = _mm([(act, wd)], residual=h, alpha=MACARON, tn=1024, name=f"{tag}_down")
    return out, (h, xn, gg, uu, act)


def _ffn_input_bwd(dg, du, wg, wu, x, gain, dres, chip_part, *, name, scale, tm=256):
    t, d = x.shape
    f = wg.shape[0]
    tm = min(tm, t)
    nt = t // tm
    fused = chip_part is not None

    def body(dg_ref, du_ref, wg_ref, wu_ref, x_ref, g_ref, dres_ref, *rest):
        if fused:
            part_ref, dx_ref, dxb_ref, dgain_ref, parts_ref = rest[:5]
            start, finish = _chip_exchange_phases(part_ref, parts_ref, *rest[5:])
            pl.when(pl.program_id(0) == 0)(start)
        else:
            dx_ref, dxb_ref, dgain_ref = rest
        dxn_v = _dot(dg_ref[...], wg_ref[...]) + _dot(du_ref[...], wu_ref[...])
        xv = x_ref[...]
        rstd = lax.rsqrt(jnp.mean(xv * xv, axis=-1, keepdims=True) + RMS_EPS)
        xhat = xv * rstd
        dxhat = dxn_v * g_ref[...]
        dx = dres_ref[...] + rstd * (dxhat - xhat * jnp.mean(dxhat * xhat, axis=-1, keepdims=True))
        dx_ref[...] = dx
        dxb_ref[...] = (dx * scale).astype(BF16)

        @pl.when(pl.program_id(0) == 0)
        def _():
            dgain_ref[...] = jnp.zeros_like(dgain_ref)

        dgain_ref[...] += jnp.sum(dxn_v * xhat, axis=0, keepdims=True)
        if fused:
            pl.when(pl.program_id(0) == nt - 1)(finish)

    wide = pl.BlockSpec((tm, f), lambda i: (i, 0))
    wsp = pl.BlockSpec((f, d), lambda i: (0, 0))
    row = pl.BlockSpec((tm, d), lambda i: (i, 0))
    vec = pl.BlockSpec((1, d), lambda i: (0, 0))
    args = [dg, du, wg, wu, x, gain, dres] + ([chip_part] if fused else [])
    return pl.pallas_call(
        body, name=name, grid=(nt,),
        in_specs=[wide, wide, wsp, wsp, row, vec, row] + ([HBM_SPEC] if fused else []),
        out_specs=[row, row, vec] + ([HBM_SPEC] if fused else []),
        out_shape=[jax.ShapeDtypeStruct((t, d), F32), jax.ShapeDtypeStruct((t, d), BF16), jax.ShapeDtypeStruct((1, d), F32)]
        + ([jax.ShapeDtypeStruct(chip_part.shape, chip_part.dtype)] if fused else []),
        scratch_shapes=_chip_exchange_scratch() if fused else [],
        compiler_params=_cp(dimension_semantics=("arbitrary",)))(*args)


def _ffn_bwd(dout, dout_half, saved, gain, wg, wu, wd, tag, next_scale, make_chip_part=None):
    h, xn, gg, uu, act = saved
    dg, du = _swiglu_bwd(dout_half, wd, gg, uu, name=f"{tag}_dact")
    dwd = _mm([(act, dout_half)], ta=True, tm=256, tn=1024, out_dtype=BF16, name=f"{tag}_dwd")
    dwg = _mm([(dg, xn)], ta=True, tm=256, tn=1024, out_dtype=BF16, name=f"{tag}_dwg")
    dwu = _mm([(du, xn)], ta=True, tm=256, tn=1024, out_dtype=BF16, name=f"{tag}_dwu")
    chip_part = make_chip_part(dwg, dwu, dwd) if make_chip_part is not None else None
    dh, dh_b, dgain, *parts = _ffn_input_bwd(dg, du, wg, wu, h, gain, dout, chip_part, scale=next_scale,
                                             name=f"{tag}_input_bwd")
    return dh, dh_b, dwg, dwu, dwd, dgain, (parts[0] if parts else None)


def kernel(x, ffn_pre_norm, ffn_pre_w_gate, ffn_pre_w_up, ffn_pre_w_down, mix_norm, ffn_post_norm, ffn_post_w_gate, ffn_post_w_up, ffn_post_w_down, ab_w_in, ab_conv_w, ab_w_out, c_w_in, c_lower_bounds, c_out_norm, c_w_out, final_norm, loss_target, m_ffn_pre_norm, m_ffn_pre_w_gate, m_ffn_pre_w_up, m_ffn_pre_w_down, m_mix_norm, m_ffn_post_norm, m_ffn_post_w_gate, m_ffn_post_w_up, m_ffn_post_w_down, m_ab_w_in, m_ab_conv_w, m_ab_w_out, m_c_w_in, m_c_lower_bounds, m_c_out_norm, m_c_w_out, m_final_norm, v_ffn_pre_norm, v_ffn_pre_w_gate, v_ffn_pre_w_up, v_ffn_pre_w_down, v_mix_norm, v_ffn_post_norm, v_ffn_post_w_gate, v_ffn_post_w_up, v_ffn_post_w_down, v_ab_w_in, v_ab_conv_w, v_ab_w_out, v_c_w_in, v_c_lower_bounds, v_c_out_norm, v_c_w_out, v_final_norm):
    d = D_MODEL
    h0 = x[0]
    target = loss_target[0]
    core = lax.axis_index("c").astype(jnp.int32).reshape(1)

    big = [("pre_g", ffn_pre_w_gate, m_ffn_pre_w_gate, v_ffn_pre_w_gate),
           ("pre_u", ffn_pre_w_up, m_ffn_pre_w_up, v_ffn_pre_w_up),
           ("pre_d", ffn_pre_w_down, m_ffn_pre_w_down, v_ffn_pre_w_down),
           ("post_g", ffn_post_w_gate, m_ffn_post_w_gate, v_ffn_post_w_gate),
           ("post_u", ffn_post_w_up, m_ffn_post_w_up, v_ffn_post_w_up),
           ("post_d", ffn_post_w_down, m_ffn_post_w_down, v_ffn_post_w_down),
           ("ab_in", ab_w_in, m_ab_w_in, v_ab_w_in),
           ("ab_out", ab_w_out, m_ab_w_out, v_ab_w_out),
           ("c_in", c_w_in, m_c_w_in, v_c_w_in),
           ("c_out", c_w_out, m_c_w_out, v_c_w_out)]
    by_tag = {tag: (w, m, v) for tag, w, m, v in big}

    def layer_rows(tag):
        w = by_tag[tag][0]
        return w.size // d // w.shape[0]

    def layout(items):
        offs, off = {}, 0
        for item in items:
            offs[item] = off
            off += layer_rows(item[0])
        return offs, off

    ffn = [f"{pos}_{kind}" for pos in ("pre", "post") for kind in "gud"]
    early_items = [("pre_g", 0), ("pre_u", 0), ("pre_d", 0), ("ab_in", 0)]
    late_items = ([("pre_g", 1), ("pre_u", 1), ("pre_d", 1)] + [(f"post_{kind}", l) for l in (0, 1) for kind in "gud"]
                  + [("ab_out", 0), ("c_in", 0), ("c_out", 0)])
    grad_items = {"A": ([(tag, 1) for tag in ffn] + [(f"post_{kind}", 0) for kind in "gud"]
                        + [("c_in", 0), ("c_out", 0), ("ab_out", 0)]),
                  "B": [(f"pre_{kind}", 0) for kind in "gud"] + [("ab_in", 0)]}
    grad_offs = {k: layout(items)[0] for k, items in grad_items.items()}
    grad_conv_row = layout(grad_items["B"])[1]

    def conv_rows(a, split):
        flat = a.reshape(-1)
        if split:
            hi = flat.astype(BF16)
            flat = jnp.concatenate([hi, (flat - hi.astype(F32)).astype(BF16)])
        return jnp.zeros((16, d), flat.dtype).at[0, :flat.shape[0]].set(flat)

    nconv = ab_conv_w.size
    col_sharded = {"pre_g", "pre_u", "post_g", "post_u", "ab_in", "c_in"}

    def pack_rows(item):
        tag, layer = item
        a = by_tag[tag][0][layer]
        return (a.T if tag in col_sharded else a).reshape(-1, d).astype(BF16)

    early_pack = jnp.concatenate([pack_rows(item) for item in early_items] + [conv_rows(ab_conv_w, True)], axis=0)
    late_pack = jnp.concatenate([pack_rows(item) for item in late_items], axis=0)
    early_w = _all_gather(early_pack, [layer_rows(tag) for tag, _ in early_items] + [16], name="gather_early_weights")
    full = {item: g.reshape(-1, d) for item, g in zip(early_items, early_w)}
    ffn_w = {("pre", 0): tuple(full[f"pre_{kind}", 0] for kind in "gud")}
    w_ab_in = full["ab_in", 0]
    cg = early_w[-1][:, 0, :2 * nconv].astype(F32)
    conv_w = (cg[:, :nconv] + cg[:, nconv:]).reshape(8, 3, -1).transpose(1, 0, 2).reshape(3, -1)
    half = w_ab_in.shape[0] // 2
    w_a_in, w_b_in = w_ab_in[:half], w_ab_in[half:]
    aw = half // 3

    h1, s_pre0 = _ffn_fwd(h0, ffn_pre_norm[0:1], *ffn_w["pre", 0], "l0pre")
    hn0 = _rmsnorm_fwd(h1, mix_norm[0:1], name="l0_mix_norm")
    pa = _mm([(hn0, w_a_in)], tb=True, tn=1536, name="ab_proj_a")
    pb = _mm([(hn0, w_b_in)], tb=True, tn=1536, out_dtype=BF16, name="ab_proj_b")
    ya = _conv_fwd(pa, conv_w, name="conv_fwd")
    yb, ltot, *late_w = _attn_fwd(pb, late_pack, [layer_rows(tag) for tag, _ in late_items],
                                  name="attn_fwd_gather_late_weights")
    full.update({item: g.reshape(-1, d) for item, g in zip(late_items, late_w)})
    for pos, layer in (("post", 0), ("pre", 1), ("post", 1)):
        ffn_w[pos, layer] = tuple(full[f"{pos}_{kind}", layer] for kind in "gud")
    w_ab_out, w_c_in, w_c_out = full["ab_out", 0], full["c_in", 0], full["c_out", 0]
    h2 = _mm([(ya, w_ab_out[:aw]), (yb, w_ab_out[aw:])], residual=h1, tn=1024, name="ab_out")
    h3, s_post0 = _ffn_fwd(h2, ffn_post_norm[0:1], *ffn_w["post", 0], "l0post")
    h4, s_pre1 = _ffn_fwd(h3, ffn_pre_norm[1:2], *ffn_w["pre", 1], "l1pre")
    hn1 = _rmsnorm_fwd(h4, mix_norm[1:2], name="l1_mix_norm")
    pc = _mm([(hn1, w_c_in)], tb=True, tm=256, tn=4096, name="c_proj")
    yc, o_saved, states = _hgrn_fwd(pc, c_lower_bounds, c_out_norm, name="hgrn_fwd")
    h5 = _mm([(yc, w_c_out)], residual=h4, tn=1024, name="c_out")
    h6, s_post1 = _ffn_fwd(h5, ffn_post_norm[1:2], *ffn_w["post", 1], "l1post")
    dh6, dh6_b, d_final, loss_vec = _loss_head(h6, final_norm.reshape(1, d), target, name="loss_head")

    gw = {}
    dh5, dh5_b, gw["post_g", 1], gw["post_u", 1], gw["post_d", 1], d_post1, _ = _ffn_bwd(
        dh6, dh6_b, s_post1, ffn_post_norm[1:2], *ffn_w["post", 1], "l1post", 1.0)
    dyc = _mm([(dh5_b, w_c_out)], tb=True, tn=1024, name="c_out_dy")
    g_c_out = _mm([(yc, dh5_b)], ta=True, tm=256, tn=1024, out_dtype=BF16, name="c_out_dw")
    dcq, dcf, dci, dcg, dlb, d_onorm = _hgrn_bwd(pc, o_saved, states, dyc, c_lower_bounds, c_out_norm, name="hgrn_bwd")
    dparts = [dcq, dcf, dci, dcg]
    g_c_in = jnp.concatenate(
        [_mm([(dp, hn1)], ta=True, tm=256, tn=1024, out_dtype=BF16, name=f"c_in_dw{i}") for i, dp in enumerate(dparts)],
        axis=0)
    cw = w_c_in.shape[0] // 4
    dhn1 = _mm([(dp, w_c_in[i * cw:(i + 1) * cw]) for i, dp in enumerate(dparts)], tm=512, tn=1024, name="c_in_dx")
    dh4, dh4_b, d_mix1 = _rmsnorm_bwd(h4, mix_norm[1:2], dhn1, dh5, scale=MACARON, name="l1_mix_norm_bwd")
    dh3, dh3_b, gw["pre_g", 1], gw["pre_u", 1], gw["pre_d", 1], d_pre1, _ = _ffn_bwd(
        dh4, dh4_b, s_pre1, ffn_pre_norm[1:2], *ffn_w["pre", 1], "l1pre", MACARON)
    dh2, dh2_b, gw["post_g", 0], gw["post_u", 0], gw["post_d", 0], d_post0, _ = _ffn_bwd(
        dh3, dh3_b, s_post0, ffn_post_norm[0:1], *ffn_w["post", 0], "l0post", 1.0)
    dyab = _mm([(dh2_b, w_ab_out)], tb=True, tn=1024, name="ab_out_dy")
    g_ab_out = jnp.concatenate([_mm([(ya, dh2_b)], ta=True, tm=256, tn=1024, out_dtype=BF16, name="ab_out_dw_a"),
                                _mm([(yb, dh2_b)], ta=True, tm=256, tn=1024, out_dtype=BF16, name="ab_out_dw_b")], axis=0)
    dab, dac, dax, g_conv = _conv_bwd(pa, dyab, conv_w, name="conv_bwd")

    def chip_partials(key, grads, extra=()):
        gpack = jnp.concatenate([grads[item].reshape(8, -1, d) for item in grad_items[key]] + list(extra), axis=1)
        send = gpack.reshape(4, 2, gpack.shape[1], d)
        from_sibling = _sibling_exchange(send, name=f"grad{key}_sibling_exchange")
        return _pair_add(send, from_sibling, core, name=f"grad{key}_pair_add")

    gw["c_in", 0], gw["c_out", 0], gw["ab_out", 0] = g_c_in, g_c_out, g_ab_out
    chip_part_a = chip_partials("A", gw)
    dq, dk, dv, parts_a = _attn_bwd(pb, dyab, ltot, chip_part_a, name="attn_bwd_exchange_grads_a")
    dparts = [dab, dac, dax, dq, dk, dv]
    g_ab_in = jnp.concatenate(
        [_mm([(dp, hn0)], ta=True, tm=256, tn=1024, out_dtype=BF16, name=f"ab_in_dw{i}") for i, dp in enumerate(dparts)],
        axis=0)
    dhn0 = _mm([(dp, w_ab_in[i * aw:(i + 1) * aw]) for i, dp in enumerate(dparts)], tm=512, tn=1024, name="ab_in_dx")
    dh1, dh1_b, d_mix0 = _rmsnorm_bwd(h1, mix_norm[0:1], dhn0, dh2, scale=MACARON, name="l0_mix_norm_bwd")
    gw["ab_in", 0] = g_ab_in
    gconv_own = g_conv.reshape(3, 8, -1).transpose(1, 0, 2).reshape(8, -1)
    conv_piece = jnp.zeros((8, 16, d), F32).at[:, 0, :nconv].set(gconv_own).astype(BF16)

    def chip_part_b(dwg, dwu, dwd):
        gw["pre_g", 0], gw["pre_u", 0], gw["pre_d", 0] = dwg, dwu, dwd
        return chip_partials("B", gw, [conv_piece])

    dh0, _, _, _, _, d_pre0, parts_b = _ffn_bwd(
        dh1, dh1_b, s_pre0, ffn_pre_norm[0:1], *ffn_w["pre", 0], "l0pre", 1.0, chip_part_b)

    g_sum = {"A": _grad_sum(parts_a, name="gradA_sum"), "B": _grad_sum(parts_b, name="gradB_sum")}

    upd = {}
    for tag, w, m, v in big:
        nl = layer_rows(tag)
        view = (lambda a: jnp.swapaxes(a, 1, 2)) if tag in col_sharded else (lambda a: a)
        where = {layer: (key, grad_offs[key][tag, layer])
                 for key in ("A", "B") for t2, layer in grad_items[key] if t2 == tag}
        res = None
        for layer in sorted(where):
            key, off = where[layer]
            res = _adamw_shard(g_sum[key], off, view(w), view(m), view(v), layer, res, name=f"adamw_{tag}{layer}")
        g_nat = jnp.stack([g_sum[where[layer][0]][where[layer][1]:where[layer][1] + nl] for layer in sorted(where)])
        upd[tag] = [view(a) for a in [g_nat] + list(res)]
    res = _adamw_shard(g_sum["B"], grad_conv_row, *(conv_rows(a, False)[None] for a in (ab_conv_w, m_ab_conv_w, v_ab_conv_w)),
                       0, None, name="adamw_conv")
    g_conv_rows = g_sum["B"][grad_conv_row:grad_conv_row + 16]
    upd["conv"] = [r[0, :nconv].reshape(ab_conv_w.shape) for r in [g_conv_rows] + [r[0] for r in res]]

    def small_pack(pre, mix, post, final, lbs, onorm):
        def slot(parts):
            out, r = jnp.zeros((SLOT, d), F32), 0
            for a in (parts if isinstance(parts, tuple) else (parts,)):
                out = out.at[r:r + a.shape[0], :a.shape[1]].set(a)
                r += a.shape[0]
            return out

        return jnp.concatenate([slot(pre), slot(mix), slot(post), slot(final.reshape(1, d)), slot(lbs), slot(onorm)], axis=0)

    d_on = d_onorm.reshape(-1, c_out_norm.shape[1]).sum(axis=0, keepdims=True)
    gsmall = small_pack((d_pre0, d_pre1), (d_mix0, d_mix1), (d_post0, d_post1), d_final, dlb, d_on)
    gsmall_all = _all_gather(gsmall, name="gather_small_grads")
    sres = _small_update(
        gsmall_all,
        small_pack(ffn_pre_norm, mix_norm, ffn_post_norm, final_norm, c_lower_bounds, c_out_norm),
        small_pack(m_ffn_pre_norm, m_mix_norm, m_ffn_post_norm, m_final_norm, m_c_lower_bounds, m_c_out_norm),
        small_pack(v_ffn_pre_norm, v_mix_norm, v_ffn_post_norm, v_final_norm, v_c_lower_bounds, v_c_out_norm),
        name="small_update")

    def small_out(r):
        return {"pre_norm": r[0:2], "mix_norm": r[SLOT:SLOT + 2], "post_norm": r[2 * SLOT:2 * SLOT + 2],
                "final": r[3 * SLOT], "lb": r[ROW_LB:ROW_LB + 2], "onorm": r[5 * SLOT:5 * SLOT + 1, :c_out_norm.shape[1]]}

    small = [small_out(r) for r in sres]
    outs = []
    for k in range(4):
        s = small[k]
        outs += [s["pre_norm"], upd["pre_g"][k], upd["pre_u"][k], upd["pre_d"][k], s["mix_norm"], s["post_norm"],
                 upd["post_g"][k], upd["post_u"][k], upd["post_d"][k], upd["ab_in"][k], upd["conv"][k],
                 upd["ab_out"][k], upd["c_in"][k], s["lb"], s["onorm"], upd["c_out"][k], s["final"]]
    loss = lax.psum(loss_vec[0, 0], ("x", "y", "c"))
    return (loss, dh0[None], *outs)
```

```python
import functools
import math

import jax
import jax.numpy as jnp
from jax import lax
from jax.experimental import pallas as pl
from jax.experimental.pallas import tpu as pltpu

F32 = jnp.float32
BF16 = jnp.bfloat16
MESH = pl.DeviceIdType.MESH

RMS_EPS = 1e-6
MACARON = 0.5
LANES = 128
CHUNK = 64
N_LEVELS = 6
HGRN_HEADS = 2
SB_KEYS = 256
ADAM_LR, ADAM_B1, ADAM_B2, ADAM_EPS, ADAM_WD, ADAM_STEP = 0.001, 0.9, 0.999, 1e-08, 0.01, 10
VMEM_LIMIT = 48 * 1024 * 1024


def _cp(**kw):
    return pltpu.CompilerParams(vmem_limit_bytes=VMEM_LIMIT, **kw)


def _sigmoid(x):
    return 1.0 / (1.0 + jnp.exp(-x))


def _bf(x):
    return x if x.dtype == BF16 else x.astype(BF16)


def _split3(x):
    hi = x.astype(BF16)
    r1 = x - hi.astype(F32)
    mid = r1.astype(BF16)
    lo = (r1 - mid.astype(F32)).astype(BF16)
    return hi, mid, lo


def _dot(a, b, ca=1, cb=0):
    return lax.dot_general(a, b, (((ca,), (cb,)), ((), ())), preferred_element_type=F32)


def _dot_exact_lhs(m, x):
    hi, mid, lo = _split3(x)
    return _dot(m, hi) + _dot(m, mid) + _dot(m, lo)


def _dot_exact_rhs(x, m):
    hi, mid, lo = _split3(x)
    return _dot(hi, m) + _dot(mid, m) + _dot(lo, m)


def _mm(terms, *, name, ta=False, tb=False, out_dtype=F32, residual=None, alpha=1.0, tm=512, tn=512):
    nt = len(terms)
    a0, b0 = terms[0]
    m = a0.shape[1] if ta else a0.shape[0]
    n = b0.shape[0] if tb else b0.shape[1]
    tm, tn = min(tm, m), min(tn, n)
    assert m % tm == 0 and n % tn == 0, (name, m, n, tm, tn)
    has_res = residual is not None

    def body(*refs):
        o_ref = refs[-1]
        acc = None
        for i in range(nt):
            a = _bf(refs[2 * i][...])
            b = _bf(refs[2 * i + 1][...])
            p = _dot(a, b, 0 if ta else 1, 1 if tb else 0)
            acc = p if acc is None else acc + p
        if alpha != 1.0:
            acc = acc * alpha
        if has_res:
            acc = acc + refs[2 * nt][...]
        o_ref[...] = acc.astype(out_dtype)

    in_specs, args = [], []
    for a, b in terms:
        k = a.shape[0] if ta else a.shape[1]
        assert (b.shape[1] if tb else b.shape[0]) == k, (name, a.shape, b.shape)
        in_specs.append(pl.BlockSpec((k, tm), lambda i, j: (0, i)) if ta else pl.BlockSpec((tm, k), lambda i, j: (i, 0)))
        in_specs.append(pl.BlockSpec((tn, k), lambda i, j: (j, 0)) if tb else pl.BlockSpec((k, tn), lambda i, j: (0, j)))
        args += [a, b]
    if has_res:
        in_specs.append(pl.BlockSpec((tm, tn), lambda i, j: (i, j)))
        args.append(residual)
    return pl.pallas_call(
        body, name=name, grid=(m // tm, n // tn), in_specs=in_specs,
        out_specs=pl.BlockSpec((tm, tn), lambda i, j: (i, j)),
        out_shape=jax.ShapeDtypeStruct((m, n), out_dtype), compiler_params=_cp())(*args)


def _rmsnorm_fwd(x, gain, *, name, tm=512):
    t, d = x.shape
    tm = min(tm, t)

    def body(x_ref, g_ref, o_ref):
        xv = x_ref[...]
        rstd = lax.rsqrt(jnp.mean(xv * xv, axis=-1, keepdims=True) + RMS_EPS)
        o_ref[...] = (xv * rstd * g_ref[...]).astype(BF16)

    return pl.pallas_call(
        body, name=name, grid=(t // tm,),
        in_specs=[pl.BlockSpec((tm, d), lambda i: (i, 0)), pl.BlockSpec((1, d), lambda i: (0, 0))],
        out_specs=pl.BlockSpec((tm, d), lambda i: (i, 0)),
        out_shape=jax.ShapeDtypeStruct((t, d), BF16), compiler_params=_cp())(x, gain)


def _rmsnorm_bwd(x, gain, dxn, dres, *, name, scale, tm=512):
    t, d = x.shape
    tm = min(tm, t)

    def body(x_ref, g_ref, dxn_ref, dres_ref, dx_ref, dxb_ref, dg_ref):
        xv = x_ref[...]
        rstd = lax.rsqrt(jnp.mean(xv * xv, axis=-1, keepdims=True) + RMS_EPS)
        xhat = xv * rstd
        dxn_v = dxn_ref[...]
        dxhat = dxn_v * g_ref[...]
        dx = dres_ref[...] + rstd * (dxhat - xhat * jnp.mean(dxhat * xhat, axis=-1, keepdims=True))
        dx_ref[...] = dx
        dxb_ref[...] = (dx * scale).astype(BF16)

        @pl.when(pl.program_id(0) == 0)
        def _():
            dg_ref[...] = jnp.zeros_like(dg_ref)

        dg_ref[...] += jnp.sum(dxn_v * xhat, axis=0, keepdims=True)

    row = pl.BlockSpec((tm, d), lambda i: (i, 0))
    vec = pl.BlockSpec((1, d), lambda i: (0, 0))
    return pl.pallas_call(
        body, name=name, grid=(t // tm,), in_specs=[row, vec, row, row], out_specs=[row, row, vec],
        out_shape=[jax.ShapeDtypeStruct((t, d), F32), jax.ShapeDtypeStruct((t, d), BF16), jax.ShapeDtypeStruct((1, d), F32)],
        compiler_params=_cp())(x, gain, dxn, dres)


def _loss_head(h, gain, target, *, name, tm=512):
    t, d = h.shape
    tm = min(tm, t)

    def body(h_ref, g_ref, t_ref, dh_ref, dhb_ref, dg_ref, loss_ref):
        hv = h_ref[...]
        rstd = lax.rsqrt(jnp.mean(hv * hv, axis=-1, keepdims=True) + RMS_EPS)
        xhat = hv * rstd
        err = xhat * g_ref[...] - t_ref[...]
        dy = err * (1.0 / d)
        dxhat = dy * g_ref[...]
        dh = rstd * (dxhat - xhat * jnp.mean(dxhat * xhat, axis=-1, keepdims=True))
        dh_ref[...] = dh
        dhb_ref[...] = (dh * MACARON).astype(BF16)

        @pl.when(pl.program_id(0) == 0)
        def _():
            dg_ref[...] = jnp.zeros_like(dg_ref)
            loss_ref[...] = jnp.zeros_like(loss_ref)

        dg_ref[...] += jnp.sum(dy * xhat, axis=0, keepdims=True)
        part = jnp.sum(jnp.sum(err * err, axis=-1, keepdims=True), axis=0, keepdims=True) * (0.5 / d)
        loss_ref[...] += jnp.broadcast_to(part, loss_ref.shape)

    row = pl.BlockSpec((tm, d), lambda i: (i, 0))
    vec = pl.BlockSpec((1, d), lambda i: (0, 0))
    return pl.pallas_call(
        body, name=name, grid=(t // tm,), in_specs=[row, vec, row],
        out_specs=[row, row, vec, pl.BlockSpec((1, LANES), lambda i: (0, 0))],
        out_shape=[jax.ShapeDtypeStruct((t, d), F32), jax.ShapeDtypeStruct((t, d), BF16), jax.ShapeDtypeStruct((1, d), F32),
                   jax.ShapeDtypeStruct((1, LANES), F32)],
        compiler_params=_cp())(h, gain, target)


def _norm_gate_up(x, gain, wg, wu, *, name, tm=512, tf=1408):
    t, d = x.shape
    f = wg.shape[0]
    tm, tf = min(tm, t), min(tf, f)
    assert f % tf == 0

    def body(x_ref, g_ref, wg_ref, wu_ref, xn_ref, gg_ref, uu_ref, act_ref):
        @pl.when(pl.program_id(1) == 0)
        def _():
            xv = x_ref[...]
            rstd = lax.rsqrt(jnp.mean(xv * xv, axis=-1, keepdims=True) + RMS_EPS)
            xn_ref[...] = (xv * rstd * g_ref[...]).astype(BF16)

        xn = xn_ref[...]
        gv = _dot(xn, wg_ref[...], 1, 1)
        uv = _dot(xn, wu_ref[...], 1, 1)
        gg_ref[...] = gv.astype(BF16)
        uu_ref[...] = uv.astype(BF16)
        act_ref[...] = (gv * _sigmoid(gv) * uv).astype(BF16)

    row = pl.BlockSpec((tm, d), lambda i, j: (i, 0))
    wsp = pl.BlockSpec((tf, d), lambda i, j: (j, 0))
    osp = pl.BlockSpec((tm, tf), lambda i, j: (i, j))
    return pl.pallas_call(
        body, name=name, grid=(t // tm, f // tf),
        in_specs=[row, pl.BlockSpec((1, d), lambda i, j: (0, 0)), wsp, wsp],
        out_specs=[row, osp, osp, osp],
        out_shape=[jax.ShapeDtypeStruct((t, d), BF16)] + [jax.ShapeDtypeStruct((t, f), BF16)] * 3,
        compiler_params=_cp())(x, gain, wg, wu)


def _swiglu_bwd(dout, wd, gg, uu, *, name, tm=512, tf=1408):
    t, d = dout.shape
    f = wd.shape[0]
    tm, tf = min(tm, t), min(tf, f)

    def body(do_ref, wd_ref, g_ref, u_ref, dg_ref, du_ref):
        dact = _dot(do_ref[...], wd_ref[...], 1, 1)
        gv = g_ref[...].astype(F32)
        uv = u_ref[...].astype(F32)
        sg = _sigmoid(gv)
        dg_ref[...] = (dact * uv * (sg * (1.0 + gv * (1.0 - sg)))).astype(BF16)
        du_ref[...] = (dact * (gv * sg)).astype(BF16)

    osp = pl.BlockSpec((tm, tf), lambda i, j: (i, j))
    return pl.pallas_call(
        body, name=name, grid=(t // tm, f // tf),
        in_specs=[pl.BlockSpec((tm, d), lambda i, j: (i, 0)), pl.BlockSpec((tf, d), lambda i, j: (j, 0)), osp, osp],
        out_specs=[osp, osp], out_shape=[jax.ShapeDtypeStruct((t, f), BF16)] * 2,
        compiler_params=_cp())(dout, wd, gg, uu)


def _shift_down(x, n):
    rows = lax.broadcasted_iota(jnp.int32, x.shape, 0)
    return jnp.where(rows >= n, pltpu.roll(x, n, 0), 0.0)


def _shift_up(x, n):
    t = x.shape[0]
    rows = lax.broadcasted_iota(jnp.int32, x.shape, 0)
    return jnp.where(rows < t - n, pltpu.roll(x, t - n, 0), 0.0)


def _conv_fwd(pa, conv_w, *, name):
    t = pa.shape[0]
    nb = pa.shape[1] // 3 // LANES

    def body(b_ref, c_ref, x_ref, w_ref, y_ref):
        u = c_ref[...] * x_ref[...]
        w = w_ref[...]
        conv = w[2:3, :] * u + w[1:2, :] * _shift_down(u, 1) + w[0:1, :] * _shift_down(u, 2)
        y_ref[...] = (b_ref[...] * conv).astype(BF16)

    def col(off):
        return pl.BlockSpec((t, LANES), lambda j: (0, off + j))

    return pl.pallas_call(
        body, name=name, grid=(nb,),
        in_specs=[col(0), col(nb), col(2 * nb), pl.BlockSpec((3, LANES), lambda j: (0, j))],
        out_specs=pl.BlockSpec((t, LANES), lambda j: (0, j)),
        out_shape=jax.ShapeDtypeStruct((t, nb * LANES), BF16), compiler_params=_cp())(pa, pa, pa, conv_w)


def _conv_bwd(pa, dy, conv_w, *, name):
    t = pa.shape[0]
    nb = pa.shape[1] // 3 // LANES

    def body(b_ref, c_ref, x_ref, dy_ref, w_ref, db_ref, dc_ref, dx_ref, dw_ref):
        cv, xv = c_ref[...], x_ref[...]
        u = cv * xv
        u1, u2 = _shift_down(u, 1), _shift_down(u, 2)
        w = w_ref[...]
        conv = w[2:3, :] * u + w[1:2, :] * u1 + w[0:1, :] * u2
        dyv = dy_ref[...]
        db_ref[...] = (dyv * conv).astype(BF16)
        dconv = dyv * b_ref[...]
        du = w[2:3, :] * dconv + w[1:2, :] * _shift_up(dconv, 1) + w[0:1, :] * _shift_up(dconv, 2)
        dc_ref[...] = (du * xv).astype(BF16)
        dx_ref[...] = (du * cv).astype(BF16)
        dw_ref[0:1, :] = jnp.sum(dconv * u2, axis=0, keepdims=True)
        dw_ref[1:2, :] = jnp.sum(dconv * u1, axis=0, keepdims=True)
        dw_ref[2:3, :] = jnp.sum(dconv * u, axis=0, keepdims=True)

    def col(off):
        return pl.BlockSpec((t, LANES), lambda j: (0, off + j))

    osp = pl.BlockSpec((t, LANES), lambda j: (0, j))
    wsp = pl.BlockSpec((3, LANES), lambda j: (0, j))
    return pl.pallas_call(
        body, name=name, grid=(nb,), in_specs=[col(0), col(nb), col(2 * nb), col(0), wsp],
        out_specs=[osp, osp, osp, wsp],
        out_shape=[jax.ShapeDtypeStruct((t, nb * LANES), BF16)] * 3 + [jax.ShapeDtypeStruct((3, nb * LANES), F32)],
        compiler_params=_cp())(pa, pa, pa, dy, conv_w)


def _sb_consts():
    j = lax.broadcasted_iota(jnp.int32, (SB_KEYS, SB_KEYS), 0)
    s = lax.broadcasted_iota(jnp.int32, (SB_KEYS, SB_KEYS), 1)
    after = (j > s).astype(BF16)
    upto = (j <= s).astype(BF16)
    before = (j < s).astype(BF16)
    return after, jnp.stack([upto, before])


def _log_sigmoid(z):
    return jnp.minimum(z, 0.0) - jnp.log(1.0 + jnp.exp(-jnp.abs(z)))


def _attn_fwd(pb, late_pack, seg_rows, *, name, tq=256):
    t = pb.shape[0]
    npair = pb.shape[1] // 3 // LANES
    tq = min(tq, t)
    nq = t // tq
    cmat, _ = _sb_consts()
    scale = 1.0 / math.sqrt(LANES // 2)

    nseg = len(seg_rows)

    def body(q_ref, k_ref, v_ref, c_ref, late_ref, y_ref, lt_ref, *rest):
        i = pl.program_id(1)
        pair = pl.program_id(0)
        start, forward, finish = _gather_phases(late_ref, rest[:nseg], seg_rows, *rest[nseg:])
        pl.when((pair == 0) & (i == 0))(start)
        pl.when((pair == npair - 1) & (i == nq // 2))(forward)
        lane = lax.broadcasted_iota(jnp.int32, (tq, LANES), 1)
        rowpos = i * tq + lax.broadcasted_iota(jnp.int32, (tq, SB_KEYS), 0)
        colid = lax.broadcasted_iota(jnp.int32, (tq, SB_KEYS), 1)
        q2 = q_ref[...] * jnp.asarray(scale, BF16)
        cm = c_ref[...]
        hi_lanes = lane >= LANES // 2
        qhs = [jnp.where(hi_lanes == (hh == 1), q2, jnp.zeros_like(q2)) for hh in range(2)]
        per_q = tq // SB_KEYS

        def blk(jb):
            return pl.ds(pl.multiple_of(jb * SB_KEYS, SB_KEYS), SB_KEYS)

        def scores(jb):
            kb = k_ref[blk(jb), :]
            return tuple(_dot(qhs[hh], kb, 1, 1) for hh in range(2))

        def weights(jb, zs, runs, masked):
            mask = (jb * SB_KEYS + colid) < rowpos if masked else None
            ws, new_runs = [], []
            for hh in range(2):
                lb = _log_sigmoid(zs[hh])
                lk = lb - zs[hh]
                if masked:
                    lk = jnp.where(mask, lk, 0.0)
                lk_hi, lk_lo = _split2(lk)
                cs = _dot(lk_hi, cm) + _dot(lk_lo, cm)
                w = jnp.exp(lb + runs[hh] + cs)
                if masked:
                    w = jnp.where(mask, w, 0.0)
                ws.append(w.astype(BF16))
                new_runs.append(runs[hh] + jnp.sum(lk, axis=1, keepdims=True))
            return tuple(ws), tuple(new_runs)

        def values(jb, accs, ws):
            vb = v_ref[blk(jb), :]
            return tuple(accs[hh] + _dot(ws[hh], vb) for hh in range(2))

        zero = jnp.zeros((tq, LANES), F32)
        zero_col = jnp.zeros((tq, 1), F32)
        nfull = i * per_q
        runs, accs, ws = (zero_col, zero_col), (zero, zero), None
        zs = scores(nfull + per_q - 1)
        for dblk in reversed(range(per_q)):
            jb = nfull + dblk
            zs_next = scores(jnp.maximum(jb - 1, 0))
            if ws is not None:
                accs = values(jb + 1, accs, ws)
            ws, runs = weights(jb, zs, runs, True)
            zs = zs_next

        def full_block(n, carry):
            zs, ws, runs, accs = carry
            jb = nfull - 1 - n
            zs_next = scores(jnp.maximum(jb - 1, 0))
            accs = values(jb + 1, accs, ws)
            ws, runs = weights(jb, zs, runs, False)
            return zs_next, ws, runs, accs

        _, ws, runs, accs = lax.fori_loop(0, nfull, full_block, (zs, ws, runs, accs))
        accs = values(0, accs, ws)
        y_ref[...] = jnp.where(hi_lanes, accs[1], accs[0]).astype(BF16)
        lt_ref[...] = jnp.where(hi_lanes, runs[1], runs[0])
        pl.when((pair == npair - 1) & (i == nq - 1))(finish)

    return pl.pallas_call(
        body, name=name, grid=(npair, nq),
        in_specs=[pl.BlockSpec((tq, LANES), lambda p, i: (i, p)),
                  pl.BlockSpec((t, LANES), lambda p, i: (0, npair + p)),
                  pl.BlockSpec((t, LANES), lambda p, i: (0, 2 * npair + p)),
                  pl.BlockSpec((SB_KEYS, SB_KEYS), lambda p, i: (0, 0)),
                  HBM_SPEC],
        out_specs=[pl.BlockSpec((tq, LANES), lambda p, i: (i, p))] * 2 + [HBM_SPEC] * nseg,
        out_shape=[jax.ShapeDtypeStruct((t, npair * LANES), BF16), jax.ShapeDtypeStruct((t, npair * LANES), F32),
                   ] + [jax.ShapeDtypeStruct((8, n, late_pack.shape[1]), late_pack.dtype) for n in seg_rows],
        scratch_shapes=_gather_scratch(),
        compiler_params=_cp(dimension_semantics=("arbitrary", "arbitrary")))(pb, pb, pb, cmat, late_pack)


def _attn_bwd(pb, dy, ltot, chip_part, *, name, tq=256):
    t = pb.shape[0]
    npair = pb.shape[1] // 3 // LANES
    tq = min(tq, t)
    nq = t // tq
    _, cmats = _sb_consts()
    scale = 1.0 / math.sqrt(LANES // 2)

    def body(q_ref, k_ref, v_ref, dy_ref, lt_ref, c_ref, part_ref, dq_ref, dk_ref, dv_ref, parts_ref, dk_acc, dv_acc, *sems):
        i = pl.program_id(1)
        pair = pl.program_id(0)
        start, finish = _chip_exchange_phases(part_ref, parts_ref, *sems)
        pl.when((pair == 0) & (i == 0))(start)

        @pl.when(i == 0)
        def _():
            dk_acc[...] = jnp.zeros_like(dk_acc)
            dv_acc[...] = jnp.zeros_like(dv_acc)

        lane = lax.broadcasted_iota(jnp.int32, (tq, LANES), 1)
        rowpos = i * tq + lax.broadcasted_iota(jnp.int32, (tq, SB_KEYS), 0)
        colid = lax.broadcasted_iota(jnp.int32, (tq, SB_KEYS), 1)
        q2 = q_ref[...] * jnp.asarray(scale, BF16)
        do2 = dy_ref[...].astype(BF16)
        ltv = lt_ref[...]
        c_upto, c_before = c_ref[0], c_ref[1]
        hi_lanes = lane >= LANES // 2
        sels = [hi_lanes == (hh == 1) for hh in range(2)]
        qhs = [jnp.where(s, q2, jnp.zeros_like(q2)) for s in sels]
        dohs = [jnp.where(s, do2, jnp.zeros_like(do2)) for s in sels]
        lts = [ltv[:, 0:1], ltv[:, LANES // 2:LANES // 2 + 1]]
        per_q = tq // SB_KEYS

        def blk(jb):
            return pl.ds(pl.multiple_of(jb * SB_KEYS, SB_KEYS), SB_KEYS)

        def scores(jb):
            kb, vb = k_ref[blk(jb), :], v_ref[blk(jb), :]
            return tuple((_dot(qhs[hh], kb, 1, 1), _dot(dohs[hh], vb, 1, 1)) for hh in range(2))

        def products(jb, dqs, pend):
            kb = k_ref[blk(jb), :]
            dk = _dot(pend[0][0], qhs[0], 0, 0) + _dot(pend[1][0], qhs[1], 0, 0)
            dv = _dot(pend[0][1], dohs[0], 0, 0) + _dot(pend[1][1], dohs[1], 0, 0)
            dk_acc[blk(jb), :] += dk
            dv_acc[blk(jb), :] += dv
            return tuple(dqs[hh] + _dot(pend[hh][0], kb) for hh in range(2))

        def chain(jb, zs, sums, masked):
            mask = (jb * SB_KEYS + colid) < rowpos if masked else None
            pend, new_sums = [], []
            for hh in range(2):
                z, da = zs[hh]
                csum, prun = sums[hh]
                lb = _log_sigmoid(z)
                lk = lb - z
                if masked:
                    lk = jnp.where(mask, lk, 0.0)
                lk_hi, lk_lo = _split2(lk)
                cs = _dot(lk_hi, c_upto) + _dot(lk_lo, c_upto)
                a = jnp.exp(lb + ((lts[hh] - csum) - cs))
                if masked:
                    a = jnp.where(mask, a, 0.0)
                e = a * da
                e_hi, e_lo = _split2(e)
                ce = _dot(e_hi, c_before) + _dot(e_lo, c_before)
                beta = jnp.exp(lb)
                dz = e * (1.0 - beta) - (prun + ce) * beta
                if masked:
                    dz = jnp.where(mask, dz, 0.0)
                pend.append((dz.astype(BF16), a.astype(BF16)))
                new_sums.append((csum + jnp.sum(lk, axis=1, keepdims=True), prun + jnp.sum(e, axis=1, keepdims=True)))
            return tuple(pend), tuple(new_sums)

        zero = jnp.zeros((tq, LANES), F32)
        zero_b = jnp.zeros((tq, SB_KEYS), BF16)
        nfull = i * per_q
        last = nfull + per_q - 1

        def full_block(jb, carry):
            zs, pend, sums, dqs = carry
            zs_next = scores(jb + 1)
            dqs = products(jnp.maximum(jb - 1, 0), dqs, pend)
            pend, sums = chain(jb, zs, sums, False)
            return zs_next, pend, sums, dqs

        zero_col = jnp.zeros((tq, 1), F32)
        carry = (scores(0), ((zero_b, zero_b),) * 2, ((zero_col, zero_col),) * 2, (zero, zero))
        zs, pend, sums, dqs = lax.fori_loop(0, nfull, full_block, carry)
        for dblk in range(per_q):
            jb = nfull + dblk
            zs_next = scores(jnp.minimum(jb + 1, last))
            dqs = products(jnp.maximum(jb - 1, 0), dqs, pend)
            pend, sums = chain(jb, zs, sums, True)
            zs = zs_next
        dqs = products(last, dqs, pend)
        dq_ref[...] = (jnp.where(hi_lanes, dqs[1], dqs[0]) * scale).astype(BF16)

        @pl.when(i == nq - 1)
        def _():
            dk_ref[...] = dk_acc[...].astype(BF16)
            dv_ref[...] = dv_acc[...].astype(BF16)

        pl.when((pair == npair - 1) & (i == nq - 1))(finish)

    blk = pl.BlockSpec((tq, LANES), lambda p, i: (i, p))
    full = pl.BlockSpec((t, LANES), lambda p, i: (0, p))
    return pl.pallas_call(
        body, name=name, grid=(npair, nq),
        in_specs=[blk,
                  pl.BlockSpec((t, LANES), lambda p, i: (0, npair + p)),
                  pl.BlockSpec((t, LANES), lambda p, i: (0, 2 * npair + p)),
                  pl.BlockSpec((tq, LANES), lambda p, i: (i, npair + p)),
                  blk,
                  pl.BlockSpec((2, SB_KEYS, SB_KEYS), lambda p, i: (0, 0, 0)),
                  HBM_SPEC],
        out_specs=[blk, full, full, HBM_SPEC],
        out_shape=[jax.ShapeDtypeStruct((t, npair * LANES), BF16)] * 3 + [jax.ShapeDtypeStruct(chip_part.shape, chip_part.dtype)],
        scratch_shapes=[pltpu.VMEM((t, LANES), F32), pltpu.VMEM((t, LANES), F32)] + _chip_exchange_scratch(),
        compiler_params=_cp(dimension_semantics=("arbitrary", "arbitrary")))(pb, pb, pb, dy, ltot, cmats, chip_part)


def _hgrn_consts():
    t = lax.broadcasted_iota(jnp.int32, (CHUNK, CHUNK), 0)
    s = lax.broadcasted_iota(jnp.int32, (CHUNK, CHUNK), 1)
    masks = []
    for lvl in range(N_LEVELS):
        half = CHUNK >> (lvl + 1)
        same = (t // (2 * half)) == (s // (2 * half))
        masks.append((same & (t % (2 * half) >= half) & (s % (2 * half) < half)).astype(F32))
    masks.append((t == s).astype(F32))
    prefix = (s <= t).astype(BF16)
    suffix = (s >= t).astype(BF16)
    return prefix, jnp.stack(masks), suffix


def _hgrn_gates(qr, fr, lbv):
    sg = _sigmoid(fr)
    fval = lbv + (1.0 - lbv) * sg
    kk = (1.0 - lbv) * _sigmoid(-fr)
    sq = _sigmoid(qr)
    return sg, fval, jnp.log(fval), kk, sq, qr * sq


def _lower_bound(c_ref):
    c = c_ref[...]
    mx = jnp.max(c, axis=0, keepdims=True)
    ex = jnp.exp(c - mx)
    return ex[1:2, :] / jnp.sum(ex, axis=0, keepdims=True)


def _level_ref(b, lvl):
    half = CHUNK >> (lvl + 1)
    seg = 2 * half
    if seg >= 8:
        b3 = b.reshape(CHUNK // seg, seg, LANES)
        return jnp.broadcast_to(b3[:, half - 1:half, :], b3.shape).reshape(CHUNK, LANES)
    pos = lax.broadcasted_iota(jnp.int32, b.shape, 0) % seg
    out = b
    for p in range(seg):
        if p != half - 1:
            out = jnp.where(pos == p, pltpu.roll(b, (p - (half - 1)) % CHUNK, 0), out)
    return out


def _hgrn_levels(b, qs, kk):
    out = []
    for lvl in range(N_LEVELS):
        bref = _level_ref(b, lvl)
        eq = jnp.exp(jnp.minimum(b - bref, 0.0))
        ek = jnp.exp(jnp.minimum(bref - b, 0.0))
        out.append((qs * eq, kk * ek, eq, ek))
    out.append((qs, kk, None, None))
    return out


def _split2(x):
    hi = x.astype(BF16)
    return hi, (x - hi.astype(F32)).astype(BF16)


def _hgrn_fwd(pc, c_lb, out_norm, *, name, tc=512):
    t = pc.shape[0]
    nh = pc.shape[1] // 4 // LANES
    tc = min(tc, t)
    nch = tc // CHUNK
    cum_all, masks, _ = _hgrn_consts()

    def body(q_ref, f_ref, i_ref, g_ref, lb_ref, on_ref, cum_ref, m_ref, y_ref, o_ref, st_ref, state):
        @pl.when(pl.program_id(1) == 0)
        def _():
            state[...] = jnp.zeros_like(state)

        lbv = _lower_bound(lb_ref)
        onv = on_ref[...]

        def chunk(c, carry):
            rows = pl.ds(pl.multiple_of(c * CHUNK, CHUNK), CHUNK)
            for hh in range(HGRN_HEADS):
                lanes = slice(hh * LANES, (hh + 1) * LANES)
                _, _, g, kk, _, qs = _hgrn_gates(q_ref[rows, lanes], f_ref[rows, lanes], lbv[:, lanes])
                vb = i_ref[rows, lanes].astype(BF16)
                b = _dot_exact_lhs(cum_ref[...], g)
                scores = jnp.zeros((CHUNK, CHUNK), F32)
                for lvl, (ql, kl, _, _) in enumerate(_hgrn_levels(b, qs, kk)):
                    scores = scores + _dot(ql.astype(BF16), kl.astype(BF16), 1, 1) * m_ref[lvl]
                st = state[hh]
                st_ref[hh, c] = st
                o = _dot(scores.astype(BF16), vb) + _dot((qs * jnp.exp(b)).astype(BF16), st.astype(BF16), 1, 1)
                blast = b[CHUNK - 1:CHUNK, :]
                kdec = (kk * jnp.exp(blast - b)).astype(BF16)
                state[hh] = st * jnp.exp(blast) + _dot(vb, kdec, 0, 0)
                o_ref[rows, lanes] = o
                rstd = lax.rsqrt(jnp.mean(o * o, axis=-1, keepdims=True) + RMS_EPS)
                gate = g_ref[rows, lanes]
                y_ref[rows, lanes] = (o * rstd * onv * (gate * _sigmoid(gate))).astype(BF16)
            return carry

        lax.fori_loop(0, nch, chunk, 0, unroll=2)

    hw = HGRN_HEADS * LANES

    def col(off):
        return pl.BlockSpec((tc, hw), lambda h, i: (i, off // HGRN_HEADS + h))

    osp = pl.BlockSpec((tc, hw), lambda h, i: (i, h))
    return pl.pallas_call(
        body, name=name, grid=(nh // HGRN_HEADS, t // tc),
        in_specs=[col(0), col(nh), col(2 * nh), col(3 * nh),
                  pl.BlockSpec((2, hw), lambda h, i: (0, h)),
                  pl.BlockSpec((1, LANES), lambda h, i: (0, 0)),
                  pl.BlockSpec(cum_all.shape, lambda h, i: (0, 0)),
                  pl.BlockSpec(masks.shape, lambda h, i: (0, 0, 0))],
        out_specs=[osp, osp, pl.BlockSpec((HGRN_HEADS, nch, LANES, LANES), lambda h, i: (h, i, 0, 0))],
        out_shape=[jax.ShapeDtypeStruct((t, nh * LANES), BF16), jax.ShapeDtypeStruct((t, nh * LANES), F32),
                   jax.ShapeDtypeStruct((nh, t // CHUNK, LANES, LANES), F32)],
        scratch_shapes=[pltpu.VMEM((HGRN_HEADS, LANES, LANES), F32)],
        compiler_params=_cp())(pc, pc, pc, pc, c_lb, out_norm, cum_all, masks)


def _hgrn_bwd(pc, o_saved, states, dy, c_lb, out_norm, *, name, tc=512):
    t = pc.shape[0]
    nh = pc.shape[1] // 4 // LANES
    tc = min(tc, t)
    nch = tc // CHUNK
    nt = t // tc
    cum_all, masks, suffix = _hgrn_consts()

    def body(q_ref, f_ref, i_ref, g_ref, o_ref, st_ref, dy_ref, lb_ref, on_ref, cum_ref, m_ref, suf_ref,
             dq_ref, df_ref, di_ref, dg_ref, dlb_ref, don_ref, dstate):
        @pl.when(pl.program_id(1) == 0)
        def _():
            dstate[...] = jnp.zeros_like(dstate)
            dlb_ref[...] = jnp.zeros_like(dlb_ref)
            don_ref[...] = jnp.zeros_like(don_ref)

        lbv = _lower_bound(lb_ref)
        onv = on_ref[...]

        def head(hh, c, rows):
            lanes = slice(hh * LANES, (hh + 1) * LANES)
            qr = q_ref[rows, lanes]
            sg, fval, g, kk, sq, qs = _hgrn_gates(qr, f_ref[rows, lanes], lbv[:, lanes])
            vb = i_ref[rows, lanes].astype(BF16)
            o = o_ref[rows, lanes]
            gate = g_ref[rows, lanes]
            sgt = _sigmoid(gate)
            rstd = lax.rsqrt(jnp.mean(o * o, axis=-1, keepdims=True) + RMS_EPS)
            ohat = o * rstd
            dyv = dy_ref[rows, lanes]
            don = dyv * (gate * sgt)
            dg_ref[rows, lanes] = (dyv * ohat * onv * (sgt * (1.0 + gate * (1.0 - sgt)))).astype(BF16)
            don_ref[:, lanes] += jnp.sum(don * ohat, axis=0, keepdims=True)
            dxhat = don * onv
            dob = (rstd * (dxhat - ohat * jnp.mean(dxhat * ohat, axis=-1, keepdims=True))).astype(BF16)
            b = _dot_exact_lhs(cum_ref[...], g)
            blast = b[CHUNK - 1:CHUNK, :]
            eb = jnp.exp(b)
            edec = jnp.exp(blast - b)
            st32 = st_ref[hh, c]
            st = st32.astype(BF16)
            dst = dstate[hh]
            dstb = dst.astype(BF16)
            da = _dot(dob, vb, 1, 1)
            levels = _hgrn_levels(b, qs, kk)
            scores = jnp.zeros((CHUNK, CHUNK), F32)
            dq = eb * _dot(dob, st)
            dk_inter = edec * _dot(vb, dstb)
            dk = dk_inter
            for lvl, (ql, kl, eq, ek) in enumerate(levels):
                mk = m_ref[lvl]
                (qh, qlo), (kh, klo) = _split2(ql), _split2(kl)
                scores = scores + _dot(qh, kh, 1, 1) * mk
                dal = (da * mk).astype(BF16)
                dql = _dot(dal, kh) + _dot(dal, klo)
                dkl = _dot(dal, qh, 0, 0) + _dot(dal, qlo, 0, 0)
                dq = dq + (dql if eq is None else dql * eq)
                dk = dk + (dkl if ek is None else dkl * ek)
            kdec = (kk * edec).astype(BF16)
            dv = _dot(scores.astype(BF16), dob, 0, 0) + _dot(kdec, dstb, 1, 1)
            dstate[hh] = dst * jnp.exp(blast) + _dot(dob, (qs * eb).astype(BF16), 0, 0)
            db = qs * dq - kk * dk
            last = jnp.sum(kk * dk_inter, axis=0, keepdims=True) + jnp.exp(blast) * jnp.sum(dst * st32, axis=0, keepdims=True)
            dgl = _dot_exact_lhs(suf_ref[...], db) + last
            dfv = dgl / fval - dk
            df_ref[rows, lanes] = (dfv * (1.0 - lbv[:, lanes]) * sg * (1.0 - sg)).astype(BF16)
            dlb_ref[:, lanes] += jnp.sum(dfv * (1.0 - sg), axis=0, keepdims=True)
            dq_ref[rows, lanes] = (dq * (sq * (1.0 + qr * (1.0 - sq)))).astype(BF16)
            di_ref[rows, lanes] = dv.astype(BF16)

        def chunk(n, carry):
            c = nch - 1 - n
            rows = pl.ds(pl.multiple_of(c * CHUNK, CHUNK), CHUNK)
            for hh in range(HGRN_HEADS):
                head(hh, c, rows)
            return carry

        lax.fori_loop(0, nch, chunk, 0, unroll=2)

    hw = HGRN_HEADS * LANES

    def col(off):
        return pl.BlockSpec((tc, hw), lambda h, i: (nt - 1 - i, off // HGRN_HEADS + h))

    osp = pl.BlockSpec((tc, hw), lambda h, i: (nt - 1 - i, h))
    vec = pl.BlockSpec((1, hw), lambda h, i: (0, h))
    return pl.pallas_call(
        body, name=name, grid=(nh // HGRN_HEADS, nt),
        in_specs=[col(0), col(nh), col(2 * nh), col(3 * nh), osp,
                  pl.BlockSpec((HGRN_HEADS, nch, LANES, LANES), lambda h, i: (h, nt - 1 - i, 0, 0)),
                  osp,
                  pl.BlockSpec((2, hw), lambda h, i: (0, h)),
                  pl.BlockSpec((1, LANES), lambda h, i: (0, 0)),
                  pl.BlockSpec(cum_all.shape, lambda h, i: (0, 0)),
                  pl.BlockSpec(masks.shape, lambda h, i: (0, 0, 0)),
                  pl.BlockSpec(suffix.shape, lambda h, i: (0, 0))],
        out_specs=[osp, osp, osp, osp, vec, vec],
        out_shape=[jax.ShapeDtypeStruct((t, nh * LANES), BF16)] * 4 + [jax.ShapeDtypeStruct((1, nh * LANES), F32)] * 2,
        scratch_shapes=[pltpu.VMEM((HGRN_HEADS, LANES, LANES), F32)],
        compiler_params=_cp())(pc, pc, pc, pc, o_saved, states, dy, c_lb, out_norm, cum_all, masks, suffix)


HBM_SPEC = pl.BlockSpec(memory_space=pltpu.HBM)


def _gather_scratch():
    return [pltpu.SemaphoreType.DMA((7,)), pltpu.SemaphoreType.DMA((7,)), pltpu.SemaphoreType.DMA]


def _gather_phases(x_ref, out_refs, seg_rows, send_sems, recv_sems, local_sem):
    x, y, c = lax.axis_index("x"), lax.axis_index("y"), lax.axis_index("c")
    me, sibling = (x, y, c), (x, y, 1 - c)
    chips = [(1 - x, y), (x, 1 - y), (1 - x, 1 - y)]
    offs = [sum(seg_rows[:s]) for s in range(len(seg_rows))]
    assert sum(seg_rows) == x_ref.shape[0]

    def index(px, py, pc):
        return 4 * px + 2 * py + pc

    def copies(k, block, to, own):
        return [pltpu.make_async_remote_copy(
            src_ref=x_ref.at[pl.ds(offs[s], n)] if own else out_refs[s].at[index(*block)],
            dst_ref=out_refs[s].at[index(*block)],
            send_sem=send_sems.at[k], recv_sem=recv_sems.at[k], device_id=to, device_id_type=MESH)
            for s, n in enumerate(seg_rows)]

    def all_bytes(k):
        return pltpu.make_async_remote_copy(src_ref=x_ref, dst_ref=x_ref, send_sem=send_sems.at[k],
                                            recv_sem=recv_sems.at[k], device_id=me, device_id_type=MESH)

    mine = [pltpu.make_async_copy(x_ref.at[pl.ds(offs[s], n)], out_refs[s].at[index(*me)], local_sem)
            for s, n in enumerate(seg_rows)]
    first = copies(0, me, sibling, True)
    for j, chip in enumerate(chips):
        first += copies(1 + j, me, (*chip, c), True)

    def start():
        for cp in mine + first:
            cp.start()

    def forward():
        for j, chip in enumerate(chips):
            all_bytes(1 + j).wait_recv()
            for cp in copies(4 + j, (*chip, c), sibling, False):
                cp.start()

    def finish():
        all_bytes(0).wait_recv()
        for j in range(3):
            all_bytes(4 + j).wait_recv()
        for k in range(7):
            all_bytes(k).wait_send()
        pltpu.make_async_copy(x_ref, x_ref, local_sem).wait()

    return start, forward, finish


def _all_gather(xs, seg_rows=None, *, name):
    segs = [xs.shape[0]] if seg_rows is None else list(seg_rows)

    def body(x_ref, *rest):
        start, forward, finish = _gather_phases(x_ref, rest[:len(segs)], segs, *rest[len(segs):])
        start()
        forward()
        finish()

    outs = pl.pallas_call(
        body, name=name, in_specs=[HBM_SPEC], out_specs=[HBM_SPEC] * len(segs),
        out_shape=[jax.ShapeDtypeStruct((8, n, xs.shape[1]), xs.dtype) for n in segs],
        scratch_shapes=_gather_scratch())(xs)
    return outs[0] if seg_rows is None else outs


def _sibling_exchange(s, *, name):
    def body(s_ref, rb_ref, send_sem, recv_sem):
        x, y, c = lax.axis_index("x"), lax.axis_index("y"), lax.axis_index("c")
        cp = pltpu.make_async_remote_copy(
            src_ref=s_ref.at[:, 1 - c], dst_ref=rb_ref, send_sem=send_sem, recv_sem=recv_sem,
            device_id=(x, y, 1 - c), device_id_type=MESH)
        cp.start()
        cp.wait()

    return pl.pallas_call(
        body, name=name, in_specs=[HBM_SPEC], out_specs=HBM_SPEC,
        out_shape=jax.ShapeDtypeStruct(s.shape[:1] + s.shape[2:], s.dtype),
        scratch_shapes=[pltpu.SemaphoreType.DMA, pltpu.SemaphoreType.DMA])(s)


def _row_tile(n, cap=1024):
    return max(b for b in range(16, cap + 1, 16) if n % b == 0)


def _pair_add(s, rb, core, *, name):
    nchip, _, r, c = s.shape
    tb = _row_tile(r)

    def body(core_ref, a_ref, b_ref, o_ref):
        o_ref[...] = (a_ref[...].astype(F32) + b_ref[...].astype(F32)).astype(BF16)

    blk = pl.BlockSpec((None, tb, c), lambda ch, i, cr: (ch, i, 0))
    return pl.pallas_call(
        body, name=name,
        grid_spec=pltpu.PrefetchScalarGridSpec(
            num_scalar_prefetch=1, grid=(nchip, r // tb),
            in_specs=[pl.BlockSpec((None, None, tb, c), lambda ch, i, cr: (ch, cr[0], i, 0)), blk],
            out_specs=blk),
        out_shape=jax.ShapeDtypeStruct((nchip, r, c), BF16), compiler_params=_cp())(core, s, rb)


def _chip_exchange_scratch():
    return [pltpu.SemaphoreType.DMA((3,)), pltpu.SemaphoreType.DMA((3,)), pltpu.SemaphoreType.DMA]


def _chip_exchange_phases(p_ref, out_ref, send_sems, recv_sems, local_sem):
    x, y, c = lax.axis_index("x"), lax.axis_index("y"), lax.axis_index("c")
    mine = 2 * x + y
    own = pltpu.make_async_copy(p_ref.at[mine], out_ref.at[mine], local_sem)
    copies = [pltpu.make_async_remote_copy(
        src_ref=p_ref.at[2 * tx + ty], dst_ref=out_ref.at[mine],
        send_sem=send_sems.at[k], recv_sem=recv_sems.at[k], device_id=(tx, ty, c), device_id_type=MESH)
        for k, (tx, ty) in enumerate([(1 - x, y), (x, 1 - y), (1 - x, 1 - y)])]

    def start():
        own.start()
        for cp in copies:
            cp.start()

    def finish():
        for cp in copies:
            cp.wait()
        own.wait()

    return start, finish


def _adamw_math(w, g, m, v):
    m2 = ADAM_B1 * m + (1.0 - ADAM_B1) * g
    v2 = ADAM_B2 * v + (1.0 - ADAM_B2) * (g * g)
    m_hat = m2 / (1.0 - ADAM_B1 ** ADAM_STEP)
    v_hat = v2 / (1.0 - ADAM_B2 ** ADAM_STEP)
    return -ADAM_LR * (m_hat / (jnp.sqrt(v_hat) + ADAM_EPS) + ADAM_WD * w), m2, v2


def _grad_sum(parts, *, name):
    _, r, c = parts.shape
    tb = _row_tile(r)

    def body(p0, p1, p2, p3, g_out):
        g_out[...] = ((p0[...].astype(F32) + p1[...].astype(F32)) + p2[...].astype(F32)) + p3[...].astype(F32)

    def part(ch):
        return pl.BlockSpec((None, tb, c), lambda i: (ch, i, 0))

    return pl.pallas_call(
        body, name=name, grid=(r // tb,), in_specs=[part(0), part(1), part(2), part(3)],
        out_specs=pl.BlockSpec((tb, c), lambda i: (i, 0)), out_shape=jax.ShapeDtypeStruct((r, c), F32),
        compiler_params=_cp())(parts, parts, parts, parts)


def _adamw_shard(g, g_off, w, m, v, layer, prev, *, name):
    _, r, c = w.shape
    tb = next(b for b in range(min(r, 512), 0, -8) if r % b == 0 and g_off % b == 0)

    def body(g_ref, w_ref, m_ref, v_ref, *rest):
        d_out, m_out, v_out = rest[-3:]
        d, m2, v2 = _adamw_math(w_ref[...], g_ref[...], m_ref[...], v_ref[...])
        d_out[...] = d
        m_out[...] = m2
        v_out[...] = v2

    blk = pl.BlockSpec((None, tb, c), lambda i: (layer, i, 0))
    prev = list(prev) if prev is not None else []
    return pl.pallas_call(
        body, name=name, grid=(r // tb,),
        in_specs=[pl.BlockSpec((tb, c), lambda i: (g_off // tb + i, 0)), blk, blk, blk] + [pl.BlockSpec(memory_space=pl.ANY)] * len(prev),
        out_specs=[blk] * 3, out_shape=[jax.ShapeDtypeStruct(w.shape, F32)] * 3,
        input_output_aliases={4 + k: k for k in range(len(prev))},
        compiler_params=_cp())(g, w, m, v, *prev)


SLOT = 8
SMALL_ROWS = 6 * SLOT
ROW_LB = 4 * SLOT


def _small_update(gath, w, m, v, *, name):
    def body(g_ref, w_ref, m_ref, v_ref, g_out, d_out, m_out, v_out):
        tot = g_ref[0]
        for k in range(1, 8):
            tot = tot + g_ref[k]
        wv = w_ref[...]
        c0, c1 = wv[ROW_LB:ROW_LB + 1, :], wv[ROW_LB + 1:ROW_LB + 2, :]
        mx = jnp.maximum(c0, c1)
        e0, e1 = jnp.exp(c0 - mx), jnp.exp(c1 - mx)
        lb = e1 / (e0 + e1)
        gl = tot[ROW_LB:ROW_LB + 1, :] * lb * (1.0 - lb)
        row = lax.broadcasted_iota(jnp.int32, tot.shape, 0)
        g = jnp.where(row == ROW_LB, -gl, jnp.where(row == ROW_LB + 1, gl, tot))
        d, m2, v2 = _adamw_math(wv, g, m_ref[...], v_ref[...])
        g_out[...] = g
        d_out[...] = d
        m_out[...] = m2
        v_out[...] = v2

    return pl.pallas_call(
        body, name=name, out_shape=[jax.ShapeDtypeStruct(w.shape, F32)] * 4, compiler_params=_cp())(gath, w, m, v)


D_MODEL = 1024


def _ffn_fwd(h, gain, wg, wu, wd, tag):
    xn, gg, uu, act = _norm_gate_up(h, gain, wg, wu, name=f"{tag}_gate_up")
    out = _mm([(act, wd)], residual=h, alpha=MACARON, tn=1024, name=f"{tag}_down")
    return out, (h, xn, gg, uu, act)


def _ffn_input_bwd(dg, du, wg, wu, x, gain, dres, chip_part, *, name, scale, tm=256):
    t, d = x.shape
    f = wg.shape[0]
    tm = min(tm, t)
    nt = t // tm
    fused = chip_part is not None

    def body(dg_ref, du_ref, wg_ref, wu_ref, x_ref, g_ref, dres_ref, *rest):
        if fused:
            part_ref, dx_ref, dxb_ref, dgain_ref, parts_ref = rest[:5]
            start, finish = _chip_exchange_phases(part_ref, parts_ref, *rest[5:])
            pl.when(pl.program_id(0) == 0)(start)
        else:
            dx_ref, dxb_ref, dgain_ref = rest
        dxn_v = _dot(dg_ref[...], wg_ref[...]) + _dot(du_ref[...], wu_ref[...])
        xv = x_ref[...]
        rstd = lax.rsqrt(jnp.mean(xv * xv, axis=-1, keepdims=True) + RMS_EPS)
        xhat = xv * rstd
        dxhat = dxn_v * g_ref[...]
        dx = dres_ref[...] + rstd * (dxhat - xhat * jnp.mean(dxhat * xhat, axis=-1, keepdims=True))
        dx_ref[...] = dx
        dxb_ref[...] = (dx * scale).astype(BF16)

        @pl.when(pl.program_id(0) == 0)
        def _():
            dgain_ref[...] = jnp.zeros_like(dgain_ref)

        dgain_ref[...] += jnp.sum(dxn_v * xhat, axis=0, keepdims=True)
        if fused:
            pl.when(pl.program_id(0) == nt - 1)(finish)

    wide = pl.BlockSpec((tm, f), lambda i: (i, 0))
    wsp = pl.BlockSpec((f, d), lambda i: (0, 0))
    row = pl.BlockSpec((tm, d), lambda i: (i, 0))
    vec = pl.BlockSpec((1, d), lambda i: (0, 0))
    args = [dg, du, wg, wu, x, gain, dres] + ([chip_part] if fused else [])
    return pl.pallas_call(
        body, name=name, grid=(nt,),
        in_specs=[wide, wide, wsp, wsp, row, vec, row] + ([HBM_SPEC] if fused else []),
        out_specs=[row, row, vec] + ([HBM_SPEC] if fused else []),
        out_shape=[jax.ShapeDtypeStruct((t, d), F32), jax.ShapeDtypeStruct((t, d), BF16), jax.ShapeDtypeStruct((1, d), F32)]
        + ([jax.ShapeDtypeStruct(chip_part.shape, chip_part.dtype)] if fused else []),
        scratch_shapes=_chip_exchange_scratch() if fused else [],
        compiler_params=_cp(dimension_semantics=("arbitrary",)))(*args)


def _ffn_bwd(dout, dout_half, saved, gain, wg, wu, wd, tag, next_scale, make_chip_part=None):
    h, xn, gg, uu, act = saved
    dg, du = _swiglu_bwd(dout_half, wd, gg, uu, name=f"{tag}_dact")
    dwd = _mm([(act, dout_half)], ta=True, tm=256, tn=1024, out_dtype=BF16, name=f"{tag}_dwd")
    dwg = _mm([(dg, xn)], ta=True, tm=256, tn=1024, out_dtype=BF16, name=f"{tag}_dwg")
    dwu = _mm([(du, xn)], ta=True, tm=256, tn=1024, out_dtype=BF16, name=f"{tag}_dwu")
    chip_part = make_chip_part(dwg, dwu, dwd) if make_chip_part is not None else None
    dh, dh_b, dgain, *parts = _ffn_input_bwd(dg, du, wg, wu, h, gain, dout, chip_part, scale=next_scale,
                                             name=f"{tag}_input_bwd")
    return dh, dh_b, dwg, dwu, dwd, dgain, (parts[0] if parts else None)


def kernel(x, ffn_pre_norm, ffn_pre_w_gate, ffn_pre_w_up, ffn_pre_w_down, mix_norm, ffn_post_norm, ffn_post_w_gate, ffn_post_w_up, ffn_post_w_down, ab_w_in, ab_conv_w, ab_w_out, c_w_in, c_lower_bounds, c_out_norm, c_w_out, final_norm, loss_target, m_ffn_pre_norm, m_ffn_pre_w_gate, m_ffn_pre_w_up, m_ffn_pre_w_down, m_mix_norm, m_ffn_post_norm, m_ffn_post_w_gate, m_ffn_post_w_up, m_ffn_post_w_down, m_ab_w_in, m_ab_conv_w, m_ab_w_out, m_c_w_in, m_c_lower_bounds, m_c_out_norm, m_c_w_out, m_final_norm, v_ffn_pre_norm, v_ffn_pre_w_gate, v_ffn_pre_w_up, v_ffn_pre_w_down, v_mix_norm, v_ffn_post_norm, v_ffn_post_w_gate, v_ffn_post_w_up, v_ffn_post_w_down, v_ab_w_in, v_ab_conv_w, v_ab_w_out, v_c_w_in, v_c_lower_bounds, v_c_out_norm, v_c_w_out, v_final_norm):
    d = D_MODEL
    h0 = x[0]
    target = loss_target[0]
    core = lax.axis_index("c").astype(jnp.int32).reshape(1)

    big = [("pre_g", ffn_pre_w_gate, m_ffn_pre_w_gate, v_ffn_pre_w_gate),
           ("pre_u", ffn_pre_w_up, m_ffn_pre_w_up, v_ffn_pre_w_up),
           ("pre_d", ffn_pre_w_down, m_ffn_pre_w_down, v_ffn_pre_w_down),
           ("post_g", ffn_post_w_gate, m_ffn_post_w_gate, v_ffn_post_w_gate),
           ("post_u", ffn_post_w_up, m_ffn_post_w_up, v_ffn_post_w_up),
           ("post_d", ffn_post_w_down, m_ffn_post_w_down, v_ffn_post_w_down),
           ("ab_in", ab_w_in, m_ab_w_in, v_ab_w_in),
           ("ab_out", ab_w_out, m_ab_w_out, v_ab_w_out),
           ("c_in", c_w_in, m_c_w_in, v_c_w_in),
           ("c_out", c_w_out, m_c_w_out, v_c_w_out)]
    by_tag = {tag: (w, m, v) for tag, w, m, v in big}

    def layer_rows(tag):
        w = by_tag[tag][0]
        return w.size // d // w.shape[0]

    def layout(items):
        offs, off = {}, 0
        for item in items:
            offs[item] = off
            off += layer_rows(item[0])
        return offs, off

    ffn = [f"{pos}_{kind}" for pos in ("pre", "post") for kind in "gud"]
    early_items = [("pre_g", 0), ("pre_u", 0), ("pre_d", 0), ("ab_in", 0)]
    late_items = ([("pre_g", 1), ("pre_u", 1), ("pre_d", 1)] + [(f"post_{kind}", l) for l in (0, 1) for kind in "gud"]
                  + [("ab_out", 0), ("c_in", 0), ("c_out", 0)])
    grad_items = {"A": ([(tag, 1) for tag in ffn] + [(f"post_{kind}", 0) for kind in "gud"]
                        + [("c_in", 0), ("c_out", 0), ("ab_out", 0)]),
                  "B": [(f"pre_{kind}", 0) for kind in "gud"] + [("ab_in", 0)]}
    grad_offs = {k: layout(items)[0] for k, items in grad_items.items()}
    grad_conv_row = layout(grad_items["B"])[1]

    def conv_rows(a, split):
        flat = a.reshape(-1)
        if split:
            hi = flat.astype(BF16)
            flat = jnp.concatenate([hi, (flat - hi.astype(F32)).astype(BF16)])
        return jnp.zeros((16, d), flat.dtype).at[0, :flat.shape[0]].set(flat)

    nconv = ab_conv_w.size
    col_sharded = {"pre_g", "pre_u", "post_g", "post_u", "ab_in", "c_in"}

    def pack_rows(item):
        tag, layer = item
        a = by_tag[tag][0][layer]
        return (a.T if tag in col_sharded else a).reshape(-1, d).astype(BF16)

    early_pack = jnp.concatenate([pack_rows(item) for item in early_items] + [conv_rows(ab_conv_w, True)], axis=0)
    late_pack = jnp.concatenate([pack_rows(item) for item in late_items], axis=0)
    early_w = _all_gather(early_pack, [layer_rows(tag) for tag, _ in early_items] + [16], name="gather_early_weights")
    full = {item: g.reshape(-1, d) for item, g in zip(early_items, early_w)}
    ffn_w = {("pre", 0): tuple(full[f"pre_{kind}", 0] for kind in "gud")}
    w_ab_in = full["ab_in", 0]
    cg = early_w[-1][:, 0, :2 * nconv].astype(F32)
    conv_w = (cg[:, :nconv] + cg[:, nconv:]).reshape(8, 3, -1).transpose(1, 0, 2).reshape(3, -1)
    half = w_ab_in.shape[0] // 2
    w_a_in, w_b_in = w_ab_in[:half], w_ab_in[half:]
    aw = half // 3

    h1, s_pre0 = _ffn_fwd(h0, ffn_pre_norm[0:1], *ffn_w["pre", 0], "l0pre")
    hn0 = _rmsnorm_fwd(h1, mix_norm[0:1], name="l0_mix_norm")
    pa = _mm([(hn0, w_a_in)], tb=True, tn=1536, name="ab_proj_a")
    pb = _mm([(hn0, w_b_in)], tb=True, tn=1536, out_dtype=BF16, name="ab_proj_b")
    ya = _conv_fwd(pa, conv_w, name="conv_fwd")
    yb, ltot, *late_w = _attn_fwd(pb, late_pack, [layer_rows(tag) for tag, _ in late_items],
                                  name="attn_fwd_gather_late_weights")
    full.update({item: g.reshape(-1, d) for item, g in zip(late_items, late_w)})
    for pos, layer in (("post", 0), ("pre", 1), ("post", 1)):
        ffn_w[pos, layer] = tuple(full[f"{pos}_{kind}", layer] for kind in "gud")
    w_ab_out, w_c_in, w_c_out = full["ab_out", 0], full["c_in", 0], full["c_out", 0]
    h2 = _mm([(ya, w_ab_out[:aw]), (yb, w_ab_out[aw:])], residual=h1, tn=1024, name="ab_out")
    h3, s_post0 = _ffn_fwd(h2, ffn_post_norm[0:1], *ffn_w["post", 0], "l0post")
    h4, s_pre1 = _ffn_fwd(h3, ffn_pre_norm[1:2], *ffn_w["pre", 1], "l1pre")
    hn1 = _rmsnorm_fwd(h4, mix_norm[1:2], name="l1_mix_norm")
    pc = _mm([(hn1, w_c_in)], tb=True, tm=256, tn=4096, name="c_proj")
    yc, o_saved, states = _hgrn_fwd(pc, c_lower_bounds, c_out_norm, name="hgrn_fwd")
    h5 = _mm([(yc, w_c_out)], residual=h4, tn=1024, name="c_out")
    h6, s_post1 = _ffn_fwd(h5, ffn_post_norm[1:2], *ffn_w["post", 1], "l1post")
    dh6, dh6_b, d_final, loss_vec = _loss_head(h6, final_norm.reshape(1, d), target, name="loss_head")

    gw = {}
    dh5, dh5_b, gw["post_g", 1], gw["post_u", 1], gw["post_d", 1], d_post1, _ = _ffn_bwd(
        dh6, dh6_b, s_post1, ffn_post_norm[1:2], *ffn_w["post", 1], "l1post", 1.0)
    dyc = _mm([(dh5_b, w_c_out)], tb=True, tn=1024, name="c_out_dy")
    g_c_out = _mm([(yc, dh5_b)], ta=True, tm=256, tn=1024, out_dtype=BF16, name="c_out_dw")
    dcq, dcf, dci, dcg, dlb, d_onorm = _hgrn_bwd(pc, o_saved, states, dyc, c_lower_bounds, c_out_norm, name="hgrn_bwd")
    dparts = [dcq, dcf, dci, dcg]
    g_c_in = jnp.concatenate(
        [_mm([(dp, hn1)], ta=True, tm=256, tn=1024, out_dtype=BF16, name=f"c_in_dw{i}") for i, dp in enumerate(dparts)],
        axis=0)
    cw = w_c_in.shape[0] // 4
    dhn1 = _mm([(dp, w_c_in[i * cw:(i + 1) * cw]) for i, dp in enumerate(dparts)], tm=512, tn=1024, name="c_in_dx")
    dh4, dh4_b, d_mix1 = _rmsnorm_bwd(h4, mix_norm[1:2], dhn1, dh5, scale=MACARON, name="l1_mix_norm_bwd")
    dh3, dh3_b, gw["pre_g", 1], gw["pre_u", 1], gw["pre_d", 1], d_pre1, _ = _ffn_bwd(
        dh4, dh4_b, s_pre1, ffn_pre_norm[1:2], *ffn_w["pre", 1], "l1pre", MACARON)
    dh2, dh2_b, gw["post_g", 0], gw["post_u", 0], gw["post_d", 0], d_post0, _ = _ffn_bwd(
        dh3, dh3_b, s_post0, ffn_post_norm[0:1], *ffn_w["post", 0], "l0post", 1.0)
    dyab = _mm([(dh2_b, w_ab_out)], tb=True, tn=1024, name="ab_out_dy")
    g_ab_out = jnp.concatenate([_mm([(ya, dh2_b)], ta=True, tm=256, tn=1024, out_dtype=BF16, name="ab_out_dw_a"),
                                _mm([(yb, dh2_b)], ta=True, tm=256, tn=1024, out_dtype=BF16, name="ab_out_dw_b")], axis=0)
    dab, dac, dax, g_conv = _conv_bwd(pa, dyab, conv_w, name="conv_bwd")

    def chip_partials(key, grads, extra=()):
        gpack = jnp.concatenate([grads[item].reshape(8, -1, d) for item in grad_items[key]] + list(extra), axis=1)
        send = gpack.reshape(4, 2, gpack.shape[1], d)
        from_sibling = _sibling_exchange(send, name=f"grad{key}_sibling_exchange")
        return _pair_add(send, from_sibling, core, name=f"grad{key}_pair_add")

    gw["c_in", 0], gw["c_out", 0], gw["ab_out", 0] = g_c_in, g_c_out, g_ab_out
    chip_part_a = chip_partials("A", gw)
    dq, dk, dv, parts_a = _attn_bwd(pb, dyab, ltot, chip_part_a, name="attn_bwd_exchange_grads_a")
    dparts = [dab, dac, dax, dq, dk, dv]
    g_ab_in = jnp.concatenate(
        [_mm([(dp, hn0)], ta=True, tm=256, tn=1024, out_dtype=BF16, name=f"ab_in_dw{i}") for i, dp in enumerate(dparts)],
        axis=0)
    dhn0 = _mm([(dp, w_ab_in[i * aw:(i + 1) * aw]) for i, dp in enumerate(dparts)], tm=512, tn=1024, name="ab_in_dx")
    dh1, dh1_b, d_mix0 = _rmsnorm_bwd(h1, mix_norm[0:1], dhn0, dh2, scale=MACARON, name="l0_mix_norm_bwd")
    gw["ab_in", 0] = g_ab_in
    gconv_own = g_conv.reshape(3, 8, -1).transpose(1, 0, 2).reshape(8, -1)
    conv_piece = jnp.zeros((8, 16, d), F32).at[:, 0, :nconv].set(gconv_own).astype(BF16)

    def chip_part_b(dwg, dwu, dwd):
        gw["pre_g", 0], gw["pre_u", 0], gw["pre_d", 0] = dwg, dwu, dwd
        return chip_partials("B", gw, [conv_piece])

    dh0, _, _, _, _, d_pre0, parts_b = _ffn_bwd(
        dh1, dh1_b, s_pre0, ffn_pre_norm[0:1], *ffn_w["pre", 0], "l0pre", 1.0, chip_part_b)

    g_sum = {"A": _grad_sum(parts_a, name="gradA_sum"), "B": _grad_sum(parts_b, name="gradB_sum")}

    upd = {}
    for tag, w, m, v in big:
        nl = layer_rows(tag)
        view = (lambda a: jnp.swapaxes(a, 1, 2)) if tag in col_sharded else (lambda a: a)
        where = {layer: (key, grad_offs[key][tag, layer])
                 for key in ("A", "B") for t2, layer in grad_items[key] if t2 == tag}
        res = None
        for layer in sorted(where):
            key, off = where[layer]
            res = _adamw_shard(g_sum[key], off, view(w), view(m), view(v), layer, res, name=f"adamw_{tag}{layer}")
        g_nat = jnp.stack([g_sum[where[layer][0]][where[layer][1]:where[layer][1] + nl] for layer in sorted(where)])
        upd[tag] = [view(a) for a in [g_nat] + list(res)]
    res = _adamw_shard(g_sum["B"], grad_conv_row, *(conv_rows(a, False)[None] for a in (ab_conv_w, m_ab_conv_w, v_ab_conv_w)),
                       0, None, name="adamw_conv")
    g_conv_rows = g_sum["B"][grad_conv_row:grad_conv_row + 16]
    upd["conv"] = [r[0, :nconv].reshape(ab_conv_w.shape) for r in [g_conv_rows] + [r[0] for r in res]]

    def small_pack(pre, mix, post, final, lbs, onorm):
        def slot(parts):
            out, r = jnp.zeros((SLOT, d), F32), 0
            for a in (parts if isinstance(parts, tuple) else (parts,)):
                out = out.at[r:r + a.shape[0], :a.shape[1]].set(a)
                r += a.shape[0]
            return out

        return jnp.concatenate([slot(pre), slot(mix), slot(post), slot(final.reshape(1, d)), slot(lbs), slot(onorm)], axis=0)

    d_on = d_onorm.reshape(-1, c_out_norm.shape[1]).sum(axis=0, keepdims=True)
    gsmall = small_pack((d_pre0, d_pre1), (d_mix0, d_mix1), (d_post0, d_post1), d_final, dlb, d_on)
    gsmall_all = _all_gather(gsmall, name="gather_small_grads")
    sres = _small_update(
        gsmall_all,
        small_pack(ffn_pre_norm, mix_norm, ffn_post_norm, final_norm, c_lower_bounds, c_out_norm),
        small_pack(m_ffn_pre_norm, m_mix_norm, m_ffn_post_norm, m_final_norm, m_c_lower_bounds, m_c_out_norm),
        small_pack(v_ffn_pre_norm, v_mix_norm, v_ffn_post_norm, v_final_norm, v_c_lower_bounds, v_c_out_norm),
        name="small_update")

    def small_out(r):
        return {"pre_norm": r[0:2], "mix_norm": r[SLOT:SLOT + 2], "post_norm": r[2 * SLOT:2 * SLOT + 2],
                "final": r[3 * SLOT], "lb": r[ROW_LB:ROW_LB + 2], "onorm": r[5 * SLOT:5 * SLOT + 1, :c_out_norm.shape[1]]}

    small = [small_out(r) for r in sres]
    outs = []
    for k in range(4):
        s = small[k]
        outs += [s["pre_norm"], upd["pre_g"][k], upd["pre_u"][k], upd["pre_d"][k], s["mix_norm"], s["post_norm"],
                 upd["post_g"][k], upd["post_u"][k], upd["post_d"][k], upd["ab_in"][k], upd["conv"][k],
                 upd["ab_out"][k], upd["c_in"][k], s["lb"], s["onorm"], upd["c_out"][k], s["final"]]
    loss = lax.psum(loss_vec[0, 0], ("x", "y", "c"))
    return (loss, dh0[None], *outs)
```

```python
import functools
import math

import jax
import jax.numpy as jnp
from jax import lax
from jax.experimental import pallas as pl
from jax.experimental.pallas import tpu as pltpu

F32 = jnp.float32
BF16 = jnp.bfloat16
MESH = pl.DeviceIdType.MESH

RMS_EPS = 1e-6
MACARON = 0.5
LANES = 128
CHUNK = 64
N_LEVELS = 6
HGRN_HEADS = 2
SB_KEYS = 256
ADAM_LR, ADAM_B1, ADAM_B2, ADAM_EPS, ADAM_WD, ADAM_STEP = 0.001, 0.9, 0.999, 1e-08, 0.01, 10
VMEM_LIMIT = 48 * 1024 * 1024


def _cp(**kw):
    return pltpu.CompilerParams(vmem_limit_bytes=VMEM_LIMIT, **kw)


def _sigmoid(x):
    return 1.0 / (1.0 + jnp.exp(-x))


def _bf(x):
    return x if x.dtype == BF16 else x.astype(BF16)


def _split3(x):
    hi = x.astype(BF16)
    r1 = x - hi.astype(F32)
    mid = r1.astype(BF16)
    lo = (r1 - mid.astype(F32)).astype(BF16)
    return hi, mid, lo


def _dot(a, b, ca=1, cb=0):
    return lax.dot_general(a, b, (((ca,), (cb,)), ((), ())), preferred_element_type=F32)


def _dot_exact_lhs(m, x):
    hi, mid, lo = _split3(x)
    return _dot(m, hi) + _dot(m, mid) + _dot(m, lo)


def _dot_exact_rhs(x, m):
    hi, mid, lo = _split3(x)
    return _dot(hi, m) + _dot(mid, m) + _dot(lo, m)


def _mm(terms, *, name, ta=False, tb=False, out_dtype=F32, residual=None, alpha=1.0, tm=512, tn=512):
    nt = len(terms)
    a0, b0 = terms[0]
    m = a0.shape[1] if ta else a0.shape[0]
    n = b0.shape[0] if tb else b0.shape[1]
    tm, tn = min(tm, m), min(tn, n)
    assert m % tm == 0 and n % tn == 0, (name, m, n, tm, tn)
    has_res = residual is not None

    def body(*refs):
        o_ref = refs[-1]
        acc = None
        for i in range(nt):
            a = _bf(refs[2 * i][...])
            b = _bf(refs[2 * i + 1][...])
            p = _dot(a, b, 0 if ta else 1, 1 if tb else 0)
            acc = p if acc is None else acc + p
        if alpha != 1.0:
            acc = acc * alpha
        if has_res:
            acc = acc + refs[2 * nt][...]
        o_ref[...] = acc.astype(out_dtype)

    in_specs, args = [], []
    for a, b in terms:
        k = a.shape[0] if ta else a.shape[1]
        assert (b.shape[1] if tb else b.shape[0]) == k, (name, a.shape, b.shape)
        in_specs.append(pl.BlockSpec((k, tm), lambda i, j: (0, i)) if ta else pl.BlockSpec((tm, k), lambda i, j: (i, 0)))
        in_specs.append(pl.BlockSpec((tn, k), lambda i, j: (j, 0)) if tb else pl.BlockSpec((k, tn), lambda i, j: (0, j)))
        args += [a, b]
    if has_res:
        in_specs.append(pl.BlockSpec((tm, tn), lambda i, j: (i, j)))
        args.append(residual)
    return pl.pallas_call(
        body, name=name, grid=(m // tm, n // tn), in_specs=in_specs,
        out_specs=pl.BlockSpec((tm, tn), lambda i, j: (i, j)),
        out_shape=jax.ShapeDtypeStruct((m, n), out_dtype), compiler_params=_cp())(*args)


def _rmsnorm_fwd(x, gain, *, name, tm=512):
    t, d = x.shape
    tm = min(tm, t)

    def body(x_ref, g_ref, o_ref):
        xv = x_ref[...]
        rstd = lax.rsqrt(jnp.mean(xv * xv, axis=-1, keepdims=True) + RMS_EPS)
        o_ref[...] = (xv * rstd * g_ref[...]).astype(BF16)

    return pl.pallas_call(
        body, name=name, grid=(t // tm,),
        in_specs=[pl.BlockSpec((tm, d), lambda i: (i, 0)), pl.BlockSpec((1, d), lambda i: (0, 0))],
        out_specs=pl.BlockSpec((tm, d), lambda i: (i, 0)),
        out_shape=jax.ShapeDtypeStruct((t, d), BF16), compiler_params=_cp())(x, gain)


def _rmsnorm_bwd(x, gain, dxn, dres, *, name, scale, tm=512):
    t, d = x.shape
    tm = min(tm, t)

    def body(x_ref, g_ref, dxn_ref, dres_ref, dx_ref, dxb_ref, dg_ref):
        xv = x_ref[...]
        rstd = lax.rsqrt(jnp.mean(xv * xv, axis=-1, keepdims=True) + RMS_EPS)
        xhat = xv * rstd
        dxn_v = dxn_ref[...]
        dxhat = dxn_v * g_ref[...]
        dx = dres_ref[...] + rstd * (dxhat - xhat * jnp.mean(dxhat * xhat, axis=-1, keepdims=True))
        dx_ref[...] = dx
        dxb_ref[...] = (dx * scale).astype(BF16)

        @pl.when(pl.program_id(0) == 0)
        def _():
            dg_ref[...] = jnp.zeros_like(dg_ref)

        dg_ref[...] += jnp.sum(dxn_v * xhat, axis=0, keepdims=True)

    row = pl.BlockSpec((tm, d), lambda i: (i, 0))
    vec = pl.BlockSpec((1, d), lambda i: (0, 0))
    return pl.pallas_call(
        body, name=name, grid=(t // tm,), in_specs=[row, vec, row, row], out_specs=[row, row, vec],
        out_shape=[jax.ShapeDtypeStruct((t, d), F32), jax.ShapeDtypeStruct((t, d), BF16), jax.ShapeDtypeStruct((1, d), F32)],
        compiler_params=_cp())(x, gain, dxn, dres)


def _loss_head(h, gain, target, *, name, tm=512):
    t, d = h.shape
    tm = min(tm, t)

    def body(h_ref, g_ref, t_ref, dh_ref, dhb_ref, dg_ref, loss_ref):
        hv = h_ref[...]
        rstd = lax.rsqrt(jnp.mean(hv * hv, axis=-1, keepdims=True) + RMS_EPS)
        xhat = hv * rstd
        err = xhat * g_ref[...] - t_ref[...]
        dy = err * (1.0 / d)
        dxhat = dy * g_ref[...]
        dh = rstd * (dxhat - xhat * jnp.mean(dxhat * xhat, axis=-1, keepdims=True))
        dh_ref[...] = dh
        dhb_ref[...] = (dh * MACARON).astype(BF16)

        @pl.when(pl.program_id(0) == 0)
        def _():
            dg_ref[...] = jnp.zeros_like(dg_ref)
            loss_ref[...] = jnp.zeros_like(loss_ref)

        dg_ref[...] += jnp.sum(dy * xhat, axis=0, keepdims=True)
        part = jnp.sum(jnp.sum(err * err, axis=-1, keepdims=True), axis=0, keepdims=True) * (0.5 / d)
        loss_ref[...] += jnp.broadcast_to(part, loss_ref.shape)

    row = pl.BlockSpec((tm, d), lambda i: (i, 0))
    vec = pl.BlockSpec((1, d), lambda i: (0, 0))
    return pl.pallas_call(
        body, name=name, grid=(t // tm,), in_specs=[row, vec, row],
        out_specs=[row, row, vec, pl.BlockSpec((1, LANES), lambda i: (0, 0))],
        out_shape=[jax.ShapeDtypeStruct((t, d), F32), jax.ShapeDtypeStruct((t, d), BF16), jax.ShapeDtypeStruct((1, d), F32),
                   jax.ShapeDtypeStruct((1, LANES), F32)],
        compiler_params=_cp())(h, gain, target)


def _norm_gate_up(x, gain, wg, wu, *, name, tm=512, tf=1408):
    t, d = x.shape
    f = wg.shape[0]
    tm, tf = min(tm, t), min(tf, f)
    assert f % tf == 0

    def body(x_ref, g_ref, wg_ref, wu_ref, xn_ref, gg_ref, uu_ref, act_ref):
        @pl.when(pl.program_id(1) == 0)
        def _():
            xv = x_ref[...]
            rstd = lax.rsqrt(jnp.mean(xv * xv, axis=-1, keepdims=True) + RMS_EPS)
            xn_ref[...] = (xv * rstd * g_ref[...]).astype(BF16)

        xn = xn_ref[...]
        gv = _dot(xn, wg_ref[...], 1, 1)
        uv = _dot(xn, wu_ref[...], 1, 1)
        gg_ref[...] = gv.astype(BF16)
        uu_ref[...] = uv.astype(BF16)
        act_ref[...] = (gv * _sigmoid(gv) * uv).astype(BF16)

    row = pl.BlockSpec((tm, d), lambda i, j: (i, 0))
    wsp = pl.BlockSpec((tf, d), lambda i, j: (j, 0))
    osp = pl.BlockSpec((tm, tf), lambda i, j: (i, j))
    return pl.pallas_call(
        body, name=name, grid=(t // tm, f // tf),
        in_specs=[row, pl.BlockSpec((1, d), lambda i, j: (0, 0)), wsp, wsp],
        out_specs=[row, osp, osp, osp],
        out_shape=[jax.ShapeDtypeStruct((t, d), BF16)] + [jax.ShapeDtypeStruct((t, f), BF16)] * 3,
        compiler_params=_cp())(x, gain, wg, wu)


def _swiglu_bwd(dout, wd, gg, uu, *, name, tm=512, tf=1408):
    t, d = dout.shape
    f = wd.shape[0]
    tm, tf = min(tm, t), min(tf, f)

    def body(do_ref, wd_ref, g_ref, u_ref, dg_ref, du_ref):
        dact = _dot(do_ref[...], wd_ref[...], 1, 1)
        gv = g_ref[...].astype(F32)
        uv = u_ref[...].astype(F32)
        sg = _sigmoid(gv)
        dg_ref[...] = (dact * uv * (sg * (1.0 + gv * (1.0 - sg)))).astype(BF16)
        du_ref[...] = (dact * (gv * sg)).astype(BF16)

    osp = pl.BlockSpec((tm, tf), lambda i, j: (i, j))
    return pl.pallas_call(
        body, name=name, grid=(t // tm, f // tf),
        in_specs=[pl.BlockSpec((tm, d), lambda i, j: (i, 0)), pl.BlockSpec((tf, d), lambda i, j: (j, 0)), osp, osp],
        out_specs=[osp, osp], out_shape=[jax.ShapeDtypeStruct((t, f), BF16)] * 2,
        compiler_params=_cp())(dout, wd, gg, uu)


def _shift_down(x, n):
    rows = lax.broadcasted_iota(jnp.int32, x.shape, 0)
    return jnp.where(rows >= n, pltpu.roll(x, n, 0), 0.0)


def _shift_up(x, n):
    t = x.shape[0]
    rows = lax.broadcasted_iota(jnp.int32, x.shape, 0)
    return jnp.where(rows < t - n, pltpu.roll(x, t - n, 0), 0.0)


def _conv_fwd(pa, conv_w, *, name):
    t = pa.shape[0]
    nb = pa.shape[1] // 3 // LANES

    def body(b_ref, c_ref, x_ref, w_ref, y_ref):
        u = c_ref[...] * x_ref[...]
        w = w_ref[...]
        conv = w[2:3, :] * u + w[1:2, :] * _shift_down(u, 1) + w[0:1, :] * _shift_down(u, 2)
        y_ref[...] = (b_ref[...] * conv).astype(BF16)

    def col(off):
        return pl.BlockSpec((t, LANES), lambda j: (0, off + j))

    return pl.pallas_call(
        body, name=name, grid=(nb,),
        in_specs=[col(0), col(nb), col(2 * nb), pl.BlockSpec((3, LANES), lambda j: (0, j))],
        out_specs=pl.BlockSpec((t, LANES), lambda j: (0, j)),
        out_shape=jax.ShapeDtypeStruct((t, nb * LANES), BF16), compiler_params=_cp())(pa, pa, pa, conv_w)


def _conv_bwd(pa, dy, conv_w, *, name):
    t = pa.shape[0]
    nb = pa.shape[1] // 3 // LANES

    def body(b_ref, c_ref, x_ref, dy_ref, w_ref, db_ref, dc_ref, dx_ref, dw_ref):
        cv, xv = c_ref[...], x_ref[...]
        u = cv * xv
        u1, u2 = _shift_down(u, 1), _shift_down(u, 2)
        w = w_ref[...]
        conv = w[2:3, :] * u + w[1:2, :] * u1 + w[0:1, :] * u2
        dyv = dy_ref[...]
        db_ref[...] = (dyv * conv).astype(BF16)
        dconv = dyv * b_ref[...]
        du = w[2:3, :] * dconv + w[1:2, :] * _shift_up(dconv, 1) + w[0:1, :] * _shift_up(dconv, 2)
        dc_ref[...] = (du * xv).astype(BF16)
        dx_ref[...] = (du * cv).astype(BF16)
        dw_ref[0:1, :] = jnp.sum(dconv * u2, axis=0, keepdims=True)
        dw_ref[1:2, :] = jnp.sum(dconv * u1, axis=0, keepdims=True)
        dw_ref[2:3, :] = jnp.sum(dconv * u, axis=0, keepdims=True)

    def col(off):
        return pl.BlockSpec((t, LANES), lambda j: (0, off + j))

    osp = pl.BlockSpec((t, LANES), lambda j: (0, j))
    wsp = pl.BlockSpec((3, LANES), lambda j: (0, j))
    return pl.pallas_call(
        body, name=name, grid=(nb,), in_specs=[col(0), col(nb), col(2 * nb), col(0), wsp],
        out_specs=[osp, osp, osp, wsp],
        out_shape=[jax.ShapeDtypeStruct((t, nb * LANES), BF16)] * 3 + [jax.ShapeDtypeStruct((3, nb * LANES), F32)],
        compiler_params=_cp())(pa, pa, pa, dy, conv_w)


def _sb_consts():
    j = lax.broadcasted_iota(jnp.int32, (SB_KEYS, SB_KEYS), 0)
    s = lax.broadcasted_iota(jnp.int32, (SB_KEYS, SB_KEYS), 1)
    after = (j > s).astype(BF16)
    upto = (j <= s).astype(BF16)
    before = (j < s).astype(BF16)
    return after, jnp.stack([upto, before])


def _log_sigmoid(z):
    return jnp.minimum(z, 0.0) - jnp.log(1.0 + jnp.exp(-jnp.abs(z)))


def _attn_fwd(pb, late_pack, seg_rows, *, name, tq=256):
    t = pb.shape[0]
    npair = pb.shape[1] // 3 // LANES
    tq = min(tq, t)
    nq = t // tq
    cmat, _ = _sb_consts()
    scale = 1.0 / math.sqrt(LANES // 2)

    nseg = len(seg_rows)

    def body(q_ref, k_ref, v_ref, c_ref, late_ref, y_ref, lt_ref, *rest):
        i = pl.program_id(1)
        pair = pl.program_id(0)
        scratch = rest[nseg:nseg + 4]
        start, forward, finish = _gather_phases(late_ref, rest[:nseg], seg_rows, *rest[nseg + 4:])
        pl.when((pair == 0) & (i == 0))(start)
        pl.when((pair == npair - 1) & (i == nq // 2))(forward)
        lane = lax.broadcasted_iota(jnp.int32, (tq, LANES), 1)
        rowpos = i * tq + lax.broadcasted_iota(jnp.int32, (tq, SB_KEYS), 0)
        colid = lax.broadcasted_iota(jnp.int32, (tq, SB_KEYS), 1)
        q2 = q_ref[...] * jnp.asarray(scale, BF16)
        cm = c_ref[...]
        hi_lanes = lane >= LANES // 2
        qhs = [jnp.where(hi_lanes == (hh == 1), q2, jnp.zeros_like(q2)) for hh in range(2)]
        per_q = tq // SB_KEYS

        def blk(jb):
            return pl.ds(pl.multiple_of(jb * SB_KEYS, SB_KEYS), SB_KEYS)

        zbuf, wbuf, accbuf, runbuf = scratch

        def scores(jb):
            kb = k_ref[blk(jb), :]
            for hh in range(2):
                zbuf[hh] = _dot(qhs[hh], kb, 1, 1)

        def values(jb):
            vb = v_ref[blk(jb), :]
            for hh in range(2):
                accbuf[hh] += _dot(wbuf[hh], vb)

        def trip(jb, masked, first=False):
            mask = (jb * SB_KEYS + colid) < rowpos if masked else None
            if not first:
                values(jb + 1)
            pre, css = [], []
            for hh in range(2):
                z = zbuf[hh]
                lb = _log_sigmoid(z)
                lk = lb - z
                if masked:
                    lk = jnp.where(mask, lk, 0.0)
                lk_hi, lk_lo = _split2(lk)
                css.append(_dot(lk_hi, cm) + _dot(lk_lo, cm))
                run = runbuf[hh]
                pre.append(lb + run)
                runbuf[hh] = run + jnp.sum(lk, axis=1, keepdims=True)
            scores(jnp.maximum(jb - 1, 0))
            for hh in range(2):
                w = jnp.exp(pre[hh] + css[hh])
                if masked:
                    w = jnp.where(mask, w, 0.0)
                wbuf[hh] = w.astype(BF16)

        nfull = i * per_q
        accbuf[...] = jnp.zeros_like(accbuf)
        runbuf[...] = jnp.zeros_like(runbuf)
        scores(nfull + per_q - 1)
        for dblk in reversed(range(per_q)):
            trip(nfull + dblk, True, first=dblk == per_q - 1)

        def full_block(n, carry):
            trip(nfull - 1 - n, False)
            return carry

        lax.fori_loop(0, nfull, full_block, 0)
        values(0)
        y_ref[...] = jnp.where(hi_lanes, accbuf[1], accbuf[0]).astype(BF16)
        lt_ref[...] = jnp.where(hi_lanes, runbuf[1], runbuf[0])
        pl.when((pair == npair - 1) & (i == nq - 1))(finish)

    return pl.pallas_call(
        body, name=name, grid=(npair, nq),
        in_specs=[pl.BlockSpec((tq, LANES), lambda p, i: (i, p)),
                  pl.BlockSpec((t, LANES), lambda p, i: (0, npair + p)),
                  pl.BlockSpec((t, LANES), lambda p, i: (0, 2 * npair + p)),
                  pl.BlockSpec((SB_KEYS, SB_KEYS), lambda p, i: (0, 0)),
                  HBM_SPEC],
        out_specs=[pl.BlockSpec((tq, LANES), lambda p, i: (i, p))] * 2 + [HBM_SPEC] * nseg,
        out_shape=[jax.ShapeDtypeStruct((t, npair * LANES), BF16), jax.ShapeDtypeStruct((t, npair * LANES), F32),
                   ] + [jax.ShapeDtypeStruct((8, n, late_pack.shape[1]), late_pack.dtype) for n in seg_rows],
        scratch_shapes=[pltpu.VMEM((2, tq, SB_KEYS), F32), pltpu.VMEM((2, tq, SB_KEYS), BF16),
                        pltpu.VMEM((2, tq, LANES), F32), pltpu.VMEM((2, tq, 1), F32)] + _gather_scratch(),
        compiler_params=_cp(dimension_semantics=("arbitrary", "arbitrary")))(pb, pb, pb, cmat, late_pack)


def _attn_bwd(pb, dy, ltot, chip_part, *, name, tq=256):
    t = pb.shape[0]
    npair = pb.shape[1] // 3 // LANES
    tq = min(tq, t)
    nq = t // tq
    _, cmats = _sb_consts()
    scale = 1.0 / math.sqrt(LANES // 2)

    def body(q_ref, k_ref, v_ref, dy_ref, lt_ref, c_ref, part_ref, dq_ref, dk_ref, dv_ref, parts_ref, dk_acc, dv_acc, *rest):
        i = pl.program_id(1)
        pair = pl.program_id(0)
        scratch = rest[:6]
        start, finish = _chip_exchange_phases(part_ref, parts_ref, *rest[6:])
        pl.when((pair == 0) & (i == 0))(start)

        @pl.when(i == 0)
        def _():
            dk_acc[...] = jnp.zeros_like(dk_acc)
            dv_acc[...] = jnp.zeros_like(dv_acc)

        lane = lax.broadcasted_iota(jnp.int32, (tq, LANES), 1)
        rowpos = i * tq + lax.broadcasted_iota(jnp.int32, (tq, SB_KEYS), 0)
        colid = lax.broadcasted_iota(jnp.int32, (tq, SB_KEYS), 1)
        q2 = q_ref[...] * jnp.asarray(scale, BF16)
        do2 = dy_ref[...].astype(BF16)
        ltv = lt_ref[...]
        c_upto, c_before = c_ref[0], c_ref[1]
        hi_lanes = lane >= LANES // 2
        sels = [hi_lanes == (hh == 1) for hh in range(2)]
        qhs = [jnp.where(s, q2, jnp.zeros_like(q2)) for s in sels]
        dohs = [jnp.where(s, do2, jnp.zeros_like(do2)) for s in sels]
        lts = [ltv[:, 0:1], ltv[:, LANES // 2:LANES // 2 + 1]]
        per_q = tq // SB_KEYS

        def blk(jb):
            return pl.ds(pl.multiple_of(jb * SB_KEYS, SB_KEYS), SB_KEYS)

        zbuf, dabuf, dzbuf, abuf, dqbuf, sumbuf = scratch

        def scores(jb):
            kb, vb = k_ref[blk(jb), :], v_ref[blk(jb), :]
            for hh in range(2):
                zbuf[hh] = _dot(qhs[hh], kb, 1, 1)
                dabuf[hh] = _dot(dohs[hh], vb, 1, 1)

        def products(jb):
            kb = k_ref[blk(jb), :]
            dk_acc[blk(jb), :] += _dot(dzbuf[0], qhs[0], 0, 0) + _dot(dzbuf[1], qhs[1], 0, 0)
            dv_acc[blk(jb), :] += _dot(abuf[0], dohs[0], 0, 0) + _dot(abuf[1], dohs[1], 0, 0)
            for hh in range(2):
                dqbuf[hh] += _dot(dzbuf[hh], kb)

        def trip(jb, masked):
            mask = (jb * SB_KEYS + colid) < rowpos if masked else None
            products(jnp.maximum(jb - 1, 0))
            lbs, css, es, ces = [], [], [], []
            for hh in range(2):
                z = zbuf[hh]
                lb = _log_sigmoid(z)
                lk = lb - z
                if masked:
                    lk = jnp.where(mask, lk, 0.0)
                lk_hi, lk_lo = _split2(lk)
                css.append(_dot(lk_hi, c_upto) + _dot(lk_lo, c_upto))
                csum = sumbuf[2 * hh]
                lbs.append((lb, lb + (lts[hh] - csum)))
                sumbuf[2 * hh] = csum + jnp.sum(lk, axis=1, keepdims=True)
            for hh in range(2):
                a = jnp.exp(lbs[hh][1] - css[hh])
                if masked:
                    a = jnp.where(mask, a, 0.0)
                e = a * dabuf[hh]
                e_hi, e_lo = _split2(e)
                ces.append(_dot(e_hi, c_before) + _dot(e_lo, c_before))
                abuf[hh] = a.astype(BF16)
                es.append(e)
            scores(jnp.minimum(jb + 1, last))
            for hh in range(2):
                prun = sumbuf[2 * hh + 1]
                beta = jnp.exp(lbs[hh][0])
                dz = es[hh] * (1.0 - beta) - (prun + ces[hh]) * beta
                if masked:
                    dz = jnp.where(mask, dz, 0.0)
                dzbuf[hh] = dz.astype(BF16)
                sumbuf[2 * hh + 1] = prun + jnp.sum(es[hh], axis=1, keepdims=True)

        nfull = i * per_q
        last = nfull + per_q - 1
        for buf in (dzbuf, abuf, dqbuf, sumbuf):
            buf[...] = jnp.zeros_like(buf)
        scores(0)

        def full_block(jb, carry):
            trip(jb, False)
            return carry

        lax.fori_loop(0, nfull, full_block, 0)
        for dblk in range(per_q):
            trip(nfull + dblk, True)
        products(last)
        dq_ref[...] = (jnp.where(hi_lanes, dqbuf[1], dqbuf[0]) * scale).astype(BF16)

        @pl.when(i == nq - 1)
        def _():
            dk_ref[...] = dk_acc[...].astype(BF16)
            dv_ref[...] = dv_acc[...].astype(BF16)

        pl.when((pair == npair - 1) & (i == nq - 1))(finish)

    blk = pl.BlockSpec((tq, LANES), lambda p, i: (i, p))
    full = pl.BlockSpec((t, LANES), lambda p, i: (0, p))
    return pl.pallas_call(
        body, name=name, grid=(npair, nq),
        in_specs=[blk,
                  pl.BlockSpec((t, LANES), lambda p, i: (0, npair + p)),
                  pl.BlockSpec((t, LANES), lambda p, i: (0, 2 * npair + p)),
                  pl.BlockSpec((tq, LANES), lambda p, i: (i, npair + p)),
                  blk,
                  pl.BlockSpec((2, SB_KEYS, SB_KEYS), lambda p, i: (0, 0, 0)),
                  HBM_SPEC],
        out_specs=[blk, full, full, HBM_SPEC],
        out_shape=[jax.ShapeDtypeStruct((t, npair * LANES), BF16)] * 3 + [jax.ShapeDtypeStruct(chip_part.shape, chip_part.dtype)],
        scratch_shapes=[pltpu.VMEM((t, LANES), F32), pltpu.VMEM((t, LANES), F32),
                        pltpu.VMEM((2, tq, SB_KEYS), F32), pltpu.VMEM((2, tq, SB_KEYS), F32),
                        pltpu.VMEM((2, tq, SB_KEYS), BF16), pltpu.VMEM((2, tq, SB_KEYS), BF16),
                        pltpu.VMEM((2, tq, LANES), F32), pltpu.VMEM((4, tq, 1), F32)] + _chip_exchange_scratch(),
        compiler_params=_cp(dimension_semantics=("arbitrary", "arbitrary")))(pb, pb, pb, dy, ltot, cmats, chip_part)


def _hgrn_consts():
    t = lax.broadcasted_iota(jnp.int32, (CHUNK, CHUNK), 0)
    s = lax.broadcasted_iota(jnp.int32, (CHUNK, CHUNK), 1)
    masks = []
    for lvl in range(N_LEVELS):
        half = CHUNK >> (lvl + 1)
        same = (t // (2 * half)) == (s // (2 * half))
        masks.append((same & (t % (2 * half) >= half) & (s % (2 * half) < half)).astype(F32))
    masks.append((t == s).astype(F32))
    prefix = (s <= t).astype(BF16)
    suffix = (s >= t).astype(BF16)
    return prefix, jnp.stack(masks), suffix


def _hgrn_gates(qr, fr, lbv):
    sg = _sigmoid(fr)
    fval = lbv + (1.0 - lbv) * sg
    kk = (1.0 - lbv) * _sigmoid(-fr)
    sq = _sigmoid(qr)
    return sg, fval, jnp.log(fval), kk, sq, qr * sq


def _lower_bound(c_ref):
    c = c_ref[...]
    mx = jnp.max(c, axis=0, keepdims=True)
    ex = jnp.exp(c - mx)
    return ex[1:2, :] / jnp.sum(ex, axis=0, keepdims=True)


def _level_ref(b, lvl):
    half = CHUNK >> (lvl + 1)
    seg = 2 * half
    if seg >= 8:
        b3 = b.reshape(CHUNK // seg, seg, LANES)
        return jnp.broadcast_to(b3[:, half - 1:half, :], b3.shape).reshape(CHUNK, LANES)
    pos = lax.broadcasted_iota(jnp.int32, b.shape, 0) % seg
    out = b
    for p in range(seg):
        if p != half - 1:
            out = jnp.where(pos == p, pltpu.roll(b, (p - (half - 1)) % CHUNK, 0), out)
    return out


def _hgrn_levels(b, qs, kk):
    out = []
    for lvl in range(N_LEVELS):
        bref = _level_ref(b, lvl)
        eq = jnp.exp(jnp.minimum(b - bref, 0.0))
        ek = jnp.exp(jnp.minimum(bref - b, 0.0))
        out.append((qs * eq, kk * ek, eq, ek))
    out.append((qs, kk, None, None))
    return out


def _split2(x):
    hi = x.astype(BF16)
    return hi, (x - hi.astype(F32)).astype(BF16)


def _hgrn_fwd(pc, c_lb, out_norm, *, name, tc=512):
    t = pc.shape[0]
    nh = pc.shape[1] // 4 // LANES
    tc = min(tc, t)
    nch = tc // CHUNK
    cum_all, masks, _ = _hgrn_consts()

    def body(q_ref, f_ref, i_ref, g_ref, lb_ref, on_ref, cum_ref, m_ref, y_ref, o_ref, st_ref, state):
        @pl.when(pl.program_id(1) == 0)
        def _():
            state[...] = jnp.zeros_like(state)

        lbv = _lower_bound(lb_ref)
        onv = on_ref[...]

        def chunk(c, carry):
            rows = pl.ds(pl.multiple_of(c * CHUNK, CHUNK), CHUNK)
            for hh in range(HGRN_HEADS):
                lanes = slice(hh * LANES, (hh + 1) * LANES)
                _, _, g, kk, _, qs = _hgrn_gates(q_ref[rows, lanes], f_ref[rows, lanes], lbv[:, lanes])
                vb = i_ref[rows, lanes].astype(BF16)
                b = _dot_exact_lhs(cum_ref[...], g)
                scores = jnp.zeros((CHUNK, CHUNK), F32)
                for lvl, (ql, kl, _, _) in enumerate(_hgrn_levels(b, qs, kk)):
                    scores = scores + _dot(ql.astype(BF16), kl.astype(BF16), 1, 1) * m_ref[lvl]
                st = state[hh]
                st_ref[hh, c] = st
                o = _dot(scores.astype(BF16), vb) + _dot((qs * jnp.exp(b)).astype(BF16), st.astype(BF16), 1, 1)
                blast = b[CHUNK - 1:CHUNK, :]
                kdec = (kk * jnp.exp(blast - b)).astype(BF16)
                state[hh] = st * jnp.exp(blast) + _dot(vb, kdec, 0, 0)
                o_ref[rows, lanes] = o
                rstd = lax.rsqrt(jnp.mean(o * o, axis=-1, keepdims=True) + RMS_EPS)
                gate = g_ref[rows, lanes]
                y_ref[rows, lanes] = (o * rstd * onv * (gate * _sigmoid(gate))).astype(BF16)
            return carry

        lax.fori_loop(0, nch, chunk, 0, unroll=2)

    hw = HGRN_HEADS * LANES

    def col(off):
        return pl.BlockSpec((tc, hw), lambda h, i: (i, off // HGRN_HEADS + h))

    osp = pl.BlockSpec((tc, hw), lambda h, i: (i, h))
    return pl.pallas_call(
        body, name=name, grid=(nh // HGRN_HEADS, t // tc),
        in_specs=[col(0), col(nh), col(2 * nh), col(3 * nh),
                  pl.BlockSpec((2, hw), lambda h, i: (0, h)),
                  pl.BlockSpec((1, LANES), lambda h, i: (0, 0)),
                  pl.BlockSpec(cum_all.shape, lambda h, i: (0, 0)),
                  pl.BlockSpec(masks.shape, lambda h, i: (0, 0, 0))],
        out_specs=[osp, osp, pl.BlockSpec((HGRN_HEADS, nch, LANES, LANES), lambda h, i: (h, i, 0, 0))],
        out_shape=[jax.ShapeDtypeStruct((t, nh * LANES), BF16), jax.ShapeDtypeStruct((t, nh * LANES), F32),
                   jax.ShapeDtypeStruct((nh, t // CHUNK, LANES, LANES), F32)],
        scratch_shapes=[pltpu.VMEM((HGRN_HEADS, LANES, LANES), F32)],
        compiler_params=_cp())(pc, pc, pc, pc, c_lb, out_norm, cum_all, masks)


def _hgrn_bwd(pc, o_saved, states, dy, c_lb, out_norm, *, name, tc=512):
    t = pc.shape[0]
    nh = pc.shape[1] // 4 // LANES
    tc = min(tc, t)
    nch = tc // CHUNK
    nt = t // tc
    cum_all, masks, suffix = _hgrn_consts()

    def body(q_ref, f_ref, i_ref, g_ref, o_ref, st_ref, dy_ref, lb_ref, on_ref, cum_ref, m_ref, suf_ref,
             dq_ref, df_ref, di_ref, dg_ref, dlb_ref, don_ref, dstate):
        @pl.when(pl.program_id(1) == 0)
        def _():
            dstate[...] = jnp.zeros_like(dstate)
            dlb_ref[...] = jnp.zeros_like(dlb_ref)
            don_ref[...] = jnp.zeros_like(don_ref)

        lbv = _lower_bound(lb_ref)
        onv = on_ref[...]

        def head(hh, c, rows):
            lanes = slice(hh * LANES, (hh + 1) * LANES)
            qr = q_ref[rows, lanes]
            sg, fval, g, kk, sq, qs = _hgrn_gates(qr, f_ref[rows, lanes], lbv[:, lanes])
            vb = i_ref[rows, lanes].astype(BF16)
            o = o_ref[rows, lanes]
            gate = g_ref[rows, lanes]
            sgt = _sigmoid(gate)
            rstd = lax.rsqrt(jnp.mean(o * o, axis=-1, keepdims=True) + RMS_EPS)
            ohat = o * rstd
            dyv = dy_ref[rows, lanes]
            don = dyv * (gate * sgt)
            dg_ref[rows, lanes] = (dyv * ohat * onv * (sgt * (1.0 + gate * (1.0 - sgt)))).astype(BF16)
            don_ref[:, lanes] += jnp.sum(don * ohat, axis=0, keepdims=True)
            dxhat = don * onv
            dob = (rstd * (dxhat - ohat * jnp.mean(dxhat * ohat, axis=-1, keepdims=True))).astype(BF16)
            b = _dot_exact_lhs(cum_ref[...], g)
            blast = b[CHUNK - 1:CHUNK, :]
            eb = jnp.exp(b)
            edec = jnp.exp(blast - b)
            st32 = st_ref[hh, c]
            st = st32.astype(BF16)
            dst = dstate[hh]
            dstb = dst.astype(BF16)
            da = _dot(dob, vb, 1, 1)
            levels = _hgrn_levels(b, qs, kk)
            scores = jnp.zeros((CHUNK, CHUNK), F32)
            dq = eb * _dot(dob, st)
            dk_inter = edec * _dot(vb, dstb)
            dk = dk_inter
            for lvl, (ql, kl, eq, ek) in enumerate(levels):
                mk = m_ref[lvl]
                (qh, qlo), (kh, klo) = _split2(ql), _split2(kl)
                scores = scores + _dot(qh, kh, 1, 1) * mk
                dal = (da * mk).astype(BF16)
                dql = _dot(dal, kh) + _dot(dal, klo)
                dkl = _dot(dal, qh, 0, 0) + _dot(dal, qlo, 0, 0)
                dq = dq + (dql if eq is None else dql * eq)
                dk = dk + (dkl if ek is None else dkl * ek)
            kdec = (kk * edec).astype(BF16)
            dv = _dot(scores.astype(BF16), dob, 0, 0) + _dot(kdec, dstb, 1, 1)
            dstate[hh] = dst * jnp.exp(blast) + _dot(dob, (qs * eb).astype(BF16), 0, 0)
            db = qs * dq - kk * dk
            last = jnp.sum(kk * dk_inter, axis=0, keepdims=True) + jnp.exp(blast) * jnp.sum(dst * st32, axis=0, keepdims=True)
            dgl = _dot_exact_lhs(suf_ref[...], db) + last
            dfv = dgl / fval - dk
            df_ref[rows, lanes] = (dfv * (1.0 - lbv[:, lanes]) * sg * (1.0 - sg)).astype(BF16)
            dlb_ref[:, lanes] += jnp.sum(dfv * (1.0 - sg), axis=0, keepdims=True)
            dq_ref[rows, lanes] = (dq * (sq * (1.0 + qr * (1.0 - sq)))).astype(BF16)
            di_ref[rows, lanes] = dv.astype(BF16)

        def chunk(n, carry):
            c = nch - 1 - n
            rows = pl.ds(pl.multiple_of(c * CHUNK, CHUNK), CHUNK)
            for hh in range(HGRN_HEADS):
                head(hh, c, rows)
            return carry

        lax.fori_loop(0, nch, chunk, 0, unroll=2)

    hw = HGRN_HEADS * LANES

    def col(off):
        return pl.BlockSpec((tc, hw), lambda h, i: (nt - 1 - i, off // HGRN_HEADS + h))

    osp = pl.BlockSpec((tc, hw), lambda h, i: (nt - 1 - i, h))
    vec = pl.BlockSpec((1, hw), lambda h, i: (0, h))
    return pl.pallas_call(
        body, name=name, grid=(nh // HGRN_HEADS, nt),
        in_specs=[col(0), col(nh), col(2 * nh), col(3 * nh), osp,
                  pl.BlockSpec((HGRN_HEADS, nch, LANES, LANES), lambda h, i: (h, nt - 1 - i, 0, 0)),
                  osp,
                  pl.BlockSpec((2, hw), lambda h, i: (0, h)),
                  pl.BlockSpec((1, LANES), lambda h, i: (0, 0)),
                  pl.BlockSpec(cum_all.shape, lambda h, i: (0, 0)),
                  pl.BlockSpec(masks.shape, lambda h, i: (0, 0, 0)),
                  pl.BlockSpec(suffix.shape, lambda h, i: (0, 0))],
        out_specs=[osp, osp, osp, osp, vec, vec],
        out_shape=[jax.ShapeDtypeStruct((t, nh * LANES), BF16)] * 4 + [jax.ShapeDtypeStruct((1, nh * LANES), F32)] * 2,
        scratch_shapes=[pltpu.VMEM((HGRN_HEADS, LANES, LANES), F32)],
        compiler_params=_cp())(pc, pc, pc, pc, o_saved, states, dy, c_lb, out_norm, cum_all, masks, suffix)


HBM_SPEC = pl.BlockSpec(memory_space=pltpu.HBM)


def _gather_scratch():
    return [pltpu.SemaphoreType.DMA((7,)), pltpu.SemaphoreType.DMA((7,)), pltpu.SemaphoreType.DMA]


def _gather_phases(x_ref, out_refs, seg_rows, send_sems, recv_sems, local_sem):
    x, y, c = lax.axis_index("x"), lax.axis_index("y"), lax.axis_index("c")
    me, sibling = (x, y, c), (x, y, 1 - c)
    chips = [(1 - x, y), (x, 1 - y), (1 - x, 1 - y)]
    offs = [sum(seg_rows[:s]) for s in range(len(seg_rows))]
    assert sum(seg_rows) == x_ref.shape[0]

    def index(px, py, pc):
        return 4 * px + 2 * py + pc

    def copies(k, block, to, own):
        return [pltpu.make_async_remote_copy(
            src_ref=x_ref.at[pl.ds(offs[s], n)] if own else out_refs[s].at[index(*block)],
            dst_ref=out_refs[s].at[index(*block)],
            send_sem=send_sems.at[k], recv_sem=recv_sems.at[k], device_id=to, device_id_type=MESH)
            for s, n in enumerate(seg_rows)]

    def all_bytes(k):
        return pltpu.make_async_remote_copy(src_ref=x_ref, dst_ref=x_ref, send_sem=send_sems.at[k],
                                            recv_sem=recv_sems.at[k], device_id=me, device_id_type=MESH)

    mine = [pltpu.make_async_copy(x_ref.at[pl.ds(offs[s], n)], out_refs[s].at[index(*me)], local_sem)
            for s, n in enumerate(seg_rows)]
    first = copies(0, me, sibling, True)
    for j, chip in enumerate(chips):
        first += copies(1 + j, me, (*chip, c), True)

    def start():
        for cp in mine + first:
            cp.start()

    def forward():
        for j, chip in enumerate(chips):
            all_bytes(1 + j).wait_recv()
            for cp in copies(4 + j, (*chip, c), sibling, False):
                cp.start()

    def finish():
        all_bytes(0).wait_recv()
        for j in range(3):
            all_bytes(4 + j).wait_recv()
        for k in range(7):
            all_bytes(k).wait_send()
        pltpu.make_async_copy(x_ref, x_ref, local_sem).wait()

    return start, forward, finish


def _all_gather(xs, seg_rows=None, *, name):
    segs = [xs.shape[0]] if seg_rows is None else list(seg_rows)

    def body(x_ref, *rest):
        start, forward, finish = _gather_phases(x_ref, rest[:len(segs)], segs, *rest[len(segs):])
        start()
        forward()
        finish()

    outs = pl.pallas_call(
        body, name=name, in_specs=[HBM_SPEC], out_specs=[HBM_SPEC] * len(segs),
        out_shape=[jax.ShapeDtypeStruct((8, n, xs.shape[1]), xs.dtype) for n in segs],
        scratch_shapes=_gather_scratch())(xs)
    return outs[0] if seg_rows is None else outs


def _sibling_exchange(s, *, name):
    def body(s_ref, rb_ref, send_sem, recv_sem):
        x, y, c = lax.axis_index("x"), lax.axis_index("y"), lax.axis_index("c")
        cp = pltpu.make_async_remote_copy(
            src_ref=s_ref.at[:, 1 - c], dst_ref=rb_ref, send_sem=send_sem, recv_sem=recv_sem,
            device_id=(x, y, 1 - c), device_id_type=MESH)
        cp.start()
        cp.wait()

    return pl.pallas_call(
        body, name=name, in_specs=[HBM_SPEC], out_specs=HBM_SPEC,
        out_shape=jax.ShapeDtypeStruct(s.shape[:1] + s.shape[2:], s.dtype),
        scratch_shapes=[pltpu.SemaphoreType.DMA, pltpu.SemaphoreType.DMA])(s)


def _row_tile(n, cap=1024):
    return max(b for b in range(16, cap + 1, 16) if n % b == 0)


def _pair_add(s, rb, core, *, name):
    nchip, _, r, c = s.shape
    tb = _row_tile(r)

    def body(core_ref, a_ref, b_ref, o_ref):
        o_ref[...] = (a_ref[...].astype(F32) + b_ref[...].astype(F32)).astype(BF16)

    blk = pl.BlockSpec((None, tb, c), lambda ch, i, cr: (ch, i, 0))
    return pl.pallas_call(
        body, name=name,
        grid_spec=pltpu.PrefetchScalarGridSpec(
            num_scalar_prefetch=1, grid=(nchip, r // tb),
            in_specs=[pl.BlockSpec((None, None, tb, c), lambda ch, i, cr: (ch, cr[0], i, 0)), blk],
            out_specs=blk),
        out_shape=jax.ShapeDtypeStruct((nchip, r, c), BF16), compiler_params=_cp())(core, s, rb)


def _chip_exchange_scratch():
    return [pltpu.SemaphoreType.DMA((3,)), pltpu.SemaphoreType.DMA((3,)), pltpu.SemaphoreType.DMA]


def _chip_exchange_phases(p_ref, out_ref, send_sems, recv_sems, local_sem):
    x, y, c = lax.axis_index("x"), lax.axis_index("y"), lax.axis_index("c")
    mine = 2 * x + y
    own = pltpu.make_async_copy(p_ref.at[mine], out_ref.at[mine], local_sem)
    copies = [pltpu.make_async_remote_copy(
        src_ref=p_ref.at[2 * tx + ty], dst_ref=out_ref.at[mine],
        send_sem=send_sems.at[k], recv_sem=recv_sems.at[k], device_id=(tx, ty, c), device_id_type=MESH)
        for k, (tx, ty) in enumerate([(1 - x, y), (x, 1 - y), (1 - x, 1 - y)])]

    def start():
        own.start()
        for cp in copies:
            cp.start()

    def finish():
        for cp in copies:
            cp.wait()
        own.wait()

    return start, finish


def _adamw_math(w, g, m, v):
    m2 = ADAM_B1 * m + (1.0 - ADAM_B1) * g
    v2 = ADAM_B2 * v + (1.0 - ADAM_B2) * (g * g)
    m_hat = m2 / (1.0 - ADAM_B1 ** ADAM_STEP)
    v_hat = v2 / (1.0 - ADAM_B2 ** ADAM_STEP)
    return -ADAM_LR * (m_hat / (jnp.sqrt(v_hat) + ADAM_EPS) + ADAM_WD * w), m2, v2


def _grad_sum(parts, *, name):
    _, r, c = parts.shape
    tb = _row_tile(r)

    def body(p0, p1, p2, p3, g_out):
        g_out[...] = ((p0[...].astype(F32) + p1[...].astype(F32)) + p2[...].astype(F32)) + p3[...].astype(F32)

    def part(ch):
        return pl.BlockSpec((None, tb, c), lambda i: (ch, i, 0))

    return pl.pallas_call(
        body, name=name, grid=(r // tb,), in_specs=[part(0), part(1), part(2), part(3)],
        out_specs=pl.BlockSpec((tb, c), lambda i: (i, 0)), out_shape=jax.ShapeDtypeStruct((r, c), F32),
        compiler_params=_cp())(parts, parts, parts, parts)


def _adamw_shard(g, g_off, w, m, v, layer, prev, *, name):
    _, r, c = w.shape
    tb = next(b for b in range(min(r, 512), 0, -8) if r % b == 0 and g_off % b == 0)

    def body(g_ref, w_ref, m_ref, v_ref, *rest):
        d_out, m_out, v_out = rest[-3:]
        d, m2, v2 = _adamw_math(w_ref[...], g_ref[...], m_ref[...], v_ref[...])
        d_out[...] = d
        m_out[...] = m2
        v_out[...] = v2

    blk = pl.BlockSpec((None, tb, c), lambda i: (layer, i, 0))
    prev = list(prev) if prev is not None else []
    return pl.pallas_call(
        body, name=name, grid=(r // tb,),
        in_specs=[pl.BlockSpec((tb, c), lambda i: (g_off // tb + i, 0)), blk, blk, blk] + [pl.BlockSpec(memory_space=pl.ANY)] * len(prev),
        out_specs=[blk] * 3, out_shape=[jax.ShapeDtypeStruct(w.shape, F32)] * 3,
        input_output_aliases={4 + k: k for k in range(len(prev))},
        compiler_params=_cp())(g, w, m, v, *prev)


SLOT = 8
SMALL_ROWS = 6 * SLOT
ROW_LB = 4 * SLOT


def _small_update(gath, w, m, v, *, name):
    def body(g_ref, w_ref, m_ref, v_ref, g_out, d_out, m_out, v_out):
        tot = g_ref[0]
        for k in range(1, 8):
            tot = tot + g_ref[k]
        wv = w_ref[...]
        c0, c1 = wv[ROW_LB:ROW_LB + 1, :], wv[ROW_LB + 1:ROW_LB + 2, :]
        mx = jnp.maximum(c0, c1)
        e0, e1 = jnp.exp(c0 - mx), jnp.exp(c1 - mx)
        lb = e1 / (e0 + e1)
        gl = tot[ROW_LB:ROW_LB + 1, :] * lb * (1.0 - lb)
        row = lax.broadcasted_iota(jnp.int32, tot.shape, 0)
        g = jnp.where(row == ROW_LB, -gl, jnp.where(row == ROW_LB + 1, gl, tot))
        d, m2, v2 = _adamw_math(wv, g, m_ref[...], v_ref[...])
        g_out[...] = g
        d_out[...] = d
        m_out[...] = m2
        v_out[...] = v2

    return pl.pallas_call(
        body, name=name, out_shape=[jax.ShapeDtypeStruct(w.shape, F32)] * 4, compiler_params=_cp())(gath, w, m, v)


D_MODEL = 1024


def _ffn_fwd(h, gain, wg, wu, wd, tag):
    xn, gg, uu, act = _norm_gate_up(h, gain, wg, wu, name=f"{tag}_gate_up")
    out = _mm([(act, wd)], residual=h, alpha=MACARON, tn=1024, name=f"{tag}_down")
    return out, (h, xn, gg, uu, act)


def _ffn_input_bwd(dg, du, wg, wu, x, gain, dres, chip_part, *, name, scale, tm=256):
    t, d = x.shape
    f = wg.shape[0]
    tm = min(tm, t)
    nt = t // tm
    fused = chip_part is not None

    def body(dg_ref, du_ref, wg_ref, wu_ref, x_ref, g_ref, dres_ref, *rest):
        if fused:
            part_ref, dx_ref, dxb_ref, dgain_ref, parts_ref = rest[:5]
            start, finish = _chip_exchange_phases(part_ref, parts_ref, *rest[5:])
            pl.when(pl.program_id(0) == 0)(start)
        else:
            dx_ref, dxb_ref, dgain_ref = rest
        dxn_v = _dot(dg_ref[...], wg_ref[...]) + _dot(du_ref[...], wu_ref[...])
        xv = x_ref[...]
        rstd = lax.rsqrt(jnp.mean(xv * xv, axis=-1, keepdims=True) + RMS_EPS)
        xhat = xv * rstd
        dxhat = dxn_v * g_ref[...]
        dx = dres_ref[...] + rstd * (dxhat - xhat * jnp.mean(dxhat * xhat, axis=-1, keepdims=True))
        dx_ref[...] = dx
        dxb_ref[...] = (dx * scale).astype(BF16)

        @pl.when(pl.program_id(0) == 0)
        def _():
            dgain_ref[...] = jnp.zeros_like(dgain_ref)

        dgain_ref[...] += jnp.sum(dxn_v * xhat, axis=0, keepdims=True)
        if fused:
            pl.when(pl.program_id(0) == nt - 1)(finish)

    wide = pl.BlockSpec((tm, f), lambda i: (i, 0))
    wsp = pl.BlockSpec((f, d), lambda i: (0, 0))
    row = pl.BlockSpec((tm, d), lambda i: (i, 0))
    vec = pl.BlockSpec((1, d), lambda i: (0, 0))
    args = [dg, du, wg, wu, x, gain, dres] + ([chip_part] if fused else [])
    return pl.pallas_call(
        body, name=name, grid=(nt,),
        in_specs=[wide, wide, wsp, wsp, row, vec, row] + ([HBM_SPEC] if fused else []),
        out_specs=[row, row, vec] + ([HBM_SPEC] if fused else []),
        out_shape=[jax.ShapeDtypeStruct((t, d), F32), jax.ShapeDtypeStruct((t, d), BF16), jax.ShapeDtypeStruct((1, d), F32)]
        + ([jax.ShapeDtypeStruct(chip_part.shape, chip_part.dtype)] if fused else []),
        scratch_shapes=_chip_exchange_scratch() if fused else [],
        compiler_params=_cp(dimension_semantics=("arbitrary",)))(*args)


def _ffn_bwd(dout, dout_half, saved, gain, wg, wu, wd, tag, next_scale, make_chip_part=None):
    h, xn, gg, uu, act = saved
    dg, du = _swiglu_bwd(dout_half, wd, gg, uu, name=f"{tag}_dact")
    dwd = _mm([(act, dout_half)], ta=True, tm=256, tn=1024, out_dtype=BF16, name=f"{tag}_dwd")
    dwg = _mm([(dg, xn)], ta=True, tm=256, tn=1024, out_dtype=BF16, name=f"{tag}_dwg")
    dwu = _mm([(du, xn)], ta=True, tm=256, tn=1024, out_dtype=BF16, name=f"{tag}_dwu")
    chip_part = make_chip_part(dwg, dwu, dwd) if make_chip_part is not None else None
    dh, dh_b, dgain, *parts = _ffn_input_bwd(dg, du, wg, wu, h, gain, dout, chip_part, scale=next_scale,
                                             name=f"{tag}_input_bwd")
    return dh, dh_b, dwg, dwu, dwd, dgain, (parts[0] if parts else None)


def kernel(x, ffn_pre_norm, ffn_pre_w_gate, ffn_pre_w_up, ffn_pre_w_down, mix_norm, ffn_post_norm, ffn_post_w_gate, ffn_post_w_up, ffn_post_w_down, ab_w_in, ab_conv_w, ab_w_out, c_w_in, c_lower_bounds, c_out_norm, c_w_out, final_norm, loss_target, m_ffn_pre_norm, m_ffn_pre_w_gate, m_ffn_pre_w_up, m_ffn_pre_w_down, m_mix_norm, m_ffn_post_norm, m_ffn_post_w_gate, m_ffn_post_w_up, m_ffn_post_w_down, m_ab_w_in, m_ab_conv_w, m_ab_w_out, m_c_w_in, m_c_lower_bounds, m_c_out_norm, m_c_w_out, m_final_norm, v_ffn_pre_norm, v_ffn_pre_w_gate, v_ffn_pre_w_up, v_ffn_pre_w_down, v_mix_norm, v_ffn_post_norm, v_ffn_post_w_gate, v_ffn_post_w_up, v_ffn_post_w_down, v_ab_w_in, v_ab_conv_w, v_ab_w_out, v_c_w_in, v_c_lower_bounds, v_c_out_norm, v_c_w_out, v_final_norm):
    d = D_MODEL
    h0 = x[0]
    target = loss_target[0]
    core = lax.axis_index("c").astype(jnp.int32).reshape(1)

    big = [("pre_g", ffn_pre_w_gate, m_ffn_pre_w_gate, v_ffn_pre_w_gate),
           ("pre_u", ffn_pre_w_up, m_ffn_pre_w_up, v_ffn_pre_w_up),
           ("pre_d", ffn_pre_w_down, m_ffn_pre_w_down, v_ffn_pre_w_down),
           ("post_g", ffn_post_w_gate, m_ffn_post_w_gate, v_ffn_post_w_gate),
           ("post_u", ffn_post_w_up, m_ffn_post_w_up, v_ffn_post_w_up),
           ("post_d", ffn_post_w_down, m_ffn_post_w_down, v_ffn_post_w_down),
           ("ab_in", ab_w_in, m_ab_w_in, v_ab_w_in),
           ("ab_out", ab_w_out, m_ab_w_out, v_ab_w_out),
           ("c_in", c_w_in, m_c_w_in, v_c_w_in),
           ("c_out", c_w_out, m_c_w_out, v_c_w_out)]
    by_tag = {tag: (w, m, v) for tag, w, m, v in big}

    def layer_rows(tag):
        w = by_tag[tag][0]
        return w.size // d // w.shape[0]

    def layout(items):
        offs, off = {}, 0
        for item in items:
            offs[item] = off
            off += layer_rows(item[0])
        return offs, off

    ffn = [f"{pos}_{kind}" for pos in ("pre", "post") for kind in "gud"]
    early_items = [("pre_g", 0), ("pre_u", 0), ("pre_d", 0), ("ab_in", 0)]
    late_items = ([("pre_g", 1), ("pre_u", 1), ("pre_d", 1)] + [(f"post_{kind}", l) for l in (0, 1) for kind in "gud"]
                  + [("ab_out", 0), ("c_in", 0), ("c_out", 0)])
    grad_items = {"A": ([(tag, 1) for tag in ffn] + [(f"post_{kind}", 0) for kind in "gud"]
                        + [("c_in", 0), ("c_out", 0), ("ab_out", 0)]),
                  "B": [(f"pre_{kind}", 0) for kind in "gud"] + [("ab_in", 0)]}
    grad_offs = {k: layout(items)[0] for k, items in grad_items.items()}
    grad_conv_row = layout(grad_items["B"])[1]

    def conv_rows(a, split):
        flat = a.reshape(-1)
        if split:
            hi = flat.astype(BF16)
            flat = jnp.concatenate([hi, (flat - hi.astype(F32)).astype(BF16)])
        return jnp.zeros((16, d), flat.dtype).at[0, :flat.shape[0]].set(flat)

    nconv = ab_conv_w.size
    col_sharded = {"pre_g", "pre_u", "post_g", "post_u", "ab_in", "c_in"}

    def pack_rows(item):
        tag, layer = item
        a = by_tag[tag][0][layer]
        return (a.T if tag in col_sharded else a).reshape(-1, d).astype(BF16)

    early_pack = jnp.concatenate([pack_rows(item) for item in early_items] + [conv_rows(ab_conv_w, True)], axis=0)
    late_pack = jnp.concatenate([pack_rows(item) for item in late_items], axis=0)
    early_w = _all_gather(early_pack, [layer_rows(tag) for tag, _ in early_items] + [16], name="gather_early_weights")
    full = {item: g.reshape(-1, d) for item, g in zip(early_items, early_w)}
    ffn_w = {("pre", 0): tuple(full[f"pre_{kind}", 0] for kind in "gud")}
    w_ab_in = full["ab_in", 0]
    cg = early_w[-1][:, 0, :2 * nconv].astype(F32)
    conv_w = (cg[:, :nconv] + cg[:, nconv:]).reshape(8, 3, -1).transpose(1, 0, 2).reshape(3, -1)
    half = w_ab_in.shape[0] // 2
    w_a_in, w_b_in = w_ab_in[:half], w_ab_in[half:]
    aw = half // 3

    h1, s_pre0 = _ffn_fwd(h0, ffn_pre_norm[0:1], *ffn_w["pre", 0], "l0pre")
    hn0 = _rmsnorm_fwd(h1, mix_norm[0:1], name="l0_mix_norm")
    pa = _mm([(hn0, w_a_in)], tb=True, tn=1536, name="ab_proj_a")
    pb = _mm([(hn0, w_b_in)], tb=True, tn=1536, out_dtype=BF16, name="ab_proj_b")
    ya = _conv_fwd(pa, conv_w, name="conv_fwd")
    yb, ltot, *late_w = _attn_fwd(pb, late_pack, [layer_rows(tag) for tag, _ in late_items],
                                  name="attn_fwd_gather_late_weights")
    full.update({item: g.reshape(-1, d) for item, g in zip(late_items, late_w)})
    for pos, layer in (("post", 0), ("pre", 1), ("post", 1)):
        ffn_w[pos, layer] = tuple(full[f"{pos}_{kind}", layer] for kind in "gud")
    w_ab_out, w_c_in, w_c_out = full["ab_out", 0], full["c_in", 0], full["c_out", 0]
    h2 = _mm([(ya, w_ab_out[:aw]), (yb, w_ab_out[aw:])], residual=h1, tn=1024, name="ab_out")
    h3, s_post0 = _ffn_fwd(h2, ffn_post_norm[0:1], *ffn_w["post", 0], "l0post")
    h4, s_pre1 = _ffn_fwd(h3, ffn_pre_norm[1:2], *ffn_w["pre", 1], "l1pre")
    hn1 = _rmsnorm_fwd(h4, mix_norm[1:2], name="l1_mix_norm")
    pc = _mm([(hn1, w_c_in)], tb=True, tm=256, tn=4096, name="c_proj")
    yc, o_saved, states = _hgrn_fwd(pc, c_lower_bounds, c_out_norm, name="hgrn_fwd")
    h5 = _mm([(yc, w_c_out)], residual=h4, tn=1024, name="c_out")
    h6, s_post1 = _ffn_fwd(h5, ffn_post_norm[1:2], *ffn_w["post", 1], "l1post")
    dh6, dh6_b, d_final, loss_vec = _loss_head(h6, final_norm.reshape(1, d), target, name="loss_head")

    gw = {}
    dh5, dh5_b, gw["post_g", 1], gw["post_u", 1], gw["post_d", 1], d_post1, _ = _ffn_bwd(
        dh6, dh6_b, s_post1, ffn_post_norm[1:2], *ffn_w["post", 1], "l1post", 1.0)
    dyc = _mm([(dh5_b, w_c_out)], tb=True, tn=1024, name="c_out_dy")
    g_c_out = _mm([(yc, dh5_b)], ta=True, tm=256, tn=1024, out_dtype=BF16, name="c_out_dw")
    dcq, dcf, dci, dcg, dlb, d_onorm = _hgrn_bwd(pc, o_saved, states, dyc, c_lower_bounds, c_out_norm, name="hgrn_bwd")
    dparts = [dcq, dcf, dci, dcg]
    g_c_in = jnp.concatenate(
        [_mm([(dp, hn1)], ta=True, tm=256, tn=1024, out_dtype=BF16, name=f"c_in_dw{i}") for i, dp in enumerate(dparts)],
        axis=0)
    cw = w_c_in.shape[0] // 4
    dhn1 = _mm([(dp, w_c_in[i * cw:(i + 1) * cw]) for i, dp in enumerate(dparts)], tm=512, tn=1024, name="c_in_dx")
    dh4, dh4_b, d_mix1 = _rmsnorm_bwd(h4, mix_norm[1:2], dhn1, dh5, scale=MACARON, name="l1_mix_norm_bwd")
    dh3, dh3_b, gw["pre_g", 1], gw["pre_u", 1], gw["pre_d", 1], d_pre1, _ = _ffn_bwd(
        dh4, dh4_b, s_pre1, ffn_pre_norm[1:2], *ffn_w["pre", 1], "l1pre", MACARON)
    dh2, dh2_b, gw["post_g", 0], gw["post_u", 0], gw["post_d", 0], d_post0, _ = _ffn_bwd(
        dh3, dh3_b, s_post0, ffn_post_norm[0:1], *ffn_w["post", 0], "l0post", 1.0)
    dyab = _mm([(dh2_b, w_ab_out)], tb=True, tn=1024, name="ab_out_dy")
    g_ab_out = jnp.concatenate([_mm([(ya, dh2_b)], ta=True, tm=256, tn=1024, out_dtype=BF16, name="ab_out_dw_a"),
                                _mm([(yb, dh2_b)], ta=True, tm=256, tn=1024, out_dtype=BF16, name="ab_out_dw_b")], axis=0)
    dab, dac, dax, g_conv = _conv_bwd(pa, dyab, conv_w, name="conv_bwd")

    def chip_partials(key, grads, extra=()):
        gpack = jnp.concatenate([grads[item].reshape(8, -1, d) for item in grad_items[key]] + list(extra), axis=1)
        send = gpack.reshape(4, 2, gpack.shape[1], d)
        from_sibling = _sibling_exchange(send, name=f"grad{key}_sibling_exchange")
        return _pair_add(send, from_sibling, core, name=f"grad{key}_pair_add")

    gw["c_in", 0], gw["c_out", 0], gw["ab_out", 0] = g_c_in, g_c_out, g_ab_out
    chip_part_a = chip_partials("A", gw)
    dq, dk, dv, parts_a = _attn_bwd(pb, dyab, ltot, chip_part_a, name="attn_bwd_exchange_grads_a")
    dparts = [dab, dac, dax, dq, dk, dv]
    g_ab_in = jnp.concatenate(
        [_mm([(dp, hn0)], ta=True, tm=256, tn=1024, out_dtype=BF16, name=f"ab_in_dw{i}") for i, dp in enumerate(dparts)],
        axis=0)
    dhn0 = _mm([(dp, w_ab_in[i * aw:(i + 1) * aw]) for i, dp in enumerate(dparts)], tm=512, tn=1024, name="ab_in_dx")
    dh1, dh1_b, d_mix0 = _rmsnorm_bwd(h1, mix_norm[0:1], dhn0, dh2, scale=MACARON, name="l0_mix_norm_bwd")
    gw["ab_in", 0] = g_ab_in
    gconv_own = g_conv.reshape(3, 8, -1).transpose(1, 0, 2).reshape(8, -1)
    conv_piece = jnp.zeros((8, 16, d), F32).at[:, 0, :nconv].set(gconv_own).astype(BF16)

    def chip_part_b(dwg, dwu, dwd):
        gw["pre_g", 0], gw["pre_u", 0], gw["pre_d", 0] = dwg, dwu, dwd
        return chip_partials("B", gw, [conv_piece])

    dh0, _, _, _, _, d_pre0, parts_b = _ffn_bwd(
        dh1, dh1_b, s_pre0, ffn_pre_norm[0:1], *ffn_w["pre", 0], "l0pre", 1.0, chip_part_b)

    g_sum = {"A": _grad_sum(parts_a, name="gradA_sum"), "B": _grad_sum(parts_b, name="gradB_sum")}

    upd = {}
    for tag, w, m, v in big:
        nl = layer_rows(tag)
        view = (lambda a: jnp.swapaxes(a, 1, 2)) if tag in col_sharded else (lambda a: a)
        where = {layer: (key, grad_offs[key][tag, layer])
                 for key in ("A", "B") for t2, layer in grad_items[key] if t2 == tag}
        res = None
        for layer in sorted(where):
            key, off = where[layer]
            res = _adamw_shard(g_sum[key], off, view(w), view(m), view(v), layer, res, name=f"adamw_{tag}{layer}")
        g_nat = jnp.stack([g_sum[where[layer][0]][where[layer][1]:where[layer][1] + nl] for layer in sorted(where)])
        upd[tag] = [view(a) for a in [g_nat] + list(res)]
    res = _adamw_shard(g_sum["B"], grad_conv_row, *(conv_rows(a, False)[None] for a in (ab_conv_w, m_ab_conv_w, v_ab_conv_w)),
                       0, None, name="adamw_conv")
    g_conv_rows = g_sum["B"][grad_conv_row:grad_conv_row + 16]
    upd["conv"] = [r[0, :nconv].reshape(ab_conv_w.shape) for r in [g_conv_rows] + [r[0] for r in res]]

    def small_pack(pre, mix, post, final, lbs, onorm):
        def slot(parts):
            out, r = jnp.zeros((SLOT, d), F32), 0
            for a in (parts if isinstance(parts, tuple) else (parts,)):
                out = out.at[r:r + a.shape[0], :a.shape[1]].set(a)
                r += a.shape[0]
            return out

        return jnp.concatenate([slot(pre), slot(mix), slot(post), slot(final.reshape(1, d)), slot(lbs), slot(onorm)], axis=0)

    d_on = d_onorm.reshape(-1, c_out_norm.shape[1]).sum(axis=0, keepdims=True)
    gsmall = small_pack((d_pre0, d_pre1), (d_mix0, d_mix1), (d_post0, d_post1), d_final, dlb, d_on)
    gsmall_all = _all_gather(gsmall, name="gather_small_grads")
    sres = _small_update(
        gsmall_all,
        small_pack(ffn_pre_norm, mix_norm, ffn_post_norm, final_norm, c_lower_bounds, c_out_norm),
        small_pack(m_ffn_pre_norm, m_mix_norm, m_ffn_post_norm, m_final_norm, m_c_lower_bounds, m_c_out_norm),
        small_pack(v_ffn_pre_norm, v_mix_norm, v_ffn_post_norm, v_final_norm, v_c_lower_bounds, v_c_out_norm),
        name="small_update")

    def small_out(r):
        return {"pre_norm": r[0:2], "mix_norm": r[SLOT:SLOT + 2], "post_norm": r[2 * SLOT:2 * SLOT + 2],
                "final": r[3 * SLOT], "lb": r[ROW_LB:ROW_LB + 2], "onorm": r[5 * SLOT:5 * SLOT + 1, :c_out_norm.shape[1]]}

    small = [small_out(r) for r in sres]
    outs = []
    for k in range(4):
        s = small[k]
        outs += [s["pre_norm"], upd["pre_g"][k], upd["pre_u"][k], upd["pre_d"][k], s["mix_norm"], s["post_norm"],
                 upd["post_g"][k], upd["post_u"][k], upd["post_d"][k], upd["ab_in"][k], upd["conv"][k],
                 upd["ab_out"][k], upd["c_in"][k], s["lb"], s["onorm"], upd["c_out"][k], s["final"]]
    loss = lax.psum(loss_vec[0, 0], ("x", "y", "c"))
    return (loss, dh0[None], *outs)
```

```python
import functools
import math

import jax
import jax.numpy as jnp
from jax import lax
from jax.experimental import pallas as pl
from jax.experimental.pallas import tpu as pltpu

F32 = jnp.float32
BF16 = jnp.bfloat16
MESH = pl.DeviceIdType.MESH

RMS_EPS = 1e-6
MACARON = 0.5
LANES = 128
CHUNK = 64
N_LEVELS = 6
HGRN_HEADS = 2
SB_KEYS = 256
ADAM_LR, ADAM_B1, ADAM_B2, ADAM_EPS, ADAM_WD, ADAM_STEP = 0.001, 0.9, 0.999, 1e-08, 0.01, 10
VMEM_LIMIT = 48 * 1024 * 1024


def _cp(**kw):
    return pltpu.CompilerParams(vmem_limit_bytes=VMEM_LIMIT, **kw)


def _sigmoid(x):
    return 1.0 / (1.0 + jnp.exp(-x))


def _bf(x):
    return x if x.dtype == BF16 else x.astype(BF16)


def _split3(x):
    hi = x.astype(BF16)
    r1 = x - hi.astype(F32)
    mid = r1.astype(BF16)
    lo = (r1 - mid.astype(F32)).astype(BF16)
    return hi, mid, lo


def _dot(a, b, ca=1, cb=0):
    return lax.dot_general(a, b, (((ca,), (cb,)), ((), ())), preferred_element_type=F32)


def _dot_exact_lhs(m, x):
    hi, mid, lo = _split3(x)
    return _dot(m, hi) + _dot(m, mid) + _dot(m, lo)


def _dot_exact_rhs(x, m):
    hi, mid, lo = _split3(x)
    return _dot(hi, m) + _dot(mid, m) + _dot(lo, m)


def _mm(terms, *, name, ta=False, tb=False, out_dtype=F32, residual=None, alpha=1.0, tm=512, tn=512):
    nt = len(terms)
    a0, b0 = terms[0]
    m = a0.shape[1] if ta else a0.shape[0]
    n = b0.shape[0] if tb else b0.shape[1]
    tm, tn = min(tm, m), min(tn, n)
    assert m % tm == 0 and n % tn == 0, (name, m, n, tm, tn)
    has_res = residual is not None

    def body(*refs):
        o_ref = refs[-1]
        acc = None
        for i in range(nt):
            a = _bf(refs[2 * i][...])
            b = _bf(refs[2 * i + 1][...])
            p = _dot(a, b, 0 if ta else 1, 1 if tb else 0)
            acc = p if acc is None else acc + p
        if alpha != 1.0:
            acc = acc * alpha
        if has_res:
            acc = acc + refs[2 * nt][...]
        o_ref[...] = acc.astype(out_dtype)

    in_specs, args = [], []
    for a, b in terms:
        k = a.shape[0] if ta else a.shape[1]
        assert (b.shape[1] if tb else b.shape[0]) == k, (name, a.shape, b.shape)
        in_specs.append(pl.BlockSpec((k, tm), lambda i, j: (0, i)) if ta else pl.BlockSpec((tm, k), lambda i, j: (i, 0)))
        in_specs.append(pl.BlockSpec((tn, k), lambda i, j: (j, 0)) if tb else pl.BlockSpec((k, tn), lambda i, j: (0, j)))
        args += [a, b]
    if has_res:
        in_specs.append(pl.BlockSpec((tm, tn), lambda i, j: (i, j)))
        args.append(residual)
    return pl.pallas_call(
        body, name=name, grid=(m // tm, n // tn), in_specs=in_specs,
        out_specs=pl.BlockSpec((tm, tn), lambda i, j: (i, j)),
        out_shape=jax.ShapeDtypeStruct((m, n), out_dtype), compiler_params=_cp())(*args)


def _rmsnorm_fwd(x, gain, *, name, tm=512):
    t, d = x.shape
    tm = min(tm, t)

    def body(x_ref, g_ref, o_ref):
        xv = x_ref[...]
        rstd = lax.rsqrt(jnp.mean(xv * xv, axis=-1, keepdims=True) + RMS_EPS)
        o_ref[...] = (xv * rstd * g_ref[...]).astype(BF16)

    return pl.pallas_call(
        body, name=name, grid=(t // tm,),
        in_specs=[pl.BlockSpec((tm, d), lambda i: (i, 0)), pl.BlockSpec((1, d), lambda i: (0, 0))],
        out_specs=pl.BlockSpec((tm, d), lambda i: (i, 0)),
        out_shape=jax.ShapeDtypeStruct((t, d), BF16), compiler_params=_cp())(x, gain)


def _rmsnorm_bwd(x, gain, dxn, dres, *, name, scale, tm=512):
    t, d = x.shape
    tm = min(tm, t)

    def body(x_ref, g_ref, dxn_ref, dres_ref, dx_ref, dxb_ref, dg_ref):
        xv = x_ref[...]
        rstd = lax.rsqrt(jnp.mean(xv * xv, axis=-1, keepdims=True) + RMS_EPS)
        xhat = xv * rstd
        dxn_v = dxn_ref[...]
        dxhat = dxn_v * g_ref[...]
        dx = dres_ref[...] + rstd * (dxhat - xhat * jnp.mean(dxhat * xhat, axis=-1, keepdims=True))
        dx_ref[...] = dx
        dxb_ref[...] = (dx * scale).astype(BF16)

        @pl.when(pl.program_id(0) == 0)
        def _():
            dg_ref[...] = jnp.zeros_like(dg_ref)

        dg_ref[...] += jnp.sum(dxn_v * xhat, axis=0, keepdims=True)

    row = pl.BlockSpec((tm, d), lambda i: (i, 0))
    vec = pl.BlockSpec((1, d), lambda i: (0, 0))
    return pl.pallas_call(
        body, name=name, grid=(t // tm,), in_specs=[row, vec, row, row], out_specs=[row, row, vec],
        out_shape=[jax.ShapeDtypeStruct((t, d), F32), jax.ShapeDtypeStruct((t, d), BF16), jax.ShapeDtypeStruct((1, d), F32)],
        compiler_params=_cp())(x, gain, dxn, dres)


def _loss_head(h, gain, target, *, name, tm=512):
    t, d = h.shape
    tm = min(tm, t)

    def body(h_ref, g_ref, t_ref, dh_ref, dhb_ref, dg_ref, loss_ref):
        hv = h_ref[...]
        rstd = lax.rsqrt(jnp.mean(hv * hv, axis=-1, keepdims=True) + RMS_EPS)
        xhat = hv * rstd
        err = xhat * g_ref[...] - t_ref[...]
        dy = err * (1.0 / d)
        dxhat = dy * g_ref[...]
        dh = rstd * (dxhat - xhat * jnp.mean(dxhat * xhat, axis=-1, keepdims=True))
        dh_ref[...] = dh
        dhb_ref[...] = (dh * MACARON).astype(BF16)

        @pl.when(pl.program_id(0) == 0)
        def _():
            dg_ref[...] = jnp.zeros_like(dg_ref)
            loss_ref[...] = jnp.zeros_like(loss_ref)

        dg_ref[...] += jnp.sum(dy * xhat, axis=0, keepdims=True)
        part = jnp.sum(jnp.sum(err * err, axis=-1, keepdims=True), axis=0, keepdims=True) * (0.5 / d)
        loss_ref[...] += jnp.broadcast_to(part, loss_ref.shape)

    row = pl.BlockSpec((tm, d), lambda i: (i, 0))
    vec = pl.BlockSpec((1, d), lambda i: (0, 0))
    return pl.pallas_call(
        body, name=name, grid=(t // tm,), in_specs=[row, vec, row],
        out_specs=[row, row, vec, pl.BlockSpec((1, LANES), lambda i: (0, 0))],
        out_shape=[jax.ShapeDtypeStruct((t, d), F32), jax.ShapeDtypeStruct((t, d), BF16), jax.ShapeDtypeStruct((1, d), F32),
                   jax.ShapeDtypeStruct((1, LANES), F32)],
        compiler_params=_cp())(h, gain, target)


def _norm_gate_up(x, gain, wg, wu, *, name, tm=512, tf=1408, pack=None, seg_rows=()):
    t, d = x.shape
    f = wg.shape[0]
    tm, tf = min(tm, t), min(tf, f)
    assert f % tf == 0
    ni, nj = t // tm, f // tf
    nseg = len(seg_rows)

    def body(x_ref, g_ref, wg_ref, wu_ref, *rest):
        if pack is not None:
            pack_ref, xn_ref, gg_ref, uu_ref, act_ref = rest[:5]
            start, forward, finish = _gather_phases(pack_ref, rest[5:5 + nseg], seg_rows, *rest[5 + nseg:])
            step = pl.program_id(0) * nj + pl.program_id(1)
            pl.when(step == 0)(start)
            pl.when(step == (3 * ni * nj) // 4)(forward)
        else:
            xn_ref, gg_ref, uu_ref, act_ref = rest

        @pl.when(pl.program_id(1) == 0)
        def _():
            xv = x_ref[...]
            rstd = lax.rsqrt(jnp.mean(xv * xv, axis=-1, keepdims=True) + RMS_EPS)
            xn_ref[...] = (xv * rstd * g_ref[...]).astype(BF16)

        xn = xn_ref[...]
        gv = _dot(xn, wg_ref[...], 1, 1)
        uv = _dot(xn, wu_ref[...], 1, 1)
        gg_ref[...] = gv.astype(BF16)
        uu_ref[...] = uv.astype(BF16)
        act_ref[...] = (gv * _sigmoid(gv) * uv).astype(BF16)
        if pack is not None:
            pl.when(step == ni * nj - 1)(finish)

    row = pl.BlockSpec((tm, d), lambda i, j: (i, 0))
    wsp = pl.BlockSpec((tf, d), lambda i, j: (j, 0))
    osp = pl.BlockSpec((tm, tf), lambda i, j: (i, j))
    fused = pack is not None
    return pl.pallas_call(
        body, name=name, grid=(ni, nj),
        in_specs=[row, pl.BlockSpec((1, d), lambda i, j: (0, 0)), wsp, wsp] + ([HBM_SPEC] if fused else []),
        out_specs=[row, osp, osp, osp] + [HBM_SPEC] * nseg,
        out_shape=[jax.ShapeDtypeStruct((t, d), BF16)] + [jax.ShapeDtypeStruct((t, f), BF16)] * 3
        + [jax.ShapeDtypeStruct((8, n, d), BF16) for n in seg_rows],
        scratch_shapes=_gather_scratch() if fused else [],
        compiler_params=_cp(dimension_semantics=("arbitrary", "arbitrary")))(x, gain, wg, wu, *([pack] if fused else []))


def _swiglu_bwd(dout, wd, gg, uu, *, name, tm=512, tf=1408):
    t, d = dout.shape
    f = wd.shape[0]
    tm, tf = min(tm, t), min(tf, f)

    def body(do_ref, wd_ref, g_ref, u_ref, dg_ref, du_ref):
        dact = _dot(do_ref[...], wd_ref[...], 1, 1)
        gv = g_ref[...].astype(F32)
        uv = u_ref[...].astype(F32)
        sg = _sigmoid(gv)
        dg_ref[...] = (dact * uv * (sg * (1.0 + gv * (1.0 - sg)))).astype(BF16)
        du_ref[...] = (dact * (gv * sg)).astype(BF16)

    osp = pl.BlockSpec((tm, tf), lambda i, j: (i, j))
    return pl.pallas_call(
        body, name=name, grid=(t // tm, f // tf),
        in_specs=[pl.BlockSpec((tm, d), lambda i, j: (i, 0)), pl.BlockSpec((tf, d), lambda i, j: (j, 0)), osp, osp],
        out_specs=[osp, osp], out_shape=[jax.ShapeDtypeStruct((t, f), BF16)] * 2,
        compiler_params=_cp())(dout, wd, gg, uu)


def _shift_down(x, n):
    rows = lax.broadcasted_iota(jnp.int32, x.shape, 0)
    return jnp.where(rows >= n, pltpu.roll(x, n, 0), 0.0)


def _shift_up(x, n):
    t = x.shape[0]
    rows = lax.broadcasted_iota(jnp.int32, x.shape, 0)
    return jnp.where(rows < t - n, pltpu.roll(x, t - n, 0), 0.0)


def _conv_fwd(pa, conv_w, *, name):
    t = pa.shape[0]
    nb = pa.shape[1] // 3 // LANES

    def body(b_ref, c_ref, x_ref, w_ref, y_ref):
        u = c_ref[...] * x_ref[...]
        w = w_ref[...]
        conv = w[2:3, :] * u + w[1:2, :] * _shift_down(u, 1) + w[0:1, :] * _shift_down(u, 2)
        y_ref[...] = (b_ref[...] * conv).astype(BF16)

    def col(off):
        return pl.BlockSpec((t, LANES), lambda j: (0, off + j))

    return pl.pallas_call(
        body, name=name, grid=(nb,),
        in_specs=[col(0), col(nb), col(2 * nb), pl.BlockSpec((3, LANES), lambda j: (0, j))],
        out_specs=pl.BlockSpec((t, LANES), lambda j: (0, j)),
        out_shape=jax.ShapeDtypeStruct((t, nb * LANES), BF16), compiler_params=_cp())(pa, pa, pa, conv_w)


def _conv_bwd(pa, dy, conv_w, *, name):
    t = pa.shape[0]
    nb = pa.shape[1] // 3 // LANES

    def body(b_ref, c_ref, x_ref, dy_ref, w_ref, db_ref, dc_ref, dx_ref, dw_ref):
        cv, xv = c_ref[...], x_ref[...]
        u = cv * xv
        u1, u2 = _shift_down(u, 1), _shift_down(u, 2)
        w = w_ref[...]
        conv = w[2:3, :] * u + w[1:2, :] * u1 + w[0:1, :] * u2
        dyv = dy_ref[...]
        db_ref[...] = (dyv * conv).astype(BF16)
        dconv = dyv * b_ref[...]
        du = w[2:3, :] * dconv + w[1:2, :] * _shift_up(dconv, 1) + w[0:1, :] * _shift_up(dconv, 2)
        dc_ref[...] = (du * xv).astype(BF16)
        dx_ref[...] = (du * cv).astype(BF16)
        dw_ref[0:1, :] = jnp.sum(dconv * u2, axis=0, keepdims=True)
        dw_ref[1:2, :] = jnp.sum(dconv * u1, axis=0, keepdims=True)
        dw_ref[2:3, :] = jnp.sum(dconv * u, axis=0, keepdims=True)

    def col(off):
        return pl.BlockSpec((t, LANES), lambda j: (0, off + j))

    osp = pl.BlockSpec((t, LANES), lambda j: (0, j))
    wsp = pl.BlockSpec((3, LANES), lambda j: (0, j))
    return pl.pallas_call(
        body, name=name, grid=(nb,), in_specs=[col(0), col(nb), col(2 * nb), col(0), wsp],
        out_specs=[osp, osp, osp, wsp],
        out_shape=[jax.ShapeDtypeStruct((t, nb * LANES), BF16)] * 3 + [jax.ShapeDtypeStruct((3, nb * LANES), F32)],
        compiler_params=_cp())(pa, pa, pa, dy, conv_w)


def _sb_consts():
    j = lax.broadcasted_iota(jnp.int32, (SB_KEYS, SB_KEYS), 0)
    s = lax.broadcasted_iota(jnp.int32, (SB_KEYS, SB_KEYS), 1)
    after = (j > s).astype(BF16)
    upto = (j <= s).astype(BF16)
    before = (j < s).astype(BF16)
    return after, jnp.stack([upto, before])


def _log_sigmoid(z):
    return jnp.minimum(z, 0.0) - jnp.log(1.0 + jnp.exp(-jnp.abs(z)))


def _attn_fwd(pb, late_pack, seg_rows, *, name, tq=256):
    t = pb.shape[0]
    npair = pb.shape[1] // 3 // LANES
    tq = min(tq, t)
    nq = t // tq
    cmat, _ = _sb_consts()
    scale = 1.0 / math.sqrt(LANES // 2)

    nseg = len(seg_rows)

    def body(q_ref, k_ref, v_ref, c_ref, late_ref, y_ref, lt_ref, *rest):
        i = pl.program_id(1)
        pair = pl.program_id(0)
        scratch = rest[nseg:nseg + 4]
        start, forward, finish = _gather_phases(late_ref, rest[:nseg], seg_rows, *rest[nseg + 4:])
        pl.when((pair == 0) & (i == 0))(start)
        pl.when((pair == npair - 1) & (i == nq // 2))(forward)
        lane = lax.broadcasted_iota(jnp.int32, (tq, LANES), 1)
        rowpos = i * tq + lax.broadcasted_iota(jnp.int32, (tq, SB_KEYS), 0)
        colid = lax.broadcasted_iota(jnp.int32, (tq, SB_KEYS), 1)
        q2 = q_ref[...] * jnp.asarray(scale, BF16)
        cm = c_ref[...]
        hi_lanes = lane >= LANES // 2
        qhs = [jnp.where(hi_lanes == (hh == 1), q2, jnp.zeros_like(q2)) for hh in range(2)]
        per_q = tq // SB_KEYS

        def blk(jb):
            return pl.ds(pl.multiple_of(jb * SB_KEYS, SB_KEYS), SB_KEYS)

        zbuf, wbuf, accbuf, runbuf = scratch

        def scores(jb):
            kb = k_ref[blk(jb), :]
            for hh in range(2):
                zbuf[hh] = _dot(qhs[hh], kb, 1, 1)

        def values(jb):
            vb = v_ref[blk(jb), :]
            for hh in range(2):
                accbuf[hh] += _dot(wbuf[hh], vb)

        def trip(jb, masked, first=False):
            mask = (jb * SB_KEYS + colid) < rowpos if masked else None
            if not first:
                values(jb + 1)
            pre, css = [], []
            for hh in range(2):
                z = zbuf[hh]
                lb = _log_sigmoid(z)
                lk = lb - z
                if masked:
                    lk = jnp.where(mask, lk, 0.0)
                lk_hi, lk_lo = _split2(lk)
                css.append(_dot(lk_hi, cm) + _dot(lk_lo, cm))
                run = runbuf[hh]
                pre.append(lb + run)
                runbuf[hh] = run + jnp.sum(lk, axis=1, keepdims=True)
            scores(jnp.maximum(jb - 1, 0))
            for hh in range(2):
                w = jnp.exp(pre[hh] + css[hh])
                if masked:
                    w = jnp.where(mask, w, 0.0)
                wbuf[hh] = w.astype(BF16)

        nfull = i * per_q
        accbuf[...] = jnp.zeros_like(accbuf)
        runbuf[...] = jnp.zeros_like(runbuf)
        scores(nfull + per_q - 1)
        for dblk in reversed(range(per_q)):
            trip(nfull + dblk, True, first=dblk == per_q - 1)

        def full_block(n, carry):
            trip(nfull - 1 - n, False)
            return carry

        lax.fori_loop(0, nfull, full_block, 0)
        values(0)
        y_ref[...] = jnp.where(hi_lanes, accbuf[1], accbuf[0]).astype(BF16)
        lt_ref[...] = jnp.where(hi_lanes, runbuf[1], runbuf[0])
        pl.when((pair == npair - 1) & (i == nq - 1))(finish)

    return pl.pallas_call(
        body, name=name, grid=(npair, nq),
        in_specs=[pl.BlockSpec((tq, LANES), lambda p, i: (i, p)),
                  pl.BlockSpec((t, LANES), lambda p, i: (0, npair + p)),
                  pl.BlockSpec((t, LANES), lambda p, i: (0, 2 * npair + p)),
                  pl.BlockSpec((SB_KEYS, SB_KEYS), lambda p, i: (0, 0)),
                  HBM_SPEC],
        out_specs=[pl.BlockSpec((tq, LANES), lambda p, i: (i, p))] * 2 + [HBM_SPEC] * nseg,
        out_shape=[jax.ShapeDtypeStruct((t, npair * LANES), BF16), jax.ShapeDtypeStruct((t, npair * LANES), F32),
                   ] + [jax.ShapeDtypeStruct((8, n, late_pack.shape[1]), late_pack.dtype) for n in seg_rows],
        scratch_shapes=[pltpu.VMEM((2, tq, SB_KEYS), F32), pltpu.VMEM((2, tq, SB_KEYS), BF16),
                        pltpu.VMEM((2, tq, LANES), F32), pltpu.VMEM((2, tq, 1), F32)] + _gather_scratch(),
        compiler_params=_cp(dimension_semantics=("arbitrary", "arbitrary")))(pb, pb, pb, cmat, late_pack)


def _attn_bwd(pb, dy, ltot, chip_part, *, name, tq=256):
    t = pb.shape[0]
    npair = pb.shape[1] // 3 // LANES
    tq = min(tq, t)
    nq = t // tq
    _, cmats = _sb_consts()
    scale = 1.0 / math.sqrt(LANES // 2)

    def body(q_ref, k_ref, v_ref, dy_ref, lt_ref, c_ref, part_ref, dq_ref, dk_ref, dv_ref, parts_ref, dk_acc, dv_acc, *rest):
        i = pl.program_id(1)
        pair = pl.program_id(0)
        scratch = rest[:6]
        start, finish = _chip_exchange_phases(part_ref, parts_ref, *rest[6:])
        pl.when((pair == 0) & (i == 0))(start)

        @pl.when(i == 0)
        def _():
            dk_acc[...] = jnp.zeros_like(dk_acc)
            dv_acc[...] = jnp.zeros_like(dv_acc)

        lane = lax.broadcasted_iota(jnp.int32, (tq, LANES), 1)
        rowpos = i * tq + lax.broadcasted_iota(jnp.int32, (tq, SB_KEYS), 0)
        colid = lax.broadcasted_iota(jnp.int32, (tq, SB_KEYS), 1)
        q2 = q_ref[...] * jnp.asarray(scale, BF16)
        do2 = dy_ref[...].astype(BF16)
        ltv = lt_ref[...]
        c_upto, c_before = c_ref[0], c_ref[1]
        hi_lanes = lane >= LANES // 2
        sels = [hi_lanes == (hh == 1) for hh in range(2)]
        qhs = [jnp.where(s, q2, jnp.zeros_like(q2)) for s in sels]
        dohs = [jnp.where(s, do2, jnp.zeros_like(do2)) for s in sels]
        lts = [ltv[:, 0:1], ltv[:, LANES // 2:LANES // 2 + 1]]
        per_q = tq // SB_KEYS

        def blk(jb):
            return pl.ds(pl.multiple_of(jb * SB_KEYS, SB_KEYS), SB_KEYS)

        zbuf, dabuf, dzbuf, abuf, dqbuf, sumbuf = scratch

        def scores(jb):
            kb, vb = k_ref[blk(jb), :], v_ref[blk(jb), :]
            for hh in range(2):
                zbuf[hh] = _dot(qhs[hh], kb, 1, 1)
                dabuf[hh] = _dot(dohs[hh], vb, 1, 1)

        def products(jb):
            kb = k_ref[blk(jb), :]
            dk_acc[blk(jb), :] += _dot(dzbuf[0], qhs[0], 0, 0) + _dot(dzbuf[1], qhs[1], 0, 0)
            dv_acc[blk(jb), :] += _dot(abuf[0], dohs[0], 0, 0) + _dot(abuf[1], dohs[1], 0, 0)
            for hh in range(2):
                dqbuf[hh] += _dot(dzbuf[hh], kb)

        def trip(jb, masked):
            mask = (jb * SB_KEYS + colid) < rowpos if masked else None
            products(jnp.maximum(jb - 1, 0))
            lbs, css, es, ces = [], [], [], []
            for hh in range(2):
                z = zbuf[hh]
                lb = _log_sigmoid(z)
                lk = lb - z
                if masked:
                    lk = jnp.where(mask, lk, 0.0)
                lk_hi, lk_lo = _split2(lk)
                css.append(_dot(lk_hi, c_upto) + _dot(lk_lo, c_upto))
                csum = sumbuf[2 * hh]
                lbs.append((lb, lb + (lts[hh] - csum)))
                sumbuf[2 * hh] = csum + jnp.sum(lk, axis=1, keepdims=True)
            for hh in range(2):
                a = jnp.exp(lbs[hh][1] - css[hh])
                if masked:
                    a = jnp.where(mask, a, 0.0)
                e = a * dabuf[hh]
                e_hi, e_lo = _split2(e)
                ces.append(_dot(e_hi, c_before) + _dot(e_lo, c_before))
                abuf[hh] = a.astype(BF16)
                es.append(e)
            scores(jnp.minimum(jb + 1, last))
            for hh in range(2):
                prun = sumbuf[2 * hh + 1]
                beta = jnp.exp(lbs[hh][0])
                dz = es[hh] * (1.0 - beta) - (prun + ces[hh]) * beta
                if masked:
                    dz = jnp.where(mask, dz, 0.0)
                dzbuf[hh] = dz.astype(BF16)
                sumbuf[2 * hh + 1] = prun + jnp.sum(es[hh], axis=1, keepdims=True)

        nfull = i * per_q
        last = nfull + per_q - 1
        for buf in (dzbuf, abuf, dqbuf, sumbuf):
            buf[...] = jnp.zeros_like(buf)
        scores(0)

        def full_block(jb, carry):
            trip(jb, False)
            return carry

        lax.fori_loop(0, nfull, full_block, 0)
        for dblk in range(per_q):
            trip(nfull + dblk, True)
        products(last)
        dq_ref[...] = (jnp.where(hi_lanes, dqbuf[1], dqbuf[0]) * scale).astype(BF16)

        @pl.when(i == nq - 1)
        def _():
            dk_ref[...] = dk_acc[...].astype(BF16)
            dv_ref[...] = dv_acc[...].astype(BF16)

        pl.when((pair == npair - 1) & (i == nq - 1))(finish)

    blk = pl.BlockSpec((tq, LANES), lambda p, i: (i, p))
    full = pl.BlockSpec((t, LANES), lambda p, i: (0, p))
    return pl.pallas_call(
        body, name=name, grid=(npair, nq),
        in_specs=[blk,
                  pl.BlockSpec((t, LANES), lambda p, i: (0, npair + p)),
                  pl.BlockSpec((t, LANES), lambda p, i: (0, 2 * npair + p)),
                  pl.BlockSpec((tq, LANES), lambda p, i: (i, npair + p)),
                  blk,
                  pl.BlockSpec((2, SB_KEYS, SB_KEYS), lambda p, i: (0, 0, 0)),
                  HBM_SPEC],
        out_specs=[blk, full, full, HBM_SPEC],
        out_shape=[jax.ShapeDtypeStruct((t, npair * LANES), BF16)] * 3 + [jax.ShapeDtypeStruct(chip_part.shape, chip_part.dtype)],
        scratch_shapes=[pltpu.VMEM((t, LANES), F32), pltpu.VMEM((t, LANES), F32),
                        pltpu.VMEM((2, tq, SB_KEYS), F32), pltpu.VMEM((2, tq, SB_KEYS), F32),
                        pltpu.VMEM((2, tq, SB_KEYS), BF16), pltpu.VMEM((2, tq, SB_KEYS), BF16),
                        pltpu.VMEM((2, tq, LANES), F32), pltpu.VMEM((4, tq, 1), F32)] + _chip_exchange_scratch(),
        compiler_params=_cp(dimension_semantics=("arbitrary", "arbitrary")))(pb, pb, pb, dy, ltot, cmats, chip_part)


def _hgrn_consts():
    t = lax.broadcasted_iota(jnp.int32, (CHUNK, CHUNK), 0)
    s = lax.broadcasted_iota(jnp.int32, (CHUNK, CHUNK), 1)
    masks = []
    for lvl in range(N_LEVELS):
        half = CHUNK >> (lvl + 1)
        same = (t // (2 * half)) == (s // (2 * half))
        masks.append((same & (t % (2 * half) >= half) & (s % (2 * half) < half)).astype(F32))
    masks.append((t == s).astype(F32))
    prefix = (s <= t).astype(BF16)
    suffix = (s >= t).astype(BF16)
    return prefix, jnp.stack(masks), suffix


def _hgrn_gates(qr, fr, lbv):
    sg = _sigmoid(fr)
    fval = lbv + (1.0 - lbv) * sg
    kk = (1.0 - lbv) * _sigmoid(-fr)
    sq = _sigmoid(qr)
    return sg, fval, jnp.log(fval), kk, sq, qr * sq


def _lower_bound(c_ref):
    c = c_ref[...]
    mx = jnp.max(c, axis=0, keepdims=True)
    ex = jnp.exp(c - mx)
    return ex[1:2, :] / jnp.sum(ex, axis=0, keepdims=True)


def _level_ref(b, lvl):
    half = CHUNK >> (lvl + 1)
    seg = 2 * half
    if seg >= 8:
        b3 = b.reshape(CHUNK // seg, seg, LANES)
        return jnp.broadcast_to(b3[:, half - 1:half, :], b3.shape).reshape(CHUNK, LANES)
    pos = lax.broadcasted_iota(jnp.int32, b.shape, 0) % seg
    out = b
    for p in range(seg):
        if p != half - 1:
            out = jnp.where(pos == p, pltpu.roll(b, (p - (half - 1)) % CHUNK, 0), out)
    return out


def _hgrn_levels(b, qs, kk):
    out = []
    for lvl in range(N_LEVELS):
        fac = jnp.exp(-jnp.abs(b - _level_ref(b, lvl)))
        out.append((qs * fac, kk * fac, fac, fac))
    out.append((qs, kk, None, None))
    return out


def _split2(x):
    hi = x.astype(BF16)
    return hi, (x - hi.astype(F32)).astype(BF16)


def _hgrn_fwd(pc, c_lb, out_norm, *, name, tc=512):
    t = pc.shape[0]
    nh = pc.shape[1] // 4 // LANES
    tc = min(tc, t)
    nch = tc // CHUNK
    cum_all, masks, _ = _hgrn_consts()

    def body(q_ref, f_ref, i_ref, g_ref, lb_ref, on_ref, cum_ref, m_ref, y_ref, o_ref, st_ref, state):
        @pl.when(pl.program_id(1) == 0)
        def _():
            state[...] = jnp.zeros_like(state)

        lbv = _lower_bound(lb_ref)
        onv = on_ref[...]

        def chunk(c, carry):
            rows = pl.ds(pl.multiple_of(c * CHUNK, CHUNK), CHUNK)
            for hh in range(HGRN_HEADS):
                lanes = slice(hh * LANES, (hh + 1) * LANES)
                _, _, g, kk, _, qs = _hgrn_gates(q_ref[rows, lanes], f_ref[rows, lanes], lbv[:, lanes])
                vb = i_ref[rows, lanes].astype(BF16)
                b = _dot_exact_lhs(cum_ref[...], g)
                scores = jnp.zeros((CHUNK, CHUNK), F32)
                for lvl, (ql, kl, _, _) in enumerate(_hgrn_levels(b, qs, kk)):
                    scores = scores + _dot(ql.astype(BF16), kl.astype(BF16), 1, 1) * m_ref[lvl]
                st = state[hh]
                st_ref[hh, c] = st
                o = _dot(scores.astype(BF16), vb) + _dot((qs * jnp.exp(b)).astype(BF16), st.astype(BF16), 1, 1)
                blast = b[CHUNK - 1:CHUNK, :]
                kdec = (kk * jnp.exp(blast - b)).astype(BF16)
                state[hh] = st * jnp.exp(blast) + _dot(vb, kdec, 0, 0)
                o_ref[rows, lanes] = o
                rstd = lax.rsqrt(jnp.mean(o * o, axis=-1, keepdims=True) + RMS_EPS)
                gate = g_ref[rows, lanes]
                y_ref[rows, lanes] = (o * rstd * onv * (gate * _sigmoid(gate))).astype(BF16)
            return carry

        lax.fori_loop(0, nch, chunk, 0, unroll=2)

    hw = HGRN_HEADS * LANES

    def col(off):
        return pl.BlockSpec((tc, hw), lambda h, i: (i, off // HGRN_HEADS + h))

    osp = pl.BlockSpec((tc, hw), lambda h, i: (i, h))
    return pl.pallas_call(
        body, name=name, grid=(nh // HGRN_HEADS, t // tc),
        in_specs=[col(0), col(nh), col(2 * nh), col(3 * nh),
                  pl.BlockSpec((2, hw), lambda h, i: (0, h)),
                  pl.BlockSpec((1, LANES), lambda h, i: (0, 0)),
                  pl.BlockSpec(cum_all.shape, lambda h, i: (0, 0)),
                  pl.BlockSpec(masks.shape, lambda h, i: (0, 0, 0))],
        out_specs=[osp, osp, pl.BlockSpec((HGRN_HEADS, nch, LANES, LANES), lambda h, i: (h, i, 0, 0))],
        out_shape=[jax.ShapeDtypeStruct((t, nh * LANES), BF16), jax.ShapeDtypeStruct((t, nh * LANES), F32),
                   jax.ShapeDtypeStruct((nh, t // CHUNK, LANES, LANES), F32)],
        scratch_shapes=[pltpu.VMEM((HGRN_HEADS, LANES, LANES), F32)],
        compiler_params=_cp())(pc, pc, pc, pc, c_lb, out_norm, cum_all, masks)


def _hgrn_bwd(pc, o_saved, states, dy, c_lb, out_norm, *, name, tc=512):
    t = pc.shape[0]
    nh = pc.shape[1] // 4 // LANES
    tc = min(tc, t)
    nch = tc // CHUNK
    nt = t // tc
    cum_all, masks, suffix = _hgrn_consts()

    def body(q_ref, f_ref, i_ref, g_ref, o_ref, st_ref, dy_ref, lb_ref, on_ref, cum_ref, m_ref, suf_ref,
             dq_ref, df_ref, di_ref, dg_ref, dlb_ref, don_ref, dstate):
        @pl.when(pl.program_id(1) == 0)
        def _():
            dstate[...] = jnp.zeros_like(dstate)
            dlb_ref[...] = jnp.zeros_like(dlb_ref)
            don_ref[...] = jnp.zeros_like(don_ref)

        lbv = _lower_bound(lb_ref)
        onv = on_ref[...]

        def head(hh, c, rows):
            lanes = slice(hh * LANES, (hh + 1) * LANES)
            qr = q_ref[rows, lanes]
            sg, fval, g, kk, sq, qs = _hgrn_gates(qr, f_ref[rows, lanes], lbv[:, lanes])
            vb = i_ref[rows, lanes].astype(BF16)
            o = o_ref[rows, lanes]
            gate = g_ref[rows, lanes]
            sgt = _sigmoid(gate)
            rstd = lax.rsqrt(jnp.mean(o * o, axis=-1, keepdims=True) + RMS_EPS)
            ohat = o * rstd
            dyv = dy_ref[rows, lanes]
            don = dyv * (gate * sgt)
            dg_ref[rows, lanes] = (dyv * ohat * onv * (sgt * (1.0 + gate * (1.0 - sgt)))).astype(BF16)
            don_ref[:, lanes] += jnp.sum(don * ohat, axis=0, keepdims=True)
            dxhat = don * onv
            dob = (rstd * (dxhat - ohat * jnp.mean(dxhat * ohat, axis=-1, keepdims=True))).astype(BF16)
            b = _dot_exact_lhs(cum_ref[...], g)
            blast = b[CHUNK - 1:CHUNK, :]
            eb = jnp.exp(b)
            edec = jnp.exp(blast - b)
            st32 = st_ref[hh, c]
            st = st32.astype(BF16)
            dst = dstate[hh]
            dstb = dst.astype(BF16)
            da = _dot(dob, vb, 1, 1)
            levels = _hgrn_levels(b, qs, kk)
            scores = jnp.zeros((CHUNK, CHUNK), F32)
            dq = eb * _dot(dob, st)
            dk_inter = edec * _dot(vb, dstb)
            dk = dk_inter
            for lvl, (ql, kl, eq, ek) in enumerate(levels):
                mk = m_ref[lvl]
                (qh, qlo), (kh, klo) = _split2(ql), _split2(kl)
                scores = scores + _dot(qh, kh, 1, 1) * mk
                dal = (da * mk).astype(BF16)
                dql = _dot(dal, kh) + _dot(dal, klo)
                dkl = _dot(dal, qh, 0, 0) + _dot(dal, qlo, 0, 0)
                dq = dq + (dql if eq is None else dql * eq)
                dk = dk + (dkl if ek is None else dkl * ek)
            kdec = (kk * edec).astype(BF16)
            dv = _dot(scores.astype(BF16), dob, 0, 0) + _dot(kdec, dstb, 1, 1)
            dstate[hh] = dst * jnp.exp(blast) + _dot(dob, (qs * eb).astype(BF16), 0, 0)
            db = qs * dq - kk * dk
            last = jnp.sum(kk * dk_inter, axis=0, keepdims=True) + jnp.exp(blast) * jnp.sum(dst * st32, axis=0, keepdims=True)
            dgl = _dot_exact_lhs(suf_ref[...], db) + last
            dfv = dgl / fval - dk
            df_ref[rows, lanes] = (dfv * (1.0 - lbv[:, lanes]) * sg * (1.0 - sg)).astype(BF16)
            dlb_ref[:, lanes] += jnp.sum(dfv * (1.0 - sg), axis=0, keepdims=True)
            dq_ref[rows, lanes] = (dq * (sq * (1.0 + qr * (1.0 - sq)))).astype(BF16)
            di_ref[rows, lanes] = dv.astype(BF16)

        def chunk(n, carry):
            c = nch - 1 - n
            rows = pl.ds(pl.multiple_of(c * CHUNK, CHUNK), CHUNK)
            for hh in range(HGRN_HEADS):
                head(hh, c, rows)
            return carry

        lax.fori_loop(0, nch, chunk, 0, unroll=2)

    hw = HGRN_HEADS * LANES

    def col(off):
        return pl.BlockSpec((tc, hw), lambda h, i: (nt - 1 - i, off // HGRN_HEADS + h))

    osp = pl.BlockSpec((tc, hw), lambda h, i: (nt - 1 - i, h))
    vec = pl.BlockSpec((1, hw), lambda h, i: (0, h))
    return pl.pallas_call(
        body, name=name, grid=(nh // HGRN_HEADS, nt),
        in_specs=[col(0), col(nh), col(2 * nh), col(3 * nh), osp,
                  pl.BlockSpec((HGRN_HEADS, nch, LANES, LANES), lambda h, i: (h, nt - 1 - i, 0, 0)),
                  osp,
                  pl.BlockSpec((2, hw), lambda h, i: (0, h)),
                  pl.BlockSpec((1, LANES), lambda h, i: (0, 0)),
                  pl.BlockSpec(cum_all.shape, lambda h, i: (0, 0)),
                  pl.BlockSpec(masks.shape, lambda h, i: (0, 0, 0)),
                  pl.BlockSpec(suffix.shape, lambda h, i: (0, 0))],
        out_specs=[osp, osp, osp, osp, vec, vec],
        out_shape=[jax.ShapeDtypeStruct((t, nh * LANES), BF16)] * 4 + [jax.ShapeDtypeStruct((1, nh * LANES), F32)] * 2,
        scratch_shapes=[pltpu.VMEM((HGRN_HEADS, LANES, LANES), F32)],
        compiler_params=_cp())(pc, pc, pc, pc, o_saved, states, dy, c_lb, out_norm, cum_all, masks, suffix)


HBM_SPEC = pl.BlockSpec(memory_space=pltpu.HBM)


def _gather_scratch():
    return [pltpu.SemaphoreType.DMA((7,)), pltpu.SemaphoreType.DMA((7,)), pltpu.SemaphoreType.DMA]


def _gather_phases(x_ref, out_refs, seg_rows, send_sems, recv_sems, local_sem):
    x, y, c = lax.axis_index("x"), lax.axis_index("y"), lax.axis_index("c")
    me, sibling = (x, y, c), (x, y, 1 - c)
    chips = [(1 - x, y), (x, 1 - y), (1 - x, 1 - y)]
    offs = [sum(seg_rows[:s]) for s in range(len(seg_rows))]
    assert sum(seg_rows) == x_ref.shape[0]

    def index(px, py, pc):
        return 4 * px + 2 * py + pc

    def copies(k, block, to, own):
        return [pltpu.make_async_remote_copy(
            src_ref=x_ref.at[pl.ds(offs[s], n)] if own else out_refs[s].at[index(*block)],
            dst_ref=out_refs[s].at[index(*block)],
            send_sem=send_sems.at[k], recv_sem=recv_sems.at[k], device_id=to, device_id_type=MESH)
            for s, n in enumerate(seg_rows)]

    def all_bytes(k):
        return pltpu.make_async_remote_copy(src_ref=x_ref, dst_ref=x_ref, send_sem=send_sems.at[k],
                                            recv_sem=recv_sems.at[k], device_id=me, device_id_type=MESH)

    mine = [pltpu.make_async_copy(x_ref.at[pl.ds(offs[s], n)], out_refs[s].at[index(*me)], local_sem)
            for s, n in enumerate(seg_rows)]
    first = copies(0, me, sibling, True)
    for j, chip in enumerate(chips):
        first += copies(1 + j, me, (*chip, c), True)

    def start():
        for cp in mine + first:
            cp.start()

    def forward():
        for j, chip in enumerate(chips):
            all_bytes(1 + j).wait_recv()
            for cp in copies(4 + j, (*chip, c), sibling, False):
                cp.start()

    def finish():
        all_bytes(0).wait_recv()
        for j in range(3):
            all_bytes(4 + j).wait_recv()
        for k in range(7):
            all_bytes(k).wait_send()
        pltpu.make_async_copy(x_ref, x_ref, local_sem).wait()

    return start, forward, finish


def _all_gather(xs, seg_rows=None, *, name):
    segs = [xs.shape[0]] if seg_rows is None else list(seg_rows)

    def body(x_ref, *rest):
        start, forward, finish = _gather_phases(x_ref, rest[:len(segs)], segs, *rest[len(segs):])
        start()
        forward()
        finish()

    outs = pl.pallas_call(
        body, name=name, in_specs=[HBM_SPEC], out_specs=[HBM_SPEC] * len(segs),
        out_shape=[jax.ShapeDtypeStruct((8, n, xs.shape[1]), xs.dtype) for n in segs],
        scratch_shapes=_gather_scratch())(xs)
    return outs[0] if seg_rows is None else outs


def _sibling_exchange(s, *, name):
    def body(s_ref, rb_ref, send_sem, recv_sem):
        x, y, c = lax.axis_index("x"), lax.axis_index("y"), lax.axis_index("c")
        cp = pltpu.make_async_remote_copy(
            src_ref=s_ref.at[:, 1 - c], dst_ref=rb_ref, send_sem=send_sem, recv_sem=recv_sem,
            device_id=(x, y, 1 - c), device_id_type=MESH)
        cp.start()
        cp.wait()

    return pl.pallas_call(
        body, name=name, in_specs=[HBM_SPEC], out_specs=HBM_SPEC,
        out_shape=jax.ShapeDtypeStruct(s.shape[:1] + s.shape[2:], s.dtype),
        scratch_shapes=[pltpu.SemaphoreType.DMA, pltpu.SemaphoreType.DMA])(s)


def _row_tile(n, cap=1024):
    return max(b for b in range(16, cap + 1, 16) if n % b == 0)


def _pair_add(s, rb, core, *, name):
    nchip, _, r, c = s.shape
    tb = _row_tile(r)

    def body(core_ref, a_ref, b_ref, o_ref):
        o_ref[...] = (a_ref[...].astype(F32) + b_ref[...].astype(F32)).astype(BF16)

    blk = pl.BlockSpec((None, tb, c), lambda ch, i, cr: (ch, i, 0))
    return pl.pallas_call(
        body, name=name,
        grid_spec=pltpu.PrefetchScalarGridSpec(
            num_scalar_prefetch=1, grid=(nchip, r // tb),
            in_specs=[pl.BlockSpec((None, None, tb, c), lambda ch, i, cr: (ch, cr[0], i, 0)), blk],
            out_specs=blk),
        out_shape=jax.ShapeDtypeStruct((nchip, r, c), BF16), compiler_params=_cp())(core, s, rb)


def _chip_exchange_scratch():
    return [pltpu.SemaphoreType.DMA((3,)), pltpu.SemaphoreType.DMA((3,)), pltpu.SemaphoreType.DMA]


def _chip_exchange_phases(p_ref, out_ref, send_sems, recv_sems, local_sem):
    x, y, c = lax.axis_index("x"), lax.axis_index("y"), lax.axis_index("c")
    mine = 2 * x + y
    own = pltpu.make_async_copy(p_ref.at[mine], out_ref.at[mine], local_sem)
    copies = [pltpu.make_async_remote_copy(
        src_ref=p_ref.at[2 * tx + ty], dst_ref=out_ref.at[mine],
        send_sem=send_sems.at[k], recv_sem=recv_sems.at[k], device_id=(tx, ty, c), device_id_type=MESH)
        for k, (tx, ty) in enumerate([(1 - x, y), (x, 1 - y), (1 - x, 1 - y)])]

    def start():
        own.start()
        for cp in copies:
            cp.start()

    def finish():
        for cp in copies:
            cp.wait()
        own.wait()

    return start, finish


def _adamw_math(w, g, m, v):
    m2 = ADAM_B1 * m + (1.0 - ADAM_B1) * g
    v2 = ADAM_B2 * v + (1.0 - ADAM_B2) * (g * g)
    m_hat = m2 / (1.0 - ADAM_B1 ** ADAM_STEP)
    v_hat = v2 / (1.0 - ADAM_B2 ** ADAM_STEP)
    return -ADAM_LR * (m_hat / (jnp.sqrt(v_hat) + ADAM_EPS) + ADAM_WD * w), m2, v2


def _grad_sum(parts, *, name):
    _, r, c = parts.shape
    tb = _row_tile(r)

    def body(p0, p1, p2, p3, g_out):
        g_out[...] = ((p0[...].astype(F32) + p1[...].astype(F32)) + p2[...].astype(F32)) + p3[...].astype(F32)

    def part(ch):
        return pl.BlockSpec((None, tb, c), lambda i: (ch, i, 0))

    return pl.pallas_call(
        body, name=name, grid=(r // tb,), in_specs=[part(0), part(1), part(2), part(3)],
        out_specs=pl.BlockSpec((tb, c), lambda i: (i, 0)), out_shape=jax.ShapeDtypeStruct((r, c), F32),
        compiler_params=_cp())(parts, parts, parts, parts)


def _adamw_shard(g, g_off, w, m, v, layer, prev, *, name):
    _, r, c = w.shape
    tb = next(b for b in range(min(r, 512), 0, -8) if r % b == 0 and g_off % b == 0)

    def body(g_ref, w_ref, m_ref, v_ref, *rest):
        d_out, m_out, v_out = rest[-3:]
        d, m2, v2 = _adamw_math(w_ref[...], g_ref[...], m_ref[...], v_ref[...])
        d_out[...] = d
        m_out[...] = m2
        v_out[...] = v2

    blk = pl.BlockSpec((None, tb, c), lambda i: (layer, i, 0))
    prev = list(prev) if prev is not None else []
    return pl.pallas_call(
        body, name=name, grid=(r // tb,),
        in_specs=[pl.BlockSpec((tb, c), lambda i: (g_off // tb + i, 0)), blk, blk, blk] + [pl.BlockSpec(memory_space=pl.ANY)] * len(prev),
        out_specs=[blk] * 3, out_shape=[jax.ShapeDtypeStruct(w.shape, F32)] * 3,
        input_output_aliases={4 + k: k for k in range(len(prev))},
        compiler_params=_cp())(g, w, m, v, *prev)


SLOT = 8
SMALL_ROWS = 6 * SLOT
ROW_LB = 4 * SLOT


def _small_update(gath, w, m, v, *, name):
    def body(g_ref, w_ref, m_ref, v_ref, g_out, d_out, m_out, v_out):
        tot = g_ref[0]
        for k in range(1, 8):
            tot = tot + g_ref[k]
        wv = w_ref[...]
        c0, c1 = wv[ROW_LB:ROW_LB + 1, :], wv[ROW_LB + 1:ROW_LB + 2, :]
        mx = jnp.maximum(c0, c1)
        e0, e1 = jnp.exp(c0 - mx), jnp.exp(c1 - mx)
        lb = e1 / (e0 + e1)
        gl = tot[ROW_LB:ROW_LB + 1, :] * lb * (1.0 - lb)
        row = lax.broadcasted_iota(jnp.int32, tot.shape, 0)
        g = jnp.where(row == ROW_LB, -gl, jnp.where(row == ROW_LB + 1, gl, tot))
        d, m2, v2 = _adamw_math(wv, g, m_ref[...], v_ref[...])
        g_out[...] = g
        d_out[...] = d
        m_out[...] = m2
        v_out[...] = v2

    return pl.pallas_call(
        body, name=name, out_shape=[jax.ShapeDtypeStruct(w.shape, F32)] * 4, compiler_params=_cp())(gath, w, m, v)


D_MODEL = 1024


def _ffn_fwd(h, gain, wg, wu, wd, tag):
    xn, gg, uu, act = _norm_gate_up(h, gain, wg, wu, name=f"{tag}_gate_up")
    out = _mm([(act, wd)], residual=h, alpha=MACARON, tn=1024, name=f"{tag}_down")
    return out, (h, xn, gg, uu, act)


def _ffn_input_bwd(dg, du, wg, wu, x, gain, dres, chip_part, *, name, scale, tm=256):
    t, d = x.shape
    f = wg.shape[0]
    tm = min(tm, t)
    nt = t // tm
    fused = chip_part is not None

    def body(dg_ref, du_ref, wg_ref, wu_ref, x_ref, g_ref, dres_ref, *rest):
        if fused:
            part_ref, dx_ref, dxb_ref, dgain_ref, parts_ref = rest[:5]
            start, finish = _chip_exchange_phases(part_ref, parts_ref, *rest[5:])
            pl.when(pl.program_id(0) == 0)(start)
        else:
            dx_ref, dxb_ref, dgain_ref = rest
        dxn_v = _dot(dg_ref[...], wg_ref[...]) + _dot(du_ref[...], wu_ref[...])
        xv = x_ref[...]
        rstd = lax.rsqrt(jnp.mean(xv * xv, axis=-1, keepdims=True) + RMS_EPS)
        xhat = xv * rstd
        dxhat = dxn_v * g_ref[...]
        dx = dres_ref[...] + rstd * (dxhat - xhat * jnp.mean(dxhat * xhat, axis=-1, keepdims=True))
        dx_ref[...] = dx
        dxb_ref[...] = (dx * scale).astype(BF16)

        @pl.when(pl.program_id(0) == 0)
        def _():
            dgain_ref[...] = jnp.zeros_like(dgain_ref)

        dgain_ref[...] += jnp.sum(dxn_v * xhat, axis=0, keepdims=True)
        if fused:
            pl.when(pl.program_id(0) == nt - 1)(finish)

    wide = pl.BlockSpec((tm, f), lambda i: (i, 0))
    wsp = pl.BlockSpec((f, d), lambda i: (0, 0))
    row = pl.BlockSpec((tm, d), lambda i: (i, 0))
    vec = pl.BlockSpec((1, d), lambda i: (0, 0))
    args = [dg, du, wg, wu, x, gain, dres] + ([chip_part] if fused else [])
    return pl.pallas_call(
        body, name=name, grid=(nt,),
        in_specs=[wide, wide, wsp, wsp, row, vec, row] + ([HBM_SPEC] if fused else []),
        out_specs=[row, row, vec] + ([HBM_SPEC] if fused else []),
        out_shape=[jax.ShapeDtypeStruct((t, d), F32), jax.ShapeDtypeStruct((t, d), BF16), jax.ShapeDtypeStruct((1, d), F32)]
        + ([jax.ShapeDtypeStruct(chip_part.shape, chip_part.dtype)] if fused else []),
        scratch_shapes=_chip_exchange_scratch() if fused else [],
        compiler_params=_cp(dimension_semantics=("arbitrary",)))(*args)


def _ffn_bwd(dout, dout_half, saved, gain, wg, wu, wd, tag, next_scale, make_chip_part=None):
    h, xn, gg, uu, act = saved
    dg, du = _swiglu_bwd(dout_half, wd, gg, uu, name=f"{tag}_dact")
    dwd = _mm([(act, dout_half)], ta=True, tm=256, tn=1024, out_dtype=BF16, name=f"{tag}_dwd")
    dwg = _mm([(dg, xn)], ta=True, tm=256, tn=1024, out_dtype=BF16, name=f"{tag}_dwg")
    dwu = _mm([(du, xn)], ta=True, tm=256, tn=1024, out_dtype=BF16, name=f"{tag}_dwu")
    chip_part = make_chip_part(dwg, dwu, dwd) if make_chip_part is not None else None
    dh, dh_b, dgain, *parts = _ffn_input_bwd(dg, du, wg, wu, h, gain, dout, chip_part, scale=next_scale,
                                             name=f"{tag}_input_bwd")
    return dh, dh_b, dwg, dwu, dwd, dgain, (parts[0] if parts else None)


def kernel(x, ffn_pre_norm, ffn_pre_w_gate, ffn_pre_w_up, ffn_pre_w_down, mix_norm, ffn_post_norm, ffn_post_w_gate, ffn_post_w_up, ffn_post_w_down, ab_w_in, ab_conv_w, ab_w_out, c_w_in, c_lower_bounds, c_out_norm, c_w_out, final_norm, loss_target, m_ffn_pre_norm, m_ffn_pre_w_gate, m_ffn_pre_w_up, m_ffn_pre_w_down, m_mix_norm, m_ffn_post_norm, m_ffn_post_w_gate, m_ffn_post_w_up, m_ffn_post_w_down, m_ab_w_in, m_ab_conv_w, m_ab_w_out, m_c_w_in, m_c_lower_bounds, m_c_out_norm, m_c_w_out, m_final_norm, v_ffn_pre_norm, v_ffn_pre_w_gate, v_ffn_pre_w_up, v_ffn_pre_w_down, v_mix_norm, v_ffn_post_norm, v_ffn_post_w_gate, v_ffn_post_w_up, v_ffn_post_w_down, v_ab_w_in, v_ab_conv_w, v_ab_w_out, v_c_w_in, v_c_lower_bounds, v_c_out_norm, v_c_w_out, v_final_norm):
    d = D_MODEL
    h0 = x[0]
    target = loss_target[0]
    core = lax.axis_index("c").astype(jnp.int32).reshape(1)

    big = [("pre_g", ffn_pre_w_gate, m_ffn_pre_w_gate, v_ffn_pre_w_gate),
           ("pre_u", ffn_pre_w_up, m_ffn_pre_w_up, v_ffn_pre_w_up),
           ("pre_d", ffn_pre_w_down, m_ffn_pre_w_down, v_ffn_pre_w_down),
           ("post_g", ffn_post_w_gate, m_ffn_post_w_gate, v_ffn_post_w_gate),
           ("post_u", ffn_post_w_up, m_ffn_post_w_up, v_ffn_post_w_up),
           ("post_d", ffn_post_w_down, m_ffn_post_w_down, v_ffn_post_w_down),
           ("ab_in", ab_w_in, m_ab_w_in, v_ab_w_in),
           ("ab_out", ab_w_out, m_ab_w_out, v_ab_w_out),
           ("c_in", c_w_in, m_c_w_in, v_c_w_in),
           ("c_out", c_w_out, m_c_w_out, v_c_w_out)]
    by_tag = {tag: (w, m, v) for tag, w, m, v in big}

    def layer_rows(tag):
        w = by_tag[tag][0]
        return w.size // d // w.shape[0]

    def layout(items):
        offs, off = {}, 0
        for item in items:
            offs[item] = off
            off += layer_rows(item[0])
        return offs, off

    ffn = [f"{pos}_{kind}" for pos in ("pre", "post") for kind in "gud"]
    first_items = [("pre_g", 0), ("pre_u", 0)]
    early_items = [("pre_d", 0), ("ab_in", 0)]
    late_items = ([("pre_g", 1), ("pre_u", 1), ("pre_d", 1)] + [(f"post_{kind}", l) for l in (0, 1) for kind in "gud"]
                  + [("ab_out", 0), ("c_in", 0), ("c_out", 0)])
    grad_items = {"A": ([(tag, 1) for tag in ffn] + [(f"post_{kind}", 0) for kind in "gud"]
                        + [("c_in", 0), ("c_out", 0), ("ab_out", 0)]),
                  "B": [(f"pre_{kind}", 0) for kind in "gud"] + [("ab_in", 0)]}
    grad_offs = {k: layout(items)[0] for k, items in grad_items.items()}
    grad_conv_row = layout(grad_items["B"])[1]

    def conv_rows(a, split):
        flat = a.reshape(-1)
        if split:
            hi = flat.astype(BF16)
            flat = jnp.concatenate([hi, (flat - hi.astype(F32)).astype(BF16)])
        return jnp.zeros((16, d), flat.dtype).at[0, :flat.shape[0]].set(flat)

    nconv = ab_conv_w.size
    col_sharded = {"pre_g", "pre_u", "post_g", "post_u", "ab_in", "c_in"}

    def pack_rows(item):
        tag, layer = item
        a = by_tag[tag][0][layer]
        return (a.T if tag in col_sharded else a).reshape(-1, d).astype(BF16)

    first_pack = jnp.concatenate([pack_rows(item) for item in first_items], axis=0)
    early_pack = jnp.concatenate([pack_rows(item) for item in early_items] + [conv_rows(ab_conv_w, True)], axis=0)
    late_pack = jnp.concatenate([pack_rows(item) for item in late_items], axis=0)
    first_w = _all_gather(first_pack, [layer_rows(tag) for tag, _ in first_items], name="gather_first_weights")
    full = {item: g.reshape(-1, d) for item, g in zip(first_items, first_w)}

    xn0, gg0, uu0, act0, *early_w = _norm_gate_up(
        h0, ffn_pre_norm[0:1], full["pre_g", 0], full["pre_u", 0], name="l0pre_gate_up_gather_early_weights",
        pack=early_pack, seg_rows=[layer_rows(tag) for tag, _ in early_items] + [16])
    full.update({item: g.reshape(-1, d) for item, g in zip(early_items, early_w)})
    ffn_w = {("pre", 0): tuple(full[f"pre_{kind}", 0] for kind in "gud")}
    w_ab_in = full["ab_in", 0]
    cg = early_w[-1][:, 0, :2 * nconv].astype(F32)
    conv_w = (cg[:, :nconv] + cg[:, nconv:]).reshape(8, 3, -1).transpose(1, 0, 2).reshape(3, -1)
    half = w_ab_in.shape[0] // 2
    w_a_in, w_b_in = w_ab_in[:half], w_ab_in[half:]
    aw = half // 3
    h1 = _mm([(act0, full["pre_d", 0])], residual=h0, alpha=MACARON, tn=1024, name="l0pre_down")
    s_pre0 = (h0, xn0, gg0, uu0, act0)
    hn0 = _rmsnorm_fwd(h1, mix_norm[0:1], name="l0_mix_norm")
    pa = _mm([(hn0, w_a_in)], tb=True, tn=1536, name="ab_proj_a")
    pb = _mm([(hn0, w_b_in)], tb=True, tn=1536, out_dtype=BF16, name="ab_proj_b")
    ya = _conv_fwd(pa, conv_w, name="conv_fwd")
    yb, ltot, *late_w = _attn_fwd(pb, late_pack, [layer_rows(tag) for tag, _ in late_items],
                                  name="attn_fwd_gather_late_weights")
    full.update({item: g.reshape(-1, d) for item, g in zip(late_items, late_w)})
    for pos, layer in (("post", 0), ("pre", 1), ("post", 1)):
        ffn_w[pos, layer] = tuple(full[f"{pos}_{kind}", layer] for kind in "gud")
    w_ab_out, w_c_in, w_c_out = full["ab_out", 0], full["c_in", 0], full["c_out", 0]
    h2 = _mm([(ya, w_ab_out[:aw]), (yb, w_ab_out[aw:])], residual=h1, tn=1024, name="ab_out")
    h3, s_post0 = _ffn_fwd(h2, ffn_post_norm[0:1], *ffn_w["post", 0], "l0post")
    h4, s_pre1 = _ffn_fwd(h3, ffn_pre_norm[1:2], *ffn_w["pre", 1], "l1pre")
    hn1 = _rmsnorm_fwd(h4, mix_norm[1:2], name="l1_mix_norm")
    pc = _mm([(hn1, w_c_in)], tb=True, tm=256, tn=4096, name="c_proj")
    yc, o_saved, states = _hgrn_fwd(pc, c_lower_bounds, c_out_norm, name="hgrn_fwd")
    h5 = _mm([(yc, w_c_out)], residual=h4, tn=1024, name="c_out")
    h6, s_post1 = _ffn_fwd(h5, ffn_post_norm[1:2], *ffn_w["post", 1], "l1post")
    dh6, dh6_b, d_final, loss_vec = _loss_head(h6, final_norm.reshape(1, d), target, name="loss_head")

    gw = {}
    dh5, dh5_b, gw["post_g", 1], gw["post_u", 1], gw["post_d", 1], d_post1, _ = _ffn_bwd(
        dh6, dh6_b, s_post1, ffn_post_norm[1:2], *ffn_w["post", 1], "l1post", 1.0)
    dyc = _mm([(dh5_b, w_c_out)], tb=True, tn=1024, name="c_out_dy")
    g_c_out = _mm([(yc, dh5_b)], ta=True, tm=256, tn=1024, out_dtype=BF16, name="c_out_dw")
    dcq, dcf, dci, dcg, dlb, d_onorm = _hgrn_bwd(pc, o_saved, states, dyc, c_lower_bounds, c_out_norm, name="hgrn_bwd")
    dparts = [dcq, dcf, dci, dcg]
    g_c_in = jnp.concatenate(
        [_mm([(dp, hn1)], ta=True, tm=256, tn=1024, out_dtype=BF16, name=f"c_in_dw{i}") for i, dp in enumerate(dparts)],
        axis=0)
    cw = w_c_in.shape[0] // 4
    dhn1 = _mm([(dp, w_c_in[i * cw:(i + 1) * cw]) for i, dp in enumerate(dparts)], tm=512, tn=1024, name="c_in_dx")
    dh4, dh4_b, d_mix1 = _rmsnorm_bwd(h4, mix_norm[1:2], dhn1, dh5, scale=MACARON, name="l1_mix_norm_bwd")
    dh3, dh3_b, gw["pre_g", 1], gw["pre_u", 1], gw["pre_d", 1], d_pre1, _ = _ffn_bwd(
        dh4, dh4_b, s_pre1, ffn_pre_norm[1:2], *ffn_w["pre", 1], "l1pre", MACARON)
    dh2, dh2_b, gw["post_g", 0], gw["post_u", 0], gw["post_d", 0], d_post0, _ = _ffn_bwd(
        dh3, dh3_b, s_post0, ffn_post_norm[0:1], *ffn_w["post", 0], "l0post", 1.0)
    dyab = _mm([(dh2_b, w_ab_out)], tb=True, tn=1024, name="ab_out_dy")
    g_ab_out = jnp.concatenate([_mm([(ya, dh2_b)], ta=True, tm=256, tn=1024, out_dtype=BF16, name="ab_out_dw_a"),
                                _mm([(yb, dh2_b)], ta=True, tm=256, tn=1024, out_dtype=BF16, name="ab_out_dw_b")], axis=0)
    dab, dac, dax, g_conv = _conv_bwd(pa, dyab, conv_w, name="conv_bwd")

    def chip_partials(key, grads, extra=()):
        gpack = jnp.concatenate([grads[item].reshape(8, -1, d) for item in grad_items[key]] + list(extra), axis=1)
        send = gpack.reshape(4, 2, gpack.shape[1], d)
        from_sibling = _sibling_exchange(send, name=f"grad{key}_sibling_exchange")
        return _pair_add(send, from_sibling, core, name=f"grad{key}_pair_add")

    gw["c_in", 0], gw["c_out", 0], gw["ab_out", 0] = g_c_in, g_c_out, g_ab_out
    chip_part_a = chip_partials("A", gw)
    dq, dk, dv, parts_a = _attn_bwd(pb, dyab, ltot, chip_part_a, name="attn_bwd_exchange_grads_a")
    dparts = [dab, dac, dax, dq, dk, dv]
    g_ab_in = jnp.concatenate(
        [_mm([(dp, hn0)], ta=True, tm=256, tn=1024, out_dtype=BF16, name=f"ab_in_dw{i}") for i, dp in enumerate(dparts)],
        axis=0)
    dhn0 = _mm([(dp, w_ab_in[i * aw:(i + 1) * aw]) for i, dp in enumerate(dparts)], tm=512, tn=1024, name="ab_in_dx")
    dh1, dh1_b, d_mix0 = _rmsnorm_bwd(h1, mix_norm[0:1], dhn0, dh2, scale=MACARON, name="l0_mix_norm_bwd")
    gw["ab_in", 0] = g_ab_in
    gconv_own = g_conv.reshape(3, 8, -1).transpose(1, 0, 2).reshape(8, -1)
    conv_piece = jnp.zeros((8, 16, d), F32).at[:, 0, :nconv].set(gconv_own).astype(BF16)

    def chip_part_b(dwg, dwu, dwd):
        gw["pre_g", 0], gw["pre_u", 0], gw["pre_d", 0] = dwg, dwu, dwd
        return chip_partials("B", gw, [conv_piece])

    dh0, _, _, _, _, d_pre0, parts_b = _ffn_bwd(
        dh1, dh1_b, s_pre0, ffn_pre_norm[0:1], *ffn_w["pre", 0], "l0pre", 1.0, chip_part_b)

    g_sum = {"A": _grad_sum(parts_a, name="gradA_sum"), "B": _grad_sum(parts_b, name="gradB_sum")}

    upd = {}
    for tag, w, m, v in big:
        nl = layer_rows(tag)
        view = (lambda a: jnp.swapaxes(a, 1, 2)) if tag in col_sharded else (lambda a: a)
        where = {layer: (key, grad_offs[key][tag, layer])
                 for key in ("A", "B") for t2, layer in grad_items[key] if t2 == tag}
        res = None
        for layer in sorted(where):
            key, off = where[layer]
            res = _adamw_shard(g_sum[key], off, view(w), view(m), view(v), layer, res, name=f"adamw_{tag}{layer}")
        g_nat = jnp.stack([g_sum[where[layer][0]][where[layer][1]:where[layer][1] + nl] for layer in sorted(where)])
        upd[tag] = [view(a) for a in [g_nat] + list(res)]
    res = _adamw_shard(g_sum["B"], grad_conv_row, *(conv_rows(a, False)[None] for a in (ab_conv_w, m_ab_conv_w, v_ab_conv_w)),
                       0, None, name="adamw_conv")
    g_conv_rows = g_sum["B"][grad_conv_row:grad_conv_row + 16]
    upd["conv"] = [r[0, :nconv].reshape(ab_conv_w.shape) for r in [g_conv_rows] + [r[0] for r in res]]

    def small_pack(pre, mix, post, final, lbs, onorm):
        def slot(parts):
            out, r = jnp.zeros((SLOT, d), F32), 0
            for a in (parts if isinstance(parts, tuple) else (parts,)):
                out = out.at[r:r + a.shape[0], :a.shape[1]].set(a)
                r += a.shape[0]
            return out

        return jnp.concatenate([slot(pre), slot(mix), slot(post), slot(final.reshape(1, d)), slot(lbs), slot(onorm)], axis=0)

    d_on = d_onorm.reshape(-1, c_out_norm.shape[1]).sum(axis=0, keepdims=True)
    gsmall = small_pack((d_pre0, d_pre1), (d_mix0, d_mix1), (d_post0, d_post1), d_final, dlb, d_on)
    gsmall_all = _all_gather(gsmall, name="gather_small_grads")
    sres = _small_update(
        gsmall_all,
        small_pack(ffn_pre_norm, mix_norm, ffn_post_norm, final_norm, c_lower_bounds, c_out_norm),
        small_pack(m_ffn_pre_norm, m_mix_norm, m_ffn_post_norm, m_final_norm, m_c_lower_bounds, m_c_out_norm),
        small_pack(v_ffn_pre_norm, v_mix_norm, v_ffn_post_norm, v_final_norm, v_c_lower_bounds, v_c_out_norm),
        name="small_update")

    def small_out(r):
        return {"pre_norm": r[0:2], "mix_norm": r[SLOT:SLOT + 2], "post_norm": r[2 * SLOT:2 * SLOT + 2],
                "final": r[3 * SLOT], "lb": r[ROW_LB:ROW_LB + 2], "onorm": r[5 * SLOT:5 * SLOT + 1, :c_out_norm.shape[1]]}

    small = [small_out(r) for r in sres]
    outs = []
    for k in range(4):
        s = small[k]
        outs += [s["pre_norm"], upd["pre_g"][k], upd["pre_u"][k], upd["pre_d"][k], s["mix_norm"], s["post_norm"],
                 upd["post_g"][k], upd["post_u"][k], upd["post_d"][k], upd["ab_in"][k], upd["conv"][k],
                 upd["ab_out"][k], upd["c_in"][k], s["lb"], s["onorm"], upd["c_out"][k], s["final"]]
    loss = lax.psum(loss_vec[0, 0], ("x", "y", "c"))
    return (loss, dh0[None], *outs)
```

```python
import functools
import math

import jax
import jax.numpy as jnp
from jax import lax
from jax.experimental import pallas as pl
from jax.experimental.pallas import tpu as pltpu

F32 = jnp.float32
BF16 = jnp.bfloat16
MESH = pl.DeviceIdType.MESH

RMS_EPS = 1e-6
MACARON = 0.5
LANES = 128
CHUNK = 64
N_LEVELS = 6
HGRN_HEADS = 2
SB_KEYS = 256
ADAM_LR, ADAM_B1, ADAM_B2, ADAM_EPS, ADAM_WD, ADAM_STEP = 0.001, 0.9, 0.999, 1e-08, 0.01, 10
VMEM_LIMIT = 48 * 1024 * 1024


def _cp(**kw):
    return pltpu.CompilerParams(vmem_limit_bytes=VMEM_LIMIT, **kw)


def _sigmoid(x):
    return 1.0 / (1.0 + jnp.exp(-x))


def _bf(x):
    return x if x.dtype == BF16 else x.astype(BF16)


def _split3(x):
    hi = x.astype(BF16)
    r1 = x - hi.astype(F32)
    mid = r1.astype(BF16)
    lo = (r1 - mid.astype(F32)).astype(BF16)
    return hi, mid, lo


def _dot(a, b, ca=1, cb=0):
    return lax.dot_general(a, b, (((ca,), (cb,)), ((), ())), preferred_element_type=F32)


def _dot_exact_lhs(m, x):
    hi, mid, lo = _split3(x)
    return _dot(m, hi) + _dot(m, mid) + _dot(m, lo)


def _dot_exact_rhs(x, m):
    hi, mid, lo = _split3(x)
    return _dot(hi, m) + _dot(mid, m) + _dot(lo, m)


def _mm(terms, *, name, ta=False, tb=False, out_dtype=F32, residual=None, alpha=1.0, tm=512, tn=512):
    nt = len(terms)
    a0, b0 = terms[0]
    m = a0.shape[1] if ta else a0.shape[0]
    n = b0.shape[0] if tb else b0.shape[1]
    tm, tn = min(tm, m), min(tn, n)
    assert m % tm == 0 and n % tn == 0, (name, m, n, tm, tn)
    has_res = residual is not None

    def body(*refs):
        o_ref = refs[-1]
        acc = None
        for i in range(nt):
            a = _bf(refs[2 * i][...])
            b = _bf(refs[2 * i + 1][...])
            p = _dot(a, b, 0 if ta else 1, 1 if tb else 0)
            acc = p if acc is None else acc + p
        if alpha != 1.0:
            acc = acc * alpha
        if has_res:
            acc = acc + refs[2 * nt][...]
        o_ref[...] = acc.astype(out_dtype)

    in_specs, args = [], []
    for a, b in terms:
        k = a.shape[0] if ta else a.shape[1]
        assert (b.shape[1] if tb else b.shape[0]) == k, (name, a.shape, b.shape)
        in_specs.append(pl.BlockSpec((k, tm), lambda i, j: (0, i)) if ta else pl.BlockSpec((tm, k), lambda i, j: (i, 0)))
        in_specs.append(pl.BlockSpec((tn, k), lambda i, j: (j, 0)) if tb else pl.BlockSpec((k, tn), lambda i, j: (0, j)))
        args += [a, b]
    if has_res:
        in_specs.append(pl.BlockSpec((tm, tn), lambda i, j: (i, j)))
        args.append(residual)
    return pl.pallas_call(
        body, name=name, grid=(m // tm, n // tn), in_specs=in_specs,
        out_specs=pl.BlockSpec((tm, tn), lambda i, j: (i, j)),
        out_shape=jax.ShapeDtypeStruct((m, n), out_dtype), compiler_params=_cp())(*args)


def _rmsnorm_fwd(x, gain, *, name, tm=512):
    t, d = x.shape
    tm = min(tm, t)

    def body(x_ref, g_ref, o_ref):
        xv = x_ref[...]
        rstd = lax.rsqrt(jnp.mean(xv * xv, axis=-1, keepdims=True) + RMS_EPS)
        o_ref[...] = (xv * rstd * g_ref[...]).astype(BF16)

    return pl.pallas_call(
        body, name=name, grid=(t // tm,),
        in_specs=[pl.BlockSpec((tm, d), lambda i: (i, 0)), pl.BlockSpec((1, d), lambda i: (0, 0))],
        out_specs=pl.BlockSpec((tm, d), lambda i: (i, 0)),
        out_shape=jax.ShapeDtypeStruct((t, d), BF16), compiler_params=_cp())(x, gain)


def _rmsnorm_bwd(x, gain, dxn, dres, *, name, scale, tm=512):
    t, d = x.shape
    tm = min(tm, t)

    def body(x_ref, g_ref, dxn_ref, dres_ref, dx_ref, dxb_ref, dg_ref):
        xv = x_ref[...]
        rstd = lax.rsqrt(jnp.mean(xv * xv, axis=-1, keepdims=True) + RMS_EPS)
        xhat = xv * rstd
        dxn_v = dxn_ref[...]
        dxhat = dxn_v * g_ref[...]
        dx = dres_ref[...] + rstd * (dxhat - xhat * jnp.mean(dxhat * xhat, axis=-1, keepdims=True))
        dx_ref[...] = dx
        dxb_ref[...] = (dx * scale).astype(BF16)

        @pl.when(pl.program_id(0) == 0)
        def _():
            dg_ref[...] = jnp.zeros_like(dg_ref)

        dg_ref[...] += jnp.sum(dxn_v * xhat, axis=0, keepdims=True)

    row = pl.BlockSpec((tm, d), lambda i: (i, 0))
    vec = pl.BlockSpec((1, d), lambda i: (0, 0))
    return pl.pallas_call(
        body, name=name, grid=(t // tm,), in_specs=[row, vec, row, row], out_specs=[row, row, vec],
        out_shape=[jax.ShapeDtypeStruct((t, d), F32), jax.ShapeDtypeStruct((t, d), BF16), jax.ShapeDtypeStruct((1, d), F32)],
        compiler_params=_cp())(x, gain, dxn, dres)


def _loss_head(h, gain, target, *, name, tm=512):
    t, d = h.shape
    tm = min(tm, t)

    def body(h_ref, g_ref, t_ref, dh_ref, dhb_ref, dg_ref, loss_ref):
        hv = h_ref[...]
        rstd = lax.rsqrt(jnp.mean(hv * hv, axis=-1, keepdims=True) + RMS_EPS)
        xhat = hv * rstd
        err = xhat * g_ref[...] - t_ref[...]
        dy = err * (1.0 / d)
        dxhat = dy * g_ref[...]
        dh = rstd * (dxhat - xhat * jnp.mean(dxhat * xhat, axis=-1, keepdims=True))
        dh_ref[...] = dh
        dhb_ref[...] = (dh * MACARON).astype(BF16)

        @pl.when(pl.program_id(0) == 0)
        def _():
            dg_ref[...] = jnp.zeros_like(dg_ref)
            loss_ref[...] = jnp.zeros_like(loss_ref)

        dg_ref[...] += jnp.sum(dy * xhat, axis=0, keepdims=True)
        part = jnp.sum(jnp.sum(err * err, axis=-1, keepdims=True), axis=0, keepdims=True) * (0.5 / d)
        loss_ref[...] += jnp.broadcast_to(part, loss_ref.shape)

    row = pl.BlockSpec((tm, d), lambda i: (i, 0))
    vec = pl.BlockSpec((1, d), lambda i: (0, 0))
    return pl.pallas_call(
        body, name=name, grid=(t // tm,), in_specs=[row, vec, row],
        out_specs=[row, row, vec, pl.BlockSpec((1, LANES), lambda i: (0, 0))],
        out_shape=[jax.ShapeDtypeStruct((t, d), F32), jax.ShapeDtypeStruct((t, d), BF16), jax.ShapeDtypeStruct((1, d), F32),
                   jax.ShapeDtypeStruct((1, LANES), F32)],
        compiler_params=_cp())(h, gain, target)


def _norm_gate_up(x, gain, wg, wu, *, name, tm=512, tf=1408, pack=None, seg_rows=()):
    t, d = x.shape
    f = wg.shape[0]
    tm, tf = min(tm, t), min(tf, f)
    assert f % tf == 0
    ni, nj = t // tm, f // tf
    nseg = len(seg_rows)

    def body(x_ref, g_ref, wg_ref, wu_ref, *rest):
        if pack is not None:
            pack_ref, xn_ref, gg_ref, uu_ref, act_ref = rest[:5]
            start, forward, finish = _gather_phases(pack_ref, rest[5:5 + nseg], seg_rows, *rest[5 + nseg:])
            step = pl.program_id(0) * nj + pl.program_id(1)
            pl.when(step == 0)(start)
            pl.when(step == (3 * ni * nj) // 4)(forward)
        else:
            xn_ref, gg_ref, uu_ref, act_ref = rest

        @pl.when(pl.program_id(1) == 0)
        def _():
            xv = x_ref[...]
            rstd = lax.rsqrt(jnp.mean(xv * xv, axis=-1, keepdims=True) + RMS_EPS)
            xn_ref[...] = (xv * rstd * g_ref[...]).astype(BF16)

        xn = xn_ref[...]
        gv = _dot(xn, wg_ref[...], 1, 1)
        uv = _dot(xn, wu_ref[...], 1, 1)
        gg_ref[...] = gv.astype(BF16)
        uu_ref[...] = uv.astype(BF16)
        act_ref[...] = (gv * _sigmoid(gv) * uv).astype(BF16)
        if pack is not None:
            pl.when(step == ni * nj - 1)(finish)

    row = pl.BlockSpec((tm, d), lambda i, j: (i, 0))
    wsp = pl.BlockSpec((tf, d), lambda i, j: (j, 0))
    osp = pl.BlockSpec((tm, tf), lambda i, j: (i, j))
    fused = pack is not None
    return pl.pallas_call(
        body, name=name, grid=(ni, nj),
        in_specs=[row, pl.BlockSpec((1, d), lambda i, j: (0, 0)), wsp, wsp] + ([HBM_SPEC] if fused else []),
        out_specs=[row, osp, osp, osp] + [HBM_SPEC] * nseg,
        out_shape=[jax.ShapeDtypeStruct((t, d), BF16)] + [jax.ShapeDtypeStruct((t, f), BF16)] * 3
        + [jax.ShapeDtypeStruct((8, n, d), BF16) for n in seg_rows],
        scratch_shapes=_gather_scratch() if fused else [],
        compiler_params=_cp(dimension_semantics=("arbitrary", "arbitrary")))(x, gain, wg, wu, *([pack] if fused else []))


def _swiglu_bwd(dout, wd, gg, uu, chip_part=None, *, name, tm=512, tf=1408):
    t, d = dout.shape
    f = wd.shape[0]
    tm, tf = min(tm, t), min(tf, f)
    nj, ni = f // tf, t // tm
    fused = chip_part is not None

    def body(do_ref, wd_ref, g_ref, u_ref, *rest):
        if fused:
            part_ref, dg_ref, du_ref, parts_ref = rest[:4]
            start, finish = _chip_exchange_phases(part_ref, parts_ref, *rest[4:])
            step = pl.program_id(0) * ni + pl.program_id(1)
            pl.when(step == 0)(start)
        else:
            dg_ref, du_ref = rest
        dact = _dot(do_ref[...], wd_ref[...], 1, 1)
        gv = g_ref[...].astype(F32)
        uv = u_ref[...].astype(F32)
        sg = _sigmoid(gv)
        dg_ref[...] = (dact * uv * (sg * (1.0 + gv * (1.0 - sg)))).astype(BF16)
        du_ref[...] = (dact * (gv * sg)).astype(BF16)
        if fused:
            pl.when(step == nj * ni - 1)(finish)

    osp = pl.BlockSpec((tm, tf), lambda j, i: (i, j))
    return pl.pallas_call(
        body, name=name, grid=(nj, ni),
        in_specs=[pl.BlockSpec((tm, d), lambda j, i: (i, 0)), pl.BlockSpec((tf, d), lambda j, i: (j, 0)), osp, osp]
        + ([HBM_SPEC] if fused else []),
        out_specs=[osp, osp] + ([HBM_SPEC] if fused else []),
        out_shape=[jax.ShapeDtypeStruct((t, f), BF16)] * 2
        + ([jax.ShapeDtypeStruct(chip_part.shape, chip_part.dtype)] if fused else []),
        scratch_shapes=_chip_exchange_scratch() if fused else [],
        compiler_params=_cp(dimension_semantics=("arbitrary", "arbitrary")))(dout, wd, gg, uu, *([chip_part] if fused else []))


def _shift_down(x, n):
    rows = lax.broadcasted_iota(jnp.int32, x.shape, 0)
    return jnp.where(rows >= n, pltpu.roll(x, n, 0), 0.0)


def _shift_up(x, n):
    t = x.shape[0]
    rows = lax.broadcasted_iota(jnp.int32, x.shape, 0)
    return jnp.where(rows < t - n, pltpu.roll(x, t - n, 0), 0.0)


def _conv_fwd(pa, conv_w, *, name):
    t = pa.shape[0]
    nb = pa.shape[1] // 3 // LANES

    def body(b_ref, c_ref, x_ref, w_ref, y_ref):
        u = c_ref[...] * x_ref[...]
        w = w_ref[...]
        conv = w[2:3, :] * u + w[1:2, :] * _shift_down(u, 1) + w[0:1, :] * _shift_down(u, 2)
        y_ref[...] = (b_ref[...] * conv).astype(BF16)

    def col(off):
        return pl.BlockSpec((t, LANES), lambda j: (0, off + j))

    return pl.pallas_call(
        body, name=name, grid=(nb,),
        in_specs=[col(0), col(nb), col(2 * nb), pl.BlockSpec((3, LANES), lambda j: (0, j))],
        out_specs=pl.BlockSpec((t, LANES), lambda j: (0, j)),
        out_shape=jax.ShapeDtypeStruct((t, nb * LANES), BF16), compiler_params=_cp())(pa, pa, pa, conv_w)


def _conv_bwd(pa, dy, conv_w, *, name):
    t = pa.shape[0]
    nb = pa.shape[1] // 3 // LANES

    def body(b_ref, c_ref, x_ref, dy_ref, w_ref, db_ref, dc_ref, dx_ref, dw_ref):
        cv, xv = c_ref[...], x_ref[...]
        u = cv * xv
        u1, u2 = _shift_down(u, 1), _shift_down(u, 2)
        w = w_ref[...]
        conv = w[2:3, :] * u + w[1:2, :] * u1 + w[0:1, :] * u2
        dyv = dy_ref[...]
        db_ref[...] = (dyv * conv).astype(BF16)
        dconv = dyv * b_ref[...]
        du = w[2:3, :] * dconv + w[1:2, :] * _shift_up(dconv, 1) + w[0:1, :] * _shift_up(dconv, 2)
        dc_ref[...] = (du * xv).astype(BF16)
        dx_ref[...] = (du * cv).astype(BF16)
        dw_ref[0:1, :] = jnp.sum(dconv * u2, axis=0, keepdims=True)
        dw_ref[1:2, :] = jnp.sum(dconv * u1, axis=0, keepdims=True)
        dw_ref[2:3, :] = jnp.sum(dconv * u, axis=0, keepdims=True)

    def col(off):
        return pl.BlockSpec((t, LANES), lambda j: (0, off + j))

    osp = pl.BlockSpec((t, LANES), lambda j: (0, j))
    wsp = pl.BlockSpec((3, LANES), lambda j: (0, j))
    return pl.pallas_call(
        body, name=name, grid=(nb,), in_specs=[col(0), col(nb), col(2 * nb), col(0), wsp],
        out_specs=[osp, osp, osp, wsp],
        out_shape=[jax.ShapeDtypeStruct((t, nb * LANES), BF16)] * 3 + [jax.ShapeDtypeStruct((3, nb * LANES), F32)],
        compiler_params=_cp())(pa, pa, pa, dy, conv_w)


def _sb_consts():
    j = lax.broadcasted_iota(jnp.int32, (SB_KEYS, SB_KEYS), 0)
    s = lax.broadcasted_iota(jnp.int32, (SB_KEYS, SB_KEYS), 1)
    after = (j > s).astype(BF16)
    upto = (j <= s).astype(BF16)
    before = (j < s).astype(BF16)
    return after, jnp.stack([upto, before])


def _log_sigmoid(z):
    return jnp.minimum(z, 0.0) - jnp.log(1.0 + jnp.exp(-jnp.abs(z)))


def _attn_fwd(pb, late_pack, seg_rows, *, name, tq=256):
    t = pb.shape[0]
    npair = pb.shape[1] // 3 // LANES
    tq = min(tq, t)
    nq = t // tq
    cmat, _ = _sb_consts()
    scale = 1.0 / math.sqrt(LANES // 2)

    nseg = len(seg_rows)

    def body(q_ref, k_ref, v_ref, c_ref, late_ref, y_ref, lt_ref, *rest):
        i = pl.program_id(1)
        pair = pl.program_id(0)
        scratch = rest[nseg:nseg + 4]
        start, forward, finish = _gather_phases(late_ref, rest[:nseg], seg_rows, *rest[nseg + 4:])
        pl.when((pair == 0) & (i == 0))(start)
        pl.when((pair == npair - 1) & (i == nq // 2))(forward)
        lane = lax.broadcasted_iota(jnp.int32, (tq, LANES), 1)
        rowpos = i * tq + lax.broadcasted_iota(jnp.int32, (tq, SB_KEYS), 0)
        colid = lax.broadcasted_iota(jnp.int32, (tq, SB_KEYS), 1)
        q2 = q_ref[...] * jnp.asarray(scale, BF16)
        cm = c_ref[...]
        hi_lanes = lane >= LANES // 2
        qhs = [jnp.where(hi_lanes == (hh == 1), q2, jnp.zeros_like(q2)) for hh in range(2)]
        per_q = tq // SB_KEYS

        def blk(jb):
            return pl.ds(pl.multiple_of(jb * SB_KEYS, SB_KEYS), SB_KEYS)

        zbuf, wbuf, accbuf, runbuf = scratch

        def scores(jb):
            kb = k_ref[blk(jb), :]
            for hh in range(2):
                zbuf[hh] = _dot(qhs[hh], kb, 1, 1)

        def values(jb):
            vb = v_ref[blk(jb), :]
            for hh in range(2):
                accbuf[hh] += _dot(wbuf[hh], vb)

        def trip(jb, masked, first=False):
            mask = (jb * SB_KEYS + colid) < rowpos if masked else None
            if not first:
                values(jb + 1)
            pre, css = [], []
            for hh in range(2):
                z = zbuf[hh]
                lb = _log_sigmoid(z)
                lk = lb - z
                if masked:
                    lk = jnp.where(mask, lk, 0.0)
                lk_hi, lk_lo = _split2(lk)
                css.append(_dot(lk_hi, cm) + _dot(lk_lo, cm))
                run = runbuf[hh]
                pre.append(lb + run)
                runbuf[hh] = run + jnp.sum(lk, axis=1, keepdims=True)
            scores(jnp.maximum(jb - 1, 0))
            for hh in range(2):
                w = jnp.exp(pre[hh] + css[hh])
                if masked:
                    w = jnp.where(mask, w, 0.0)
                wbuf[hh] = w.astype(BF16)

        nfull = i * per_q
        accbuf[...] = jnp.zeros_like(accbuf)
        runbuf[...] = jnp.zeros_like(runbuf)
        scores(nfull + per_q - 1)
        for dblk in reversed(range(per_q)):
            trip(nfull + dblk, True, first=dblk == per_q - 1)

        def full_block(n, carry):
            trip(nfull - 1 - n, False)
            return carry

        lax.fori_loop(0, nfull, full_block, 0)
        values(0)
        y_ref[...] = jnp.where(hi_lanes, accbuf[1], accbuf[0]).astype(BF16)
        lt_ref[...] = jnp.where(hi_lanes, runbuf[1], runbuf[0])
        pl.when((pair == npair - 1) & (i == nq - 1))(finish)

    return pl.pallas_call(
        body, name=name, grid=(npair, nq),
        in_specs=[pl.BlockSpec((tq, LANES), lambda p, i: (i, p)),
                  pl.BlockSpec((t, LANES), lambda p, i: (0, npair + p)),
                  pl.BlockSpec((t, LANES), lambda p, i: (0, 2 * npair + p)),
                  pl.BlockSpec((SB_KEYS, SB_KEYS), lambda p, i: (0, 0)),
                  HBM_SPEC],
        out_specs=[pl.BlockSpec((tq, LANES), lambda p, i: (i, p))] * 2 + [HBM_SPEC] * nseg,
        out_shape=[jax.ShapeDtypeStruct((t, npair * LANES), BF16), jax.ShapeDtypeStruct((t, npair * LANES), F32),
                   ] + [jax.ShapeDtypeStruct((8, n, late_pack.shape[1]), late_pack.dtype) for n in seg_rows],
        scratch_shapes=[pltpu.VMEM((2, tq, SB_KEYS), F32), pltpu.VMEM((2, tq, SB_KEYS), BF16),
                        pltpu.VMEM((2, tq, LANES), F32), pltpu.VMEM((2, tq, 1), F32)] + _gather_scratch(),
        compiler_params=_cp(dimension_semantics=("arbitrary", "arbitrary")))(pb, pb, pb, cmat, late_pack)


def _attn_bwd(pb, dy, ltot, chip_part, *, name, tq=256):
    t = pb.shape[0]
    npair = pb.shape[1] // 3 // LANES
    tq = min(tq, t)
    nq = t // tq
    _, cmats = _sb_consts()
    scale = 1.0 / math.sqrt(LANES // 2)

    def body(q_ref, k_ref, v_ref, dy_ref, lt_ref, c_ref, part_ref, dq_ref, dk_ref, dv_ref, parts_ref, dk_acc, dv_acc, *rest):
        i = pl.program_id(1)
        pair = pl.program_id(0)
        scratch = rest[:6]
        start, finish = _chip_exchange_phases(part_ref, parts_ref, *rest[6:])
        pl.when((pair == 0) & (i == 0))(start)

        @pl.when(i == 0)
        def _():
            dk_acc[...] = jnp.zeros_like(dk_acc)
            dv_acc[...] = jnp.zeros_like(dv_acc)

        lane = lax.broadcasted_iota(jnp.int32, (tq, LANES), 1)
        rowpos = i * tq + lax.broadcasted_iota(jnp.int32, (tq, SB_KEYS), 0)
        colid = lax.broadcasted_iota(jnp.int32, (tq, SB_KEYS), 1)
        q2 = q_ref[...] * jnp.asarray(scale, BF16)
        do2 = dy_ref[...].astype(BF16)
        ltv = lt_ref[...]
        c_upto, c_before = c_ref[0], c_ref[1]
        hi_lanes = lane >= LANES // 2
        sels = [hi_lanes == (hh == 1) for hh in range(2)]
        qhs = [jnp.where(s, q2, jnp.zeros_like(q2)) for s in sels]
        dohs = [jnp.where(s, do2, jnp.zeros_like(do2)) for s in sels]
        lts = [ltv[:, 0:1], ltv[:, LANES // 2:LANES // 2 + 1]]
        per_q = tq // SB_KEYS

        def blk(jb):
            return pl.ds(pl.multiple_of(jb * SB_KEYS, SB_KEYS), SB_KEYS)

        zbuf, dabuf, dzbuf, abuf, dqbuf, sumbuf = scratch

        def scores(jb):
            kb, vb = k_ref[blk(jb), :], v_ref[blk(jb), :]
            for hh in range(2):
                zbuf[hh] = _dot(qhs[hh], kb, 1, 1)
                dabuf[hh] = _dot(dohs[hh], vb, 1, 1)

        def products(jb):
            kb = k_ref[blk(jb), :]
            dk_acc[blk(jb), :] += _dot(dzbuf[0], qhs[0], 0, 0) + _dot(dzbuf[1], qhs[1], 0, 0)
            dv_acc[blk(jb), :] += _dot(abuf[0], dohs[0], 0, 0) + _dot(abuf[1], dohs[1], 0, 0)
            for hh in range(2):
                dqbuf[hh] += _dot(dzbuf[hh], kb)

        def trip(jb, masked):
            mask = (jb * SB_KEYS + colid) < rowpos if masked else None
            products(jnp.maximum(jb - 1, 0))
            lbs, css, es, ces = [], [], [], []
            for hh in range(2):
                z = zbuf[hh]
                lb = _log_sigmoid(z)
                lk = lb - z
                if masked:
                    lk = jnp.where(mask, lk, 0.0)
                lk_hi, lk_lo = _split2(lk)
                css.append(_dot(lk_hi, c_upto) + _dot(lk_lo, c_upto))
                csum = sumbuf[2 * hh]
                lbs.append((lb, lb + (lts[hh] - csum)))
                sumbuf[2 * hh] = csum + jnp.sum(lk, axis=1, keepdims=True)
            for hh in range(2):
                a = jnp.exp(lbs[hh][1] - css[hh])
                if masked:
                    a = jnp.where(mask, a, 0.0)
                e = a * dabuf[hh]
                e_hi, e_lo = _split2(e)
                ces.append(_dot(e_hi, c_before) + _dot(e_lo, c_before))
                abuf[hh] = a.astype(BF16)
                es.append(e)
            scores(jnp.minimum(jb + 1, last))
            for hh in range(2):
                prun = sumbuf[2 * hh + 1]
                beta = jnp.exp(lbs[hh][0])
                dz = es[hh] * (1.0 - beta) - (prun + ces[hh]) * beta
                if masked:
                    dz = jnp.where(mask, dz, 0.0)
                dzbuf[hh] = dz.astype(BF16)
                sumbuf[2 * hh + 1] = prun + jnp.sum(es[hh], axis=1, keepdims=True)

        nfull = i * per_q
        last = nfull + per_q - 1
        for buf in (dzbuf, abuf, dqbuf, sumbuf):
            buf[...] = jnp.zeros_like(buf)
        scores(0)

        def full_block(jb, carry):
            trip(jb, False)
            return carry

        lax.fori_loop(0, nfull, full_block, 0)
        for dblk in range(per_q):
            trip(nfull + dblk, True)
        products(last)
        dq_ref[...] = (jnp.where(hi_lanes, dqbuf[1], dqbuf[0]) * scale).astype(BF16)

        @pl.when(i == nq - 1)
        def _():
            dk_ref[...] = dk_acc[...].astype(BF16)
            dv_ref[...] = dv_acc[...].astype(BF16)

        pl.when((pair == npair - 1) & (i == nq - 1))(finish)

    blk = pl.BlockSpec((tq, LANES), lambda p, i: (i, p))
    full = pl.BlockSpec((t, LANES), lambda p, i: (0, p))
    return pl.pallas_call(
        body, name=name, grid=(npair, nq),
        in_specs=[blk,
                  pl.BlockSpec((t, LANES), lambda p, i: (0, npair + p)),
                  pl.BlockSpec((t, LANES), lambda p, i: (0, 2 * npair + p)),
                  pl.BlockSpec((tq, LANES), lambda p, i: (i, npair + p)),
                  blk,
                  pl.BlockSpec((2, SB_KEYS, SB_KEYS), lambda p, i: (0, 0, 0)),
                  HBM_SPEC],
        out_specs=[blk, full, full, HBM_SPEC],
        out_shape=[jax.ShapeDtypeStruct((t, npair * LANES), BF16)] * 3 + [jax.ShapeDtypeStruct(chip_part.shape, chip_part.dtype)],
        scratch_shapes=[pltpu.VMEM((t, LANES), F32), pltpu.VMEM((t, LANES), F32),
                        pltpu.VMEM((2, tq, SB_KEYS), F32), pltpu.VMEM((2, tq, SB_KEYS), F32),
                        pltpu.VMEM((2, tq, SB_KEYS), BF16), pltpu.VMEM((2, tq, SB_KEYS), BF16),
                        pltpu.VMEM((2, tq, LANES), F32), pltpu.VMEM((4, tq, 1), F32)] + _chip_exchange_scratch(),
        compiler_params=_cp(dimension_semantics=("arbitrary", "arbitrary")))(pb, pb, pb, dy, ltot, cmats, chip_part)


def _hgrn_consts():
    t = lax.broadcasted_iota(jnp.int32, (CHUNK, CHUNK), 0)
    s = lax.broadcasted_iota(jnp.int32, (CHUNK, CHUNK), 1)
    masks = []
    for lvl in range(N_LEVELS):
        half = CHUNK >> (lvl + 1)
        same = (t // (2 * half)) == (s // (2 * half))
        masks.append((same & (t % (2 * half) >= half) & (s % (2 * half) < half)).astype(F32))
    masks.append((t == s).astype(F32))
    prefix = (s <= t).astype(BF16)
    suffix = (s >= t).astype(BF16)
    return prefix, jnp.stack(masks), suffix


def _hgrn_gates(qr, fr, lbv):
    sg = _sigmoid(fr)
    fval = lbv + (1.0 - lbv) * sg
    kk = (1.0 - lbv) * _sigmoid(-fr)
    sq = _sigmoid(qr)
    return sg, fval, jnp.log(fval), kk, sq, qr * sq


def _lower_bound(c_ref):
    c = c_ref[...]
    mx = jnp.max(c, axis=0, keepdims=True)
    ex = jnp.exp(c - mx)
    return ex[1:2, :] / jnp.sum(ex, axis=0, keepdims=True)


def _level_ref(b, lvl):
    half = CHUNK >> (lvl + 1)
    seg = 2 * half
    if seg >= 8:
        b3 = b.reshape(CHUNK // seg, seg, LANES)
        return jnp.broadcast_to(b3[:, half - 1:half, :], b3.shape).reshape(CHUNK, LANES)
    pos = lax.broadcasted_iota(jnp.int32, b.shape, 0) % seg
    out = b
    for p in range(seg):
        if p != half - 1:
            out = jnp.where(pos == p, pltpu.roll(b, (p - (half - 1)) % CHUNK, 0), out)
    return out


def _hgrn_levels(b, qs, kk):
    out = []
    for lvl in range(N_LEVELS):
        fac = jnp.exp(-jnp.abs(b - _level_ref(b, lvl)))
        out.append((qs * fac, kk * fac, fac, fac))
    out.append((qs, kk, None, None))
    return out


def _split2(x):
    hi = x.astype(BF16)
    return hi, (x - hi.astype(F32)).astype(BF16)


def _hgrn_fwd(pc, c_lb, out_norm, *, name, tc=512):
    t = pc.shape[0]
    nh = pc.shape[1] // 4 // LANES
    tc = min(tc, t)
    nch = tc // CHUNK
    cum_all, masks, _ = _hgrn_consts()

    def body(q_ref, f_ref, i_ref, g_ref, lb_ref, on_ref, cum_ref, m_ref, y_ref, o_ref, st_ref, state):
        @pl.when(pl.program_id(1) == 0)
        def _():
            state[...] = jnp.zeros_like(state)

        lbv = _lower_bound(lb_ref)
        onv = on_ref[...]

        def chunk(c, carry):
            rows = pl.ds(pl.multiple_of(c * CHUNK, CHUNK), CHUNK)
            for hh in range(HGRN_HEADS):
                lanes = slice(hh * LANES, (hh + 1) * LANES)
                _, _, g, kk, _, qs = _hgrn_gates(q_ref[rows, lanes], f_ref[rows, lanes], lbv[:, lanes])
                vb = i_ref[rows, lanes].astype(BF16)
                b = _dot_exact_lhs(cum_ref[...], g)
                scores = jnp.zeros((CHUNK, CHUNK), F32)
                for lvl, (ql, kl, _, _) in enumerate(_hgrn_levels(b, qs, kk)):
                    scores = scores + _dot(ql.astype(BF16), kl.astype(BF16), 1, 1) * m_ref[lvl]
                st = state[hh]
                st_ref[hh, c] = st
                o = _dot(scores.astype(BF16), vb) + _dot((qs * jnp.exp(b)).astype(BF16), st.astype(BF16), 1, 1)
                blast = b[CHUNK - 1:CHUNK, :]
                kdec = (kk * jnp.exp(blast - b)).astype(BF16)
                state[hh] = st * jnp.exp(blast) + _dot(vb, kdec, 0, 0)
                o_ref[rows, lanes] = o
                rstd = lax.rsqrt(jnp.mean(o * o, axis=-1, keepdims=True) + RMS_EPS)
                gate = g_ref[rows, lanes]
                y_ref[rows, lanes] = (o * rstd * onv * (gate * _sigmoid(gate))).astype(BF16)
            return carry

        lax.fori_loop(0, nch, chunk, 0, unroll=2)

    hw = HGRN_HEADS * LANES

    def col(off):
        return pl.BlockSpec((tc, hw), lambda h, i: (i, off // HGRN_HEADS + h))

    osp = pl.BlockSpec((tc, hw), lambda h, i: (i, h))
    return pl.pallas_call(
        body, name=name, grid=(nh // HGRN_HEADS, t // tc),
        in_specs=[col(0), col(nh), col(2 * nh), col(3 * nh),
                  pl.BlockSpec((2, hw), lambda h, i: (0, h)),
                  pl.BlockSpec((1, LANES), lambda h, i: (0, 0)),
                  pl.BlockSpec(cum_all.shape, lambda h, i: (0, 0)),
                  pl.BlockSpec(masks.shape, lambda h, i: (0, 0, 0))],
        out_specs=[osp, osp, pl.BlockSpec((HGRN_HEADS, nch, LANES, LANES), lambda h, i: (h, i, 0, 0))],
        out_shape=[jax.ShapeDtypeStruct((t, nh * LANES), BF16), jax.ShapeDtypeStruct((t, nh * LANES), F32),
                   jax.ShapeDtypeStruct((nh, t // CHUNK, LANES, LANES), F32)],
        scratch_shapes=[pltpu.VMEM((HGRN_HEADS, LANES, LANES), F32)],
        compiler_params=_cp())(pc, pc, pc, pc, c_lb, out_norm, cum_all, masks)


def _hgrn_bwd(pc, o_saved, states, dy, c_lb, out_norm, *, name, tc=512):
    t = pc.shape[0]
    nh = pc.shape[1] // 4 // LANES
    tc = min(tc, t)
    nch = tc // CHUNK
    nt = t // tc
    cum_all, masks, suffix = _hgrn_consts()

    def body(q_ref, f_ref, i_ref, g_ref, o_ref, st_ref, dy_ref, lb_ref, on_ref, cum_ref, m_ref, suf_ref,
             dq_ref, df_ref, di_ref, dg_ref, dlb_ref, don_ref, dstate):
        @pl.when(pl.program_id(1) == 0)
        def _():
            dstate[...] = jnp.zeros_like(dstate)
            dlb_ref[...] = jnp.zeros_like(dlb_ref)
            don_ref[...] = jnp.zeros_like(don_ref)

        lbv = _lower_bound(lb_ref)
        onv = on_ref[...]

        def head(hh, c, rows):
            lanes = slice(hh * LANES, (hh + 1) * LANES)
            qr = q_ref[rows, lanes]
            sg, fval, g, kk, sq, qs = _hgrn_gates(qr, f_ref[rows, lanes], lbv[:, lanes])
            vb = i_ref[rows, lanes].astype(BF16)
            o = o_ref[rows, lanes]
            gate = g_ref[rows, lanes]
            sgt = _sigmoid(gate)
            rstd = lax.rsqrt(jnp.mean(o * o, axis=-1, keepdims=True) + RMS_EPS)
            ohat = o * rstd
            dyv = dy_ref[rows, lanes]
            don = dyv * (gate * sgt)
            dg_ref[rows, lanes] = (dyv * ohat * onv * (sgt * (1.0 + gate * (1.0 - sgt)))).astype(BF16)
            don_ref[:, lanes] += jnp.sum(don * ohat, axis=0, keepdims=True)
            dxhat = don * onv
            dob = (rstd * (dxhat - ohat * jnp.mean(dxhat * ohat, axis=-1, keepdims=True))).astype(BF16)
            b = _dot_exact_lhs(cum_ref[...], g)
            blast = b[CHUNK - 1:CHUNK, :]
            eb = jnp.exp(b)
            edec = jnp.exp(blast - b)
            st32 = st_ref[hh, c]
            st = st32.astype(BF16)
            dst = dstate[hh]
            dstb = dst.astype(BF16)
            da = _dot(dob, vb, 1, 1)
            levels = _hgrn_levels(b, qs, kk)
            scores = jnp.zeros((CHUNK, CHUNK), F32)
            dq = eb * _dot(dob, st)
            dk_inter = edec * _dot(vb, dstb)
            dk = dk_inter
            for lvl, (ql, kl, eq, ek) in enumerate(levels):
                mk = m_ref[lvl]
                (qh, qlo), (kh, klo) = _split2(ql), _split2(kl)
                scores = scores + _dot(qh, kh, 1, 1) * mk
                dal = (da * mk).astype(BF16)
                dql = _dot(dal, kh) + _dot(dal, klo)
                dkl = _dot(dal, qh, 0, 0) + _dot(dal, qlo, 0, 0)
                dq = dq + (dql if eq is None else dql * eq)
                dk = dk + (dkl if ek is None else dkl * ek)
            kdec = (kk * edec).astype(BF16)
            dv = _dot(scores.astype(BF16), dob, 0, 0) + _dot(kdec, dstb, 1, 1)
            dstate[hh] = dst * jnp.exp(blast) + _dot(dob, (qs * eb).astype(BF16), 0, 0)
            db = qs * dq - kk * dk
            last = jnp.sum(kk * dk_inter, axis=0, keepdims=True) + jnp.exp(blast) * jnp.sum(dst * st32, axis=0, keepdims=True)
            dgl = _dot_exact_lhs(suf_ref[...], db) + last
            dfv = dgl / fval - dk
            df_ref[rows, lanes] = (dfv * (1.0 - lbv[:, lanes]) * sg * (1.0 - sg)).astype(BF16)
            dlb_ref[:, lanes] += jnp.sum(dfv * (1.0 - sg), axis=0, keepdims=True)
            dq_ref[rows, lanes] = (dq * (sq * (1.0 + qr * (1.0 - sq)))).astype(BF16)
            di_ref[rows, lanes] = dv.astype(BF16)

        def chunk(n, carry):
            c = nch - 1 - n
            rows = pl.ds(pl.multiple_of(c * CHUNK, CHUNK), CHUNK)
            for hh in range(HGRN_HEADS):
                head(hh, c, rows)
            return carry

        lax.fori_loop(0, nch, chunk, 0, unroll=2)

    hw = HGRN_HEADS * LANES

    def col(off):
        return pl.BlockSpec((tc, hw), lambda h, i: (nt - 1 - i, off // HGRN_HEADS + h))

    osp = pl.BlockSpec((tc, hw), lambda h, i: (nt - 1 - i, h))
    vec = pl.BlockSpec((1, hw), lambda h, i: (0, h))
    return pl.pallas_call(
        body, name=name, grid=(nh // HGRN_HEADS, nt),
        in_specs=[col(0), col(nh), col(2 * nh), col(3 * nh), osp,
                  pl.BlockSpec((HGRN_HEADS, nch, LANES, LANES), lambda h, i: (h, nt - 1 - i, 0, 0)),
                  osp,
                  pl.BlockSpec((2, hw), lambda h, i: (0, h)),
                  pl.BlockSpec((1, LANES), lambda h, i: (0, 0)),
                  pl.BlockSpec(cum_all.shape, lambda h, i: (0, 0)),
                  pl.BlockSpec(masks.shape, lambda h, i: (0, 0, 0)),
                  pl.BlockSpec(suffix.shape, lambda h, i: (0, 0))],
        out_specs=[osp, osp, osp, osp, vec, vec],
        out_shape=[jax.ShapeDtypeStruct((t, nh * LANES), BF16)] * 4 + [jax.ShapeDtypeStruct((1, nh * LANES), F32)] * 2,
        scratch_shapes=[pltpu.VMEM((HGRN_HEADS, LANES, LANES), F32)],
        compiler_params=_cp())(pc, pc, pc, pc, o_saved, states, dy, c_lb, out_norm, cum_all, masks, suffix)


HBM_SPEC = pl.BlockSpec(memory_space=pltpu.HBM)


def _gather_scratch():
    return [pltpu.SemaphoreType.DMA((7,)), pltpu.SemaphoreType.DMA((7,)), pltpu.SemaphoreType.DMA]


def _gather_phases(x_ref, out_refs, seg_rows, send_sems, recv_sems, local_sem):
    x, y, c = lax.axis_index("x"), lax.axis_index("y"), lax.axis_index("c")
    me, sibling = (x, y, c), (x, y, 1 - c)
    chips = [(1 - x, y), (x, 1 - y), (1 - x, 1 - y)]
    offs = [sum(seg_rows[:s]) for s in range(len(seg_rows))]
    assert sum(seg_rows) == x_ref.shape[0]

    def index(px, py, pc):
        return 4 * px + 2 * py + pc

    def copies(k, block, to, own):
        return [pltpu.make_async_remote_copy(
            src_ref=x_ref.at[pl.ds(offs[s], n)] if own else out_refs[s].at[index(*block)],
            dst_ref=out_refs[s].at[index(*block)],
            send_sem=send_sems.at[k], recv_sem=recv_sems.at[k], device_id=to, device_id_type=MESH)
            for s, n in enumerate(seg_rows)]

    def all_bytes(k):
        return pltpu.make_async_remote_copy(src_ref=x_ref, dst_ref=x_ref, send_sem=send_sems.at[k],
                                            recv_sem=recv_sems.at[k], device_id=me, device_id_type=MESH)

    mine = [pltpu.make_async_copy(x_ref.at[pl.ds(offs[s], n)], out_refs[s].at[index(*me)], local_sem)
            for s, n in enumerate(seg_rows)]
    first = copies(0, me, sibling, True)
    for j, chip in enumerate(chips):
        first += copies(1 + j, me, (*chip, c), True)

    def start():
        for cp in mine + first:
            cp.start()

    def forward():
        for j, chip in enumerate(chips):
            all_bytes(1 + j).wait_recv()
            for cp in copies(4 + j, (*chip, c), sibling, False):
                cp.start()

    def finish():
        all_bytes(0).wait_recv()
        for j in range(3):
            all_bytes(4 + j).wait_recv()
        for k in range(7):
            all_bytes(k).wait_send()
        pltpu.make_async_copy(x_ref, x_ref, local_sem).wait()

    return start, forward, finish


def _all_gather(xs, seg_rows=None, *, name):
    segs = [xs.shape[0]] if seg_rows is None else list(seg_rows)

    def body(x_ref, *rest):
        start, forward, finish = _gather_phases(x_ref, rest[:len(segs)], segs, *rest[len(segs):])
        start()
        forward()
        finish()

    outs = pl.pallas_call(
        body, name=name, in_specs=[HBM_SPEC], out_specs=[HBM_SPEC] * len(segs),
        out_shape=[jax.ShapeDtypeStruct((8, n, xs.shape[1]), xs.dtype) for n in segs],
        scratch_shapes=_gather_scratch())(xs)
    return outs[0] if seg_rows is None else outs


def _sibling_exchange(s, *, name):
    def body(s_ref, rb_ref, send_sem, recv_sem):
        x, y, c = lax.axis_index("x"), lax.axis_index("y"), lax.axis_index("c")
        cp = pltpu.make_async_remote_copy(
            src_ref=s_ref.at[:, 1 - c], dst_ref=rb_ref, send_sem=send_sem, recv_sem=recv_sem,
            device_id=(x, y, 1 - c), device_id_type=MESH)
        cp.start()
        cp.wait()

    return pl.pallas_call(
        body, name=name, in_specs=[HBM_SPEC], out_specs=HBM_SPEC,
        out_shape=jax.ShapeDtypeStruct(s.shape[:1] + s.shape[2:], s.dtype),
        scratch_shapes=[pltpu.SemaphoreType.DMA, pltpu.SemaphoreType.DMA])(s)


def _row_tile(n, cap=1024):
    return max(b for b in range(16, cap + 1, 16) if n % b == 0)


def _pair_add(s, rb, core, *, name):
    nchip, _, r, c = s.shape
    tb = _row_tile(r)

    def body(core_ref, a_ref, b_ref, o_ref):
        o_ref[...] = (a_ref[...].astype(F32) + b_ref[...].astype(F32)).astype(BF16)

    blk = pl.BlockSpec((None, tb, c), lambda ch, i, cr: (ch, i, 0))
    return pl.pallas_call(
        body, name=name,
        grid_spec=pltpu.PrefetchScalarGridSpec(
            num_scalar_prefetch=1, grid=(nchip, r // tb),
            in_specs=[pl.BlockSpec((None, None, tb, c), lambda ch, i, cr: (ch, cr[0], i, 0)), blk],
            out_specs=blk),
        out_shape=jax.ShapeDtypeStruct((nchip, r, c), BF16), compiler_params=_cp())(core, s, rb)


def _chip_exchange_scratch():
    return [pltpu.SemaphoreType.DMA((3,)), pltpu.SemaphoreType.DMA((3,)), pltpu.SemaphoreType.DMA]


def _chip_exchange_phases(p_ref, out_ref, send_sems, recv_sems, local_sem):
    x, y, c = lax.axis_index("x"), lax.axis_index("y"), lax.axis_index("c")
    mine = 2 * x + y
    own = pltpu.make_async_copy(p_ref.at[mine], out_ref.at[mine], local_sem)
    copies = [pltpu.make_async_remote_copy(
        src_ref=p_ref.at[2 * tx + ty], dst_ref=out_ref.at[mine],
        send_sem=send_sems.at[k], recv_sem=recv_sems.at[k], device_id=(tx, ty, c), device_id_type=MESH)
        for k, (tx, ty) in enumerate([(1 - x, y), (x, 1 - y), (1 - x, 1 - y)])]

    def start():
        own.start()
        for cp in copies:
            cp.start()

    def finish():
        for cp in copies:
            cp.wait()
        own.wait()

    return start, finish


def _adamw_math(w, g, m, v):
    m2 = ADAM_B1 * m + (1.0 - ADAM_B1) * g
    v2 = ADAM_B2 * v + (1.0 - ADAM_B2) * (g * g)
    m_hat = m2 / (1.0 - ADAM_B1 ** ADAM_STEP)
    v_hat = v2 / (1.0 - ADAM_B2 ** ADAM_STEP)
    return -ADAM_LR * (m_hat / (jnp.sqrt(v_hat) + ADAM_EPS) + ADAM_WD * w), m2, v2


def _grad_sum(parts, *, name):
    _, r, c = parts.shape
    tb = _row_tile(r)

    def body(p0, p1, p2, p3, g_out):
        g_out[...] = ((p0[...].astype(F32) + p1[...].astype(F32)) + p2[...].astype(F32)) + p3[...].astype(F32)

    def part(ch):
        return pl.BlockSpec((None, tb, c), lambda i: (ch, i, 0))

    return pl.pallas_call(
        body, name=name, grid=(r // tb,), in_specs=[part(0), part(1), part(2), part(3)],
        out_specs=pl.BlockSpec((tb, c), lambda i: (i, 0)), out_shape=jax.ShapeDtypeStruct((r, c), F32),
        compiler_params=_cp())(parts, parts, parts, parts)


def _adamw_shard(g, g_off, w, m, v, layer, prev, *, name):
    _, r, c = w.shape
    tb = next(b for b in range(min(r, 512), 0, -8) if r % b == 0 and g_off % b == 0)

    def body(g_ref, w_ref, m_ref, v_ref, *rest):
        d_out, m_out, v_out = rest[-3:]
        d, m2, v2 = _adamw_math(w_ref[...], g_ref[...], m_ref[...], v_ref[...])
        d_out[...] = d
        m_out[...] = m2
        v_out[...] = v2

    blk = pl.BlockSpec((None, tb, c), lambda i: (layer, i, 0))
    prev = list(prev) if prev is not None else []
    return pl.pallas_call(
        body, name=name, grid=(r // tb,),
        in_specs=[pl.BlockSpec((tb, c), lambda i: (g_off // tb + i, 0)), blk, blk, blk] + [pl.BlockSpec(memory_space=pl.ANY)] * len(prev),
        out_specs=[blk] * 3, out_shape=[jax.ShapeDtypeStruct(w.shape, F32)] * 3,
        input_output_aliases={4 + k: k for k in range(len(prev))},
        compiler_params=_cp())(g, w, m, v, *prev)


SLOT = 8
SMALL_ROWS = 6 * SLOT
ROW_LB = 4 * SLOT


def _small_update(gath, w, m, v, *, name):
    def body(g_ref, w_ref, m_ref, v_ref, g_out, d_out, m_out, v_out):
        tot = g_ref[0]
        for k in range(1, 8):
            tot = tot + g_ref[k]
        wv = w_ref[...]
        c0, c1 = wv[ROW_LB:ROW_LB + 1, :], wv[ROW_LB + 1:ROW_LB + 2, :]
        mx = jnp.maximum(c0, c1)
        e0, e1 = jnp.exp(c0 - mx), jnp.exp(c1 - mx)
        lb = e1 / (e0 + e1)
        gl = tot[ROW_LB:ROW_LB + 1, :] * lb * (1.0 - lb)
        row = lax.broadcasted_iota(jnp.int32, tot.shape, 0)
        g = jnp.where(row == ROW_LB, -gl, jnp.where(row == ROW_LB + 1, gl, tot))
        d, m2, v2 = _adamw_math(wv, g, m_ref[...], v_ref[...])
        g_out[...] = g
        d_out[...] = d
        m_out[...] = m2
        v_out[...] = v2

    return pl.pallas_call(
        body, name=name, out_shape=[jax.ShapeDtypeStruct(w.shape, F32)] * 4, compiler_params=_cp())(gath, w, m, v)


D_MODEL = 1024


def _ffn_fwd(h, gain, wg, wu, wd, tag):
    xn, gg, uu, act = _norm_gate_up(h, gain, wg, wu, name=f"{tag}_gate_up")
    out = _mm([(act, wd)], residual=h, alpha=MACARON, tn=1024, name=f"{tag}_down")
    return out, (h, xn, gg, uu, act)


def _ffn_input_bwd(dg, du, wg, wu, x, gain, dres, chip_part, *, name, scale, tm=256):
    t, d = x.shape
    f = wg.shape[0]
    tm = min(tm, t)
    nt = t // tm
    fused = chip_part is not None

    def body(dg_ref, du_ref, wg_ref, wu_ref, x_ref, g_ref, dres_ref, *rest):
        if fused:
            part_ref, dx_ref, dxb_ref, dgain_ref, parts_ref = rest[:5]
            start, finish = _chip_exchange_phases(part_ref, parts_ref, *rest[5:])
            pl.when(pl.program_id(0) == 0)(start)
        else:
            dx_ref, dxb_ref, dgain_ref = rest
        dxn_v = _dot(dg_ref[...], wg_ref[...]) + _dot(du_ref[...], wu_ref[...])
        xv = x_ref[...]
        rstd = lax.rsqrt(jnp.mean(xv * xv, axis=-1, keepdims=True) + RMS_EPS)
        xhat = xv * rstd
        dxhat = dxn_v * g_ref[...]
        dx = dres_ref[...] + rstd * (dxhat - xhat * jnp.mean(dxhat * xhat, axis=-1, keepdims=True))
        dx_ref[...] = dx
        dxb_ref[...] = (dx * scale).astype(BF16)

        @pl.when(pl.program_id(0) == 0)
        def _():
            dgain_ref[...] = jnp.zeros_like(dgain_ref)

        dgain_ref[...] += jnp.sum(dxn_v * xhat, axis=0, keepdims=True)
        if fused:
            pl.when(pl.program_id(0) == nt - 1)(finish)

    wide = pl.BlockSpec((tm, f), lambda i: (i, 0))
    wsp = pl.BlockSpec((f, d), lambda i: (0, 0))
    row = pl.BlockSpec((tm, d), lambda i: (i, 0))
    vec = pl.BlockSpec((1, d), lambda i: (0, 0))
    args = [dg, du, wg, wu, x, gain, dres] + ([chip_part] if fused else [])
    return pl.pallas_call(
        body, name=name, grid=(nt,),
        in_specs=[wide, wide, wsp, wsp, row, vec, row] + ([HBM_SPEC] if fused else []),
        out_specs=[row, row, vec] + ([HBM_SPEC] if fused else []),
        out_shape=[jax.ShapeDtypeStruct((t, d), F32), jax.ShapeDtypeStruct((t, d), BF16), jax.ShapeDtypeStruct((1, d), F32)]
        + ([jax.ShapeDtypeStruct(chip_part.shape, chip_part.dtype)] if fused else []),
        scratch_shapes=_chip_exchange_scratch() if fused else [],
        compiler_params=_cp(dimension_semantics=("arbitrary",)))(*args)


def _ffn_bwd(dout, dout_half, saved, gain, wg, wu, wd, tag, next_scale, make_chip_part=None, early_chip_part=None):
    h, xn, gg, uu, act = saved
    dg, du, *early_parts = _swiglu_bwd(dout_half, wd, gg, uu, early_chip_part, name=f"{tag}_dact")
    dwd = _mm([(act, dout_half)], ta=True, tm=256, tn=1024, out_dtype=BF16, name=f"{tag}_dwd")
    dwg = _mm([(dg, xn)], ta=True, tm=256, tn=1024, out_dtype=BF16, name=f"{tag}_dwg")
    dwu = _mm([(du, xn)], ta=True, tm=256, tn=1024, out_dtype=BF16, name=f"{tag}_dwu")
    chip_part = make_chip_part(dwg, dwu, dwd) if make_chip_part is not None else None
    dh, dh_b, dgain, *parts = _ffn_input_bwd(dg, du, wg, wu, h, gain, dout, chip_part, scale=next_scale,
                                             name=f"{tag}_input_bwd")
    return dh, dh_b, dwg, dwu, dwd, dgain, (parts[0] if parts else None), (early_parts[0] if early_parts else None)


def kernel(x, ffn_pre_norm, ffn_pre_w_gate, ffn_pre_w_up, ffn_pre_w_down, mix_norm, ffn_post_norm, ffn_post_w_gate, ffn_post_w_up, ffn_post_w_down, ab_w_in, ab_conv_w, ab_w_out, c_w_in, c_lower_bounds, c_out_norm, c_w_out, final_norm, loss_target, m_ffn_pre_norm, m_ffn_pre_w_gate, m_ffn_pre_w_up, m_ffn_pre_w_down, m_mix_norm, m_ffn_post_norm, m_ffn_post_w_gate, m_ffn_post_w_up, m_ffn_post_w_down, m_ab_w_in, m_ab_conv_w, m_ab_w_out, m_c_w_in, m_c_lower_bounds, m_c_out_norm, m_c_w_out, m_final_norm, v_ffn_pre_norm, v_ffn_pre_w_gate, v_ffn_pre_w_up, v_ffn_pre_w_down, v_mix_norm, v_ffn_post_norm, v_ffn_post_w_gate, v_ffn_post_w_up, v_ffn_post_w_down, v_ab_w_in, v_ab_conv_w, v_ab_w_out, v_c_w_in, v_c_lower_bounds, v_c_out_norm, v_c_w_out, v_final_norm):
    d = D_MODEL
    h0 = x[0]
    target = loss_target[0]
    core = lax.axis_index("c").astype(jnp.int32).reshape(1)

    big = [("pre_g", ffn_pre_w_gate, m_ffn_pre_w_gate, v_ffn_pre_w_gate),
           ("pre_u", ffn_pre_w_up, m_ffn_pre_w_up, v_ffn_pre_w_up),
           ("pre_d", ffn_pre_w_down, m_ffn_pre_w_down, v_ffn_pre_w_down),
           ("post_g", ffn_post_w_gate, m_ffn_post_w_gate, v_ffn_post_w_gate),
           ("post_u", ffn_post_w_up, m_ffn_post_w_up, v_ffn_post_w_up),
           ("post_d", ffn_post_w_down, m_ffn_post_w_down, v_ffn_post_w_down),
           ("ab_in", ab_w_in, m_ab_w_in, v_ab_w_in),
           ("ab_out", ab_w_out, m_ab_w_out, v_ab_w_out),
           ("c_in", c_w_in, m_c_w_in, v_c_w_in),
           ("c_out", c_w_out, m_c_w_out, v_c_w_out)]
    by_tag = {tag: (w, m, v) for tag, w, m, v in big}

    def layer_rows(tag):
        w = by_tag[tag][0]
        return w.size // d // w.shape[0]

    def layout(items):
        offs, off = {}, 0
        for item in items:
            offs[item] = off
            off += layer_rows(item[0])
        return offs, off

    ffn = [f"{pos}_{kind}" for pos in ("pre", "post") for kind in "gud"]
    first_items = [("pre_g", 0), ("pre_u", 0)]
    early_items = [("pre_d", 0), ("ab_in", 0)]
    late_items = ([("pre_g", 1), ("pre_u", 1), ("pre_d", 1)] + [(f"post_{kind}", l) for l in (0, 1) for kind in "gud"]
                  + [("ab_out", 0), ("c_in", 0), ("c_out", 0)])
    grad_items = {"A": ([(tag, 1) for tag in ffn] + [(f"post_{kind}", 0) for kind in "gud"]
                        + [("c_in", 0), ("c_out", 0), ("ab_out", 0)]),
                  "B": [(f"pre_{kind}", 0) for kind in "gud"], "C": [("ab_in", 0)]}
    grad_offs = {k: layout(items)[0] for k, items in grad_items.items()}
    grad_conv_row = layout(grad_items["C"])[1]

    def conv_rows(a, split):
        flat = a.reshape(-1)
        if split:
            hi = flat.astype(BF16)
            flat = jnp.concatenate([hi, (flat - hi.astype(F32)).astype(BF16)])
        return jnp.zeros((16, d), flat.dtype).at[0, :flat.shape[0]].set(flat)

    nconv = ab_conv_w.size
    col_sharded = {"pre_g", "pre_u", "post_g", "post_u", "ab_in", "c_in"}

    def pack_rows(item):
        tag, layer = item
        a = by_tag[tag][0][layer]
        return (a.T if tag in col_sharded else a).reshape(-1, d).astype(BF16)

    first_pack = jnp.concatenate([pack_rows(item) for item in first_items], axis=0)
    early_pack = jnp.concatenate([pack_rows(item) for item in early_items] + [conv_rows(ab_conv_w, True)], axis=0)
    late_pack = jnp.concatenate([pack_rows(item) for item in late_items], axis=0)
    first_w = _all_gather(first_pack, [layer_rows(tag) for tag, _ in first_items], name="gather_first_weights")
    full = {item: g.reshape(-1, d) for item, g in zip(first_items, first_w)}

    xn0, gg0, uu0, act0, *early_w = _norm_gate_up(
        h0, ffn_pre_norm[0:1], full["pre_g", 0], full["pre_u", 0], name="l0pre_gate_up_gather_early_weights",
        pack=early_pack, seg_rows=[layer_rows(tag) for tag, _ in early_items] + [16])
    full.update({item: g.reshape(-1, d) for item, g in zip(early_items, early_w)})
    ffn_w = {("pre", 0): tuple(full[f"pre_{kind}", 0] for kind in "gud")}
    w_ab_in = full["ab_in", 0]
    cg = early_w[-1][:, 0, :2 * nconv].astype(F32)
    conv_w = (cg[:, :nconv] + cg[:, nconv:]).reshape(8, 3, -1).transpose(1, 0, 2).reshape(3, -1)
    half = w_ab_in.shape[0] // 2
    w_a_in, w_b_in = w_ab_in[:half], w_ab_in[half:]
    aw = half // 3
    h1 = _mm([(act0, full["pre_d", 0])], residual=h0, alpha=MACARON, tn=1024, name="l0pre_down")
    s_pre0 = (h0, xn0, gg0, uu0, act0)
    hn0 = _rmsnorm_fwd(h1, mix_norm[0:1], name="l0_mix_norm")
    pa = _mm([(hn0, w_a_in)], tb=True, tn=1536, name="ab_proj_a")
    pb = _mm([(hn0, w_b_in)], tb=True, tn=1536, out_dtype=BF16, name="ab_proj_b")
    ya = _conv_fwd(pa, conv_w, name="conv_fwd")
    yb, ltot, *late_w = _attn_fwd(pb, late_pack, [layer_rows(tag) for tag, _ in late_items],
                                  name="attn_fwd_gather_late_weights")
    full.update({item: g.reshape(-1, d) for item, g in zip(late_items, late_w)})
    for pos, layer in (("post", 0), ("pre", 1), ("post", 1)):
        ffn_w[pos, layer] = tuple(full[f"{pos}_{kind}", layer] for kind in "gud")
    w_ab_out, w_c_in, w_c_out = full["ab_out", 0], full["c_in", 0], full["c_out", 0]
    h2 = _mm([(ya, w_ab_out[:aw]), (yb, w_ab_out[aw:])], residual=h1, tn=1024, name="ab_out")
    h3, s_post0 = _ffn_fwd(h2, ffn_post_norm[0:1], *ffn_w["post", 0], "l0post")
    h4, s_pre1 = _ffn_fwd(h3, ffn_pre_norm[1:2], *ffn_w["pre", 1], "l1pre")
    hn1 = _rmsnorm_fwd(h4, mix_norm[1:2], name="l1_mix_norm")
    pc = _mm([(hn1, w_c_in)], tb=True, tm=256, tn=4096, name="c_proj")
    yc, o_saved, states = _hgrn_fwd(pc, c_lower_bounds, c_out_norm, name="hgrn_fwd")
    h5 = _mm([(yc, w_c_out)], residual=h4, tn=1024, name="c_out")
    h6, s_post1 = _ffn_fwd(h5, ffn_post_norm[1:2], *ffn_w["post", 1], "l1post")
    dh6, dh6_b, d_final, loss_vec = _loss_head(h6, final_norm.reshape(1, d), target, name="loss_head")

    gw = {}
    dh5, dh5_b, gw["post_g", 1], gw["post_u", 1], gw["post_d", 1], d_post1, *_ = _ffn_bwd(
        dh6, dh6_b, s_post1, ffn_post_norm[1:2], *ffn_w["post", 1], "l1post", 1.0)
    dyc = _mm([(dh5_b, w_c_out)], tb=True, tn=1024, name="c_out_dy")
    g_c_out = _mm([(yc, dh5_b)], ta=True, tm=256, tn=1024, out_dtype=BF16, name="c_out_dw")
    dcq, dcf, dci, dcg, dlb, d_onorm = _hgrn_bwd(pc, o_saved, states, dyc, c_lower_bounds, c_out_norm, name="hgrn_bwd")
    dparts = [dcq, dcf, dci, dcg]
    g_c_in = jnp.concatenate(
        [_mm([(dp, hn1)], ta=True, tm=256, tn=1024, out_dtype=BF16, name=f"c_in_dw{i}") for i, dp in enumerate(dparts)],
        axis=0)
    cw = w_c_in.shape[0] // 4
    dhn1 = _mm([(dp, w_c_in[i * cw:(i + 1) * cw]) for i, dp in enumerate(dparts)], tm=512, tn=1024, name="c_in_dx")
    dh4, dh4_b, d_mix1 = _rmsnorm_bwd(h4, mix_norm[1:2], dhn1, dh5, scale=MACARON, name="l1_mix_norm_bwd")
    dh3, dh3_b, gw["pre_g", 1], gw["pre_u", 1], gw["pre_d", 1], d_pre1, *_ = _ffn_bwd(
        dh4, dh4_b, s_pre1, ffn_pre_norm[1:2], *ffn_w["pre", 1], "l1pre", MACARON)
    dh2, dh2_b, gw["post_g", 0], gw["post_u", 0], gw["post_d", 0], d_post0, *_ = _ffn_bwd(
        dh3, dh3_b, s_post0, ffn_post_norm[0:1], *ffn_w["post", 0], "l0post", 1.0)
    dyab = _mm([(dh2_b, w_ab_out)], tb=True, tn=1024, name="ab_out_dy")
    g_ab_out = jnp.concatenate([_mm([(ya, dh2_b)], ta=True, tm=256, tn=1024, out_dtype=BF16, name="ab_out_dw_a"),
                                _mm([(yb, dh2_b)], ta=True, tm=256, tn=1024, out_dtype=BF16, name="ab_out_dw_b")], axis=0)
    dab, dac, dax, g_conv = _conv_bwd(pa, dyab, conv_w, name="conv_bwd")

    def chip_partials(key, grads, extra=()):
        gpack = jnp.concatenate([grads[item].reshape(8, -1, d) for item in grad_items[key]] + list(extra), axis=1)
        send = gpack.reshape(4, 2, gpack.shape[1], d)
        from_sibling = _sibling_exchange(send, name=f"grad{key}_sibling_exchange")
        return _pair_add(send, from_sibling, core, name=f"grad{key}_pair_add")

    gw["c_in", 0], gw["c_out", 0], gw["ab_out", 0] = g_c_in, g_c_out, g_ab_out
    chip_part_a = chip_partials("A", gw)
    dq, dk, dv, parts_a = _attn_bwd(pb, dyab, ltot, chip_part_a, name="attn_bwd_exchange_grads_a")
    dparts = [dab, dac, dax, dq, dk, dv]
    g_ab_in = jnp.concatenate(
        [_mm([(dp, hn0)], ta=True, tm=256, tn=1024, out_dtype=BF16, name=f"ab_in_dw{i}") for i, dp in enumerate(dparts)],
        axis=0)
    dhn0 = _mm([(dp, w_ab_in[i * aw:(i + 1) * aw]) for i, dp in enumerate(dparts)], tm=512, tn=1024, name="ab_in_dx")
    dh1, dh1_b, d_mix0 = _rmsnorm_bwd(h1, mix_norm[0:1], dhn0, dh2, scale=MACARON, name="l0_mix_norm_bwd")
    gw["ab_in", 0] = g_ab_in
    gconv_own = g_conv.reshape(3, 8, -1).transpose(1, 0, 2).reshape(8, -1)
    conv_piece = jnp.zeros((8, 16, d), F32).at[:, 0, :nconv].set(gconv_own).astype(BF16)

    def chip_part_b(dwg, dwu, dwd):
        gw["pre_g", 0], gw["pre_u", 0], gw["pre_d", 0] = dwg, dwu, dwd
        return chip_partials("B", gw)

    dh0, _, _, _, _, d_pre0, parts_b, parts_c = _ffn_bwd(
        dh1, dh1_b, s_pre0, ffn_pre_norm[0:1], *ffn_w["pre", 0], "l0pre", 1.0, chip_part_b,
        chip_partials("C", gw, [conv_piece]))

    g_sum = {key: _grad_sum(p, name=f"grad{key}_sum") for key, p in (("A", parts_a), ("B", parts_b), ("C", parts_c))}

    upd = {}
    for tag, w, m, v in big:
        nl = layer_rows(tag)
        view = (lambda a: jnp.swapaxes(a, 1, 2)) if tag in col_sharded else (lambda a: a)
        where = {layer: (key, grad_offs[key][tag, layer])
                 for key in grad_items for t2, layer in grad_items[key] if t2 == tag}
        res = None
        for layer in sorted(where):
            key, off = where[layer]
            res = _adamw_shard(g_sum[key], off, view(w), view(m), view(v), layer, res, name=f"adamw_{tag}{layer}")
        g_nat = jnp.stack([g_sum[where[layer][0]][where[layer][1]:where[layer][1] + nl] for layer in sorted(where)])
        upd[tag] = [view(a) for a in [g_nat] + list(res)]
    res = _adamw_shard(g_sum["C"], grad_conv_row, *(conv_rows(a, False)[None] for a in (ab_conv_w, m_ab_conv_w, v_ab_conv_w)),
                       0, None, name="adamw_conv")
    g_conv_rows = g_sum["C"][grad_conv_row:grad_conv_row + 16]
    upd["conv"] = [r[0, :nconv].reshape(ab_conv_w.shape) for r in [g_conv_rows] + [r[0] for r in res]]

    def small_pack(pre, mix, post, final, lbs, onorm):
        def slot(parts):
            out, r = jnp.zeros((SLOT, d), F32), 0
            for a in (parts if isinstance(parts, tuple) else (parts,)):
                out = out.at[r:r + a.shape[0], :a.shape[1]].set(a)
                r += a.shape[0]
            return out

        return jnp.concatenate([slot(pre), slot(mix), slot(post), slot(final.reshape(1, d)), slot(lbs), slot(onorm)], axis=0)

    d_on = d_onorm.reshape(-1, c_out_norm.shape[1]).sum(axis=0, keepdims=True)
    gsmall = small_pack((d_pre0, d_pre1), (d_mix0, d_mix1), (d_post0, d_post1), d_final, dlb, d_on)
    gsmall_all = _all_gather(gsmall, name="gather_small_grads")
    sres = _small_update(
        gsmall_all,
        small_pack(ffn_pre_norm, mix_norm, ffn_post_norm, final_norm, c_lower_bounds, c_out_norm),
        small_pack(m_ffn_pre_norm, m_mix_norm, m_ffn_post_norm, m_final_norm, m_c_lower_bounds, m_c_out_norm),
        small_pack(v_ffn_pre_norm, v_mix_norm, v_ffn_post_norm, v_final_norm, v_c_lower_bounds, v_c_out_norm),
        name="small_update")

    def small_out(r):
        return {"pre_norm": r[0:2], "mix_norm": r[SLOT:SLOT + 2], "post_norm": r[2 * SLOT:2 * SLOT + 2],
                "final": r[3 * SLOT], "lb": r[ROW_LB:ROW_LB + 2], "onorm": r[5 * SLOT:5 * SLOT + 1, :c_out_norm.shape[1]]}

    small = [small_out(r) for r in sres]
    outs = []
    for k in range(4):
        s = small[k]
        outs += [s["pre_norm"], upd["pre_g"][k], upd["pre_u"][k], upd["pre_d"][k], s["mix_norm"], s["post_norm"],
                 upd["post_g"][k], upd["post_u"][k], upd["post_d"][k], upd["ab_in"][k], upd["conv"][k],
                 upd["ab_out"][k], upd["c_in"][k], s["lb"], s["onorm"], upd["c_out"][k], s["final"]]
    loss = lax.psum(loss_vec[0, 0], ("x", "y", "c"))
    return (loss, dh0[None], *outs)
```

```python
import functools
import math

import jax
import jax.numpy as jnp
from jax import lax
from jax.experimental import pallas as pl
from jax.experimental.pallas import tpu as pltpu

F32 = jnp.float32
BF16 = jnp.bfloat16
MESH = pl.DeviceIdType.MESH

RMS_EPS = 1e-6
MACARON = 0.5
LANES = 128
CHUNK = 64
N_LEVELS = 6
HGRN_HEADS = 2
SB_KEYS = 256
ADAM_LR, ADAM_B1, ADAM_B2, ADAM_EPS, ADAM_WD, ADAM_STEP = 0.001, 0.9, 0.999, 1e-08, 0.01, 10
VMEM_LIMIT = 48 * 1024 * 1024


def _cp(**kw):
    return pltpu.CompilerParams(vmem_limit_bytes=VMEM_LIMIT, **kw)


def _sigmoid(x):
    return 0.5 * jnp.tanh(0.5 * x) + 0.5


def _bf(x):
    return x if x.dtype == BF16 else x.astype(BF16)


def _split3(x):
    hi = x.astype(BF16)
    r1 = x - hi.astype(F32)
    mid = r1.astype(BF16)
    lo = (r1 - mid.astype(F32)).astype(BF16)
    return hi, mid, lo


def _dot(a, b, ca=1, cb=0):
    return lax.dot_general(a, b, (((ca,), (cb,)), ((), ())), preferred_element_type=F32)


def _dot_exact_lhs(m, x):
    hi, mid, lo = _split3(x)
    return _dot(m, hi) + _dot(m, mid) + _dot(m, lo)


def _dot_exact_rhs(x, m):
    hi, mid, lo = _split3(x)
    return _dot(hi, m) + _dot(mid, m) + _dot(lo, m)


def _mm(terms, *, name, ta=False, tb=False, out_dtype=F32, residual=None, alpha=1.0, tm=512, tn=512):
    nt = len(terms)
    a0, b0 = terms[0]
    m = a0.shape[1] if ta else a0.shape[0]
    n = b0.shape[0] if tb else b0.shape[1]
    tm, tn = min(tm, m), min(tn, n)
    assert m % tm == 0 and n % tn == 0, (name, m, n, tm, tn)
    has_res = residual is not None

    def body(*refs):
        o_ref = refs[-1]
        acc = None
        for i in range(nt):
            a = _bf(refs[2 * i][...])
            b = _bf(refs[2 * i + 1][...])
            p = _dot(a, b, 0 if ta else 1, 1 if tb else 0)
            acc = p if acc is None else acc + p
        if alpha != 1.0:
            acc = acc * alpha
        if has_res:
            acc = acc + refs[2 * nt][...]
        o_ref[...] = acc.astype(out_dtype)

    in_specs, args = [], []
    for a, b in terms:
        k = a.shape[0] if ta else a.shape[1]
        assert (b.shape[1] if tb else b.shape[0]) == k, (name, a.shape, b.shape)
        in_specs.append(pl.BlockSpec((k, tm), lambda i, j: (0, i)) if ta else pl.BlockSpec((tm, k), lambda i, j: (i, 0)))
        in_specs.append(pl.BlockSpec((tn, k), lambda i, j: (j, 0)) if tb else pl.BlockSpec((k, tn), lambda i, j: (0, j)))
        args += [a, b]
    if has_res:
        in_specs.append(pl.BlockSpec((tm, tn), lambda i, j: (i, j)))
        args.append(residual)
    return pl.pallas_call(
        body, name=name, grid=(m // tm, n // tn), in_specs=in_specs,
        out_specs=pl.BlockSpec((tm, tn), lambda i, j: (i, j)),
        out_shape=jax.ShapeDtypeStruct((m, n), out_dtype), compiler_params=_cp())(*args)


def _rmsnorm_fwd(x, gain, *, name, tm=512):
    t, d = x.shape
    tm = min(tm, t)

    def body(x_ref, g_ref, o_ref):
        xv = x_ref[...]
        rstd = lax.rsqrt(jnp.mean(xv * xv, axis=-1, keepdims=True) + RMS_EPS)
        o_ref[...] = (xv * rstd * g_ref[...]).astype(BF16)

    return pl.pallas_call(
        body, name=name, grid=(t // tm,),
        in_specs=[pl.BlockSpec((tm, d), lambda i: (i, 0)), pl.BlockSpec((1, d), lambda i: (0, 0))],
        out_specs=pl.BlockSpec((tm, d), lambda i: (i, 0)),
        out_shape=jax.ShapeDtypeStruct((t, d), BF16), compiler_params=_cp())(x, gain)


def _rmsnorm_bwd(x, gain, dxn, dres, *, name, scale, tm=512):
    t, d = x.shape
    tm = min(tm, t)

    def body(x_ref, g_ref, dxn_ref, dres_ref, dx_ref, dxb_ref, dg_ref):
        xv = x_ref[...]
        rstd = lax.rsqrt(jnp.mean(xv * xv, axis=-1, keepdims=True) + RMS_EPS)
        xhat = xv * rstd
        dxn_v = dxn_ref[...]
        dxhat = dxn_v * g_ref[...]
        dx = dres_ref[...] + rstd * (dxhat - xhat * jnp.mean(dxhat * xhat, axis=-1, keepdims=True))
        dx_ref[...] = dx
        dxb_ref[...] = (dx * scale).astype(BF16)

        @pl.when(pl.program_id(0) == 0)
        def _():
            dg_ref[...] = jnp.zeros_like(dg_ref)

        dg_ref[...] += jnp.sum(dxn_v * xhat, axis=0, keepdims=True)

    row = pl.BlockSpec((tm, d), lambda i: (i, 0))
    vec = pl.BlockSpec((1, d), lambda i: (0, 0))
    return pl.pallas_call(
        body, name=name, grid=(t // tm,), in_specs=[row, vec, row, row], out_specs=[row, row, vec],
        out_shape=[jax.ShapeDtypeStruct((t, d), F32), jax.ShapeDtypeStruct((t, d), BF16), jax.ShapeDtypeStruct((1, d), F32)],
        compiler_params=_cp())(x, gain, dxn, dres)


def _loss_head(h, gain, target, *, name, tm=512):
    t, d = h.shape
    tm = min(tm, t)

    def body(h_ref, g_ref, t_ref, dh_ref, dhb_ref, dg_ref, loss_ref):
        hv = h_ref[...]
        rstd = lax.rsqrt(jnp.mean(hv * hv, axis=-1, keepdims=True) + RMS_EPS)
        xhat = hv * rstd
        err = xhat * g_ref[...] - t_ref[...]
        dy = err * (1.0 / d)
        dxhat = dy * g_ref[...]
        dh = rstd * (dxhat - xhat * jnp.mean(dxhat * xhat, axis=-1, keepdims=True))
        dh_ref[...] = dh
        dhb_ref[...] = (dh * MACARON).astype(BF16)

        @pl.when(pl.program_id(0) == 0)
        def _():
            dg_ref[...] = jnp.zeros_like(dg_ref)
            loss_ref[...] = jnp.zeros_like(loss_ref)

        dg_ref[...] += jnp.sum(dy * xhat, axis=0, keepdims=True)
        part = jnp.sum(jnp.sum(err * err, axis=-1, keepdims=True), axis=0, keepdims=True) * (0.5 / d)
        loss_ref[...] += jnp.broadcast_to(part, loss_ref.shape)

    row = pl.BlockSpec((tm, d), lambda i: (i, 0))
    vec = pl.BlockSpec((1, d), lambda i: (0, 0))
    return pl.pallas_call(
        body, name=name, grid=(t // tm,), in_specs=[row, vec, row],
        out_specs=[row, row, vec, pl.BlockSpec((1, LANES), lambda i: (0, 0))],
        out_shape=[jax.ShapeDtypeStruct((t, d), F32), jax.ShapeDtypeStruct((t, d), BF16), jax.ShapeDtypeStruct((1, d), F32),
                   jax.ShapeDtypeStruct((1, LANES), F32)],
        compiler_params=_cp())(h, gain, target)


def _norm_gate_up(x, gain, wg, wu, *, name, tm=512, tf=1408, pack=None, seg_rows=()):
    t, d = x.shape
    f = wg.shape[0]
    tm, tf = min(tm, t), min(tf, f)
    assert f % tf == 0
    ni, nj = t // tm, f // tf
    nseg = len(seg_rows)

    def body(x_ref, g_ref, wg_ref, wu_ref, *rest):
        if pack is not None:
            pack_ref, xn_ref, gg_ref, uu_ref, act_ref = rest[:5]
            start, forward, finish = _gather_phases(pack_ref, rest[5:5 + nseg], seg_rows, *rest[5 + nseg:])
            step = pl.program_id(0) * nj + pl.program_id(1)
            pl.when(step == 0)(start)
            pl.when(step == (3 * ni * nj) // 4)(forward)
        else:
            xn_ref, gg_ref, uu_ref, act_ref = rest

        @pl.when(pl.program_id(1) == 0)
        def _():
            xv = x_ref[...]
            rstd = lax.rsqrt(jnp.mean(xv * xv, axis=-1, keepdims=True) + RMS_EPS)
            xn_ref[...] = (xv * rstd * g_ref[...]).astype(BF16)

        xn = xn_ref[...]
        gv = _dot(xn, wg_ref[...], 1, 1)
        uv = _dot(xn, wu_ref[...], 1, 1)
        gg_ref[...] = gv.astype(BF16)
        uu_ref[...] = uv.astype(BF16)
        act_ref[...] = (gv * _sigmoid(gv) * uv).astype(BF16)
        if pack is not None:
            pl.when(step == ni * nj - 1)(finish)

    row = pl.BlockSpec((tm, d), lambda i, j: (i, 0))
    wsp = pl.BlockSpec((tf, d), lambda i, j: (j, 0))
    osp = pl.BlockSpec((tm, tf), lambda i, j: (i, j))
    fused = pack is not None
    return pl.pallas_call(
        body, name=name, grid=(ni, nj),
        in_specs=[row, pl.BlockSpec((1, d), lambda i, j: (0, 0)), wsp, wsp] + ([HBM_SPEC] if fused else []),
        out_specs=[row, osp, osp, osp] + [HBM_SPEC] * nseg,
        out_shape=[jax.ShapeDtypeStruct((t, d), BF16)] + [jax.ShapeDtypeStruct((t, f), BF16)] * 3
        + [jax.ShapeDtypeStruct((8, n, d), BF16) for n in seg_rows],
        scratch_shapes=_gather_scratch() if fused else [],
        compiler_params=_cp(dimension_semantics=("arbitrary", "arbitrary")))(x, gain, wg, wu, *([pack] if fused else []))


def _swiglu_bwd(dout, wd, gg, uu, chip_part=None, *, name, tm=512, tf=1408):
    t, d = dout.shape
    f = wd.shape[0]
    tm, tf = min(tm, t), min(tf, f)
    nj, ni = f // tf, t // tm
    fused = chip_part is not None

    def body(do_ref, wd_ref, g_ref, u_ref, *rest):
        if fused:
            part_ref, dg_ref, du_ref, parts_ref = rest[:4]
            start, finish = _chip_exchange_phases(part_ref, parts_ref, *rest[4:])
            step = pl.program_id(0) * ni + pl.program_id(1)
            pl.when(step == 0)(start)
        else:
            dg_ref, du_ref = rest
        dact = _dot(do_ref[...], wd_ref[...], 1, 1)
        gv = g_ref[...].astype(F32)
        uv = u_ref[...].astype(F32)
        sg = _sigmoid(gv)
        dg_ref[...] = (dact * uv * (sg * (1.0 + gv * (1.0 - sg)))).astype(BF16)
        du_ref[...] = (dact * (gv * sg)).astype(BF16)
        if fused:
            pl.when(step == nj * ni - 1)(finish)

    osp = pl.BlockSpec((tm, tf), lambda j, i: (i, j))
    return pl.pallas_call(
        body, name=name, grid=(nj, ni),
        in_specs=[pl.BlockSpec((tm, d), lambda j, i: (i, 0)), pl.BlockSpec((tf, d), lambda j, i: (j, 0)), osp, osp]
        + ([HBM_SPEC] if fused else []),
        out_specs=[osp, osp] + ([HBM_SPEC] if fused else []),
        out_shape=[jax.ShapeDtypeStruct((t, f), BF16)] * 2
        + ([jax.ShapeDtypeStruct(chip_part.shape, chip_part.dtype)] if fused else []),
        scratch_shapes=_chip_exchange_scratch() if fused else [],
        compiler_params=_cp(dimension_semantics=("arbitrary", "arbitrary")))(dout, wd, gg, uu, *([chip_part] if fused else []))


def _shift_down(x, n):
    rows = lax.broadcasted_iota(jnp.int32, x.shape, 0)
    return jnp.where(rows >= n, pltpu.roll(x, n, 0), 0.0)


def _shift_up(x, n):
    t = x.shape[0]
    rows = lax.broadcasted_iota(jnp.int32, x.shape, 0)
    return jnp.where(rows < t - n, pltpu.roll(x, t - n, 0), 0.0)


def _conv_fwd(pa, conv_w, *, name):
    t = pa.shape[0]
    nb = pa.shape[1] // 3 // LANES

    def body(b_ref, c_ref, x_ref, w_ref, y_ref):
        u = c_ref[...] * x_ref[...]
        w = w_ref[...]
        conv = w[2:3, :] * u + w[1:2, :] * _shift_down(u, 1) + w[0:1, :] * _shift_down(u, 2)
        y_ref[...] = (b_ref[...] * conv).astype(BF16)

    def col(off):
        return pl.BlockSpec((t, LANES), lambda j: (0, off + j))

    return pl.pallas_call(
        body, name=name, grid=(nb,),
        in_specs=[col(0), col(nb), col(2 * nb), pl.BlockSpec((3, LANES), lambda j: (0, j))],
        out_specs=pl.BlockSpec((t, LANES), lambda j: (0, j)),
        out_shape=jax.ShapeDtypeStruct((t, nb * LANES), BF16), compiler_params=_cp())(pa, pa, pa, conv_w)


def _conv_bwd(pa, dy, conv_w, *, name):
    t = pa.shape[0]
    nb = pa.shape[1] // 3 // LANES

    def body(b_ref, c_ref, x_ref, dy_ref, w_ref, db_ref, dc_ref, dx_ref, dw_ref):
        cv, xv = c_ref[...], x_ref[...]
        u = cv * xv
        u1, u2 = _shift_down(u, 1), _shift_down(u, 2)
        w = w_ref[...]
        conv = w[2:3, :] * u + w[1:2, :] * u1 + w[0:1, :] * u2
        dyv = dy_ref[...]
        db_ref[...] = (dyv * conv).astype(BF16)
        dconv = dyv * b_ref[...]
        du = w[2:3, :] * dconv + w[1:2, :] * _shift_up(dconv, 1) + w[0:1, :] * _shift_up(dconv, 2)
        dc_ref[...] = (du * xv).astype(BF16)
        dx_ref[...] = (du * cv).astype(BF16)
        dw_ref[0:1, :] = jnp.sum(dconv * u2, axis=0, keepdims=True)
        dw_ref[1:2, :] = jnp.sum(dconv * u1, axis=0, keepdims=True)
        dw_ref[2:3, :] = jnp.sum(dconv * u, axis=0, keepdims=True)

    def col(off):
        return pl.BlockSpec((t, LANES), lambda j: (0, off + j))

    osp = pl.BlockSpec((t, LANES), lambda j: (0, j))
    wsp = pl.BlockSpec((3, LANES), lambda j: (0, j))
    return pl.pallas_call(
        body, name=name, grid=(nb,), in_specs=[col(0), col(nb), col(2 * nb), col(0), wsp],
        out_specs=[osp, osp, osp, wsp],
        out_shape=[jax.ShapeDtypeStruct((t, nb * LANES), BF16)] * 3 + [jax.ShapeDtypeStruct((3, nb * LANES), F32)],
        compiler_params=_cp())(pa, pa, pa, dy, conv_w)


def _sb_consts():
    j = lax.broadcasted_iota(jnp.int32, (SB_KEYS, SB_KEYS), 0)
    s = lax.broadcasted_iota(jnp.int32, (SB_KEYS, SB_KEYS), 1)
    after = (j > s).astype(BF16)
    upto = (j <= s).astype(BF16)
    before = (j < s).astype(BF16)
    return after, jnp.stack([upto, before])


def _log_sigmoid(z):
    return jnp.minimum(z, 0.0) - jnp.log(1.0 + jnp.exp(-jnp.abs(z)))


def _attn_fwd(pb, late_pack, seg_rows, *, name, tq=256):
    t = pb.shape[0]
    npair = pb.shape[1] // 3 // LANES
    tq = min(tq, t)
    nq = t // tq
    cmat, _ = _sb_consts()
    scale = 1.0 / math.sqrt(LANES // 2)

    nseg = len(seg_rows)

    def body(q_ref, k_ref, v_ref, c_ref, late_ref, y_ref, lt_ref, *rest):
        i = pl.program_id(1)
        pair = pl.program_id(0)
        scratch = rest[nseg:nseg + 4]
        start, forward, finish = _gather_phases(late_ref, rest[:nseg], seg_rows, *rest[nseg + 4:])
        pl.when((pair == 0) & (i == 0))(start)
        pl.when((pair == npair - 1) & (i == nq // 2))(forward)
        lane = lax.broadcasted_iota(jnp.int32, (tq, LANES), 1)
        rowpos = i * tq + lax.broadcasted_iota(jnp.int32, (tq, SB_KEYS), 0)
        colid = lax.broadcasted_iota(jnp.int32, (tq, SB_KEYS), 1)
        q2 = q_ref[...] * jnp.asarray(scale, BF16)
        cm = c_ref[...]
        hi_lanes = lane >= LANES // 2
        qhs = [jnp.where(hi_lanes == (hh == 1), q2, jnp.zeros_like(q2)) for hh in range(2)]
        per_q = tq // SB_KEYS

        def blk(jb):
            return pl.ds(pl.multiple_of(jb * SB_KEYS, SB_KEYS), SB_KEYS)

        zbuf, wbuf, accbuf, runbuf = scratch

        def scores(jb):
            kb = k_ref[blk(jb), :]
            for hh in range(2):
                zbuf[hh] = _dot(qhs[hh], kb, 1, 1)

        def values(jb):
            vb = v_ref[blk(jb), :]
            for hh in range(2):
                accbuf[hh] += _dot(wbuf[hh], vb)

        def trip(jb, masked, first=False):
            mask = (jb * SB_KEYS + colid) < rowpos if masked else None
            if not first:
                values(jb + 1)
            pre, css = [], []
            for hh in range(2):
                z = zbuf[hh]
                lb = _log_sigmoid(z)
                lk = lb - z
                if masked:
                    lk = jnp.where(mask, lk, 0.0)
                lk_hi, lk_lo = _split2(lk)
                css.append(_dot(lk_hi, cm) + _dot(lk_lo, cm))
                run = runbuf[hh]
                pre.append(lb + run)
                runbuf[hh] = run + jnp.sum(lk, axis=1, keepdims=True)
            scores(jnp.maximum(jb - 1, 0))
            for hh in range(2):
                w = jnp.exp(pre[hh] + css[hh])
                if masked:
                    w = jnp.where(mask, w, 0.0)
                wbuf[hh] = w.astype(BF16)

        nfull = i * per_q
        accbuf[...] = jnp.zeros_like(accbuf)
        runbuf[...] = jnp.zeros_like(runbuf)
        scores(nfull + per_q - 1)
        for dblk in reversed(range(per_q)):
            trip(nfull + dblk, True, first=dblk == per_q - 1)

        def full_block(n, carry):
            trip(nfull - 1 - n, False)
            return carry

        lax.fori_loop(0, nfull, full_block, 0)
        values(0)
        y_ref[...] = jnp.where(hi_lanes, accbuf[1], accbuf[0]).astype(BF16)
        lt_ref[...] = jnp.where(hi_lanes, runbuf[1], runbuf[0])
        pl.when((pair == npair - 1) & (i == nq - 1))(finish)

    return pl.pallas_call(
        body, name=name, grid=(npair, nq),
        in_specs=[pl.BlockSpec((tq, LANES), lambda p, i: (i, p)),
                  pl.BlockSpec((t, LANES), lambda p, i: (0, npair + p)),
                  pl.BlockSpec((t, LANES), lambda p, i: (0, 2 * npair + p)),
                  pl.BlockSpec((SB_KEYS, SB_KEYS), lambda p, i: (0, 0)),
                  HBM_SPEC],
        out_specs=[pl.BlockSpec((tq, LANES), lambda p, i: (i, p))] * 2 + [HBM_SPEC] * nseg,
        out_shape=[jax.ShapeDtypeStruct((t, npair * LANES), BF16), jax.ShapeDtypeStruct((t, npair * LANES), F32),
                   ] + [jax.ShapeDtypeStruct((8, n, late_pack.shape[1]), late_pack.dtype) for n in seg_rows],
        scratch_shapes=[pltpu.VMEM((2, tq, SB_KEYS), F32), pltpu.VMEM((2, tq, SB_KEYS), BF16),
                        pltpu.VMEM((2, tq, LANES), F32), pltpu.VMEM((2, tq, 1), F32)] + _gather_scratch(),
        compiler_params=_cp(dimension_semantics=("arbitrary", "arbitrary")))(pb, pb, pb, cmat, late_pack)


def _attn_bwd(pb, dy, ltot, chip_part, *, name, tq=256):
    t = pb.shape[0]
    npair = pb.shape[1] // 3 // LANES
    tq = min(tq, t)
    nq = t // tq
    _, cmats = _sb_consts()
    scale = 1.0 / math.sqrt(LANES // 2)

    def body(q_ref, k_ref, v_ref, dy_ref, lt_ref, c_ref, part_ref, dq_ref, dk_ref, dv_ref, parts_ref, dk_acc, dv_acc, *rest):
        i = pl.program_id(1)
        pair = pl.program_id(0)
        scratch = rest[:6]
        start, finish = _chip_exchange_phases(part_ref, parts_ref, *rest[6:])
        pl.when((pair == 0) & (i == 0))(start)

        @pl.when(i == 0)
        def _():
            dk_acc[...] = jnp.zeros_like(dk_acc)
            dv_acc[...] = jnp.zeros_like(dv_acc)

        lane = lax.broadcasted_iota(jnp.int32, (tq, LANES), 1)
        rowpos = i * tq + lax.broadcasted_iota(jnp.int32, (tq, SB_KEYS), 0)
        colid = lax.broadcasted_iota(jnp.int32, (tq, SB_KEYS), 1)
        q2 = q_ref[...] * jnp.asarray(scale, BF16)
        do2 = dy_ref[...].astype(BF16)
        ltv = lt_ref[...]
        c_upto, c_before = c_ref[0], c_ref[1]
        hi_lanes = lane >= LANES // 2
        sels = [hi_lanes == (hh == 1) for hh in range(2)]
        qhs = [jnp.where(s, q2, jnp.zeros_like(q2)) for s in sels]
        dohs = [jnp.where(s, do2, jnp.zeros_like(do2)) for s in sels]
        lts = [ltv[:, 0:1], ltv[:, LANES // 2:LANES // 2 + 1]]
        per_q = tq // SB_KEYS

        def blk(jb):
            return pl.ds(pl.multiple_of(jb * SB_KEYS, SB_KEYS), SB_KEYS)

        zbuf, dabuf, dzbuf, abuf, dqbuf, sumbuf = scratch

        def scores(jb):
            kb, vb = k_ref[blk(jb), :], v_ref[blk(jb), :]
            for hh in range(2):
                zbuf[hh] = _dot(qhs[hh], kb, 1, 1)
                dabuf[hh] = _dot(dohs[hh], vb, 1, 1)

        def products(jb):
            kb = k_ref[blk(jb), :]
            dk_acc[blk(jb), :] += _dot(dzbuf[0], qhs[0], 0, 0) + _dot(dzbuf[1], qhs[1], 0, 0)
            dv_acc[blk(jb), :] += _dot(abuf[0], dohs[0], 0, 0) + _dot(abuf[1], dohs[1], 0, 0)
            for hh in range(2):
                dqbuf[hh] += _dot(dzbuf[hh], kb)

        def trip(jb, masked):
            mask = (jb * SB_KEYS + colid) < rowpos if masked else None
            products(jnp.maximum(jb - 1, 0))
            lbs, css, es, ces = [], [], [], []
            for hh in range(2):
                z = zbuf[hh]
                lb = _log_sigmoid(z)
                lk = lb - z
                if masked:
                    lk = jnp.where(mask, lk, 0.0)
                lk_hi, lk_lo = _split2(lk)
                css.append(_dot(lk_hi, c_upto) + _dot(lk_lo, c_upto))
                csum = sumbuf[2 * hh]
                lbs.append((lb, lb + (lts[hh] - csum)))
                sumbuf[2 * hh] = csum + jnp.sum(lk, axis=1, keepdims=True)
            for hh in range(2):
                a = jnp.exp(lbs[hh][1] - css[hh])
                if masked:
                    a = jnp.where(mask, a, 0.0)
                e = a * dabuf[hh]
                e_hi, e_lo = _split2(e)
                ces.append(_dot(e_hi, c_before) + _dot(e_lo, c_before))
                abuf[hh] = a.astype(BF16)
                es.append(e)
            scores(jnp.minimum(jb + 1, last))
            for hh in range(2):
                prun = sumbuf[2 * hh + 1]
                beta = jnp.exp(lbs[hh][0])
                dz = es[hh] * (1.0 - beta) - (prun + ces[hh]) * beta
                if masked:
                    dz = jnp.where(mask, dz, 0.0)
                dzbuf[hh] = dz.astype(BF16)
                sumbuf[2 * hh + 1] = prun + jnp.sum(es[hh], axis=1, keepdims=True)

        nfull = i * per_q
        last = nfull + per_q - 1
        for buf in (dzbuf, abuf, dqbuf, sumbuf):
            buf[...] = jnp.zeros_like(buf)
        scores(0)

        def full_block(jb, carry):
            trip(jb, False)
            return carry

        lax.fori_loop(0, nfull, full_block, 0)
        for dblk in range(per_q):
            trip(nfull + dblk, True)
        products(last)
        dq_ref[...] = (jnp.where(hi_lanes, dqbuf[1], dqbuf[0]) * scale).astype(BF16)

        @pl.when(i == nq - 1)
        def _():
            dk_ref[...] = dk_acc[...].astype(BF16)
            dv_ref[...] = dv_acc[...].astype(BF16)

        pl.when((pair == npair - 1) & (i == nq - 1))(finish)

    blk = pl.BlockSpec((tq, LANES), lambda p, i: (i, p))
    full = pl.BlockSpec((t, LANES), lambda p, i: (0, p))
    return pl.pallas_call(
        body, name=name, grid=(npair, nq),
        in_specs=[blk,
                  pl.BlockSpec((t, LANES), lambda p, i: (0, npair + p)),
                  pl.BlockSpec((t, LANES), lambda p, i: (0, 2 * npair + p)),
                  pl.BlockSpec((tq, LANES), lambda p, i: (i, npair + p)),
                  blk,
                  pl.BlockSpec((2, SB_KEYS, SB_KEYS), lambda p, i: (0, 0, 0)),
                  HBM_SPEC],
        out_specs=[blk, full, full, HBM_SPEC],
        out_shape=[jax.ShapeDtypeStruct((t, npair * LANES), BF16)] * 3 + [jax.ShapeDtypeStruct(chip_part.shape, chip_part.dtype)],
        scratch_shapes=[pltpu.VMEM((t, LANES), F32), pltpu.VMEM((t, LANES), F32),
                        pltpu.VMEM((2, tq, SB_KEYS), F32), pltpu.VMEM((2, tq, SB_KEYS), F32),
                        pltpu.VMEM((2, tq, SB_KEYS), BF16), pltpu.VMEM((2, tq, SB_KEYS), BF16),
                        pltpu.VMEM((2, tq, LANES), F32), pltpu.VMEM((4, tq, 1), F32)] + _chip_exchange_scratch(),
        compiler_params=_cp(dimension_semantics=("arbitrary", "arbitrary")))(pb, pb, pb, dy, ltot, cmats, chip_part)


def _hgrn_consts():
    t = lax.broadcasted_iota(jnp.int32, (CHUNK, CHUNK), 0)
    s = lax.broadcasted_iota(jnp.int32, (CHUNK, CHUNK), 1)
    masks = []
    for lvl in range(N_LEVELS):
        half = CHUNK >> (lvl + 1)
        same = (t // (2 * half)) == (s // (2 * half))
        masks.append((same & (t % (2 * half) >= half) & (s % (2 * half) < half)).astype(F32))
    masks.append((t == s).astype(F32))
    prefix = (s <= t).astype(BF16)
    suffix = (s >= t).astype(BF16)
    return prefix, jnp.stack(masks), suffix


def _hgrn_gates(qr, fr, lbv):
    sg = _sigmoid(fr)
    fval = lbv + (1.0 - lbv) * sg
    kk = (1.0 - lbv) * _sigmoid(-fr)
    sq = _sigmoid(qr)
    return sg, fval, jnp.log(fval), kk, sq, qr * sq


def _lower_bound(c_ref):
    c = c_ref[...]
    mx = jnp.max(c, axis=0, keepdims=True)
    ex = jnp.exp(c - mx)
    return ex[1:2, :] / jnp.sum(ex, axis=0, keepdims=True)


def _level_ref(b, lvl):
    half = CHUNK >> (lvl + 1)
    seg = 2 * half
    if seg >= 8:
        b3 = b.reshape(CHUNK // seg, seg, LANES)
        return jnp.broadcast_to(b3[:, half - 1:half, :], b3.shape).reshape(CHUNK, LANES)
    pos = lax.broadcasted_iota(jnp.int32, b.shape, 0) % seg
    out = b
    for p in range(seg):
        if p != half - 1:
            out = jnp.where(pos == p, pltpu.roll(b, (p - (half - 1)) % CHUNK, 0), out)
    return out


def _hgrn_levels(b, qs, kk):
    out = []
    for lvl in range(N_LEVELS):
        fac = jnp.exp(-jnp.abs(b - _level_ref(b, lvl)))
        out.append((qs * fac, kk * fac, fac, fac))
    out.append((qs, kk, None, None))
    return out


def _split2(x):
    hi = x.astype(BF16)
    return hi, (x - hi.astype(F32)).astype(BF16)


def _hgrn_fwd(pc, c_lb, out_norm, *, name, tc=512):
    t = pc.shape[0]
    nh = pc.shape[1] // 4 // LANES
    tc = min(tc, t)
    nch = tc // CHUNK
    cum_all, masks, _ = _hgrn_consts()

    def body(q_ref, f_ref, i_ref, g_ref, lb_ref, on_ref, cum_ref, m_ref, y_ref, o_ref, st_ref, state):
        @pl.when(pl.program_id(1) == 0)
        def _():
            state[...] = jnp.zeros_like(state)

        lbv = _lower_bound(lb_ref)
        onv = on_ref[...]

        def chunk(c, carry):
            rows = pl.ds(pl.multiple_of(c * CHUNK, CHUNK), CHUNK)
            for hh in range(HGRN_HEADS):
                lanes = slice(hh * LANES, (hh + 1) * LANES)
                _, _, g, kk, _, qs = _hgrn_gates(q_ref[rows, lanes], f_ref[rows, lanes], lbv[:, lanes])
                vb = i_ref[rows, lanes].astype(BF16)
                b = _dot_exact_lhs(cum_ref[...], g)
                scores = jnp.zeros((CHUNK, CHUNK), F32)
                for lvl, (ql, kl, _, _) in enumerate(_hgrn_levels(b, qs, kk)):
                    scores = scores + _dot(ql.astype(BF16), kl.astype(BF16), 1, 1) * m_ref[lvl]
                st = state[hh]
                st_ref[hh, c] = st
                o = _dot(scores.astype(BF16), vb) + _dot((qs * jnp.exp(b)).astype(BF16), st.astype(BF16), 1, 1)
                blast = b[CHUNK - 1:CHUNK, :]
                kdec = (kk * jnp.exp(blast - b)).astype(BF16)
                state[hh] = st * jnp.exp(blast) + _dot(vb, kdec, 0, 0)
                o_ref[rows, lanes] = o
                rstd = lax.rsqrt(jnp.mean(o * o, axis=-1, keepdims=True) + RMS_EPS)
                gate = g_ref[rows, lanes]
                y_ref[rows, lanes] = (o * rstd * onv * (gate * _sigmoid(gate))).astype(BF16)
            return carry

        lax.fori_loop(0, nch, chunk, 0, unroll=2)

    hw = HGRN_HEADS * LANES

    def col(off):
        return pl.BlockSpec((tc, hw), lambda h, i: (i, off // HGRN_HEADS + h))

    osp = pl.BlockSpec((tc, hw), lambda h, i: (i, h))
    return pl.pallas_call(
        body, name=name, grid=(nh // HGRN_HEADS, t // tc),
        in_specs=[col(0), col(nh), col(2 * nh), col(3 * nh),
                  pl.BlockSpec((2, hw), lambda h, i: (0, h)),
                  pl.BlockSpec((1, LANES), lambda h, i: (0, 0)),
                  pl.BlockSpec(cum_all.shape, lambda h, i: (0, 0)),
                  pl.BlockSpec(masks.shape, lambda h, i: (0, 0, 0))],
        out_specs=[osp, osp, pl.BlockSpec((HGRN_HEADS, nch, LANES, LANES), lambda h, i: (h, i, 0, 0))],
        out_shape=[jax.ShapeDtypeStruct((t, nh * LANES), BF16), jax.ShapeDtypeStruct((t, nh * LANES), F32),
                   jax.ShapeDtypeStruct((nh, t // CHUNK, LANES, LANES), F32)],
        scratch_shapes=[pltpu.VMEM((HGRN_HEADS, LANES, LANES), F32)],
        compiler_params=_cp())(pc, pc, pc, pc, c_lb, out_norm, cum_all, masks)


def _hgrn_bwd(pc, o_saved, states, dy, c_lb, out_norm, *, name, tc=512):
    t = pc.shape[0]
    nh = pc.shape[1] // 4 // LANES
    tc = min(tc, t)
    nch = tc // CHUNK
    nt = t // tc
    cum_all, masks, suffix = _hgrn_consts()

    def body(q_ref, f_ref, i_ref, g_ref, o_ref, st_ref, dy_ref, lb_ref, on_ref, cum_ref, m_ref, suf_ref,
             dq_ref, df_ref, di_ref, dg_ref, dlb_ref, don_ref, dstate):
        @pl.when(pl.program_id(1) == 0)
        def _():
            dstate[...] = jnp.zeros_like(dstate)
            dlb_ref[...] = jnp.zeros_like(dlb_ref)
            don_ref[...] = jnp.zeros_like(don_ref)

        lbv = _lower_bound(lb_ref)
        onv = on_ref[...]

        def head(hh, c, rows):
            lanes = slice(hh * LANES, (hh + 1) * LANES)
            qr = q_ref[rows, lanes]
            sg, fval, g, kk, sq, qs = _hgrn_gates(qr, f_ref[rows, lanes], lbv[:, lanes])
            vb = i_ref[rows, lanes].astype(BF16)
            o = o_ref[rows, lanes]
            gate = g_ref[rows, lanes]
            sgt = _sigmoid(gate)
            rstd = lax.rsqrt(jnp.mean(o * o, axis=-1, keepdims=True) + RMS_EPS)
            ohat = o * rstd
            dyv = dy_ref[rows, lanes]
            don = dyv * (gate * sgt)
            dg_ref[rows, lanes] = (dyv * ohat * onv * (sgt * (1.0 + gate * (1.0 - sgt)))).astype(BF16)
            don_ref[:, lanes] += jnp.sum(don * ohat, axis=0, keepdims=True)
            dxhat = don * onv
            dob = (rstd * (dxhat - ohat * jnp.mean(dxhat * ohat, axis=-1, keepdims=True))).astype(BF16)
            b = _dot_exact_lhs(cum_ref[...], g)
            blast = b[CHUNK - 1:CHUNK, :]
            eb = jnp.exp(b)
            edec = jnp.exp(blast - b)
            st32 = st_ref[hh, c]
            st = st32.astype(BF16)
            dst = dstate[hh]
            dstb = dst.astype(BF16)
            da = _dot(dob, vb, 1, 1)
            levels = _hgrn_levels(b, qs, kk)
            scores = jnp.zeros((CHUNK, CHUNK), F32)
            dq = eb * _dot(dob, st)
            dk_inter = edec * _dot(vb, dstb)
            dk = dk_inter
            for lvl, (ql, kl, eq, ek) in enumerate(levels):
                mk = m_ref[lvl]
                (qh, qlo), (kh, klo) = _split2(ql), _split2(kl)
                scores = scores + _dot(qh, kh, 1, 1) * mk
                dal = (da * mk).astype(BF16)
                dql = _dot(dal, kh) + _dot(dal, klo)
                dkl = _dot(dal, qh, 0, 0) + _dot(dal, qlo, 0, 0)
                dq = dq + (dql if eq is None else dql * eq)
                dk = dk + (dkl if ek is None else dkl * ek)
            kdec = (kk * edec).astype(BF16)
            dv = _dot(scores.astype(BF16), dob, 0, 0) + _dot(kdec, dstb, 1, 1)
            dstate[hh] = dst * jnp.exp(blast) + _dot(dob, (qs * eb).astype(BF16), 0, 0)
            db = qs * dq - kk * dk
            last = jnp.sum(kk * dk_inter, axis=0, keepdims=True) + jnp.exp(blast) * jnp.sum(dst * st32, axis=0, keepdims=True)
            dgl = _dot_exact_lhs(suf_ref[...], db) + last
            dfv = dgl / fval - dk
            df_ref[rows, lanes] = (dfv * (1.0 - lbv[:, lanes]) * sg * (1.0 - sg)).astype(BF16)
            dlb_ref[:, lanes] += jnp.sum(dfv * (1.0 - sg), axis=0, keepdims=True)
            dq_ref[rows, lanes] = (dq * (sq * (1.0 + qr * (1.0 - sq)))).astype(BF16)
            di_ref[rows, lanes] = dv.astype(BF16)

        def chunk(n, carry):
            c = nch - 1 - n
            rows = pl.ds(pl.multiple_of(c * CHUNK, CHUNK), CHUNK)
            for hh in range(HGRN_HEADS):
                head(hh, c, rows)
            return carry

        lax.fori_loop(0, nch, chunk, 0, unroll=2)

    hw = HGRN_HEADS * LANES

    def col(off):
        return pl.BlockSpec((tc, hw), lambda h, i: (nt - 1 - i, off // HGRN_HEADS + h))

    osp = pl.BlockSpec((tc, hw), lambda h, i: (nt - 1 - i, h))
    vec = pl.BlockSpec((1, hw), lambda h, i: (0, h))
    return pl.pallas_call(
        body, name=name, grid=(nh // HGRN_HEADS, nt),
        in_specs=[col(0), col(nh), col(2 * nh), col(3 * nh), osp,
                  pl.BlockSpec((HGRN_HEADS, nch, LANES, LANES), lambda h, i: (h, nt - 1 - i, 0, 0)),
                  osp,
                  pl.BlockSpec((2, hw), lambda h, i: (0, h)),
                  pl.BlockSpec((1, LANES), lambda h, i: (0, 0)),
                  pl.BlockSpec(cum_all.shape, lambda h, i: (0, 0)),
                  pl.BlockSpec(masks.shape, lambda h, i: (0, 0, 0)),
                  pl.BlockSpec(suffix.shape, lambda h, i: (0, 0))],
        out_specs=[osp, osp, osp, osp, vec, vec],
        out_shape=[jax.ShapeDtypeStruct((t, nh * LANES), BF16)] * 4 + [jax.ShapeDtypeStruct((1, nh * LANES), F32)] * 2,
        scratch_shapes=[pltpu.VMEM((HGRN_HEADS, LANES, LANES), F32)],
        compiler_params=_cp())(pc, pc, pc, pc, o_saved, states, dy, c_lb, out_norm, cum_all, masks, suffix)


HBM_SPEC = pl.BlockSpec(memory_space=pltpu.HBM)


def _gather_scratch():
    return [pltpu.SemaphoreType.DMA((7,)), pltpu.SemaphoreType.DMA((7,)), pltpu.SemaphoreType.DMA]


def _gather_phases(x_ref, out_refs, seg_rows, send_sems, recv_sems, local_sem):
    x, y, c = lax.axis_index("x"), lax.axis_index("y"), lax.axis_index("c")
    me, sibling = (x, y, c), (x, y, 1 - c)
    chips = [(1 - x, y), (x, 1 - y), (1 - x, 1 - y)]
    offs = [sum(seg_rows[:s]) for s in range(len(seg_rows))]
    assert sum(seg_rows) == x_ref.shape[0]

    def index(px, py, pc):
        return 4 * px + 2 * py + pc

    def copies(k, block, to, own):
        return [pltpu.make_async_remote_copy(
            src_ref=x_ref.at[pl.ds(offs[s], n)] if own else out_refs[s].at[index(*block)],
            dst_ref=out_refs[s].at[index(*block)],
            send_sem=send_sems.at[k], recv_sem=recv_sems.at[k], device_id=to, device_id_type=MESH)
            for s, n in enumerate(seg_rows)]

    def all_bytes(k):
        return pltpu.make_async_remote_copy(src_ref=x_ref, dst_ref=x_ref, send_sem=send_sems.at[k],
                                            recv_sem=recv_sems.at[k], device_id=me, device_id_type=MESH)

    mine = [pltpu.make_async_copy(x_ref.at[pl.ds(offs[s], n)], out_refs[s].at[index(*me)], local_sem)
            for s, n in enumerate(seg_rows)]
    first = copies(0, me, sibling, True)
    for j, chip in enumerate(chips):
        first += copies(1 + j, me, (*chip, c), True)

    def start():
        for cp in mine + first:
            cp.start()

    def forward():
        for j, chip in enumerate(chips):
            all_bytes(1 + j).wait_recv()
            for cp in copies(4 + j, (*chip, c), sibling, False):
                cp.start()

    def finish():
        all_bytes(0).wait_recv()
        for j in range(3):
            all_bytes(4 + j).wait_recv()
        for k in range(7):
            all_bytes(k).wait_send()
        pltpu.make_async_copy(x_ref, x_ref, local_sem).wait()

    return start, forward, finish


def _all_gather(xs, seg_rows=None, *, name):
    segs = [xs.shape[0]] if seg_rows is None else list(seg_rows)

    def body(x_ref, *rest):
        start, forward, finish = _gather_phases(x_ref, rest[:len(segs)], segs, *rest[len(segs):])
        start()
        forward()
        finish()

    outs = pl.pallas_call(
        body, name=name, in_specs=[HBM_SPEC], out_specs=[HBM_SPEC] * len(segs),
        out_shape=[jax.ShapeDtypeStruct((8, n, xs.shape[1]), xs.dtype) for n in segs],
        scratch_shapes=_gather_scratch())(xs)
    return outs[0] if seg_rows is None else outs


def _sibling_exchange(s, *, name):
    def body(s_ref, rb_ref, send_sem, recv_sem):
        x, y, c = lax.axis_index("x"), lax.axis_index("y"), lax.axis_index("c")
        cp = pltpu.make_async_remote_copy(
            src_ref=s_ref.at[:, 1 - c], dst_ref=rb_ref, send_sem=send_sem, recv_sem=recv_sem,
            device_id=(x, y, 1 - c), device_id_type=MESH)
        cp.start()
        cp.wait()

    return pl.pallas_call(
        body, name=name, in_specs=[HBM_SPEC], out_specs=HBM_SPEC,
        out_shape=jax.ShapeDtypeStruct(s.shape[:1] + s.shape[2:], s.dtype),
        scratch_shapes=[pltpu.SemaphoreType.DMA, pltpu.SemaphoreType.DMA])(s)


def _row_tile(n, cap=1024):
    return max(b for b in range(16, cap + 1, 16) if n % b == 0)


def _pair_add(s, rb, core, *, name):
    nchip, _, r, c = s.shape
    tb = _row_tile(r)

    def body(core_ref, a_ref, b_ref, o_ref):
        o_ref[...] = (a_ref[...].astype(F32) + b_ref[...].astype(F32)).astype(BF16)

    blk = pl.BlockSpec((None, tb, c), lambda ch, i, cr: (ch, i, 0))
    return pl.pallas_call(
        body, name=name,
        grid_spec=pltpu.PrefetchScalarGridSpec(
            num_scalar_prefetch=1, grid=(nchip, r // tb),
            in_specs=[pl.BlockSpec((None, None, tb, c), lambda ch, i, cr: (ch, cr[0], i, 0)), blk],
            out_specs=blk),
        out_shape=jax.ShapeDtypeStruct((nchip, r, c), BF16), compiler_params=_cp())(core, s, rb)


def _chip_exchange_scratch():
    return [pltpu.SemaphoreType.DMA((3,)), pltpu.SemaphoreType.DMA((3,)), pltpu.SemaphoreType.DMA]


def _chip_exchange_phases(p_ref, out_ref, send_sems, recv_sems, local_sem):
    x, y, c = lax.axis_index("x"), lax.axis_index("y"), lax.axis_index("c")
    mine = 2 * x + y
    own = pltpu.make_async_copy(p_ref.at[mine], out_ref.at[mine], local_sem)
    copies = [pltpu.make_async_remote_copy(
        src_ref=p_ref.at[2 * tx + ty], dst_ref=out_ref.at[mine],
        send_sem=send_sems.at[k], recv_sem=recv_sems.at[k], device_id=(tx, ty, c), device_id_type=MESH)
        for k, (tx, ty) in enumerate([(1 - x, y), (x, 1 - y), (1 - x, 1 - y)])]

    def start():
        own.start()
        for cp in copies:
            cp.start()

    def finish():
        for cp in copies:
            cp.wait()
        own.wait()

    return start, finish


def _adamw_math(w, g, m, v):
    m2 = ADAM_B1 * m + (1.0 - ADAM_B1) * g
    v2 = ADAM_B2 * v + (1.0 - ADAM_B2) * (g * g)
    m_hat = m2 / (1.0 - ADAM_B1 ** ADAM_STEP)
    v_hat = v2 / (1.0 - ADAM_B2 ** ADAM_STEP)
    return -ADAM_LR * (m_hat / (jnp.sqrt(v_hat) + ADAM_EPS) + ADAM_WD * w), m2, v2


def _grad_sum(parts, *, name):
    _, r, c = parts.shape
    tb = _row_tile(r)

    def body(p0, p1, p2, p3, g_out):
        g_out[...] = ((p0[...].astype(F32) + p1[...].astype(F32)) + p2[...].astype(F32)) + p3[...].astype(F32)

    def part(ch):
        return pl.BlockSpec((None, tb, c), lambda i: (ch, i, 0))

    return pl.pallas_call(
        body, name=name, grid=(r // tb,), in_specs=[part(0), part(1), part(2), part(3)],
        out_specs=pl.BlockSpec((tb, c), lambda i: (i, 0)), out_shape=jax.ShapeDtypeStruct((r, c), F32),
        compiler_params=_cp())(parts, parts, parts, parts)


def _adamw_shard(g, g_off, w, m, v, layer, prev, *, name):
    _, r, c = w.shape
    tb = next(b for b in range(min(r, 512), 0, -8) if r % b == 0 and g_off % b == 0)

    def body(g_ref, w_ref, m_ref, v_ref, *rest):
        d_out, m_out, v_out = rest[-3:]
        d, m2, v2 = _adamw_math(w_ref[...], g_ref[...], m_ref[...], v_ref[...])
        d_out[...] = d
        m_out[...] = m2
        v_out[...] = v2

    blk = pl.BlockSpec((None, tb, c), lambda i: (layer, i, 0))
    prev = list(prev) if prev is not None else []
    return pl.pallas_call(
        body, name=name, grid=(r // tb,),
        in_specs=[pl.BlockSpec((tb, c), lambda i: (g_off // tb + i, 0)), blk, blk, blk] + [pl.BlockSpec(memory_space=pl.ANY)] * len(prev),
        out_specs=[blk] * 3, out_shape=[jax.ShapeDtypeStruct(w.shape, F32)] * 3,
        input_output_aliases={4 + k: k for k in range(len(prev))},
        compiler_params=_cp())(g, w, m, v, *prev)


SLOT = 8
SMALL_ROWS = 6 * SLOT
ROW_LB = 4 * SLOT


def _small_update(gath, w, m, v, *, name):
    def body(g_ref, w_ref, m_ref, v_ref, g_out, d_out, m_out, v_out):
        tot = g_ref[0]
        for k in range(1, 8):
            tot = tot + g_ref[k]
        wv = w_ref[...]
        c0, c1 = wv[ROW_LB:ROW_LB + 1, :], wv[ROW_LB + 1:ROW_LB + 2, :]
        mx = jnp.maximum(c0, c1)
        e0, e1 = jnp.exp(c0 - mx), jnp.exp(c1 - mx)
        lb = e1 / (e0 + e1)
        gl = tot[ROW_LB:ROW_LB + 1, :] * lb * (1.0 - lb)
        row = lax.broadcasted_iota(jnp.int32, tot.shape, 0)
        g = jnp.where(row == ROW_LB, -gl, jnp.where(row == ROW_LB + 1, gl, tot))
        d, m2, v2 = _adamw_math(wv, g, m_ref[...], v_ref[...])
        g_out[...] = g
        d_out[...] = d
        m_out[...] = m2
        v_out[...] = v2

    return pl.pallas_call(
        body, name=name, out_shape=[jax.ShapeDtypeStruct(w.shape, F32)] * 4, compiler_params=_cp())(gath, w, m, v)


D_MODEL = 1024


def _ffn_fwd(h, gain, wg, wu, wd, tag):
    xn, gg, uu, act = _norm_gate_up(h, gain, wg, wu, name=f"{tag}_gate_up")
    out = _mm([(act, wd)], residual=h, alpha=MACARON, tn=1024, name=f"{tag}_down")
    return out, (h, xn, gg, uu, act)


def _ffn_input_bwd(dg, du, wg, wu, x, gain, dres, chip_part, *, name, scale, tm=256):
    t, d = x.shape
    f = wg.shape[0]
    tm = min(tm, t)
    nt = t // tm
    fused = chip_part is not None

    def body(dg_ref, du_ref, wg_ref, wu_ref, x_ref, g_ref, dres_ref, *rest):
        if fused:
            part_ref, dx_ref, dxb_ref, dgain_ref, parts_ref = rest[:5]
            start, finish = _chip_exchange_phases(part_ref, parts_ref, *rest[5:])
            pl.when(pl.program_id(0) == 0)(start)
        else:
            dx_ref, dxb_ref, dgain_ref = rest
        dxn_v = _dot(dg_ref[...], wg_ref[...]) + _dot(du_ref[...], wu_ref[...])
        xv = x_ref[...]
        rstd = lax.rsqrt(jnp.mean(xv * xv, axis=-1, keepdims=True) + RMS_EPS)
        xhat = xv * rstd
        dxhat = dxn_v * g_ref[...]
        dx = dres_ref[...] + rstd * (dxhat - xhat * jnp.mean(dxhat * xhat, axis=-1, keepdims=True))
        dx_ref[...] = dx
        dxb_ref[...] = (dx * scale).astype(BF16)

        @pl.when(pl.program_id(0) == 0)
        def _():
            dgain_ref[...] = jnp.zeros_like(dgain_ref)

        dgain_ref[...] += jnp.sum(dxn_v * xhat, axis=0, keepdims=True)
        if fused:
            pl.when(pl.program_id(0) == nt - 1)(finish)

    wide = pl.BlockSpec((tm, f), lambda i: (i, 0))
    wsp = pl.BlockSpec((f, d), lambda i: (0, 0))
    row = pl.BlockSpec((tm, d), lambda i: (i, 0))
    vec = pl.BlockSpec((1, d), lambda i: (0, 0))
    args = [dg, du, wg, wu, x, gain, dres] + ([chip_part] if fused else [])
    return pl.pallas_call(
        body, name=name, grid=(nt,),
        in_specs=[wide, wide, wsp, wsp, row, vec, row] + ([HBM_SPEC] if fused else []),
        out_specs=[row, row, vec] + ([HBM_SPEC] if fused else []),
        out_shape=[jax.ShapeDtypeStruct((t, d), F32), jax.ShapeDtypeStruct((t, d), BF16), jax.ShapeDtypeStruct((1, d), F32)]
        + ([jax.ShapeDtypeStruct(chip_part.shape, chip_part.dtype)] if fused else []),
        scratch_shapes=_chip_exchange_scratch() if fused else [],
        compiler_params=_cp(dimension_semantics=("arbitrary",)))(*args)


def _ffn_bwd(dout, dout_half, saved, gain, wg, wu, wd, tag, next_scale, make_chip_part=None, early_chip_part=None):
    h, xn, gg, uu, act = saved
    dg, du, *early_parts = _swiglu_bwd(dout_half, wd, gg, uu, early_chip_part, name=f"{tag}_dact")
    dwd = _mm([(act, dout_half)], ta=True, tm=256, tn=1024, out_dtype=BF16, name=f"{tag}_dwd")
    dwg = _mm([(dg, xn)], ta=True, tm=256, tn=1024, out_dtype=BF16, name=f"{tag}_dwg")
    dwu = _mm([(du, xn)], ta=True, tm=256, tn=1024, out_dtype=BF16, name=f"{tag}_dwu")
    chip_part = make_chip_part(dwg, dwu, dwd) if make_chip_part is not None else None
    dh, dh_b, dgain, *parts = _ffn_input_bwd(dg, du, wg, wu, h, gain, dout, chip_part, scale=next_scale,
                                             name=f"{tag}_input_bwd")
    return dh, dh_b, dwg, dwu, dwd, dgain, (parts[0] if parts else None), (early_parts[0] if early_parts else None)


def kernel(x, ffn_pre_norm, ffn_pre_w_gate, ffn_pre_w_up, ffn_pre_w_down, mix_norm, ffn_post_norm, ffn_post_w_gate, ffn_post_w_up, ffn_post_w_down, ab_w_in, ab_conv_w, ab_w_out, c_w_in, c_lower_bounds, c_out_norm, c_w_out, final_norm, loss_target, m_ffn_pre_norm, m_ffn_pre_w_gate, m_ffn_pre_w_up, m_ffn_pre_w_down, m_mix_norm, m_ffn_post_norm, m_ffn_post_w_gate, m_ffn_post_w_up, m_ffn_post_w_down, m_ab_w_in, m_ab_conv_w, m_ab_w_out, m_c_w_in, m_c_lower_bounds, m_c_out_norm, m_c_w_out, m_final_norm, v_ffn_pre_norm, v_ffn_pre_w_gate, v_ffn_pre_w_up, v_ffn_pre_w_down, v_mix_norm, v_ffn_post_norm, v_ffn_post_w_gate, v_ffn_post_w_up, v_ffn_post_w_down, v_ab_w_in, v_ab_conv_w, v_ab_w_out, v_c_w_in, v_c_lower_bounds, v_c_out_norm, v_c_w_out, v_final_norm):
    d = D_MODEL
    h0 = x[0]
    target = loss_target[0]
    core = lax.axis_index("c").astype(jnp.int32).reshape(1)

    big = [("pre_g", ffn_pre_w_gate, m_ffn_pre_w_gate, v_ffn_pre_w_gate),
           ("pre_u", ffn_pre_w_up, m_ffn_pre_w_up, v_ffn_pre_w_up),
           ("pre_d", ffn_pre_w_down, m_ffn_pre_w_down, v_ffn_pre_w_down),
           ("post_g", ffn_post_w_gate, m_ffn_post_w_gate, v_ffn_post_w_gate),
           ("post_u", ffn_post_w_up, m_ffn_post_w_up, v_ffn_post_w_up),
           ("post_d", ffn_post_w_down, m_ffn_post_w_down, v_ffn_post_w_down),
           ("ab_in", ab_w_in, m_ab_w_in, v_ab_w_in),
           ("ab_out", ab_w_out, m_ab_w_out, v_ab_w_out),
           ("c_in", c_w_in, m_c_w_in, v_c_w_in),
           ("c_out", c_w_out, m_c_w_out, v_c_w_out)]
    by_tag = {tag: (w, m, v) for tag, w, m, v in big}

    def layer_rows(tag):
        w = by_tag[tag][0]
        return w.size // d // w.shape[0]

    def layout(items):
        offs, off = {}, 0
        for item in items:
            offs[item] = off
            off += layer_rows(item[0])
        return offs, off

    ffn = [f"{pos}_{kind}" for pos in ("pre", "post") for kind in "gud"]
    first_items = [("pre_g", 0), ("pre_u", 0)]
    early_items = [("pre_d", 0), ("ab_in", 0)]
    late_items = ([("pre_g", 1), ("pre_u", 1), ("pre_d", 1)] + [(f"post_{kind}", l) for l in (0, 1) for kind in "gud"]
                  + [("ab_out", 0), ("c_in", 0), ("c_out", 0)])
    grad_items = {"A": ([(tag, 1) for tag in ffn] + [(f"post_{kind}", 0) for kind in "gud"]
                        + [("c_in", 0), ("c_out", 0), ("ab_out", 0)]),
                  "B": [(f"pre_{kind}", 0) for kind in "gud"], "C": [("ab_in", 0)]}
    grad_offs = {k: layout(items)[0] for k, items in grad_items.items()}
    grad_conv_row = layout(grad_items["C"])[1]

    def conv_rows(a, split):
        flat = a.reshape(-1)
        if split:
            hi = flat.astype(BF16)
            flat = jnp.concatenate([hi, (flat - hi.astype(F32)).astype(BF16)])
        return jnp.zeros((16, d), flat.dtype).at[0, :flat.shape[0]].set(flat)

    nconv = ab_conv_w.size
    col_sharded = {"pre_g", "pre_u", "post_g", "post_u", "ab_in", "c_in"}

    def pack_rows(item):
        tag, layer = item
        a = by_tag[tag][0][layer]
        return (a.T if tag in col_sharded else a).reshape(-1, d).astype(BF16)

    first_pack = jnp.concatenate([pack_rows(item) for item in first_items], axis=0)
    early_pack = jnp.concatenate([pack_rows(item) for item in early_items] + [conv_rows(ab_conv_w, True)], axis=0)
    late_pack = jnp.concatenate([pack_rows(item) for item in late_items], axis=0)
    first_w = _all_gather(first_pack, [layer_rows(tag) for tag, _ in first_items], name="gather_first_weights")
    full = {item: g.reshape(-1, d) for item, g in zip(first_items, first_w)}

    xn0, gg0, uu0, act0, *early_w = _norm_gate_up(
        h0, ffn_pre_norm[0:1], full["pre_g", 0], full["pre_u", 0], name="l0pre_gate_up_gather_early_weights",
        pack=early_pack, seg_rows=[layer_rows(tag) for tag, _ in early_items] + [16])
    full.update({item: g.reshape(-1, d) for item, g in zip(early_items, early_w)})
    ffn_w = {("pre", 0): tuple(full[f"pre_{kind}", 0] for kind in "gud")}
    w_ab_in = full["ab_in", 0]
    cg = early_w[-1][:, 0, :2 * nconv].astype(F32)
    conv_w = (cg[:, :nconv] + cg[:, nconv:]).reshape(8, 3, -1).transpose(1, 0, 2).reshape(3, -1)
    half = w_ab_in.shape[0] // 2
    w_a_in, w_b_in = w_ab_in[:half], w_ab_in[half:]
    aw = half // 3
    h1 = _mm([(act0, full["pre_d", 0])], residual=h0, alpha=MACARON, tn=1024, name="l0pre_down")
    s_pre0 = (h0, xn0, gg0, uu0, act0)
    hn0 = _rmsnorm_fwd(h1, mix_norm[0:1], name="l0_mix_norm")
    pa = _mm([(hn0, w_a_in)], tb=True, tn=1536, name="ab_proj_a")
    pb = _mm([(hn0, w_b_in)], tb=True, tn=1536, out_dtype=BF16, name="ab_proj_b")
    ya = _conv_fwd(pa, conv_w, name="conv_fwd")
    yb, ltot, *late_w = _attn_fwd(pb, late_pack, [layer_rows(tag) for tag, _ in late_items],
                                  name="attn_fwd_gather_late_weights")
    full.update({item: g.reshape(-1, d) for item, g in zip(late_items, late_w)})
    for pos, layer in (("post", 0), ("pre", 1), ("post", 1)):
        ffn_w[pos, layer] = tuple(full[f"{pos}_{kind}", layer] for kind in "gud")
    w_ab_out, w_c_in, w_c_out = full["ab_out", 0], full["c_in", 0], full["c_out", 0]
    h2 = _mm([(ya, w_ab_out[:aw]), (yb, w_ab_out[aw:])], residual=h1, tn=1024, name="ab_out")
    h3, s_post0 = _ffn_fwd(h2, ffn_post_norm[0:1], *ffn_w["post", 0], "l0post")
    h4, s_pre1 = _ffn_fwd(h3, ffn_pre_norm[1:2], *ffn_w["pre", 1], "l1pre")
    hn1 = _rmsnorm_fwd(h4, mix_norm[1:2], name="l1_mix_norm")
    pc = _mm([(hn1, w_c_in)], tb=True, tm=256, tn=4096, name="c_proj")
    yc, o_saved, states = _hgrn_fwd(pc, c_lower_bounds, c_out_norm, name="hgrn_fwd")
    h5 = _mm([(yc, w_c_out)], residual=h4, tn=1024, name="c_out")
    h6, s_post1 = _ffn_fwd(h5, ffn_post_norm[1:2], *ffn_w["post", 1], "l1post")
    dh6, dh6_b, d_final, loss_vec = _loss_head(h6, final_norm.reshape(1, d), target, name="loss_head")

    gw = {}
    dh5, dh5_b, gw["post_g", 1], gw["post_u", 1], gw["post_d", 1], d_post1, *_ = _ffn_bwd(
        dh6, dh6_b, s_post1, ffn_post_norm[1:2], *ffn_w["post", 1], "l1post", 1.0)
    dyc = _mm([(dh5_b, w_c_out)], tb=True, tn=1024, name="c_out_dy")
    g_c_out = _mm([(yc, dh5_b)], ta=True, tm=256, tn=1024, out_dtype=BF16, name="c_out_dw")
    dcq, dcf, dci, dcg, dlb, d_onorm = _hgrn_bwd(pc, o_saved, states, dyc, c_lower_bounds, c_out_norm, name="hgrn_bwd")
    dparts = [dcq, dcf, dci, dcg]
    g_c_in = jnp.concatenate(
        [_mm([(dp, hn1)], ta=True, tm=256, tn=1024, out_dtype=BF16, name=f"c_in_dw{i}") for i, dp in enumerate(dparts)],
        axis=0)
    cw = w_c_in.shape[0] // 4
    dhn1 = _mm([(dp, w_c_in[i * cw:(i + 1) * cw]) for i, dp in enumerate(dparts)], tm=512, tn=1024, name="c_in_dx")
    dh4, dh4_b, d_mix1 = _rmsnorm_bwd(h4, mix_norm[1:2], dhn1, dh5, scale=MACARON, name="l1_mix_norm_bwd")
    dh3, dh3_b, gw["pre_g", 1], gw["pre_u", 1], gw["pre_d", 1], d_pre1, *_ = _ffn_bwd(
        dh4, dh4_b, s_pre1, ffn_pre_norm[1:2], *ffn_w["pre", 1], "l1pre", MACARON)
    dh2, dh2_b, gw["post_g", 0], gw["post_u", 0], gw["post_d", 0], d_post0, *_ = _ffn_bwd(
        dh3, dh3_b, s_post0, ffn_post_norm[0:1], *ffn_w["post", 0], "l0post", 1.0)
    dyab = _mm([(dh2_b, w_ab_out)], tb=True, tn=1024, name="ab_out_dy")
    g_ab_out = jnp.concatenate([_mm([(ya, dh2_b)], ta=True, tm=256, tn=1024, out_dtype=BF16, name="ab_out_dw_a"),
                                _mm([(yb, dh2_b)], ta=True, tm=256, tn=1024, out_dtype=BF16, name="ab_out_dw_b")], axis=0)
    dab, dac, dax, g_conv = _conv_bwd(pa, dyab, conv_w, name="conv_bwd")

    def chip_partials(key, grads, extra=()):
        gpack = jnp.concatenate([grads[item].reshape(8, -1, d) for item in grad_items[key]] + list(extra), axis=1)
        send = gpack.reshape(4, 2, gpack.shape[1], d)
        from_sibling = _sibling_exchange(send, name=f"grad{key}_sibling_exchange")
        return _pair_add(send, from_sibling, core, name=f"grad{key}_pair_add")

    gw["c_in", 0], gw["c_out", 0], gw["ab_out", 0] = g_c_in, g_c_out, g_ab_out
    chip_part_a = chip_partials("A", gw)
    dq, dk, dv, parts_a = _attn_bwd(pb, dyab, ltot, chip_part_a, name="attn_bwd_exchange_grads_a")
    dparts = [dab, dac, dax, dq, dk, dv]
    g_ab_in = jnp.concatenate(
        [_mm([(dp, hn0)], ta=True, tm=256, tn=1024, out_dtype=BF16, name=f"ab_in_dw{i}") for i, dp in enumerate(dparts)],
        axis=0)
    dhn0 = _mm([(dp, w_ab_in[i * aw:(i + 1) * aw]) for i, dp in enumerate(dparts)], tm=512, tn=1024, name="ab_in_dx")
    dh1, dh1_b, d_mix0 = _rmsnorm_bwd(h1, mix_norm[0:1], dhn0, dh2, scale=MACARON, name="l0_mix_norm_bwd")
    gw["ab_in", 0] = g_ab_in
    gconv_own = g_conv.reshape(3, 8, -1).transpose(1, 0, 2).reshape(8, -1)
    conv_piece = jnp.zeros((8, 16, d), F32).at[:, 0, :nconv].set(gconv_own).astype(BF16)

    def chip_part_b(dwg, dwu, dwd):
        gw["pre_g", 0], gw["pre_u", 0], gw["pre_d", 0] = dwg, dwu, dwd
        return chip_partials("B", gw)

    dh0, _, _, _, _, d_pre0, parts_b, parts_c = _ffn_bwd(
        dh1, dh1_b, s_pre0, ffn_pre_norm[0:1], *ffn_w["pre", 0], "l0pre", 1.0, chip_part_b,
        chip_partials("C", gw, [conv_piece]))

    g_sum = {key: _grad_sum(p, name=f"grad{key}_sum") for key, p in (("A", parts_a), ("B", parts_b), ("C", parts_c))}

    upd = {}
    for tag, w, m, v in big:
        nl = layer_rows(tag)
        view = (lambda a: jnp.swapaxes(a, 1, 2)) if tag in col_sharded else (lambda a: a)
        where = {layer: (key, grad_offs[key][tag, layer])
                 for key in grad_items for t2, layer in grad_items[key] if t2 == tag}
        res = None
        for layer in sorted(where):
            key, off = where[layer]
            res = _adamw_shard(g_sum[key], off, view(w), view(m), view(v), layer, res, name=f"adamw_{tag}{layer}")
        g_nat = jnp.stack([g_sum[where[layer][0]][where[layer][1]:where[layer][1] + nl] for layer in sorted(where)])
        upd[tag] = [view(a) for a in [g_nat] + list(res)]
    res = _adamw_shard(g_sum["C"], grad_conv_row, *(conv_rows(a, False)[None] for a in (ab_conv_w, m_ab_conv_w, v_ab_conv_w)),
                       0, None, name="adamw_conv")
    g_conv_rows = g_sum["C"][grad_conv_row:grad_conv_row + 16]
    upd["conv"] = [r[0, :nconv].reshape(ab_conv_w.shape) for r in [g_conv_rows] + [r[0] for r in res]]

    def small_pack(pre, mix, post, final, lbs, onorm):
        def slot(parts):
            out, r = jnp.zeros((SLOT, d), F32), 0
            for a in (parts if isinstance(parts, tuple) else (parts,)):
                out = out.at[r:r + a.shape[0], :a.shape[1]].set(a)
                r += a.shape[0]
            return out

        return jnp.concatenate([slot(pre), slot(mix), slot(post), slot(final.reshape(1, d)), slot(lbs), slot(onorm)], axis=0)

    d_on = d_onorm.reshape(-1, c_out_norm.shape[1]).sum(axis=0, keepdims=True)
    gsmall = small_pack((d_pre0, d_pre1), (d_mix0, d_mix1), (d_post0, d_post1), d_final, dlb, d_on)
    gsmall_all = _all_gather(gsmall, name="gather_small_grads")
    sres = _small_update(
        gsmall_all,
        small_pack(ffn_pre_norm, mix_norm, ffn_post_norm, final_norm, c_lower_bounds, c_out_norm),
        small_pack(m_ffn_pre_norm, m_mix_norm, m_ffn_post_norm, m_final_norm, m_c_lower_bounds, m_c_out_norm),
        small_pack(v_ffn_pre_norm, v_mix_norm, v_ffn_post_norm, v_final_norm, v_c_lower_bounds, v_c_out_norm),
        name="small_update")

    def small_out(r):
        return {"pre_norm": r[0:2], "mix_norm": r[SLOT:SLOT + 2], "post_norm": r[2 * SLOT:2 * SLOT + 2],
                "final": r[3 * SLOT], "lb": r[ROW_LB:ROW_LB + 2], "onorm": r[5 * SLOT:5 * SLOT + 1, :c_out_norm.shape[1]]}

    small = [small_out(r) for r in sres]
    outs = []
    for k in range(4):
        s = small[k]
        outs += [s["pre_norm"], upd["pre_g"][k], upd["pre_u"][k], upd["pre_d"][k], s["mix_norm"], s["post_norm"],
                 upd["post_g"][k], upd["post_u"][k], upd["post_d"][k], upd["ab_in"][k], upd["conv"][k],
                 upd["ab_out"][k], upd["c_in"][k], s["lb"], s["onorm"], upd["c_out"][k], s["final"]]
    loss = lax.psum(loss_vec[0, 0], ("x", "y", "c"))
    return (loss, dh0[None], *outs)
```

```python
import functools
import math

import jax
import jax.numpy as jnp
from jax import lax
from jax.experimental import pallas as pl
from jax.experimental.pallas import tpu as pltpu

F32 = jnp.float32
BF16 = jnp.bfloat16
MESH = pl.DeviceIdType.MESH

RMS_EPS = 1e-6
MACARON = 0.5
LANES = 128
CHUNK = 64
N_LEVELS = 6
HGRN_HEADS = 2
SB_KEYS = 256
ADAM_LR, ADAM_B1, ADAM_B2, ADAM_EPS, ADAM_WD, ADAM_STEP = 0.001, 0.9, 0.999, 1e-08, 0.01, 10
VMEM_LIMIT = 48 * 1024 * 1024


def _cp(**kw):
    return pltpu.CompilerParams(vmem_limit_bytes=VMEM_LIMIT, **kw)


def _sigmoid(x):
    return 0.5 * jnp.tanh(0.5 * x) + 0.5


def _bf(x):
    return x if x.dtype == BF16 else x.astype(BF16)


def _split3(x):
    hi = x.astype(BF16)
    r1 = x - hi.astype(F32)
    mid = r1.astype(BF16)
    lo = (r1 - mid.astype(F32)).astype(BF16)
    return hi, mid, lo


def _dot(a, b, ca=1, cb=0):
    return lax.dot_general(a, b, (((ca,), (cb,)), ((), ())), preferred_element_type=F32)


def _dot_exact_lhs(m, x):
    hi, mid, lo = _split3(x)
    return _dot(m, hi) + _dot(m, mid) + _dot(m, lo)


def _dot_exact_rhs(x, m):
    hi, mid, lo = _split3(x)
    return _dot(hi, m) + _dot(mid, m) + _dot(lo, m)


def _mm(terms, *, name, ta=False, tb=False, out_dtype=F32, residual=None, alpha=1.0, tm=512, tn=512):
    nt = len(terms)
    a0, b0 = terms[0]
    m = a0.shape[1] if ta else a0.shape[0]
    n = b0.shape[0] if tb else b0.shape[1]
    tm, tn = min(tm, m), min(tn, n)
    assert m % tm == 0 and n % tn == 0, (name, m, n, tm, tn)
    has_res = residual is not None

    def body(*refs):
        o_ref = refs[-1]
        acc = None
        for i in range(nt):
            a = _bf(refs[2 * i][...])
            b = _bf(refs[2 * i + 1][...])
            p = _dot(a, b, 0 if ta else 1, 1 if tb else 0)
            acc = p if acc is None else acc + p
        if alpha != 1.0:
            acc = acc * alpha
        if has_res:
            acc = acc + refs[2 * nt][...]
        o_ref[...] = acc.astype(out_dtype)

    in_specs, args = [], []
    for a, b in terms:
        k = a.shape[0] if ta else a.shape[1]
        assert (b.shape[1] if tb else b.shape[0]) == k, (name, a.shape, b.shape)
        in_specs.append(pl.BlockSpec((k, tm), lambda i, j: (0, i)) if ta else pl.BlockSpec((tm, k), lambda i, j: (i, 0)))
        in_specs.append(pl.BlockSpec((tn, k), lambda i, j: (j, 0)) if tb else pl.BlockSpec((k, tn), lambda i, j: (0, j)))
        args += [a, b]
    if has_res:
        in_specs.append(pl.BlockSpec((tm, tn), lambda i, j: (i, j)))
        args.append(residual)
    return pl.pallas_call(
        body, name=name, grid=(m // tm, n // tn), in_specs=in_specs,
        out_specs=pl.BlockSpec((tm, tn), lambda i, j: (i, j)),
        out_shape=jax.ShapeDtypeStruct((m, n), out_dtype), compiler_params=_cp())(*args)


def _norm_proj(x, gain, w_t, out_dtypes, *, name, tm):
    t, d = x.shape
    n = w_t.shape[0]
    tm = min(tm, t)
    npart = len(out_dtypes)
    width = n // npart

    def body(x_ref, g_ref, w_ref, xn_ref, *part_refs):
        xv = x_ref[...]
        rstd = lax.rsqrt(jnp.mean(xv * xv, axis=-1, keepdims=True) + RMS_EPS)
        xn = (xv * rstd * g_ref[...]).astype(BF16)
        xn_ref[...] = xn
        for p, ref in enumerate(part_refs):
            ref[...] = _dot(xn, w_ref[p * width:(p + 1) * width, :], 1, 1).astype(out_dtypes[p])

    row = pl.BlockSpec((tm, d), lambda i: (i, 0))
    return pl.pallas_call(
        body, name=name, grid=(t // tm,),
        in_specs=[row, pl.BlockSpec((1, d), lambda i: (0, 0)), pl.BlockSpec((n, d), lambda i: (0, 0))],
        out_specs=[row] + [pl.BlockSpec((tm, width), lambda i: (i, 0))] * npart,
        out_shape=[jax.ShapeDtypeStruct((t, d), BF16)] + [jax.ShapeDtypeStruct((t, width), dt) for dt in out_dtypes],
        compiler_params=_cp())(x, gain, w_t)


def _mm_shared_rhs(a_list, b, *, name, tm, out_dtype=BF16):
    k, n = b.shape
    assert all(a.shape[0] == k and a.shape[1] % tm == 0 and a.shape[1] == a_list[0].shape[1] for a in a_list)
    m = a_list[0].shape[1]
    na = len(a_list)

    def body(*refs):
        bv = refs[na][...]
        for i in range(na):
            refs[na + 1 + i][...] = _dot(refs[i][...], bv, 0, 0).astype(out_dtype)

    return pl.pallas_call(
        body, name=name, grid=(m // tm,),
        in_specs=[pl.BlockSpec((k, tm), lambda i: (0, i))] * na + [pl.BlockSpec((k, n), lambda i: (0, 0))],
        out_specs=[pl.BlockSpec((tm, n), lambda i: (i, 0))] * na,
        out_shape=[jax.ShapeDtypeStruct((m, n), out_dtype)] * na, compiler_params=_cp())(*a_list, b)


def _rmsnorm_bwd(x, gain, dxn, dres, *, name, scale, tm=512):
    t, d = x.shape
    tm = min(tm, t)

    def body(x_ref, g_ref, dxn_ref, dres_ref, dx_ref, dxb_ref, dg_ref):
        xv = x_ref[...]
        rstd = lax.rsqrt(jnp.mean(xv * xv, axis=-1, keepdims=True) + RMS_EPS)
        xhat = xv * rstd
        dxn_v = dxn_ref[...]
        dxhat = dxn_v * g_ref[...]
        dx = dres_ref[...] + rstd * (dxhat - xhat * jnp.mean(dxhat * xhat, axis=-1, keepdims=True))
        dx_ref[...] = dx
        dxb_ref[...] = (dx * scale).astype(BF16)

        @pl.when(pl.program_id(0) == 0)
        def _():
            dg_ref[...] = jnp.zeros_like(dg_ref)

        dg_ref[...] += jnp.sum(dxn_v * xhat, axis=0, keepdims=True)

    row = pl.BlockSpec((tm, d), lambda i: (i, 0))
    vec = pl.BlockSpec((1, d), lambda i: (0, 0))
    return pl.pallas_call(
        body, name=name, grid=(t // tm,), in_specs=[row, vec, row, row], out_specs=[row, row, vec],
        out_shape=[jax.ShapeDtypeStruct((t, d), F32), jax.ShapeDtypeStruct((t, d), BF16), jax.ShapeDtypeStruct((1, d), F32)],
        compiler_params=_cp())(x, gain, dxn, dres)


def _loss_head(h, gain, target, *, name, tm=512):
    t, d = h.shape
    tm = min(tm, t)

    def body(h_ref, g_ref, t_ref, dh_ref, dhb_ref, dg_ref, loss_ref):
        hv = h_ref[...]
        rstd = lax.rsqrt(jnp.mean(hv * hv, axis=-1, keepdims=True) + RMS_EPS)
        xhat = hv * rstd
        err = xhat * g_ref[...] - t_ref[...]
        dy = err * (1.0 / d)
        dxhat = dy * g_ref[...]
        dh = rstd * (dxhat - xhat * jnp.mean(dxhat * xhat, axis=-1, keepdims=True))
        dh_ref[...] = dh
        dhb_ref[...] = (dh * MACARON).astype(BF16)

        @pl.when(pl.program_id(0) == 0)
        def _():
            dg_ref[...] = jnp.zeros_like(dg_ref)
            loss_ref[...] = jnp.zeros_like(loss_ref)

        dg_ref[...] += jnp.sum(dy * xhat, axis=0, keepdims=True)
        part = jnp.sum(jnp.sum(err * err, axis=-1, keepdims=True), axis=0, keepdims=True) * (0.5 / d)
        loss_ref[...] += jnp.broadcast_to(part, loss_ref.shape)

    row = pl.BlockSpec((tm, d), lambda i: (i, 0))
    vec = pl.BlockSpec((1, d), lambda i: (0, 0))
    return pl.pallas_call(
        body, name=name, grid=(t // tm,), in_specs=[row, vec, row],
        out_specs=[row, row, vec, pl.BlockSpec((1, LANES), lambda i: (0, 0))],
        out_shape=[jax.ShapeDtypeStruct((t, d), F32), jax.ShapeDtypeStruct((t, d), BF16), jax.ShapeDtypeStruct((1, d), F32),
                   jax.ShapeDtypeStruct((1, LANES), F32)],
        compiler_params=_cp())(h, gain, target)


def _norm_gate_up(x, gain, wg, wu, *, name, tm=512, tf=1408, pack=None, seg_rows=()):
    t, d = x.shape
    f = wg.shape[0]
    tm, tf = min(tm, t), min(tf, f)
    assert f % tf == 0
    ni, nj = t // tm, f // tf
    nseg = len(seg_rows)

    def body(x_ref, g_ref, wg_ref, wu_ref, *rest):
        if pack is not None:
            pack_ref, xn_ref, gg_ref, uu_ref, act_ref = rest[:5]
            start, forward, finish = _gather_phases(pack_ref, rest[5:5 + nseg], seg_rows, *rest[5 + nseg:])
            step = pl.program_id(0) * nj + pl.program_id(1)
            pl.when(step == 0)(start)
            pl.when(step == (3 * ni * nj) // 4)(forward)
        else:
            xn_ref, gg_ref, uu_ref, act_ref = rest

        @pl.when(pl.program_id(1) == 0)
        def _():
            xv = x_ref[...]
            rstd = lax.rsqrt(jnp.mean(xv * xv, axis=-1, keepdims=True) + RMS_EPS)
            xn_ref[...] = (xv * rstd * g_ref[...]).astype(BF16)

        xn = xn_ref[...]
        gv = _dot(xn, wg_ref[...], 1, 1)
        uv = _dot(xn, wu_ref[...], 1, 1)
        gg_ref[...] = gv.astype(BF16)
        uu_ref[...] = uv.astype(BF16)
        act_ref[...] = (gv * _sigmoid(gv) * uv).astype(BF16)
        if pack is not None:
            pl.when(step == ni * nj - 1)(finish)

    row = pl.BlockSpec((tm, d), lambda i, j: (i, 0))
    wsp = pl.BlockSpec((tf, d), lambda i, j: (j, 0))
    osp = pl.BlockSpec((tm, tf), lambda i, j: (i, j))
    fused = pack is not None
    return pl.pallas_call(
        body, name=name, grid=(ni, nj),
        in_specs=[row, pl.BlockSpec((1, d), lambda i, j: (0, 0)), wsp, wsp] + ([HBM_SPEC] if fused else []),
        out_specs=[row, osp, osp, osp] + [HBM_SPEC] * nseg,
        out_shape=[jax.ShapeDtypeStruct((t, d), BF16)] + [jax.ShapeDtypeStruct((t, f), BF16)] * 3
        + [jax.ShapeDtypeStruct((8, n, d), BF16) for n in seg_rows],
        scratch_shapes=_gather_scratch() if fused else [],
        compiler_params=_cp(dimension_semantics=("arbitrary", "arbitrary")))(x, gain, wg, wu, *([pack] if fused else []))


def _swiglu_bwd(dout, wd, gg, uu, chip_part=None, *, name, tm=512, tf=1408):
    t, d = dout.shape
    f = wd.shape[0]
    tm, tf = min(tm, t), min(tf, f)
    nj, ni = f // tf, t // tm
    fused = chip_part is not None

    def body(do_ref, wd_ref, g_ref, u_ref, *rest):
        if fused:
            part_ref, dg_ref, du_ref, parts_ref = rest[:4]
            start, finish = _chip_exchange_phases(part_ref, parts_ref, *rest[4:])
            step = pl.program_id(0) * ni + pl.program_id(1)
            pl.when(step == 0)(start)
        else:
            dg_ref, du_ref = rest
        dact = _dot(do_ref[...], wd_ref[...], 1, 1)
        gv = g_ref[...].astype(F32)
        uv = u_ref[...].astype(F32)
        sg = _sigmoid(gv)
        dg_ref[...] = (dact * uv * (sg * (1.0 + gv * (1.0 - sg)))).astype(BF16)
        du_ref[...] = (dact * (gv * sg)).astype(BF16)
        if fused:
            pl.when(step == nj * ni - 1)(finish)

    osp = pl.BlockSpec((tm, tf), lambda j, i: (i, j))
    return pl.pallas_call(
        body, name=name, grid=(nj, ni),
        in_specs=[pl.BlockSpec((tm, d), lambda j, i: (i, 0)), pl.BlockSpec((tf, d), lambda j, i: (j, 0)), osp, osp]
        + ([HBM_SPEC] if fused else []),
        out_specs=[osp, osp] + ([HBM_SPEC] if fused else []),
        out_shape=[jax.ShapeDtypeStruct((t, f), BF16)] * 2
        + ([jax.ShapeDtypeStruct(chip_part.shape, chip_part.dtype)] if fused else []),
        scratch_shapes=_chip_exchange_scratch() if fused else [],
        compiler_params=_cp(dimension_semantics=("arbitrary", "arbitrary")))(dout, wd, gg, uu, *([chip_part] if fused else []))


def _shift_down(x, n):
    rows = lax.broadcasted_iota(jnp.int32, x.shape, 0)
    return jnp.where(rows >= n, pltpu.roll(x, n, 0), 0.0)


def _shift_up(x, n):
    t = x.shape[0]
    rows = lax.broadcasted_iota(jnp.int32, x.shape, 0)
    return jnp.where(rows < t - n, pltpu.roll(x, t - n, 0), 0.0)


def _conv_fwd(pa, conv_w, *, name):
    t = pa.shape[0]
    nb = pa.shape[1] // 3 // LANES

    def body(b_ref, c_ref, x_ref, w_ref, y_ref):
        u = c_ref[...] * x_ref[...]
        w = w_ref[...]
        conv = w[2:3, :] * u + w[1:2, :] * _shift_down(u, 1) + w[0:1, :] * _shift_down(u, 2)
        y_ref[...] = (b_ref[...] * conv).astype(BF16)

    def col(off):
        return pl.BlockSpec((t, LANES), lambda j: (0, off + j))

    return pl.pallas_call(
        body, name=name, grid=(nb,),
        in_specs=[col(0), col(nb), col(2 * nb), pl.BlockSpec((3, LANES), lambda j: (0, j))],
        out_specs=pl.BlockSpec((t, LANES), lambda j: (0, j)),
        out_shape=jax.ShapeDtypeStruct((t, nb * LANES), BF16), compiler_params=_cp())(pa, pa, pa, conv_w)


def _conv_bwd(pa, dy, conv_w, *, name):
    t = pa.shape[0]
    nb = pa.shape[1] // 3 // LANES

    def body(b_ref, c_ref, x_ref, dy_ref, w_ref, db_ref, dc_ref, dx_ref, dw_ref):
        cv, xv = c_ref[...], x_ref[...]
        u = cv * xv
        u1, u2 = _shift_down(u, 1), _shift_down(u, 2)
        w = w_ref[...]
        conv = w[2:3, :] * u + w[1:2, :] * u1 + w[0:1, :] * u2
        dyv = dy_ref[...]
        db_ref[...] = (dyv * conv).astype(BF16)
        dconv = dyv * b_ref[...]
        du = w[2:3, :] * dconv + w[1:2, :] * _shift_up(dconv, 1) + w[0:1, :] * _shift_up(dconv, 2)
        dc_ref[...] = (du * xv).astype(BF16)
        dx_ref[...] = (du * cv).astype(BF16)
        dw_ref[0:1, :] = jnp.sum(dconv * u2, axis=0, keepdims=True)
        dw_ref[1:2, :] = jnp.sum(dconv * u1, axis=0, keepdims=True)
        dw_ref[2:3, :] = jnp.sum(dconv * u, axis=0, keepdims=True)

    def col(off):
        return pl.BlockSpec((t, LANES), lambda j: (0, off + j))

    osp = pl.BlockSpec((t, LANES), lambda j: (0, j))
    wsp = pl.BlockSpec((3, LANES), lambda j: (0, j))
    return pl.pallas_call(
        body, name=name, grid=(nb,), in_specs=[col(0), col(nb), col(2 * nb), col(0), wsp],
        out_specs=[osp, osp, osp, wsp],
        out_shape=[jax.ShapeDtypeStruct((t, nb * LANES), BF16)] * 3 + [jax.ShapeDtypeStruct((3, nb * LANES), F32)],
        compiler_params=_cp())(pa, pa, pa, dy, conv_w)


def _sb_consts():
    j = lax.broadcasted_iota(jnp.int32, (SB_KEYS, SB_KEYS), 0)
    s = lax.broadcasted_iota(jnp.int32, (SB_KEYS, SB_KEYS), 1)
    after = (j > s).astype(BF16)
    upto = (j <= s).astype(BF16)
    before = (j < s).astype(BF16)
    return after, jnp.stack([upto, before])


def _log_sigmoid(z):
    return jnp.minimum(z, 0.0) - jnp.log(1.0 + jnp.exp(-jnp.abs(z)))


def _attn_fwd(pb, late_pack, seg_rows, *, name, tq=256):
    t = pb.shape[0]
    npair = pb.shape[1] // 3 // LANES
    tq = min(tq, t)
    nq = t // tq
    cmat, _ = _sb_consts()
    scale = 1.0 / math.sqrt(LANES // 2)

    nseg = len(seg_rows)

    def body(q_ref, k_ref, v_ref, c_ref, late_ref, y_ref, lt_ref, *rest):
        i = pl.program_id(1)
        pair = pl.program_id(0)
        scratch = rest[nseg:nseg + 4]
        start, forward, finish = _gather_phases(late_ref, rest[:nseg], seg_rows, *rest[nseg + 4:])
        pl.when((pair == 0) & (i == 0))(start)
        pl.when((pair == npair - 1) & (i == nq // 2))(forward)
        lane = lax.broadcasted_iota(jnp.int32, (tq, LANES), 1)
        rowpos = i * tq + lax.broadcasted_iota(jnp.int32, (tq, SB_KEYS), 0)
        colid = lax.broadcasted_iota(jnp.int32, (tq, SB_KEYS), 1)
        q2 = q_ref[...] * jnp.asarray(scale, BF16)
        cm = c_ref[...]
        hi_lanes = lane >= LANES // 2
        qhs = [jnp.where(hi_lanes == (hh == 1), q2, jnp.zeros_like(q2)) for hh in range(2)]
        per_q = tq // SB_KEYS

        def blk(jb):
            return pl.ds(pl.multiple_of(jb * SB_KEYS, SB_KEYS), SB_KEYS)

        zbuf, wbuf, accbuf, runbuf = scratch

        def scores(jb):
            kb = k_ref[blk(jb), :]
            for hh in range(2):
                zbuf[hh] = _dot(qhs[hh], kb, 1, 1)

        def values(jb):
            vb = v_ref[blk(jb), :]
            for hh in range(2):
                accbuf[hh] += _dot(wbuf[hh], vb)

        def trip(jb, masked, first=False):
            mask = (jb * SB_KEYS + colid) < rowpos if masked else None
            if not first:
                values(jb + 1)
            pre, css = [], []
            for hh in range(2):
                z = zbuf[hh]
                lb = _log_sigmoid(z)
                lk = lb - z
                if masked:
                    lk = jnp.where(mask, lk, 0.0)
                lk_hi, lk_lo = _split2(lk)
                css.append(_dot(lk_hi, cm) + _dot(lk_lo, cm))
                run = runbuf[hh]
                pre.append(lb + run)
                runbuf[hh] = run + jnp.sum(lk, axis=1, keepdims=True)
            scores(jnp.maximum(jb - 1, 0))
            for hh in range(2):
                w = jnp.exp(pre[hh] + css[hh])
                if masked:
                    w = jnp.where(mask, w, 0.0)
                wbuf[hh] = w.astype(BF16)

        nfull = i * per_q
        accbuf[...] = jnp.zeros_like(accbuf)
        runbuf[...] = jnp.zeros_like(runbuf)
        scores(nfull + per_q - 1)
        for dblk in reversed(range(per_q)):
            trip(nfull + dblk, True, first=dblk == per_q - 1)

        def full_block(n, carry):
            trip(nfull - 1 - n, False)
            return carry

        lax.fori_loop(0, nfull, full_block, 0)
        values(0)
        y_ref[...] = jnp.where(hi_lanes, accbuf[1], accbuf[0]).astype(BF16)
        lt_ref[...] = jnp.where(hi_lanes, runbuf[1], runbuf[0])
        pl.when((pair == npair - 1) & (i == nq - 1))(finish)

    return pl.pallas_call(
        body, name=name, grid=(npair, nq),
        in_specs=[pl.BlockSpec((tq, LANES), lambda p, i: (i, p)),
                  pl.BlockSpec((t, LANES), lambda p, i: (0, npair + p)),
                  pl.BlockSpec((t, LANES), lambda p, i: (0, 2 * npair + p)),
                  pl.BlockSpec((SB_KEYS, SB_KEYS), lambda p, i: (0, 0)),
                  HBM_SPEC],
        out_specs=[pl.BlockSpec((tq, LANES), lambda p, i: (i, p))] * 2 + [HBM_SPEC] * nseg,
        out_shape=[jax.ShapeDtypeStruct((t, npair * LANES), BF16), jax.ShapeDtypeStruct((t, npair * LANES), F32),
                   ] + [jax.ShapeDtypeStruct((8, n, late_pack.shape[1]), late_pack.dtype) for n in seg_rows],
        scratch_shapes=[pltpu.VMEM((2, tq, SB_KEYS), F32), pltpu.VMEM((2, tq, SB_KEYS), BF16),
                        pltpu.VMEM((2, tq, LANES), F32), pltpu.VMEM((2, tq, 1), F32)] + _gather_scratch(),
        compiler_params=_cp(dimension_semantics=("arbitrary", "arbitrary")))(pb, pb, pb, cmat, late_pack)


def _attn_bwd(pb, dy, ltot, chip_part, *, name, tq=256):
    t = pb.shape[0]
    npair = pb.shape[1] // 3 // LANES
    tq = min(tq, t)
    nq = t // tq
    _, cmats = _sb_consts()
    scale = 1.0 / math.sqrt(LANES // 2)

    def body(q_ref, k_ref, v_ref, dy_ref, lt_ref, c_ref, part_ref, dq_ref, dk_ref, dv_ref, parts_ref, dk_acc, dv_acc, *rest):
        i = pl.program_id(1)
        pair = pl.program_id(0)
        scratch = rest[:6]
        start, finish = _chip_exchange_phases(part_ref, parts_ref, *rest[6:])
        pl.when((pair == 0) & (i == 0))(start)

        @pl.when(i == 0)
        def _():
            dk_acc[...] = jnp.zeros_like(dk_acc)
            dv_acc[...] = jnp.zeros_like(dv_acc)

        lane = lax.broadcasted_iota(jnp.int32, (tq, LANES), 1)
        rowpos = i * tq + lax.broadcasted_iota(jnp.int32, (tq, SB_KEYS), 0)
        colid = lax.broadcasted_iota(jnp.int32, (tq, SB_KEYS), 1)
        q2 = q_ref[...] * jnp.asarray(scale, BF16)
        do2 = dy_ref[...].astype(BF16)
        ltv = lt_ref[...]
        c_upto, c_before = c_ref[0], c_ref[1]
        hi_lanes = lane >= LANES // 2
        sels = [hi_lanes == (hh == 1) for hh in range(2)]
        qhs = [jnp.where(s, q2, jnp.zeros_like(q2)) for s in sels]
        dohs = [jnp.where(s, do2, jnp.zeros_like(do2)) for s in sels]
        lts = [ltv[:, 0:1], ltv[:, LANES // 2:LANES // 2 + 1]]
        per_q = tq // SB_KEYS

        def blk(jb):
            return pl.ds(pl.multiple_of(jb * SB_KEYS, SB_KEYS), SB_KEYS)

        zbuf, dabuf, dzbuf, abuf, dqbuf, sumbuf = scratch

        def scores(jb):
            kb, vb = k_ref[blk(jb), :], v_ref[blk(jb), :]
            for hh in range(2):
                zbuf[hh] = _dot(qhs[hh], kb, 1, 1)
                dabuf[hh] = _dot(dohs[hh], vb, 1, 1)

        def products(jb):
            kb = k_ref[blk(jb), :]
            dk_acc[blk(jb), :] += _dot(dzbuf[0], qhs[0], 0, 0) + _dot(dzbuf[1], qhs[1], 0, 0)
            dv_acc[blk(jb), :] += _dot(abuf[0], dohs[0], 0, 0) + _dot(abuf[1], dohs[1], 0, 0)
            for hh in range(2):
                dqbuf[hh] += _dot(dzbuf[hh], kb)

        def trip(jb, masked):
            mask = (jb * SB_KEYS + colid) < rowpos if masked else None
            products(jnp.maximum(jb - 1, 0))
            lbs, css, es, ces = [], [], [], []
            for hh in range(2):
                z = zbuf[hh]
                lb = _log_sigmoid(z)
                lk = lb - z
                if masked:
                    lk = jnp.where(mask, lk, 0.0)
                lk_hi, lk_lo = _split2(lk)
                css.append(_dot(lk_hi, c_upto) + _dot(lk_lo, c_upto))
                csum = sumbuf[2 * hh]
                lbs.append((lb, lb + (lts[hh] - csum)))
                sumbuf[2 * hh] = csum + jnp.sum(lk, axis=1, keepdims=True)
            for hh in range(2):
                a = jnp.exp(lbs[hh][1] - css[hh])
                if masked:
                    a = jnp.where(mask, a, 0.0)
                e = a * dabuf[hh]
                e_hi, e_lo = _split2(e)
                ces.append(_dot(e_hi, c_before) + _dot(e_lo, c_before))
                abuf[hh] = a.astype(BF16)
                es.append(e)
            scores(jnp.minimum(jb + 1, last))
            for hh in range(2):
                prun = sumbuf[2 * hh + 1]
                beta = jnp.exp(lbs[hh][0])
                dz = es[hh] * (1.0 - beta) - (prun + ces[hh]) * beta
                if masked:
                    dz = jnp.where(mask, dz, 0.0)
                dzbuf[hh] = dz.astype(BF16)
                sumbuf[2 * hh + 1] = prun + jnp.sum(es[hh], axis=1, keepdims=True)

        nfull = i * per_q
        last = nfull + per_q - 1
        for buf in (dzbuf, abuf, dqbuf, sumbuf):
            buf[...] = jnp.zeros_like(buf)
        scores(0)

        def full_block(jb, carry):
            trip(jb, False)
            return carry

        lax.fori_loop(0, nfull, full_block, 0)
        for dblk in range(per_q):
            trip(nfull + dblk, True)
        products(last)
        dq_ref[...] = (jnp.where(hi_lanes, dqbuf[1], dqbuf[0]) * scale).astype(BF16)

        @pl.when(i == nq - 1)
        def _():
            dk_ref[...] = dk_acc[...].astype(BF16)
            dv_ref[...] = dv_acc[...].astype(BF16)

        pl.when((pair == npair - 1) & (i == nq - 1))(finish)

    blk = pl.BlockSpec((tq, LANES), lambda p, i: (i, p))
    full = pl.BlockSpec((t, LANES), lambda p, i: (0, p))
    return pl.pallas_call(
        body, name=name, grid=(npair, nq),
        in_specs=[blk,
                  pl.BlockSpec((t, LANES), lambda p, i: (0, npair + p)),
                  pl.BlockSpec((t, LANES), lambda p, i: (0, 2 * npair + p)),
                  pl.BlockSpec((tq, LANES), lambda p, i: (i, npair + p)),
                  blk,
                  pl.BlockSpec((2, SB_KEYS, SB_KEYS), lambda p, i: (0, 0, 0)),
                  HBM_SPEC],
        out_specs=[blk, full, full, HBM_SPEC],
        out_shape=[jax.ShapeDtypeStruct((t, npair * LANES), BF16)] * 3 + [jax.ShapeDtypeStruct(chip_part.shape, chip_part.dtype)],
        scratch_shapes=[pltpu.VMEM((t, LANES), F32), pltpu.VMEM((t, LANES), F32),
                        pltpu.VMEM((2, tq, SB_KEYS), F32), pltpu.VMEM((2, tq, SB_KEYS), F32),
                        pltpu.VMEM((2, tq, SB_KEYS), BF16), pltpu.VMEM((2, tq, SB_KEYS), BF16),
                        pltpu.VMEM((2, tq, LANES), F32), pltpu.VMEM((4, tq, 1), F32)] + _chip_exchange_scratch(),
        compiler_params=_cp(dimension_semantics=("arbitrary", "arbitrary")))(pb, pb, pb, dy, ltot, cmats, chip_part)


def _hgrn_consts():
    t = lax.broadcasted_iota(jnp.int32, (CHUNK, CHUNK), 0)
    s = lax.broadcasted_iota(jnp.int32, (CHUNK, CHUNK), 1)
    masks = []
    for lvl in range(N_LEVELS):
        half = CHUNK >> (lvl + 1)
        same = (t // (2 * half)) == (s // (2 * half))
        masks.append((same & (t % (2 * half) >= half) & (s % (2 * half) < half)).astype(F32))
    masks.append((t == s).astype(F32))
    prefix = (s <= t).astype(BF16)
    suffix = (s >= t).astype(BF16)
    return prefix, jnp.stack(masks), suffix


def _hgrn_gates(qr, fr, lbv):
    sg = _sigmoid(fr)
    fval = lbv + (1.0 - lbv) * sg
    kk = (1.0 - lbv) * _sigmoid(-fr)
    sq = _sigmoid(qr)
    return sg, fval, jnp.log(fval), kk, sq, qr * sq


def _lower_bound(c_ref):
    c = c_ref[...]
    mx = jnp.max(c, axis=0, keepdims=True)
    ex = jnp.exp(c - mx)
    return ex[1:2, :] / jnp.sum(ex, axis=0, keepdims=True)


def _level_ref(b, lvl):
    half = CHUNK >> (lvl + 1)
    seg = 2 * half
    if seg >= 8:
        b3 = b.reshape(CHUNK // seg, seg, LANES)
        return jnp.broadcast_to(b3[:, half - 1:half, :], b3.shape).reshape(CHUNK, LANES)
    pos = lax.broadcasted_iota(jnp.int32, b.shape, 0) % seg
    out = b
    for p in range(seg):
        if p != half - 1:
            out = jnp.where(pos == p, pltpu.roll(b, (p - (half - 1)) % CHUNK, 0), out)
    return out


def _hgrn_levels(b, qs, kk):
    out = []
    for lvl in range(N_LEVELS):
        fac = jnp.exp(-jnp.abs(b - _level_ref(b, lvl)))
        out.append((qs * fac, kk * fac, fac, fac))
    out.append((qs, kk, None, None))
    return out


def _split2(x):
    hi = x.astype(BF16)
    return hi, (x - hi.astype(F32)).astype(BF16)


def _hgrn_fwd(pc, c_lb, out_norm, *, name, tc=512):
    t = pc.shape[0]
    nh = pc.shape[1] // 4 // LANES
    tc = min(tc, t)
    nch = tc // CHUNK
    cum_all, masks, _ = _hgrn_consts()

    def body(q_ref, f_ref, i_ref, g_ref, lb_ref, on_ref, cum_ref, m_ref, y_ref, o_ref, st_ref, state):
        @pl.when(pl.program_id(1) == 0)
        def _():
            state[...] = jnp.zeros_like(state)

        lbv = _lower_bound(lb_ref)
        onv = on_ref[...]

        def chunk(c, carry):
            rows = pl.ds(pl.multiple_of(c * CHUNK, CHUNK), CHUNK)
            for hh in range(HGRN_HEADS):
                lanes = slice(hh * LANES, (hh + 1) * LANES)
                _, _, g, kk, _, qs = _hgrn_gates(q_ref[rows, lanes], f_ref[rows, lanes], lbv[:, lanes])
                vb = i_ref[rows, lanes].astype(BF16)
                b = _dot_exact_lhs(cum_ref[...], g)
                scores = jnp.zeros((CHUNK, CHUNK), F32)
                for lvl, (ql, kl, _, _) in enumerate(_hgrn_levels(b, qs, kk)):
                    scores = scores + _dot(ql.astype(BF16), kl.astype(BF16), 1, 1) * m_ref[lvl]
                st = state[hh]
                st_ref[hh, c] = st
                o = _dot(scores.astype(BF16), vb) + _dot((qs * jnp.exp(b)).astype(BF16), st.astype(BF16), 1, 1)
                blast = b[CHUNK - 1:CHUNK, :]
                kdec = (kk * jnp.exp(blast - b)).astype(BF16)
                state[hh] = st * jnp.exp(blast) + _dot(vb, kdec, 0, 0)
                o_ref[rows, lanes] = o
                rstd = lax.rsqrt(jnp.mean(o * o, axis=-1, keepdims=True) + RMS_EPS)
                gate = g_ref[rows, lanes]
                y_ref[rows, lanes] = (o * rstd * onv * (gate * _sigmoid(gate))).astype(BF16)
            return carry

        lax.fori_loop(0, nch, chunk, 0, unroll=2)

    hw = HGRN_HEADS * LANES

    def col(off):
        return pl.BlockSpec((tc, hw), lambda h, i: (i, off // HGRN_HEADS + h))

    osp = pl.BlockSpec((tc, hw), lambda h, i: (i, h))
    return pl.pallas_call(
        body, name=name, grid=(nh // HGRN_HEADS, t // tc),
        in_specs=[col(0), col(nh), col(2 * nh), col(3 * nh),
                  pl.BlockSpec((2, hw), lambda h, i: (0, h)),
                  pl.BlockSpec((1, LANES), lambda h, i: (0, 0)),
                  pl.BlockSpec(cum_all.shape, lambda h, i: (0, 0)),
                  pl.BlockSpec(masks.shape, lambda h, i: (0, 0, 0))],
        out_specs=[osp, osp, pl.BlockSpec((HGRN_HEADS, nch, LANES, LANES), lambda h, i: (h, i, 0, 0))],
        out_shape=[jax.ShapeDtypeStruct((t, nh * LANES), BF16), jax.ShapeDtypeStruct((t, nh * LANES), F32),
                   jax.ShapeDtypeStruct((nh, t // CHUNK, LANES, LANES), F32)],
        scratch_shapes=[pltpu.VMEM((HGRN_HEADS, LANES, LANES), F32)],
        compiler_params=_cp())(pc, pc, pc, pc, c_lb, out_norm, cum_all, masks)


def _hgrn_bwd(pc, o_saved, states, dy, c_lb, out_norm, *, name, tc=512):
    t = pc.shape[0]
    nh = pc.shape[1] // 4 // LANES
    tc = min(tc, t)
    nch = tc // CHUNK
    nt = t // tc
    cum_all, masks, suffix = _hgrn_consts()

    def body(q_ref, f_ref, i_ref, g_ref, o_ref, st_ref, dy_ref, lb_ref, on_ref, cum_ref, m_ref, suf_ref,
             dq_ref, df_ref, di_ref, dg_ref, dlb_ref, don_ref, dstate):
        @pl.when(pl.program_id(1) == 0)
        def _():
            dstate[...] = jnp.zeros_like(dstate)
            dlb_ref[...] = jnp.zeros_like(dlb_ref)
            don_ref[...] = jnp.zeros_like(don_ref)

        lbv = _lower_bound(lb_ref)
        onv = on_ref[...]

        def head(hh, c, rows):
            lanes = slice(hh * LANES, (hh + 1) * LANES)
            qr = q_ref[rows, lanes]
            sg, fval, g, kk, sq, qs = _hgrn_gates(qr, f_ref[rows, lanes], lbv[:, lanes])
            vb = i_ref[rows, lanes].astype(BF16)
            o = o_ref[rows, lanes]
            gate = g_ref[rows, lanes]
            sgt = _sigmoid(gate)
            rstd = lax.rsqrt(jnp.mean(o * o, axis=-1, keepdims=True) + RMS_EPS)
            ohat = o * rstd
            dyv = dy_ref[rows, lanes]
            don = dyv * (gate * sgt)
            dg_ref[rows, lanes] = (dyv * ohat * onv * (sgt * (1.0 + gate * (1.0 - sgt)))).astype(BF16)
            don_ref[:, lanes] += jnp.sum(don * ohat, axis=0, keepdims=True)
            dxhat = don * onv
            dob = (rstd * (dxhat - ohat * jnp.mean(dxhat * ohat, axis=-1, keepdims=True))).astype(BF16)
            b = _dot_exact_lhs(cum_ref[...], g)
            blast = b[CHUNK - 1:CHUNK, :]
            eb = jnp.exp(b)
            edec = jnp.exp(blast - b)
            st32 = st_ref[hh, c]
            st = st32.astype(BF16)
            dst = dstate[hh]
            dstb = dst.astype(BF16)
            da = _dot(dob, vb, 1, 1)
            levels = _hgrn_levels(b, qs, kk)
            scores = jnp.zeros((CHUNK, CHUNK), F32)
            dq = eb * _dot(dob, st)
            dk_inter = edec * _dot(vb, dstb)
            dk = dk_inter
            for lvl, (ql, kl, eq, ek) in enumerate(levels):
                mk = m_ref[lvl]
                (qh, qlo), (kh, klo) = _split2(ql), _split2(kl)
                scores = scores + _dot(qh, kh, 1, 1) * mk
                dal = (da * mk).astype(BF16)
                dql = _dot(dal, kh) + _dot(dal, klo)
                dkl = _dot(dal, qh, 0, 0) + _dot(dal, qlo, 0, 0)
                dq = dq + (dql if eq is None else dql * eq)
                dk = dk + (dkl if ek is None else dkl * ek)
            kdec = (kk * edec).astype(BF16)
            dv = _dot(scores.astype(BF16), dob, 0, 0) + _dot(kdec, dstb, 1, 1)
            dstate[hh] = dst * jnp.exp(blast) + _dot(dob, (qs * eb).astype(BF16), 0, 0)
            db = qs * dq - kk * dk
            last = jnp.sum(kk * dk_inter, axis=0, keepdims=True) + jnp.exp(blast) * jnp.sum(dst * st32, axis=0, keepdims=True)
            dgl = _dot_exact_lhs(suf_ref[...], db) + last
            dfv = dgl / fval - dk
            df_ref[rows, lanes] = (dfv * (1.0 - lbv[:, lanes]) * sg * (1.0 - sg)).astype(BF16)
            dlb_ref[:, lanes] += jnp.sum(dfv * (1.0 - sg), axis=0, keepdims=True)
            dq_ref[rows, lanes] = (dq * (sq * (1.0 + qr * (1.0 - sq)))).astype(BF16)
            di_ref[rows, lanes] = dv.astype(BF16)

        def chunk(n, carry):
            c = nch - 1 - n
            rows = pl.ds(pl.multiple_of(c * CHUNK, CHUNK), CHUNK)
            for hh in range(HGRN_HEADS):
                head(hh, c, rows)
            return carry

        lax.fori_loop(0, nch, chunk, 0, unroll=2)

    hw = HGRN_HEADS * LANES

    def col(off):
        return pl.BlockSpec((tc, hw), lambda h, i: (nt - 1 - i, off // HGRN_HEADS + h))

    osp = pl.BlockSpec((tc, hw), lambda h, i: (nt - 1 - i, h))
    vec = pl.BlockSpec((1, hw), lambda h, i: (0, h))
    return pl.pallas_call(
        body, name=name, grid=(nh // HGRN_HEADS, nt),
        in_specs=[col(0), col(nh), col(2 * nh), col(3 * nh), osp,
                  pl.BlockSpec((HGRN_HEADS, nch, LANES, LANES), lambda h, i: (h, nt - 1 - i, 0, 0)),
                  osp,
                  pl.BlockSpec((2, hw), lambda h, i: (0, h)),
                  pl.BlockSpec((1, LANES), lambda h, i: (0, 0)),
                  pl.BlockSpec(cum_all.shape, lambda h, i: (0, 0)),
                  pl.BlockSpec(masks.shape, lambda h, i: (0, 0, 0)),
                  pl.BlockSpec(suffix.shape, lambda h, i: (0, 0))],
        out_specs=[osp, osp, osp, osp, vec, vec],
        out_shape=[jax.ShapeDtypeStruct((t, nh * LANES), BF16)] * 4 + [jax.ShapeDtypeStruct((1, nh * LANES), F32)] * 2,
        scratch_shapes=[pltpu.VMEM((HGRN_HEADS, LANES, LANES), F32)],
        compiler_params=_cp())(pc, pc, pc, pc, o_saved, states, dy, c_lb, out_norm, cum_all, masks, suffix)


HBM_SPEC = pl.BlockSpec(memory_space=pltpu.HBM)


def _gather_scratch():
    return [pltpu.SemaphoreType.DMA((7,)), pltpu.SemaphoreType.DMA((7,)), pltpu.SemaphoreType.DMA]


def _gather_phases(x_ref, out_refs, seg_rows, send_sems, recv_sems, local_sem):
    x, y, c = lax.axis_index("x"), lax.axis_index("y"), lax.axis_index("c")
    me, sibling = (x, y, c), (x, y, 1 - c)
    chips = [(1 - x, y), (x, 1 - y), (1 - x, 1 - y)]
    offs = [sum(seg_rows[:s]) for s in range(len(seg_rows))]
    assert sum(seg_rows) == x_ref.shape[0]

    def index(px, py, pc):
        return 4 * px + 2 * py + pc

    def copies(k, block, to, own):
        return [pltpu.make_async_remote_copy(
            src_ref=x_ref.at[pl.ds(offs[s], n)] if own else out_refs[s].at[index(*block)],
            dst_ref=out_refs[s].at[index(*block)],
            send_sem=send_sems.at[k], recv_sem=recv_sems.at[k], device_id=to, device_id_type=MESH)
            for s, n in enumerate(seg_rows)]

    def all_bytes(k):
        return pltpu.make_async_remote_copy(src_ref=x_ref, dst_ref=x_ref, send_sem=send_sems.at[k],
                                            recv_sem=recv_sems.at[k], device_id=me, device_id_type=MESH)

    mine = [pltpu.make_async_copy(x_ref.at[pl.ds(offs[s], n)], out_refs[s].at[index(*me)], local_sem)
            for s, n in enumerate(seg_rows)]
    first = copies(0, me, sibling, True)
    for j, chip in enumerate(chips):
        first += copies(1 + j, me, (*chip, c), True)

    def start():
        for cp in mine + first:
            cp.start()

    def forward():
        for j, chip in enumerate(chips):
            all_bytes(1 + j).wait_recv()
            for cp in copies(4 + j, (*chip, c), sibling, False):
                cp.start()

    def finish():
        all_bytes(0).wait_recv()
        for j in range(3):
            all_bytes(4 + j).wait_recv()
        for k in range(7):
            all_bytes(k).wait_send()
        pltpu.make_async_copy(x_ref, x_ref, local_sem).wait()

    return start, forward, finish


def _all_gather(xs, seg_rows=None, *, name):
    segs = [xs.shape[0]] if seg_rows is None else list(seg_rows)

    def body(x_ref, *rest):
        start, forward, finish = _gather_phases(x_ref, rest[:len(segs)], segs, *rest[len(segs):])
        start()
        forward()
        finish()

    outs = pl.pallas_call(
        body, name=name, in_specs=[HBM_SPEC], out_specs=[HBM_SPEC] * len(segs),
        out_shape=[jax.ShapeDtypeStruct((8, n, xs.shape[1]), xs.dtype) for n in segs],
        scratch_shapes=_gather_scratch())(xs)
    return outs[0] if seg_rows is None else outs


def _sibling_exchange(s, *, name):
    def body(s_ref, rb_ref, send_sem, recv_sem):
        x, y, c = lax.axis_index("x"), lax.axis_index("y"), lax.axis_index("c")
        cp = pltpu.make_async_remote_copy(
            src_ref=s_ref.at[:, 1 - c], dst_ref=rb_ref, send_sem=send_sem, recv_sem=recv_sem,
            device_id=(x, y, 1 - c), device_id_type=MESH)
        cp.start()
        cp.wait()

    return pl.pallas_call(
        body, name=name, in_specs=[HBM_SPEC], out_specs=HBM_SPEC,
        out_shape=jax.ShapeDtypeStruct(s.shape[:1] + s.shape[2:], s.dtype),
        scratch_shapes=[pltpu.SemaphoreType.DMA, pltpu.SemaphoreType.DMA])(s)


def _row_tile(n, cap=1024):
    return max(b for b in range(16, cap + 1, 16) if n % b == 0)


def _pair_add(s, rb, core, *, name):
    nchip, _, r, c = s.shape
    tb = _row_tile(r)

    def body(core_ref, a_ref, b_ref, o_ref):
        o_ref[...] = (a_ref[...].astype(F32) + b_ref[...].astype(F32)).astype(BF16)

    blk = pl.BlockSpec((None, tb, c), lambda ch, i, cr: (ch, i, 0))
    return pl.pallas_call(
        body, name=name,
        grid_spec=pltpu.PrefetchScalarGridSpec(
            num_scalar_prefetch=1, grid=(nchip, r // tb),
            in_specs=[pl.BlockSpec((None, None, tb, c), lambda ch, i, cr: (ch, cr[0], i, 0)), blk],
            out_specs=blk),
        out_shape=jax.ShapeDtypeStruct((nchip, r, c), BF16), compiler_params=_cp())(core, s, rb)


def _chip_exchange_scratch():
    return [pltpu.SemaphoreType.DMA((3,)), pltpu.SemaphoreType.DMA((3,)), pltpu.SemaphoreType.DMA]


def _chip_exchange_phases(p_ref, out_ref, send_sems, recv_sems, local_sem):
    x, y, c = lax.axis_index("x"), lax.axis_index("y"), lax.axis_index("c")
    mine = 2 * x + y
    own = pltpu.make_async_copy(p_ref.at[mine], out_ref.at[mine], local_sem)
    copies = [pltpu.make_async_remote_copy(
        src_ref=p_ref.at[2 * tx + ty], dst_ref=out_ref.at[mine],
        send_sem=send_sems.at[k], recv_sem=recv_sems.at[k], device_id=(tx, ty, c), device_id_type=MESH)
        for k, (tx, ty) in enumerate([(1 - x, y), (x, 1 - y), (1 - x, 1 - y)])]

    def start():
        own.start()
        for cp in copies:
            cp.start()

    def finish():
        for cp in copies:
            cp.wait()
        own.wait()

    return start, finish


def _adamw_math(w, g, m, v):
    m2 = ADAM_B1 * m + (1.0 - ADAM_B1) * g
    v2 = ADAM_B2 * v + (1.0 - ADAM_B2) * (g * g)
    m_hat = m2 / (1.0 - ADAM_B1 ** ADAM_STEP)
    v_hat = v2 / (1.0 - ADAM_B2 ** ADAM_STEP)
    return -ADAM_LR * (m_hat / (jnp.sqrt(v_hat) + ADAM_EPS) + ADAM_WD * w), m2, v2


def _grad_sum(parts, *, name):
    _, r, c = parts.shape
    tb = _row_tile(r)

    def body(p0, p1, p2, p3, g_out):
        g_out[...] = ((p0[...].astype(F32) + p1[...].astype(F32)) + p2[...].astype(F32)) + p3[...].astype(F32)

    def part(ch):
        return pl.BlockSpec((None, tb, c), lambda i: (ch, i, 0))

    return pl.pallas_call(
        body, name=name, grid=(r // tb,), in_specs=[part(0), part(1), part(2), part(3)],
        out_specs=pl.BlockSpec((tb, c), lambda i: (i, 0)), out_shape=jax.ShapeDtypeStruct((r, c), F32),
        compiler_params=_cp())(parts, parts, parts, parts)


def _adamw_shard(g, g_off, w, m, v, layer, prev, *, name):
    _, r, c = w.shape
    tb = next(b for b in range(min(r, 512), 0, -8) if r % b == 0 and g_off % b == 0)

    def body(g_ref, w_ref, m_ref, v_ref, *rest):
        d_out, m_out, v_out = rest[-3:]
        d, m2, v2 = _adamw_math(w_ref[...], g_ref[...], m_ref[...], v_ref[...])
        d_out[...] = d
        m_out[...] = m2
        v_out[...] = v2

    blk = pl.BlockSpec((None, tb, c), lambda i: (layer, i, 0))
    prev = list(prev) if prev is not None else []
    return pl.pallas_call(
        body, name=name, grid=(r // tb,),
        in_specs=[pl.BlockSpec((tb, c), lambda i: (g_off // tb + i, 0)), blk, blk, blk] + [pl.BlockSpec(memory_space=pl.ANY)] * len(prev),
        out_specs=[blk] * 3, out_shape=[jax.ShapeDtypeStruct(w.shape, F32)] * 3,
        input_output_aliases={4 + k: k for k in range(len(prev))},
        compiler_params=_cp())(g, w, m, v, *prev)


SLOT = 8
SMALL_ROWS = 6 * SLOT
ROW_LB = 4 * SLOT


def _small_update(gath, w, m, v, *, name):
    def body(g_ref, w_ref, m_ref, v_ref, g_out, d_out, m_out, v_out):
        tot = g_ref[0]
        for k in range(1, 8):
            tot = tot + g_ref[k]
        wv = w_ref[...]
        c0, c1 = wv[ROW_LB:ROW_LB + 1, :], wv[ROW_LB + 1:ROW_LB + 2, :]
        mx = jnp.maximum(c0, c1)
        e0, e1 = jnp.exp(c0 - mx), jnp.exp(c1 - mx)
        lb = e1 / (e0 + e1)
        gl = tot[ROW_LB:ROW_LB + 1, :] * lb * (1.0 - lb)
        row = lax.broadcasted_iota(jnp.int32, tot.shape, 0)
        g = jnp.where(row == ROW_LB, -gl, jnp.where(row == ROW_LB + 1, gl, tot))
        d, m2, v2 = _adamw_math(wv, g, m_ref[...], v_ref[...])
        g_out[...] = g
        d_out[...] = d
        m_out[...] = m2
        v_out[...] = v2

    return pl.pallas_call(
        body, name=name, out_shape=[jax.ShapeDtypeStruct(w.shape, F32)] * 4, compiler_params=_cp())(gath, w, m, v)


D_MODEL = 1024


def _ffn_fwd(h, gain, wg, wu, wd, tag):
    xn, gg, uu, act = _norm_gate_up(h, gain, wg, wu, name=f"{tag}_gate_up")
    out = _mm([(act, wd)], residual=h, alpha=MACARON, tn=1024, name=f"{tag}_down")
    return out, (h, xn, gg, uu, act)


def _ffn_input_bwd(dg, du, wg, wu, x, gain, dres, chip_part, *, name, scale, tm=256):
    t, d = x.shape
    f = wg.shape[0]
    tm = min(tm, t)
    nt = t // tm
    fused = chip_part is not None

    def body(dg_ref, du_ref, wg_ref, wu_ref, x_ref, g_ref, dres_ref, *rest):
        if fused:
            part_ref, dx_ref, dxb_ref, dgain_ref, parts_ref = rest[:5]
            start, finish = _chip_exchange_phases(part_ref, parts_ref, *rest[5:])
            pl.when(pl.program_id(0) == 0)(start)
        else:
            dx_ref, dxb_ref, dgain_ref = rest
        dxn_v = _dot(dg_ref[...], wg_ref[...]) + _dot(du_ref[...], wu_ref[...])
        xv = x_ref[...]
        rstd = lax.rsqrt(jnp.mean(xv * xv, axis=-1, keepdims=True) + RMS_EPS)
        xhat = xv * rstd
        dxhat = dxn_v * g_ref[...]
        dx = dres_ref[...] + rstd * (dxhat - xhat * jnp.mean(dxhat * xhat, axis=-1, keepdims=True))
        dx_ref[...] = dx
        dxb_ref[...] = (dx * scale).astype(BF16)

        @pl.when(pl.program_id(0) == 0)
        def _():
            dgain_ref[...] = jnp.zeros_like(dgain_ref)

        dgain_ref[...] += jnp.sum(dxn_v * xhat, axis=0, keepdims=True)
        if fused:
            pl.when(pl.program_id(0) == nt - 1)(finish)

    wide = pl.BlockSpec((tm, f), lambda i: (i, 0))
    wsp = pl.BlockSpec((f, d), lambda i: (0, 0))
    row = pl.BlockSpec((tm, d), lambda i: (i, 0))
    vec = pl.BlockSpec((1, d), lambda i: (0, 0))
    args = [dg, du, wg, wu, x, gain, dres] + ([chip_part] if fused else [])
    return pl.pallas_call(
        body, name=name, grid=(nt,),
        in_specs=[wide, wide, wsp, wsp, row, vec, row] + ([HBM_SPEC] if fused else []),
        out_specs=[row, row, vec] + ([HBM_SPEC] if fused else []),
        out_shape=[jax.ShapeDtypeStruct((t, d), F32), jax.ShapeDtypeStruct((t, d), BF16), jax.ShapeDtypeStruct((1, d), F32)]
        + ([jax.ShapeDtypeStruct(chip_part.shape, chip_part.dtype)] if fused else []),
        scratch_shapes=_chip_exchange_scratch() if fused else [],
        compiler_params=_cp(dimension_semantics=("arbitrary",)))(*args)


def _ffn_bwd(dout, dout_half, saved, gain, wg, wu, wd, tag, next_scale, make_chip_part=None, early_chip_part=None):
    h, xn, gg, uu, act = saved
    dg, du, *early_parts = _swiglu_bwd(dout_half, wd, gg, uu, early_chip_part, name=f"{tag}_dact")
    dwd = _mm([(act, dout_half)], ta=True, tm=256, tn=1024, out_dtype=BF16, name=f"{tag}_dwd")
    dwg, dwu = _mm_shared_rhs([dg, du], xn, tm=256, name=f"{tag}_dwgu")
    chip_part = make_chip_part(dwg, dwu, dwd) if make_chip_part is not None else None
    dh, dh_b, dgain, *parts = _ffn_input_bwd(dg, du, wg, wu, h, gain, dout, chip_part, scale=next_scale,
                                             name=f"{tag}_input_bwd")
    return dh, dh_b, dwg, dwu, dwd, dgain, (parts[0] if parts else None), (early_parts[0] if early_parts else None)


def kernel(x, ffn_pre_norm, ffn_pre_w_gate, ffn_pre_w_up, ffn_pre_w_down, mix_norm, ffn_post_norm, ffn_post_w_gate, ffn_post_w_up, ffn_post_w_down, ab_w_in, ab_conv_w, ab_w_out, c_w_in, c_lower_bounds, c_out_norm, c_w_out, final_norm, loss_target, m_ffn_pre_norm, m_ffn_pre_w_gate, m_ffn_pre_w_up, m_ffn_pre_w_down, m_mix_norm, m_ffn_post_norm, m_ffn_post_w_gate, m_ffn_post_w_up, m_ffn_post_w_down, m_ab_w_in, m_ab_conv_w, m_ab_w_out, m_c_w_in, m_c_lower_bounds, m_c_out_norm, m_c_w_out, m_final_norm, v_ffn_pre_norm, v_ffn_pre_w_gate, v_ffn_pre_w_up, v_ffn_pre_w_down, v_mix_norm, v_ffn_post_norm, v_ffn_post_w_gate, v_ffn_post_w_up, v_ffn_post_w_down, v_ab_w_in, v_ab_conv_w, v_ab_w_out, v_c_w_in, v_c_lower_bounds, v_c_out_norm, v_c_w_out, v_final_norm):
    d = D_MODEL
    h0 = x[0]
    target = loss_target[0]
    core = lax.axis_index("c").astype(jnp.int32).reshape(1)

    big = [("pre_g", ffn_pre_w_gate, m_ffn_pre_w_gate, v_ffn_pre_w_gate),
           ("pre_u", ffn_pre_w_up, m_ffn_pre_w_up, v_ffn_pre_w_up),
           ("pre_d", ffn_pre_w_down, m_ffn_pre_w_down, v_ffn_pre_w_down),
           ("post_g", ffn_post_w_gate, m_ffn_post_w_gate, v_ffn_post_w_gate),
           ("post_u", ffn_post_w_up, m_ffn_post_w_up, v_ffn_post_w_up),
           ("post_d", ffn_post_w_down, m_ffn_post_w_down, v_ffn_post_w_down),
           ("ab_in", ab_w_in, m_ab_w_in, v_ab_w_in),
           ("ab_out", ab_w_out, m_ab_w_out, v_ab_w_out),
           ("c_in", c_w_in, m_c_w_in, v_c_w_in),
           ("c_out", c_w_out, m_c_w_out, v_c_w_out)]
    by_tag = {tag: (w, m, v) for tag, w, m, v in big}

    def layer_rows(tag):
        w = by_tag[tag][0]
        return w.size // d // w.shape[0]

    def layout(items):
        offs, off = {}, 0
        for item in items:
            offs[item] = off
            off += layer_rows(item[0])
        return offs, off

    ffn = [f"{pos}_{kind}" for pos in ("pre", "post") for kind in "gud"]
    first_items = [("pre_g", 0), ("pre_u", 0)]
    early_items = [("pre_d", 0), ("ab_in", 0)]
    late_items = ([("pre_g", 1), ("pre_u", 1), ("pre_d", 1)] + [(f"post_{kind}", l) for l in (0, 1) for kind in "gud"]
                  + [("ab_out", 0), ("c_in", 0), ("c_out", 0)])
    grad_items = {"A": ([(tag, 1) for tag in ffn] + [(f"post_{kind}", 0) for kind in "gud"]
                        + [("c_in", 0), ("c_out", 0), ("ab_out", 0)]),
                  "B": [(f"pre_{kind}", 0) for kind in "gud"], "C": [("ab_in", 0)]}
    grad_offs = {k: layout(items)[0] for k, items in grad_items.items()}
    grad_conv_row = layout(grad_items["C"])[1]

    def conv_rows(a, split):
        flat = a.reshape(-1)
        if split:
            hi = flat.astype(BF16)
            flat = jnp.concatenate([hi, (flat - hi.astype(F32)).astype(BF16)])
        return jnp.zeros((16, d), flat.dtype).at[0, :flat.shape[0]].set(flat)

    nconv = ab_conv_w.size
    col_sharded = {"pre_g", "pre_u", "post_g", "post_u", "ab_in", "c_in"}

    def pack_rows(item):
        tag, layer = item
        a = by_tag[tag][0][layer]
        return (a.T if tag in col_sharded else a).reshape(-1, d).astype(BF16)

    first_pack = jnp.concatenate([pack_rows(item) for item in first_items], axis=0)
    early_pack = jnp.concatenate([pack_rows(item) for item in early_items] + [conv_rows(ab_conv_w, True)], axis=0)
    late_pack = jnp.concatenate([pack_rows(item) for item in late_items], axis=0)
    first_w = _all_gather(first_pack, [layer_rows(tag) for tag, _ in first_items], name="gather_first_weights")
    full = {item: g.reshape(-1, d) for item, g in zip(first_items, first_w)}

    xn0, gg0, uu0, act0, *early_w = _norm_gate_up(
        h0, ffn_pre_norm[0:1], full["pre_g", 0], full["pre_u", 0], name="l0pre_gate_up_gather_early_weights",
        pack=early_pack, seg_rows=[layer_rows(tag) for tag, _ in early_items] + [16])
    full.update({item: g.reshape(-1, d) for item, g in zip(early_items, early_w)})
    ffn_w = {("pre", 0): tuple(full[f"pre_{kind}", 0] for kind in "gud")}
    w_ab_in = full["ab_in", 0]
    cg = early_w[-1][:, 0, :2 * nconv].astype(F32)
    conv_w = (cg[:, :nconv] + cg[:, nconv:]).reshape(8, 3, -1).transpose(1, 0, 2).reshape(3, -1)
    aw = w_ab_in.shape[0] // 6
    h1 = _mm([(act0, full["pre_d", 0])], residual=h0, alpha=MACARON, tn=1024, name="l0pre_down")
    s_pre0 = (h0, xn0, gg0, uu0, act0)
    hn0, pa, pb = _norm_proj(h1, mix_norm[0:1], w_ab_in, (F32, BF16), tm=512, name="ab_norm_proj")
    ya = _conv_fwd(pa, conv_w, name="conv_fwd")
    yb, ltot, *late_w = _attn_fwd(pb, late_pack, [layer_rows(tag) for tag, _ in late_items],
                                  name="attn_fwd_gather_late_weights")
    full.update({item: g.reshape(-1, d) for item, g in zip(late_items, late_w)})
    for pos, layer in (("post", 0), ("pre", 1), ("post", 1)):
        ffn_w[pos, layer] = tuple(full[f"{pos}_{kind}", layer] for kind in "gud")
    w_ab_out, w_c_in, w_c_out = full["ab_out", 0], full["c_in", 0], full["c_out", 0]
    h2 = _mm([(ya, w_ab_out[:aw]), (yb, w_ab_out[aw:])], residual=h1, tn=1024, name="ab_out")
    h3, s_post0 = _ffn_fwd(h2, ffn_post_norm[0:1], *ffn_w["post", 0], "l0post")
    h4, s_pre1 = _ffn_fwd(h3, ffn_pre_norm[1:2], *ffn_w["pre", 1], "l1pre")
    hn1, pc = _norm_proj(h4, mix_norm[1:2], w_c_in, (F32,), tm=256, name="c_norm_proj")
    yc, o_saved, states = _hgrn_fwd(pc, c_lower_bounds, c_out_norm, name="hgrn_fwd")
    h5 = _mm([(yc, w_c_out)], residual=h4, tn=1024, name="c_out")
    h6, s_post1 = _ffn_fwd(h5, ffn_post_norm[1:2], *ffn_w["post", 1], "l1post")
    dh6, dh6_b, d_final, loss_vec = _loss_head(h6, final_norm.reshape(1, d), target, name="loss_head")

    gw = {}
    dh5, dh5_b, gw["post_g", 1], gw["post_u", 1], gw["post_d", 1], d_post1, *_ = _ffn_bwd(
        dh6, dh6_b, s_post1, ffn_post_norm[1:2], *ffn_w["post", 1], "l1post", 1.0)
    dyc = _mm([(dh5_b, w_c_out)], tb=True, tn=1024, name="c_out_dy")
    g_c_out = _mm([(yc, dh5_b)], ta=True, tm=256, tn=1024, out_dtype=BF16, name="c_out_dw")
    dcq, dcf, dci, dcg, dlb, d_onorm = _hgrn_bwd(pc, o_saved, states, dyc, c_lower_bounds, c_out_norm, name="hgrn_bwd")
    dparts = [dcq, dcf, dci, dcg]
    g_c_in = jnp.concatenate(_mm_shared_rhs(dparts, hn1, tm=256, name="c_in_dw"), axis=0)
    cw = w_c_in.shape[0] // 4
    dhn1 = _mm([(dp, w_c_in[i * cw:(i + 1) * cw]) for i, dp in enumerate(dparts)], tm=512, tn=1024, name="c_in_dx")
    dh4, dh4_b, d_mix1 = _rmsnorm_bwd(h4, mix_norm[1:2], dhn1, dh5, scale=MACARON, name="l1_mix_norm_bwd")
    dh3, dh3_b, gw["pre_g", 1], gw["pre_u", 1], gw["pre_d", 1], d_pre1, *_ = _ffn_bwd(
        dh4, dh4_b, s_pre1, ffn_pre_norm[1:2], *ffn_w["pre", 1], "l1pre", MACARON)
    dh2, dh2_b, gw["post_g", 0], gw["post_u", 0], gw["post_d", 0], d_post0, *_ = _ffn_bwd(
        dh3, dh3_b, s_post0, ffn_post_norm[0:1], *ffn_w["post", 0], "l0post", 1.0)
    dyab = _mm([(dh2_b, w_ab_out)], tb=True, tn=1024, name="ab_out_dy")
    g_ab_out = jnp.concatenate(_mm_shared_rhs([ya, yb], dh2_b, tm=256, name="ab_out_dw"), axis=0)
    dab, dac, dax, g_conv = _conv_bwd(pa, dyab, conv_w, name="conv_bwd")

    def chip_partials(key, grads, extra=()):
        gpack = jnp.concatenate([grads[item].reshape(8, -1, d) for item in grad_items[key]] + list(extra), axis=1)
        send = gpack.reshape(4, 2, gpack.shape[1], d)
        from_sibling = _sibling_exchange(send, name=f"grad{key}_sibling_exchange")
        return _pair_add(send, from_sibling, core, name=f"grad{key}_pair_add")

    gw["c_in", 0], gw["c_out", 0], gw["ab_out", 0] = g_c_in, g_c_out, g_ab_out
    chip_part_a = chip_partials("A", gw)
    dq, dk, dv, parts_a = _attn_bwd(pb, dyab, ltot, chip_part_a, name="attn_bwd_exchange_grads_a")
    dparts = [dab, dac, dax, dq, dk, dv]
    g_ab_in = jnp.concatenate(_mm_shared_rhs(dparts, hn0, tm=128, name="ab_in_dw"), axis=0)
    dhn0 = _mm([(dp, w_ab_in[i * aw:(i + 1) * aw]) for i, dp in enumerate(dparts)], tm=512, tn=1024, name="ab_in_dx")
    dh1, dh1_b, d_mix0 = _rmsnorm_bwd(h1, mix_norm[0:1], dhn0, dh2, scale=MACARON, name="l0_mix_norm_bwd")
    gw["ab_in", 0] = g_ab_in
    gconv_own = g_conv.reshape(3, 8, -1).transpose(1, 0, 2).reshape(8, -1)
    conv_piece = jnp.zeros((8, 16, d), F32).at[:, 0, :nconv].set(gconv_own).astype(BF16)

    def chip_part_b(dwg, dwu, dwd):
        gw["pre_g", 0], gw["pre_u", 0], gw["pre_d", 0] = dwg, dwu, dwd
        return chip_partials("B", gw)

    dh0, _, _, _, _, d_pre0, parts_b, parts_c = _ffn_bwd(
        dh1, dh1_b, s_pre0, ffn_pre_norm[0:1], *ffn_w["pre", 0], "l0pre", 1.0, chip_part_b,
        chip_partials("C", gw, [conv_piece]))

    g_sum = {key: _grad_sum(p, name=f"grad{key}_sum") for key, p in (("A", parts_a), ("B", parts_b), ("C", parts_c))}

    upd = {}
    for tag, w, m, v in big:
        nl = layer_rows(tag)
        view = (lambda a: jnp.swapaxes(a, 1, 2)) if tag in col_sharded else (lambda a: a)
        where = {layer: (key, grad_offs[key][tag, layer])
                 for key in grad_items for t2, layer in grad_items[key] if t2 == tag}
        res = None
        for layer in sorted(where):
            key, off = where[layer]
            res = _adamw_shard(g_sum[key], off, view(w), view(m), view(v), layer, res, name=f"adamw_{tag}{layer}")
        g_nat = jnp.stack([g_sum[where[layer][0]][where[layer][1]:where[layer][1] + nl] for layer in sorted(where)])
        upd[tag] = [view(a) for a in [g_nat] + list(res)]
    res = _adamw_shard(g_sum["C"], grad_conv_row, *(conv_rows(a, False)[None] for a in (ab_conv_w, m_ab_conv_w, v_ab_conv_w)),
                       0, None, name="adamw_conv")
    g_conv_rows = g_sum["C"][grad_conv_row:grad_conv_row + 16]
    upd["conv"] = [r[0, :nconv].reshape(ab_conv_w.shape) for r in [g_conv_rows] + [r[0] for r in res]]

    def small_pack(pre, mix, post, final, lbs, onorm):
        def slot(parts):
            out, r = jnp.zeros((SLOT, d), F32), 0
            for a in (parts if isinstance(parts, tuple) else (parts,)):
                out = out.at[r:r + a.shape[0], :a.shape[1]].set(a)
                r += a.shape[0]
            return out

        return jnp.concatenate([slot(pre), slot(mix), slot(post), slot(final.reshape(1, d)), slot(lbs), slot(onorm)], axis=0)

    d_on = d_onorm.reshape(-1, c_out_norm.shape[1]).sum(axis=0, keepdims=True)
    gsmall = small_pack((d_pre0, d_pre1), (d_mix0, d_mix1), (d_post0, d_post1), d_final, dlb, d_on)
    gsmall_all = _all_gather(gsmall, name="gather_small_grads")
    sres = _small_update(
        gsmall_all,
        small_pack(ffn_pre_norm, mix_norm, ffn_post_norm, final_norm, c_lower_bounds, c_out_norm),
        small_pack(m_ffn_pre_norm, m_mix_norm, m_ffn_post_norm, m_final_norm, m_c_lower_bounds, m_c_out_norm),
        small_pack(v_ffn_pre_norm, v_mix_norm, v_ffn_post_norm, v_final_norm, v_c_lower_bounds, v_c_out_norm),
        name="small_update")

    def small_out(r):
        return {"pre_norm": r[0:2], "mix_norm": r[SLOT:SLOT + 2], "post_norm": r[2 * SLOT:2 * SLOT + 2],
                "final": r[3 * SLOT], "lb": r[ROW_LB:ROW_LB + 2], "onorm": r[5 * SLOT:5 * SLOT + 1, :c_out_norm.shape[1]]}

    small = [small_out(r) for r in sres]
    outs = []
    for k in range(4):
        s = small[k]
        outs += [s["pre_norm"], upd["pre_g"][k], upd["pre_u"][k], upd["pre_d"][k], s["mix_norm"], s["post_norm"],
                 upd["post_g"][k], upd["post_u"][k], upd["post_d"][k], upd["ab_in"][k], upd["conv"][k],
                 upd["ab_out"][k], upd["c_in"][k], s["lb"], s["onorm"], upd["c_out"][k], s["final"]]
    loss = lax.psum(loss_vec[0, 0], ("x", "y", "c"))
    return (loss, dh0[None], *outs)
```

```python
import functools
import math

import jax
import jax.numpy as jnp
from jax import lax
from jax.experimental import pallas as pl
from jax.experimental.pallas import tpu as pltpu

F32 = jnp.float32
BF16 = jnp.bfloat16
MESH = pl.DeviceIdType.MESH

RMS_EPS = 1e-6
MACARON = 0.5
LANES = 128
CHUNK = 64
N_LEVELS = 6
HGRN_HEADS = 2
SB_KEYS = 256
ADAM_LR, ADAM_B1, ADAM_B2, ADAM_EPS, ADAM_WD, ADAM_STEP = 0.001, 0.9, 0.999, 1e-08, 0.01, 10
VMEM_LIMIT = 48 * 1024 * 1024


def _cp(**kw):
    return pltpu.CompilerParams(vmem_limit_bytes=VMEM_LIMIT, **kw)


def _sigmoid(x):
    return 0.5 * jnp.tanh(0.5 * x) + 0.5


def _bf(x):
    return x if x.dtype == BF16 else x.astype(BF16)


def _split3(x):
    hi = x.astype(BF16)
    r1 = x - hi.astype(F32)
    mid = r1.astype(BF16)
    lo = (r1 - mid.astype(F32)).astype(BF16)
    return hi, mid, lo


def _dot(a, b, ca=1, cb=0):
    return lax.dot_general(a, b, (((ca,), (cb,)), ((), ())), preferred_element_type=F32)


def _dot_exact_lhs(m, x):
    hi, mid, lo = _split3(x)
    return _dot(m, hi) + _dot(m, mid) + _dot(m, lo)


def _dot_exact_rhs(x, m):
    hi, mid, lo = _split3(x)
    return _dot(hi, m) + _dot(mid, m) + _dot(lo, m)


def _mm(terms, *, name, ta=False, tb=False, out_dtype=F32, residual=None, alpha=1.0, tm=512, tn=512):
    nt = len(terms)
    a0, b0 = terms[0]
    m = a0.shape[1] if ta else a0.shape[0]
    n = b0.shape[0] if tb else b0.shape[1]
    tm, tn = min(tm, m), min(tn, n)
    assert m % tm == 0 and n % tn == 0, (name, m, n, tm, tn)
    has_res = residual is not None

    def body(*refs):
        o_ref = refs[-1]
        acc = None
        for i in range(nt):
            a = _bf(refs[2 * i][...])
            b = _bf(refs[2 * i + 1][...])
            p = _dot(a, b, 0 if ta else 1, 1 if tb else 0)
            acc = p if acc is None else acc + p
        if alpha != 1.0:
            acc = acc * alpha
        if has_res:
            acc = acc + refs[2 * nt][...]
        o_ref[...] = acc.astype(out_dtype)

    in_specs, args = [], []
    for a, b in terms:
        k = a.shape[0] if ta else a.shape[1]
        assert (b.shape[1] if tb else b.shape[0]) == k, (name, a.shape, b.shape)
        in_specs.append(pl.BlockSpec((k, tm), lambda i, j: (0, i)) if ta else pl.BlockSpec((tm, k), lambda i, j: (i, 0)))
        in_specs.append(pl.BlockSpec((tn, k), lambda i, j: (j, 0)) if tb else pl.BlockSpec((k, tn), lambda i, j: (0, j)))
        args += [a, b]
    if has_res:
        in_specs.append(pl.BlockSpec((tm, tn), lambda i, j: (i, j)))
        args.append(residual)
    return pl.pallas_call(
        body, name=name, grid=(m // tm, n // tn), in_specs=in_specs,
        out_specs=pl.BlockSpec((tm, tn), lambda i, j: (i, j)),
        out_shape=jax.ShapeDtypeStruct((m, n), out_dtype), compiler_params=_cp())(*args)


def _norm_proj(x, gain, w_t, out_dtypes, *, name, tm):
    t, d = x.shape
    n = w_t.shape[0]
    tm = min(tm, t)
    npart = len(out_dtypes)
    width = n // npart

    def body(x_ref, g_ref, w_ref, xn_ref, *part_refs):
        xv = x_ref[...]
        rstd = lax.rsqrt(jnp.mean(xv * xv, axis=-1, keepdims=True) + RMS_EPS)
        xn = (xv * rstd * g_ref[...]).astype(BF16)
        xn_ref[...] = xn
        for p, ref in enumerate(part_refs):
            ref[...] = _dot(xn, w_ref[p * width:(p + 1) * width, :], 1, 1).astype(out_dtypes[p])

    row = pl.BlockSpec((tm, d), lambda i: (i, 0))
    return pl.pallas_call(
        body, name=name, grid=(t // tm,),
        in_specs=[row, pl.BlockSpec((1, d), lambda i: (0, 0)), pl.BlockSpec((n, d), lambda i: (0, 0))],
        out_specs=[row] + [pl.BlockSpec((tm, width), lambda i: (i, 0))] * npart,
        out_shape=[jax.ShapeDtypeStruct((t, d), BF16)] + [jax.ShapeDtypeStruct((t, width), dt) for dt in out_dtypes],
        compiler_params=_cp())(x, gain, w_t)


def _mm_shared_rhs(a_list, b, *, name, tm, out_dtype=BF16):
    k, n = b.shape
    assert all(a.shape[0] == k and a.shape[1] % tm == 0 and a.shape[1] == a_list[0].shape[1] for a in a_list)
    m = a_list[0].shape[1]
    na = len(a_list)

    def body(*refs):
        bv = refs[na][...]
        for i in range(na):
            refs[na + 1 + i][...] = _dot(refs[i][...], bv, 0, 0).astype(out_dtype)

    return pl.pallas_call(
        body, name=name, grid=(m // tm,),
        in_specs=[pl.BlockSpec((k, tm), lambda i: (0, i))] * na + [pl.BlockSpec((k, n), lambda i: (0, 0))],
        out_specs=[pl.BlockSpec((tm, n), lambda i: (i, 0))] * na,
        out_shape=[jax.ShapeDtypeStruct((m, n), out_dtype)] * na, compiler_params=_cp())(*a_list, b)


def _rmsnorm_bwd(x, gain, dxn, dres, *, name, scale, tm=512):
    t, d = x.shape
    tm = min(tm, t)

    def body(x_ref, g_ref, dxn_ref, dres_ref, dx_ref, dxb_ref, dg_ref):
        xv = x_ref[...]
        rstd = lax.rsqrt(jnp.mean(xv * xv, axis=-1, keepdims=True) + RMS_EPS)
        xhat = xv * rstd
        dxn_v = dxn_ref[...]
        dxhat = dxn_v * g_ref[...]
        dx = dres_ref[...] + rstd * (dxhat - xhat * jnp.mean(dxhat * xhat, axis=-1, keepdims=True))
        dx_ref[...] = dx
        dxb_ref[...] = (dx * scale).astype(BF16)

        @pl.when(pl.program_id(0) == 0)
        def _():
            dg_ref[...] = jnp.zeros_like(dg_ref)

        dg_ref[...] += jnp.sum(dxn_v * xhat, axis=0, keepdims=True)

    row = pl.BlockSpec((tm, d), lambda i: (i, 0))
    vec = pl.BlockSpec((1, d), lambda i: (0, 0))
    return pl.pallas_call(
        body, name=name, grid=(t // tm,), in_specs=[row, vec, row, row], out_specs=[row, row, vec],
        out_shape=[jax.ShapeDtypeStruct((t, d), F32), jax.ShapeDtypeStruct((t, d), BF16), jax.ShapeDtypeStruct((1, d), F32)],
        compiler_params=_cp())(x, gain, dxn, dres)


def _loss_head(h, gain, target, *, name, tm=512):
    t, d = h.shape
    tm = min(tm, t)

    def body(h_ref, g_ref, t_ref, dh_ref, dhb_ref, dg_ref, loss_ref):
        hv = h_ref[...]
        rstd = lax.rsqrt(jnp.mean(hv * hv, axis=-1, keepdims=True) + RMS_EPS)
        xhat = hv * rstd
        err = xhat * g_ref[...] - t_ref[...]
        dy = err * (1.0 / d)
        dxhat = dy * g_ref[...]
        dh = rstd * (dxhat - xhat * jnp.mean(dxhat * xhat, axis=-1, keepdims=True))
        dh_ref[...] = dh
        dhb_ref[...] = (dh * MACARON).astype(BF16)

        @pl.when(pl.program_id(0) == 0)
        def _():
            dg_ref[...] = jnp.zeros_like(dg_ref)
            loss_ref[...] = jnp.zeros_like(loss_ref)

        dg_ref[...] += jnp.sum(dy * xhat, axis=0, keepdims=True)
        part = jnp.sum(jnp.sum(err * err, axis=-1, keepdims=True), axis=0, keepdims=True) * (0.5 / d)
        loss_ref[...] += jnp.broadcast_to(part, loss_ref.shape)

    row = pl.BlockSpec((tm, d), lambda i: (i, 0))
    vec = pl.BlockSpec((1, d), lambda i: (0, 0))
    return pl.pallas_call(
        body, name=name, grid=(t // tm,), in_specs=[row, vec, row],
        out_specs=[row, row, vec, pl.BlockSpec((1, LANES), lambda i: (0, 0))],
        out_shape=[jax.ShapeDtypeStruct((t, d), F32), jax.ShapeDtypeStruct((t, d), BF16), jax.ShapeDtypeStruct((1, d), F32),
                   jax.ShapeDtypeStruct((1, LANES), F32)],
        compiler_params=_cp())(h, gain, target)


def _norm_gate_up(x, gain, wg, wu, *, name, tm=512, tf=1408, pack=None, seg_rows=()):
    t, d = x.shape
    f = wg.shape[0]
    tm, tf = min(tm, t), min(tf, f)
    assert f % tf == 0
    ni, nj = t // tm, f // tf
    nseg = len(seg_rows)

    def body(x_ref, g_ref, wg_ref, wu_ref, *rest):
        if pack is not None:
            pack_ref, xn_ref, gg_ref, uu_ref, act_ref = rest[:5]
            start, forward, finish = _gather_phases(pack_ref, rest[5:5 + nseg], seg_rows, *rest[5 + nseg:])
            step = pl.program_id(0) * nj + pl.program_id(1)
            pl.when(step == 0)(start)
            pl.when(step == (3 * ni * nj) // 4)(forward)
        else:
            xn_ref, gg_ref, uu_ref, act_ref = rest

        @pl.when(pl.program_id(1) == 0)
        def _():
            xv = x_ref[...]
            rstd = lax.rsqrt(jnp.mean(xv * xv, axis=-1, keepdims=True) + RMS_EPS)
            xn_ref[...] = (xv * rstd * g_ref[...]).astype(BF16)

        xn = xn_ref[...]
        gv = _dot(xn, wg_ref[...], 1, 1)
        uv = _dot(xn, wu_ref[...], 1, 1)
        gg_ref[...] = gv.astype(BF16)
        uu_ref[...] = uv.astype(BF16)
        act_ref[...] = (gv * _sigmoid(gv) * uv).astype(BF16)
        if pack is not None:
            pl.when(step == ni * nj - 1)(finish)

    row = pl.BlockSpec((tm, d), lambda i, j: (i, 0))
    wsp = pl.BlockSpec((tf, d), lambda i, j: (j, 0))
    osp = pl.BlockSpec((tm, tf), lambda i, j: (i, j))
    fused = pack is not None
    return pl.pallas_call(
        body, name=name, grid=(ni, nj),
        in_specs=[row, pl.BlockSpec((1, d), lambda i, j: (0, 0)), wsp, wsp] + ([HBM_SPEC] if fused else []),
        out_specs=[row, osp, osp, osp] + [HBM_SPEC] * nseg,
        out_shape=[jax.ShapeDtypeStruct((t, d), BF16)] + [jax.ShapeDtypeStruct((t, f), BF16)] * 3
        + [jax.ShapeDtypeStruct((8, n, d), BF16) for n in seg_rows],
        scratch_shapes=_gather_scratch() if fused else [],
        compiler_params=_cp(dimension_semantics=("arbitrary", "arbitrary")))(x, gain, wg, wu, *([pack] if fused else []))


def _swiglu_bwd(dout, wd, gg, uu, chip_part=None, *, name, tm=512, tf=1408):
    t, d = dout.shape
    f = wd.shape[0]
    tm, tf = min(tm, t), min(tf, f)
    nj, ni = f // tf, t // tm
    fused = chip_part is not None

    def body(do_ref, wd_ref, g_ref, u_ref, *rest):
        if fused:
            part_ref, dg_ref, du_ref, parts_ref = rest[:4]
            start, finish = _chip_exchange_phases(part_ref, parts_ref, *rest[4:])
            step = pl.program_id(0) * ni + pl.program_id(1)
            pl.when(step == 0)(start)
        else:
            dg_ref, du_ref = rest
        dact = _dot(do_ref[...], wd_ref[...], 1, 1)
        gv = g_ref[...].astype(F32)
        uv = u_ref[...].astype(F32)
        sg = _sigmoid(gv)
        dg_ref[...] = (dact * uv * (sg * (1.0 + gv * (1.0 - sg)))).astype(BF16)
        du_ref[...] = (dact * (gv * sg)).astype(BF16)
        if fused:
            pl.when(step == nj * ni - 1)(finish)

    osp = pl.BlockSpec((tm, tf), lambda j, i: (i, j))
    return pl.pallas_call(
        body, name=name, grid=(nj, ni),
        in_specs=[pl.BlockSpec((tm, d), lambda j, i: (i, 0)), pl.BlockSpec((tf, d), lambda j, i: (j, 0)), osp, osp]
        + ([HBM_SPEC] if fused else []),
        out_specs=[osp, osp] + ([HBM_SPEC] if fused else []),
        out_shape=[jax.ShapeDtypeStruct((t, f), BF16)] * 2
        + ([jax.ShapeDtypeStruct(chip_part.shape, chip_part.dtype)] if fused else []),
        scratch_shapes=_chip_exchange_scratch() if fused else [],
        compiler_params=_cp(dimension_semantics=("arbitrary", "arbitrary")))(dout, wd, gg, uu, *([chip_part] if fused else []))


def _shift_down(x, n):
    rows = lax.broadcasted_iota(jnp.int32, x.shape, 0)
    return jnp.where(rows >= n, pltpu.roll(x, n, 0), 0.0)


def _shift_up(x, n):
    t = x.shape[0]
    rows = lax.broadcasted_iota(jnp.int32, x.shape, 0)
    return jnp.where(rows < t - n, pltpu.roll(x, t - n, 0), 0.0)


def _conv_fwd(pa, conv_w, *, name):
    t = pa.shape[0]
    nb = pa.shape[1] // 3 // LANES

    def body(b_ref, c_ref, x_ref, w_ref, y_ref):
        u = c_ref[...] * x_ref[...]
        w = w_ref[...]
        conv = w[2:3, :] * u + w[1:2, :] * _shift_down(u, 1) + w[0:1, :] * _shift_down(u, 2)
        y_ref[...] = (b_ref[...] * conv).astype(BF16)

    def col(off):
        return pl.BlockSpec((t, LANES), lambda j: (0, off + j))

    return pl.pallas_call(
        body, name=name, grid=(nb,),
        in_specs=[col(0), col(nb), col(2 * nb), pl.BlockSpec((3, LANES), lambda j: (0, j))],
        out_specs=pl.BlockSpec((t, LANES), lambda j: (0, j)),
        out_shape=jax.ShapeDtypeStruct((t, nb * LANES), BF16), compiler_params=_cp())(pa, pa, pa, conv_w)


def _conv_bwd(pa, dy, conv_w, *, name):
    t = pa.shape[0]
    nb = pa.shape[1] // 3 // LANES

    def body(b_ref, c_ref, x_ref, dy_ref, w_ref, db_ref, dc_ref, dx_ref, dw_ref):
        cv, xv = c_ref[...], x_ref[...]
        u = cv * xv
        u1, u2 = _shift_down(u, 1), _shift_down(u, 2)
        w = w_ref[...]
        conv = w[2:3, :] * u + w[1:2, :] * u1 + w[0:1, :] * u2
        dyv = dy_ref[...]
        db_ref[...] = (dyv * conv).astype(BF16)
        dconv = dyv * b_ref[...]
        du = w[2:3, :] * dconv + w[1:2, :] * _shift_up(dconv, 1) + w[0:1, :] * _shift_up(dconv, 2)
        dc_ref[...] = (du * xv).astype(BF16)
        dx_ref[...] = (du * cv).astype(BF16)
        dw_ref[0:1, :] = jnp.sum(dconv * u2, axis=0, keepdims=True)
        dw_ref[1:2, :] = jnp.sum(dconv * u1, axis=0, keepdims=True)
        dw_ref[2:3, :] = jnp.sum(dconv * u, axis=0, keepdims=True)

    def col(off):
        return pl.BlockSpec((t, LANES), lambda j: (0, off + j))

    osp = pl.BlockSpec((t, LANES), lambda j: (0, j))
    wsp = pl.BlockSpec((3, LANES), lambda j: (0, j))
    return pl.pallas_call(
        body, name=name, grid=(nb,), in_specs=[col(0), col(nb), col(2 * nb), col(0), wsp],
        out_specs=[osp, osp, osp, wsp],
        out_shape=[jax.ShapeDtypeStruct((t, nb * LANES), BF16)] * 3 + [jax.ShapeDtypeStruct((3, nb * LANES), F32)],
        compiler_params=_cp())(pa, pa, pa, dy, conv_w)


def _sb_consts():
    j = lax.broadcasted_iota(jnp.int32, (SB_KEYS, SB_KEYS), 0)
    s = lax.broadcasted_iota(jnp.int32, (SB_KEYS, SB_KEYS), 1)
    after = (j > s).astype(BF16)
    upto = (j <= s).astype(BF16)
    before = (j < s).astype(BF16)
    return after, jnp.stack([upto, before])


def _log_sigmoid(z):
    return jnp.minimum(z, 0.0) - jnp.log(1.0 + jnp.exp(-jnp.abs(z)))


def _attn_fwd(pb, late_pack, seg_rows, *, name, tq=256):
    t = pb.shape[0]
    npair = pb.shape[1] // 3 // LANES
    tq = min(tq, t)
    nq = t // tq
    cmat, _ = _sb_consts()
    scale = 1.0 / math.sqrt(LANES // 2)

    nseg = len(seg_rows)

    def body(q_ref, k_ref, v_ref, c_ref, late_ref, y_ref, lt_ref, *rest):
        i = pl.program_id(1)
        pair = pl.program_id(0)
        scratch = rest[nseg:nseg + 4]
        start, forward, finish = _gather_phases(late_ref, rest[:nseg], seg_rows, *rest[nseg + 4:])
        pl.when((pair == 0) & (i == 0))(start)
        pl.when((pair == npair - 1) & (i == nq // 2))(forward)
        lane = lax.broadcasted_iota(jnp.int32, (tq, LANES), 1)
        rowpos = i * tq + lax.broadcasted_iota(jnp.int32, (tq, SB_KEYS), 0)
        colid = lax.broadcasted_iota(jnp.int32, (tq, SB_KEYS), 1)
        q2 = q_ref[...] * jnp.asarray(scale, BF16)
        cm = c_ref[...]
        hi_lanes = lane >= LANES // 2
        qhs = [jnp.where(hi_lanes == (hh == 1), q2, jnp.zeros_like(q2)) for hh in range(2)]
        per_q = tq // SB_KEYS

        def blk(jb):
            return pl.ds(pl.multiple_of(jb * SB_KEYS, SB_KEYS), SB_KEYS)

        zbuf, wbuf, accbuf, runbuf = scratch

        def scores(jb):
            kb = k_ref[blk(jb), :]
            for hh in range(2):
                zbuf[hh] = _dot(qhs[hh], kb, 1, 1)

        def values(jb):
            vb = v_ref[blk(jb), :]
            for hh in range(2):
                accbuf[hh] += _dot(wbuf[hh], vb)

        def trip(jb, masked, first=False):
            mask = (jb * SB_KEYS + colid) < rowpos if masked else None
            if not first:
                values(jb + 1)
            pre, css = [], []
            for hh in range(2):
                z = zbuf[hh]
                lb = _log_sigmoid(z)
                lk = lb - z
                if masked:
                    lk = jnp.where(mask, lk, 0.0)
                lk_hi, lk_lo = _split2(lk)
                css.append(_dot(lk_hi, cm) + _dot(lk_lo, cm))
                run = runbuf[hh]
                pre.append(lb + run)
                runbuf[hh] = run + jnp.sum(lk, axis=1, keepdims=True)
            scores(jnp.maximum(jb - 1, 0))
            for hh in range(2):
                w = jnp.exp(pre[hh] + css[hh])
                if masked:
                    w = jnp.where(mask, w, 0.0)
                wbuf[hh] = w.astype(BF16)

        nfull = i * per_q
        accbuf[...] = jnp.zeros_like(accbuf)
        runbuf[...] = jnp.zeros_like(runbuf)
        scores(nfull + per_q - 1)
        for dblk in reversed(range(per_q)):
            trip(nfull + dblk, True, first=dblk == per_q - 1)

        def full_block(n, carry):
            trip(nfull - 1 - n, False)
            return carry

        lax.fori_loop(0, nfull, full_block, 0)
        values(0)
        y_ref[...] = jnp.where(hi_lanes, accbuf[1], accbuf[0]).astype(BF16)
        lt_ref[...] = jnp.where(hi_lanes, runbuf[1], runbuf[0])
        pl.when((pair == npair - 1) & (i == nq - 1))(finish)

    return pl.pallas_call(
        body, name=name, grid=(npair, nq),
        in_specs=[pl.BlockSpec((tq, LANES), lambda p, i: (i, p)),
                  pl.BlockSpec((t, LANES), lambda p, i: (0, npair + p)),
                  pl.BlockSpec((t, LANES), lambda p, i: (0, 2 * npair + p)),
                  pl.BlockSpec((SB_KEYS, SB_KEYS), lambda p, i: (0, 0)),
                  HBM_SPEC],
        out_specs=[pl.BlockSpec((tq, LANES), lambda p, i: (i, p))] * 2 + [HBM_SPEC] * nseg,
        out_shape=[jax.ShapeDtypeStruct((t, npair * LANES), BF16), jax.ShapeDtypeStruct((t, npair * LANES), F32),
                   ] + [jax.ShapeDtypeStruct((8, n, late_pack.shape[1]), late_pack.dtype) for n in seg_rows],
        scratch_shapes=[pltpu.VMEM((2, tq, SB_KEYS), F32), pltpu.VMEM((2, tq, SB_KEYS), BF16),
                        pltpu.VMEM((2, tq, LANES), F32), pltpu.VMEM((2, tq, 1), F32)] + _gather_scratch(),
        compiler_params=_cp(dimension_semantics=("arbitrary", "arbitrary")))(pb, pb, pb, cmat, late_pack)


def _attn_bwd(pb, dy, ltot, chip_part, *, name, tq=256):
    t = pb.shape[0]
    npair = pb.shape[1] // 3 // LANES
    tq = min(tq, t)
    nq = t // tq
    _, cmats = _sb_consts()
    scale = 1.0 / math.sqrt(LANES // 2)

    def body(q_ref, k_ref, v_ref, dy_ref, lt_ref, c_ref, part_ref, dq_ref, dk_ref, dv_ref, parts_ref, dk_acc, dv_acc, *rest):
        i = pl.program_id(1)
        pair = pl.program_id(0)
        scratch = rest[:6]
        start, finish = _chip_exchange_phases(part_ref, parts_ref, *rest[6:])
        pl.when((pair == 0) & (i == 0))(start)

        @pl.when(i == 0)
        def _():
            dk_acc[...] = jnp.zeros_like(dk_acc)
            dv_acc[...] = jnp.zeros_like(dv_acc)

        lane = lax.broadcasted_iota(jnp.int32, (tq, LANES), 1)
        rowpos = i * tq + lax.broadcasted_iota(jnp.int32, (tq, SB_KEYS), 0)
        colid = lax.broadcasted_iota(jnp.int32, (tq, SB_KEYS), 1)
        q2 = q_ref[...] * jnp.asarray(scale, BF16)
        do2 = dy_ref[...].astype(BF16)
        ltv = lt_ref[...]
        c_upto, c_before = c_ref[0], c_ref[1]
        hi_lanes = lane >= LANES // 2
        sels = [hi_lanes == (hh == 1) for hh in range(2)]
        qhs = [jnp.where(s, q2, jnp.zeros_like(q2)) for s in sels]
        dohs = [jnp.where(s, do2, jnp.zeros_like(do2)) for s in sels]
        lts = [ltv[:, 0:1], ltv[:, LANES // 2:LANES // 2 + 1]]
        per_q = tq // SB_KEYS

        def blk(jb):
            return pl.ds(pl.multiple_of(jb * SB_KEYS, SB_KEYS), SB_KEYS)

        zbuf, dabuf, dzbuf, abuf, dqbuf, sumbuf = scratch

        def scores(jb):
            kb, vb = k_ref[blk(jb), :], v_ref[blk(jb), :]
            for hh in range(2):
                zbuf[hh] = _dot(qhs[hh], kb, 1, 1)
                dabuf[hh] = _dot(dohs[hh], vb, 1, 1)

        def products(jb):
            kb = k_ref[blk(jb), :]
            dk_acc[blk(jb), :] += _dot(dzbuf[0], qhs[0], 0, 0) + _dot(dzbuf[1], qhs[1], 0, 0)
            dv_acc[blk(jb), :] += _dot(abuf[0], dohs[0], 0, 0) + _dot(abuf[1], dohs[1], 0, 0)
            for hh in range(2):
                dqbuf[hh] += _dot(dzbuf[hh], kb)

        def trip(jb, masked):
            mask = (jb * SB_KEYS + colid) < rowpos if masked else None
            products(jnp.maximum(jb - 1, 0))
            lbs, css, es, ces = [], [], [], []
            for hh in range(2):
                z = zbuf[hh]
                lb = _log_sigmoid(z)
                lk = lb - z
                if masked:
                    lk = jnp.where(mask, lk, 0.0)
                lk_hi, lk_lo = _split2(lk)
                css.append(_dot(lk_hi, c_upto) + _dot(lk_lo, c_upto))
                csum = sumbuf[2 * hh]
                lbs.append((lb, lb + (lts[hh] - csum)))
                sumbuf[2 * hh] = csum + jnp.sum(lk, axis=1, keepdims=True)
            for hh in range(2):
                a = jnp.exp(lbs[hh][1] - css[hh])
                if masked:
                    a = jnp.where(mask, a, 0.0)
                e = a * dabuf[hh]
                e_hi, e_lo = _split2(e)
                ces.append(_dot(e_hi, c_before) + _dot(e_lo, c_before))
                abuf[hh] = a.astype(BF16)
                es.append(e)
            scores(jnp.minimum(jb + 1, last))
            for hh in range(2):
                prun = sumbuf[2 * hh + 1]
                beta = jnp.exp(lbs[hh][0])
                dz = es[hh] * (1.0 - beta) - (prun + ces[hh]) * beta
                if masked:
                    dz = jnp.where(mask, dz, 0.0)
                dzbuf[hh] = dz.astype(BF16)
                sumbuf[2 * hh + 1] = prun + jnp.sum(es[hh], axis=1, keepdims=True)

        nfull = i * per_q
        last = nfull + per_q - 1
        for buf in (dzbuf, abuf, dqbuf, sumbuf):
            buf[...] = jnp.zeros_like(buf)
        scores(0)

        def full_block(jb, carry):
            trip(jb, False)
            return carry

        lax.fori_loop(0, nfull, full_block, 0)
        for dblk in range(per_q):
            trip(nfull + dblk, True)
        products(last)
        dq_ref[...] = (jnp.where(hi_lanes, dqbuf[1], dqbuf[0]) * scale).astype(BF16)

        @pl.when(i == nq - 1)
        def _():
            dk_ref[...] = dk_acc[...].astype(BF16)
            dv_ref[...] = dv_acc[...].astype(BF16)

        pl.when((pair == npair - 1) & (i == nq - 1))(finish)

    blk = pl.BlockSpec((tq, LANES), lambda p, i: (i, p))
    full = pl.BlockSpec((t, LANES), lambda p, i: (0, p))
    return pl.pallas_call(
        body, name=name, grid=(npair, nq),
        in_specs=[blk,
                  pl.BlockSpec((t, LANES), lambda p, i: (0, npair + p)),
                  pl.BlockSpec((t, LANES), lambda p, i: (0, 2 * npair + p)),
                  pl.BlockSpec((tq, LANES), lambda p, i: (i, npair + p)),
                  blk,
                  pl.BlockSpec((2, SB_KEYS, SB_KEYS), lambda p, i: (0, 0, 0)),
                  HBM_SPEC],
        out_specs=[blk, full, full, HBM_SPEC],
        out_shape=[jax.ShapeDtypeStruct((t, npair * LANES), BF16)] * 3 + [jax.ShapeDtypeStruct(chip_part.shape, chip_part.dtype)],
        scratch_shapes=[pltpu.VMEM((t, LANES), F32), pltpu.VMEM((t, LANES), F32),
                        pltpu.VMEM((2, tq, SB_KEYS), F32), pltpu.VMEM((2, tq, SB_KEYS), F32),
                        pltpu.VMEM((2, tq, SB_KEYS), BF16), pltpu.VMEM((2, tq, SB_KEYS), BF16),
                        pltpu.VMEM((2, tq, LANES), F32), pltpu.VMEM((4, tq, 1), F32)] + _chip_exchange_scratch(),
        compiler_params=_cp(dimension_semantics=("arbitrary", "arbitrary")))(pb, pb, pb, dy, ltot, cmats, chip_part)


def _hgrn_consts():
    t = lax.broadcasted_iota(jnp.int32, (CHUNK, CHUNK), 0)
    s = lax.broadcasted_iota(jnp.int32, (CHUNK, CHUNK), 1)
    masks = []
    for lvl in range(N_LEVELS):
        half = CHUNK >> (lvl + 1)
        same = (t // (2 * half)) == (s // (2 * half))
        masks.append((same & (t % (2 * half) >= half) & (s % (2 * half) < half)).astype(F32))
    masks.append((t == s).astype(F32))
    prefix = (s <= t).astype(BF16)
    suffix = (s >= t).astype(BF16)
    return prefix, jnp.stack(masks), suffix


def _hgrn_gates(qr, fr, lbv):
    sg = _sigmoid(fr)
    fval = lbv + (1.0 - lbv) * sg
    kk = (1.0 - lbv) * _sigmoid(-fr)
    sq = _sigmoid(qr)
    return sg, fval, jnp.log(fval), kk, sq, qr * sq


def _lower_bound(c_ref):
    c = c_ref[...]
    mx = jnp.max(c, axis=0, keepdims=True)
    ex = jnp.exp(c - mx)
    return ex[1:2, :] / jnp.sum(ex, axis=0, keepdims=True)


def _level_ref(b, lvl):
    half = CHUNK >> (lvl + 1)
    seg = 2 * half
    if seg >= 8:
        b3 = b.reshape(CHUNK // seg, seg, LANES)
        return jnp.broadcast_to(b3[:, half - 1:half, :], b3.shape).reshape(CHUNK, LANES)
    pos = lax.broadcasted_iota(jnp.int32, b.shape, 0) % seg
    out = b
    for p in range(seg):
        if p != half - 1:
            out = jnp.where(pos == p, pltpu.roll(b, (p - (half - 1)) % CHUNK, 0), out)
    return out


def _hgrn_levels(b, qs, kk):
    out = []
    for lvl in range(N_LEVELS):
        fac = jnp.exp(-jnp.abs(b - _level_ref(b, lvl)))
        out.append((qs * fac, kk * fac, fac, fac))
    out.append((qs, kk, None, None))
    return out


def _split2(x):
    hi = x.astype(BF16)
    return hi, (x - hi.astype(F32)).astype(BF16)


def _hgrn_fwd(pc, c_lb, out_norm, *, name, tc=512):
    t = pc.shape[0]
    nh = pc.shape[1] // 4 // LANES
    tc = min(tc, t)
    nch = tc // CHUNK
    cum_all, masks, _ = _hgrn_consts()

    def body(q_ref, f_ref, i_ref, g_ref, lb_ref, on_ref, cum_ref, m_ref, y_ref, o_ref, st_ref, state):
        @pl.when(pl.program_id(1) == 0)
        def _():
            state[...] = jnp.zeros_like(state)

        lbv = _lower_bound(lb_ref)
        onv = on_ref[...]

        def chunk(c, carry):
            rows = pl.ds(pl.multiple_of(c * CHUNK, CHUNK), CHUNK)
            for hh in range(HGRN_HEADS):
                lanes = slice(hh * LANES, (hh + 1) * LANES)
                _, _, g, kk, _, qs = _hgrn_gates(q_ref[rows, lanes], f_ref[rows, lanes], lbv[:, lanes])
                vb = i_ref[rows, lanes].astype(BF16)
                b = _dot_exact_lhs(cum_ref[...], g)
                scores = jnp.zeros((CHUNK, CHUNK), F32)
                for lvl, (ql, kl, _, _) in enumerate(_hgrn_levels(b, qs, kk)):
                    scores = scores + _dot(ql.astype(BF16), kl.astype(BF16), 1, 1) * m_ref[lvl]
                st = state[hh]
                st_ref[hh, c] = st
                o = _dot(scores.astype(BF16), vb) + _dot((qs * jnp.exp(b)).astype(BF16), st.astype(BF16), 1, 1)
                blast = b[CHUNK - 1:CHUNK, :]
                kdec = (kk * jnp.exp(blast - b)).astype(BF16)
                state[hh] = st * jnp.exp(blast) + _dot(vb, kdec, 0, 0)
                o_ref[rows, lanes] = o
                rstd = lax.rsqrt(jnp.mean(o * o, axis=-1, keepdims=True) + RMS_EPS)
                gate = g_ref[rows, lanes]
                y_ref[rows, lanes] = (o * rstd * onv * (gate * _sigmoid(gate))).astype(BF16)
            return carry

        lax.fori_loop(0, nch, chunk, 0, unroll=2)

    hw = HGRN_HEADS * LANES

    def col(off):
        return pl.BlockSpec((tc, hw), lambda h, i: (i, off // HGRN_HEADS + h))

    osp = pl.BlockSpec((tc, hw), lambda h, i: (i, h))
    return pl.pallas_call(
        body, name=name, grid=(nh // HGRN_HEADS, t // tc),
        in_specs=[col(0), col(nh), col(2 * nh), col(3 * nh),
                  pl.BlockSpec((2, hw), lambda h, i: (0, h)),
                  pl.BlockSpec((1, LANES), lambda h, i: (0, 0)),
                  pl.BlockSpec(cum_all.shape, lambda h, i: (0, 0)),
                  pl.BlockSpec(masks.shape, lambda h, i: (0, 0, 0))],
        out_specs=[osp, osp, pl.BlockSpec((HGRN_HEADS, nch, LANES, LANES), lambda h, i: (h, i, 0, 0))],
        out_shape=[jax.ShapeDtypeStruct((t, nh * LANES), BF16), jax.ShapeDtypeStruct((t, nh * LANES), F32),
                   jax.ShapeDtypeStruct((nh, t // CHUNK, LANES, LANES), F32)],
        scratch_shapes=[pltpu.VMEM((HGRN_HEADS, LANES, LANES), F32)],
        compiler_params=_cp())(pc, pc, pc, pc, c_lb, out_norm, cum_all, masks)


def _hgrn_bwd(pc, o_saved, states, dy, c_lb, out_norm, *, name, tc=512):
    t = pc.shape[0]
    nh = pc.shape[1] // 4 // LANES
    tc = min(tc, t)
    nch = tc // CHUNK
    nt = t // tc
    cum_all, masks, suffix = _hgrn_consts()

    def body(q_ref, f_ref, i_ref, g_ref, o_ref, st_ref, dy_ref, lb_ref, on_ref, cum_ref, m_ref, suf_ref,
             dq_ref, df_ref, di_ref, dg_ref, dlb_ref, don_ref, dstate):
        @pl.when(pl.program_id(1) == 0)
        def _():
            dstate[...] = jnp.zeros_like(dstate)
            dlb_ref[...] = jnp.zeros_like(dlb_ref)
            don_ref[...] = jnp.zeros_like(don_ref)

        lbv = _lower_bound(lb_ref)
        onv = on_ref[...]

        def head(hh, c, rows):
            lanes = slice(hh * LANES, (hh + 1) * LANES)
            qr = q_ref[rows, lanes]
            sg, fval, g, kk, sq, qs = _hgrn_gates(qr, f_ref[rows, lanes], lbv[:, lanes])
            vb = i_ref[rows, lanes].astype(BF16)
            o = o_ref[rows, lanes]
            gate = g_ref[rows, lanes]
            sgt = _sigmoid(gate)
            rstd = lax.rsqrt(jnp.mean(o * o, axis=-1, keepdims=True) + RMS_EPS)
            ohat = o * rstd
            dyv = dy_ref[rows, lanes]
            don = dyv * (gate * sgt)
            dg_ref[rows, lanes] = (dyv * ohat * onv * (sgt * (1.0 + gate * (1.0 - sgt)))).astype(BF16)
            don_ref[:, lanes] += jnp.sum(don * ohat, axis=0, keepdims=True)
            dxhat = don * onv
            dob = (rstd * (dxhat - ohat * jnp.mean(dxhat * ohat, axis=-1, keepdims=True))).astype(BF16)
            b = _dot_exact_lhs(cum_ref[...], g)
            blast = b[CHUNK - 1:CHUNK, :]
            eb = jnp.exp(b)
            edec = jnp.exp(blast - b)
            st32 = st_ref[hh, c]
            st = st32.astype(BF16)
            dst = dstate[hh]
            dstb = dst.astype(BF16)
            da = _dot(dob, vb, 1, 1)
            levels = _hgrn_levels(b, qs, kk)
            scores = jnp.zeros((CHUNK, CHUNK), F32)
            dq = eb * _dot(dob, st)
            dk_inter = edec * _dot(vb, dstb)
            dk = dk_inter
            for lvl, (ql, kl, eq, ek) in enumerate(levels):
                mk = m_ref[lvl]
                (qh, qlo), (kh, klo) = _split2(ql), _split2(kl)
                scores = scores + _dot(qh, kh, 1, 1) * mk
                dal = (da * mk).astype(BF16)
                dql = _dot(dal, kh) + _dot(dal, klo)
                dkl = _dot(dal, qh, 0, 0) + _dot(dal, qlo, 0, 0)
                dq = dq + (dql if eq is None else dql * eq)
                dk = dk + (dkl if ek is None else dkl * ek)
            kdec = (kk * edec).astype(BF16)
            dv = _dot(scores.astype(BF16), dob, 0, 0) + _dot(kdec, dstb, 1, 1)
            dstate[hh] = dst * jnp.exp(blast) + _dot(dob, (qs * eb).astype(BF16), 0, 0)
            db = qs * dq - kk * dk
            last = jnp.sum(kk * dk_inter, axis=0, keepdims=True) + jnp.exp(blast) * jnp.sum(dst * st32, axis=0, keepdims=True)
            dgl = _dot_exact_lhs(suf_ref[...], db) + last
            dfv = dgl / fval - dk
            df_ref[rows, lanes] = (dfv * (1.0 - lbv[:, lanes]) * sg * (1.0 - sg)).astype(BF16)
            dlb_ref[:, lanes] += jnp.sum(dfv * (1.0 - sg), axis=0, keepdims=True)
            dq_ref[rows, lanes] = (dq * (sq * (1.0 + qr * (1.0 - sq)))).astype(BF16)
            di_ref[rows, lanes] = dv.astype(BF16)

        def chunk(n, carry):
            c = nch - 1 - n
            rows = pl.ds(pl.multiple_of(c * CHUNK, CHUNK), CHUNK)
            for hh in range(HGRN_HEADS):
                head(hh, c, rows)
            return carry

        lax.fori_loop(0, nch, chunk, 0, unroll=2)

    hw = HGRN_HEADS * LANES

    def col(off):
        return pl.BlockSpec((tc, hw), lambda h, i: (nt - 1 - i, off // HGRN_HEADS + h))

    osp = pl.BlockSpec((tc, hw), lambda h, i: (nt - 1 - i, h))
    vec = pl.BlockSpec((1, hw), lambda h, i: (0, h))
    return pl.pallas_call(
        body, name=name, grid=(nh // HGRN_HEADS, nt),
        in_specs=[col(0), col(nh), col(2 * nh), col(3 * nh), osp,
                  pl.BlockSpec((HGRN_HEADS, nch, LANES, LANES), lambda h, i: (h, nt - 1 - i, 0, 0)),
                  osp,
                  pl.BlockSpec((2, hw), lambda h, i: (0, h)),
                  pl.BlockSpec((1, LANES), lambda h, i: (0, 0)),
                  pl.BlockSpec(cum_all.shape, lambda h, i: (0, 0)),
                  pl.BlockSpec(masks.shape, lambda h, i: (0, 0, 0)),
                  pl.BlockSpec(suffix.shape, lambda h, i: (0, 0))],
        out_specs=[osp, osp, osp, osp, vec, vec],
        out_shape=[jax.ShapeDtypeStruct((t, nh * LANES), BF16)] * 4 + [jax.ShapeDtypeStruct((1, nh * LANES), F32)] * 2,
        scratch_shapes=[pltpu.VMEM((HGRN_HEADS, LANES, LANES), F32)],
        compiler_params=_cp())(pc, pc, pc, pc, o_saved, states, dy, c_lb, out_norm, cum_all, masks, suffix)


HBM_SPEC = pl.BlockSpec(memory_space=pltpu.HBM)


def _gather_scratch():
    return [pltpu.SemaphoreType.DMA((7,)), pltpu.SemaphoreType.DMA((7,)), pltpu.SemaphoreType.DMA]


def _gather_phases(x_ref, out_refs, seg_rows, send_sems, recv_sems, local_sem):
    x, y, c = lax.axis_index("x"), lax.axis_index("y"), lax.axis_index("c")
    me, sibling = (x, y, c), (x, y, 1 - c)
    chips = [(1 - x, y), (x, 1 - y), (1 - x, 1 - y)]
    offs = [sum(seg_rows[:s]) for s in range(len(seg_rows))]
    assert sum(seg_rows) == x_ref.shape[0]

    def index(px, py, pc):
        return 4 * px + 2 * py + pc

    def copies(k, block, to, own):
        return [pltpu.make_async_remote_copy(
            src_ref=x_ref.at[pl.ds(offs[s], n)] if own else out_refs[s].at[index(*block)],
            dst_ref=out_refs[s].at[index(*block)],
            send_sem=send_sems.at[k], recv_sem=recv_sems.at[k], device_id=to, device_id_type=MESH)
            for s, n in enumerate(seg_rows)]

    def all_bytes(k):
        return pltpu.make_async_remote_copy(src_ref=x_ref, dst_ref=x_ref, send_sem=send_sems.at[k],
                                            recv_sem=recv_sems.at[k], device_id=me, device_id_type=MESH)

    mine = [pltpu.make_async_copy(x_ref.at[pl.ds(offs[s], n)], out_refs[s].at[index(*me)], local_sem)
            for s, n in enumerate(seg_rows)]
    first = copies(0, me, sibling, True)
    for j, chip in enumerate(chips):
        first += copies(1 + j, me, (*chip, c), True)

    def start():
        for cp in mine + first:
            cp.start()

    def forward():
        for j, chip in enumerate(chips):
            all_bytes(1 + j).wait_recv()
            for cp in copies(4 + j, (*chip, c), sibling, False):
                cp.start()

    def finish():
        all_bytes(0).wait_recv()
        for j in range(3):
            all_bytes(4 + j).wait_recv()
        for k in range(7):
            all_bytes(k).wait_send()
        pltpu.make_async_copy(x_ref, x_ref, local_sem).wait()

    return start, forward, finish


def _all_gather(xs, seg_rows=None, *, name):
    segs = [xs.shape[0]] if seg_rows is None else list(seg_rows)

    def body(x_ref, *rest):
        start, forward, finish = _gather_phases(x_ref, rest[:len(segs)], segs, *rest[len(segs):])
        start()
        forward()
        finish()

    outs = pl.pallas_call(
        body, name=name, in_specs=[HBM_SPEC], out_specs=[HBM_SPEC] * len(segs),
        out_shape=[jax.ShapeDtypeStruct((8, n, xs.shape[1]), xs.dtype) for n in segs],
        scratch_shapes=_gather_scratch())(xs)
    return outs[0] if seg_rows is None else outs


def _sibling_exchange(s, *, name):
    def body(s_ref, rb_ref, send_sem, recv_sem):
        x, y, c = lax.axis_index("x"), lax.axis_index("y"), lax.axis_index("c")
        cp = pltpu.make_async_remote_copy(
            src_ref=s_ref.at[:, 1 - c], dst_ref=rb_ref, send_sem=send_sem, recv_sem=recv_sem,
            device_id=(x, y, 1 - c), device_id_type=MESH)
        cp.start()
        cp.wait()

    return pl.pallas_call(
        body, name=name, in_specs=[HBM_SPEC], out_specs=HBM_SPEC,
        out_shape=jax.ShapeDtypeStruct(s.shape[:1] + s.shape[2:], s.dtype),
        scratch_shapes=[pltpu.SemaphoreType.DMA, pltpu.SemaphoreType.DMA])(s)


def _row_tile(n, cap=1024):
    return max(b for b in range(16, cap + 1, 16) if n % b == 0)


def _pair_add(s, rb, core, *, name):
    nchip, _, r, c = s.shape
    tb = _row_tile(r)

    def body(core_ref, a_ref, b_ref, o_ref):
        o_ref[...] = (a_ref[...].astype(F32) + b_ref[...].astype(F32)).astype(BF16)

    blk = pl.BlockSpec((None, tb, c), lambda ch, i, cr: (ch, i, 0))
    return pl.pallas_call(
        body, name=name,
        grid_spec=pltpu.PrefetchScalarGridSpec(
            num_scalar_prefetch=1, grid=(nchip, r // tb),
            in_specs=[pl.BlockSpec((None, None, tb, c), lambda ch, i, cr: (ch, cr[0], i, 0)), blk],
            out_specs=blk),
        out_shape=jax.ShapeDtypeStruct((nchip, r, c), BF16), compiler_params=_cp())(core, s, rb)


def _chip_exchange_scratch():
    return [pltpu.SemaphoreType.DMA((3,)), pltpu.SemaphoreType.DMA((3,)), pltpu.SemaphoreType.DMA]


def _chip_exchange_phases(p_ref, out_ref, send_sems, recv_sems, local_sem):
    x, y, c = lax.axis_index("x"), lax.axis_index("y"), lax.axis_index("c")
    mine = 2 * x + y
    own = pltpu.make_async_copy(p_ref.at[mine], out_ref.at[mine], local_sem)
    copies = [pltpu.make_async_remote_copy(
        src_ref=p_ref.at[2 * tx + ty], dst_ref=out_ref.at[mine],
        send_sem=send_sems.at[k], recv_sem=recv_sems.at[k], device_id=(tx, ty, c), device_id_type=MESH)
        for k, (tx, ty) in enumerate([(1 - x, y), (x, 1 - y), (1 - x, 1 - y)])]

    def start():
        own.start()
        for cp in copies:
            cp.start()

    def finish():
        for cp in copies:
            cp.wait()
        own.wait()

    return start, finish


def _adamw_math(w, g, m, v):
    m2 = ADAM_B1 * m + (1.0 - ADAM_B1) * g
    v2 = ADAM_B2 * v + (1.0 - ADAM_B2) * (g * g)
    m_hat = m2 / (1.0 - ADAM_B1 ** ADAM_STEP)
    v_hat = v2 / (1.0 - ADAM_B2 ** ADAM_STEP)
    return -ADAM_LR * (m_hat / (jnp.sqrt(v_hat) + ADAM_EPS) + ADAM_WD * w), m2, v2


def _adamw_shard(parts, g_off, w, m, v, layer, prev, *, name):
    _, r, c = w.shape
    tb = next(b for b in range(min(r, 512), 0, -16) if r % b == 0 and g_off % b == 0)

    def body(p0, p1, p2, p3, w_ref, m_ref, v_ref, *rest):
        g_out, d_out, m_out, v_out = rest[-4:]
        g = ((p0[...].astype(F32) + p1[...].astype(F32)) + p2[...].astype(F32)) + p3[...].astype(F32)
        d, m2, v2 = _adamw_math(w_ref[...], g, m_ref[...], v_ref[...])
        g_out[...] = g
        d_out[...] = d
        m_out[...] = m2
        v_out[...] = v2

    def part(ch):
        return pl.BlockSpec((None, tb, c), lambda i: (ch, g_off // tb + i, 0))

    blk = pl.BlockSpec((None, tb, c), lambda i: (layer, i, 0))
    prev = list(prev) if prev is not None else []
    return pl.pallas_call(
        body, name=name, grid=(r // tb,),
        in_specs=[part(0), part(1), part(2), part(3), blk, blk, blk] + [pl.BlockSpec(memory_space=pl.ANY)] * len(prev),
        out_specs=[blk] * 4, out_shape=[jax.ShapeDtypeStruct(w.shape, F32)] * 4,
        input_output_aliases={7 + k: k for k in range(len(prev))},
        compiler_params=_cp())(parts, parts, parts, parts, w, m, v, *prev)


SLOT = 8
SMALL_ROWS = 6 * SLOT
ROW_LB = 4 * SLOT


def _small_update(gath, w, m, v, *, name):
    def body(g_ref, w_ref, m_ref, v_ref, g_out, d_out, m_out, v_out):
        tot = g_ref[0]
        for k in range(1, 8):
            tot = tot + g_ref[k]
        wv = w_ref[...]
        c0, c1 = wv[ROW_LB:ROW_LB + 1, :], wv[ROW_LB + 1:ROW_LB + 2, :]
        mx = jnp.maximum(c0, c1)
        e0, e1 = jnp.exp(c0 - mx), jnp.exp(c1 - mx)
        lb = e1 / (e0 + e1)
        gl = tot[ROW_LB:ROW_LB + 1, :] * lb * (1.0 - lb)
        row = lax.broadcasted_iota(jnp.int32, tot.shape, 0)
        g = jnp.where(row == ROW_LB, -gl, jnp.where(row == ROW_LB + 1, gl, tot))
        d, m2, v2 = _adamw_math(wv, g, m_ref[...], v_ref[...])
        g_out[...] = g
        d_out[...] = d
        m_out[...] = m2
        v_out[...] = v2

    return pl.pallas_call(
        body, name=name, out_shape=[jax.ShapeDtypeStruct(w.shape, F32)] * 4, compiler_params=_cp())(gath, w, m, v)


D_MODEL = 1024


def _ffn_fwd(h, gain, wg, wu, wd, tag):
    xn, gg, uu, act = _norm_gate_up(h, gain, wg, wu, name=f"{tag}_gate_up")
    out = _mm([(act, wd)], residual=h, alpha=MACARON, tn=1024, name=f"{tag}_down")
    return out, (h, xn, gg, uu, act)


def _ffn_input_bwd(dg, du, wg, wu, x, gain, dres, chip_part, *, name, scale, tm=256):
    t, d = x.shape
    f = wg.shape[0]
    tm = min(tm, t)
    nt = t // tm
    fused = chip_part is not None

    def body(dg_ref, du_ref, wg_ref, wu_ref, x_ref, g_ref, dres_ref, *rest):
        if fused:
            part_ref, dx_ref, dxb_ref, dgain_ref, parts_ref = rest[:5]
            start, finish = _chip_exchange_phases(part_ref, parts_ref, *rest[5:])
            pl.when(pl.program_id(0) == 0)(start)
        else:
            dx_ref, dxb_ref, dgain_ref = rest
        dxn_v = _dot(dg_ref[...], wg_ref[...]) + _dot(du_ref[...], wu_ref[...])
        xv = x_ref[...]
        rstd = lax.rsqrt(jnp.mean(xv * xv, axis=-1, keepdims=True) + RMS_EPS)
        xhat = xv * rstd
        dxhat = dxn_v * g_ref[...]
        dx = dres_ref[...] + rstd * (dxhat - xhat * jnp.mean(dxhat * xhat, axis=-1, keepdims=True))
        dx_ref[...] = dx
        dxb_ref[...] = (dx * scale).astype(BF16)

        @pl.when(pl.program_id(0) == 0)
        def _():
            dgain_ref[...] = jnp.zeros_like(dgain_ref)

        dgain_ref[...] += jnp.sum(dxn_v * xhat, axis=0, keepdims=True)
        if fused:
            pl.when(pl.program_id(0) == nt - 1)(finish)

    wide = pl.BlockSpec((tm, f), lambda i: (i, 0))
    wsp = pl.BlockSpec((f, d), lambda i: (0, 0))
    row = pl.BlockSpec((tm, d), lambda i: (i, 0))
    vec = pl.BlockSpec((1, d), lambda i: (0, 0))
    args = [dg, du, wg, wu, x, gain, dres] + ([chip_part] if fused else [])
    return pl.pallas_call(
        body, name=name, grid=(nt,),
        in_specs=[wide, wide, wsp, wsp, row, vec, row] + ([HBM_SPEC] if fused else []),
        out_specs=[row, row, vec] + ([HBM_SPEC] if fused else []),
        out_shape=[jax.ShapeDtypeStruct((t, d), F32), jax.ShapeDtypeStruct((t, d), BF16), jax.ShapeDtypeStruct((1, d), F32)]
        + ([jax.ShapeDtypeStruct(chip_part.shape, chip_part.dtype)] if fused else []),
        scratch_shapes=_chip_exchange_scratch() if fused else [],
        compiler_params=_cp(dimension_semantics=("arbitrary",)))(*args)


def _ffn_bwd(dout, dout_half, saved, gain, wg, wu, wd, tag, next_scale, make_chip_part=None, early_chip_part=None):
    h, xn, gg, uu, act = saved
    dg, du, *early_parts = _swiglu_bwd(dout_half, wd, gg, uu, early_chip_part, name=f"{tag}_dact")
    dwd = _mm([(act, dout_half)], ta=True, tm=256, tn=1024, out_dtype=BF16, name=f"{tag}_dwd")
    dwg, dwu = _mm_shared_rhs([dg, du], xn, tm=256, name=f"{tag}_dwgu")
    chip_part = make_chip_part(dwg, dwu, dwd) if make_chip_part is not None else None
    dh, dh_b, dgain, *parts = _ffn_input_bwd(dg, du, wg, wu, h, gain, dout, chip_part, scale=next_scale,
                                             name=f"{tag}_input_bwd")
    return dh, dh_b, dwg, dwu, dwd, dgain, (parts[0] if parts else None), (early_parts[0] if early_parts else None)


def kernel(x, ffn_pre_norm, ffn_pre_w_gate, ffn_pre_w_up, ffn_pre_w_down, mix_norm, ffn_post_norm, ffn_post_w_gate, ffn_post_w_up, ffn_post_w_down, ab_w_in, ab_conv_w, ab_w_out, c_w_in, c_lower_bounds, c_out_norm, c_w_out, final_norm, loss_target, m_ffn_pre_norm, m_ffn_pre_w_gate, m_ffn_pre_w_up, m_ffn_pre_w_down, m_mix_norm, m_ffn_post_norm, m_ffn_post_w_gate, m_ffn_post_w_up, m_ffn_post_w_down, m_ab_w_in, m_ab_conv_w, m_ab_w_out, m_c_w_in, m_c_lower_bounds, m_c_out_norm, m_c_w_out, m_final_norm, v_ffn_pre_norm, v_ffn_pre_w_gate, v_ffn_pre_w_up, v_ffn_pre_w_down, v_mix_norm, v_ffn_post_norm, v_ffn_post_w_gate, v_ffn_post_w_up, v_ffn_post_w_down, v_ab_w_in, v_ab_conv_w, v_ab_w_out, v_c_w_in, v_c_lower_bounds, v_c_out_norm, v_c_w_out, v_final_norm):
    d = D_MODEL
    h0 = x[0]
    target = loss_target[0]
    core = lax.axis_index("c").astype(jnp.int32).reshape(1)

    big = [("pre_g", ffn_pre_w_gate, m_ffn_pre_w_gate, v_ffn_pre_w_gate),
           ("pre_u", ffn_pre_w_up, m_ffn_pre_w_up, v_ffn_pre_w_up),
           ("pre_d", ffn_pre_w_down, m_ffn_pre_w_down, v_ffn_pre_w_down),
           ("post_g", ffn_post_w_gate, m_ffn_post_w_gate, v_ffn_post_w_gate),
           ("post_u", ffn_post_w_up, m_ffn_post_w_up, v_ffn_post_w_up),
           ("post_d", ffn_post_w_down, m_ffn_post_w_down, v_ffn_post_w_down),
           ("ab_in", ab_w_in, m_ab_w_in, v_ab_w_in),
           ("ab_out", ab_w_out, m_ab_w_out, v_ab_w_out),
           ("c_in", c_w_in, m_c_w_in, v_c_w_in),
           ("c_out", c_w_out, m_c_w_out, v_c_w_out)]
    by_tag = {tag: (w, m, v) for tag, w, m, v in big}

    def layer_rows(tag):
        w = by_tag[tag][0]
        return w.size // d // w.shape[0]

    def layout(items):
        offs, off = {}, 0
        for item in items:
            offs[item] = off
            off += layer_rows(item[0])
        return offs, off

    ffn = [f"{pos}_{kind}" for pos in ("pre", "post") for kind in "gud"]
    first_items = [("pre_g", 0), ("pre_u", 0)]
    early_items = [("pre_d", 0), ("ab_in", 0)]
    late_items = ([("pre_g", 1), ("pre_u", 1), ("pre_d", 1)] + [(f"post_{kind}", l) for l in (0, 1) for kind in "gud"]
                  + [("ab_out", 0), ("c_in", 0), ("c_out", 0)])
    grad_items = {"A": ([(tag, 1) for tag in ffn] + [(f"post_{kind}", 0) for kind in "gud"]
                        + [("c_in", 0), ("c_out", 0), ("ab_out", 0)]),
                  "B": [(f"pre_{kind}", 0) for kind in "gud"], "C": [("ab_in", 0)]}
    grad_offs = {k: layout(items)[0] for k, items in grad_items.items()}
    grad_conv_row = layout(grad_items["C"])[1]

    def conv_rows(a, split):
        flat = a.reshape(-1)
        if split:
            hi = flat.astype(BF16)
            flat = jnp.concatenate([hi, (flat - hi.astype(F32)).astype(BF16)])
        return jnp.zeros((16, d), flat.dtype).at[0, :flat.shape[0]].set(flat)

    nconv = ab_conv_w.size
    col_sharded = {"pre_g", "pre_u", "post_g", "post_u", "ab_in", "c_in"}

    def pack_rows(item):
        tag, layer = item
        a = by_tag[tag][0][layer]
        return (a.T if tag in col_sharded else a).reshape(-1, d).astype(BF16)

    first_pack = jnp.concatenate([pack_rows(item) for item in first_items], axis=0)
    early_pack = jnp.concatenate([pack_rows(item) for item in early_items] + [conv_rows(ab_conv_w, True)], axis=0)
    late_pack = jnp.concatenate([pack_rows(item) for item in late_items], axis=0)
    first_w = _all_gather(first_pack, [layer_rows(tag) for tag, _ in first_items], name="gather_first_weights")
    full = {item: g.reshape(-1, d) for item, g in zip(first_items, first_w)}

    xn0, gg0, uu0, act0, *early_w = _norm_gate_up(
        h0, ffn_pre_norm[0:1], full["pre_g", 0], full["pre_u", 0], name="l0pre_gate_up_gather_early_weights",
        pack=early_pack, seg_rows=[layer_rows(tag) for tag, _ in early_items] + [16])
    full.update({item: g.reshape(-1, d) for item, g in zip(early_items, early_w)})
    ffn_w = {("pre", 0): tuple(full[f"pre_{kind}", 0] for kind in "gud")}
    w_ab_in = full["ab_in", 0]
    cg = early_w[-1][:, 0, :2 * nconv].astype(F32)
    conv_w = (cg[:, :nconv] + cg[:, nconv:]).reshape(8, 3, -1).transpose(1, 0, 2).reshape(3, -1)
    aw = w_ab_in.shape[0] // 6
    h1 = _mm([(act0, full["pre_d", 0])], residual=h0, alpha=MACARON, tn=1024, name="l0pre_down")
    s_pre0 = (h0, xn0, gg0, uu0, act0)
    hn0, pa, pb = _norm_proj(h1, mix_norm[0:1], w_ab_in, (F32, BF16), tm=512, name="ab_norm_proj")
    ya = _conv_fwd(pa, conv_w, name="conv_fwd")
    yb, ltot, *late_w = _attn_fwd(pb, late_pack, [layer_rows(tag) for tag, _ in late_items],
                                  name="attn_fwd_gather_late_weights")
    full.update({item: g.reshape(-1, d) for item, g in zip(late_items, late_w)})
    for pos, layer in (("post", 0), ("pre", 1), ("post", 1)):
        ffn_w[pos, layer] = tuple(full[f"{pos}_{kind}", layer] for kind in "gud")
    w_ab_out, w_c_in, w_c_out = full["ab_out", 0], full["c_in", 0], full["c_out", 0]
    h2 = _mm([(ya, w_ab_out[:aw]), (yb, w_ab_out[aw:])], residual=h1, tn=1024, name="ab_out")
    h3, s_post0 = _ffn_fwd(h2, ffn_post_norm[0:1], *ffn_w["post", 0], "l0post")
    h4, s_pre1 = _ffn_fwd(h3, ffn_pre_norm[1:2], *ffn_w["pre", 1], "l1pre")
    hn1, pc = _norm_proj(h4, mix_norm[1:2], w_c_in, (F32,), tm=256, name="c_norm_proj")
    yc, o_saved, states = _hgrn_fwd(pc, c_lower_bounds, c_out_norm, name="hgrn_fwd")
    h5 = _mm([(yc, w_c_out)], residual=h4, tn=1024, name="c_out")
    h6, s_post1 = _ffn_fwd(h5, ffn_post_norm[1:2], *ffn_w["post", 1], "l1post")
    dh6, dh6_b, d_final, loss_vec = _loss_head(h6, final_norm.reshape(1, d), target, name="loss_head")

    gw = {}
    dh5, dh5_b, gw["post_g", 1], gw["post_u", 1], gw["post_d", 1], d_post1, *_ = _ffn_bwd(
        dh6, dh6_b, s_post1, ffn_post_norm[1:2], *ffn_w["post", 1], "l1post", 1.0)
    dyc = _mm([(dh5_b, w_c_out)], tb=True, tn=1024, name="c_out_dy")
    g_c_out = _mm([(yc, dh5_b)], ta=True, tm=256, tn=1024, out_dtype=BF16, name="c_out_dw")
    dcq, dcf, dci, dcg, dlb, d_onorm = _hgrn_bwd(pc, o_saved, states, dyc, c_lower_bounds, c_out_norm, name="hgrn_bwd")
    dparts = [dcq, dcf, dci, dcg]
    g_c_in = jnp.concatenate(_mm_shared_rhs(dparts, hn1, tm=256, name="c_in_dw"), axis=0)
    cw = w_c_in.shape[0] // 4
    dhn1 = _mm([(dp, w_c_in[i * cw:(i + 1) * cw]) for i, dp in enumerate(dparts)], tm=512, tn=1024, name="c_in_dx")
    dh4, dh4_b, d_mix1 = _rmsnorm_bwd(h4, mix_norm[1:2], dhn1, dh5, scale=MACARON, name="l1_mix_norm_bwd")
    dh3, dh3_b, gw["pre_g", 1], gw["pre_u", 1], gw["pre_d", 1], d_pre1, *_ = _ffn_bwd(
        dh4, dh4_b, s_pre1, ffn_pre_norm[1:2], *ffn_w["pre", 1], "l1pre", MACARON)
    dh2, dh2_b, gw["post_g", 0], gw["post_u", 0], gw["post_d", 0], d_post0, *_ = _ffn_bwd(
        dh3, dh3_b, s_post0, ffn_post_norm[0:1], *ffn_w["post", 0], "l0post", 1.0)
    dyab = _mm([(dh2_b, w_ab_out)], tb=True, tn=1024, name="ab_out_dy")
    g_ab_out = jnp.concatenate(_mm_shared_rhs([ya, yb], dh2_b, tm=256, name="ab_out_dw"), axis=0)
    dab, dac, dax, g_conv = _conv_bwd(pa, dyab, conv_w, name="conv_bwd")

    def chip_partials(key, grads, extra=()):
        gpack = jnp.concatenate([grads[item].reshape(8, -1, d) for item in grad_items[key]] + list(extra), axis=1)
        send = gpack.reshape(4, 2, gpack.shape[1], d)
        from_sibling = _sibling_exchange(send, name=f"grad{key}_sibling_exchange")
        return _pair_add(send, from_sibling, core, name=f"grad{key}_pair_add")

    gw["c_in", 0], gw["c_out", 0], gw["ab_out", 0] = g_c_in, g_c_out, g_ab_out
    chip_part_a = chip_partials("A", gw)
    dq, dk, dv, parts_a = _attn_bwd(pb, dyab, ltot, chip_part_a, name="attn_bwd_exchange_grads_a")
    dparts = [dab, dac, dax, dq, dk, dv]
    g_ab_in = jnp.concatenate(_mm_shared_rhs(dparts, hn0, tm=128, name="ab_in_dw"), axis=0)
    dhn0 = _mm([(dp, w_ab_in[i * aw:(i + 1) * aw]) for i, dp in enumerate(dparts)], tm=512, tn=1024, name="ab_in_dx")
    dh1, dh1_b, d_mix0 = _rmsnorm_bwd(h1, mix_norm[0:1], dhn0, dh2, scale=MACARON, name="l0_mix_norm_bwd")
    gw["ab_in", 0] = g_ab_in
    gconv_own = g_conv.reshape(3, 8, -1).transpose(1, 0, 2).reshape(8, -1)
    conv_piece = jnp.zeros((8, 16, d), F32).at[:, 0, :nconv].set(gconv_own).astype(BF16)

    def chip_part_b(dwg, dwu, dwd):
        gw["pre_g", 0], gw["pre_u", 0], gw["pre_d", 0] = dwg, dwu, dwd
        return chip_partials("B", gw)

    dh0, _, _, _, _, d_pre0, parts_b, parts_c = _ffn_bwd(
        dh1, dh1_b, s_pre0, ffn_pre_norm[0:1], *ffn_w["pre", 0], "l0pre", 1.0, chip_part_b,
        chip_partials("C", gw, [conv_piece]))

    parts = {"A": parts_a, "B": parts_b, "C": parts_c}
    upd = {}
    for tag, w, m, v in big:
        view = (lambda a: jnp.swapaxes(a, 1, 2)) if tag in col_sharded else (lambda a: a)
        where = {layer: (key, grad_offs[key][tag, layer])
                 for key in grad_items for t2, layer in grad_items[key] if t2 == tag}
        res = None
        for layer in sorted(where):
            key, off = where[layer]
            res = _adamw_shard(parts[key], off, view(w), view(m), view(v), layer, res, name=f"adamw_{tag}{layer}")
        upd[tag] = [view(a) for a in res]
    res = _adamw_shard(parts["C"], grad_conv_row, *(conv_rows(a, False)[None] for a in (ab_conv_w, m_ab_conv_w, v_ab_conv_w)),
                       0, None, name="adamw_conv")
    upd["conv"] = [r[0, 0, :nconv].reshape(ab_conv_w.shape) for r in res]

    def small_pack(pre, mix, post, final, lbs, onorm):
        def slot(parts):
            out, r = jnp.zeros((SLOT, d), F32), 0
            for a in (parts if isinstance(parts, tuple) else (parts,)):
                out = out.at[r:r + a.shape[0], :a.shape[1]].set(a)
                r += a.shape[0]
            return out

        return jnp.concatenate([slot(pre), slot(mix), slot(post), slot(final.reshape(1, d)), slot(lbs), slot(onorm)], axis=0)

    d_on = d_onorm.reshape(-1, c_out_norm.shape[1]).sum(axis=0, keepdims=True)
    gsmall = small_pack((d_pre0, d_pre1), (d_mix0, d_mix1), (d_post0, d_post1), d_final, dlb, d_on)
    gsmall_all = _all_gather(gsmall, name="gather_small_grads")
    sres = _small_update(
        gsmall_all,
        small_pack(ffn_pre_norm, mix_norm, ffn_post_norm, final_norm, c_lower_bounds, c_out_norm),
        small_pack(m_ffn_pre_norm, m_mix_norm, m_ffn_post_norm, m_final_norm, m_c_lower_bounds, m_c_out_norm),
        small_pack(v_ffn_pre_norm, v_mix_norm, v_ffn_post_norm, v_final_norm, v_c_lower_bounds, v_c_out_norm),
        name="small_update")

    def small_out(r):
        return {"pre_norm": r[0:2], "mix_norm": r[SLOT:SLOT + 2], "post_norm": r[2 * SLOT:2 * SLOT + 2],
                "final": r[3 * SLOT], "lb": r[ROW_LB:ROW_LB + 2], "onorm": r[5 * SLOT:5 * SLOT + 1, :c_out_norm.shape[1]]}

    small = [small_out(r) for r in sres]
    outs = []
    for k in range(4):
        s = small[k]
        outs += [s["pre_norm"], upd["pre_g"][k], upd["pre_u"][k], upd["pre_d"][k], s["mix_norm"], s["post_norm"],
                 upd["post_g"][k], upd["post_u"][k], upd["post_d"][k], upd["ab_in"][k], upd["conv"][k],
                 upd["ab_out"][k], upd["c_in"][k], s["lb"], s["onorm"], upd["c_out"][k], s["final"]]
    loss = lax.psum(loss_vec[0, 0], ("x", "y", "c"))
    return (loss, dh0[None], *outs)
```

```python
import functools
import math

import jax
import jax.numpy as jnp
from jax import lax
from jax.experimental import pallas as pl
from jax.experimental.pallas import tpu as pltpu

F32 = jnp.float32
BF16 = jnp.bfloat16
MESH = pl.DeviceIdType.MESH

RMS_EPS = 1e-6
MACARON = 0.5
LANES = 128
CHUNK = 64
N_LEVELS = 6
HGRN_HEADS = 2
SB_KEYS = 256
ADAM_LR, ADAM_B1, ADAM_B2, ADAM_EPS, ADAM_WD, ADAM_STEP = 0.001, 0.9, 0.999, 1e-08, 0.01, 10
VMEM_LIMIT = 48 * 1024 * 1024


def _cp(**kw):
    return pltpu.CompilerParams(vmem_limit_bytes=VMEM_LIMIT, **kw)


def _sigmoid(x):
    return 0.5 * jnp.tanh(0.5 * x) + 0.5


def _bf(x):
    return x if x.dtype == BF16 else x.astype(BF16)


def _split3(x):
    hi = x.astype(BF16)
    r1 = x - hi.astype(F32)
    mid = r1.astype(BF16)
    lo = (r1 - mid.astype(F32)).astype(BF16)
    return hi, mid, lo


def _dot(a, b, ca=1, cb=0):
    return lax.dot_general(a, b, (((ca,), (cb,)), ((), ())), preferred_element_type=F32)


def _dot_exact_lhs(m, x):
    hi, mid, lo = _split3(x)
    return _dot(m, hi) + _dot(m, mid) + _dot(m, lo)


def _dot_exact_rhs(x, m):
    hi, mid, lo = _split3(x)
    return _dot(hi, m) + _dot(mid, m) + _dot(lo, m)


def _mm(terms, *, name, ta=False, tb=False, out_dtype=F32, residual=None, alpha=1.0, tm=512, tn=512):
    nt = len(terms)
    a0, b0 = terms[0]
    m = a0.shape[1] if ta else a0.shape[0]
    n = b0.shape[0] if tb else b0.shape[1]
    tm, tn = min(tm, m), min(tn, n)
    assert m % tm == 0 and n % tn == 0, (name, m, n, tm, tn)
    has_res = residual is not None

    def body(*refs):
        o_ref = refs[-1]
        acc = None
        for i in range(nt):
            a = _bf(refs[2 * i][...])
            b = _bf(refs[2 * i + 1][...])
            p = _dot(a, b, 0 if ta else 1, 1 if tb else 0)
            acc = p if acc is None else acc + p
        if alpha != 1.0:
            acc = acc * alpha
        if has_res:
            acc = acc + refs[2 * nt][...]
        o_ref[...] = acc.astype(out_dtype)

    in_specs, args = [], []
    for a, b in terms:
        k = a.shape[0] if ta else a.shape[1]
        assert (b.shape[1] if tb else b.shape[0]) == k, (name, a.shape, b.shape)
        in_specs.append(pl.BlockSpec((k, tm), lambda i, j: (0, i)) if ta else pl.BlockSpec((tm, k), lambda i, j: (i, 0)))
        in_specs.append(pl.BlockSpec((tn, k), lambda i, j: (j, 0)) if tb else pl.BlockSpec((k, tn), lambda i, j: (0, j)))
        args += [a, b]
    if has_res:
        in_specs.append(pl.BlockSpec((tm, tn), lambda i, j: (i, j)))
        args.append(residual)
    return pl.pallas_call(
        body, name=name, grid=(m // tm, n // tn), in_specs=in_specs,
        out_specs=pl.BlockSpec((tm, tn), lambda i, j: (i, j)),
        out_shape=jax.ShapeDtypeStruct((m, n), out_dtype), compiler_params=_cp())(*args)


def _norm_proj(x, gain, w_t, out_dtypes, *, name, tm):
    t, d = x.shape
    n = w_t.shape[0]
    tm = min(tm, t)
    npart = len(out_dtypes)
    width = n // npart

    def body(x_ref, g_ref, w_ref, xn_ref, *part_refs):
        xv = x_ref[...]
        rstd = lax.rsqrt(jnp.mean(xv * xv, axis=-1, keepdims=True) + RMS_EPS)
        xn = (xv * rstd * g_ref[...]).astype(BF16)
        xn_ref[...] = xn
        for p, ref in enumerate(part_refs):
            ref[...] = _dot(xn, w_ref[p * width:(p + 1) * width, :], 1, 1).astype(out_dtypes[p])

    row = pl.BlockSpec((tm, d), lambda i: (i, 0))
    return pl.pallas_call(
        body, name=name, grid=(t // tm,),
        in_specs=[row, pl.BlockSpec((1, d), lambda i: (0, 0)), pl.BlockSpec((n, d), lambda i: (0, 0))],
        out_specs=[row] + [pl.BlockSpec((tm, width), lambda i: (i, 0))] * npart,
        out_shape=[jax.ShapeDtypeStruct((t, d), BF16)] + [jax.ShapeDtypeStruct((t, width), dt) for dt in out_dtypes],
        compiler_params=_cp())(x, gain, w_t)


def _mm_shared_rhs(a_list, b, *, name, tm, out_dtype=BF16):
    k, n = b.shape
    assert all(a.shape[0] == k and a.shape[1] % tm == 0 and a.shape[1] == a_list[0].shape[1] for a in a_list)
    m = a_list[0].shape[1]
    na = len(a_list)

    def body(*refs):
        bv = refs[na][...]
        for i in range(na):
            refs[na + 1 + i][...] = _dot(refs[i][...], bv, 0, 0).astype(out_dtype)

    return pl.pallas_call(
        body, name=name, grid=(m // tm,),
        in_specs=[pl.BlockSpec((k, tm), lambda i: (0, i))] * na + [pl.BlockSpec((k, n), lambda i: (0, 0))],
        out_specs=[pl.BlockSpec((tm, n), lambda i: (i, 0))] * na,
        out_shape=[jax.ShapeDtypeStruct((m, n), out_dtype)] * na, compiler_params=_cp())(*a_list, b)


def _rmsnorm_bwd(x, gain, dxn, dres, *, name, scale, tm=512):
    t, d = x.shape
    tm = min(tm, t)

    def body(x_ref, g_ref, dxn_ref, dres_ref, dx_ref, dxb_ref, dg_ref):
        xv = x_ref[...]
        rstd = lax.rsqrt(jnp.mean(xv * xv, axis=-1, keepdims=True) + RMS_EPS)
        xhat = xv * rstd
        dxn_v = dxn_ref[...]
        dxhat = dxn_v * g_ref[...]
        dx = dres_ref[...] + rstd * (dxhat - xhat * jnp.mean(dxhat * xhat, axis=-1, keepdims=True))
        dx_ref[...] = dx
        dxb_ref[...] = (dx * scale).astype(BF16)

        @pl.when(pl.program_id(0) == 0)
        def _():
            dg_ref[...] = jnp.zeros_like(dg_ref)

        dg_ref[...] += jnp.sum(dxn_v * xhat, axis=0, keepdims=True)

    row = pl.BlockSpec((tm, d), lambda i: (i, 0))
    vec = pl.BlockSpec((1, d), lambda i: (0, 0))
    return pl.pallas_call(
        body, name=name, grid=(t // tm,), in_specs=[row, vec, row, row], out_specs=[row, row, vec],
        out_shape=[jax.ShapeDtypeStruct((t, d), F32), jax.ShapeDtypeStruct((t, d), BF16), jax.ShapeDtypeStruct((1, d), F32)],
        compiler_params=_cp())(x, gain, dxn, dres)


def _loss_head(h, gain, target, *, name, tm=512):
    t, d = h.shape
    tm = min(tm, t)

    def body(h_ref, g_ref, t_ref, dh_ref, dhb_ref, dg_ref, loss_ref):
        hv = h_ref[...]
        rstd = lax.rsqrt(jnp.mean(hv * hv, axis=-1, keepdims=True) + RMS_EPS)
        xhat = hv * rstd
        err = xhat * g_ref[...] - t_ref[...]
        dy = err * (1.0 / d)
        dxhat = dy * g_ref[...]
        dh = rstd * (dxhat - xhat * jnp.mean(dxhat * xhat, axis=-1, keepdims=True))
        dh_ref[...] = dh
        dhb_ref[...] = (dh * MACARON).astype(BF16)

        @pl.when(pl.program_id(0) == 0)
        def _():
            dg_ref[...] = jnp.zeros_like(dg_ref)
            loss_ref[...] = jnp.zeros_like(loss_ref)

        dg_ref[...] += jnp.sum(dy * xhat, axis=0, keepdims=True)
        part = jnp.sum(jnp.sum(err * err, axis=-1, keepdims=True), axis=0, keepdims=True) * (0.5 / d)
        loss_ref[...] += jnp.broadcast_to(part, loss_ref.shape)

    row = pl.BlockSpec((tm, d), lambda i: (i, 0))
    vec = pl.BlockSpec((1, d), lambda i: (0, 0))
    return pl.pallas_call(
        body, name=name, grid=(t // tm,), in_specs=[row, vec, row],
        out_specs=[row, row, vec, pl.BlockSpec((1, LANES), lambda i: (0, 0))],
        out_shape=[jax.ShapeDtypeStruct((t, d), F32), jax.ShapeDtypeStruct((t, d), BF16), jax.ShapeDtypeStruct((1, d), F32),
                   jax.ShapeDtypeStruct((1, LANES), F32)],
        compiler_params=_cp())(h, gain, target)


def _norm_gate_up(x, gain, wg, wu, *, name, tm=256, tf=2816, pack=None, seg_rows=()):
    t, d = x.shape
    f = wg.shape[0]
    tm, tf = min(tm, t), min(tf, f)
    assert f % tf == 0
    ni, nj = t // tm, f // tf
    nseg = len(seg_rows)

    def body(x_ref, g_ref, wg_ref, wu_ref, *rest):
        if pack is not None:
            pack_ref, xn_ref, gg_ref, uu_ref, act_ref = rest[:5]
            start, forward, finish = _gather_phases(pack_ref, rest[5:5 + nseg], seg_rows, *rest[5 + nseg:])
            step = pl.program_id(0) * nj + pl.program_id(1)
            pl.when(step == 0)(start)
            pl.when(step == (3 * ni * nj) // 4)(forward)
        else:
            xn_ref, gg_ref, uu_ref, act_ref = rest

        @pl.when(pl.program_id(1) == 0)
        def _():
            xv = x_ref[...]
            rstd = lax.rsqrt(jnp.mean(xv * xv, axis=-1, keepdims=True) + RMS_EPS)
            xn_ref[...] = (xv * rstd * g_ref[...]).astype(BF16)

        xn = xn_ref[...]
        gv = _dot(xn, wg_ref[...], 1, 1)
        uv = _dot(xn, wu_ref[...], 1, 1)
        gg_ref[...] = gv.astype(BF16)
        uu_ref[...] = uv.astype(BF16)
        act_ref[...] = (gv * _sigmoid(gv) * uv).astype(BF16)
        if pack is not None:
            pl.when(step == ni * nj - 1)(finish)

    row = pl.BlockSpec((tm, d), lambda i, j: (i, 0))
    wsp = pl.BlockSpec((tf, d), lambda i, j: (j, 0))
    osp = pl.BlockSpec((tm, tf), lambda i, j: (i, j))
    fused = pack is not None
    return pl.pallas_call(
        body, name=name, grid=(ni, nj),
        in_specs=[row, pl.BlockSpec((1, d), lambda i, j: (0, 0)), wsp, wsp] + ([HBM_SPEC] if fused else []),
        out_specs=[row, osp, osp, osp] + [HBM_SPEC] * nseg,
        out_shape=[jax.ShapeDtypeStruct((t, d), BF16)] + [jax.ShapeDtypeStruct((t, f), BF16)] * 3
        + [jax.ShapeDtypeStruct((8, n, d), BF16) for n in seg_rows],
        scratch_shapes=_gather_scratch() if fused else [],
        compiler_params=_cp(dimension_semantics=("arbitrary", "arbitrary")))(x, gain, wg, wu, *([pack] if fused else []))


def _swiglu_bwd(dout, wd, gg, uu, chip_part=None, *, name, tm=512, tf=1408):
    t, d = dout.shape
    f = wd.shape[0]
    tm, tf = min(tm, t), min(tf, f)
    nj, ni = f // tf, t // tm
    fused = chip_part is not None

    def body(do_ref, wd_ref, g_ref, u_ref, *rest):
        if fused:
            part_ref, dg_ref, du_ref, parts_ref = rest[:4]
            start, finish = _chip_exchange_phases(part_ref, parts_ref, *rest[4:])
            step = pl.program_id(0) * ni + pl.program_id(1)
            pl.when(step == 0)(start)
        else:
            dg_ref, du_ref = rest
        dact = _dot(do_ref[...], wd_ref[...], 1, 1)
        gv = g_ref[...].astype(F32)
        uv = u_ref[...].astype(F32)
        sg = _sigmoid(gv)
        dg_ref[...] = (dact * uv * (sg * (1.0 + gv * (1.0 - sg)))).astype(BF16)
        du_ref[...] = (dact * (gv * sg)).astype(BF16)
        if fused:
            pl.when(step == nj * ni - 1)(finish)

    osp = pl.BlockSpec((tm, tf), lambda j, i: (i, j))
    return pl.pallas_call(
        body, name=name, grid=(nj, ni),
        in_specs=[pl.BlockSpec((tm, d), lambda j, i: (i, 0)), pl.BlockSpec((tf, d), lambda j, i: (j, 0)), osp, osp]
        + ([HBM_SPEC] if fused else []),
        out_specs=[osp, osp] + ([HBM_SPEC] if fused else []),
        out_shape=[jax.ShapeDtypeStruct((t, f), BF16)] * 2
        + ([jax.ShapeDtypeStruct(chip_part.shape, chip_part.dtype)] if fused else []),
        scratch_shapes=_chip_exchange_scratch() if fused else [],
        compiler_params=_cp(dimension_semantics=("arbitrary", "arbitrary")))(dout, wd, gg, uu, *([chip_part] if fused else []))


def _shift_down(x, n):
    rows = lax.broadcasted_iota(jnp.int32, x.shape, 0)
    return jnp.where(rows >= n, pltpu.roll(x, n, 0), 0.0)


def _shift_up(x, n):
    t = x.shape[0]
    rows = lax.broadcasted_iota(jnp.int32, x.shape, 0)
    return jnp.where(rows < t - n, pltpu.roll(x, t - n, 0), 0.0)


def _conv_fwd(pa, conv_w, *, name):
    t = pa.shape[0]
    nb = pa.shape[1] // 3 // LANES

    def body(b_ref, c_ref, x_ref, w_ref, y_ref):
        u = c_ref[...] * x_ref[...]
        w = w_ref[...]
        conv = w[2:3, :] * u + w[1:2, :] * _shift_down(u, 1) + w[0:1, :] * _shift_down(u, 2)
        y_ref[...] = (b_ref[...] * conv).astype(BF16)

    def col(off):
        return pl.BlockSpec((t, LANES), lambda j: (0, off + j))

    return pl.pallas_call(
        body, name=name, grid=(nb,),
        in_specs=[col(0), col(nb), col(2 * nb), pl.BlockSpec((3, LANES), lambda j: (0, j))],
        out_specs=pl.BlockSpec((t, LANES), lambda j: (0, j)),
        out_shape=jax.ShapeDtypeStruct((t, nb * LANES), BF16), compiler_params=_cp())(pa, pa, pa, conv_w)


def _conv_bwd(pa, dy, conv_w, *, name):
    t = pa.shape[0]
    nb = pa.shape[1] // 3 // LANES

    def body(b_ref, c_ref, x_ref, dy_ref, w_ref, db_ref, dc_ref, dx_ref, dw_ref):
        cv, xv = c_ref[...], x_ref[...]
        u = cv * xv
        u1, u2 = _shift_down(u, 1), _shift_down(u, 2)
        w = w_ref[...]
        conv = w[2:3, :] * u + w[1:2, :] * u1 + w[0:1, :] * u2
        dyv = dy_ref[...]
        db_ref[...] = (dyv * conv).astype(BF16)
        dconv = dyv * b_ref[...]
        du = w[2:3, :] * dconv + w[1:2, :] * _shift_up(dconv, 1) + w[0:1, :] * _shift_up(dconv, 2)
        dc_ref[...] = (du * xv).astype(BF16)
        dx_ref[...] = (du * cv).astype(BF16)
        dw_ref[0:1, :] = jnp.sum(dconv * u2, axis=0, keepdims=True)
        dw_ref[1:2, :] = jnp.sum(dconv * u1, axis=0, keepdims=True)
        dw_ref[2:3, :] = jnp.sum(dconv * u, axis=0, keepdims=True)

    def col(off):
        return pl.BlockSpec((t, LANES), lambda j: (0, off + j))

    osp = pl.BlockSpec((t, LANES), lambda j: (0, j))
    wsp = pl.BlockSpec((3, LANES), lambda j: (0, j))
    return pl.pallas_call(
        body, name=name, grid=(nb,), in_specs=[col(0), col(nb), col(2 * nb), col(0), wsp],
        out_specs=[osp, osp, osp, wsp],
        out_shape=[jax.ShapeDtypeStruct((t, nb * LANES), BF16)] * 3 + [jax.ShapeDtypeStruct((3, nb * LANES), F32)],
        compiler_params=_cp())(pa, pa, pa, dy, conv_w)


def _sb_consts():
    j = lax.broadcasted_iota(jnp.int32, (SB_KEYS, SB_KEYS), 0)
    s = lax.broadcasted_iota(jnp.int32, (SB_KEYS, SB_KEYS), 1)
    after = (j > s).astype(BF16)
    upto = (j <= s).astype(BF16)
    before = (j < s).astype(BF16)
    return after, jnp.stack([upto, before])


def _log_sigmoid(z):
    return jnp.minimum(z, 0.0) - jnp.log(1.0 + jnp.exp(-jnp.abs(z)))


def _attn_fwd(pb, late_pack, seg_rows, *, name, tq=256):
    t = pb.shape[0]
    npair = pb.shape[1] // 3 // LANES
    tq = min(tq, t)
    nq = t // tq
    cmat, _ = _sb_consts()
    scale = 1.0 / math.sqrt(LANES // 2)

    nseg = len(seg_rows)

    def body(q_ref, k_ref, v_ref, c_ref, late_ref, y_ref, lt_ref, *rest):
        i = pl.program_id(1)
        pair = pl.program_id(0)
        scratch = rest[nseg:nseg + 4]
        start, forward, finish = _gather_phases(late_ref, rest[:nseg], seg_rows, *rest[nseg + 4:])
        pl.when((pair == 0) & (i == 0))(start)
        pl.when((pair == npair - 1) & (i == nq // 2))(forward)
        lane = lax.broadcasted_iota(jnp.int32, (tq, LANES), 1)
        rowpos = i * tq + lax.broadcasted_iota(jnp.int32, (tq, SB_KEYS), 0)
        colid = lax.broadcasted_iota(jnp.int32, (tq, SB_KEYS), 1)
        q2 = q_ref[...] * jnp.asarray(scale, BF16)
        cm = c_ref[...]
        hi_lanes = lane >= LANES // 2
        qhs = [jnp.where(hi_lanes == (hh == 1), q2, jnp.zeros_like(q2)) for hh in range(2)]
        per_q = tq // SB_KEYS

        def blk(jb):
            return pl.ds(pl.multiple_of(jb * SB_KEYS, SB_KEYS), SB_KEYS)

        zbuf, wbuf, accbuf, runbuf = scratch

        def scores(jb):
            kb = k_ref[blk(jb), :]
            for hh in range(2):
                zbuf[hh] = _dot(qhs[hh], kb, 1, 1)

        def values(jb):
            vb = v_ref[blk(jb), :]
            for hh in range(2):
                accbuf[hh] += _dot(wbuf[hh], vb)

        def trip(jb, masked, first=False):
            mask = (jb * SB_KEYS + colid) < rowpos if masked else None
            if not first:
                values(jb + 1)
            pre, css = [], []
            for hh in range(2):
                z = zbuf[hh]
                lb = _log_sigmoid(z)
                lk = lb - z
                if masked:
                    lk = jnp.where(mask, lk, 0.0)
                lk_hi, lk_lo = _split2(lk)
                css.append(_dot(lk_hi, cm) + _dot(lk_lo, cm))
                run = runbuf[hh]
                pre.append(lb + run)
                runbuf[hh] = run + jnp.sum(lk, axis=1, keepdims=True)
            scores(jnp.maximum(jb - 1, 0))
            for hh in range(2):
                w = jnp.exp(pre[hh] + css[hh])
                if masked:
                    w = jnp.where(mask, w, 0.0)
                wbuf[hh] = w.astype(BF16)

        nfull = i * per_q
        accbuf[...] = jnp.zeros_like(accbuf)
        runbuf[...] = jnp.zeros_like(runbuf)
        scores(nfull + per_q - 1)
        for dblk in reversed(range(per_q)):
            trip(nfull + dblk, True, first=dblk == per_q - 1)

        def full_block(n, carry):
            trip(nfull - 1 - n, False)
            return carry

        lax.fori_loop(0, nfull, full_block, 0)
        values(0)
        y_ref[...] = jnp.where(hi_lanes, accbuf[1], accbuf[0]).astype(BF16)
        lt_ref[...] = jnp.where(hi_lanes, runbuf[1], runbuf[0])
        pl.when((pair == npair - 1) & (i == nq - 1))(finish)

    return pl.pallas_call(
        body, name=name, grid=(npair, nq),
        in_specs=[pl.BlockSpec((tq, LANES), lambda p, i: (i, p)),
                  pl.BlockSpec((t, LANES), lambda p, i: (0, npair + p)),
                  pl.BlockSpec((t, LANES), lambda p, i: (0, 2 * npair + p)),
                  pl.BlockSpec((SB_KEYS, SB_KEYS), lambda p, i: (0, 0)),
                  HBM_SPEC],
        out_specs=[pl.BlockSpec((tq, LANES), lambda p, i: (i, p))] * 2 + [HBM_SPEC] * nseg,
        out_shape=[jax.ShapeDtypeStruct((t, npair * LANES), BF16), jax.ShapeDtypeStruct((t, npair * LANES), F32),
                   ] + [jax.ShapeDtypeStruct((8, n, late_pack.shape[1]), late_pack.dtype) for n in seg_rows],
        scratch_shapes=[pltpu.VMEM((2, tq, SB_KEYS), F32), pltpu.VMEM((2, tq, SB_KEYS), BF16),
                        pltpu.VMEM((2, tq, LANES), F32), pltpu.VMEM((2, tq, 1), F32)] + _gather_scratch(),
        compiler_params=_cp(dimension_semantics=("arbitrary", "arbitrary")))(pb, pb, pb, cmat, late_pack)


def _attn_bwd(pb, dy, ltot, chip_part, *, name, tq=256):
    t = pb.shape[0]
    npair = pb.shape[1] // 3 // LANES
    tq = min(tq, t)
    nq = t // tq
    _, cmats = _sb_consts()
    scale = 1.0 / math.sqrt(LANES // 2)

    def body(q_ref, k_ref, v_ref, dy_ref, lt_ref, c_ref, part_ref, dq_ref, dk_ref, dv_ref, parts_ref, dk_acc, dv_acc, *rest):
        i = pl.program_id(1)
        pair = pl.program_id(0)
        scratch = rest[:6]
        start, finish = _chip_exchange_phases(part_ref, parts_ref, *rest[6:])
        pl.when((pair == 0) & (i == 0))(start)

        @pl.when(i == 0)
        def _():
            dk_acc[...] = jnp.zeros_like(dk_acc)
            dv_acc[...] = jnp.zeros_like(dv_acc)

        lane = lax.broadcasted_iota(jnp.int32, (tq, LANES), 1)
        rowpos = i * tq + lax.broadcasted_iota(jnp.int32, (tq, SB_KEYS), 0)
        colid = lax.broadcasted_iota(jnp.int32, (tq, SB_KEYS), 1)
        q2 = q_ref[...] * jnp.asarray(scale, BF16)
        do2 = dy_ref[...].astype(BF16)
        ltv = lt_ref[...]
        c_upto, c_before = c_ref[0], c_ref[1]
        hi_lanes = lane >= LANES // 2
        sels = [hi_lanes == (hh == 1) for hh in range(2)]
        qhs = [jnp.where(s, q2, jnp.zeros_like(q2)) for s in sels]
        dohs = [jnp.where(s, do2, jnp.zeros_like(do2)) for s in sels]
        lts = [ltv[:, 0:1], ltv[:, LANES // 2:LANES // 2 + 1]]
        per_q = tq // SB_KEYS

        def blk(jb):
            return pl.ds(pl.multiple_of(jb * SB_KEYS, SB_KEYS), SB_KEYS)

        zbuf, dabuf, dzbuf, abuf, dqbuf, sumbuf = scratch

        def scores(jb):
            kb, vb = k_ref[blk(jb), :], v_ref[blk(jb), :]
            for hh in range(2):
                zbuf[hh] = _dot(qhs[hh], kb, 1, 1)
                dabuf[hh] = _dot(dohs[hh], vb, 1, 1)

        def products(jb):
            kb = k_ref[blk(jb), :]
            dk_acc[blk(jb), :] += _dot(dzbuf[0], qhs[0], 0, 0) + _dot(dzbuf[1], qhs[1], 0, 0)
            dv_acc[blk(jb), :] += _dot(abuf[0], dohs[0], 0, 0) + _dot(abuf[1], dohs[1], 0, 0)
            for hh in range(2):
                dqbuf[hh] += _dot(dzbuf[hh], kb)

        def trip(jb, masked):
            mask = (jb * SB_KEYS + colid) < rowpos if masked else None
            products(jnp.maximum(jb - 1, 0))
            lbs, css, es, ces = [], [], [], []
            for hh in range(2):
                z = zbuf[hh]
                lb = _log_sigmoid(z)
                lk = lb - z
                if masked:
                    lk = jnp.where(mask, lk, 0.0)
                lk_hi, lk_lo = _split2(lk)
                css.append(_dot(lk_hi, c_upto) + _dot(lk_lo, c_upto))
                csum = sumbuf[2 * hh]
                lbs.append((lb, lb + (lts[hh] - csum)))
                sumbuf[2 * hh] = csum + jnp.sum(lk, axis=1, keepdims=True)
            for hh in range(2):
                a = jnp.exp(lbs[hh][1] - css[hh])
                if masked:
                    a = jnp.where(mask, a, 0.0)
                e = a * dabuf[hh]
                e_hi, e_lo = _split2(e)
                ces.append(_dot(e_hi, c_before) + _dot(e_lo, c_before))
                abuf[hh] = a.astype(BF16)
                es.append(e)
            scores(jnp.minimum(jb + 1, last))
            for hh in range(2):
                prun = sumbuf[2 * hh + 1]
                beta = jnp.exp(lbs[hh][0])
                dz = es[hh] * (1.0 - beta) - (prun + ces[hh]) * beta
                if masked:
                    dz = jnp.where(mask, dz, 0.0)
                dzbuf[hh] = dz.astype(BF16)
                sumbuf[2 * hh + 1] = prun + jnp.sum(es[hh], axis=1, keepdims=True)

        nfull = i * per_q
        last = nfull + per_q - 1
        for buf in (dzbuf, abuf, dqbuf, sumbuf):
            buf[...] = jnp.zeros_like(buf)
        scores(0)

        def full_block(jb, carry):
            trip(jb, False)
            return carry

        lax.fori_loop(0, nfull, full_block, 0)
        for dblk in range(per_q):
            trip(nfull + dblk, True)
        products(last)
        dq_ref[...] = (jnp.where(hi_lanes, dqbuf[1], dqbuf[0]) * scale).astype(BF16)

        @pl.when(i == nq - 1)
        def _():
            dk_ref[...] = dk_acc[...].astype(BF16)
            dv_ref[...] = dv_acc[...].astype(BF16)

        pl.when((pair == npair - 1) & (i == nq - 1))(finish)

    blk = pl.BlockSpec((tq, LANES), lambda p, i: (i, p))
    full = pl.BlockSpec((t, LANES), lambda p, i: (0, p))
    return pl.pallas_call(
        body, name=name, grid=(npair, nq),
        in_specs=[blk,
                  pl.BlockSpec((t, LANES), lambda p, i: (0, npair + p)),
                  pl.BlockSpec((t, LANES), lambda p, i: (0, 2 * npair + p)),
                  pl.BlockSpec((tq, LANES), lambda p, i: (i, npair + p)),
                  blk,
                  pl.BlockSpec((2, SB_KEYS, SB_KEYS), lambda p, i: (0, 0, 0)),
                  HBM_SPEC],
        out_specs=[blk, full, full, HBM_SPEC],
        out_shape=[jax.ShapeDtypeStruct((t, npair * LANES), BF16)] * 3 + [jax.ShapeDtypeStruct(chip_part.shape, chip_part.dtype)],
        scratch_shapes=[pltpu.VMEM((t, LANES), F32), pltpu.VMEM((t, LANES), F32),
                        pltpu.VMEM((2, tq, SB_KEYS), F32), pltpu.VMEM((2, tq, SB_KEYS), F32),
                        pltpu.VMEM((2, tq, SB_KEYS), BF16), pltpu.VMEM((2, tq, SB_KEYS), BF16),
                        pltpu.VMEM((2, tq, LANES), F32), pltpu.VMEM((4, tq, 1), F32)] + _chip_exchange_scratch(),
        compiler_params=_cp(dimension_semantics=("arbitrary", "arbitrary")))(pb, pb, pb, dy, ltot, cmats, chip_part)


def _hgrn_consts():
    t = lax.broadcasted_iota(jnp.int32, (CHUNK, CHUNK), 0)
    s = lax.broadcasted_iota(jnp.int32, (CHUNK, CHUNK), 1)
    masks = []
    for lvl in range(N_LEVELS):
        half = CHUNK >> (lvl + 1)
        same = (t // (2 * half)) == (s // (2 * half))
        masks.append((same & (t % (2 * half) >= half) & (s % (2 * half) < half)).astype(F32))
    masks.append((t == s).astype(F32))
    prefix = (s <= t).astype(BF16)
    suffix = (s >= t).astype(BF16)
    return prefix, jnp.stack(masks), suffix


def _hgrn_gates(qr, fr, lbv):
    sg = _sigmoid(fr)
    fval = lbv + (1.0 - lbv) * sg
    kk = (1.0 - lbv) * _sigmoid(-fr)
    sq = _sigmoid(qr)
    return sg, fval, jnp.log(fval), kk, sq, qr * sq


def _lower_bound(c_ref):
    c = c_ref[...]
    mx = jnp.max(c, axis=0, keepdims=True)
    ex = jnp.exp(c - mx)
    return ex[1:2, :] / jnp.sum(ex, axis=0, keepdims=True)


def _level_ref(b, lvl):
    half = CHUNK >> (lvl + 1)
    seg = 2 * half
    if seg >= 8:
        b3 = b.reshape(CHUNK // seg, seg, LANES)
        return jnp.broadcast_to(b3[:, half - 1:half, :], b3.shape).reshape(CHUNK, LANES)
    pos = lax.broadcasted_iota(jnp.int32, b.shape, 0) % seg
    out = b
    for p in range(seg):
        if p != half - 1:
            out = jnp.where(pos == p, pltpu.roll(b, (p - (half - 1)) % CHUNK, 0), out)
    return out


def _hgrn_levels(b, qs, kk):
    out = []
    for lvl in range(N_LEVELS):
        fac = jnp.exp(-jnp.abs(b - _level_ref(b, lvl)))
        out.append((qs * fac, kk * fac, fac, fac))
    out.append((qs, kk, None, None))
    return out


def _split2(x):
    hi = x.astype(BF16)
    return hi, (x - hi.astype(F32)).astype(BF16)


def _hgrn_fwd(pc, c_lb, out_norm, *, name, tc=512):
    t = pc.shape[0]
    nh = pc.shape[1] // 4 // LANES
    tc = min(tc, t)
    nch = tc // CHUNK
    cum_all, masks, _ = _hgrn_consts()

    def body(q_ref, f_ref, i_ref, g_ref, lb_ref, on_ref, cum_ref, m_ref, y_ref, o_ref, st_ref, state):
        @pl.when(pl.program_id(1) == 0)
        def _():
            state[...] = jnp.zeros_like(state)

        lbv = _lower_bound(lb_ref)
        onv = on_ref[...]

        def chunk(c, carry):
            rows = pl.ds(pl.multiple_of(c * CHUNK, CHUNK), CHUNK)
            for hh in range(HGRN_HEADS):
                lanes = slice(hh * LANES, (hh + 1) * LANES)
                _, _, g, kk, _, qs = _hgrn_gates(q_ref[rows, lanes], f_ref[rows, lanes], lbv[:, lanes])
                vb = i_ref[rows, lanes].astype(BF16)
                b = _dot_exact_lhs(cum_ref[...], g)
                scores = jnp.zeros((CHUNK, CHUNK), F32)
                for lvl, (ql, kl, _, _) in enumerate(_hgrn_levels(b, qs, kk)):
                    scores = scores + _dot(ql.astype(BF16), kl.astype(BF16), 1, 1) * m_ref[lvl]
                st = state[hh]
                st_ref[hh, c] = st
                o = _dot(scores.astype(BF16), vb) + _dot((qs * jnp.exp(b)).astype(BF16), st.astype(BF16), 1, 1)
                blast = b[CHUNK - 1:CHUNK, :]
                kdec = (kk * jnp.exp(blast - b)).astype(BF16)
                state[hh] = st * jnp.exp(blast) + _dot(vb, kdec, 0, 0)
                o_ref[rows, lanes] = o
                rstd = lax.rsqrt(jnp.mean(o * o, axis=-1, keepdims=True) + RMS_EPS)
                gate = g_ref[rows, lanes]
                y_ref[rows, lanes] = (o * rstd * onv * (gate * _sigmoid(gate))).astype(BF16)
            return carry

        lax.fori_loop(0, nch, chunk, 0, unroll=2)

    hw = HGRN_HEADS * LANES

    def col(off):
        return pl.BlockSpec((tc, hw), lambda h, i: (i, off // HGRN_HEADS + h))

    osp = pl.BlockSpec((tc, hw), lambda h, i: (i, h))
    return pl.pallas_call(
        body, name=name, grid=(nh // HGRN_HEADS, t // tc),
        in_specs=[col(0), col(nh), col(2 * nh), col(3 * nh),
                  pl.BlockSpec((2, hw), lambda h, i: (0, h)),
                  pl.BlockSpec((1, LANES), lambda h, i: (0, 0)),
                  pl.BlockSpec(cum_all.shape, lambda h, i: (0, 0)),
                  pl.BlockSpec(masks.shape, lambda h, i: (0, 0, 0))],
        out_specs=[osp, osp, pl.BlockSpec((HGRN_HEADS, nch, LANES, LANES), lambda h, i: (h, i, 0, 0))],
        out_shape=[jax.ShapeDtypeStruct((t, nh * LANES), BF16), jax.ShapeDtypeStruct((t, nh * LANES), F32),
                   jax.ShapeDtypeStruct((nh, t // CHUNK, LANES, LANES), F32)],
        scratch_shapes=[pltpu.VMEM((HGRN_HEADS, LANES, LANES), F32)],
        compiler_params=_cp())(pc, pc, pc, pc, c_lb, out_norm, cum_all, masks)


def _hgrn_bwd(pc, o_saved, states, dy, c_lb, out_norm, *, name, tc=512):
    t = pc.shape[0]
    nh = pc.shape[1] // 4 // LANES
    tc = min(tc, t)
    nch = tc // CHUNK
    nt = t // tc
    cum_all, masks, suffix = _hgrn_consts()

    def body(q_ref, f_ref, i_ref, g_ref, o_ref, st_ref, dy_ref, lb_ref, on_ref, cum_ref, m_ref, suf_ref,
             dq_ref, df_ref, di_ref, dg_ref, dlb_ref, don_ref, dstate):
        @pl.when(pl.program_id(1) == 0)
        def _():
            dstate[...] = jnp.zeros_like(dstate)
            dlb_ref[...] = jnp.zeros_like(dlb_ref)
            don_ref[...] = jnp.zeros_like(don_ref)

        lbv = _lower_bound(lb_ref)
        onv = on_ref[...]

        def head(hh, c, rows):
            lanes = slice(hh * LANES, (hh + 1) * LANES)
            qr = q_ref[rows, lanes]
            sg, fval, g, kk, sq, qs = _hgrn_gates(qr, f_ref[rows, lanes], lbv[:, lanes])
            vb = i_ref[rows, lanes].astype(BF16)
            o = o_ref[rows, lanes]
            gate = g_ref[rows, lanes]
            sgt = _sigmoid(gate)
            rstd = lax.rsqrt(jnp.mean(o * o, axis=-1, keepdims=True) + RMS_EPS)
            ohat = o * rstd
            dyv = dy_ref[rows, lanes]
            don = dyv * (gate * sgt)
            dg_ref[rows, lanes] = (dyv * ohat * onv * (sgt * (1.0 + gate * (1.0 - sgt)))).astype(BF16)
            don_ref[:, lanes] += jnp.sum(don * ohat, axis=0, keepdims=True)
            dxhat = don * onv
            dob = (rstd * (dxhat - ohat * jnp.mean(dxhat * ohat, axis=-1, keepdims=True))).astype(BF16)
            b = _dot_exact_lhs(cum_ref[...], g)
            blast = b[CHUNK - 1:CHUNK, :]
            eb = jnp.exp(b)
            edec = jnp.exp(blast - b)
            st32 = st_ref[hh, c]
            st = st32.astype(BF16)
            dst = dstate[hh]
            dstb = dst.astype(BF16)
            da = _dot(dob, vb, 1, 1)
            levels = _hgrn_levels(b, qs, kk)
            scores = jnp.zeros((CHUNK, CHUNK), F32)
            dq = eb * _dot(dob, st)
            dk_inter = edec * _dot(vb, dstb)
            dk = dk_inter
            for lvl, (ql, kl, eq, ek) in enumerate(levels):
                mk = m_ref[lvl]
                (qh, qlo), (kh, klo) = _split2(ql), _split2(kl)
                scores = scores + _dot(qh, kh, 1, 1) * mk
                dal = (da * mk).astype(BF16)
                dql = _dot(dal, kh) + _dot(dal, klo)
                dkl = _dot(dal, qh, 0, 0) + _dot(dal, qlo, 0, 0)
                dq = dq + (dql if eq is None else dql * eq)
                dk = dk + (dkl if ek is None else dkl * ek)
            kdec = (kk * edec).astype(BF16)
            dv = _dot(scores.astype(BF16), dob, 0, 0) + _dot(kdec, dstb, 1, 1)
            dstate[hh] = dst * jnp.exp(blast) + _dot(dob, (qs * eb).astype(BF16), 0, 0)
            db = qs * dq - kk * dk
            last = jnp.sum(kk * dk_inter, axis=0, keepdims=True) + jnp.exp(blast) * jnp.sum(dst * st32, axis=0, keepdims=True)
            dgl = _dot_exact_lhs(suf_ref[...], db) + last
            dfv = dgl / fval - dk
            df_ref[rows, lanes] = (dfv * (1.0 - lbv[:, lanes]) * sg * (1.0 - sg)).astype(BF16)
            dlb_ref[:, lanes] += jnp.sum(dfv * (1.0 - sg), axis=0, keepdims=True)
            dq_ref[rows, lanes] = (dq * (sq * (1.0 + qr * (1.0 - sq)))).astype(BF16)
            di_ref[rows, lanes] = dv.astype(BF16)

        def chunk(n, carry):
            c = nch - 1 - n
            rows = pl.ds(pl.multiple_of(c * CHUNK, CHUNK), CHUNK)
            for hh in range(HGRN_HEADS):
                head(hh, c, rows)
            return carry

        lax.fori_loop(0, nch, chunk, 0, unroll=2)

    hw = HGRN_HEADS * LANES

    def col(off):
        return pl.BlockSpec((tc, hw), lambda h, i: (nt - 1 - i, off // HGRN_HEADS + h))

    osp = pl.BlockSpec((tc, hw), lambda h, i: (nt - 1 - i, h))
    vec = pl.BlockSpec((1, hw), lambda h, i: (0, h))
    return pl.pallas_call(
        body, name=name, grid=(nh // HGRN_HEADS, nt),
        in_specs=[col(0), col(nh), col(2 * nh), col(3 * nh), osp,
                  pl.BlockSpec((HGRN_HEADS, nch, LANES, LANES), lambda h, i: (h, nt - 1 - i, 0, 0)),
                  osp,
                  pl.BlockSpec((2, hw), lambda h, i: (0, h)),
                  pl.BlockSpec((1, LANES), lambda h, i: (0, 0)),
                  pl.BlockSpec(cum_all.shape, lambda h, i: (0, 0)),
                  pl.BlockSpec(masks.shape, lambda h, i: (0, 0, 0)),
                  pl.BlockSpec(suffix.shape, lambda h, i: (0, 0))],
        out_specs=[osp, osp, osp, osp, vec, vec],
        out_shape=[jax.ShapeDtypeStruct((t, nh * LANES), BF16)] * 4 + [jax.ShapeDtypeStruct((1, nh * LANES), F32)] * 2,
        scratch_shapes=[pltpu.VMEM((HGRN_HEADS, LANES, LANES), F32)],
        compiler_params=_cp())(pc, pc, pc, pc, o_saved, states, dy, c_lb, out_norm, cum_all, masks, suffix)


HBM_SPEC = pl.BlockSpec(memory_space=pltpu.HBM)


def _gather_scratch():
    return [pltpu.SemaphoreType.DMA((7,)), pltpu.SemaphoreType.DMA((7,)), pltpu.SemaphoreType.DMA]


def _gather_phases(x_ref, out_refs, seg_rows, send_sems, recv_sems, local_sem):
    x, y, c = lax.axis_index("x"), lax.axis_index("y"), lax.axis_index("c")
    me, sibling = (x, y, c), (x, y, 1 - c)
    chips = [(1 - x, y), (x, 1 - y), (1 - x, 1 - y)]
    offs = [sum(seg_rows[:s]) for s in range(len(seg_rows))]
    assert sum(seg_rows) == x_ref.shape[0]

    def index(px, py, pc):
        return 4 * px + 2 * py + pc

    def copies(k, block, to, own):
        return [pltpu.make_async_remote_copy(
            src_ref=x_ref.at[pl.ds(offs[s], n)] if own else out_refs[s].at[index(*block)],
            dst_ref=out_refs[s].at[index(*block)],
            send_sem=send_sems.at[k], recv_sem=recv_sems.at[k], device_id=to, device_id_type=MESH)
            for s, n in enumerate(seg_rows)]

    def all_bytes(k):
        return pltpu.make_async_remote_copy(src_ref=x_ref, dst_ref=x_ref, send_sem=send_sems.at[k],
                                            recv_sem=recv_sems.at[k], device_id=me, device_id_type=MESH)

    mine = [pltpu.make_async_copy(x_ref.at[pl.ds(offs[s], n)], out_refs[s].at[index(*me)], local_sem)
            for s, n in enumerate(seg_rows)]
    first = copies(0, me, sibling, True)
    for j, chip in enumerate(chips):
        first += copies(1 + j, me, (*chip, c), True)

    def start():
        for cp in mine + first:
            cp.start()

    def forward():
        for j, chip in enumerate(chips):
            all_bytes(1 + j).wait_recv()
            for cp in copies(4 + j, (*chip, c), sibling, False):
                cp.start()

    def finish():
        all_bytes(0).wait_recv()
        for j in range(3):
            all_bytes(4 + j).wait_recv()
        for k in range(7):
            all_bytes(k).wait_send()
        pltpu.make_async_copy(x_ref, x_ref, local_sem).wait()

    return start, forward, finish


def _all_gather(xs, seg_rows=None, *, name):
    segs = [xs.shape[0]] if seg_rows is None else list(seg_rows)

    def body(x_ref, *rest):
        start, forward, finish = _gather_phases(x_ref, rest[:len(segs)], segs, *rest[len(segs):])
        start()
        forward()
        finish()

    outs = pl.pallas_call(
        body, name=name, in_specs=[HBM_SPEC], out_specs=[HBM_SPEC] * len(segs),
        out_shape=[jax.ShapeDtypeStruct((8, n, xs.shape[1]), xs.dtype) for n in segs],
        scratch_shapes=_gather_scratch())(xs)
    return outs[0] if seg_rows is None else outs


def _sibling_exchange(s, *, name):
    def body(s_ref, rb_ref, send_sem, recv_sem):
        x, y, c = lax.axis_index("x"), lax.axis_index("y"), lax.axis_index("c")
        cp = pltpu.make_async_remote_copy(
            src_ref=s_ref.at[:, 1 - c], dst_ref=rb_ref, send_sem=send_sem, recv_sem=recv_sem,
            device_id=(x, y, 1 - c), device_id_type=MESH)
        cp.start()
        cp.wait()

    return pl.pallas_call(
        body, name=name, in_specs=[HBM_SPEC], out_specs=HBM_SPEC,
        out_shape=jax.ShapeDtypeStruct(s.shape[:1] + s.shape[2:], s.dtype),
        scratch_shapes=[pltpu.SemaphoreType.DMA, pltpu.SemaphoreType.DMA])(s)


def _row_tile(n, cap=1024):
    return max(b for b in range(16, cap + 1, 16) if n % b == 0)


def _pair_add(s, rb, core, *, name):
    nchip, _, r, c = s.shape
    tb = _row_tile(r)

    def body(core_ref, a_ref, b_ref, o_ref):
        o_ref[...] = (a_ref[...].astype(F32) + b_ref[...].astype(F32)).astype(BF16)

    blk = pl.BlockSpec((None, tb, c), lambda ch, i, cr: (ch, i, 0))
    return pl.pallas_call(
        body, name=name,
        grid_spec=pltpu.PrefetchScalarGridSpec(
            num_scalar_prefetch=1, grid=(nchip, r // tb),
            in_specs=[pl.BlockSpec((None, None, tb, c), lambda ch, i, cr: (ch, cr[0], i, 0)), blk],
            out_specs=blk),
        out_shape=jax.ShapeDtypeStruct((nchip, r, c), BF16), compiler_params=_cp())(core, s, rb)


def _chip_exchange_scratch():
    return [pltpu.SemaphoreType.DMA((3,)), pltpu.SemaphoreType.DMA((3,)), pltpu.SemaphoreType.DMA]


def _chip_exchange_phases(p_ref, out_ref, send_sems, recv_sems, local_sem):
    x, y, c = lax.axis_index("x"), lax.axis_index("y"), lax.axis_index("c")
    mine = 2 * x + y
    own = pltpu.make_async_copy(p_ref.at[mine], out_ref.at[mine], local_sem)
    copies = [pltpu.make_async_remote_copy(
        src_ref=p_ref.at[2 * tx + ty], dst_ref=out_ref.at[mine],
        send_sem=send_sems.at[k], recv_sem=recv_sems.at[k], device_id=(tx, ty, c), device_id_type=MESH)
        for k, (tx, ty) in enumerate([(1 - x, y), (x, 1 - y), (1 - x, 1 - y)])]

    def start():
        own.start()
        for cp in copies:
            cp.start()

    def finish():
        for cp in copies:
            cp.wait()
        own.wait()

    return start, finish


def _adamw_math(w, g, m, v):
    m2 = ADAM_B1 * m + (1.0 - ADAM_B1) * g
    v2 = ADAM_B2 * v + (1.0 - ADAM_B2) * (g * g)
    m_hat = m2 / (1.0 - ADAM_B1 ** ADAM_STEP)
    v_hat = v2 / (1.0 - ADAM_B2 ** ADAM_STEP)
    return -ADAM_LR * (m_hat / (jnp.sqrt(v_hat) + ADAM_EPS) + ADAM_WD * w), m2, v2


def _adamw_shard(parts, g_off, w, m, v, layer, prev, *, name):
    _, r, c = w.shape
    tb = next(b for b in range(min(r, 512), 0, -16) if r % b == 0 and g_off % b == 0)

    def body(p0, p1, p2, p3, w_ref, m_ref, v_ref, *rest):
        g_out, d_out, m_out, v_out = rest[-4:]
        g = ((p0[...].astype(F32) + p1[...].astype(F32)) + p2[...].astype(F32)) + p3[...].astype(F32)
        d, m2, v2 = _adamw_math(w_ref[...], g, m_ref[...], v_ref[...])
        g_out[...] = g
        d_out[...] = d
        m_out[...] = m2
        v_out[...] = v2

    def part(ch):
        return pl.BlockSpec((None, tb, c), lambda i: (ch, g_off // tb + i, 0))

    blk = pl.BlockSpec((None, tb, c), lambda i: (layer, i, 0))
    prev = list(prev) if prev is not None else []
    return pl.pallas_call(
        body, name=name, grid=(r // tb,),
        in_specs=[part(0), part(1), part(2), part(3), blk, blk, blk] + [pl.BlockSpec(memory_space=pl.ANY)] * len(prev),
        out_specs=[blk] * 4, out_shape=[jax.ShapeDtypeStruct(w.shape, F32)] * 4,
        input_output_aliases={7 + k: k for k in range(len(prev))},
        compiler_params=_cp())(parts, parts, parts, parts, w, m, v, *prev)


SLOT = 8
SMALL_ROWS = 6 * SLOT
ROW_LB = 4 * SLOT


def _small_update(gath, w, m, v, *, name):
    def body(g_ref, w_ref, m_ref, v_ref, g_out, d_out, m_out, v_out):
        tot = g_ref[0]
        for k in range(1, 8):
            tot = tot + g_ref[k]
        wv = w_ref[...]
        c0, c1 = wv[ROW_LB:ROW_LB + 1, :], wv[ROW_LB + 1:ROW_LB + 2, :]
        mx = jnp.maximum(c0, c1)
        e0, e1 = jnp.exp(c0 - mx), jnp.exp(c1 - mx)
        lb = e1 / (e0 + e1)
        gl = tot[ROW_LB:ROW_LB + 1, :] * lb * (1.0 - lb)
        row = lax.broadcasted_iota(jnp.int32, tot.shape, 0)
        g = jnp.where(row == ROW_LB, -gl, jnp.where(row == ROW_LB + 1, gl, tot))
        d, m2, v2 = _adamw_math(wv, g, m_ref[...], v_ref[...])
        g_out[...] = g
        d_out[...] = d
        m_out[...] = m2
        v_out[...] = v2

    return pl.pallas_call(
        body, name=name, out_shape=[jax.ShapeDtypeStruct(w.shape, F32)] * 4, compiler_params=_cp())(gath, w, m, v)


D_MODEL = 1024


def _ffn_fwd(h, gain, wg, wu, wd, tag):
    xn, gg, uu, act = _norm_gate_up(h, gain, wg, wu, name=f"{tag}_gate_up")
    out = _mm([(act, wd)], residual=h, alpha=MACARON, tn=1024, name=f"{tag}_down")
    return out, (h, xn, gg, uu, act)


def _ffn_input_bwd(dg, du, wg, wu, x, gain, dres, chip_part, *, name, scale, tm=256):
    t, d = x.shape
    f = wg.shape[0]
    tm = min(tm, t)
    nt = t // tm
    fused = chip_part is not None

    def body(dg_ref, du_ref, wg_ref, wu_ref, x_ref, g_ref, dres_ref, *rest):
        if fused:
            part_ref, dx_ref, dxb_ref, dgain_ref, parts_ref = rest[:5]
            start, finish = _chip_exchange_phases(part_ref, parts_ref, *rest[5:])
            pl.when(pl.program_id(0) == 0)(start)
        else:
            dx_ref, dxb_ref, dgain_ref = rest
        dxn_v = _dot(dg_ref[...], wg_ref[...]) + _dot(du_ref[...], wu_ref[...])
        xv = x_ref[...]
        rstd = lax.rsqrt(jnp.mean(xv * xv, axis=-1, keepdims=True) + RMS_EPS)
        xhat = xv * rstd
        dxhat = dxn_v * g_ref[...]
        dx = dres_ref[...] + rstd * (dxhat - xhat * jnp.mean(dxhat * xhat, axis=-1, keepdims=True))
        dx_ref[...] = dx
        dxb_ref[...] = (dx * scale).astype(BF16)

        @pl.when(pl.program_id(0) == 0)
        def _():
            dgain_ref[...] = jnp.zeros_like(dgain_ref)

        dgain_ref[...] += jnp.sum(dxn_v * xhat, axis=0, keepdims=True)
        if fused:
            pl.when(pl.program_id(0) == nt - 1)(finish)

    wide = pl.BlockSpec((tm, f), lambda i: (i, 0))
    wsp = pl.BlockSpec((f, d), lambda i: (0, 0))
    row = pl.BlockSpec((tm, d), lambda i: (i, 0))
    vec = pl.BlockSpec((1, d), lambda i: (0, 0))
    args = [dg, du, wg, wu, x, gain, dres] + ([chip_part] if fused else [])
    return pl.pallas_call(
        body, name=name, grid=(nt,),
        in_specs=[wide, wide, wsp, wsp, row, vec, row] + ([HBM_SPEC] if fused else []),
        out_specs=[row, row, vec] + ([HBM_SPEC] if fused else []),
        out_shape=[jax.ShapeDtypeStruct((t, d), F32), jax.ShapeDtypeStruct((t, d), BF16), jax.ShapeDtypeStruct((1, d), F32)]
        + ([jax.ShapeDtypeStruct(chip_part.shape, chip_part.dtype)] if fused else []),
        scratch_shapes=_chip_exchange_scratch() if fused else [],
        compiler_params=_cp(dimension_semantics=("arbitrary",)))(*args)


def _ffn_bwd(dout, dout_half, saved, gain, wg, wu, wd, tag, next_scale, make_chip_part=None, early_chip_part=None):
    h, xn, gg, uu, act = saved
    dg, du, *early_parts = _swiglu_bwd(dout_half, wd, gg, uu, early_chip_part, tm=256, tf=wd.shape[0],
                                       name=f"{tag}_dact")
    dwd = _mm([(act, dout_half)], ta=True, tm=256, tn=1024, out_dtype=BF16, name=f"{tag}_dwd")
    dwg, dwu = _mm_shared_rhs([dg, du], xn, tm=256, name=f"{tag}_dwgu")
    chip_part = make_chip_part(dwg, dwu, dwd) if make_chip_part is not None else None
    dh, dh_b, dgain, *parts = _ffn_input_bwd(dg, du, wg, wu, h, gain, dout, chip_part, scale=next_scale,
                                             name=f"{tag}_input_bwd")
    return dh, dh_b, dwg, dwu, dwd, dgain, (parts[0] if parts else None), (early_parts[0] if early_parts else None)


def kernel(x, ffn_pre_norm, ffn_pre_w_gate, ffn_pre_w_up, ffn_pre_w_down, mix_norm, ffn_post_norm, ffn_post_w_gate, ffn_post_w_up, ffn_post_w_down, ab_w_in, ab_conv_w, ab_w_out, c_w_in, c_lower_bounds, c_out_norm, c_w_out, final_norm, loss_target, m_ffn_pre_norm, m_ffn_pre_w_gate, m_ffn_pre_w_up, m_ffn_pre_w_down, m_mix_norm, m_ffn_post_norm, m_ffn_post_w_gate, m_ffn_post_w_up, m_ffn_post_w_down, m_ab_w_in, m_ab_conv_w, m_ab_w_out, m_c_w_in, m_c_lower_bounds, m_c_out_norm, m_c_w_out, m_final_norm, v_ffn_pre_norm, v_ffn_pre_w_gate, v_ffn_pre_w_up, v_ffn_pre_w_down, v_mix_norm, v_ffn_post_norm, v_ffn_post_w_gate, v_ffn_post_w_up, v_ffn_post_w_down, v_ab_w_in, v_ab_conv_w, v_ab_w_out, v_c_w_in, v_c_lower_bounds, v_c_out_norm, v_c_w_out, v_final_norm):
    d = D_MODEL
    h0 = x[0]
    target = loss_target[0]
    core = lax.axis_index("c").astype(jnp.int32).reshape(1)

    big = [("pre_g", ffn_pre_w_gate, m_ffn_pre_w_gate, v_ffn_pre_w_gate),
           ("pre_u", ffn_pre_w_up, m_ffn_pre_w_up, v_ffn_pre_w_up),
           ("pre_d", ffn_pre_w_down, m_ffn_pre_w_down, v_ffn_pre_w_down),
           ("post_g", ffn_post_w_gate, m_ffn_post_w_gate, v_ffn_post_w_gate),
           ("post_u", ffn_post_w_up, m_ffn_post_w_up, v_ffn_post_w_up),
           ("post_d", ffn_post_w_down, m_ffn_post_w_down, v_ffn_post_w_down),
           ("ab_in", ab_w_in, m_ab_w_in, v_ab_w_in),
           ("ab_out", ab_w_out, m_ab_w_out, v_ab_w_out),
           ("c_in", c_w_in, m_c_w_in, v_c_w_in),
           ("c_out", c_w_out, m_c_w_out, v_c_w_out)]
    by_tag = {tag: (w, m, v) for tag, w, m, v in big}

    def layer_rows(tag):
        w = by_tag[tag][0]
        return w.size // d // w.shape[0]

    def layout(items):
        offs, off = {}, 0
        for item in items:
            offs[item] = off
            off += layer_rows(item[0])
        return offs, off

    ffn = [f"{pos}_{kind}" for pos in ("pre", "post") for kind in "gud"]
    first_items = [("pre_g", 0), ("pre_u", 0)]
    early_items = [("pre_d", 0), ("ab_in", 0)]
    late_items = ([("pre_g", 1), ("pre_u", 1), ("pre_d", 1)] + [(f"post_{kind}", l) for l in (0, 1) for kind in "gud"]
                  + [("ab_out", 0), ("c_in", 0), ("c_out", 0)])
    grad_items = {"A": ([(tag, 1) for tag in ffn] + [(f"post_{kind}", 0) for kind in "gud"]
                        + [("c_in", 0), ("c_out", 0), ("ab_out", 0)]),
                  "B": [(f"pre_{kind}", 0) for kind in "gud"], "C": [("ab_in", 0)]}
    grad_offs = {k: layout(items)[0] for k, items in grad_items.items()}
    grad_conv_row = layout(grad_items["C"])[1]

    def conv_rows(a, split):
        flat = a.reshape(-1)
        if split:
            hi = flat.astype(BF16)
            flat = jnp.concatenate([hi, (flat - hi.astype(F32)).astype(BF16)])
        return jnp.zeros((16, d), flat.dtype).at[0, :flat.shape[0]].set(flat)

    nconv = ab_conv_w.size
    col_sharded = {"pre_g", "pre_u", "post_g", "post_u", "ab_in", "c_in"}

    def pack_rows(item):
        tag, layer = item
        a = by_tag[tag][0][layer]
        return (a.T if tag in col_sharded else a).reshape(-1, d).astype(BF16)

    first_pack = jnp.concatenate([pack_rows(item) for item in first_items], axis=0)
    early_pack = jnp.concatenate([pack_rows(item) for item in early_items] + [conv_rows(ab_conv_w, True)], axis=0)
    late_pack = jnp.concatenate([pack_rows(item) for item in late_items], axis=0)
    first_w = _all_gather(first_pack, [layer_rows(tag) for tag, _ in first_items], name="gather_first_weights")
    full = {item: g.reshape(-1, d) for item, g in zip(first_items, first_w)}

    xn0, gg0, uu0, act0, *early_w = _norm_gate_up(
        h0, ffn_pre_norm[0:1], full["pre_g", 0], full["pre_u", 0], name="l0pre_gate_up_gather_early_weights",
        pack=early_pack, seg_rows=[layer_rows(tag) for tag, _ in early_items] + [16])
    full.update({item: g.reshape(-1, d) for item, g in zip(early_items, early_w)})
    ffn_w = {("pre", 0): tuple(full[f"pre_{kind}", 0] for kind in "gud")}
    w_ab_in = full["ab_in", 0]
    cg = early_w[-1][:, 0, :2 * nconv].astype(F32)
    conv_w = (cg[:, :nconv] + cg[:, nconv:]).reshape(8, 3, -1).transpose(1, 0, 2).reshape(3, -1)
    aw = w_ab_in.shape[0] // 6
    h1 = _mm([(act0, full["pre_d", 0])], residual=h0, alpha=MACARON, tn=1024, name="l0pre_down")
    s_pre0 = (h0, xn0, gg0, uu0, act0)
    hn0, pa, pb = _norm_proj(h1, mix_norm[0:1], w_ab_in, (F32, BF16), tm=512, name="ab_norm_proj")
    ya = _conv_fwd(pa, conv_w, name="conv_fwd")
    yb, ltot, *late_w = _attn_fwd(pb, late_pack, [layer_rows(tag) for tag, _ in late_items],
                                  name="attn_fwd_gather_late_weights")
    full.update({item: g.reshape(-1, d) for item, g in zip(late_items, late_w)})
    for pos, layer in (("post", 0), ("pre", 1), ("post", 1)):
        ffn_w[pos, layer] = tuple(full[f"{pos}_{kind}", layer] for kind in "gud")
    w_ab_out, w_c_in, w_c_out = full["ab_out", 0], full["c_in", 0], full["c_out", 0]
    h2 = _mm([(ya, w_ab_out[:aw]), (yb, w_ab_out[aw:])], residual=h1, tn=1024, name="ab_out")
    h3, s_post0 = _ffn_fwd(h2, ffn_post_norm[0:1], *ffn_w["post", 0], "l0post")
    h4, s_pre1 = _ffn_fwd(h3, ffn_pre_norm[1:2], *ffn_w["pre", 1], "l1pre")
    hn1, pc = _norm_proj(h4, mix_norm[1:2], w_c_in, (F32,), tm=256, name="c_norm_proj")
    yc, o_saved, states = _hgrn_fwd(pc, c_lower_bounds, c_out_norm, name="hgrn_fwd")
    h5 = _mm([(yc, w_c_out)], residual=h4, tn=1024, name="c_out")
    h6, s_post1 = _ffn_fwd(h5, ffn_post_norm[1:2], *ffn_w["post", 1], "l1post")
    dh6, dh6_b, d_final, loss_vec = _loss_head(h6, final_norm.reshape(1, d), target, name="loss_head")

    gw = {}
    dh5, dh5_b, gw["post_g", 1], gw["post_u", 1], gw["post_d", 1], d_post1, *_ = _ffn_bwd(
        dh6, dh6_b, s_post1, ffn_post_norm[1:2], *ffn_w["post", 1], "l1post", 1.0)
    dyc = _mm([(dh5_b, w_c_out)], tb=True, tn=1024, name="c_out_dy")
    g_c_out = _mm([(yc, dh5_b)], ta=True, tm=256, tn=1024, out_dtype=BF16, name="c_out_dw")
    dcq, dcf, dci, dcg, dlb, d_onorm = _hgrn_bwd(pc, o_saved, states, dyc, c_lower_bounds, c_out_norm, name="hgrn_bwd")
    dparts = [dcq, dcf, dci, dcg]
    g_c_in = jnp.concatenate(_mm_shared_rhs(dparts, hn1, tm=256, name="c_in_dw"), axis=0)
    cw = w_c_in.shape[0] // 4
    dhn1 = _mm([(dp, w_c_in[i * cw:(i + 1) * cw]) for i, dp in enumerate(dparts)], tm=512, tn=1024, name="c_in_dx")
    dh4, dh4_b, d_mix1 = _rmsnorm_bwd(h4, mix_norm[1:2], dhn1, dh5, scale=MACARON, name="l1_mix_norm_bwd")
    dh3, dh3_b, gw["pre_g", 1], gw["pre_u", 1], gw["pre_d", 1], d_pre1, *_ = _ffn_bwd(
        dh4, dh4_b, s_pre1, ffn_pre_norm[1:2], *ffn_w["pre", 1], "l1pre", MACARON)
    dh2, dh2_b, gw["post_g", 0], gw["post_u", 0], gw["post_d", 0], d_post0, *_ = _ffn_bwd(
        dh3, dh3_b, s_post0, ffn_post_norm[0:1], *ffn_w["post", 0], "l0post", 1.0)
    dyab = _mm([(dh2_b, w_ab_out)], tb=True, tn=1024, name="ab_out_dy")
    g_ab_out = jnp.concatenate(_mm_shared_rhs([ya, yb], dh2_b, tm=256, name="ab_out_dw"), axis=0)
    dab, dac, dax, g_conv = _conv_bwd(pa, dyab, conv_w, name="conv_bwd")

    def chip_partials(key, grads, extra=()):
        gpack = jnp.concatenate([grads[item].reshape(8, -1, d) for item in grad_items[key]] + list(extra), axis=1)
        send = gpack.reshape(4, 2, gpack.shape[1], d)
        from_sibling = _sibling_exchange(send, name=f"grad{key}_sibling_exchange")
        return _pair_add(send, from_sibling, core, name=f"grad{key}_pair_add")

    gw["c_in", 0], gw["c_out", 0], gw["ab_out", 0] = g_c_in, g_c_out, g_ab_out
    chip_part_a = chip_partials("A", gw)
    dq, dk, dv, parts_a = _attn_bwd(pb, dyab, ltot, chip_part_a, name="attn_bwd_exchange_grads_a")
    dparts = [dab, dac, dax, dq, dk, dv]
    g_ab_in = jnp.concatenate(_mm_shared_rhs(dparts, hn0, tm=128, name="ab_in_dw"), axis=0)
    dhn0 = _mm([(dp, w_ab_in[i * aw:(i + 1) * aw]) for i, dp in enumerate(dparts)], tm=512, tn=1024, name="ab_in_dx")
    dh1, dh1_b, d_mix0 = _rmsnorm_bwd(h1, mix_norm[0:1], dhn0, dh2, scale=MACARON, name="l0_mix_norm_bwd")
    gw["ab_in", 0] = g_ab_in
    gconv_own = g_conv.reshape(3, 8, -1).transpose(1, 0, 2).reshape(8, -1)
    conv_piece = jnp.zeros((8, 16, d), F32).at[:, 0, :nconv].set(gconv_own).astype(BF16)

    def chip_part_b(dwg, dwu, dwd):
        gw["pre_g", 0], gw["pre_u", 0], gw["pre_d", 0] = dwg, dwu, dwd
        return chip_partials("B", gw)

    dh0, _, _, _, _, d_pre0, parts_b, parts_c = _ffn_bwd(
        dh1, dh1_b, s_pre0, ffn_pre_norm[0:1], *ffn_w["pre", 0], "l0pre", 1.0, chip_part_b,
        chip_partials("C", gw, [conv_piece]))

    parts = {"A": parts_a, "B": parts_b, "C": parts_c}
    upd = {}
    for tag, w, m, v in big:
        view = (lambda a: jnp.swapaxes(a, 1, 2)) if tag in col_sharded else (lambda a: a)
        where = {layer: (key, grad_offs[key][tag, layer])
                 for key in grad_items for t2, layer in grad_items[key] if t2 == tag}
        res = None
        for layer in sorted(where):
            key, off = where[layer]
            res = _adamw_shard(parts[key], off, view(w), view(m), view(v), layer, res, name=f"adamw_{tag}{layer}")
        upd[tag] = [view(a) for a in res]
    res = _adamw_shard(parts["C"], grad_conv_row, *(conv_rows(a, False)[None] for a in (ab_conv_w, m_ab_conv_w, v_ab_conv_w)),
                       0, None, name="adamw_conv")
    upd["conv"] = [r[0, 0, :nconv].reshape(ab_conv_w.shape) for r in res]

    def small_pack(pre, mix, post, final, lbs, onorm):
        def slot(parts):
            out, r = jnp.zeros((SLOT, d), F32), 0
            for a in (parts if isinstance(parts, tuple) else (parts,)):
                out = out.at[r:r + a.shape[0], :a.shape[1]].set(a)
                r += a.shape[0]
            return out

        return jnp.concatenate([slot(pre), slot(mix), slot(post), slot(final.reshape(1, d)), slot(lbs), slot(onorm)], axis=0)

    d_on = d_onorm.reshape(-1, c_out_norm.shape[1]).sum(axis=0, keepdims=True)
    gsmall = small_pack((d_pre0, d_pre1), (d_mix0, d_mix1), (d_post0, d_post1), d_final, dlb, d_on)
    gsmall_all = _all_gather(gsmall, name="gather_small_grads")
    sres = _small_update(
        gsmall_all,
        small_pack(ffn_pre_norm, mix_norm, ffn_post_norm, final_norm, c_lower_bounds, c_out_norm),
        small_pack(m_ffn_pre_norm, m_mix_norm, m_ffn_post_norm, m_final_norm, m_c_lower_bounds, m_c_out_norm),
        small_pack(v_ffn_pre_norm, v_mix_norm, v_ffn_post_norm, v_final_norm, v_c_lower_bounds, v_c_out_norm),
        name="small_update")

    def small_out(r):
        return {"pre_norm": r[0:2], "mix_norm": r[SLOT:SLOT + 2], "post_norm": r[2 * SLOT:2 * SLOT + 2],
                "final": r[3 * SLOT], "lb": r[ROW_LB:ROW_LB + 2], "onorm": r[5 * SLOT:5 * SLOT + 1, :c_out_norm.shape[1]]}

    small = [small_out(r) for r in sres]
    outs = []
    for k in range(4):
        s = small[k]
        outs += [s["pre_norm"], upd["pre_g"][k], upd["pre_u"][k], upd["pre_d"][k], s["mix_norm"], s["post_norm"],
                 upd["post_g"][k], upd["post_u"][k], upd["post_d"][k], upd["ab_in"][k], upd["conv"][k],
                 upd["ab_out"][k], upd["c_in"][k], s["lb"], s["onorm"], upd["c_out"][k], s["final"]]
    loss = lax.psum(loss_vec[0, 0], ("x", "y", "c"))
    return (loss, dh0[None], *outs)
```

```python
import functools
import math

import jax
import jax.numpy as jnp
from jax import lax
from jax.experimental import pallas as pl
from jax.experimental.pallas import tpu as pltpu

F32 = jnp.float32
BF16 = jnp.bfloat16
MESH = pl.DeviceIdType.MESH

RMS_EPS = 1e-6
MACARON = 0.5
LANES = 128
CHUNK = 64
N_LEVELS = 6
HGRN_HEADS = 2
SB_KEYS = 256
ADAM_LR, ADAM_B1, ADAM_B2, ADAM_EPS, ADAM_WD, ADAM_STEP = 0.001, 0.9, 0.999, 1e-08, 0.01, 10
VMEM_LIMIT = 48 * 1024 * 1024


def _cp(**kw):
    return pltpu.CompilerParams(vmem_limit_bytes=VMEM_LIMIT, **kw)


def _sigmoid(x):
    return 0.5 * jnp.tanh(0.5 * x) + 0.5


def _bf(x):
    return x if x.dtype == BF16 else x.astype(BF16)


def _split3(x):
    hi = x.astype(BF16)
    r1 = x - hi.astype(F32)
    mid = r1.astype(BF16)
    lo = (r1 - mid.astype(F32)).astype(BF16)
    return hi, mid, lo


def _dot(a, b, ca=1, cb=0):
    return lax.dot_general(a, b, (((ca,), (cb,)), ((), ())), preferred_element_type=F32)


def _dot_exact_lhs(m, x):
    hi, mid, lo = _split3(x)
    return _dot(m, hi) + _dot(m, mid) + _dot(m, lo)


def _dot_exact_rhs(x, m):
    hi, mid, lo = _split3(x)
    return _dot(hi, m) + _dot(mid, m) + _dot(lo, m)


def _mm(terms, *, name, ta=False, tb=False, out_dtype=F32, residual=None, alpha=1.0, tm=512, tn=512):
    nt = len(terms)
    a0, b0 = terms[0]
    m = a0.shape[1] if ta else a0.shape[0]
    n = b0.shape[0] if tb else b0.shape[1]
    tm, tn = min(tm, m), min(tn, n)
    assert m % tm == 0 and n % tn == 0, (name, m, n, tm, tn)
    has_res = residual is not None

    def body(*refs):
        o_ref = refs[-1]
        acc = None
        for i in range(nt):
            a = _bf(refs[2 * i][...])
            b = _bf(refs[2 * i + 1][...])
            p = _dot(a, b, 0 if ta else 1, 1 if tb else 0)
            acc = p if acc is None else acc + p
        if alpha != 1.0:
            acc = acc * alpha
        if has_res:
            acc = acc + refs[2 * nt][...]
        o_ref[...] = acc.astype(out_dtype)

    in_specs, args = [], []
    for a, b in terms:
        k = a.shape[0] if ta else a.shape[1]
        assert (b.shape[1] if tb else b.shape[0]) == k, (name, a.shape, b.shape)
        in_specs.append(pl.BlockSpec((k, tm), lambda i, j: (0, i)) if ta else pl.BlockSpec((tm, k), lambda i, j: (i, 0)))
        in_specs.append(pl.BlockSpec((tn, k), lambda i, j: (j, 0)) if tb else pl.BlockSpec((k, tn), lambda i, j: (0, j)))
        args += [a, b]
    if has_res:
        in_specs.append(pl.BlockSpec((tm, tn), lambda i, j: (i, j)))
        args.append(residual)
    return pl.pallas_call(
        body, name=name, grid=(m // tm, n // tn), in_specs=in_specs,
        out_specs=pl.BlockSpec((tm, tn), lambda i, j: (i, j)),
        out_shape=jax.ShapeDtypeStruct((m, n), out_dtype), compiler_params=_cp())(*args)


def _norm_proj(x, gain, w_t, out_dtypes, *, name, tm):
    t, d = x.shape
    n = w_t.shape[0]
    tm = min(tm, t)
    npart = len(out_dtypes)
    width = n // npart

    def body(x_ref, g_ref, w_ref, xn_ref, *part_refs):
        xv = x_ref[...]
        rstd = lax.rsqrt(jnp.mean(xv * xv, axis=-1, keepdims=True) + RMS_EPS)
        xn = (xv * rstd * g_ref[...]).astype(BF16)
        xn_ref[...] = xn
        for p, ref in enumerate(part_refs):
            ref[...] = _dot(xn, w_ref[p * width:(p + 1) * width, :], 1, 1).astype(out_dtypes[p])

    row = pl.BlockSpec((tm, d), lambda i: (i, 0))
    return pl.pallas_call(
        body, name=name, grid=(t // tm,),
        in_specs=[row, pl.BlockSpec((1, d), lambda i: (0, 0)), pl.BlockSpec((n, d), lambda i: (0, 0))],
        out_specs=[row] + [pl.BlockSpec((tm, width), lambda i: (i, 0))] * npart,
        out_shape=[jax.ShapeDtypeStruct((t, d), BF16)] + [jax.ShapeDtypeStruct((t, width), dt) for dt in out_dtypes],
        compiler_params=_cp())(x, gain, w_t)


def _mm_shared_rhs(a_list, b, *, name, tm, out_dtype=BF16):
    k, n = b.shape
    assert all(a.shape[0] == k and a.shape[1] % tm == 0 and a.shape[1] == a_list[0].shape[1] for a in a_list)
    m = a_list[0].shape[1]
    na = len(a_list)

    def body(*refs):
        bv = refs[na][...]
        for i in range(na):
            refs[na + 1 + i][...] = _dot(refs[i][...], bv, 0, 0).astype(out_dtype)

    return pl.pallas_call(
        body, name=name, grid=(m // tm,),
        in_specs=[pl.BlockSpec((k, tm), lambda i: (0, i))] * na + [pl.BlockSpec((k, n), lambda i: (0, 0))],
        out_specs=[pl.BlockSpec((tm, n), lambda i: (i, 0))] * na,
        out_shape=[jax.ShapeDtypeStruct((m, n), out_dtype)] * na, compiler_params=_cp())(*a_list, b)


def _rmsnorm_bwd(x, gain, dxn, dres, *, name, scale, tm=512):
    t, d = x.shape
    tm = min(tm, t)

    def body(x_ref, g_ref, dxn_ref, dres_ref, dx_ref, dxb_ref, dg_ref):
        xv = x_ref[...]
        rstd = lax.rsqrt(jnp.mean(xv * xv, axis=-1, keepdims=True) + RMS_EPS)
        xhat = xv * rstd
        dxn_v = dxn_ref[...]
        dxhat = dxn_v * g_ref[...]
        dx = dres_ref[...] + rstd * (dxhat - xhat * jnp.mean(dxhat * xhat, axis=-1, keepdims=True))
        dx_ref[...] = dx
        dxb_ref[...] = (dx * scale).astype(BF16)

        @pl.when(pl.program_id(0) == 0)
        def _():
            dg_ref[...] = jnp.zeros_like(dg_ref)

        dg_ref[...] += jnp.sum(dxn_v * xhat, axis=0, keepdims=True)

    row = pl.BlockSpec((tm, d), lambda i: (i, 0))
    vec = pl.BlockSpec((1, d), lambda i: (0, 0))
    return pl.pallas_call(
        body, name=name, grid=(t // tm,), in_specs=[row, vec, row, row], out_specs=[row, row, vec],
        out_shape=[jax.ShapeDtypeStruct((t, d), F32), jax.ShapeDtypeStruct((t, d), BF16), jax.ShapeDtypeStruct((1, d), F32)],
        compiler_params=_cp())(x, gain, dxn, dres)


def _loss_head(h, gain, target, *, name, tm=512):
    t, d = h.shape
    tm = min(tm, t)

    def body(h_ref, g_ref, t_ref, dh_ref, dhb_ref, dg_ref, loss_ref):
        hv = h_ref[...]
        rstd = lax.rsqrt(jnp.mean(hv * hv, axis=-1, keepdims=True) + RMS_EPS)
        xhat = hv * rstd
        err = xhat * g_ref[...] - t_ref[...]
        dy = err * (1.0 / d)
        dxhat = dy * g_ref[...]
        dh = rstd * (dxhat - xhat * jnp.mean(dxhat * xhat, axis=-1, keepdims=True))
        dh_ref[...] = dh
        dhb_ref[...] = (dh * MACARON).astype(BF16)

        @pl.when(pl.program_id(0) == 0)
        def _():
            dg_ref[...] = jnp.zeros_like(dg_ref)
            loss_ref[...] = jnp.zeros_like(loss_ref)

        dg_ref[...] += jnp.sum(dy * xhat, axis=0, keepdims=True)
        part = jnp.sum(jnp.sum(err * err, axis=-1, keepdims=True), axis=0, keepdims=True) * (0.5 / d)
        loss_ref[...] += jnp.broadcast_to(part, loss_ref.shape)

    row = pl.BlockSpec((tm, d), lambda i: (i, 0))
    vec = pl.BlockSpec((1, d), lambda i: (0, 0))
    return pl.pallas_call(
        body, name=name, grid=(t // tm,), in_specs=[row, vec, row],
        out_specs=[row, row, vec, pl.BlockSpec((1, LANES), lambda i: (0, 0))],
        out_shape=[jax.ShapeDtypeStruct((t, d), F32), jax.ShapeDtypeStruct((t, d), BF16), jax.ShapeDtypeStruct((1, d), F32),
                   jax.ShapeDtypeStruct((1, LANES), F32)],
        compiler_params=_cp())(h, gain, target)


def _norm_gate_up(x, gain, wg, wu, *, name, tm=256, tf=2816, pack=None, seg_rows=()):
    t, d = x.shape
    f = wg.shape[0]
    tm, tf = min(tm, t), min(tf, f)
    assert f % tf == 0
    ni, nj = t // tm, f // tf
    nseg = len(seg_rows)

    def body(x_ref, g_ref, wg_ref, wu_ref, *rest):
        if pack is not None:
            pack_ref, xn_ref, gg_ref, uu_ref, act_ref = rest[:5]
            start, forward, finish = _gather_phases(pack_ref, rest[5:5 + nseg], seg_rows, *rest[5 + nseg:])
            step = pl.program_id(0) * nj + pl.program_id(1)
            pl.when(step == 0)(start)
            pl.when(step == (3 * ni * nj) // 4)(forward)
        else:
            xn_ref, gg_ref, uu_ref, act_ref = rest

        @pl.when(pl.program_id(1) == 0)
        def _():
            xv = x_ref[...]
            rstd = lax.rsqrt(jnp.mean(xv * xv, axis=-1, keepdims=True) + RMS_EPS)
            xn_ref[...] = (xv * rstd * g_ref[...]).astype(BF16)

        xn = xn_ref[...]
        gv = _dot(xn, wg_ref[...], 1, 1)
        uv = _dot(xn, wu_ref[...], 1, 1)
        gg_ref[...] = gv.astype(BF16)
        uu_ref[...] = uv.astype(BF16)
        act_ref[...] = (gv * _sigmoid(gv) * uv).astype(BF16)
        if pack is not None:
            pl.when(step == ni * nj - 1)(finish)

    row = pl.BlockSpec((tm, d), lambda i, j: (i, 0))
    wsp = pl.BlockSpec((tf, d), lambda i, j: (j, 0))
    osp = pl.BlockSpec((tm, tf), lambda i, j: (i, j))
    fused = pack is not None
    return pl.pallas_call(
        body, name=name, grid=(ni, nj),
        in_specs=[row, pl.BlockSpec((1, d), lambda i, j: (0, 0)), wsp, wsp] + ([HBM_SPEC] if fused else []),
        out_specs=[row, osp, osp, osp] + [HBM_SPEC] * nseg,
        out_shape=[jax.ShapeDtypeStruct((t, d), BF16)] + [jax.ShapeDtypeStruct((t, f), BF16)] * 3
        + [jax.ShapeDtypeStruct((8, n, d), BF16) for n in seg_rows],
        scratch_shapes=_gather_scratch() if fused else [],
        compiler_params=_cp(dimension_semantics=("arbitrary", "arbitrary")))(x, gain, wg, wu, *([pack] if fused else []))


def _swiglu_bwd(dout, wd, gg, uu, chip_part=None, *, name, tm=512, tf=1408):
    t, d = dout.shape
    f = wd.shape[0]
    tm, tf = min(tm, t), min(tf, f)
    nj, ni = f // tf, t // tm
    fused = chip_part is not None

    def body(do_ref, wd_ref, g_ref, u_ref, *rest):
        if fused:
            part_ref, dg_ref, du_ref, parts_ref = rest[:4]
            start, finish = _chip_exchange_phases(part_ref, parts_ref, *rest[4:])
            step = pl.program_id(0) * ni + pl.program_id(1)
            pl.when(step == 0)(start)
        else:
            dg_ref, du_ref = rest
        dact = _dot(do_ref[...], wd_ref[...], 1, 1)
        gv = g_ref[...].astype(F32)
        uv = u_ref[...].astype(F32)
        sg = _sigmoid(gv)
        dg_ref[...] = (dact * uv * (sg * (1.0 + gv * (1.0 - sg)))).astype(BF16)
        du_ref[...] = (dact * (gv * sg)).astype(BF16)
        if fused:
            pl.when(step == nj * ni - 1)(finish)

    osp = pl.BlockSpec((tm, tf), lambda j, i: (i, j))
    return pl.pallas_call(
        body, name=name, grid=(nj, ni),
        in_specs=[pl.BlockSpec((tm, d), lambda j, i: (i, 0)), pl.BlockSpec((tf, d), lambda j, i: (j, 0)), osp, osp]
        + ([HBM_SPEC] if fused else []),
        out_specs=[osp, osp] + ([HBM_SPEC] if fused else []),
        out_shape=[jax.ShapeDtypeStruct((t, f), BF16)] * 2
        + ([jax.ShapeDtypeStruct(chip_part.shape, chip_part.dtype)] if fused else []),
        scratch_shapes=_chip_exchange_scratch() if fused else [],
        compiler_params=_cp(dimension_semantics=("arbitrary", "arbitrary")))(dout, wd, gg, uu, *([chip_part] if fused else []))


def _shift_down(x, n):
    rows = lax.broadcasted_iota(jnp.int32, x.shape, 0)
    return jnp.where(rows >= n, pltpu.roll(x, n, 0), 0.0)


def _shift_up(x, n):
    t = x.shape[0]
    rows = lax.broadcasted_iota(jnp.int32, x.shape, 0)
    return jnp.where(rows < t - n, pltpu.roll(x, t - n, 0), 0.0)


def _conv_fwd(pa, conv_w, *, name):
    t = pa.shape[0]
    nb = pa.shape[1] // 3 // LANES

    def body(b_ref, c_ref, x_ref, w_ref, y_ref):
        u = c_ref[...] * x_ref[...]
        w = w_ref[...]
        conv = w[2:3, :] * u + w[1:2, :] * _shift_down(u, 1) + w[0:1, :] * _shift_down(u, 2)
        y_ref[...] = (b_ref[...] * conv).astype(BF16)

    def col(off):
        return pl.BlockSpec((t, LANES), lambda j: (0, off + j))

    return pl.pallas_call(
        body, name=name, grid=(nb,),
        in_specs=[col(0), col(nb), col(2 * nb), pl.BlockSpec((3, LANES), lambda j: (0, j))],
        out_specs=pl.BlockSpec((t, LANES), lambda j: (0, j)),
        out_shape=jax.ShapeDtypeStruct((t, nb * LANES), BF16), compiler_params=_cp())(pa, pa, pa, conv_w)


def _conv_bwd(pa, dy, conv_w, *, name):
    t = pa.shape[0]
    nb = pa.shape[1] // 3 // LANES

    def body(b_ref, c_ref, x_ref, dy_ref, w_ref, db_ref, dc_ref, dx_ref, dw_ref):
        cv, xv = c_ref[...], x_ref[...]
        u = cv * xv
        u1, u2 = _shift_down(u, 1), _shift_down(u, 2)
        w = w_ref[...]
        conv = w[2:3, :] * u + w[1:2, :] * u1 + w[0:1, :] * u2
        dyv = dy_ref[...]
        db_ref[...] = (dyv * conv).astype(BF16)
        dconv = dyv * b_ref[...]
        du = w[2:3, :] * dconv + w[1:2, :] * _shift_up(dconv, 1) + w[0:1, :] * _shift_up(dconv, 2)
        dc_ref[...] = (du * xv).astype(BF16)
        dx_ref[...] = (du * cv).astype(BF16)
        dw_ref[0:1, :] = jnp.sum(dconv * u2, axis=0, keepdims=True)
        dw_ref[1:2, :] = jnp.sum(dconv * u1, axis=0, keepdims=True)
        dw_ref[2:3, :] = jnp.sum(dconv * u, axis=0, keepdims=True)

    def col(off):
        return pl.BlockSpec((t, LANES), lambda j: (0, off + j))

    osp = pl.BlockSpec((t, LANES), lambda j: (0, j))
    wsp = pl.BlockSpec((3, LANES), lambda j: (0, j))
    return pl.pallas_call(
        body, name=name, grid=(nb,), in_specs=[col(0), col(nb), col(2 * nb), col(0), wsp],
        out_specs=[osp, osp, osp, wsp],
        out_shape=[jax.ShapeDtypeStruct((t, nb * LANES), BF16)] * 3 + [jax.ShapeDtypeStruct((3, nb * LANES), F32)],
        compiler_params=_cp())(pa, pa, pa, dy, conv_w)


def _sb_consts():
    j = lax.broadcasted_iota(jnp.int32, (SB_KEYS, SB_KEYS), 0)
    s = lax.broadcasted_iota(jnp.int32, (SB_KEYS, SB_KEYS), 1)
    after = (j > s).astype(BF16)
    upto = (j <= s).astype(BF16)
    before = (j < s).astype(BF16)
    return after, jnp.stack([upto, before])


def _log_sigmoid(z):
    return jnp.minimum(z, 0.0) - jnp.log(1.0 + jnp.exp(-jnp.abs(z)))


def _attn_fwd(pb, late_pack, seg_rows, *, name, tq=256):
    t = pb.shape[0]
    npair = pb.shape[1] // 3 // LANES
    tq = min(tq, t)
    nq = t // tq
    cmat, _ = _sb_consts()
    scale = 1.0 / math.sqrt(LANES // 2)

    nseg = len(seg_rows)

    def body(q_ref, k_ref, v_ref, c_ref, late_ref, y_ref, lt_ref, *rest):
        i = pl.program_id(1)
        pair = pl.program_id(0)
        scratch = rest[nseg:nseg + 4]
        start, forward, finish = _gather_phases(late_ref, rest[:nseg], seg_rows, *rest[nseg + 4:])
        pl.when((pair == 0) & (i == 0))(start)
        pl.when((pair == npair - 1) & (i == nq // 2))(forward)
        lane = lax.broadcasted_iota(jnp.int32, (tq, LANES), 1)
        rowpos = i * tq + lax.broadcasted_iota(jnp.int32, (tq, SB_KEYS), 0)
        colid = lax.broadcasted_iota(jnp.int32, (tq, SB_KEYS), 1)
        q2 = q_ref[...] * jnp.asarray(scale, BF16)
        cm = c_ref[...]
        hi_lanes = lane >= LANES // 2
        qhs = [jnp.where(hi_lanes == (hh == 1), q2, jnp.zeros_like(q2)) for hh in range(2)]
        per_q = tq // SB_KEYS

        def blk(jb):
            return pl.ds(pl.multiple_of(jb * SB_KEYS, SB_KEYS), SB_KEYS)

        zbuf, wbuf, accbuf, runbuf = scratch

        def scores(jb):
            kb = k_ref[blk(jb), :]
            for hh in range(2):
                zbuf[hh] = _dot(qhs[hh], kb, 1, 1)

        def values(jb):
            vb = v_ref[blk(jb), :]
            for hh in range(2):
                accbuf[hh] += _dot(wbuf[hh], vb)

        def trip(jb, masked, first=False):
            mask = (jb * SB_KEYS + colid) < rowpos if masked else None
            if not first:
                values(jb + 1)
            pre, css = [], []
            for hh in range(2):
                z = zbuf[hh]
                lb = _log_sigmoid(z)
                lk = lb - z
                if masked:
                    lk = jnp.where(mask, lk, 0.0)
                lk_hi, lk_lo = _split2(lk)
                css.append(_dot(lk_hi, cm) + _dot(lk_lo, cm))
                run = runbuf[hh]
                pre.append(lb + run)
                runbuf[hh] = run + jnp.sum(lk, axis=1, keepdims=True)
            scores(jnp.maximum(jb - 1, 0))
            for hh in range(2):
                w = jnp.exp(pre[hh] + css[hh])
                if masked:
                    w = jnp.where(mask, w, 0.0)
                wbuf[hh] = w.astype(BF16)

        nfull = i * per_q
        accbuf[...] = jnp.zeros_like(accbuf)
        runbuf[...] = jnp.zeros_like(runbuf)
        scores(nfull + per_q - 1)
        for dblk in reversed(range(per_q)):
            trip(nfull + dblk, True, first=dblk == per_q - 1)

        def full_block(n, carry):
            trip(nfull - 1 - n, False)
            return carry

        lax.fori_loop(0, nfull, full_block, 0)
        values(0)
        y_ref[...] = jnp.where(hi_lanes, accbuf[1], accbuf[0]).astype(BF16)
        lt_ref[...] = jnp.where(hi_lanes, runbuf[1], runbuf[0])
        pl.when((pair == npair - 1) & (i == nq - 1))(finish)

    return pl.pallas_call(
        body, name=name, grid=(npair, nq),
        in_specs=[pl.BlockSpec((tq, LANES), lambda p, i: (i, p)),
                  pl.BlockSpec((t, LANES), lambda p, i: (0, npair + p)),
                  pl.BlockSpec((t, LANES), lambda p, i: (0, 2 * npair + p)),
                  pl.BlockSpec((SB_KEYS, SB_KEYS), lambda p, i: (0, 0)),
                  HBM_SPEC],
        out_specs=[pl.BlockSpec((tq, LANES), lambda p, i: (i, p))] * 2 + [HBM_SPEC] * nseg,
        out_shape=[jax.ShapeDtypeStruct((t, npair * LANES), BF16), jax.ShapeDtypeStruct((t, npair * LANES), F32),
                   ] + [jax.ShapeDtypeStruct((8, n, late_pack.shape[1]), late_pack.dtype) for n in seg_rows],
        scratch_shapes=[pltpu.VMEM((2, tq, SB_KEYS), F32), pltpu.VMEM((2, tq, SB_KEYS), BF16),
                        pltpu.VMEM((2, tq, LANES), F32), pltpu.VMEM((2, tq, 1), F32)] + _gather_scratch(),
        compiler_params=_cp(dimension_semantics=("arbitrary", "arbitrary")))(pb, pb, pb, cmat, late_pack)


def _attn_bwd(pb, dy, ltot, chip_part, *, name, tq=256):
    t = pb.shape[0]
    npair = pb.shape[1] // 3 // LANES
    tq = min(tq, t)
    nq = t // tq
    _, cmats = _sb_consts()
    scale = 1.0 / math.sqrt(LANES // 2)

    def body(q_ref, k_ref, v_ref, dy_ref, lt_ref, c_ref, part_ref, dq_ref, dk_ref, dv_ref, parts_ref, dk_acc, dv_acc, *rest):
        i = pl.program_id(1)
        pair = pl.program_id(0)
        scratch = rest[:6]
        start, finish = _direct_exchange_phases(part_ref, parts_ref, *rest[6:])
        pl.when((pair == 0) & (i == 0))(start)

        @pl.when(i == 0)
        def _():
            dk_acc[...] = jnp.zeros_like(dk_acc)
            dv_acc[...] = jnp.zeros_like(dv_acc)

        lane = lax.broadcasted_iota(jnp.int32, (tq, LANES), 1)
        rowpos = i * tq + lax.broadcasted_iota(jnp.int32, (tq, SB_KEYS), 0)
        colid = lax.broadcasted_iota(jnp.int32, (tq, SB_KEYS), 1)
        q2 = q_ref[...] * jnp.asarray(scale, BF16)
        do2 = dy_ref[...].astype(BF16)
        ltv = lt_ref[...]
        c_upto, c_before = c_ref[0], c_ref[1]
        hi_lanes = lane >= LANES // 2
        sels = [hi_lanes == (hh == 1) for hh in range(2)]
        qhs = [jnp.where(s, q2, jnp.zeros_like(q2)) for s in sels]
        dohs = [jnp.where(s, do2, jnp.zeros_like(do2)) for s in sels]
        lts = [ltv[:, 0:1], ltv[:, LANES // 2:LANES // 2 + 1]]
        per_q = tq // SB_KEYS

        def blk(jb):
            return pl.ds(pl.multiple_of(jb * SB_KEYS, SB_KEYS), SB_KEYS)

        zbuf, dabuf, dzbuf, abuf, dqbuf, sumbuf = scratch

        def scores(jb):
            kb, vb = k_ref[blk(jb), :], v_ref[blk(jb), :]
            for hh in range(2):
                zbuf[hh] = _dot(qhs[hh], kb, 1, 1)
                dabuf[hh] = _dot(dohs[hh], vb, 1, 1)

        def products(jb):
            kb = k_ref[blk(jb), :]
            dk_acc[blk(jb), :] += _dot(dzbuf[0], qhs[0], 0, 0) + _dot(dzbuf[1], qhs[1], 0, 0)
            dv_acc[blk(jb), :] += _dot(abuf[0], dohs[0], 0, 0) + _dot(abuf[1], dohs[1], 0, 0)
            for hh in range(2):
                dqbuf[hh] += _dot(dzbuf[hh], kb)

        def trip(jb, masked):
            mask = (jb * SB_KEYS + colid) < rowpos if masked else None
            products(jnp.maximum(jb - 1, 0))
            lbs, css, es, ces = [], [], [], []
            for hh in range(2):
                z = zbuf[hh]
                lb = _log_sigmoid(z)
                lk = lb - z
                if masked:
                    lk = jnp.where(mask, lk, 0.0)
                lk_hi, lk_lo = _split2(lk)
                css.append(_dot(lk_hi, c_upto) + _dot(lk_lo, c_upto))
                csum = sumbuf[2 * hh]
                lbs.append((lb, lb + (lts[hh] - csum)))
                sumbuf[2 * hh] = csum + jnp.sum(lk, axis=1, keepdims=True)
            for hh in range(2):
                a = jnp.exp(lbs[hh][1] - css[hh])
                if masked:
                    a = jnp.where(mask, a, 0.0)
                e = a * dabuf[hh]
                e_hi, e_lo = _split2(e)
                ces.append(_dot(e_hi, c_before) + _dot(e_lo, c_before))
                abuf[hh] = a.astype(BF16)
                es.append(e)
            scores(jnp.minimum(jb + 1, last))
            for hh in range(2):
                prun = sumbuf[2 * hh + 1]
                beta = jnp.exp(lbs[hh][0])
                dz = es[hh] * (1.0 - beta) - (prun + ces[hh]) * beta
                if masked:
                    dz = jnp.where(mask, dz, 0.0)
                dzbuf[hh] = dz.astype(BF16)
                sumbuf[2 * hh + 1] = prun + jnp.sum(es[hh], axis=1, keepdims=True)

        nfull = i * per_q
        last = nfull + per_q - 1
        for buf in (dzbuf, abuf, dqbuf, sumbuf):
            buf[...] = jnp.zeros_like(buf)
        scores(0)

        def full_block(jb, carry):
            trip(jb, False)
            return carry

        lax.fori_loop(0, nfull, full_block, 0)
        for dblk in range(per_q):
            trip(nfull + dblk, True)
        products(last)
        dq_ref[...] = (jnp.where(hi_lanes, dqbuf[1], dqbuf[0]) * scale).astype(BF16)

        @pl.when(i == nq - 1)
        def _():
            dk_ref[...] = dk_acc[...].astype(BF16)
            dv_ref[...] = dv_acc[...].astype(BF16)

        pl.when((pair == npair - 1) & (i == nq - 1))(finish)

    blk = pl.BlockSpec((tq, LANES), lambda p, i: (i, p))
    full = pl.BlockSpec((t, LANES), lambda p, i: (0, p))
    return pl.pallas_call(
        body, name=name, grid=(npair, nq),
        in_specs=[blk,
                  pl.BlockSpec((t, LANES), lambda p, i: (0, npair + p)),
                  pl.BlockSpec((t, LANES), lambda p, i: (0, 2 * npair + p)),
                  pl.BlockSpec((tq, LANES), lambda p, i: (i, npair + p)),
                  blk,
                  pl.BlockSpec((2, SB_KEYS, SB_KEYS), lambda p, i: (0, 0, 0)),
                  HBM_SPEC],
        out_specs=[blk, full, full, HBM_SPEC],
        out_shape=[jax.ShapeDtypeStruct((t, npair * LANES), BF16)] * 3
        + [jax.ShapeDtypeStruct((8,) + chip_part.shape[2:], chip_part.dtype)],
        scratch_shapes=[pltpu.VMEM((t, LANES), F32), pltpu.VMEM((t, LANES), F32),
                        pltpu.VMEM((2, tq, SB_KEYS), F32), pltpu.VMEM((2, tq, SB_KEYS), F32),
                        pltpu.VMEM((2, tq, SB_KEYS), BF16), pltpu.VMEM((2, tq, SB_KEYS), BF16),
                        pltpu.VMEM((2, tq, LANES), F32), pltpu.VMEM((4, tq, 1), F32)] + _direct_exchange_scratch(),
        compiler_params=_cp(dimension_semantics=("arbitrary", "arbitrary")))(pb, pb, pb, dy, ltot, cmats, chip_part)


def _hgrn_consts():
    t = lax.broadcasted_iota(jnp.int32, (CHUNK, CHUNK), 0)
    s = lax.broadcasted_iota(jnp.int32, (CHUNK, CHUNK), 1)
    masks = []
    for lvl in range(N_LEVELS):
        half = CHUNK >> (lvl + 1)
        same = (t // (2 * half)) == (s // (2 * half))
        masks.append((same & (t % (2 * half) >= half) & (s % (2 * half) < half)).astype(F32))
    masks.append((t == s).astype(F32))
    prefix = (s <= t).astype(BF16)
    suffix = (s >= t).astype(BF16)
    return prefix, jnp.stack(masks), suffix


def _hgrn_gates(qr, fr, lbv):
    sg = _sigmoid(fr)
    fval = lbv + (1.0 - lbv) * sg
    kk = (1.0 - lbv) * _sigmoid(-fr)
    sq = _sigmoid(qr)
    return sg, fval, jnp.log(fval), kk, sq, qr * sq


def _lower_bound(c_ref):
    c = c_ref[...]
    mx = jnp.max(c, axis=0, keepdims=True)
    ex = jnp.exp(c - mx)
    return ex[1:2, :] / jnp.sum(ex, axis=0, keepdims=True)


def _level_ref(b, lvl):
    half = CHUNK >> (lvl + 1)
    seg = 2 * half
    if seg >= 8:
        b3 = b.reshape(CHUNK // seg, seg, LANES)
        return jnp.broadcast_to(b3[:, half - 1:half, :], b3.shape).reshape(CHUNK, LANES)
    pos = lax.broadcasted_iota(jnp.int32, b.shape, 0) % seg
    out = b
    for p in range(seg):
        if p != half - 1:
            out = jnp.where(pos == p, pltpu.roll(b, (p - (half - 1)) % CHUNK, 0), out)
    return out


def _hgrn_levels(b, qs, kk):
    out = []
    for lvl in range(N_LEVELS):
        fac = jnp.exp(-jnp.abs(b - _level_ref(b, lvl)))
        out.append((qs * fac, kk * fac, fac, fac))
    out.append((qs, kk, None, None))
    return out


def _split2(x):
    hi = x.astype(BF16)
    return hi, (x - hi.astype(F32)).astype(BF16)


def _hgrn_fwd(pc, c_lb, out_norm, *, name, tc=512):
    t = pc.shape[0]
    nh = pc.shape[1] // 4 // LANES
    tc = min(tc, t)
    nch = tc // CHUNK
    cum_all, masks, _ = _hgrn_consts()

    def body(q_ref, f_ref, i_ref, g_ref, lb_ref, on_ref, cum_ref, m_ref, y_ref, o_ref, st_ref, state):
        @pl.when(pl.program_id(1) == 0)
        def _():
            state[...] = jnp.zeros_like(state)

        lbv = _lower_bound(lb_ref)
        onv = on_ref[...]

        def chunk(c, carry):
            rows = pl.ds(pl.multiple_of(c * CHUNK, CHUNK), CHUNK)
            for hh in range(HGRN_HEADS):
                lanes = slice(hh * LANES, (hh + 1) * LANES)
                _, _, g, kk, _, qs = _hgrn_gates(q_ref[rows, lanes], f_ref[rows, lanes], lbv[:, lanes])
                vb = i_ref[rows, lanes].astype(BF16)
                b = _dot_exact_lhs(cum_ref[...], g)
                scores = jnp.zeros((CHUNK, CHUNK), F32)
                for lvl, (ql, kl, _, _) in enumerate(_hgrn_levels(b, qs, kk)):
                    scores = scores + _dot(ql.astype(BF16), kl.astype(BF16), 1, 1) * m_ref[lvl]
                st = state[hh]
                st_ref[hh, c] = st
                o = _dot(scores.astype(BF16), vb) + _dot((qs * jnp.exp(b)).astype(BF16), st.astype(BF16), 1, 1)
                blast = b[CHUNK - 1:CHUNK, :]
                kdec = (kk * jnp.exp(blast - b)).astype(BF16)
                state[hh] = st * jnp.exp(blast) + _dot(vb, kdec, 0, 0)
                o_ref[rows, lanes] = o
                rstd = lax.rsqrt(jnp.mean(o * o, axis=-1, keepdims=True) + RMS_EPS)
                gate = g_ref[rows, lanes]
                y_ref[rows, lanes] = (o * rstd * onv * (gate * _sigmoid(gate))).astype(BF16)
            return carry

        lax.fori_loop(0, nch, chunk, 0, unroll=2)

    hw = HGRN_HEADS * LANES

    def col(off):
        return pl.BlockSpec((tc, hw), lambda h, i: (i, off // HGRN_HEADS + h))

    osp = pl.BlockSpec((tc, hw), lambda h, i: (i, h))
    return pl.pallas_call(
        body, name=name, grid=(nh // HGRN_HEADS, t // tc),
        in_specs=[col(0), col(nh), col(2 * nh), col(3 * nh),
                  pl.BlockSpec((2, hw), lambda h, i: (0, h)),
                  pl.BlockSpec((1, LANES), lambda h, i: (0, 0)),
                  pl.BlockSpec(cum_all.shape, lambda h, i: (0, 0)),
                  pl.BlockSpec(masks.shape, lambda h, i: (0, 0, 0))],
        out_specs=[osp, osp, pl.BlockSpec((HGRN_HEADS, nch, LANES, LANES), lambda h, i: (h, i, 0, 0))],
        out_shape=[jax.ShapeDtypeStruct((t, nh * LANES), BF16), jax.ShapeDtypeStruct((t, nh * LANES), F32),
                   jax.ShapeDtypeStruct((nh, t // CHUNK, LANES, LANES), F32)],
        scratch_shapes=[pltpu.VMEM((HGRN_HEADS, LANES, LANES), F32)],
        compiler_params=_cp())(pc, pc, pc, pc, c_lb, out_norm, cum_all, masks)


def _hgrn_bwd(pc, o_saved, states, dy, c_lb, out_norm, send, *, name, tc=512):
    t = pc.shape[0]
    nh = pc.shape[1] // 4 // LANES
    tc = min(tc, t)
    nch = tc // CHUNK
    nt = t // tc
    cum_all, masks, suffix = _hgrn_consts()
    ngroup = nh // HGRN_HEADS

    def body(q_ref, f_ref, i_ref, g_ref, o_ref, st_ref, dy_ref, lb_ref, on_ref, cum_ref, m_ref, suf_ref, send_ref,
             dq_ref, df_ref, di_ref, dg_ref, dlb_ref, don_ref, parts_ref, dstate, *sems):
        start, finish = _direct_exchange_phases(send_ref, parts_ref, *sems)
        pl.when((pl.program_id(0) == 0) & (pl.program_id(1) == 0))(start)

        @pl.when(pl.program_id(1) == 0)
        def _():
            dstate[...] = jnp.zeros_like(dstate)
            dlb_ref[...] = jnp.zeros_like(dlb_ref)
            don_ref[...] = jnp.zeros_like(don_ref)

        lbv = _lower_bound(lb_ref)
        onv = on_ref[...]

        def head(hh, c, rows):
            lanes = slice(hh * LANES, (hh + 1) * LANES)
            qr = q_ref[rows, lanes]
            sg, fval, g, kk, sq, qs = _hgrn_gates(qr, f_ref[rows, lanes], lbv[:, lanes])
            vb = i_ref[rows, lanes].astype(BF16)
            o = o_ref[rows, lanes]
            gate = g_ref[rows, lanes]
            sgt = _sigmoid(gate)
            rstd = lax.rsqrt(jnp.mean(o * o, axis=-1, keepdims=True) + RMS_EPS)
            ohat = o * rstd
            dyv = dy_ref[rows, lanes]
            don = dyv * (gate * sgt)
            dg_ref[rows, lanes] = (dyv * ohat * onv * (sgt * (1.0 + gate * (1.0 - sgt)))).astype(BF16)
            don_ref[:, lanes] += jnp.sum(don * ohat, axis=0, keepdims=True)
            dxhat = don * onv
            dob = (rstd * (dxhat - ohat * jnp.mean(dxhat * ohat, axis=-1, keepdims=True))).astype(BF16)
            b = _dot_exact_lhs(cum_ref[...], g)
            blast = b[CHUNK - 1:CHUNK, :]
            eb = jnp.exp(b)
            edec = jnp.exp(blast - b)
            st32 = st_ref[hh, c]
            st = st32.astype(BF16)
            dst = dstate[hh]
            dstb = dst.astype(BF16)
            da = _dot(dob, vb, 1, 1)
            levels = _hgrn_levels(b, qs, kk)
            scores = jnp.zeros((CHUNK, CHUNK), F32)
            dq = eb * _dot(dob, st)
            dk_inter = edec * _dot(vb, dstb)
            dk = dk_inter
            for lvl, (ql, kl, eq, ek) in enumerate(levels):
                mk = m_ref[lvl]
                (qh, qlo), (kh, klo) = _split2(ql), _split2(kl)
                scores = scores + _dot(qh, kh, 1, 1) * mk
                dal = (da * mk).astype(BF16)
                dql = _dot(dal, kh) + _dot(dal, klo)
                dkl = _dot(dal, qh, 0, 0) + _dot(dal, qlo, 0, 0)
                dq = dq + (dql if eq is None else dql * eq)
                dk = dk + (dkl if ek is None else dkl * ek)
            kdec = (kk * edec).astype(BF16)
            dv = _dot(scores.astype(BF16), dob, 0, 0) + _dot(kdec, dstb, 1, 1)
            dstate[hh] = dst * jnp.exp(blast) + _dot(dob, (qs * eb).astype(BF16), 0, 0)
            db = qs * dq - kk * dk
            last = jnp.sum(kk * dk_inter, axis=0, keepdims=True) + jnp.exp(blast) * jnp.sum(dst * st32, axis=0, keepdims=True)
            dgl = _dot_exact_lhs(suf_ref[...], db) + last
            dfv = dgl / fval - dk
            df_ref[rows, lanes] = (dfv * (1.0 - lbv[:, lanes]) * sg * (1.0 - sg)).astype(BF16)
            dlb_ref[:, lanes] += jnp.sum(dfv * (1.0 - sg), axis=0, keepdims=True)
            dq_ref[rows, lanes] = (dq * (sq * (1.0 + qr * (1.0 - sq)))).astype(BF16)
            di_ref[rows, lanes] = dv.astype(BF16)

        def chunk(n, carry):
            c = nch - 1 - n
            rows = pl.ds(pl.multiple_of(c * CHUNK, CHUNK), CHUNK)
            for hh in range(HGRN_HEADS):
                head(hh, c, rows)
            return carry

        lax.fori_loop(0, nch, chunk, 0, unroll=2)
        pl.when((pl.program_id(0) == ngroup - 1) & (pl.program_id(1) == nt - 1))(finish)

    hw = HGRN_HEADS * LANES

    def col(off):
        return pl.BlockSpec((tc, hw), lambda h, i: (nt - 1 - i, off // HGRN_HEADS + h))

    osp = pl.BlockSpec((tc, hw), lambda h, i: (nt - 1 - i, h))
    vec = pl.BlockSpec((1, hw), lambda h, i: (0, h))
    return pl.pallas_call(
        body, name=name, grid=(nh // HGRN_HEADS, nt),
        in_specs=[col(0), col(nh), col(2 * nh), col(3 * nh), osp,
                  pl.BlockSpec((HGRN_HEADS, nch, LANES, LANES), lambda h, i: (h, nt - 1 - i, 0, 0)),
                  osp,
                  pl.BlockSpec((2, hw), lambda h, i: (0, h)),
                  pl.BlockSpec((1, LANES), lambda h, i: (0, 0)),
                  pl.BlockSpec(cum_all.shape, lambda h, i: (0, 0)),
                  pl.BlockSpec(masks.shape, lambda h, i: (0, 0, 0)),
                  pl.BlockSpec(suffix.shape, lambda h, i: (0, 0)),
                  HBM_SPEC],
        out_specs=[osp, osp, osp, osp, vec, vec, HBM_SPEC],
        out_shape=[jax.ShapeDtypeStruct((t, nh * LANES), BF16)] * 4 + [jax.ShapeDtypeStruct((1, nh * LANES), F32)] * 2
        + [jax.ShapeDtypeStruct((8,) + send.shape[2:], send.dtype)],
        scratch_shapes=[pltpu.VMEM((HGRN_HEADS, LANES, LANES), F32)] + _direct_exchange_scratch(),
        compiler_params=_cp(dimension_semantics=("arbitrary", "arbitrary")))(
            pc, pc, pc, pc, o_saved, states, dy, c_lb, out_norm, cum_all, masks, suffix, send)


HBM_SPEC = pl.BlockSpec(memory_space=pltpu.HBM)


def _gather_scratch():
    return [pltpu.SemaphoreType.DMA((7,)), pltpu.SemaphoreType.DMA((7,)), pltpu.SemaphoreType.DMA]


def _gather_phases(x_ref, out_refs, seg_rows, send_sems, recv_sems, local_sem):
    x, y, c = lax.axis_index("x"), lax.axis_index("y"), lax.axis_index("c")
    me, sibling = (x, y, c), (x, y, 1 - c)
    chips = [(1 - x, y), (x, 1 - y), (1 - x, 1 - y)]
    offs = [sum(seg_rows[:s]) for s in range(len(seg_rows))]
    assert sum(seg_rows) == x_ref.shape[0]

    def index(px, py, pc):
        return 4 * px + 2 * py + pc

    def copies(k, block, to, own):
        return [pltpu.make_async_remote_copy(
            src_ref=x_ref.at[pl.ds(offs[s], n)] if own else out_refs[s].at[index(*block)],
            dst_ref=out_refs[s].at[index(*block)],
            send_sem=send_sems.at[k], recv_sem=recv_sems.at[k], device_id=to, device_id_type=MESH)
            for s, n in enumerate(seg_rows)]

    def all_bytes(k):
        return pltpu.make_async_remote_copy(src_ref=x_ref, dst_ref=x_ref, send_sem=send_sems.at[k],
                                            recv_sem=recv_sems.at[k], device_id=me, device_id_type=MESH)

    mine = [pltpu.make_async_copy(x_ref.at[pl.ds(offs[s], n)], out_refs[s].at[index(*me)], local_sem)
            for s, n in enumerate(seg_rows)]
    first = copies(0, me, sibling, True)
    for j, chip in enumerate(chips):
        first += copies(1 + j, me, (*chip, c), True)

    def start():
        for cp in mine + first:
            cp.start()

    def forward():
        for j, chip in enumerate(chips):
            all_bytes(1 + j).wait_recv()
            for cp in copies(4 + j, (*chip, c), sibling, False):
                cp.start()

    def finish():
        all_bytes(0).wait_recv()
        for j in range(3):
            all_bytes(4 + j).wait_recv()
        for k in range(7):
            all_bytes(k).wait_send()
        pltpu.make_async_copy(x_ref, x_ref, local_sem).wait()

    return start, forward, finish


def _all_gather(xs, seg_rows=None, *, name):
    segs = [xs.shape[0]] if seg_rows is None else list(seg_rows)

    def body(x_ref, *rest):
        start, forward, finish = _gather_phases(x_ref, rest[:len(segs)], segs, *rest[len(segs):])
        start()
        forward()
        finish()

    outs = pl.pallas_call(
        body, name=name, in_specs=[HBM_SPEC], out_specs=[HBM_SPEC] * len(segs),
        out_shape=[jax.ShapeDtypeStruct((8, n, xs.shape[1]), xs.dtype) for n in segs],
        scratch_shapes=_gather_scratch())(xs)
    return outs[0] if seg_rows is None else outs


def _sibling_exchange(s, *, name):
    def body(s_ref, rb_ref, send_sem, recv_sem):
        x, y, c = lax.axis_index("x"), lax.axis_index("y"), lax.axis_index("c")
        cp = pltpu.make_async_remote_copy(
            src_ref=s_ref.at[:, 1 - c], dst_ref=rb_ref, send_sem=send_sem, recv_sem=recv_sem,
            device_id=(x, y, 1 - c), device_id_type=MESH)
        cp.start()
        cp.wait()

    return pl.pallas_call(
        body, name=name, in_specs=[HBM_SPEC], out_specs=HBM_SPEC,
        out_shape=jax.ShapeDtypeStruct(s.shape[:1] + s.shape[2:], s.dtype),
        scratch_shapes=[pltpu.SemaphoreType.DMA, pltpu.SemaphoreType.DMA])(s)


def _row_tile(n, cap=1024):
    return max(b for b in range(16, cap + 1, 16) if n % b == 0)


def _pair_add(s, rb, core, *, name):
    nchip, _, r, c = s.shape
    tb = _row_tile(r)

    def body(core_ref, a_ref, b_ref, o_ref):
        o_ref[...] = (a_ref[...].astype(F32) + b_ref[...].astype(F32)).astype(BF16)

    blk = pl.BlockSpec((None, tb, c), lambda ch, i, cr: (ch, i, 0))
    return pl.pallas_call(
        body, name=name,
        grid_spec=pltpu.PrefetchScalarGridSpec(
            num_scalar_prefetch=1, grid=(nchip, r // tb),
            in_specs=[pl.BlockSpec((None, None, tb, c), lambda ch, i, cr: (ch, cr[0], i, 0)), blk],
            out_specs=blk),
        out_shape=jax.ShapeDtypeStruct((nchip, r, c), BF16), compiler_params=_cp())(core, s, rb)


def _chip_exchange_scratch():
    return [pltpu.SemaphoreType.DMA((3,)), pltpu.SemaphoreType.DMA((3,)), pltpu.SemaphoreType.DMA]


def _chip_exchange_phases(p_ref, out_ref, send_sems, recv_sems, local_sem):
    x, y, c = lax.axis_index("x"), lax.axis_index("y"), lax.axis_index("c")
    mine = 2 * x + y
    own = pltpu.make_async_copy(p_ref.at[mine], out_ref.at[mine], local_sem)
    copies = [pltpu.make_async_remote_copy(
        src_ref=p_ref.at[2 * tx + ty], dst_ref=out_ref.at[mine],
        send_sem=send_sems.at[k], recv_sem=recv_sems.at[k], device_id=(tx, ty, c), device_id_type=MESH)
        for k, (tx, ty) in enumerate([(1 - x, y), (x, 1 - y), (1 - x, 1 - y)])]

    def start():
        own.start()
        for cp in copies:
            cp.start()

    def finish():
        for cp in copies:
            cp.wait()
        own.wait()

    return start, finish


def _direct_exchange_scratch():
    return [pltpu.SemaphoreType.DMA((7,)), pltpu.SemaphoreType.DMA((7,)), pltpu.SemaphoreType.DMA]


def _direct_exchange_phases(s_ref, out_ref, send_sems, recv_sems, local_sem):
    x, y, c = lax.axis_index("x"), lax.axis_index("y"), lax.axis_index("c")
    me = 4 * x + 2 * y + c
    own = pltpu.make_async_copy(s_ref.at[2 * x + y, c], out_ref.at[me], local_sem)
    flips = [(fx, fy, fc) for fx in (0, 1) for fy in (0, 1) for fc in (0, 1) if (fx, fy, fc) != (0, 0, 0)]
    copies = []
    for k, (fx, fy, fc) in enumerate(flips):
        tx, ty, tc = (1 - x if fx else x), (1 - y if fy else y), (1 - c if fc else c)
        copies.append(pltpu.make_async_remote_copy(
            src_ref=s_ref.at[2 * tx + ty, tc], dst_ref=out_ref.at[me],
            send_sem=send_sems.at[k], recv_sem=recv_sems.at[k], device_id=(tx, ty, tc), device_id_type=MESH))

    def start():
        own.start()
        for cp in copies:
            cp.start()

    def finish():
        for cp in copies:
            cp.wait()
        own.wait()

    return start, finish


def _adamw_math(w, g, m, v):
    m2 = ADAM_B1 * m + (1.0 - ADAM_B1) * g
    v2 = ADAM_B2 * v + (1.0 - ADAM_B2) * (g * g)
    m_hat = m2 / (1.0 - ADAM_B1 ** ADAM_STEP)
    v_hat = v2 / (1.0 - ADAM_B2 ** ADAM_STEP)
    return -ADAM_LR * (m_hat / (jnp.sqrt(v_hat) + ADAM_EPS) + ADAM_WD * w), m2, v2


def _adamw_shard(parts, g_off, w, m, v, layer, prev, *, name):
    _, r, c = w.shape
    npart = parts.shape[0]
    tb = next(b for b in range(min(r, 512), 0, -16) if r % b == 0 and g_off % b == 0)

    def body(*refs):
        w_ref, m_ref, v_ref = refs[npart:npart + 3]
        g_out, d_out, m_out, v_out = refs[-4:]
        g = refs[0][...].astype(F32)
        for p_ref in refs[1:npart]:
            g = g + p_ref[...].astype(F32)
        d, m2, v2 = _adamw_math(w_ref[...], g, m_ref[...], v_ref[...])
        g_out[...] = g
        d_out[...] = d
        m_out[...] = m2
        v_out[...] = v2

    def part(ch):
        return pl.BlockSpec((None, tb, c), lambda i: (ch, g_off // tb + i, 0))

    blk = pl.BlockSpec((None, tb, c), lambda i: (layer, i, 0))
    prev = list(prev) if prev is not None else []
    return pl.pallas_call(
        body, name=name, grid=(r // tb,),
        in_specs=[part(ch) for ch in range(npart)] + [blk, blk, blk] + [pl.BlockSpec(memory_space=pl.ANY)] * len(prev),
        out_specs=[blk] * 4, out_shape=[jax.ShapeDtypeStruct(w.shape, F32)] * 4,
        input_output_aliases={npart + 3 + k: k for k in range(len(prev))},
        compiler_params=_cp())(*([parts] * npart), w, m, v, *prev)


SLOT = 8
SMALL_ROWS = 6 * SLOT
ROW_LB = 4 * SLOT


def _small_update(gath, w, m, v, *, name):
    def body(g_ref, w_ref, m_ref, v_ref, g_out, d_out, m_out, v_out):
        tot = g_ref[0]
        for k in range(1, 8):
            tot = tot + g_ref[k]
        wv = w_ref[...]
        c0, c1 = wv[ROW_LB:ROW_LB + 1, :], wv[ROW_LB + 1:ROW_LB + 2, :]
        mx = jnp.maximum(c0, c1)
        e0, e1 = jnp.exp(c0 - mx), jnp.exp(c1 - mx)
        lb = e1 / (e0 + e1)
        gl = tot[ROW_LB:ROW_LB + 1, :] * lb * (1.0 - lb)
        row = lax.broadcasted_iota(jnp.int32, tot.shape, 0)
        g = jnp.where(row == ROW_LB, -gl, jnp.where(row == ROW_LB + 1, gl, tot))
        d, m2, v2 = _adamw_math(wv, g, m_ref[...], v_ref[...])
        g_out[...] = g
        d_out[...] = d
        m_out[...] = m2
        v_out[...] = v2

    return pl.pallas_call(
        body, name=name, out_shape=[jax.ShapeDtypeStruct(w.shape, F32)] * 4, compiler_params=_cp())(gath, w, m, v)


D_MODEL = 1024


def _ffn_fwd(h, gain, wg, wu, wd, tag):
    xn, gg, uu, act = _norm_gate_up(h, gain, wg, wu, name=f"{tag}_gate_up")
    out = _mm([(act, wd)], residual=h, alpha=MACARON, tn=1024, name=f"{tag}_down")
    return out, (h, xn, gg, uu, act)


def _ffn_input_bwd(dg, du, wg, wu, x, gain, dres, chip_part, *, name, scale, tm=256):
    t, d = x.shape
    f = wg.shape[0]
    tm = min(tm, t)
    nt = t // tm
    fused = chip_part is not None

    def body(dg_ref, du_ref, wg_ref, wu_ref, x_ref, g_ref, dres_ref, *rest):
        if fused:
            part_ref, dx_ref, dxb_ref, dgain_ref, parts_ref = rest[:5]
            start, finish = _chip_exchange_phases(part_ref, parts_ref, *rest[5:])
            pl.when(pl.program_id(0) == 0)(start)
        else:
            dx_ref, dxb_ref, dgain_ref = rest
        dxn_v = _dot(dg_ref[...], wg_ref[...]) + _dot(du_ref[...], wu_ref[...])
        xv = x_ref[...]
        rstd = lax.rsqrt(jnp.mean(xv * xv, axis=-1, keepdims=True) + RMS_EPS)
        xhat = xv * rstd
        dxhat = dxn_v * g_ref[...]
        dx = dres_ref[...] + rstd * (dxhat - xhat * jnp.mean(dxhat * xhat, axis=-1, keepdims=True))
        dx_ref[...] = dx
        dxb_ref[...] = (dx * scale).astype(BF16)

        @pl.when(pl.program_id(0) == 0)
        def _():
            dgain_ref[...] = jnp.zeros_like(dgain_ref)

        dgain_ref[...] += jnp.sum(dxn_v * xhat, axis=0, keepdims=True)
        if fused:
            pl.when(pl.program_id(0) == nt - 1)(finish)

    wide = pl.BlockSpec((tm, f), lambda i: (i, 0))
    wsp = pl.BlockSpec((f, d), lambda i: (0, 0))
    row = pl.BlockSpec((tm, d), lambda i: (i, 0))
    vec = pl.BlockSpec((1, d), lambda i: (0, 0))
    args = [dg, du, wg, wu, x, gain, dres] + ([chip_part] if fused else [])
    return pl.pallas_call(
        body, name=name, grid=(nt,),
        in_specs=[wide, wide, wsp, wsp, row, vec, row] + ([HBM_SPEC] if fused else []),
        out_specs=[row, row, vec] + ([HBM_SPEC] if fused else []),
        out_shape=[jax.ShapeDtypeStruct((t, d), F32), jax.ShapeDtypeStruct((t, d), BF16), jax.ShapeDtypeStruct((1, d), F32)]
        + ([jax.ShapeDtypeStruct(chip_part.shape, chip_part.dtype)] if fused else []),
        scratch_shapes=_chip_exchange_scratch() if fused else [],
        compiler_params=_cp(dimension_semantics=("arbitrary",)))(*args)


def _ffn_bwd(dout, dout_half, saved, gain, wg, wu, wd, tag, next_scale, make_chip_part=None, early_chip_part=None):
    h, xn, gg, uu, act = saved
    dg, du, *early_parts = _swiglu_bwd(dout_half, wd, gg, uu, early_chip_part, tm=256, tf=wd.shape[0],
                                       name=f"{tag}_dact")
    dwd = _mm([(act, dout_half)], ta=True, tm=256, tn=1024, out_dtype=BF16, name=f"{tag}_dwd")
    dwg, dwu = _mm_shared_rhs([dg, du], xn, tm=256, name=f"{tag}_dwgu")
    chip_part = make_chip_part(dwg, dwu, dwd) if make_chip_part is not None else None
    dh, dh_b, dgain, *parts = _ffn_input_bwd(dg, du, wg, wu, h, gain, dout, chip_part, scale=next_scale,
                                             name=f"{tag}_input_bwd")
    return dh, dh_b, dwg, dwu, dwd, dgain, (parts[0] if parts else None), (early_parts[0] if early_parts else None)


def kernel(x, ffn_pre_norm, ffn_pre_w_gate, ffn_pre_w_up, ffn_pre_w_down, mix_norm, ffn_post_norm, ffn_post_w_gate, ffn_post_w_up, ffn_post_w_down, ab_w_in, ab_conv_w, ab_w_out, c_w_in, c_lower_bounds, c_out_norm, c_w_out, final_norm, loss_target, m_ffn_pre_norm, m_ffn_pre_w_gate, m_ffn_pre_w_up, m_ffn_pre_w_down, m_mix_norm, m_ffn_post_norm, m_ffn_post_w_gate, m_ffn_post_w_up, m_ffn_post_w_down, m_ab_w_in, m_ab_conv_w, m_ab_w_out, m_c_w_in, m_c_lower_bounds, m_c_out_norm, m_c_w_out, m_final_norm, v_ffn_pre_norm, v_ffn_pre_w_gate, v_ffn_pre_w_up, v_ffn_pre_w_down, v_mix_norm, v_ffn_post_norm, v_ffn_post_w_gate, v_ffn_post_w_up, v_ffn_post_w_down, v_ab_w_in, v_ab_conv_w, v_ab_w_out, v_c_w_in, v_c_lower_bounds, v_c_out_norm, v_c_w_out, v_final_norm):
    d = D_MODEL
    h0 = x[0]
    target = loss_target[0]
    core = lax.axis_index("c").astype(jnp.int32).reshape(1)

    big = [("pre_g", ffn_pre_w_gate, m_ffn_pre_w_gate, v_ffn_pre_w_gate),
           ("pre_u", ffn_pre_w_up, m_ffn_pre_w_up, v_ffn_pre_w_up),
           ("pre_d", ffn_pre_w_down, m_ffn_pre_w_down, v_ffn_pre_w_down),
           ("post_g", ffn_post_w_gate, m_ffn_post_w_gate, v_ffn_post_w_gate),
           ("post_u", ffn_post_w_up, m_ffn_post_w_up, v_ffn_post_w_up),
           ("post_d", ffn_post_w_down, m_ffn_post_w_down, v_ffn_post_w_down),
           ("ab_in", ab_w_in, m_ab_w_in, v_ab_w_in),
           ("ab_out", ab_w_out, m_ab_w_out, v_ab_w_out),
           ("c_in", c_w_in, m_c_w_in, v_c_w_in),
           ("c_out", c_w_out, m_c_w_out, v_c_w_out)]
    by_tag = {tag: (w, m, v) for tag, w, m, v in big}

    def layer_rows(tag):
        w = by_tag[tag][0]
        return w.size // d // w.shape[0]

    def layout(items):
        offs, off = {}, 0
        for item in items:
            offs[item] = off
            off += layer_rows(item[0])
        return offs, off

    ffn = [f"{pos}_{kind}" for pos in ("pre", "post") for kind in "gud"]
    first_items = [("pre_g", 0), ("pre_u", 0)]
    early_items = [("pre_d", 0), ("ab_in", 0)]
    late_items = ([("pre_g", 1), ("pre_u", 1), ("pre_d", 1)] + [(f"post_{kind}", l) for l in (0, 1) for kind in "gud"]
                  + [("ab_out", 0), ("c_in", 0), ("c_out", 0)])
    grad_items = {"A0": [(f"post_{kind}", 1) for kind in "gud"] + [("c_out", 0)],
                  "A1": ([(f"pre_{kind}", 1) for kind in "gud"] + [(f"post_{kind}", 0) for kind in "gud"]
                         + [("c_in", 0), ("ab_out", 0)]),
                  "B": [(f"pre_{kind}", 0) for kind in "gud"], "C": [("ab_in", 0)]}
    grad_offs = {k: layout(items)[0] for k, items in grad_items.items()}
    grad_conv_row = layout(grad_items["C"])[1]

    def conv_rows(a, split):
        flat = a.reshape(-1)
        if split:
            hi = flat.astype(BF16)
            flat = jnp.concatenate([hi, (flat - hi.astype(F32)).astype(BF16)])
        return jnp.zeros((16, d), flat.dtype).at[0, :flat.shape[0]].set(flat)

    nconv = ab_conv_w.size
    col_sharded = {"pre_g", "pre_u", "post_g", "post_u", "ab_in", "c_in"}

    def pack_rows(item):
        tag, layer = item
        a = by_tag[tag][0][layer]
        return (a.T if tag in col_sharded else a).reshape(-1, d).astype(BF16)

    first_pack = jnp.concatenate([pack_rows(item) for item in first_items], axis=0)
    early_pack = jnp.concatenate([pack_rows(item) for item in early_items] + [conv_rows(ab_conv_w, True)], axis=0)
    late_pack = jnp.concatenate([pack_rows(item) for item in late_items], axis=0)
    first_w = _all_gather(first_pack, [layer_rows(tag) for tag, _ in first_items], name="gather_first_weights")
    full = {item: g.reshape(-1, d) for item, g in zip(first_items, first_w)}

    xn0, gg0, uu0, act0, *early_w = _norm_gate_up(
        h0, ffn_pre_norm[0:1], full["pre_g", 0], full["pre_u", 0], name="l0pre_gate_up_gather_early_weights",
        pack=early_pack, seg_rows=[layer_rows(tag) for tag, _ in early_items] + [16])
    full.update({item: g.reshape(-1, d) for item, g in zip(early_items, early_w)})
    ffn_w = {("pre", 0): tuple(full[f"pre_{kind}", 0] for kind in "gud")}
    w_ab_in = full["ab_in", 0]
    cg = early_w[-1][:, 0, :2 * nconv].astype(F32)
    conv_w = (cg[:, :nconv] + cg[:, nconv:]).reshape(8, 3, -1).transpose(1, 0, 2).reshape(3, -1)
    aw = w_ab_in.shape[0] // 6
    h1 = _mm([(act0, full["pre_d", 0])], residual=h0, alpha=MACARON, tn=1024, name="l0pre_down")
    s_pre0 = (h0, xn0, gg0, uu0, act0)
    hn0, pa, pb = _norm_proj(h1, mix_norm[0:1], w_ab_in, (F32, BF16), tm=512, name="ab_norm_proj")
    ya = _conv_fwd(pa, conv_w, name="conv_fwd")
    yb, ltot, *late_w = _attn_fwd(pb, late_pack, [layer_rows(tag) for tag, _ in late_items],
                                  name="attn_fwd_gather_late_weights")
    full.update({item: g.reshape(-1, d) for item, g in zip(late_items, late_w)})
    for pos, layer in (("post", 0), ("pre", 1), ("post", 1)):
        ffn_w[pos, layer] = tuple(full[f"{pos}_{kind}", layer] for kind in "gud")
    w_ab_out, w_c_in, w_c_out = full["ab_out", 0], full["c_in", 0], full["c_out", 0]
    h2 = _mm([(ya, w_ab_out[:aw]), (yb, w_ab_out[aw:])], residual=h1, tn=1024, name="ab_out")
    h3, s_post0 = _ffn_fwd(h2, ffn_post_norm[0:1], *ffn_w["post", 0], "l0post")
    h4, s_pre1 = _ffn_fwd(h3, ffn_pre_norm[1:2], *ffn_w["pre", 1], "l1pre")
    hn1, pc = _norm_proj(h4, mix_norm[1:2], w_c_in, (F32,), tm=256, name="c_norm_proj")
    yc, o_saved, states = _hgrn_fwd(pc, c_lower_bounds, c_out_norm, name="hgrn_fwd")
    h5 = _mm([(yc, w_c_out)], residual=h4, tn=1024, name="c_out")
    h6, s_post1 = _ffn_fwd(h5, ffn_post_norm[1:2], *ffn_w["post", 1], "l1post")
    dh6, dh6_b, d_final, loss_vec = _loss_head(h6, final_norm.reshape(1, d), target, name="loss_head")

    gw = {}

    def grad_send(key, extra=()):
        gpack = jnp.concatenate([gw[item].reshape(8, -1, d) for item in grad_items[key]] + list(extra), axis=1)
        return gpack.reshape(4, 2, gpack.shape[1], d)

    def chip_partials(key, extra=()):
        send = grad_send(key, extra)
        from_sibling = _sibling_exchange(send, name=f"grad{key}_sibling_exchange")
        return _pair_add(send, from_sibling, core, name=f"grad{key}_pair_add")

    dh5, dh5_b, gw["post_g", 1], gw["post_u", 1], gw["post_d", 1], d_post1, *_ = _ffn_bwd(
        dh6, dh6_b, s_post1, ffn_post_norm[1:2], *ffn_w["post", 1], "l1post", 1.0)
    dyc = _mm([(dh5_b, w_c_out)], tb=True, tn=1024, name="c_out_dy")
    g_c_out = _mm([(yc, dh5_b)], ta=True, tm=256, tn=1024, out_dtype=BF16, name="c_out_dw")
    gw["c_out", 0] = g_c_out
    dcq, dcf, dci, dcg, dlb, d_onorm, parts_a0 = _hgrn_bwd(pc, o_saved, states, dyc, c_lower_bounds, c_out_norm,
                                                           grad_send("A0"), name="hgrn_bwd_exchange_grads_a0")
    dparts = [dcq, dcf, dci, dcg]
    g_c_in = jnp.concatenate(_mm_shared_rhs(dparts, hn1, tm=256, name="c_in_dw"), axis=0)
    cw = w_c_in.shape[0] // 4
    dhn1 = _mm([(dp, w_c_in[i * cw:(i + 1) * cw]) for i, dp in enumerate(dparts)], tm=512, tn=1024, name="c_in_dx")
    dh4, dh4_b, d_mix1 = _rmsnorm_bwd(h4, mix_norm[1:2], dhn1, dh5, scale=MACARON, name="l1_mix_norm_bwd")
    dh3, dh3_b, gw["pre_g", 1], gw["pre_u", 1], gw["pre_d", 1], d_pre1, *_ = _ffn_bwd(
        dh4, dh4_b, s_pre1, ffn_pre_norm[1:2], *ffn_w["pre", 1], "l1pre", MACARON)
    dh2, dh2_b, gw["post_g", 0], gw["post_u", 0], gw["post_d", 0], d_post0, *_ = _ffn_bwd(
        dh3, dh3_b, s_post0, ffn_post_norm[0:1], *ffn_w["post", 0], "l0post", 1.0)
    dyab = _mm([(dh2_b, w_ab_out)], tb=True, tn=1024, name="ab_out_dy")
    g_ab_out = jnp.concatenate(_mm_shared_rhs([ya, yb], dh2_b, tm=256, name="ab_out_dw"), axis=0)
    dab, dac, dax, g_conv = _conv_bwd(pa, dyab, conv_w, name="conv_bwd")

    gw["c_in", 0], gw["ab_out", 0] = g_c_in, g_ab_out
    dq, dk, dv, parts_a1 = _attn_bwd(pb, dyab, ltot, grad_send("A1"), name="attn_bwd_exchange_grads_a1")
    dparts = [dab, dac, dax, dq, dk, dv]
    g_ab_in = jnp.concatenate(_mm_shared_rhs(dparts, hn0, tm=128, name="ab_in_dw"), axis=0)
    dhn0 = _mm([(dp, w_ab_in[i * aw:(i + 1) * aw]) for i, dp in enumerate(dparts)], tm=512, tn=1024, name="ab_in_dx")
    dh1, dh1_b, d_mix0 = _rmsnorm_bwd(h1, mix_norm[0:1], dhn0, dh2, scale=MACARON, name="l0_mix_norm_bwd")
    gw["ab_in", 0] = g_ab_in
    gconv_own = g_conv.reshape(3, 8, -1).transpose(1, 0, 2).reshape(8, -1)
    conv_piece = jnp.zeros((8, 16, d), F32).at[:, 0, :nconv].set(gconv_own).astype(BF16)

    def chip_part_b(dwg, dwu, dwd):
        gw["pre_g", 0], gw["pre_u", 0], gw["pre_d", 0] = dwg, dwu, dwd
        return chip_partials("B")

    dh0, _, _, _, _, d_pre0, parts_b, parts_c = _ffn_bwd(
        dh1, dh1_b, s_pre0, ffn_pre_norm[0:1], *ffn_w["pre", 0], "l0pre", 1.0, chip_part_b,
        chip_partials("C", [conv_piece]))

    parts = {"A0": parts_a0, "A1": parts_a1, "B": parts_b, "C": parts_c}
    upd = {}
    for tag, w, m, v in big:
        view = (lambda a: jnp.swapaxes(a, 1, 2)) if tag in col_sharded else (lambda a: a)
        where = {layer: (key, grad_offs[key][tag, layer])
                 for key in grad_items for t2, layer in grad_items[key] if t2 == tag}
        res = None
        for layer in sorted(where):
            key, off = where[layer]
            res = _adamw_shard(parts[key], off, view(w), view(m), view(v), layer, res, name=f"adamw_{tag}{layer}")
        upd[tag] = [view(a) for a in res]
    res = _adamw_shard(parts["C"], grad_conv_row, *(conv_rows(a, False)[None] for a in (ab_conv_w, m_ab_conv_w, v_ab_conv_w)),
                       0, None, name="adamw_conv")
    upd["conv"] = [r[0, 0, :nconv].reshape(ab_conv_w.shape) for r in res]

    def small_pack(pre, mix, post, final, lbs, onorm):
        def slot(parts):
            out, r = jnp.zeros((SLOT, d), F32), 0
            for a in (parts if isinstance(parts, tuple) else (parts,)):
                out = out.at[r:r + a.shape[0], :a.shape[1]].set(a)
                r += a.shape[0]
            return out

        return jnp.concatenate([slot(pre), slot(mix), slot(post), slot(final.reshape(1, d)), slot(lbs), slot(onorm)], axis=0)

    d_on = d_onorm.reshape(-1, c_out_norm.shape[1]).sum(axis=0, keepdims=True)
    gsmall = small_pack((d_pre0, d_pre1), (d_mix0, d_mix1), (d_post0, d_post1), d_final, dlb, d_on)
    gsmall_all = _all_gather(gsmall, name="gather_small_grads")
    sres = _small_update(
        gsmall_all,
        small_pack(ffn_pre_norm, mix_norm, ffn_post_norm, final_norm, c_lower_bounds, c_out_norm),
        small_pack(m_ffn_pre_norm, m_mix_norm, m_ffn_post_norm, m_final_norm, m_c_lower_bounds, m_c_out_norm),
        small_pack(v_ffn_pre_norm, v_mix_norm, v_ffn_post_norm, v_final_norm, v_c_lower_bounds, v_c_out_norm),
        name="small_update")

    def small_out(r):
        return {"pre_norm": r[0:2], "mix_norm": r[SLOT:SLOT + 2], "post_norm": r[2 * SLOT:2 * SLOT + 2],
                "final": r[3 * SLOT], "lb": r[ROW_LB:ROW_LB + 2], "onorm": r[5 * SLOT:5 * SLOT + 1, :c_out_norm.shape[1]]}

    small = [small_out(r) for r in sres]
    outs = []
    for k in range(4):
        s = small[k]
        outs += [s["pre_norm"], upd["pre_g"][k], upd["pre_u"][k], upd["pre_d"][k], s["mix_norm"], s["post_norm"],
                 upd["post_g"][k], upd["post_u"][k], upd["post_d"][k], upd["ab_in"][k], upd["conv"][k],
                 upd["ab_out"][k], upd["c_in"][k], s["lb"], s["onorm"], upd["c_out"][k], s["final"]]
    loss = lax.psum(loss_vec[0, 0], ("x", "y", "c"))
    return (loss, dh0[None], *outs)
```

```python
import functools
import math

import jax
import jax.numpy as jnp
from jax import lax
from jax.experimental import pallas as pl
from jax.experimental.pallas import tpu as pltpu

F32 = jnp.float32
BF16 = jnp.bfloat16
MESH = pl.DeviceIdType.MESH

RMS_EPS = 1e-6
MACARON = 0.5
LANES = 128
CHUNK = 64
N_LEVELS = 6
HGRN_HEADS = 2
SB_KEYS = 256
ADAM_LR, ADAM_B1, ADAM_B2, ADAM_EPS, ADAM_WD, ADAM_STEP = 0.001, 0.9, 0.999, 1e-08, 0.01, 10
VMEM_LIMIT = 48 * 1024 * 1024


def _cp(**kw):
    return pltpu.CompilerParams(vmem_limit_bytes=VMEM_LIMIT, **kw)


def _sigmoid(x):
    return 0.5 * jnp.tanh(0.5 * x) + 0.5


def _bf(x):
    return x if x.dtype == BF16 else x.astype(BF16)


def _split3(x):
    hi = x.astype(BF16)
    r1 = x - hi.astype(F32)
    mid = r1.astype(BF16)
    lo = (r1 - mid.astype(F32)).astype(BF16)
    return hi, mid, lo


def _dot(a, b, ca=1, cb=0):
    return lax.dot_general(a, b, (((ca,), (cb,)), ((), ())), preferred_element_type=F32)


def _dot_exact_lhs(m, x):
    hi, mid, lo = _split3(x)
    return _dot(m, hi) + _dot(m, mid) + _dot(m, lo)


def _dot_exact_rhs(x, m):
    hi, mid, lo = _split3(x)
    return _dot(hi, m) + _dot(mid, m) + _dot(lo, m)


def _mm(terms, *, name, ta=False, tb=False, out_dtype=F32, residual=None, alpha=1.0, tm=512, tn=512):
    nt = len(terms)
    a0, b0 = terms[0]
    m = a0.shape[1] if ta else a0.shape[0]
    n = b0.shape[0] if tb else b0.shape[1]
    tm, tn = min(tm, m), min(tn, n)
    assert m % tm == 0 and n % tn == 0, (name, m, n, tm, tn)
    has_res = residual is not None

    def body(*refs):
        o_ref = refs[-1]
        acc = None
        for i in range(nt):
            a = _bf(refs[2 * i][...])
            b = _bf(refs[2 * i + 1][...])
            p = _dot(a, b, 0 if ta else 1, 1 if tb else 0)
            acc = p if acc is None else acc + p
        if alpha != 1.0:
            acc = acc * alpha
        if has_res:
            acc = acc + refs[2 * nt][...]
        o_ref[...] = acc.astype(out_dtype)

    in_specs, args = [], []
    for a, b in terms:
        k = a.shape[0] if ta else a.shape[1]
        assert (b.shape[1] if tb else b.shape[0]) == k, (name, a.shape, b.shape)
        in_specs.append(pl.BlockSpec((k, tm), lambda i, j: (0, i)) if ta else pl.BlockSpec((tm, k), lambda i, j: (i, 0)))
        in_specs.append(pl.BlockSpec((tn, k), lambda i, j: (j, 0)) if tb else pl.BlockSpec((k, tn), lambda i, j: (0, j)))
        args += [a, b]
    if has_res:
        in_specs.append(pl.BlockSpec((tm, tn), lambda i, j: (i, j)))
        args.append(residual)
    return pl.pallas_call(
        body, name=name, grid=(m // tm, n // tn), in_specs=in_specs,
        out_specs=pl.BlockSpec((tm, tn), lambda i, j: (i, j)),
        out_shape=jax.ShapeDtypeStruct((m, n), out_dtype), compiler_params=_cp())(*args)


def _norm_proj(x, gain, w_t, out_dtypes, *, name, tm):
    t, d = x.shape
    n = w_t.shape[0]
    tm = min(tm, t)
    npart = len(out_dtypes)
    width = n // npart

    def body(x_ref, g_ref, w_ref, xn_ref, *part_refs):
        xv = x_ref[...]
        rstd = lax.rsqrt(jnp.mean(xv * xv, axis=-1, keepdims=True) + RMS_EPS)
        xn = (xv * rstd * g_ref[...]).astype(BF16)
        xn_ref[...] = xn
        for p, ref in enumerate(part_refs):
            ref[...] = _dot(xn, w_ref[p * width:(p + 1) * width, :], 1, 1).astype(out_dtypes[p])

    row = pl.BlockSpec((tm, d), lambda i: (i, 0))
    return pl.pallas_call(
        body, name=name, grid=(t // tm,),
        in_specs=[row, pl.BlockSpec((1, d), lambda i: (0, 0)), pl.BlockSpec((n, d), lambda i: (0, 0))],
        out_specs=[row] + [pl.BlockSpec((tm, width), lambda i: (i, 0))] * npart,
        out_shape=[jax.ShapeDtypeStruct((t, d), BF16)] + [jax.ShapeDtypeStruct((t, width), dt) for dt in out_dtypes],
        compiler_params=_cp())(x, gain, w_t)


def _mm_shared_rhs(a_list, b, *, name, tm, out_dtype=BF16, send=None):
    k, n = b.shape
    assert all(a.shape[0] == k and a.shape[1] % tm == 0 and a.shape[1] == a_list[0].shape[1] for a in a_list)
    m = a_list[0].shape[1]
    na = len(a_list)
    nsteps = m // tm
    fused = send is not None

    def body(*refs):
        if fused:
            start, finish = _direct_exchange_phases(refs[na + 1], refs[2 * na + 2], *refs[2 * na + 3:])
            pl.when(pl.program_id(0) == 0)(start)
        first_out = na + 1 + fused
        bv = refs[na][...]
        for i in range(na):
            refs[first_out + i][...] = _dot(refs[i][...], bv, 0, 0).astype(out_dtype)
        if fused:
            pl.when(pl.program_id(0) == nsteps - 1)(finish)

    return pl.pallas_call(
        body, name=name, grid=(nsteps,),
        in_specs=[pl.BlockSpec((k, tm), lambda i: (0, i))] * na + [pl.BlockSpec((k, n), lambda i: (0, 0))]
        + ([HBM_SPEC] if fused else []),
        out_specs=[pl.BlockSpec((tm, n), lambda i: (i, 0))] * na + ([HBM_SPEC] if fused else []),
        out_shape=[jax.ShapeDtypeStruct((m, n), out_dtype)] * na
        + ([jax.ShapeDtypeStruct((8,) + send.shape[2:], send.dtype)] if fused else []),
        scratch_shapes=_direct_exchange_scratch() if fused else [],
        compiler_params=_cp(dimension_semantics=("arbitrary",)))(*a_list, b, *([send] if fused else []))


def _rmsnorm_bwd(x, gain, dxn, dres, *, name, scale, tm=512):
    t, d = x.shape
    tm = min(tm, t)

    def body(x_ref, g_ref, dxn_ref, dres_ref, dx_ref, dxb_ref, dg_ref):
        xv = x_ref[...]
        rstd = lax.rsqrt(jnp.mean(xv * xv, axis=-1, keepdims=True) + RMS_EPS)
        xhat = xv * rstd
        dxn_v = dxn_ref[...]
        dxhat = dxn_v * g_ref[...]
        dx = dres_ref[...] + rstd * (dxhat - xhat * jnp.mean(dxhat * xhat, axis=-1, keepdims=True))
        dx_ref[...] = dx
        dxb_ref[...] = (dx * scale).astype(BF16)

        @pl.when(pl.program_id(0) == 0)
        def _():
            dg_ref[...] = jnp.zeros_like(dg_ref)

        dg_ref[...] += jnp.sum(dxn_v * xhat, axis=0, keepdims=True)

    row = pl.BlockSpec((tm, d), lambda i: (i, 0))
    vec = pl.BlockSpec((1, d), lambda i: (0, 0))
    return pl.pallas_call(
        body, name=name, grid=(t // tm,), in_specs=[row, vec, row, row], out_specs=[row, row, vec],
        out_shape=[jax.ShapeDtypeStruct((t, d), F32), jax.ShapeDtypeStruct((t, d), BF16), jax.ShapeDtypeStruct((1, d), F32)],
        compiler_params=_cp())(x, gain, dxn, dres)


def _loss_head(h, gain, target, *, name, tm=512):
    t, d = h.shape
    tm = min(tm, t)

    def body(h_ref, g_ref, t_ref, dh_ref, dhb_ref, dg_ref, loss_ref):
        hv = h_ref[...]
        rstd = lax.rsqrt(jnp.mean(hv * hv, axis=-1, keepdims=True) + RMS_EPS)
        xhat = hv * rstd
        err = xhat * g_ref[...] - t_ref[...]
        dy = err * (1.0 / d)
        dxhat = dy * g_ref[...]
        dh = rstd * (dxhat - xhat * jnp.mean(dxhat * xhat, axis=-1, keepdims=True))
        dh_ref[...] = dh
        dhb_ref[...] = (dh * MACARON).astype(BF16)

        @pl.when(pl.program_id(0) == 0)
        def _():
            dg_ref[...] = jnp.zeros_like(dg_ref)
            loss_ref[...] = jnp.zeros_like(loss_ref)

        dg_ref[...] += jnp.sum(dy * xhat, axis=0, keepdims=True)
        part = jnp.sum(jnp.sum(err * err, axis=-1, keepdims=True), axis=0, keepdims=True) * (0.5 / d)
        loss_ref[...] += jnp.broadcast_to(part, loss_ref.shape)

    row = pl.BlockSpec((tm, d), lambda i: (i, 0))
    vec = pl.BlockSpec((1, d), lambda i: (0, 0))
    return pl.pallas_call(
        body, name=name, grid=(t // tm,), in_specs=[row, vec, row],
        out_specs=[row, row, vec, pl.BlockSpec((1, LANES), lambda i: (0, 0))],
        out_shape=[jax.ShapeDtypeStruct((t, d), F32), jax.ShapeDtypeStruct((t, d), BF16), jax.ShapeDtypeStruct((1, d), F32),
                   jax.ShapeDtypeStruct((1, LANES), F32)],
        compiler_params=_cp())(h, gain, target)


def _norm_gate_up(x, gain, wg, wu, *, name, tm=256, tf=2816, pack=None, seg_rows=()):
    t, d = x.shape
    f = wg.shape[0]
    tm, tf = min(tm, t), min(tf, f)
    assert f % tf == 0
    ni, nj = t // tm, f // tf
    nseg = len(seg_rows)

    def body(x_ref, g_ref, wg_ref, wu_ref, *rest):
        if pack is not None:
            pack_ref, xn_ref, gg_ref, uu_ref, act_ref = rest[:5]
            start, forward, finish = _gather_phases(pack_ref, rest[5:5 + nseg], seg_rows, *rest[5 + nseg:])
            step = pl.program_id(0) * nj + pl.program_id(1)
            pl.when(step == 0)(start)
            pl.when(step == (3 * ni * nj) // 4)(forward)
        else:
            xn_ref, gg_ref, uu_ref, act_ref = rest

        @pl.when(pl.program_id(1) == 0)
        def _():
            xv = x_ref[...]
            rstd = lax.rsqrt(jnp.mean(xv * xv, axis=-1, keepdims=True) + RMS_EPS)
            xn_ref[...] = (xv * rstd * g_ref[...]).astype(BF16)

        xn = xn_ref[...]
        gv = _dot(xn, wg_ref[...], 1, 1)
        uv = _dot(xn, wu_ref[...], 1, 1)
        gg_ref[...] = gv.astype(BF16)
        uu_ref[...] = uv.astype(BF16)
        act_ref[...] = (gv * _sigmoid(gv) * uv).astype(BF16)
        if pack is not None:
            pl.when(step == ni * nj - 1)(finish)

    row = pl.BlockSpec((tm, d), lambda i, j: (i, 0))
    wsp = pl.BlockSpec((tf, d), lambda i, j: (j, 0))
    osp = pl.BlockSpec((tm, tf), lambda i, j: (i, j))
    fused = pack is not None
    return pl.pallas_call(
        body, name=name, grid=(ni, nj),
        in_specs=[row, pl.BlockSpec((1, d), lambda i, j: (0, 0)), wsp, wsp] + ([HBM_SPEC] if fused else []),
        out_specs=[row, osp, osp, osp] + [HBM_SPEC] * nseg,
        out_shape=[jax.ShapeDtypeStruct((t, d), BF16)] + [jax.ShapeDtypeStruct((t, f), BF16)] * 3
        + [jax.ShapeDtypeStruct((8, n, d), BF16) for n in seg_rows],
        scratch_shapes=_gather_scratch() if fused else [],
        compiler_params=_cp(dimension_semantics=("arbitrary", "arbitrary")))(x, gain, wg, wu, *([pack] if fused else []))


def _swiglu_bwd(dout, wd, gg, uu, chip_part=None, *, name, tm=512, tf=1408):
    t, d = dout.shape
    f = wd.shape[0]
    tm, tf = min(tm, t), min(tf, f)
    nj, ni = f // tf, t // tm
    fused = chip_part is not None

    def body(do_ref, wd_ref, g_ref, u_ref, *rest):
        if fused:
            part_ref, dg_ref, du_ref, parts_ref = rest[:4]
            start, finish = _chip_exchange_phases(part_ref, parts_ref, *rest[4:])
            step = pl.program_id(0) * ni + pl.program_id(1)
            pl.when(step == 0)(start)
        else:
            dg_ref, du_ref = rest
        dact = _dot(do_ref[...], wd_ref[...], 1, 1)
        gv = g_ref[...].astype(F32)
        uv = u_ref[...].astype(F32)
        sg = _sigmoid(gv)
        dg_ref[...] = (dact * uv * (sg * (1.0 + gv * (1.0 - sg)))).astype(BF16)
        du_ref[...] = (dact * (gv * sg)).astype(BF16)
        if fused:
            pl.when(step == nj * ni - 1)(finish)

    osp = pl.BlockSpec((tm, tf), lambda j, i: (i, j))
    return pl.pallas_call(
        body, name=name, grid=(nj, ni),
        in_specs=[pl.BlockSpec((tm, d), lambda j, i: (i, 0)), pl.BlockSpec((tf, d), lambda j, i: (j, 0)), osp, osp]
        + ([HBM_SPEC] if fused else []),
        out_specs=[osp, osp] + ([HBM_SPEC] if fused else []),
        out_shape=[jax.ShapeDtypeStruct((t, f), BF16)] * 2
        + ([jax.ShapeDtypeStruct(chip_part.shape, chip_part.dtype)] if fused else []),
        scratch_shapes=_chip_exchange_scratch() if fused else [],
        compiler_params=_cp(dimension_semantics=("arbitrary", "arbitrary")))(dout, wd, gg, uu, *([chip_part] if fused else []))


def _shift_down(x, n):
    rows = lax.broadcasted_iota(jnp.int32, x.shape, 0)
    return jnp.where(rows >= n, pltpu.roll(x, n, 0), 0.0)


def _shift_up(x, n):
    t = x.shape[0]
    rows = lax.broadcasted_iota(jnp.int32, x.shape, 0)
    return jnp.where(rows < t - n, pltpu.roll(x, t - n, 0), 0.0)


def _conv_fwd(pa, conv_w, *, name):
    t = pa.shape[0]
    nb = pa.shape[1] // 3 // LANES

    def body(b_ref, c_ref, x_ref, w_ref, y_ref):
        u = c_ref[...] * x_ref[...]
        w = w_ref[...]
        conv = w[2:3, :] * u + w[1:2, :] * _shift_down(u, 1) + w[0:1, :] * _shift_down(u, 2)
        y_ref[...] = (b_ref[...] * conv).astype(BF16)

    def col(off):
        return pl.BlockSpec((t, LANES), lambda j: (0, off + j))

    return pl.pallas_call(
        body, name=name, grid=(nb,),
        in_specs=[col(0), col(nb), col(2 * nb), pl.BlockSpec((3, LANES), lambda j: (0, j))],
        out_specs=pl.BlockSpec((t, LANES), lambda j: (0, j)),
        out_shape=jax.ShapeDtypeStruct((t, nb * LANES), BF16), compiler_params=_cp())(pa, pa, pa, conv_w)


def _conv_bwd(pa, dy, conv_w, *, name):
    t = pa.shape[0]
    nb = pa.shape[1] // 3 // LANES

    def body(b_ref, c_ref, x_ref, dy_ref, w_ref, db_ref, dc_ref, dx_ref, dw_ref):
        cv, xv = c_ref[...], x_ref[...]
        u = cv * xv
        u1, u2 = _shift_down(u, 1), _shift_down(u, 2)
        w = w_ref[...]
        conv = w[2:3, :] * u + w[1:2, :] * u1 + w[0:1, :] * u2
        dyv = dy_ref[...]
        db_ref[...] = (dyv * conv).astype(BF16)
        dconv = dyv * b_ref[...]
        du = w[2:3, :] * dconv + w[1:2, :] * _shift_up(dconv, 1) + w[0:1, :] * _shift_up(dconv, 2)
        dc_ref[...] = (du * xv).astype(BF16)
        dx_ref[...] = (du * cv).astype(BF16)
        dw_ref[0:1, :] = jnp.sum(dconv * u2, axis=0, keepdims=True)
        dw_ref[1:2, :] = jnp.sum(dconv * u1, axis=0, keepdims=True)
        dw_ref[2:3, :] = jnp.sum(dconv * u, axis=0, keepdims=True)

    def col(off):
        return pl.BlockSpec((t, LANES), lambda j: (0, off + j))

    osp = pl.BlockSpec((t, LANES), lambda j: (0, j))
    wsp = pl.BlockSpec((3, LANES), lambda j: (0, j))
    return pl.pallas_call(
        body, name=name, grid=(nb,), in_specs=[col(0), col(nb), col(2 * nb), col(0), wsp],
        out_specs=[osp, osp, osp, wsp],
        out_shape=[jax.ShapeDtypeStruct((t, nb * LANES), BF16)] * 3 + [jax.ShapeDtypeStruct((3, nb * LANES), F32)],
        compiler_params=_cp())(pa, pa, pa, dy, conv_w)


def _sb_consts():
    j = lax.broadcasted_iota(jnp.int32, (SB_KEYS, SB_KEYS), 0)
    s = lax.broadcasted_iota(jnp.int32, (SB_KEYS, SB_KEYS), 1)
    after = (j > s).astype(BF16)
    upto = (j <= s).astype(BF16)
    before = (j < s).astype(BF16)
    return after, jnp.stack([upto, before])


def _log_sigmoid(z):
    return jnp.minimum(z, 0.0) - jnp.log(1.0 + jnp.exp(-jnp.abs(z)))


def _attn_fwd(pb, late_pack, seg_rows, *, name, tq=256):
    t = pb.shape[0]
    npair = pb.shape[1] // 3 // LANES
    tq = min(tq, t)
    nq = t // tq
    cmat, _ = _sb_consts()
    scale = 1.0 / math.sqrt(LANES // 2)

    nseg = len(seg_rows)

    def body(q_ref, k_ref, v_ref, c_ref, late_ref, y_ref, lt_ref, *rest):
        i = pl.program_id(1)
        pair = pl.program_id(0)
        scratch = rest[nseg:nseg + 4]
        start, forward, finish = _gather_phases(late_ref, rest[:nseg], seg_rows, *rest[nseg + 4:])
        pl.when((pair == 0) & (i == 0))(start)
        pl.when((pair == npair - 1) & (i == nq // 2))(forward)
        lane = lax.broadcasted_iota(jnp.int32, (tq, LANES), 1)
        rowpos = i * tq + lax.broadcasted_iota(jnp.int32, (tq, SB_KEYS), 0)
        colid = lax.broadcasted_iota(jnp.int32, (tq, SB_KEYS), 1)
        q2 = q_ref[...] * jnp.asarray(scale, BF16)
        cm = c_ref[...]
        hi_lanes = lane >= LANES // 2
        qhs = [jnp.where(hi_lanes == (hh == 1), q2, jnp.zeros_like(q2)) for hh in range(2)]
        per_q = tq // SB_KEYS

        def blk(jb):
            return pl.ds(pl.multiple_of(jb * SB_KEYS, SB_KEYS), SB_KEYS)

        zbuf, wbuf, accbuf, runbuf = scratch

        def scores(jb):
            kb = k_ref[blk(jb), :]
            for hh in range(2):
                zbuf[hh] = _dot(qhs[hh], kb, 1, 1)

        def values(jb):
            vb = v_ref[blk(jb), :]
            for hh in range(2):
                accbuf[hh] += _dot(wbuf[hh], vb)

        def trip(jb, masked, first=False):
            mask = (jb * SB_KEYS + colid) < rowpos if masked else None
            if not first:
                values(jb + 1)
            pre, css = [], []
            for hh in range(2):
                z = zbuf[hh]
                lb = _log_sigmoid(z)
                lk = lb - z
                if masked:
                    lk = jnp.where(mask, lk, 0.0)
                lk_hi, lk_lo = _split2(lk)
                css.append(_dot(lk_hi, cm) + _dot(lk_lo, cm))
                run = runbuf[hh]
                pre.append(lb + run)
                runbuf[hh] = run + jnp.sum(lk, axis=1, keepdims=True)
            scores(jnp.maximum(jb - 1, 0))
            for hh in range(2):
                w = jnp.exp(pre[hh] + css[hh])
                if masked:
                    w = jnp.where(mask, w, 0.0)
                wbuf[hh] = w.astype(BF16)

        nfull = i * per_q
        accbuf[...] = jnp.zeros_like(accbuf)
        runbuf[...] = jnp.zeros_like(runbuf)
        scores(nfull + per_q - 1)
        for dblk in reversed(range(per_q)):
            trip(nfull + dblk, True, first=dblk == per_q - 1)

        def full_block(n, carry):
            trip(nfull - 1 - n, False)
            return carry

        lax.fori_loop(0, nfull, full_block, 0)
        values(0)
        y_ref[...] = jnp.where(hi_lanes, accbuf[1], accbuf[0]).astype(BF16)
        lt_ref[...] = jnp.where(hi_lanes, runbuf[1], runbuf[0])
        pl.when((pair == npair - 1) & (i == nq - 1))(finish)

    return pl.pallas_call(
        body, name=name, grid=(npair, nq),
        in_specs=[pl.BlockSpec((tq, LANES), lambda p, i: (i, p)),
                  pl.BlockSpec((t, LANES), lambda p, i: (0, npair + p)),
                  pl.BlockSpec((t, LANES), lambda p, i: (0, 2 * npair + p)),
                  pl.BlockSpec((SB_KEYS, SB_KEYS), lambda p, i: (0, 0)),
                  HBM_SPEC],
        out_specs=[pl.BlockSpec((tq, LANES), lambda p, i: (i, p))] * 2 + [HBM_SPEC] * nseg,
        out_shape=[jax.ShapeDtypeStruct((t, npair * LANES), BF16), jax.ShapeDtypeStruct((t, npair * LANES), F32),
                   ] + [jax.ShapeDtypeStruct((8, n, late_pack.shape[1]), late_pack.dtype) for n in seg_rows],
        scratch_shapes=[pltpu.VMEM((2, tq, SB_KEYS), F32), pltpu.VMEM((2, tq, SB_KEYS), BF16),
                        pltpu.VMEM((2, tq, LANES), F32), pltpu.VMEM((2, tq, 1), F32)] + _gather_scratch(),
        compiler_params=_cp(dimension_semantics=("arbitrary", "arbitrary")))(pb, pb, pb, cmat, late_pack)


def _attn_bwd(pb, dy, ltot, chip_part, *, name, tq=256):
    t = pb.shape[0]
    npair = pb.shape[1] // 3 // LANES
    tq = min(tq, t)
    nq = t // tq
    _, cmats = _sb_consts()
    scale = 1.0 / math.sqrt(LANES // 2)

    def body(q_ref, k_ref, v_ref, dy_ref, lt_ref, c_ref, part_ref, dq_ref, dk_ref, dv_ref, parts_ref, dk_acc, dv_acc, *rest):
        i = pl.program_id(1)
        pair = pl.program_id(0)
        scratch = rest[:6]
        start, finish = _direct_exchange_phases(part_ref, parts_ref, *rest[6:])
        pl.when((pair == 0) & (i == 0))(start)

        @pl.when(i == 0)
        def _():
            dk_acc[...] = jnp.zeros_like(dk_acc)
            dv_acc[...] = jnp.zeros_like(dv_acc)

        lane = lax.broadcasted_iota(jnp.int32, (tq, LANES), 1)
        rowpos = i * tq + lax.broadcasted_iota(jnp.int32, (tq, SB_KEYS), 0)
        colid = lax.broadcasted_iota(jnp.int32, (tq, SB_KEYS), 1)
        q2 = q_ref[...] * jnp.asarray(scale, BF16)
        do2 = dy_ref[...].astype(BF16)
        ltv = lt_ref[...]
        c_upto, c_before = c_ref[0], c_ref[1]
        hi_lanes = lane >= LANES // 2
        sels = [hi_lanes == (hh == 1) for hh in range(2)]
        qhs = [jnp.where(s, q2, jnp.zeros_like(q2)) for s in sels]
        dohs = [jnp.where(s, do2, jnp.zeros_like(do2)) for s in sels]
        lts = [ltv[:, 0:1], ltv[:, LANES // 2:LANES // 2 + 1]]
        per_q = tq // SB_KEYS

        def blk(jb):
            return pl.ds(pl.multiple_of(jb * SB_KEYS, SB_KEYS), SB_KEYS)

        zbuf, dabuf, dzbuf, abuf, dqbuf, sumbuf = scratch

        def scores(jb):
            kb, vb = k_ref[blk(jb), :], v_ref[blk(jb), :]
            for hh in range(2):
                zbuf[hh] = _dot(qhs[hh], kb, 1, 1)
                dabuf[hh] = _dot(dohs[hh], vb, 1, 1)

        def products(jb):
            kb = k_ref[blk(jb), :]
            dk_acc[blk(jb), :] += _dot(dzbuf[0], qhs[0], 0, 0) + _dot(dzbuf[1], qhs[1], 0, 0)
            dv_acc[blk(jb), :] += _dot(abuf[0], dohs[0], 0, 0) + _dot(abuf[1], dohs[1], 0, 0)
            for hh in range(2):
                dqbuf[hh] += _dot(dzbuf[hh], kb)

        def trip(jb, masked):
            mask = (jb * SB_KEYS + colid) < rowpos if masked else None
            products(jnp.maximum(jb - 1, 0))
            lbs, css, es, ces = [], [], [], []
            for hh in range(2):
                z = zbuf[hh]
                lb = _log_sigmoid(z)
                lk = lb - z
                if masked:
                    lk = jnp.where(mask, lk, 0.0)
                lk_hi, lk_lo = _split2(lk)
                css.append(_dot(lk_hi, c_upto) + _dot(lk_lo, c_upto))
                csum = sumbuf[2 * hh]
                lbs.append((lb, lb + (lts[hh] - csum)))
                sumbuf[2 * hh] = csum + jnp.sum(lk, axis=1, keepdims=True)
            for hh in range(2):
                a = jnp.exp(lbs[hh][1] - css[hh])
                if masked:
                    a = jnp.where(mask, a, 0.0)
                e = a * dabuf[hh]
                e_hi, e_lo = _split2(e)
                ces.append(_dot(e_hi, c_before) + _dot(e_lo, c_before))
                abuf[hh] = a.astype(BF16)
                es.append(e)
            scores(jnp.minimum(jb + 1, last))
            for hh in range(2):
                prun = sumbuf[2 * hh + 1]
                beta = jnp.exp(lbs[hh][0])
                dz = es[hh] * (1.0 - beta) - (prun + ces[hh]) * beta
                if masked:
                    dz = jnp.where(mask, dz, 0.0)
                dzbuf[hh] = dz.astype(BF16)
                sumbuf[2 * hh + 1] = prun + jnp.sum(es[hh], axis=1, keepdims=True)

        nfull = i * per_q
        last = nfull + per_q - 1
        for buf in (dzbuf, abuf, dqbuf, sumbuf):
            buf[...] = jnp.zeros_like(buf)
        scores(0)

        def full_block(jb, carry):
            trip(jb, False)
            return carry

        lax.fori_loop(0, nfull, full_block, 0)
        for dblk in range(per_q):
            trip(nfull + dblk, True)
        products(last)
        dq_ref[...] = (jnp.where(hi_lanes, dqbuf[1], dqbuf[0]) * scale).astype(BF16)

        @pl.when(i == nq - 1)
        def _():
            dk_ref[...] = dk_acc[...].astype(BF16)
            dv_ref[...] = dv_acc[...].astype(BF16)

        pl.when((pair == npair - 1) & (i == nq - 1))(finish)

    blk = pl.BlockSpec((tq, LANES), lambda p, i: (i, p))
    full = pl.BlockSpec((t, LANES), lambda p, i: (0, p))
    return pl.pallas_call(
        body, name=name, grid=(npair, nq),
        in_specs=[blk,
                  pl.BlockSpec((t, LANES), lambda p, i: (0, npair + p)),
                  pl.BlockSpec((t, LANES), lambda p, i: (0, 2 * npair + p)),
                  pl.BlockSpec((tq, LANES), lambda p, i: (i, npair + p)),
                  blk,
                  pl.BlockSpec((2, SB_KEYS, SB_KEYS), lambda p, i: (0, 0, 0)),
                  HBM_SPEC],
        out_specs=[blk, full, full, HBM_SPEC],
        out_shape=[jax.ShapeDtypeStruct((t, npair * LANES), BF16)] * 3
        + [jax.ShapeDtypeStruct((8,) + chip_part.shape[2:], chip_part.dtype)],
        scratch_shapes=[pltpu.VMEM((t, LANES), F32), pltpu.VMEM((t, LANES), F32),
                        pltpu.VMEM((2, tq, SB_KEYS), F32), pltpu.VMEM((2, tq, SB_KEYS), F32),
                        pltpu.VMEM((2, tq, SB_KEYS), BF16), pltpu.VMEM((2, tq, SB_KEYS), BF16),
                        pltpu.VMEM((2, tq, LANES), F32), pltpu.VMEM((4, tq, 1), F32)] + _direct_exchange_scratch(),
        compiler_params=_cp(dimension_semantics=("arbitrary", "arbitrary")))(pb, pb, pb, dy, ltot, cmats, chip_part)


def _hgrn_consts():
    t = lax.broadcasted_iota(jnp.int32, (CHUNK, CHUNK), 0)
    s = lax.broadcasted_iota(jnp.int32, (CHUNK, CHUNK), 1)
    masks = []
    for lvl in range(N_LEVELS):
        half = CHUNK >> (lvl + 1)
        same = (t // (2 * half)) == (s // (2 * half))
        masks.append((same & (t % (2 * half) >= half) & (s % (2 * half) < half)).astype(F32))
    masks.append((t == s).astype(F32))
    prefix = (s <= t).astype(BF16)
    suffix = (s >= t).astype(BF16)
    return prefix, jnp.stack(masks), suffix


def _hgrn_gates(qr, fr, lbv):
    sg = _sigmoid(fr)
    fval = lbv + (1.0 - lbv) * sg
    kk = (1.0 - lbv) * _sigmoid(-fr)
    sq = _sigmoid(qr)
    return sg, fval, jnp.log(fval), kk, sq, qr * sq


def _lower_bound(c_ref):
    c = c_ref[...]
    mx = jnp.max(c, axis=0, keepdims=True)
    ex = jnp.exp(c - mx)
    return ex[1:2, :] / jnp.sum(ex, axis=0, keepdims=True)


def _level_ref(b, lvl):
    half = CHUNK >> (lvl + 1)
    seg = 2 * half
    if seg >= 8:
        b3 = b.reshape(CHUNK // seg, seg, LANES)
        return jnp.broadcast_to(b3[:, half - 1:half, :], b3.shape).reshape(CHUNK, LANES)
    pos = lax.broadcasted_iota(jnp.int32, b.shape, 0) % seg
    out = b
    for p in range(seg):
        if p != half - 1:
            out = jnp.where(pos == p, pltpu.roll(b, (p - (half - 1)) % CHUNK, 0), out)
    return out


def _hgrn_levels(b, qs, kk):
    out = []
    for lvl in range(N_LEVELS):
        fac = jnp.exp(-jnp.abs(b - _level_ref(b, lvl)))
        out.append((qs * fac, kk * fac, fac, fac))
    out.append((qs, kk, None, None))
    return out


def _split2(x):
    hi = x.astype(BF16)
    return hi, (x - hi.astype(F32)).astype(BF16)


def _hgrn_fwd(pc, c_lb, out_norm, *, name, tc=512):
    t = pc.shape[0]
    nh = pc.shape[1] // 4 // LANES
    tc = min(tc, t)
    nch = tc // CHUNK
    cum_all, masks, _ = _hgrn_consts()

    def body(q_ref, f_ref, i_ref, g_ref, lb_ref, on_ref, cum_ref, m_ref, y_ref, o_ref, st_ref, state):
        @pl.when(pl.program_id(1) == 0)
        def _():
            state[...] = jnp.zeros_like(state)

        lbv = _lower_bound(lb_ref)
        onv = on_ref[...]

        def chunk(c, carry):
            rows = pl.ds(pl.multiple_of(c * CHUNK, CHUNK), CHUNK)
            for hh in range(HGRN_HEADS):
                lanes = slice(hh * LANES, (hh + 1) * LANES)
                _, _, g, kk, _, qs = _hgrn_gates(q_ref[rows, lanes], f_ref[rows, lanes], lbv[:, lanes])
                vb = i_ref[rows, lanes].astype(BF16)
                b = _dot_exact_lhs(cum_ref[...], g)
                scores = jnp.zeros((CHUNK, CHUNK), F32)
                for lvl, (ql, kl, _, _) in enumerate(_hgrn_levels(b, qs, kk)):
                    scores = scores + _dot(ql.astype(BF16), kl.astype(BF16), 1, 1) * m_ref[lvl]
                st = state[hh]
                st_ref[hh, c] = st
                o = _dot(scores.astype(BF16), vb) + _dot((qs * jnp.exp(b)).astype(BF16), st.astype(BF16), 1, 1)
                blast = b[CHUNK - 1:CHUNK, :]
                kdec = (kk * jnp.exp(blast - b)).astype(BF16)
                state[hh] = st * jnp.exp(blast) + _dot(vb, kdec, 0, 0)
                o_ref[rows, lanes] = o
                rstd = lax.rsqrt(jnp.mean(o * o, axis=-1, keepdims=True) + RMS_EPS)
                gate = g_ref[rows, lanes]
                y_ref[rows, lanes] = (o * rstd * onv * (gate * _sigmoid(gate))).astype(BF16)
            return carry

        lax.fori_loop(0, nch, chunk, 0, unroll=2)

    hw = HGRN_HEADS * LANES

    def col(off):
        return pl.BlockSpec((tc, hw), lambda h, i: (i, off // HGRN_HEADS + h))

    osp = pl.BlockSpec((tc, hw), lambda h, i: (i, h))
    return pl.pallas_call(
        body, name=name, grid=(nh // HGRN_HEADS, t // tc),
        in_specs=[col(0), col(nh), col(2 * nh), col(3 * nh),
                  pl.BlockSpec((2, hw), lambda h, i: (0, h)),
                  pl.BlockSpec((1, LANES), lambda h, i: (0, 0)),
                  pl.BlockSpec(cum_all.shape, lambda h, i: (0, 0)),
                  pl.BlockSpec(masks.shape, lambda h, i: (0, 0, 0))],
        out_specs=[osp, osp, pl.BlockSpec((HGRN_HEADS, nch, LANES, LANES), lambda h, i: (h, i, 0, 0))],
        out_shape=[jax.ShapeDtypeStruct((t, nh * LANES), BF16), jax.ShapeDtypeStruct((t, nh * LANES), F32),
                   jax.ShapeDtypeStruct((nh, t // CHUNK, LANES, LANES), F32)],
        scratch_shapes=[pltpu.VMEM((HGRN_HEADS, LANES, LANES), F32)],
        compiler_params=_cp())(pc, pc, pc, pc, c_lb, out_norm, cum_all, masks)


def _hgrn_bwd(pc, o_saved, states, dy, c_lb, out_norm, send, *, name, tc=512):
    t = pc.shape[0]
    nh = pc.shape[1] // 4 // LANES
    tc = min(tc, t)
    nch = tc // CHUNK
    nt = t // tc
    cum_all, masks, suffix = _hgrn_consts()
    ngroup = nh // HGRN_HEADS

    def body(q_ref, f_ref, i_ref, g_ref, o_ref, st_ref, dy_ref, lb_ref, on_ref, cum_ref, m_ref, suf_ref, send_ref,
             dq_ref, df_ref, di_ref, dg_ref, dlb_ref, don_ref, parts_ref, dstate, *sems):
        start, finish = _direct_exchange_phases(send_ref, parts_ref, *sems)
        pl.when((pl.program_id(0) == 0) & (pl.program_id(1) == 0))(start)

        @pl.when(pl.program_id(1) == 0)
        def _():
            dstate[...] = jnp.zeros_like(dstate)
            dlb_ref[...] = jnp.zeros_like(dlb_ref)
            don_ref[...] = jnp.zeros_like(don_ref)

        lbv = _lower_bound(lb_ref)
        onv = on_ref[...]

        def head(hh, c, rows):
            lanes = slice(hh * LANES, (hh + 1) * LANES)
            qr = q_ref[rows, lanes]
            sg, fval, g, kk, sq, qs = _hgrn_gates(qr, f_ref[rows, lanes], lbv[:, lanes])
            vb = i_ref[rows, lanes].astype(BF16)
            o = o_ref[rows, lanes]
            gate = g_ref[rows, lanes]
            sgt = _sigmoid(gate)
            rstd = lax.rsqrt(jnp.mean(o * o, axis=-1, keepdims=True) + RMS_EPS)
            ohat = o * rstd
            dyv = dy_ref[rows, lanes]
            don = dyv * (gate * sgt)
            dg_ref[rows, lanes] = (dyv * ohat * onv * (sgt * (1.0 + gate * (1.0 - sgt)))).astype(BF16)
            don_ref[:, lanes] += jnp.sum(don * ohat, axis=0, keepdims=True)
            dxhat = don * onv
            dob = (rstd * (dxhat - ohat * jnp.mean(dxhat * ohat, axis=-1, keepdims=True))).astype(BF16)
            b = _dot_exact_lhs(cum_ref[...], g)
            blast = b[CHUNK - 1:CHUNK, :]
            eb = jnp.exp(b)
            edec = jnp.exp(blast - b)
            st32 = st_ref[hh, c]
            st = st32.astype(BF16)
            dst = dstate[hh]
            dstb = dst.astype(BF16)
            da = _dot(dob, vb, 1, 1)
            levels = _hgrn_levels(b, qs, kk)
            scores = jnp.zeros((CHUNK, CHUNK), F32)
            dq = eb * _dot(dob, st)
            dk_inter = edec * _dot(vb, dstb)
            dk = dk_inter
            for lvl, (ql, kl, eq, ek) in enumerate(levels):
                mk = m_ref[lvl]
                (qh, qlo), (kh, klo) = _split2(ql), _split2(kl)
                scores = scores + _dot(qh, kh, 1, 1) * mk
                dal = (da * mk).astype(BF16)
                dql = _dot(dal, kh) + _dot(dal, klo)
                dkl = _dot(dal, qh, 0, 0) + _dot(dal, qlo, 0, 0)
                dq = dq + (dql if eq is None else dql * eq)
                dk = dk + (dkl if ek is None else dkl * ek)
            kdec = (kk * edec).astype(BF16)
            dv = _dot(scores.astype(BF16), dob, 0, 0) + _dot(kdec, dstb, 1, 1)
            dstate[hh] = dst * jnp.exp(blast) + _dot(dob, (qs * eb).astype(BF16), 0, 0)
            db = qs * dq - kk * dk
            last = jnp.sum(kk * dk_inter, axis=0, keepdims=True) + jnp.exp(blast) * jnp.sum(dst * st32, axis=0, keepdims=True)
            dgl = _dot_exact_lhs(suf_ref[...], db) + last
            dfv = dgl / fval - dk
            df_ref[rows, lanes] = (dfv * (1.0 - lbv[:, lanes]) * sg * (1.0 - sg)).astype(BF16)
            dlb_ref[:, lanes] += jnp.sum(dfv * (1.0 - sg), axis=0, keepdims=True)
            dq_ref[rows, lanes] = (dq * (sq * (1.0 + qr * (1.0 - sq)))).astype(BF16)
            di_ref[rows, lanes] = dv.astype(BF16)

        def chunk(n, carry):
            c = nch - 1 - n
            rows = pl.ds(pl.multiple_of(c * CHUNK, CHUNK), CHUNK)
            for hh in range(HGRN_HEADS):
                head(hh, c, rows)
            return carry

        lax.fori_loop(0, nch, chunk, 0, unroll=2)
        pl.when((pl.program_id(0) == ngroup - 1) & (pl.program_id(1) == nt - 1))(finish)

    hw = HGRN_HEADS * LANES

    def col(off):
        return pl.BlockSpec((tc, hw), lambda h, i: (nt - 1 - i, off // HGRN_HEADS + h))

    osp = pl.BlockSpec((tc, hw), lambda h, i: (nt - 1 - i, h))
    vec = pl.BlockSpec((1, hw), lambda h, i: (0, h))
    return pl.pallas_call(
        body, name=name, grid=(nh // HGRN_HEADS, nt),
        in_specs=[col(0), col(nh), col(2 * nh), col(3 * nh), osp,
                  pl.BlockSpec((HGRN_HEADS, nch, LANES, LANES), lambda h, i: (h, nt - 1 - i, 0, 0)),
                  osp,
                  pl.BlockSpec((2, hw), lambda h, i: (0, h)),
                  pl.BlockSpec((1, LANES), lambda h, i: (0, 0)),
                  pl.BlockSpec(cum_all.shape, lambda h, i: (0, 0)),
                  pl.BlockSpec(masks.shape, lambda h, i: (0, 0, 0)),
                  pl.BlockSpec(suffix.shape, lambda h, i: (0, 0)),
                  HBM_SPEC],
        out_specs=[osp, osp, osp, osp, vec, vec, HBM_SPEC],
        out_shape=[jax.ShapeDtypeStruct((t, nh * LANES), BF16)] * 4 + [jax.ShapeDtypeStruct((1, nh * LANES), F32)] * 2
        + [jax.ShapeDtypeStruct((8,) + send.shape[2:], send.dtype)],
        scratch_shapes=[pltpu.VMEM((HGRN_HEADS, LANES, LANES), F32)] + _direct_exchange_scratch(),
        compiler_params=_cp(dimension_semantics=("arbitrary", "arbitrary")))(
            pc, pc, pc, pc, o_saved, states, dy, c_lb, out_norm, cum_all, masks, suffix, send)


HBM_SPEC = pl.BlockSpec(memory_space=pltpu.HBM)


def _gather_scratch():
    return [pltpu.SemaphoreType.DMA((7,)), pltpu.SemaphoreType.DMA((7,)), pltpu.SemaphoreType.DMA]


def _gather_phases(x_ref, out_refs, seg_rows, send_sems, recv_sems, local_sem):
    x, y, c = lax.axis_index("x"), lax.axis_index("y"), lax.axis_index("c")
    me, sibling = (x, y, c), (x, y, 1 - c)
    chips = [(1 - x, y), (x, 1 - y), (1 - x, 1 - y)]
    offs = [sum(seg_rows[:s]) for s in range(len(seg_rows))]
    assert sum(seg_rows) == x_ref.shape[0]

    def index(px, py, pc):
        return 4 * px + 2 * py + pc

    def copies(k, block, to, own):
        return [pltpu.make_async_remote_copy(
            src_ref=x_ref.at[pl.ds(offs[s], n)] if own else out_refs[s].at[index(*block)],
            dst_ref=out_refs[s].at[index(*block)],
            send_sem=send_sems.at[k], recv_sem=recv_sems.at[k], device_id=to, device_id_type=MESH)
            for s, n in enumerate(seg_rows)]

    def all_bytes(k):
        return pltpu.make_async_remote_copy(src_ref=x_ref, dst_ref=x_ref, send_sem=send_sems.at[k],
                                            recv_sem=recv_sems.at[k], device_id=me, device_id_type=MESH)

    mine = [pltpu.make_async_copy(x_ref.at[pl.ds(offs[s], n)], out_refs[s].at[index(*me)], local_sem)
            for s, n in enumerate(seg_rows)]
    first = copies(0, me, sibling, True)
    for j, chip in enumerate(chips):
        first += copies(1 + j, me, (*chip, c), True)

    def start():
        for cp in mine + first:
            cp.start()

    def forward():
        for j, chip in enumerate(chips):
            all_bytes(1 + j).wait_recv()
            for cp in copies(4 + j, (*chip, c), sibling, False):
                cp.start()

    def finish():
        all_bytes(0).wait_recv()
        for j in range(3):
            all_bytes(4 + j).wait_recv()
        for k in range(7):
            all_bytes(k).wait_send()
        pltpu.make_async_copy(x_ref, x_ref, local_sem).wait()

    return start, forward, finish


def _all_gather(xs, seg_rows=None, *, name):
    segs = [xs.shape[0]] if seg_rows is None else list(seg_rows)

    def body(x_ref, *rest):
        start, forward, finish = _gather_phases(x_ref, rest[:len(segs)], segs, *rest[len(segs):])
        start()
        forward()
        finish()

    outs = pl.pallas_call(
        body, name=name, in_specs=[HBM_SPEC], out_specs=[HBM_SPEC] * len(segs),
        out_shape=[jax.ShapeDtypeStruct((8, n, xs.shape[1]), xs.dtype) for n in segs],
        scratch_shapes=_gather_scratch())(xs)
    return outs[0] if seg_rows is None else outs


def _sibling_exchange(s, *, name):
    def body(s_ref, rb_ref, send_sem, recv_sem):
        x, y, c = lax.axis_index("x"), lax.axis_index("y"), lax.axis_index("c")
        cp = pltpu.make_async_remote_copy(
            src_ref=s_ref.at[:, 1 - c], dst_ref=rb_ref, send_sem=send_sem, recv_sem=recv_sem,
            device_id=(x, y, 1 - c), device_id_type=MESH)
        cp.start()
        cp.wait()

    return pl.pallas_call(
        body, name=name, in_specs=[HBM_SPEC], out_specs=HBM_SPEC,
        out_shape=jax.ShapeDtypeStruct(s.shape[:1] + s.shape[2:], s.dtype),
        scratch_shapes=[pltpu.SemaphoreType.DMA, pltpu.SemaphoreType.DMA])(s)


def _row_tile(n, cap=1024):
    return max(b for b in range(16, cap + 1, 16) if n % b == 0)


def _pair_add(s, rb, core, *, name):
    nchip, _, r, c = s.shape
    tb = _row_tile(r)

    def body(core_ref, a_ref, b_ref, o_ref):
        o_ref[...] = (a_ref[...].astype(F32) + b_ref[...].astype(F32)).astype(BF16)

    blk = pl.BlockSpec((None, tb, c), lambda ch, i, cr: (ch, i, 0))
    return pl.pallas_call(
        body, name=name,
        grid_spec=pltpu.PrefetchScalarGridSpec(
            num_scalar_prefetch=1, grid=(nchip, r // tb),
            in_specs=[pl.BlockSpec((None, None, tb, c), lambda ch, i, cr: (ch, cr[0], i, 0)), blk],
            out_specs=blk),
        out_shape=jax.ShapeDtypeStruct((nchip, r, c), BF16), compiler_params=_cp())(core, s, rb)


def _chip_exchange_scratch():
    return [pltpu.SemaphoreType.DMA((3,)), pltpu.SemaphoreType.DMA((3,)), pltpu.SemaphoreType.DMA]


def _chip_exchange_phases(p_ref, out_ref, send_sems, recv_sems, local_sem):
    x, y, c = lax.axis_index("x"), lax.axis_index("y"), lax.axis_index("c")
    mine = 2 * x + y
    own = pltpu.make_async_copy(p_ref.at[mine], out_ref.at[mine], local_sem)
    copies = [pltpu.make_async_remote_copy(
        src_ref=p_ref.at[2 * tx + ty], dst_ref=out_ref.at[mine],
        send_sem=send_sems.at[k], recv_sem=recv_sems.at[k], device_id=(tx, ty, c), device_id_type=MESH)
        for k, (tx, ty) in enumerate([(1 - x, y), (x, 1 - y), (1 - x, 1 - y)])]

    def start():
        own.start()
        for cp in copies:
            cp.start()

    def finish():
        for cp in copies:
            cp.wait()
        own.wait()

    return start, finish


def _direct_exchange_scratch():
    return [pltpu.SemaphoreType.DMA((7,)), pltpu.SemaphoreType.DMA((7,)), pltpu.SemaphoreType.DMA]


def _direct_exchange_phases(s_ref, out_ref, send_sems, recv_sems, local_sem):
    x, y, c = lax.axis_index("x"), lax.axis_index("y"), lax.axis_index("c")
    me = 4 * x + 2 * y + c
    own = pltpu.make_async_copy(s_ref.at[2 * x + y, c], out_ref.at[me], local_sem)
    flips = [(fx, fy, fc) for fx in (0, 1) for fy in (0, 1) for fc in (0, 1) if (fx, fy, fc) != (0, 0, 0)]
    copies = []
    for k, (fx, fy, fc) in enumerate(flips):
        tx, ty, tc = (1 - x if fx else x), (1 - y if fy else y), (1 - c if fc else c)
        copies.append(pltpu.make_async_remote_copy(
            src_ref=s_ref.at[2 * tx + ty, tc], dst_ref=out_ref.at[me],
            send_sem=send_sems.at[k], recv_sem=recv_sems.at[k], device_id=(tx, ty, tc), device_id_type=MESH))

    def start():
        own.start()
        for cp in copies:
            cp.start()

    def finish():
        for cp in copies:
            cp.wait()
        own.wait()

    return start, finish


def _adamw_math(w, g, m, v):
    m2 = ADAM_B1 * m + (1.0 - ADAM_B1) * g
    v2 = ADAM_B2 * v + (1.0 - ADAM_B2) * (g * g)
    m_hat = m2 / (1.0 - ADAM_B1 ** ADAM_STEP)
    v_hat = v2 / (1.0 - ADAM_B2 ** ADAM_STEP)
    return -ADAM_LR * (m_hat / (jnp.sqrt(v_hat) + ADAM_EPS) + ADAM_WD * w), m2, v2


def _adamw_shard(parts, g_off, w, m, v, layer, prev, *, name):
    _, r, c = w.shape
    npart = parts.shape[0]
    tb = next(b for b in range(min(r, 512), 0, -16) if r % b == 0 and g_off % b == 0)

    def body(*refs):
        w_ref, m_ref, v_ref = refs[npart:npart + 3]
        g_out, d_out, m_out, v_out = refs[-4:]
        g = refs[0][...].astype(F32)
        for p_ref in refs[1:npart]:
            g = g + p_ref[...].astype(F32)
        d, m2, v2 = _adamw_math(w_ref[...], g, m_ref[...], v_ref[...])
        g_out[...] = g
        d_out[...] = d
        m_out[...] = m2
        v_out[...] = v2

    def part(ch):
        return pl.BlockSpec((None, tb, c), lambda i: (ch, g_off // tb + i, 0))

    blk = pl.BlockSpec((None, tb, c), lambda i: (layer, i, 0))
    prev = list(prev) if prev is not None else []
    return pl.pallas_call(
        body, name=name, grid=(r // tb,),
        in_specs=[part(ch) for ch in range(npart)] + [blk, blk, blk] + [pl.BlockSpec(memory_space=pl.ANY)] * len(prev),
        out_specs=[blk] * 4, out_shape=[jax.ShapeDtypeStruct(w.shape, F32)] * 4,
        input_output_aliases={npart + 3 + k: k for k in range(len(prev))},
        compiler_params=_cp())(*([parts] * npart), w, m, v, *prev)


SLOT = 8
SMALL_ROWS = 6 * SLOT
ROW_LB = 4 * SLOT


def _small_update(gath, w, m, v, *, name):
    def body(g_ref, w_ref, m_ref, v_ref, g_out, d_out, m_out, v_out):
        tot = g_ref[0]
        for k in range(1, 8):
            tot = tot + g_ref[k]
        wv = w_ref[...]
        c0, c1 = wv[ROW_LB:ROW_LB + 1, :], wv[ROW_LB + 1:ROW_LB + 2, :]
        mx = jnp.maximum(c0, c1)
        e0, e1 = jnp.exp(c0 - mx), jnp.exp(c1 - mx)
        lb = e1 / (e0 + e1)
        gl = tot[ROW_LB:ROW_LB + 1, :] * lb * (1.0 - lb)
        row = lax.broadcasted_iota(jnp.int32, tot.shape, 0)
        g = jnp.where(row == ROW_LB, -gl, jnp.where(row == ROW_LB + 1, gl, tot))
        d, m2, v2 = _adamw_math(wv, g, m_ref[...], v_ref[...])
        g_out[...] = g
        d_out[...] = d
        m_out[...] = m2
        v_out[...] = v2

    return pl.pallas_call(
        body, name=name, out_shape=[jax.ShapeDtypeStruct(w.shape, F32)] * 4, compiler_params=_cp())(gath, w, m, v)


D_MODEL = 1024


def _ffn_fwd(h, gain, wg, wu, wd, tag):
    xn, gg, uu, act = _norm_gate_up(h, gain, wg, wu, name=f"{tag}_gate_up")
    out = _mm([(act, wd)], residual=h, alpha=MACARON, tn=1024, name=f"{tag}_down")
    return out, (h, xn, gg, uu, act)


def _ffn_input_bwd(dg, du, wg, wu, x, gain, dres, chip_part, *, name, scale, tm=256):
    t, d = x.shape
    f = wg.shape[0]
    tm = min(tm, t)
    nt = t // tm
    fused = chip_part is not None

    def body(dg_ref, du_ref, wg_ref, wu_ref, x_ref, g_ref, dres_ref, *rest):
        if fused:
            part_ref, dx_ref, dxb_ref, dgain_ref, parts_ref = rest[:5]
            start, finish = _chip_exchange_phases(part_ref, parts_ref, *rest[5:])
            pl.when(pl.program_id(0) == 0)(start)
        else:
            dx_ref, dxb_ref, dgain_ref = rest
        dxn_v = _dot(dg_ref[...], wg_ref[...]) + _dot(du_ref[...], wu_ref[...])
        xv = x_ref[...]
        rstd = lax.rsqrt(jnp.mean(xv * xv, axis=-1, keepdims=True) + RMS_EPS)
        xhat = xv * rstd
        dxhat = dxn_v * g_ref[...]
        dx = dres_ref[...] + rstd * (dxhat - xhat * jnp.mean(dxhat * xhat, axis=-1, keepdims=True))
        dx_ref[...] = dx
        dxb_ref[...] = (dx * scale).astype(BF16)

        @pl.when(pl.program_id(0) == 0)
        def _():
            dgain_ref[...] = jnp.zeros_like(dgain_ref)

        dgain_ref[...] += jnp.sum(dxn_v * xhat, axis=0, keepdims=True)
        if fused:
            pl.when(pl.program_id(0) == nt - 1)(finish)

    wide = pl.BlockSpec((tm, f), lambda i: (i, 0))
    wsp = pl.BlockSpec((f, d), lambda i: (0, 0))
    row = pl.BlockSpec((tm, d), lambda i: (i, 0))
    vec = pl.BlockSpec((1, d), lambda i: (0, 0))
    args = [dg, du, wg, wu, x, gain, dres] + ([chip_part] if fused else [])
    return pl.pallas_call(
        body, name=name, grid=(nt,),
        in_specs=[wide, wide, wsp, wsp, row, vec, row] + ([HBM_SPEC] if fused else []),
        out_specs=[row, row, vec] + ([HBM_SPEC] if fused else []),
        out_shape=[jax.ShapeDtypeStruct((t, d), F32), jax.ShapeDtypeStruct((t, d), BF16), jax.ShapeDtypeStruct((1, d), F32)]
        + ([jax.ShapeDtypeStruct(chip_part.shape, chip_part.dtype)] if fused else []),
        scratch_shapes=_chip_exchange_scratch() if fused else [],
        compiler_params=_cp(dimension_semantics=("arbitrary",)))(*args)


def _ffn_bwd(dout, dout_half, saved, gain, wg, wu, wd, tag, next_scale, exchanges=None):
    h, xn, gg, uu, act = saved
    early_chip_part, send_after_dwd, chip_part_after_dwgu = exchanges if exchanges is not None else (None, None, None)
    dg, du, *early_parts = _swiglu_bwd(dout_half, wd, gg, uu, early_chip_part, tm=256, tf=wd.shape[0],
                                       name=f"{tag}_dact")
    dwd = _mm([(act, dout_half)], ta=True, tm=256, tn=1024, out_dtype=BF16, name=f"{tag}_dwd")
    send = send_after_dwd(dwd) if exchanges is not None else None
    dwg, dwu, *mid_parts = _mm_shared_rhs([dg, du], xn, tm=256, send=send, name=f"{tag}_dwgu")
    chip_part = chip_part_after_dwgu(dwg, dwu) if exchanges is not None else None
    dh, dh_b, dgain, *parts = _ffn_input_bwd(dg, du, wg, wu, h, gain, dout, chip_part, scale=next_scale,
                                             name=f"{tag}_input_bwd")
    return dh, dh_b, dwg, dwu, dwd, dgain, (early_parts + mid_parts + parts)


def kernel(x, ffn_pre_norm, ffn_pre_w_gate, ffn_pre_w_up, ffn_pre_w_down, mix_norm, ffn_post_norm, ffn_post_w_gate, ffn_post_w_up, ffn_post_w_down, ab_w_in, ab_conv_w, ab_w_out, c_w_in, c_lower_bounds, c_out_norm, c_w_out, final_norm, loss_target, m_ffn_pre_norm, m_ffn_pre_w_gate, m_ffn_pre_w_up, m_ffn_pre_w_down, m_mix_norm, m_ffn_post_norm, m_ffn_post_w_gate, m_ffn_post_w_up, m_ffn_post_w_down, m_ab_w_in, m_ab_conv_w, m_ab_w_out, m_c_w_in, m_c_lower_bounds, m_c_out_norm, m_c_w_out, m_final_norm, v_ffn_pre_norm, v_ffn_pre_w_gate, v_ffn_pre_w_up, v_ffn_pre_w_down, v_mix_norm, v_ffn_post_norm, v_ffn_post_w_gate, v_ffn_post_w_up, v_ffn_post_w_down, v_ab_w_in, v_ab_conv_w, v_ab_w_out, v_c_w_in, v_c_lower_bounds, v_c_out_norm, v_c_w_out, v_final_norm):
    d = D_MODEL
    h0 = x[0]
    target = loss_target[0]
    core = lax.axis_index("c").astype(jnp.int32).reshape(1)

    big = [("pre_g", ffn_pre_w_gate, m_ffn_pre_w_gate, v_ffn_pre_w_gate),
           ("pre_u", ffn_pre_w_up, m_ffn_pre_w_up, v_ffn_pre_w_up),
           ("pre_d", ffn_pre_w_down, m_ffn_pre_w_down, v_ffn_pre_w_down),
           ("post_g", ffn_post_w_gate, m_ffn_post_w_gate, v_ffn_post_w_gate),
           ("post_u", ffn_post_w_up, m_ffn_post_w_up, v_ffn_post_w_up),
           ("post_d", ffn_post_w_down, m_ffn_post_w_down, v_ffn_post_w_down),
           ("ab_in", ab_w_in, m_ab_w_in, v_ab_w_in),
           ("ab_out", ab_w_out, m_ab_w_out, v_ab_w_out),
           ("c_in", c_w_in, m_c_w_in, v_c_w_in),
           ("c_out", c_w_out, m_c_w_out, v_c_w_out)]
    by_tag = {tag: (w, m, v) for tag, w, m, v in big}

    def layer_rows(tag):
        w = by_tag[tag][0]
        return w.size // d // w.shape[0]

    def layout(items):
        offs, off = {}, 0
        for item in items:
            offs[item] = off
            off += layer_rows(item[0])
        return offs, off

    ffn = [f"{pos}_{kind}" for pos in ("pre", "post") for kind in "gud"]
    first_items = [("pre_g", 0), ("pre_u", 0)]
    early_items = [("pre_d", 0), ("ab_in", 0)]
    late_items = ([("pre_g", 1), ("pre_u", 1), ("pre_d", 1)] + [(f"post_{kind}", l) for l in (0, 1) for kind in "gud"]
                  + [("ab_out", 0), ("c_in", 0), ("c_out", 0)])
    grad_items = {"A0": [(f"post_{kind}", 1) for kind in "gud"] + [("c_out", 0)],
                  "A1": ([(f"pre_{kind}", 1) for kind in "gud"] + [(f"post_{kind}", 0) for kind in "gud"]
                         + [("c_in", 0), ("ab_out", 0)]),
                  "C": [("ab_in", 0)], "B0": [("pre_d", 0)], "B1": [("pre_g", 0), ("pre_u", 0)]}
    grad_offs = {k: layout(items)[0] for k, items in grad_items.items()}
    grad_conv_row = layout(grad_items["C"])[1]

    def conv_rows(a, split):
        flat = a.reshape(-1)
        if split:
            hi = flat.astype(BF16)
            flat = jnp.concatenate([hi, (flat - hi.astype(F32)).astype(BF16)])
        return jnp.zeros((16, d), flat.dtype).at[0, :flat.shape[0]].set(flat)

    nconv = ab_conv_w.size
    col_sharded = {"pre_g", "pre_u", "post_g", "post_u", "ab_in", "c_in"}

    def pack_rows(item):
        tag, layer = item
        a = by_tag[tag][0][layer]
        return (a.T if tag in col_sharded else a).reshape(-1, d).astype(BF16)

    first_pack = jnp.concatenate([pack_rows(item) for item in first_items], axis=0)
    early_pack = jnp.concatenate([pack_rows(item) for item in early_items] + [conv_rows(ab_conv_w, True)], axis=0)
    late_pack = jnp.concatenate([pack_rows(item) for item in late_items], axis=0)
    first_w = _all_gather(first_pack, [layer_rows(tag) for tag, _ in first_items], name="gather_first_weights")
    full = {item: g.reshape(-1, d) for item, g in zip(first_items, first_w)}

    xn0, gg0, uu0, act0, *early_w = _norm_gate_up(
        h0, ffn_pre_norm[0:1], full["pre_g", 0], full["pre_u", 0], name="l0pre_gate_up_gather_early_weights",
        pack=early_pack, seg_rows=[layer_rows(tag) for tag, _ in early_items] + [16])
    full.update({item: g.reshape(-1, d) for item, g in zip(early_items, early_w)})
    ffn_w = {("pre", 0): tuple(full[f"pre_{kind}", 0] for kind in "gud")}
    w_ab_in = full["ab_in", 0]
    cg = early_w[-1][:, 0, :2 * nconv].astype(F32)
    conv_w = (cg[:, :nconv] + cg[:, nconv:]).reshape(8, 3, -1).transpose(1, 0, 2).reshape(3, -1)
    aw = w_ab_in.shape[0] // 6
    h1 = _mm([(act0, full["pre_d", 0])], residual=h0, alpha=MACARON, tn=1024, name="l0pre_down")
    s_pre0 = (h0, xn0, gg0, uu0, act0)
    hn0, pa, pb = _norm_proj(h1, mix_norm[0:1], w_ab_in, (F32, BF16), tm=512, name="ab_norm_proj")
    ya = _conv_fwd(pa, conv_w, name="conv_fwd")
    yb, ltot, *late_w = _attn_fwd(pb, late_pack, [layer_rows(tag) for tag, _ in late_items],
                                  name="attn_fwd_gather_late_weights")
    full.update({item: g.reshape(-1, d) for item, g in zip(late_items, late_w)})
    for pos, layer in (("post", 0), ("pre", 1), ("post", 1)):
        ffn_w[pos, layer] = tuple(full[f"{pos}_{kind}", layer] for kind in "gud")
    w_ab_out, w_c_in, w_c_out = full["ab_out", 0], full["c_in", 0], full["c_out", 0]
    h2 = _mm([(ya, w_ab_out[:aw]), (yb, w_ab_out[aw:])], residual=h1, tn=1024, name="ab_out")
    h3, s_post0 = _ffn_fwd(h2, ffn_post_norm[0:1], *ffn_w["post", 0], "l0post")
    h4, s_pre1 = _ffn_fwd(h3, ffn_pre_norm[1:2], *ffn_w["pre", 1], "l1pre")
    hn1, pc = _norm_proj(h4, mix_norm[1:2], w_c_in, (F32,), tm=256, name="c_norm_proj")
    yc, o_saved, states = _hgrn_fwd(pc, c_lower_bounds, c_out_norm, name="hgrn_fwd")
    h5 = _mm([(yc, w_c_out)], residual=h4, tn=1024, name="c_out")
    h6, s_post1 = _ffn_fwd(h5, ffn_post_norm[1:2], *ffn_w["post", 1], "l1post")
    dh6, dh6_b, d_final, loss_vec = _loss_head(h6, final_norm.reshape(1, d), target, name="loss_head")

    gw = {}

    def grad_send(key, extra=()):
        gpack = jnp.concatenate([gw[item].reshape(8, -1, d) for item in grad_items[key]] + list(extra), axis=1)
        return gpack.reshape(4, 2, gpack.shape[1], d)

    def chip_partials(key, extra=()):
        send = grad_send(key, extra)
        from_sibling = _sibling_exchange(send, name=f"grad{key}_sibling_exchange")
        return _pair_add(send, from_sibling, core, name=f"grad{key}_pair_add")

    dh5, dh5_b, gw["post_g", 1], gw["post_u", 1], gw["post_d", 1], d_post1, *_ = _ffn_bwd(
        dh6, dh6_b, s_post1, ffn_post_norm[1:2], *ffn_w["post", 1], "l1post", 1.0)
    dyc = _mm([(dh5_b, w_c_out)], tb=True, tn=1024, name="c_out_dy")
    g_c_out = _mm([(yc, dh5_b)], ta=True, tm=256, tn=1024, out_dtype=BF16, name="c_out_dw")
    gw["c_out", 0] = g_c_out
    dcq, dcf, dci, dcg, dlb, d_onorm, parts_a0 = _hgrn_bwd(pc, o_saved, states, dyc, c_lower_bounds, c_out_norm,
                                                           grad_send("A0"), name="hgrn_bwd_exchange_grads_a0")
    dparts = [dcq, dcf, dci, dcg]
    g_c_in = jnp.concatenate(_mm_shared_rhs(dparts, hn1, tm=256, name="c_in_dw"), axis=0)
    cw = w_c_in.shape[0] // 4
    dhn1 = _mm([(dp, w_c_in[i * cw:(i + 1) * cw]) for i, dp in enumerate(dparts)], tm=512, tn=1024, name="c_in_dx")
    dh4, dh4_b, d_mix1 = _rmsnorm_bwd(h4, mix_norm[1:2], dhn1, dh5, scale=MACARON, name="l1_mix_norm_bwd")
    dh3, dh3_b, gw["pre_g", 1], gw["pre_u", 1], gw["pre_d", 1], d_pre1, *_ = _ffn_bwd(
        dh4, dh4_b, s_pre1, ffn_pre_norm[1:2], *ffn_w["pre", 1], "l1pre", MACARON)
    dh2, dh2_b, gw["post_g", 0], gw["post_u", 0], gw["post_d", 0], d_post0, *_ = _ffn_bwd(
        dh3, dh3_b, s_post0, ffn_post_norm[0:1], *ffn_w["post", 0], "l0post", 1.0)
    dyab = _mm([(dh2_b, w_ab_out)], tb=True, tn=1024, name="ab_out_dy")
    g_ab_out = jnp.concatenate(_mm_shared_rhs([ya, yb], dh2_b, tm=256, name="ab_out_dw"), axis=0)
    dab, dac, dax, g_conv = _conv_bwd(pa, dyab, conv_w, name="conv_bwd")

    gw["c_in", 0], gw["ab_out", 0] = g_c_in, g_ab_out
    dq, dk, dv, parts_a1 = _attn_bwd(pb, dyab, ltot, grad_send("A1"), name="attn_bwd_exchange_grads_a1")
    dparts = [dab, dac, dax, dq, dk, dv]
    g_ab_in = jnp.concatenate(_mm_shared_rhs(dparts, hn0, tm=128, name="ab_in_dw"), axis=0)
    dhn0 = _mm([(dp, w_ab_in[i * aw:(i + 1) * aw]) for i, dp in enumerate(dparts)], tm=512, tn=1024, name="ab_in_dx")
    dh1, dh1_b, d_mix0 = _rmsnorm_bwd(h1, mix_norm[0:1], dhn0, dh2, scale=MACARON, name="l0_mix_norm_bwd")
    gw["ab_in", 0] = g_ab_in
    gconv_own = g_conv.reshape(3, 8, -1).transpose(1, 0, 2).reshape(8, -1)
    conv_piece = jnp.zeros((8, 16, d), F32).at[:, 0, :nconv].set(gconv_own).astype(BF16)

    def send_b0(dwd):
        gw["pre_d", 0] = dwd
        return grad_send("B0")

    def chip_part_b1(dwg, dwu):
        gw["pre_g", 0], gw["pre_u", 0] = dwg, dwu
        return chip_partials("B1")

    dh0, _, _, _, _, d_pre0, (parts_c, parts_b0, parts_b1) = _ffn_bwd(
        dh1, dh1_b, s_pre0, ffn_pre_norm[0:1], *ffn_w["pre", 0], "l0pre", 1.0,
        (chip_partials("C", [conv_piece]), send_b0, chip_part_b1))

    parts = {"A0": parts_a0, "A1": parts_a1, "B0": parts_b0, "B1": parts_b1, "C": parts_c}
    upd = {}
    for tag, w, m, v in big:
        view = (lambda a: jnp.swapaxes(a, 1, 2)) if tag in col_sharded else (lambda a: a)
        where = {layer: (key, grad_offs[key][tag, layer])
                 for key in grad_items for t2, layer in grad_items[key] if t2 == tag}
        res = None
        for layer in sorted(where):
            key, off = where[layer]
            res = _adamw_shard(parts[key], off, view(w), view(m), view(v), layer, res, name=f"adamw_{tag}{layer}")
        upd[tag] = [view(a) for a in res]
    res = _adamw_shard(parts["C"], grad_conv_row, *(conv_rows(a, False)[None] for a in (ab_conv_w, m_ab_conv_w, v_ab_conv_w)),
                       0, None, name="adamw_conv")
    upd["conv"] = [r[0, 0, :nconv].reshape(ab_conv_w.shape) for r in res]

    def small_pack(pre, mix, post, final, lbs, onorm):
        def slot(parts):
            out, r = jnp.zeros((SLOT, d), F32), 0
            for a in (parts if isinstance(parts, tuple) else (parts,)):
                out = out.at[r:r + a.shape[0], :a.shape[1]].set(a)
                r += a.shape[0]
            return out

        return jnp.concatenate([slot(pre), slot(mix), slot(post), slot(final.reshape(1, d)), slot(lbs), slot(onorm)], axis=0)

    d_on = d_onorm.reshape(-1, c_out_norm.shape[1]).sum(axis=0, keepdims=True)
    gsmall = small_pack((d_pre0, d_pre1), (d_mix0, d_mix1), (d_post0, d_post1), d_final, dlb, d_on)
    gsmall_all = _all_gather(gsmall, name="gather_small_grads")
    sres = _small_update(
        gsmall_all,
        small_pack(ffn_pre_norm, mix_norm, ffn_post_norm, final_norm, c_lower_bounds, c_out_norm),
        small_pack(m_ffn_pre_norm, m_mix_norm, m_ffn_post_norm, m_final_norm, m_c_lower_bounds, m_c_out_norm),
        small_pack(v_ffn_pre_norm, v_mix_norm, v_ffn_post_norm, v_final_norm, v_c_lower_bounds, v_c_out_norm),
        name="small_update")

    def small_out(r):
        return {"pre_norm": r[0:2], "mix_norm": r[SLOT:SLOT + 2], "post_norm": r[2 * SLOT:2 * SLOT + 2],
                "final": r[3 * SLOT], "lb": r[ROW_LB:ROW_LB + 2], "onorm": r[5 * SLOT:5 * SLOT + 1, :c_out_norm.shape[1]]}

    small = [small_out(r) for r in sres]
    outs = []
    for k in range(4):
        s = small[k]
        outs += [s["pre_norm"], upd["pre_g"][k], upd["pre_u"][k], upd["pre_d"][k], s["mix_norm"], s["post_norm"],
                 upd["post_g"][k], upd["post_u"][k], upd["post_d"][k], upd["ab_in"][k], upd["conv"][k],
                 upd["ab_out"][k], upd["c_in"][k], s["lb"], s["onorm"], upd["c_out"][k], s["final"]]
    loss = lax.psum(loss_vec[0, 0], ("x", "y", "c"))
    return (loss, dh0[None], *outs)
```

```python
import math

import jax
import jax.numpy as jnp
from jax import lax
from jax.experimental import pallas as pl
from jax.experimental.pallas import tpu as pltpu

F32 = jnp.float32
BF16 = jnp.bfloat16
MESH = pl.DeviceIdType.MESH

RMS_EPS = 1e-6
MACARON = 0.5
LANES = 128
CHUNK = 64
N_LEVELS = 6
HGRN_HEADS = 2
SB_KEYS = 256
ADAM_LR, ADAM_B1, ADAM_B2, ADAM_EPS, ADAM_WD, ADAM_STEP = 0.001, 0.9, 0.999, 1e-08, 0.01, 10
VMEM_LIMIT = 48 * 1024 * 1024


def _cp(**kw):
    return pltpu.CompilerParams(vmem_limit_bytes=VMEM_LIMIT, **kw)


def _sigmoid(x):
    return 0.5 * jnp.tanh(0.5 * x) + 0.5


def _bf(x):
    return x if x.dtype == BF16 else x.astype(BF16)


def _split3(x):
    hi = x.astype(BF16)
    r1 = x - hi.astype(F32)
    mid = r1.astype(BF16)
    lo = (r1 - mid.astype(F32)).astype(BF16)
    return hi, mid, lo


def _dot(a, b, ca=1, cb=0):
    return lax.dot_general(a, b, (((ca,), (cb,)), ((), ())), preferred_element_type=F32)


def _dot_exact_lhs(m, x):
    hi, mid, lo = _split3(x)
    return _dot(m, hi) + _dot(m, mid) + _dot(m, lo)


def _mm(terms, *, name, ta=False, tb=False, out_dtype=F32, residual=None, alpha=1.0, tm=512, tn=512):
    nt = len(terms)
    a0, b0 = terms[0]
    m = a0.shape[1] if ta else a0.shape[0]
    n = b0.shape[0] if tb else b0.shape[1]
    tm, tn = min(tm, m), min(tn, n)
    assert m % tm == 0 and n % tn == 0, (name, m, n, tm, tn)
    has_res = residual is not None

    def body(*refs):
        o_ref = refs[-1]
        acc = None
        for i in range(nt):
            a = _bf(refs[2 * i][...])
            b = _bf(refs[2 * i + 1][...])
            p = _dot(a, b, 0 if ta else 1, 1 if tb else 0)
            acc = p if acc is None else acc + p
        if alpha != 1.0:
            acc = acc * alpha
        if has_res:
            acc = acc + refs[2 * nt][...]
        o_ref[...] = acc.astype(out_dtype)

    in_specs, args = [], []
    for a, b in terms:
        k = a.shape[0] if ta else a.shape[1]
        assert (b.shape[1] if tb else b.shape[0]) == k, (name, a.shape, b.shape)
        in_specs.append(pl.BlockSpec((k, tm), lambda i, j: (0, i)) if ta else pl.BlockSpec((tm, k), lambda i, j: (i, 0)))
        in_specs.append(pl.BlockSpec((tn, k), lambda i, j: (j, 0)) if tb else pl.BlockSpec((k, tn), lambda i, j: (0, j)))
        args += [a, b]
    if has_res:
        in_specs.append(pl.BlockSpec((tm, tn), lambda i, j: (i, j)))
        args.append(residual)
    return pl.pallas_call(
        body, name=name, grid=(m // tm, n // tn), in_specs=in_specs,
        out_specs=pl.BlockSpec((tm, tn), lambda i, j: (i, j)),
        out_shape=jax.ShapeDtypeStruct((m, n), out_dtype), compiler_params=_cp())(*args)


def _norm_proj(x, gain, w_t, out_dtypes, *, name, tm):
    t, d = x.shape
    n = w_t.shape[0]
    tm = min(tm, t)
    npart = len(out_dtypes)
    width = n // npart

    def body(x_ref, g_ref, w_ref, xn_ref, *part_refs):
        xv = x_ref[...]
        rstd = lax.rsqrt(jnp.mean(xv * xv, axis=-1, keepdims=True) + RMS_EPS)
        xn = (xv * rstd * g_ref[...]).astype(BF16)
        xn_ref[...] = xn
        for p, ref in enumerate(part_refs):
            ref[...] = _dot(xn, w_ref[p * width:(p + 1) * width, :], 1, 1).astype(out_dtypes[p])

    row = pl.BlockSpec((tm, d), lambda i: (i, 0))
    return pl.pallas_call(
        body, name=name, grid=(t // tm,),
        in_specs=[row, pl.BlockSpec((1, d), lambda i: (0, 0)), pl.BlockSpec((n, d), lambda i: (0, 0))],
        out_specs=[row] + [pl.BlockSpec((tm, width), lambda i: (i, 0))] * npart,
        out_shape=[jax.ShapeDtypeStruct((t, d), BF16)] + [jax.ShapeDtypeStruct((t, width), dt) for dt in out_dtypes],
        compiler_params=_cp())(x, gain, w_t)


def _mm_shared_rhs(a_list, b, *, name, tm, out_dtype=BF16, send=None):
    k, n = b.shape
    assert all(a.shape[0] == k and a.shape[1] % tm == 0 and a.shape[1] == a_list[0].shape[1] for a in a_list)
    m = a_list[0].shape[1]
    na = len(a_list)
    nsteps = m // tm
    sends = list(send) if send is not None else []
    ns = len(sends)

    def body(*refs):
        first_out = na + 1 + ns
        if ns:
            start, finish = _direct_exchange_phases(refs[na + 1:first_out], refs[first_out + na], *refs[first_out + na + 1:])
            pl.when(pl.program_id(0) == 0)(start)
        bv = refs[na][...]
        for i in range(na):
            refs[first_out + i][...] = _dot(refs[i][...], bv, 0, 0).astype(out_dtype)
        if ns:
            pl.when(pl.program_id(0) == nsteps - 1)(finish)

    return pl.pallas_call(
        body, name=name, grid=(nsteps,),
        in_specs=[pl.BlockSpec((k, tm), lambda i: (0, i))] * na + [pl.BlockSpec((k, n), lambda i: (0, 0))] + [HBM_SPEC] * ns,
        out_specs=[pl.BlockSpec((tm, n), lambda i: (i, 0))] * na + ([HBM_SPEC] if ns else []),
        out_shape=[jax.ShapeDtypeStruct((m, n), out_dtype)] * na
        + ([jax.ShapeDtypeStruct((8, _direct_exchange_rows(sends), sends[0].shape[3]), sends[0].dtype)] if ns else []),
        scratch_shapes=_direct_exchange_scratch() if ns else [],
        compiler_params=_cp(dimension_semantics=("arbitrary",)))(*a_list, b, *sends)


def _rmsnorm_bwd(x, gain, dxn, dres, *, name, scale, tm=512):
    t, d = x.shape
    tm = min(tm, t)

    def body(x_ref, g_ref, dxn_ref, dres_ref, dx_ref, dxb_ref, dg_ref):
        xv = x_ref[...]
        rstd = lax.rsqrt(jnp.mean(xv * xv, axis=-1, keepdims=True) + RMS_EPS)
        xhat = xv * rstd
        dxn_v = dxn_ref[...]
        dxhat = dxn_v * g_ref[...]
        dx = dres_ref[...] + rstd * (dxhat - xhat * jnp.mean(dxhat * xhat, axis=-1, keepdims=True))
        dx_ref[...] = dx
        dxb_ref[...] = (dx * scale).astype(BF16)

        @pl.when(pl.program_id(0) == 0)
        def _():
            dg_ref[...] = jnp.zeros_like(dg_ref)

        dg_ref[...] += jnp.sum(dxn_v * xhat, axis=0, keepdims=True)

    row = pl.BlockSpec((tm, d), lambda i: (i, 0))
    vec = pl.BlockSpec((1, d), lambda i: (0, 0))
    return pl.pallas_call(
        body, name=name, grid=(t // tm,), in_specs=[row, vec, row, row], out_specs=[row, row, vec],
        out_shape=[jax.ShapeDtypeStruct((t, d), F32), jax.ShapeDtypeStruct((t, d), BF16), jax.ShapeDtypeStruct((1, d), F32)],
        compiler_params=_cp())(x, gain, dxn, dres)


def _loss_head(h, gain, target, *, name, tm=512):
    t, d = h.shape
    tm = min(tm, t)

    def body(h_ref, g_ref, t_ref, dh_ref, dhb_ref, dg_ref, loss_ref):
        hv = h_ref[...]
        rstd = lax.rsqrt(jnp.mean(hv * hv, axis=-1, keepdims=True) + RMS_EPS)
        xhat = hv * rstd
        err = xhat * g_ref[...] - t_ref[...]
        dy = err * (1.0 / d)
        dxhat = dy * g_ref[...]
        dh = rstd * (dxhat - xhat * jnp.mean(dxhat * xhat, axis=-1, keepdims=True))
        dh_ref[...] = dh
        dhb_ref[...] = (dh * MACARON).astype(BF16)

        @pl.when(pl.program_id(0) == 0)
        def _():
            dg_ref[...] = jnp.zeros_like(dg_ref)
            loss_ref[...] = jnp.zeros_like(loss_ref)

        dg_ref[...] += jnp.sum(dy * xhat, axis=0, keepdims=True)
        part = jnp.sum(jnp.sum(err * err, axis=-1, keepdims=True), axis=0, keepdims=True) * (0.5 / d)
        loss_ref[...] += jnp.broadcast_to(part, loss_ref.shape)

    row = pl.BlockSpec((tm, d), lambda i: (i, 0))
    vec = pl.BlockSpec((1, d), lambda i: (0, 0))
    return pl.pallas_call(
        body, name=name, grid=(t // tm,), in_specs=[row, vec, row],
        out_specs=[row, row, vec, pl.BlockSpec((1, LANES), lambda i: (0, 0))],
        out_shape=[jax.ShapeDtypeStruct((t, d), F32), jax.ShapeDtypeStruct((t, d), BF16), jax.ShapeDtypeStruct((1, d), F32),
                   jax.ShapeDtypeStruct((1, LANES), F32)],
        compiler_params=_cp())(h, gain, target)


def _norm_gate_up(x, gain, wg, wu, *, name, tm=256, tf=2816, pack=None, seg_rows=()):
    t, d = x.shape
    f = wg.shape[0]
    tm, tf = min(tm, t), min(tf, f)
    assert f % tf == 0
    ni, nj = t // tm, f // tf
    nseg = len(seg_rows)

    def body(x_ref, g_ref, wg_ref, wu_ref, *rest):
        if pack is not None:
            pack_ref, xn_ref, gg_ref, uu_ref, act_ref = rest[:5]
            start, forward, finish = _gather_phases(pack_ref, rest[5:5 + nseg], seg_rows, *rest[5 + nseg:])
            step = pl.program_id(0) * nj + pl.program_id(1)
            pl.when(step == 0)(start)
            pl.when(step == (3 * ni * nj) // 4)(forward)
        else:
            xn_ref, gg_ref, uu_ref, act_ref = rest

        @pl.when(pl.program_id(1) == 0)
        def _():
            xv = x_ref[...]
            rstd = lax.rsqrt(jnp.mean(xv * xv, axis=-1, keepdims=True) + RMS_EPS)
            xn_ref[...] = (xv * rstd * g_ref[...]).astype(BF16)

        xn = xn_ref[...]
        gv = _dot(xn, wg_ref[...], 1, 1)
        uv = _dot(xn, wu_ref[...], 1, 1)
        gg_ref[...] = gv.astype(BF16)
        uu_ref[...] = uv.astype(BF16)
        act_ref[...] = (gv * _sigmoid(gv) * uv).astype(BF16)
        if pack is not None:
            pl.when(step == ni * nj - 1)(finish)

    row = pl.BlockSpec((tm, d), lambda i, j: (i, 0))
    wsp = pl.BlockSpec((tf, d), lambda i, j: (j, 0))
    osp = pl.BlockSpec((tm, tf), lambda i, j: (i, j))
    fused = pack is not None
    return pl.pallas_call(
        body, name=name, grid=(ni, nj),
        in_specs=[row, pl.BlockSpec((1, d), lambda i, j: (0, 0)), wsp, wsp] + ([HBM_SPEC] if fused else []),
        out_specs=[row, osp, osp, osp] + [HBM_SPEC] * nseg,
        out_shape=[jax.ShapeDtypeStruct((t, d), BF16)] + [jax.ShapeDtypeStruct((t, f), BF16)] * 3
        + [jax.ShapeDtypeStruct((8, n, d), BF16) for n in seg_rows],
        scratch_shapes=_gather_scratch() if fused else [],
        compiler_params=_cp(dimension_semantics=("arbitrary", "arbitrary")))(x, gain, wg, wu, *([pack] if fused else []))


def _swiglu_bwd(dout, wd, gg, uu, chip_part=None, *, name, tm=512, tf=1408):
    t, d = dout.shape
    f = wd.shape[0]
    tm, tf = min(tm, t), min(tf, f)
    nj, ni = f // tf, t // tm
    fused = chip_part is not None

    def body(do_ref, wd_ref, g_ref, u_ref, *rest):
        if fused:
            part_ref, dg_ref, du_ref, parts_ref = rest[:4]
            start, finish = _chip_exchange_phases(part_ref, parts_ref, *rest[4:])
            step = pl.program_id(0) * ni + pl.program_id(1)
            pl.when(step == 0)(start)
        else:
            dg_ref, du_ref = rest
        dact = _dot(do_ref[...], wd_ref[...], 1, 1)
        gv = g_ref[...].astype(F32)
        uv = u_ref[...].astype(F32)
        sg = _sigmoid(gv)
        dg_ref[...] = (dact * uv * (sg * (1.0 + gv * (1.0 - sg)))).astype(BF16)
        du_ref[...] = (dact * (gv * sg)).astype(BF16)
        if fused:
            pl.when(step == nj * ni - 1)(finish)

    osp = pl.BlockSpec((tm, tf), lambda j, i: (i, j))
    return pl.pallas_call(
        body, name=name, grid=(nj, ni),
        in_specs=[pl.BlockSpec((tm, d), lambda j, i: (i, 0)), pl.BlockSpec((tf, d), lambda j, i: (j, 0)), osp, osp]
        + ([HBM_SPEC] if fused else []),
        out_specs=[osp, osp] + ([HBM_SPEC] if fused else []),
        out_shape=[jax.ShapeDtypeStruct((t, f), BF16)] * 2
        + ([jax.ShapeDtypeStruct(chip_part.shape, chip_part.dtype)] if fused else []),
        scratch_shapes=_chip_exchange_scratch() if fused else [],
        compiler_params=_cp(dimension_semantics=("arbitrary", "arbitrary")))(dout, wd, gg, uu, *([chip_part] if fused else []))


def _shift_down(x, n):
    rows = lax.broadcasted_iota(jnp.int32, x.shape, 0)
    return jnp.where(rows >= n, pltpu.roll(x, n, 0), 0.0)


def _shift_up(x, n):
    t = x.shape[0]
    rows = lax.broadcasted_iota(jnp.int32, x.shape, 0)
    return jnp.where(rows < t - n, pltpu.roll(x, t - n, 0), 0.0)


def _conv_fwd(pa, conv_w, *, name):
    t = pa.shape[0]
    nb = pa.shape[1] // 3 // LANES

    def body(b_ref, c_ref, x_ref, w_ref, y_ref):
        u = c_ref[...] * x_ref[...]
        w = w_ref[...]
        conv = w[2:3, :] * u + w[1:2, :] * _shift_down(u, 1) + w[0:1, :] * _shift_down(u, 2)
        y_ref[...] = (b_ref[...] * conv).astype(BF16)

    def col(off):
        return pl.BlockSpec((t, LANES), lambda j: (0, off + j))

    return pl.pallas_call(
        body, name=name, grid=(nb,),
        in_specs=[col(0), col(nb), col(2 * nb), pl.BlockSpec((3, LANES), lambda j: (0, j))],
        out_specs=pl.BlockSpec((t, LANES), lambda j: (0, j)),
        out_shape=jax.ShapeDtypeStruct((t, nb * LANES), BF16), compiler_params=_cp())(pa, pa, pa, conv_w)


def _conv_bwd(pa, dy, conv_w, *, name):
    t = pa.shape[0]
    nb = pa.shape[1] // 3 // LANES

    def body(b_ref, c_ref, x_ref, dy_ref, w_ref, db_ref, dc_ref, dx_ref, dw_ref):
        cv, xv = c_ref[...], x_ref[...]
        u = cv * xv
        u1, u2 = _shift_down(u, 1), _shift_down(u, 2)
        w = w_ref[...]
        conv = w[2:3, :] * u + w[1:2, :] * u1 + w[0:1, :] * u2
        dyv = dy_ref[...]
        db_ref[...] = (dyv * conv).astype(BF16)
        dconv = dyv * b_ref[...]
        du = w[2:3, :] * dconv + w[1:2, :] * _shift_up(dconv, 1) + w[0:1, :] * _shift_up(dconv, 2)
        dc_ref[...] = (du * xv).astype(BF16)
        dx_ref[...] = (du * cv).astype(BF16)
        dw_ref[0:1, :] = jnp.sum(dconv * u2, axis=0, keepdims=True)
        dw_ref[1:2, :] = jnp.sum(dconv * u1, axis=0, keepdims=True)
        dw_ref[2:3, :] = jnp.sum(dconv * u, axis=0, keepdims=True)

    def col(off):
        return pl.BlockSpec((t, LANES), lambda j: (0, off + j))

    osp = pl.BlockSpec((t, LANES), lambda j: (0, j))
    wsp = pl.BlockSpec((3, LANES), lambda j: (0, j))
    return pl.pallas_call(
        body, name=name, grid=(nb,), in_specs=[col(0), col(nb), col(2 * nb), col(0), wsp],
        out_specs=[osp, osp, osp, wsp],
        out_shape=[jax.ShapeDtypeStruct((t, nb * LANES), BF16)] * 3 + [jax.ShapeDtypeStruct((3, nb * LANES), F32)],
        compiler_params=_cp())(pa, pa, pa, dy, conv_w)


def _sb_consts():
    j = lax.broadcasted_iota(jnp.int32, (SB_KEYS, SB_KEYS), 0)
    s = lax.broadcasted_iota(jnp.int32, (SB_KEYS, SB_KEYS), 1)
    after = (j > s).astype(BF16)
    upto = (j <= s).astype(BF16)
    before = (j < s).astype(BF16)
    return after, jnp.stack([upto, before])


def _log_sigmoid(z):
    return jnp.minimum(z, 0.0) - jnp.log(1.0 + jnp.exp(-jnp.abs(z)))


def _attn_fwd(pb, late_pack, seg_rows, *, name, tq=256):
    t = pb.shape[0]
    npair = pb.shape[1] // 3 // LANES
    tq = min(tq, t)
    nq = t // tq
    cmat, _ = _sb_consts()
    scale = 1.0 / math.sqrt(LANES // 2)

    nseg = len(seg_rows)

    def body(q_ref, k_ref, v_ref, c_ref, late_ref, y_ref, lt_ref, *rest):
        i = pl.program_id(1)
        pair = pl.program_id(0)
        scratch = rest[nseg:nseg + 4]
        start, forward, finish = _gather_phases(late_ref, rest[:nseg], seg_rows, *rest[nseg + 4:])
        pl.when((pair == 0) & (i == 0))(start)
        pl.when((pair == npair - 1) & (i == nq // 2))(forward)
        lane = lax.broadcasted_iota(jnp.int32, (tq, LANES), 1)
        rowpos = i * tq + lax.broadcasted_iota(jnp.int32, (tq, SB_KEYS), 0)
        colid = lax.broadcasted_iota(jnp.int32, (tq, SB_KEYS), 1)
        q2 = q_ref[...] * jnp.asarray(scale, BF16)
        cm = c_ref[...]
        hi_lanes = lane >= LANES // 2
        qhs = [jnp.where(hi_lanes == (hh == 1), q2, jnp.zeros_like(q2)) for hh in range(2)]
        per_q = tq // SB_KEYS

        def blk(jb):
            return pl.ds(pl.multiple_of(jb * SB_KEYS, SB_KEYS), SB_KEYS)

        zbuf, wbuf, accbuf, runbuf = scratch

        def scores(jb):
            kb = k_ref[blk(jb), :]
            for hh in range(2):
                zbuf[hh] = _dot(qhs[hh], kb, 1, 1)

        def values(jb):
            vb = v_ref[blk(jb), :]
            for hh in range(2):
                accbuf[hh] += _dot(wbuf[hh], vb)

        def trip(jb, masked, first=False):
            mask = (jb * SB_KEYS + colid) < rowpos if masked else None
            if not first:
                values(jb + 1)
            pre, css = [], []
            for hh in range(2):
                z = zbuf[hh]
                lb = _log_sigmoid(z)
                lk = lb - z
                if masked:
                    lk = jnp.where(mask, lk, 0.0)
                lk_hi, lk_lo = _split2(lk)
                css.append(_dot(lk_hi, cm) + _dot(lk_lo, cm))
                run = runbuf[hh]
                pre.append(lb + run)
                runbuf[hh] = run + jnp.sum(lk, axis=1, keepdims=True)
            scores(jnp.maximum(jb - 1, 0))
            for hh in range(2):
                w = jnp.exp(pre[hh] + css[hh])
                if masked:
                    w = jnp.where(mask, w, 0.0)
                wbuf[hh] = w.astype(BF16)

        nfull = i * per_q
        accbuf[...] = jnp.zeros_like(accbuf)
        runbuf[...] = jnp.zeros_like(runbuf)
        scores(nfull + per_q - 1)
        for dblk in reversed(range(per_q)):
            trip(nfull + dblk, True, first=dblk == per_q - 1)

        def full_block(n, carry):
            trip(nfull - 1 - n, False)
            return carry

        lax.fori_loop(0, nfull, full_block, 0)
        values(0)
        y_ref[...] = jnp.where(hi_lanes, accbuf[1], accbuf[0]).astype(BF16)
        lt_ref[...] = jnp.where(hi_lanes, runbuf[1], runbuf[0])
        pl.when((pair == npair - 1) & (i == nq - 1))(finish)

    return pl.pallas_call(
        body, name=name, grid=(npair, nq),
        in_specs=[pl.BlockSpec((tq, LANES), lambda p, i: (i, p)),
                  pl.BlockSpec((t, LANES), lambda p, i: (0, npair + p)),
                  pl.BlockSpec((t, LANES), lambda p, i: (0, 2 * npair + p)),
                  pl.BlockSpec((SB_KEYS, SB_KEYS), lambda p, i: (0, 0)),
                  HBM_SPEC],
        out_specs=[pl.BlockSpec((tq, LANES), lambda p, i: (i, p))] * 2 + [HBM_SPEC] * nseg,
        out_shape=[jax.ShapeDtypeStruct((t, npair * LANES), BF16), jax.ShapeDtypeStruct((t, npair * LANES), F32),
                   ] + [jax.ShapeDtypeStruct((8, n, late_pack.shape[1]), late_pack.dtype) for n in seg_rows],
        scratch_shapes=[pltpu.VMEM((2, tq, SB_KEYS), F32), pltpu.VMEM((2, tq, SB_KEYS), BF16),
                        pltpu.VMEM((2, tq, LANES), F32), pltpu.VMEM((2, tq, 1), F32)] + _gather_scratch(),
        compiler_params=_cp(dimension_semantics=("arbitrary", "arbitrary")))(pb, pb, pb, cmat, late_pack)


def _attn_bwd(pb, dy, ltot, send, *, name, tq=256):
    t = pb.shape[0]
    npair = pb.shape[1] // 3 // LANES
    tq = min(tq, t)
    nq = t // tq
    _, cmats = _sb_consts()
    scale = 1.0 / math.sqrt(LANES // 2)
    sends = list(send)
    ns = len(sends)

    def body(q_ref, k_ref, v_ref, dy_ref, lt_ref, c_ref, *rest):
        i = pl.program_id(1)
        pair = pl.program_id(0)
        send_refs = rest[:ns]
        dq_ref, dk_ref, dv_ref, parts_ref, dk_acc, dv_acc = rest[ns:ns + 6]
        scratch = rest[ns + 6:ns + 12]
        start, finish = _direct_exchange_phases(send_refs, parts_ref, *rest[ns + 12:])
        pl.when((pair == 0) & (i == 0))(start)

        @pl.when(i == 0)
        def _():
            dk_acc[...] = jnp.zeros_like(dk_acc)
            dv_acc[...] = jnp.zeros_like(dv_acc)

        lane = lax.broadcasted_iota(jnp.int32, (tq, LANES), 1)
        rowpos = i * tq + lax.broadcasted_iota(jnp.int32, (tq, SB_KEYS), 0)
        colid = lax.broadcasted_iota(jnp.int32, (tq, SB_KEYS), 1)
        q2 = q_ref[...] * jnp.asarray(scale, BF16)
        do2 = dy_ref[...].astype(BF16)
        ltv = lt_ref[...]
        c_upto, c_before = c_ref[0], c_ref[1]
        hi_lanes = lane >= LANES // 2
        sels = [hi_lanes == (hh == 1) for hh in range(2)]
        qhs = [jnp.where(s, q2, jnp.zeros_like(q2)) for s in sels]
        dohs = [jnp.where(s, do2, jnp.zeros_like(do2)) for s in sels]
        lts = [ltv[:, 0:1], ltv[:, LANES // 2:LANES // 2 + 1]]
        per_q = tq // SB_KEYS

        def blk(jb):
            return pl.ds(pl.multiple_of(jb * SB_KEYS, SB_KEYS), SB_KEYS)

        zbuf, dabuf, dzbuf, abuf, dqbuf, sumbuf = scratch

        def scores(jb):
            kb, vb = k_ref[blk(jb), :], v_ref[blk(jb), :]
            for hh in range(2):
                zbuf[hh] = _dot(qhs[hh], kb, 1, 1)
                dabuf[hh] = _dot(dohs[hh], vb, 1, 1)

        def products(jb):
            kb = k_ref[blk(jb), :]
            dk_acc[blk(jb), :] += _dot(dzbuf[0], qhs[0], 0, 0) + _dot(dzbuf[1], qhs[1], 0, 0)
            dv_acc[blk(jb), :] += _dot(abuf[0], dohs[0], 0, 0) + _dot(abuf[1], dohs[1], 0, 0)
            for hh in range(2):
                dqbuf[hh] += _dot(dzbuf[hh], kb)

        def trip(jb, masked):
            mask = (jb * SB_KEYS + colid) < rowpos if masked else None
            products(jnp.maximum(jb - 1, 0))
            lbs, css, es, ces = [], [], [], []
            for hh in range(2):
                z = zbuf[hh]
                lb = _log_sigmoid(z)
                lk = lb - z
                if masked:
                    lk = jnp.where(mask, lk, 0.0)
                lk_hi, lk_lo = _split2(lk)
                css.append(_dot(lk_hi, c_upto) + _dot(lk_lo, c_upto))
                csum = sumbuf[2 * hh]
                lbs.append((lb, lb + (lts[hh] - csum)))
                sumbuf[2 * hh] = csum + jnp.sum(lk, axis=1, keepdims=True)
            for hh in range(2):
                a = jnp.exp(lbs[hh][1] - css[hh])
                if masked:
                    a = jnp.where(mask, a, 0.0)
                e = a * dabuf[hh]
                e_hi, e_lo = _split2(e)
                ces.append(_dot(e_hi, c_before) + _dot(e_lo, c_before))
                abuf[hh] = a.astype(BF16)
                es.append(e)
            scores(jnp.minimum(jb + 1, last))
            for hh in range(2):
                prun = sumbuf[2 * hh + 1]
                beta = jnp.exp(lbs[hh][0])
                dz = es[hh] * (1.0 - beta) - (prun + ces[hh]) * beta
                if masked:
                    dz = jnp.where(mask, dz, 0.0)
                dzbuf[hh] = dz.astype(BF16)
                sumbuf[2 * hh + 1] = prun + jnp.sum(es[hh], axis=1, keepdims=True)

        nfull = i * per_q
        last = nfull + per_q - 1
        for buf in (dzbuf, abuf, dqbuf, sumbuf):
            buf[...] = jnp.zeros_like(buf)
        scores(0)

        def full_block(jb, carry):
            trip(jb, False)
            return carry

        lax.fori_loop(0, nfull, full_block, 0)
        for dblk in range(per_q):
            trip(nfull + dblk, True)
        products(last)
        dq_ref[...] = (jnp.where(hi_lanes, dqbuf[1], dqbuf[0]) * scale).astype(BF16)

        @pl.when(i == nq - 1)
        def _():
            dk_ref[...] = dk_acc[...].astype(BF16)
            dv_ref[...] = dv_acc[...].astype(BF16)

        pl.when((pair == npair - 1) & (i == nq - 1))(finish)

    blk = pl.BlockSpec((tq, LANES), lambda p, i: (i, p))
    full = pl.BlockSpec((t, LANES), lambda p, i: (0, p))
    return pl.pallas_call(
        body, name=name, grid=(npair, nq),
        in_specs=[blk,
                  pl.BlockSpec((t, LANES), lambda p, i: (0, npair + p)),
                  pl.BlockSpec((t, LANES), lambda p, i: (0, 2 * npair + p)),
                  pl.BlockSpec((tq, LANES), lambda p, i: (i, npair + p)),
                  blk,
                  pl.BlockSpec((2, SB_KEYS, SB_KEYS), lambda p, i: (0, 0, 0))] + [HBM_SPEC] * ns,
        out_specs=[blk, full, full, HBM_SPEC],
        out_shape=[jax.ShapeDtypeStruct((t, npair * LANES), BF16)] * 3
        + [jax.ShapeDtypeStruct((8, _direct_exchange_rows(sends), sends[0].shape[3]), sends[0].dtype)],
        scratch_shapes=[pltpu.VMEM((t, LANES), F32), pltpu.VMEM((t, LANES), F32),
                        pltpu.VMEM((2, tq, SB_KEYS), F32), pltpu.VMEM((2, tq, SB_KEYS), F32),
                        pltpu.VMEM((2, tq, SB_KEYS), BF16), pltpu.VMEM((2, tq, SB_KEYS), BF16),
                        pltpu.VMEM((2, tq, LANES), F32), pltpu.VMEM((4, tq, 1), F32)] + _direct_exchange_scratch(),
        compiler_params=_cp(dimension_semantics=("arbitrary", "arbitrary")))(pb, pb, pb, dy, ltot, cmats, *sends)


def _hgrn_consts():
    t = lax.broadcasted_iota(jnp.int32, (CHUNK, CHUNK), 0)
    s = lax.broadcasted_iota(jnp.int32, (CHUNK, CHUNK), 1)
    masks = []
    for lvl in range(N_LEVELS):
        half = CHUNK >> (lvl + 1)
        same = (t // (2 * half)) == (s // (2 * half))
        masks.append((same & (t % (2 * half) >= half) & (s % (2 * half) < half)).astype(F32))
    masks.append((t == s).astype(F32))
    prefix = (s <= t).astype(BF16)
    suffix = (s >= t).astype(BF16)
    return prefix, jnp.stack(masks), suffix


def _hgrn_gates(qr, fr, lbv):
    sg = _sigmoid(fr)
    fval = lbv + (1.0 - lbv) * sg
    kk = (1.0 - lbv) * _sigmoid(-fr)
    sq = _sigmoid(qr)
    return sg, fval, jnp.log(fval), kk, sq, qr * sq


def _lower_bound(c_ref):
    c = c_ref[...]
    mx = jnp.max(c, axis=0, keepdims=True)
    ex = jnp.exp(c - mx)
    return ex[1:2, :] / jnp.sum(ex, axis=0, keepdims=True)


def _level_ref(b, lvl):
    half = CHUNK >> (lvl + 1)
    seg = 2 * half
    if seg >= 8:
        b3 = b.reshape(CHUNK // seg, seg, LANES)
        return jnp.broadcast_to(b3[:, half - 1:half, :], b3.shape).reshape(CHUNK, LANES)
    pos = lax.broadcasted_iota(jnp.int32, b.shape, 0) % seg
    out = b
    for p in range(seg):
        if p != half - 1:
            out = jnp.where(pos == p, pltpu.roll(b, (p - (half - 1)) % CHUNK, 0), out)
    return out


def _hgrn_levels(b, qs, kk):
    out = []
    for lvl in range(N_LEVELS):
        fac = jnp.exp(-jnp.abs(b - _level_ref(b, lvl)))
        out.append((qs * fac, kk * fac, fac, fac))
    out.append((qs, kk, None, None))
    return out


def _split2(x):
    hi = x.astype(BF16)
    return hi, (x - hi.astype(F32)).astype(BF16)


def _hgrn_fwd(pc, c_lb, out_norm, *, name, tc=512):
    t = pc.shape[0]
    nh = pc.shape[1] // 4 // LANES
    tc = min(tc, t)
    nch = tc // CHUNK
    cum_all, masks, _ = _hgrn_consts()

    def body(q_ref, f_ref, i_ref, g_ref, lb_ref, on_ref, cum_ref, m_ref, y_ref, o_ref, st_ref, state):
        @pl.when(pl.program_id(1) == 0)
        def _():
            state[...] = jnp.zeros_like(state)

        lbv = _lower_bound(lb_ref)
        onv = on_ref[...]

        def chunk(c, carry):
            rows = pl.ds(pl.multiple_of(c * CHUNK, CHUNK), CHUNK)
            for hh in range(HGRN_HEADS):
                lanes = slice(hh * LANES, (hh + 1) * LANES)
                _, _, g, kk, _, qs = _hgrn_gates(q_ref[rows, lanes], f_ref[rows, lanes], lbv[:, lanes])
                vb = i_ref[rows, lanes].astype(BF16)
                b = _dot_exact_lhs(cum_ref[...], g)
                scores = jnp.zeros((CHUNK, CHUNK), F32)
                for lvl, (ql, kl, _, _) in enumerate(_hgrn_levels(b, qs, kk)):
                    scores = scores + _dot(ql.astype(BF16), kl.astype(BF16), 1, 1) * m_ref[lvl]
                st = state[hh]
                st_ref[hh, c] = st
                o = _dot(scores.astype(BF16), vb) + _dot((qs * jnp.exp(b)).astype(BF16), st.astype(BF16), 1, 1)
                blast = b[CHUNK - 1:CHUNK, :]
                kdec = (kk * jnp.exp(blast - b)).astype(BF16)
                state[hh] = st * jnp.exp(blast) + _dot(vb, kdec, 0, 0)
                o_ref[rows, lanes] = o
                rstd = lax.rsqrt(jnp.mean(o * o, axis=-1, keepdims=True) + RMS_EPS)
                gate = g_ref[rows, lanes]
                y_ref[rows, lanes] = (o * rstd * onv * (gate * _sigmoid(gate))).astype(BF16)
            return carry

        lax.fori_loop(0, nch, chunk, 0, unroll=2)

    hw = HGRN_HEADS * LANES

    def col(off):
        return pl.BlockSpec((tc, hw), lambda h, i: (i, off // HGRN_HEADS + h))

    osp = pl.BlockSpec((tc, hw), lambda h, i: (i, h))
    return pl.pallas_call(
        body, name=name, grid=(nh // HGRN_HEADS, t // tc),
        in_specs=[col(0), col(nh), col(2 * nh), col(3 * nh),
                  pl.BlockSpec((2, hw), lambda h, i: (0, h)),
                  pl.BlockSpec((1, LANES), lambda h, i: (0, 0)),
                  pl.BlockSpec(cum_all.shape, lambda h, i: (0, 0)),
                  pl.BlockSpec(masks.shape, lambda h, i: (0, 0, 0))],
        out_specs=[osp, osp, pl.BlockSpec((HGRN_HEADS, nch, LANES, LANES), lambda h, i: (h, i, 0, 0))],
        out_shape=[jax.ShapeDtypeStruct((t, nh * LANES), BF16), jax.ShapeDtypeStruct((t, nh * LANES), F32),
                   jax.ShapeDtypeStruct((nh, t // CHUNK, LANES, LANES), F32)],
        scratch_shapes=[pltpu.VMEM((HGRN_HEADS, LANES, LANES), F32)],
        compiler_params=_cp())(pc, pc, pc, pc, c_lb, out_norm, cum_all, masks)


def _hgrn_bwd(pc, o_saved, states, dy, c_lb, out_norm, send, *, name, tc=512):
    t = pc.shape[0]
    nh = pc.shape[1] // 4 // LANES
    tc = min(tc, t)
    nch = tc // CHUNK
    nt = t // tc
    cum_all, masks, suffix = _hgrn_consts()
    ngroup = nh // HGRN_HEADS
    sends = list(send)
    ns = len(sends)

    def body(q_ref, f_ref, i_ref, g_ref, o_ref, st_ref, dy_ref, lb_ref, on_ref, cum_ref, m_ref, suf_ref, *rest):
        send_refs = rest[:ns]
        dq_ref, df_ref, di_ref, dg_ref, dlb_ref, don_ref, parts_ref, dstate = rest[ns:ns + 8]
        start, finish = _direct_exchange_phases(send_refs, parts_ref, *rest[ns + 8:])
        pl.when((pl.program_id(0) == 0) & (pl.program_id(1) == 0))(start)

        @pl.when(pl.program_id(1) == 0)
        def _():
            dstate[...] = jnp.zeros_like(dstate)
            dlb_ref[...] = jnp.zeros_like(dlb_ref)
            don_ref[...] = jnp.zeros_like(don_ref)

        lbv = _lower_bound(lb_ref)
        onv = on_ref[...]

        def head(hh, c, rows):
            lanes = slice(hh * LANES, (hh + 1) * LANES)
            qr = q_ref[rows, lanes]
            sg, fval, g, kk, sq, qs = _hgrn_gates(qr, f_ref[rows, lanes], lbv[:, lanes])
            vb = i_ref[rows, lanes].astype(BF16)
            o = o_ref[rows, lanes]
            gate = g_ref[rows, lanes]
            sgt = _sigmoid(gate)
            rstd = lax.rsqrt(jnp.mean(o * o, axis=-1, keepdims=True) + RMS_EPS)
            ohat = o * rstd
            dyv = dy_ref[rows, lanes]
            don = dyv * (gate * sgt)
            dg_ref[rows, lanes] = (dyv * ohat * onv * (sgt * (1.0 + gate * (1.0 - sgt)))).astype(BF16)
            don_ref[:, lanes] += jnp.sum(don * ohat, axis=0, keepdims=True)
            dxhat = don * onv
            dob = (rstd * (dxhat - ohat * jnp.mean(dxhat * ohat, axis=-1, keepdims=True))).astype(BF16)
            b = _dot_exact_lhs(cum_ref[...], g)
            blast = b[CHUNK - 1:CHUNK, :]
            eb = jnp.exp(b)
            edec = jnp.exp(blast - b)
            st32 = st_ref[hh, c]
            st = st32.astype(BF16)
            dst = dstate[hh]
            dstb = dst.astype(BF16)
            da = _dot(dob, vb, 1, 1)
            levels = _hgrn_levels(b, qs, kk)
            scores = jnp.zeros((CHUNK, CHUNK), F32)
            dq = eb * _dot(dob, st)
            dk_inter = edec * _dot(vb, dstb)
            dk = dk_inter
            for lvl, (ql, kl, eq, ek) in enumerate(levels):
                mk = m_ref[lvl]
                (qh, qlo), (kh, klo) = _split2(ql), _split2(kl)
                scores = scores + _dot(qh, kh, 1, 1) * mk
                dal = (da * mk).astype(BF16)
                dql = _dot(dal, kh) + _dot(dal, klo)
                dkl = _dot(dal, qh, 0, 0) + _dot(dal, qlo, 0, 0)
                dq = dq + (dql if eq is None else dql * eq)
                dk = dk + (dkl if ek is None else dkl * ek)
            kdec = (kk * edec).astype(BF16)
            dv = _dot(scores.astype(BF16), dob, 0, 0) + _dot(kdec, dstb, 1, 1)
            dstate[hh] = dst * jnp.exp(blast) + _dot(dob, (qs * eb).astype(BF16), 0, 0)
            db = qs * dq - kk * dk
            last = jnp.sum(kk * dk_inter, axis=0, keepdims=True) + jnp.exp(blast) * jnp.sum(dst * st32, axis=0, keepdims=True)
            dgl = _dot_exact_lhs(suf_ref[...], db) + last
            dfv = dgl / fval - dk
            df_ref[rows, lanes] = (dfv * (1.0 - lbv[:, lanes]) * sg * (1.0 - sg)).astype(BF16)
            dlb_ref[:, lanes] += jnp.sum(dfv * (1.0 - sg), axis=0, keepdims=True)
            dq_ref[rows, lanes] = (dq * (sq * (1.0 + qr * (1.0 - sq)))).astype(BF16)
            di_ref[rows, lanes] = dv.astype(BF16)

        def chunk(n, carry):
            c = nch - 1 - n
            rows = pl.ds(pl.multiple_of(c * CHUNK, CHUNK), CHUNK)
            for hh in range(HGRN_HEADS):
                head(hh, c, rows)
            return carry

        lax.fori_loop(0, nch, chunk, 0, unroll=2)
        pl.when((pl.program_id(0) == ngroup - 1) & (pl.program_id(1) == nt - 1))(finish)

    hw = HGRN_HEADS * LANES

    def col(off):
        return pl.BlockSpec((tc, hw), lambda h, i: (nt - 1 - i, off // HGRN_HEADS + h))

    osp = pl.BlockSpec((tc, hw), lambda h, i: (nt - 1 - i, h))
    vec = pl.BlockSpec((1, hw), lambda h, i: (0, h))
    return pl.pallas_call(
        body, name=name, grid=(nh // HGRN_HEADS, nt),
        in_specs=[col(0), col(nh), col(2 * nh), col(3 * nh), osp,
                  pl.BlockSpec((HGRN_HEADS, nch, LANES, LANES), lambda h, i: (h, nt - 1 - i, 0, 0)),
                  osp,
                  pl.BlockSpec((2, hw), lambda h, i: (0, h)),
                  pl.BlockSpec((1, LANES), lambda h, i: (0, 0)),
                  pl.BlockSpec(cum_all.shape, lambda h, i: (0, 0)),
                  pl.BlockSpec(masks.shape, lambda h, i: (0, 0, 0)),
                  pl.BlockSpec(suffix.shape, lambda h, i: (0, 0))] + [HBM_SPEC] * ns,
        out_specs=[osp, osp, osp, osp, vec, vec, HBM_SPEC],
        out_shape=[jax.ShapeDtypeStruct((t, nh * LANES), BF16)] * 4 + [jax.ShapeDtypeStruct((1, nh * LANES), F32)] * 2
        + [jax.ShapeDtypeStruct((8, _direct_exchange_rows(sends), sends[0].shape[3]), sends[0].dtype)],
        scratch_shapes=[pltpu.VMEM((HGRN_HEADS, LANES, LANES), F32)] + _direct_exchange_scratch(),
        compiler_params=_cp(dimension_semantics=("arbitrary", "arbitrary")))(
            pc, pc, pc, pc, o_saved, states, dy, c_lb, out_norm, cum_all, masks, suffix, *sends)


HBM_SPEC = pl.BlockSpec(memory_space=pltpu.HBM)


def _gather_scratch():
    return [pltpu.SemaphoreType.DMA((7,)), pltpu.SemaphoreType.DMA((7,)), pltpu.SemaphoreType.DMA]


def _gather_phases(x_ref, out_refs, seg_rows, send_sems, recv_sems, local_sem):
    x, y, c = lax.axis_index("x"), lax.axis_index("y"), lax.axis_index("c")
    me, sibling = (x, y, c), (x, y, 1 - c)
    chips = [(1 - x, y), (x, 1 - y), (1 - x, 1 - y)]
    offs = [sum(seg_rows[:s]) for s in range(len(seg_rows))]
    assert sum(seg_rows) == x_ref.shape[0]

    def index(px, py, pc):
        return 4 * px + 2 * py + pc

    def copies(k, block, to, own):
        return [pltpu.make_async_remote_copy(
            src_ref=x_ref.at[pl.ds(offs[s], n)] if own else out_refs[s].at[index(*block)],
            dst_ref=out_refs[s].at[index(*block)],
            send_sem=send_sems.at[k], recv_sem=recv_sems.at[k], device_id=to, device_id_type=MESH)
            for s, n in enumerate(seg_rows)]

    def all_bytes(k):
        return pltpu.make_async_remote_copy(src_ref=x_ref, dst_ref=x_ref, send_sem=send_sems.at[k],
                                            recv_sem=recv_sems.at[k], device_id=me, device_id_type=MESH)

    mine = [pltpu.make_async_copy(x_ref.at[pl.ds(offs[s], n)], out_refs[s].at[index(*me)], local_sem)
            for s, n in enumerate(seg_rows)]
    first = copies(0, me, sibling, True)
    for j, chip in enumerate(chips):
        first += copies(1 + j, me, (*chip, c), True)

    def start():
        for cp in mine + first:
            cp.start()

    def forward():
        for j, chip in enumerate(chips):
            all_bytes(1 + j).wait_recv()
            for cp in copies(4 + j, (*chip, c), sibling, False):
                cp.start()

    def finish():
        all_bytes(0).wait_recv()
        for j in range(3):
            all_bytes(4 + j).wait_recv()
        for k in range(7):
            all_bytes(k).wait_send()
        pltpu.make_async_copy(x_ref, x_ref, local_sem).wait()

    return start, forward, finish


def _all_gather(xs, seg_rows=None, *, name):
    segs = [xs.shape[0]] if seg_rows is None else list(seg_rows)

    def body(x_ref, *rest):
        start, forward, finish = _gather_phases(x_ref, rest[:len(segs)], segs, *rest[len(segs):])
        start()
        forward()
        finish()

    outs = pl.pallas_call(
        body, name=name, in_specs=[HBM_SPEC], out_specs=[HBM_SPEC] * len(segs),
        out_shape=[jax.ShapeDtypeStruct((8, n, xs.shape[1]), xs.dtype) for n in segs],
        scratch_shapes=_gather_scratch())(xs)
    return outs[0] if seg_rows is None else outs


def _sibling_exchange(s, *, name):
    def body(s_ref, rb_ref, send_sem, recv_sem):
        x, y, c = lax.axis_index("x"), lax.axis_index("y"), lax.axis_index("c")
        cp = pltpu.make_async_remote_copy(
            src_ref=s_ref.at[:, 1 - c], dst_ref=rb_ref, send_sem=send_sem, recv_sem=recv_sem,
            device_id=(x, y, 1 - c), device_id_type=MESH)
        cp.start()
        cp.wait()

    return pl.pallas_call(
        body, name=name, in_specs=[HBM_SPEC], out_specs=HBM_SPEC,
        out_shape=jax.ShapeDtypeStruct(s.shape[:1] + s.shape[2:], s.dtype),
        scratch_shapes=[pltpu.SemaphoreType.DMA, pltpu.SemaphoreType.DMA])(s)


def _row_tile(n, cap=1024):
    return max(b for b in range(16, cap + 1, 16) if n % b == 0)


def _pair_add(s, rb, core, *, name):
    nchip, _, r, c = s.shape
    tb = _row_tile(r)

    def body(core_ref, a_ref, b_ref, o_ref):
        o_ref[...] = (a_ref[...].astype(F32) + b_ref[...].astype(F32)).astype(BF16)

    blk = pl.BlockSpec((None, tb, c), lambda ch, i, cr: (ch, i, 0))
    return pl.pallas_call(
        body, name=name,
        grid_spec=pltpu.PrefetchScalarGridSpec(
            num_scalar_prefetch=1, grid=(nchip, r // tb),
            in_specs=[pl.BlockSpec((None, None, tb, c), lambda ch, i, cr: (ch, cr[0], i, 0)), blk],
            out_specs=blk),
        out_shape=jax.ShapeDtypeStruct((nchip, r, c), BF16), compiler_params=_cp())(core, s, rb)


def _chip_exchange_scratch():
    return [pltpu.SemaphoreType.DMA((3,)), pltpu.SemaphoreType.DMA((3,)), pltpu.SemaphoreType.DMA]


def _chip_exchange_phases(p_ref, out_ref, send_sems, recv_sems, local_sem):
    x, y, c = lax.axis_index("x"), lax.axis_index("y"), lax.axis_index("c")
    mine = 2 * x + y
    own = pltpu.make_async_copy(p_ref.at[mine], out_ref.at[mine], local_sem)
    copies = [pltpu.make_async_remote_copy(
        src_ref=p_ref.at[2 * tx + ty], dst_ref=out_ref.at[mine],
        send_sem=send_sems.at[k], recv_sem=recv_sems.at[k], device_id=(tx, ty, c), device_id_type=MESH)
        for k, (tx, ty) in enumerate([(1 - x, y), (x, 1 - y), (1 - x, 1 - y)])]

    def start():
        own.start()
        for cp in copies:
            cp.start()

    def finish():
        for cp in copies:
            cp.wait()
        own.wait()

    return start, finish


def _direct_exchange_scratch():
    return [pltpu.SemaphoreType.DMA((7,)), pltpu.SemaphoreType.DMA((7,)), pltpu.SemaphoreType.DMA]


def _direct_exchange_rows(sends):
    return sum(s.shape[2] for s in sends)


def _direct_exchange_phases(s_refs, out_ref, send_sems, recv_sems, local_sem):
    x, y, c = lax.axis_index("x"), lax.axis_index("y"), lax.axis_index("c")
    me = 4 * x + 2 * y + c
    offs, off = [], 0
    for s in s_refs:
        offs.append(off)
        off += s.shape[2]

    def slot(p):
        return out_ref.at[me, pl.ds(offs[p], s_refs[p].shape[2])]

    own = [pltpu.make_async_copy(s.at[2 * x + y, c], slot(p), local_sem) for p, s in enumerate(s_refs)]
    flips = [(fx, fy, fc) for fx in (0, 1) for fy in (0, 1) for fc in (0, 1) if (fx, fy, fc) != (0, 0, 0)]
    copies = []
    for k, (fx, fy, fc) in enumerate(flips):
        tx, ty, tc = (1 - x if fx else x), (1 - y if fy else y), (1 - c if fc else c)
        copies += [pltpu.make_async_remote_copy(
            src_ref=s.at[2 * tx + ty, tc], dst_ref=slot(p),
            send_sem=send_sems.at[k], recv_sem=recv_sems.at[k], device_id=(tx, ty, tc), device_id_type=MESH)
            for p, s in enumerate(s_refs)]

    def start():
        for cp in own + copies:
            cp.start()

    def finish():
        whole = out_ref.at[me]
        for k in range(len(flips)):
            pltpu.make_async_remote_copy(src_ref=whole, dst_ref=whole, send_sem=send_sems.at[k],
                                         recv_sem=recv_sems.at[k], device_id=(x, y, c), device_id_type=MESH).wait()
        pltpu.make_async_copy(whole, whole, local_sem).wait()

    return start, finish


def _adamw_math(w, g, m, v):
    m2 = ADAM_B1 * m + (1.0 - ADAM_B1) * g
    v2 = ADAM_B2 * v + (1.0 - ADAM_B2) * (g * g)
    m_hat = m2 / (1.0 - ADAM_B1 ** ADAM_STEP)
    v_hat = v2 / (1.0 - ADAM_B2 ** ADAM_STEP)
    return -ADAM_LR * (m_hat / (jnp.sqrt(v_hat) + ADAM_EPS) + ADAM_WD * w), m2, v2


def _adamw_shard(parts, g_off, w, m, v, layer, prev, *, name):
    _, r, c = w.shape
    npart = parts.shape[0]
    tb = next(b for b in range(min(r, 512), 0, -16) if r % b == 0 and g_off % b == 0)

    def body(*refs):
        w_ref, m_ref, v_ref = refs[npart:npart + 3]
        g_out, d_out, m_out, v_out = refs[-4:]
        g = refs[0][...].astype(F32)
        for p_ref in refs[1:npart]:
            g = g + p_ref[...].astype(F32)
        d, m2, v2 = _adamw_math(w_ref[...], g, m_ref[...], v_ref[...])
        g_out[...] = g
        d_out[...] = d
        m_out[...] = m2
        v_out[...] = v2

    def part(ch):
        return pl.BlockSpec((None, tb, c), lambda i: (ch, g_off // tb + i, 0))

    blk = pl.BlockSpec((None, tb, c), lambda i: (layer, i, 0))
    prev = list(prev) if prev is not None else []
    return pl.pallas_call(
        body, name=name, grid=(r // tb,),
        in_specs=[part(ch) for ch in range(npart)] + [blk, blk, blk] + [pl.BlockSpec(memory_space=pl.ANY)] * len(prev),
        out_specs=[blk] * 4, out_shape=[jax.ShapeDtypeStruct(w.shape, F32)] * 4,
        input_output_aliases={npart + 3 + k: k for k in range(len(prev))},
        compiler_params=_cp())(*([parts] * npart), w, m, v, *prev)


SLOT = 8
SMALL_ROWS = 6 * SLOT
ROW_LB = 4 * SLOT


def _small_update(gath, w, m, v, *, name):
    def body(g_ref, w_ref, m_ref, v_ref, g_out, d_out, m_out, v_out):
        tot = g_ref[0]
        for k in range(1, 8):
            tot = tot + g_ref[k]
        wv = w_ref[...]
        c0, c1 = wv[ROW_LB:ROW_LB + 1, :], wv[ROW_LB + 1:ROW_LB + 2, :]
        mx = jnp.maximum(c0, c1)
        e0, e1 = jnp.exp(c0 - mx), jnp.exp(c1 - mx)
        lb = e1 / (e0 + e1)
        gl = tot[ROW_LB:ROW_LB + 1, :] * lb * (1.0 - lb)
        row = lax.broadcasted_iota(jnp.int32, tot.shape, 0)
        g = jnp.where(row == ROW_LB, -gl, jnp.where(row == ROW_LB + 1, gl, tot))
        d, m2, v2 = _adamw_math(wv, g, m_ref[...], v_ref[...])
        g_out[...] = g
        d_out[...] = d
        m_out[...] = m2
        v_out[...] = v2

    return pl.pallas_call(
        body, name=name, out_shape=[jax.ShapeDtypeStruct(w.shape, F32)] * 4, compiler_params=_cp())(gath, w, m, v)


D_MODEL = 1024


def _ffn_fwd(h, gain, wg, wu, wd, tag):
    xn, gg, uu, act = _norm_gate_up(h, gain, wg, wu, name=f"{tag}_gate_up")
    out = _mm([(act, wd)], residual=h, alpha=MACARON, tn=1024, name=f"{tag}_down")
    return out, (h, xn, gg, uu, act)


def _ffn_input_bwd(dg, du, wg, wu, x, gain, dres, chip_part, *, name, scale, tm=256):
    t, d = x.shape
    f = wg.shape[0]
    tm = min(tm, t)
    nt = t // tm
    fused = chip_part is not None

    def body(dg_ref, du_ref, wg_ref, wu_ref, x_ref, g_ref, dres_ref, *rest):
        if fused:
            part_ref, dx_ref, dxb_ref, dgain_ref, parts_ref = rest[:5]
            start, finish = _chip_exchange_phases(part_ref, parts_ref, *rest[5:])
            pl.when(pl.program_id(0) == 0)(start)
        else:
            dx_ref, dxb_ref, dgain_ref = rest
        dxn_v = _dot(dg_ref[...], wg_ref[...]) + _dot(du_ref[...], wu_ref[...])
        xv = x_ref[...]
        rstd = lax.rsqrt(jnp.mean(xv * xv, axis=-1, keepdims=True) + RMS_EPS)
        xhat = xv * rstd
        dxhat = dxn_v * g_ref[...]
        dx = dres_ref[...] + rstd * (dxhat - xhat * jnp.mean(dxhat * xhat, axis=-1, keepdims=True))
        dx_ref[...] = dx
        dxb_ref[...] = (dx * scale).astype(BF16)

        @pl.when(pl.program_id(0) == 0)
        def _():
            dgain_ref[...] = jnp.zeros_like(dgain_ref)

        dgain_ref[...] += jnp.sum(dxn_v * xhat, axis=0, keepdims=True)
        if fused:
            pl.when(pl.program_id(0) == nt - 1)(finish)

    wide = pl.BlockSpec((tm, f), lambda i: (i, 0))
    wsp = pl.BlockSpec((f, d), lambda i: (0, 0))
    row = pl.BlockSpec((tm, d), lambda i: (i, 0))
    vec = pl.BlockSpec((1, d), lambda i: (0, 0))
    args = [dg, du, wg, wu, x, gain, dres] + ([chip_part] if fused else [])
    return pl.pallas_call(
        body, name=name, grid=(nt,),
        in_specs=[wide, wide, wsp, wsp, row, vec, row] + ([HBM_SPEC] if fused else []),
        out_specs=[row, row, vec] + ([HBM_SPEC] if fused else []),
        out_shape=[jax.ShapeDtypeStruct((t, d), F32), jax.ShapeDtypeStruct((t, d), BF16), jax.ShapeDtypeStruct((1, d), F32)]
        + ([jax.ShapeDtypeStruct(chip_part.shape, chip_part.dtype)] if fused else []),
        scratch_shapes=_chip_exchange_scratch() if fused else [],
        compiler_params=_cp(dimension_semantics=("arbitrary",)))(*args)


def _ffn_bwd(dout, dout_half, saved, gain, wg, wu, wd, tag, next_scale, exchanges=None):
    h, xn, gg, uu, act = saved
    early_chip_part, send_after_dwd, chip_part_after_dwgu = exchanges if exchanges is not None else (None, None, None)
    dg, du, *early_parts = _swiglu_bwd(dout_half, wd, gg, uu, early_chip_part, tm=256, tf=wd.shape[0],
                                       name=f"{tag}_dact")
    dwd = _mm([(act, dout_half)], ta=True, tm=256, tn=1024, out_dtype=BF16, name=f"{tag}_dwd")
    send = send_after_dwd(dwd) if exchanges is not None else None
    dwg, dwu, *mid_parts = _mm_shared_rhs([dg, du], xn, tm=256, send=send, name=f"{tag}_dwgu")
    chip_part = chip_part_after_dwgu(dwg, dwu) if exchanges is not None else None
    dh, dh_b, dgain, *parts = _ffn_input_bwd(dg, du, wg, wu, h, gain, dout, chip_part, scale=next_scale,
                                             name=f"{tag}_input_bwd")
    return dh, dh_b, dwg, dwu, dwd, dgain, (early_parts + mid_parts + parts)


def kernel(x, ffn_pre_norm, ffn_pre_w_gate, ffn_pre_w_up, ffn_pre_w_down, mix_norm, ffn_post_norm, ffn_post_w_gate, ffn_post_w_up, ffn_post_w_down, ab_w_in, ab_conv_w, ab_w_out, c_w_in, c_lower_bounds, c_out_norm, c_w_out, final_norm, loss_target, m_ffn_pre_norm, m_ffn_pre_w_gate, m_ffn_pre_w_up, m_ffn_pre_w_down, m_mix_norm, m_ffn_post_norm, m_ffn_post_w_gate, m_ffn_post_w_up, m_ffn_post_w_down, m_ab_w_in, m_ab_conv_w, m_ab_w_out, m_c_w_in, m_c_lower_bounds, m_c_out_norm, m_c_w_out, m_final_norm, v_ffn_pre_norm, v_ffn_pre_w_gate, v_ffn_pre_w_up, v_ffn_pre_w_down, v_mix_norm, v_ffn_post_norm, v_ffn_post_w_gate, v_ffn_post_w_up, v_ffn_post_w_down, v_ab_w_in, v_ab_conv_w, v_ab_w_out, v_c_w_in, v_c_lower_bounds, v_c_out_norm, v_c_w_out, v_final_norm):
    d = D_MODEL
    h0 = x[0]
    target = loss_target[0]
    core = lax.axis_index("c").astype(jnp.int32).reshape(1)

    big = [("pre_g", ffn_pre_w_gate, m_ffn_pre_w_gate, v_ffn_pre_w_gate),
           ("pre_u", ffn_pre_w_up, m_ffn_pre_w_up, v_ffn_pre_w_up),
           ("pre_d", ffn_pre_w_down, m_ffn_pre_w_down, v_ffn_pre_w_down),
           ("post_g", ffn_post_w_gate, m_ffn_post_w_gate, v_ffn_post_w_gate),
           ("post_u", ffn_post_w_up, m_ffn_post_w_up, v_ffn_post_w_up),
           ("post_d", ffn_post_w_down, m_ffn_post_w_down, v_ffn_post_w_down),
           ("ab_in", ab_w_in, m_ab_w_in, v_ab_w_in),
           ("ab_out", ab_w_out, m_ab_w_out, v_ab_w_out),
           ("c_in", c_w_in, m_c_w_in, v_c_w_in),
           ("c_out", c_w_out, m_c_w_out, v_c_w_out)]
    by_tag = {tag: (w, m, v) for tag, w, m, v in big}

    def layer_rows(tag):
        w = by_tag[tag][0]
        return w.size // d // w.shape[0]

    def layout(items):
        offs, off = {}, 0
        for item in items:
            offs[item] = off
            off += layer_rows(item[0])
        return offs, off

    ffn = [f"{pos}_{kind}" for pos in ("pre", "post") for kind in "gud"]
    first_items = [("pre_g", 0), ("pre_u", 0)]
    early_items = [("pre_d", 0), ("ab_in", 0)]
    late_items = ([("pre_g", 1), ("pre_u", 1), ("pre_d", 1)] + [(f"post_{kind}", l) for l in (0, 1) for kind in "gud"]
                  + [("ab_out", 0), ("c_in", 0), ("c_out", 0)])
    grad_items = {"A0": [(f"post_{kind}", 1) for kind in "gud"] + [("c_out", 0)],
                  "A1": ([(f"pre_{kind}", 1) for kind in "gud"] + [(f"post_{kind}", 0) for kind in "gud"]
                         + [("c_in", 0), ("ab_out", 0)]),
                  "C": [("ab_in", 0)], "B0": [("pre_d", 0)], "B1": [("pre_g", 0), ("pre_u", 0)]}
    grad_offs = {k: layout(items)[0] for k, items in grad_items.items()}
    grad_conv_row = layout(grad_items["C"])[1]

    def conv_rows(a, split):
        flat = a.reshape(-1)
        if split:
            hi = flat.astype(BF16)
            flat = jnp.concatenate([hi, (flat - hi.astype(F32)).astype(BF16)])
        return jnp.zeros((16, d), flat.dtype).at[0, :flat.shape[0]].set(flat)

    nconv = ab_conv_w.size
    col_sharded = {"pre_g", "pre_u", "post_g", "post_u", "ab_in", "c_in"}

    def pack_rows(item):
        tag, layer = item
        a = by_tag[tag][0][layer]
        return (a.T if tag in col_sharded else a).reshape(-1, d).astype(BF16)

    first_pack = jnp.concatenate([pack_rows(item) for item in first_items], axis=0)
    early_pack = jnp.concatenate([pack_rows(item) for item in early_items] + [conv_rows(ab_conv_w, True)], axis=0)
    late_pack = jnp.concatenate([pack_rows(item) for item in late_items], axis=0)
    first_w = _all_gather(first_pack, [layer_rows(tag) for tag, _ in first_items], name="gather_first_weights")
    full = {item: g.reshape(-1, d) for item, g in zip(first_items, first_w)}

    xn0, gg0, uu0, act0, *early_w = _norm_gate_up(
        h0, ffn_pre_norm[0:1], full["pre_g", 0], full["pre_u", 0], name="l0pre_gate_up_gather_early_weights",
        pack=early_pack, seg_rows=[layer_rows(tag) for tag, _ in early_items] + [16])
    full.update({item: g.reshape(-1, d) for item, g in zip(early_items, early_w)})
    ffn_w = {("pre", 0): tuple(full[f"pre_{kind}", 0] for kind in "gud")}
    w_ab_in = full["ab_in", 0]
    cg = early_w[-1][:, 0, :2 * nconv].astype(F32)
    conv_w = (cg[:, :nconv] + cg[:, nconv:]).reshape(8, 3, -1).transpose(1, 0, 2).reshape(3, -1)
    aw = w_ab_in.shape[0] // 6
    h1 = _mm([(act0, full["pre_d", 0])], residual=h0, alpha=MACARON, tn=1024, name="l0pre_down")
    s_pre0 = (h0, xn0, gg0, uu0, act0)
    hn0, pa, pb = _norm_proj(h1, mix_norm[0:1], w_ab_in, (F32, BF16), tm=512, name="ab_norm_proj")
    ya = _conv_fwd(pa, conv_w, name="conv_fwd")
    yb, ltot, *late_w = _attn_fwd(pb, late_pack, [layer_rows(tag) for tag, _ in late_items],
                                  name="attn_fwd_gather_late_weights")
    full.update({item: g.reshape(-1, d) for item, g in zip(late_items, late_w)})
    for pos, layer in (("post", 0), ("pre", 1), ("post", 1)):
        ffn_w[pos, layer] = tuple(full[f"{pos}_{kind}", layer] for kind in "gud")
    w_ab_out, w_c_in, w_c_out = full["ab_out", 0], full["c_in", 0], full["c_out", 0]
    h2 = _mm([(ya, w_ab_out[:aw]), (yb, w_ab_out[aw:])], residual=h1, tn=1024, name="ab_out")
    h3, s_post0 = _ffn_fwd(h2, ffn_post_norm[0:1], *ffn_w["post", 0], "l0post")
    h4, s_pre1 = _ffn_fwd(h3, ffn_pre_norm[1:2], *ffn_w["pre", 1], "l1pre")
    hn1, pc = _norm_proj(h4, mix_norm[1:2], w_c_in, (F32,), tm=256, name="c_norm_proj")
    yc, o_saved, states = _hgrn_fwd(pc, c_lower_bounds, c_out_norm, name="hgrn_fwd")
    h5 = _mm([(yc, w_c_out)], residual=h4, tn=1024, name="c_out")
    h6, s_post1 = _ffn_fwd(h5, ffn_post_norm[1:2], *ffn_w["post", 1], "l1post")
    dh6, dh6_b, d_final, loss_vec = _loss_head(h6, final_norm.reshape(1, d), target, name="loss_head")

    gw = {}

    def grad_send(key, extra=()):
        return [g.reshape(4, 2, -1, d) for g in [gw[item] for item in grad_items[key]] + list(extra)]

    def chip_partials(key, extra=()):
        send = jnp.concatenate(grad_send(key, extra), axis=2)
        from_sibling = _sibling_exchange(send, name=f"grad{key}_sibling_exchange")
        return _pair_add(send, from_sibling, core, name=f"grad{key}_pair_add")

    dh5, dh5_b, gw["post_g", 1], gw["post_u", 1], gw["post_d", 1], d_post1, *_ = _ffn_bwd(
        dh6, dh6_b, s_post1, ffn_post_norm[1:2], *ffn_w["post", 1], "l1post", 1.0)
    dyc = _mm([(dh5_b, w_c_out)], tb=True, tn=1024, name="c_out_dy")
    g_c_out = _mm([(yc, dh5_b)], ta=True, tm=256, tn=1024, out_dtype=BF16, name="c_out_dw")
    gw["c_out", 0] = g_c_out
    dcq, dcf, dci, dcg, dlb, d_onorm, parts_a0 = _hgrn_bwd(pc, o_saved, states, dyc, c_lower_bounds, c_out_norm,
                                                           grad_send("A0"), name="hgrn_bwd_exchange_grads_a0")
    dparts = [dcq, dcf, dci, dcg]
    g_c_in = jnp.concatenate(_mm_shared_rhs(dparts, hn1, tm=256, name="c_in_dw"), axis=0)
    cw = w_c_in.shape[0] // 4
    dhn1 = _mm([(dp, w_c_in[i * cw:(i + 1) * cw]) for i, dp in enumerate(dparts)], tm=512, tn=1024, name="c_in_dx")
    dh4, dh4_b, d_mix1 = _rmsnorm_bwd(h4, mix_norm[1:2], dhn1, dh5, scale=MACARON, name="l1_mix_norm_bwd")
    dh3, dh3_b, gw["pre_g", 1], gw["pre_u", 1], gw["pre_d", 1], d_pre1, *_ = _ffn_bwd(
        dh4, dh4_b, s_pre1, ffn_pre_norm[1:2], *ffn_w["pre", 1], "l1pre", MACARON)
    dh2, dh2_b, gw["post_g", 0], gw["post_u", 0], gw["post_d", 0], d_post0, *_ = _ffn_bwd(
        dh3, dh3_b, s_post0, ffn_post_norm[0:1], *ffn_w["post", 0], "l0post", 1.0)
    dyab = _mm([(dh2_b, w_ab_out)], tb=True, tn=1024, name="ab_out_dy")
    g_ab_out = jnp.concatenate(_mm_shared_rhs([ya, yb], dh2_b, tm=256, name="ab_out_dw"), axis=0)
    dab, dac, dax, g_conv = _conv_bwd(pa, dyab, conv_w, name="conv_bwd")

    gw["c_in", 0], gw["ab_out", 0] = g_c_in, g_ab_out
    dq, dk, dv, parts_a1 = _attn_bwd(pb, dyab, ltot, grad_send("A1"), name="attn_bwd_exchange_grads_a1")
    dparts = [dab, dac, dax, dq, dk, dv]
    g_ab_in = jnp.concatenate(_mm_shared_rhs(dparts, hn0, tm=128, name="ab_in_dw"), axis=0)
    dhn0 = _mm([(dp, w_ab_in[i * aw:(i + 1) * aw]) for i, dp in enumerate(dparts)], tm=512, tn=1024, name="ab_in_dx")
    dh1, dh1_b, d_mix0 = _rmsnorm_bwd(h1, mix_norm[0:1], dhn0, dh2, scale=MACARON, name="l0_mix_norm_bwd")
    gw["ab_in", 0] = g_ab_in
    gconv_own = g_conv.reshape(3, 8, -1).transpose(1, 0, 2).reshape(8, -1)
    conv_piece = jnp.zeros((8, 16, d), F32).at[:, 0, :nconv].set(gconv_own).astype(BF16)

    def send_b0(dwd):
        gw["pre_d", 0] = dwd
        return grad_send("B0")

    def chip_part_b1(dwg, dwu):
        gw["pre_g", 0], gw["pre_u", 0] = dwg, dwu
        return chip_partials("B1")

    dh0, _, _, _, _, d_pre0, (parts_c, parts_b0, parts_b1) = _ffn_bwd(
        dh1, dh1_b, s_pre0, ffn_pre_norm[0:1], *ffn_w["pre", 0], "l0pre", 1.0,
        (chip_partials("C", [conv_piece]), send_b0, chip_part_b1))

    parts = {"A0": parts_a0, "A1": parts_a1, "B0": parts_b0, "B1": parts_b1, "C": parts_c}
    upd = {}
    for tag, w, m, v in big:
        view = (lambda a: jnp.swapaxes(a, 1, 2)) if tag in col_sharded else (lambda a: a)
        where = {layer: (key, grad_offs[key][tag, layer])
                 for key in grad_items for t2, layer in grad_items[key] if t2 == tag}
        res = None
        for layer in sorted(where):
            key, off = where[layer]
            res = _adamw_shard(parts[key], off, view(w), view(m), view(v), layer, res, name=f"adamw_{tag}{layer}")
        upd[tag] = [view(a) for a in res]
    res = _adamw_shard(parts["C"], grad_conv_row, *(conv_rows(a, False)[None] for a in (ab_conv_w, m_ab_conv_w, v_ab_conv_w)),
                       0, None, name="adamw_conv")
    upd["conv"] = [r[0, 0, :nconv].reshape(ab_conv_w.shape) for r in res]

    def small_pack(pre, mix, post, final, lbs, onorm):
        def slot(parts):
            out, r = jnp.zeros((SLOT, d), F32), 0
            for a in (parts if isinstance(parts, tuple) else (parts,)):
                out = out.at[r:r + a.shape[0], :a.shape[1]].set(a)
                r += a.shape[0]
            return out

        return jnp.concatenate([slot(pre), slot(mix), slot(post), slot(final.reshape(1, d)), slot(lbs), slot(onorm)], axis=0)

    d_on = d_onorm.reshape(-1, c_out_norm.shape[1]).sum(axis=0, keepdims=True)
    gsmall = small_pack((d_pre0, d_pre1), (d_mix0, d_mix1), (d_post0, d_post1), d_final, dlb, d_on)
    gsmall_all = _all_gather(gsmall, name="gather_small_grads")
    sres = _small_update(
        gsmall_all,
        small_pack(ffn_pre_norm, mix_norm, ffn_post_norm, final_norm, c_lower_bounds, c_out_norm),
        small_pack(m_ffn_pre_norm, m_mix_norm, m_ffn_post_norm, m_final_norm, m_c_lower_bounds, m_c_out_norm),
        small_pack(v_ffn_pre_norm, v_mix_norm, v_ffn_post_norm, v_final_norm, v_c_lower_bounds, v_c_out_norm),
        name="small_update")

    def small_out(r):
        return {"pre_norm": r[0:2], "mix_norm": r[SLOT:SLOT + 2], "post_norm": r[2 * SLOT:2 * SLOT + 2],
                "final": r[3 * SLOT], "lb": r[ROW_LB:ROW_LB + 2], "onorm": r[5 * SLOT:5 * SLOT + 1, :c_out_norm.shape[1]]}

    small = [small_out(r) for r in sres]
    outs = []
    for k in range(4):
        s = small[k]
        outs += [s["pre_norm"], upd["pre_g"][k], upd["pre_u"][k], upd["pre_d"][k], s["mix_norm"], s["post_norm"],
                 upd["post_g"][k], upd["post_u"][k], upd["post_d"][k], upd["ab_in"][k], upd["conv"][k],
                 upd["ab_out"][k], upd["c_in"][k], s["lb"], s["onorm"], upd["c_out"][k], s["final"]]
    loss = lax.psum(loss_vec[0, 0], ("x", "y", "c"))
    return (loss, dh0[None], *outs)
```

```python
import math

import jax
import jax.numpy as jnp
from jax import lax
from jax.experimental import pallas as pl
from jax.experimental.pallas import tpu as pltpu

F32 = jnp.float32
BF16 = jnp.bfloat16
MESH = pl.DeviceIdType.MESH

RMS_EPS = 1e-6
MACARON = 0.5
LANES = 128
CHUNK = 64
N_LEVELS = 6
HGRN_HEADS = 2
SB_KEYS = 256
ADAM_LR, ADAM_B1, ADAM_B2, ADAM_EPS, ADAM_WD, ADAM_STEP = 0.001, 0.9, 0.999, 1e-08, 0.01, 10
VMEM_LIMIT = 48 * 1024 * 1024


def _cp(**kw):
    return pltpu.CompilerParams(vmem_limit_bytes=VMEM_LIMIT, **kw)


def _sigmoid(x):
    return 0.5 * jnp.tanh(0.5 * x) + 0.5


def _bf(x):
    return x if x.dtype == BF16 else x.astype(BF16)


def _split3(x):
    hi = x.astype(BF16)
    r1 = x - hi.astype(F32)
    mid = r1.astype(BF16)
    lo = (r1 - mid.astype(F32)).astype(BF16)
    return hi, mid, lo


def _dot(a, b, ca=1, cb=0):
    return lax.dot_general(a, b, (((ca,), (cb,)), ((), ())), preferred_element_type=F32)


def _dot_exact_lhs(m, x):
    hi, mid, lo = _split3(x)
    return _dot(m, hi) + _dot(m, mid) + _dot(m, lo)


def _mm(terms, *, name, ta=False, tb=False, out_dtype=F32, residual=None, alpha=1.0, tm=512, tn=512):
    nt = len(terms)
    a0, b0 = terms[0]
    m = a0.shape[1] if ta else a0.shape[0]
    n = b0.shape[0] if tb else b0.shape[1]
    tm, tn = min(tm, m), min(tn, n)
    assert m % tm == 0 and n % tn == 0, (name, m, n, tm, tn)
    has_res = residual is not None

    def body(*refs):
        o_ref = refs[-1]
        acc = None
        for i in range(nt):
            a = _bf(refs[2 * i][...])
            b = _bf(refs[2 * i + 1][...])
            p = _dot(a, b, 0 if ta else 1, 1 if tb else 0)
            acc = p if acc is None else acc + p
        if alpha != 1.0:
            acc = acc * alpha
        if has_res:
            acc = acc + refs[2 * nt][...]
        o_ref[...] = acc.astype(out_dtype)

    in_specs, args = [], []
    for a, b in terms:
        k = a.shape[0] if ta else a.shape[1]
        assert (b.shape[1] if tb else b.shape[0]) == k, (name, a.shape, b.shape)
        in_specs.append(pl.BlockSpec((k, tm), lambda i, j: (0, i)) if ta else pl.BlockSpec((tm, k), lambda i, j: (i, 0)))
        in_specs.append(pl.BlockSpec((tn, k), lambda i, j: (j, 0)) if tb else pl.BlockSpec((k, tn), lambda i, j: (0, j)))
        args += [a, b]
    if has_res:
        in_specs.append(pl.BlockSpec((tm, tn), lambda i, j: (i, j)))
        args.append(residual)
    return pl.pallas_call(
        body, name=name, grid=(m // tm, n // tn), in_specs=in_specs,
        out_specs=pl.BlockSpec((tm, tn), lambda i, j: (i, j)),
        out_shape=jax.ShapeDtypeStruct((m, n), out_dtype), compiler_params=_cp())(*args)


def _norm_proj(x, gain, w_t, out_dtypes, *, name, tm):
    t, d = x.shape
    n = w_t.shape[0]
    tm = min(tm, t)
    npart = len(out_dtypes)
    width = n // npart

    def body(x_ref, g_ref, w_ref, xn_ref, *part_refs):
        xv = x_ref[...]
        rstd = lax.rsqrt(jnp.mean(xv * xv, axis=-1, keepdims=True) + RMS_EPS)
        xn = (xv * rstd * g_ref[...]).astype(BF16)
        xn_ref[...] = xn
        for p, ref in enumerate(part_refs):
            ref[...] = _dot(xn, w_ref[p * width:(p + 1) * width, :], 1, 1).astype(out_dtypes[p])

    row = pl.BlockSpec((tm, d), lambda i: (i, 0))
    return pl.pallas_call(
        body, name=name, grid=(t // tm,),
        in_specs=[row, pl.BlockSpec((1, d), lambda i: (0, 0)), pl.BlockSpec((n, d), lambda i: (0, 0))],
        out_specs=[row] + [pl.BlockSpec((tm, width), lambda i: (i, 0))] * npart,
        out_shape=[jax.ShapeDtypeStruct((t, d), BF16)] + [jax.ShapeDtypeStruct((t, width), dt) for dt in out_dtypes],
        compiler_params=_cp())(x, gain, w_t)


def _mm_shared_rhs(a_list, b, *, name, tm, out_dtype=BF16, send=None):
    k, n = b.shape
    assert all(a.shape[0] == k and a.shape[1] % tm == 0 and a.shape[1] == a_list[0].shape[1] for a in a_list)
    m = a_list[0].shape[1]
    na = len(a_list)
    nsteps = m // tm
    sends = list(send) if send is not None else []
    ns = len(sends)

    def body(*refs):
        first_out = na + 1 + ns
        if ns:
            start, finish = _direct_exchange_phases(refs[na + 1:first_out], refs[first_out + na], *refs[first_out + na + 1:])
            pl.when(pl.program_id(0) == 0)(start)
        bv = refs[na][...]
        for i in range(na):
            refs[first_out + i][...] = _dot(refs[i][...], bv, 0, 0).astype(out_dtype)
        if ns:
            pl.when(pl.program_id(0) == nsteps - 1)(finish)

    return pl.pallas_call(
        body, name=name, grid=(nsteps,),
        in_specs=[pl.BlockSpec((k, tm), lambda i: (0, i))] * na + [pl.BlockSpec((k, n), lambda i: (0, 0))] + [HBM_SPEC] * ns,
        out_specs=[pl.BlockSpec((tm, n), lambda i: (i, 0))] * na + ([HBM_SPEC] if ns else []),
        out_shape=[jax.ShapeDtypeStruct((m, n), out_dtype)] * na
        + ([jax.ShapeDtypeStruct((8, _direct_exchange_rows(sends), sends[0].shape[3]), sends[0].dtype)] if ns else []),
        scratch_shapes=_direct_exchange_scratch() if ns else [],
        compiler_params=_cp(dimension_semantics=("arbitrary",)))(*a_list, b, *sends)


def _rmsnorm_bwd(x, gain, dxn, dres, *, name, scale, tm=512):
    t, d = x.shape
    tm = min(tm, t)

    def body(x_ref, g_ref, dxn_ref, dres_ref, dx_ref, dxb_ref, dg_ref):
        xv = x_ref[...]
        rstd = lax.rsqrt(jnp.mean(xv * xv, axis=-1, keepdims=True) + RMS_EPS)
        xhat = xv * rstd
        dxn_v = dxn_ref[...]
        dxhat = dxn_v * g_ref[...]
        dx = dres_ref[...] + rstd * (dxhat - xhat * jnp.mean(dxhat * xhat, axis=-1, keepdims=True))
        dx_ref[...] = dx
        dxb_ref[...] = (dx * scale).astype(BF16)

        @pl.when(pl.program_id(0) == 0)
        def _():
            dg_ref[...] = jnp.zeros_like(dg_ref)

        dg_ref[...] += jnp.sum(dxn_v * xhat, axis=0, keepdims=True)

    row = pl.BlockSpec((tm, d), lambda i: (i, 0))
    vec = pl.BlockSpec((1, d), lambda i: (0, 0))
    return pl.pallas_call(
        body, name=name, grid=(t // tm,), in_specs=[row, vec, row, row], out_specs=[row, row, vec],
        out_shape=[jax.ShapeDtypeStruct((t, d), F32), jax.ShapeDtypeStruct((t, d), BF16), jax.ShapeDtypeStruct((1, d), F32)],
        compiler_params=_cp())(x, gain, dxn, dres)


def _loss_head(h, gain, target, *, name, tm=512):
    t, d = h.shape
    tm = min(tm, t)

    def body(h_ref, g_ref, t_ref, dh_ref, dhb_ref, dg_ref, loss_ref):
        hv = h_ref[...]
        rstd = lax.rsqrt(jnp.mean(hv * hv, axis=-1, keepdims=True) + RMS_EPS)
        xhat = hv * rstd
        err = xhat * g_ref[...] - t_ref[...]
        dy = err * (1.0 / d)
        dxhat = dy * g_ref[...]
        dh = rstd * (dxhat - xhat * jnp.mean(dxhat * xhat, axis=-1, keepdims=True))
        dh_ref[...] = dh
        dhb_ref[...] = (dh * MACARON).astype(BF16)

        @pl.when(pl.program_id(0) == 0)
        def _():
            dg_ref[...] = jnp.zeros_like(dg_ref)
            loss_ref[...] = jnp.zeros_like(loss_ref)

        dg_ref[...] += jnp.sum(dy * xhat, axis=0, keepdims=True)
        part = jnp.sum(jnp.sum(err * err, axis=-1, keepdims=True), axis=0, keepdims=True) * (0.5 / d)
        loss_ref[...] += jnp.broadcast_to(part, loss_ref.shape)

    row = pl.BlockSpec((tm, d), lambda i: (i, 0))
    vec = pl.BlockSpec((1, d), lambda i: (0, 0))
    return pl.pallas_call(
        body, name=name, grid=(t // tm,), in_specs=[row, vec, row],
        out_specs=[row, row, vec, pl.BlockSpec((1, LANES), lambda i: (0, 0))],
        out_shape=[jax.ShapeDtypeStruct((t, d), F32), jax.ShapeDtypeStruct((t, d), BF16), jax.ShapeDtypeStruct((1, d), F32),
                   jax.ShapeDtypeStruct((1, LANES), F32)],
        compiler_params=_cp())(h, gain, target)


def _norm_gate_up(x, gain, wg, wu, *, name, tm=256, tf=2816, pack=None, seg_rows=()):
    t, d = x.shape
    f = wg.shape[0]
    tm, tf = min(tm, t), min(tf, f)
    assert f % tf == 0
    ni, nj = t // tm, f // tf
    nseg = len(seg_rows)

    def body(x_ref, g_ref, wg_ref, wu_ref, *rest):
        if pack is not None:
            pack_ref, xn_ref, gg_ref, uu_ref, act_ref = rest[:5]
            start, forward, finish = _gather_phases(pack_ref, rest[5:5 + nseg], seg_rows, *rest[5 + nseg:])
            step = pl.program_id(0) * nj + pl.program_id(1)
            pl.when(step == 0)(start)
            pl.when(step == (3 * ni * nj) // 4)(forward)
        else:
            xn_ref, gg_ref, uu_ref, act_ref = rest

        @pl.when(pl.program_id(1) == 0)
        def _():
            xv = x_ref[...]
            rstd = lax.rsqrt(jnp.mean(xv * xv, axis=-1, keepdims=True) + RMS_EPS)
            xn_ref[...] = (xv * rstd * g_ref[...]).astype(BF16)

        xn = xn_ref[...]
        gv = _dot(xn, wg_ref[...], 1, 1)
        uv = _dot(xn, wu_ref[...], 1, 1)
        gg_ref[...] = gv.astype(BF16)
        uu_ref[...] = uv.astype(BF16)
        act_ref[...] = (gv * _sigmoid(gv) * uv).astype(BF16)
        if pack is not None:
            pl.when(step == ni * nj - 1)(finish)

    row = pl.BlockSpec((tm, d), lambda i, j: (i, 0))
    wsp = pl.BlockSpec((tf, d), lambda i, j: (j, 0))
    osp = pl.BlockSpec((tm, tf), lambda i, j: (i, j))
    fused = pack is not None
    return pl.pallas_call(
        body, name=name, grid=(ni, nj),
        in_specs=[row, pl.BlockSpec((1, d), lambda i, j: (0, 0)), wsp, wsp] + ([HBM_SPEC] if fused else []),
        out_specs=[row, osp, osp, osp] + [HBM_SPEC] * nseg,
        out_shape=[jax.ShapeDtypeStruct((t, d), BF16)] + [jax.ShapeDtypeStruct((t, f), BF16)] * 3
        + [jax.ShapeDtypeStruct((8, n, d), BF16) for n in seg_rows],
        scratch_shapes=_gather_scratch() if fused else [],
        compiler_params=_cp(dimension_semantics=("arbitrary", "arbitrary")))(x, gain, wg, wu, *([pack] if fused else []))


def _swiglu_bwd(dout, wd, gg, uu, chip_part=None, *, name, tm=512, tf=1408):
    t, d = dout.shape
    f = wd.shape[0]
    tm, tf = min(tm, t), min(tf, f)
    nj, ni = f // tf, t // tm
    fused = chip_part is not None

    def body(do_ref, wd_ref, g_ref, u_ref, *rest):
        if fused:
            part_ref, dg_ref, du_ref, parts_ref = rest[:4]
            start, finish = _chip_exchange_phases(part_ref, parts_ref, *rest[4:])
            step = pl.program_id(0) * ni + pl.program_id(1)
            pl.when(step == 0)(start)
        else:
            dg_ref, du_ref = rest
        dact = _dot(do_ref[...], wd_ref[...], 1, 1)
        gv = g_ref[...].astype(F32)
        uv = u_ref[...].astype(F32)
        sg = _sigmoid(gv)
        dg_ref[...] = (dact * uv * (sg * (1.0 + gv * (1.0 - sg)))).astype(BF16)
        du_ref[...] = (dact * (gv * sg)).astype(BF16)
        if fused:
            pl.when(step == nj * ni - 1)(finish)

    osp = pl.BlockSpec((tm, tf), lambda j, i: (i, j))
    return pl.pallas_call(
        body, name=name, grid=(nj, ni),
        in_specs=[pl.BlockSpec((tm, d), lambda j, i: (i, 0)), pl.BlockSpec((tf, d), lambda j, i: (j, 0)), osp, osp]
        + ([HBM_SPEC] if fused else []),
        out_specs=[osp, osp] + ([HBM_SPEC] if fused else []),
        out_shape=[jax.ShapeDtypeStruct((t, f), BF16)] * 2
        + ([jax.ShapeDtypeStruct(chip_part.shape, chip_part.dtype)] if fused else []),
        scratch_shapes=_chip_exchange_scratch() if fused else [],
        compiler_params=_cp(dimension_semantics=("arbitrary", "arbitrary")))(dout, wd, gg, uu, *([chip_part] if fused else []))


def _shift_down(x, n):
    rows = lax.broadcasted_iota(jnp.int32, x.shape, 0)
    return jnp.where(rows >= n, pltpu.roll(x, n, 0), 0.0)


def _shift_up(x, n):
    t = x.shape[0]
    rows = lax.broadcasted_iota(jnp.int32, x.shape, 0)
    return jnp.where(rows < t - n, pltpu.roll(x, t - n, 0), 0.0)


def _conv_fwd(pa, conv_w, *, name):
    t = pa.shape[0]
    nb = pa.shape[1] // 3 // LANES

    def body(b_ref, c_ref, x_ref, w_ref, y_ref):
        u = c_ref[...] * x_ref[...]
        w = w_ref[...]
        conv = w[2:3, :] * u + w[1:2, :] * _shift_down(u, 1) + w[0:1, :] * _shift_down(u, 2)
        y_ref[...] = (b_ref[...] * conv).astype(BF16)

    def col(off):
        return pl.BlockSpec((t, LANES), lambda j: (0, off + j))

    return pl.pallas_call(
        body, name=name, grid=(nb,),
        in_specs=[col(0), col(nb), col(2 * nb), pl.BlockSpec((3, LANES), lambda j: (0, j))],
        out_specs=pl.BlockSpec((t, LANES), lambda j: (0, j)),
        out_shape=jax.ShapeDtypeStruct((t, nb * LANES), BF16), compiler_params=_cp())(pa, pa, pa, conv_w)


def _conv_bwd(pa, dy, conv_w, *, name):
    t = pa.shape[0]
    nb = pa.shape[1] // 3 // LANES

    def body(b_ref, c_ref, x_ref, dy_ref, w_ref, db_ref, dc_ref, dx_ref, dw_ref):
        cv, xv = c_ref[...], x_ref[...]
        u = cv * xv
        u1, u2 = _shift_down(u, 1), _shift_down(u, 2)
        w = w_ref[...]
        conv = w[2:3, :] * u + w[1:2, :] * u1 + w[0:1, :] * u2
        dyv = dy_ref[...]
        db_ref[...] = (dyv * conv).astype(BF16)
        dconv = dyv * b_ref[...]
        du = w[2:3, :] * dconv + w[1:2, :] * _shift_up(dconv, 1) + w[0:1, :] * _shift_up(dconv, 2)
        dc_ref[...] = (du * xv).astype(BF16)
        dx_ref[...] = (du * cv).astype(BF16)
        dw_ref[0:1, :] = jnp.sum(dconv * u2, axis=0, keepdims=True)
        dw_ref[1:2, :] = jnp.sum(dconv * u1, axis=0, keepdims=True)
        dw_ref[2:3, :] = jnp.sum(dconv * u, axis=0, keepdims=True)

    def col(off):
        return pl.BlockSpec((t, LANES), lambda j: (0, off + j))

    osp = pl.BlockSpec((t, LANES), lambda j: (0, j))
    wsp = pl.BlockSpec((3, LANES), lambda j: (0, j))
    return pl.pallas_call(
        body, name=name, grid=(nb,), in_specs=[col(0), col(nb), col(2 * nb), col(0), wsp],
        out_specs=[osp, osp, osp, wsp],
        out_shape=[jax.ShapeDtypeStruct((t, nb * LANES), BF16)] * 3 + [jax.ShapeDtypeStruct((3, nb * LANES), F32)],
        compiler_params=_cp())(pa, pa, pa, dy, conv_w)


def _sb_consts():
    j = lax.broadcasted_iota(jnp.int32, (SB_KEYS, SB_KEYS), 0)
    s = lax.broadcasted_iota(jnp.int32, (SB_KEYS, SB_KEYS), 1)
    after = (j > s).astype(BF16)
    upto = (j <= s).astype(BF16)
    before = (j < s).astype(BF16)
    return after, jnp.stack([upto, before])


def _log_sigmoid(z):
    return jnp.minimum(z, 0.0) - jnp.log(1.0 + jnp.exp(-jnp.abs(z)))


def _attn_fwd(pb, late_pack, seg_rows, *, name, tq=256):
    t = pb.shape[0]
    npair = pb.shape[1] // 3 // LANES
    tq = min(tq, t)
    nq = t // tq
    cmat, _ = _sb_consts()
    scale = 1.0 / math.sqrt(LANES // 2)

    nseg = len(seg_rows)

    def body(q_ref, k_ref, v_ref, c_ref, late_ref, y_ref, lt_ref, *rest):
        i = pl.program_id(1)
        pair = pl.program_id(0)
        scratch = rest[nseg:nseg + 4]
        start, forward, finish = _gather_phases(late_ref, rest[:nseg], seg_rows, *rest[nseg + 4:])
        pl.when((pair == 0) & (i == 0))(start)
        pl.when((pair == npair - 1) & (i == nq // 2))(forward)
        lane = lax.broadcasted_iota(jnp.int32, (tq, LANES), 1)
        rowpos = i * tq + lax.broadcasted_iota(jnp.int32, (tq, SB_KEYS), 0)
        colid = lax.broadcasted_iota(jnp.int32, (tq, SB_KEYS), 1)
        q2 = q_ref[...] * jnp.asarray(scale, BF16)
        cm = c_ref[...]
        hi_lanes = lane >= LANES // 2
        qhs = [jnp.where(hi_lanes == (hh == 1), q2, jnp.zeros_like(q2)) for hh in range(2)]
        per_q = tq // SB_KEYS

        def blk(jb):
            return pl.ds(pl.multiple_of(jb * SB_KEYS, SB_KEYS), SB_KEYS)

        zbuf, wbuf, accbuf, runbuf = scratch

        def scores(jb):
            kb = k_ref[blk(jb), :]
            for hh in range(2):
                zbuf[hh] = _dot(qhs[hh], kb, 1, 1)

        def values(jb):
            vb = v_ref[blk(jb), :]
            for hh in range(2):
                accbuf[hh] += _dot(wbuf[hh], vb)

        def trip(jb, masked, first=False):
            mask = (jb * SB_KEYS + colid) < rowpos if masked else None
            if not first:
                values(jb + 1)
            pre, css = [], []
            for hh in range(2):
                z = zbuf[hh]
                lb = _log_sigmoid(z)
                lk = lb - z
                if masked:
                    lk = jnp.where(mask, lk, 0.0)
                lk_hi, lk_lo = _split2(lk)
                css.append(_dot(lk_hi, cm) + _dot(lk_lo, cm))
                run = runbuf[hh]
                pre.append(lb + run)
                runbuf[hh] = run + jnp.sum(lk, axis=1, keepdims=True)
            scores(jnp.maximum(jb - 1, 0))
            for hh in range(2):
                w = jnp.exp(pre[hh] + css[hh])
                if masked:
                    w = jnp.where(mask, w, 0.0)
                wbuf[hh] = w.astype(BF16)

        nfull = i * per_q
        accbuf[...] = jnp.zeros_like(accbuf)
        runbuf[...] = jnp.zeros_like(runbuf)
        scores(nfull + per_q - 1)
        for dblk in reversed(range(per_q)):
            trip(nfull + dblk, True, first=dblk == per_q - 1)

        def full_block(n, carry):
            trip(nfull - 1 - n, False)
            return carry

        lax.fori_loop(0, nfull, full_block, 0)
        values(0)
        y_ref[...] = jnp.where(hi_lanes, accbuf[1], accbuf[0]).astype(BF16)
        lt_ref[...] = jnp.where(hi_lanes, runbuf[1], runbuf[0])
        pl.when((pair == npair - 1) & (i == nq - 1))(finish)

    return pl.pallas_call(
        body, name=name, grid=(npair, nq),
        in_specs=[pl.BlockSpec((tq, LANES), lambda p, i: (i, p)),
                  pl.BlockSpec((t, LANES), lambda p, i: (0, npair + p)),
                  pl.BlockSpec((t, LANES), lambda p, i: (0, 2 * npair + p)),
                  pl.BlockSpec((SB_KEYS, SB_KEYS), lambda p, i: (0, 0)),
                  HBM_SPEC],
        out_specs=[pl.BlockSpec((tq, LANES), lambda p, i: (i, p))] * 2 + [HBM_SPEC] * nseg,
        out_shape=[jax.ShapeDtypeStruct((t, npair * LANES), BF16), jax.ShapeDtypeStruct((t, npair * LANES), F32),
                   ] + [jax.ShapeDtypeStruct((8, n, late_pack.shape[1]), late_pack.dtype) for n in seg_rows],
        scratch_shapes=[pltpu.VMEM((2, tq, SB_KEYS), F32), pltpu.VMEM((2, tq, SB_KEYS), BF16),
                        pltpu.VMEM((2, tq, LANES), F32), pltpu.VMEM((2, tq, 1), F32)] + _gather_scratch(),
        compiler_params=_cp(dimension_semantics=("arbitrary", "arbitrary")))(pb, pb, pb, cmat, late_pack)


def _attn_bwd(pb, dy, ltot, send, *, name, tq=256):
    t = pb.shape[0]
    npair = pb.shape[1] // 3 // LANES
    tq = min(tq, t)
    nq = t // tq
    _, cmats = _sb_consts()
    scale = 1.0 / math.sqrt(LANES // 2)
    sends = list(send)
    ns = len(sends)

    def body(q_ref, k_ref, v_ref, dy_ref, lt_ref, c_ref, *rest):
        i = pl.program_id(1)
        pair = pl.program_id(0)
        send_refs = rest[:ns]
        dq_ref, dk_ref, dv_ref, parts_ref, dk_acc, dv_acc = rest[ns:ns + 6]
        scratch = rest[ns + 6:ns + 12]
        start, finish = _direct_exchange_phases(send_refs, parts_ref, *rest[ns + 12:])
        pl.when((pair == 0) & (i == 0))(start)

        @pl.when(i == 0)
        def _():
            dk_acc[...] = jnp.zeros_like(dk_acc)
            dv_acc[...] = jnp.zeros_like(dv_acc)

        lane = lax.broadcasted_iota(jnp.int32, (tq, LANES), 1)
        rowpos = i * tq + lax.broadcasted_iota(jnp.int32, (tq, SB_KEYS), 0)
        colid = lax.broadcasted_iota(jnp.int32, (tq, SB_KEYS), 1)
        q2 = q_ref[...] * jnp.asarray(scale, BF16)
        do2 = dy_ref[...].astype(BF16)
        ltv = lt_ref[...]
        c_upto, c_before = c_ref[0], c_ref[1]
        hi_lanes = lane >= LANES // 2
        sels = [hi_lanes == (hh == 1) for hh in range(2)]
        qhs = [jnp.where(s, q2, jnp.zeros_like(q2)) for s in sels]
        dohs = [jnp.where(s, do2, jnp.zeros_like(do2)) for s in sels]
        lts = [ltv[:, 0:1], ltv[:, LANES // 2:LANES // 2 + 1]]
        per_q = tq // SB_KEYS

        def blk(jb):
            return pl.ds(pl.multiple_of(jb * SB_KEYS, SB_KEYS), SB_KEYS)

        zbuf, dabuf, dzbuf, abuf, dqbuf, sumbuf = scratch

        def scores(jb):
            kb, vb = k_ref[blk(jb), :], v_ref[blk(jb), :]
            for hh in range(2):
                zbuf[hh] = _dot(qhs[hh], kb, 1, 1)
                dabuf[hh] = _dot(dohs[hh], vb, 1, 1)

        def products(jb):
            kb = k_ref[blk(jb), :]
            dk_acc[blk(jb), :] += _dot(dzbuf[0], qhs[0], 0, 0) + _dot(dzbuf[1], qhs[1], 0, 0)
            dv_acc[blk(jb), :] += _dot(abuf[0], dohs[0], 0, 0) + _dot(abuf[1], dohs[1], 0, 0)
            for hh in range(2):
                dqbuf[hh] += _dot(dzbuf[hh], kb)

        def trip(jb, masked):
            mask = (jb * SB_KEYS + colid) < rowpos if masked else None
            products(jnp.maximum(jb - 1, 0))
            lbs, css, es, ces = [], [], [], []
            for hh in range(2):
                z = zbuf[hh]
                lb = _log_sigmoid(z)
                lk = lb - z
                if masked:
                    lk = jnp.where(mask, lk, 0.0)
                lk_hi, lk_lo = _split2(lk)
                css.append(_dot(lk_hi, c_upto) + _dot(lk_lo, c_upto))
                csum = sumbuf[2 * hh]
                lbs.append((lb, lb + (lts[hh] - csum)))
                sumbuf[2 * hh] = csum + jnp.sum(lk, axis=1, keepdims=True)
            for hh in range(2):
                a = jnp.exp(lbs[hh][1] - css[hh])
                if masked:
                    a = jnp.where(mask, a, 0.0)
                e = a * dabuf[hh]
                e_hi, e_lo = _split2(e)
                ces.append(_dot(e_hi, c_before) + _dot(e_lo, c_before))
                abuf[hh] = a.astype(BF16)
                es.append(e)
            scores(jnp.minimum(jb + 1, last))
            for hh in range(2):
                prun = sumbuf[2 * hh + 1]
                beta = jnp.exp(lbs[hh][0])
                dz = es[hh] * (1.0 - beta) - (prun + ces[hh]) * beta
                if masked:
                    dz = jnp.where(mask, dz, 0.0)
                dzbuf[hh] = dz.astype(BF16)
                sumbuf[2 * hh + 1] = prun + jnp.sum(es[hh], axis=1, keepdims=True)

        nfull = i * per_q
        last = nfull + per_q - 1
        for buf in (dzbuf, abuf, dqbuf, sumbuf):
            buf[...] = jnp.zeros_like(buf)
        scores(0)

        def full_block(jb, carry):
            trip(jb, False)
            return carry

        lax.fori_loop(0, nfull, full_block, 0)
        for dblk in range(per_q):
            trip(nfull + dblk, True)
        products(last)
        dq_ref[...] = (jnp.where(hi_lanes, dqbuf[1], dqbuf[0]) * scale).astype(BF16)

        @pl.when(i == nq - 1)
        def _():
            dk_ref[...] = dk_acc[...].astype(BF16)
            dv_ref[...] = dv_acc[...].astype(BF16)

        pl.when((pair == npair - 1) & (i == nq - 1))(finish)

    blk = pl.BlockSpec((tq, LANES), lambda p, i: (i, p))
    full = pl.BlockSpec((t, LANES), lambda p, i: (0, p))
    return pl.pallas_call(
        body, name=name, grid=(npair, nq),
        in_specs=[blk,
                  pl.BlockSpec((t, LANES), lambda p, i: (0, npair + p)),
                  pl.BlockSpec((t, LANES), lambda p, i: (0, 2 * npair + p)),
                  pl.BlockSpec((tq, LANES), lambda p, i: (i, npair + p)),
                  blk,
                  pl.BlockSpec((2, SB_KEYS, SB_KEYS), lambda p, i: (0, 0, 0))] + [HBM_SPEC] * ns,
        out_specs=[blk, full, full, HBM_SPEC],
        out_shape=[jax.ShapeDtypeStruct((t, npair * LANES), BF16)] * 3
        + [jax.ShapeDtypeStruct((8, _direct_exchange_rows(sends), sends[0].shape[3]), sends[0].dtype)],
        scratch_shapes=[pltpu.VMEM((t, LANES), F32), pltpu.VMEM((t, LANES), F32),
                        pltpu.VMEM((2, tq, SB_KEYS), F32), pltpu.VMEM((2, tq, SB_KEYS), F32),
                        pltpu.VMEM((2, tq, SB_KEYS), BF16), pltpu.VMEM((2, tq, SB_KEYS), BF16),
                        pltpu.VMEM((2, tq, LANES), F32), pltpu.VMEM((4, tq, 1), F32)] + _direct_exchange_scratch(),
        compiler_params=_cp(dimension_semantics=("arbitrary", "arbitrary")))(pb, pb, pb, dy, ltot, cmats, *sends)


def _hgrn_consts():
    t = lax.broadcasted_iota(jnp.int32, (CHUNK, CHUNK), 0)
    s = lax.broadcasted_iota(jnp.int32, (CHUNK, CHUNK), 1)
    masks = []
    for lvl in range(N_LEVELS):
        half = CHUNK >> (lvl + 1)
        same = (t // (2 * half)) == (s // (2 * half))
        masks.append((same & (t % (2 * half) >= half) & (s % (2 * half) < half)).astype(F32))
    masks.append((t == s).astype(F32))
    prefix = (s <= t).astype(BF16)
    suffix = (s >= t).astype(BF16)
    return prefix, jnp.stack(masks), suffix


def _hgrn_gates(qr, fr, lbv):
    sg = 1.0 / (1.0 + jnp.exp(-fr))
    fval = lbv + (1.0 - lbv) * sg
    kk = (1.0 - lbv) * (1.0 / (1.0 + jnp.exp(fr)))
    sq = _sigmoid(qr)
    return sg, fval, jnp.log(fval), kk, sq, qr * sq


def _lower_bound(c_ref):
    c = c_ref[...]
    mx = jnp.max(c, axis=0, keepdims=True)
    ex = jnp.exp(c - mx)
    return ex[1:2, :] / jnp.sum(ex, axis=0, keepdims=True)


def _level_ref(b, lvl):
    half = CHUNK >> (lvl + 1)
    seg = 2 * half
    if seg >= 8:
        b3 = b.reshape(CHUNK // seg, seg, LANES)
        return jnp.broadcast_to(b3[:, half - 1:half, :], b3.shape).reshape(CHUNK, LANES)
    pos = lax.broadcasted_iota(jnp.int32, b.shape, 0) % seg
    out = b
    for p in range(seg):
        if p != half - 1:
            out = jnp.where(pos == p, pltpu.roll(b, (p - (half - 1)) % CHUNK, 0), out)
    return out


def _hgrn_levels(b, qs, kk):
    out = []
    for lvl in range(N_LEVELS):
        fac = jnp.exp(-jnp.abs(b - _level_ref(b, lvl)))
        out.append((qs * fac, kk * fac, fac, fac))
    out.append((qs, kk, None, None))
    return out


def _split2(x):
    hi = x.astype(BF16)
    return hi, (x - hi.astype(F32)).astype(BF16)


def _hgrn_fwd(pc, c_lb, out_norm, *, name, tc=512):
    t = pc.shape[0]
    nh = pc.shape[1] // 4 // LANES
    tc = min(tc, t)
    nch = tc // CHUNK
    cum_all, masks, _ = _hgrn_consts()

    def body(q_ref, f_ref, i_ref, g_ref, lb_ref, on_ref, cum_ref, m_ref, y_ref, o_ref, st_ref, state):
        @pl.when(pl.program_id(1) == 0)
        def _():
            state[...] = jnp.zeros_like(state)

        lbv = _lower_bound(lb_ref)
        onv = on_ref[...]

        def chunk(c, carry):
            rows = pl.ds(pl.multiple_of(c * CHUNK, CHUNK), CHUNK)
            for hh in range(HGRN_HEADS):
                lanes = slice(hh * LANES, (hh + 1) * LANES)
                _, _, g, kk, _, qs = _hgrn_gates(q_ref[rows, lanes], f_ref[rows, lanes], lbv[:, lanes])
                vb = i_ref[rows, lanes].astype(BF16)
                b = _dot_exact_lhs(cum_ref[...], g)
                scores = jnp.zeros((CHUNK, CHUNK), F32)
                for lvl, (ql, kl, _, _) in enumerate(_hgrn_levels(b, qs, kk)):
                    scores = scores + _dot(ql.astype(BF16), kl.astype(BF16), 1, 1) * m_ref[lvl]
                st = state[hh]
                st_ref[hh, c] = st
                o = _dot(scores.astype(BF16), vb) + _dot((qs * jnp.exp(b)).astype(BF16), st.astype(BF16), 1, 1)
                blast = b[CHUNK - 1:CHUNK, :]
                kdec = (kk * jnp.exp(blast - b)).astype(BF16)
                state[hh] = st * jnp.exp(blast) + _dot(vb, kdec, 0, 0)
                o_ref[rows, lanes] = o
                rstd = lax.rsqrt(jnp.mean(o * o, axis=-1, keepdims=True) + RMS_EPS)
                gate = g_ref[rows, lanes]
                y_ref[rows, lanes] = (o * rstd * onv * (gate * _sigmoid(gate))).astype(BF16)
            return carry

        lax.fori_loop(0, nch, chunk, 0, unroll=2)

    hw = HGRN_HEADS * LANES

    def col(off):
        return pl.BlockSpec((tc, hw), lambda h, i: (i, off // HGRN_HEADS + h))

    osp = pl.BlockSpec((tc, hw), lambda h, i: (i, h))
    return pl.pallas_call(
        body, name=name, grid=(nh // HGRN_HEADS, t // tc),
        in_specs=[col(0), col(nh), col(2 * nh), col(3 * nh),
                  pl.BlockSpec((2, hw), lambda h, i: (0, h)),
                  pl.BlockSpec((1, LANES), lambda h, i: (0, 0)),
                  pl.BlockSpec(cum_all.shape, lambda h, i: (0, 0)),
                  pl.BlockSpec(masks.shape, lambda h, i: (0, 0, 0))],
        out_specs=[osp, osp, pl.BlockSpec((HGRN_HEADS, nch, LANES, LANES), lambda h, i: (h, i, 0, 0))],
        out_shape=[jax.ShapeDtypeStruct((t, nh * LANES), BF16), jax.ShapeDtypeStruct((t, nh * LANES), F32),
                   jax.ShapeDtypeStruct((nh, t // CHUNK, LANES, LANES), F32)],
        scratch_shapes=[pltpu.VMEM((HGRN_HEADS, LANES, LANES), F32)],
        compiler_params=_cp())(pc, pc, pc, pc, c_lb, out_norm, cum_all, masks)


def _hgrn_bwd(pc, o_saved, states, dy, c_lb, out_norm, send, *, name, tc=512):
    t = pc.shape[0]
    nh = pc.shape[1] // 4 // LANES
    tc = min(tc, t)
    nch = tc // CHUNK
    nt = t // tc
    cum_all, masks, suffix = _hgrn_consts()
    ngroup = nh // HGRN_HEADS
    sends = list(send)
    ns = len(sends)

    def body(q_ref, f_ref, i_ref, g_ref, o_ref, st_ref, dy_ref, lb_ref, on_ref, cum_ref, m_ref, suf_ref, *rest):
        send_refs = rest[:ns]
        dq_ref, df_ref, di_ref, dg_ref, dlb_ref, don_ref, parts_ref, dstate = rest[ns:ns + 8]
        start, finish = _direct_exchange_phases(send_refs, parts_ref, *rest[ns + 8:])
        pl.when((pl.program_id(0) == 0) & (pl.program_id(1) == 0))(start)

        @pl.when(pl.program_id(1) == 0)
        def _():
            dstate[...] = jnp.zeros_like(dstate)
            dlb_ref[...] = jnp.zeros_like(dlb_ref)
            don_ref[...] = jnp.zeros_like(don_ref)

        lbv = _lower_bound(lb_ref)
        onv = on_ref[...]

        def head(hh, c, rows):
            lanes = slice(hh * LANES, (hh + 1) * LANES)
            qr = q_ref[rows, lanes]
            sg, fval, g, kk, sq, qs = _hgrn_gates(qr, f_ref[rows, lanes], lbv[:, lanes])
            vb = i_ref[rows, lanes].astype(BF16)
            o = o_ref[rows, lanes]
            gate = g_ref[rows, lanes]
            sgt = _sigmoid(gate)
            rstd = lax.rsqrt(jnp.mean(o * o, axis=-1, keepdims=True) + RMS_EPS)
            ohat = o * rstd
            dyv = dy_ref[rows, lanes]
            don = dyv * (gate * sgt)
            dg_ref[rows, lanes] = (dyv * ohat * onv * (sgt * (1.0 + gate * (1.0 - sgt)))).astype(BF16)
            don_ref[:, lanes] += jnp.sum(don * ohat, axis=0, keepdims=True)
            dxhat = don * onv
            dob = (rstd * (dxhat - ohat * jnp.mean(dxhat * ohat, axis=-1, keepdims=True))).astype(BF16)
            b = _dot_exact_lhs(cum_ref[...], g)
            blast = b[CHUNK - 1:CHUNK, :]
            eb = jnp.exp(b)
            edec = jnp.exp(blast - b)
            st32 = st_ref[hh, c]
            st = st32.astype(BF16)
            dst = dstate[hh]
            dstb = dst.astype(BF16)
            da = _dot(dob, vb, 1, 1)
            levels = _hgrn_levels(b, qs, kk)
            scores = jnp.zeros((CHUNK, CHUNK), F32)
            dq = eb * _dot(dob, st)
            dk_inter = edec * _dot(vb, dstb)
            dk = dk_inter
            for lvl, (ql, kl, eq, ek) in enumerate(levels):
                mk = m_ref[lvl]
                (qh, qlo), (kh, klo) = _split2(ql), _split2(kl)
                scores = scores + _dot(qh, kh, 1, 1) * mk
                dal = (da * mk).astype(BF16)
                dql = _dot(dal, kh) + _dot(dal, klo)
                dkl = _dot(dal, qh, 0, 0) + _dot(dal, qlo, 0, 0)
                dq = dq + (dql if eq is None else dql * eq)
                dk = dk + (dkl if ek is None else dkl * ek)
            kdec = (kk * edec).astype(BF16)
            dv = _dot(scores.astype(BF16), dob, 0, 0) + _dot(kdec, dstb, 1, 1)
            dstate[hh] = dst * jnp.exp(blast) + _dot(dob, (qs * eb).astype(BF16), 0, 0)
            db = qs * dq - kk * dk
            last = jnp.sum(kk * dk_inter, axis=0, keepdims=True) + jnp.exp(blast) * jnp.sum(dst * st32, axis=0, keepdims=True)
            dgl = _dot_exact_lhs(suf_ref[...], db) + last
            dfv = dgl / fval - dk
            df_ref[rows, lanes] = (dfv * (1.0 - lbv[:, lanes]) * sg * (1.0 - sg)).astype(BF16)
            dlb_ref[:, lanes] += jnp.sum(dfv * (1.0 - sg), axis=0, keepdims=True)
            dq_ref[rows, lanes] = (dq * (sq * (1.0 + qr * (1.0 - sq)))).astype(BF16)
            di_ref[rows, lanes] = dv.astype(BF16)

        def chunk(n, carry):
            c = nch - 1 - n
            rows = pl.ds(pl.multiple_of(c * CHUNK, CHUNK), CHUNK)
            for hh in range(HGRN_HEADS):
                head(hh, c, rows)
            return carry

        lax.fori_loop(0, nch, chunk, 0, unroll=2)
        pl.when((pl.program_id(0) == ngroup - 1) & (pl.program_id(1) == nt - 1))(finish)

    hw = HGRN_HEADS * LANES

    def col(off):
        return pl.BlockSpec((tc, hw), lambda h, i: (nt - 1 - i, off // HGRN_HEADS + h))

    osp = pl.BlockSpec((tc, hw), lambda h, i: (nt - 1 - i, h))
    vec = pl.BlockSpec((1, hw), lambda h, i: (0, h))
    return pl.pallas_call(
        body, name=name, grid=(nh // HGRN_HEADS, nt),
        in_specs=[col(0), col(nh), col(2 * nh), col(3 * nh), osp,
                  pl.BlockSpec((HGRN_HEADS, nch, LANES, LANES), lambda h, i: (h, nt - 1 - i, 0, 0)),
                  osp,
                  pl.BlockSpec((2, hw), lambda h, i: (0, h)),
                  pl.BlockSpec((1, LANES), lambda h, i: (0, 0)),
                  pl.BlockSpec(cum_all.shape, lambda h, i: (0, 0)),
                  pl.BlockSpec(masks.shape, lambda h, i: (0, 0, 0)),
                  pl.BlockSpec(suffix.shape, lambda h, i: (0, 0))] + [HBM_SPEC] * ns,
        out_specs=[osp, osp, osp, osp, vec, vec, HBM_SPEC],
        out_shape=[jax.ShapeDtypeStruct((t, nh * LANES), BF16)] * 4 + [jax.ShapeDtypeStruct((1, nh * LANES), F32)] * 2
        + [jax.ShapeDtypeStruct((8, _direct_exchange_rows(sends), sends[0].shape[3]), sends[0].dtype)],
        scratch_shapes=[pltpu.VMEM((HGRN_HEADS, LANES, LANES), F32)] + _direct_exchange_scratch(),
        compiler_params=_cp(dimension_semantics=("arbitrary", "arbitrary")))(
            pc, pc, pc, pc, o_saved, states, dy, c_lb, out_norm, cum_all, masks, suffix, *sends)


HBM_SPEC = pl.BlockSpec(memory_space=pltpu.HBM)


def _gather_scratch():
    return [pltpu.SemaphoreType.DMA((7,)), pltpu.SemaphoreType.DMA((7,)), pltpu.SemaphoreType.DMA]


def _gather_phases(x_ref, out_refs, seg_rows, send_sems, recv_sems, local_sem):
    x, y, c = lax.axis_index("x"), lax.axis_index("y"), lax.axis_index("c")
    me, sibling = (x, y, c), (x, y, 1 - c)
    chips = [(1 - x, y), (x, 1 - y), (1 - x, 1 - y)]
    offs = [sum(seg_rows[:s]) for s in range(len(seg_rows))]
    assert sum(seg_rows) == x_ref.shape[0]

    def index(px, py, pc):
        return 4 * px + 2 * py + pc

    def copies(k, block, to, own):
        return [pltpu.make_async_remote_copy(
            src_ref=x_ref.at[pl.ds(offs[s], n)] if own else out_refs[s].at[index(*block)],
            dst_ref=out_refs[s].at[index(*block)],
            send_sem=send_sems.at[k], recv_sem=recv_sems.at[k], device_id=to, device_id_type=MESH)
            for s, n in enumerate(seg_rows)]

    def all_bytes(k):
        return pltpu.make_async_remote_copy(src_ref=x_ref, dst_ref=x_ref, send_sem=send_sems.at[k],
                                            recv_sem=recv_sems.at[k], device_id=me, device_id_type=MESH)

    mine = [pltpu.make_async_copy(x_ref.at[pl.ds(offs[s], n)], out_refs[s].at[index(*me)], local_sem)
            for s, n in enumerate(seg_rows)]
    first = copies(0, me, sibling, True)
    for j, chip in enumerate(chips):
        first += copies(1 + j, me, (*chip, c), True)

    def start():
        for cp in mine + first:
            cp.start()

    def forward():
        for j, chip in enumerate(chips):
            all_bytes(1 + j).wait_recv()
            for cp in copies(4 + j, (*chip, c), sibling, False):
                cp.start()

    def finish():
        all_bytes(0).wait_recv()
        for j in range(3):
            all_bytes(4 + j).wait_recv()
        for k in range(7):
            all_bytes(k).wait_send()
        pltpu.make_async_copy(x_ref, x_ref, local_sem).wait()

    return start, forward, finish


def _all_gather(xs, seg_rows=None, *, name):
    segs = [xs.shape[0]] if seg_rows is None else list(seg_rows)

    def body(x_ref, *rest):
        start, forward, finish = _gather_phases(x_ref, rest[:len(segs)], segs, *rest[len(segs):])
        start()
        forward()
        finish()

    outs = pl.pallas_call(
        body, name=name, in_specs=[HBM_SPEC], out_specs=[HBM_SPEC] * len(segs),
        out_shape=[jax.ShapeDtypeStruct((8, n, xs.shape[1]), xs.dtype) for n in segs],
        scratch_shapes=_gather_scratch())(xs)
    return outs[0] if seg_rows is None else outs


def _sibling_exchange(s, *, name):
    def body(s_ref, rb_ref, send_sem, recv_sem):
        x, y, c = lax.axis_index("x"), lax.axis_index("y"), lax.axis_index("c")
        cp = pltpu.make_async_remote_copy(
            src_ref=s_ref.at[:, 1 - c], dst_ref=rb_ref, send_sem=send_sem, recv_sem=recv_sem,
            device_id=(x, y, 1 - c), device_id_type=MESH)
        cp.start()
        cp.wait()

    return pl.pallas_call(
        body, name=name, in_specs=[HBM_SPEC], out_specs=HBM_SPEC,
        out_shape=jax.ShapeDtypeStruct(s.shape[:1] + s.shape[2:], s.dtype),
        scratch_shapes=[pltpu.SemaphoreType.DMA, pltpu.SemaphoreType.DMA])(s)


def _row_tile(n, cap=1024):
    return max(b for b in range(16, cap + 1, 16) if n % b == 0)


def _pair_add(s, rb, core, *, name):
    nchip, _, r, c = s.shape
    tb = _row_tile(r)

    def body(core_ref, a_ref, b_ref, o_ref):
        o_ref[...] = (a_ref[...].astype(F32) + b_ref[...].astype(F32)).astype(BF16)

    blk = pl.BlockSpec((None, tb, c), lambda ch, i, cr: (ch, i, 0))
    return pl.pallas_call(
        body, name=name,
        grid_spec=pltpu.PrefetchScalarGridSpec(
            num_scalar_prefetch=1, grid=(nchip, r // tb),
            in_specs=[pl.BlockSpec((None, None, tb, c), lambda ch, i, cr: (ch, cr[0], i, 0)), blk],
            out_specs=blk),
        out_shape=jax.ShapeDtypeStruct((nchip, r, c), BF16), compiler_params=_cp())(core, s, rb)


def _chip_exchange_scratch():
    return [pltpu.SemaphoreType.DMA((3,)), pltpu.SemaphoreType.DMA((3,)), pltpu.SemaphoreType.DMA]


def _chip_exchange_phases(p_ref, out_ref, send_sems, recv_sems, local_sem):
    x, y, c = lax.axis_index("x"), lax.axis_index("y"), lax.axis_index("c")
    mine = 2 * x + y
    own = pltpu.make_async_copy(p_ref.at[mine], out_ref.at[mine], local_sem)
    copies = [pltpu.make_async_remote_copy(
        src_ref=p_ref.at[2 * tx + ty], dst_ref=out_ref.at[mine],
        send_sem=send_sems.at[k], recv_sem=recv_sems.at[k], device_id=(tx, ty, c), device_id_type=MESH)
        for k, (tx, ty) in enumerate([(1 - x, y), (x, 1 - y), (1 - x, 1 - y)])]

    def start():
        own.start()
        for cp in copies:
            cp.start()

    def finish():
        for cp in copies:
            cp.wait()
        own.wait()

    return start, finish


def _direct_exchange_scratch():
    return [pltpu.SemaphoreType.DMA((7,)), pltpu.SemaphoreType.DMA((7,)), pltpu.SemaphoreType.DMA]


def _direct_exchange_rows(sends):
    return sum(s.shape[2] for s in sends)


def _direct_exchange_phases(s_refs, out_ref, send_sems, recv_sems, local_sem):
    x, y, c = lax.axis_index("x"), lax.axis_index("y"), lax.axis_index("c")
    me = 4 * x + 2 * y + c
    offs, off = [], 0
    for s in s_refs:
        offs.append(off)
        off += s.shape[2]

    def slot(p):
        return out_ref.at[me, pl.ds(offs[p], s_refs[p].shape[2])]

    own = [pltpu.make_async_copy(s.at[2 * x + y, c], slot(p), local_sem) for p, s in enumerate(s_refs)]
    flips = [(fx, fy, fc) for fx in (0, 1) for fy in (0, 1) for fc in (0, 1) if (fx, fy, fc) != (0, 0, 0)]
    copies = []
    for k, (fx, fy, fc) in enumerate(flips):
        tx, ty, tc = (1 - x if fx else x), (1 - y if fy else y), (1 - c if fc else c)
        copies += [pltpu.make_async_remote_copy(
            src_ref=s.at[2 * tx + ty, tc], dst_ref=slot(p),
            send_sem=send_sems.at[k], recv_sem=recv_sems.at[k], device_id=(tx, ty, tc), device_id_type=MESH)
            for p, s in enumerate(s_refs)]

    def start():
        for cp in own + copies:
            cp.start()

    def finish():
        whole = out_ref.at[me]
        for k in range(len(flips)):
            pltpu.make_async_remote_copy(src_ref=whole, dst_ref=whole, send_sem=send_sems.at[k],
                                         recv_sem=recv_sems.at[k], device_id=(x, y, c), device_id_type=MESH).wait()
        pltpu.make_async_copy(whole, whole, local_sem).wait()

    return start, finish


def _adamw_math(w, g, m, v):
    m2 = ADAM_B1 * m + (1.0 - ADAM_B1) * g
    v2 = ADAM_B2 * v + (1.0 - ADAM_B2) * (g * g)
    m_hat = m2 / (1.0 - ADAM_B1 ** ADAM_STEP)
    v_hat = v2 / (1.0 - ADAM_B2 ** ADAM_STEP)
    return -ADAM_LR * (m_hat / (jnp.sqrt(v_hat) + ADAM_EPS) + ADAM_WD * w), m2, v2


def _adamw_shard(parts, g_off, w, m, v, layer, prev, *, name):
    _, r, c = w.shape
    npart = parts.shape[0]
    tb = next(b for b in range(min(r, 512), 0, -16) if r % b == 0 and g_off % b == 0)

    def body(*refs):
        w_ref, m_ref, v_ref = refs[npart:npart + 3]
        g_out, d_out, m_out, v_out = refs[-4:]
        g = refs[0][...].astype(F32)
        for p_ref in refs[1:npart]:
            g = g + p_ref[...].astype(F32)
        d, m2, v2 = _adamw_math(w_ref[...], g, m_ref[...], v_ref[...])
        g_out[...] = g
        d_out[...] = d
        m_out[...] = m2
        v_out[...] = v2

    def part(ch):
        return pl.BlockSpec((None, tb, c), lambda i: (ch, g_off // tb + i, 0))

    blk = pl.BlockSpec((None, tb, c), lambda i: (layer, i, 0))
    prev = list(prev) if prev is not None else []
    return pl.pallas_call(
        body, name=name, grid=(r // tb,),
        in_specs=[part(ch) for ch in range(npart)] + [blk, blk, blk] + [pl.BlockSpec(memory_space=pl.ANY)] * len(prev),
        out_specs=[blk] * 4, out_shape=[jax.ShapeDtypeStruct(w.shape, F32)] * 4,
        input_output_aliases={npart + 3 + k: k for k in range(len(prev))},
        compiler_params=_cp())(*([parts] * npart), w, m, v, *prev)


SLOT = 8
SMALL_ROWS = 6 * SLOT
ROW_LB = 4 * SLOT


def _small_update(gath, w, m, v, *, name):
    def body(g_ref, w_ref, m_ref, v_ref, g_out, d_out, m_out, v_out):
        tot = g_ref[0]
        for k in range(1, 8):
            tot = tot + g_ref[k]
        wv = w_ref[...]
        c0, c1 = wv[ROW_LB:ROW_LB + 1, :], wv[ROW_LB + 1:ROW_LB + 2, :]
        mx = jnp.maximum(c0, c1)
        e0, e1 = jnp.exp(c0 - mx), jnp.exp(c1 - mx)
        lb = e1 / (e0 + e1)
        gl = tot[ROW_LB:ROW_LB + 1, :] * lb * (1.0 - lb)
        row = lax.broadcasted_iota(jnp.int32, tot.shape, 0)
        g = jnp.where(row == ROW_LB, -gl, jnp.where(row == ROW_LB + 1, gl, tot))
        d, m2, v2 = _adamw_math(wv, g, m_ref[...], v_ref[...])
        g_out[...] = g
        d_out[...] = d
        m_out[...] = m2
        v_out[...] = v2

    return pl.pallas_call(
        body, name=name, out_shape=[jax.ShapeDtypeStruct(w.shape, F32)] * 4, compiler_params=_cp())(gath, w, m, v)


D_MODEL = 1024


def _ffn_fwd(h, gain, wg, wu, wd, tag):
    xn, gg, uu, act = _norm_gate_up(h, gain, wg, wu, name=f"{tag}_gate_up")
    out = _mm([(act, wd)], residual=h, alpha=MACARON, tn=1024, name=f"{tag}_down")
    return out, (h, xn, gg, uu, act)


def _ffn_input_bwd(dg, du, wg, wu, x, gain, dres, chip_part, *, name, scale, tm=256):
    t, d = x.shape
    f = wg.shape[0]
    tm = min(tm, t)
    nt = t // tm
    fused = chip_part is not None

    def body(dg_ref, du_ref, wg_ref, wu_ref, x_ref, g_ref, dres_ref, *rest):
        if fused:
            part_ref, dx_ref, dxb_ref, dgain_ref, parts_ref = rest[:5]
            start, finish = _chip_exchange_phases(part_ref, parts_ref, *rest[5:])
            pl.when(pl.program_id(0) == 0)(start)
        else:
            dx_ref, dxb_ref, dgain_ref = rest
        dxn_v = _dot(dg_ref[...], wg_ref[...]) + _dot(du_ref[...], wu_ref[...])
        xv = x_ref[...]
        rstd = lax.rsqrt(jnp.mean(xv * xv, axis=-1, keepdims=True) + RMS_EPS)
        xhat = xv * rstd
        dxhat = dxn_v * g_ref[...]
        dx = dres_ref[...] + rstd * (dxhat - xhat * jnp.mean(dxhat * xhat, axis=-1, keepdims=True))
        dx_ref[...] = dx
        dxb_ref[...] = (dx * scale).astype(BF16)

        @pl.when(pl.program_id(0) == 0)
        def _():
            dgain_ref[...] = jnp.zeros_like(dgain_ref)

        dgain_ref[...] += jnp.sum(dxn_v * xhat, axis=0, keepdims=True)
        if fused:
            pl.when(pl.program_id(0) == nt - 1)(finish)

    wide = pl.BlockSpec((tm, f), lambda i: (i, 0))
    wsp = pl.BlockSpec((f, d), lambda i: (0, 0))
    row = pl.BlockSpec((tm, d), lambda i: (i, 0))
    vec = pl.BlockSpec((1, d), lambda i: (0, 0))
    args = [dg, du, wg, wu, x, gain, dres] + ([chip_part] if fused else [])
    return pl.pallas_call(
        body, name=name, grid=(nt,),
        in_specs=[wide, wide, wsp, wsp, row, vec, row] + ([HBM_SPEC] if fused else []),
        out_specs=[row, row, vec] + ([HBM_SPEC] if fused else []),
        out_shape=[jax.ShapeDtypeStruct((t, d), F32), jax.ShapeDtypeStruct((t, d), BF16), jax.ShapeDtypeStruct((1, d), F32)]
        + ([jax.ShapeDtypeStruct(chip_part.shape, chip_part.dtype)] if fused else []),
        scratch_shapes=_chip_exchange_scratch() if fused else [],
        compiler_params=_cp(dimension_semantics=("arbitrary",)))(*args)


def _ffn_bwd(dout, dout_half, saved, gain, wg, wu, wd, tag, next_scale, exchanges=None):
    h, xn, gg, uu, act = saved
    early_chip_part, send_after_dwd, chip_part_after_dwgu = exchanges if exchanges is not None else (None, None, None)
    dg, du, *early_parts = _swiglu_bwd(dout_half, wd, gg, uu, early_chip_part, tm=256, tf=wd.shape[0],
                                       name=f"{tag}_dact")
    dwd = _mm([(act, dout_half)], ta=True, tm=256, tn=1024, out_dtype=BF16, name=f"{tag}_dwd")
    send = send_after_dwd(dwd) if exchanges is not None else None
    dwg, dwu, *mid_parts = _mm_shared_rhs([dg, du], xn, tm=256, send=send, name=f"{tag}_dwgu")
    chip_part = chip_part_after_dwgu(dwg, dwu) if exchanges is not None else None
    dh, dh_b, dgain, *parts = _ffn_input_bwd(dg, du, wg, wu, h, gain, dout, chip_part, scale=next_scale,
                                             name=f"{tag}_input_bwd")
    return dh, dh_b, dwg, dwu, dwd, dgain, (early_parts + mid_parts + parts)


def kernel(x, ffn_pre_norm, ffn_pre_w_gate, ffn_pre_w_up, ffn_pre_w_down, mix_norm, ffn_post_norm, ffn_post_w_gate, ffn_post_w_up, ffn_post_w_down, ab_w_in, ab_conv_w, ab_w_out, c_w_in, c_lower_bounds, c_out_norm, c_w_out, final_norm, loss_target, m_ffn_pre_norm, m_ffn_pre_w_gate, m_ffn_pre_w_up, m_ffn_pre_w_down, m_mix_norm, m_ffn_post_norm, m_ffn_post_w_gate, m_ffn_post_w_up, m_ffn_post_w_down, m_ab_w_in, m_ab_conv_w, m_ab_w_out, m_c_w_in, m_c_lower_bounds, m_c_out_norm, m_c_w_out, m_final_norm, v_ffn_pre_norm, v_ffn_pre_w_gate, v_ffn_pre_w_up, v_ffn_pre_w_down, v_mix_norm, v_ffn_post_norm, v_ffn_post_w_gate, v_ffn_post_w_up, v_ffn_post_w_down, v_ab_w_in, v_ab_conv_w, v_ab_w_out, v_c_w_in, v_c_lower_bounds, v_c_out_norm, v_c_w_out, v_final_norm):
    d = D_MODEL
    h0 = x[0]
    target = loss_target[0]
    core = lax.axis_index("c").astype(jnp.int32).reshape(1)

    big = [("pre_g", ffn_pre_w_gate, m_ffn_pre_w_gate, v_ffn_pre_w_gate),
           ("pre_u", ffn_pre_w_up, m_ffn_pre_w_up, v_ffn_pre_w_up),
           ("pre_d", ffn_pre_w_down, m_ffn_pre_w_down, v_ffn_pre_w_down),
           ("post_g", ffn_post_w_gate, m_ffn_post_w_gate, v_ffn_post_w_gate),
           ("post_u", ffn_post_w_up, m_ffn_post_w_up, v_ffn_post_w_up),
           ("post_d", ffn_post_w_down, m_ffn_post_w_down, v_ffn_post_w_down),
           ("ab_in", ab_w_in, m_ab_w_in, v_ab_w_in),
           ("ab_out", ab_w_out, m_ab_w_out, v_ab_w_out),
           ("c_in", c_w_in, m_c_w_in, v_c_w_in),
           ("c_out", c_w_out, m_c_w_out, v_c_w_out)]
    by_tag = {tag: (w, m, v) for tag, w, m, v in big}

    def layer_rows(tag):
        w = by_tag[tag][0]
        return w.size // d // w.shape[0]

    def layout(items):
        offs, off = {}, 0
        for item in items:
            offs[item] = off
            off += layer_rows(item[0])
        return offs, off

    ffn = [f"{pos}_{kind}" for pos in ("pre", "post") for kind in "gud"]
    first_items = [("pre_g", 0), ("pre_u", 0)]
    early_items = [("pre_d", 0), ("ab_in", 0)]
    late_items = ([("pre_g", 1), ("pre_u", 1), ("pre_d", 1)] + [(f"post_{kind}", l) for l in (0, 1) for kind in "gud"]
                  + [("ab_out", 0), ("c_in", 0), ("c_out", 0)])
    grad_items = {"A0": [(f"post_{kind}", 1) for kind in "gud"] + [("c_out", 0)],
                  "A1": ([(f"pre_{kind}", 1) for kind in "gud"] + [(f"post_{kind}", 0) for kind in "gud"]
                         + [("c_in", 0), ("ab_out", 0)]),
                  "C": [("ab_in", 0)], "B0": [("pre_d", 0)], "B1": [("pre_g", 0), ("pre_u", 0)]}
    grad_offs = {k: layout(items)[0] for k, items in grad_items.items()}
    grad_conv_row = layout(grad_items["C"])[1]

    def conv_rows(a, split):
        flat = a.reshape(-1)
        if split:
            hi = flat.astype(BF16)
            flat = jnp.concatenate([hi, (flat - hi.astype(F32)).astype(BF16)])
        return jnp.zeros((16, d), flat.dtype).at[0, :flat.shape[0]].set(flat)

    nconv = ab_conv_w.size
    col_sharded = {"pre_g", "pre_u", "post_g", "post_u", "ab_in", "c_in"}

    def pack_rows(item):
        tag, layer = item
        a = by_tag[tag][0][layer]
        return (a.T if tag in col_sharded else a).reshape(-1, d).astype(BF16)

    first_pack = jnp.concatenate([pack_rows(item) for item in first_items], axis=0)
    early_pack = jnp.concatenate([pack_rows(item) for item in early_items] + [conv_rows(ab_conv_w, True)], axis=0)
    late_pack = jnp.concatenate([pack_rows(item) for item in late_items], axis=0)
    first_w = _all_gather(first_pack, [layer_rows(tag) for tag, _ in first_items], name="gather_first_weights")
    full = {item: g.reshape(-1, d) for item, g in zip(first_items, first_w)}

    xn0, gg0, uu0, act0, *early_w = _norm_gate_up(
        h0, ffn_pre_norm[0:1], full["pre_g", 0], full["pre_u", 0], name="l0pre_gate_up_gather_early_weights",
        pack=early_pack, seg_rows=[layer_rows(tag) for tag, _ in early_items] + [16])
    full.update({item: g.reshape(-1, d) for item, g in zip(early_items, early_w)})
    ffn_w = {("pre", 0): tuple(full[f"pre_{kind}", 0] for kind in "gud")}
    w_ab_in = full["ab_in", 0]
    cg = early_w[-1][:, 0, :2 * nconv].astype(F32)
    conv_w = (cg[:, :nconv] + cg[:, nconv:]).reshape(8, 3, -1).transpose(1, 0, 2).reshape(3, -1)
    aw = w_ab_in.shape[0] // 6
    h1 = _mm([(act0, full["pre_d", 0])], residual=h0, alpha=MACARON, tn=1024, name="l0pre_down")
    s_pre0 = (h0, xn0, gg0, uu0, act0)
    hn0, pa, pb = _norm_proj(h1, mix_norm[0:1], w_ab_in, (F32, BF16), tm=512, name="ab_norm_proj")
    ya = _conv_fwd(pa, conv_w, name="conv_fwd")
    yb, ltot, *late_w = _attn_fwd(pb, late_pack, [layer_rows(tag) for tag, _ in late_items],
                                  name="attn_fwd_gather_late_weights")
    full.update({item: g.reshape(-1, d) for item, g in zip(late_items, late_w)})
    for pos, layer in (("post", 0), ("pre", 1), ("post", 1)):
        ffn_w[pos, layer] = tuple(full[f"{pos}_{kind}", layer] for kind in "gud")
    w_ab_out, w_c_in, w_c_out = full["ab_out", 0], full["c_in", 0], full["c_out", 0]
    h2 = _mm([(ya, w_ab_out[:aw]), (yb, w_ab_out[aw:])], residual=h1, tn=1024, name="ab_out")
    h3, s_post0 = _ffn_fwd(h2, ffn_post_norm[0:1], *ffn_w["post", 0], "l0post")
    h4, s_pre1 = _ffn_fwd(h3, ffn_pre_norm[1:2], *ffn_w["pre", 1], "l1pre")
    hn1, pc = _norm_proj(h4, mix_norm[1:2], w_c_in, (F32,), tm=256, name="c_norm_proj")
    yc, o_saved, states = _hgrn_fwd(pc, c_lower_bounds, c_out_norm, name="hgrn_fwd")
    h5 = _mm([(yc, w_c_out)], residual=h4, tn=1024, name="c_out")
    h6, s_post1 = _ffn_fwd(h5, ffn_post_norm[1:2], *ffn_w["post", 1], "l1post")
    dh6, dh6_b, d_final, loss_vec = _loss_head(h6, final_norm.reshape(1, d), target, name="loss_head")

    gw = {}

    def grad_send(key, extra=()):
        return [g.reshape(4, 2, -1, d) for g in [gw[item] for item in grad_items[key]] + list(extra)]

    def chip_partials(key, extra=()):
        send = jnp.concatenate(grad_send(key, extra), axis=2)
        from_sibling = _sibling_exchange(send, name=f"grad{key}_sibling_exchange")
        return _pair_add(send, from_sibling, core, name=f"grad{key}_pair_add")

    dh5, dh5_b, gw["post_g", 1], gw["post_u", 1], gw["post_d", 1], d_post1, *_ = _ffn_bwd(
        dh6, dh6_b, s_post1, ffn_post_norm[1:2], *ffn_w["post", 1], "l1post", 1.0)
    dyc = _mm([(dh5_b, w_c_out)], tb=True, tn=1024, name="c_out_dy")
    g_c_out = _mm([(yc, dh5_b)], ta=True, tm=256, tn=1024, out_dtype=BF16, name="c_out_dw")
    gw["c_out", 0] = g_c_out
    dcq, dcf, dci, dcg, dlb, d_onorm, parts_a0 = _hgrn_bwd(pc, o_saved, states, dyc, c_lower_bounds, c_out_norm,
                                                           grad_send("A0"), name="hgrn_bwd_exchange_grads_a0")
    dparts = [dcq, dcf, dci, dcg]
    g_c_in = jnp.concatenate(_mm_shared_rhs(dparts, hn1, tm=256, name="c_in_dw"), axis=0)
    cw = w_c_in.shape[0] // 4
    dhn1 = _mm([(dp, w_c_in[i * cw:(i + 1) * cw]) for i, dp in enumerate(dparts)], tm=512, tn=1024, name="c_in_dx")
    dh4, dh4_b, d_mix1 = _rmsnorm_bwd(h4, mix_norm[1:2], dhn1, dh5, scale=MACARON, name="l1_mix_norm_bwd")
    dh3, dh3_b, gw["pre_g", 1], gw["pre_u", 1], gw["pre_d", 1], d_pre1, *_ = _ffn_bwd(
        dh4, dh4_b, s_pre1, ffn_pre_norm[1:2], *ffn_w["pre", 1], "l1pre", MACARON)
    dh2, dh2_b, gw["post_g", 0], gw["post_u", 0], gw["post_d", 0], d_post0, *_ = _ffn_bwd(
        dh3, dh3_b, s_post0, ffn_post_norm[0:1], *ffn_w["post", 0], "l0post", 1.0)
    dyab = _mm([(dh2_b, w_ab_out)], tb=True, tn=1024, name="ab_out_dy")
    g_ab_out = jnp.concatenate(_mm_shared_rhs([ya, yb], dh2_b, tm=256, name="ab_out_dw"), axis=0)
    dab, dac, dax, g_conv = _conv_bwd(pa, dyab, conv_w, name="conv_bwd")

    gw["c_in", 0], gw["ab_out", 0] = g_c_in, g_ab_out
    dq, dk, dv, parts_a1 = _attn_bwd(pb, dyab, ltot, grad_send("A1"), name="attn_bwd_exchange_grads_a1")
    dparts = [dab, dac, dax, dq, dk, dv]
    g_ab_in = jnp.concatenate(_mm_shared_rhs(dparts, hn0, tm=128, name="ab_in_dw"), axis=0)
    dhn0 = _mm([(dp, w_ab_in[i * aw:(i + 1) * aw]) for i, dp in enumerate(dparts)], tm=512, tn=1024, name="ab_in_dx")
    dh1, dh1_b, d_mix0 = _rmsnorm_bwd(h1, mix_norm[0:1], dhn0, dh2, scale=MACARON, name="l0_mix_norm_bwd")
    gw["ab_in", 0] = g_ab_in
    gconv_own = g_conv.reshape(3, 8, -1).transpose(1, 0, 2).reshape(8, -1)
    conv_piece = jnp.zeros((8, 16, d), F32).at[:, 0, :nconv].set(gconv_own).astype(BF16)

    def send_b0(dwd):
        gw["pre_d", 0] = dwd
        return grad_send("B0")

    def chip_part_b1(dwg, dwu):
        gw["pre_g", 0], gw["pre_u", 0] = dwg, dwu
        return chip_partials("B1")

    dh0, _, _, _, _, d_pre0, (parts_c, parts_b0, parts_b1) = _ffn_bwd(
        dh1, dh1_b, s_pre0, ffn_pre_norm[0:1], *ffn_w["pre", 0], "l0pre", 1.0,
        (chip_partials("C", [conv_piece]), send_b0, chip_part_b1))

    parts = {"A0": parts_a0, "A1": parts_a1, "B0": parts_b0, "B1": parts_b1, "C": parts_c}
    upd = {}
    for tag, w, m, v in big:
        view = (lambda a: jnp.swapaxes(a, 1, 2)) if tag in col_sharded else (lambda a: a)
        where = {layer: (key, grad_offs[key][tag, layer])
                 for key in grad_items for t2, layer in grad_items[key] if t2 == tag}
        res = None
        for layer in sorted(where):
            key, off = where[layer]
            res = _adamw_shard(parts[key], off, view(w), view(m), view(v), layer, res, name=f"adamw_{tag}{layer}")
        upd[tag] = [view(a) for a in res]
    res = _adamw_shard(parts["C"], grad_conv_row, *(conv_rows(a, False)[None] for a in (ab_conv_w, m_ab_conv_w, v_ab_conv_w)),
                       0, None, name="adamw_conv")
    upd["conv"] = [r[0, 0, :nconv].reshape(ab_conv_w.shape) for r in res]

    def small_pack(pre, mix, post, final, lbs, onorm):
        def slot(parts):
            out, r = jnp.zeros((SLOT, d), F32), 0
            for a in (parts if isinstance(parts, tuple) else (parts,)):
                out = out.at[r:r + a.shape[0], :a.shape[1]].set(a)
                r += a.shape[0]
            return out

        return jnp.concatenate([slot(pre), slot(mix), slot(post), slot(final.reshape(1, d)), slot(lbs), slot(onorm)], axis=0)

    d_on = d_onorm.reshape(-1, c_out_norm.shape[1]).sum(axis=0, keepdims=True)
    gsmall = small_pack((d_pre0, d_pre1), (d_mix0, d_mix1), (d_post0, d_post1), d_final, dlb, d_on)
    gsmall_all = _all_gather(gsmall, name="gather_small_grads")
    sres = _small_update(
        gsmall_all,
        small_pack(ffn_pre_norm, mix_norm, ffn_post_norm, final_norm, c_lower_bounds, c_out_norm),
        small_pack(m_ffn_pre_norm, m_mix_norm, m_ffn_post_norm, m_final_norm, m_c_lower_bounds, m_c_out_norm),
        small_pack(v_ffn_pre_norm, v_mix_norm, v_ffn_post_norm, v_final_norm, v_c_lower_bounds, v_c_out_norm),
        name="small_update")

    def small_out(r):
        return {"pre_norm": r[0:2], "mix_norm": r[SLOT:SLOT + 2], "post_norm": r[2 * SLOT:2 * SLOT + 2],
                "final": r[3 * SLOT], "lb": r[ROW_LB:ROW_LB + 2], "onorm": r[5 * SLOT:5 * SLOT + 1, :c_out_norm.shape[1]]}

    small = [small_out(r) for r in sres]
    outs = []
    for k in range(4):
        s = small[k]
        outs += [s["pre_norm"], upd["pre_g"][k], upd["pre_u"][k], upd["pre_d"][k], s["mix_norm"], s["post_norm"],
                 upd["post_g"][k], upd["post_u"][k], upd["post_d"][k], upd["ab_in"][k], upd["conv"][k],
                 upd["ab_out"][k], upd["c_in"][k], s["lb"], s["onorm"], upd["c_out"][k], s["final"]]
    loss = lax.psum(loss_vec[0, 0], ("x", "y", "c"))
    return (loss, dh0[None], *outs)
```

```python
import math

import jax
import jax.numpy as jnp
from jax import lax
from jax.experimental import pallas as pl
from jax.experimental.pallas import tpu as pltpu

F32 = jnp.float32
BF16 = jnp.bfloat16
MESH = pl.DeviceIdType.MESH

RMS_EPS = 1e-6
MACARON = 0.5
LANES = 128
CHUNK = 64
N_LEVELS = 6
HGRN_HEADS = 2
SB_KEYS = 256
ADAM_LR, ADAM_B1, ADAM_B2, ADAM_EPS, ADAM_WD, ADAM_STEP = 0.001, 0.9, 0.999, 1e-08, 0.01, 10
VMEM_LIMIT = 48 * 1024 * 1024


def _cp(**kw):
    return pltpu.CompilerParams(vmem_limit_bytes=VMEM_LIMIT, **kw)


def _sigmoid(x):
    return 0.5 * jnp.tanh(0.5 * x) + 0.5


def _bf(x):
    return x if x.dtype == BF16 else x.astype(BF16)


def _split3(x):
    hi = x.astype(BF16)
    r1 = x - hi.astype(F32)
    mid = r1.astype(BF16)
    lo = (r1 - mid.astype(F32)).astype(BF16)
    return hi, mid, lo


def _dot(a, b, ca=1, cb=0):
    return lax.dot_general(a, b, (((ca,), (cb,)), ((), ())), preferred_element_type=F32)


def _dot_exact_lhs(m, x):
    hi, mid, lo = _split3(x)
    return _dot(m, hi) + _dot(m, mid) + _dot(m, lo)


def _mm(terms, *, name, ta=False, tb=False, out_dtype=F32, residual=None, alpha=1.0, tm=512, tn=512):
    nt = len(terms)
    a0, b0 = terms[0]
    m = a0.shape[1] if ta else a0.shape[0]
    n = b0.shape[0] if tb else b0.shape[1]
    tm, tn = min(tm, m), min(tn, n)
    assert m % tm == 0 and n % tn == 0, (name, m, n, tm, tn)
    has_res = residual is not None

    def body(*refs):
        o_ref = refs[-1]
        acc = None
        for i in range(nt):
            a = _bf(refs[2 * i][...])
            b = _bf(refs[2 * i + 1][...])
            p = _dot(a, b, 0 if ta else 1, 1 if tb else 0)
            acc = p if acc is None else acc + p
        if alpha != 1.0:
            acc = acc * alpha
        if has_res:
            acc = acc + refs[2 * nt][...]
        o_ref[...] = acc.astype(out_dtype)

    in_specs, args = [], []
    for a, b in terms:
        k = a.shape[0] if ta else a.shape[1]
        assert (b.shape[1] if tb else b.shape[0]) == k, (name, a.shape, b.shape)
        in_specs.append(pl.BlockSpec((k, tm), lambda i, j: (0, i)) if ta else pl.BlockSpec((tm, k), lambda i, j: (i, 0)))
        in_specs.append(pl.BlockSpec((tn, k), lambda i, j: (j, 0)) if tb else pl.BlockSpec((k, tn), lambda i, j: (0, j)))
        args += [a, b]
    if has_res:
        in_specs.append(pl.BlockSpec((tm, tn), lambda i, j: (i, j)))
        args.append(residual)
    return pl.pallas_call(
        body, name=name, grid=(m // tm, n // tn), in_specs=in_specs,
        out_specs=pl.BlockSpec((tm, tn), lambda i, j: (i, j)),
        out_shape=jax.ShapeDtypeStruct((m, n), out_dtype), compiler_params=_cp())(*args)


def _norm_proj(x, gain, w_t, out_dtypes, *, name, tm):
    t, d = x.shape
    n = w_t.shape[0]
    tm = min(tm, t)
    npart = len(out_dtypes)
    width = n // npart

    def body(x_ref, g_ref, w_ref, xn_ref, *part_refs):
        xv = x_ref[...]
        rstd = lax.rsqrt(jnp.mean(xv * xv, axis=-1, keepdims=True) + RMS_EPS)
        xn = (xv * rstd * g_ref[...]).astype(BF16)
        xn_ref[...] = xn
        for p, ref in enumerate(part_refs):
            ref[...] = _dot(xn, w_ref[p * width:(p + 1) * width, :], 1, 1).astype(out_dtypes[p])

    row = pl.BlockSpec((tm, d), lambda i: (i, 0))
    return pl.pallas_call(
        body, name=name, grid=(t // tm,),
        in_specs=[row, pl.BlockSpec((1, d), lambda i: (0, 0)), pl.BlockSpec((n, d), lambda i: (0, 0))],
        out_specs=[row] + [pl.BlockSpec((tm, width), lambda i: (i, 0))] * npart,
        out_shape=[jax.ShapeDtypeStruct((t, d), BF16)] + [jax.ShapeDtypeStruct((t, width), dt) for dt in out_dtypes],
        compiler_params=_cp())(x, gain, w_t)


def _mm_shared_rhs(a_list, b, *, name, tm, out_dtype=BF16, send=None):
    k, n = b.shape
    assert all(a.shape[0] == k and a.shape[1] % tm == 0 and a.shape[1] == a_list[0].shape[1] for a in a_list)
    m = a_list[0].shape[1]
    na = len(a_list)
    nsteps = m // tm
    sends = list(send) if send is not None else []
    ns = len(sends)

    def body(*refs):
        first_out = na + 1 + ns
        if ns:
            start, finish = _direct_exchange_phases(refs[na + 1:first_out], refs[first_out + na], *refs[first_out + na + 1:])
            pl.when(pl.program_id(0) == 0)(start)
        bv = refs[na][...]
        for i in range(na):
            refs[first_out + i][...] = _dot(refs[i][...], bv, 0, 0).astype(out_dtype)
        if ns:
            pl.when(pl.program_id(0) == nsteps - 1)(finish)

    return pl.pallas_call(
        body, name=name, grid=(nsteps,),
        in_specs=[pl.BlockSpec((k, tm), lambda i: (0, i))] * na + [pl.BlockSpec((k, n), lambda i: (0, 0))] + [HBM_SPEC] * ns,
        out_specs=[pl.BlockSpec((tm, n), lambda i: (i, 0))] * na + ([HBM_SPEC] if ns else []),
        out_shape=[jax.ShapeDtypeStruct((m, n), out_dtype)] * na
        + ([jax.ShapeDtypeStruct((8, _direct_exchange_rows(sends), sends[0].shape[3]), sends[0].dtype)] if ns else []),
        scratch_shapes=_direct_exchange_scratch() if ns else [],
        compiler_params=_cp(dimension_semantics=("arbitrary",)))(*a_list, b, *sends)


def _rmsnorm_bwd(x, gain, dxn, dres, *, name, scale, tm=512):
    t, d = x.shape
    tm = min(tm, t)

    def body(x_ref, g_ref, dxn_ref, dres_ref, dx_ref, dxb_ref, dg_ref):
        xv = x_ref[...]
        rstd = lax.rsqrt(jnp.mean(xv * xv, axis=-1, keepdims=True) + RMS_EPS)
        xhat = xv * rstd
        dxn_v = dxn_ref[...]
        dxhat = dxn_v * g_ref[...]
        dx = dres_ref[...] + rstd * (dxhat - xhat * jnp.mean(dxhat * xhat, axis=-1, keepdims=True))
        dx_ref[...] = dx
        dxb_ref[...] = (dx * scale).astype(BF16)

        @pl.when(pl.program_id(0) == 0)
        def _():
            dg_ref[...] = jnp.zeros_like(dg_ref)

        dg_ref[...] += jnp.sum(dxn_v * xhat, axis=0, keepdims=True)

    row = pl.BlockSpec((tm, d), lambda i: (i, 0))
    vec = pl.BlockSpec((1, d), lambda i: (0, 0))
    return pl.pallas_call(
        body, name=name, grid=(t // tm,), in_specs=[row, vec, row, row], out_specs=[row, row, vec],
        out_shape=[jax.ShapeDtypeStruct((t, d), F32), jax.ShapeDtypeStruct((t, d), BF16), jax.ShapeDtypeStruct((1, d), F32)],
        compiler_params=_cp())(x, gain, dxn, dres)


def _loss_head(h, gain, target, *, name, tm=512):
    t, d = h.shape
    tm = min(tm, t)

    def body(h_ref, g_ref, t_ref, dh_ref, dhb_ref, dg_ref, loss_ref):
        hv = h_ref[...]
        rstd = lax.rsqrt(jnp.mean(hv * hv, axis=-1, keepdims=True) + RMS_EPS)
        xhat = hv * rstd
        err = xhat * g_ref[...] - t_ref[...]
        dy = err * (1.0 / d)
        dxhat = dy * g_ref[...]
        dh = rstd * (dxhat - xhat * jnp.mean(dxhat * xhat, axis=-1, keepdims=True))
        dh_ref[...] = dh
        dhb_ref[...] = (dh * MACARON).astype(BF16)

        @pl.when(pl.program_id(0) == 0)
        def _():
            dg_ref[...] = jnp.zeros_like(dg_ref)
            loss_ref[...] = jnp.zeros_like(loss_ref)

        dg_ref[...] += jnp.sum(dy * xhat, axis=0, keepdims=True)
        part = jnp.sum(jnp.sum(err * err, axis=-1, keepdims=True), axis=0, keepdims=True) * (0.5 / d)
        loss_ref[...] += jnp.broadcast_to(part, loss_ref.shape)

    row = pl.BlockSpec((tm, d), lambda i: (i, 0))
    vec = pl.BlockSpec((1, d), lambda i: (0, 0))
    return pl.pallas_call(
        body, name=name, grid=(t // tm,), in_specs=[row, vec, row],
        out_specs=[row, row, vec, pl.BlockSpec((1, LANES), lambda i: (0, 0))],
        out_shape=[jax.ShapeDtypeStruct((t, d), F32), jax.ShapeDtypeStruct((t, d), BF16), jax.ShapeDtypeStruct((1, d), F32),
                   jax.ShapeDtypeStruct((1, LANES), F32)],
        compiler_params=_cp())(h, gain, target)


def _norm_gate_up(x, gain, wg, wu, *, name, tm=256, tf=2816, pack=None, seg_rows=()):
    t, d = x.shape
    f = wg.shape[0]
    tm, tf = min(tm, t), min(tf, f)
    assert f % tf == 0
    ni, nj = t // tm, f // tf
    nseg = len(seg_rows)

    def body(x_ref, g_ref, wg_ref, wu_ref, *rest):
        if pack is not None:
            pack_ref, xn_ref, gg_ref, uu_ref, act_ref = rest[:5]
            start, forward, finish = _gather_phases(pack_ref, rest[5:5 + nseg], seg_rows, *rest[5 + nseg:])
            step = pl.program_id(0) * nj + pl.program_id(1)
            pl.when(step == 0)(start)
            pl.when(step == (3 * ni * nj) // 4)(forward)
        else:
            xn_ref, gg_ref, uu_ref, act_ref = rest

        @pl.when(pl.program_id(1) == 0)
        def _():
            xv = x_ref[...]
            rstd = lax.rsqrt(jnp.mean(xv * xv, axis=-1, keepdims=True) + RMS_EPS)
            xn_ref[...] = (xv * rstd * g_ref[...]).astype(BF16)

        xn = xn_ref[...]
        gv = _dot(xn, wg_ref[...], 1, 1)
        uv = _dot(xn, wu_ref[...], 1, 1)
        gg_ref[...] = gv.astype(BF16)
        uu_ref[...] = uv.astype(BF16)
        act_ref[...] = (gv * _sigmoid(gv) * uv).astype(BF16)
        if pack is not None:
            pl.when(step == ni * nj - 1)(finish)

    row = pl.BlockSpec((tm, d), lambda i, j: (i, 0))
    wsp = pl.BlockSpec((tf, d), lambda i, j: (j, 0))
    osp = pl.BlockSpec((tm, tf), lambda i, j: (i, j))
    fused = pack is not None
    return pl.pallas_call(
        body, name=name, grid=(ni, nj),
        in_specs=[row, pl.BlockSpec((1, d), lambda i, j: (0, 0)), wsp, wsp] + ([HBM_SPEC] if fused else []),
        out_specs=[row, osp, osp, osp] + [HBM_SPEC] * nseg,
        out_shape=[jax.ShapeDtypeStruct((t, d), BF16)] + [jax.ShapeDtypeStruct((t, f), BF16)] * 3
        + [jax.ShapeDtypeStruct((8, n, d), BF16) for n in seg_rows],
        scratch_shapes=_gather_scratch() if fused else [],
        compiler_params=_cp(dimension_semantics=("arbitrary", "arbitrary")))(x, gain, wg, wu, *([pack] if fused else []))


def _swiglu_bwd(dout, wd, gg, uu, chip_part=None, *, name, tm=512, tf=1408):
    t, d = dout.shape
    f = wd.shape[0]
    tm, tf = min(tm, t), min(tf, f)
    nj, ni = f // tf, t // tm
    fused = chip_part is not None

    def body(do_ref, wd_ref, g_ref, u_ref, *rest):
        if fused:
            part_ref, dg_ref, du_ref, parts_ref = rest[:4]
            start, finish = _chip_exchange_phases(part_ref, parts_ref, *rest[4:])
            step = pl.program_id(0) * ni + pl.program_id(1)
            pl.when(step == 0)(start)
        else:
            dg_ref, du_ref = rest
        dact = _dot(do_ref[...], wd_ref[...], 1, 1)
        gv = g_ref[...].astype(F32)
        uv = u_ref[...].astype(F32)
        sg = _sigmoid(gv)
        dg_ref[...] = (dact * uv * (sg * (1.0 + gv * (1.0 - sg)))).astype(BF16)
        du_ref[...] = (dact * (gv * sg)).astype(BF16)
        if fused:
            pl.when(step == nj * ni - 1)(finish)

    osp = pl.BlockSpec((tm, tf), lambda j, i: (i, j))
    return pl.pallas_call(
        body, name=name, grid=(nj, ni),
        in_specs=[pl.BlockSpec((tm, d), lambda j, i: (i, 0)), pl.BlockSpec((tf, d), lambda j, i: (j, 0)), osp, osp]
        + ([HBM_SPEC] if fused else []),
        out_specs=[osp, osp] + ([HBM_SPEC] if fused else []),
        out_shape=[jax.ShapeDtypeStruct((t, f), BF16)] * 2
        + ([jax.ShapeDtypeStruct(chip_part.shape, chip_part.dtype)] if fused else []),
        scratch_shapes=_chip_exchange_scratch() if fused else [],
        compiler_params=_cp(dimension_semantics=("arbitrary", "arbitrary")))(dout, wd, gg, uu, *([chip_part] if fused else []))


def _shift_down(x, n):
    rows = lax.broadcasted_iota(jnp.int32, x.shape, 0)
    return jnp.where(rows >= n, pltpu.roll(x, n, 0), 0.0)


def _shift_up(x, n):
    t = x.shape[0]
    rows = lax.broadcasted_iota(jnp.int32, x.shape, 0)
    return jnp.where(rows < t - n, pltpu.roll(x, t - n, 0), 0.0)


def _conv_fwd(pa, conv_w, *, name):
    t = pa.shape[0]
    nb = pa.shape[1] // 3 // LANES

    def body(b_ref, c_ref, x_ref, w_ref, y_ref):
        u = c_ref[...] * x_ref[...]
        w = w_ref[...]
        conv = w[2:3, :] * u + w[1:2, :] * _shift_down(u, 1) + w[0:1, :] * _shift_down(u, 2)
        y_ref[...] = (b_ref[...] * conv).astype(BF16)

    def col(off):
        return pl.BlockSpec((t, LANES), lambda j: (0, off + j))

    return pl.pallas_call(
        body, name=name, grid=(nb,),
        in_specs=[col(0), col(nb), col(2 * nb), pl.BlockSpec((3, LANES), lambda j: (0, j))],
        out_specs=pl.BlockSpec((t, LANES), lambda j: (0, j)),
        out_shape=jax.ShapeDtypeStruct((t, nb * LANES), BF16), compiler_params=_cp())(pa, pa, pa, conv_w)


def _conv_bwd(pa, dy, conv_w, *, name):
    t = pa.shape[0]
    nb = pa.shape[1] // 3 // LANES

    def body(b_ref, c_ref, x_ref, dy_ref, w_ref, db_ref, dc_ref, dx_ref, dw_ref):
        cv, xv = c_ref[...], x_ref[...]
        u = cv * xv
        u1, u2 = _shift_down(u, 1), _shift_down(u, 2)
        w = w_ref[...]
        conv = w[2:3, :] * u + w[1:2, :] * u1 + w[0:1, :] * u2
        dyv = dy_ref[...]
        db_ref[...] = (dyv * conv).astype(BF16)
        dconv = dyv * b_ref[...]
        du = w[2:3, :] * dconv + w[1:2, :] * _shift_up(dconv, 1) + w[0:1, :] * _shift_up(dconv, 2)
        dc_ref[...] = (du * xv).astype(BF16)
        dx_ref[...] = (du * cv).astype(BF16)
        dw_ref[0:1, :] = jnp.sum(dconv * u2, axis=0, keepdims=True)
        dw_ref[1:2, :] = jnp.sum(dconv * u1, axis=0, keepdims=True)
        dw_ref[2:3, :] = jnp.sum(dconv * u, axis=0, keepdims=True)

    def col(off):
        return pl.BlockSpec((t, LANES), lambda j: (0, off + j))

    osp = pl.BlockSpec((t, LANES), lambda j: (0, j))
    wsp = pl.BlockSpec((3, LANES), lambda j: (0, j))
    return pl.pallas_call(
        body, name=name, grid=(nb,), in_specs=[col(0), col(nb), col(2 * nb), col(0), wsp],
        out_specs=[osp, osp, osp, wsp],
        out_shape=[jax.ShapeDtypeStruct((t, nb * LANES), BF16)] * 3 + [jax.ShapeDtypeStruct((3, nb * LANES), F32)],
        compiler_params=_cp())(pa, pa, pa, dy, conv_w)


def _sb_consts():
    j = lax.broadcasted_iota(jnp.int32, (SB_KEYS, SB_KEYS), 0)
    s = lax.broadcasted_iota(jnp.int32, (SB_KEYS, SB_KEYS), 1)
    after = (j > s).astype(BF16)
    upto = (j <= s).astype(BF16)
    before = (j < s).astype(BF16)
    return after, jnp.stack([upto, before])


def _log_sigmoid(z):
    return jnp.minimum(z, 0.0) - jnp.log(1.0 + jnp.exp(-jnp.abs(z)))


def _attn_fwd(pb, late_pack, seg_rows, *, name, tq=256):
    t = pb.shape[0]
    npair = pb.shape[1] // 3 // LANES
    tq = min(tq, t)
    nq = t // tq
    cmat, _ = _sb_consts()
    scale = 1.0 / math.sqrt(LANES // 2)

    nseg = len(seg_rows)

    def body(q_ref, k_ref, v_ref, c_ref, late_ref, y_ref, lt_ref, *rest):
        i = pl.program_id(1)
        pair = pl.program_id(0)
        scratch = rest[nseg:nseg + 4]
        start, forward, finish = _gather_phases(late_ref, rest[:nseg], seg_rows, *rest[nseg + 4:])
        pl.when((pair == 0) & (i == 0))(start)
        pl.when((pair == npair - 1) & (i == nq // 2))(forward)
        lane = lax.broadcasted_iota(jnp.int32, (tq, LANES), 1)
        rowpos = i * tq + lax.broadcasted_iota(jnp.int32, (tq, SB_KEYS), 0)
        colid = lax.broadcasted_iota(jnp.int32, (tq, SB_KEYS), 1)
        q2 = q_ref[...] * jnp.asarray(scale, BF16)
        cm = c_ref[...]
        hi_lanes = lane >= LANES // 2
        qhs = [jnp.where(hi_lanes == (hh == 1), q2, jnp.zeros_like(q2)) for hh in range(2)]
        per_q = tq // SB_KEYS

        def blk(jb):
            return pl.ds(pl.multiple_of(jb * SB_KEYS, SB_KEYS), SB_KEYS)

        zbuf, wbuf, accbuf, runbuf = scratch

        def scores(jb):
            kb = k_ref[blk(jb), :]
            for hh in range(2):
                zbuf[hh] = _dot(qhs[hh], kb, 1, 1)

        def values(jb):
            vb = v_ref[blk(jb), :]
            for hh in range(2):
                accbuf[hh] += _dot(wbuf[hh], vb)

        def trip(jb, masked, first=False):
            mask = (jb * SB_KEYS + colid) < rowpos if masked else None
            if not first:
                values(jb + 1)
            pre, css = [], []
            for hh in range(2):
                z = zbuf[hh]
                lb = _log_sigmoid(z)
                lk = lb - z
                if masked:
                    lk = jnp.where(mask, lk, 0.0)
                lk_hi, lk_lo = _split2(lk)
                css.append(_dot(lk_hi, cm) + _dot(lk_lo, cm))
                run = runbuf[hh]
                pre.append(lb + run)
                runbuf[hh] = run + jnp.sum(lk, axis=1, keepdims=True)
            scores(jnp.maximum(jb - 1, 0))
            for hh in range(2):
                w = jnp.exp(pre[hh] + css[hh])
                if masked:
                    w = jnp.where(mask, w, 0.0)
                wbuf[hh] = w.astype(BF16)

        nfull = i * per_q
        accbuf[...] = jnp.zeros_like(accbuf)
        runbuf[...] = jnp.zeros_like(runbuf)
        scores(nfull + per_q - 1)
        for dblk in reversed(range(per_q)):
            trip(nfull + dblk, True, first=dblk == per_q - 1)

        def full_block(n, carry):
            trip(nfull - 1 - n, False)
            return carry

        lax.fori_loop(0, nfull, full_block, 0)
        values(0)
        y_ref[...] = jnp.where(hi_lanes, accbuf[1], accbuf[0]).astype(BF16)
        lt_ref[...] = jnp.where(hi_lanes, runbuf[1], runbuf[0])
        pl.when((pair == npair - 1) & (i == nq - 1))(finish)

    return pl.pallas_call(
        body, name=name, grid=(npair, nq),
        in_specs=[pl.BlockSpec((tq, LANES), lambda p, i: (i, p)),
                  pl.BlockSpec((t, LANES), lambda p, i: (0, npair + p)),
                  pl.BlockSpec((t, LANES), lambda p, i: (0, 2 * npair + p)),
                  pl.BlockSpec((SB_KEYS, SB_KEYS), lambda p, i: (0, 0)),
                  HBM_SPEC],
        out_specs=[pl.BlockSpec((tq, LANES), lambda p, i: (i, p))] * 2 + [HBM_SPEC] * nseg,
        out_shape=[jax.ShapeDtypeStruct((t, npair * LANES), BF16), jax.ShapeDtypeStruct((t, npair * LANES), F32),
                   ] + [jax.ShapeDtypeStruct((8, n, late_pack.shape[1]), late_pack.dtype) for n in seg_rows],
        scratch_shapes=[pltpu.VMEM((2, tq, SB_KEYS), F32), pltpu.VMEM((2, tq, SB_KEYS), BF16),
                        pltpu.VMEM((2, tq, LANES), F32), pltpu.VMEM((2, tq, 1), F32)] + _gather_scratch(),
        compiler_params=_cp(dimension_semantics=("arbitrary", "arbitrary")))(pb, pb, pb, cmat, late_pack)


def _attn_bwd(pb, dy, ltot, send, *, name, tq=256):
    t = pb.shape[0]
    npair = pb.shape[1] // 3 // LANES
    tq = min(tq, t)
    nq = t // tq
    _, cmats = _sb_consts()
    scale = 1.0 / math.sqrt(LANES // 2)
    sends = list(send)
    ns = len(sends)

    def body(q_ref, k_ref, v_ref, dy_ref, lt_ref, c_ref, *rest):
        i = pl.program_id(1)
        pair = pl.program_id(0)
        send_refs = rest[:ns]
        dq_ref, dk_ref, dv_ref, parts_ref, dk_acc, dv_acc = rest[ns:ns + 6]
        scratch = rest[ns + 6:ns + 12]
        start, finish = _direct_exchange_phases(send_refs, parts_ref, *rest[ns + 12:])
        pl.when((pair == 0) & (i == 0))(start)

        @pl.when(i == 0)
        def _():
            dk_acc[...] = jnp.zeros_like(dk_acc)
            dv_acc[...] = jnp.zeros_like(dv_acc)

        lane = lax.broadcasted_iota(jnp.int32, (tq, LANES), 1)
        rowpos = i * tq + lax.broadcasted_iota(jnp.int32, (tq, SB_KEYS), 0)
        colid = lax.broadcasted_iota(jnp.int32, (tq, SB_KEYS), 1)
        q2 = q_ref[...] * jnp.asarray(scale, BF16)
        do2 = dy_ref[...].astype(BF16)
        ltv = lt_ref[...]
        c_upto, c_before = c_ref[0], c_ref[1]
        hi_lanes = lane >= LANES // 2
        sels = [hi_lanes == (hh == 1) for hh in range(2)]
        qhs = [jnp.where(s, q2, jnp.zeros_like(q2)) for s in sels]
        dohs = [jnp.where(s, do2, jnp.zeros_like(do2)) for s in sels]
        lts = [ltv[:, 0:1], ltv[:, LANES // 2:LANES // 2 + 1]]
        per_q = tq // SB_KEYS

        def blk(jb):
            return pl.ds(pl.multiple_of(jb * SB_KEYS, SB_KEYS), SB_KEYS)

        zbuf, dabuf, dzbuf, abuf, dqbuf, sumbuf = scratch

        def scores(jb):
            kb, vb = k_ref[blk(jb), :], v_ref[blk(jb), :]
            for hh in range(2):
                zbuf[hh] = _dot(qhs[hh], kb, 1, 1)
                dabuf[hh] = _dot(dohs[hh], vb, 1, 1)

        def products(jb):
            kb = k_ref[blk(jb), :]
            dk_acc[blk(jb), :] += _dot(dzbuf[0], qhs[0], 0, 0) + _dot(dzbuf[1], qhs[1], 0, 0)
            dv_acc[blk(jb), :] += _dot(abuf[0], dohs[0], 0, 0) + _dot(abuf[1], dohs[1], 0, 0)
            for hh in range(2):
                dqbuf[hh] += _dot(dzbuf[hh], kb)

        def trip(jb, masked):
            mask = (jb * SB_KEYS + colid) < rowpos if masked else None
            products(jnp.maximum(jb - 1, 0))
            lbs, css, es, ces = [], [], [], []
            for hh in range(2):
                z = zbuf[hh]
                lb = _log_sigmoid(z)
                lk = lb - z
                if masked:
                    lk = jnp.where(mask, lk, 0.0)
                lk_hi, lk_lo = _split2(lk)
                css.append(_dot(lk_hi, c_upto) + _dot(lk_lo, c_upto))
                csum = sumbuf[2 * hh]
                lbs.append((lb, lb + (lts[hh] - csum)))
                sumbuf[2 * hh] = csum + jnp.sum(lk, axis=1, keepdims=True)
            for hh in range(2):
                a = jnp.exp(lbs[hh][1] - css[hh])
                if masked:
                    a = jnp.where(mask, a, 0.0)
                e = a * dabuf[hh]
                e_hi, e_lo = _split2(e)
                ces.append(_dot(e_hi, c_before) + _dot(e_lo, c_before))
                abuf[hh] = a.astype(BF16)
                es.append(e)
            scores(jnp.minimum(jb + 1, last))
            for hh in range(2):
                prun = sumbuf[2 * hh + 1]
                beta = jnp.exp(lbs[hh][0])
                dz = es[hh] * (1.0 - beta) - (prun + ces[hh]) * beta
                if masked:
                    dz = jnp.where(mask, dz, 0.0)
                dzbuf[hh] = dz.astype(BF16)
                sumbuf[2 * hh + 1] = prun + jnp.sum(es[hh], axis=1, keepdims=True)

        nfull = i * per_q
        last = nfull + per_q - 1
        for buf in (dzbuf, abuf, dqbuf, sumbuf):
            buf[...] = jnp.zeros_like(buf)
        scores(0)

        def full_block(jb, carry):
            trip(jb, False)
            return carry

        lax.fori_loop(0, nfull, full_block, 0)
        for dblk in range(per_q):
            trip(nfull + dblk, True)
        products(last)
        dq_ref[...] = (jnp.where(hi_lanes, dqbuf[1], dqbuf[0]) * scale).astype(BF16)

        @pl.when(i == nq - 1)
        def _():
            dk_ref[...] = dk_acc[...].astype(BF16)
            dv_ref[...] = dv_acc[...].astype(BF16)

        pl.when((pair == npair - 1) & (i == nq - 1))(finish)

    blk = pl.BlockSpec((tq, LANES), lambda p, i: (i, p))
    full = pl.BlockSpec((t, LANES), lambda p, i: (0, p))
    return pl.pallas_call(
        body, name=name, grid=(npair, nq),
        in_specs=[blk,
                  pl.BlockSpec((t, LANES), lambda p, i: (0, npair + p)),
                  pl.BlockSpec((t, LANES), lambda p, i: (0, 2 * npair + p)),
                  pl.BlockSpec((tq, LANES), lambda p, i: (i, npair + p)),
                  blk,
                  pl.BlockSpec((2, SB_KEYS, SB_KEYS), lambda p, i: (0, 0, 0))] + [HBM_SPEC] * ns,
        out_specs=[blk, full, full, HBM_SPEC],
        out_shape=[jax.ShapeDtypeStruct((t, npair * LANES), BF16)] * 3
        + [jax.ShapeDtypeStruct((8, _direct_exchange_rows(sends), sends[0].shape[3]), sends[0].dtype)],
        scratch_shapes=[pltpu.VMEM((t, LANES), F32), pltpu.VMEM((t, LANES), F32),
                        pltpu.VMEM((2, tq, SB_KEYS), F32), pltpu.VMEM((2, tq, SB_KEYS), F32),
                        pltpu.VMEM((2, tq, SB_KEYS), BF16), pltpu.VMEM((2, tq, SB_KEYS), BF16),
                        pltpu.VMEM((2, tq, LANES), F32), pltpu.VMEM((4, tq, 1), F32)] + _direct_exchange_scratch(),
        compiler_params=_cp(dimension_semantics=("arbitrary", "arbitrary")))(pb, pb, pb, dy, ltot, cmats, *sends)


def _hgrn_consts():
    t = lax.broadcasted_iota(jnp.int32, (CHUNK, CHUNK), 0)
    s = lax.broadcasted_iota(jnp.int32, (CHUNK, CHUNK), 1)
    masks = []
    for lvl in range(N_LEVELS):
        half = CHUNK >> (lvl + 1)
        same = (t // (2 * half)) == (s // (2 * half))
        masks.append((same & (t % (2 * half) >= half) & (s % (2 * half) < half)).astype(F32))
    masks.append((t == s).astype(F32))
    prefix = (s <= t).astype(BF16)
    suffix = (s >= t).astype(BF16)
    return prefix, jnp.stack(masks), suffix


def _hgrn_gates(qr, fr, lbv):
    sg = 1.0 / (1.0 + jnp.exp(-fr))
    fval = lbv + (1.0 - lbv) * sg
    kk = (1.0 - lbv) * (1.0 / (1.0 + jnp.exp(fr)))
    sq = _sigmoid(qr)
    return sg, fval, jnp.log(fval), kk, sq, qr * sq


def _lower_bound(c_ref):
    c = c_ref[...]
    mx = jnp.max(c, axis=0, keepdims=True)
    ex = jnp.exp(c - mx)
    return ex[1:2, :] / jnp.sum(ex, axis=0, keepdims=True)


def _level_ref(b, lvl):
    half = CHUNK >> (lvl + 1)
    seg = 2 * half
    if seg >= 8:
        b3 = b.reshape(CHUNK // seg, seg, LANES)
        return jnp.broadcast_to(b3[:, half - 1:half, :], b3.shape).reshape(CHUNK, LANES)
    pos = lax.broadcasted_iota(jnp.int32, b.shape, 0) % seg
    out = b
    for p in range(seg):
        if p != half - 1:
            out = jnp.where(pos == p, pltpu.roll(b, (p - (half - 1)) % CHUNK, 0), out)
    return out


def _hgrn_levels(b, qs, kk):
    out = []
    for lvl in range(N_LEVELS):
        fac = jnp.exp(-jnp.abs(b - _level_ref(b, lvl)))
        out.append((qs * fac, kk * fac, fac, fac))
    out.append((qs, kk, None, None))
    return out


def _split2(x):
    hi = x.astype(BF16)
    return hi, (x - hi.astype(F32)).astype(BF16)


def _hgrn_fwd(pc, c_lb, out_norm, *, name, tc=1024):
    t = pc.shape[0]
    nh = pc.shape[1] // 4 // LANES
    tc = min(tc, t)
    nch = tc // CHUNK
    cum_all, masks, _ = _hgrn_consts()

    def body(q_ref, f_ref, i_ref, g_ref, lb_ref, on_ref, cum_ref, m_ref, y_ref, o_ref, st_ref, state):
        @pl.when(pl.program_id(1) == 0)
        def _():
            state[...] = jnp.zeros_like(state)

        lbv = _lower_bound(lb_ref)
        onv = on_ref[...]

        def chunk(c, carry):
            rows = pl.ds(pl.multiple_of(c * CHUNK, CHUNK), CHUNK)
            for hh in range(HGRN_HEADS):
                lanes = slice(hh * LANES, (hh + 1) * LANES)
                _, _, g, kk, _, qs = _hgrn_gates(q_ref[rows, lanes], f_ref[rows, lanes], lbv[:, lanes])
                vb = i_ref[rows, lanes].astype(BF16)
                b = _dot_exact_lhs(cum_ref[...], g)
                scores = jnp.zeros((CHUNK, CHUNK), F32)
                for lvl, (ql, kl, _, _) in enumerate(_hgrn_levels(b, qs, kk)):
                    scores = scores + _dot(ql.astype(BF16), kl.astype(BF16), 1, 1) * m_ref[lvl]
                st = state[hh]
                st_ref[hh, c] = st
                o = _dot(scores.astype(BF16), vb) + _dot((qs * jnp.exp(b)).astype(BF16), st.astype(BF16), 1, 1)
                blast = b[CHUNK - 1:CHUNK, :]
                kdec = (kk * jnp.exp(blast - b)).astype(BF16)
                state[hh] = st * jnp.exp(blast) + _dot(vb, kdec, 0, 0)
                o_ref[rows, lanes] = o
                rstd = lax.rsqrt(jnp.mean(o * o, axis=-1, keepdims=True) + RMS_EPS)
                gate = g_ref[rows, lanes]
                y_ref[rows, lanes] = (o * rstd * onv * (gate * _sigmoid(gate))).astype(BF16)
            return carry

        lax.fori_loop(0, nch, chunk, 0, unroll=2)

    hw = HGRN_HEADS * LANES

    def col(off):
        return pl.BlockSpec((tc, hw), lambda h, i: (i, off // HGRN_HEADS + h))

    osp = pl.BlockSpec((tc, hw), lambda h, i: (i, h))
    return pl.pallas_call(
        body, name=name, grid=(nh // HGRN_HEADS, t // tc),
        in_specs=[col(0), col(nh), col(2 * nh), col(3 * nh),
                  pl.BlockSpec((2, hw), lambda h, i: (0, h)),
                  pl.BlockSpec((1, LANES), lambda h, i: (0, 0)),
                  pl.BlockSpec(cum_all.shape, lambda h, i: (0, 0)),
                  pl.BlockSpec(masks.shape, lambda h, i: (0, 0, 0))],
        out_specs=[osp, osp, pl.BlockSpec((HGRN_HEADS, nch, LANES, LANES), lambda h, i: (h, i, 0, 0))],
        out_shape=[jax.ShapeDtypeStruct((t, nh * LANES), BF16), jax.ShapeDtypeStruct((t, nh * LANES), F32),
                   jax.ShapeDtypeStruct((nh, t // CHUNK, LANES, LANES), F32)],
        scratch_shapes=[pltpu.VMEM((HGRN_HEADS, LANES, LANES), F32)],
        compiler_params=_cp())(pc, pc, pc, pc, c_lb, out_norm, cum_all, masks)


def _hgrn_bwd(pc, o_saved, states, dy, c_lb, out_norm, send, *, name, tc=1024):
    t = pc.shape[0]
    nh = pc.shape[1] // 4 // LANES
    tc = min(tc, t)
    nch = tc // CHUNK
    nt = t // tc
    cum_all, masks, suffix = _hgrn_consts()
    ngroup = nh // HGRN_HEADS
    sends = list(send)
    ns = len(sends)

    def body(q_ref, f_ref, i_ref, g_ref, o_ref, st_ref, dy_ref, lb_ref, on_ref, cum_ref, m_ref, suf_ref, *rest):
        send_refs = rest[:ns]
        dq_ref, df_ref, di_ref, dg_ref, dlb_ref, don_ref, parts_ref, dstate = rest[ns:ns + 8]
        start, finish = _direct_exchange_phases(send_refs, parts_ref, *rest[ns + 8:])
        pl.when((pl.program_id(0) == 0) & (pl.program_id(1) == 0))(start)

        @pl.when(pl.program_id(1) == 0)
        def _():
            dstate[...] = jnp.zeros_like(dstate)
            dlb_ref[...] = jnp.zeros_like(dlb_ref)
            don_ref[...] = jnp.zeros_like(don_ref)

        lbv = _lower_bound(lb_ref)
        onv = on_ref[...]

        def head(hh, c, rows):
            lanes = slice(hh * LANES, (hh + 1) * LANES)
            qr = q_ref[rows, lanes]
            sg, fval, g, kk, sq, qs = _hgrn_gates(qr, f_ref[rows, lanes], lbv[:, lanes])
            vb = i_ref[rows, lanes].astype(BF16)
            o = o_ref[rows, lanes]
            gate = g_ref[rows, lanes]
            sgt = _sigmoid(gate)
            rstd = lax.rsqrt(jnp.mean(o * o, axis=-1, keepdims=True) + RMS_EPS)
            ohat = o * rstd
            dyv = dy_ref[rows, lanes]
            don = dyv * (gate * sgt)
            dg_ref[rows, lanes] = (dyv * ohat * onv * (sgt * (1.0 + gate * (1.0 - sgt)))).astype(BF16)
            don_ref[:, lanes] += jnp.sum(don * ohat, axis=0, keepdims=True)
            dxhat = don * onv
            dob = (rstd * (dxhat - ohat * jnp.mean(dxhat * ohat, axis=-1, keepdims=True))).astype(BF16)
            b = _dot_exact_lhs(cum_ref[...], g)
            blast = b[CHUNK - 1:CHUNK, :]
            eb = jnp.exp(b)
            edec = jnp.exp(blast - b)
            st32 = st_ref[hh, c]
            st = st32.astype(BF16)
            dst = dstate[hh]
            dstb = dst.astype(BF16)
            da = _dot(dob, vb, 1, 1)
            levels = _hgrn_levels(b, qs, kk)
            scores = jnp.zeros((CHUNK, CHUNK), F32)
            dq = eb * _dot(dob, st)
            dk_inter = edec * _dot(vb, dstb)
            dk = dk_inter
            for lvl, (ql, kl, eq, ek) in enumerate(levels):
                mk = m_ref[lvl]
                (qh, qlo), (kh, klo) = _split2(ql), _split2(kl)
                scores = scores + _dot(qh, kh, 1, 1) * mk
                dal = (da * mk).astype(BF16)
                dql = _dot(dal, kh) + _dot(dal, klo)
                dkl = _dot(dal, qh, 0, 0) + _dot(dal, qlo, 0, 0)
                dq = dq + (dql if eq is None else dql * eq)
                dk = dk + (dkl if ek is None else dkl * ek)
            kdec = (kk * edec).astype(BF16)
            dv = _dot(scores.astype(BF16), dob, 0, 0) + _dot(kdec, dstb, 1, 1)
            dstate[hh] = dst * jnp.exp(blast) + _dot(dob, (qs * eb).astype(BF16), 0, 0)
            db = qs * dq - kk * dk
            last = jnp.sum(kk * dk_inter, axis=0, keepdims=True) + jnp.exp(blast) * jnp.sum(dst * st32, axis=0, keepdims=True)
            dgl = _dot_exact_lhs(suf_ref[...], db) + last
            dfv = dgl / fval - dk
            df_ref[rows, lanes] = (dfv * (1.0 - lbv[:, lanes]) * sg * (1.0 - sg)).astype(BF16)
            dlb_ref[:, lanes] += jnp.sum(dfv * (1.0 - sg), axis=0, keepdims=True)
            dq_ref[rows, lanes] = (dq * (sq * (1.0 + qr * (1.0 - sq)))).astype(BF16)
            di_ref[rows, lanes] = dv.astype(BF16)

        def chunk(n, carry):
            c = nch - 1 - n
            rows = pl.ds(pl.multiple_of(c * CHUNK, CHUNK), CHUNK)
            for hh in range(HGRN_HEADS):
                head(hh, c, rows)
            return carry

        lax.fori_loop(0, nch, chunk, 0, unroll=2)
        pl.when((pl.program_id(0) == ngroup - 1) & (pl.program_id(1) == nt - 1))(finish)

    hw = HGRN_HEADS * LANES

    def col(off):
        return pl.BlockSpec((tc, hw), lambda h, i: (nt - 1 - i, off // HGRN_HEADS + h))

    osp = pl.BlockSpec((tc, hw), lambda h, i: (nt - 1 - i, h))
    vec = pl.BlockSpec((1, hw), lambda h, i: (0, h))
    return pl.pallas_call(
        body, name=name, grid=(nh // HGRN_HEADS, nt),
        in_specs=[col(0), col(nh), col(2 * nh), col(3 * nh), osp,
                  pl.BlockSpec((HGRN_HEADS, nch, LANES, LANES), lambda h, i: (h, nt - 1 - i, 0, 0)),
                  osp,
                  pl.BlockSpec((2, hw), lambda h, i: (0, h)),
                  pl.BlockSpec((1, LANES), lambda h, i: (0, 0)),
                  pl.BlockSpec(cum_all.shape, lambda h, i: (0, 0)),
                  pl.BlockSpec(masks.shape, lambda h, i: (0, 0, 0)),
                  pl.BlockSpec(suffix.shape, lambda h, i: (0, 0))] + [HBM_SPEC] * ns,
        out_specs=[osp, osp, osp, osp, vec, vec, HBM_SPEC],
        out_shape=[jax.ShapeDtypeStruct((t, nh * LANES), BF16)] * 4 + [jax.ShapeDtypeStruct((1, nh * LANES), F32)] * 2
        + [jax.ShapeDtypeStruct((8, _direct_exchange_rows(sends), sends[0].shape[3]), sends[0].dtype)],
        scratch_shapes=[pltpu.VMEM((HGRN_HEADS, LANES, LANES), F32)] + _direct_exchange_scratch(),
        compiler_params=_cp(dimension_semantics=("arbitrary", "arbitrary")))(
            pc, pc, pc, pc, o_saved, states, dy, c_lb, out_norm, cum_all, masks, suffix, *sends)


HBM_SPEC = pl.BlockSpec(memory_space=pltpu.HBM)


def _gather_scratch():
    return [pltpu.SemaphoreType.DMA((7,)), pltpu.SemaphoreType.DMA((7,)), pltpu.SemaphoreType.DMA]


def _gather_phases(x_ref, out_refs, seg_rows, send_sems, recv_sems, local_sem):
    x, y, c = lax.axis_index("x"), lax.axis_index("y"), lax.axis_index("c")
    me, sibling = (x, y, c), (x, y, 1 - c)
    chips = [(1 - x, y), (x, 1 - y), (1 - x, 1 - y)]
    offs = [sum(seg_rows[:s]) for s in range(len(seg_rows))]
    assert sum(seg_rows) == x_ref.shape[0]

    def index(px, py, pc):
        return 4 * px + 2 * py + pc

    def copies(k, block, to, own):
        return [pltpu.make_async_remote_copy(
            src_ref=x_ref.at[pl.ds(offs[s], n)] if own else out_refs[s].at[index(*block)],
            dst_ref=out_refs[s].at[index(*block)],
            send_sem=send_sems.at[k], recv_sem=recv_sems.at[k], device_id=to, device_id_type=MESH)
            for s, n in enumerate(seg_rows)]

    def all_bytes(k):
        return pltpu.make_async_remote_copy(src_ref=x_ref, dst_ref=x_ref, send_sem=send_sems.at[k],
                                            recv_sem=recv_sems.at[k], device_id=me, device_id_type=MESH)

    mine = [pltpu.make_async_copy(x_ref.at[pl.ds(offs[s], n)], out_refs[s].at[index(*me)], local_sem)
            for s, n in enumerate(seg_rows)]
    first = copies(0, me, sibling, True)
    for j, chip in enumerate(chips):
        first += copies(1 + j, me, (*chip, c), True)

    def start():
        for cp in mine + first:
            cp.start()

    def forward():
        for j, chip in enumerate(chips):
            all_bytes(1 + j).wait_recv()
            for cp in copies(4 + j, (*chip, c), sibling, False):
                cp.start()

    def finish():
        all_bytes(0).wait_recv()
        for j in range(3):
            all_bytes(4 + j).wait_recv()
        for k in range(7):
            all_bytes(k).wait_send()
        pltpu.make_async_copy(x_ref, x_ref, local_sem).wait()

    return start, forward, finish


def _all_gather(xs, seg_rows=None, *, name):
    segs = [xs.shape[0]] if seg_rows is None else list(seg_rows)

    def body(x_ref, *rest):
        start, forward, finish = _gather_phases(x_ref, rest[:len(segs)], segs, *rest[len(segs):])
        start()
        forward()
        finish()

    outs = pl.pallas_call(
        body, name=name, in_specs=[HBM_SPEC], out_specs=[HBM_SPEC] * len(segs),
        out_shape=[jax.ShapeDtypeStruct((8, n, xs.shape[1]), xs.dtype) for n in segs],
        scratch_shapes=_gather_scratch())(xs)
    return outs[0] if seg_rows is None else outs


def _sibling_exchange(s, *, name):
    def body(s_ref, rb_ref, send_sem, recv_sem):
        x, y, c = lax.axis_index("x"), lax.axis_index("y"), lax.axis_index("c")
        cp = pltpu.make_async_remote_copy(
            src_ref=s_ref.at[:, 1 - c], dst_ref=rb_ref, send_sem=send_sem, recv_sem=recv_sem,
            device_id=(x, y, 1 - c), device_id_type=MESH)
        cp.start()
        cp.wait()

    return pl.pallas_call(
        body, name=name, in_specs=[HBM_SPEC], out_specs=HBM_SPEC,
        out_shape=jax.ShapeDtypeStruct(s.shape[:1] + s.shape[2:], s.dtype),
        scratch_shapes=[pltpu.SemaphoreType.DMA, pltpu.SemaphoreType.DMA])(s)


def _row_tile(n, cap=1024):
    return max(b for b in range(16, cap + 1, 16) if n % b == 0)


def _pair_add(s, rb, core, *, name):
    nchip, _, r, c = s.shape
    tb = _row_tile(r)

    def body(core_ref, a_ref, b_ref, o_ref):
        o_ref[...] = (a_ref[...].astype(F32) + b_ref[...].astype(F32)).astype(BF16)

    blk = pl.BlockSpec((None, tb, c), lambda ch, i, cr: (ch, i, 0))
    return pl.pallas_call(
        body, name=name,
        grid_spec=pltpu.PrefetchScalarGridSpec(
            num_scalar_prefetch=1, grid=(nchip, r // tb),
            in_specs=[pl.BlockSpec((None, None, tb, c), lambda ch, i, cr: (ch, cr[0], i, 0)), blk],
            out_specs=blk),
        out_shape=jax.ShapeDtypeStruct((nchip, r, c), BF16), compiler_params=_cp())(core, s, rb)


def _chip_exchange_scratch():
    return [pltpu.SemaphoreType.DMA((3,)), pltpu.SemaphoreType.DMA((3,)), pltpu.SemaphoreType.DMA]


def _chip_exchange_phases(p_ref, out_ref, send_sems, recv_sems, local_sem):
    x, y, c = lax.axis_index("x"), lax.axis_index("y"), lax.axis_index("c")
    mine = 2 * x + y
    own = pltpu.make_async_copy(p_ref.at[mine], out_ref.at[mine], local_sem)
    copies = [pltpu.make_async_remote_copy(
        src_ref=p_ref.at[2 * tx + ty], dst_ref=out_ref.at[mine],
        send_sem=send_sems.at[k], recv_sem=recv_sems.at[k], device_id=(tx, ty, c), device_id_type=MESH)
        for k, (tx, ty) in enumerate([(1 - x, y), (x, 1 - y), (1 - x, 1 - y)])]

    def start():
        own.start()
        for cp in copies:
            cp.start()

    def finish():
        for cp in copies:
            cp.wait()
        own.wait()

    return start, finish


def _direct_exchange_scratch():
    return [pltpu.SemaphoreType.DMA((7,)), pltpu.SemaphoreType.DMA((7,)), pltpu.SemaphoreType.DMA]


def _direct_exchange_rows(sends):
    return sum(s.shape[2] for s in sends)


def _direct_exchange_phases(s_refs, out_ref, send_sems, recv_sems, local_sem):
    x, y, c = lax.axis_index("x"), lax.axis_index("y"), lax.axis_index("c")
    me = 4 * x + 2 * y + c
    offs, off = [], 0
    for s in s_refs:
        offs.append(off)
        off += s.shape[2]

    def slot(p):
        return out_ref.at[me, pl.ds(offs[p], s_refs[p].shape[2])]

    own = [pltpu.make_async_copy(s.at[2 * x + y, c], slot(p), local_sem) for p, s in enumerate(s_refs)]
    flips = [(fx, fy, fc) for fx in (0, 1) for fy in (0, 1) for fc in (0, 1) if (fx, fy, fc) != (0, 0, 0)]
    copies = []
    for k, (fx, fy, fc) in enumerate(flips):
        tx, ty, tc = (1 - x if fx else x), (1 - y if fy else y), (1 - c if fc else c)
        copies += [pltpu.make_async_remote_copy(
            src_ref=s.at[2 * tx + ty, tc], dst_ref=slot(p),
            send_sem=send_sems.at[k], recv_sem=recv_sems.at[k], device_id=(tx, ty, tc), device_id_type=MESH)
            for p, s in enumerate(s_refs)]

    def start():
        for cp in own + copies:
            cp.start()

    def finish():
        whole = out_ref.at[me]
        for k in range(len(flips)):
            pltpu.make_async_remote_copy(src_ref=whole, dst_ref=whole, send_sem=send_sems.at[k],
                                         recv_sem=recv_sems.at[k], device_id=(x, y, c), device_id_type=MESH).wait()
        pltpu.make_async_copy(whole, whole, local_sem).wait()

    return start, finish


def _adamw_math(w, g, m, v):
    m2 = ADAM_B1 * m + (1.0 - ADAM_B1) * g
    v2 = ADAM_B2 * v + (1.0 - ADAM_B2) * (g * g)
    m_hat = m2 / (1.0 - ADAM_B1 ** ADAM_STEP)
    v_hat = v2 / (1.0 - ADAM_B2 ** ADAM_STEP)
    return -ADAM_LR * (m_hat / (jnp.sqrt(v_hat) + ADAM_EPS) + ADAM_WD * w), m2, v2


def _adamw_shard(parts, g_off, w, m, v, layer, prev, *, name):
    _, r, c = w.shape
    npart = parts.shape[0]
    tb = next(b for b in range(min(r, 512), 0, -16) if r % b == 0 and g_off % b == 0)

    def body(*refs):
        w_ref, m_ref, v_ref = refs[npart:npart + 3]
        g_out, d_out, m_out, v_out = refs[-4:]
        g = refs[0][...].astype(F32)
        for p_ref in refs[1:npart]:
            g = g + p_ref[...].astype(F32)
        d, m2, v2 = _adamw_math(w_ref[...], g, m_ref[...], v_ref[...])
        g_out[...] = g
        d_out[...] = d
        m_out[...] = m2
        v_out[...] = v2

    def part(ch):
        return pl.BlockSpec((None, tb, c), lambda i: (ch, g_off // tb + i, 0))

    blk = pl.BlockSpec((None, tb, c), lambda i: (layer, i, 0))
    prev = list(prev) if prev is not None else []
    return pl.pallas_call(
        body, name=name, grid=(r // tb,),
        in_specs=[part(ch) for ch in range(npart)] + [blk, blk, blk] + [pl.BlockSpec(memory_space=pl.ANY)] * len(prev),
        out_specs=[blk] * 4, out_shape=[jax.ShapeDtypeStruct(w.shape, F32)] * 4,
        input_output_aliases={npart + 3 + k: k for k in range(len(prev))},
        compiler_params=_cp())(*([parts] * npart), w, m, v, *prev)


SLOT = 8
SMALL_ROWS = 6 * SLOT
ROW_LB = 4 * SLOT


def _small_update(gath, w, m, v, *, name):
    def body(g_ref, w_ref, m_ref, v_ref, g_out, d_out, m_out, v_out):
        tot = g_ref[0]
        for k in range(1, 8):
            tot = tot + g_ref[k]
        wv = w_ref[...]
        c0, c1 = wv[ROW_LB:ROW_LB + 1, :], wv[ROW_LB + 1:ROW_LB + 2, :]
        mx = jnp.maximum(c0, c1)
        e0, e1 = jnp.exp(c0 - mx), jnp.exp(c1 - mx)
        lb = e1 / (e0 + e1)
        gl = tot[ROW_LB:ROW_LB + 1, :] * lb * (1.0 - lb)
        row = lax.broadcasted_iota(jnp.int32, tot.shape, 0)
        g = jnp.where(row == ROW_LB, -gl, jnp.where(row == ROW_LB + 1, gl, tot))
        d, m2, v2 = _adamw_math(wv, g, m_ref[...], v_ref[...])
        g_out[...] = g
        d_out[...] = d
        m_out[...] = m2
        v_out[...] = v2

    return pl.pallas_call(
        body, name=name, out_shape=[jax.ShapeDtypeStruct(w.shape, F32)] * 4, compiler_params=_cp())(gath, w, m, v)


D_MODEL = 1024


def _ffn_fwd(h, gain, wg, wu, wd, tag):
    xn, gg, uu, act = _norm_gate_up(h, gain, wg, wu, name=f"{tag}_gate_up")
    out = _mm([(act, wd)], residual=h, alpha=MACARON, tn=1024, name=f"{tag}_down")
    return out, (h, xn, gg, uu, act)


def _ffn_input_bwd(dg, du, wg, wu, x, gain, dres, chip_part, *, name, scale, tm=256):
    t, d = x.shape
    f = wg.shape[0]
    tm = min(tm, t)
    nt = t // tm
    fused = chip_part is not None

    def body(dg_ref, du_ref, wg_ref, wu_ref, x_ref, g_ref, dres_ref, *rest):
        if fused:
            part_ref, dx_ref, dxb_ref, dgain_ref, parts_ref = rest[:5]
            start, finish = _chip_exchange_phases(part_ref, parts_ref, *rest[5:])
            pl.when(pl.program_id(0) == 0)(start)
        else:
            dx_ref, dxb_ref, dgain_ref = rest
        dxn_v = _dot(dg_ref[...], wg_ref[...]) + _dot(du_ref[...], wu_ref[...])
        xv = x_ref[...]
        rstd = lax.rsqrt(jnp.mean(xv * xv, axis=-1, keepdims=True) + RMS_EPS)
        xhat = xv * rstd
        dxhat = dxn_v * g_ref[...]
        dx = dres_ref[...] + rstd * (dxhat - xhat * jnp.mean(dxhat * xhat, axis=-1, keepdims=True))
        dx_ref[...] = dx
        dxb_ref[...] = (dx * scale).astype(BF16)

        @pl.when(pl.program_id(0) == 0)
        def _():
            dgain_ref[...] = jnp.zeros_like(dgain_ref)

        dgain_ref[...] += jnp.sum(dxn_v * xhat, axis=0, keepdims=True)
        if fused:
            pl.when(pl.program_id(0) == nt - 1)(finish)

    wide = pl.BlockSpec((tm, f), lambda i: (i, 0))
    wsp = pl.BlockSpec((f, d), lambda i: (0, 0))
    row = pl.BlockSpec((tm, d), lambda i: (i, 0))
    vec = pl.BlockSpec((1, d), lambda i: (0, 0))
    args = [dg, du, wg, wu, x, gain, dres] + ([chip_part] if fused else [])
    return pl.pallas_call(
        body, name=name, grid=(nt,),
        in_specs=[wide, wide, wsp, wsp, row, vec, row] + ([HBM_SPEC] if fused else []),
        out_specs=[row, row, vec] + ([HBM_SPEC] if fused else []),
        out_shape=[jax.ShapeDtypeStruct((t, d), F32), jax.ShapeDtypeStruct((t, d), BF16), jax.ShapeDtypeStruct((1, d), F32)]
        + ([jax.ShapeDtypeStruct(chip_part.shape, chip_part.dtype)] if fused else []),
        scratch_shapes=_chip_exchange_scratch() if fused else [],
        compiler_params=_cp(dimension_semantics=("arbitrary",)))(*args)


def _ffn_bwd(dout, dout_half, saved, gain, wg, wu, wd, tag, next_scale, exchanges=None):
    h, xn, gg, uu, act = saved
    early_chip_part, send_after_dwd, chip_part_after_dwgu = exchanges if exchanges is not None else (None, None, None)
    dg, du, *early_parts = _swiglu_bwd(dout_half, wd, gg, uu, early_chip_part, tm=256, tf=wd.shape[0],
                                       name=f"{tag}_dact")
    dwd = _mm([(act, dout_half)], ta=True, tm=256, tn=1024, out_dtype=BF16, name=f"{tag}_dwd")
    send = send_after_dwd(dwd) if exchanges is not None else None
    dwg, dwu, *mid_parts = _mm_shared_rhs([dg, du], xn, tm=256, send=send, name=f"{tag}_dwgu")
    chip_part = chip_part_after_dwgu(dwg, dwu) if exchanges is not None else None
    dh, dh_b, dgain, *parts = _ffn_input_bwd(dg, du, wg, wu, h, gain, dout, chip_part, scale=next_scale,
                                             name=f"{tag}_input_bwd")
    return dh, dh_b, dwg, dwu, dwd, dgain, (early_parts + mid_parts + parts)


def kernel(x, ffn_pre_norm, ffn_pre_w_gate, ffn_pre_w_up, ffn_pre_w_down, mix_norm, ffn_post_norm, ffn_post_w_gate, ffn_post_w_up, ffn_post_w_down, ab_w_in, ab_conv_w, ab_w_out, c_w_in, c_lower_bounds, c_out_norm, c_w_out, final_norm, loss_target, m_ffn_pre_norm, m_ffn_pre_w_gate, m_ffn_pre_w_up, m_ffn_pre_w_down, m_mix_norm, m_ffn_post_norm, m_ffn_post_w_gate, m_ffn_post_w_up, m_ffn_post_w_down, m_ab_w_in, m_ab_conv_w, m_ab_w_out, m_c_w_in, m_c_lower_bounds, m_c_out_norm, m_c_w_out, m_final_norm, v_ffn_pre_norm, v_ffn_pre_w_gate, v_ffn_pre_w_up, v_ffn_pre_w_down, v_mix_norm, v_ffn_post_norm, v_ffn_post_w_gate, v_ffn_post_w_up, v_ffn_post_w_down, v_ab_w_in, v_ab_conv_w, v_ab_w_out, v_c_w_in, v_c_lower_bounds, v_c_out_norm, v_c_w_out, v_final_norm):
    d = D_MODEL
    h0 = x[0]
    target = loss_target[0]
    core = lax.axis_index("c").astype(jnp.int32).reshape(1)

    big = [("pre_g", ffn_pre_w_gate, m_ffn_pre_w_gate, v_ffn_pre_w_gate),
           ("pre_u", ffn_pre_w_up, m_ffn_pre_w_up, v_ffn_pre_w_up),
           ("pre_d", ffn_pre_w_down, m_ffn_pre_w_down, v_ffn_pre_w_down),
           ("post_g", ffn_post_w_gate, m_ffn_post_w_gate, v_ffn_post_w_gate),
           ("post_u", ffn_post_w_up, m_ffn_post_w_up, v_ffn_post_w_up),
           ("post_d", ffn_post_w_down, m_ffn_post_w_down, v_ffn_post_w_down),
           ("ab_in", ab_w_in, m_ab_w_in, v_ab_w_in),
           ("ab_out", ab_w_out, m_ab_w_out, v_ab_w_out),
           ("c_in", c_w_in, m_c_w_in, v_c_w_in),
           ("c_out", c_w_out, m_c_w_out, v_c_w_out)]
    by_tag = {tag: (w, m, v) for tag, w, m, v in big}

    def layer_rows(tag):
        w = by_tag[tag][0]
        return w.size // d // w.shape[0]

    def layout(items):
        offs, off = {}, 0
        for item in items:
            offs[item] = off
            off += layer_rows(item[0])
        return offs, off

    ffn = [f"{pos}_{kind}" for pos in ("pre", "post") for kind in "gud"]
    first_items = [("pre_g", 0), ("pre_u", 0)]
    early_items = [("pre_d", 0), ("ab_in", 0)]
    late_items = ([("pre_g", 1), ("pre_u", 1), ("pre_d", 1)] + [(f"post_{kind}", l) for l in (0, 1) for kind in "gud"]
                  + [("ab_out", 0), ("c_in", 0), ("c_out", 0)])
    grad_items = {"A0": [(f"post_{kind}", 1) for kind in "gud"] + [("c_out", 0)],
                  "A1": ([(f"pre_{kind}", 1) for kind in "gud"] + [(f"post_{kind}", 0) for kind in "gud"]
                         + [("c_in", 0), ("ab_out", 0)]),
                  "C": [("ab_in", 0)], "B0": [("pre_d", 0)], "B1": [("pre_g", 0), ("pre_u", 0)]}
    grad_offs = {k: layout(items)[0] for k, items in grad_items.items()}
    grad_conv_row = layout(grad_items["C"])[1]

    def conv_rows(a, split):
        flat = a.reshape(-1)
        if split:
            hi = flat.astype(BF16)
            flat = jnp.concatenate([hi, (flat - hi.astype(F32)).astype(BF16)])
        return jnp.zeros((16, d), flat.dtype).at[0, :flat.shape[0]].set(flat)

    nconv = ab_conv_w.size
    col_sharded = {"pre_g", "pre_u", "post_g", "post_u", "ab_in", "c_in"}

    def pack_rows(item):
        tag, layer = item
        a = by_tag[tag][0][layer]
        return (a.T if tag in col_sharded else a).reshape(-1, d).astype(BF16)

    first_pack = jnp.concatenate([pack_rows(item) for item in first_items], axis=0)
    early_pack = jnp.concatenate([pack_rows(item) for item in early_items] + [conv_rows(ab_conv_w, True)], axis=0)
    late_pack = jnp.concatenate([pack_rows(item) for item in late_items], axis=0)
    first_w = _all_gather(first_pack, [layer_rows(tag) for tag, _ in first_items], name="gather_first_weights")
    full = {item: g.reshape(-1, d) for item, g in zip(first_items, first_w)}

    xn0, gg0, uu0, act0, *early_w = _norm_gate_up(
        h0, ffn_pre_norm[0:1], full["pre_g", 0], full["pre_u", 0], name="l0pre_gate_up_gather_early_weights",
        pack=early_pack, seg_rows=[layer_rows(tag) for tag, _ in early_items] + [16])
    full.update({item: g.reshape(-1, d) for item, g in zip(early_items, early_w)})
    ffn_w = {("pre", 0): tuple(full[f"pre_{kind}", 0] for kind in "gud")}
    w_ab_in = full["ab_in", 0]
    cg = early_w[-1][:, 0, :2 * nconv].astype(F32)
    conv_w = (cg[:, :nconv] + cg[:, nconv:]).reshape(8, 3, -1).transpose(1, 0, 2).reshape(3, -1)
    aw = w_ab_in.shape[0] // 6
    h1 = _mm([(act0, full["pre_d", 0])], residual=h0, alpha=MACARON, tn=1024, name="l0pre_down")
    s_pre0 = (h0, xn0, gg0, uu0, act0)
    hn0, pa, pb = _norm_proj(h1, mix_norm[0:1], w_ab_in, (F32, BF16), tm=512, name="ab_norm_proj")
    ya = _conv_fwd(pa, conv_w, name="conv_fwd")
    yb, ltot, *late_w = _attn_fwd(pb, late_pack, [layer_rows(tag) for tag, _ in late_items],
                                  name="attn_fwd_gather_late_weights")
    full.update({item: g.reshape(-1, d) for item, g in zip(late_items, late_w)})
    for pos, layer in (("post", 0), ("pre", 1), ("post", 1)):
        ffn_w[pos, layer] = tuple(full[f"{pos}_{kind}", layer] for kind in "gud")
    w_ab_out, w_c_in, w_c_out = full["ab_out", 0], full["c_in", 0], full["c_out", 0]
    h2 = _mm([(ya, w_ab_out[:aw]), (yb, w_ab_out[aw:])], residual=h1, tn=1024, name="ab_out")
    h3, s_post0 = _ffn_fwd(h2, ffn_post_norm[0:1], *ffn_w["post", 0], "l0post")
    h4, s_pre1 = _ffn_fwd(h3, ffn_pre_norm[1:2], *ffn_w["pre", 1], "l1pre")
    hn1, pc = _norm_proj(h4, mix_norm[1:2], w_c_in, (F32,), tm=256, name="c_norm_proj")
    yc, o_saved, states = _hgrn_fwd(pc, c_lower_bounds, c_out_norm, name="hgrn_fwd")
    h5 = _mm([(yc, w_c_out)], residual=h4, tn=1024, name="c_out")
    h6, s_post1 = _ffn_fwd(h5, ffn_post_norm[1:2], *ffn_w["post", 1], "l1post")
    dh6, dh6_b, d_final, loss_vec = _loss_head(h6, final_norm.reshape(1, d), target, name="loss_head")

    gw = {}

    def grad_send(key, extra=()):
        return [g.reshape(4, 2, -1, d) for g in [gw[item] for item in grad_items[key]] + list(extra)]

    def chip_partials(key, extra=()):
        send = jnp.concatenate(grad_send(key, extra), axis=2)
        from_sibling = _sibling_exchange(send, name=f"grad{key}_sibling_exchange")
        return _pair_add(send, from_sibling, core, name=f"grad{key}_pair_add")

    dh5, dh5_b, gw["post_g", 1], gw["post_u", 1], gw["post_d", 1], d_post1, *_ = _ffn_bwd(
        dh6, dh6_b, s_post1, ffn_post_norm[1:2], *ffn_w["post", 1], "l1post", 1.0)
    dyc = _mm([(dh5_b, w_c_out)], tb=True, tn=1024, name="c_out_dy")
    g_c_out = _mm([(yc, dh5_b)], ta=True, tm=256, tn=1024, out_dtype=BF16, name="c_out_dw")
    gw["c_out", 0] = g_c_out
    dcq, dcf, dci, dcg, dlb, d_onorm, parts_a0 = _hgrn_bwd(pc, o_saved, states, dyc, c_lower_bounds, c_out_norm,
                                                           grad_send("A0"), name="hgrn_bwd_exchange_grads_a0")
    dparts = [dcq, dcf, dci, dcg]
    g_c_in = jnp.concatenate(_mm_shared_rhs(dparts, hn1, tm=256, name="c_in_dw"), axis=0)
    cw = w_c_in.shape[0] // 4
    dhn1 = _mm([(dp, w_c_in[i * cw:(i + 1) * cw]) for i, dp in enumerate(dparts)], tm=512, tn=1024, name="c_in_dx")
    dh4, dh4_b, d_mix1 = _rmsnorm_bwd(h4, mix_norm[1:2], dhn1, dh5, scale=MACARON, name="l1_mix_norm_bwd")
    dh3, dh3_b, gw["pre_g", 1], gw["pre_u", 1], gw["pre_d", 1], d_pre1, *_ = _ffn_bwd(
        dh4, dh4_b, s_pre1, ffn_pre_norm[1:2], *ffn_w["pre", 1], "l1pre", MACARON)
    dh2, dh2_b, gw["post_g", 0], gw["post_u", 0], gw["post_d", 0], d_post0, *_ = _ffn_bwd(
        dh3, dh3_b, s_post0, ffn_post_norm[0:1], *ffn_w["post", 0], "l0post", 1.0)
    dyab = _mm([(dh2_b, w_ab_out)], tb=True, tn=1024, name="ab_out_dy")
    g_ab_out = jnp.concatenate(_mm_shared_rhs([ya, yb], dh2_b, tm=256, name="ab_out_dw"), axis=0)
    dab, dac, dax, g_conv = _conv_bwd(pa, dyab, conv_w, name="conv_bwd")

    gw["c_in", 0], gw["ab_out", 0] = g_c_in, g_ab_out
    dq, dk, dv, parts_a1 = _attn_bwd(pb, dyab, ltot, grad_send("A1"), name="attn_bwd_exchange_grads_a1")
    dparts = [dab, dac, dax, dq, dk, dv]
    g_ab_in = jnp.concatenate(_mm_shared_rhs(dparts, hn0, tm=128, name="ab_in_dw"), axis=0)
    dhn0 = _mm([(dp, w_ab_in[i * aw:(i + 1) * aw]) for i, dp in enumerate(dparts)], tm=512, tn=1024, name="ab_in_dx")
    dh1, dh1_b, d_mix0 = _rmsnorm_bwd(h1, mix_norm[0:1], dhn0, dh2, scale=MACARON, name="l0_mix_norm_bwd")
    gw["ab_in", 0] = g_ab_in
    gconv_own = g_conv.reshape(3, 8, -1).transpose(1, 0, 2).reshape(8, -1)
    conv_piece = jnp.zeros((8, 16, d), F32).at[:, 0, :nconv].set(gconv_own).astype(BF16)

    def send_b0(dwd):
        gw["pre_d", 0] = dwd
        return grad_send("B0")

    def chip_part_b1(dwg, dwu):
        gw["pre_g", 0], gw["pre_u", 0] = dwg, dwu
        return chip_partials("B1")

    dh0, _, _, _, _, d_pre0, (parts_c, parts_b0, parts_b1) = _ffn_bwd(
        dh1, dh1_b, s_pre0, ffn_pre_norm[0:1], *ffn_w["pre", 0], "l0pre", 1.0,
        (chip_partials("C", [conv_piece]), send_b0, chip_part_b1))

    parts = {"A0": parts_a0, "A1": parts_a1, "B0": parts_b0, "B1": parts_b1, "C": parts_c}
    upd = {}
    for tag, w, m, v in big:
        view = (lambda a: jnp.swapaxes(a, 1, 2)) if tag in col_sharded else (lambda a: a)
        where = {layer: (key, grad_offs[key][tag, layer])
                 for key in grad_items for t2, layer in grad_items[key] if t2 == tag}
        res = None
        for layer in sorted(where):
            key, off = where[layer]
            res = _adamw_shard(parts[key], off, view(w), view(m), view(v), layer, res, name=f"adamw_{tag}{layer}")
        upd[tag] = [view(a) for a in res]
    res = _adamw_shard(parts["C"], grad_conv_row, *(conv_rows(a, False)[None] for a in (ab_conv_w, m_ab_conv_w, v_ab_conv_w)),
                       0, None, name="adamw_conv")
    upd["conv"] = [r[0, 0, :nconv].reshape(ab_conv_w.shape) for r in res]

    def small_pack(pre, mix, post, final, lbs, onorm):
        def slot(parts):
            out, r = jnp.zeros((SLOT, d), F32), 0
            for a in (parts if isinstance(parts, tuple) else (parts,)):
                out = out.at[r:r + a.shape[0], :a.shape[1]].set(a)
                r += a.shape[0]
            return out

        return jnp.concatenate([slot(pre), slot(mix), slot(post), slot(final.reshape(1, d)), slot(lbs), slot(onorm)], axis=0)

    d_on = d_onorm.reshape(-1, c_out_norm.shape[1]).sum(axis=0, keepdims=True)
    gsmall = small_pack((d_pre0, d_pre1), (d_mix0, d_mix1), (d_post0, d_post1), d_final, dlb, d_on)
    gsmall_all = _all_gather(gsmall, name="gather_small_grads")
    sres = _small_update(
        gsmall_all,
        small_pack(ffn_pre_norm, mix_norm, ffn_post_norm, final_norm, c_lower_bounds, c_out_norm),
        small_pack(m_ffn_pre_norm, m_mix_norm, m_ffn_post_norm, m_final_norm, m_c_lower_bounds, m_c_out_norm),
        small_pack(v_ffn_pre_norm, v_mix_norm, v_ffn_post_norm, v_final_norm, v_c_lower_bounds, v_c_out_norm),
        name="small_update")

    def small_out(r):
        return {"pre_norm": r[0:2], "mix_norm": r[SLOT:SLOT + 2], "post_norm": r[2 * SLOT:2 * SLOT + 2],
                "final": r[3 * SLOT], "lb": r[ROW_LB:ROW_LB + 2], "onorm": r[5 * SLOT:5 * SLOT + 1, :c_out_norm.shape[1]]}

    small = [small_out(r) for r in sres]
    outs = []
    for k in range(4):
        s = small[k]
        outs += [s["pre_norm"], upd["pre_g"][k], upd["pre_u"][k], upd["pre_d"][k], s["mix_norm"], s["post_norm"],
                 upd["post_g"][k], upd["post_u"][k], upd["post_d"][k], upd["ab_in"][k], upd["conv"][k],
                 upd["ab_out"][k], upd["c_in"][k], s["lb"], s["onorm"], upd["c_out"][k], s["final"]]
    loss = lax.psum(loss_vec[0, 0], ("x", "y", "c"))
    return (loss, dh0[None], *outs)
```

```python
import math

import jax
import jax.numpy as jnp
from jax import lax
from jax.experimental import pallas as pl
from jax.experimental.pallas import tpu as pltpu

F32 = jnp.float32
BF16 = jnp.bfloat16
MESH = pl.DeviceIdType.MESH

RMS_EPS = 1e-6
MACARON = 0.5
LANES = 128
CHUNK = 64
N_LEVELS = 6
HGRN_HEADS = 2
SB_KEYS = 256
ADAM_LR, ADAM_B1, ADAM_B2, ADAM_EPS, ADAM_WD, ADAM_STEP = 0.001, 0.9, 0.999, 1e-08, 0.01, 10
VMEM_LIMIT = 48 * 1024 * 1024


def _cp(**kw):
    return pltpu.CompilerParams(vmem_limit_bytes=VMEM_LIMIT, **kw)


def _sigmoid(x):
    return 0.5 * jnp.tanh(0.5 * x) + 0.5


def _bf(x):
    return x if x.dtype == BF16 else x.astype(BF16)


def _split3(x):
    hi = x.astype(BF16)
    r1 = x - hi.astype(F32)
    mid = r1.astype(BF16)
    lo = (r1 - mid.astype(F32)).astype(BF16)
    return hi, mid, lo


def _dot(a, b, ca=1, cb=0):
    return lax.dot_general(a, b, (((ca,), (cb,)), ((), ())), preferred_element_type=F32)


def _dot_exact_lhs(m, x):
    hi, mid, lo = _split3(x)
    return _dot(m, hi) + _dot(m, mid) + _dot(m, lo)


def _mm(terms, *, name, ta=False, tb=False, out_dtype=F32, residual=None, alpha=1.0, tm=512, tn=512):
    nt = len(terms)
    a0, b0 = terms[0]
    m = a0.shape[1] if ta else a0.shape[0]
    n = b0.shape[0] if tb else b0.shape[1]
    tm, tn = min(tm, m), min(tn, n)
    assert m % tm == 0 and n % tn == 0, (name, m, n, tm, tn)
    has_res = residual is not None

    def body(*refs):
        o_ref = refs[-1]
        acc = None
        for i in range(nt):
            a = _bf(refs[2 * i][...])
            b = _bf(refs[2 * i + 1][...])
            p = _dot(a, b, 0 if ta else 1, 1 if tb else 0)
            acc = p if acc is None else acc + p
        if alpha != 1.0:
            acc = acc * alpha
        if has_res:
            acc = acc + refs[2 * nt][...]
        o_ref[...] = acc.astype(out_dtype)

    in_specs, args = [], []
    for a, b in terms:
        k = a.shape[0] if ta else a.shape[1]
        assert (b.shape[1] if tb else b.shape[0]) == k, (name, a.shape, b.shape)
        in_specs.append(pl.BlockSpec((k, tm), lambda i, j: (0, i)) if ta else pl.BlockSpec((tm, k), lambda i, j: (i, 0)))
        in_specs.append(pl.BlockSpec((tn, k), lambda i, j: (j, 0)) if tb else pl.BlockSpec((k, tn), lambda i, j: (0, j)))
        args += [a, b]
    if has_res:
        in_specs.append(pl.BlockSpec((tm, tn), lambda i, j: (i, j)))
        args.append(residual)
    return pl.pallas_call(
        body, name=name, grid=(m // tm, n // tn), in_specs=in_specs,
        out_specs=pl.BlockSpec((tm, tn), lambda i, j: (i, j)),
        out_shape=jax.ShapeDtypeStruct((m, n), out_dtype), compiler_params=_cp())(*args)


def _norm_proj(x, gain, w_t, out_dtypes, *, name, tm):
    t, d = x.shape
    n = w_t.shape[0]
    tm = min(tm, t)
    npart = len(out_dtypes)
    width = n // npart

    def body(x_ref, g_ref, w_ref, xn_ref, *part_refs):
        xv = x_ref[...]
        rstd = lax.rsqrt(jnp.mean(xv * xv, axis=-1, keepdims=True) + RMS_EPS)
        xn = (xv * rstd * g_ref[...]).astype(BF16)
        xn_ref[...] = xn
        for p, ref in enumerate(part_refs):
            ref[...] = _dot(xn, w_ref[p * width:(p + 1) * width, :], 1, 1).astype(out_dtypes[p])

    row = pl.BlockSpec((tm, d), lambda i: (i, 0))
    return pl.pallas_call(
        body, name=name, grid=(t // tm,),
        in_specs=[row, pl.BlockSpec((1, d), lambda i: (0, 0)), pl.BlockSpec((n, d), lambda i: (0, 0))],
        out_specs=[row] + [pl.BlockSpec((tm, width), lambda i: (i, 0))] * npart,
        out_shape=[jax.ShapeDtypeStruct((t, d), BF16)] + [jax.ShapeDtypeStruct((t, width), dt) for dt in out_dtypes],
        compiler_params=_cp())(x, gain, w_t)


def _mm_shared_rhs(a_list, b, *, name, tm, out_dtype=BF16, send=None):
    k, n = b.shape
    assert all(a.shape[0] == k and a.shape[1] % tm == 0 and a.shape[1] == a_list[0].shape[1] for a in a_list)
    m = a_list[0].shape[1]
    na = len(a_list)
    nsteps = m // tm
    sends = list(send) if send is not None else []
    ns = len(sends)

    def body(*refs):
        first_out = na + 1 + ns
        if ns:
            start, finish = _direct_exchange_phases(refs[na + 1:first_out], refs[first_out + na], *refs[first_out + na + 1:])
            pl.when(pl.program_id(0) == 0)(start)
        bv = refs[na][...]
        for i in range(na):
            refs[first_out + i][...] = _dot(refs[i][...], bv, 0, 0).astype(out_dtype)
        if ns:
            pl.when(pl.program_id(0) == nsteps - 1)(finish)

    return pl.pallas_call(
        body, name=name, grid=(nsteps,),
        in_specs=[pl.BlockSpec((k, tm), lambda i: (0, i))] * na + [pl.BlockSpec((k, n), lambda i: (0, 0))] + [HBM_SPEC] * ns,
        out_specs=[pl.BlockSpec((tm, n), lambda i: (i, 0))] * na + ([HBM_SPEC] if ns else []),
        out_shape=[jax.ShapeDtypeStruct((m, n), out_dtype)] * na
        + ([jax.ShapeDtypeStruct((8, _direct_exchange_rows(sends), sends[0].shape[3]), sends[0].dtype)] if ns else []),
        scratch_shapes=_direct_exchange_scratch() if ns else [],
        compiler_params=_cp(dimension_semantics=("arbitrary",)))(*a_list, b, *sends)


def _rmsnorm_bwd(x, gain, dxn, dres, *, name, scale, tm=1024):
    t, d = x.shape
    tm = min(tm, t)

    def body(x_ref, g_ref, dxn_ref, dres_ref, dx_ref, dxb_ref, dg_ref):
        xv = x_ref[...]
        rstd = lax.rsqrt(jnp.mean(xv * xv, axis=-1, keepdims=True) + RMS_EPS)
        xhat = xv * rstd
        dxn_v = dxn_ref[...]
        dxhat = dxn_v * g_ref[...]
        dx = dres_ref[...] + rstd * (dxhat - xhat * jnp.mean(dxhat * xhat, axis=-1, keepdims=True))
        dx_ref[...] = dx
        dxb_ref[...] = (dx * scale).astype(BF16)

        @pl.when(pl.program_id(0) == 0)
        def _():
            dg_ref[...] = jnp.zeros_like(dg_ref)

        dg_ref[...] += jnp.sum(dxn_v * xhat, axis=0, keepdims=True)

    row = pl.BlockSpec((tm, d), lambda i: (i, 0))
    vec = pl.BlockSpec((1, d), lambda i: (0, 0))
    return pl.pallas_call(
        body, name=name, grid=(t // tm,), in_specs=[row, vec, row, row], out_specs=[row, row, vec],
        out_shape=[jax.ShapeDtypeStruct((t, d), F32), jax.ShapeDtypeStruct((t, d), BF16), jax.ShapeDtypeStruct((1, d), F32)],
        compiler_params=_cp())(x, gain, dxn, dres)


def _loss_head(h, gain, target, *, name, tm=1024):
    t, d = h.shape
    tm = min(tm, t)

    def body(h_ref, g_ref, t_ref, dh_ref, dhb_ref, dg_ref, loss_ref):
        hv = h_ref[...]
        rstd = lax.rsqrt(jnp.mean(hv * hv, axis=-1, keepdims=True) + RMS_EPS)
        xhat = hv * rstd
        err = xhat * g_ref[...] - t_ref[...]
        dy = err * (1.0 / d)
        dxhat = dy * g_ref[...]
        dh = rstd * (dxhat - xhat * jnp.mean(dxhat * xhat, axis=-1, keepdims=True))
        dh_ref[...] = dh
        dhb_ref[...] = (dh * MACARON).astype(BF16)

        @pl.when(pl.program_id(0) == 0)
        def _():
            dg_ref[...] = jnp.zeros_like(dg_ref)
            loss_ref[...] = jnp.zeros_like(loss_ref)

        dg_ref[...] += jnp.sum(dy * xhat, axis=0, keepdims=True)
        part = jnp.sum(jnp.sum(err * err, axis=-1, keepdims=True), axis=0, keepdims=True) * (0.5 / d)
        loss_ref[...] += jnp.broadcast_to(part, loss_ref.shape)

    row = pl.BlockSpec((tm, d), lambda i: (i, 0))
    vec = pl.BlockSpec((1, d), lambda i: (0, 0))
    return pl.pallas_call(
        body, name=name, grid=(t // tm,), in_specs=[row, vec, row],
        out_specs=[row, row, vec, pl.BlockSpec((1, LANES), lambda i: (0, 0))],
        out_shape=[jax.ShapeDtypeStruct((t, d), F32), jax.ShapeDtypeStruct((t, d), BF16), jax.ShapeDtypeStruct((1, d), F32),
                   jax.ShapeDtypeStruct((1, LANES), F32)],
        compiler_params=_cp())(h, gain, target)


def _norm_gate_up(x, gain, wg, wu, *, name, tm=256, tf=2816, pack=None, seg_rows=()):
    t, d = x.shape
    f = wg.shape[0]
    tm, tf = min(tm, t), min(tf, f)
    assert f % tf == 0
    ni, nj = t // tm, f // tf
    nseg = len(seg_rows)

    def body(x_ref, g_ref, wg_ref, wu_ref, *rest):
        if pack is not None:
            pack_ref, xn_ref, gg_ref, uu_ref, act_ref = rest[:5]
            start, forward, finish = _gather_phases(pack_ref, rest[5:5 + nseg], seg_rows, *rest[5 + nseg:])
            step = pl.program_id(0) * nj + pl.program_id(1)
            pl.when(step == 0)(start)
            pl.when(step == (3 * ni * nj) // 4)(forward)
        else:
            xn_ref, gg_ref, uu_ref, act_ref = rest

        @pl.when(pl.program_id(1) == 0)
        def _():
            xv = x_ref[...]
            rstd = lax.rsqrt(jnp.mean(xv * xv, axis=-1, keepdims=True) + RMS_EPS)
            xn_ref[...] = (xv * rstd * g_ref[...]).astype(BF16)

        xn = xn_ref[...]
        gv = _dot(xn, wg_ref[...], 1, 1)
        uv = _dot(xn, wu_ref[...], 1, 1)
        gg_ref[...] = gv.astype(BF16)
        uu_ref[...] = uv.astype(BF16)
        act_ref[...] = (gv * _sigmoid(gv) * uv).astype(BF16)
        if pack is not None:
            pl.when(step == ni * nj - 1)(finish)

    row = pl.BlockSpec((tm, d), lambda i, j: (i, 0))
    wsp = pl.BlockSpec((tf, d), lambda i, j: (j, 0))
    osp = pl.BlockSpec((tm, tf), lambda i, j: (i, j))
    fused = pack is not None
    return pl.pallas_call(
        body, name=name, grid=(ni, nj),
        in_specs=[row, pl.BlockSpec((1, d), lambda i, j: (0, 0)), wsp, wsp] + ([HBM_SPEC] if fused else []),
        out_specs=[row, osp, osp, osp] + [HBM_SPEC] * nseg,
        out_shape=[jax.ShapeDtypeStruct((t, d), BF16)] + [jax.ShapeDtypeStruct((t, f), BF16)] * 3
        + [jax.ShapeDtypeStruct((8, n, d), BF16) for n in seg_rows],
        scratch_shapes=_gather_scratch() if fused else [],
        compiler_params=_cp(dimension_semantics=("arbitrary", "arbitrary")))(x, gain, wg, wu, *([pack] if fused else []))


def _swiglu_bwd(dout, wd, gg, uu, chip_part=None, *, name, tm=512, tf=1408):
    t, d = dout.shape
    f = wd.shape[0]
    tm, tf = min(tm, t), min(tf, f)
    nj, ni = f // tf, t // tm
    fused = chip_part is not None

    def body(do_ref, wd_ref, g_ref, u_ref, *rest):
        if fused:
            part_ref, dg_ref, du_ref, parts_ref = rest[:4]
            start, finish = _chip_exchange_phases(part_ref, parts_ref, *rest[4:])
            step = pl.program_id(0) * ni + pl.program_id(1)
            pl.when(step == 0)(start)
        else:
            dg_ref, du_ref = rest
        dact = _dot(do_ref[...], wd_ref[...], 1, 1)
        gv = g_ref[...].astype(F32)
        uv = u_ref[...].astype(F32)
        sg = _sigmoid(gv)
        dg_ref[...] = (dact * uv * (sg * (1.0 + gv * (1.0 - sg)))).astype(BF16)
        du_ref[...] = (dact * (gv * sg)).astype(BF16)
        if fused:
            pl.when(step == nj * ni - 1)(finish)

    osp = pl.BlockSpec((tm, tf), lambda j, i: (i, j))
    return pl.pallas_call(
        body, name=name, grid=(nj, ni),
        in_specs=[pl.BlockSpec((tm, d), lambda j, i: (i, 0)), pl.BlockSpec((tf, d), lambda j, i: (j, 0)), osp, osp]
        + ([HBM_SPEC] if fused else []),
        out_specs=[osp, osp] + ([HBM_SPEC] if fused else []),
        out_shape=[jax.ShapeDtypeStruct((t, f), BF16)] * 2
        + ([jax.ShapeDtypeStruct(chip_part.shape, chip_part.dtype)] if fused else []),
        scratch_shapes=_chip_exchange_scratch() if fused else [],
        compiler_params=_cp(dimension_semantics=("arbitrary", "arbitrary")))(dout, wd, gg, uu, *([chip_part] if fused else []))


def _shift_down(x, n):
    rows = lax.broadcasted_iota(jnp.int32, x.shape, 0)
    return jnp.where(rows >= n, pltpu.roll(x, n, 0), 0.0)


def _shift_up(x, n):
    t = x.shape[0]
    rows = lax.broadcasted_iota(jnp.int32, x.shape, 0)
    return jnp.where(rows < t - n, pltpu.roll(x, t - n, 0), 0.0)


def _conv_fwd(pa, conv_w, *, name):
    t = pa.shape[0]
    nb = pa.shape[1] // 3 // LANES

    def body(b_ref, c_ref, x_ref, w_ref, y_ref):
        u = c_ref[...] * x_ref[...]
        w = w_ref[...]
        conv = w[2:3, :] * u + w[1:2, :] * _shift_down(u, 1) + w[0:1, :] * _shift_down(u, 2)
        y_ref[...] = (b_ref[...] * conv).astype(BF16)

    def col(off):
        return pl.BlockSpec((t, LANES), lambda j: (0, off + j))

    return pl.pallas_call(
        body, name=name, grid=(nb,),
        in_specs=[col(0), col(nb), col(2 * nb), pl.BlockSpec((3, LANES), lambda j: (0, j))],
        out_specs=pl.BlockSpec((t, LANES), lambda j: (0, j)),
        out_shape=jax.ShapeDtypeStruct((t, nb * LANES), BF16), compiler_params=_cp())(pa, pa, pa, conv_w)


def _conv_bwd(pa, dy, conv_w, *, name):
    t = pa.shape[0]
    nb = pa.shape[1] // 3 // LANES

    def body(b_ref, c_ref, x_ref, dy_ref, w_ref, db_ref, dc_ref, dx_ref, dw_ref):
        cv, xv = c_ref[...], x_ref[...]
        u = cv * xv
        u1, u2 = _shift_down(u, 1), _shift_down(u, 2)
        w = w_ref[...]
        conv = w[2:3, :] * u + w[1:2, :] * u1 + w[0:1, :] * u2
        dyv = dy_ref[...]
        db_ref[...] = (dyv * conv).astype(BF16)
        dconv = dyv * b_ref[...]
        du = w[2:3, :] * dconv + w[1:2, :] * _shift_up(dconv, 1) + w[0:1, :] * _shift_up(dconv, 2)
        dc_ref[...] = (du * xv).astype(BF16)
        dx_ref[...] = (du * cv).astype(BF16)
        dw_ref[0:1, :] = jnp.sum(dconv * u2, axis=0, keepdims=True)
        dw_ref[1:2, :] = jnp.sum(dconv * u1, axis=0, keepdims=True)
        dw_ref[2:3, :] = jnp.sum(dconv * u, axis=0, keepdims=True)

    def col(off):
        return pl.BlockSpec((t, LANES), lambda j: (0, off + j))

    osp = pl.BlockSpec((t, LANES), lambda j: (0, j))
    wsp = pl.BlockSpec((3, LANES), lambda j: (0, j))
    return pl.pallas_call(
        body, name=name, grid=(nb,), in_specs=[col(0), col(nb), col(2 * nb), col(0), wsp],
        out_specs=[osp, osp, osp, wsp],
        out_shape=[jax.ShapeDtypeStruct((t, nb * LANES), BF16)] * 3 + [jax.ShapeDtypeStruct((3, nb * LANES), F32)],
        compiler_params=_cp())(pa, pa, pa, dy, conv_w)


def _sb_consts():
    j = lax.broadcasted_iota(jnp.int32, (SB_KEYS, SB_KEYS), 0)
    s = lax.broadcasted_iota(jnp.int32, (SB_KEYS, SB_KEYS), 1)
    after = (j > s).astype(BF16)
    upto = (j <= s).astype(BF16)
    before = (j < s).astype(BF16)
    return after, jnp.stack([upto, before])


def _log_sigmoid(z):
    return jnp.minimum(z, 0.0) - jnp.log(1.0 + jnp.exp(-jnp.abs(z)))


def _attn_fwd(pb, late_pack, seg_rows, *, name, tq=256):
    t = pb.shape[0]
    npair = pb.shape[1] // 3 // LANES
    tq = min(tq, t)
    nq = t // tq
    cmat, _ = _sb_consts()
    scale = 1.0 / math.sqrt(LANES // 2)

    nseg = len(seg_rows)

    def body(q_ref, k_ref, v_ref, c_ref, late_ref, y_ref, lt_ref, *rest):
        i = pl.program_id(1)
        pair = pl.program_id(0)
        scratch = rest[nseg:nseg + 4]
        start, forward, finish = _gather_phases(late_ref, rest[:nseg], seg_rows, *rest[nseg + 4:])
        pl.when((pair == 0) & (i == 0))(start)
        pl.when((pair == npair - 1) & (i == nq // 2))(forward)
        lane = lax.broadcasted_iota(jnp.int32, (tq, LANES), 1)
        rowpos = i * tq + lax.broadcasted_iota(jnp.int32, (tq, SB_KEYS), 0)
        colid = lax.broadcasted_iota(jnp.int32, (tq, SB_KEYS), 1)
        q2 = q_ref[...] * jnp.asarray(scale, BF16)
        cm = c_ref[...]
        hi_lanes = lane >= LANES // 2
        qhs = [jnp.where(hi_lanes == (hh == 1), q2, jnp.zeros_like(q2)) for hh in range(2)]
        per_q = tq // SB_KEYS

        def blk(jb):
            return pl.ds(pl.multiple_of(jb * SB_KEYS, SB_KEYS), SB_KEYS)

        zbuf, wbuf, accbuf, runbuf = scratch

        def scores(jb):
            kb = k_ref[blk(jb), :]
            for hh in range(2):
                zbuf[hh] = _dot(qhs[hh], kb, 1, 1)

        def values(jb):
            vb = v_ref[blk(jb), :]
            for hh in range(2):
                accbuf[hh] += _dot(wbuf[hh], vb)

        def trip(jb, masked, first=False):
            mask = (jb * SB_KEYS + colid) < rowpos if masked else None
            if not first:
                values(jb + 1)
            pre, css = [], []
            for hh in range(2):
                z = zbuf[hh]
                lb = _log_sigmoid(z)
                lk = lb - z
                if masked:
                    lk = jnp.where(mask, lk, 0.0)
                lk_hi, lk_lo = _split2(lk)
                css.append(_dot(lk_hi, cm) + _dot(lk_lo, cm))
                run = runbuf[hh]
                pre.append(lb + run)
                runbuf[hh] = run + jnp.sum(lk, axis=1, keepdims=True)
            scores(jnp.maximum(jb - 1, 0))
            for hh in range(2):
                w = jnp.exp(pre[hh] + css[hh])
                if masked:
                    w = jnp.where(mask, w, 0.0)
                wbuf[hh] = w.astype(BF16)

        nfull = i * per_q
        accbuf[...] = jnp.zeros_like(accbuf)
        runbuf[...] = jnp.zeros_like(runbuf)
        scores(nfull + per_q - 1)
        for dblk in reversed(range(per_q)):
            trip(nfull + dblk, True, first=dblk == per_q - 1)

        def full_block(n, carry):
            trip(nfull - 1 - n, False)
            return carry

        lax.fori_loop(0, nfull, full_block, 0)
        values(0)
        y_ref[...] = jnp.where(hi_lanes, accbuf[1], accbuf[0]).astype(BF16)
        lt_ref[...] = jnp.where(hi_lanes, runbuf[1], runbuf[0])
        pl.when((pair == npair - 1) & (i == nq - 1))(finish)

    return pl.pallas_call(
        body, name=name, grid=(npair, nq),
        in_specs=[pl.BlockSpec((tq, LANES), lambda p, i: (i, p)),
                  pl.BlockSpec((t, LANES), lambda p, i: (0, npair + p)),
                  pl.BlockSpec((t, LANES), lambda p, i: (0, 2 * npair + p)),
                  pl.BlockSpec((SB_KEYS, SB_KEYS), lambda p, i: (0, 0)),
                  HBM_SPEC],
        out_specs=[pl.BlockSpec((tq, LANES), lambda p, i: (i, p))] * 2 + [HBM_SPEC] * nseg,
        out_shape=[jax.ShapeDtypeStruct((t, npair * LANES), BF16), jax.ShapeDtypeStruct((t, npair * LANES), F32),
                   ] + [jax.ShapeDtypeStruct((8, n, late_pack.shape[1]), late_pack.dtype) for n in seg_rows],
        scratch_shapes=[pltpu.VMEM((2, tq, SB_KEYS), F32), pltpu.VMEM((2, tq, SB_KEYS), BF16),
                        pltpu.VMEM((2, tq, LANES), F32), pltpu.VMEM((2, tq, 1), F32)] + _gather_scratch(),
        compiler_params=_cp(dimension_semantics=("arbitrary", "arbitrary")))(pb, pb, pb, cmat, late_pack)


def _attn_bwd(pb, dy, ltot, send, *, name, tq=256):
    t = pb.shape[0]
    npair = pb.shape[1] // 3 // LANES
    tq = min(tq, t)
    nq = t // tq
    _, cmats = _sb_consts()
    scale = 1.0 / math.sqrt(LANES // 2)
    sends = list(send)
    ns = len(sends)

    def body(q_ref, k_ref, v_ref, dy_ref, lt_ref, c_ref, *rest):
        i = pl.program_id(1)
        pair = pl.program_id(0)
        send_refs = rest[:ns]
        dq_ref, dk_ref, dv_ref, parts_ref, dk_acc, dv_acc = rest[ns:ns + 6]
        scratch = rest[ns + 6:ns + 12]
        start, finish = _direct_exchange_phases(send_refs, parts_ref, *rest[ns + 12:])
        pl.when((pair == 0) & (i == 0))(start)

        @pl.when(i == 0)
        def _():
            dk_acc[...] = jnp.zeros_like(dk_acc)
            dv_acc[...] = jnp.zeros_like(dv_acc)

        lane = lax.broadcasted_iota(jnp.int32, (tq, LANES), 1)
        rowpos = i * tq + lax.broadcasted_iota(jnp.int32, (tq, SB_KEYS), 0)
        colid = lax.broadcasted_iota(jnp.int32, (tq, SB_KEYS), 1)
        q2 = q_ref[...] * jnp.asarray(scale, BF16)
        do2 = dy_ref[...].astype(BF16)
        ltv = lt_ref[...]
        c_upto, c_before = c_ref[0], c_ref[1]
        hi_lanes = lane >= LANES // 2
        sels = [hi_lanes == (hh == 1) for hh in range(2)]
        qhs = [jnp.where(s, q2, jnp.zeros_like(q2)) for s in sels]
        dohs = [jnp.where(s, do2, jnp.zeros_like(do2)) for s in sels]
        lts = [ltv[:, 0:1], ltv[:, LANES // 2:LANES // 2 + 1]]
        per_q = tq // SB_KEYS

        def blk(jb):
            return pl.ds(pl.multiple_of(jb * SB_KEYS, SB_KEYS), SB_KEYS)

        zbuf, dabuf, dzbuf, abuf, dqbuf, sumbuf = scratch

        def scores(jb):
            kb, vb = k_ref[blk(jb), :], v_ref[blk(jb), :]
            for hh in range(2):
                zbuf[hh] = _dot(qhs[hh], kb, 1, 1)
                dabuf[hh] = _dot(dohs[hh], vb, 1, 1)

        def products(jb):
            kb = k_ref[blk(jb), :]
            dk_acc[blk(jb), :] += _dot(dzbuf[0], qhs[0], 0, 0) + _dot(dzbuf[1], qhs[1], 0, 0)
            dv_acc[blk(jb), :] += _dot(abuf[0], dohs[0], 0, 0) + _dot(abuf[1], dohs[1], 0, 0)
            for hh in range(2):
                dqbuf[hh] += _dot(dzbuf[hh], kb)

        def trip(jb, masked):
            mask = (jb * SB_KEYS + colid) < rowpos if masked else None
            products(jnp.maximum(jb - 1, 0))
            lbs, css, es, ces = [], [], [], []
            for hh in range(2):
                z = zbuf[hh]
                lb = _log_sigmoid(z)
                lk = lb - z
                if masked:
                    lk = jnp.where(mask, lk, 0.0)
                lk_hi, lk_lo = _split2(lk)
                css.append(_dot(lk_hi, c_upto) + _dot(lk_lo, c_upto))
                csum = sumbuf[2 * hh]
                lbs.append((lb, lb + (lts[hh] - csum)))
                sumbuf[2 * hh] = csum + jnp.sum(lk, axis=1, keepdims=True)
            for hh in range(2):
                a = jnp.exp(lbs[hh][1] - css[hh])
                if masked:
                    a = jnp.where(mask, a, 0.0)
                e = a * dabuf[hh]
                e_hi, e_lo = _split2(e)
                ces.append(_dot(e_hi, c_before) + _dot(e_lo, c_before))
                abuf[hh] = a.astype(BF16)
                es.append(e)
            scores(jnp.minimum(jb + 1, last))
            for hh in range(2):
                prun = sumbuf[2 * hh + 1]
                beta = jnp.exp(lbs[hh][0])
                dz = es[hh] * (1.0 - beta) - (prun + ces[hh]) * beta
                if masked:
                    dz = jnp.where(mask, dz, 0.0)
                dzbuf[hh] = dz.astype(BF16)
                sumbuf[2 * hh + 1] = prun + jnp.sum(es[hh], axis=1, keepdims=True)

        nfull = i * per_q
        last = nfull + per_q - 1
        for buf in (dzbuf, abuf, dqbuf, sumbuf):
            buf[...] = jnp.zeros_like(buf)
        scores(0)

        def full_block(jb, carry):
            trip(jb, False)
            return carry

        lax.fori_loop(0, nfull, full_block, 0)
        for dblk in range(per_q):
            trip(nfull + dblk, True)
        products(last)
        dq_ref[...] = (jnp.where(hi_lanes, dqbuf[1], dqbuf[0]) * scale).astype(BF16)

        @pl.when(i == nq - 1)
        def _():
            dk_ref[...] = dk_acc[...].astype(BF16)
            dv_ref[...] = dv_acc[...].astype(BF16)

        pl.when((pair == npair - 1) & (i == nq - 1))(finish)

    blk = pl.BlockSpec((tq, LANES), lambda p, i: (i, p))
    full = pl.BlockSpec((t, LANES), lambda p, i: (0, p))
    return pl.pallas_call(
        body, name=name, grid=(npair, nq),
        in_specs=[blk,
                  pl.BlockSpec((t, LANES), lambda p, i: (0, npair + p)),
                  pl.BlockSpec((t, LANES), lambda p, i: (0, 2 * npair + p)),
                  pl.BlockSpec((tq, LANES), lambda p, i: (i, npair + p)),
                  blk,
                  pl.BlockSpec((2, SB_KEYS, SB_KEYS), lambda p, i: (0, 0, 0))] + [HBM_SPEC] * ns,
        out_specs=[blk, full, full, HBM_SPEC],
        out_shape=[jax.ShapeDtypeStruct((t, npair * LANES), BF16)] * 3
        + [jax.ShapeDtypeStruct((8, _direct_exchange_rows(sends), sends[0].shape[3]), sends[0].dtype)],
        scratch_shapes=[pltpu.VMEM((t, LANES), F32), pltpu.VMEM((t, LANES), F32),
                        pltpu.VMEM((2, tq, SB_KEYS), F32), pltpu.VMEM((2, tq, SB_KEYS), F32),
                        pltpu.VMEM((2, tq, SB_KEYS), BF16), pltpu.VMEM((2, tq, SB_KEYS), BF16),
                        pltpu.VMEM((2, tq, LANES), F32), pltpu.VMEM((4, tq, 1), F32)] + _direct_exchange_scratch(),
        compiler_params=_cp(dimension_semantics=("arbitrary", "arbitrary")))(pb, pb, pb, dy, ltot, cmats, *sends)


def _hgrn_consts():
    t = lax.broadcasted_iota(jnp.int32, (CHUNK, CHUNK), 0)
    s = lax.broadcasted_iota(jnp.int32, (CHUNK, CHUNK), 1)
    masks = []
    for lvl in range(N_LEVELS):
        half = CHUNK >> (lvl + 1)
        same = (t // (2 * half)) == (s // (2 * half))
        masks.append((same & (t % (2 * half) >= half) & (s % (2 * half) < half)).astype(F32))
    masks.append((t == s).astype(F32))
    prefix = (s <= t).astype(BF16)
    suffix = (s >= t).astype(BF16)
    return prefix, jnp.stack(masks), suffix


def _hgrn_gates(qr, fr, lbv):
    sg = 1.0 / (1.0 + jnp.exp(-fr))
    fval = lbv + (1.0 - lbv) * sg
    kk = (1.0 - lbv) * (1.0 / (1.0 + jnp.exp(fr)))
    sq = _sigmoid(qr)
    return sg, fval, jnp.log(fval), kk, sq, qr * sq


def _lower_bound(c_ref):
    c = c_ref[...]
    mx = jnp.max(c, axis=0, keepdims=True)
    ex = jnp.exp(c - mx)
    return ex[1:2, :] / jnp.sum(ex, axis=0, keepdims=True)


def _level_ref(b, lvl):
    half = CHUNK >> (lvl + 1)
    seg = 2 * half
    if seg >= 8:
        b3 = b.reshape(CHUNK // seg, seg, LANES)
        return jnp.broadcast_to(b3[:, half - 1:half, :], b3.shape).reshape(CHUNK, LANES)
    pos = lax.broadcasted_iota(jnp.int32, b.shape, 0) % seg
    out = b
    for p in range(seg):
        if p != half - 1:
            out = jnp.where(pos == p, pltpu.roll(b, (p - (half - 1)) % CHUNK, 0), out)
    return out


def _hgrn_levels(b, qs, kk):
    out = []
    for lvl in range(N_LEVELS):
        fac = jnp.exp(-jnp.abs(b - _level_ref(b, lvl)))
        out.append((qs * fac, kk * fac, fac, fac))
    out.append((qs, kk, None, None))
    return out


def _split2(x):
    hi = x.astype(BF16)
    return hi, (x - hi.astype(F32)).astype(BF16)


def _hgrn_fwd(pc, c_lb, out_norm, *, name, tc=1024):
    t = pc.shape[0]
    nh = pc.shape[1] // 4 // LANES
    tc = min(tc, t)
    nch = tc // CHUNK
    cum_all, masks, _ = _hgrn_consts()

    def body(q_ref, f_ref, i_ref, g_ref, lb_ref, on_ref, cum_ref, m_ref, y_ref, o_ref, st_ref, state):
        @pl.when(pl.program_id(1) == 0)
        def _():
            state[...] = jnp.zeros_like(state)

        lbv = _lower_bound(lb_ref)
        onv = on_ref[...]

        def chunk(c, carry):
            rows = pl.ds(pl.multiple_of(c * CHUNK, CHUNK), CHUNK)
            for hh in range(HGRN_HEADS):
                lanes = slice(hh * LANES, (hh + 1) * LANES)
                _, _, g, kk, _, qs = _hgrn_gates(q_ref[rows, lanes], f_ref[rows, lanes], lbv[:, lanes])
                vb = i_ref[rows, lanes].astype(BF16)
                b = _dot_exact_lhs(cum_ref[...], g)
                scores = jnp.zeros((CHUNK, CHUNK), F32)
                for lvl, (ql, kl, _, _) in enumerate(_hgrn_levels(b, qs, kk)):
                    scores = scores + _dot(ql.astype(BF16), kl.astype(BF16), 1, 1) * m_ref[lvl]
                st = state[hh]
                st_ref[hh, c] = st
                o = _dot(scores.astype(BF16), vb) + _dot((qs * jnp.exp(b)).astype(BF16), st.astype(BF16), 1, 1)
                blast = b[CHUNK - 1:CHUNK, :]
                kdec = (kk * jnp.exp(blast - b)).astype(BF16)
                state[hh] = st * jnp.exp(blast) + _dot(vb, kdec, 0, 0)
                o_ref[rows, lanes] = o
                rstd = lax.rsqrt(jnp.mean(o * o, axis=-1, keepdims=True) + RMS_EPS)
                gate = g_ref[rows, lanes]
                y_ref[rows, lanes] = (o * rstd * onv * (gate * _sigmoid(gate))).astype(BF16)
            return carry

        lax.fori_loop(0, nch, chunk, 0, unroll=2)

    hw = HGRN_HEADS * LANES

    def col(off):
        return pl.BlockSpec((tc, hw), lambda h, i: (i, off // HGRN_HEADS + h))

    osp = pl.BlockSpec((tc, hw), lambda h, i: (i, h))
    return pl.pallas_call(
        body, name=name, grid=(nh // HGRN_HEADS, t // tc),
        in_specs=[col(0), col(nh), col(2 * nh), col(3 * nh),
                  pl.BlockSpec((2, hw), lambda h, i: (0, h)),
                  pl.BlockSpec((1, LANES), lambda h, i: (0, 0)),
                  pl.BlockSpec(cum_all.shape, lambda h, i: (0, 0)),
                  pl.BlockSpec(masks.shape, lambda h, i: (0, 0, 0))],
        out_specs=[osp, osp, pl.BlockSpec((HGRN_HEADS, nch, LANES, LANES), lambda h, i: (h, i, 0, 0))],
        out_shape=[jax.ShapeDtypeStruct((t, nh * LANES), BF16), jax.ShapeDtypeStruct((t, nh * LANES), F32),
                   jax.ShapeDtypeStruct((nh, t // CHUNK, LANES, LANES), F32)],
        scratch_shapes=[pltpu.VMEM((HGRN_HEADS, LANES, LANES), F32)],
        compiler_params=_cp())(pc, pc, pc, pc, c_lb, out_norm, cum_all, masks)


def _hgrn_bwd(pc, o_saved, states, dy, c_lb, out_norm, send, *, name, tc=1024):
    t = pc.shape[0]
    nh = pc.shape[1] // 4 // LANES
    tc = min(tc, t)
    nch = tc // CHUNK
    nt = t // tc
    cum_all, masks, suffix = _hgrn_consts()
    ngroup = nh // HGRN_HEADS
    sends = list(send)
    ns = len(sends)

    def body(q_ref, f_ref, i_ref, g_ref, o_ref, st_ref, dy_ref, lb_ref, on_ref, cum_ref, m_ref, suf_ref, *rest):
        send_refs = rest[:ns]
        dq_ref, df_ref, di_ref, dg_ref, dlb_ref, don_ref, parts_ref, dstate = rest[ns:ns + 8]
        start, finish = _direct_exchange_phases(send_refs, parts_ref, *rest[ns + 8:])
        pl.when((pl.program_id(0) == 0) & (pl.program_id(1) == 0))(start)

        @pl.when(pl.program_id(1) == 0)
        def _():
            dstate[...] = jnp.zeros_like(dstate)
            dlb_ref[...] = jnp.zeros_like(dlb_ref)
            don_ref[...] = jnp.zeros_like(don_ref)

        lbv = _lower_bound(lb_ref)
        onv = on_ref[...]

        def head(hh, c, rows):
            lanes = slice(hh * LANES, (hh + 1) * LANES)
            qr = q_ref[rows, lanes]
            sg, fval, g, kk, sq, qs = _hgrn_gates(qr, f_ref[rows, lanes], lbv[:, lanes])
            vb = i_ref[rows, lanes].astype(BF16)
            o = o_ref[rows, lanes]
            gate = g_ref[rows, lanes]
            sgt = _sigmoid(gate)
            rstd = lax.rsqrt(jnp.mean(o * o, axis=-1, keepdims=True) + RMS_EPS)
            ohat = o * rstd
            dyv = dy_ref[rows, lanes]
            don = dyv * (gate * sgt)
            dg_ref[rows, lanes] = (dyv * ohat * onv * (sgt * (1.0 + gate * (1.0 - sgt)))).astype(BF16)
            don_ref[:, lanes] += jnp.sum(don * ohat, axis=0, keepdims=True)
            dxhat = don * onv
            dob = (rstd * (dxhat - ohat * jnp.mean(dxhat * ohat, axis=-1, keepdims=True))).astype(BF16)
            b = _dot_exact_lhs(cum_ref[...], g)
            blast = b[CHUNK - 1:CHUNK, :]
            eb = jnp.exp(b)
            edec = jnp.exp(blast - b)
            st32 = st_ref[hh, c]
            st = st32.astype(BF16)
            dst = dstate[hh]
            dstb = dst.astype(BF16)
            da = _dot(dob, vb, 1, 1)
            levels = _hgrn_levels(b, qs, kk)
            scores = jnp.zeros((CHUNK, CHUNK), F32)
            dq = eb * _dot(dob, st)
            dk_inter = edec * _dot(vb, dstb)
            dk = dk_inter
            for lvl, (ql, kl, eq, ek) in enumerate(levels):
                mk = m_ref[lvl]
                (qh, qlo), (kh, klo) = _split2(ql), _split2(kl)
                scores = scores + _dot(qh, kh, 1, 1) * mk
                dal = (da * mk).astype(BF16)
                dql = _dot(dal, kh) + _dot(dal, klo)
                dkl = _dot(dal, qh, 0, 0) + _dot(dal, qlo, 0, 0)
                dq = dq + (dql if eq is None else dql * eq)
                dk = dk + (dkl if ek is None else dkl * ek)
            kdec = (kk * edec).astype(BF16)
            dv = _dot(scores.astype(BF16), dob, 0, 0) + _dot(kdec, dstb, 1, 1)
            dstate[hh] = dst * jnp.exp(blast) + _dot(dob, (qs * eb).astype(BF16), 0, 0)
            db = qs * dq - kk * dk
            last = jnp.sum(kk * dk_inter, axis=0, keepdims=True) + jnp.exp(blast) * jnp.sum(dst * st32, axis=0, keepdims=True)
            dgl = _dot_exact_lhs(suf_ref[...], db) + last
            dfv = dgl / fval - dk
            df_ref[rows, lanes] = (dfv * (1.0 - lbv[:, lanes]) * sg * (1.0 - sg)).astype(BF16)
            dlb_ref[:, lanes] += jnp.sum(dfv * (1.0 - sg), axis=0, keepdims=True)
            dq_ref[rows, lanes] = (dq * (sq * (1.0 + qr * (1.0 - sq)))).astype(BF16)
            di_ref[rows, lanes] = dv.astype(BF16)

        def chunk(n, carry):
            c = nch - 1 - n
            rows = pl.ds(pl.multiple_of(c * CHUNK, CHUNK), CHUNK)
            for hh in range(HGRN_HEADS):
                head(hh, c, rows)
            return carry

        lax.fori_loop(0, nch, chunk, 0, unroll=2)
        pl.when((pl.program_id(0) == ngroup - 1) & (pl.program_id(1) == nt - 1))(finish)

    hw = HGRN_HEADS * LANES

    def col(off):
        return pl.BlockSpec((tc, hw), lambda h, i: (nt - 1 - i, off // HGRN_HEADS + h))

    osp = pl.BlockSpec((tc, hw), lambda h, i: (nt - 1 - i, h))
    vec = pl.BlockSpec((1, hw), lambda h, i: (0, h))
    return pl.pallas_call(
        body, name=name, grid=(nh // HGRN_HEADS, nt),
        in_specs=[col(0), col(nh), col(2 * nh), col(3 * nh), osp,
                  pl.BlockSpec((HGRN_HEADS, nch, LANES, LANES), lambda h, i: (h, nt - 1 - i, 0, 0)),
                  osp,
                  pl.BlockSpec((2, hw), lambda h, i: (0, h)),
                  pl.BlockSpec((1, LANES), lambda h, i: (0, 0)),
                  pl.BlockSpec(cum_all.shape, lambda h, i: (0, 0)),
                  pl.BlockSpec(masks.shape, lambda h, i: (0, 0, 0)),
                  pl.BlockSpec(suffix.shape, lambda h, i: (0, 0))] + [HBM_SPEC] * ns,
        out_specs=[osp, osp, osp, osp, vec, vec, HBM_SPEC],
        out_shape=[jax.ShapeDtypeStruct((t, nh * LANES), BF16)] * 4 + [jax.ShapeDtypeStruct((1, nh * LANES), F32)] * 2
        + [jax.ShapeDtypeStruct((8, _direct_exchange_rows(sends), sends[0].shape[3]), sends[0].dtype)],
        scratch_shapes=[pltpu.VMEM((HGRN_HEADS, LANES, LANES), F32)] + _direct_exchange_scratch(),
        compiler_params=_cp(dimension_semantics=("arbitrary", "arbitrary")))(
            pc, pc, pc, pc, o_saved, states, dy, c_lb, out_norm, cum_all, masks, suffix, *sends)


HBM_SPEC = pl.BlockSpec(memory_space=pltpu.HBM)


def _gather_scratch():
    return [pltpu.SemaphoreType.DMA((7,)), pltpu.SemaphoreType.DMA((7,)), pltpu.SemaphoreType.DMA]


def _gather_phases(x_ref, out_refs, seg_rows, send_sems, recv_sems, local_sem):
    x, y, c = lax.axis_index("x"), lax.axis_index("y"), lax.axis_index("c")
    me, sibling = (x, y, c), (x, y, 1 - c)
    chips = [(1 - x, y), (x, 1 - y), (1 - x, 1 - y)]
    offs = [sum(seg_rows[:s]) for s in range(len(seg_rows))]
    assert sum(seg_rows) == x_ref.shape[0]

    def index(px, py, pc):
        return 4 * px + 2 * py + pc

    def copies(k, block, to, own):
        return [pltpu.make_async_remote_copy(
            src_ref=x_ref.at[pl.ds(offs[s], n)] if own else out_refs[s].at[index(*block)],
            dst_ref=out_refs[s].at[index(*block)],
            send_sem=send_sems.at[k], recv_sem=recv_sems.at[k], device_id=to, device_id_type=MESH)
            for s, n in enumerate(seg_rows)]

    def all_bytes(k):
        return pltpu.make_async_remote_copy(src_ref=x_ref, dst_ref=x_ref, send_sem=send_sems.at[k],
                                            recv_sem=recv_sems.at[k], device_id=me, device_id_type=MESH)

    mine = [pltpu.make_async_copy(x_ref.at[pl.ds(offs[s], n)], out_refs[s].at[index(*me)], local_sem)
            for s, n in enumerate(seg_rows)]
    first = copies(0, me, sibling, True)
    for j, chip in enumerate(chips):
        first += copies(1 + j, me, (*chip, c), True)

    def start():
        for cp in mine + first:
            cp.start()

    def forward():
        for j, chip in enumerate(chips):
            all_bytes(1 + j).wait_recv()
            for cp in copies(4 + j, (*chip, c), sibling, False):
                cp.start()

    def finish():
        all_bytes(0).wait_recv()
        for j in range(3):
            all_bytes(4 + j).wait_recv()
        for k in range(7):
            all_bytes(k).wait_send()
        pltpu.make_async_copy(x_ref, x_ref, local_sem).wait()

    return start, forward, finish


def _all_gather(xs, seg_rows=None, *, name):
    segs = [xs.shape[0]] if seg_rows is None else list(seg_rows)

    def body(x_ref, *rest):
        start, forward, finish = _gather_phases(x_ref, rest[:len(segs)], segs, *rest[len(segs):])
        start()
        forward()
        finish()

    outs = pl.pallas_call(
        body, name=name, in_specs=[HBM_SPEC], out_specs=[HBM_SPEC] * len(segs),
        out_shape=[jax.ShapeDtypeStruct((8, n, xs.shape[1]), xs.dtype) for n in segs],
        scratch_shapes=_gather_scratch())(xs)
    return outs[0] if seg_rows is None else outs


def _sibling_exchange(s, *, name):
    def body(s_ref, rb_ref, send_sem, recv_sem):
        x, y, c = lax.axis_index("x"), lax.axis_index("y"), lax.axis_index("c")
        cp = pltpu.make_async_remote_copy(
            src_ref=s_ref.at[:, 1 - c], dst_ref=rb_ref, send_sem=send_sem, recv_sem=recv_sem,
            device_id=(x, y, 1 - c), device_id_type=MESH)
        cp.start()
        cp.wait()

    return pl.pallas_call(
        body, name=name, in_specs=[HBM_SPEC], out_specs=HBM_SPEC,
        out_shape=jax.ShapeDtypeStruct(s.shape[:1] + s.shape[2:], s.dtype),
        scratch_shapes=[pltpu.SemaphoreType.DMA, pltpu.SemaphoreType.DMA])(s)


def _row_tile(n, cap=1024):
    return max(b for b in range(16, cap + 1, 16) if n % b == 0)


def _pair_add(s, rb, core, *, name):
    nchip, _, r, c = s.shape
    tb = _row_tile(r)

    def body(core_ref, a_ref, b_ref, o_ref):
        o_ref[...] = (a_ref[...].astype(F32) + b_ref[...].astype(F32)).astype(BF16)

    blk = pl.BlockSpec((None, tb, c), lambda ch, i, cr: (ch, i, 0))
    return pl.pallas_call(
        body, name=name,
        grid_spec=pltpu.PrefetchScalarGridSpec(
            num_scalar_prefetch=1, grid=(nchip, r // tb),
            in_specs=[pl.BlockSpec((None, None, tb, c), lambda ch, i, cr: (ch, cr[0], i, 0)), blk],
            out_specs=blk),
        out_shape=jax.ShapeDtypeStruct((nchip, r, c), BF16), compiler_params=_cp())(core, s, rb)


def _chip_exchange_scratch():
    return [pltpu.SemaphoreType.DMA((3,)), pltpu.SemaphoreType.DMA((3,)), pltpu.SemaphoreType.DMA]


def _chip_exchange_phases(p_ref, out_ref, send_sems, recv_sems, local_sem):
    x, y, c = lax.axis_index("x"), lax.axis_index("y"), lax.axis_index("c")
    mine = 2 * x + y
    own = pltpu.make_async_copy(p_ref.at[mine], out_ref.at[mine], local_sem)
    copies = [pltpu.make_async_remote_copy(
        src_ref=p_ref.at[2 * tx + ty], dst_ref=out_ref.at[mine],
        send_sem=send_sems.at[k], recv_sem=recv_sems.at[k], device_id=(tx, ty, c), device_id_type=MESH)
        for k, (tx, ty) in enumerate([(1 - x, y), (x, 1 - y), (1 - x, 1 - y)])]

    def start():
        own.start()
        for cp in copies:
            cp.start()

    def finish():
        for cp in copies:
            cp.wait()
        own.wait()

    return start, finish


def _direct_exchange_scratch():
    return [pltpu.SemaphoreType.DMA((7,)), pltpu.SemaphoreType.DMA((7,)), pltpu.SemaphoreType.DMA]


def _direct_exchange_rows(sends):
    return sum(s.shape[2] for s in sends)


def _direct_exchange_phases(s_refs, out_ref, send_sems, recv_sems, local_sem):
    x, y, c = lax.axis_index("x"), lax.axis_index("y"), lax.axis_index("c")
    me = 4 * x + 2 * y + c
    offs, off = [], 0
    for s in s_refs:
        offs.append(off)
        off += s.shape[2]

    def slot(p):
        return out_ref.at[me, pl.ds(offs[p], s_refs[p].shape[2])]

    own = [pltpu.make_async_copy(s.at[2 * x + y, c], slot(p), local_sem) for p, s in enumerate(s_refs)]
    flips = [(fx, fy, fc) for fx in (0, 1) for fy in (0, 1) for fc in (0, 1) if (fx, fy, fc) != (0, 0, 0)]
    copies = []
    for k, (fx, fy, fc) in enumerate(flips):
        tx, ty, tc = (1 - x if fx else x), (1 - y if fy else y), (1 - c if fc else c)
        copies += [pltpu.make_async_remote_copy(
            src_ref=s.at[2 * tx + ty, tc], dst_ref=slot(p),
            send_sem=send_sems.at[k], recv_sem=recv_sems.at[k], device_id=(tx, ty, tc), device_id_type=MESH)
            for p, s in enumerate(s_refs)]

    def start():
        for cp in own + copies:
            cp.start()

    def finish():
        whole = out_ref.at[me]
        for k in range(len(flips)):
            pltpu.make_async_remote_copy(src_ref=whole, dst_ref=whole, send_sem=send_sems.at[k],
                                         recv_sem=recv_sems.at[k], device_id=(x, y, c), device_id_type=MESH).wait()
        pltpu.make_async_copy(whole, whole, local_sem).wait()

    return start, finish


def _adamw_math(w, g, m, v):
    m2 = ADAM_B1 * m + (1.0 - ADAM_B1) * g
    v2 = ADAM_B2 * v + (1.0 - ADAM_B2) * (g * g)
    m_hat = m2 / (1.0 - ADAM_B1 ** ADAM_STEP)
    v_hat = v2 / (1.0 - ADAM_B2 ** ADAM_STEP)
    return -ADAM_LR * (m_hat / (jnp.sqrt(v_hat) + ADAM_EPS) + ADAM_WD * w), m2, v2


def _adamw_shard(parts, g_off, w, m, v, layer, prev, *, name):
    _, r, c = w.shape
    npart = parts.shape[0]
    tb = next(b for b in range(min(r, 512), 0, -16) if r % b == 0 and g_off % b == 0)

    def body(*refs):
        w_ref, m_ref, v_ref = refs[npart:npart + 3]
        g_out, d_out, m_out, v_out = refs[-4:]
        g = refs[0][...].astype(F32)
        for p_ref in refs[1:npart]:
            g = g + p_ref[...].astype(F32)
        d, m2, v2 = _adamw_math(w_ref[...], g, m_ref[...], v_ref[...])
        g_out[...] = g
        d_out[...] = d
        m_out[...] = m2
        v_out[...] = v2

    def part(ch):
        return pl.BlockSpec((None, tb, c), lambda i: (ch, g_off // tb + i, 0))

    blk = pl.BlockSpec((None, tb, c), lambda i: (layer, i, 0))
    prev = list(prev) if prev is not None else []
    return pl.pallas_call(
        body, name=name, grid=(r // tb,),
        in_specs=[part(ch) for ch in range(npart)] + [blk, blk, blk] + [pl.BlockSpec(memory_space=pl.ANY)] * len(prev),
        out_specs=[blk] * 4, out_shape=[jax.ShapeDtypeStruct(w.shape, F32)] * 4,
        input_output_aliases={npart + 3 + k: k for k in range(len(prev))},
        compiler_params=_cp())(*([parts] * npart), w, m, v, *prev)


SLOT = 8
SMALL_ROWS = 6 * SLOT
ROW_LB = 4 * SLOT


def _small_update(gath, w, m, v, *, name):
    def body(g_ref, w_ref, m_ref, v_ref, g_out, d_out, m_out, v_out):
        tot = g_ref[0]
        for k in range(1, 8):
            tot = tot + g_ref[k]
        wv = w_ref[...]
        c0, c1 = wv[ROW_LB:ROW_LB + 1, :], wv[ROW_LB + 1:ROW_LB + 2, :]
        mx = jnp.maximum(c0, c1)
        e0, e1 = jnp.exp(c0 - mx), jnp.exp(c1 - mx)
        lb = e1 / (e0 + e1)
        gl = tot[ROW_LB:ROW_LB + 1, :] * lb * (1.0 - lb)
        row = lax.broadcasted_iota(jnp.int32, tot.shape, 0)
        g = jnp.where(row == ROW_LB, -gl, jnp.where(row == ROW_LB + 1, gl, tot))
        d, m2, v2 = _adamw_math(wv, g, m_ref[...], v_ref[...])
        g_out[...] = g
        d_out[...] = d
        m_out[...] = m2
        v_out[...] = v2

    return pl.pallas_call(
        body, name=name, out_shape=[jax.ShapeDtypeStruct(w.shape, F32)] * 4, compiler_params=_cp())(gath, w, m, v)


D_MODEL = 1024


def _ffn_fwd(h, gain, wg, wu, wd, tag):
    xn, gg, uu, act = _norm_gate_up(h, gain, wg, wu, name=f"{tag}_gate_up")
    out = _mm([(act, wd)], residual=h, alpha=MACARON, tn=1024, name=f"{tag}_down")
    return out, (h, xn, gg, uu, act)


def _ffn_input_bwd(dg, du, wg, wu, x, gain, dres, chip_part, *, name, scale, tm=256):
    t, d = x.shape
    f = wg.shape[0]
    tm = min(tm, t)
    nt = t // tm
    fused = chip_part is not None

    def body(dg_ref, du_ref, wg_ref, wu_ref, x_ref, g_ref, dres_ref, *rest):
        if fused:
            part_ref, dx_ref, dxb_ref, dgain_ref, parts_ref = rest[:5]
            start, finish = _chip_exchange_phases(part_ref, parts_ref, *rest[5:])
            pl.when(pl.program_id(0) == 0)(start)
        else:
            dx_ref, dxb_ref, dgain_ref = rest
        dxn_v = _dot(dg_ref[...], wg_ref[...]) + _dot(du_ref[...], wu_ref[...])
        xv = x_ref[...]
        rstd = lax.rsqrt(jnp.mean(xv * xv, axis=-1, keepdims=True) + RMS_EPS)
        xhat = xv * rstd
        dxhat = dxn_v * g_ref[...]
        dx = dres_ref[...] + rstd * (dxhat - xhat * jnp.mean(dxhat * xhat, axis=-1, keepdims=True))
        dx_ref[...] = dx
        dxb_ref[...] = (dx * scale).astype(BF16)

        @pl.when(pl.program_id(0) == 0)
        def _():
            dgain_ref[...] = jnp.zeros_like(dgain_ref)

        dgain_ref[...] += jnp.sum(dxn_v * xhat, axis=0, keepdims=True)
        if fused:
            pl.when(pl.program_id(0) == nt - 1)(finish)

    wide = pl.BlockSpec((tm, f), lambda i: (i, 0))
    wsp = pl.BlockSpec((f, d), lambda i: (0, 0))
    row = pl.BlockSpec((tm, d), lambda i: (i, 0))
    vec = pl.BlockSpec((1, d), lambda i: (0, 0))
    args = [dg, du, wg, wu, x, gain, dres] + ([chip_part] if fused else [])
    return pl.pallas_call(
        body, name=name, grid=(nt,),
        in_specs=[wide, wide, wsp, wsp, row, vec, row] + ([HBM_SPEC] if fused else []),
        out_specs=[row, row, vec] + ([HBM_SPEC] if fused else []),
        out_shape=[jax.ShapeDtypeStruct((t, d), F32), jax.ShapeDtypeStruct((t, d), BF16), jax.ShapeDtypeStruct((1, d), F32)]
        + ([jax.ShapeDtypeStruct(chip_part.shape, chip_part.dtype)] if fused else []),
        scratch_shapes=_chip_exchange_scratch() if fused else [],
        compiler_params=_cp(dimension_semantics=("arbitrary",)))(*args)


def _ffn_bwd(dout, dout_half, saved, gain, wg, wu, wd, tag, next_scale, exchanges=None):
    h, xn, gg, uu, act = saved
    early_chip_part, send_after_dwd, chip_part_after_dwgu = exchanges if exchanges is not None else (None, None, None)
    dg, du, *early_parts = _swiglu_bwd(dout_half, wd, gg, uu, early_chip_part, tm=256, tf=wd.shape[0],
                                       name=f"{tag}_dact")
    dwd = _mm([(act, dout_half)], ta=True, tm=256, tn=1024, out_dtype=BF16, name=f"{tag}_dwd")
    send = send_after_dwd(dwd) if exchanges is not None else None
    dwg, dwu, *mid_parts = _mm_shared_rhs([dg, du], xn, tm=256, send=send, name=f"{tag}_dwgu")
    chip_part = chip_part_after_dwgu(dwg, dwu) if exchanges is not None else None
    dh, dh_b, dgain, *parts = _ffn_input_bwd(dg, du, wg, wu, h, gain, dout, chip_part, scale=next_scale,
                                             name=f"{tag}_input_bwd")
    return dh, dh_b, dwg, dwu, dwd, dgain, (early_parts + mid_parts + parts)


def kernel(x, ffn_pre_norm, ffn_pre_w_gate, ffn_pre_w_up, ffn_pre_w_down, mix_norm, ffn_post_norm, ffn_post_w_gate, ffn_post_w_up, ffn_post_w_down, ab_w_in, ab_conv_w, ab_w_out, c_w_in, c_lower_bounds, c_out_norm, c_w_out, final_norm, loss_target, m_ffn_pre_norm, m_ffn_pre_w_gate, m_ffn_pre_w_up, m_ffn_pre_w_down, m_mix_norm, m_ffn_post_norm, m_ffn_post_w_gate, m_ffn_post_w_up, m_ffn_post_w_down, m_ab_w_in, m_ab_conv_w, m_ab_w_out, m_c_w_in, m_c_lower_bounds, m_c_out_norm, m_c_w_out, m_final_norm, v_ffn_pre_norm, v_ffn_pre_w_gate, v_ffn_pre_w_up, v_ffn_pre_w_down, v_mix_norm, v_ffn_post_norm, v_ffn_post_w_gate, v_ffn_post_w_up, v_ffn_post_w_down, v_ab_w_in, v_ab_conv_w, v_ab_w_out, v_c_w_in, v_c_lower_bounds, v_c_out_norm, v_c_w_out, v_final_norm):
    d = D_MODEL
    h0 = x[0]
    target = loss_target[0]
    core = lax.axis_index("c").astype(jnp.int32).reshape(1)

    big = [("pre_g", ffn_pre_w_gate, m_ffn_pre_w_gate, v_ffn_pre_w_gate),
           ("pre_u", ffn_pre_w_up, m_ffn_pre_w_up, v_ffn_pre_w_up),
           ("pre_d", ffn_pre_w_down, m_ffn_pre_w_down, v_ffn_pre_w_down),
           ("post_g", ffn_post_w_gate, m_ffn_post_w_gate, v_ffn_post_w_gate),
           ("post_u", ffn_post_w_up, m_ffn_post_w_up, v_ffn_post_w_up),
           ("post_d", ffn_post_w_down, m_ffn_post_w_down, v_ffn_post_w_down),
           ("ab_in", ab_w_in, m_ab_w_in, v_ab_w_in),
           ("ab_out", ab_w_out, m_ab_w_out, v_ab_w_out),
           ("c_in", c_w_in, m_c_w_in, v_c_w_in),
           ("c_out", c_w_out, m_c_w_out, v_c_w_out)]
    by_tag = {tag: (w, m, v) for tag, w, m, v in big}

    def layer_rows(tag):
        w = by_tag[tag][0]
        return w.size // d // w.shape[0]

    def layout(items):
        offs, off = {}, 0
        for item in items:
            offs[item] = off
            off += layer_rows(item[0])
        return offs, off

    ffn = [f"{pos}_{kind}" for pos in ("pre", "post") for kind in "gud"]
    first_items = [("pre_g", 0), ("pre_u", 0)]
    early_items = [("pre_d", 0), ("ab_in", 0)]
    late_items = ([("pre_g", 1), ("pre_u", 1), ("pre_d", 1)] + [(f"post_{kind}", l) for l in (0, 1) for kind in "gud"]
                  + [("ab_out", 0), ("c_in", 0), ("c_out", 0)])
    grad_items = {"A0": [(f"post_{kind}", 1) for kind in "gud"] + [("c_out", 0)],
                  "A1": ([(f"pre_{kind}", 1) for kind in "gud"] + [(f"post_{kind}", 0) for kind in "gud"]
                         + [("c_in", 0), ("ab_out", 0)]),
                  "C": [("ab_in", 0)], "B0": [("pre_d", 0)], "B1": [("pre_g", 0), ("pre_u", 0)]}
    grad_offs = {k: layout(items)[0] for k, items in grad_items.items()}
    grad_conv_row = layout(grad_items["C"])[1]

    def conv_rows(a, split):
        flat = a.reshape(-1)
        if split:
            hi = flat.astype(BF16)
            flat = jnp.concatenate([hi, (flat - hi.astype(F32)).astype(BF16)])
        return jnp.zeros((16, d), flat.dtype).at[0, :flat.shape[0]].set(flat)

    nconv = ab_conv_w.size
    col_sharded = {"pre_g", "pre_u", "post_g", "post_u", "ab_in", "c_in"}

    def pack_rows(item):
        tag, layer = item
        a = by_tag[tag][0][layer]
        return (a.T if tag in col_sharded else a).reshape(-1, d).astype(BF16)

    first_pack = jnp.concatenate([pack_rows(item) for item in first_items], axis=0)
    early_pack = jnp.concatenate([pack_rows(item) for item in early_items] + [conv_rows(ab_conv_w, True)], axis=0)
    late_pack = jnp.concatenate([pack_rows(item) for item in late_items], axis=0)
    first_w = _all_gather(first_pack, [layer_rows(tag) for tag, _ in first_items], name="gather_first_weights")
    full = {item: g.reshape(-1, d) for item, g in zip(first_items, first_w)}

    xn0, gg0, uu0, act0, *early_w = _norm_gate_up(
        h0, ffn_pre_norm[0:1], full["pre_g", 0], full["pre_u", 0], name="l0pre_gate_up_gather_early_weights",
        pack=early_pack, seg_rows=[layer_rows(tag) for tag, _ in early_items] + [16])
    full.update({item: g.reshape(-1, d) for item, g in zip(early_items, early_w)})
    ffn_w = {("pre", 0): tuple(full[f"pre_{kind}", 0] for kind in "gud")}
    w_ab_in = full["ab_in", 0]
    cg = early_w[-1][:, 0, :2 * nconv].astype(F32)
    conv_w = (cg[:, :nconv] + cg[:, nconv:]).reshape(8, 3, -1).transpose(1, 0, 2).reshape(3, -1)
    aw = w_ab_in.shape[0] // 6
    h1 = _mm([(act0, full["pre_d", 0])], residual=h0, alpha=MACARON, tn=1024, name="l0pre_down")
    s_pre0 = (h0, xn0, gg0, uu0, act0)
    hn0, pa, pb = _norm_proj(h1, mix_norm[0:1], w_ab_in, (F32, BF16), tm=512, name="ab_norm_proj")
    ya = _conv_fwd(pa, conv_w, name="conv_fwd")
    yb, ltot, *late_w = _attn_fwd(pb, late_pack, [layer_rows(tag) for tag, _ in late_items],
                                  name="attn_fwd_gather_late_weights")
    full.update({item: g.reshape(-1, d) for item, g in zip(late_items, late_w)})
    for pos, layer in (("post", 0), ("pre", 1), ("post", 1)):
        ffn_w[pos, layer] = tuple(full[f"{pos}_{kind}", layer] for kind in "gud")
    w_ab_out, w_c_in, w_c_out = full["ab_out", 0], full["c_in", 0], full["c_out", 0]
    h2 = _mm([(ya, w_ab_out[:aw]), (yb, w_ab_out[aw:])], residual=h1, tn=1024, name="ab_out")
    h3, s_post0 = _ffn_fwd(h2, ffn_post_norm[0:1], *ffn_w["post", 0], "l0post")
    h4, s_pre1 = _ffn_fwd(h3, ffn_pre_norm[1:2], *ffn_w["pre", 1], "l1pre")
    hn1, pc = _norm_proj(h4, mix_norm[1:2], w_c_in, (F32,), tm=256, name="c_norm_proj")
    yc, o_saved, states = _hgrn_fwd(pc, c_lower_bounds, c_out_norm, name="hgrn_fwd")
    h5 = _mm([(yc, w_c_out)], residual=h4, tn=1024, name="c_out")
    h6, s_post1 = _ffn_fwd(h5, ffn_post_norm[1:2], *ffn_w["post", 1], "l1post")
    dh6, dh6_b, d_final, loss_vec = _loss_head(h6, final_norm.reshape(1, d), target, name="loss_head")

    gw = {}

    def grad_send(key, extra=()):
        return [g.reshape(4, 2, -1, d) for g in [gw[item] for item in grad_items[key]] + list(extra)]

    def chip_partials(key, extra=()):
        send = jnp.concatenate(grad_send(key, extra), axis=2)
        from_sibling = _sibling_exchange(send, name=f"grad{key}_sibling_exchange")
        return _pair_add(send, from_sibling, core, name=f"grad{key}_pair_add")

    dh5, dh5_b, gw["post_g", 1], gw["post_u", 1], gw["post_d", 1], d_post1, *_ = _ffn_bwd(
        dh6, dh6_b, s_post1, ffn_post_norm[1:2], *ffn_w["post", 1], "l1post", 1.0)
    dyc = _mm([(dh5_b, w_c_out)], tb=True, tn=1024, name="c_out_dy")
    g_c_out = _mm([(yc, dh5_b)], ta=True, tm=256, tn=1024, out_dtype=BF16, name="c_out_dw")
    gw["c_out", 0] = g_c_out
    dcq, dcf, dci, dcg, dlb, d_onorm, parts_a0 = _hgrn_bwd(pc, o_saved, states, dyc, c_lower_bounds, c_out_norm,
                                                           grad_send("A0"), name="hgrn_bwd_exchange_grads_a0")
    dparts = [dcq, dcf, dci, dcg]
    g_c_in = jnp.concatenate(_mm_shared_rhs(dparts, hn1, tm=256, name="c_in_dw"), axis=0)
    cw = w_c_in.shape[0] // 4
    dhn1 = _mm([(dp, w_c_in[i * cw:(i + 1) * cw]) for i, dp in enumerate(dparts)], tm=512, tn=1024, name="c_in_dx")
    dh4, dh4_b, d_mix1 = _rmsnorm_bwd(h4, mix_norm[1:2], dhn1, dh5, scale=MACARON, name="l1_mix_norm_bwd")
    dh3, dh3_b, gw["pre_g", 1], gw["pre_u", 1], gw["pre_d", 1], d_pre1, *_ = _ffn_bwd(
        dh4, dh4_b, s_pre1, ffn_pre_norm[1:2], *ffn_w["pre", 1], "l1pre", MACARON)
    dh2, dh2_b, gw["post_g", 0], gw["post_u", 0], gw["post_d", 0], d_post0, *_ = _ffn_bwd(
        dh3, dh3_b, s_post0, ffn_post_norm[0:1], *ffn_w["post", 0], "l0post", 1.0)
    dyab = _mm([(dh2_b, w_ab_out)], tb=True, tn=1024, name="ab_out_dy")
    g_ab_out = jnp.concatenate(_mm_shared_rhs([ya, yb], dh2_b, tm=256, name="ab_out_dw"), axis=0)
    dab, dac, dax, g_conv = _conv_bwd(pa, dyab, conv_w, name="conv_bwd")

    gw["c_in", 0], gw["ab_out", 0] = g_c_in, g_ab_out
    dq, dk, dv, parts_a1 = _attn_bwd(pb, dyab, ltot, grad_send("A1"), name="attn_bwd_exchange_grads_a1")
    dparts = [dab, dac, dax, dq, dk, dv]
    g_ab_in = jnp.concatenate(_mm_shared_rhs(dparts, hn0, tm=128, name="ab_in_dw"), axis=0)
    dhn0 = _mm([(dp, w_ab_in[i * aw:(i + 1) * aw]) for i, dp in enumerate(dparts)], tm=512, tn=1024, name="ab_in_dx")
    dh1, dh1_b, d_mix0 = _rmsnorm_bwd(h1, mix_norm[0:1], dhn0, dh2, scale=MACARON, name="l0_mix_norm_bwd")
    gw["ab_in", 0] = g_ab_in
    gconv_own = g_conv.reshape(3, 8, -1).transpose(1, 0, 2).reshape(8, -1)
    conv_piece = jnp.zeros((8, 16, d), F32).at[:, 0, :nconv].set(gconv_own).astype(BF16)

    def send_b0(dwd):
        gw["pre_d", 0] = dwd
        return grad_send("B0")

    def chip_part_b1(dwg, dwu):
        gw["pre_g", 0], gw["pre_u", 0] = dwg, dwu
        return chip_partials("B1")

    dh0, _, _, _, _, d_pre0, (parts_c, parts_b0, parts_b1) = _ffn_bwd(
        dh1, dh1_b, s_pre0, ffn_pre_norm[0:1], *ffn_w["pre", 0], "l0pre", 1.0,
        (chip_partials("C", [conv_piece]), send_b0, chip_part_b1))

    parts = {"A0": parts_a0, "A1": parts_a1, "B0": parts_b0, "B1": parts_b1, "C": parts_c}
    upd = {}
    for tag, w, m, v in big:
        view = (lambda a: jnp.swapaxes(a, 1, 2)) if tag in col_sharded else (lambda a: a)
        where = {layer: (key, grad_offs[key][tag, layer])
                 for key in grad_items for t2, layer in grad_items[key] if t2 == tag}
        res = None
        for layer in sorted(where):
            key, off = where[layer]
            res = _adamw_shard(parts[key], off, view(w), view(m), view(v), layer, res, name=f"adamw_{tag}{layer}")
        upd[tag] = [view(a) for a in res]
    res = _adamw_shard(parts["C"], grad_conv_row, *(conv_rows(a, False)[None] for a in (ab_conv_w, m_ab_conv_w, v_ab_conv_w)),
                       0, None, name="adamw_conv")
    upd["conv"] = [r[0, 0, :nconv].reshape(ab_conv_w.shape) for r in res]

    def small_pack(pre, mix, post, final, lbs, onorm):
        def slot(parts):
            out, r = jnp.zeros((SLOT, d), F32), 0
            for a in (parts if isinstance(parts, tuple) else (parts,)):
                out = out.at[r:r + a.shape[0], :a.shape[1]].set(a)
                r += a.shape[0]
            return out

        return jnp.concatenate([slot(pre), slot(mix), slot(post), slot(final.reshape(1, d)), slot(lbs), slot(onorm)], axis=0)

    d_on = d_onorm.reshape(-1, c_out_norm.shape[1]).sum(axis=0, keepdims=True)
    gsmall = small_pack((d_pre0, d_pre1), (d_mix0, d_mix1), (d_post0, d_post1), d_final, dlb, d_on)
    gsmall_all = _all_gather(gsmall, name="gather_small_grads")
    sres = _small_update(
        gsmall_all,
        small_pack(ffn_pre_norm, mix_norm, ffn_post_norm, final_norm, c_lower_bounds, c_out_norm),
        small_pack(m_ffn_pre_norm, m_mix_norm, m_ffn_post_norm, m_final_norm, m_c_lower_bounds, m_c_out_norm),
        small_pack(v_ffn_pre_norm, v_mix_norm, v_ffn_post_norm, v_final_norm, v_c_lower_bounds, v_c_out_norm),
        name="small_update")

    def small_out(r):
        return {"pre_norm": r[0:2], "mix_norm": r[SLOT:SLOT + 2], "post_norm": r[2 * SLOT:2 * SLOT + 2],
                "final": r[3 * SLOT], "lb": r[ROW_LB:ROW_LB + 2], "onorm": r[5 * SLOT:5 * SLOT + 1, :c_out_norm.shape[1]]}

    small = [small_out(r) for r in sres]
    outs = []
    for k in range(4):
        s = small[k]
        outs += [s["pre_norm"], upd["pre_g"][k], upd["pre_u"][k], upd["pre_d"][k], s["mix_norm"], s["post_norm"],
                 upd["post_g"][k], upd["post_u"][k], upd["post_d"][k], upd["ab_in"][k], upd["conv"][k],
                 upd["ab_out"][k], upd["c_in"][k], s["lb"], s["onorm"], upd["c_out"][k], s["final"]]
    loss = lax.psum(loss_vec[0, 0], ("x", "y", "c"))
    return (loss, dh0[None], *outs)
```

```python
import math

import jax
import jax.numpy as jnp
from jax import lax
from jax.experimental import pallas as pl
from jax.experimental.pallas import tpu as pltpu

F32 = jnp.float32
BF16 = jnp.bfloat16
MESH = pl.DeviceIdType.MESH

RMS_EPS = 1e-6
MACARON = 0.5
LANES = 128
CHUNK = 64
N_LEVELS = 6
HGRN_HEADS = 2
SB_KEYS = 256
SB_STRIP = 64
ADAM_LR, ADAM_B1, ADAM_B2, ADAM_EPS, ADAM_WD, ADAM_STEP = 0.001, 0.9, 0.999, 1e-08, 0.01, 10
VMEM_LIMIT = 48 * 1024 * 1024


def _cp(**kw):
    return pltpu.CompilerParams(vmem_limit_bytes=VMEM_LIMIT, **kw)


def _sigmoid(x):
    return 0.5 * jnp.tanh(0.5 * x) + 0.5


def _bf(x):
    return x if x.dtype == BF16 else x.astype(BF16)


def _split3(x):
    hi = x.astype(BF16)
    r1 = x - hi.astype(F32)
    mid = r1.astype(BF16)
    lo = (r1 - mid.astype(F32)).astype(BF16)
    return hi, mid, lo


def _dot(a, b, ca=1, cb=0):
    return lax.dot_general(a, b, (((ca,), (cb,)), ((), ())), preferred_element_type=F32)


def _dot_exact_lhs(m, x):
    hi, mid, lo = _split3(x)
    return _dot(m, hi) + _dot(m, mid) + _dot(m, lo)


def _mm(terms, *, name, ta=False, tb=False, out_dtype=F32, residual=None, alpha=1.0, tm=512, tn=512):
    nt = len(terms)
    a0, b0 = terms[0]
    m = a0.shape[1] if ta else a0.shape[0]
    n = b0.shape[0] if tb else b0.shape[1]
    tm, tn = min(tm, m), min(tn, n)
    assert m % tm == 0 and n % tn == 0, (name, m, n, tm, tn)
    has_res = residual is not None

    def body(*refs):
        o_ref = refs[-1]
        acc = None
        for i in range(nt):
            a = _bf(refs[2 * i][...])
            b = _bf(refs[2 * i + 1][...])
            p = _dot(a, b, 0 if ta else 1, 1 if tb else 0)
            acc = p if acc is None else acc + p
        if alpha != 1.0:
            acc = acc * alpha
        if has_res:
            acc = acc + refs[2 * nt][...]
        o_ref[...] = acc.astype(out_dtype)

    in_specs, args = [], []
    for a, b in terms:
        k = a.shape[0] if ta else a.shape[1]
        assert (b.shape[1] if tb else b.shape[0]) == k, (name, a.shape, b.shape)
        in_specs.append(pl.BlockSpec((k, tm), lambda i, j: (0, i)) if ta else pl.BlockSpec((tm, k), lambda i, j: (i, 0)))
        in_specs.append(pl.BlockSpec((tn, k), lambda i, j: (j, 0)) if tb else pl.BlockSpec((k, tn), lambda i, j: (0, j)))
        args += [a, b]
    if has_res:
        in_specs.append(pl.BlockSpec((tm, tn), lambda i, j: (i, j)))
        args.append(residual)
    return pl.pallas_call(
        body, name=name, grid=(m // tm, n // tn), in_specs=in_specs,
        out_specs=pl.BlockSpec((tm, tn), lambda i, j: (i, j)),
        out_shape=jax.ShapeDtypeStruct((m, n), out_dtype), compiler_params=_cp())(*args)


def _norm_proj(x, gain, w_t, out_dtypes, *, name, tm):
    t, d = x.shape
    n = w_t.shape[0]
    tm = min(tm, t)
    npart = len(out_dtypes)
    width = n // npart

    def body(x_ref, g_ref, w_ref, xn_ref, *part_refs):
        xv = x_ref[...]
        rstd = lax.rsqrt(jnp.mean(xv * xv, axis=-1, keepdims=True) + RMS_EPS)
        xn = (xv * rstd * g_ref[...]).astype(BF16)
        xn_ref[...] = xn
        for p, ref in enumerate(part_refs):
            ref[...] = _dot(xn, w_ref[p * width:(p + 1) * width, :], 1, 1).astype(out_dtypes[p])

    row = pl.BlockSpec((tm, d), lambda i: (i, 0))
    return pl.pallas_call(
        body, name=name, grid=(t // tm,),
        in_specs=[row, pl.BlockSpec((1, d), lambda i: (0, 0)), pl.BlockSpec((n, d), lambda i: (0, 0))],
        out_specs=[row] + [pl.BlockSpec((tm, width), lambda i: (i, 0))] * npart,
        out_shape=[jax.ShapeDtypeStruct((t, d), BF16)] + [jax.ShapeDtypeStruct((t, width), dt) for dt in out_dtypes],
        compiler_params=_cp())(x, gain, w_t)


def _mm_shared_rhs(a_list, b, *, name, tm, out_dtype=BF16, send=None):
    k, n = b.shape
    assert all(a.shape[0] == k and a.shape[1] % tm == 0 and a.shape[1] == a_list[0].shape[1] for a in a_list)
    m = a_list[0].shape[1]
    na = len(a_list)
    nsteps = m // tm
    sends = list(send) if send is not None else []
    ns = len(sends)

    def body(*refs):
        first_out = na + 1 + ns
        if ns:
            start, finish = _direct_exchange_phases(refs[na + 1:first_out], refs[first_out + na], *refs[first_out + na + 1:])
            pl.when(pl.program_id(0) == 0)(start)
        bv = refs[na][...]
        for i in range(na):
            refs[first_out + i][...] = _dot(refs[i][...], bv, 0, 0).astype(out_dtype)
        if ns:
            pl.when(pl.program_id(0) == nsteps - 1)(finish)

    return pl.pallas_call(
        body, name=name, grid=(nsteps,),
        in_specs=[pl.BlockSpec((k, tm), lambda i: (0, i))] * na + [pl.BlockSpec((k, n), lambda i: (0, 0))] + [HBM_SPEC] * ns,
        out_specs=[pl.BlockSpec((tm, n), lambda i: (i, 0))] * na + ([HBM_SPEC] if ns else []),
        out_shape=[jax.ShapeDtypeStruct((m, n), out_dtype)] * na
        + ([jax.ShapeDtypeStruct((8, _direct_exchange_rows(sends), sends[0].shape[3]), sends[0].dtype)] if ns else []),
        scratch_shapes=_direct_exchange_scratch() if ns else [],
        compiler_params=_cp(dimension_semantics=("arbitrary",)))(*a_list, b, *sends)


def _rmsnorm_bwd(x, gain, dxn, dres, *, name, scale, tm=512):
    t, d = x.shape
    tm = min(tm, t)

    def body(x_ref, g_ref, dxn_ref, dres_ref, dx_ref, dxb_ref, dg_ref):
        xv = x_ref[...]
        rstd = lax.rsqrt(jnp.mean(xv * xv, axis=-1, keepdims=True) + RMS_EPS)
        xhat = xv * rstd
        dxn_v = dxn_ref[...]
        dxhat = dxn_v * g_ref[...]
        dx = dres_ref[...] + rstd * (dxhat - xhat * jnp.mean(dxhat * xhat, axis=-1, keepdims=True))
        dx_ref[...] = dx
        dxb_ref[...] = (dx * scale).astype(BF16)

        @pl.when(pl.program_id(0) == 0)
        def _():
            dg_ref[...] = jnp.zeros_like(dg_ref)

        dg_ref[...] += jnp.sum(dxn_v * xhat, axis=0, keepdims=True)

    row = pl.BlockSpec((tm, d), lambda i: (i, 0))
    vec = pl.BlockSpec((1, d), lambda i: (0, 0))
    return pl.pallas_call(
        body, name=name, grid=(t // tm,), in_specs=[row, vec, row, row], out_specs=[row, row, vec],
        out_shape=[jax.ShapeDtypeStruct((t, d), F32), jax.ShapeDtypeStruct((t, d), BF16), jax.ShapeDtypeStruct((1, d), F32)],
        compiler_params=_cp())(x, gain, dxn, dres)


def _loss_head(h, gain, target, *, name, tm=512):
    t, d = h.shape
    tm = min(tm, t)

    def body(h_ref, g_ref, t_ref, dh_ref, dhb_ref, dg_ref, loss_ref):
        hv = h_ref[...]
        rstd = lax.rsqrt(jnp.mean(hv * hv, axis=-1, keepdims=True) + RMS_EPS)
        xhat = hv * rstd
        err = xhat * g_ref[...] - t_ref[...]
        dy = err * (1.0 / d)
        dxhat = dy * g_ref[...]
        dh = rstd * (dxhat - xhat * jnp.mean(dxhat * xhat, axis=-1, keepdims=True))
        dh_ref[...] = dh
        dhb_ref[...] = (dh * MACARON).astype(BF16)

        @pl.when(pl.program_id(0) == 0)
        def _():
            dg_ref[...] = jnp.zeros_like(dg_ref)
            loss_ref[...] = jnp.zeros_like(loss_ref)

        dg_ref[...] += jnp.sum(dy * xhat, axis=0, keepdims=True)
        part = jnp.sum(jnp.sum(err * err, axis=-1, keepdims=True), axis=0, keepdims=True) * (0.5 / d)
        loss_ref[...] += jnp.broadcast_to(part, loss_ref.shape)

    row = pl.BlockSpec((tm, d), lambda i: (i, 0))
    vec = pl.BlockSpec((1, d), lambda i: (0, 0))
    return pl.pallas_call(
        body, name=name, grid=(t // tm,), in_specs=[row, vec, row],
        out_specs=[row, row, vec, pl.BlockSpec((1, LANES), lambda i: (0, 0))],
        out_shape=[jax.ShapeDtypeStruct((t, d), F32), jax.ShapeDtypeStruct((t, d), BF16), jax.ShapeDtypeStruct((1, d), F32),
                   jax.ShapeDtypeStruct((1, LANES), F32)],
        compiler_params=_cp())(h, gain, target)


def _norm_gate_up(x, gain, wg, wu, *, name, tm=256, tf=2816, pack=None, seg_rows=()):
    t, d = x.shape
    f = wg.shape[0]
    tm, tf = min(tm, t), min(tf, f)
    assert f % tf == 0
    ni, nj = t // tm, f // tf
    nseg = len(seg_rows)

    def body(x_ref, g_ref, wg_ref, wu_ref, *rest):
        if pack is not None:
            pack_ref, xn_ref, gg_ref, uu_ref, act_ref = rest[:5]
            start, forward, finish = _gather_phases(pack_ref, rest[5:5 + nseg], seg_rows, *rest[5 + nseg:])
            step = pl.program_id(0) * nj + pl.program_id(1)
            pl.when(step == 0)(start)
            pl.when(step == (3 * ni * nj) // 4)(forward)
        else:
            xn_ref, gg_ref, uu_ref, act_ref = rest

        @pl.when(pl.program_id(1) == 0)
        def _():
            xv = x_ref[...]
            rstd = lax.rsqrt(jnp.mean(xv * xv, axis=-1, keepdims=True) + RMS_EPS)
            xn_ref[...] = (xv * rstd * g_ref[...]).astype(BF16)

        xn = xn_ref[...]
        gv = _dot(xn, wg_ref[...], 1, 1)
        uv = _dot(xn, wu_ref[...], 1, 1)
        gg_ref[...] = gv.astype(BF16)
        uu_ref[...] = uv.astype(BF16)
        act_ref[...] = (gv * _sigmoid(gv) * uv).astype(BF16)
        if pack is not None:
            pl.when(step == ni * nj - 1)(finish)

    row = pl.BlockSpec((tm, d), lambda i, j: (i, 0))
    wsp = pl.BlockSpec((tf, d), lambda i, j: (j, 0))
    osp = pl.BlockSpec((tm, tf), lambda i, j: (i, j))
    fused = pack is not None
    return pl.pallas_call(
        body, name=name, grid=(ni, nj),
        in_specs=[row, pl.BlockSpec((1, d), lambda i, j: (0, 0)), wsp, wsp] + ([HBM_SPEC] if fused else []),
        out_specs=[row, osp, osp, osp] + [HBM_SPEC] * nseg,
        out_shape=[jax.ShapeDtypeStruct((t, d), BF16)] + [jax.ShapeDtypeStruct((t, f), BF16)] * 3
        + [jax.ShapeDtypeStruct((8, n, d), BF16) for n in seg_rows],
        scratch_shapes=_gather_scratch() if fused else [],
        compiler_params=_cp(dimension_semantics=("arbitrary", "arbitrary")))(x, gain, wg, wu, *([pack] if fused else []))


def _swiglu_bwd(dout, wd, gg, uu, chip_part=None, *, name, tm=512, tf=1408):
    t, d = dout.shape
    f = wd.shape[0]
    tm, tf = min(tm, t), min(tf, f)
    nj, ni = f // tf, t // tm
    fused = chip_part is not None

    def body(do_ref, wd_ref, g_ref, u_ref, *rest):
        if fused:
            part_ref, dg_ref, du_ref, parts_ref = rest[:4]
            start, finish = _chip_exchange_phases(part_ref, parts_ref, *rest[4:])
            step = pl.program_id(0) * ni + pl.program_id(1)
            pl.when(step == 0)(start)
        else:
            dg_ref, du_ref = rest
        dact = _dot(do_ref[...], wd_ref[...], 1, 1)
        gv = g_ref[...].astype(F32)
        uv = u_ref[...].astype(F32)
        sg = _sigmoid(gv)
        dg_ref[...] = (dact * uv * (sg * (1.0 + gv * (1.0 - sg)))).astype(BF16)
        du_ref[...] = (dact * (gv * sg)).astype(BF16)
        if fused:
            pl.when(step == nj * ni - 1)(finish)

    osp = pl.BlockSpec((tm, tf), lambda j, i: (i, j))
    return pl.pallas_call(
        body, name=name, grid=(nj, ni),
        in_specs=[pl.BlockSpec((tm, d), lambda j, i: (i, 0)), pl.BlockSpec((tf, d), lambda j, i: (j, 0)), osp, osp]
        + ([HBM_SPEC] if fused else []),
        out_specs=[osp, osp] + ([HBM_SPEC] if fused else []),
        out_shape=[jax.ShapeDtypeStruct((t, f), BF16)] * 2
        + ([jax.ShapeDtypeStruct(chip_part.shape, chip_part.dtype)] if fused else []),
        scratch_shapes=_chip_exchange_scratch() if fused else [],
        compiler_params=_cp(dimension_semantics=("arbitrary", "arbitrary")))(dout, wd, gg, uu, *([chip_part] if fused else []))


def _shift_down(x, n):
    rows = lax.broadcasted_iota(jnp.int32, x.shape, 0)
    return jnp.where(rows >= n, pltpu.roll(x, n, 0), 0.0)


def _shift_up(x, n):
    t = x.shape[0]
    rows = lax.broadcasted_iota(jnp.int32, x.shape, 0)
    return jnp.where(rows < t - n, pltpu.roll(x, t - n, 0), 0.0)


def _conv_fwd(pa, conv_w, *, name):
    t = pa.shape[0]
    nb = pa.shape[1] // 3 // LANES

    def body(b_ref, c_ref, x_ref, w_ref, y_ref):
        u = c_ref[...] * x_ref[...]
        w = w_ref[...]
        conv = w[2:3, :] * u + w[1:2, :] * _shift_down(u, 1) + w[0:1, :] * _shift_down(u, 2)
        y_ref[...] = (b_ref[...] * conv).astype(BF16)

    def col(off):
        return pl.BlockSpec((t, LANES), lambda j: (0, off + j))

    return pl.pallas_call(
        body, name=name, grid=(nb,),
        in_specs=[col(0), col(nb), col(2 * nb), pl.BlockSpec((3, LANES), lambda j: (0, j))],
        out_specs=pl.BlockSpec((t, LANES), lambda j: (0, j)),
        out_shape=jax.ShapeDtypeStruct((t, nb * LANES), BF16), compiler_params=_cp())(pa, pa, pa, conv_w)


def _conv_bwd(pa, dy, conv_w, *, name):
    t = pa.shape[0]
    nb = pa.shape[1] // 3 // LANES

    def body(b_ref, c_ref, x_ref, dy_ref, w_ref, db_ref, dc_ref, dx_ref, dw_ref):
        cv, xv = c_ref[...], x_ref[...]
        u = cv * xv
        u1, u2 = _shift_down(u, 1), _shift_down(u, 2)
        w = w_ref[...]
        conv = w[2:3, :] * u + w[1:2, :] * u1 + w[0:1, :] * u2
        dyv = dy_ref[...]
        db_ref[...] = (dyv * conv).astype(BF16)
        dconv = dyv * b_ref[...]
        du = w[2:3, :] * dconv + w[1:2, :] * _shift_up(dconv, 1) + w[0:1, :] * _shift_up(dconv, 2)
        dc_ref[...] = (du * xv).astype(BF16)
        dx_ref[...] = (du * cv).astype(BF16)
        dw_ref[0:1, :] = jnp.sum(dconv * u2, axis=0, keepdims=True)
        dw_ref[1:2, :] = jnp.sum(dconv * u1, axis=0, keepdims=True)
        dw_ref[2:3, :] = jnp.sum(dconv * u, axis=0, keepdims=True)

    def col(off):
        return pl.BlockSpec((t, LANES), lambda j: (0, off + j))

    osp = pl.BlockSpec((t, LANES), lambda j: (0, j))
    wsp = pl.BlockSpec((3, LANES), lambda j: (0, j))
    return pl.pallas_call(
        body, name=name, grid=(nb,), in_specs=[col(0), col(nb), col(2 * nb), col(0), wsp],
        out_specs=[osp, osp, osp, wsp],
        out_shape=[jax.ShapeDtypeStruct((t, nb * LANES), BF16)] * 3 + [jax.ShapeDtypeStruct((3, nb * LANES), F32)],
        compiler_params=_cp())(pa, pa, pa, dy, conv_w)


def _sb_consts():
    j = lax.broadcasted_iota(jnp.int32, (SB_KEYS, SB_KEYS), 0)
    s = lax.broadcasted_iota(jnp.int32, (SB_KEYS, SB_KEYS), 1)
    after = (j > s).astype(BF16)
    upto = (j <= s).astype(BF16)
    before = (j < s).astype(BF16)
    return after, jnp.stack([upto, before])


def _log_sigmoid(z):
    return jnp.minimum(z, 0.0) - jnp.log(1.0 + jnp.exp(-jnp.abs(z)))


def _attn_fwd(pb, late_pack, seg_rows, *, name, tq=256):
    t = pb.shape[0]
    npair = pb.shape[1] // 3 // LANES
    tq = min(tq, t)
    nq = t // tq
    cmat, _ = _sb_consts()
    scale = 1.0 / math.sqrt(LANES // 2)

    nseg = len(seg_rows)

    def body(q_ref, k_ref, v_ref, c_ref, late_ref, y_ref, lt_ref, *rest):
        i = pl.program_id(1)
        pair = pl.program_id(0)
        scratch = rest[nseg:nseg + 7]
        start, forward, finish = _gather_phases(late_ref, rest[:nseg], seg_rows, *rest[nseg + 7:])
        pl.when((pair == 0) & (i == 0))(start)
        pl.when((pair == npair - 1) & (i == nq // 2))(forward)
        lane = lax.broadcasted_iota(jnp.int32, (tq, LANES), 1)
        rowpos = i * tq + lax.broadcasted_iota(jnp.int32, (tq, SB_KEYS), 0)
        colid = lax.broadcasted_iota(jnp.int32, (tq, SB_KEYS), 1)
        q2 = q_ref[...] * jnp.asarray(scale, BF16)
        cm = c_ref[...]
        hi_lanes = lane >= LANES // 2
        qhs = [jnp.where(hi_lanes == (hh == 1), q2, jnp.zeros_like(q2)) for hh in range(2)]
        per_q = tq // SB_KEYS

        def blk(jb):
            return pl.ds(pl.multiple_of(jb * SB_KEYS, SB_KEYS), SB_KEYS)

        zbuf, wbuf, accbuf, runbuf, hibuf, lobuf, prebuf = scratch

        def scores(jb):
            kb = k_ref[blk(jb), :]
            for hh in range(2):
                zbuf[hh] = _dot(qhs[hh], kb, 1, 1)

        def values(jb):
            vb = v_ref[blk(jb), :]
            for hh in range(2):
                accbuf[hh] += _dot(wbuf[hh], vb)

        def trip(jb, masked, first=False):
            strips = [slice(r, r + SB_STRIP) for r in range(0, tq, SB_STRIP)]
            masks = [(jb * SB_KEYS + colid[s]) < rowpos[s] for s in strips] if masked else None
            if not first:
                values(jb + 1)
            css = []
            for hh in range(2):
                for n, s in enumerate(strips):
                    z = zbuf[hh, s]
                    lb = _log_sigmoid(z)
                    lk = lb - z
                    if masked:
                        lk = jnp.where(masks[n], lk, 0.0)
                    hibuf[hh, s], lobuf[hh, s] = _split2(lk)
                    run = runbuf[hh, s]
                    prebuf[hh, s] = lb + run
                    runbuf[hh, s] = run + jnp.sum(lk, axis=1, keepdims=True)
                css.append(_dot(hibuf[hh], cm) + _dot(lobuf[hh], cm))
            scores(jnp.maximum(jb - 1, 0))
            for hh in range(2):
                for n, s in enumerate(strips):
                    w = jnp.exp(prebuf[hh, s] + css[hh][s])
                    if masked:
                        w = jnp.where(masks[n], w, 0.0)
                    wbuf[hh, s] = w.astype(BF16)

        nfull = i * per_q
        accbuf[...] = jnp.zeros_like(accbuf)
        runbuf[...] = jnp.zeros_like(runbuf)
        scores(nfull + per_q - 1)
        for dblk in reversed(range(per_q)):
            trip(nfull + dblk, True, first=dblk == per_q - 1)

        def full_block(n, carry):
            trip(nfull - 1 - n, False)
            return carry

        lax.fori_loop(0, nfull, full_block, 0)
        values(0)
        y_ref[...] = jnp.where(hi_lanes, accbuf[1], accbuf[0]).astype(BF16)
        lt_ref[...] = jnp.where(hi_lanes, runbuf[1], runbuf[0])
        pl.when((pair == npair - 1) & (i == nq - 1))(finish)

    return pl.pallas_call(
        body, name=name, grid=(npair, nq),
        in_specs=[pl.BlockSpec((tq, LANES), lambda p, i: (i, p)),
                  pl.BlockSpec((t, LANES), lambda p, i: (0, npair + p)),
                  pl.BlockSpec((t, LANES), lambda p, i: (0, 2 * npair + p)),
                  pl.BlockSpec((SB_KEYS, SB_KEYS), lambda p, i: (0, 0)),
                  HBM_SPEC],
        out_specs=[pl.BlockSpec((tq, LANES), lambda p, i: (i, p))] * 2 + [HBM_SPEC] * nseg,
        out_shape=[jax.ShapeDtypeStruct((t, npair * LANES), BF16), jax.ShapeDtypeStruct((t, npair * LANES), F32),
                   ] + [jax.ShapeDtypeStruct((8, n, late_pack.shape[1]), late_pack.dtype) for n in seg_rows],
        scratch_shapes=[pltpu.VMEM((2, tq, SB_KEYS), F32), pltpu.VMEM((2, tq, SB_KEYS), BF16),
                        pltpu.VMEM((2, tq, LANES), F32), pltpu.VMEM((2, tq, 1), F32),
                        pltpu.VMEM((2, tq, SB_KEYS), BF16), pltpu.VMEM((2, tq, SB_KEYS), BF16),
                        pltpu.VMEM((2, tq, SB_KEYS), F32)] + _gather_scratch(),
        compiler_params=_cp(dimension_semantics=("arbitrary", "arbitrary")))(pb, pb, pb, cmat, late_pack)


def _attn_bwd(pb, dy, ltot, send, *, name, tq=256):
    t = pb.shape[0]
    npair = pb.shape[1] // 3 // LANES
    tq = min(tq, t)
    nq = t // tq
    _, cmats = _sb_consts()
    scale = 1.0 / math.sqrt(LANES // 2)
    sends = list(send)
    ns = len(sends)

    def body(q_ref, k_ref, v_ref, dy_ref, lt_ref, c_ref, *rest):
        i = pl.program_id(1)
        pair = pl.program_id(0)
        send_refs = rest[:ns]
        dq_ref, dk_ref, dv_ref, parts_ref, dk_acc, dv_acc = rest[ns:ns + 6]
        scratch = rest[ns + 6:ns + 12]
        start, finish = _direct_exchange_phases(send_refs, parts_ref, *rest[ns + 12:])
        pl.when((pair == 0) & (i == 0))(start)

        @pl.when(i == 0)
        def _():
            dk_acc[...] = jnp.zeros_like(dk_acc)
            dv_acc[...] = jnp.zeros_like(dv_acc)

        lane = lax.broadcasted_iota(jnp.int32, (tq, LANES), 1)
        rowpos = i * tq + lax.broadcasted_iota(jnp.int32, (tq, SB_KEYS), 0)
        colid = lax.broadcasted_iota(jnp.int32, (tq, SB_KEYS), 1)
        q2 = q_ref[...] * jnp.asarray(scale, BF16)
        do2 = dy_ref[...].astype(BF16)
        ltv = lt_ref[...]
        c_upto, c_before = c_ref[0], c_ref[1]
        hi_lanes = lane >= LANES // 2
        sels = [hi_lanes == (hh == 1) for hh in range(2)]
        qhs = [jnp.where(s, q2, jnp.zeros_like(q2)) for s in sels]
        dohs = [jnp.where(s, do2, jnp.zeros_like(do2)) for s in sels]
        lts = [ltv[:, 0:1], ltv[:, LANES // 2:LANES // 2 + 1]]
        per_q = tq // SB_KEYS

        def blk(jb):
            return pl.ds(pl.multiple_of(jb * SB_KEYS, SB_KEYS), SB_KEYS)

        zbuf, dabuf, dzbuf, abuf, dqbuf, sumbuf = scratch

        def scores(jb):
            kb, vb = k_ref[blk(jb), :], v_ref[blk(jb), :]
            for hh in range(2):
                zbuf[hh] = _dot(qhs[hh], kb, 1, 1)
                dabuf[hh] = _dot(dohs[hh], vb, 1, 1)

        def products(jb):
            kb = k_ref[blk(jb), :]
            dk_acc[blk(jb), :] += _dot(dzbuf[0], qhs[0], 0, 0) + _dot(dzbuf[1], qhs[1], 0, 0)
            dv_acc[blk(jb), :] += _dot(abuf[0], dohs[0], 0, 0) + _dot(abuf[1], dohs[1], 0, 0)
            for hh in range(2):
                dqbuf[hh] += _dot(dzbuf[hh], kb)

        def trip(jb, masked):
            mask = (jb * SB_KEYS + colid) < rowpos if masked else None
            products(jnp.maximum(jb - 1, 0))
            lbs, css, es, ces = [], [], [], []
            for hh in range(2):
                z = zbuf[hh]
                lb = _log_sigmoid(z)
                lk = lb - z
                if masked:
                    lk = jnp.where(mask, lk, 0.0)
                lk_hi, lk_lo = _split2(lk)
                css.append(_dot(lk_hi, c_upto) + _dot(lk_lo, c_upto))
                csum = sumbuf[2 * hh]
                lbs.append((lb, lb + (lts[hh] - csum)))
                sumbuf[2 * hh] = csum + jnp.sum(lk, axis=1, keepdims=True)
            for hh in range(2):
                a = jnp.exp(lbs[hh][1] - css[hh])
                if masked:
                    a = jnp.where(mask, a, 0.0)
                e = a * dabuf[hh]
                e_hi, e_lo = _split2(e)
                ces.append(_dot(e_hi, c_before) + _dot(e_lo, c_before))
                abuf[hh] = a.astype(BF16)
                es.append(e)
            scores(jnp.minimum(jb + 1, last))
            for hh in range(2):
                prun = sumbuf[2 * hh + 1]
                beta = jnp.exp(lbs[hh][0])
                dz = es[hh] * (1.0 - beta) - (prun + ces[hh]) * beta
                if masked:
                    dz = jnp.where(mask, dz, 0.0)
                dzbuf[hh] = dz.astype(BF16)
                sumbuf[2 * hh + 1] = prun + jnp.sum(es[hh], axis=1, keepdims=True)

        nfull = i * per_q
        last = nfull + per_q - 1
        for buf in (dzbuf, abuf, dqbuf, sumbuf):
            buf[...] = jnp.zeros_like(buf)
        scores(0)

        def full_block(jb, carry):
            trip(jb, False)
            return carry

        lax.fori_loop(0, nfull, full_block, 0)
        for dblk in range(per_q):
            trip(nfull + dblk, True)
        products(last)
        dq_ref[...] = (jnp.where(hi_lanes, dqbuf[1], dqbuf[0]) * scale).astype(BF16)

        @pl.when(i == nq - 1)
        def _():
            dk_ref[...] = dk_acc[...].astype(BF16)
            dv_ref[...] = dv_acc[...].astype(BF16)

        pl.when((pair == npair - 1) & (i == nq - 1))(finish)

    blk = pl.BlockSpec((tq, LANES), lambda p, i: (i, p))
    full = pl.BlockSpec((t, LANES), lambda p, i: (0, p))
    return pl.pallas_call(
        body, name=name, grid=(npair, nq),
        in_specs=[blk,
                  pl.BlockSpec((t, LANES), lambda p, i: (0, npair + p)),
                  pl.BlockSpec((t, LANES), lambda p, i: (0, 2 * npair + p)),
                  pl.BlockSpec((tq, LANES), lambda p, i: (i, npair + p)),
                  blk,
                  pl.BlockSpec((2, SB_KEYS, SB_KEYS), lambda p, i: (0, 0, 0))] + [HBM_SPEC] * ns,
        out_specs=[blk, full, full, HBM_SPEC],
        out_shape=[jax.ShapeDtypeStruct((t, npair * LANES), BF16)] * 3
        + [jax.ShapeDtypeStruct((8, _direct_exchange_rows(sends), sends[0].shape[3]), sends[0].dtype)],
        scratch_shapes=[pltpu.VMEM((t, LANES), F32), pltpu.VMEM((t, LANES), F32),
                        pltpu.VMEM((2, tq, SB_KEYS), F32), pltpu.VMEM((2, tq, SB_KEYS), F32),
                        pltpu.VMEM((2, tq, SB_KEYS), BF16), pltpu.VMEM((2, tq, SB_KEYS), BF16),
                        pltpu.VMEM((2, tq, LANES), F32), pltpu.VMEM((4, tq, 1), F32)] + _direct_exchange_scratch(),
        compiler_params=_cp(dimension_semantics=("arbitrary", "arbitrary")))(pb, pb, pb, dy, ltot, cmats, *sends)


def _hgrn_consts():
    t = lax.broadcasted_iota(jnp.int32, (CHUNK, CHUNK), 0)
    s = lax.broadcasted_iota(jnp.int32, (CHUNK, CHUNK), 1)
    masks = []
    for lvl in range(N_LEVELS):
        half = CHUNK >> (lvl + 1)
        same = (t // (2 * half)) == (s // (2 * half))
        masks.append((same & (t % (2 * half) >= half) & (s % (2 * half) < half)).astype(F32))
    masks.append((t == s).astype(F32))
    prefix = (s <= t).astype(BF16)
    suffix = (s >= t).astype(BF16)
    return prefix, jnp.stack(masks), suffix


def _hgrn_gates(qr, fr, lbv):
    sg = 1.0 / (1.0 + jnp.exp(-fr))
    fval = lbv + (1.0 - lbv) * sg
    kk = (1.0 - lbv) * (1.0 / (1.0 + jnp.exp(fr)))
    sq = _sigmoid(qr)
    return sg, fval, jnp.log(fval), kk, sq, qr * sq


def _lower_bound(c_ref):
    c = c_ref[...]
    mx = jnp.max(c, axis=0, keepdims=True)
    ex = jnp.exp(c - mx)
    return ex[1:2, :] / jnp.sum(ex, axis=0, keepdims=True)


def _level_ref(b, lvl):
    half = CHUNK >> (lvl + 1)
    seg = 2 * half
    if seg >= 8:
        b3 = b.reshape(CHUNK // seg, seg, LANES)
        return jnp.broadcast_to(b3[:, half - 1:half, :], b3.shape).reshape(CHUNK, LANES)
    pos = lax.broadcasted_iota(jnp.int32, b.shape, 0) % seg
    out = b
    for p in range(seg):
        if p != half - 1:
            out = jnp.where(pos == p, pltpu.roll(b, (p - (half - 1)) % CHUNK, 0), out)
    return out


def _hgrn_levels(b, qs, kk):
    out = []
    for lvl in range(N_LEVELS):
        fac = jnp.exp(-jnp.abs(b - _level_ref(b, lvl)))
        out.append((qs * fac, kk * fac, fac, fac))
    out.append((qs, kk, None, None))
    return out


def _split2(x):
    hi = x.astype(BF16)
    return hi, (x - hi.astype(F32)).astype(BF16)


def _hgrn_fwd(pc, c_lb, out_norm, *, name, tc=1024):
    t = pc.shape[0]
    nh = pc.shape[1] // 4 // LANES
    tc = min(tc, t)
    nch = tc // CHUNK
    cum_all, masks, _ = _hgrn_consts()

    def body(q_ref, f_ref, i_ref, g_ref, lb_ref, on_ref, cum_ref, m_ref, y_ref, o_ref, st_ref, state):
        @pl.when(pl.program_id(1) == 0)
        def _():
            state[...] = jnp.zeros_like(state)

        lbv = _lower_bound(lb_ref)
        onv = on_ref[...]

        def chunk(c, carry):
            rows = pl.ds(pl.multiple_of(c * CHUNK, CHUNK), CHUNK)
            for hh in range(HGRN_HEADS):
                lanes = slice(hh * LANES, (hh + 1) * LANES)
                _, _, g, kk, _, qs = _hgrn_gates(q_ref[rows, lanes], f_ref[rows, lanes], lbv[:, lanes])
                vb = i_ref[rows, lanes].astype(BF16)
                b = _dot_exact_lhs(cum_ref[...], g)
                scores = jnp.zeros((CHUNK, CHUNK), F32)
                for lvl, (ql, kl, _, _) in enumerate(_hgrn_levels(b, qs, kk)):
                    scores = scores + _dot(ql.astype(BF16), kl.astype(BF16), 1, 1) * m_ref[lvl]
                st = state[hh]
                st_ref[hh, c] = st
                o = _dot(scores.astype(BF16), vb) + _dot((qs * jnp.exp(b)).astype(BF16), st.astype(BF16), 1, 1)
                blast = b[CHUNK - 1:CHUNK, :]
                kdec = (kk * jnp.exp(blast - b)).astype(BF16)
                state[hh] = st * jnp.exp(blast) + _dot(vb, kdec, 0, 0)
                o_ref[rows, lanes] = o
                rstd = lax.rsqrt(jnp.mean(o * o, axis=-1, keepdims=True) + RMS_EPS)
                gate = g_ref[rows, lanes]
                y_ref[rows, lanes] = (o * rstd * onv * (gate * _sigmoid(gate))).astype(BF16)
            return carry

        lax.fori_loop(0, nch, chunk, 0, unroll=2)

    hw = HGRN_HEADS * LANES

    def col(off):
        return pl.BlockSpec((tc, hw), lambda h, i: (i, off // HGRN_HEADS + h))

    osp = pl.BlockSpec((tc, hw), lambda h, i: (i, h))
    return pl.pallas_call(
        body, name=name, grid=(nh // HGRN_HEADS, t // tc),
        in_specs=[col(0), col(nh), col(2 * nh), col(3 * nh),
                  pl.BlockSpec((2, hw), lambda h, i: (0, h)),
                  pl.BlockSpec((1, LANES), lambda h, i: (0, 0)),
                  pl.BlockSpec(cum_all.shape, lambda h, i: (0, 0)),
                  pl.BlockSpec(masks.shape, lambda h, i: (0, 0, 0))],
        out_specs=[osp, osp, pl.BlockSpec((HGRN_HEADS, nch, LANES, LANES), lambda h, i: (h, i, 0, 0))],
        out_shape=[jax.ShapeDtypeStruct((t, nh * LANES), BF16), jax.ShapeDtypeStruct((t, nh * LANES), F32),
                   jax.ShapeDtypeStruct((nh, t // CHUNK, LANES, LANES), F32)],
        scratch_shapes=[pltpu.VMEM((HGRN_HEADS, LANES, LANES), F32)],
        compiler_params=_cp())(pc, pc, pc, pc, c_lb, out_norm, cum_all, masks)


def _hgrn_bwd(pc, o_saved, states, dy, c_lb, out_norm, send, *, name, tc=1024):
    t = pc.shape[0]
    nh = pc.shape[1] // 4 // LANES
    tc = min(tc, t)
    nch = tc // CHUNK
    nt = t // tc
    cum_all, masks, suffix = _hgrn_consts()
    ngroup = nh // HGRN_HEADS
    sends = list(send)
    ns = len(sends)

    def body(q_ref, f_ref, i_ref, g_ref, o_ref, st_ref, dy_ref, lb_ref, on_ref, cum_ref, m_ref, suf_ref, *rest):
        send_refs = rest[:ns]
        dq_ref, df_ref, di_ref, dg_ref, dlb_ref, don_ref, parts_ref, dstate = rest[ns:ns + 8]
        start, finish = _direct_exchange_phases(send_refs, parts_ref, *rest[ns + 8:])
        pl.when((pl.program_id(0) == 0) & (pl.program_id(1) == 0))(start)

        @pl.when(pl.program_id(1) == 0)
        def _():
            dstate[...] = jnp.zeros_like(dstate)
            dlb_ref[...] = jnp.zeros_like(dlb_ref)
            don_ref[...] = jnp.zeros_like(don_ref)

        lbv = _lower_bound(lb_ref)
        onv = on_ref[...]

        def head(hh, c, rows):
            lanes = slice(hh * LANES, (hh + 1) * LANES)
            qr = q_ref[rows, lanes]
            sg, fval, g, kk, sq, qs = _hgrn_gates(qr, f_ref[rows, lanes], lbv[:, lanes])
            vb = i_ref[rows, lanes].astype(BF16)
            o = o_ref[rows, lanes]
            gate = g_ref[rows, lanes]
            sgt = _sigmoid(gate)
            rstd = lax.rsqrt(jnp.mean(o * o, axis=-1, keepdims=True) + RMS_EPS)
            ohat = o * rstd
            dyv = dy_ref[rows, lanes]
            don = dyv * (gate * sgt)
            dg_ref[rows, lanes] = (dyv * ohat * onv * (sgt * (1.0 + gate * (1.0 - sgt)))).astype(BF16)
            don_ref[:, lanes] += jnp.sum(don * ohat, axis=0, keepdims=True)
            dxhat = don * onv
            dob = (rstd * (dxhat - ohat * jnp.mean(dxhat * ohat, axis=-1, keepdims=True))).astype(BF16)
            b = _dot_exact_lhs(cum_ref[...], g)
            blast = b[CHUNK - 1:CHUNK, :]
            eb = jnp.exp(b)
            edec = jnp.exp(blast - b)
            st32 = st_ref[hh, c]
            st = st32.astype(BF16)
            dst = dstate[hh]
            dstb = dst.astype(BF16)
            da = _dot(dob, vb, 1, 1)
            levels = _hgrn_levels(b, qs, kk)
            scores = jnp.zeros((CHUNK, CHUNK), F32)
            dq = eb * _dot(dob, st)
            dk_inter = edec * _dot(vb, dstb)
            dk = dk_inter
            for lvl, (ql, kl, eq, ek) in enumerate(levels):
                mk = m_ref[lvl]
                (qh, qlo), (kh, klo) = _split2(ql), _split2(kl)
                scores = scores + _dot(qh, kh, 1, 1) * mk
                dal = (da * mk).astype(BF16)
                dql = _dot(dal, kh) + _dot(dal, klo)
                dkl = _dot(dal, qh, 0, 0) + _dot(dal, qlo, 0, 0)
                dq = dq + (dql if eq is None else dql * eq)
                dk = dk + (dkl if ek is None else dkl * ek)
            kdec = (kk * edec).astype(BF16)
            dv = _dot(scores.astype(BF16), dob, 0, 0) + _dot(kdec, dstb, 1, 1)
            dstate[hh] = dst * jnp.exp(blast) + _dot(dob, (qs * eb).astype(BF16), 0, 0)
            db = qs * dq - kk * dk
            last = jnp.sum(kk * dk_inter, axis=0, keepdims=True) + jnp.exp(blast) * jnp.sum(dst * st32, axis=0, keepdims=True)
            dgl = _dot_exact_lhs(suf_ref[...], db) + last
            dfv = dgl / fval - dk
            df_ref[rows, lanes] = (dfv * (1.0 - lbv[:, lanes]) * sg * (1.0 - sg)).astype(BF16)
            dlb_ref[:, lanes] += jnp.sum(dfv * (1.0 - sg), axis=0, keepdims=True)
            dq_ref[rows, lanes] = (dq * (sq * (1.0 + qr * (1.0 - sq)))).astype(BF16)
            di_ref[rows, lanes] = dv.astype(BF16)

        def chunk(n, carry):
            c = nch - 1 - n
            rows = pl.ds(pl.multiple_of(c * CHUNK, CHUNK), CHUNK)
            for hh in range(HGRN_HEADS):
                head(hh, c, rows)
            return carry

        lax.fori_loop(0, nch, chunk, 0, unroll=2)
        pl.when((pl.program_id(0) == ngroup - 1) & (pl.program_id(1) == nt - 1))(finish)

    hw = HGRN_HEADS * LANES

    def col(off):
        return pl.BlockSpec((tc, hw), lambda h, i: (nt - 1 - i, off // HGRN_HEADS + h))

    osp = pl.BlockSpec((tc, hw), lambda h, i: (nt - 1 - i, h))
    vec = pl.BlockSpec((1, hw), lambda h, i: (0, h))
    return pl.pallas_call(
        body, name=name, grid=(nh // HGRN_HEADS, nt),
        in_specs=[col(0), col(nh), col(2 * nh), col(3 * nh), osp,
                  pl.BlockSpec((HGRN_HEADS, nch, LANES, LANES), lambda h, i: (h, nt - 1 - i, 0, 0)),
                  osp,
                  pl.BlockSpec((2, hw), lambda h, i: (0, h)),
                  pl.BlockSpec((1, LANES), lambda h, i: (0, 0)),
                  pl.BlockSpec(cum_all.shape, lambda h, i: (0, 0)),
                  pl.BlockSpec(masks.shape, lambda h, i: (0, 0, 0)),
                  pl.BlockSpec(suffix.shape, lambda h, i: (0, 0))] + [HBM_SPEC] * ns,
        out_specs=[osp, osp, osp, osp, vec, vec, HBM_SPEC],
        out_shape=[jax.ShapeDtypeStruct((t, nh * LANES), BF16)] * 4 + [jax.ShapeDtypeStruct((1, nh * LANES), F32)] * 2
        + [jax.ShapeDtypeStruct((8, _direct_exchange_rows(sends), sends[0].shape[3]), sends[0].dtype)],
        scratch_shapes=[pltpu.VMEM((HGRN_HEADS, LANES, LANES), F32)] + _direct_exchange_scratch(),
        compiler_params=_cp(dimension_semantics=("arbitrary", "arbitrary")))(
            pc, pc, pc, pc, o_saved, states, dy, c_lb, out_norm, cum_all, masks, suffix, *sends)


HBM_SPEC = pl.BlockSpec(memory_space=pltpu.HBM)


def _gather_scratch():
    return [pltpu.SemaphoreType.DMA((7,)), pltpu.SemaphoreType.DMA((7,)), pltpu.SemaphoreType.DMA]


def _gather_phases(x_ref, out_refs, seg_rows, send_sems, recv_sems, local_sem):
    x, y, c = lax.axis_index("x"), lax.axis_index("y"), lax.axis_index("c")
    me, sibling = (x, y, c), (x, y, 1 - c)
    chips = [(1 - x, y), (x, 1 - y), (1 - x, 1 - y)]
    offs = [sum(seg_rows[:s]) for s in range(len(seg_rows))]
    assert sum(seg_rows) == x_ref.shape[0]

    def index(px, py, pc):
        return 4 * px + 2 * py + pc

    def copies(k, block, to, own):
        return [pltpu.make_async_remote_copy(
            src_ref=x_ref.at[pl.ds(offs[s], n)] if own else out_refs[s].at[index(*block)],
            dst_ref=out_refs[s].at[index(*block)],
            send_sem=send_sems.at[k], recv_sem=recv_sems.at[k], device_id=to, device_id_type=MESH)
            for s, n in enumerate(seg_rows)]

    def all_bytes(k):
        return pltpu.make_async_remote_copy(src_ref=x_ref, dst_ref=x_ref, send_sem=send_sems.at[k],
                                            recv_sem=recv_sems.at[k], device_id=me, device_id_type=MESH)

    mine = [pltpu.make_async_copy(x_ref.at[pl.ds(offs[s], n)], out_refs[s].at[index(*me)], local_sem)
            for s, n in enumerate(seg_rows)]
    first = copies(0, me, sibling, True)
    for j, chip in enumerate(chips):
        first += copies(1 + j, me, (*chip, c), True)

    def start():
        for cp in mine + first:
            cp.start()

    def forward():
        for j, chip in enumerate(chips):
            all_bytes(1 + j).wait_recv()
            for cp in copies(4 + j, (*chip, c), sibling, False):
                cp.start()

    def finish():
        all_bytes(0).wait_recv()
        for j in range(3):
            all_bytes(4 + j).wait_recv()
        for k in range(7):
            all_bytes(k).wait_send()
        pltpu.make_async_copy(x_ref, x_ref, local_sem).wait()

    return start, forward, finish


def _all_gather(xs, seg_rows=None, *, name):
    segs = [xs.shape[0]] if seg_rows is None else list(seg_rows)

    def body(x_ref, *rest):
        start, forward, finish = _gather_phases(x_ref, rest[:len(segs)], segs, *rest[len(segs):])
        start()
        forward()
        finish()

    outs = pl.pallas_call(
        body, name=name, in_specs=[HBM_SPEC], out_specs=[HBM_SPEC] * len(segs),
        out_shape=[jax.ShapeDtypeStruct((8, n, xs.shape[1]), xs.dtype) for n in segs],
        scratch_shapes=_gather_scratch())(xs)
    return outs[0] if seg_rows is None else outs


def _sibling_exchange(s, *, name):
    def body(s_ref, rb_ref, send_sem, recv_sem):
        x, y, c = lax.axis_index("x"), lax.axis_index("y"), lax.axis_index("c")
        cp = pltpu.make_async_remote_copy(
            src_ref=s_ref.at[:, 1 - c], dst_ref=rb_ref, send_sem=send_sem, recv_sem=recv_sem,
            device_id=(x, y, 1 - c), device_id_type=MESH)
        cp.start()
        cp.wait()

    return pl.pallas_call(
        body, name=name, in_specs=[HBM_SPEC], out_specs=HBM_SPEC,
        out_shape=jax.ShapeDtypeStruct(s.shape[:1] + s.shape[2:], s.dtype),
        scratch_shapes=[pltpu.SemaphoreType.DMA, pltpu.SemaphoreType.DMA])(s)


def _row_tile(n, cap=1024):
    return max(b for b in range(16, cap + 1, 16) if n % b == 0)


def _pair_add(s, rb, core, *, name):
    nchip, _, r, c = s.shape
    tb = _row_tile(r)

    def body(core_ref, a_ref, b_ref, o_ref):
        o_ref[...] = (a_ref[...].astype(F32) + b_ref[...].astype(F32)).astype(BF16)

    blk = pl.BlockSpec((None, tb, c), lambda ch, i, cr: (ch, i, 0))
    return pl.pallas_call(
        body, name=name,
        grid_spec=pltpu.PrefetchScalarGridSpec(
            num_scalar_prefetch=1, grid=(nchip, r // tb),
            in_specs=[pl.BlockSpec((None, None, tb, c), lambda ch, i, cr: (ch, cr[0], i, 0)), blk],
            out_specs=blk),
        out_shape=jax.ShapeDtypeStruct((nchip, r, c), BF16), compiler_params=_cp())(core, s, rb)


def _chip_exchange_scratch():
    return [pltpu.SemaphoreType.DMA((3,)), pltpu.SemaphoreType.DMA((3,)), pltpu.SemaphoreType.DMA]


def _chip_exchange_phases(p_ref, out_ref, send_sems, recv_sems, local_sem):
    x, y, c = lax.axis_index("x"), lax.axis_index("y"), lax.axis_index("c")
    mine = 2 * x + y
    own = pltpu.make_async_copy(p_ref.at[mine], out_ref.at[mine], local_sem)
    copies = [pltpu.make_async_remote_copy(
        src_ref=p_ref.at[2 * tx + ty], dst_ref=out_ref.at[mine],
        send_sem=send_sems.at[k], recv_sem=recv_sems.at[k], device_id=(tx, ty, c), device_id_type=MESH)
        for k, (tx, ty) in enumerate([(1 - x, y), (x, 1 - y), (1 - x, 1 - y)])]

    def start():
        own.start()
        for cp in copies:
            cp.start()

    def finish():
        for cp in copies:
            cp.wait()
        own.wait()

    return start, finish


def _direct_exchange_scratch():
    return [pltpu.SemaphoreType.DMA((7,)), pltpu.SemaphoreType.DMA((7,)), pltpu.SemaphoreType.DMA]


def _direct_exchange_rows(sends):
    return sum(s.shape[2] for s in sends)


def _direct_exchange_phases(s_refs, out_ref, send_sems, recv_sems, local_sem):
    x, y, c = lax.axis_index("x"), lax.axis_index("y"), lax.axis_index("c")
    me = 4 * x + 2 * y + c
    offs, off = [], 0
    for s in s_refs:
        offs.append(off)
        off += s.shape[2]

    def slot(p):
        return out_ref.at[me, pl.ds(offs[p], s_refs[p].shape[2])]

    own = [pltpu.make_async_copy(s.at[2 * x + y, c], slot(p), local_sem) for p, s in enumerate(s_refs)]
    flips = [(fx, fy, fc) for fx in (0, 1) for fy in (0, 1) for fc in (0, 1) if (fx, fy, fc) != (0, 0, 0)]
    copies = []
    for k, (fx, fy, fc) in enumerate(flips):
        tx, ty, tc = (1 - x if fx else x), (1 - y if fy else y), (1 - c if fc else c)
        copies += [pltpu.make_async_remote_copy(
            src_ref=s.at[2 * tx + ty, tc], dst_ref=slot(p),
            send_sem=send_sems.at[k], recv_sem=recv_sems.at[k], device_id=(tx, ty, tc), device_id_type=MESH)
            for p, s in enumerate(s_refs)]

    def start():
        for cp in own + copies:
            cp.start()

    def finish():
        whole = out_ref.at[me]
        for k in range(len(flips)):
            pltpu.make_async_remote_copy(src_ref=whole, dst_ref=whole, send_sem=send_sems.at[k],
                                         recv_sem=recv_sems.at[k], device_id=(x, y, c), device_id_type=MESH).wait()
        pltpu.make_async_copy(whole, whole, local_sem).wait()

    return start, finish


def _adamw_math(w, g, m, v):
    m2 = ADAM_B1 * m + (1.0 - ADAM_B1) * g
    v2 = ADAM_B2 * v + (1.0 - ADAM_B2) * (g * g)
    m_hat = m2 / (1.0 - ADAM_B1 ** ADAM_STEP)
    v_hat = v2 / (1.0 - ADAM_B2 ** ADAM_STEP)
    return -ADAM_LR * (m_hat / (jnp.sqrt(v_hat) + ADAM_EPS) + ADAM_WD * w), m2, v2


def _adamw_shard(parts, g_off, w, m, v, layer, prev, *, name):
    _, r, c = w.shape
    npart = parts.shape[0]
    tb = next(b for b in range(min(r, 512), 0, -16) if r % b == 0 and g_off % b == 0)

    def body(*refs):
        w_ref, m_ref, v_ref = refs[npart:npart + 3]
        g_out, d_out, m_out, v_out = refs[-4:]
        g = refs[0][...].astype(F32)
        for p_ref in refs[1:npart]:
            g = g + p_ref[...].astype(F32)
        d, m2, v2 = _adamw_math(w_ref[...], g, m_ref[...], v_ref[...])
        g_out[...] = g
        d_out[...] = d
        m_out[...] = m2
        v_out[...] = v2

    def part(ch):
        return pl.BlockSpec((None, tb, c), lambda i: (ch, g_off // tb + i, 0))

    blk = pl.BlockSpec((None, tb, c), lambda i: (layer, i, 0))
    prev = list(prev) if prev is not None else []
    return pl.pallas_call(
        body, name=name, grid=(r // tb,),
        in_specs=[part(ch) for ch in range(npart)] + [blk, blk, blk] + [pl.BlockSpec(memory_space=pl.ANY)] * len(prev),
        out_specs=[blk] * 4, out_shape=[jax.ShapeDtypeStruct(w.shape, F32)] * 4,
        input_output_aliases={npart + 3 + k: k for k in range(len(prev))},
        compiler_params=_cp())(*([parts] * npart), w, m, v, *prev)


SLOT = 8
SMALL_ROWS = 6 * SLOT
ROW_LB = 4 * SLOT


def _small_update(gath, w, m, v, *, name):
    def body(g_ref, w_ref, m_ref, v_ref, g_out, d_out, m_out, v_out):
        tot = g_ref[0]
        for k in range(1, 8):
            tot = tot + g_ref[k]
        wv = w_ref[...]
        c0, c1 = wv[ROW_LB:ROW_LB + 1, :], wv[ROW_LB + 1:ROW_LB + 2, :]
        mx = jnp.maximum(c0, c1)
        e0, e1 = jnp.exp(c0 - mx), jnp.exp(c1 - mx)
        lb = e1 / (e0 + e1)
        gl = tot[ROW_LB:ROW_LB + 1, :] * lb * (1.0 - lb)
        row = lax.broadcasted_iota(jnp.int32, tot.shape, 0)
        g = jnp.where(row == ROW_LB, -gl, jnp.where(row == ROW_LB + 1, gl, tot))
        d, m2, v2 = _adamw_math(wv, g, m_ref[...], v_ref[...])
        g_out[...] = g
        d_out[...] = d
        m_out[...] = m2
        v_out[...] = v2

    return pl.pallas_call(
        body, name=name, out_shape=[jax.ShapeDtypeStruct(w.shape, F32)] * 4, compiler_params=_cp())(gath, w, m, v)


D_MODEL = 1024


def _ffn_fwd(h, gain, wg, wu, wd, tag):
    xn, gg, uu, act = _norm_gate_up(h, gain, wg, wu, name=f"{tag}_gate_up")
    out = _mm([(act, wd)], residual=h, alpha=MACARON, tn=1024, name=f"{tag}_down")
    return out, (h, xn, gg, uu, act)


def _ffn_input_bwd(dg, du, wg, wu, x, gain, dres, chip_part, *, name, scale, tm=256):
    t, d = x.shape
    f = wg.shape[0]
    tm = min(tm, t)
    nt = t // tm
    fused = chip_part is not None

    def body(dg_ref, du_ref, wg_ref, wu_ref, x_ref, g_ref, dres_ref, *rest):
        if fused:
            part_ref, dx_ref, dxb_ref, dgain_ref, parts_ref = rest[:5]
            start, finish = _chip_exchange_phases(part_ref, parts_ref, *rest[5:])
            pl.when(pl.program_id(0) == 0)(start)
        else:
            dx_ref, dxb_ref, dgain_ref = rest
        dxn_v = _dot(dg_ref[...], wg_ref[...]) + _dot(du_ref[...], wu_ref[...])
        xv = x_ref[...]
        rstd = lax.rsqrt(jnp.mean(xv * xv, axis=-1, keepdims=True) + RMS_EPS)
        xhat = xv * rstd
        dxhat = dxn_v * g_ref[...]
        dx = dres_ref[...] + rstd * (dxhat - xhat * jnp.mean(dxhat * xhat, axis=-1, keepdims=True))
        dx_ref[...] = dx
        dxb_ref[...] = (dx * scale).astype(BF16)

        @pl.when(pl.program_id(0) == 0)
        def _():
            dgain_ref[...] = jnp.zeros_like(dgain_ref)

        dgain_ref[...] += jnp.sum(dxn_v * xhat, axis=0, keepdims=True)
        if fused:
            pl.when(pl.program_id(0) == nt - 1)(finish)

    wide = pl.BlockSpec((tm, f), lambda i: (i, 0))
    wsp = pl.BlockSpec((f, d), lambda i: (0, 0))
    row = pl.BlockSpec((tm, d), lambda i: (i, 0))
    vec = pl.BlockSpec((1, d), lambda i: (0, 0))
    args = [dg, du, wg, wu, x, gain, dres] + ([chip_part] if fused else [])
    return pl.pallas_call(
        body, name=name, grid=(nt,),
        in_specs=[wide, wide, wsp, wsp, row, vec, row] + ([HBM_SPEC] if fused else []),
        out_specs=[row, row, vec] + ([HBM_SPEC] if fused else []),
        out_shape=[jax.ShapeDtypeStruct((t, d), F32), jax.ShapeDtypeStruct((t, d), BF16), jax.ShapeDtypeStruct((1, d), F32)]
        + ([jax.ShapeDtypeStruct(chip_part.shape, chip_part.dtype)] if fused else []),
        scratch_shapes=_chip_exchange_scratch() if fused else [],
        compiler_params=_cp(dimension_semantics=("arbitrary",)))(*args)


def _ffn_bwd(dout, dout_half, saved, gain, wg, wu, wd, tag, next_scale, exchanges=None):
    h, xn, gg, uu, act = saved
    early_chip_part, send_after_dwd, chip_part_after_dwgu = exchanges if exchanges is not None else (None, None, None)
    dg, du, *early_parts = _swiglu_bwd(dout_half, wd, gg, uu, early_chip_part, tm=256, tf=wd.shape[0],
                                       name=f"{tag}_dact")
    dwd = _mm([(act, dout_half)], ta=True, tm=256, tn=1024, out_dtype=BF16, name=f"{tag}_dwd")
    send = send_after_dwd(dwd) if exchanges is not None else None
    dwg, dwu, *mid_parts = _mm_shared_rhs([dg, du], xn, tm=256, send=send, name=f"{tag}_dwgu")
    chip_part = chip_part_after_dwgu(dwg, dwu) if exchanges is not None else None
    dh, dh_b, dgain, *parts = _ffn_input_bwd(dg, du, wg, wu, h, gain, dout, chip_part, scale=next_scale,
                                             name=f"{tag}_input_bwd")
    return dh, dh_b, dwg, dwu, dwd, dgain, (early_parts + mid_parts + parts)


def kernel(x, ffn_pre_norm, ffn_pre_w_gate, ffn_pre_w_up, ffn_pre_w_down, mix_norm, ffn_post_norm, ffn_post_w_gate, ffn_post_w_up, ffn_post_w_down, ab_w_in, ab_conv_w, ab_w_out, c_w_in, c_lower_bounds, c_out_norm, c_w_out, final_norm, loss_target, m_ffn_pre_norm, m_ffn_pre_w_gate, m_ffn_pre_w_up, m_ffn_pre_w_down, m_mix_norm, m_ffn_post_norm, m_ffn_post_w_gate, m_ffn_post_w_up, m_ffn_post_w_down, m_ab_w_in, m_ab_conv_w, m_ab_w_out, m_c_w_in, m_c_lower_bounds, m_c_out_norm, m_c_w_out, m_final_norm, v_ffn_pre_norm, v_ffn_pre_w_gate, v_ffn_pre_w_up, v_ffn_pre_w_down, v_mix_norm, v_ffn_post_norm, v_ffn_post_w_gate, v_ffn_post_w_up, v_ffn_post_w_down, v_ab_w_in, v_ab_conv_w, v_ab_w_out, v_c_w_in, v_c_lower_bounds, v_c_out_norm, v_c_w_out, v_final_norm):
    d = D_MODEL
    h0 = x[0]
    target = loss_target[0]
    core = lax.axis_index("c").astype(jnp.int32).reshape(1)

    big = [("pre_g", ffn_pre_w_gate, m_ffn_pre_w_gate, v_ffn_pre_w_gate),
           ("pre_u", ffn_pre_w_up, m_ffn_pre_w_up, v_ffn_pre_w_up),
           ("pre_d", ffn_pre_w_down, m_ffn_pre_w_down, v_ffn_pre_w_down),
           ("post_g", ffn_post_w_gate, m_ffn_post_w_gate, v_ffn_post_w_gate),
           ("post_u", ffn_post_w_up, m_ffn_post_w_up, v_ffn_post_w_up),
           ("post_d", ffn_post_w_down, m_ffn_post_w_down, v_ffn_post_w_down),
           ("ab_in", ab_w_in, m_ab_w_in, v_ab_w_in),
           ("ab_out", ab_w_out, m_ab_w_out, v_ab_w_out),
           ("c_in", c_w_in, m_c_w_in, v_c_w_in),
           ("c_out", c_w_out, m_c_w_out, v_c_w_out)]
    by_tag = {tag: (w, m, v) for tag, w, m, v in big}

    def layer_rows(tag):
        w = by_tag[tag][0]
        return w.size // d // w.shape[0]

    def layout(items):
        offs, off = {}, 0
        for item in items:
            offs[item] = off
            off += layer_rows(item[0])
        return offs, off

    ffn = [f"{pos}_{kind}" for pos in ("pre", "post") for kind in "gud"]
    first_items = [("pre_g", 0), ("pre_u", 0)]
    early_items = [("pre_d", 0), ("ab_in", 0)]
    late_items = ([("pre_g", 1), ("pre_u", 1), ("pre_d", 1)] + [(f"post_{kind}", l) for l in (0, 1) for kind in "gud"]
                  + [("ab_out", 0), ("c_in", 0), ("c_out", 0)])
    grad_items = {"A0": [(f"post_{kind}", 1) for kind in "gud"] + [("c_out", 0)],
                  "A1": ([(f"pre_{kind}", 1) for kind in "gud"] + [(f"post_{kind}", 0) for kind in "gud"]
                         + [("c_in", 0), ("ab_out", 0)]),
                  "C": [("ab_in", 0)], "B0": [("pre_d", 0)], "B1": [("pre_g", 0), ("pre_u", 0)]}
    grad_offs = {k: layout(items)[0] for k, items in grad_items.items()}
    grad_conv_row = layout(grad_items["C"])[1]

    def conv_rows(a, split):
        flat = a.reshape(-1)
        if split:
            hi = flat.astype(BF16)
            flat = jnp.concatenate([hi, (flat - hi.astype(F32)).astype(BF16)])
        return jnp.zeros((16, d), flat.dtype).at[0, :flat.shape[0]].set(flat)

    nconv = ab_conv_w.size
    col_sharded = {"pre_g", "pre_u", "post_g", "post_u", "ab_in", "c_in"}

    def pack_rows(item):
        tag, layer = item
        a = by_tag[tag][0][layer]
        return (a.T if tag in col_sharded else a).reshape(-1, d).astype(BF16)

    first_pack = jnp.concatenate([pack_rows(item) for item in first_items], axis=0)
    early_pack = jnp.concatenate([pack_rows(item) for item in early_items] + [conv_rows(ab_conv_w, True)], axis=0)
    late_pack = jnp.concatenate([pack_rows(item) for item in late_items], axis=0)
    first_w = _all_gather(first_pack, [layer_rows(tag) for tag, _ in first_items], name="gather_first_weights")
    full = {item: g.reshape(-1, d) for item, g in zip(first_items, first_w)}

    xn0, gg0, uu0, act0, *early_w = _norm_gate_up(
        h0, ffn_pre_norm[0:1], full["pre_g", 0], full["pre_u", 0], name="l0pre_gate_up_gather_early_weights",
        pack=early_pack, seg_rows=[layer_rows(tag) for tag, _ in early_items] + [16])
    full.update({item: g.reshape(-1, d) for item, g in zip(early_items, early_w)})
    ffn_w = {("pre", 0): tuple(full[f"pre_{kind}", 0] for kind in "gud")}
    w_ab_in = full["ab_in", 0]
    cg = early_w[-1][:, 0, :2 * nconv].astype(F32)
    conv_w = (cg[:, :nconv] + cg[:, nconv:]).reshape(8, 3, -1).transpose(1, 0, 2).reshape(3, -1)
    aw = w_ab_in.shape[0] // 6
    h1 = _mm([(act0, full["pre_d", 0])], residual=h0, alpha=MACARON, tn=1024, name="l0pre_down")
    s_pre0 = (h0, xn0, gg0, uu0, act0)
    hn0, pa, pb = _norm_proj(h1, mix_norm[0:1], w_ab_in, (F32, BF16), tm=512, name="ab_norm_proj")
    ya = _conv_fwd(pa, conv_w, name="conv_fwd")
    yb, ltot, *late_w = _attn_fwd(pb, late_pack, [layer_rows(tag) for tag, _ in late_items],
                                  name="attn_fwd_gather_late_weights")
    full.update({item: g.reshape(-1, d) for item, g in zip(late_items, late_w)})
    for pos, layer in (("post", 0), ("pre", 1), ("post", 1)):
        ffn_w[pos, layer] = tuple(full[f"{pos}_{kind}", layer] for kind in "gud")
    w_ab_out, w_c_in, w_c_out = full["ab_out", 0], full["c_in", 0], full["c_out", 0]
    h2 = _mm([(ya, w_ab_out[:aw]), (yb, w_ab_out[aw:])], residual=h1, tn=1024, name="ab_out")
    h3, s_post0 = _ffn_fwd(h2, ffn_post_norm[0:1], *ffn_w["post", 0], "l0post")
    h4, s_pre1 = _ffn_fwd(h3, ffn_pre_norm[1:2], *ffn_w["pre", 1], "l1pre")
    hn1, pc = _norm_proj(h4, mix_norm[1:2], w_c_in, (F32,), tm=256, name="c_norm_proj")
    yc, o_saved, states = _hgrn_fwd(pc, c_lower_bounds, c_out_norm, name="hgrn_fwd")
    h5 = _mm([(yc, w_c_out)], residual=h4, tn=1024, name="c_out")
    h6, s_post1 = _ffn_fwd(h5, ffn_post_norm[1:2], *ffn_w["post", 1], "l1post")
    dh6, dh6_b, d_final, loss_vec = _loss_head(h6, final_norm.reshape(1, d), target, name="loss_head")

    gw = {}

    def grad_send(key, extra=()):
        return [g.reshape(4, 2, -1, d) for g in [gw[item] for item in grad_items[key]] + list(extra)]

    def chip_partials(key, extra=()):
        send = jnp.concatenate(grad_send(key, extra), axis=2)
        from_sibling = _sibling_exchange(send, name=f"grad{key}_sibling_exchange")
        return _pair_add(send, from_sibling, core, name=f"grad{key}_pair_add")

    dh5, dh5_b, gw["post_g", 1], gw["post_u", 1], gw["post_d", 1], d_post1, *_ = _ffn_bwd(
        dh6, dh6_b, s_post1, ffn_post_norm[1:2], *ffn_w["post", 1], "l1post", 1.0)
    dyc = _mm([(dh5_b, w_c_out)], tb=True, tn=1024, name="c_out_dy")
    g_c_out = _mm([(yc, dh5_b)], ta=True, tm=256, tn=1024, out_dtype=BF16, name="c_out_dw")
    gw["c_out", 0] = g_c_out
    dcq, dcf, dci, dcg, dlb, d_onorm, parts_a0 = _hgrn_bwd(pc, o_saved, states, dyc, c_lower_bounds, c_out_norm,
                                                           grad_send("A0"), name="hgrn_bwd_exchange_grads_a0")
    dparts = [dcq, dcf, dci, dcg]
    g_c_in = jnp.concatenate(_mm_shared_rhs(dparts, hn1, tm=256, name="c_in_dw"), axis=0)
    cw = w_c_in.shape[0] // 4
    dhn1 = _mm([(dp, w_c_in[i * cw:(i + 1) * cw]) for i, dp in enumerate(dparts)], tm=512, tn=1024, name="c_in_dx")
    dh4, dh4_b, d_mix1 = _rmsnorm_bwd(h4, mix_norm[1:2], dhn1, dh5, scale=MACARON, name="l1_mix_norm_bwd")
    dh3, dh3_b, gw["pre_g", 1], gw["pre_u", 1], gw["pre_d", 1], d_pre1, *_ = _ffn_bwd(
        dh4, dh4_b, s_pre1, ffn_pre_norm[1:2], *ffn_w["pre", 1], "l1pre", MACARON)
    dh2, dh2_b, gw["post_g", 0], gw["post_u", 0], gw["post_d", 0], d_post0, *_ = _ffn_bwd(
        dh3, dh3_b, s_post0, ffn_post_norm[0:1], *ffn_w["post", 0], "l0post", 1.0)
    dyab = _mm([(dh2_b, w_ab_out)], tb=True, tn=1024, name="ab_out_dy")
    g_ab_out = jnp.concatenate(_mm_shared_rhs([ya, yb], dh2_b, tm=256, name="ab_out_dw"), axis=0)
    dab, dac, dax, g_conv = _conv_bwd(pa, dyab, conv_w, name="conv_bwd")

    gw["c_in", 0], gw["ab_out", 0] = g_c_in, g_ab_out
    dq, dk, dv, parts_a1 = _attn_bwd(pb, dyab, ltot, grad_send("A1"), name="attn_bwd_exchange_grads_a1")
    dparts = [dab, dac, dax, dq, dk, dv]
    g_ab_in = jnp.concatenate(_mm_shared_rhs(dparts, hn0, tm=128, name="ab_in_dw"), axis=0)
    dhn0 = _mm([(dp, w_ab_in[i * aw:(i + 1) * aw]) for i, dp in enumerate(dparts)], tm=512, tn=1024, name="ab_in_dx")
    dh1, dh1_b, d_mix0 = _rmsnorm_bwd(h1, mix_norm[0:1], dhn0, dh2, scale=MACARON, name="l0_mix_norm_bwd")
    gw["ab_in", 0] = g_ab_in
    gconv_own = g_conv.reshape(3, 8, -1).transpose(1, 0, 2).reshape(8, -1)
    conv_piece = jnp.zeros((8, 16, d), F32).at[:, 0, :nconv].set(gconv_own).astype(BF16)

    def send_b0(dwd):
        gw["pre_d", 0] = dwd
        return grad_send("B0")

    def chip_part_b1(dwg, dwu):
        gw["pre_g", 0], gw["pre_u", 0] = dwg, dwu
        return chip_partials("B1")

    dh0, _, _, _, _, d_pre0, (parts_c, parts_b0, parts_b1) = _ffn_bwd(
        dh1, dh1_b, s_pre0, ffn_pre_norm[0:1], *ffn_w["pre", 0], "l0pre", 1.0,
        (chip_partials("C", [conv_piece]), send_b0, chip_part_b1))

    parts = {"A0": parts_a0, "A1": parts_a1, "B0": parts_b0, "B1": parts_b1, "C": parts_c}
    upd = {}
    for tag, w, m, v in big:
        view = (lambda a: jnp.swapaxes(a, 1, 2)) if tag in col_sharded else (lambda a: a)
        where = {layer: (key, grad_offs[key][tag, layer])
                 for key in grad_items for t2, layer in grad_items[key] if t2 == tag}
        res = None
        for layer in sorted(where):
            key, off = where[layer]
            res = _adamw_shard(parts[key], off, view(w), view(m), view(v), layer, res, name=f"adamw_{tag}{layer}")
        upd[tag] = [view(a) for a in res]
    res = _adamw_shard(parts["C"], grad_conv_row, *(conv_rows(a, False)[None] for a in (ab_conv_w, m_ab_conv_w, v_ab_conv_w)),
                       0, None, name="adamw_conv")
    upd["conv"] = [r[0, 0, :nconv].reshape(ab_conv_w.shape) for r in res]

    def small_pack(pre, mix, post, final, lbs, onorm):
        def slot(parts):
            out, r = jnp.zeros((SLOT, d), F32), 0
            for a in (parts if isinstance(parts, tuple) else (parts,)):
                out = out.at[r:r + a.shape[0], :a.shape[1]].set(a)
                r += a.shape[0]
            return out

        return jnp.concatenate([slot(pre), slot(mix), slot(post), slot(final.reshape(1, d)), slot(lbs), slot(onorm)], axis=0)

    d_on = d_onorm.reshape(-1, c_out_norm.shape[1]).sum(axis=0, keepdims=True)
    gsmall = small_pack((d_pre0, d_pre1), (d_mix0, d_mix1), (d_post0, d_post1), d_final, dlb, d_on)
    gsmall_all = _all_gather(gsmall, name="gather_small_grads")
    sres = _small_update(
        gsmall_all,
        small_pack(ffn_pre_norm, mix_norm, ffn_post_norm, final_norm, c_lower_bounds, c_out_norm),
        small_pack(m_ffn_pre_norm, m_mix_norm, m_ffn_post_norm, m_final_norm, m_c_lower_bounds, m_c_out_norm),
        small_pack(v_ffn_pre_norm, v_mix_norm, v_ffn_post_norm, v_final_norm, v_c_lower_bounds, v_c_out_norm),
        name="small_update")

    def small_out(r):
        return {"pre_norm": r[0:2], "mix_norm": r[SLOT:SLOT + 2], "post_norm": r[2 * SLOT:2 * SLOT + 2],
                "final": r[3 * SLOT], "lb": r[ROW_LB:ROW_LB + 2], "onorm": r[5 * SLOT:5 * SLOT + 1, :c_out_norm.shape[1]]}

    small = [small_out(r) for r in sres]
    outs = []
    for k in range(4):
        s = small[k]
        outs += [s["pre_norm"], upd["pre_g"][k], upd["pre_u"][k], upd["pre_d"][k], s["mix_norm"], s["post_norm"],
                 upd["post_g"][k], upd["post_u"][k], upd["post_d"][k], upd["ab_in"][k], upd["conv"][k],
                 upd["ab_out"][k], upd["c_in"][k], s["lb"], s["onorm"], upd["c_out"][k], s["final"]]
    loss = lax.psum(loss_vec[0, 0], ("x", "y", "c"))
    return (loss, dh0[None], *outs)
```

```python
import math

import jax
import jax.numpy as jnp
from jax import lax
from jax.experimental import pallas as pl
from jax.experimental.pallas import tpu as pltpu

F32 = jnp.float32
BF16 = jnp.bfloat16
MESH = pl.DeviceIdType.MESH

RMS_EPS = 1e-6
MACARON = 0.5
LANES = 128
CHUNK = 64
N_LEVELS = 6
HGRN_HEADS = 2
SB_KEYS = 256
ADAM_LR, ADAM_B1, ADAM_B2, ADAM_EPS, ADAM_WD, ADAM_STEP = 0.001, 0.9, 0.999, 1e-08, 0.01, 10
VMEM_LIMIT = 48 * 1024 * 1024


def _cp(**kw):
    return pltpu.CompilerParams(vmem_limit_bytes=VMEM_LIMIT, **kw)


def _sigmoid(x):
    return 0.5 * jnp.tanh(0.5 * x) + 0.5


def _bf(x):
    return x if x.dtype == BF16 else x.astype(BF16)


def _split3(x):
    hi = x.astype(BF16)
    r1 = x - hi.astype(F32)
    mid = r1.astype(BF16)
    lo = (r1 - mid.astype(F32)).astype(BF16)
    return hi, mid, lo


def _dot(a, b, ca=1, cb=0):
    return lax.dot_general(a, b, (((ca,), (cb,)), ((), ())), preferred_element_type=F32)


def _dot_exact_lhs(m, x):
    hi, mid, lo = _split3(x)
    return _dot(m, hi) + _dot(m, mid) + _dot(m, lo)


def _mm(terms, *, name, ta=False, tb=False, out_dtype=F32, residual=None, alpha=1.0, tm=512, tn=512, pack=None,
        seg_rows=()):
    nt = len(terms)
    a0, b0 = terms[0]
    m = a0.shape[1] if ta else a0.shape[0]
    n = b0.shape[0] if tb else b0.shape[1]
    tm, tn = min(tm, m), min(tn, n)
    assert m % tm == 0 and n % tn == 0, (name, m, n, tm, tn)
    has_res = residual is not None
    fused = pack is not None
    nseg = len(seg_rows)
    gm, gn = m // tm, n // tn

    def body(*refs):
        n_in = 2 * nt + has_res + fused
        o_ref = refs[n_in]
        if fused:
            start, forward, finish = _gather_phases(refs[n_in - 1], refs[n_in + 1:n_in + 1 + nseg], seg_rows,
                                                    *refs[n_in + 1 + nseg:])
            step = pl.program_id(0) * gn + pl.program_id(1)
            pl.when(step == 0)(start)
            pl.when(step == (3 * gm * gn) // 4)(forward)
        acc = None
        for i in range(nt):
            a = _bf(refs[2 * i][...])
            b = _bf(refs[2 * i + 1][...])
            p = _dot(a, b, 0 if ta else 1, 1 if tb else 0)
            acc = p if acc is None else acc + p
        if alpha != 1.0:
            acc = acc * alpha
        if has_res:
            acc = acc + refs[2 * nt][...]
        o_ref[...] = acc.astype(out_dtype)
        if fused:
            pl.when(step == gm * gn - 1)(finish)

    in_specs, args = [], []
    for a, b in terms:
        k = a.shape[0] if ta else a.shape[1]
        assert (b.shape[1] if tb else b.shape[0]) == k, (name, a.shape, b.shape)
        in_specs.append(pl.BlockSpec((k, tm), lambda i, j: (0, i)) if ta else pl.BlockSpec((tm, k), lambda i, j: (i, 0)))
        in_specs.append(pl.BlockSpec((tn, k), lambda i, j: (j, 0)) if tb else pl.BlockSpec((k, tn), lambda i, j: (0, j)))
        args += [a, b]
    if has_res:
        in_specs.append(pl.BlockSpec((tm, tn), lambda i, j: (i, j)))
        args.append(residual)
    osp = pl.BlockSpec((tm, tn), lambda i, j: (i, j))
    if not fused:
        return pl.pallas_call(
            body, name=name, grid=(gm, gn), in_specs=in_specs, out_specs=osp,
            out_shape=jax.ShapeDtypeStruct((m, n), out_dtype), compiler_params=_cp())(*args)
    return pl.pallas_call(
        body, name=name, grid=(gm, gn), in_specs=in_specs + [HBM_SPEC], out_specs=[osp] + [HBM_SPEC] * nseg,
        out_shape=[jax.ShapeDtypeStruct((m, n), out_dtype)]
        + [jax.ShapeDtypeStruct((8, r, pack.shape[1]), pack.dtype) for r in seg_rows],
        scratch_shapes=_gather_scratch(),
        compiler_params=_cp(dimension_semantics=("arbitrary", "arbitrary")))(*args, pack)


def _norm_proj(x, gain, w_t, out_dtypes, *, name, tm):
    t, d = x.shape
    n = w_t.shape[0]
    tm = min(tm, t)
    npart = len(out_dtypes)
    width = n // npart

    def body(x_ref, g_ref, w_ref, xn_ref, *part_refs):
        xv = x_ref[...]
        rstd = lax.rsqrt(jnp.mean(xv * xv, axis=-1, keepdims=True) + RMS_EPS)
        xn = (xv * rstd * g_ref[...]).astype(BF16)
        xn_ref[...] = xn
        for p, ref in enumerate(part_refs):
            ref[...] = _dot(xn, w_ref[p * width:(p + 1) * width, :], 1, 1).astype(out_dtypes[p])

    row = pl.BlockSpec((tm, d), lambda i: (i, 0))
    return pl.pallas_call(
        body, name=name, grid=(t // tm,),
        in_specs=[row, pl.BlockSpec((1, d), lambda i: (0, 0)), pl.BlockSpec((n, d), lambda i: (0, 0))],
        out_specs=[row] + [pl.BlockSpec((tm, width), lambda i: (i, 0))] * npart,
        out_shape=[jax.ShapeDtypeStruct((t, d), BF16)] + [jax.ShapeDtypeStruct((t, width), dt) for dt in out_dtypes],
        compiler_params=_cp())(x, gain, w_t)


def _mm_shared_rhs(a_list, b, *, name, tm, out_dtype=BF16, send=None):
    k, n = b.shape
    assert all(a.shape[0] == k and a.shape[1] % tm == 0 and a.shape[1] == a_list[0].shape[1] for a in a_list)
    m = a_list[0].shape[1]
    na = len(a_list)
    nsteps = m // tm
    sends = list(send) if send is not None else []
    ns = len(sends)

    def body(*refs):
        first_out = na + 1 + ns
        if ns:
            start, finish = _direct_exchange_phases(refs[na + 1:first_out], refs[first_out + na], *refs[first_out + na + 1:])
            pl.when(pl.program_id(0) == 0)(start)
        bv = refs[na][...]
        for i in range(na):
            refs[first_out + i][...] = _dot(refs[i][...], bv, 0, 0).astype(out_dtype)
        if ns:
            pl.when(pl.program_id(0) == nsteps - 1)(finish)

    return pl.pallas_call(
        body, name=name, grid=(nsteps,),
        in_specs=[pl.BlockSpec((k, tm), lambda i: (0, i))] * na + [pl.BlockSpec((k, n), lambda i: (0, 0))] + [HBM_SPEC] * ns,
        out_specs=[pl.BlockSpec((tm, n), lambda i: (i, 0))] * na + ([HBM_SPEC] if ns else []),
        out_shape=[jax.ShapeDtypeStruct((m, n), out_dtype)] * na
        + ([jax.ShapeDtypeStruct((8, _direct_exchange_rows(sends), sends[0].shape[3]), sends[0].dtype)] if ns else []),
        scratch_shapes=_direct_exchange_scratch() if ns else [],
        compiler_params=_cp(dimension_semantics=("arbitrary",)))(*a_list, b, *sends)


def _rmsnorm_bwd(x, gain, dxn, dres, *, name, scale, tm=512):
    t, d = x.shape
    tm = min(tm, t)

    def body(x_ref, g_ref, dxn_ref, dres_ref, dx_ref, dxb_ref, dg_ref):
        xv = x_ref[...]
        rstd = lax.rsqrt(jnp.mean(xv * xv, axis=-1, keepdims=True) + RMS_EPS)
        xhat = xv * rstd
        dxn_v = dxn_ref[...]
        dxhat = dxn_v * g_ref[...]
        dx = dres_ref[...] + rstd * (dxhat - xhat * jnp.mean(dxhat * xhat, axis=-1, keepdims=True))
        dx_ref[...] = dx
        dxb_ref[...] = (dx * scale).astype(BF16)

        @pl.when(pl.program_id(0) == 0)
        def _():
            dg_ref[...] = jnp.zeros_like(dg_ref)

        dg_ref[...] += jnp.sum(dxn_v * xhat, axis=0, keepdims=True)

    row = pl.BlockSpec((tm, d), lambda i: (i, 0))
    vec = pl.BlockSpec((1, d), lambda i: (0, 0))
    return pl.pallas_call(
        body, name=name, grid=(t // tm,), in_specs=[row, vec, row, row], out_specs=[row, row, vec],
        out_shape=[jax.ShapeDtypeStruct((t, d), F32), jax.ShapeDtypeStruct((t, d), BF16), jax.ShapeDtypeStruct((1, d), F32)],
        compiler_params=_cp())(x, gain, dxn, dres)


def _loss_head(h, gain, target, *, name, tm=512):
    t, d = h.shape
    tm = min(tm, t)

    def body(h_ref, g_ref, t_ref, dh_ref, dhb_ref, dg_ref, loss_ref):
        hv = h_ref[...]
        rstd = lax.rsqrt(jnp.mean(hv * hv, axis=-1, keepdims=True) + RMS_EPS)
        xhat = hv * rstd
        err = xhat * g_ref[...] - t_ref[...]
        dy = err * (1.0 / d)
        dxhat = dy * g_ref[...]
        dh = rstd * (dxhat - xhat * jnp.mean(dxhat * xhat, axis=-1, keepdims=True))
        dh_ref[...] = dh
        dhb_ref[...] = (dh * MACARON).astype(BF16)

        @pl.when(pl.program_id(0) == 0)
        def _():
            dg_ref[...] = jnp.zeros_like(dg_ref)
            loss_ref[...] = jnp.zeros_like(loss_ref)

        dg_ref[...] += jnp.sum(dy * xhat, axis=0, keepdims=True)
        part = jnp.sum(jnp.sum(err * err, axis=-1, keepdims=True), axis=0, keepdims=True) * (0.5 / d)
        loss_ref[...] += jnp.broadcast_to(part, loss_ref.shape)

    row = pl.BlockSpec((tm, d), lambda i: (i, 0))
    vec = pl.BlockSpec((1, d), lambda i: (0, 0))
    return pl.pallas_call(
        body, name=name, grid=(t // tm,), in_specs=[row, vec, row],
        out_specs=[row, row, vec, pl.BlockSpec((1, LANES), lambda i: (0, 0))],
        out_shape=[jax.ShapeDtypeStruct((t, d), F32), jax.ShapeDtypeStruct((t, d), BF16), jax.ShapeDtypeStruct((1, d), F32),
                   jax.ShapeDtypeStruct((1, LANES), F32)],
        compiler_params=_cp())(h, gain, target)


def _norm_gate_up(x, gain, wg, wu, *, name, tm=256, tf=2816, pack=None, seg_rows=()):
    t, d = x.shape
    f = wg.shape[0]
    tm, tf = min(tm, t), min(tf, f)
    assert f % tf == 0
    ni, nj = t // tm, f // tf
    nseg = len(seg_rows)

    def body(x_ref, g_ref, wg_ref, wu_ref, *rest):
        if pack is not None:
            pack_ref, xn_ref, gg_ref, uu_ref, act_ref = rest[:5]
            start, forward, finish = _gather_phases(pack_ref, rest[5:5 + nseg], seg_rows, *rest[5 + nseg:])
            step = pl.program_id(0) * nj + pl.program_id(1)
            pl.when(step == 0)(start)
            pl.when(step == (3 * ni * nj) // 4)(forward)
        else:
            xn_ref, gg_ref, uu_ref, act_ref = rest

        @pl.when(pl.program_id(1) == 0)
        def _():
            xv = x_ref[...]
            rstd = lax.rsqrt(jnp.mean(xv * xv, axis=-1, keepdims=True) + RMS_EPS)
            xn_ref[...] = (xv * rstd * g_ref[...]).astype(BF16)

        xn = xn_ref[...]
        gv = _dot(xn, wg_ref[...], 1, 1)
        uv = _dot(xn, wu_ref[...], 1, 1)
        gg_ref[...] = gv.astype(BF16)
        uu_ref[...] = uv.astype(BF16)
        act_ref[...] = (gv * _sigmoid(gv) * uv).astype(BF16)
        if pack is not None:
            pl.when(step == ni * nj - 1)(finish)

    row = pl.BlockSpec((tm, d), lambda i, j: (i, 0))
    wsp = pl.BlockSpec((tf, d), lambda i, j: (j, 0))
    osp = pl.BlockSpec((tm, tf), lambda i, j: (i, j))
    fused = pack is not None
    return pl.pallas_call(
        body, name=name, grid=(ni, nj),
        in_specs=[row, pl.BlockSpec((1, d), lambda i, j: (0, 0)), wsp, wsp] + ([HBM_SPEC] if fused else []),
        out_specs=[row, osp, osp, osp] + [HBM_SPEC] * nseg,
        out_shape=[jax.ShapeDtypeStruct((t, d), BF16)] + [jax.ShapeDtypeStruct((t, f), BF16)] * 3
        + [jax.ShapeDtypeStruct((8, n, d), BF16) for n in seg_rows],
        scratch_shapes=_gather_scratch() if fused else [],
        compiler_params=_cp(dimension_semantics=("arbitrary", "arbitrary")))(x, gain, wg, wu, *([pack] if fused else []))


def _swiglu_bwd(dout, wd, gg, uu, chip_part=None, *, name, tm=512, tf=1408):
    t, d = dout.shape
    f = wd.shape[0]
    tm, tf = min(tm, t), min(tf, f)
    nj, ni = f // tf, t // tm
    fused = chip_part is not None

    def body(do_ref, wd_ref, g_ref, u_ref, *rest):
        if fused:
            part_ref, dg_ref, du_ref, parts_ref = rest[:4]
            start, finish = _chip_exchange_phases(part_ref, parts_ref, *rest[4:])
            step = pl.program_id(0) * ni + pl.program_id(1)
            pl.when(step == 0)(start)
        else:
            dg_ref, du_ref = rest
        dact = _dot(do_ref[...], wd_ref[...], 1, 1)
        gv = g_ref[...].astype(F32)
        uv = u_ref[...].astype(F32)
        sg = _sigmoid(gv)
        dg_ref[...] = (dact * uv * (sg * (1.0 + gv * (1.0 - sg)))).astype(BF16)
        du_ref[...] = (dact * (gv * sg)).astype(BF16)
        if fused:
            pl.when(step == nj * ni - 1)(finish)

    osp = pl.BlockSpec((tm, tf), lambda j, i: (i, j))
    return pl.pallas_call(
        body, name=name, grid=(nj, ni),
        in_specs=[pl.BlockSpec((tm, d), lambda j, i: (i, 0)), pl.BlockSpec((tf, d), lambda j, i: (j, 0)), osp, osp]
        + ([HBM_SPEC] if fused else []),
        out_specs=[osp, osp] + ([HBM_SPEC] if fused else []),
        out_shape=[jax.ShapeDtypeStruct((t, f), BF16)] * 2
        + ([jax.ShapeDtypeStruct(chip_part.shape, chip_part.dtype)] if fused else []),
        scratch_shapes=_chip_exchange_scratch() if fused else [],
        compiler_params=_cp(dimension_semantics=("arbitrary", "arbitrary")))(dout, wd, gg, uu, *([chip_part] if fused else []))


def _shift_down(x, n):
    rows = lax.broadcasted_iota(jnp.int32, x.shape, 0)
    return jnp.where(rows >= n, pltpu.roll(x, n, 0), 0.0)


def _shift_up(x, n):
    t = x.shape[0]
    rows = lax.broadcasted_iota(jnp.int32, x.shape, 0)
    return jnp.where(rows < t - n, pltpu.roll(x, t - n, 0), 0.0)


def _conv_fwd(pa, conv_w, *, name):
    t = pa.shape[0]
    nb = pa.shape[1] // 3 // LANES

    def body(b_ref, c_ref, x_ref, w_ref, y_ref):
        u = c_ref[...] * x_ref[...]
        w = w_ref[...]
        conv = w[2:3, :] * u + w[1:2, :] * _shift_down(u, 1) + w[0:1, :] * _shift_down(u, 2)
        y_ref[...] = (b_ref[...] * conv).astype(BF16)

    def col(off):
        return pl.BlockSpec((t, LANES), lambda j: (0, off + j))

    return pl.pallas_call(
        body, name=name, grid=(nb,),
        in_specs=[col(0), col(nb), col(2 * nb), pl.BlockSpec((3, LANES), lambda j: (0, j))],
        out_specs=pl.BlockSpec((t, LANES), lambda j: (0, j)),
        out_shape=jax.ShapeDtypeStruct((t, nb * LANES), BF16), compiler_params=_cp())(pa, pa, pa, conv_w)


def _conv_bwd(pa, dy, conv_w, *, name):
    t = pa.shape[0]
    nb = pa.shape[1] // 3 // LANES

    def body(b_ref, c_ref, x_ref, dy_ref, w_ref, db_ref, dc_ref, dx_ref, dw_ref):
        cv, xv = c_ref[...], x_ref[...]
        u = cv * xv
        u1, u2 = _shift_down(u, 1), _shift_down(u, 2)
        w = w_ref[...]
        conv = w[2:3, :] * u + w[1:2, :] * u1 + w[0:1, :] * u2
        dyv = dy_ref[...]
        db_ref[...] = (dyv * conv).astype(BF16)
        dconv = dyv * b_ref[...]
        du = w[2:3, :] * dconv + w[1:2, :] * _shift_up(dconv, 1) + w[0:1, :] * _shift_up(dconv, 2)
        dc_ref[...] = (du * xv).astype(BF16)
        dx_ref[...] = (du * cv).astype(BF16)
        dw_ref[0:1, :] = jnp.sum(dconv * u2, axis=0, keepdims=True)
        dw_ref[1:2, :] = jnp.sum(dconv * u1, axis=0, keepdims=True)
        dw_ref[2:3, :] = jnp.sum(dconv * u, axis=0, keepdims=True)

    def col(off):
        return pl.BlockSpec((t, LANES), lambda j: (0, off + j))

    osp = pl.BlockSpec((t, LANES), lambda j: (0, j))
    wsp = pl.BlockSpec((3, LANES), lambda j: (0, j))
    return pl.pallas_call(
        body, name=name, grid=(nb,), in_specs=[col(0), col(nb), col(2 * nb), col(0), wsp],
        out_specs=[osp, osp, osp, wsp],
        out_shape=[jax.ShapeDtypeStruct((t, nb * LANES), BF16)] * 3 + [jax.ShapeDtypeStruct((3, nb * LANES), F32)],
        compiler_params=_cp())(pa, pa, pa, dy, conv_w)


def _sb_consts():
    j = lax.broadcasted_iota(jnp.int32, (SB_KEYS, SB_KEYS), 0)
    s = lax.broadcasted_iota(jnp.int32, (SB_KEYS, SB_KEYS), 1)
    after = (j > s).astype(BF16)
    upto = (j <= s).astype(BF16)
    before = (j < s).astype(BF16)
    return after, jnp.stack([upto, before])


def _log_sigmoid(z):
    return jnp.minimum(z, 0.0) - jnp.log(1.0 + jnp.exp(-jnp.abs(z)))


def _attn_fwd(pb, late_pack, seg_rows, *, name, tq=256):
    t = pb.shape[0]
    npair = pb.shape[1] // 3 // LANES
    tq = min(tq, t)
    nq = t // tq
    cmat, _ = _sb_consts()
    scale = 1.0 / math.sqrt(LANES // 2)

    nseg = len(seg_rows)

    def body(q_ref, k_ref, v_ref, c_ref, late_ref, y_ref, lt_ref, *rest):
        i = pl.program_id(1)
        pair = pl.program_id(0)
        scratch = rest[nseg:nseg + 4]
        start, forward, finish = _gather_phases(late_ref, rest[:nseg], seg_rows, *rest[nseg + 4:])
        pl.when((pair == 0) & (i == 0))(start)
        pl.when((pair == npair - 1) & (i == nq // 2))(forward)
        lane = lax.broadcasted_iota(jnp.int32, (tq, LANES), 1)
        rowpos = i * tq + lax.broadcasted_iota(jnp.int32, (tq, SB_KEYS), 0)
        colid = lax.broadcasted_iota(jnp.int32, (tq, SB_KEYS), 1)
        q2 = q_ref[...] * jnp.asarray(scale, BF16)
        cm = c_ref[...]
        hi_lanes = lane >= LANES // 2
        qhs = [jnp.where(hi_lanes == (hh == 1), q2, jnp.zeros_like(q2)) for hh in range(2)]
        per_q = tq // SB_KEYS

        def blk(jb):
            return pl.ds(pl.multiple_of(jb * SB_KEYS, SB_KEYS), SB_KEYS)

        zbuf, wbuf, accbuf, runbuf = scratch

        def scores(jb):
            kb = k_ref[blk(jb), :]
            for hh in range(2):
                zbuf[hh] = _dot(qhs[hh], kb, 1, 1)

        def values(jb):
            vb = v_ref[blk(jb), :]
            for hh in range(2):
                accbuf[hh] += _dot(wbuf[hh], vb)

        def trip(jb, masked, first=False):
            mask = (jb * SB_KEYS + colid) < rowpos if masked else None
            if not first:
                values(jb + 1)
            pre, css = [], []
            for hh in range(2):
                z = zbuf[hh]
                lb = _log_sigmoid(z)
                lk = lb - z
                if masked:
                    lk = jnp.where(mask, lk, 0.0)
                lk_hi, lk_lo = _split2(lk)
                css.append(_dot(lk_hi, cm) + _dot(lk_lo, cm))
                run = runbuf[hh]
                pre.append(lb + run)
                runbuf[hh] = run + jnp.sum(lk, axis=1, keepdims=True)
            scores(jnp.maximum(jb - 1, 0))
            for hh in range(2):
                w = jnp.exp(pre[hh] + css[hh])
                if masked:
                    w = jnp.where(mask, w, 0.0)
                wbuf[hh] = w.astype(BF16)

        nfull = i * per_q
        accbuf[...] = jnp.zeros_like(accbuf)
        runbuf[...] = jnp.zeros_like(runbuf)
        scores(nfull + per_q - 1)
        for dblk in reversed(range(per_q)):
            trip(nfull + dblk, True, first=dblk == per_q - 1)

        def full_block(n, carry):
            trip(nfull - 1 - n, False)
            return carry

        lax.fori_loop(0, nfull, full_block, 0)
        values(0)
        y_ref[...] = jnp.where(hi_lanes, accbuf[1], accbuf[0]).astype(BF16)
        lt_ref[...] = jnp.where(hi_lanes, runbuf[1], runbuf[0])
        pl.when((pair == npair - 1) & (i == nq - 1))(finish)

    return pl.pallas_call(
        body, name=name, grid=(npair, nq),
        in_specs=[pl.BlockSpec((tq, LANES), lambda p, i: (i, p)),
                  pl.BlockSpec((t, LANES), lambda p, i: (0, npair + p)),
                  pl.BlockSpec((t, LANES), lambda p, i: (0, 2 * npair + p)),
                  pl.BlockSpec((SB_KEYS, SB_KEYS), lambda p, i: (0, 0)),
                  HBM_SPEC],
        out_specs=[pl.BlockSpec((tq, LANES), lambda p, i: (i, p))] * 2 + [HBM_SPEC] * nseg,
        out_shape=[jax.ShapeDtypeStruct((t, npair * LANES), BF16), jax.ShapeDtypeStruct((t, npair * LANES), F32),
                   ] + [jax.ShapeDtypeStruct((8, n, late_pack.shape[1]), late_pack.dtype) for n in seg_rows],
        scratch_shapes=[pltpu.VMEM((2, tq, SB_KEYS), F32), pltpu.VMEM((2, tq, SB_KEYS), BF16),
                        pltpu.VMEM((2, tq, LANES), F32), pltpu.VMEM((2, tq, 1), F32)] + _gather_scratch(),
        compiler_params=_cp(dimension_semantics=("arbitrary", "arbitrary")))(pb, pb, pb, cmat, late_pack)


def _attn_bwd(pb, dy, ltot, send, *, name, tq=256):
    t = pb.shape[0]
    npair = pb.shape[1] // 3 // LANES
    tq = min(tq, t)
    nq = t // tq
    _, cmats = _sb_consts()
    scale = 1.0 / math.sqrt(LANES // 2)
    sends = list(send)
    ns = len(sends)

    def body(q_ref, k_ref, v_ref, dy_ref, lt_ref, c_ref, *rest):
        i = pl.program_id(1)
        pair = pl.program_id(0)
        send_refs = rest[:ns]
        dq_ref, dk_ref, dv_ref, parts_ref, dk_acc, dv_acc = rest[ns:ns + 6]
        scratch = rest[ns + 6:ns + 12]
        start, finish = _direct_exchange_phases(send_refs, parts_ref, *rest[ns + 12:])
        pl.when((pair == 0) & (i == 0))(start)

        @pl.when(i == 0)
        def _():
            dk_acc[...] = jnp.zeros_like(dk_acc)
            dv_acc[...] = jnp.zeros_like(dv_acc)

        lane = lax.broadcasted_iota(jnp.int32, (tq, LANES), 1)
        rowpos = i * tq + lax.broadcasted_iota(jnp.int32, (tq, SB_KEYS), 0)
        colid = lax.broadcasted_iota(jnp.int32, (tq, SB_KEYS), 1)
        q2 = q_ref[...] * jnp.asarray(scale, BF16)
        do2 = dy_ref[...].astype(BF16)
        ltv = lt_ref[...]
        c_upto, c_before = c_ref[0], c_ref[1]
        hi_lanes = lane >= LANES // 2
        sels = [hi_lanes == (hh == 1) for hh in range(2)]
        qhs = [jnp.where(s, q2, jnp.zeros_like(q2)) for s in sels]
        dohs = [jnp.where(s, do2, jnp.zeros_like(do2)) for s in sels]
        lts = [ltv[:, 0:1], ltv[:, LANES // 2:LANES // 2 + 1]]
        per_q = tq // SB_KEYS

        def blk(jb):
            return pl.ds(pl.multiple_of(jb * SB_KEYS, SB_KEYS), SB_KEYS)

        zbuf, dabuf, dzbuf, abuf, dqbuf, sumbuf = scratch

        def scores(jb):
            kb, vb = k_ref[blk(jb), :], v_ref[blk(jb), :]
            for hh in range(2):
                zbuf[hh] = _dot(qhs[hh], kb, 1, 1)
                dabuf[hh] = _dot(dohs[hh], vb, 1, 1)

        def products(jb):
            kb = k_ref[blk(jb), :]
            dk_acc[blk(jb), :] += _dot(dzbuf[0], qhs[0], 0, 0) + _dot(dzbuf[1], qhs[1], 0, 0)
            dv_acc[blk(jb), :] += _dot(abuf[0], dohs[0], 0, 0) + _dot(abuf[1], dohs[1], 0, 0)
            for hh in range(2):
                dqbuf[hh] += _dot(dzbuf[hh], kb)

        def trip(jb, masked):
            mask = (jb * SB_KEYS + colid) < rowpos if masked else None
            products(jnp.maximum(jb - 1, 0))
            lbs, css, es, ces = [], [], [], []
            for hh in range(2):
                z = zbuf[hh]
                lb = _log_sigmoid(z)
                lk = lb - z
                if masked:
                    lk = jnp.where(mask, lk, 0.0)
                lk_hi, lk_lo = _split2(lk)
                css.append(_dot(lk_hi, c_upto) + _dot(lk_lo, c_upto))
                csum = sumbuf[2 * hh]
                lbs.append((lb, lb + (lts[hh] - csum)))
                sumbuf[2 * hh] = csum + jnp.sum(lk, axis=1, keepdims=True)
            for hh in range(2):
                a = jnp.exp(lbs[hh][1] - css[hh])
                if masked:
                    a = jnp.where(mask, a, 0.0)
                e = a * dabuf[hh]
                e_hi, e_lo = _split2(e)
                ces.append(_dot(e_hi, c_before) + _dot(e_lo, c_before))
                abuf[hh] = a.astype(BF16)
                es.append(e)
            scores(jnp.minimum(jb + 1, last))
            for hh in range(2):
                prun = sumbuf[2 * hh + 1]
                beta = jnp.exp(lbs[hh][0])
                dz = es[hh] * (1.0 - beta) - (prun + ces[hh]) * beta
                if masked:
                    dz = jnp.where(mask, dz, 0.0)
                dzbuf[hh] = dz.astype(BF16)
                sumbuf[2 * hh + 1] = prun + jnp.sum(es[hh], axis=1, keepdims=True)

        nfull = i * per_q
        last = nfull + per_q - 1
        for buf in (dzbuf, abuf, dqbuf, sumbuf):
            buf[...] = jnp.zeros_like(buf)
        scores(0)

        def full_block(jb, carry):
            trip(jb, False)
            return carry

        lax.fori_loop(0, nfull, full_block, 0)
        for dblk in range(per_q):
            trip(nfull + dblk, True)
        products(last)
        dq_ref[...] = (jnp.where(hi_lanes, dqbuf[1], dqbuf[0]) * scale).astype(BF16)

        @pl.when(i == nq - 1)
        def _():
            dk_ref[...] = dk_acc[...].astype(BF16)
            dv_ref[...] = dv_acc[...].astype(BF16)

        pl.when((pair == npair - 1) & (i == nq - 1))(finish)

    blk = pl.BlockSpec((tq, LANES), lambda p, i: (i, p))
    full = pl.BlockSpec((t, LANES), lambda p, i: (0, p))
    return pl.pallas_call(
        body, name=name, grid=(npair, nq),
        in_specs=[blk,
                  pl.BlockSpec((t, LANES), lambda p, i: (0, npair + p)),
                  pl.BlockSpec((t, LANES), lambda p, i: (0, 2 * npair + p)),
                  pl.BlockSpec((tq, LANES), lambda p, i: (i, npair + p)),
                  blk,
                  pl.BlockSpec((2, SB_KEYS, SB_KEYS), lambda p, i: (0, 0, 0))] + [HBM_SPEC] * ns,
        out_specs=[blk, full, full, HBM_SPEC],
        out_shape=[jax.ShapeDtypeStruct((t, npair * LANES), BF16)] * 3
        + [jax.ShapeDtypeStruct((8, _direct_exchange_rows(sends), sends[0].shape[3]), sends[0].dtype)],
        scratch_shapes=[pltpu.VMEM((t, LANES), F32), pltpu.VMEM((t, LANES), F32),
                        pltpu.VMEM((2, tq, SB_KEYS), F32), pltpu.VMEM((2, tq, SB_KEYS), F32),
                        pltpu.VMEM((2, tq, SB_KEYS), BF16), pltpu.VMEM((2, tq, SB_KEYS), BF16),
                        pltpu.VMEM((2, tq, LANES), F32), pltpu.VMEM((4, tq, 1), F32)] + _direct_exchange_scratch(),
        compiler_params=_cp(dimension_semantics=("arbitrary", "arbitrary")))(pb, pb, pb, dy, ltot, cmats, *sends)


def _hgrn_consts():
    t = lax.broadcasted_iota(jnp.int32, (CHUNK, CHUNK), 0)
    s = lax.broadcasted_iota(jnp.int32, (CHUNK, CHUNK), 1)
    masks = []
    for lvl in range(N_LEVELS):
        half = CHUNK >> (lvl + 1)
        same = (t // (2 * half)) == (s // (2 * half))
        masks.append((same & (t % (2 * half) >= half) & (s % (2 * half) < half)).astype(F32))
    masks.append((t == s).astype(F32))
    prefix = (s <= t).astype(BF16)
    suffix = (s >= t).astype(BF16)
    return prefix, jnp.stack(masks), suffix


def _hgrn_gates(qr, fr, lbv):
    sg = 1.0 / (1.0 + jnp.exp(-fr))
    fval = lbv + (1.0 - lbv) * sg
    kk = (1.0 - lbv) * (1.0 / (1.0 + jnp.exp(fr)))
    sq = _sigmoid(qr)
    return sg, fval, jnp.log(fval), kk, sq, qr * sq


def _lower_bound(c_ref):
    c = c_ref[...]
    mx = jnp.max(c, axis=0, keepdims=True)
    ex = jnp.exp(c - mx)
    return ex[1:2, :] / jnp.sum(ex, axis=0, keepdims=True)


def _level_ref(b, lvl):
    half = CHUNK >> (lvl + 1)
    seg = 2 * half
    if seg >= 8:
        b3 = b.reshape(CHUNK // seg, seg, LANES)
        return jnp.broadcast_to(b3[:, half - 1:half, :], b3.shape).reshape(CHUNK, LANES)
    pos = lax.broadcasted_iota(jnp.int32, b.shape, 0) % seg
    out = b
    for p in range(seg):
        if p != half - 1:
            out = jnp.where(pos == p, pltpu.roll(b, (p - (half - 1)) % CHUNK, 0), out)
    return out


def _hgrn_levels(b, qs, kk):
    out = []
    for lvl in range(N_LEVELS):
        fac = jnp.exp(-jnp.abs(b - _level_ref(b, lvl)))
        out.append((qs * fac, kk * fac, fac, fac))
    out.append((qs, kk, None, None))
    return out


def _split2(x):
    hi = x.astype(BF16)
    return hi, (x - hi.astype(F32)).astype(BF16)


def _hgrn_fwd(pc, c_lb, out_norm, *, name, tc=1024):
    t = pc.shape[0]
    nh = pc.shape[1] // 4 // LANES
    tc = min(tc, t)
    nch = tc // CHUNK
    cum_all, masks, _ = _hgrn_consts()

    def body(q_ref, f_ref, i_ref, g_ref, lb_ref, on_ref, cum_ref, m_ref, y_ref, o_ref, st_ref, state):
        @pl.when(pl.program_id(1) == 0)
        def _():
            state[...] = jnp.zeros_like(state)

        lbv = _lower_bound(lb_ref)
        onv = on_ref[...]

        def chunk(c, carry):
            rows = pl.ds(pl.multiple_of(c * CHUNK, CHUNK), CHUNK)
            for hh in range(HGRN_HEADS):
                lanes = slice(hh * LANES, (hh + 1) * LANES)
                _, _, g, kk, _, qs = _hgrn_gates(q_ref[rows, lanes], f_ref[rows, lanes], lbv[:, lanes])
                vb = i_ref[rows, lanes].astype(BF16)
                b = _dot_exact_lhs(cum_ref[...], g)
                scores = jnp.zeros((CHUNK, CHUNK), F32)
                for lvl, (ql, kl, _, _) in enumerate(_hgrn_levels(b, qs, kk)):
                    scores = scores + _dot(ql.astype(BF16), kl.astype(BF16), 1, 1) * m_ref[lvl]
                st = state[hh]
                st_ref[hh, c] = st
                o = _dot(scores.astype(BF16), vb) + _dot((qs * jnp.exp(b)).astype(BF16), st.astype(BF16), 1, 1)
                blast = b[CHUNK - 1:CHUNK, :]
                kdec = (kk * jnp.exp(blast - b)).astype(BF16)
                state[hh] = st * jnp.exp(blast) + _dot(vb, kdec, 0, 0)
                o_ref[rows, lanes] = o
                rstd = lax.rsqrt(jnp.mean(o * o, axis=-1, keepdims=True) + RMS_EPS)
                gate = g_ref[rows, lanes]
                y_ref[rows, lanes] = (o * rstd * onv * (gate * _sigmoid(gate))).astype(BF16)
            return carry

        lax.fori_loop(0, nch, chunk, 0, unroll=2)

    hw = HGRN_HEADS * LANES

    def col(off):
        return pl.BlockSpec((tc, hw), lambda h, i: (i, off // HGRN_HEADS + h))

    osp = pl.BlockSpec((tc, hw), lambda h, i: (i, h))
    return pl.pallas_call(
        body, name=name, grid=(nh // HGRN_HEADS, t // tc),
        in_specs=[col(0), col(nh), col(2 * nh), col(3 * nh),
                  pl.BlockSpec((2, hw), lambda h, i: (0, h)),
                  pl.BlockSpec((1, LANES), lambda h, i: (0, 0)),
                  pl.BlockSpec(cum_all.shape, lambda h, i: (0, 0)),
                  pl.BlockSpec(masks.shape, lambda h, i: (0, 0, 0))],
        out_specs=[osp, osp, pl.BlockSpec((HGRN_HEADS, nch, LANES, LANES), lambda h, i: (h, i, 0, 0))],
        out_shape=[jax.ShapeDtypeStruct((t, nh * LANES), BF16), jax.ShapeDtypeStruct((t, nh * LANES), F32),
                   jax.ShapeDtypeStruct((nh, t // CHUNK, LANES, LANES), F32)],
        scratch_shapes=[pltpu.VMEM((HGRN_HEADS, LANES, LANES), F32)],
        compiler_params=_cp())(pc, pc, pc, pc, c_lb, out_norm, cum_all, masks)


def _hgrn_bwd(pc, o_saved, states, dy, c_lb, out_norm, send, *, name, tc=1024):
    t = pc.shape[0]
    nh = pc.shape[1] // 4 // LANES
    tc = min(tc, t)
    nch = tc // CHUNK
    nt = t // tc
    cum_all, masks, suffix = _hgrn_consts()
    ngroup = nh // HGRN_HEADS
    sends = list(send)
    ns = len(sends)

    def body(q_ref, f_ref, i_ref, g_ref, o_ref, st_ref, dy_ref, lb_ref, on_ref, cum_ref, m_ref, suf_ref, *rest):
        send_refs = rest[:ns]
        dq_ref, df_ref, di_ref, dg_ref, dlb_ref, don_ref, parts_ref, dstate = rest[ns:ns + 8]
        start, finish = _direct_exchange_phases(send_refs, parts_ref, *rest[ns + 8:])
        pl.when((pl.program_id(0) == 0) & (pl.program_id(1) == 0))(start)

        @pl.when(pl.program_id(1) == 0)
        def _():
            dstate[...] = jnp.zeros_like(dstate)
            dlb_ref[...] = jnp.zeros_like(dlb_ref)
            don_ref[...] = jnp.zeros_like(don_ref)

        lbv = _lower_bound(lb_ref)
        onv = on_ref[...]

        def head(hh, c, rows):
            lanes = slice(hh * LANES, (hh + 1) * LANES)
            qr = q_ref[rows, lanes]
            sg, fval, g, kk, sq, qs = _hgrn_gates(qr, f_ref[rows, lanes], lbv[:, lanes])
            vb = i_ref[rows, lanes].astype(BF16)
            o = o_ref[rows, lanes]
            gate = g_ref[rows, lanes]
            sgt = _sigmoid(gate)
            rstd = lax.rsqrt(jnp.mean(o * o, axis=-1, keepdims=True) + RMS_EPS)
            ohat = o * rstd
            dyv = dy_ref[rows, lanes]
            don = dyv * (gate * sgt)
            dg_ref[rows, lanes] = (dyv * ohat * onv * (sgt * (1.0 + gate * (1.0 - sgt)))).astype(BF16)
            don_ref[:, lanes] += jnp.sum(don * ohat, axis=0, keepdims=True)
            dxhat = don * onv
            dob = (rstd * (dxhat - ohat * jnp.mean(dxhat * ohat, axis=-1, keepdims=True))).astype(BF16)
            b = _dot_exact_lhs(cum_ref[...], g)
            blast = b[CHUNK - 1:CHUNK, :]
            eb = jnp.exp(b)
            edec = jnp.exp(blast - b)
            st32 = st_ref[hh, c]
            st = st32.astype(BF16)
            dst = dstate[hh]
            dstb = dst.astype(BF16)
            da = _dot(dob, vb, 1, 1)
            levels = _hgrn_levels(b, qs, kk)
            scores = jnp.zeros((CHUNK, CHUNK), F32)
            dq = eb * _dot(dob, st)
            dk_inter = edec * _dot(vb, dstb)
            dk = dk_inter
            for lvl, (ql, kl, eq, ek) in enumerate(levels):
                mk = m_ref[lvl]
                (qh, qlo), (kh, klo) = _split2(ql), _split2(kl)
                scores = scores + _dot(qh, kh, 1, 1) * mk
                dal = (da * mk).astype(BF16)
                dql = _dot(dal, kh) + _dot(dal, klo)
                dkl = _dot(dal, qh, 0, 0) + _dot(dal, qlo, 0, 0)
                dq = dq + (dql if eq is None else dql * eq)
                dk = dk + (dkl if ek is None else dkl * ek)
            kdec = (kk * edec).astype(BF16)
            dv = _dot(scores.astype(BF16), dob, 0, 0) + _dot(kdec, dstb, 1, 1)
            dstate[hh] = dst * jnp.exp(blast) + _dot(dob, (qs * eb).astype(BF16), 0, 0)
            db = qs * dq - kk * dk
            last = jnp.sum(kk * dk_inter, axis=0, keepdims=True) + jnp.exp(blast) * jnp.sum(dst * st32, axis=0, keepdims=True)
            dgl = _dot_exact_lhs(suf_ref[...], db) + last
            dfv = dgl / fval - dk
            df_ref[rows, lanes] = (dfv * (1.0 - lbv[:, lanes]) * sg * (1.0 - sg)).astype(BF16)
            dlb_ref[:, lanes] += jnp.sum(dfv * (1.0 - sg), axis=0, keepdims=True)
            dq_ref[rows, lanes] = (dq * (sq * (1.0 + qr * (1.0 - sq)))).astype(BF16)
            di_ref[rows, lanes] = dv.astype(BF16)

        def chunk(n, carry):
            c = nch - 1 - n
            rows = pl.ds(pl.multiple_of(c * CHUNK, CHUNK), CHUNK)
            for hh in range(HGRN_HEADS):
                head(hh, c, rows)
            return carry

        lax.fori_loop(0, nch, chunk, 0, unroll=2)
        pl.when((pl.program_id(0) == ngroup - 1) & (pl.program_id(1) == nt - 1))(finish)

    hw = HGRN_HEADS * LANES

    def col(off):
        return pl.BlockSpec((tc, hw), lambda h, i: (nt - 1 - i, off // HGRN_HEADS + h))

    osp = pl.BlockSpec((tc, hw), lambda h, i: (nt - 1 - i, h))
    vec = pl.BlockSpec((1, hw), lambda h, i: (0, h))
    return pl.pallas_call(
        body, name=name, grid=(nh // HGRN_HEADS, nt),
        in_specs=[col(0), col(nh), col(2 * nh), col(3 * nh), osp,
                  pl.BlockSpec((HGRN_HEADS, nch, LANES, LANES), lambda h, i: (h, nt - 1 - i, 0, 0)),
                  osp,
                  pl.BlockSpec((2, hw), lambda h, i: (0, h)),
                  pl.BlockSpec((1, LANES), lambda h, i: (0, 0)),
                  pl.BlockSpec(cum_all.shape, lambda h, i: (0, 0)),
                  pl.BlockSpec(masks.shape, lambda h, i: (0, 0, 0)),
                  pl.BlockSpec(suffix.shape, lambda h, i: (0, 0))] + [HBM_SPEC] * ns,
        out_specs=[osp, osp, osp, osp, vec, vec, HBM_SPEC],
        out_shape=[jax.ShapeDtypeStruct((t, nh * LANES), BF16)] * 4 + [jax.ShapeDtypeStruct((1, nh * LANES), F32)] * 2
        + [jax.ShapeDtypeStruct((8, _direct_exchange_rows(sends), sends[0].shape[3]), sends[0].dtype)],
        scratch_shapes=[pltpu.VMEM((HGRN_HEADS, LANES, LANES), F32)] + _direct_exchange_scratch(),
        compiler_params=_cp(dimension_semantics=("arbitrary", "arbitrary")))(
            pc, pc, pc, pc, o_saved, states, dy, c_lb, out_norm, cum_all, masks, suffix, *sends)


HBM_SPEC = pl.BlockSpec(memory_space=pltpu.HBM)


def _gather_scratch():
    return [pltpu.SemaphoreType.DMA((7,)), pltpu.SemaphoreType.DMA((7,)), pltpu.SemaphoreType.DMA]


def _gather_phases(x_ref, out_refs, seg_rows, send_sems, recv_sems, local_sem):
    x, y, c = lax.axis_index("x"), lax.axis_index("y"), lax.axis_index("c")
    me, sibling = (x, y, c), (x, y, 1 - c)
    chips = [(1 - x, y), (x, 1 - y), (1 - x, 1 - y)]
    offs = [sum(seg_rows[:s]) for s in range(len(seg_rows))]
    assert sum(seg_rows) == x_ref.shape[0]

    def index(px, py, pc):
        return 4 * px + 2 * py + pc

    def copies(k, block, to, own):
        return [pltpu.make_async_remote_copy(
            src_ref=x_ref.at[pl.ds(offs[s], n)] if own else out_refs[s].at[index(*block)],
            dst_ref=out_refs[s].at[index(*block)],
            send_sem=send_sems.at[k], recv_sem=recv_sems.at[k], device_id=to, device_id_type=MESH)
            for s, n in enumerate(seg_rows)]

    def all_bytes(k):
        return pltpu.make_async_remote_copy(src_ref=x_ref, dst_ref=x_ref, send_sem=send_sems.at[k],
                                            recv_sem=recv_sems.at[k], device_id=me, device_id_type=MESH)

    mine = [pltpu.make_async_copy(x_ref.at[pl.ds(offs[s], n)], out_refs[s].at[index(*me)], local_sem)
            for s, n in enumerate(seg_rows)]
    first = copies(0, me, sibling, True)
    for j, chip in enumerate(chips):
        first += copies(1 + j, me, (*chip, c), True)

    def start():
        for cp in mine + first:
            cp.start()

    def forward():
        for j, chip in enumerate(chips):
            all_bytes(1 + j).wait_recv()
            for cp in copies(4 + j, (*chip, c), sibling, False):
                cp.start()

    def finish():
        all_bytes(0).wait_recv()
        for j in range(3):
            all_bytes(4 + j).wait_recv()
        for k in range(7):
            all_bytes(k).wait_send()
        pltpu.make_async_copy(x_ref, x_ref, local_sem).wait()

    return start, forward, finish


def _all_gather(xs, seg_rows=None, *, name):
    segs = [xs.shape[0]] if seg_rows is None else list(seg_rows)

    def body(x_ref, *rest):
        start, forward, finish = _gather_phases(x_ref, rest[:len(segs)], segs, *rest[len(segs):])
        start()
        forward()
        finish()

    outs = pl.pallas_call(
        body, name=name, in_specs=[HBM_SPEC], out_specs=[HBM_SPEC] * len(segs),
        out_shape=[jax.ShapeDtypeStruct((8, n, xs.shape[1]), xs.dtype) for n in segs],
        scratch_shapes=_gather_scratch())(xs)
    return outs[0] if seg_rows is None else outs


def _sibling_exchange(s, *, name):
    def body(s_ref, rb_ref, send_sem, recv_sem):
        x, y, c = lax.axis_index("x"), lax.axis_index("y"), lax.axis_index("c")
        cp = pltpu.make_async_remote_copy(
            src_ref=s_ref.at[:, 1 - c], dst_ref=rb_ref, send_sem=send_sem, recv_sem=recv_sem,
            device_id=(x, y, 1 - c), device_id_type=MESH)
        cp.start()
        cp.wait()

    return pl.pallas_call(
        body, name=name, in_specs=[HBM_SPEC], out_specs=HBM_SPEC,
        out_shape=jax.ShapeDtypeStruct(s.shape[:1] + s.shape[2:], s.dtype),
        scratch_shapes=[pltpu.SemaphoreType.DMA, pltpu.SemaphoreType.DMA])(s)


def _row_tile(n, cap=1024):
    return max(b for b in range(16, cap + 1, 16) if n % b == 0)


def _pair_add(s, rb, core, *, name):
    nchip, _, r, c = s.shape
    tb = _row_tile(r)

    def body(core_ref, a_ref, b_ref, o_ref):
        o_ref[...] = (a_ref[...].astype(F32) + b_ref[...].astype(F32)).astype(BF16)

    blk = pl.BlockSpec((None, tb, c), lambda ch, i, cr: (ch, i, 0))
    return pl.pallas_call(
        body, name=name,
        grid_spec=pltpu.PrefetchScalarGridSpec(
            num_scalar_prefetch=1, grid=(nchip, r // tb),
            in_specs=[pl.BlockSpec((None, None, tb, c), lambda ch, i, cr: (ch, cr[0], i, 0)), blk],
            out_specs=blk),
        out_shape=jax.ShapeDtypeStruct((nchip, r, c), BF16), compiler_params=_cp())(core, s, rb)


def _chip_exchange_scratch():
    return [pltpu.SemaphoreType.DMA((3,)), pltpu.SemaphoreType.DMA((3,)), pltpu.SemaphoreType.DMA]


def _chip_exchange_phases(p_ref, out_ref, send_sems, recv_sems, local_sem):
    x, y, c = lax.axis_index("x"), lax.axis_index("y"), lax.axis_index("c")
    mine = 2 * x + y
    own = pltpu.make_async_copy(p_ref.at[mine], out_ref.at[mine], local_sem)
    copies = [pltpu.make_async_remote_copy(
        src_ref=p_ref.at[2 * tx + ty], dst_ref=out_ref.at[mine],
        send_sem=send_sems.at[k], recv_sem=recv_sems.at[k], device_id=(tx, ty, c), device_id_type=MESH)
        for k, (tx, ty) in enumerate([(1 - x, y), (x, 1 - y), (1 - x, 1 - y)])]

    def start():
        own.start()
        for cp in copies:
            cp.start()

    def finish():
        for cp in copies:
            cp.wait()
        own.wait()

    return start, finish


def _direct_exchange_scratch():
    return [pltpu.SemaphoreType.DMA((7,)), pltpu.SemaphoreType.DMA((7,)), pltpu.SemaphoreType.DMA]


def _direct_exchange_rows(sends):
    return sum(s.shape[2] for s in sends)


def _direct_exchange_phases(s_refs, out_ref, send_sems, recv_sems, local_sem):
    x, y, c = lax.axis_index("x"), lax.axis_index("y"), lax.axis_index("c")
    me = 4 * x + 2 * y + c
    offs, off = [], 0
    for s in s_refs:
        offs.append(off)
        off += s.shape[2]

    def slot(p):
        return out_ref.at[me, pl.ds(offs[p], s_refs[p].shape[2])]

    own = [pltpu.make_async_copy(s.at[2 * x + y, c], slot(p), local_sem) for p, s in enumerate(s_refs)]
    flips = [(fx, fy, fc) for fx in (0, 1) for fy in (0, 1) for fc in (0, 1) if (fx, fy, fc) != (0, 0, 0)]
    copies = []
    for k, (fx, fy, fc) in enumerate(flips):
        tx, ty, tc = (1 - x if fx else x), (1 - y if fy else y), (1 - c if fc else c)
        copies += [pltpu.make_async_remote_copy(
            src_ref=s.at[2 * tx + ty, tc], dst_ref=slot(p),
            send_sem=send_sems.at[k], recv_sem=recv_sems.at[k], device_id=(tx, ty, tc), device_id_type=MESH)
            for p, s in enumerate(s_refs)]

    def start():
        for cp in own + copies:
            cp.start()

    def finish():
        whole = out_ref.at[me]
        for k in range(len(flips)):
            pltpu.make_async_remote_copy(src_ref=whole, dst_ref=whole, send_sem=send_sems.at[k],
                                         recv_sem=recv_sems.at[k], device_id=(x, y, c), device_id_type=MESH).wait()
        pltpu.make_async_copy(whole, whole, local_sem).wait()

    return start, finish


def _adamw_math(w, g, m, v):
    m2 = ADAM_B1 * m + (1.0 - ADAM_B1) * g
    v2 = ADAM_B2 * v + (1.0 - ADAM_B2) * (g * g)
    m_hat = m2 / (1.0 - ADAM_B1 ** ADAM_STEP)
    v_hat = v2 / (1.0 - ADAM_B2 ** ADAM_STEP)
    return -ADAM_LR * (m_hat / (jnp.sqrt(v_hat) + ADAM_EPS) + ADAM_WD * w), m2, v2


def _adamw_shard(parts, g_off, w, m, v, layer, prev, *, name):
    _, r, c = w.shape
    npart = parts.shape[0]
    tb = next(b for b in range(min(r, 512), 0, -16) if r % b == 0 and g_off % b == 0)

    def body(*refs):
        w_ref, m_ref, v_ref = refs[npart:npart + 3]
        g_out, d_out, m_out, v_out = refs[-4:]
        g = refs[0][...].astype(F32)
        for p_ref in refs[1:npart]:
            g = g + p_ref[...].astype(F32)
        d, m2, v2 = _adamw_math(w_ref[...], g, m_ref[...], v_ref[...])
        g_out[...] = g
        d_out[...] = d
        m_out[...] = m2
        v_out[...] = v2

    def part(ch):
        return pl.BlockSpec((None, tb, c), lambda i: (ch, g_off // tb + i, 0))

    blk = pl.BlockSpec((None, tb, c), lambda i: (layer, i, 0))
    prev = list(prev) if prev is not None else []
    return pl.pallas_call(
        body, name=name, grid=(r // tb,),
        in_specs=[part(ch) for ch in range(npart)] + [blk, blk, blk] + [pl.BlockSpec(memory_space=pl.ANY)] * len(prev),
        out_specs=[blk] * 4, out_shape=[jax.ShapeDtypeStruct(w.shape, F32)] * 4,
        input_output_aliases={npart + 3 + k: k for k in range(len(prev))},
        compiler_params=_cp())(*([parts] * npart), w, m, v, *prev)


SLOT = 8
SMALL_ROWS = 6 * SLOT
ROW_LB = 4 * SLOT


def _small_update(gath, w, m, v, *, name):
    def body(g_ref, w_ref, m_ref, v_ref, g_out, d_out, m_out, v_out):
        tot = g_ref[0]
        for k in range(1, 8):
            tot = tot + g_ref[k]
        wv = w_ref[...]
        c0, c1 = wv[ROW_LB:ROW_LB + 1, :], wv[ROW_LB + 1:ROW_LB + 2, :]
        mx = jnp.maximum(c0, c1)
        e0, e1 = jnp.exp(c0 - mx), jnp.exp(c1 - mx)
        lb = e1 / (e0 + e1)
        gl = tot[ROW_LB:ROW_LB + 1, :] * lb * (1.0 - lb)
        row = lax.broadcasted_iota(jnp.int32, tot.shape, 0)
        g = jnp.where(row == ROW_LB, -gl, jnp.where(row == ROW_LB + 1, gl, tot))
        d, m2, v2 = _adamw_math(wv, g, m_ref[...], v_ref[...])
        g_out[...] = g
        d_out[...] = d
        m_out[...] = m2
        v_out[...] = v2

    return pl.pallas_call(
        body, name=name, out_shape=[jax.ShapeDtypeStruct(w.shape, F32)] * 4, compiler_params=_cp())(gath, w, m, v)


D_MODEL = 1024


def _ffn_fwd(h, gain, wg, wu, wd, tag):
    xn, gg, uu, act = _norm_gate_up(h, gain, wg, wu, name=f"{tag}_gate_up")
    out = _mm([(act, wd)], residual=h, alpha=MACARON, tn=1024, name=f"{tag}_down")
    return out, (h, xn, gg, uu, act)


def _ffn_input_bwd(dg, du, wg, wu, x, gain, dres, chip_part, *, name, scale, tm=256):
    t, d = x.shape
    f = wg.shape[0]
    tm = min(tm, t)
    nt = t // tm
    fused = chip_part is not None

    def body(dg_ref, du_ref, wg_ref, wu_ref, x_ref, g_ref, dres_ref, *rest):
        if fused:
            part_ref, dx_ref, dxb_ref, dgain_ref, parts_ref = rest[:5]
            start, finish = _chip_exchange_phases(part_ref, parts_ref, *rest[5:])
            pl.when(pl.program_id(0) == 0)(start)
        else:
            dx_ref, dxb_ref, dgain_ref = rest
        dxn_v = _dot(dg_ref[...], wg_ref[...]) + _dot(du_ref[...], wu_ref[...])
        xv = x_ref[...]
        rstd = lax.rsqrt(jnp.mean(xv * xv, axis=-1, keepdims=True) + RMS_EPS)
        xhat = xv * rstd
        dxhat = dxn_v * g_ref[...]
        dx = dres_ref[...] + rstd * (dxhat - xhat * jnp.mean(dxhat * xhat, axis=-1, keepdims=True))
        dx_ref[...] = dx
        dxb_ref[...] = (dx * scale).astype(BF16)

        @pl.when(pl.program_id(0) == 0)
        def _():
            dgain_ref[...] = jnp.zeros_like(dgain_ref)

        dgain_ref[...] += jnp.sum(dxn_v * xhat, axis=0, keepdims=True)
        if fused:
            pl.when(pl.program_id(0) == nt - 1)(finish)

    wide = pl.BlockSpec((tm, f), lambda i: (i, 0))
    wsp = pl.BlockSpec((f, d), lambda i: (0, 0))
    row = pl.BlockSpec((tm, d), lambda i: (i, 0))
    vec = pl.BlockSpec((1, d), lambda i: (0, 0))
    args = [dg, du, wg, wu, x, gain, dres] + ([chip_part] if fused else [])
    return pl.pallas_call(
        body, name=name, grid=(nt,),
        in_specs=[wide, wide, wsp, wsp, row, vec, row] + ([HBM_SPEC] if fused else []),
        out_specs=[row, row, vec] + ([HBM_SPEC] if fused else []),
        out_shape=[jax.ShapeDtypeStruct((t, d), F32), jax.ShapeDtypeStruct((t, d), BF16), jax.ShapeDtypeStruct((1, d), F32)]
        + ([jax.ShapeDtypeStruct(chip_part.shape, chip_part.dtype)] if fused else []),
        scratch_shapes=_chip_exchange_scratch() if fused else [],
        compiler_params=_cp(dimension_semantics=("arbitrary",)))(*args)


def _ffn_bwd(dout, dout_half, saved, gain, wg, wu, wd, tag, next_scale, exchanges=None):
    h, xn, gg, uu, act = saved
    early_chip_part, send_after_dwd, chip_part_after_dwgu = exchanges if exchanges is not None else (None, None, None)
    dg, du, *early_parts = _swiglu_bwd(dout_half, wd, gg, uu, early_chip_part, tm=256, tf=wd.shape[0],
                                       name=f"{tag}_dact")
    dwd = _mm([(act, dout_half)], ta=True, tm=256, tn=1024, out_dtype=BF16, name=f"{tag}_dwd")
    send = send_after_dwd(dwd) if exchanges is not None else None
    dwg, dwu, *mid_parts = _mm_shared_rhs([dg, du], xn, tm=256, send=send, name=f"{tag}_dwgu")
    chip_part = chip_part_after_dwgu(dwg, dwu) if exchanges is not None else None
    dh, dh_b, dgain, *parts = _ffn_input_bwd(dg, du, wg, wu, h, gain, dout, chip_part, scale=next_scale,
                                             name=f"{tag}_input_bwd")
    return dh, dh_b, dwg, dwu, dwd, dgain, (early_parts + mid_parts + parts)


def kernel(x, ffn_pre_norm, ffn_pre_w_gate, ffn_pre_w_up, ffn_pre_w_down, mix_norm, ffn_post_norm, ffn_post_w_gate, ffn_post_w_up, ffn_post_w_down, ab_w_in, ab_conv_w, ab_w_out, c_w_in, c_lower_bounds, c_out_norm, c_w_out, final_norm, loss_target, m_ffn_pre_norm, m_ffn_pre_w_gate, m_ffn_pre_w_up, m_ffn_pre_w_down, m_mix_norm, m_ffn_post_norm, m_ffn_post_w_gate, m_ffn_post_w_up, m_ffn_post_w_down, m_ab_w_in, m_ab_conv_w, m_ab_w_out, m_c_w_in, m_c_lower_bounds, m_c_out_norm, m_c_w_out, m_final_norm, v_ffn_pre_norm, v_ffn_pre_w_gate, v_ffn_pre_w_up, v_ffn_pre_w_down, v_mix_norm, v_ffn_post_norm, v_ffn_post_w_gate, v_ffn_post_w_up, v_ffn_post_w_down, v_ab_w_in, v_ab_conv_w, v_ab_w_out, v_c_w_in, v_c_lower_bounds, v_c_out_norm, v_c_w_out, v_final_norm):
    d = D_MODEL
    h0 = x[0]
    target = loss_target[0]
    core = lax.axis_index("c").astype(jnp.int32).reshape(1)

    big = [("pre_g", ffn_pre_w_gate, m_ffn_pre_w_gate, v_ffn_pre_w_gate),
           ("pre_u", ffn_pre_w_up, m_ffn_pre_w_up, v_ffn_pre_w_up),
           ("pre_d", ffn_pre_w_down, m_ffn_pre_w_down, v_ffn_pre_w_down),
           ("post_g", ffn_post_w_gate, m_ffn_post_w_gate, v_ffn_post_w_gate),
           ("post_u", ffn_post_w_up, m_ffn_post_w_up, v_ffn_post_w_up),
           ("post_d", ffn_post_w_down, m_ffn_post_w_down, v_ffn_post_w_down),
           ("ab_in", ab_w_in, m_ab_w_in, v_ab_w_in),
           ("ab_out", ab_w_out, m_ab_w_out, v_ab_w_out),
           ("c_in", c_w_in, m_c_w_in, v_c_w_in),
           ("c_out", c_w_out, m_c_w_out, v_c_w_out)]
    by_tag = {tag: (w, m, v) for tag, w, m, v in big}

    def layer_rows(tag):
        w = by_tag[tag][0]
        return w.size // d // w.shape[0]

    def layout(items):
        offs, off = {}, 0
        for item in items:
            offs[item] = off
            off += layer_rows(item[0])
        return offs, off

    ffn = [f"{pos}_{kind}" for pos in ("pre", "post") for kind in "gud"]
    first_items = [("pre_g", 0), ("pre_u", 0)]
    early_items = [("pre_d", 0)]
    early2_items = [("ab_in", 0)]
    late_items = ([("pre_g", 1), ("pre_u", 1), ("pre_d", 1)] + [(f"post_{kind}", l) for l in (0, 1) for kind in "gud"]
                  + [("ab_out", 0), ("c_in", 0), ("c_out", 0)])
    grad_items = {"A0": [(f"post_{kind}", 1) for kind in "gud"] + [("c_out", 0)],
                  "A1": ([(f"pre_{kind}", 1) for kind in "gud"] + [(f"post_{kind}", 0) for kind in "gud"]
                         + [("c_in", 0), ("ab_out", 0)]),
                  "C": [("ab_in", 0)], "B0": [("pre_d", 0)], "B1": [("pre_g", 0), ("pre_u", 0)]}
    grad_offs = {k: layout(items)[0] for k, items in grad_items.items()}
    grad_conv_row = layout(grad_items["C"])[1]

    def conv_rows(a, split):
        flat = a.reshape(-1)
        if split:
            hi = flat.astype(BF16)
            flat = jnp.concatenate([hi, (flat - hi.astype(F32)).astype(BF16)])
        return jnp.zeros((16, d), flat.dtype).at[0, :flat.shape[0]].set(flat)

    nconv = ab_conv_w.size
    col_sharded = {"pre_g", "pre_u", "post_g", "post_u", "ab_in", "c_in"}

    def pack_rows(item):
        tag, layer = item
        a = by_tag[tag][0][layer]
        return (a.T if tag in col_sharded else a).reshape(-1, d).astype(BF16)

    first_pack = jnp.concatenate([pack_rows(item) for item in first_items], axis=0)
    early_pack = jnp.concatenate([pack_rows(item) for item in early_items], axis=0)
    early2_pack = jnp.concatenate([pack_rows(item) for item in early2_items] + [conv_rows(ab_conv_w, True)], axis=0)
    late_pack = jnp.concatenate([pack_rows(item) for item in late_items], axis=0)
    first_w = _all_gather(first_pack, [layer_rows(tag) for tag, _ in first_items], name="gather_first_weights")
    full = {item: g.reshape(-1, d) for item, g in zip(first_items, first_w)}

    xn0, gg0, uu0, act0, *early_w = _norm_gate_up(
        h0, ffn_pre_norm[0:1], full["pre_g", 0], full["pre_u", 0], name="l0pre_gate_up_gather_early_weights",
        pack=early_pack, seg_rows=[layer_rows(tag) for tag, _ in early_items])
    full.update({item: g.reshape(-1, d) for item, g in zip(early_items, early_w)})
    ffn_w = {("pre", 0): tuple(full[f"pre_{kind}", 0] for kind in "gud")}
    h1, *early2_w = _mm([(act0, full["pre_d", 0])], residual=h0, alpha=MACARON, tn=1024, pack=early2_pack,
                        seg_rows=[layer_rows(tag) for tag, _ in early2_items] + [16], name="l0pre_down_gather_ab_weights")
    full.update({item: g.reshape(-1, d) for item, g in zip(early2_items, early2_w)})
    w_ab_in = full["ab_in", 0]
    cg = early2_w[-1][:, 0, :2 * nconv].astype(F32)
    conv_w = (cg[:, :nconv] + cg[:, nconv:]).reshape(8, 3, -1).transpose(1, 0, 2).reshape(3, -1)
    aw = w_ab_in.shape[0] // 6
    s_pre0 = (h0, xn0, gg0, uu0, act0)
    hn0, pa, pb = _norm_proj(h1, mix_norm[0:1], w_ab_in, (F32, BF16), tm=512, name="ab_norm_proj")
    ya = _conv_fwd(pa, conv_w, name="conv_fwd")
    yb, ltot, *late_w = _attn_fwd(pb, late_pack, [layer_rows(tag) for tag, _ in late_items],
                                  name="attn_fwd_gather_late_weights")
    full.update({item: g.reshape(-1, d) for item, g in zip(late_items, late_w)})
    for pos, layer in (("post", 0), ("pre", 1), ("post", 1)):
        ffn_w[pos, layer] = tuple(full[f"{pos}_{kind}", layer] for kind in "gud")
    w_ab_out, w_c_in, w_c_out = full["ab_out", 0], full["c_in", 0], full["c_out", 0]
    h2 = _mm([(ya, w_ab_out[:aw]), (yb, w_ab_out[aw:])], residual=h1, tn=1024, name="ab_out")
    h3, s_post0 = _ffn_fwd(h2, ffn_post_norm[0:1], *ffn_w["post", 0], "l0post")
    h4, s_pre1 = _ffn_fwd(h3, ffn_pre_norm[1:2], *ffn_w["pre", 1], "l1pre")
    hn1, pc = _norm_proj(h4, mix_norm[1:2], w_c_in, (F32,), tm=256, name="c_norm_proj")
    yc, o_saved, states = _hgrn_fwd(pc, c_lower_bounds, c_out_norm, name="hgrn_fwd")
    h5 = _mm([(yc, w_c_out)], residual=h4, tn=1024, name="c_out")
    h6, s_post1 = _ffn_fwd(h5, ffn_post_norm[1:2], *ffn_w["post", 1], "l1post")
    dh6, dh6_b, d_final, loss_vec = _loss_head(h6, final_norm.reshape(1, d), target, name="loss_head")

    gw = {}

    def grad_send(key, extra=()):
        return [g.reshape(4, 2, -1, d) for g in [gw[item] for item in grad_items[key]] + list(extra)]

    def chip_partials(key, extra=()):
        send = jnp.concatenate(grad_send(key, extra), axis=2)
        from_sibling = _sibling_exchange(send, name=f"grad{key}_sibling_exchange")
        return _pair_add(send, from_sibling, core, name=f"grad{key}_pair_add")

    dh5, dh5_b, gw["post_g", 1], gw["post_u", 1], gw["post_d", 1], d_post1, *_ = _ffn_bwd(
        dh6, dh6_b, s_post1, ffn_post_norm[1:2], *ffn_w["post", 1], "l1post", 1.0)
    dyc = _mm([(dh5_b, w_c_out)], tb=True, tn=1024, name="c_out_dy")
    g_c_out = _mm([(yc, dh5_b)], ta=True, tm=256, tn=1024, out_dtype=BF16, name="c_out_dw")
    gw["c_out", 0] = g_c_out
    dcq, dcf, dci, dcg, dlb, d_onorm, parts_a0 = _hgrn_bwd(pc, o_saved, states, dyc, c_lower_bounds, c_out_norm,
                                                           grad_send("A0"), name="hgrn_bwd_exchange_grads_a0")
    dparts = [dcq, dcf, dci, dcg]
    g_c_in = jnp.concatenate(_mm_shared_rhs(dparts, hn1, tm=256, name="c_in_dw"), axis=0)
    cw = w_c_in.shape[0] // 4
    dhn1 = _mm([(dp, w_c_in[i * cw:(i + 1) * cw]) for i, dp in enumerate(dparts)], tm=512, tn=1024, name="c_in_dx")
    dh4, dh4_b, d_mix1 = _rmsnorm_bwd(h4, mix_norm[1:2], dhn1, dh5, scale=MACARON, name="l1_mix_norm_bwd")
    dh3, dh3_b, gw["pre_g", 1], gw["pre_u", 1], gw["pre_d", 1], d_pre1, *_ = _ffn_bwd(
        dh4, dh4_b, s_pre1, ffn_pre_norm[1:2], *ffn_w["pre", 1], "l1pre", MACARON)
    dh2, dh2_b, gw["post_g", 0], gw["post_u", 0], gw["post_d", 0], d_post0, *_ = _ffn_bwd(
        dh3, dh3_b, s_post0, ffn_post_norm[0:1], *ffn_w["post", 0], "l0post", 1.0)
    dyab = _mm([(dh2_b, w_ab_out)], tb=True, tn=1024, name="ab_out_dy")
    g_ab_out = jnp.concatenate(_mm_shared_rhs([ya, yb], dh2_b, tm=256, name="ab_out_dw"), axis=0)
    dab, dac, dax, g_conv = _conv_bwd(pa, dyab, conv_w, name="conv_bwd")

    gw["c_in", 0], gw["ab_out", 0] = g_c_in, g_ab_out
    dq, dk, dv, parts_a1 = _attn_bwd(pb, dyab, ltot, grad_send("A1"), name="attn_bwd_exchange_grads_a1")
    dparts = [dab, dac, dax, dq, dk, dv]
    g_ab_in = jnp.concatenate(_mm_shared_rhs(dparts, hn0, tm=128, name="ab_in_dw"), axis=0)
    dhn0 = _mm([(dp, w_ab_in[i * aw:(i + 1) * aw]) for i, dp in enumerate(dparts)], tm=512, tn=1024, name="ab_in_dx")
    dh1, dh1_b, d_mix0 = _rmsnorm_bwd(h1, mix_norm[0:1], dhn0, dh2, scale=MACARON, name="l0_mix_norm_bwd")
    gw["ab_in", 0] = g_ab_in
    gconv_own = g_conv.reshape(3, 8, -1).transpose(1, 0, 2).reshape(8, -1)
    conv_piece = jnp.zeros((8, 16, d), F32).at[:, 0, :nconv].set(gconv_own).astype(BF16)

    def send_b0(dwd):
        gw["pre_d", 0] = dwd
        return grad_send("B0")

    def chip_part_b1(dwg, dwu):
        gw["pre_g", 0], gw["pre_u", 0] = dwg, dwu
        return chip_partials("B1")

    dh0, _, _, _, _, d_pre0, (parts_c, parts_b0, parts_b1) = _ffn_bwd(
        dh1, dh1_b, s_pre0, ffn_pre_norm[0:1], *ffn_w["pre", 0], "l0pre", 1.0,
        (chip_partials("C", [conv_piece]), send_b0, chip_part_b1))

    parts = {"A0": parts_a0, "A1": parts_a1, "B0": parts_b0, "B1": parts_b1, "C": parts_c}
    upd = {}
    for tag, w, m, v in big:
        view = (lambda a: jnp.swapaxes(a, 1, 2)) if tag in col_sharded else (lambda a: a)
        where = {layer: (key, grad_offs[key][tag, layer])
                 for key in grad_items for t2, layer in grad_items[key] if t2 == tag}
        res = None
        for layer in sorted(where):
            key, off = where[layer]
            res = _adamw_shard(parts[key], off, view(w), view(m), view(v), layer, res, name=f"adamw_{tag}{layer}")
        upd[tag] = [view(a) for a in res]
    res = _adamw_shard(parts["C"], grad_conv_row, *(conv_rows(a, False)[None] for a in (ab_conv_w, m_ab_conv_w, v_ab_conv_w)),
                       0, None, name="adamw_conv")
    upd["conv"] = [r[0, 0, :nconv].reshape(ab_conv_w.shape) for r in res]

    def small_pack(pre, mix, post, final, lbs, onorm):
        def slot(parts):
            out, r = jnp.zeros((SLOT, d), F32), 0
            for a in (parts if isinstance(parts, tuple) else (parts,)):
                out = out.at[r:r + a.shape[0], :a.shape[1]].set(a)
                r += a.shape[0]
            return out

        return jnp.concatenate([slot(pre), slot(mix), slot(post), slot(final.reshape(1, d)), slot(lbs), slot(onorm)], axis=0)

    d_on = d_onorm.reshape(-1, c_out_norm.shape[1]).sum(axis=0, keepdims=True)
    gsmall = small_pack((d_pre0, d_pre1), (d_mix0, d_mix1), (d_post0, d_post1), d_final, dlb, d_on)
    gsmall_all = _all_gather(gsmall, name="gather_small_grads")
    sres = _small_update(
        gsmall_all,
        small_pack(ffn_pre_norm, mix_norm, ffn_post_norm, final_norm, c_lower_bounds, c_out_norm),
        small_pack(m_ffn_pre_norm, m_mix_norm, m_ffn_post_norm, m_final_norm, m_c_lower_bounds, m_c_out_norm),
        small_pack(v_ffn_pre_norm, v_mix_norm, v_ffn_post_norm, v_final_norm, v_c_lower_bounds, v_c_out_norm),
        name="small_update")

    def small_out(r):
        return {"pre_norm": r[0:2], "mix_norm": r[SLOT:SLOT + 2], "post_norm": r[2 * SLOT:2 * SLOT + 2],
                "final": r[3 * SLOT], "lb": r[ROW_LB:ROW_LB + 2], "onorm": r[5 * SLOT:5 * SLOT + 1, :c_out_norm.shape[1]]}

    small = [small_out(r) for r in sres]
    outs = []
    for k in range(4):
        s = small[k]
        outs += [s["pre_norm"], upd["pre_g"][k], upd["pre_u"][k], upd["pre_d"][k], s["mix_norm"], s["post_norm"],
                 upd["post_g"][k], upd["post_u"][k], upd["post_d"][k], upd["ab_in"][k], upd["conv"][k],
                 upd["ab_out"][k], upd["c_in"][k], s["lb"], s["onorm"], upd["c_out"][k], s["final"]]
    loss = lax.psum(loss_vec[0, 0], ("x", "y", "c"))
    return (loss, dh0[None], *outs)
```

```python
import math

import jax
import jax.numpy as jnp
from jax import lax
from jax.experimental import pallas as pl
from jax.experimental.pallas import tpu as pltpu

F32 = jnp.float32
BF16 = jnp.bfloat16
MESH = pl.DeviceIdType.MESH

RMS_EPS = 1e-6
MACARON = 0.5
LANES = 128
CHUNK = 64
N_LEVELS = 6
HGRN_HEADS = 2
SB_KEYS = 256
ADAM_LR, ADAM_B1, ADAM_B2, ADAM_EPS, ADAM_WD, ADAM_STEP = 0.001, 0.9, 0.999, 1e-08, 0.01, 10
VMEM_LIMIT = 48 * 1024 * 1024


def _cp(**kw):
    return pltpu.CompilerParams(vmem_limit_bytes=VMEM_LIMIT, **kw)


def _sigmoid(x):
    return 0.5 * jnp.tanh(0.5 * x) + 0.5


def _bf(x):
    return x if x.dtype == BF16 else x.astype(BF16)


def _split3(x):
    hi = x.astype(BF16)
    r1 = x - hi.astype(F32)
    mid = r1.astype(BF16)
    lo = (r1 - mid.astype(F32)).astype(BF16)
    return hi, mid, lo


def _dot(a, b, ca=1, cb=0):
    return lax.dot_general(a, b, (((ca,), (cb,)), ((), ())), preferred_element_type=F32)


def _dot_exact_lhs(m, x):
    hi, mid, lo = _split3(x)
    return _dot(m, hi) + _dot(m, mid) + _dot(m, lo)


def _mm(terms, *, name, ta=False, tb=False, out_dtype=F32, residual=None, alpha=1.0, tm=512, tn=512):
    nt = len(terms)
    a0, b0 = terms[0]
    m = a0.shape[1] if ta else a0.shape[0]
    n = b0.shape[0] if tb else b0.shape[1]
    tm, tn = min(tm, m), min(tn, n)
    assert m % tm == 0 and n % tn == 0, (name, m, n, tm, tn)
    has_res = residual is not None

    def body(*refs):
        o_ref = refs[-1]
        acc = None
        for i in range(nt):
            a = _bf(refs[2 * i][...])
            b = _bf(refs[2 * i + 1][...])
            p = _dot(a, b, 0 if ta else 1, 1 if tb else 0)
            acc = p if acc is None else acc + p
        if alpha != 1.0:
            acc = acc * alpha
        if has_res:
            acc = acc + refs[2 * nt][...]
        o_ref[...] = acc.astype(out_dtype)

    in_specs, args = [], []
    for a, b in terms:
        k = a.shape[0] if ta else a.shape[1]
        assert (b.shape[1] if tb else b.shape[0]) == k, (name, a.shape, b.shape)
        in_specs.append(pl.BlockSpec((k, tm), lambda i, j: (0, i)) if ta else pl.BlockSpec((tm, k), lambda i, j: (i, 0)))
        in_specs.append(pl.BlockSpec((tn, k), lambda i, j: (j, 0)) if tb else pl.BlockSpec((k, tn), lambda i, j: (0, j)))
        args += [a, b]
    if has_res:
        in_specs.append(pl.BlockSpec((tm, tn), lambda i, j: (i, j)))
        args.append(residual)
    return pl.pallas_call(
        body, name=name, grid=(m // tm, n // tn), in_specs=in_specs,
        out_specs=pl.BlockSpec((tm, tn), lambda i, j: (i, j)),
        out_shape=jax.ShapeDtypeStruct((m, n), out_dtype), compiler_params=_cp())(*args)


def _norm_proj(x, gain, w_t, out_dtypes, *, name, tm):
    t, d = x.shape
    n = w_t.shape[0]
    tm = min(tm, t)
    npart = len(out_dtypes)
    width = n // npart

    def body(x_ref, g_ref, w_ref, xn_ref, *part_refs):
        xv = x_ref[...]
        rstd = lax.rsqrt(jnp.mean(xv * xv, axis=-1, keepdims=True) + RMS_EPS)
        xn = (xv * rstd * g_ref[...]).astype(BF16)
        xn_ref[...] = xn
        for p, ref in enumerate(part_refs):
            ref[...] = _dot(xn, w_ref[p * width:(p + 1) * width, :], 1, 1).astype(out_dtypes[p])

    row = pl.BlockSpec((tm, d), lambda i: (i, 0))
    return pl.pallas_call(
        body, name=name, grid=(t // tm,),
        in_specs=[row, pl.BlockSpec((1, d), lambda i: (0, 0)), pl.BlockSpec((n, d), lambda i: (0, 0))],
        out_specs=[row] + [pl.BlockSpec((tm, width), lambda i: (i, 0))] * npart,
        out_shape=[jax.ShapeDtypeStruct((t, d), BF16)] + [jax.ShapeDtypeStruct((t, width), dt) for dt in out_dtypes],
        compiler_params=_cp())(x, gain, w_t)


def _mm_shared_rhs(a_list, b, *, name, tm, out_dtype=BF16, send=None):
    k, n = b.shape
    assert all(a.shape[0] == k and a.shape[1] % tm == 0 and a.shape[1] == a_list[0].shape[1] for a in a_list)
    m = a_list[0].shape[1]
    na = len(a_list)
    nsteps = m // tm
    sends = list(send) if send is not None else []
    ns = len(sends)

    def body(*refs):
        first_out = na + 1 + ns
        if ns:
            start, finish = _direct_exchange_phases(refs[na + 1:first_out], refs[first_out + na], *refs[first_out + na + 1:])
            pl.when(pl.program_id(0) == 0)(start)
        bv = refs[na][...]
        for i in range(na):
            refs[first_out + i][...] = _dot(refs[i][...], bv, 0, 0).astype(out_dtype)
        if ns:
            pl.when(pl.program_id(0) == nsteps - 1)(finish)

    return pl.pallas_call(
        body, name=name, grid=(nsteps,),
        in_specs=[pl.BlockSpec((k, tm), lambda i: (0, i))] * na + [pl.BlockSpec((k, n), lambda i: (0, 0))] + [HBM_SPEC] * ns,
        out_specs=[pl.BlockSpec((tm, n), lambda i: (i, 0))] * na + ([HBM_SPEC] if ns else []),
        out_shape=[jax.ShapeDtypeStruct((m, n), out_dtype)] * na
        + ([jax.ShapeDtypeStruct((8, _direct_exchange_rows(sends), sends[0].shape[3]), sends[0].dtype)] if ns else []),
        scratch_shapes=_direct_exchange_scratch() if ns else [],
        compiler_params=_cp(dimension_semantics=("arbitrary",)))(*a_list, b, *sends)


def _rmsnorm_bwd(x, gain, dxn, dres, *, name, scale, tm=512):
    t, d = x.shape
    tm = min(tm, t)

    def body(x_ref, g_ref, dxn_ref, dres_ref, dx_ref, dxb_ref, dg_ref):
        xv = x_ref[...]
        rstd = lax.rsqrt(jnp.mean(xv * xv, axis=-1, keepdims=True) + RMS_EPS)
        xhat = xv * rstd
        dxn_v = dxn_ref[...]
        dxhat = dxn_v * g_ref[...]
        dx = dres_ref[...] + rstd * (dxhat - xhat * jnp.mean(dxhat * xhat, axis=-1, keepdims=True))
        dx_ref[...] = dx
        dxb_ref[...] = (dx * scale).astype(BF16)

        @pl.when(pl.program_id(0) == 0)
        def _():
            dg_ref[...] = jnp.zeros_like(dg_ref)

        dg_ref[...] += jnp.sum(dxn_v * xhat, axis=0, keepdims=True)

    row = pl.BlockSpec((tm, d), lambda i: (i, 0))
    vec = pl.BlockSpec((1, d), lambda i: (0, 0))
    return pl.pallas_call(
        body, name=name, grid=(t // tm,), in_specs=[row, vec, row, row], out_specs=[row, row, vec],
        out_shape=[jax.ShapeDtypeStruct((t, d), F32), jax.ShapeDtypeStruct((t, d), BF16), jax.ShapeDtypeStruct((1, d), F32)],
        compiler_params=_cp())(x, gain, dxn, dres)


def _loss_head(h, gain, target, *, name, tm=512):
    t, d = h.shape
    tm = min(tm, t)

    def body(h_ref, g_ref, t_ref, dh_ref, dhb_ref, dg_ref, loss_ref):
        hv = h_ref[...]
        rstd = lax.rsqrt(jnp.mean(hv * hv, axis=-1, keepdims=True) + RMS_EPS)
        xhat = hv * rstd
        err = xhat * g_ref[...] - t_ref[...]
        dy = err * (1.0 / d)
        dxhat = dy * g_ref[...]
        dh = rstd * (dxhat - xhat * jnp.mean(dxhat * xhat, axis=-1, keepdims=True))
        dh_ref[...] = dh
        dhb_ref[...] = (dh * MACARON).astype(BF16)

        @pl.when(pl.program_id(0) == 0)
        def _():
            dg_ref[...] = jnp.zeros_like(dg_ref)
            loss_ref[...] = jnp.zeros_like(loss_ref)

        dg_ref[...] += jnp.sum(dy * xhat, axis=0, keepdims=True)
        part = jnp.sum(jnp.sum(err * err, axis=-1, keepdims=True), axis=0, keepdims=True) * (0.5 / d)
        loss_ref[...] += jnp.broadcast_to(part, loss_ref.shape)

    row = pl.BlockSpec((tm, d), lambda i: (i, 0))
    vec = pl.BlockSpec((1, d), lambda i: (0, 0))
    return pl.pallas_call(
        body, name=name, grid=(t // tm,), in_specs=[row, vec, row],
        out_specs=[row, row, vec, pl.BlockSpec((1, LANES), lambda i: (0, 0))],
        out_shape=[jax.ShapeDtypeStruct((t, d), F32), jax.ShapeDtypeStruct((t, d), BF16), jax.ShapeDtypeStruct((1, d), F32),
                   jax.ShapeDtypeStruct((1, LANES), F32)],
        compiler_params=_cp())(h, gain, target)


def _norm_gate_up(x, gain, wg, wu, *, name, tm=256, tf=2816, pack=None, seg_rows=()):
    t, d = x.shape
    f = wg.shape[0]
    tm, tf = min(tm, t), min(tf, f)
    assert f % tf == 0
    ni, nj = t // tm, f // tf
    nseg = len(seg_rows)

    def body(x_ref, g_ref, wg_ref, wu_ref, *rest):
        if pack is not None:
            pack_ref, xn_ref, gg_ref, uu_ref, act_ref = rest[:5]
            start, forward, finish = _gather_phases(pack_ref, rest[5:5 + nseg], seg_rows, *rest[5 + nseg:])
            step = pl.program_id(0) * nj + pl.program_id(1)
            pl.when(step == 0)(start)
            pl.when(step == (3 * ni * nj) // 4)(forward)
        else:
            xn_ref, gg_ref, uu_ref, act_ref = rest

        @pl.when(pl.program_id(1) == 0)
        def _():
            xv = x_ref[...]
            rstd = lax.rsqrt(jnp.mean(xv * xv, axis=-1, keepdims=True) + RMS_EPS)
            xn_ref[...] = (xv * rstd * g_ref[...]).astype(BF16)

        xn = xn_ref[...]
        gv = _dot(xn, wg_ref[...], 1, 1)
        uv = _dot(xn, wu_ref[...], 1, 1)
        gg_ref[...] = gv.astype(BF16)
        uu_ref[...] = uv.astype(BF16)
        act_ref[...] = (gv * _sigmoid(gv) * uv).astype(BF16)
        if pack is not None:
            pl.when(step == ni * nj - 1)(finish)

    row = pl.BlockSpec((tm, d), lambda i, j: (i, 0))
    wsp = pl.BlockSpec((tf, d), lambda i, j: (j, 0))
    osp = pl.BlockSpec((tm, tf), lambda i, j: (i, j))
    fused = pack is not None
    return pl.pallas_call(
        body, name=name, grid=(ni, nj),
        in_specs=[row, pl.BlockSpec((1, d), lambda i, j: (0, 0)), wsp, wsp] + ([HBM_SPEC] if fused else []),
        out_specs=[row, osp, osp, osp] + [HBM_SPEC] * nseg,
        out_shape=[jax.ShapeDtypeStruct((t, d), BF16)] + [jax.ShapeDtypeStruct((t, f), BF16)] * 3
        + [jax.ShapeDtypeStruct((8, n, d), BF16) for n in seg_rows],
        scratch_shapes=_gather_scratch() if fused else [],
        compiler_params=_cp(dimension_semantics=("arbitrary", "arbitrary")))(x, gain, wg, wu, *([pack] if fused else []))


def _swiglu_bwd(dout, wd, gg, uu, chip_part=None, *, name, tm=512, tf=1408):
    t, d = dout.shape
    f = wd.shape[0]
    tm, tf = min(tm, t), min(tf, f)
    nj, ni = f // tf, t // tm
    fused = chip_part is not None

    def body(do_ref, wd_ref, g_ref, u_ref, *rest):
        if fused:
            part_ref, dg_ref, du_ref, parts_ref = rest[:4]
            start, finish = _chip_exchange_phases(part_ref, parts_ref, *rest[4:])
            step = pl.program_id(0) * ni + pl.program_id(1)
            pl.when(step == 0)(start)
        else:
            dg_ref, du_ref = rest
        dact = _dot(do_ref[...], wd_ref[...], 1, 1)
        gv = g_ref[...].astype(F32)
        uv = u_ref[...].astype(F32)
        sg = _sigmoid(gv)
        dg_ref[...] = (dact * uv * (sg * (1.0 + gv * (1.0 - sg)))).astype(BF16)
        du_ref[...] = (dact * (gv * sg)).astype(BF16)
        if fused:
            pl.when(step == nj * ni - 1)(finish)

    osp = pl.BlockSpec((tm, tf), lambda j, i: (i, j))
    return pl.pallas_call(
        body, name=name, grid=(nj, ni),
        in_specs=[pl.BlockSpec((tm, d), lambda j, i: (i, 0)), pl.BlockSpec((tf, d), lambda j, i: (j, 0)), osp, osp]
        + ([HBM_SPEC] if fused else []),
        out_specs=[osp, osp] + ([HBM_SPEC] if fused else []),
        out_shape=[jax.ShapeDtypeStruct((t, f), BF16)] * 2
        + ([jax.ShapeDtypeStruct(chip_part.shape, chip_part.dtype)] if fused else []),
        scratch_shapes=_chip_exchange_scratch() if fused else [],
        compiler_params=_cp(dimension_semantics=("arbitrary", "arbitrary")))(dout, wd, gg, uu, *([chip_part] if fused else []))


def _shift_down(x, n):
    rows = lax.broadcasted_iota(jnp.int32, x.shape, 0)
    return jnp.where(rows >= n, pltpu.roll(x, n, 0), 0.0)


def _shift_up(x, n):
    t = x.shape[0]
    rows = lax.broadcasted_iota(jnp.int32, x.shape, 0)
    return jnp.where(rows < t - n, pltpu.roll(x, t - n, 0), 0.0)


def _conv_fwd(pa, conv_w, *, name):
    t = pa.shape[0]
    nb = pa.shape[1] // 3 // LANES

    def body(b_ref, c_ref, x_ref, w_ref, y_ref):
        u = c_ref[...] * x_ref[...]
        w = w_ref[...]
        conv = w[2:3, :] * u + w[1:2, :] * _shift_down(u, 1) + w[0:1, :] * _shift_down(u, 2)
        y_ref[...] = (b_ref[...] * conv).astype(BF16)

    def col(off):
        return pl.BlockSpec((t, LANES), lambda j: (0, off + j))

    return pl.pallas_call(
        body, name=name, grid=(nb,),
        in_specs=[col(0), col(nb), col(2 * nb), pl.BlockSpec((3, LANES), lambda j: (0, j))],
        out_specs=pl.BlockSpec((t, LANES), lambda j: (0, j)),
        out_shape=jax.ShapeDtypeStruct((t, nb * LANES), BF16), compiler_params=_cp())(pa, pa, pa, conv_w)


def _conv_bwd(pa, dy, conv_w, *, name):
    t = pa.shape[0]
    nb = pa.shape[1] // 3 // LANES

    def body(b_ref, c_ref, x_ref, dy_ref, w_ref, db_ref, dc_ref, dx_ref, dw_ref):
        cv, xv = c_ref[...], x_ref[...]
        u = cv * xv
        u1, u2 = _shift_down(u, 1), _shift_down(u, 2)
        w = w_ref[...]
        conv = w[2:3, :] * u + w[1:2, :] * u1 + w[0:1, :] * u2
        dyv = dy_ref[...]
        db_ref[...] = (dyv * conv).astype(BF16)
        dconv = dyv * b_ref[...]
        du = w[2:3, :] * dconv + w[1:2, :] * _shift_up(dconv, 1) + w[0:1, :] * _shift_up(dconv, 2)
        dc_ref[...] = (du * xv).astype(BF16)
        dx_ref[...] = (du * cv).astype(BF16)
        dw_ref[0:1, :] = jnp.sum(dconv * u2, axis=0, keepdims=True)
        dw_ref[1:2, :] = jnp.sum(dconv * u1, axis=0, keepdims=True)
        dw_ref[2:3, :] = jnp.sum(dconv * u, axis=0, keepdims=True)

    def col(off):
        return pl.BlockSpec((t, LANES), lambda j: (0, off + j))

    osp = pl.BlockSpec((t, LANES), lambda j: (0, j))
    wsp = pl.BlockSpec((3, LANES), lambda j: (0, j))
    return pl.pallas_call(
        body, name=name, grid=(nb,), in_specs=[col(0), col(nb), col(2 * nb), col(0), wsp],
        out_specs=[osp, osp, osp, wsp],
        out_shape=[jax.ShapeDtypeStruct((t, nb * LANES), BF16)] * 3 + [jax.ShapeDtypeStruct((3, nb * LANES), F32)],
        compiler_params=_cp())(pa, pa, pa, dy, conv_w)


def _sb_consts():
    j = lax.broadcasted_iota(jnp.int32, (SB_KEYS, SB_KEYS), 0)
    s = lax.broadcasted_iota(jnp.int32, (SB_KEYS, SB_KEYS), 1)
    after = (j > s).astype(BF16)
    upto = (j <= s).astype(BF16)
    before = (j < s).astype(BF16)
    return after, jnp.stack([upto, before])


def _log_sigmoid(z):
    return jnp.minimum(z, 0.0) - jnp.log(1.0 + jnp.exp(-jnp.abs(z)))


def _attn_fwd(pb, late_pack, seg_rows, *, name, tq=256):
    t = pb.shape[0]
    npair = pb.shape[1] // 3 // LANES
    tq = min(tq, t)
    nq = t // tq
    cmat, _ = _sb_consts()
    scale = 1.0 / math.sqrt(LANES // 2)

    nseg = len(seg_rows)

    def body(q_ref, k_ref, v_ref, c_ref, late_ref, y_ref, lt_ref, *rest):
        i = pl.program_id(1)
        pair = pl.program_id(0)
        scratch = rest[nseg:nseg + 4]
        start, forward, finish = _gather_phases(late_ref, rest[:nseg], seg_rows, *rest[nseg + 4:])
        pl.when((pair == 0) & (i == 0))(start)
        pl.when((pair == npair - 1) & (i == nq // 2))(forward)
        lane = lax.broadcasted_iota(jnp.int32, (tq, LANES), 1)
        rowpos = i * tq + lax.broadcasted_iota(jnp.int32, (tq, SB_KEYS), 0)
        colid = lax.broadcasted_iota(jnp.int32, (tq, SB_KEYS), 1)
        q2 = q_ref[...] * jnp.asarray(scale, BF16)
        cm = c_ref[...]
        hi_lanes = lane >= LANES // 2
        qhs = [jnp.where(hi_lanes == (hh == 1), q2, jnp.zeros_like(q2)) for hh in range(2)]
        per_q = tq // SB_KEYS

        def blk(jb):
            return pl.ds(pl.multiple_of(jb * SB_KEYS, SB_KEYS), SB_KEYS)

        zbuf, wbuf, accbuf, runbuf = scratch

        def scores(jb):
            kb = k_ref[blk(jb), :]
            for hh in range(2):
                zbuf[hh] = _dot(qhs[hh], kb, 1, 1)

        def values(jb):
            vb = v_ref[blk(jb), :]
            for hh in range(2):
                accbuf[hh] += _dot(wbuf[hh], vb)

        def trip(jb, masked, first=False):
            mask = (jb * SB_KEYS + colid) < rowpos if masked else None
            if not first:
                values(jb + 1)
            pre, css = [], []
            for hh in range(2):
                z = zbuf[hh]
                lb = _log_sigmoid(z)
                lk = lb - z
                if masked:
                    lk = jnp.where(mask, lk, 0.0)
                lk_hi, lk_lo = _split2(lk)
                css.append(_dot(lk_hi, cm) + _dot(lk_lo, cm))
                run = runbuf[hh]
                pre.append(lb + run)
                runbuf[hh] = run + jnp.sum(lk, axis=1, keepdims=True)
            scores(jnp.maximum(jb - 1, 0))
            for hh in range(2):
                w = jnp.exp(pre[hh] + css[hh])
                if masked:
                    w = jnp.where(mask, w, 0.0)
                wbuf[hh] = w.astype(BF16)

        nfull = i * per_q
        accbuf[...] = jnp.zeros_like(accbuf)
        runbuf[...] = jnp.zeros_like(runbuf)
        scores(nfull + per_q - 1)
        for dblk in reversed(range(per_q)):
            trip(nfull + dblk, True, first=dblk == per_q - 1)

        def full_block(n, carry):
            trip(nfull - 1 - n, False)
            return carry

        lax.fori_loop(0, nfull, full_block, 0)
        values(0)
        y_ref[...] = jnp.where(hi_lanes, accbuf[1], accbuf[0]).astype(BF16)
        lt_ref[...] = jnp.where(hi_lanes, runbuf[1], runbuf[0])
        pl.when((pair == npair - 1) & (i == nq - 1))(finish)

    return pl.pallas_call(
        body, name=name, grid=(npair, nq),
        in_specs=[pl.BlockSpec((tq, LANES), lambda p, i: (i, p)),
                  pl.BlockSpec((t, LANES), lambda p, i: (0, npair + p)),
                  pl.BlockSpec((t, LANES), lambda p, i: (0, 2 * npair + p)),
                  pl.BlockSpec((SB_KEYS, SB_KEYS), lambda p, i: (0, 0)),
                  HBM_SPEC],
        out_specs=[pl.BlockSpec((tq, LANES), lambda p, i: (i, p))] * 2 + [HBM_SPEC] * nseg,
        out_shape=[jax.ShapeDtypeStruct((t, npair * LANES), BF16), jax.ShapeDtypeStruct((t, npair * LANES), F32),
                   ] + [jax.ShapeDtypeStruct((8, n, late_pack.shape[1]), late_pack.dtype) for n in seg_rows],
        scratch_shapes=[pltpu.VMEM((2, tq, SB_KEYS), F32), pltpu.VMEM((2, tq, SB_KEYS), BF16),
                        pltpu.VMEM((2, tq, LANES), F32), pltpu.VMEM((2, tq, 1), F32)] + _gather_scratch(),
        compiler_params=_cp(dimension_semantics=("arbitrary", "arbitrary")))(pb, pb, pb, cmat, late_pack)


def _attn_bwd(pb, dy, ltot, send, *, name, tq=256):
    t = pb.shape[0]
    npair = pb.shape[1] // 3 // LANES
    tq = min(tq, t)
    nq = t // tq
    _, cmats = _sb_consts()
    scale = 1.0 / math.sqrt(LANES // 2)
    sends = list(send)
    ns = len(sends)

    def body(q_ref, k_ref, v_ref, dy_ref, lt_ref, c_ref, *rest):
        i = pl.program_id(1)
        pair = pl.program_id(0)
        send_refs = rest[:ns]
        dq_ref, dk_ref, dv_ref, parts_ref, dk_acc, dv_acc = rest[ns:ns + 6]
        scratch = rest[ns + 6:ns + 12]
        start, finish = _direct_exchange_phases(send_refs, parts_ref, *rest[ns + 12:])
        pl.when((pair == 0) & (i == 0))(start)

        @pl.when(i == 0)
        def _():
            dk_acc[...] = jnp.zeros_like(dk_acc)
            dv_acc[...] = jnp.zeros_like(dv_acc)

        lane = lax.broadcasted_iota(jnp.int32, (tq, LANES), 1)
        rowpos = i * tq + lax.broadcasted_iota(jnp.int32, (tq, SB_KEYS), 0)
        colid = lax.broadcasted_iota(jnp.int32, (tq, SB_KEYS), 1)
        q2 = q_ref[...] * jnp.asarray(scale, BF16)
        do2 = dy_ref[...].astype(BF16)
        ltv = lt_ref[...]
        c_upto, c_before = c_ref[0], c_ref[1]
        hi_lanes = lane >= LANES // 2
        sels = [hi_lanes == (hh == 1) for hh in range(2)]
        qhs = [jnp.where(s, q2, jnp.zeros_like(q2)) for s in sels]
        dohs = [jnp.where(s, do2, jnp.zeros_like(do2)) for s in sels]
        lts = [ltv[:, 0:1], ltv[:, LANES // 2:LANES // 2 + 1]]
        per_q = tq // SB_KEYS

        def blk(jb):
            return pl.ds(pl.multiple_of(jb * SB_KEYS, SB_KEYS), SB_KEYS)

        zbuf, dabuf, dzbuf, abuf, dqbuf, sumbuf = scratch

        def scores(jb):
            kb, vb = k_ref[blk(jb), :], v_ref[blk(jb), :]
            for hh in range(2):
                zbuf[hh] = _dot(qhs[hh], kb, 1, 1)
                dabuf[hh] = _dot(dohs[hh], vb, 1, 1)

        def products(jb):
            kb = k_ref[blk(jb), :]
            dk_acc[blk(jb), :] += _dot(dzbuf[0], qhs[0], 0, 0) + _dot(dzbuf[1], qhs[1], 0, 0)
            dv_acc[blk(jb), :] += _dot(abuf[0], dohs[0], 0, 0) + _dot(abuf[1], dohs[1], 0, 0)
            for hh in range(2):
                dqbuf[hh] += _dot(dzbuf[hh], kb)

        def trip(jb, masked):
            mask = (jb * SB_KEYS + colid) < rowpos if masked else None
            products(jnp.maximum(jb - 1, 0))
            lbs, css, es, ces = [], [], [], []
            for hh in range(2):
                z = zbuf[hh]
                lb = _log_sigmoid(z)
                lk = lb - z
                if masked:
                    lk = jnp.where(mask, lk, 0.0)
                lk_hi, lk_lo = _split2(lk)
                css.append(_dot(lk_hi, c_upto) + _dot(lk_lo, c_upto))
                csum = sumbuf[2 * hh]
                lbs.append((lb, lb + (lts[hh] - csum)))
                sumbuf[2 * hh] = csum + jnp.sum(lk, axis=1, keepdims=True)
            for hh in range(2):
                a = jnp.exp(lbs[hh][1] - css[hh])
                if masked:
                    a = jnp.where(mask, a, 0.0)
                e = a * dabuf[hh]
                e_hi, e_lo = _split2(e)
                ces.append(_dot(e_hi, c_before) + _dot(e_lo, c_before))
                abuf[hh] = a.astype(BF16)
                es.append(e)
            scores(jnp.minimum(jb + 1, last))
            for hh in range(2):
                prun = sumbuf[2 * hh + 1]
                beta = jnp.exp(lbs[hh][0])
                dz = es[hh] * (1.0 - beta) - (prun + ces[hh]) * beta
                if masked:
                    dz = jnp.where(mask, dz, 0.0)
                dzbuf[hh] = dz.astype(BF16)
                sumbuf[2 * hh + 1] = prun + jnp.sum(es[hh], axis=1, keepdims=True)

        nfull = i * per_q
        last = nfull + per_q - 1
        for buf in (dzbuf, abuf, dqbuf, sumbuf):
            buf[...] = jnp.zeros_like(buf)
        scores(0)

        def full_block(jb, carry):
            trip(jb, False)
            return carry

        lax.fori_loop(0, nfull, full_block, 0)
        for dblk in range(per_q):
            trip(nfull + dblk, True)
        products(last)
        dq_ref[...] = (jnp.where(hi_lanes, dqbuf[1], dqbuf[0]) * scale).astype(BF16)

        @pl.when(i == nq - 1)
        def _():
            dk_ref[...] = dk_acc[...].astype(BF16)
            dv_ref[...] = dv_acc[...].astype(BF16)

        pl.when((pair == npair - 1) & (i == nq - 1))(finish)

    blk = pl.BlockSpec((tq, LANES), lambda p, i: (i, p))
    full = pl.BlockSpec((t, LANES), lambda p, i: (0, p))
    return pl.pallas_call(
        body, name=name, grid=(npair, nq),
        in_specs=[blk,
                  pl.BlockSpec((t, LANES), lambda p, i: (0, npair + p)),
                  pl.BlockSpec((t, LANES), lambda p, i: (0, 2 * npair + p)),
                  pl.BlockSpec((tq, LANES), lambda p, i: (i, npair + p)),
                  blk,
                  pl.BlockSpec((2, SB_KEYS, SB_KEYS), lambda p, i: (0, 0, 0))] + [HBM_SPEC] * ns,
        out_specs=[blk, full, full, HBM_SPEC],
        out_shape=[jax.ShapeDtypeStruct((t, npair * LANES), BF16)] * 3
        + [jax.ShapeDtypeStruct((8, _direct_exchange_rows(sends), sends[0].shape[3]), sends[0].dtype)],
        scratch_shapes=[pltpu.VMEM((t, LANES), F32), pltpu.VMEM((t, LANES), F32),
                        pltpu.VMEM((2, tq, SB_KEYS), F32), pltpu.VMEM((2, tq, SB_KEYS), F32),
                        pltpu.VMEM((2, tq, SB_KEYS), BF16), pltpu.VMEM((2, tq, SB_KEYS), BF16),
                        pltpu.VMEM((2, tq, LANES), F32), pltpu.VMEM((4, tq, 1), F32)] + _direct_exchange_scratch(),
        compiler_params=_cp(dimension_semantics=("arbitrary", "arbitrary")))(pb, pb, pb, dy, ltot, cmats, *sends)


def _hgrn_consts():
    t = lax.broadcasted_iota(jnp.int32, (CHUNK, CHUNK), 0)
    s = lax.broadcasted_iota(jnp.int32, (CHUNK, CHUNK), 1)
    masks = []
    for lvl in range(N_LEVELS):
        half = CHUNK >> (lvl + 1)
        same = (t // (2 * half)) == (s // (2 * half))
        masks.append((same & (t % (2 * half) >= half) & (s % (2 * half) < half)).astype(F32))
    masks.append((t == s).astype(F32))
    prefix = (s <= t).astype(BF16)
    suffix = (s >= t).astype(BF16)
    return prefix, jnp.stack(masks), suffix


def _hgrn_gates(qr, fr, lbv):
    sg = 1.0 / (1.0 + jnp.exp(-fr))
    fval = lbv + (1.0 - lbv) * sg
    kk = (1.0 - lbv) * (1.0 / (1.0 + jnp.exp(fr)))
    sq = _sigmoid(qr)
    return sg, fval, jnp.log(fval), kk, sq, qr * sq


def _lower_bound(c_ref):
    c = c_ref[...]
    mx = jnp.max(c, axis=0, keepdims=True)
    ex = jnp.exp(c - mx)
    return ex[1:2, :] / jnp.sum(ex, axis=0, keepdims=True)


def _level_ref(b, lvl):
    half = CHUNK >> (lvl + 1)
    seg = 2 * half
    if seg >= 8:
        b3 = b.reshape(CHUNK // seg, seg, LANES)
        return jnp.broadcast_to(b3[:, half - 1:half, :], b3.shape).reshape(CHUNK, LANES)
    pos = lax.broadcasted_iota(jnp.int32, b.shape, 0) % seg
    out = b
    for p in range(seg):
        if p != half - 1:
            out = jnp.where(pos == p, pltpu.roll(b, (p - (half - 1)) % CHUNK, 0), out)
    return out


def _hgrn_levels(b, qs, kk):
    out = []
    for lvl in range(N_LEVELS):
        fac = jnp.exp(-jnp.abs(b - _level_ref(b, lvl)))
        out.append((qs * fac, kk * fac, fac, fac))
    out.append((qs, kk, None, None))
    return out


def _split2(x):
    hi = x.astype(BF16)
    return hi, (x - hi.astype(F32)).astype(BF16)


def _hgrn_fwd(pc, c_lb, out_norm, *, name, tc=1024):
    t = pc.shape[0]
    nh = pc.shape[1] // 4 // LANES
    tc = min(tc, t)
    nch = tc // CHUNK
    cum_all, masks, _ = _hgrn_consts()

    def body(q_ref, f_ref, i_ref, g_ref, lb_ref, on_ref, cum_ref, m_ref, y_ref, o_ref, st_ref, state):
        @pl.when(pl.program_id(1) == 0)
        def _():
            state[...] = jnp.zeros_like(state)

        lbv = _lower_bound(lb_ref)
        onv = on_ref[...]

        def chunk(c, carry):
            rows = pl.ds(pl.multiple_of(c * CHUNK, CHUNK), CHUNK)
            for hh in range(HGRN_HEADS):
                lanes = slice(hh * LANES, (hh + 1) * LANES)
                _, _, g, kk, _, qs = _hgrn_gates(q_ref[rows, lanes], f_ref[rows, lanes], lbv[:, lanes])
                vb = i_ref[rows, lanes].astype(BF16)
                b = _dot_exact_lhs(cum_ref[...], g)
                scores = jnp.zeros((CHUNK, CHUNK), F32)
                for lvl, (ql, kl, _, _) in enumerate(_hgrn_levels(b, qs, kk)):
                    scores = scores + _dot(ql.astype(BF16), kl.astype(BF16), 1, 1) * m_ref[lvl]
                st = state[hh]
                st_ref[hh, c] = st
                o = _dot(scores.astype(BF16), vb) + _dot((qs * jnp.exp(b)).astype(BF16), st.astype(BF16), 1, 1)
                blast = b[CHUNK - 1:CHUNK, :]
                kdec = (kk * jnp.exp(blast - b)).astype(BF16)
                state[hh] = st * jnp.exp(blast) + _dot(vb, kdec, 0, 0)
                o_ref[rows, lanes] = o
                rstd = lax.rsqrt(jnp.mean(o * o, axis=-1, keepdims=True) + RMS_EPS)
                gate = g_ref[rows, lanes]
                y_ref[rows, lanes] = (o * rstd * onv * (gate * _sigmoid(gate))).astype(BF16)
            return carry

        lax.fori_loop(0, nch, chunk, 0, unroll=2)

    hw = HGRN_HEADS * LANES

    def col(off):
        return pl.BlockSpec((tc, hw), lambda h, i: (i, off // HGRN_HEADS + h))

    osp = pl.BlockSpec((tc, hw), lambda h, i: (i, h))
    return pl.pallas_call(
        body, name=name, grid=(nh // HGRN_HEADS, t // tc),
        in_specs=[col(0), col(nh), col(2 * nh), col(3 * nh),
                  pl.BlockSpec((2, hw), lambda h, i: (0, h)),
                  pl.BlockSpec((1, LANES), lambda h, i: (0, 0)),
                  pl.BlockSpec(cum_all.shape, lambda h, i: (0, 0)),
                  pl.BlockSpec(masks.shape, lambda h, i: (0, 0, 0))],
        out_specs=[osp, osp, pl.BlockSpec((HGRN_HEADS, nch, LANES, LANES), lambda h, i: (h, i, 0, 0))],
        out_shape=[jax.ShapeDtypeStruct((t, nh * LANES), BF16), jax.ShapeDtypeStruct((t, nh * LANES), F32),
                   jax.ShapeDtypeStruct((nh, t // CHUNK, LANES, LANES), F32)],
        scratch_shapes=[pltpu.VMEM((HGRN_HEADS, LANES, LANES), F32)],
        compiler_params=_cp())(pc, pc, pc, pc, c_lb, out_norm, cum_all, masks)


def _hgrn_bwd(pc, o_saved, states, dy, c_lb, out_norm, send, *, name, tc=1024):
    t = pc.shape[0]
    nh = pc.shape[1] // 4 // LANES
    tc = min(tc, t)
    nch = tc // CHUNK
    nt = t // tc
    cum_all, masks, suffix = _hgrn_consts()
    ngroup = nh // HGRN_HEADS
    sends = list(send)
    ns = len(sends)

    def body(q_ref, f_ref, i_ref, g_ref, o_ref, st_ref, dy_ref, lb_ref, on_ref, cum_ref, m_ref, suf_ref, *rest):
        send_refs = rest[:ns]
        dq_ref, df_ref, di_ref, dg_ref, dlb_ref, don_ref, parts_ref, dstate = rest[ns:ns + 8]
        start, finish = _direct_exchange_phases(send_refs, parts_ref, *rest[ns + 8:])
        pl.when((pl.program_id(0) == 0) & (pl.program_id(1) == 0))(start)

        @pl.when(pl.program_id(1) == 0)
        def _():
            dstate[...] = jnp.zeros_like(dstate)
            dlb_ref[...] = jnp.zeros_like(dlb_ref)
            don_ref[...] = jnp.zeros_like(don_ref)

        lbv = _lower_bound(lb_ref)
        onv = on_ref[...]

        def head(hh, c, rows):
            lanes = slice(hh * LANES, (hh + 1) * LANES)
            qr = q_ref[rows, lanes]
            sg, fval, g, kk, sq, qs = _hgrn_gates(qr, f_ref[rows, lanes], lbv[:, lanes])
            vb = i_ref[rows, lanes].astype(BF16)
            o = o_ref[rows, lanes]
            gate = g_ref[rows, lanes]
            sgt = _sigmoid(gate)
            rstd = lax.rsqrt(jnp.mean(o * o, axis=-1, keepdims=True) + RMS_EPS)
            ohat = o * rstd
            dyv = dy_ref[rows, lanes]
            don = dyv * (gate * sgt)
            dg_ref[rows, lanes] = (dyv * ohat * onv * (sgt * (1.0 + gate * (1.0 - sgt)))).astype(BF16)
            don_ref[:, lanes] += jnp.sum(don * ohat, axis=0, keepdims=True)
            dxhat = don * onv
            dob = (rstd * (dxhat - ohat * jnp.mean(dxhat * ohat, axis=-1, keepdims=True))).astype(BF16)
            b = _dot_exact_lhs(cum_ref[...], g)
            blast = b[CHUNK - 1:CHUNK, :]
            eb = jnp.exp(b)
            edec = jnp.exp(blast - b)
            st32 = st_ref[hh, c]
            st = st32.astype(BF16)
            dst = dstate[hh]
            dstb = dst.astype(BF16)
            da = _dot(dob, vb, 1, 1)
            levels = _hgrn_levels(b, qs, kk)
            scores = jnp.zeros((CHUNK, CHUNK), F32)
            dq = eb * _dot(dob, st)
            dk_inter = edec * _dot(vb, dstb)
            dk = dk_inter
            for lvl, (ql, kl, eq, ek) in enumerate(levels):
                mk = m_ref[lvl]
                (qh, qlo), (kh, klo) = _split2(ql), _split2(kl)
                scores = scores + _dot(qh, kh, 1, 1) * mk
                dal = (da * mk).astype(BF16)
                dql = _dot(dal, kh) + _dot(dal, klo)
                dkl = _dot(dal, qh, 0, 0) + _dot(dal, qlo, 0, 0)
                dq = dq + (dql if eq is None else dql * eq)
                dk = dk + (dkl if ek is None else dkl * ek)
            kdec = (kk * edec).astype(BF16)
            dv = _dot(scores.astype(BF16), dob, 0, 0) + _dot(kdec, dstb, 1, 1)
            dstate[hh] = dst * jnp.exp(blast) + _dot(dob, (qs * eb).astype(BF16), 0, 0)
            db = qs * dq - kk * dk
            last = jnp.sum(kk * dk_inter, axis=0, keepdims=True) + jnp.exp(blast) * jnp.sum(dst * st32, axis=0, keepdims=True)
            dgl = _dot_exact_lhs(suf_ref[...], db) + last
            dfv = dgl / fval - dk
            df_ref[rows, lanes] = (dfv * (1.0 - lbv[:, lanes]) * sg * (1.0 - sg)).astype(BF16)
            dlb_ref[:, lanes] += jnp.sum(dfv * (1.0 - sg), axis=0, keepdims=True)
            dq_ref[rows, lanes] = (dq * (sq * (1.0 + qr * (1.0 - sq)))).astype(BF16)
            di_ref[rows, lanes] = dv.astype(BF16)

        def chunk(n, carry):
            c = nch - 1 - n
            rows = pl.ds(pl.multiple_of(c * CHUNK, CHUNK), CHUNK)
            for hh in range(HGRN_HEADS):
                head(hh, c, rows)
            return carry

        lax.fori_loop(0, nch, chunk, 0, unroll=2)
        pl.when((pl.program_id(0) == ngroup - 1) & (pl.program_id(1) == nt - 1))(finish)

    hw = HGRN_HEADS * LANES

    def col(off):
        return pl.BlockSpec((tc, hw), lambda h, i: (nt - 1 - i, off // HGRN_HEADS + h))

    osp = pl.BlockSpec((tc, hw), lambda h, i: (nt - 1 - i, h))
    vec = pl.BlockSpec((1, hw), lambda h, i: (0, h))
    return pl.pallas_call(
        body, name=name, grid=(nh // HGRN_HEADS, nt),
        in_specs=[col(0), col(nh), col(2 * nh), col(3 * nh), osp,
                  pl.BlockSpec((HGRN_HEADS, nch, LANES, LANES), lambda h, i: (h, nt - 1 - i, 0, 0)),
                  osp,
                  pl.BlockSpec((2, hw), lambda h, i: (0, h)),
                  pl.BlockSpec((1, LANES), lambda h, i: (0, 0)),
                  pl.BlockSpec(cum_all.shape, lambda h, i: (0, 0)),
                  pl.BlockSpec(masks.shape, lambda h, i: (0, 0, 0)),
                  pl.BlockSpec(suffix.shape, lambda h, i: (0, 0))] + [HBM_SPEC] * ns,
        out_specs=[osp, osp, osp, osp, vec, vec, HBM_SPEC],
        out_shape=[jax.ShapeDtypeStruct((t, nh * LANES), BF16)] * 4 + [jax.ShapeDtypeStruct((1, nh * LANES), F32)] * 2
        + [jax.ShapeDtypeStruct((8, _direct_exchange_rows(sends), sends[0].shape[3]), sends[0].dtype)],
        scratch_shapes=[pltpu.VMEM((HGRN_HEADS, LANES, LANES), F32)] + _direct_exchange_scratch(),
        compiler_params=_cp(dimension_semantics=("arbitrary", "arbitrary")))(
            pc, pc, pc, pc, o_saved, states, dy, c_lb, out_norm, cum_all, masks, suffix, *sends)


HBM_SPEC = pl.BlockSpec(memory_space=pltpu.HBM)


def _gather_scratch():
    return [pltpu.SemaphoreType.DMA((7,)), pltpu.SemaphoreType.DMA((7,)), pltpu.SemaphoreType.DMA]


def _gather_phases(x_ref, out_refs, seg_rows, send_sems, recv_sems, local_sem):
    x, y, c = lax.axis_index("x"), lax.axis_index("y"), lax.axis_index("c")
    me, sibling = (x, y, c), (x, y, 1 - c)
    chips = [(1 - x, y), (x, 1 - y), (1 - x, 1 - y)]
    offs = [sum(seg_rows[:s]) for s in range(len(seg_rows))]
    assert sum(seg_rows) == x_ref.shape[0]

    def index(px, py, pc):
        return 4 * px + 2 * py + pc

    def copies(k, block, to, own):
        return [pltpu.make_async_remote_copy(
            src_ref=x_ref.at[pl.ds(offs[s], n)] if own else out_refs[s].at[index(*block)],
            dst_ref=out_refs[s].at[index(*block)],
            send_sem=send_sems.at[k], recv_sem=recv_sems.at[k], device_id=to, device_id_type=MESH)
            for s, n in enumerate(seg_rows)]

    def all_bytes(k):
        return pltpu.make_async_remote_copy(src_ref=x_ref, dst_ref=x_ref, send_sem=send_sems.at[k],
                                            recv_sem=recv_sems.at[k], device_id=me, device_id_type=MESH)

    mine = [pltpu.make_async_copy(x_ref.at[pl.ds(offs[s], n)], out_refs[s].at[index(*me)], local_sem)
            for s, n in enumerate(seg_rows)]
    first = copies(0, me, sibling, True)
    for j, chip in enumerate(chips):
        first += copies(1 + j, me, (*chip, c), True)

    def start():
        for cp in mine + first:
            cp.start()

    def forward():
        for j, chip in enumerate(chips):
            all_bytes(1 + j).wait_recv()
            for cp in copies(4 + j, (*chip, c), sibling, False):
                cp.start()

    def finish():
        all_bytes(0).wait_recv()
        for j in range(3):
            all_bytes(4 + j).wait_recv()
        for k in range(7):
            all_bytes(k).wait_send()
        pltpu.make_async_copy(x_ref, x_ref, local_sem).wait()

    return start, forward, finish


def _all_gather(xs, seg_rows=None, *, name):
    segs = [xs.shape[0]] if seg_rows is None else list(seg_rows)

    def body(x_ref, *rest):
        start, forward, finish = _gather_phases(x_ref, rest[:len(segs)], segs, *rest[len(segs):])
        start()
        forward()
        finish()

    outs = pl.pallas_call(
        body, name=name, in_specs=[HBM_SPEC], out_specs=[HBM_SPEC] * len(segs),
        out_shape=[jax.ShapeDtypeStruct((8, n, xs.shape[1]), xs.dtype) for n in segs],
        scratch_shapes=_gather_scratch())(xs)
    return outs[0] if seg_rows is None else outs


def _sibling_exchange(s, *, name):
    def body(s_ref, rb_ref, send_sem, recv_sem):
        x, y, c = lax.axis_index("x"), lax.axis_index("y"), lax.axis_index("c")
        cp = pltpu.make_async_remote_copy(
            src_ref=s_ref.at[:, 1 - c], dst_ref=rb_ref, send_sem=send_sem, recv_sem=recv_sem,
            device_id=(x, y, 1 - c), device_id_type=MESH)
        cp.start()
        cp.wait()

    return pl.pallas_call(
        body, name=name, in_specs=[HBM_SPEC], out_specs=HBM_SPEC,
        out_shape=jax.ShapeDtypeStruct(s.shape[:1] + s.shape[2:], s.dtype),
        scratch_shapes=[pltpu.SemaphoreType.DMA, pltpu.SemaphoreType.DMA])(s)


def _row_tile(n, cap=1024):
    return max(b for b in range(16, cap + 1, 16) if n % b == 0)


def _pair_add(s, rb, core, *, name):
    nchip, _, r, c = s.shape
    tb = _row_tile(r)

    def body(core_ref, a_ref, b_ref, o_ref):
        o_ref[...] = (a_ref[...].astype(F32) + b_ref[...].astype(F32)).astype(BF16)

    blk = pl.BlockSpec((None, tb, c), lambda ch, i, cr: (ch, i, 0))
    return pl.pallas_call(
        body, name=name,
        grid_spec=pltpu.PrefetchScalarGridSpec(
            num_scalar_prefetch=1, grid=(nchip, r // tb),
            in_specs=[pl.BlockSpec((None, None, tb, c), lambda ch, i, cr: (ch, cr[0], i, 0)), blk],
            out_specs=blk),
        out_shape=jax.ShapeDtypeStruct((nchip, r, c), BF16), compiler_params=_cp())(core, s, rb)


def _chip_exchange_scratch():
    return [pltpu.SemaphoreType.DMA((3,)), pltpu.SemaphoreType.DMA((3,)), pltpu.SemaphoreType.DMA]


def _chip_exchange_phases(p_ref, out_ref, send_sems, recv_sems, local_sem):
    x, y, c = lax.axis_index("x"), lax.axis_index("y"), lax.axis_index("c")
    mine = 2 * x + y
    own = pltpu.make_async_copy(p_ref.at[mine], out_ref.at[mine], local_sem)
    copies = [pltpu.make_async_remote_copy(
        src_ref=p_ref.at[2 * tx + ty], dst_ref=out_ref.at[mine],
        send_sem=send_sems.at[k], recv_sem=recv_sems.at[k], device_id=(tx, ty, c), device_id_type=MESH)
        for k, (tx, ty) in enumerate([(1 - x, y), (x, 1 - y), (1 - x, 1 - y)])]

    def start():
        own.start()
        for cp in copies:
            cp.start()

    def finish():
        for cp in copies:
            cp.wait()
        own.wait()

    return start, finish


def _direct_exchange_scratch():
    return [pltpu.SemaphoreType.DMA((7,)), pltpu.SemaphoreType.DMA((7,)), pltpu.SemaphoreType.DMA]


def _direct_exchange_rows(sends):
    return sum(s.shape[2] for s in sends)


def _direct_exchange_phases(s_refs, out_ref, send_sems, recv_sems, local_sem):
    x, y, c = lax.axis_index("x"), lax.axis_index("y"), lax.axis_index("c")
    me = 4 * x + 2 * y + c
    offs, off = [], 0
    for s in s_refs:
        offs.append(off)
        off += s.shape[2]

    def slot(p):
        return out_ref.at[me, pl.ds(offs[p], s_refs[p].shape[2])]

    own = [pltpu.make_async_copy(s.at[2 * x + y, c], slot(p), local_sem) for p, s in enumerate(s_refs)]
    flips = [(fx, fy, fc) for fx in (0, 1) for fy in (0, 1) for fc in (0, 1) if (fx, fy, fc) != (0, 0, 0)]
    copies = []
    for k, (fx, fy, fc) in enumerate(flips):
        tx, ty, tc = (1 - x if fx else x), (1 - y if fy else y), (1 - c if fc else c)
        copies += [pltpu.make_async_remote_copy(
            src_ref=s.at[2 * tx + ty, tc], dst_ref=slot(p),
            send_sem=send_sems.at[k], recv_sem=recv_sems.at[k], device_id=(tx, ty, tc), device_id_type=MESH)
            for p, s in enumerate(s_refs)]

    def start():
        for cp in own + copies:
            cp.start()

    def finish():
        whole = out_ref.at[me]
        for k in range(len(flips)):
            pltpu.make_async_remote_copy(src_ref=whole, dst_ref=whole, send_sem=send_sems.at[k],
                                         recv_sem=recv_sems.at[k], device_id=(x, y, c), device_id_type=MESH).wait()
        pltpu.make_async_copy(whole, whole, local_sem).wait()

    return start, finish


def _direct_gather(xs, *, name):
    send = jnp.broadcast_to(xs, (4, 2) + xs.shape)

    def body(s_ref, out_ref, *sems):
        start, finish = _direct_exchange_phases([s_ref], out_ref, *sems)
        start()
        finish()

    return pl.pallas_call(
        body, name=name, in_specs=[HBM_SPEC], out_specs=HBM_SPEC,
        out_shape=jax.ShapeDtypeStruct((8,) + xs.shape, xs.dtype), scratch_shapes=_direct_exchange_scratch())(send)


def _adamw_math(w, g, m, v):
    m2 = ADAM_B1 * m + (1.0 - ADAM_B1) * g
    v2 = ADAM_B2 * v + (1.0 - ADAM_B2) * (g * g)
    m_hat = m2 / (1.0 - ADAM_B1 ** ADAM_STEP)
    v_hat = v2 / (1.0 - ADAM_B2 ** ADAM_STEP)
    return -ADAM_LR * (m_hat / (jnp.sqrt(v_hat) + ADAM_EPS) + ADAM_WD * w), m2, v2


def _adamw_shard(parts, g_off, w, m, v, layer, prev, *, name):
    _, r, c = w.shape
    npart = parts.shape[0]
    tb = next(b for b in range(min(r, 512), 0, -16) if r % b == 0 and g_off % b == 0)

    def body(*refs):
        w_ref, m_ref, v_ref = refs[npart:npart + 3]
        g_out, d_out, m_out, v_out = refs[-4:]
        g = refs[0][...].astype(F32)
        for p_ref in refs[1:npart]:
            g = g + p_ref[...].astype(F32)
        d, m2, v2 = _adamw_math(w_ref[...], g, m_ref[...], v_ref[...])
        g_out[...] = g
        d_out[...] = d
        m_out[...] = m2
        v_out[...] = v2

    def part(ch):
        return pl.BlockSpec((None, tb, c), lambda i: (ch, g_off // tb + i, 0))

    blk = pl.BlockSpec((None, tb, c), lambda i: (layer, i, 0))
    prev = list(prev) if prev is not None else []
    return pl.pallas_call(
        body, name=name, grid=(r // tb,),
        in_specs=[part(ch) for ch in range(npart)] + [blk, blk, blk] + [pl.BlockSpec(memory_space=pl.ANY)] * len(prev),
        out_specs=[blk] * 4, out_shape=[jax.ShapeDtypeStruct(w.shape, F32)] * 4,
        input_output_aliases={npart + 3 + k: k for k in range(len(prev))},
        compiler_params=_cp())(*([parts] * npart), w, m, v, *prev)


SLOT = 8
SMALL_ROWS = 6 * SLOT
ROW_LB = 4 * SLOT


def _small_update(gath, w, m, v, *, name):
    def body(g_ref, w_ref, m_ref, v_ref, g_out, d_out, m_out, v_out):
        tot = g_ref[0]
        for k in range(1, 8):
            tot = tot + g_ref[k]
        wv = w_ref[...]
        c0, c1 = wv[ROW_LB:ROW_LB + 1, :], wv[ROW_LB + 1:ROW_LB + 2, :]
        mx = jnp.maximum(c0, c1)
        e0, e1 = jnp.exp(c0 - mx), jnp.exp(c1 - mx)
        lb = e1 / (e0 + e1)
        gl = tot[ROW_LB:ROW_LB + 1, :] * lb * (1.0 - lb)
        row = lax.broadcasted_iota(jnp.int32, tot.shape, 0)
        g = jnp.where(row == ROW_LB, -gl, jnp.where(row == ROW_LB + 1, gl, tot))
        d, m2, v2 = _adamw_math(wv, g, m_ref[...], v_ref[...])
        g_out[...] = g
        d_out[...] = d
        m_out[...] = m2
        v_out[...] = v2

    return pl.pallas_call(
        body, name=name, out_shape=[jax.ShapeDtypeStruct(w.shape, F32)] * 4, compiler_params=_cp())(gath, w, m, v)


D_MODEL = 1024


def _ffn_fwd(h, gain, wg, wu, wd, tag):
    xn, gg, uu, act = _norm_gate_up(h, gain, wg, wu, name=f"{tag}_gate_up")
    out = _mm([(act, wd)], residual=h, alpha=MACARON, tn=1024, name=f"{tag}_down")
    return out, (h, xn, gg, uu, act)


def _ffn_input_bwd(dg, du, wg, wu, x, gain, dres, chip_part, *, name, scale, tm=256):
    t, d = x.shape
    f = wg.shape[0]
    tm = min(tm, t)
    nt = t // tm
    fused = chip_part is not None

    def body(dg_ref, du_ref, wg_ref, wu_ref, x_ref, g_ref, dres_ref, *rest):
        if fused:
            part_ref, dx_ref, dxb_ref, dgain_ref, parts_ref = rest[:5]
            start, finish = _chip_exchange_phases(part_ref, parts_ref, *rest[5:])
            pl.when(pl.program_id(0) == 0)(start)
        else:
            dx_ref, dxb_ref, dgain_ref = rest
        dxn_v = _dot(dg_ref[...], wg_ref[...]) + _dot(du_ref[...], wu_ref[...])
        xv = x_ref[...]
        rstd = lax.rsqrt(jnp.mean(xv * xv, axis=-1, keepdims=True) + RMS_EPS)
        xhat = xv * rstd
        dxhat = dxn_v * g_ref[...]
        dx = dres_ref[...] + rstd * (dxhat - xhat * jnp.mean(dxhat * xhat, axis=-1, keepdims=True))
        dx_ref[...] = dx
        dxb_ref[...] = (dx * scale).astype(BF16)

        @pl.when(pl.program_id(0) == 0)
        def _():
            dgain_ref[...] = jnp.zeros_like(dgain_ref)

        dgain_ref[...] += jnp.sum(dxn_v * xhat, axis=0, keepdims=True)
        if fused:
            pl.when(pl.program_id(0) == nt - 1)(finish)

    wide = pl.BlockSpec((tm, f), lambda i: (i, 0))
    wsp = pl.BlockSpec((f, d), lambda i: (0, 0))
    row = pl.BlockSpec((tm, d), lambda i: (i, 0))
    vec = pl.BlockSpec((1, d), lambda i: (0, 0))
    args = [dg, du, wg, wu, x, gain, dres] + ([chip_part] if fused else [])
    return pl.pallas_call(
        body, name=name, grid=(nt,),
        in_specs=[wide, wide, wsp, wsp, row, vec, row] + ([HBM_SPEC] if fused else []),
        out_specs=[row, row, vec] + ([HBM_SPEC] if fused else []),
        out_shape=[jax.ShapeDtypeStruct((t, d), F32), jax.ShapeDtypeStruct((t, d), BF16), jax.ShapeDtypeStruct((1, d), F32)]
        + ([jax.ShapeDtypeStruct(chip_part.shape, chip_part.dtype)] if fused else []),
        scratch_shapes=_chip_exchange_scratch() if fused else [],
        compiler_params=_cp(dimension_semantics=("arbitrary",)))(*args)


def _ffn_bwd(dout, dout_half, saved, gain, wg, wu, wd, tag, next_scale, exchanges=None):
    h, xn, gg, uu, act = saved
    early_chip_part, send_after_dwd, chip_part_after_dwgu = exchanges if exchanges is not None else (None, None, None)
    dg, du, *early_parts = _swiglu_bwd(dout_half, wd, gg, uu, early_chip_part, tm=256, tf=wd.shape[0],
                                       name=f"{tag}_dact")
    dwd = _mm([(act, dout_half)], ta=True, tm=256, tn=1024, out_dtype=BF16, name=f"{tag}_dwd")
    send = send_after_dwd(dwd) if exchanges is not None else None
    dwg, dwu, *mid_parts = _mm_shared_rhs([dg, du], xn, tm=256, send=send, name=f"{tag}_dwgu")
    chip_part = chip_part_after_dwgu(dwg, dwu) if exchanges is not None else None
    dh, dh_b, dgain, *parts = _ffn_input_bwd(dg, du, wg, wu, h, gain, dout, chip_part, scale=next_scale,
                                             name=f"{tag}_input_bwd")
    return dh, dh_b, dwg, dwu, dwd, dgain, (early_parts + mid_parts + parts)


def kernel(x, ffn_pre_norm, ffn_pre_w_gate, ffn_pre_w_up, ffn_pre_w_down, mix_norm, ffn_post_norm, ffn_post_w_gate, ffn_post_w_up, ffn_post_w_down, ab_w_in, ab_conv_w, ab_w_out, c_w_in, c_lower_bounds, c_out_norm, c_w_out, final_norm, loss_target, m_ffn_pre_norm, m_ffn_pre_w_gate, m_ffn_pre_w_up, m_ffn_pre_w_down, m_mix_norm, m_ffn_post_norm, m_ffn_post_w_gate, m_ffn_post_w_up, m_ffn_post_w_down, m_ab_w_in, m_ab_conv_w, m_ab_w_out, m_c_w_in, m_c_lower_bounds, m_c_out_norm, m_c_w_out, m_final_norm, v_ffn_pre_norm, v_ffn_pre_w_gate, v_ffn_pre_w_up, v_ffn_pre_w_down, v_mix_norm, v_ffn_post_norm, v_ffn_post_w_gate, v_ffn_post_w_up, v_ffn_post_w_down, v_ab_w_in, v_ab_conv_w, v_ab_w_out, v_c_w_in, v_c_lower_bounds, v_c_out_norm, v_c_w_out, v_final_norm):
    d = D_MODEL
    h0 = x[0]
    target = loss_target[0]
    core = lax.axis_index("c").astype(jnp.int32).reshape(1)

    big = [("pre_g", ffn_pre_w_gate, m_ffn_pre_w_gate, v_ffn_pre_w_gate),
           ("pre_u", ffn_pre_w_up, m_ffn_pre_w_up, v_ffn_pre_w_up),
           ("pre_d", ffn_pre_w_down, m_ffn_pre_w_down, v_ffn_pre_w_down),
           ("post_g", ffn_post_w_gate, m_ffn_post_w_gate, v_ffn_post_w_gate),
           ("post_u", ffn_post_w_up, m_ffn_post_w_up, v_ffn_post_w_up),
           ("post_d", ffn_post_w_down, m_ffn_post_w_down, v_ffn_post_w_down),
           ("ab_in", ab_w_in, m_ab_w_in, v_ab_w_in),
           ("ab_out", ab_w_out, m_ab_w_out, v_ab_w_out),
           ("c_in", c_w_in, m_c_w_in, v_c_w_in),
           ("c_out", c_w_out, m_c_w_out, v_c_w_out)]
    by_tag = {tag: (w, m, v) for tag, w, m, v in big}

    def layer_rows(tag):
        w = by_tag[tag][0]
        return w.size // d // w.shape[0]

    def layout(items):
        offs, off = {}, 0
        for item in items:
            offs[item] = off
            off += layer_rows(item[0])
        return offs, off

    ffn = [f"{pos}_{kind}" for pos in ("pre", "post") for kind in "gud"]
    first_items = [("pre_g", 0), ("pre_u", 0)]
    early_items = [("pre_d", 0), ("ab_in", 0)]
    late_items = ([("pre_g", 1), ("pre_u", 1), ("pre_d", 1)] + [(f"post_{kind}", l) for l in (0, 1) for kind in "gud"]
                  + [("ab_out", 0), ("c_in", 0), ("c_out", 0)])
    grad_items = {"A0": [(f"post_{kind}", 1) for kind in "gud"] + [("c_out", 0)],
                  "A1": ([(f"pre_{kind}", 1) for kind in "gud"] + [(f"post_{kind}", 0) for kind in "gud"]
                         + [("c_in", 0), ("ab_out", 0)]),
                  "C": [("ab_in", 0)], "B0": [("pre_d", 0)], "B1": [("pre_g", 0), ("pre_u", 0)]}
    grad_offs = {k: layout(items)[0] for k, items in grad_items.items()}
    grad_conv_row = layout(grad_items["C"])[1]

    def conv_rows(a, split):
        flat = a.reshape(-1)
        if split:
            hi = flat.astype(BF16)
            flat = jnp.concatenate([hi, (flat - hi.astype(F32)).astype(BF16)])
        return jnp.zeros((16, d), flat.dtype).at[0, :flat.shape[0]].set(flat)

    nconv = ab_conv_w.size
    col_sharded = {"pre_g", "pre_u", "post_g", "post_u", "ab_in", "c_in"}

    def pack_rows(item):
        tag, layer = item
        a = by_tag[tag][0][layer]
        return (a.T if tag in col_sharded else a).reshape(-1, d).astype(BF16)

    first_pack = jnp.concatenate([pack_rows(item) for item in first_items], axis=0)
    early_pack = jnp.concatenate([pack_rows(item) for item in early_items] + [conv_rows(ab_conv_w, True)], axis=0)
    late_pack = jnp.concatenate([pack_rows(item) for item in late_items], axis=0)
    first_w = _all_gather(first_pack, [layer_rows(tag) for tag, _ in first_items], name="gather_first_weights")
    full = {item: g.reshape(-1, d) for item, g in zip(first_items, first_w)}

    xn0, gg0, uu0, act0, *early_w = _norm_gate_up(
        h0, ffn_pre_norm[0:1], full["pre_g", 0], full["pre_u", 0], name="l0pre_gate_up_gather_early_weights",
        pack=early_pack, seg_rows=[layer_rows(tag) for tag, _ in early_items] + [16])
    full.update({item: g.reshape(-1, d) for item, g in zip(early_items, early_w)})
    ffn_w = {("pre", 0): tuple(full[f"pre_{kind}", 0] for kind in "gud")}
    w_ab_in = full["ab_in", 0]
    cg = early_w[-1][:, 0, :2 * nconv].astype(F32)
    conv_w = (cg[:, :nconv] + cg[:, nconv:]).reshape(8, 3, -1).transpose(1, 0, 2).reshape(3, -1)
    aw = w_ab_in.shape[0] // 6
    h1 = _mm([(act0, full["pre_d", 0])], residual=h0, alpha=MACARON, tn=1024, name="l0pre_down")
    s_pre0 = (h0, xn0, gg0, uu0, act0)
    hn0, pa, pb = _norm_proj(h1, mix_norm[0:1], w_ab_in, (F32, BF16), tm=512, name="ab_norm_proj")
    ya = _conv_fwd(pa, conv_w, name="conv_fwd")
    yb, ltot, *late_w = _attn_fwd(pb, late_pack, [layer_rows(tag) for tag, _ in late_items],
                                  name="attn_fwd_gather_late_weights")
    full.update({item: g.reshape(-1, d) for item, g in zip(late_items, late_w)})
    for pos, layer in (("post", 0), ("pre", 1), ("post", 1)):
        ffn_w[pos, layer] = tuple(full[f"{pos}_{kind}", layer] for kind in "gud")
    w_ab_out, w_c_in, w_c_out = full["ab_out", 0], full["c_in", 0], full["c_out", 0]
    h2 = _mm([(ya, w_ab_out[:aw]), (yb, w_ab_out[aw:])], residual=h1, tn=1024, name="ab_out")
    h3, s_post0 = _ffn_fwd(h2, ffn_post_norm[0:1], *ffn_w["post", 0], "l0post")
    h4, s_pre1 = _ffn_fwd(h3, ffn_pre_norm[1:2], *ffn_w["pre", 1], "l1pre")
    hn1, pc = _norm_proj(h4, mix_norm[1:2], w_c_in, (F32,), tm=256, name="c_norm_proj")
    yc, o_saved, states = _hgrn_fwd(pc, c_lower_bounds, c_out_norm, name="hgrn_fwd")
    h5 = _mm([(yc, w_c_out)], residual=h4, tn=1024, name="c_out")
    h6, s_post1 = _ffn_fwd(h5, ffn_post_norm[1:2], *ffn_w["post", 1], "l1post")
    dh6, dh6_b, d_final, loss_vec = _loss_head(h6, final_norm.reshape(1, d), target, name="loss_head")

    gw = {}

    def grad_send(key, extra=()):
        return [g.reshape(4, 2, -1, d) for g in [gw[item] for item in grad_items[key]] + list(extra)]

    def chip_partials(key, extra=()):
        send = jnp.concatenate(grad_send(key, extra), axis=2)
        from_sibling = _sibling_exchange(send, name=f"grad{key}_sibling_exchange")
        return _pair_add(send, from_sibling, core, name=f"grad{key}_pair_add")

    dh5, dh5_b, gw["post_g", 1], gw["post_u", 1], gw["post_d", 1], d_post1, *_ = _ffn_bwd(
        dh6, dh6_b, s_post1, ffn_post_norm[1:2], *ffn_w["post", 1], "l1post", 1.0)
    dyc = _mm([(dh5_b, w_c_out)], tb=True, tn=1024, name="c_out_dy")
    g_c_out = _mm([(yc, dh5_b)], ta=True, tm=256, tn=1024, out_dtype=BF16, name="c_out_dw")
    gw["c_out", 0] = g_c_out
    dcq, dcf, dci, dcg, dlb, d_onorm, parts_a0 = _hgrn_bwd(pc, o_saved, states, dyc, c_lower_bounds, c_out_norm,
                                                           grad_send("A0"), name="hgrn_bwd_exchange_grads_a0")
    dparts = [dcq, dcf, dci, dcg]
    g_c_in = jnp.concatenate(_mm_shared_rhs(dparts, hn1, tm=256, name="c_in_dw"), axis=0)
    cw = w_c_in.shape[0] // 4
    dhn1 = _mm([(dp, w_c_in[i * cw:(i + 1) * cw]) for i, dp in enumerate(dparts)], tm=512, tn=1024, name="c_in_dx")
    dh4, dh4_b, d_mix1 = _rmsnorm_bwd(h4, mix_norm[1:2], dhn1, dh5, scale=MACARON, name="l1_mix_norm_bwd")
    dh3, dh3_b, gw["pre_g", 1], gw["pre_u", 1], gw["pre_d", 1], d_pre1, *_ = _ffn_bwd(
        dh4, dh4_b, s_pre1, ffn_pre_norm[1:2], *ffn_w["pre", 1], "l1pre", MACARON)
    dh2, dh2_b, gw["post_g", 0], gw["post_u", 0], gw["post_d", 0], d_post0, *_ = _ffn_bwd(
        dh3, dh3_b, s_post0, ffn_post_norm[0:1], *ffn_w["post", 0], "l0post", 1.0)
    dyab = _mm([(dh2_b, w_ab_out)], tb=True, tn=1024, name="ab_out_dy")
    g_ab_out = jnp.concatenate(_mm_shared_rhs([ya, yb], dh2_b, tm=256, name="ab_out_dw"), axis=0)
    dab, dac, dax, g_conv = _conv_bwd(pa, dyab, conv_w, name="conv_bwd")

    gw["c_in", 0], gw["ab_out", 0] = g_c_in, g_ab_out
    dq, dk, dv, parts_a1 = _attn_bwd(pb, dyab, ltot, grad_send("A1"), name="attn_bwd_exchange_grads_a1")
    dparts = [dab, dac, dax, dq, dk, dv]
    g_ab_in = jnp.concatenate(_mm_shared_rhs(dparts, hn0, tm=128, name="ab_in_dw"), axis=0)
    dhn0 = _mm([(dp, w_ab_in[i * aw:(i + 1) * aw]) for i, dp in enumerate(dparts)], tm=512, tn=1024, name="ab_in_dx")
    dh1, dh1_b, d_mix0 = _rmsnorm_bwd(h1, mix_norm[0:1], dhn0, dh2, scale=MACARON, name="l0_mix_norm_bwd")
    gw["ab_in", 0] = g_ab_in
    gconv_own = g_conv.reshape(3, 8, -1).transpose(1, 0, 2).reshape(8, -1)
    conv_piece = jnp.zeros((8, 16, d), F32).at[:, 0, :nconv].set(gconv_own).astype(BF16)

    def send_b0(dwd):
        gw["pre_d", 0] = dwd
        return grad_send("B0")

    def chip_part_b1(dwg, dwu):
        gw["pre_g", 0], gw["pre_u", 0] = dwg, dwu
        return chip_partials("B1")

    dh0, _, _, _, _, d_pre0, (parts_c, parts_b0, parts_b1) = _ffn_bwd(
        dh1, dh1_b, s_pre0, ffn_pre_norm[0:1], *ffn_w["pre", 0], "l0pre", 1.0,
        (chip_partials("C", [conv_piece]), send_b0, chip_part_b1))

    parts = {"A0": parts_a0, "A1": parts_a1, "B0": parts_b0, "B1": parts_b1, "C": parts_c}
    upd = {}
    for tag, w, m, v in big:
        view = (lambda a: jnp.swapaxes(a, 1, 2)) if tag in col_sharded else (lambda a: a)
        where = {layer: (key, grad_offs[key][tag, layer])
                 for key in grad_items for t2, layer in grad_items[key] if t2 == tag}
        res = None
        for layer in sorted(where):
            key, off = where[layer]
            res = _adamw_shard(parts[key], off, view(w), view(m), view(v), layer, res, name=f"adamw_{tag}{layer}")
        upd[tag] = [view(a) for a in res]
    res = _adamw_shard(parts["C"], grad_conv_row, *(conv_rows(a, False)[None] for a in (ab_conv_w, m_ab_conv_w, v_ab_conv_w)),
                       0, None, name="adamw_conv")
    upd["conv"] = [r[0, 0, :nconv].reshape(ab_conv_w.shape) for r in res]

    def small_pack(pre, mix, post, final, lbs, onorm):
        def slot(parts):
            out, r = jnp.zeros((SLOT, d), F32), 0
            for a in (parts if isinstance(parts, tuple) else (parts,)):
                out = out.at[r:r + a.shape[0], :a.shape[1]].set(a)
                r += a.shape[0]
            return out

        return jnp.concatenate([slot(pre), slot(mix), slot(post), slot(final.reshape(1, d)), slot(lbs), slot(onorm)], axis=0)

    d_on = d_onorm.reshape(-1, c_out_norm.shape[1]).sum(axis=0, keepdims=True)
    gsmall = small_pack((d_pre0, d_pre1), (d_mix0, d_mix1), (d_post0, d_post1), d_final, dlb, d_on)
    gsmall_all = _direct_gather(gsmall, name="gather_small_grads")
    sres = _small_update(
        gsmall_all,
        small_pack(ffn_pre_norm, mix_norm, ffn_post_norm, final_norm, c_lower_bounds, c_out_norm),
        small_pack(m_ffn_pre_norm, m_mix_norm, m_ffn_post_norm, m_final_norm, m_c_lower_bounds, m_c_out_norm),
        small_pack(v_ffn_pre_norm, v_mix_norm, v_ffn_post_norm, v_final_norm, v_c_lower_bounds, v_c_out_norm),
        name="small_update")

    def small_out(r):
        return {"pre_norm": r[0:2], "mix_norm": r[SLOT:SLOT + 2], "post_norm": r[2 * SLOT:2 * SLOT + 2],
                "final": r[3 * SLOT], "lb": r[ROW_LB:ROW_LB + 2], "onorm": r[5 * SLOT:5 * SLOT + 1, :c_out_norm.shape[1]]}

    small = [small_out(r) for r in sres]
    outs = []
    for k in range(4):
        s = small[k]
        outs += [s["pre_norm"], upd["pre_g"][k], upd["pre_u"][k], upd["pre_d"][k], s["mix_norm"], s["post_norm"],
                 upd["post_g"][k], upd["post_u"][k], upd["post_d"][k], upd["ab_in"][k], upd["conv"][k],
                 upd["ab_out"][k], upd["c_in"][k], s["lb"], s["onorm"], upd["c_out"][k], s["final"]]
    loss = lax.psum(loss_vec[0, 0], ("x", "y", "c"))
    return (loss, dh0[None], *outs)
```
